```python
import jax, jax.numpy as jnp
from jax import lax
import numpy as np

D_MODEL = 1024
BATCH = 8
SEQ = 8192
DEPTH = 1

N_META = 16
D_MIX = D_MODEL
GLA_WIDTH = D_MIX // 2
GLA_HEADS = 4
GLA_DV = GLA_WIDTH // GLA_HEADS
GLA_DK = GLA_DV // 2
GLA_KEY_WIDTH = GLA_HEADS * GLA_DK
GLA_GATE_RANK = 16
GLA_TAU = 16.0
GLA_CHUNK = 64
SWA_WIDTH = D_MIX - GLA_WIDTH
SWA_HEAD_DIM = 64
SWA_Q_HEADS = SWA_WIDTH // SWA_HEAD_DIM
SWA_KV_HEADS = 2
SWA_GROUP = SWA_Q_HEADS // SWA_KV_HEADS
SWA_KV_WIDTH = SWA_KV_HEADS * SWA_HEAD_DIM
WINDOW = 128
SWA_BLOCK = 128
ROPE_THETA = 10000.0
D_FF = 256 * ((8 * D_MODEL // 3 + 255) // 256)
NORM_EPS = 1e-6
NEG_INF = -1e30

IN_SPLITS = (GLA_KEY_WIDTH, GLA_KEY_WIDTH, GLA_WIDTH, GLA_WIDTH, GLA_GATE_RANK,
             SWA_WIDTH, SWA_KV_WIDTH, SWA_KV_WIDTH)
D_IN = sum(IN_SPLITS)

kernel_name = "hybrid_gla_swa_macaron_layer"


def rms_norm(x, w):
    xf = x.astype(jnp.float32)
    y = xf * lax.rsqrt(jnp.mean(xf * xf, axis=-1, keepdims=True) + NORM_EPS)
    return (y * w.astype(jnp.float32)).astype(x.dtype)


def swiglu(x, w_gate, w_up, w_down):
    return (jax.nn.silu(x @ w_gate) * (x @ w_up)) @ w_down


def rope(x, pos):
    hd = x.shape[-1]
    inv_freq = 1.0 / (ROPE_THETA ** (jnp.arange(0, hd, 2, dtype=jnp.float32) / hd))
    ang = pos.astype(jnp.float32)[:, None] * inv_freq[None, :]
    ang = jnp.concatenate([ang, ang], axis=-1)[:, None, :]
    xf = x.astype(jnp.float32)
    x1, x2 = jnp.split(xf, 2, axis=-1)
    rot = jnp.concatenate([-x2, x1], axis=-1)
    return (xf * jnp.cos(ang) + rot * jnp.sin(ang)).astype(x.dtype)


def gla_chunked(q, k, v, log_a):
    B, L, H, dk = q.shape
    dv = v.shape[-1]
    C = GLA_CHUNK
    pad = (-L) % C
    n = (L + pad) // C

    def to_chunks(t):
        t = jnp.pad(t, ((0, 0), (pad, 0), (0, 0), (0, 0)))
        return t.reshape(B, n, C, H, t.shape[-1]).transpose(1, 0, 3, 2, 4).astype(jnp.float32)

    qc = to_chunks(q) * (dk ** -0.5)
    kc = to_chunks(k)
    vc = to_chunks(v)
    bc = jnp.cumsum(to_chunks(log_a), axis=3)
    causal = jnp.tril(jnp.ones((C, C), dtype=bool))[None, None, :, :, None]

    def step(S, inp):
        qi, ki, vi, bi = inp
        diff = bi[:, :, :, None, :] - bi[:, :, None, :, :]
        decay = jnp.exp(jnp.where(causal, diff, -jnp.inf))
        attn = jnp.einsum('bhid,bhjd,bhijd->bhij', qi, ki, decay)
        o = jnp.einsum('bhij,bhjv->bhiv', attn, vi) + \
            jnp.einsum('bhid,bhdv->bhiv', qi * jnp.exp(bi), S)
        b_last = bi[:, :, -1:, :]
        S = jnp.exp(b_last[:, :, 0, :])[..., None] * S + \
            jnp.einsum('bhjd,bhjv->bhdv', ki * jnp.exp(b_last - bi), vi)
        return S, o

    S0 = jnp.zeros((B, H, dk, dv), jnp.float32)
    _, o = lax.scan(step, S0, (qc, kc, vc, bc))
    o = o.transpose(1, 0, 3, 2, 4).reshape(B, n * C, H, dv)[:, pad:]
    return o.astype(v.dtype)


def swa_with_sinks(q, k, v, sinks):
    B, L, HQ, hd = q.shape
    T = SWA_BLOCK
    KV, G = SWA_KV_HEADS, SWA_GROUP
    pad = (-L) % T
    Lp = L + pad
    nb = Lp // T
    qb = jnp.pad(q, ((0, 0), (pad, 0), (0, 0), (0, 0))).reshape(B, nb, T, KV, G, hd)
    kp = jnp.pad(k, ((0, 0), (pad + T, 0), (0, 0), (0, 0))).reshape(B, nb + 1, T, KV, hd)
    vp = jnp.pad(v, ((0, 0), (pad + T, 0), (0, 0), (0, 0))).reshape(B, nb + 1, T, KV, hd)
    k_band = jnp.concatenate([kp[:, :-1], kp[:, 1:]], axis=2)
    v_band = jnp.concatenate([vp[:, :-1], vp[:, 1:]], axis=2)
    k_meta = k[:, :N_META]
    v_meta = v[:, :N_META]

    qpos = (jnp.arange(Lp) - pad).reshape(nb, T)
    kpos_all = (jnp.arange(Lp + T) - pad - T).reshape(nb + 1, T)
    kpos = jnp.concatenate([kpos_all[:-1], kpos_all[1:]], axis=1)
    dq = qpos[:, :, None]
    dk_ = kpos[:, None, :]
    band_mask = (dk_ >= N_META) & (dk_ <= dq) & (dq - dk_ < WINDOW)
    meta_mask = jnp.arange(N_META)[None, None, :] <= dq
    mask = jnp.concatenate([band_mask, meta_mask], axis=-1)

    scale = hd ** -0.5
    s_band = jnp.einsum('bntkgd,bnskd->bnkgts', qb, k_band).astype(jnp.float32)
    s_meta = jnp.einsum('bntkgd,bmkd->bnkgtm', qb, k_meta).astype(jnp.float32)
    s = jnp.concatenate([s_band, s_meta], axis=-1) * scale
    s = jnp.where(mask[None, :, None, None], s, NEG_INF)
    sink = sinks.astype(jnp.float32).reshape(KV, G)[None, None, :, :, None, None]
    m = jnp.maximum(jnp.max(s, axis=-1, keepdims=True), sink)
    p = jnp.exp(s - m)
    p = p / (jnp.sum(p, axis=-1, keepdims=True) + jnp.exp(sink - m))
    p = p.astype(v.dtype)
    o = jnp.einsum('bnkgts,bnskd->bntkgd', p[..., :2 * T], v_band) + \
        jnp.einsum('bnkgtm,bmkd->bntkgd', p[..., 2 * T:], v_meta)
    return o.reshape(B, Lp, HQ, hd)[:, pad:]


def hybrid_mixer(h, w_in, gla_w_a2, gla_b_a, gla_out_norm, swa_sinks, swa_out_norm, w_out):
    B, L, _ = h.shape
    pos = jnp.arange(L, dtype=jnp.int32)
    proj = h @ w_in
    offsets = [int(o) for o in np.cumsum(IN_SPLITS)[:-1]]
    gq, gk, gv, gg, ga, sq, sk, sv = jnp.split(proj, offsets, axis=-1)

    log_a = jax.nn.log_sigmoid((ga @ gla_w_a2 + gla_b_a).astype(jnp.float32)) / GLA_TAU
    o_gla = gla_chunked(gq.reshape(B, L, GLA_HEADS, GLA_DK),
                        gk.reshape(B, L, GLA_HEADS, GLA_DK),
                        gv.reshape(B, L, GLA_HEADS, GLA_DV),
                        log_a.reshape(B, L, GLA_HEADS, GLA_DK))
    o_gla = rms_norm(o_gla, gla_out_norm) * jax.nn.silu(gg.reshape(B, L, GLA_HEADS, GLA_DV))
    o_gla = o_gla.reshape(B, L, GLA_WIDTH)

    q = rope(sq.reshape(B, L, SWA_Q_HEADS, SWA_HEAD_DIM), pos)
    k = rope(sk.reshape(B, L, SWA_KV_HEADS, SWA_HEAD_DIM), pos)
    v = sv.reshape(B, L, SWA_KV_HEADS, SWA_HEAD_DIM)
    o_swa = swa_with_sinks(q, k, v, swa_sinks).reshape(B, L, SWA_WIDTH)
    o_swa = rms_norm(o_swa, swa_out_norm)

    return jnp.concatenate([o_gla, o_swa], axis=-1) @ w_out


def _fwd_setup_inputs(seed: int = 0) -> dict:
    key = jax.random.key(seed)
    ks = jax.random.split(key, 24)
    f32 = jnp.float32
    nrm = lambda k, shape, s: jax.random.normal(k, shape, f32) * s
    gain = lambda k, shape: 1.0 + 0.05 * jax.random.normal(k, shape, f32)
    Dp = DEPTH
    return {
        "x": jax.random.normal(ks[0], (BATCH, SEQ, D_MODEL), f32),
        "meta_tokens": nrm(ks[1], (N_META, D_MODEL), 1.0),
        "ffn1_pre_norm": gain(ks[2], (Dp, D_MODEL)),
        "ffn1_w_gate": nrm(ks[3], (Dp, D_MODEL, D_FF), D_MODEL ** -0.5),
        "ffn1_w_up": nrm(ks[4], (Dp, D_MODEL, D_FF), D_MODEL ** -0.5),
        "ffn1_w_down": nrm(ks[5], (Dp, D_FF, D_MODEL), D_FF ** -0.5),
        "ffn1_post_norm": gain(ks[6], (Dp, D_MODEL)),
        "mix_pre_norm": gain(ks[7], (Dp, D_MODEL)),
        "w_in": nrm(ks[8], (Dp, D_MODEL, D_IN), D_MODEL ** -0.5),
        "gla_w_a2": nrm(ks[9], (Dp, GLA_GATE_RANK, GLA_KEY_WIDTH), GLA_GATE_RANK ** -0.5),
        "gla_b_a": nrm(ks[10], (Dp, GLA_KEY_WIDTH), 0.1),
        "gla_out_norm": gain(ks[11], (Dp, GLA_DV)),
        "swa_sinks": nrm(ks[12], (Dp, SWA_Q_HEADS), 0.5),
        "swa_out_norm": gain(ks[13], (Dp, SWA_WIDTH)),
        "w_out": nrm(ks[14], (Dp, D_MIX, D_MODEL), D_MIX ** -0.5),
        "mix_post_norm": gain(ks[15], (Dp, D_MODEL)),
        "ffn2_pre_norm": gain(ks[16], (Dp, D_MODEL)),
        "ffn2_w_gate": nrm(ks[17], (Dp, D_MODEL, D_FF), D_MODEL ** -0.5),
        "ffn2_w_up": nrm(ks[18], (Dp, D_MODEL, D_FF), D_MODEL ** -0.5),
        "ffn2_w_down": nrm(ks[19], (Dp, D_FF, D_MODEL), D_FF ** -0.5),
        "ffn2_post_norm": gain(ks[20], (Dp, D_MODEL)),
    }


def _fwd_reference(x, meta_tokens, ffn1_pre_norm, ffn1_w_gate, ffn1_w_up, ffn1_w_down, ffn1_post_norm,
              mix_pre_norm, w_in, gla_w_a2, gla_b_a, gla_out_norm, swa_sinks, swa_out_norm, w_out,
              mix_post_norm, ffn2_pre_norm, ffn2_w_gate, ffn2_w_up, ffn2_w_down, ffn2_post_norm):
    B = x.shape[0]
    meta = jnp.broadcast_to(meta_tokens.astype(x.dtype)[None], (B, N_META, x.shape[-1]))
    h = jnp.concatenate([meta, x], axis=1)
    for l in range(DEPTH):
        f = swiglu(rms_norm(h, ffn1_pre_norm[l]), ffn1_w_gate[l], ffn1_w_up[l], ffn1_w_down[l])
        h = h + 0.5 * rms_norm(f, ffn1_post_norm[l])
        m = hybrid_mixer(rms_norm(h, mix_pre_norm[l]), w_in[l], gla_w_a2[l], gla_b_a[l],
                         gla_out_norm[l], swa_sinks[l], swa_out_norm[l], w_out[l])
        h = h + rms_norm(m, mix_post_norm[l])
        f = swiglu(rms_norm(h, ffn2_pre_norm[l]), ffn2_w_gate[l], ffn2_w_up[l], ffn2_w_down[l])
        h = h + 0.5 * rms_norm(f, ffn2_post_norm[l])
    return h[:, N_META:]


import jax as _jax
import jax.numpy as _jnp

TWIN_FORMAT = 'train_step'
FWD_PARAMS = ['x', 'meta_tokens', 'ffn1_pre_norm', 'ffn1_w_gate', 'ffn1_w_up', 'ffn1_w_down', 'ffn1_post_norm', 'mix_pre_norm', 'w_in', 'gla_w_a2', 'gla_b_a', 'gla_out_norm', 'swa_sinks', 'swa_out_norm', 'w_out', 'mix_post_norm', 'ffn2_pre_norm', 'ffn2_w_gate', 'ffn2_w_up', 'ffn2_w_down', 'ffn2_post_norm']
TWIN_WEIGHTS = ['meta_tokens', 'ffn1_pre_norm', 'ffn1_w_gate', 'ffn1_w_up', 'ffn1_w_down', 'ffn1_post_norm', 'mix_pre_norm', 'w_in', 'gla_w_a2', 'gla_b_a', 'gla_out_norm', 'swa_sinks', 'swa_out_norm', 'w_out', 'mix_post_norm', 'ffn2_pre_norm', 'ffn2_w_gate', 'ffn2_w_up', 'ffn2_w_down', 'ffn2_post_norm']
TWIN_DIFF_INPUT = 'x'
TWIN_INPUTS = ['x', 'meta_tokens', 'ffn1_pre_norm', 'ffn1_w_gate', 'ffn1_w_up', 'ffn1_w_down', 'ffn1_post_norm', 'mix_pre_norm', 'w_in', 'gla_w_a2', 'gla_b_a', 'gla_out_norm', 'swa_sinks', 'swa_out_norm', 'w_out', 'mix_post_norm', 'ffn2_pre_norm', 'ffn2_w_gate', 'ffn2_w_up', 'ffn2_w_down', 'ffn2_post_norm', 'loss_target', 'm_meta_tokens', 'm_ffn1_pre_norm', 'm_ffn1_w_gate', 'm_ffn1_w_up', 'm_ffn1_w_down', 'm_ffn1_post_norm', 'm_mix_pre_norm', 'm_w_in', 'm_gla_w_a2', 'm_gla_b_a', 'm_gla_out_norm', 'm_swa_sinks', 'm_swa_out_norm', 'm_w_out', 'm_mix_post_norm', 'm_ffn2_pre_norm', 'm_ffn2_w_gate', 'm_ffn2_w_up', 'm_ffn2_w_down', 'm_ffn2_post_norm', 'v_meta_tokens', 'v_ffn1_pre_norm', 'v_ffn1_w_gate', 'v_ffn1_w_up', 'v_ffn1_w_down', 'v_ffn1_post_norm', 'v_mix_pre_norm', 'v_w_in', 'v_gla_w_a2', 'v_gla_b_a', 'v_gla_out_norm', 'v_swa_sinks', 'v_swa_out_norm', 'v_w_out', 'v_mix_post_norm', 'v_ffn2_pre_norm', 'v_ffn2_w_gate', 'v_ffn2_w_up', 'v_ffn2_w_down', 'v_ffn2_post_norm']
TWIN_OUTPUTS = ['loss', 'grad_x', 'grad_meta_tokens', 'grad_ffn1_pre_norm', 'grad_ffn1_w_gate', 'grad_ffn1_w_up', 'grad_ffn1_w_down', 'grad_ffn1_post_norm', 'grad_mix_pre_norm', 'grad_w_in', 'grad_gla_w_a2', 'grad_gla_b_a', 'grad_gla_out_norm', 'grad_swa_sinks', 'grad_swa_out_norm', 'grad_w_out', 'grad_mix_post_norm', 'grad_ffn2_pre_norm', 'grad_ffn2_w_gate', 'grad_ffn2_w_up', 'grad_ffn2_w_down', 'grad_ffn2_post_norm', 'delta_meta_tokens', 'delta_ffn1_pre_norm', 'delta_ffn1_w_gate', 'delta_ffn1_w_up', 'delta_ffn1_w_down', 'delta_ffn1_post_norm', 'delta_mix_pre_norm', 'delta_w_in', 'delta_gla_w_a2', 'delta_gla_b_a', 'delta_gla_out_norm', 'delta_swa_sinks', 'delta_swa_out_norm', 'delta_w_out', 'delta_mix_post_norm', 'delta_ffn2_pre_norm', 'delta_ffn2_w_gate', 'delta_ffn2_w_up', 'delta_ffn2_w_down', 'delta_ffn2_post_norm', 'new_m_meta_tokens', 'new_m_ffn1_pre_norm', 'new_m_ffn1_w_gate', 'new_m_ffn1_w_up', 'new_m_ffn1_w_down', 'new_m_ffn1_post_norm', 'new_m_mix_pre_norm', 'new_m_w_in', 'new_m_gla_w_a2', 'new_m_gla_b_a', 'new_m_gla_out_norm', 'new_m_swa_sinks', 'new_m_swa_out_norm', 'new_m_w_out', 'new_m_mix_post_norm', 'new_m_ffn2_pre_norm', 'new_m_ffn2_w_gate', 'new_m_ffn2_w_up', 'new_m_ffn2_w_down', 'new_m_ffn2_post_norm', 'new_v_meta_tokens', 'new_v_ffn1_pre_norm', 'new_v_ffn1_w_gate', 'new_v_ffn1_w_up', 'new_v_ffn1_w_down', 'new_v_ffn1_post_norm', 'new_v_mix_pre_norm', 'new_v_w_in', 'new_v_gla_w_a2', 'new_v_gla_b_a', 'new_v_gla_out_norm', 'new_v_swa_sinks', 'new_v_swa_out_norm', 'new_v_w_out', 'new_v_mix_post_norm', 'new_v_ffn2_pre_norm', 'new_v_ffn2_w_gate', 'new_v_ffn2_w_up', 'new_v_ffn2_w_down', 'new_v_ffn2_post_norm']
TWIN_LEAF_KINDS = {'loss': 'loss', 'grad_x': 'grad_x', 'grad_meta_tokens': 'grad_w', 'grad_ffn1_pre_norm': 'grad_w', 'grad_ffn1_w_gate': 'grad_w', 'grad_ffn1_w_up': 'grad_w', 'grad_ffn1_w_down': 'grad_w', 'grad_ffn1_post_norm': 'grad_w', 'grad_mix_pre_norm': 'grad_w', 'grad_w_in': 'grad_w', 'grad_gla_w_a2': 'grad_w', 'grad_gla_b_a': 'grad_w', 'grad_gla_out_norm': 'grad_w', 'grad_swa_sinks': 'grad_w', 'grad_swa_out_norm': 'grad_w', 'grad_w_out': 'grad_w', 'grad_mix_post_norm': 'grad_w', 'grad_ffn2_pre_norm': 'grad_w', 'grad_ffn2_w_gate': 'grad_w', 'grad_ffn2_w_up': 'grad_w', 'grad_ffn2_w_down': 'grad_w', 'grad_ffn2_post_norm': 'grad_w', 'delta_meta_tokens': 'delta_w', 'delta_ffn1_pre_norm': 'delta_w', 'delta_ffn1_w_gate': 'delta_w', 'delta_ffn1_w_up': 'delta_w', 'delta_ffn1_w_down': 'delta_w', 'delta_ffn1_post_norm': 'delta_w', 'delta_mix_pre_norm': 'delta_w', 'delta_w_in': 'delta_w', 'delta_gla_w_a2': 'delta_w', 'delta_gla_b_a': 'delta_w', 'delta_gla_out_norm': 'delta_w', 'delta_swa_sinks': 'delta_w', 'delta_swa_out_norm': 'delta_w', 'delta_w_out': 'delta_w', 'delta_mix_post_norm': 'delta_w', 'delta_ffn2_pre_norm': 'delta_w', 'delta_ffn2_w_gate': 'delta_w', 'delta_ffn2_w_up': 'delta_w', 'delta_ffn2_w_down': 'delta_w', 'delta_ffn2_post_norm': 'delta_w', 'new_m_meta_tokens': 'new_m', 'new_m_ffn1_pre_norm': 'new_m', 'new_m_ffn1_w_gate': 'new_m', 'new_m_ffn1_w_up': 'new_m', 'new_m_ffn1_w_down': 'new_m', 'new_m_ffn1_post_norm': 'new_m', 'new_m_mix_pre_norm': 'new_m', 'new_m_w_in': 'new_m', 'new_m_gla_w_a2': 'new_m', 'new_m_gla_b_a': 'new_m', 'new_m_gla_out_norm': 'new_m', 'new_m_swa_sinks': 'new_m', 'new_m_swa_out_norm': 'new_m', 'new_m_w_out': 'new_m', 'new_m_mix_post_norm': 'new_m', 'new_m_ffn2_pre_norm': 'new_m', 'new_m_ffn2_w_gate': 'new_m', 'new_m_ffn2_w_up': 'new_m', 'new_m_ffn2_w_down': 'new_m', 'new_m_ffn2_post_norm': 'new_m', 'new_v_meta_tokens': 'new_v', 'new_v_ffn1_pre_norm': 'new_v', 'new_v_ffn1_w_gate': 'new_v', 'new_v_ffn1_w_up': 'new_v', 'new_v_ffn1_w_down': 'new_v', 'new_v_ffn1_post_norm': 'new_v', 'new_v_mix_pre_norm': 'new_v', 'new_v_w_in': 'new_v', 'new_v_gla_w_a2': 'new_v', 'new_v_gla_b_a': 'new_v', 'new_v_gla_out_norm': 'new_v', 'new_v_swa_sinks': 'new_v', 'new_v_swa_out_norm': 'new_v', 'new_v_w_out': 'new_v', 'new_v_mix_post_norm': 'new_v', 'new_v_ffn2_pre_norm': 'new_v', 'new_v_ffn2_w_gate': 'new_v', 'new_v_ffn2_w_up': 'new_v', 'new_v_ffn2_w_down': 'new_v', 'new_v_ffn2_post_norm': 'new_v'}


def _forward(args):
    return _fwd_reference(*[args[k] for k in FWD_PARAMS])


def _output_shape():
    def fwd():
        inp = _fwd_setup_inputs(0)
        return _fwd_reference(*[inp[k] for k in FWD_PARAMS])
    out = _jax.eval_shape(fwd)
    return out.shape, out.dtype

N_MICROBATCH = 1
ADAM_LR = 0.001
ADAM_B1 = 0.9
ADAM_B2 = 0.999
ADAM_EPS = 1e-08
ADAM_WD = 0.01
ADAM_STEP = 10
PER_EXAMPLE_BATCH_AXIS = {'x': 0, 'loss_target': 0}
SHARED_INPUTS = []
_WEIGHT_DTYPES = {'meta_tokens': _jnp.float32, 'ffn1_pre_norm': _jnp.float32, 'ffn1_w_gate': _jnp.float32, 'ffn1_w_up': _jnp.float32, 'ffn1_w_down': _jnp.float32, 'ffn1_post_norm': _jnp.float32, 'mix_pre_norm': _jnp.float32, 'w_in': _jnp.float32, 'gla_w_a2': _jnp.float32, 'gla_b_a': _jnp.float32, 'gla_out_norm': _jnp.float32, 'swa_sinks': _jnp.float32, 'swa_out_norm': _jnp.float32, 'w_out': _jnp.float32, 'mix_post_norm': _jnp.float32, 'ffn2_pre_norm': _jnp.float32, 'ffn2_w_gate': _jnp.float32, 'ffn2_w_up': _jnp.float32, 'ffn2_w_down': _jnp.float32, 'ffn2_post_norm': _jnp.float32}
MOMENT_SCALE = {'meta_tokens': 1.165851e-01, 'ffn1_pre_norm': 7.098565e-01, 'ffn1_w_gate': 3.244999e-01, 'ffn1_w_up': 3.381774e-01, 'ffn1_w_down': 5.602205e-01, 'ffn1_post_norm': 1.582845e+01, 'mix_pre_norm': 1.122564e+00, 'w_in': 6.715630e-01, 'gla_w_a2': 5.647932e-02, 'gla_b_a': 2.604025e-01, 'gla_out_norm': 6.416308e-01, 'swa_sinks': 8.396022e-02, 'swa_out_norm': 1.055757e+00, 'w_out': 7.216852e-01, 'mix_post_norm': 6.375266e+01, 'ffn2_pre_norm': 5.861080e-01, 'ffn2_w_gate': 1.915540e-01, 'ffn2_w_up': 3.245769e-01, 'ffn2_w_down': 5.416460e-01, 'ffn2_post_norm': 1.608948e+01}


def _to_microbatches(a, axis):
    t = _jnp.moveaxis(a, axis, 0)
    t = t.reshape((N_MICROBATCH, t.shape[0] // N_MICROBATCH) + t.shape[1:])
    return _jnp.moveaxis(t, 1, axis + 1)


def setup_inputs(seed: int = 0) -> dict:
    inp = _fwd_setup_inputs(seed)
    key = _jax.random.fold_in(_jax.random.key(seed), 7919)
    shape, _ = _output_shape()
    out = dict(inp)
    out["loss_target"] = _jax.random.normal(_jax.random.fold_in(key, 0), shape, _jnp.float32)
    for i, name in enumerate(TWIN_WEIGHTS):
        w = inp[name].astype(_jnp.float32)
        if MOMENT_SCALE is None:
            s = _jnp.sqrt(_jnp.mean(_jnp.square(w)) + 1e-30)
        else:
            s = MOMENT_SCALE[name]
        km, kv = _jax.random.split(_jax.random.fold_in(key, i + 1))
        out[name] = w
        out["m_" + name] = s * _jax.random.normal(km, w.shape, _jnp.float32)
        out["v_" + name] = (s * s) * _jax.random.uniform(kv, w.shape, _jnp.float32, 0.5, 1.5)
    if N_MICROBATCH > 1:
        for name, axis in PER_EXAMPLE_BATCH_AXIS.items():
            out[name] = _to_microbatches(out[name], axis)
    return {'x': out['x'], 'meta_tokens': out['meta_tokens'], 'ffn1_pre_norm': out['ffn1_pre_norm'], 'ffn1_w_gate': out['ffn1_w_gate'], 'ffn1_w_up': out['ffn1_w_up'], 'ffn1_w_down': out['ffn1_w_down'], 'ffn1_post_norm': out['ffn1_post_norm'], 'mix_pre_norm': out['mix_pre_norm'], 'w_in': out['w_in'], 'gla_w_a2': out['gla_w_a2'], 'gla_b_a': out['gla_b_a'], 'gla_out_norm': out['gla_out_norm'], 'swa_sinks': out['swa_sinks'], 'swa_out_norm': out['swa_out_norm'], 'w_out': out['w_out'], 'mix_post_norm': out['mix_post_norm'], 'ffn2_pre_norm': out['ffn2_pre_norm'], 'ffn2_w_gate': out['ffn2_w_gate'], 'ffn2_w_up': out['ffn2_w_up'], 'ffn2_w_down': out['ffn2_w_down'], 'ffn2_post_norm': out['ffn2_post_norm'], 'loss_target': out['loss_target'], 'm_meta_tokens': out['m_meta_tokens'], 'm_ffn1_pre_norm': out['m_ffn1_pre_norm'], 'm_ffn1_w_gate': out['m_ffn1_w_gate'], 'm_ffn1_w_up': out['m_ffn1_w_up'], 'm_ffn1_w_down': out['m_ffn1_w_down'], 'm_ffn1_post_norm': out['m_ffn1_post_norm'], 'm_mix_pre_norm': out['m_mix_pre_norm'], 'm_w_in': out['m_w_in'], 'm_gla_w_a2': out['m_gla_w_a2'], 'm_gla_b_a': out['m_gla_b_a'], 'm_gla_out_norm': out['m_gla_out_norm'], 'm_swa_sinks': out['m_swa_sinks'], 'm_swa_out_norm': out['m_swa_out_norm'], 'm_w_out': out['m_w_out'], 'm_mix_post_norm': out['m_mix_post_norm'], 'm_ffn2_pre_norm': out['m_ffn2_pre_norm'], 'm_ffn2_w_gate': out['m_ffn2_w_gate'], 'm_ffn2_w_up': out['m_ffn2_w_up'], 'm_ffn2_w_down': out['m_ffn2_w_down'], 'm_ffn2_post_norm': out['m_ffn2_post_norm'], 'v_meta_tokens': out['v_meta_tokens'], 'v_ffn1_pre_norm': out['v_ffn1_pre_norm'], 'v_ffn1_w_gate': out['v_ffn1_w_gate'], 'v_ffn1_w_up': out['v_ffn1_w_up'], 'v_ffn1_w_down': out['v_ffn1_w_down'], 'v_ffn1_post_norm': out['v_ffn1_post_norm'], 'v_mix_pre_norm': out['v_mix_pre_norm'], 'v_w_in': out['v_w_in'], 'v_gla_w_a2': out['v_gla_w_a2'], 'v_gla_b_a': out['v_gla_b_a'], 'v_gla_out_norm': out['v_gla_out_norm'], 'v_swa_sinks': out['v_swa_sinks'], 'v_swa_out_norm': out['v_swa_out_norm'], 'v_w_out': out['v_w_out'], 'v_mix_post_norm': out['v_mix_post_norm'], 'v_ffn2_pre_norm': out['v_ffn2_pre_norm'], 'v_ffn2_w_gate': out['v_ffn2_w_gate'], 'v_ffn2_w_up': out['v_ffn2_w_up'], 'v_ffn2_w_down': out['v_ffn2_w_down'], 'v_ffn2_post_norm': out['v_ffn2_post_norm']}


def _loss(weights, diff, rest, loss_target):
    with _jax.named_scope("forward"):
        args = {**rest, TWIN_DIFF_INPUT: diff, **{k: w.astype(_WEIGHT_DTYPES[k]) for k, w in weights.items()}}
        y = _forward(args)
    with _jax.named_scope("loss_head"):
        err = _jnp.square(y.astype(_jnp.float32) - loss_target)
        return 0.5 * _jnp.sum(_jnp.mean(err, axis=-1)) if err.ndim else 0.5 * err


def _adamw(w, g, m, v):
    m = ADAM_B1 * m + (1.0 - ADAM_B1) * g
    v = ADAM_B2 * v + (1.0 - ADAM_B2) * _jnp.square(g)
    m_hat = m / (1.0 - ADAM_B1 ** ADAM_STEP)
    v_hat = v / (1.0 - ADAM_B2 ** ADAM_STEP)
    delta = -ADAM_LR * (m_hat / (_jnp.sqrt(v_hat) + ADAM_EPS) + ADAM_WD * w)
    return delta, m, v


def reference(x, meta_tokens, ffn1_pre_norm, ffn1_w_gate, ffn1_w_up, ffn1_w_down, ffn1_post_norm, mix_pre_norm, w_in, gla_w_a2, gla_b_a, gla_out_norm, swa_sinks, swa_out_norm, w_out, mix_post_norm, ffn2_pre_norm, ffn2_w_gate, ffn2_w_up, ffn2_w_down, ffn2_post_norm, loss_target, m_meta_tokens, m_ffn1_pre_norm, m_ffn1_w_gate, m_ffn1_w_up, m_ffn1_w_down, m_ffn1_post_norm, m_mix_pre_norm, m_w_in, m_gla_w_a2, m_gla_b_a, m_gla_out_norm, m_swa_sinks, m_swa_out_norm, m_w_out, m_mix_post_norm, m_ffn2_pre_norm, m_ffn2_w_gate, m_ffn2_w_up, m_ffn2_w_down, m_ffn2_post_norm, v_meta_tokens, v_ffn1_pre_norm, v_ffn1_w_gate, v_ffn1_w_up, v_ffn1_w_down, v_ffn1_post_norm, v_mix_pre_norm, v_w_in, v_gla_w_a2, v_gla_b_a, v_gla_out_norm, v_swa_sinks, v_swa_out_norm, v_w_out, v_mix_post_norm, v_ffn2_pre_norm, v_ffn2_w_gate, v_ffn2_w_up, v_ffn2_w_down, v_ffn2_post_norm):
    given = dict(x=x, meta_tokens=meta_tokens, ffn1_pre_norm=ffn1_pre_norm, ffn1_w_gate=ffn1_w_gate, ffn1_w_up=ffn1_w_up, ffn1_w_down=ffn1_w_down, ffn1_post_norm=ffn1_post_norm, mix_pre_norm=mix_pre_norm, w_in=w_in, gla_w_a2=gla_w_a2, gla_b_a=gla_b_a, gla_out_norm=gla_out_norm, swa_sinks=swa_sinks, swa_out_norm=swa_out_norm, w_out=w_out, mix_post_norm=mix_post_norm, ffn2_pre_norm=ffn2_pre_norm, ffn2_w_gate=ffn2_w_gate, ffn2_w_up=ffn2_w_up, ffn2_w_down=ffn2_w_down, ffn2_post_norm=ffn2_post_norm, loss_target=loss_target, m_meta_tokens=m_meta_tokens, m_ffn1_pre_norm=m_ffn1_pre_norm, m_ffn1_w_gate=m_ffn1_w_gate, m_ffn1_w_up=m_ffn1_w_up, m_ffn1_w_down=m_ffn1_w_down, m_ffn1_post_norm=m_ffn1_post_norm, m_mix_pre_norm=m_mix_pre_norm, m_w_in=m_w_in, m_gla_w_a2=m_gla_w_a2, m_gla_b_a=m_gla_b_a, m_gla_out_norm=m_gla_out_norm, m_swa_sinks=m_swa_sinks, m_swa_out_norm=m_swa_out_norm, m_w_out=m_w_out, m_mix_post_norm=m_mix_post_norm, m_ffn2_pre_norm=m_ffn2_pre_norm, m_ffn2_w_gate=m_ffn2_w_gate, m_ffn2_w_up=m_ffn2_w_up, m_ffn2_w_down=m_ffn2_w_down, m_ffn2_post_norm=m_ffn2_post_norm, v_meta_tokens=v_meta_tokens, v_ffn1_pre_norm=v_ffn1_pre_norm, v_ffn1_w_gate=v_ffn1_w_gate, v_ffn1_w_up=v_ffn1_w_up, v_ffn1_w_down=v_ffn1_w_down, v_ffn1_post_norm=v_ffn1_post_norm, v_mix_pre_norm=v_mix_pre_norm, v_w_in=v_w_in, v_gla_w_a2=v_gla_w_a2, v_gla_b_a=v_gla_b_a, v_gla_out_norm=v_gla_out_norm, v_swa_sinks=v_swa_sinks, v_swa_out_norm=v_swa_out_norm, v_w_out=v_w_out, v_mix_post_norm=v_mix_post_norm, v_ffn2_pre_norm=v_ffn2_pre_norm, v_ffn2_w_gate=v_ffn2_w_gate, v_ffn2_w_up=v_ffn2_w_up, v_ffn2_w_down=v_ffn2_w_down, v_ffn2_post_norm=v_ffn2_post_norm)
    weights = {n: given[n] for n in TWIN_WEIGHTS}
    shared = {n: given[n] for n in SHARED_INPUTS}
    per_example = {n: given[n] for n in ['x']}
    grad_fn = _jax.value_and_grad(_loss, argnums=(0, 1))

    def one_microbatch(ex, loss_target):
        ex = dict(ex)
        diff = ex.pop(TWIN_DIFF_INPUT)
        return grad_fn(weights, diff, {**shared, **ex}, loss_target)

    if N_MICROBATCH == 1:
        loss, (grad_w, grad_x) = one_microbatch(per_example, given["loss_target"])
    else:
        def body(carry, xs):
            loss_sum, grad_sum = carry
            l_k, (gw_k, gx_k) = one_microbatch(xs[0], xs[1])
            with _jax.named_scope("update"):
                return (loss_sum + l_k, _jax.tree.map(_jnp.add, grad_sum, gw_k)), gx_k

        init = (_jnp.zeros((), _jnp.float32), _jax.tree.map(_jnp.zeros_like, weights))
        (loss, grad_w), grad_x = _jax.lax.scan(body, init, (per_example, given["loss_target"]))
    with _jax.named_scope("update"):
        delta_w, new_m, new_v = {}, {}, {}
        for n in TWIN_WEIGHTS:
            delta_w[n], new_m[n], new_v[n] = _adamw(weights[n], grad_w[n], given["m_" + n], given["v_" + n])
    return (loss, grad_x, *[grad_w[n] for n in TWIN_WEIGHTS], *[delta_w[n] for n in TWIN_WEIGHTS],
            *[new_m[n] for n in TWIN_WEIGHTS], *[new_v[n] for n in TWIN_WEIGHTS])
```

```python
import functools
import math

import numpy as np
import jax
import jax.numpy as jnp
from jax import lax
from jax.experimental import pallas as pl
from jax.experimental.pallas import tpu as pltpu

F32 = jnp.float32
BF16 = jnp.bfloat16
MESH = pl.DeviceIdType.MESH

D_MODEL = 1024
D_FF = 2816
N_CHIPS = 4
N_DEV = 8
N_META = 16
BLK = 128
PAD = BLK - N_META
GLA_CHUNK = 64
GLA_HEADS = 4
GLA_DV = 128
GLA_DK = 64
GLA_KW = GLA_HEADS * GLA_DK
GLA_W = GLA_HEADS * GLA_DV
GLA_RANK = 16
GLA_TAU = 16.0
SWA_HD = 64
SWA_QH = 8
SWA_KVH = 2
SWA_W = SWA_QH * SWA_HD
WINDOW = 128
ROPE_THETA = 10000.0
EPS = 1e-6
NEG_INF = -1e30
IN_SPLITS = (256, 256, 512, 512, 16, 512, 128, 128)
D_IN = sum(IN_SPLITS)
P_GQ, P_GK, P_GV, P_GG, P_GA, P_SQ, P_SK, P_SV, P_END = 0, 256, 512, 1024, 1536, 1664, 2176, 2432, 2688
ADAM_LR, ADAM_B1, ADAM_B2, ADAM_EPS, ADAM_WD, ADAM_STEP = 0.001, 0.9, 0.999, 1e-08, 0.01, 10
VMEM_LIMIT = 56 * 1024 * 1024

NT = (((1,), (1,)), ((), ()))
TN = (((0,), (0,)), ((), ()))


def _cparams(n_axes):
    return pltpu.CompilerParams(dimension_semantics=("arbitrary",) * n_axes, vmem_limit_bytes=VMEM_LIMIT)


def _row_tile(t):
    for tm in (640, 512, 384, 256, 128):
        if t % tm == 0:
            return tm
    raise ValueError(t)


def _div_tile(r, cap=512):
    best = None
    for tr in range(8, min(r, cap) + 1, 8):
        if r % tr == 0:
            best = tr
    return best if best is not None else r


def _dot(a, b):
    return jnp.dot(a, b, preferred_element_type=F32)


def _dg(a, b, dims):
    return lax.dot_general(a, b, dims, preferred_element_type=F32)


def _rms(x, w):
    r = lax.rsqrt(jnp.mean(x * x, axis=-1, keepdims=True) + EPS)
    xh = x * r
    return xh * w, xh, r


def _rms_bwd(xh, r, w, dy):
    wdy = dy * w
    dx = r * (wdy - xh * jnp.mean(wdy * xh, axis=-1, keepdims=True))
    dw = jnp.sum(dy * xh, axis=0, keepdims=True)
    return dx, dw


def _sigmoid(x):
    return 1.0 / (1.0 + jnp.exp(-x))


def _full(shape):
    nd = len(shape)
    return pl.BlockSpec(shape, lambda *_: (0,) * nd)


def _ffn_fwd(h, wpre, wg4, wu4, wd4, wpost):
    t = h.shape[0]
    tm = _row_tile(t)
    nj, _, fj = wg4.shape

    def body(h_ref, wpre_ref, wg_ref, wu_ref, wd_ref, wpost_ref, hout_ref, n_ref, g_ref, u_ref, f_ref, acc_ref):
        j = pl.program_id(1)

        @pl.when(j == 0)
        def _():
            y, _, _ = _rms(h_ref[...], wpre_ref[...])
            n_ref[...] = y.astype(BF16)
            acc_ref[...] = jnp.zeros_like(acc_ref)

        n = n_ref[...]
        g = _dot(n, wg_ref[...])
        u = _dot(n, wu_ref[...])
        g_ref[...] = g.astype(BF16)
        u_ref[...] = u.astype(BF16)
        a = g * _sigmoid(g) * u
        acc_ref[...] += _dot(a.astype(BF16), wd_ref[...])

        @pl.when(j == nj - 1)
        def _():
            f = acc_ref[...]
            f_ref[...] = f
            y, _, _ = _rms(f, wpost_ref[...])
            hout_ref[...] = h_ref[...] + 0.5 * y

    row = pl.BlockSpec((tm, D_MODEL), lambda i, j: (i, 0))
    vec = pl.BlockSpec((1, D_MODEL), lambda i, j: (0, 0))
    wcol = pl.BlockSpec((None, D_MODEL, fj), lambda i, j: (j, 0, 0))
    wrow = pl.BlockSpec((None, fj, D_MODEL), lambda i, j: (j, 0, 0))
    act = pl.BlockSpec((None, tm, fj), lambda i, j: (j, i, 0))
    return pl.pallas_call(
        body, name="ffn_fwd", grid=(t // tm, nj),
        in_specs=[row, vec, wcol, wcol, wrow, vec],
        out_specs=[row, row, act, act, row],
        out_shape=[jax.ShapeDtypeStruct((t, D_MODEL), F32), jax.ShapeDtypeStruct((t, D_MODEL), BF16),
                   jax.ShapeDtypeStruct((nj, t, fj), BF16), jax.ShapeDtypeStruct((nj, t, fj), BF16),
                   jax.ShapeDtypeStruct((t, D_MODEL), F32)],
        scratch_shapes=[pltpu.VMEM((tm, D_MODEL), F32)],
        compiler_params=_cparams(2),
    )(h, wpre, wg4, wu4, wd4, wpost)


def _ffn_bwd(dhout, h, f, g4, u4, wpre, wg4, wu4, wd4, wpost):
    t = h.shape[0]
    tm = _row_tile(t)
    nj, _, fj = wg4.shape

    def body(dhout_ref, h_ref, f_ref, g_ref, u_ref, wpre_ref, wg_ref, wu_ref, wd_ref, wpost_ref,
             dh_ref, df_ref, dg_ref, du_ref, a_ref, dwpre_ref, dwpost_ref, dn_ref):
        i = pl.program_id(0)
        j = pl.program_id(1)

        @pl.when((i == 0) & (j == 0))
        def _():
            dwpre_ref[...] = jnp.zeros_like(dwpre_ref)
            dwpost_ref[...] = jnp.zeros_like(dwpost_ref)

        @pl.when(j == 0)
        def _():
            wpost = wpost_ref[...]
            _, fh, r = _rms(f_ref[...], wpost)
            df, dw = _rms_bwd(fh, r, wpost, 0.5 * dhout_ref[...])
            dwpost_ref[...] += dw
            df_ref[...] = df.astype(BF16)
            dn_ref[...] = jnp.zeros_like(dn_ref)

        da = _dg(df_ref[...], wd_ref[...], NT)
        g = g_ref[...].astype(F32)
        u = u_ref[...].astype(F32)
        sg = _sigmoid(g)
        silu = g * sg
        dg = (da * u * (sg * (1.0 + g * (1.0 - sg)))).astype(BF16)
        du = (da * silu).astype(BF16)
        dg_ref[...] = dg
        du_ref[...] = du
        a_ref[...] = (silu * u).astype(BF16)
        dn_ref[...] += _dg(dg, wg_ref[...], NT) + _dg(du, wu_ref[...], NT)

        @pl.when(j == nj - 1)
        def _():
            wpre = wpre_ref[...]
            _, hh, r = _rms(h_ref[...], wpre)
            dx, dw = _rms_bwd(hh, r, wpre, dn_ref[...])
            dwpre_ref[...] += dw
            dh_ref[...] = dhout_ref[...] + dx

    row = pl.BlockSpec((tm, D_MODEL), lambda i, j: (i, 0))
    vec = pl.BlockSpec((1, D_MODEL), lambda i, j: (0, 0))
    wcol = pl.BlockSpec((None, D_MODEL, fj), lambda i, j: (j, 0, 0))
    wrow = pl.BlockSpec((None, fj, D_MODEL), lambda i, j: (j, 0, 0))
    act = pl.BlockSpec((None, tm, fj), lambda i, j: (j, i, 0))
    actshape = jax.ShapeDtypeStruct((nj, t, fj), BF16)
    return pl.pallas_call(
        body, name="ffn_bwd", grid=(t // tm, nj),
        in_specs=[row, row, row, act, act, vec, wcol, wcol, wrow, vec],
        out_specs=[row, row, act, act, act, vec, vec],
        out_shape=[jax.ShapeDtypeStruct((t, D_MODEL), F32), jax.ShapeDtypeStruct((t, D_MODEL), BF16),
                   actshape, actshape, actshape,
                   jax.ShapeDtypeStruct((1, D_MODEL), F32), jax.ShapeDtypeStruct((1, D_MODEL), F32)],
        scratch_shapes=[pltpu.VMEM((tm, D_MODEL), F32)],
        compiler_params=_cparams(2),
    )(dhout, h, f, g4, u4, wpre, wg4, wu4, wd4, wpost)


def _ffn_wgrad(n, df, dg4, du4, a4):
    t = n.shape[0]
    tm = _row_tile(t)
    nj, _, fj = dg4.shape

    def body(n_ref, df_ref, dg_ref, du_ref, a_ref, dwg_ref, dwu_ref, dwd_ref):
        @pl.when(pl.program_id(1) == 0)
        def _():
            dwg_ref[...] = jnp.zeros_like(dwg_ref)
            dwu_ref[...] = jnp.zeros_like(dwu_ref)
            dwd_ref[...] = jnp.zeros_like(dwd_ref)

        nn = n_ref[...]
        dwg_ref[...] += _dg(nn, dg_ref[...], TN)
        dwu_ref[...] += _dg(nn, du_ref[...], TN)
        dwd_ref[...] += _dg(a_ref[...], df_ref[...], TN)

    row = pl.BlockSpec((tm, D_MODEL), lambda j, i: (i, 0))
    act = pl.BlockSpec((None, tm, fj), lambda j, i: (j, i, 0))
    wcol = pl.BlockSpec((None, D_MODEL, fj), lambda j, i: (j, 0, 0))
    wrow = pl.BlockSpec((None, fj, D_MODEL), lambda j, i: (j, 0, 0))
    return pl.pallas_call(
        body, name="ffn_wgrad", grid=(nj, t // tm),
        in_specs=[row, row, act, act, act],
        out_specs=[wcol, wcol, wrow],
        out_shape=[jax.ShapeDtypeStruct((nj, D_MODEL, fj), F32), jax.ShapeDtypeStruct((nj, D_MODEL, fj), F32),
                   jax.ShapeDtypeStruct((nj, fj, D_MODEL), F32)],
        compiler_params=_cparams(2),
    )(n, df, dg4, du4, a4)


def _xty(x, y):
    t, k = x.shape
    n = y.shape[1]
    tm = _row_tile(t)
    tn = n if n <= 1024 else (896 if n % 896 == 0 else 128)

    def body(x_ref, y_ref, o_ref):
        @pl.when(pl.program_id(1) == 0)
        def _():
            o_ref[...] = jnp.zeros_like(o_ref)

        o_ref[...] += _dg(x_ref[...], y_ref[...], TN)

    return pl.pallas_call(
        body, name="xty", grid=(n // tn, t // tm),
        in_specs=[pl.BlockSpec((tm, k), lambda j, i: (i, 0)), pl.BlockSpec((tm, tn), lambda j, i: (i, j))],
        out_specs=pl.BlockSpec((k, tn), lambda j, i: (0, j)),
        out_shape=jax.ShapeDtypeStruct((k, n), F32),
        compiler_params=_cparams(2),
    )(x, y)


def _rope_tables(t):
    pos = (jnp.arange(t, dtype=jnp.int32) - PAD).astype(F32)
    inv_freq = 1.0 / (ROPE_THETA ** (jnp.arange(0, SWA_HD, 2, dtype=F32) / SWA_HD))
    ang = pos[:, None] * inv_freq[None, :]
    cos = jnp.cos(ang)
    sin = jnp.sin(ang)
    return jnp.concatenate([cos, cos, cos, cos], axis=1), jnp.concatenate([-sin, sin, -sin, sin], axis=1)


def _rot_half(x, first_half):
    return jnp.where(first_half, pltpu.roll(x, 96, 1), pltpu.roll(x, 32, 1))


def _first_half_mask(rows):
    lane = lax.broadcasted_iota(jnp.int32, (rows, 128), 1)
    return (lane % 64) < 32


def _log_sigmoid(z):
    return jnp.minimum(z, 0.0) - jnp.log(1.0 + jnp.exp(-jnp.abs(z)))


def _mix_proj(h1, wmixpre, winp, wa2p, bap, cos, sin):
    t = h1.shape[0]
    tm = _row_tile(t)

    def body(h_ref, w_ref, win_ref, wa2_ref, ba_ref, cos_ref, sin_ref,
             n_ref, gq_ref, gk_ref, gv_ref, gg_ref, ga_ref, la_ref, sq_ref, sk_ref, sv_ref):
        y, _, _ = _rms(h_ref[...], w_ref[...])
        n = y.astype(BF16)
        n_ref[...] = n
        proj = _dot(n, win_ref[...])
        gq_ref[...] = proj[:, P_GQ:P_GK]
        gk_ref[...] = proj[:, P_GK:P_GV]
        gv_ref[...] = proj[:, P_GV:P_GG]
        gg_ref[...] = proj[:, P_GG:P_GA]
        ga = proj[:, P_GA:P_SQ]
        ga_ref[...] = ga
        z = _dot(ga.astype(BF16), wa2_ref[...]) + ba_ref[...]
        la_ref[...] = _log_sigmoid(z) * (1.0 / GLA_TAU)
        c = cos_ref[...]
        s = sin_ref[...]
        fh = _first_half_mask(tm)
        for k in range(4):
            x = proj[:, P_SQ + 128 * k:P_SQ + 128 * (k + 1)]
            sq_ref[:, 128 * k:128 * (k + 1)] = (x * c + _rot_half(x, fh) * s).astype(BF16)
        for k in range(2):
            x = proj[:, P_SK + 128 * k:P_SK + 128 * (k + 1)]
            sk_ref[:, 128 * k:128 * (k + 1)] = (x * c + _rot_half(x, fh) * s).astype(BF16)
        sv_ref[...] = proj[:, P_SV:P_END].astype(BF16)

    def row(w):
        return pl.BlockSpec((tm, w), lambda i: (i, 0))

    def rshape(w, dt):
        return jax.ShapeDtypeStruct((t, w), dt)

    return pl.pallas_call(
        body, name="mix_proj", grid=(t // tm,),
        in_specs=[row(D_MODEL), _full((1, D_MODEL)), _full((D_MODEL, P_END)), _full((128, GLA_KW)),
                  _full((1, GLA_KW)), row(128), row(128)],
        out_specs=[row(D_MODEL), row(256), row(256), row(512), row(512), row(128), row(256), row(512), row(256),
                   row(256)],
        out_shape=[rshape(D_MODEL, BF16), rshape(256, F32), rshape(256, F32), rshape(512, F32), rshape(512, F32),
                   rshape(128, F32), rshape(256, F32), rshape(512, BF16), rshape(256, BF16), rshape(256, BF16)],
        compiler_params=_cparams(1),
    )(h1, wmixpre, winp, wa2p, bap, cos, sin)


def _gla_cumsum(la, tril_f):
    b = jnp.dot(tril_f, la, precision=lax.Precision.HIGHEST, preferred_element_type=F32)
    row = lax.broadcasted_iota(jnp.int32, b.shape, 0)
    bm = jnp.sum(jnp.where(row == GLA_CHUNK // 2 - 1, b, 0.0), axis=0, keepdims=True)
    bl = jnp.sum(jnp.where(row == GLA_CHUNK - 1, b, 0.0), axis=0, keepdims=True)
    return b, bm, bl


def _gla_chunk_terms(la, q, k, tril_f):
    b, bm, bl = _gla_cumsum(la, tril_f)
    qs = q * (GLA_DK ** -0.5)
    qt = qs * jnp.exp(b - bm)
    kt = k * jnp.exp(bm - b)
    qh = qs * jnp.exp(b)
    kh = k * jnp.exp(bl - b)
    ebl = jnp.exp(bl)
    return qt, kt, qh, kh, ebl


def _gla_fwd(gq, gk, gv, la):
    t = gq.shape[0]
    nb = t // BLK
    ncb = BLK // GLA_CHUNK

    def body(q_ref, k_ref, v_ref, la_ref, o_ref, ss_ref, st_ref):
        @pl.when(pl.program_id(0) == 0)
        def _():
            st_ref[...] = jnp.zeros_like(st_ref)

        r = lax.broadcasted_iota(jnp.int32, (GLA_CHUNK, GLA_CHUNK), 0)
        c = lax.broadcasted_iota(jnp.int32, (GLA_CHUNK, GLA_CHUNK), 1)
        tril = r >= c
        tril_f = tril.astype(F32)
        lane = lax.broadcasted_iota(jnp.int32, (GLA_CHUNK, 128), 1)
        halves = (lane < 64, lane >= 64)
        for ch in range(ncb):
            rows = slice(ch * GLA_CHUNK, (ch + 1) * GLA_CHUNK)
            qt, kt, qh, kh, ebl = _gla_chunk_terms(la_ref[rows, :], q_ref[rows, :], k_ref[rows, :], tril_f)
            for hp in range(2):
                ls = slice(128 * hp, 128 * (hp + 1))
                kt2 = kt[:, ls].astype(BF16)
                kh2 = kh[:, ls].astype(BF16)
                for e in range(2):
                    h = 2 * hp + e
                    vs = slice(GLA_DV * h, GLA_DV * (h + 1))
                    st = st_ref[h]
                    ss_ref[ch, h] = st
                    v = v_ref[rows, vs].astype(BF16)
                    qtm = jnp.where(halves[e], qt[:, ls], 0.0).astype(BF16)
                    qhm = jnp.where(halves[e], qh[:, ls], 0.0).astype(BF16)
                    a = jnp.where(tril, _dg(qtm, kt2, NT), 0.0)
                    o_ref[rows, vs] = _dot(a.astype(BF16), v) + _dg(qhm, st.astype(BF16), NT)
                    st_ref[h] = st * ebl[:, ls] + _dg(v, kh2, TN)

    def row(w):
        return pl.BlockSpec((BLK, w), lambda i: (i, 0))

    return pl.pallas_call(
        body, name="gla_fwd", grid=(nb,),
        in_specs=[row(256), row(256), row(512), row(256)],
        out_specs=[row(512), pl.BlockSpec((ncb, GLA_HEADS, GLA_DV, 128), lambda i: (i, 0, 0, 0))],
        out_shape=[jax.ShapeDtypeStruct((t, GLA_W), F32),
                   jax.ShapeDtypeStruct((nb * ncb, GLA_HEADS, GLA_DV, 128), F32)],
        scratch_shapes=[pltpu.VMEM((GLA_HEADS, GLA_DV, 128), F32)],
        compiler_params=_cparams(1),
    )(gq, gk, gv, la)


def _gla_bwd(gq, gk, gv, la, ss, do):
    t = gq.shape[0]
    nb = t // BLK
    ncb = BLK // GLA_CHUNK

    def body(q_ref, k_ref, v_ref, la_ref, ss_ref, do_ref, dq_ref, dk_ref, dv_ref, dla_ref, dst_ref):
        @pl.when(pl.program_id(0) == 0)
        def _():
            dst_ref[...] = jnp.zeros_like(dst_ref)

        r = lax.broadcasted_iota(jnp.int32, (GLA_CHUNK, GLA_CHUNK), 0)
        c = lax.broadcasted_iota(jnp.int32, (GLA_CHUNK, GLA_CHUNK), 1)
        tril = r >= c
        tril_f = tril.astype(F32)
        triu_f = (r <= c).astype(F32)
        lane = lax.broadcasted_iota(jnp.int32, (GLA_CHUNK, 128), 1)
        halves = (lane < 64, lane >= 64)
        last_row = lax.broadcasted_iota(jnp.int32, (GLA_CHUNK, 128), 0) == GLA_CHUNK - 1
        lane1 = lax.broadcasted_iota(jnp.int32, (1, 128), 1)
        halves1 = (lane1 < 64, lane1 >= 64)
        scale = GLA_DK ** -0.5
        for ch in reversed(range(ncb)):
            rows = slice(ch * GLA_CHUNK, (ch + 1) * GLA_CHUNK)
            la = la_ref[rows, :]
            b, bm, bl = _gla_cumsum(la, tril_f)
            eq = jnp.exp(b - bm)
            ek = jnp.exp(bm - b)
            eb = jnp.exp(b)
            ekl = jnp.exp(bl - b)
            ebl = jnp.exp(bl)
            qs = q_ref[rows, :] * scale
            kk = k_ref[rows, :]
            qt, kt, qh, kh = qs * eq, kk * ek, qs * eb, kk * ekl
            for hp in range(2):
                ls = slice(128 * hp, 128 * (hp + 1))
                kt2 = kt[:, ls].astype(BF16)
                kh2 = kh[:, ls].astype(BF16)
                dqt = jnp.zeros((GLA_CHUNK, 128), F32)
                dkt = jnp.zeros((GLA_CHUNK, 128), F32)
                dqh = jnp.zeros((GLA_CHUNK, 128), F32)
                dkh = jnp.zeros((GLA_CHUNK, 128), F32)
                dbl = jnp.zeros((1, 128), F32)
                for e in range(2):
                    h = 2 * hp + e
                    vs = slice(GLA_DV * h, GLA_DV * (h + 1))
                    m = halves[e]
                    st = ss_ref[ch, h]
                    dstn = dst_ref[h]
                    v = v_ref[rows, vs].astype(BF16)
                    dov = do_ref[rows, vs].astype(BF16)
                    qtm = jnp.where(m, qt[:, ls], 0.0).astype(BF16)
                    qhm = jnp.where(m, qh[:, ls], 0.0).astype(BF16)
                    khm = jnp.where(m, kh[:, ls], 0.0).astype(BF16)
                    a = jnp.where(tril, _dg(qtm, kt2, NT), 0.0).astype(BF16)
                    da = jnp.where(tril, _dg(dov, v, NT), 0.0).astype(BF16)
                    dstn_b = dstn.astype(BF16)
                    dv_ref[rows, vs] = _dg(a, dov, TN) + _dg(khm, dstn_b, NT)
                    dqt = dqt + jnp.where(m, _dot(da, kt2), 0.0)
                    dkt = dkt + _dg(da, qtm, TN)
                    dqh = dqh + jnp.where(m, _dot(dov, st.astype(BF16)), 0.0)
                    dkh = dkh + jnp.where(m, _dot(v, dstn_b), 0.0)
                    dbl = dbl + jnp.where(halves1[e], jnp.sum(dstn * st, axis=0, keepdims=True), 0.0)
                    dst_ref[h] = dstn * ebl[:, ls] + _dg(dov, qhm, TN)
                dq_ref[rows, ls] = scale * (dqt * eq[:, ls] + dqh * eb[:, ls])
                dk_ref[rows, ls] = dkt * ek[:, ls] + dkh * ekl[:, ls]
                dkk = dkh * kh[:, ls]
                db = dqt * qt[:, ls] - dkt * kt[:, ls] + dqh * qh[:, ls] - dkk
                db_last = jnp.sum(dkk, axis=0, keepdims=True) + ebl[:, ls] * dbl
                db = db + jnp.where(last_row, db_last, 0.0)
                dla_ref[rows, ls] = jnp.dot(triu_f, db, precision=lax.Precision.HIGHEST,
                                            preferred_element_type=F32)

    def row(w):
        return pl.BlockSpec((BLK, w), lambda i: (nb - 1 - i, 0))

    def rshape(w):
        return jax.ShapeDtypeStruct((t, w), F32)

    return pl.pallas_call(
        body, name="gla_bwd", grid=(nb,),
        in_specs=[row(256), row(256), row(512), row(256),
                  pl.BlockSpec((ncb, GLA_HEADS, GLA_DV, 128), lambda i: (nb - 1 - i, 0, 0, 0)), row(512)],
        out_specs=[row(256), row(256), row(512), row(256)],
        out_shape=[rshape(256), rshape(256), rshape(512), rshape(256)],
        scratch_shapes=[pltpu.VMEM((GLA_HEADS, GLA_DV, 128), F32)],
        compiler_params=_cparams(1),
    )(gq, gk, gv, la, ss, do)


def _swa_mask(n):
    r = lax.broadcasted_iota(jnp.int32, (BLK, 3 * BLK), 0)
    c = lax.broadcasted_iota(jnp.int32, (BLK, 3 * BLK), 1)
    seg = c // BLK
    cc = c % BLK
    qpos = n * BLK + r - PAD
    kpos = jnp.where(seg == 0, (n - 1) * BLK, jnp.where(seg == 1, n * BLK, 0)) + cc - PAD
    band = (seg < 2) & (kpos >= N_META) & (kpos <= qpos) & (qpos - kpos < WINDOW)
    meta = (seg == 2) & (kpos >= 0) & (kpos < N_META) & (kpos <= qpos)
    return band | meta


def _swa_probs(qm, kall, mask, sink):
    s = _dg(qm, kall, NT) * (SWA_HD ** -0.5)
    s = jnp.where(mask, s, NEG_INF)
    m = jnp.maximum(jnp.max(s, axis=-1, keepdims=True), sink)
    p = jnp.exp(s - m)
    es = jnp.exp(sink - m)
    inv = 1.0 / (jnp.sum(p, axis=-1, keepdims=True) + es)
    return p * inv, es * inv


def _swa_fwd(sinks, sq, sk, sv):
    t = sq.shape[0]
    nb = t // BLK

    def body(sink_ref, q_ref, kp_ref, kc_ref, km_ref, vp_ref, vc_ref, vm_ref, o_ref):
        n = pl.program_id(0)
        mask = _swa_mask(n)
        lo = lax.broadcasted_iota(jnp.int32, (BLK, 128), 1) < 64
        for kh in range(SWA_KVH):
            ls = slice(128 * kh, 128 * (kh + 1))
            kall = jnp.concatenate([kp_ref[:, ls], kc_ref[:, ls], km_ref[:, ls]], axis=0)
            vall = jnp.concatenate([vp_ref[:, ls], vc_ref[:, ls], vm_ref[:, ls]], axis=0)
            for g in range(2):
                ps = slice(128 * (2 * kh + g), 128 * (2 * kh + g + 1))
                qp = q_ref[:, ps]
                outs = []
                for e in range(2):
                    h = 4 * kh + 2 * g + e
                    qm = jnp.where(lo if e == 0 else ~lo, qp, jnp.zeros_like(qp))
                    p, _ = _swa_probs(qm, kall, mask, sink_ref[h])
                    outs.append(_dot(p.astype(BF16), vall))
                o_ref[:, ps] = jnp.where(lo, outs[0], outs[1])

    cur = lambda w: pl.BlockSpec((BLK, w), lambda i: (i, 0))
    prev = lambda w: pl.BlockSpec((BLK, w), lambda i: (jnp.maximum(i - 1, 0), 0))
    first = lambda w: pl.BlockSpec((BLK, w), lambda i: (0, 0))
    return pl.pallas_call(
        body, name="swa_fwd", grid=(nb,),
        in_specs=[pl.BlockSpec(memory_space=pltpu.SMEM), cur(512), prev(256), cur(256), first(256),
                  prev(256), cur(256), first(256)],
        out_specs=cur(512),
        out_shape=jax.ShapeDtypeStruct((t, SWA_W), F32),
        compiler_params=_cparams(1),
    )(sinks, sq, sk, sk, sk, sv, sv, sv)


def _swa_bwd(sinks, sq, sk, sv, o, do):
    t = sq.shape[0]
    nb = t // BLK

    def body(sink_ref, q_ref, kp_ref, kc_ref, km_ref, vp_ref, vc_ref, vm_ref, o_ref, do_ref,
             dq_ref, dk_ref, dv_ref, dkm_ref, dvm_ref, dsink_ref, ck_ref, cv_ref):
        n = pl.program_id(0)

        @pl.when(n == 0)
        def _():
            ck_ref[...] = jnp.zeros_like(ck_ref)
            cv_ref[...] = jnp.zeros_like(cv_ref)
            dkm_ref[...] = jnp.zeros_like(dkm_ref)
            dvm_ref[...] = jnp.zeros_like(dvm_ref)
            dsink_ref[...] = jnp.zeros_like(dsink_ref)

        @pl.when(n == nb)
        def _():
            dk_ref[...] = ck_ref[...]
            dv_ref[...] = cv_ref[...]

        @pl.when(n < nb)
        def _():
            mask = _swa_mask(n)
            lo = lax.broadcasted_iota(jnp.int32, (BLK, 128), 1) < 64
            scale = SWA_HD ** -0.5
            for kh in range(SWA_KVH):
                ls = slice(128 * kh, 128 * (kh + 1))
                kall = jnp.concatenate([kp_ref[:, ls], kc_ref[:, ls], km_ref[:, ls]], axis=0)
                vall = jnp.concatenate([vp_ref[:, ls], vc_ref[:, ls], vm_ref[:, ls]], axis=0)
                dkall = jnp.zeros((3 * BLK, 128), F32)
                dvall = jnp.zeros((3 * BLK, 128), F32)
                for g in range(2):
                    ps = slice(128 * (2 * kh + g), 128 * (2 * kh + g + 1))
                    qp = q_ref[:, ps]
                    dop = do_ref[:, ps]
                    op = o_ref[:, ps]
                    dqs = []
                    for e in range(2):
                        h = 4 * kh + 2 * g + e
                        half = lo if e == 0 else ~lo
                        qm = jnp.where(half, qp, jnp.zeros_like(qp))
                        dom = jnp.where(half, dop, 0.0)
                        p, psink = _swa_probs(qm, kall, mask, sink_ref[h])
                        delta = jnp.sum(dom * op, axis=-1, keepdims=True)
                        domb = dom.astype(BF16)
                        dp = _dg(domb, vall, NT)
                        ds = (p * (dp - delta) * scale).astype(BF16)
                        dqs.append(_dot(ds, kall))
                        dkall = dkall + _dg(ds, qm, TN)
                        dvall = dvall + _dg(p.astype(BF16), domb, TN)
                        dsink_ref[h:h + 1, :] += jnp.broadcast_to(-jnp.sum(psink * delta, axis=0, keepdims=True),
                                                                 (1, 128))
                    dq_ref[:, ps] = jnp.where(lo, dqs[0], dqs[1])
                dk_ref[:, ls] = ck_ref[:, ls] + dkall[0:BLK]
                dv_ref[:, ls] = cv_ref[:, ls] + dvall[0:BLK]
                ck_ref[:, ls] = dkall[BLK:2 * BLK]
                cv_ref[:, ls] = dvall[BLK:2 * BLK]
                dkm_ref[:, ls] += dkall[2 * BLK:3 * BLK]
                dvm_ref[:, ls] += dvall[2 * BLK:3 * BLK]

    cur = lambda w: pl.BlockSpec((BLK, w), lambda i: (jnp.minimum(i, nb - 1), 0))
    prev = lambda w: pl.BlockSpec((BLK, w), lambda i: (jnp.maximum(i - 1, 0), 0))
    first = lambda w: pl.BlockSpec((BLK, w), lambda i: (0, 0))
    return pl.pallas_call(
        body, name="swa_bwd", grid=(nb + 1,),
        in_specs=[pl.BlockSpec(memory_space=pltpu.SMEM), cur(512), prev(256), cur(256), first(256),
                  prev(256), cur(256), first(256), cur(512), cur(512)],
        out_specs=[cur(512), prev(256), prev(256), first(256), first(256), _full((SWA_QH, 128))],
        out_shape=[jax.ShapeDtypeStruct((t, SWA_W), F32), jax.ShapeDtypeStruct((t, 256), F32),
                   jax.ShapeDtypeStruct((t, 256), F32), jax.ShapeDtypeStruct((BLK, 256), F32),
                   jax.ShapeDtypeStruct((BLK, 256), F32), jax.ShapeDtypeStruct((SWA_QH, 128), F32)],
        scratch_shapes=[pltpu.VMEM((BLK, 256), F32), pltpu.VMEM((BLK, 256), F32)],
        compiler_params=_cparams(1),
    )(sinks, sq, sk, sk, sk, sv, sv, sv, o, do)


def _mix_out(h1, ogla, gg, oswa, wgn, wsn, wout, wpost):
    t = h1.shape[0]
    tm = _row_tile(t)

    def body(h_ref, og_ref, gg_ref, os_ref, wgn_ref, wsn_ref, wout_ref, wpost_ref, h2_ref, cat_ref, m_ref):
        parts = []
        for h in range(GLA_HEADS):
            ls = slice(GLA_DV * h, GLA_DV * (h + 1))
            y, _, _ = _rms(og_ref[:, ls], wgn_ref[...])
            g = gg_ref[:, ls]
            parts.append(y * (g * _sigmoid(g)))
        ys, _, _ = _rms(os_ref[...], wsn_ref[...])
        cat = jnp.concatenate(parts + [ys], axis=1).astype(BF16)
        cat_ref[...] = cat
        m = _dot(cat, wout_ref[...])
        m_ref[...] = m
        y, _, _ = _rms(m, wpost_ref[...])
        h2_ref[...] = h_ref[...] + y

    def row(w):
        return pl.BlockSpec((tm, w), lambda i: (i, 0))

    return pl.pallas_call(
        body, name="mix_out", grid=(t // tm,),
        in_specs=[row(D_MODEL), row(512), row(512), row(512), _full((1, GLA_DV)), _full((1, SWA_W)),
                  _full((D_MODEL, D_MODEL)), _full((1, D_MODEL))],
        out_specs=[row(D_MODEL), row(D_MODEL), row(D_MODEL)],
        out_shape=[jax.ShapeDtypeStruct((t, D_MODEL), F32), jax.ShapeDtypeStruct((t, D_MODEL), BF16),
                   jax.ShapeDtypeStruct((t, D_MODEL), F32)],
        compiler_params=_cparams(1),
    )(h1, ogla, gg, oswa, wgn, wsn, wout, wpost)


def _mix_out_bwd(dh2, m, ogla, gg, oswa, wgn, wsn, wout, wpost):
    t = dh2.shape[0]
    tm = _row_tile(t)

    def body(dh_ref, m_ref, og_ref, gg_ref, os_ref, wgn_ref, wsn_ref, wout_ref, wpost_ref,
             dog_ref, dgg_ref, dos_ref, dm_ref, dwpost_ref, dwgn_ref, dwsn_ref):
        @pl.when(pl.program_id(0) == 0)
        def _():
            dwpost_ref[...] = jnp.zeros_like(dwpost_ref)
            dwgn_ref[...] = jnp.zeros_like(dwgn_ref)
            dwsn_ref[...] = jnp.zeros_like(dwsn_ref)

        wpost = wpost_ref[...]
        _, mh, r = _rms(m_ref[...], wpost)
        dm, dw = _rms_bwd(mh, r, wpost, dh_ref[...])
        dwpost_ref[...] += dw
        dmb = dm.astype(BF16)
        dm_ref[...] = dmb
        dcat = _dg(dmb, wout_ref[...], NT)
        wgn = wgn_ref[...]
        for h in range(GLA_HEADS):
            ls = slice(GLA_DV * h, GLA_DV * (h + 1))
            dog = dcat[:, ls]
            g = gg_ref[:, ls]
            sg = _sigmoid(g)
            y, xh, r = _rms(og_ref[:, ls], wgn)
            dgg_ref[:, ls] = dog * y * (sg * (1.0 + g * (1.0 - sg)))
            dx, dw = _rms_bwd(xh, r, wgn, dog * (g * sg))
            dog_ref[:, ls] = dx
            dwgn_ref[...] += dw
        wsn = wsn_ref[...]
        _, xh, r = _rms(os_ref[...], wsn)
        dx, dw = _rms_bwd(xh, r, wsn, dcat[:, GLA_W:])
        dos_ref[...] = dx
        dwsn_ref[...] += dw

    def row(w):
        return pl.BlockSpec((tm, w), lambda i: (i, 0))

    def rshape(w, dt=F32):
        return jax.ShapeDtypeStruct((t, w), dt)

    return pl.pallas_call(
        body, name="mix_out_bwd", grid=(t // tm,),
        in_specs=[row(D_MODEL), row(D_MODEL), row(512), row(512), row(512), _full((1, GLA_DV)), _full((1, SWA_W)),
                  _full((D_MODEL, D_MODEL)), _full((1, D_MODEL))],
        out_specs=[row(512), row(512), row(512), row(D_MODEL), _full((1, D_MODEL)), _full((1, GLA_DV)),
                   _full((1, SWA_W))],
        out_shape=[rshape(512), rshape(512), rshape(512), rshape(D_MODEL, BF16),
                   jax.ShapeDtypeStruct((1, D_MODEL), F32), jax.ShapeDtypeStruct((1, GLA_DV), F32),
                   jax.ShapeDtypeStruct((1, SWA_W), F32)],
        compiler_params=_cparams(1),
    )(dh2, m, ogla, gg, oswa, wgn, wsn, wout, wpost)


def _mix_in_bwd(dh2, h1, wmixpre, winp, wa2p, bap, cos, sin, ga, dgq, dgk, dgv, dgg, dla, dsq, dsk, dsv, dkm, dvm):
    t = h1.shape[0]
    tm = _row_tile(t)

    def body(dh2_ref, h_ref, w_ref, win_ref, wa2_ref, ba_ref, cos_ref, sin_ref, ga_ref, dgq_ref, dgk_ref, dgv_ref,
             dgg_ref, dla_ref, dsq_ref, dsk_ref, dsv_ref, dkm_ref, dvm_ref,
             dh1_ref, dproj_ref, dw_ref, dwa2_ref, dba_ref):
        i = pl.program_id(0)

        @pl.when(i == 0)
        def _():
            dw_ref[...] = jnp.zeros_like(dw_ref)
            dwa2_ref[...] = jnp.zeros_like(dwa2_ref)
            dba_ref[...] = jnp.zeros_like(dba_ref)

        first = (i == 0).astype(F32)
        c = cos_ref[...]
        s = -sin_ref[...]
        fh = _first_half_mask(tm)
        dproj_ref[:, P_GQ:P_GK] = dgq_ref[...].astype(BF16)
        dproj_ref[:, P_GK:P_GV] = dgk_ref[...].astype(BF16)
        dproj_ref[:, P_GV:P_GG] = dgv_ref[...].astype(BF16)
        dproj_ref[:, P_GG:P_GA] = dgg_ref[...].astype(BF16)
        gab = ga_ref[...].astype(BF16)
        z = _dot(gab, wa2_ref[...]) + ba_ref[...]
        row_id = i * tm + lax.broadcasted_iota(jnp.int32, (tm, 1), 0)
        dz = jnp.where(row_id >= PAD, dla_ref[...] * (1.0 / GLA_TAU) * (1.0 - _sigmoid(z)), 0.0)
        dzb = dz.astype(BF16)
        dba_ref[...] += jnp.sum(dz, axis=0, keepdims=True)
        dwa2_ref[...] += _dg(gab, dzb, TN)
        dproj_ref[:, P_GA:P_SQ] = _dg(dzb, wa2_ref[...], NT).astype(BF16)
        for k in range(4):
            dy = dsq_ref[:, 128 * k:128 * (k + 1)]
            dproj_ref[:, P_SQ + 128 * k:P_SQ + 128 * (k + 1)] = (dy * c + _rot_half(dy, fh) * s).astype(BF16)
        for k in range(2):
            ls = slice(128 * k, 128 * (k + 1))
            dy = dsk_ref[:, ls]
            dy = jnp.concatenate([dy[:BLK] + first * dkm_ref[:, ls], dy[BLK:]], axis=0) if tm > BLK else (
                dy + first * dkm_ref[:, ls])
            dproj_ref[:, P_SK + 128 * k:P_SK + 128 * (k + 1)] = (dy * c + _rot_half(dy, fh) * s).astype(BF16)
            dv = dsv_ref[:, ls]
            dv = jnp.concatenate([dv[:BLK] + first * dvm_ref[:, ls], dv[BLK:]], axis=0) if tm > BLK else (
                dv + first * dvm_ref[:, ls])
            dproj_ref[:, P_SV + 128 * k:P_SV + 128 * (k + 1)] = dv.astype(BF16)
        dn = _dg(dproj_ref[...], win_ref[...], NT)
        w = w_ref[...]
        _, hh, r = _rms(h_ref[...], w)
        dx, dw = _rms_bwd(hh, r, w, dn)
        dw_ref[...] += dw
        dh1_ref[...] = dh2_ref[...] + dx

    def row(w):
        return pl.BlockSpec((tm, w), lambda i: (i, 0))

    return pl.pallas_call(
        body, name="mix_in_bwd", grid=(t // tm,),
        in_specs=[row(D_MODEL), row(D_MODEL), _full((1, D_MODEL)), _full((D_MODEL, P_END)), _full((128, GLA_KW)),
                  _full((1, GLA_KW)), row(128), row(128), row(128), row(256), row(256), row(512), row(512), row(256),
                  row(512), row(256), row(256), _full((BLK, 256)), _full((BLK, 256))],
        out_specs=[row(D_MODEL), row(P_END), _full((1, D_MODEL)), _full((128, GLA_KW)), _full((1, GLA_KW))],
        out_shape=[jax.ShapeDtypeStruct((t, D_MODEL), F32), jax.ShapeDtypeStruct((t, P_END), BF16),
                   jax.ShapeDtypeStruct((1, D_MODEL), F32), jax.ShapeDtypeStruct((128, GLA_KW), F32),
                   jax.ShapeDtypeStruct((1, GLA_KW), F32)],
        compiler_params=_cparams(1),
    )(dh2, h1, wmixpre, winp, wa2p, bap, cos, sin, ga, dgq, dgk, dgv, dgg, dla, dsq, dsk, dsv, dkm, dvm)


def _loss_head(h3, target):
    t = h3.shape[0]
    nb = t // BLK

    def body(h_ref, t_ref, dy_ref, loss_ref):
        n = pl.program_id(0)

        @pl.when(n == 0)
        def _():
            loss_ref[...] = jnp.zeros_like(loss_ref)
            dy_ref[...] = jnp.zeros_like(dy_ref)

        @pl.when(n > 0)
        def _():
            err = h_ref[...] - t_ref[...]
            dy_ref[...] = err * (1.0 / D_MODEL)
            part = jnp.sum(jnp.sum(err * err, axis=1, keepdims=True), axis=0, keepdims=True)
            loss_ref[...] += jnp.broadcast_to(part, (1, 128))

    return pl.pallas_call(
        body, name="loss_head", grid=(nb,),
        in_specs=[pl.BlockSpec((BLK, D_MODEL), lambda i: (i, 0)),
                  pl.BlockSpec((BLK, D_MODEL), lambda i: (jnp.maximum(i - 1, 0), 0))],
        out_specs=[pl.BlockSpec((BLK, D_MODEL), lambda i: (i, 0)), _full((1, 128))],
        out_shape=[jax.ShapeDtypeStruct((t, D_MODEL), F32), jax.ShapeDtypeStruct((1, 128), F32)],
        compiler_params=_cparams(1),
    )(h3, target)


def _adamw(w, g, m, v):
    r, c = w.shape
    tr = _div_tile(r)

    def body(w_ref, g_ref, m_ref, v_ref, d_ref, nm_ref, nv_ref):
        g = g_ref[...]
        m = ADAM_B1 * m_ref[...] + (1.0 - ADAM_B1) * g
        v = ADAM_B2 * v_ref[...] + (1.0 - ADAM_B2) * (g * g)
        m_hat = m / (1.0 - ADAM_B1 ** ADAM_STEP)
        v_hat = v / (1.0 - ADAM_B2 ** ADAM_STEP)
        d_ref[...] = -ADAM_LR * (m_hat / (jnp.sqrt(v_hat) + ADAM_EPS) + ADAM_WD * w_ref[...])
        nm_ref[...] = m
        nv_ref[...] = v

    spec = pl.BlockSpec((tr, c), lambda i: (i, 0))
    shape = jax.ShapeDtypeStruct((r, c), F32)
    return pl.pallas_call(
        body, name="adamw", grid=(r // tr,), in_specs=[spec] * 4, out_specs=[spec] * 3, out_shape=[shape] * 3,
        compiler_params=_cparams(1),
    )(w, g, m, v)


ANY = pl.BlockSpec(memory_space=pl.ANY)


def _place():
    x, y, c = lax.axis_index("x"), lax.axis_index("y"), lax.axis_index("c")
    chips = [(1 - x, y), (x, 1 - y), (1 - x, 1 - y)]
    return x, y, c, chips


def _all_gather_chips(buf):
    _, r, w = buf.shape

    def body(in_ref, out_ref, send_sems, recv_sems, local_sem):
        x, y, c, chips = _place()
        q = 2 * x + y
        sibling = (x, y, 1 - c)

        def copy(k, src, dst, to):
            return pltpu.make_async_remote_copy(src_ref=src, dst_ref=dst, send_sem=send_sems.at[k],
                                                recv_sem=recv_sems.at[k], device_id=to, device_id_type=MESH)

        mine = pltpu.make_async_copy(in_ref, out_ref.at[q], local_sem)
        mine.start()
        first = [copy(j, in_ref.at[c], out_ref.at[q, c], (cx, cy, c)) for j, (cx, cy) in enumerate(chips)]
        for cp in first:
            cp.start()
        passed = []
        for j, (cx, cy) in enumerate(chips):
            slot = out_ref.at[2 * cx + cy, c]
            copy(j, slot, slot, (cx, cy, c)).wait_recv()
            fwd = copy(3 + j, slot, slot, sibling)
            fwd.start()
            passed.append(fwd)
        for j, (cx, cy) in enumerate(chips):
            slot = out_ref.at[2 * cx + cy, 1 - c]
            copy(3 + j, slot, slot, sibling).wait_recv()
        for cp in first + passed:
            cp.wait_send()
        mine.wait()

    return pl.pallas_call(
        body, name="all_gather_chips", in_specs=[ANY], out_specs=ANY,
        out_shape=jax.ShapeDtypeStruct((N_CHIPS, 2, r, w), buf.dtype),
        scratch_shapes=[pltpu.SemaphoreType.DMA((6,)), pltpu.SemaphoreType.DMA((6,)), pltpu.SemaphoreType.DMA],
    )(buf)


def _all_gather_devices(vec):
    r, w = vec.shape

    def body(x_ref, out_ref, send_sems, recv_sems, local_sem):
        x, y, c, chips = _place()
        me, sibling = (x, y, c), (x, y, 1 - c)

        def rows(px, py, pc):
            return out_ref.at[4 * px + 2 * py + pc]

        def copy(k, block, to, src=None):
            return pltpu.make_async_remote_copy(
                src_ref=rows(*block) if src is None else src, dst_ref=rows(*block), send_sem=send_sems.at[k],
                recv_sem=recv_sems.at[k], device_id=to, device_id_type=MESH)

        mine = pltpu.make_async_copy(x_ref, rows(*me), local_sem)
        mine.start()
        first = [copy(0, me, sibling, src=x_ref)]
        first += [copy(1 + j, me, (*chip, c), src=x_ref) for j, chip in enumerate(chips)]
        for cp in first:
            cp.start()
        passed = [copy(4 + j, (*chip, c), sibling) for j, chip in enumerate(chips)]
        for j, chip in enumerate(chips):
            copy(1 + j, (*chip, c), me).wait_recv()
            passed[j].start()
        copy(0, sibling, me).wait_recv()
        for j, chip in enumerate(chips):
            copy(4 + j, (*chip, 1 - c), me).wait_recv()
        for cp in first + passed:
            cp.wait_send()
        mine.wait()

    return pl.pallas_call(
        body, name="all_gather_devices",
        in_specs=[pl.BlockSpec(memory_space=pltpu.VMEM)], out_specs=pl.BlockSpec(memory_space=pltpu.VMEM),
        out_shape=jax.ShapeDtypeStruct((N_DEV, r, w), vec.dtype),
        scratch_shapes=[pltpu.SemaphoreType.DMA((7,)), pltpu.SemaphoreType.DMA((7,)), pltpu.SemaphoreType.DMA],
    )(vec)


def _pair_exchange(g):
    nq, _, r, w = g.shape

    def body(g_ref, out_ref, send_sems, recv_sems):
        x, y, c, _ = _place()
        sibling = (x, y, 1 - c)
        copies = [pltpu.make_async_remote_copy(src_ref=g_ref.at[k, 1 - c], dst_ref=out_ref.at[k],
                                               send_sem=send_sems.at[k], recv_sem=recv_sems.at[k],
                                               device_id=sibling, device_id_type=MESH) for k in range(nq)]
        for cp in copies:
            cp.start()
        for cp in copies:
            cp.wait()

    return pl.pallas_call(
        body, name="pair_exchange", in_specs=[ANY], out_specs=ANY,
        out_shape=jax.ShapeDtypeStruct((nq, r, w), g.dtype),
        scratch_shapes=[pltpu.SemaphoreType.DMA((nq,)), pltpu.SemaphoreType.DMA((nq,))],
    )(g)


def _pair_sum(g, other, c_idx):
    nq, _, r, w = g.shape
    tr = _div_tile(r, 1024)

    def body(c_ref, g_ref, o_ref, s_ref):
        s_ref[...] = g_ref[...] + o_ref[...]

    return pl.pallas_call(
        body, name="pair_sum",
        grid_spec=pltpu.PrefetchScalarGridSpec(
            num_scalar_prefetch=1, grid=(nq, r // tr),
            in_specs=[pl.BlockSpec((None, None, tr, w), lambda k, i, c_ref: (k, c_ref[0], i, 0)),
                      pl.BlockSpec((None, tr, w), lambda k, i, c_ref: (k, i, 0))],
            out_specs=pl.BlockSpec((None, tr, w), lambda k, i, c_ref: (k, i, 0))),
        out_shape=jax.ShapeDtypeStruct((nq, r, w), g.dtype),
        compiler_params=_cparams(2),
    )(c_idx, g, other)


def _chip_scatter(s):
    nq, r, w = s.shape

    def body(s_ref, out_ref, send_sems, recv_sems):
        x, y, c, chips = _place()
        copies = [pltpu.make_async_remote_copy(src_ref=s_ref.at[2 * cx + cy], dst_ref=out_ref.at[j],
                                               send_sem=send_sems.at[j], recv_sem=recv_sems.at[j],
                                               device_id=(cx, cy, c), device_id_type=MESH)
                  for j, (cx, cy) in enumerate(chips)]
        for cp in copies:
            cp.start()
        for cp in copies:
            cp.wait()

    return pl.pallas_call(
        body, name="chip_scatter", in_specs=[ANY], out_specs=ANY,
        out_shape=jax.ShapeDtypeStruct((3, r, w), s.dtype),
        scratch_shapes=[pltpu.SemaphoreType.DMA((3,)), pltpu.SemaphoreType.DMA((3,))],
    )(s)


def _chip_sum(s, others, q_idx):
    _, r, w = s.shape
    tr = _div_tile(r, 1024)

    def body(q_ref, s_ref, o_ref, out_ref):
        out_ref[...] = ((s_ref[...] + o_ref[0]) + o_ref[1]) + o_ref[2]

    return pl.pallas_call(
        body, name="chip_sum",
        grid_spec=pltpu.PrefetchScalarGridSpec(
            num_scalar_prefetch=1, grid=(r // tr,),
            in_specs=[pl.BlockSpec((None, tr, w), lambda i, q_ref: (q_ref[0], i, 0)),
                      pl.BlockSpec((3, tr, w), lambda i, q_ref: (0, i, 0))],
            out_specs=pl.BlockSpec((tr, w), lambda i, q_ref: (i, 0))),
        out_shape=jax.ShapeDtypeStruct((r, w), s.dtype),
        compiler_params=_cparams(1),
    )(q_idx, s, others)


def _pair_share(red):
    r, w = red.shape

    def body(r_ref, out_ref, send_sem, recv_sem, local_sem):
        x, y, c, _ = _place()
        mine = pltpu.make_async_copy(r_ref, out_ref.at[c], local_sem)
        mine.start()
        cp = pltpu.make_async_remote_copy(src_ref=r_ref, dst_ref=out_ref.at[c], send_sem=send_sem, recv_sem=recv_sem,
                                          device_id=(x, y, 1 - c), device_id_type=MESH)
        cp.start()
        pltpu.make_async_remote_copy(src_ref=r_ref, dst_ref=out_ref.at[1 - c], send_sem=send_sem, recv_sem=recv_sem,
                                     device_id=(x, y, 1 - c), device_id_type=MESH).wait_recv()
        cp.wait_send()
        mine.wait()

    return pl.pallas_call(
        body, name="pair_share", in_specs=[ANY], out_specs=ANY,
        out_shape=jax.ShapeDtypeStruct((2, r, w), red.dtype),
        scratch_shapes=[pltpu.SemaphoreType.DMA, pltpu.SemaphoreType.DMA, pltpu.SemaphoreType.DMA],
    )(red)


def _sum_devices(parts):
    nd, r, w = parts.shape

    def body(p_ref, o_ref):
        acc = p_ref[0]
        for k in range(1, nd):
            acc = acc + p_ref[k]
        o_ref[...] = acc

    return pl.pallas_call(
        body, name="sum_devices", in_specs=[_full((nd, r, w))], out_specs=_full((r, w)),
        out_shape=jax.ShapeDtypeStruct((r, w), parts.dtype), grid=(1,), compiler_params=_cparams(1),
    )(parts)


def _rows128(a):
    return a.reshape(-1, 128)


def _pack_win(w_in):
    o = np.cumsum((0,) + IN_SPLITS)
    gq, gk, gv, gg, ga, sq, sk, sv = [w_in[:, o[i]:o[i + 1]] for i in range(8)]
    z = jnp.zeros((w_in.shape[0], 128 - GLA_RANK), w_in.dtype)
    dup = lambda a: jnp.concatenate([a[:, :64], a[:, :64], a[:, 64:], a[:, 64:]], axis=1)
    return jnp.concatenate([gq, gk, gv, gg, ga, z, sq, dup(sk), dup(sv)], axis=1)


def _unpack_dwin(d):
    und = lambda a: jnp.concatenate([a[:, 0:64] + a[:, 64:128], a[:, 128:192] + a[:, 192:256]], axis=1)
    return jnp.concatenate([d[:, :P_GA], d[:, P_GA:P_GA + GLA_RANK], d[:, P_SQ:P_SK], und(d[:, P_SK:P_SV]),
                            und(d[:, P_SV:P_END])], axis=1)


def _local_step(x, target, meta, p):
    s = x.shape[0]
    t = s + BLK
    h0 = jnp.concatenate([jnp.zeros((PAD, D_MODEL), F32), meta, x], axis=0)
    cos, sin = _rope_tables(t)

    h1, n1, g1, u1, f1 = _ffn_fwd(h0, p["ffn1_pre_norm"], p["ffn1_w_gate"], p["ffn1_w_up"], p["ffn1_w_down"],
                                  p["ffn1_post_norm"])
    n2, gq, gk, gv, gg, ga, la, sq, sk, sv = _mix_proj(h1, p["mix_pre_norm"], p["w_in"], p["gla_w_a2"], p["gla_b_a"],
                                                       cos, sin)
    ogla, ss = _gla_fwd(gq, gk, gv, la)
    oswa = _swa_fwd(p["swa_sinks"], sq, sk, sv)
    h2, cat, m = _mix_out(h1, ogla, gg, oswa, p["gla_out_norm"], p["swa_out_norm"], p["w_out"], p["mix_post_norm"])
    h3, n3, g3, u3, f3 = _ffn_fwd(h2, p["ffn2_pre_norm"], p["ffn2_w_gate"], p["ffn2_w_up"], p["ffn2_w_down"],
                                  p["ffn2_post_norm"])
    dy, sse = _loss_head(h3, target)

    grads = {}
    dh2, df3, dg3, du3, a3, grads["ffn2_pre_norm"], grads["ffn2_post_norm"] = _ffn_bwd(
        dy, h2, f3, g3, u3, p["ffn2_pre_norm"], p["ffn2_w_gate"], p["ffn2_w_up"], p["ffn2_w_down"],
        p["ffn2_post_norm"])
    grads["ffn2_w_gate"], grads["ffn2_w_up"], grads["ffn2_w_down"] = _ffn_wgrad(n3, df3, dg3, du3, a3)

    dogla, dgg, doswa, dm, grads["mix_post_norm"], grads["gla_out_norm"], grads["swa_out_norm"] = _mix_out_bwd(
        dh2, m, ogla, gg, oswa, p["gla_out_norm"], p["swa_out_norm"], p["w_out"], p["mix_post_norm"])
    grads["w_out"] = _xty(cat, dm)
    dsq, dsk, dsv, dkm, dvm, dsinks = _swa_bwd(p["swa_sinks"], sq, sk, sv, oswa, doswa)
    grads["swa_sinks"] = dsinks[:, 0]
    dgq, dgk, dgv, dla = _gla_bwd(gq, gk, gv, la, ss, dogla)
    dh1, dproj, grads["mix_pre_norm"], dwa2p, grads["gla_b_a"] = _mix_in_bwd(
        dh2, h1, p["mix_pre_norm"], p["w_in"], p["gla_w_a2"], p["gla_b_a"], cos, sin, ga, dgq, dgk, dgv, dgg, dla,
        dsq, dsk, dsv, dkm, dvm)
    grads["gla_w_a2"] = dwa2p[:GLA_RANK]
    grads["w_in"] = _unpack_dwin(_xty(n2, dproj))

    dh0, df1, dg1, du1, a1, grads["ffn1_pre_norm"], grads["ffn1_post_norm"] = _ffn_bwd(
        dh1, h0, f1, g1, u1, p["ffn1_pre_norm"], p["ffn1_w_gate"], p["ffn1_w_up"], p["ffn1_w_down"],
        p["ffn1_post_norm"])
    grads["ffn1_w_gate"], grads["ffn1_w_up"], grads["ffn1_w_down"] = _ffn_wgrad(n1, df1, dg1, du1, a1)
    grads["meta_tokens"] = dh0[PAD:BLK]
    return sse[0, 0], dh0[BLK:], grads


WEIGHTS = ['meta_tokens', 'ffn1_pre_norm', 'ffn1_w_gate', 'ffn1_w_up', 'ffn1_w_down', 'ffn1_post_norm',
           'mix_pre_norm', 'w_in', 'gla_w_a2', 'gla_b_a', 'gla_out_norm', 'swa_sinks', 'swa_out_norm', 'w_out',
           'mix_post_norm', 'ffn2_pre_norm', 'ffn2_w_gate', 'ffn2_w_up', 'ffn2_w_down', 'ffn2_post_norm']
BIG = ['ffn1_w_gate', 'ffn1_w_up', 'ffn1_w_down', 'w_in', 'w_out', 'ffn2_w_gate', 'ffn2_w_up', 'ffn2_w_down']
SMALL = [n for n in WEIGHTS if n not in BIG]
FJ = D_FF // N_CHIPS
D_IN_J = D_IN // N_CHIPS
D_OUT_J = D_MODEL // N_CHIPS
BIG_SHARD = {'ffn1_w_gate': (D_MODEL, FJ), 'ffn1_w_up': (D_MODEL, FJ), 'ffn1_w_down': (FJ, D_MODEL),
             'w_in': (D_MODEL, D_IN_J), 'w_out': (D_OUT_J, D_MODEL),
             'ffn2_w_gate': (D_MODEL, FJ), 'ffn2_w_up': (D_MODEL, FJ), 'ffn2_w_down': (FJ, D_MODEL)}


def _small_rows(name, a):
    flat = a.reshape(-1)
    rows = -(-flat.shape[0] // 1024) * 8
    return jnp.pad(flat, (0, rows * 128 - flat.shape[0])).reshape(rows, 128)


def kernel(x, meta_tokens, ffn1_pre_norm, ffn1_w_gate, ffn1_w_up, ffn1_w_down, ffn1_post_norm, mix_pre_norm, w_in, gla_w_a2, gla_b_a, gla_out_norm, swa_sinks, swa_out_norm, w_out, mix_post_norm, ffn2_pre_norm, ffn2_w_gate, ffn2_w_up, ffn2_w_down, ffn2_post_norm, loss_target, m_meta_tokens, m_ffn1_pre_norm, m_ffn1_w_gate, m_ffn1_w_up, m_ffn1_w_down, m_ffn1_post_norm, m_mix_pre_norm, m_w_in, m_gla_w_a2, m_gla_b_a, m_gla_out_norm, m_swa_sinks, m_swa_out_norm, m_w_out, m_mix_post_norm, m_ffn2_pre_norm, m_ffn2_w_gate, m_ffn2_w_up, m_ffn2_w_down, m_ffn2_post_norm, v_meta_tokens, v_ffn1_pre_norm, v_ffn1_w_gate, v_ffn1_w_up, v_ffn1_w_down, v_ffn1_post_norm, v_mix_pre_norm, v_w_in, v_gla_w_a2, v_gla_b_a, v_gla_out_norm, v_swa_sinks, v_swa_out_norm, v_w_out, v_mix_post_norm, v_ffn2_pre_norm, v_ffn2_w_gate, v_ffn2_w_up, v_ffn2_w_down, v_ffn2_post_norm):
    args = dict(locals())
    w = {n: args[n] for n in WEIGHTS}
    mom = {n: args["m_" + n] for n in WEIGHTS}
    var = {n: args["v_" + n] for n in WEIGHTS}
    cx, cy, cc = lax.axis_index("x"), lax.axis_index("y"), lax.axis_index("c")
    q_idx = (2 * cx + cy).astype(jnp.int32).reshape(1)
    c_idx = cc.astype(jnp.int32).reshape(1)

    big_rows = [_rows128(w[n].reshape(BIG_SHARD[n]).astype(BF16)) for n in BIG]
    sizes = [b.shape[0] for b in big_rows]
    packed = jnp.concatenate(big_rows, axis=0)
    gathered = _all_gather_chips(packed.reshape(2, -1, 128)).reshape(N_CHIPS, -1, 128)
    full = {}
    off = 0
    for n, sz in zip(BIG, sizes):
        full[n] = gathered[:, off:off + sz].reshape((N_CHIPS,) + BIG_SHARD[n])
        off += sz
    small_cols = jnp.concatenate([_rows128(w["meta_tokens"]), _rows128(w["gla_w_a2"]), jnp.zeros((8, 128), F32)],
                                 axis=0)
    small_all = _all_gather_chips(small_cols.reshape(2, -1, 128)).reshape(N_CHIPS, -1, 128)
    meta_full = small_all[:, :32].reshape(N_CHIPS, N_META, D_MODEL // N_CHIPS).transpose(1, 0, 2).reshape(
        N_META, D_MODEL)
    wa2_full = small_all[:, 32:40].reshape(N_CHIPS, GLA_RANK, GLA_KW // N_CHIPS).transpose(1, 0, 2).reshape(
        GLA_RANK, GLA_KW)

    p = {n: w[n] for n in SMALL}
    for n in ('ffn1_w_gate', 'ffn1_w_up', 'ffn1_w_down', 'ffn2_w_gate', 'ffn2_w_up', 'ffn2_w_down'):
        p[n] = full[n]
    p["w_in"] = _pack_win(full["w_in"].transpose(1, 0, 2).reshape(D_MODEL, D_IN))
    p["w_out"] = full["w_out"].reshape(D_MODEL, D_MODEL)
    p["gla_w_a2"] = jnp.pad(wa2_full, ((0, 128 - GLA_RANK), (0, 0))).astype(BF16)
    p["swa_sinks"] = w["swa_sinks"].reshape(SWA_QH)

    sse, grad_x, g = _local_step(x[0], loss_target[0], meta_full, p)
    loss = lax.psum(sse * (0.5 / D_MODEL), ("x", "y", "c"))

    g["w_in"] = g["w_in"].reshape(D_MODEL, N_CHIPS, D_IN_J).transpose(1, 0, 2)
    g["w_out"] = g["w_out"].reshape(N_CHIPS, D_OUT_J, D_MODEL)
    gpack = jnp.concatenate([g[n].reshape(N_CHIPS, -1, 128) for n in BIG], axis=1)
    rtot = gpack.shape[1]
    gpack = gpack.reshape(N_CHIPS, 2, rtot // 2, 128)
    pair = _pair_sum(gpack, _pair_exchange(gpack), c_idx)
    red = _chip_sum(pair, _chip_scatter(pair), q_idx)
    gred = _pair_share(red).reshape(rtot, 128)
    grad, delta, new_m, new_v = {}, {}, {}, {}
    off = 0
    for n, sz in zip(BIG, sizes):
        shard = BIG_SHARD[n]
        gn = gred[off:off + sz].reshape(shard)
        off += sz
        d, nm, nv = _adamw(w[n].reshape(shard), gn, mom[n].reshape(shard), var[n].reshape(shard))
        grad[n], delta[n], new_m[n], new_v[n] = [a.reshape(w[n].shape) for a in (gn, d, nm, nv)]

    g["swa_sinks"] = g["swa_sinks"].reshape(1, SWA_QH)
    small_rows = [_small_rows(n, g[n]) for n in SMALL]
    ssizes = [a.shape[0] for a in small_rows]
    gsmall = _sum_devices(_all_gather_devices(jnp.concatenate(small_rows, axis=0)))
    col0 = {"meta_tokens": D_MODEL // N_CHIPS, "gla_w_a2": GLA_KW // N_CHIPS}
    gs, ws, ms, vs = [], [], [], []
    off = 0
    for n, sz in zip(SMALL, ssizes):
        full_shape = g[n].shape
        gn = gsmall[off:off + sz].reshape(-1)[:math.prod(full_shape)].reshape(full_shape)
        off += sz
        if n in col0:
            gn = lax.dynamic_slice_in_dim(gn, (2 * cx + cy) * col0[n], col0[n], axis=1)
        grad[n] = gn.reshape(w[n].shape)
        gs.append(_small_rows(n, grad[n]))
        ws.append(_small_rows(n, w[n]))
        ms.append(_small_rows(n, mom[n]))
        vs.append(_small_rows(n, var[n]))
    psizes = [a.shape[0] for a in gs]
    d, nm, nv = _adamw(*[jnp.concatenate(a, axis=0) for a in (ws, gs, ms, vs)])
    off = 0
    for n, sz in zip(SMALL, psizes):
        cnt = math.prod(w[n].shape)
        delta[n], new_m[n], new_v[n] = [a[off:off + sz].reshape(-1)[:cnt].reshape(w[n].shape) for a in (d, nm, nv)]
        off += sz

    return (loss, grad_x[None], *[grad[n] for n in WEIGHTS], *[delta[n] for n in WEIGHTS],
            *[new_m[n] for n in WEIGHTS], *[new_v[n] for n in WEIGHTS])
```

```python
import functools
import math

import numpy as np
import jax
import jax.numpy as jnp
from jax import lax
from jax.experimental import pallas as pl
from jax.experimental.pallas import tpu as pltpu

F32 = jnp.float32
BF16 = jnp.bfloat16
MESH = pl.DeviceIdType.MESH

D_MODEL = 1024
D_FF = 2816
N_CHIPS = 4
N_DEV = 8
N_META = 16
BLK = 128
PAD = BLK - N_META
GLA_CHUNK = 64
GLA_HEADS = 4
GLA_DV = 128
GLA_DK = 64
GLA_KW = GLA_HEADS * GLA_DK
GLA_W = GLA_HEADS * GLA_DV
GLA_RANK = 16
GLA_TAU = 16.0
SWA_HD = 64
SWA_QH = 8
SWA_KVH = 2
SWA_W = SWA_QH * SWA_HD
WINDOW = 128
ROPE_THETA = 10000.0
EPS = 1e-6
NEG_INF = -1e30
IN_SPLITS = (256, 256, 512, 512, 16, 512, 128, 128)
D_IN = sum(IN_SPLITS)
P_GQ, P_GK, P_GV, P_GG, P_GA, P_SQ, P_SK, P_SV, P_END = 0, 256, 512, 1024, 1536, 1664, 2176, 2432, 2688
ADAM_LR, ADAM_B1, ADAM_B2, ADAM_EPS, ADAM_WD, ADAM_STEP = 0.001, 0.9, 0.999, 1e-08, 0.01, 10
VMEM_LIMIT = 56 * 1024 * 1024

NT = (((1,), (1,)), ((), ()))
TN = (((0,), (0,)), ((), ()))


def _cparams(n_axes):
    return pltpu.CompilerParams(dimension_semantics=("arbitrary",) * n_axes, vmem_limit_bytes=VMEM_LIMIT)


def _row_tile(t):
    for tm in (640, 512, 384, 256, 128):
        if t % tm == 0:
            return tm
    raise ValueError(t)


def _div_tile(r, cap=512):
    best = None
    for tr in range(8, min(r, cap) + 1, 8):
        if r % tr == 0:
            best = tr
    return best if best is not None else r


def _dot(a, b):
    return jnp.dot(a, b, preferred_element_type=F32)


def _dg(a, b, dims):
    return lax.dot_general(a, b, dims, preferred_element_type=F32)


def _rms(x, w):
    r = lax.rsqrt(jnp.mean(x * x, axis=-1, keepdims=True) + EPS)
    xh = x * r
    return xh * w, xh, r


def _rms_bwd(xh, r, w, dy):
    wdy = dy * w
    dx = r * (wdy - xh * jnp.mean(wdy * xh, axis=-1, keepdims=True))
    dw = jnp.sum(dy * xh, axis=0, keepdims=True)
    return dx, dw


def _sigmoid(x):
    return 1.0 / (1.0 + jnp.exp(-x))


def _full(shape):
    nd = len(shape)
    return pl.BlockSpec(shape, lambda *_: (0,) * nd)


ANY = pl.BlockSpec(memory_space=pl.ANY)


def _pallas(body, *, name, grid, in_specs, out_specs, out_shape, args, scratch_shapes=(), hook=None):
    n_axes = len(grid)
    if hook is None:
        return pl.pallas_call(body, name=name, grid=grid, in_specs=list(in_specs), out_specs=list(out_specs),
                              out_shape=list(out_shape), scratch_shapes=list(scratch_shapes),
                              compiler_params=_cparams(n_axes))(*args)
    n_in, n_out, n_scr = len(in_specs), len(out_specs), len(scratch_shapes)
    h_in, h_out = len(hook.inputs), len(hook.out_shape)
    total = math.prod(grid)

    def wrapped(*refs):
        ins, hins = refs[:n_in], refs[n_in:n_in + h_in]
        o0 = n_in + h_in
        outs, houts = refs[o0:o0 + n_out], refs[o0 + n_out:o0 + n_out + h_out]
        s0 = o0 + n_out + h_out
        scr, hscr = refs[s0:s0 + n_scr], refs[s0 + n_scr:]
        step = pl.program_id(0)
        for a in range(1, n_axes):
            step = step * grid[a] + pl.program_id(a)

        @pl.when(step == 0)
        def _():
            hook.start(hins, houts, hscr)

        body(*ins, *outs, *scr)

        if hook.has_mid:
            @pl.when(step == total // 2)
            def _():
                hook.mid(hins, houts, hscr)

        @pl.when(step == total - 1)
        def _():
            hook.finish(hins, houts, hscr)

    res = pl.pallas_call(
        wrapped, name=name, grid=grid, in_specs=list(in_specs) + [ANY] * h_in,
        out_specs=list(out_specs) + [ANY] * h_out, out_shape=list(out_shape) + list(hook.out_shape),
        scratch_shapes=list(scratch_shapes) + list(hook.scratch), compiler_params=_cparams(n_axes),
    )(*args, *hook.inputs)
    return res[:n_out], res[n_out:]


def _ffn_fwd(h, wpre, wg4, wu4, wd4, wpost, hook=None):
    t = h.shape[0]
    tm = _row_tile(t)
    nj, _, fj = wg4.shape

    def body(h_ref, wpre_ref, wg_ref, wu_ref, wd_ref, wpost_ref, hout_ref, n_ref, g_ref, u_ref, f_ref, acc_ref):
        j = pl.program_id(1)

        @pl.when(j == 0)
        def _():
            y, _, _ = _rms(h_ref[...], wpre_ref[...])
            n_ref[...] = y.astype(BF16)
            acc_ref[...] = jnp.zeros_like(acc_ref)

        n = n_ref[...]
        g = _dot(n, wg_ref[...])
        u = _dot(n, wu_ref[...])
        g_ref[...] = g.astype(BF16)
        u_ref[...] = u.astype(BF16)
        a = g * _sigmoid(g) * u
        acc_ref[...] += _dot(a.astype(BF16), wd_ref[...])

        @pl.when(j == nj - 1)
        def _():
            f = acc_ref[...]
            f_ref[...] = f
            y, _, _ = _rms(f, wpost_ref[...])
            hout_ref[...] = h_ref[...] + 0.5 * y

    row = pl.BlockSpec((tm, D_MODEL), lambda i, j: (i, 0))
    vec = pl.BlockSpec((1, D_MODEL), lambda i, j: (0, 0))
    wcol = pl.BlockSpec((None, D_MODEL, fj), lambda i, j: (j, 0, 0))
    wrow = pl.BlockSpec((None, fj, D_MODEL), lambda i, j: (j, 0, 0))
    act = pl.BlockSpec((None, tm, fj), lambda i, j: (j, i, 0))
    return _pallas(
        body, name="ffn_fwd", grid=(t // tm, nj),
        in_specs=[row, vec, wcol, wcol, wrow, vec],
        out_specs=[row, row, act, act, row],
        out_shape=[jax.ShapeDtypeStruct((t, D_MODEL), F32), jax.ShapeDtypeStruct((t, D_MODEL), BF16),
                   jax.ShapeDtypeStruct((nj, t, fj), BF16), jax.ShapeDtypeStruct((nj, t, fj), BF16),
                   jax.ShapeDtypeStruct((t, D_MODEL), F32)],
        scratch_shapes=[pltpu.VMEM((tm, D_MODEL), F32)],
        args=(h, wpre, wg4, wu4, wd4, wpost), hook=hook)


def _ffn_bwd(dhout, h, f, g4, u4, wpre, wg4, wu4, wd4, wpost, hook=None):
    t = h.shape[0]
    tm = _row_tile(t)
    nj, _, fj = wg4.shape

    def body(dhout_ref, h_ref, f_ref, g_ref, u_ref, wpre_ref, wg_ref, wu_ref, wd_ref, wpost_ref,
             dh_ref, df_ref, dg_ref, du_ref, a_ref, dwpre_ref, dwpost_ref, dn_ref):
        i = pl.program_id(0)
        j = pl.program_id(1)

        @pl.when((i == 0) & (j == 0))
        def _():
            dwpre_ref[...] = jnp.zeros_like(dwpre_ref)
            dwpost_ref[...] = jnp.zeros_like(dwpost_ref)

        @pl.when(j == 0)
        def _():
            wpost = wpost_ref[...]
            _, fh, r = _rms(f_ref[...], wpost)
            df, dw = _rms_bwd(fh, r, wpost, 0.5 * dhout_ref[...])
            dwpost_ref[...] += dw
            df_ref[...] = df.astype(BF16)
            dn_ref[...] = jnp.zeros_like(dn_ref)

        da = _dg(df_ref[...], wd_ref[...], NT)
        g = g_ref[...].astype(F32)
        u = u_ref[...].astype(F32)
        sg = _sigmoid(g)
        silu = g * sg
        dg = (da * u * (sg * (1.0 + g * (1.0 - sg)))).astype(BF16)
        du = (da * silu).astype(BF16)
        dg_ref[...] = dg
        du_ref[...] = du
        a_ref[...] = (silu * u).astype(BF16)
        dn_ref[...] += _dg(dg, wg_ref[...], NT) + _dg(du, wu_ref[...], NT)

        @pl.when(j == nj - 1)
        def _():
            wpre = wpre_ref[...]
            _, hh, r = _rms(h_ref[...], wpre)
            dx, dw = _rms_bwd(hh, r, wpre, dn_ref[...])
            dwpre_ref[...] += dw
            dh_ref[...] = dhout_ref[...] + dx

    row = pl.BlockSpec((tm, D_MODEL), lambda i, j: (i, 0))
    vec = pl.BlockSpec((1, D_MODEL), lambda i, j: (0, 0))
    wcol = pl.BlockSpec((None, D_MODEL, fj), lambda i, j: (j, 0, 0))
    wrow = pl.BlockSpec((None, fj, D_MODEL), lambda i, j: (j, 0, 0))
    act = pl.BlockSpec((None, tm, fj), lambda i, j: (j, i, 0))
    actshape = jax.ShapeDtypeStruct((nj, t, fj), BF16)
    return _pallas(
        body, name="ffn_bwd", grid=(t // tm, nj),
        in_specs=[row, row, row, act, act, vec, wcol, wcol, wrow, vec],
        out_specs=[row, row, act, act, act, vec, vec],
        out_shape=[jax.ShapeDtypeStruct((t, D_MODEL), F32), jax.ShapeDtypeStruct((t, D_MODEL), BF16),
                   actshape, actshape, actshape,
                   jax.ShapeDtypeStruct((1, D_MODEL), F32), jax.ShapeDtypeStruct((1, D_MODEL), F32)],
        scratch_shapes=[pltpu.VMEM((tm, D_MODEL), F32)],
        args=(dhout, h, f, g4, u4, wpre, wg4, wu4, wd4, wpost), hook=hook)


def _ffn_wgrad(n, df, dg4, du4, a4, hook=None):
    t = n.shape[0]
    tm = _row_tile(t)
    ni = t // tm
    nj, _, fj = dg4.shape

    def body(n_ref, df_ref, dg_ref, du_ref, a_ref, dwgu_ref, dwd_ref, gu_acc, d_acc):
        i = pl.program_id(1)

        @pl.when(i == 0)
        def _():
            gu_acc[...] = jnp.zeros_like(gu_acc)
            d_acc[...] = jnp.zeros_like(d_acc)

        nn = n_ref[...]
        gu_acc[0:D_MODEL, :] += _dg(nn, dg_ref[...], TN)
        gu_acc[D_MODEL:2 * D_MODEL, :] += _dg(nn, du_ref[...], TN)
        d_acc[...] += _dg(a_ref[...], df_ref[...], TN)

        @pl.when(i == ni - 1)
        def _():
            dwgu_ref[...] = gu_acc[...].astype(BF16)
            dwd_ref[...] = d_acc[...].astype(BF16)

    row = pl.BlockSpec((tm, D_MODEL), lambda j, i: (i, 0))
    act = pl.BlockSpec((None, tm, fj), lambda j, i: (j, i, 0))
    wcol = pl.BlockSpec((None, 2 * D_MODEL, fj), lambda j, i: (j, 0, 0))
    wrow = pl.BlockSpec((None, fj, D_MODEL), lambda j, i: (j, 0, 0))
    return _pallas(
        body, name="ffn_wgrad", grid=(nj, ni),
        in_specs=[row, row, act, act, act],
        out_specs=[wcol, wrow],
        out_shape=[jax.ShapeDtypeStruct((nj, 2 * D_MODEL, fj), BF16), jax.ShapeDtypeStruct((nj, fj, D_MODEL), BF16)],
        scratch_shapes=[pltpu.VMEM((2 * D_MODEL, fj), F32), pltpu.VMEM((fj, D_MODEL), F32)],
        args=(n, df, dg4, du4, a4), hook=hook)


def _xty(x, y):
    t, k = x.shape
    n = y.shape[1]
    tm = _row_tile(t)
    tn = n if n <= 1024 else (896 if n % 896 == 0 else 128)

    def body(x_ref, y_ref, o_ref):
        @pl.when(pl.program_id(1) == 0)
        def _():
            o_ref[...] = jnp.zeros_like(o_ref)

        o_ref[...] += _dg(x_ref[...], y_ref[...], TN)

    return pl.pallas_call(
        body, name="xty", grid=(n // tn, t // tm),
        in_specs=[pl.BlockSpec((tm, k), lambda j, i: (i, 0)), pl.BlockSpec((tm, tn), lambda j, i: (i, j))],
        out_specs=pl.BlockSpec((k, tn), lambda j, i: (0, j)),
        out_shape=jax.ShapeDtypeStruct((k, n), F32),
        compiler_params=_cparams(2),
    )(x, y)


def _rope_tables(t):
    pos = (jnp.arange(t, dtype=jnp.int32) - PAD).astype(F32)
    inv_freq = 1.0 / (ROPE_THETA ** (jnp.arange(0, SWA_HD, 2, dtype=F32) / SWA_HD))
    ang = pos[:, None] * inv_freq[None, :]
    cos = jnp.cos(ang)
    sin = jnp.sin(ang)
    return jnp.concatenate([cos, cos, cos, cos], axis=1), jnp.concatenate([-sin, sin, -sin, sin], axis=1)


def _rot_half(x, first_half):
    return jnp.where(first_half, pltpu.roll(x, 96, 1), pltpu.roll(x, 32, 1))


def _first_half_mask(rows):
    lane = lax.broadcasted_iota(jnp.int32, (rows, 128), 1)
    return (lane % 64) < 32


def _log_sigmoid(z):
    return jnp.minimum(z, 0.0) - jnp.log(1.0 + jnp.exp(-jnp.abs(z)))


def _mix_proj(h1, wmixpre, winp, wa2p, bap, cos, sin):
    t = h1.shape[0]
    tm = _row_tile(t)

    def body(h_ref, w_ref, win_ref, wa2_ref, ba_ref, cos_ref, sin_ref,
             n_ref, gq_ref, gk_ref, gv_ref, gg_ref, ga_ref, la_ref, sq_ref, sk_ref, sv_ref):
        y, _, _ = _rms(h_ref[...], w_ref[...])
        n = y.astype(BF16)
        n_ref[...] = n
        proj = _dot(n, win_ref[...])
        gq_ref[...] = proj[:, P_GQ:P_GK]
        gk_ref[...] = proj[:, P_GK:P_GV]
        gv_ref[...] = proj[:, P_GV:P_GG]
        gg_ref[...] = proj[:, P_GG:P_GA]
        ga = proj[:, P_GA:P_SQ]
        ga_ref[...] = ga
        z = _dot(ga.astype(BF16), wa2_ref[...]) + ba_ref[...]
        la_ref[...] = _log_sigmoid(z) * (1.0 / GLA_TAU)
        c = cos_ref[...]
        s = sin_ref[...]
        fh = _first_half_mask(tm)
        for k in range(4):
            x = proj[:, P_SQ + 128 * k:P_SQ + 128 * (k + 1)]
            sq_ref[:, 128 * k:128 * (k + 1)] = (x * c + _rot_half(x, fh) * s).astype(BF16)
        for k in range(2):
            x = proj[:, P_SK + 128 * k:P_SK + 128 * (k + 1)]
            sk_ref[:, 128 * k:128 * (k + 1)] = (x * c + _rot_half(x, fh) * s).astype(BF16)
        sv_ref[...] = proj[:, P_SV:P_END].astype(BF16)

    def row(w):
        return pl.BlockSpec((tm, w), lambda i: (i, 0))

    def rshape(w, dt):
        return jax.ShapeDtypeStruct((t, w), dt)

    return pl.pallas_call(
        body, name="mix_proj", grid=(t // tm,),
        in_specs=[row(D_MODEL), _full((1, D_MODEL)), _full((D_MODEL, P_END)), _full((128, GLA_KW)),
                  _full((1, GLA_KW)), row(128), row(128)],
        out_specs=[row(D_MODEL), row(256), row(256), row(512), row(512), row(128), row(256), row(512), row(256),
                   row(256)],
        out_shape=[rshape(D_MODEL, BF16), rshape(256, F32), rshape(256, F32), rshape(512, F32), rshape(512, F32),
                   rshape(128, F32), rshape(256, F32), rshape(512, BF16), rshape(256, BF16), rshape(256, BF16)],
        compiler_params=_cparams(1),
    )(h1, wmixpre, winp, wa2p, bap, cos, sin)


def _gla_cumsum(la, tril_f):
    b = jnp.dot(tril_f, la, precision=lax.Precision.HIGHEST, preferred_element_type=F32)
    row = lax.broadcasted_iota(jnp.int32, b.shape, 0)
    bm = jnp.sum(jnp.where(row == GLA_CHUNK // 2 - 1, b, 0.0), axis=0, keepdims=True)
    bl = jnp.sum(jnp.where(row == GLA_CHUNK - 1, b, 0.0), axis=0, keepdims=True)
    return b, bm, bl


def _gla_chunk_terms(la, q, k, tril_f):
    b, bm, bl = _gla_cumsum(la, tril_f)
    qs = q * (GLA_DK ** -0.5)
    qt = qs * jnp.exp(b - bm)
    kt = k * jnp.exp(bm - b)
    qh = qs * jnp.exp(b)
    kh = k * jnp.exp(bl - b)
    ebl = jnp.exp(bl)
    return qt, kt, qh, kh, ebl


def _gla_fwd(gq, gk, gv, la):
    t = gq.shape[0]
    nb = t // BLK
    ncb = BLK // GLA_CHUNK

    def body(q_ref, k_ref, v_ref, la_ref, o_ref, ss_ref, st_ref):
        @pl.when(pl.program_id(0) == 0)
        def _():
            st_ref[...] = jnp.zeros_like(st_ref)

        r = lax.broadcasted_iota(jnp.int32, (GLA_CHUNK, GLA_CHUNK), 0)
        c = lax.broadcasted_iota(jnp.int32, (GLA_CHUNK, GLA_CHUNK), 1)
        tril = r >= c
        tril_f = tril.astype(F32)
        lane = lax.broadcasted_iota(jnp.int32, (GLA_CHUNK, 128), 1)
        halves = (lane < 64, lane >= 64)
        for ch in range(ncb):
            rows = slice(ch * GLA_CHUNK, (ch + 1) * GLA_CHUNK)
            qt, kt, qh, kh, ebl = _gla_chunk_terms(la_ref[rows, :], q_ref[rows, :], k_ref[rows, :], tril_f)
            for hp in range(2):
                ls = slice(128 * hp, 128 * (hp + 1))
                kt2 = kt[:, ls].astype(BF16)
                kh2 = kh[:, ls].astype(BF16)
                for e in range(2):
                    h = 2 * hp + e
                    vs = slice(GLA_DV * h, GLA_DV * (h + 1))
                    st = st_ref[h]
                    ss_ref[ch, h] = st
                    v = v_ref[rows, vs].astype(BF16)
                    qtm = jnp.where(halves[e], qt[:, ls], 0.0).astype(BF16)
                    qhm = jnp.where(halves[e], qh[:, ls], 0.0).astype(BF16)
                    a = jnp.where(tril, _dg(qtm, kt2, NT), 0.0)
                    o_ref[rows, vs] = _dot(a.astype(BF16), v) + _dg(qhm, st.astype(BF16), NT)
                    st_ref[h] = st * ebl[:, ls] + _dg(v, kh2, TN)

    def row(w):
        return pl.BlockSpec((BLK, w), lambda i: (i, 0))

    return pl.pallas_call(
        body, name="gla_fwd", grid=(nb,),
        in_specs=[row(256), row(256), row(512), row(256)],
        out_specs=[row(512), pl.BlockSpec((ncb, GLA_HEADS, GLA_DV, 128), lambda i: (i, 0, 0, 0))],
        out_shape=[jax.ShapeDtypeStruct((t, GLA_W), F32),
                   jax.ShapeDtypeStruct((nb * ncb, GLA_HEADS, GLA_DV, 128), F32)],
        scratch_shapes=[pltpu.VMEM((GLA_HEADS, GLA_DV, 128), F32)],
        compiler_params=_cparams(1),
    )(gq, gk, gv, la)


def _gla_bwd(gq, gk, gv, la, ss, do):
    t = gq.shape[0]
    nb = t // BLK
    ncb = BLK // GLA_CHUNK

    def body(q_ref, k_ref, v_ref, la_ref, ss_ref, do_ref, dq_ref, dk_ref, dv_ref, dla_ref, dst_ref):
        @pl.when(pl.program_id(0) == 0)
        def _():
            dst_ref[...] = jnp.zeros_like(dst_ref)

        r = lax.broadcasted_iota(jnp.int32, (GLA_CHUNK, GLA_CHUNK), 0)
        c = lax.broadcasted_iota(jnp.int32, (GLA_CHUNK, GLA_CHUNK), 1)
        tril = r >= c
        tril_f = tril.astype(F32)
        triu_f = (r <= c).astype(F32)
        lane = lax.broadcasted_iota(jnp.int32, (GLA_CHUNK, 128), 1)
        halves = (lane < 64, lane >= 64)
        last_row = lax.broadcasted_iota(jnp.int32, (GLA_CHUNK, 128), 0) == GLA_CHUNK - 1
        lane1 = lax.broadcasted_iota(jnp.int32, (1, 128), 1)
        halves1 = (lane1 < 64, lane1 >= 64)
        scale = GLA_DK ** -0.5
        for ch in reversed(range(ncb)):
            rows = slice(ch * GLA_CHUNK, (ch + 1) * GLA_CHUNK)
            la = la_ref[rows, :]
            b, bm, bl = _gla_cumsum(la, tril_f)
            eq = jnp.exp(b - bm)
            ek = jnp.exp(bm - b)
            eb = jnp.exp(b)
            ekl = jnp.exp(bl - b)
            ebl = jnp.exp(bl)
            qs = q_ref[rows, :] * scale
            kk = k_ref[rows, :]
            qt, kt, qh, kh = qs * eq, kk * ek, qs * eb, kk * ekl
            for hp in range(2):
                ls = slice(128 * hp, 128 * (hp + 1))
                kt2 = kt[:, ls].astype(BF16)
                kh2 = kh[:, ls].astype(BF16)
                dqt = jnp.zeros((GLA_CHUNK, 128), F32)
                dkt = jnp.zeros((GLA_CHUNK, 128), F32)
                dqh = jnp.zeros((GLA_CHUNK, 128), F32)
                dkh = jnp.zeros((GLA_CHUNK, 128), F32)
                dbl = jnp.zeros((1, 128), F32)
                for e in range(2):
                    h = 2 * hp + e
                    vs = slice(GLA_DV * h, GLA_DV * (h + 1))
                    m = halves[e]
                    st = ss_ref[ch, h]
                    dstn = dst_ref[h]
                    v = v_ref[rows, vs].astype(BF16)
                    dov = do_ref[rows, vs].astype(BF16)
                    qtm = jnp.where(m, qt[:, ls], 0.0).astype(BF16)
                    qhm = jnp.where(m, qh[:, ls], 0.0).astype(BF16)
                    khm = jnp.where(m, kh[:, ls], 0.0).astype(BF16)
                    a = jnp.where(tril, _dg(qtm, kt2, NT), 0.0).astype(BF16)
                    da = jnp.where(tril, _dg(dov, v, NT), 0.0).astype(BF16)
                    dstn_b = dstn.astype(BF16)
                    dv_ref[rows, vs] = _dg(a, dov, TN) + _dg(khm, dstn_b, NT)
                    dqt = dqt + jnp.where(m, _dot(da, kt2), 0.0)
                    dkt = dkt + _dg(da, qtm, TN)
                    dqh = dqh + jnp.where(m, _dot(dov, st.astype(BF16)), 0.0)
                    dkh = dkh + jnp.where(m, _dot(v, dstn_b), 0.0)
                    dbl = dbl + jnp.where(halves1[e], jnp.sum(dstn * st, axis=0, keepdims=True), 0.0)
                    dst_ref[h] = dstn * ebl[:, ls] + _dg(dov, qhm, TN)
                dq_ref[rows, ls] = scale * (dqt * eq[:, ls] + dqh * eb[:, ls])
                dk_ref[rows, ls] = dkt * ek[:, ls] + dkh * ekl[:, ls]
                dkk = dkh * kh[:, ls]
                db = dqt * qt[:, ls] - dkt * kt[:, ls] + dqh * qh[:, ls] - dkk
                db_last = jnp.sum(dkk, axis=0, keepdims=True) + ebl[:, ls] * dbl
                db = db + jnp.where(last_row, db_last, 0.0)
                dla_ref[rows, ls] = jnp.dot(triu_f, db, precision=lax.Precision.HIGHEST,
                                            preferred_element_type=F32)

    def row(w):
        return pl.BlockSpec((BLK, w), lambda i: (nb - 1 - i, 0))

    def rshape(w):
        return jax.ShapeDtypeStruct((t, w), F32)

    return pl.pallas_call(
        body, name="gla_bwd", grid=(nb,),
        in_specs=[row(256), row(256), row(512), row(256),
                  pl.BlockSpec((ncb, GLA_HEADS, GLA_DV, 128), lambda i: (nb - 1 - i, 0, 0, 0)), row(512)],
        out_specs=[row(256), row(256), row(512), row(256)],
        out_shape=[rshape(256), rshape(256), rshape(512), rshape(256)],
        scratch_shapes=[pltpu.VMEM((GLA_HEADS, GLA_DV, 128), F32)],
        compiler_params=_cparams(1),
    )(gq, gk, gv, la, ss, do)


def _swa_mask(n):
    r = lax.broadcasted_iota(jnp.int32, (BLK, 3 * BLK), 0)
    c = lax.broadcasted_iota(jnp.int32, (BLK, 3 * BLK), 1)
    seg = c // BLK
    cc = c % BLK
    qpos = n * BLK + r - PAD
    kpos = jnp.where(seg == 0, (n - 1) * BLK, jnp.where(seg == 1, n * BLK, 0)) + cc - PAD
    band = (seg < 2) & (kpos >= N_META) & (kpos <= qpos) & (qpos - kpos < WINDOW)
    meta = (seg == 2) & (kpos >= 0) & (kpos < N_META) & (kpos <= qpos)
    return band | meta


def _swa_probs(qm, kall, mask, sink):
    s = _dg(qm, kall, NT) * (SWA_HD ** -0.5)
    s = jnp.where(mask, s, NEG_INF)
    m = jnp.maximum(jnp.max(s, axis=-1, keepdims=True), sink)
    p = jnp.exp(s - m)
    es = jnp.exp(sink - m)
    inv = 1.0 / (jnp.sum(p, axis=-1, keepdims=True) + es)
    return p * inv, es * inv


def _swa_fwd(sinks, sq, sk, sv):
    t = sq.shape[0]
    nb = t // BLK

    def body(sink_ref, q_ref, kp_ref, kc_ref, km_ref, vp_ref, vc_ref, vm_ref, o_ref):
        n = pl.program_id(0)
        mask = _swa_mask(n)
        lo = lax.broadcasted_iota(jnp.int32, (BLK, 128), 1) < 64
        for kh in range(SWA_KVH):
            ls = slice(128 * kh, 128 * (kh + 1))
            kall = jnp.concatenate([kp_ref[:, ls], kc_ref[:, ls], km_ref[:, ls]], axis=0)
            vall = jnp.concatenate([vp_ref[:, ls], vc_ref[:, ls], vm_ref[:, ls]], axis=0)
            for g in range(2):
                ps = slice(128 * (2 * kh + g), 128 * (2 * kh + g + 1))
                qp = q_ref[:, ps]
                outs = []
                for e in range(2):
                    h = 4 * kh + 2 * g + e
                    qm = jnp.where(lo if e == 0 else ~lo, qp, jnp.zeros_like(qp))
                    p, _ = _swa_probs(qm, kall, mask, sink_ref[h])
                    outs.append(_dot(p.astype(BF16), vall))
                o_ref[:, ps] = jnp.where(lo, outs[0], outs[1])

    cur = lambda w: pl.BlockSpec((BLK, w), lambda i: (i, 0))
    prev = lambda w: pl.BlockSpec((BLK, w), lambda i: (jnp.maximum(i - 1, 0), 0))
    first = lambda w: pl.BlockSpec((BLK, w), lambda i: (0, 0))
    return pl.pallas_call(
        body, name="swa_fwd", grid=(nb,),
        in_specs=[pl.BlockSpec(memory_space=pltpu.SMEM), cur(512), prev(256), cur(256), first(256),
                  prev(256), cur(256), first(256)],
        out_specs=cur(512),
        out_shape=jax.ShapeDtypeStruct((t, SWA_W), F32),
        compiler_params=_cparams(1),
    )(sinks, sq, sk, sk, sk, sv, sv, sv)


def _swa_bwd(sinks, sq, sk, sv, o, do, hook=None):
    t = sq.shape[0]
    nb = t // BLK

    def body(sink_ref, q_ref, kp_ref, kc_ref, km_ref, vp_ref, vc_ref, vm_ref, o_ref, do_ref,
             dq_ref, dk_ref, dv_ref, dkm_ref, dvm_ref, dsink_ref, ck_ref, cv_ref):
        n = pl.program_id(0)

        @pl.when(n == 0)
        def _():
            ck_ref[...] = jnp.zeros_like(ck_ref)
            cv_ref[...] = jnp.zeros_like(cv_ref)
            dkm_ref[...] = jnp.zeros_like(dkm_ref)
            dvm_ref[...] = jnp.zeros_like(dvm_ref)
            dsink_ref[...] = jnp.zeros_like(dsink_ref)

        @pl.when(n == nb)
        def _():
            dk_ref[...] = ck_ref[...]
            dv_ref[...] = cv_ref[...]

        @pl.when(n < nb)
        def _():
            mask = _swa_mask(n)
            lo = lax.broadcasted_iota(jnp.int32, (BLK, 128), 1) < 64
            scale = SWA_HD ** -0.5
            for kh in range(SWA_KVH):
                ls = slice(128 * kh, 128 * (kh + 1))
                kall = jnp.concatenate([kp_ref[:, ls], kc_ref[:, ls], km_ref[:, ls]], axis=0)
                vall = jnp.concatenate([vp_ref[:, ls], vc_ref[:, ls], vm_ref[:, ls]], axis=0)
                dkall = jnp.zeros((3 * BLK, 128), F32)
                dvall = jnp.zeros((3 * BLK, 128), F32)
                for g in range(2):
                    ps = slice(128 * (2 * kh + g), 128 * (2 * kh + g + 1))
                    qp = q_ref[:, ps]
                    dop = do_ref[:, ps]
                    op = o_ref[:, ps]
                    dqs = []
                    for e in range(2):
                        h = 4 * kh + 2 * g + e
                        half = lo if e == 0 else ~lo
                        qm = jnp.where(half, qp, jnp.zeros_like(qp))
                        dom = jnp.where(half, dop, 0.0)
                        p, psink = _swa_probs(qm, kall, mask, sink_ref[h])
                        delta = jnp.sum(dom * op, axis=-1, keepdims=True)
                        domb = dom.astype(BF16)
                        dp = _dg(domb, vall, NT)
                        ds = (p * (dp - delta) * scale).astype(BF16)
                        dqs.append(_dot(ds, kall))
                        dkall = dkall + _dg(ds, qm, TN)
                        dvall = dvall + _dg(p.astype(BF16), domb, TN)
                        dsink_ref[h:h + 1, :] += jnp.broadcast_to(-jnp.sum(psink * delta, axis=0, keepdims=True),
                                                                 (1, 128))
                    dq_ref[:, ps] = jnp.where(lo, dqs[0], dqs[1])
                dk_ref[:, ls] = ck_ref[:, ls] + dkall[0:BLK]
                dv_ref[:, ls] = cv_ref[:, ls] + dvall[0:BLK]
                ck_ref[:, ls] = dkall[BLK:2 * BLK]
                cv_ref[:, ls] = dvall[BLK:2 * BLK]
                dkm_ref[:, ls] += dkall[2 * BLK:3 * BLK]
                dvm_ref[:, ls] += dvall[2 * BLK:3 * BLK]

    cur = lambda w: pl.BlockSpec((BLK, w), lambda i: (jnp.minimum(i, nb - 1), 0))
    prev = lambda w: pl.BlockSpec((BLK, w), lambda i: (jnp.maximum(i - 1, 0), 0))
    first = lambda w: pl.BlockSpec((BLK, w), lambda i: (0, 0))
    return _pallas(
        body, name="swa_bwd", grid=(nb + 1,),
        in_specs=[pl.BlockSpec(memory_space=pltpu.SMEM), cur(512), prev(256), cur(256), first(256),
                  prev(256), cur(256), first(256), cur(512), cur(512)],
        out_specs=[cur(512), prev(256), prev(256), first(256), first(256), _full((SWA_QH, 128))],
        out_shape=[jax.ShapeDtypeStruct((t, SWA_W), F32), jax.ShapeDtypeStruct((t, 256), F32),
                   jax.ShapeDtypeStruct((t, 256), F32), jax.ShapeDtypeStruct((BLK, 256), F32),
                   jax.ShapeDtypeStruct((BLK, 256), F32), jax.ShapeDtypeStruct((SWA_QH, 128), F32)],
        scratch_shapes=[pltpu.VMEM((BLK, 256), F32), pltpu.VMEM((BLK, 256), F32)],
        args=(sinks, sq, sk, sk, sk, sv, sv, sv, o, do), hook=hook)


def _mix_out(h1, ogla, gg, oswa, wgn, wsn, wout, wpost):
    t = h1.shape[0]
    tm = _row_tile(t)

    def body(h_ref, og_ref, gg_ref, os_ref, wgn_ref, wsn_ref, wout_ref, wpost_ref, h2_ref, cat_ref, m_ref):
        parts = []
        for h in range(GLA_HEADS):
            ls = slice(GLA_DV * h, GLA_DV * (h + 1))
            y, _, _ = _rms(og_ref[:, ls], wgn_ref[...])
            g = gg_ref[:, ls]
            parts.append(y * (g * _sigmoid(g)))
        ys, _, _ = _rms(os_ref[...], wsn_ref[...])
        cat = jnp.concatenate(parts + [ys], axis=1).astype(BF16)
        cat_ref[...] = cat
        m = _dot(cat, wout_ref[...])
        m_ref[...] = m
        y, _, _ = _rms(m, wpost_ref[...])
        h2_ref[...] = h_ref[...] + y

    def row(w):
        return pl.BlockSpec((tm, w), lambda i: (i, 0))

    return pl.pallas_call(
        body, name="mix_out", grid=(t // tm,),
        in_specs=[row(D_MODEL), row(512), row(512), row(512), _full((1, GLA_DV)), _full((1, SWA_W)),
                  _full((D_MODEL, D_MODEL)), _full((1, D_MODEL))],
        out_specs=[row(D_MODEL), row(D_MODEL), row(D_MODEL)],
        out_shape=[jax.ShapeDtypeStruct((t, D_MODEL), F32), jax.ShapeDtypeStruct((t, D_MODEL), BF16),
                   jax.ShapeDtypeStruct((t, D_MODEL), F32)],
        compiler_params=_cparams(1),
    )(h1, ogla, gg, oswa, wgn, wsn, wout, wpost)


def _mix_out_bwd(dh2, m, ogla, gg, oswa, wgn, wsn, wout, wpost, hook=None):
    t = dh2.shape[0]
    tm = _row_tile(t)

    def body(dh_ref, m_ref, og_ref, gg_ref, os_ref, wgn_ref, wsn_ref, wout_ref, wpost_ref,
             dog_ref, dgg_ref, dos_ref, dm_ref, dwpost_ref, dwgn_ref, dwsn_ref):
        @pl.when(pl.program_id(0) == 0)
        def _():
            dwpost_ref[...] = jnp.zeros_like(dwpost_ref)
            dwgn_ref[...] = jnp.zeros_like(dwgn_ref)
            dwsn_ref[...] = jnp.zeros_like(dwsn_ref)

        wpost = wpost_ref[...]
        _, mh, r = _rms(m_ref[...], wpost)
        dm, dw = _rms_bwd(mh, r, wpost, dh_ref[...])
        dwpost_ref[...] += dw
        dmb = dm.astype(BF16)
        dm_ref[...] = dmb
        dcat = _dg(dmb, wout_ref[...], NT)
        wgn = wgn_ref[...]
        for h in range(GLA_HEADS):
            ls = slice(GLA_DV * h, GLA_DV * (h + 1))
            dog = dcat[:, ls]
            g = gg_ref[:, ls]
            sg = _sigmoid(g)
            y, xh, r = _rms(og_ref[:, ls], wgn)
            dgg_ref[:, ls] = dog * y * (sg * (1.0 + g * (1.0 - sg)))
            dx, dw = _rms_bwd(xh, r, wgn, dog * (g * sg))
            dog_ref[:, ls] = dx
            dwgn_ref[...] += dw
        wsn = wsn_ref[...]
        _, xh, r = _rms(os_ref[...], wsn)
        dx, dw = _rms_bwd(xh, r, wsn, dcat[:, GLA_W:])
        dos_ref[...] = dx
        dwsn_ref[...] += dw

    def row(w):
        return pl.BlockSpec((tm, w), lambda i: (i, 0))

    def rshape(w, dt=F32):
        return jax.ShapeDtypeStruct((t, w), dt)

    return _pallas(
        body, name="mix_out_bwd", grid=(t // tm,),
        in_specs=[row(D_MODEL), row(D_MODEL), row(512), row(512), row(512), _full((1, GLA_DV)), _full((1, SWA_W)),
                  _full((D_MODEL, D_MODEL)), _full((1, D_MODEL))],
        out_specs=[row(512), row(512), row(512), row(D_MODEL), _full((1, D_MODEL)), _full((1, GLA_DV)),
                   _full((1, SWA_W))],
        out_shape=[rshape(512), rshape(512), rshape(512), rshape(D_MODEL, BF16),
                   jax.ShapeDtypeStruct((1, D_MODEL), F32), jax.ShapeDtypeStruct((1, GLA_DV), F32),
                   jax.ShapeDtypeStruct((1, SWA_W), F32)],
        args=(dh2, m, ogla, gg, oswa, wgn, wsn, wout, wpost), hook=hook)


def _mix_in_bwd(dh2, h1, wmixpre, winp, wa2p, bap, cos, sin, ga, dgq, dgk, dgv, dgg, dla, dsq, dsk, dsv, dkm, dvm):
    t = h1.shape[0]
    tm = _row_tile(t)

    def body(dh2_ref, h_ref, w_ref, win_ref, wa2_ref, ba_ref, cos_ref, sin_ref, ga_ref, dgq_ref, dgk_ref, dgv_ref,
             dgg_ref, dla_ref, dsq_ref, dsk_ref, dsv_ref, dkm_ref, dvm_ref,
             dh1_ref, dproj_ref, dw_ref, dwa2_ref, dba_ref):
        i = pl.program_id(0)

        @pl.when(i == 0)
        def _():
            dw_ref[...] = jnp.zeros_like(dw_ref)
            dwa2_ref[...] = jnp.zeros_like(dwa2_ref)
            dba_ref[...] = jnp.zeros_like(dba_ref)

        first = (i == 0).astype(F32)
        c = cos_ref[...]
        s = -sin_ref[...]
        fh = _first_half_mask(tm)
        dproj_ref[:, P_GQ:P_GK] = dgq_ref[...].astype(BF16)
        dproj_ref[:, P_GK:P_GV] = dgk_ref[...].astype(BF16)
        dproj_ref[:, P_GV:P_GG] = dgv_ref[...].astype(BF16)
        dproj_ref[:, P_GG:P_GA] = dgg_ref[...].astype(BF16)
        gab = ga_ref[...].astype(BF16)
        z = _dot(gab, wa2_ref[...]) + ba_ref[...]
        row_id = i * tm + lax.broadcasted_iota(jnp.int32, (tm, 1), 0)
        dz = jnp.where(row_id >= PAD, dla_ref[...] * (1.0 / GLA_TAU) * (1.0 - _sigmoid(z)), 0.0)
        dzb = dz.astype(BF16)
        dba_ref[...] += jnp.sum(dz, axis=0, keepdims=True)
        dwa2_ref[...] += _dg(gab, dzb, TN)
        dproj_ref[:, P_GA:P_SQ] = _dg(dzb, wa2_ref[...], NT).astype(BF16)
        for k in range(4):
            dy = dsq_ref[:, 128 * k:128 * (k + 1)]
            dproj_ref[:, P_SQ + 128 * k:P_SQ + 128 * (k + 1)] = (dy * c + _rot_half(dy, fh) * s).astype(BF16)
        for k in range(2):
            ls = slice(128 * k, 128 * (k + 1))
            dy = dsk_ref[:, ls]
            dy = jnp.concatenate([dy[:BLK] + first * dkm_ref[:, ls], dy[BLK:]], axis=0) if tm > BLK else (
                dy + first * dkm_ref[:, ls])
            dproj_ref[:, P_SK + 128 * k:P_SK + 128 * (k + 1)] = (dy * c + _rot_half(dy, fh) * s).astype(BF16)
            dv = dsv_ref[:, ls]
            dv = jnp.concatenate([dv[:BLK] + first * dvm_ref[:, ls], dv[BLK:]], axis=0) if tm > BLK else (
                dv + first * dvm_ref[:, ls])
            dproj_ref[:, P_SV + 128 * k:P_SV + 128 * (k + 1)] = dv.astype(BF16)
        dn = _dg(dproj_ref[...], win_ref[...], NT)
        w = w_ref[...]
        _, hh, r = _rms(h_ref[...], w)
        dx, dw = _rms_bwd(hh, r, w, dn)
        dw_ref[...] += dw
        dh1_ref[...] = dh2_ref[...] + dx

    def row(w):
        return pl.BlockSpec((tm, w), lambda i: (i, 0))

    return pl.pallas_call(
        body, name="mix_in_bwd", grid=(t // tm,),
        in_specs=[row(D_MODEL), row(D_MODEL), _full((1, D_MODEL)), _full((D_MODEL, P_END)), _full((128, GLA_KW)),
                  _full((1, GLA_KW)), row(128), row(128), row(128), row(256), row(256), row(512), row(512), row(256),
                  row(512), row(256), row(256), _full((BLK, 256)), _full((BLK, 256))],
        out_specs=[row(D_MODEL), row(P_END), _full((1, D_MODEL)), _full((128, GLA_KW)), _full((1, GLA_KW))],
        out_shape=[jax.ShapeDtypeStruct((t, D_MODEL), F32), jax.ShapeDtypeStruct((t, P_END), BF16),
                   jax.ShapeDtypeStruct((1, D_MODEL), F32), jax.ShapeDtypeStruct((128, GLA_KW), F32),
                   jax.ShapeDtypeStruct((1, GLA_KW), F32)],
        compiler_params=_cparams(1),
    )(dh2, h1, wmixpre, winp, wa2p, bap, cos, sin, ga, dgq, dgk, dgv, dgg, dla, dsq, dsk, dsv, dkm, dvm)


def _loss_head(h3, target):
    t = h3.shape[0]
    nb = t // BLK

    def body(h_ref, t_ref, dy_ref, loss_ref):
        n = pl.program_id(0)

        @pl.when(n == 0)
        def _():
            loss_ref[...] = jnp.zeros_like(loss_ref)
            dy_ref[...] = jnp.zeros_like(dy_ref)

        @pl.when(n > 0)
        def _():
            err = h_ref[...] - t_ref[...]
            dy_ref[...] = err * (1.0 / D_MODEL)
            part = jnp.sum(jnp.sum(err * err, axis=1, keepdims=True), axis=0, keepdims=True)
            loss_ref[...] += jnp.broadcast_to(part, (1, 128))

    return pl.pallas_call(
        body, name="loss_head", grid=(nb,),
        in_specs=[pl.BlockSpec((BLK, D_MODEL), lambda i: (i, 0)),
                  pl.BlockSpec((BLK, D_MODEL), lambda i: (jnp.maximum(i - 1, 0), 0))],
        out_specs=[pl.BlockSpec((BLK, D_MODEL), lambda i: (i, 0)), _full((1, 128))],
        out_shape=[jax.ShapeDtypeStruct((t, D_MODEL), F32), jax.ShapeDtypeStruct((1, 128), F32)],
        compiler_params=_cparams(1),
    )(h3, target)


def _adamw(w, g, m, v, g_row0=0):
    r, c = w.shape
    tr = _div_tile(r)
    off = g_row0 // tr
    assert off * tr == g_row0

    def body(w_ref, g_ref, m_ref, v_ref, go_ref, d_ref, nm_ref, nv_ref):
        g = g_ref[...]
        m = ADAM_B1 * m_ref[...] + (1.0 - ADAM_B1) * g
        v = ADAM_B2 * v_ref[...] + (1.0 - ADAM_B2) * (g * g)
        m_hat = m / (1.0 - ADAM_B1 ** ADAM_STEP)
        v_hat = v / (1.0 - ADAM_B2 ** ADAM_STEP)
        go_ref[...] = g
        d_ref[...] = -ADAM_LR * (m_hat / (jnp.sqrt(v_hat) + ADAM_EPS) + ADAM_WD * w_ref[...])
        nm_ref[...] = m
        nv_ref[...] = v

    spec = pl.BlockSpec((tr, c), lambda i: (i, 0))
    gspec = pl.BlockSpec((tr, c), lambda i: (i + off, 0))
    shape = jax.ShapeDtypeStruct((r, c), F32)
    return pl.pallas_call(
        body, name="adamw", grid=(r // tr,), in_specs=[spec, gspec, spec, spec], out_specs=[spec] * 4,
        out_shape=[shape] * 4, compiler_params=_cparams(1),
    )(w, g, m, v)


def _place():
    x, y, c = lax.axis_index("x"), lax.axis_index("y"), lax.axis_index("c")
    chips = [(1 - x, y), (x, 1 - y), (1 - x, 1 - y)]
    return x, y, c, chips


def _remote(send_sem, recv_sem, src, dst, to):
    return pltpu.make_async_remote_copy(src_ref=src, dst_ref=dst, send_sem=send_sem, recv_sem=recv_sem,
                                        device_id=to, device_id_type=MESH)


def _half(ref_rows, c):
    h = ref_rows // 2
    return pl.ds(pl.multiple_of(c * h, 8), h)


class _GatherChips:
    has_mid = True

    def __init__(self, shards):
        n = len(shards)
        self.inputs = list(shards)
        self.out_shape = [jax.ShapeDtypeStruct((N_CHIPS,) + s.shape, s.dtype) for s in shards]
        self.scratch = [pltpu.SemaphoreType.DMA((n, 6)), pltpu.SemaphoreType.DMA((n, 6)),
                        pltpu.SemaphoreType.DMA((n,))]

    def start(self, ins, outs, scr):
        send, recv, loc = scr
        x, y, c, chips = _place()
        q = 2 * x + y
        for t, (i_ref, o_ref) in enumerate(zip(ins, outs)):
            rows = _half(i_ref.shape[0], c)
            pltpu.make_async_copy(i_ref, o_ref.at[q], loc.at[t]).start()
            for j, (cx, cy) in enumerate(chips):
                _remote(send.at[t, j], recv.at[t, j], i_ref.at[rows], o_ref.at[q, rows], (cx, cy, c)).start()

    def mid(self, ins, outs, scr):
        send, recv, loc = scr
        x, y, c, chips = _place()
        for t, (i_ref, o_ref) in enumerate(zip(ins, outs)):
            rows = _half(i_ref.shape[0], c)
            for j, (cx, cy) in enumerate(chips):
                slot = o_ref.at[2 * cx + cy, rows]
                _remote(send.at[t, j], recv.at[t, j], slot, slot, (cx, cy, c)).wait_recv()
                _remote(send.at[t, 3 + j], recv.at[t, 3 + j], slot, slot, (x, y, 1 - c)).start()

    def finish(self, ins, outs, scr):
        send, recv, loc = scr
        x, y, c, chips = _place()
        q = 2 * x + y
        for t, (i_ref, o_ref) in enumerate(zip(ins, outs)):
            mine, other = _half(i_ref.shape[0], c), _half(i_ref.shape[0], 1 - c)
            for j, (cx, cy) in enumerate(chips):
                slot = o_ref.at[2 * cx + cy, other]
                _remote(send.at[t, 3 + j], recv.at[t, 3 + j], slot, slot, (x, y, 1 - c)).wait_recv()
            for j, (cx, cy) in enumerate(chips):
                sent = o_ref.at[2 * cx + cy, mine]
                _remote(send.at[t, j], recv.at[t, j], i_ref.at[mine], sent, (cx, cy, c)).wait_send()
                _remote(send.at[t, 3 + j], recv.at[t, 3 + j], sent, sent, (x, y, 1 - c)).wait_send()
            pltpu.make_async_copy(i_ref, o_ref.at[q], loc.at[t]).wait()


class _PairExchange:
    has_mid = False

    def __init__(self, arrs):
        n = len(arrs)
        self.inputs = list(arrs)
        self.out_shape = [jax.ShapeDtypeStruct((a.shape[0], a.shape[1] // 2, a.shape[2]), a.dtype) for a in arrs]
        self.scratch = [pltpu.SemaphoreType.DMA((n,)), pltpu.SemaphoreType.DMA((n,))]

    def _copies(self, ins, outs, scr):
        send, recv = scr
        x, y, c, _ = _place()
        return [_remote(send.at[t], recv.at[t], i_ref.at[:, _half(i_ref.shape[1], 1 - c)], o_ref, (x, y, 1 - c))
                for t, (i_ref, o_ref) in enumerate(zip(ins, outs))]

    def start(self, ins, outs, scr):
        for cp in self._copies(ins, outs, scr):
            cp.start()

    def finish(self, ins, outs, scr):
        for cp in self._copies(ins, outs, scr):
            cp.wait()


class _ChipScatter:
    has_mid = False

    def __init__(self, arrs):
        n = len(arrs)
        self.inputs = list(arrs)
        self.out_shape = [jax.ShapeDtypeStruct((3,) + a.shape[1:], a.dtype) for a in arrs]
        self.scratch = [pltpu.SemaphoreType.DMA((n, 3)), pltpu.SemaphoreType.DMA((n, 3))]

    def _copies(self, ins, outs, scr):
        send, recv = scr
        x, y, c, chips = _place()
        return [_remote(send.at[t, j], recv.at[t, j], i_ref.at[2 * cx + cy], o_ref.at[j], (cx, cy, c))
                for t, (i_ref, o_ref) in enumerate(zip(ins, outs)) for j, (cx, cy) in enumerate(chips)]

    def start(self, ins, outs, scr):
        for cp in self._copies(ins, outs, scr):
            cp.start()

    def finish(self, ins, outs, scr):
        for cp in self._copies(ins, outs, scr):
            cp.wait()


class _PairShare:
    has_mid = False

    def __init__(self, arrs):
        n = len(arrs)
        self.inputs = list(arrs)
        self.out_shape = [jax.ShapeDtypeStruct((2 * a.shape[0], a.shape[1]), a.dtype) for a in arrs]
        self.scratch = [pltpu.SemaphoreType.DMA((n,)), pltpu.SemaphoreType.DMA((n,)), pltpu.SemaphoreType.DMA((n,))]

    def start(self, ins, outs, scr):
        send, recv, loc = scr
        x, y, c, _ = _place()
        for t, (i_ref, o_ref) in enumerate(zip(ins, outs)):
            mine = o_ref.at[_half(o_ref.shape[0], c)]
            pltpu.make_async_copy(i_ref, mine, loc.at[t]).start()
            _remote(send.at[t], recv.at[t], i_ref, mine, (x, y, 1 - c)).start()

    def finish(self, ins, outs, scr):
        send, recv, loc = scr
        x, y, c, _ = _place()
        for t, (i_ref, o_ref) in enumerate(zip(ins, outs)):
            mine, other = o_ref.at[_half(o_ref.shape[0], c)], o_ref.at[_half(o_ref.shape[0], 1 - c)]
            _remote(send.at[t], recv.at[t], i_ref, other, (x, y, 1 - c)).wait_recv()
            _remote(send.at[t], recv.at[t], i_ref, mine, (x, y, 1 - c)).wait_send()
            pltpu.make_async_copy(i_ref, mine, loc.at[t]).wait()


def _comm_call(hook, name):
    n_in, n_out = len(hook.inputs), len(hook.out_shape)

    def body(*refs):
        ins, outs, scr = refs[:n_in], refs[n_in:n_in + n_out], refs[n_in + n_out:]
        hook.start(ins, outs, scr)
        if hook.has_mid:
            hook.mid(ins, outs, scr)
        hook.finish(ins, outs, scr)

    return pl.pallas_call(body, name=name, in_specs=[ANY] * n_in, out_specs=[ANY] * n_out,
                          out_shape=list(hook.out_shape), scratch_shapes=list(hook.scratch))(*hook.inputs)


def _all_gather_devices(vec):
    r, w = vec.shape

    def body(x_ref, out_ref, send_sems, recv_sems, local_sem):
        x, y, c, chips = _place()
        me, sibling = (x, y, c), (x, y, 1 - c)

        def rows(px, py, pc):
            return out_ref.at[4 * px + 2 * py + pc]

        def copy(k, block, to, src=None):
            return pltpu.make_async_remote_copy(
                src_ref=rows(*block) if src is None else src, dst_ref=rows(*block), send_sem=send_sems.at[k],
                recv_sem=recv_sems.at[k], device_id=to, device_id_type=MESH)

        mine = pltpu.make_async_copy(x_ref, rows(*me), local_sem)
        mine.start()
        first = [copy(0, me, sibling, src=x_ref)]
        first += [copy(1 + j, me, (*chip, c), src=x_ref) for j, chip in enumerate(chips)]
        for cp in first:
            cp.start()
        passed = [copy(4 + j, (*chip, c), sibling) for j, chip in enumerate(chips)]
        for j, chip in enumerate(chips):
            copy(1 + j, (*chip, c), me).wait_recv()
            passed[j].start()
        copy(0, sibling, me).wait_recv()
        for j, chip in enumerate(chips):
            copy(4 + j, (*chip, 1 - c), me).wait_recv()
        for cp in first + passed:
            cp.wait_send()
        mine.wait()

    return pl.pallas_call(
        body, name="all_gather_devices",
        in_specs=[pl.BlockSpec(memory_space=pltpu.VMEM)], out_specs=pl.BlockSpec(memory_space=pltpu.VMEM),
        out_shape=jax.ShapeDtypeStruct((N_DEV, r, w), vec.dtype),
        scratch_shapes=[pltpu.SemaphoreType.DMA((7,)), pltpu.SemaphoreType.DMA((7,)), pltpu.SemaphoreType.DMA],
    )(vec)


def _pair_sum(g, other, c_idx):
    nq, r, w = g.shape
    h = r // 2
    tr = _div_tile(h)
    nt = h // tr

    def body(c_ref, g_ref, o_ref, s_ref):
        s_ref[...] = (g_ref[...].astype(F32) + o_ref[...].astype(F32)).astype(s_ref.dtype)

    return pl.pallas_call(
        body, name="pair_sum",
        grid_spec=pltpu.PrefetchScalarGridSpec(
            num_scalar_prefetch=1, grid=(nq, nt),
            in_specs=[pl.BlockSpec((None, tr, w), lambda k, i, c_ref: (k, c_ref[0] * nt + i, 0)),
                      pl.BlockSpec((None, tr, w), lambda k, i, c_ref: (k, i, 0))],
            out_specs=pl.BlockSpec((None, tr, w), lambda k, i, c_ref: (k, i, 0))),
        out_shape=jax.ShapeDtypeStruct((nq, h, w), g.dtype),
        compiler_params=_cparams(2),
    )(c_idx, g, other)


def _chip_sum(s, others, q_idx):
    _, h, w = s.shape
    tr = _div_tile(h)

    def body(q_ref, s_ref, o_ref, out_ref):
        out_ref[...] = ((s_ref[...].astype(F32) + o_ref[0].astype(F32)) + o_ref[1].astype(F32)) + o_ref[2].astype(F32)

    return pl.pallas_call(
        body, name="chip_sum",
        grid_spec=pltpu.PrefetchScalarGridSpec(
            num_scalar_prefetch=1, grid=(h // tr,),
            in_specs=[pl.BlockSpec((None, tr, w), lambda i, q_ref: (q_ref[0], i, 0)),
                      pl.BlockSpec((3, tr, w), lambda i, q_ref: (0, i, 0))],
            out_specs=pl.BlockSpec((tr, w), lambda i, q_ref: (i, 0))),
        out_shape=jax.ShapeDtypeStruct((h, w), F32),
        compiler_params=_cparams(1),
    )(q_idx, s, others)


def _sum_devices(parts):
    nd, r, w = parts.shape

    def body(p_ref, o_ref):
        acc = p_ref[0]
        for k in range(1, nd):
            acc = acc + p_ref[k]
        o_ref[...] = acc

    return pl.pallas_call(
        body, name="sum_devices", in_specs=[_full((nd, r, w))], out_specs=_full((r, w)),
        out_shape=jax.ShapeDtypeStruct((r, w), parts.dtype), grid=(1,), compiler_params=_cparams(1),
    )(parts)


def _pack_win(w_in):
    o = np.cumsum((0,) + IN_SPLITS)
    gq, gk, gv, gg, ga, sq, sk, sv = [w_in[:, o[i]:o[i + 1]] for i in range(8)]
    z = jnp.zeros((w_in.shape[0], 128 - GLA_RANK), w_in.dtype)
    dup = lambda a: jnp.concatenate([a[:, :64], a[:, :64], a[:, 64:], a[:, 64:]], axis=1)
    return jnp.concatenate([gq, gk, gv, gg, ga, z, sq, dup(sk), dup(sv)], axis=1)


def _unpack_dwin(d):
    und = lambda a: jnp.concatenate([a[:, 0:64] + a[:, 64:128], a[:, 128:192] + a[:, 192:256]], axis=1)
    return jnp.concatenate([d[:, :P_GA], d[:, P_GA:P_GA + GLA_RANK], d[:, P_SQ:P_SK], und(d[:, P_SK:P_SV]),
                            und(d[:, P_SV:P_END])], axis=1)


def _local_step(x, target, meta, p):
    s = x.shape[0]
    t = s + BLK
    h0 = jnp.concatenate([jnp.zeros((PAD, D_MODEL), F32), meta, x], axis=0)
    cos, sin = _rope_tables(t)

    h1, n1, g1, u1, f1 = _ffn_fwd(h0, p["ffn1_pre_norm"], p["ffn1_w_gate"], p["ffn1_w_up"], p["ffn1_w_down"],
                                  p["ffn1_post_norm"])
    n2, gq, gk, gv, gg, ga, la, sq, sk, sv = _mix_proj(h1, p["mix_pre_norm"], p["w_in"], p["gla_w_a2"], p["gla_b_a"],
                                                       cos, sin)
    ogla, ss = _gla_fwd(gq, gk, gv, la)
    oswa = _swa_fwd(p["swa_sinks"], sq, sk, sv)
    h2, cat, m = _mix_out(h1, ogla, gg, oswa, p["gla_out_norm"], p["swa_out_norm"], p["w_out"], p["mix_post_norm"])
    h3, n3, g3, u3, f3 = _ffn_fwd(h2, p["ffn2_pre_norm"], p["ffn2_w_gate"], p["ffn2_w_up"], p["ffn2_w_down"],
                                  p["ffn2_post_norm"])
    dy, sse = _loss_head(h3, target)

    grads = {}
    dh2, df3, dg3, du3, a3, grads["ffn2_pre_norm"], grads["ffn2_post_norm"] = _ffn_bwd(
        dy, h2, f3, g3, u3, p["ffn2_pre_norm"], p["ffn2_w_gate"], p["ffn2_w_up"], p["ffn2_w_down"],
        p["ffn2_post_norm"])
    gu, grads["ffn2_w_down"] = _ffn_wgrad(n3, df3, dg3, du3, a3)
    grads["ffn2_w_gate"], grads["ffn2_w_up"] = gu[:, :D_MODEL], gu[:, D_MODEL:]

    dogla, dgg, doswa, dm, grads["mix_post_norm"], grads["gla_out_norm"], grads["swa_out_norm"] = _mix_out_bwd(
        dh2, m, ogla, gg, oswa, p["gla_out_norm"], p["swa_out_norm"], p["w_out"], p["mix_post_norm"])
    grads["w_out"] = _xty(cat, dm)
    dsq, dsk, dsv, dkm, dvm, dsinks = _swa_bwd(p["swa_sinks"], sq, sk, sv, oswa, doswa)
    grads["swa_sinks"] = dsinks[:, 0]
    dgq, dgk, dgv, dla = _gla_bwd(gq, gk, gv, la, ss, dogla)
    dh1, dproj, grads["mix_pre_norm"], dwa2p, grads["gla_b_a"] = _mix_in_bwd(
        dh2, h1, p["mix_pre_norm"], p["w_in"], p["gla_w_a2"], p["gla_b_a"], cos, sin, ga, dgq, dgk, dgv, dgg, dla,
        dsq, dsk, dsv, dkm, dvm)
    grads["gla_w_a2"] = dwa2p[:GLA_RANK]
    grads["w_in"] = _unpack_dwin(_xty(n2, dproj))

    dh0, df1, dg1, du1, a1, grads["ffn1_pre_norm"], grads["ffn1_post_norm"] = _ffn_bwd(
        dh1, h0, f1, g1, u1, p["ffn1_pre_norm"], p["ffn1_w_gate"], p["ffn1_w_up"], p["ffn1_w_down"],
        p["ffn1_post_norm"])
    gu, grads["ffn1_w_down"] = _ffn_wgrad(n1, df1, dg1, du1, a1)
    grads["ffn1_w_gate"], grads["ffn1_w_up"] = gu[:, :D_MODEL], gu[:, D_MODEL:]
    grads["meta_tokens"] = dh0[PAD:BLK]
    return sse[0, 0], dh0[BLK:], grads


WEIGHTS = ['meta_tokens', 'ffn1_pre_norm', 'ffn1_w_gate', 'ffn1_w_up', 'ffn1_w_down', 'ffn1_post_norm',
           'mix_pre_norm', 'w_in', 'gla_w_a2', 'gla_b_a', 'gla_out_norm', 'swa_sinks', 'swa_out_norm', 'w_out',
           'mix_post_norm', 'ffn2_pre_norm', 'ffn2_w_gate', 'ffn2_w_up', 'ffn2_w_down', 'ffn2_post_norm']
BIG = ['ffn1_w_gate', 'ffn1_w_up', 'ffn1_w_down', 'w_in', 'w_out', 'ffn2_w_gate', 'ffn2_w_up', 'ffn2_w_down']
SMALL = [n for n in WEIGHTS if n not in BIG]
FJ = D_FF // N_CHIPS
D_IN_J = D_IN // N_CHIPS
D_OUT_J = D_MODEL // N_CHIPS
BIG_SHARD = {'ffn1_w_gate': (D_MODEL, FJ), 'ffn1_w_up': (D_MODEL, FJ), 'ffn1_w_down': (FJ, D_MODEL),
             'w_in': (D_MODEL, D_IN_J), 'w_out': (D_OUT_J, D_MODEL),
             'ffn2_w_gate': (D_MODEL, FJ), 'ffn2_w_up': (D_MODEL, FJ), 'ffn2_w_down': (FJ, D_MODEL)}


def _small_rows(name, a):
    flat = a.reshape(-1)
    rows = -(-flat.shape[0] // 1024) * 8
    return jnp.pad(flat, (0, rows * 128 - flat.shape[0])).reshape(rows, 128)


def kernel(x, meta_tokens, ffn1_pre_norm, ffn1_w_gate, ffn1_w_up, ffn1_w_down, ffn1_post_norm, mix_pre_norm, w_in, gla_w_a2, gla_b_a, gla_out_norm, swa_sinks, swa_out_norm, w_out, mix_post_norm, ffn2_pre_norm, ffn2_w_gate, ffn2_w_up, ffn2_w_down, ffn2_post_norm, loss_target, m_meta_tokens, m_ffn1_pre_norm, m_ffn1_w_gate, m_ffn1_w_up, m_ffn1_w_down, m_ffn1_post_norm, m_mix_pre_norm, m_w_in, m_gla_w_a2, m_gla_b_a, m_gla_out_norm, m_swa_sinks, m_swa_out_norm, m_w_out, m_mix_post_norm, m_ffn2_pre_norm, m_ffn2_w_gate, m_ffn2_w_up, m_ffn2_w_down, m_ffn2_post_norm, v_meta_tokens, v_ffn1_pre_norm, v_ffn1_w_gate, v_ffn1_w_up, v_ffn1_w_down, v_ffn1_post_norm, v_mix_pre_norm, v_w_in, v_gla_w_a2, v_gla_b_a, v_gla_out_norm, v_swa_sinks, v_swa_out_norm, v_w_out, v_mix_post_norm, v_ffn2_pre_norm, v_ffn2_w_gate, v_ffn2_w_up, v_ffn2_w_down, v_ffn2_post_norm):
    args = dict(locals())
    w = {n: args[n] for n in WEIGHTS}
    mom = {n: args["m_" + n] for n in WEIGHTS}
    var = {n: args["v_" + n] for n in WEIGHTS}
    cx, cy, cc = lax.axis_index("x"), lax.axis_index("y"), lax.axis_index("c")
    q_idx = (2 * cx + cy).astype(jnp.int32).reshape(1)
    c_idx = cc.astype(jnp.int32).reshape(1)

    bf = {n: w[n].reshape(BIG_SHARD[n]).astype(BF16) for n in BIG}
    early = _GatherChips([bf["ffn1_w_gate"], bf["ffn1_w_up"], bf["ffn1_w_down"], w["meta_tokens"],
                          w["gla_w_a2"].reshape(GLA_RANK, GLA_KW // N_CHIPS)])
    wg1, wu1, wd1, meta4, wa24 = _comm_call(early, "gather_ffn1")
    meta_full = meta4.transpose(1, 0, 2).reshape(N_META, D_MODEL)
    wa2p = jnp.pad(wa24.transpose(1, 0, 2).reshape(GLA_RANK, GLA_KW), ((0, 128 - GLA_RANK), (0, 0))).astype(BF16)
    sinks = w["swa_sinks"].reshape(SWA_QH)

    seq, target = x[0], loss_target[0]
    t = seq.shape[0] + BLK
    h0 = jnp.concatenate([jnp.zeros((PAD, D_MODEL), F32), meta_full, seq], axis=0)
    cos, sin = _rope_tables(t)
    late = _GatherChips([bf["w_in"], bf["w_out"], bf["ffn2_w_gate"], bf["ffn2_w_up"], bf["ffn2_w_down"]])
    (h1, n1, g1, u1, f1), (win4, wout4, wg2, wu2, wd2) = _ffn_fwd(
        h0, w["ffn1_pre_norm"], wg1, wu1, wd1, w["ffn1_post_norm"], hook=late)
    winp = _pack_win(win4.transpose(1, 0, 2).reshape(D_MODEL, D_IN))
    wout = wout4.reshape(D_MODEL, D_MODEL)
    n2, gq, gk, gv, gg, ga, la, sq, sk, sv = _mix_proj(h1, w["mix_pre_norm"], winp, wa2p, w["gla_b_a"], cos, sin)
    ogla, ss = _gla_fwd(gq, gk, gv, la)
    oswa = _swa_fwd(sinks, sq, sk, sv)
    h2, cat, m = _mix_out(h1, ogla, gg, oswa, w["gla_out_norm"], w["swa_out_norm"], wout, w["mix_post_norm"])
    h3, n3, g3, u3, f3 = _ffn_fwd(h2, w["ffn2_pre_norm"], wg2, wu2, wd2, w["ffn2_post_norm"])
    dy, sse = _loss_head(h3, target)
    loss = lax.psum(sse[0, 0] * (0.5 / D_MODEL), ("x", "y", "c"))

    g = {}
    dh2, df3, dg3, du3, a3, g["ffn2_pre_norm"], g["ffn2_post_norm"] = _ffn_bwd(
        dy, h2, f3, g3, u3, w["ffn2_pre_norm"], wg2, wu2, wd2, w["ffn2_post_norm"])
    gu2, gd2 = _ffn_wgrad(n3, df3, dg3, du3, a3)
    (dogla, dgg, doswa, dm, g["mix_post_norm"], g["gla_out_norm"], g["swa_out_norm"]), (rgu2, rgd2) = _mix_out_bwd(
        dh2, m, ogla, gg, oswa, w["gla_out_norm"], w["swa_out_norm"], wout, w["mix_post_norm"],
        hook=_PairExchange([gu2, gd2]))
    sgu2, sgd2 = _pair_sum(gu2, rgu2, c_idx), _pair_sum(gd2, rgd2, c_idx)
    gout = _xty(cat, dm).reshape(N_CHIPS, D_OUT_J, D_MODEL).astype(BF16)
    (dsq, dsk, dsv, dkm, dvm, dsinks), (ogu2, ogd2) = _swa_bwd(sinks, sq, sk, sv, oswa, doswa,
                                                               hook=_ChipScatter([sgu2, sgd2]))
    g["swa_sinks"] = dsinks[:, 0].reshape(1, SWA_QH)
    dgq, dgk, dgv, dla = _gla_bwd(gq, gk, gv, la, ss, dogla)
    dh1, dproj, g["mix_pre_norm"], dwa2p, g["gla_b_a"] = _mix_in_bwd(
        dh2, h1, w["mix_pre_norm"], winp, wa2p, w["gla_b_a"], cos, sin, ga, dgq, dgk, dgv, dgg, dla,
        dsq, dsk, dsv, dkm, dvm)
    g["gla_w_a2"] = dwa2p[:GLA_RANK]
    gin = _unpack_dwin(_xty(n2, dproj)).reshape(D_MODEL, N_CHIPS, D_IN_J).transpose(1, 0, 2).astype(BF16)
    (dh0, df1, dg1, du1, a1, g["ffn1_pre_norm"], g["ffn1_post_norm"]), (rgin, rgout) = _ffn_bwd(
        dh1, h0, f1, g1, u1, w["ffn1_pre_norm"], wg1, wu1, wd1, w["ffn1_post_norm"],
        hook=_PairExchange([gin, gout]))
    sgin, sgout = _pair_sum(gin, rgin, c_idx), _pair_sum(gout, rgout, c_idx)
    (gu1, gd1), (ogin, ogout) = _ffn_wgrad(n1, df1, dg1, du1, a1, hook=_ChipScatter([sgin, sgout]))
    g["meta_tokens"] = dh0[PAD:BLK]
    grad_x = dh0[BLK:]
    rgu1, rgd1 = _comm_call(_PairExchange([gu1, gd1]), "pair_exchange_ffn1")
    sgu1, sgd1 = _pair_sum(gu1, rgu1, c_idx), _pair_sum(gd1, rgd1, c_idx)
    ogu1, ogd1 = _comm_call(_ChipScatter([sgu1, sgd1]), "chip_scatter_ffn1")
    halves = [_chip_sum(s, o, q_idx) for s, o in ((sgu1, ogu1), (sgd1, ogd1), (sgin, ogin), (sgout, ogout),
                                                   (sgu2, ogu2), (sgd2, ogd2))]
    rgu1, rgd1, rgin, rgout, rgu2, rgd2 = _comm_call(_PairShare(halves), "pair_share")
    reduced = {"ffn1_w_gate": (rgu1, 0), "ffn1_w_up": (rgu1, D_MODEL), "ffn1_w_down": (rgd1, 0), "w_in": (rgin, 0),
               "w_out": (rgout, 0), "ffn2_w_gate": (rgu2, 0), "ffn2_w_up": (rgu2, D_MODEL), "ffn2_w_down": (rgd2, 0)}
    grad, delta, new_m, new_v = {}, {}, {}, {}
    for n in BIG:
        shard = BIG_SHARD[n]
        outs = _adamw(w[n].reshape(shard), reduced[n][0], mom[n].reshape(shard), var[n].reshape(shard),
                      g_row0=reduced[n][1])
        grad[n], delta[n], new_m[n], new_v[n] = [a.reshape(w[n].shape) for a in outs]

    small_rows = [_small_rows(n, g[n]) for n in SMALL]
    ssizes = [a.shape[0] for a in small_rows]
    gsmall = _sum_devices(_all_gather_devices(jnp.concatenate(small_rows, axis=0)))
    col0 = {"meta_tokens": D_MODEL // N_CHIPS, "gla_w_a2": GLA_KW // N_CHIPS}
    gs, ws, ms, vs = [], [], [], []
    off = 0
    for n, sz in zip(SMALL, ssizes):
        full_shape = g[n].shape
        gn = gsmall[off:off + sz].reshape(-1)[:math.prod(full_shape)].reshape(full_shape)
        off += sz
        if n in col0:
            gn = lax.dynamic_slice_in_dim(gn, (2 * cx + cy) * col0[n], col0[n], axis=1)
        grad[n] = gn.reshape(w[n].shape)
        gs.append(_small_rows(n, grad[n]))
        ws.append(_small_rows(n, w[n]))
        ms.append(_small_rows(n, mom[n]))
        vs.append(_small_rows(n, var[n]))
    psizes = [a.shape[0] for a in gs]
    _, d, nm, nv = _adamw(*[jnp.concatenate(a, axis=0) for a in (ws, gs, ms, vs)])
    off = 0
    for n, sz in zip(SMALL, psizes):
        cnt = math.prod(w[n].shape)
        delta[n], new_m[n], new_v[n] = [a[off:off + sz].reshape(-1)[:cnt].reshape(w[n].shape) for a in (d, nm, nv)]
        off += sz

    return (loss, grad_x[None], *[grad[n] for n in WEIGHTS], *[delta[n] for n in WEIGHTS],
            *[new_m[n] for n in WEIGHTS], *[new_v[n] for n in WEIGHTS])
```

```python
import functools
import math

import numpy as np
import jax
import jax.numpy as jnp
from jax import lax
from jax.experimental import pallas as pl
from jax.experimental.pallas import tpu as pltpu

F32 = jnp.float32
BF16 = jnp.bfloat16
MESH = pl.DeviceIdType.MESH

D_MODEL = 1024
D_FF = 2816
N_CHIPS = 4
N_DEV = 8
N_META = 16
BLK = 128
PAD = BLK - N_META
GLA_CHUNK = 64
GLA_HEADS = 4
GLA_DV = 128
GLA_DK = 64
GLA_KW = GLA_HEADS * GLA_DK
GLA_W = GLA_HEADS * GLA_DV
GLA_RANK = 16
GLA_TAU = 16.0
SWA_HD = 64
SWA_QH = 8
SWA_KVH = 2
SWA_W = SWA_QH * SWA_HD
WINDOW = 128
ROPE_THETA = 10000.0
EPS = 1e-6
NEG_INF = -1e30
IN_SPLITS = (256, 256, 512, 512, 16, 512, 128, 128)
D_IN = sum(IN_SPLITS)
P_GQ, P_GK, P_GV, P_GG, P_GA, P_SQ, P_SK, P_SV, P_END = 0, 256, 512, 1024, 1536, 1664, 2176, 2432, 2688
ADAM_LR, ADAM_B1, ADAM_B2, ADAM_EPS, ADAM_WD, ADAM_STEP = 0.001, 0.9, 0.999, 1e-08, 0.01, 10
VMEM_LIMIT = 56 * 1024 * 1024

NT = (((1,), (1,)), ((), ()))
TN = (((0,), (0,)), ((), ()))


def _cparams(n_axes):
    return pltpu.CompilerParams(dimension_semantics=("arbitrary",) * n_axes, vmem_limit_bytes=VMEM_LIMIT)


def _row_tile(t):
    for tm in (640, 512, 384, 256, 128):
        if t % tm == 0:
            return tm
    raise ValueError(t)


def _div_tile(r, cap=512):
    best = None
    for tr in range(8, min(r, cap) + 1, 8):
        if r % tr == 0:
            best = tr
    return best if best is not None else r


def _dot(a, b):
    return jnp.dot(a, b, preferred_element_type=F32)


def _dg(a, b, dims):
    return lax.dot_general(a, b, dims, preferred_element_type=F32)


def _rms(x, w):
    r = lax.rsqrt(jnp.mean(x * x, axis=-1, keepdims=True) + EPS)
    xh = x * r
    return xh * w, xh, r


def _rms_bwd(xh, r, w, dy):
    wdy = dy * w
    dx = r * (wdy - xh * jnp.mean(wdy * xh, axis=-1, keepdims=True))
    dw = jnp.sum(dy * xh, axis=0, keepdims=True)
    return dx, dw


def _sigmoid(x):
    return 1.0 / (1.0 + jnp.exp(-x))


def _full(shape):
    nd = len(shape)
    return pl.BlockSpec(shape, lambda *_: (0,) * nd)


ANY = pl.BlockSpec(memory_space=pl.ANY)


def _pallas(body, *, name, grid, in_specs, out_specs, out_shape, args, scratch_shapes=(), hook=None):
    n_axes = len(grid)
    if hook is None:
        return pl.pallas_call(body, name=name, grid=grid, in_specs=list(in_specs), out_specs=list(out_specs),
                              out_shape=list(out_shape), scratch_shapes=list(scratch_shapes),
                              compiler_params=_cparams(n_axes))(*args)
    n_in, n_out, n_scr = len(in_specs), len(out_specs), len(scratch_shapes)
    h_in, h_out = len(hook.inputs), len(hook.out_shape)
    total = math.prod(grid)

    def wrapped(*refs):
        ins, hins = refs[:n_in], refs[n_in:n_in + h_in]
        o0 = n_in + h_in
        outs, houts = refs[o0:o0 + n_out], refs[o0 + n_out:o0 + n_out + h_out]
        s0 = o0 + n_out + h_out
        scr, hscr = refs[s0:s0 + n_scr], refs[s0 + n_scr:]
        step = pl.program_id(0)
        for a in range(1, n_axes):
            step = step * grid[a] + pl.program_id(a)

        @pl.when(step == 0)
        def _():
            hook.start(hins, houts, hscr)

        body(*ins, *outs, *scr)

        if hook.has_mid:
            @pl.when(step == total // 2)
            def _():
                hook.mid(hins, houts, hscr)

        @pl.when(step == total - 1)
        def _():
            hook.finish(hins, houts, hscr)

    res = pl.pallas_call(
        wrapped, name=name, grid=grid, in_specs=list(in_specs) + [ANY] * h_in,
        out_specs=list(out_specs) + [ANY] * h_out, out_shape=list(out_shape) + list(hook.out_shape),
        scratch_shapes=list(scratch_shapes) + list(hook.scratch), compiler_params=_cparams(n_axes),
        input_output_aliases={n_in + a: n_out + b for a, b in hook.aliases},
    )(*args, *hook.inputs)
    return res[:n_out], res[n_out:]


def _ffn_fwd(h, wpre, wg4, wu4, wd4, wpost, hook=None):
    t = h.shape[0]
    tm = _row_tile(t)
    nj, _, fj = wg4.shape

    def body(h_ref, wpre_ref, wg_ref, wu_ref, wd_ref, wpost_ref, hout_ref, n_ref, g_ref, u_ref, f_ref, acc_ref):
        j = pl.program_id(1)

        @pl.when(j == 0)
        def _():
            y, _, _ = _rms(h_ref[...], wpre_ref[...])
            n_ref[...] = y.astype(BF16)
            acc_ref[...] = jnp.zeros_like(acc_ref)

        n = n_ref[...]
        g = _dot(n, wg_ref[...])
        u = _dot(n, wu_ref[...])
        g_ref[...] = g.astype(BF16)
        u_ref[...] = u.astype(BF16)
        a = g * _sigmoid(g) * u
        acc_ref[...] += _dot(a.astype(BF16), wd_ref[...])

        @pl.when(j == nj - 1)
        def _():
            f = acc_ref[...]
            f_ref[...] = f
            y, _, _ = _rms(f, wpost_ref[...])
            hout_ref[...] = h_ref[...] + 0.5 * y

    row = pl.BlockSpec((tm, D_MODEL), lambda i, j: (i, 0))
    vec = pl.BlockSpec((1, D_MODEL), lambda i, j: (0, 0))
    wcol = pl.BlockSpec((None, D_MODEL, fj), lambda i, j: (j, 0, 0))
    wrow = pl.BlockSpec((None, fj, D_MODEL), lambda i, j: (j, 0, 0))
    act = pl.BlockSpec((None, tm, fj), lambda i, j: (j, i, 0))
    return _pallas(
        body, name="ffn_fwd", grid=(t // tm, nj),
        in_specs=[row, vec, wcol, wcol, wrow, vec],
        out_specs=[row, row, act, act, row],
        out_shape=[jax.ShapeDtypeStruct((t, D_MODEL), F32), jax.ShapeDtypeStruct((t, D_MODEL), BF16),
                   jax.ShapeDtypeStruct((nj, t, fj), BF16), jax.ShapeDtypeStruct((nj, t, fj), BF16),
                   jax.ShapeDtypeStruct((t, D_MODEL), F32)],
        scratch_shapes=[pltpu.VMEM((tm, D_MODEL), F32)],
        args=(h, wpre, wg4, wu4, wd4, wpost), hook=hook)


def _ffn_bwd(dhout, h, f, g4, u4, wpre, wg4, wu4, wd4, wpost, hook=None):
    t = h.shape[0]
    tm = _row_tile(t)
    nj, _, fj = wg4.shape

    def body(dhout_ref, h_ref, f_ref, g_ref, u_ref, wpre_ref, wg_ref, wu_ref, wd_ref, wpost_ref,
             dh_ref, df_ref, dg_ref, du_ref, a_ref, dwpre_ref, dwpost_ref, dn_ref):
        i = pl.program_id(0)
        j = pl.program_id(1)

        @pl.when((i == 0) & (j == 0))
        def _():
            dwpre_ref[...] = jnp.zeros_like(dwpre_ref)
            dwpost_ref[...] = jnp.zeros_like(dwpost_ref)

        @pl.when(j == 0)
        def _():
            wpost = wpost_ref[...]
            _, fh, r = _rms(f_ref[...], wpost)
            df, dw = _rms_bwd(fh, r, wpost, 0.5 * dhout_ref[...])
            dwpost_ref[...] += dw
            df_ref[...] = df.astype(BF16)
            dn_ref[...] = jnp.zeros_like(dn_ref)

        da = _dg(df_ref[...], wd_ref[...], NT)
        g = g_ref[...].astype(F32)
        u = u_ref[...].astype(F32)
        sg = _sigmoid(g)
        silu = g * sg
        dg = (da * u * (sg * (1.0 + g * (1.0 - sg)))).astype(BF16)
        du = (da * silu).astype(BF16)
        dg_ref[...] = dg
        du_ref[...] = du
        a_ref[...] = (silu * u).astype(BF16)
        dn_ref[...] += _dg(dg, wg_ref[...], NT) + _dg(du, wu_ref[...], NT)

        @pl.when(j == nj - 1)
        def _():
            wpre = wpre_ref[...]
            _, hh, r = _rms(h_ref[...], wpre)
            dx, dw = _rms_bwd(hh, r, wpre, dn_ref[...])
            dwpre_ref[...] += dw
            dh_ref[...] = dhout_ref[...] + dx

    row = pl.BlockSpec((tm, D_MODEL), lambda i, j: (i, 0))
    vec = pl.BlockSpec((1, D_MODEL), lambda i, j: (0, 0))
    wcol = pl.BlockSpec((None, D_MODEL, fj), lambda i, j: (j, 0, 0))
    wrow = pl.BlockSpec((None, fj, D_MODEL), lambda i, j: (j, 0, 0))
    act = pl.BlockSpec((None, tm, fj), lambda i, j: (j, i, 0))
    actshape = jax.ShapeDtypeStruct((nj, t, fj), BF16)
    return _pallas(
        body, name="ffn_bwd", grid=(t // tm, nj),
        in_specs=[row, row, row, act, act, vec, wcol, wcol, wrow, vec],
        out_specs=[row, row, act, act, act, vec, vec],
        out_shape=[jax.ShapeDtypeStruct((t, D_MODEL), F32), jax.ShapeDtypeStruct((t, D_MODEL), BF16),
                   actshape, actshape, actshape,
                   jax.ShapeDtypeStruct((1, D_MODEL), F32), jax.ShapeDtypeStruct((1, D_MODEL), F32)],
        scratch_shapes=[pltpu.VMEM((tm, D_MODEL), F32)],
        args=(dhout, h, f, g4, u4, wpre, wg4, wu4, wd4, wpost), hook=hook)


def _ffn_wgrad(n, df, dg4, du4, a4, hook=None):
    t = n.shape[0]
    tm = _row_tile(t)
    ni = t // tm
    nj, _, fj = dg4.shape

    def body(n_ref, df_ref, dg_ref, du_ref, a_ref, dwgu_ref, dwd_ref, gu_acc, d_acc):
        i = pl.program_id(1)

        @pl.when(i == 0)
        def _():
            gu_acc[...] = jnp.zeros_like(gu_acc)
            d_acc[...] = jnp.zeros_like(d_acc)

        nn = n_ref[...]
        gu_acc[0:D_MODEL, :] += _dg(nn, dg_ref[...], TN)
        gu_acc[D_MODEL:2 * D_MODEL, :] += _dg(nn, du_ref[...], TN)
        d_acc[...] += _dg(a_ref[...], df_ref[...], TN)

        @pl.when(i == ni - 1)
        def _():
            dwgu_ref[...] = gu_acc[...].astype(BF16)
            dwd_ref[...] = d_acc[...].astype(BF16)

    row = pl.BlockSpec((tm, D_MODEL), lambda j, i: (i, 0))
    act = pl.BlockSpec((None, tm, fj), lambda j, i: (j, i, 0))
    wcol = pl.BlockSpec((None, 2 * D_MODEL, fj), lambda j, i: (j, 0, 0))
    wrow = pl.BlockSpec((None, fj, D_MODEL), lambda j, i: (j, 0, 0))
    return _pallas(
        body, name="ffn_wgrad", grid=(nj, ni),
        in_specs=[row, row, act, act, act],
        out_specs=[wcol, wrow],
        out_shape=[jax.ShapeDtypeStruct((nj, 2 * D_MODEL, fj), BF16), jax.ShapeDtypeStruct((nj, fj, D_MODEL), BF16)],
        scratch_shapes=[pltpu.VMEM((2 * D_MODEL, fj), F32), pltpu.VMEM((fj, D_MODEL), F32)],
        args=(n, df, dg4, du4, a4), hook=hook)


def _xty(x, y):
    t, k = x.shape
    n = y.shape[1]
    tm = _row_tile(t)
    tn = n if n <= 1024 else (896 if n % 896 == 0 else 128)

    def body(x_ref, y_ref, o_ref):
        @pl.when(pl.program_id(1) == 0)
        def _():
            o_ref[...] = jnp.zeros_like(o_ref)

        o_ref[...] += _dg(x_ref[...], y_ref[...], TN)

    return pl.pallas_call(
        body, name="xty", grid=(n // tn, t // tm),
        in_specs=[pl.BlockSpec((tm, k), lambda j, i: (i, 0)), pl.BlockSpec((tm, tn), lambda j, i: (i, j))],
        out_specs=pl.BlockSpec((k, tn), lambda j, i: (0, j)),
        out_shape=jax.ShapeDtypeStruct((k, n), F32),
        compiler_params=_cparams(2),
    )(x, y)


def _rope_tables(t):
    pos = (jnp.arange(t, dtype=jnp.int32) - PAD).astype(F32)
    inv_freq = 1.0 / (ROPE_THETA ** (jnp.arange(0, SWA_HD, 2, dtype=F32) / SWA_HD))
    ang = pos[:, None] * inv_freq[None, :]
    cos = jnp.cos(ang)
    sin = jnp.sin(ang)
    return jnp.concatenate([cos, cos, cos, cos], axis=1), jnp.concatenate([-sin, sin, -sin, sin], axis=1)


def _rot_half(x, first_half):
    return jnp.where(first_half, pltpu.roll(x, 96, 1), pltpu.roll(x, 32, 1))


def _first_half_mask(rows):
    lane = lax.broadcasted_iota(jnp.int32, (rows, 128), 1)
    return (lane % 64) < 32


def _log_sigmoid(z):
    return jnp.minimum(z, 0.0) - jnp.log(1.0 + jnp.exp(-jnp.abs(z)))


def _mix_proj(h1, wmixpre, winp, wa2p, bap, cos, sin):
    t = h1.shape[0]
    tm = _row_tile(t)

    def body(h_ref, w_ref, win_ref, wa2_ref, ba_ref, cos_ref, sin_ref,
             n_ref, gq_ref, gk_ref, gv_ref, gg_ref, ga_ref, la_ref, sq_ref, sk_ref, sv_ref):
        y, _, _ = _rms(h_ref[...], w_ref[...])
        n = y.astype(BF16)
        n_ref[...] = n
        proj = _dot(n, win_ref[...])
        gq_ref[...] = proj[:, P_GQ:P_GK]
        gk_ref[...] = proj[:, P_GK:P_GV]
        gv_ref[...] = proj[:, P_GV:P_GG]
        gg_ref[...] = proj[:, P_GG:P_GA]
        ga = proj[:, P_GA:P_SQ]
        ga_ref[...] = ga
        z = _dot(ga.astype(BF16), wa2_ref[...]) + ba_ref[...]
        la_ref[...] = _log_sigmoid(z) * (1.0 / GLA_TAU)
        c = cos_ref[...]
        s = sin_ref[...]
        fh = _first_half_mask(tm)
        for k in range(4):
            x = proj[:, P_SQ + 128 * k:P_SQ + 128 * (k + 1)]
            sq_ref[:, 128 * k:128 * (k + 1)] = (x * c + _rot_half(x, fh) * s).astype(BF16)
        for k in range(2):
            x = proj[:, P_SK + 128 * k:P_SK + 128 * (k + 1)]
            sk_ref[:, 128 * k:128 * (k + 1)] = (x * c + _rot_half(x, fh) * s).astype(BF16)
        sv_ref[...] = proj[:, P_SV:P_END].astype(BF16)

    def row(w):
        return pl.BlockSpec((tm, w), lambda i: (i, 0))

    def rshape(w, dt):
        return jax.ShapeDtypeStruct((t, w), dt)

    return pl.pallas_call(
        body, name="mix_proj", grid=(t // tm,),
        in_specs=[row(D_MODEL), _full((1, D_MODEL)), _full((D_MODEL, P_END)), _full((128, GLA_KW)),
                  _full((1, GLA_KW)), row(128), row(128)],
        out_specs=[row(D_MODEL), row(256), row(256), row(512), row(512), row(128), row(256), row(512), row(256),
                   row(256)],
        out_shape=[rshape(D_MODEL, BF16), rshape(256, F32), rshape(256, F32), rshape(512, F32), rshape(512, F32),
                   rshape(128, F32), rshape(256, F32), rshape(512, BF16), rshape(256, BF16), rshape(256, BF16)],
        compiler_params=_cparams(1),
    )(h1, wmixpre, winp, wa2p, bap, cos, sin)


def _gla_cumsum(la, tril_f):
    b = jnp.dot(tril_f, la, precision=lax.Precision.HIGHEST, preferred_element_type=F32)
    row = lax.broadcasted_iota(jnp.int32, b.shape, 0)
    bm = jnp.sum(jnp.where(row == GLA_CHUNK // 2 - 1, b, 0.0), axis=0, keepdims=True)
    bl = jnp.sum(jnp.where(row == GLA_CHUNK - 1, b, 0.0), axis=0, keepdims=True)
    return b, bm, bl


def _gla_chunk_terms(la, q, k, tril_f):
    b, bm, bl = _gla_cumsum(la, tril_f)
    qs = q * (GLA_DK ** -0.5)
    qt = qs * jnp.exp(b - bm)
    kt = k * jnp.exp(bm - b)
    qh = qs * jnp.exp(b)
    kh = k * jnp.exp(bl - b)
    ebl = jnp.exp(bl)
    return qt, kt, qh, kh, ebl


def _gla_fwd(gq, gk, gv, la):
    t = gq.shape[0]
    nb = t // BLK
    ncb = BLK // GLA_CHUNK

    def body(q_ref, k_ref, v_ref, la_ref, o_ref, ss_ref, st_ref):
        @pl.when(pl.program_id(0) == 0)
        def _():
            st_ref[...] = jnp.zeros_like(st_ref)

        r = lax.broadcasted_iota(jnp.int32, (GLA_CHUNK, GLA_CHUNK), 0)
        c = lax.broadcasted_iota(jnp.int32, (GLA_CHUNK, GLA_CHUNK), 1)
        tril = r >= c
        tril_f = tril.astype(F32)
        lane = lax.broadcasted_iota(jnp.int32, (GLA_CHUNK, 128), 1)
        halves = (lane < 64, lane >= 64)
        for ch in range(ncb):
            rows = slice(ch * GLA_CHUNK, (ch + 1) * GLA_CHUNK)
            qt, kt, qh, kh, ebl = _gla_chunk_terms(la_ref[rows, :], q_ref[rows, :], k_ref[rows, :], tril_f)
            for hp in range(2):
                ls = slice(128 * hp, 128 * (hp + 1))
                kt2 = kt[:, ls].astype(BF16)
                kh2 = kh[:, ls].astype(BF16)
                for e in range(2):
                    h = 2 * hp + e
                    vs = slice(GLA_DV * h, GLA_DV * (h + 1))
                    st = st_ref[h]
                    ss_ref[ch, h] = st
                    v = v_ref[rows, vs].astype(BF16)
                    qtm = jnp.where(halves[e], qt[:, ls], 0.0).astype(BF16)
                    qhm = jnp.where(halves[e], qh[:, ls], 0.0).astype(BF16)
                    a = jnp.where(tril, _dg(qtm, kt2, NT), 0.0)
                    o_ref[rows, vs] = _dot(a.astype(BF16), v) + _dg(qhm, st.astype(BF16), NT)
                    st_ref[h] = st * ebl[:, ls] + _dg(v, kh2, TN)

    def row(w):
        return pl.BlockSpec((BLK, w), lambda i: (i, 0))

    return pl.pallas_call(
        body, name="gla_fwd", grid=(nb,),
        in_specs=[row(256), row(256), row(512), row(256)],
        out_specs=[row(512), pl.BlockSpec((ncb, GLA_HEADS, GLA_DV, 128), lambda i: (i, 0, 0, 0))],
        out_shape=[jax.ShapeDtypeStruct((t, GLA_W), F32),
                   jax.ShapeDtypeStruct((nb * ncb, GLA_HEADS, GLA_DV, 128), F32)],
        scratch_shapes=[pltpu.VMEM((GLA_HEADS, GLA_DV, 128), F32)],
        compiler_params=_cparams(1),
    )(gq, gk, gv, la)


def _gla_bwd(gq, gk, gv, la, ss, do):
    t = gq.shape[0]
    nb = t // BLK
    ncb = BLK // GLA_CHUNK

    def body(q_ref, k_ref, v_ref, la_ref, ss_ref, do_ref, dq_ref, dk_ref, dv_ref, dla_ref, dst_ref):
        @pl.when(pl.program_id(0) == 0)
        def _():
            dst_ref[...] = jnp.zeros_like(dst_ref)

        r = lax.broadcasted_iota(jnp.int32, (GLA_CHUNK, GLA_CHUNK), 0)
        c = lax.broadcasted_iota(jnp.int32, (GLA_CHUNK, GLA_CHUNK), 1)
        tril = r >= c
        tril_f = tril.astype(F32)
        triu_f = (r <= c).astype(F32)
        lane = lax.broadcasted_iota(jnp.int32, (GLA_CHUNK, 128), 1)
        halves = (lane < 64, lane >= 64)
        last_row = lax.broadcasted_iota(jnp.int32, (GLA_CHUNK, 128), 0) == GLA_CHUNK - 1
        lane1 = lax.broadcasted_iota(jnp.int32, (1, 128), 1)
        halves1 = (lane1 < 64, lane1 >= 64)
        scale = GLA_DK ** -0.5
        for ch in reversed(range(ncb)):
            rows = slice(ch * GLA_CHUNK, (ch + 1) * GLA_CHUNK)
            la = la_ref[rows, :]
            b, bm, bl = _gla_cumsum(la, tril_f)
            eq = jnp.exp(b - bm)
            ek = jnp.exp(bm - b)
            eb = jnp.exp(b)
            ekl = jnp.exp(bl - b)
            ebl = jnp.exp(bl)
            qs = q_ref[rows, :] * scale
            kk = k_ref[rows, :]
            qt, kt, qh, kh = qs * eq, kk * ek, qs * eb, kk * ekl
            for hp in range(2):
                ls = slice(128 * hp, 128 * (hp + 1))
                kt2 = kt[:, ls].astype(BF16)
                kh2 = kh[:, ls].astype(BF16)
                dqt = jnp.zeros((GLA_CHUNK, 128), F32)
                dkt = jnp.zeros((GLA_CHUNK, 128), F32)
                dqh = jnp.zeros((GLA_CHUNK, 128), F32)
                dkh = jnp.zeros((GLA_CHUNK, 128), F32)
                dbl = jnp.zeros((1, 128), F32)
                for e in range(2):
                    h = 2 * hp + e
                    vs = slice(GLA_DV * h, GLA_DV * (h + 1))
                    m = halves[e]
                    st = ss_ref[ch, h]
                    dstn = dst_ref[h]
                    v = v_ref[rows, vs].astype(BF16)
                    dov = do_ref[rows, vs].astype(BF16)
                    qtm = jnp.where(m, qt[:, ls], 0.0).astype(BF16)
                    qhm = jnp.where(m, qh[:, ls], 0.0).astype(BF16)
                    khm = jnp.where(m, kh[:, ls], 0.0).astype(BF16)
                    a = jnp.where(tril, _dg(qtm, kt2, NT), 0.0).astype(BF16)
                    da = jnp.where(tril, _dg(dov, v, NT), 0.0).astype(BF16)
                    dstn_b = dstn.astype(BF16)
                    dv_ref[rows, vs] = _dg(a, dov, TN) + _dg(khm, dstn_b, NT)
                    dqt = dqt + jnp.where(m, _dot(da, kt2), 0.0)
                    dkt = dkt + _dg(da, qtm, TN)
                    dqh = dqh + jnp.where(m, _dot(dov, st.astype(BF16)), 0.0)
                    dkh = dkh + jnp.where(m, _dot(v, dstn_b), 0.0)
                    dbl = dbl + jnp.where(halves1[e], jnp.sum(dstn * st, axis=0, keepdims=True), 0.0)
                    dst_ref[h] = dstn * ebl[:, ls] + _dg(dov, qhm, TN)
                dq_ref[rows, ls] = scale * (dqt * eq[:, ls] + dqh * eb[:, ls])
                dk_ref[rows, ls] = dkt * ek[:, ls] + dkh * ekl[:, ls]
                dkk = dkh * kh[:, ls]
                db = dqt * qt[:, ls] - dkt * kt[:, ls] + dqh * qh[:, ls] - dkk
                db_last = jnp.sum(dkk, axis=0, keepdims=True) + ebl[:, ls] * dbl
                db = db + jnp.where(last_row, db_last, 0.0)
                dla_ref[rows, ls] = jnp.dot(triu_f, db, precision=lax.Precision.HIGHEST,
                                            preferred_element_type=F32)

    def row(w):
        return pl.BlockSpec((BLK, w), lambda i: (nb - 1 - i, 0))

    def rshape(w):
        return jax.ShapeDtypeStruct((t, w), F32)

    return pl.pallas_call(
        body, name="gla_bwd", grid=(nb,),
        in_specs=[row(256), row(256), row(512), row(256),
                  pl.BlockSpec((ncb, GLA_HEADS, GLA_DV, 128), lambda i: (nb - 1 - i, 0, 0, 0)), row(512)],
        out_specs=[row(256), row(256), row(512), row(256)],
        out_shape=[rshape(256), rshape(256), rshape(512), rshape(256)],
        scratch_shapes=[pltpu.VMEM((GLA_HEADS, GLA_DV, 128), F32)],
        compiler_params=_cparams(1),
    )(gq, gk, gv, la, ss, do)


def _swa_mask(n):
    r = lax.broadcasted_iota(jnp.int32, (BLK, 3 * BLK), 0)
    c = lax.broadcasted_iota(jnp.int32, (BLK, 3 * BLK), 1)
    seg = c // BLK
    cc = c % BLK
    qpos = n * BLK + r - PAD
    kpos = jnp.where(seg == 0, (n - 1) * BLK, jnp.where(seg == 1, n * BLK, 0)) + cc - PAD
    band = (seg < 2) & (kpos >= N_META) & (kpos <= qpos) & (qpos - kpos < WINDOW)
    meta = (seg == 2) & (kpos >= 0) & (kpos < N_META) & (kpos <= qpos)
    return band | meta


def _swa_probs(qm, kall, mask, sink):
    s = _dg(qm, kall, NT) * (SWA_HD ** -0.5)
    s = jnp.where(mask, s, NEG_INF)
    m = jnp.maximum(jnp.max(s, axis=-1, keepdims=True), sink)
    p = jnp.exp(s - m)
    es = jnp.exp(sink - m)
    inv = 1.0 / (jnp.sum(p, axis=-1, keepdims=True) + es)
    return p * inv, es * inv


def _swa_fwd(sinks, sq, sk, sv):
    t = sq.shape[0]
    nb = t // BLK

    def body(sink_ref, q_ref, kp_ref, kc_ref, km_ref, vp_ref, vc_ref, vm_ref, o_ref):
        n = pl.program_id(0)
        mask = _swa_mask(n)
        lo = lax.broadcasted_iota(jnp.int32, (BLK, 128), 1) < 64
        for kh in range(SWA_KVH):
            ls = slice(128 * kh, 128 * (kh + 1))
            kall = jnp.concatenate([kp_ref[:, ls], kc_ref[:, ls], km_ref[:, ls]], axis=0)
            vall = jnp.concatenate([vp_ref[:, ls], vc_ref[:, ls], vm_ref[:, ls]], axis=0)
            for g in range(2):
                ps = slice(128 * (2 * kh + g), 128 * (2 * kh + g + 1))
                qp = q_ref[:, ps]
                outs = []
                for e in range(2):
                    h = 4 * kh + 2 * g + e
                    qm = jnp.where(lo if e == 0 else ~lo, qp, jnp.zeros_like(qp))
                    p, _ = _swa_probs(qm, kall, mask, sink_ref[h])
                    outs.append(_dot(p.astype(BF16), vall))
                o_ref[:, ps] = jnp.where(lo, outs[0], outs[1])

    cur = lambda w: pl.BlockSpec((BLK, w), lambda i: (i, 0))
    prev = lambda w: pl.BlockSpec((BLK, w), lambda i: (jnp.maximum(i - 1, 0), 0))
    first = lambda w: pl.BlockSpec((BLK, w), lambda i: (0, 0))
    return pl.pallas_call(
        body, name="swa_fwd", grid=(nb,),
        in_specs=[pl.BlockSpec(memory_space=pltpu.SMEM), cur(512), prev(256), cur(256), first(256),
                  prev(256), cur(256), first(256)],
        out_specs=cur(512),
        out_shape=jax.ShapeDtypeStruct((t, SWA_W), F32),
        compiler_params=_cparams(1),
    )(sinks, sq, sk, sk, sk, sv, sv, sv)


def _swa_bwd(sinks, sq, sk, sv, o, do, hook=None):
    t = sq.shape[0]
    nb = t // BLK

    def body(sink_ref, q_ref, kp_ref, kc_ref, km_ref, vp_ref, vc_ref, vm_ref, o_ref, do_ref,
             dq_ref, dk_ref, dv_ref, dkm_ref, dvm_ref, dsink_ref, ck_ref, cv_ref):
        n = pl.program_id(0)

        @pl.when(n == 0)
        def _():
            ck_ref[...] = jnp.zeros_like(ck_ref)
            cv_ref[...] = jnp.zeros_like(cv_ref)
            dkm_ref[...] = jnp.zeros_like(dkm_ref)
            dvm_ref[...] = jnp.zeros_like(dvm_ref)
            dsink_ref[...] = jnp.zeros_like(dsink_ref)

        @pl.when(n == nb)
        def _():
            dk_ref[...] = ck_ref[...]
            dv_ref[...] = cv_ref[...]

        @pl.when(n < nb)
        def _():
            mask = _swa_mask(n)
            lo = lax.broadcasted_iota(jnp.int32, (BLK, 128), 1) < 64
            scale = SWA_HD ** -0.5
            for kh in range(SWA_KVH):
                ls = slice(128 * kh, 128 * (kh + 1))
                kall = jnp.concatenate([kp_ref[:, ls], kc_ref[:, ls], km_ref[:, ls]], axis=0)
                vall = jnp.concatenate([vp_ref[:, ls], vc_ref[:, ls], vm_ref[:, ls]], axis=0)
                dkall = jnp.zeros((3 * BLK, 128), F32)
                dvall = jnp.zeros((3 * BLK, 128), F32)
                for g in range(2):
                    ps = slice(128 * (2 * kh + g), 128 * (2 * kh + g + 1))
                    qp = q_ref[:, ps]
                    dop = do_ref[:, ps]
                    op = o_ref[:, ps]
                    dqs = []
                    for e in range(2):
                        h = 4 * kh + 2 * g + e
                        half = lo if e == 0 else ~lo
                        qm = jnp.where(half, qp, jnp.zeros_like(qp))
                        dom = jnp.where(half, dop, 0.0)
                        p, psink = _swa_probs(qm, kall, mask, sink_ref[h])
                        delta = jnp.sum(dom * op, axis=-1, keepdims=True)
                        domb = dom.astype(BF16)
                        dp = _dg(domb, vall, NT)
                        ds = (p * (dp - delta) * scale).astype(BF16)
                        dqs.append(_dot(ds, kall))
                        dkall = dkall + _dg(ds, qm, TN)
                        dvall = dvall + _dg(p.astype(BF16), domb, TN)
                        dsink_ref[h:h + 1, :] += jnp.broadcast_to(-jnp.sum(psink * delta, axis=0, keepdims=True),
                                                                 (1, 128))
                    dq_ref[:, ps] = jnp.where(lo, dqs[0], dqs[1])
                dk_ref[:, ls] = ck_ref[:, ls] + dkall[0:BLK]
                dv_ref[:, ls] = cv_ref[:, ls] + dvall[0:BLK]
                ck_ref[:, ls] = dkall[BLK:2 * BLK]
                cv_ref[:, ls] = dvall[BLK:2 * BLK]
                dkm_ref[:, ls] += dkall[2 * BLK:3 * BLK]
                dvm_ref[:, ls] += dvall[2 * BLK:3 * BLK]

    cur = lambda w: pl.BlockSpec((BLK, w), lambda i: (jnp.minimum(i, nb - 1), 0))
    prev = lambda w: pl.BlockSpec((BLK, w), lambda i: (jnp.maximum(i - 1, 0), 0))
    first = lambda w: pl.BlockSpec((BLK, w), lambda i: (0, 0))
    return _pallas(
        body, name="swa_bwd", grid=(nb + 1,),
        in_specs=[pl.BlockSpec(memory_space=pltpu.SMEM), cur(512), prev(256), cur(256), first(256),
                  prev(256), cur(256), first(256), cur(512), cur(512)],
        out_specs=[cur(512), prev(256), prev(256), first(256), first(256), _full((SWA_QH, 128))],
        out_shape=[jax.ShapeDtypeStruct((t, SWA_W), F32), jax.ShapeDtypeStruct((t, 256), F32),
                   jax.ShapeDtypeStruct((t, 256), F32), jax.ShapeDtypeStruct((BLK, 256), F32),
                   jax.ShapeDtypeStruct((BLK, 256), F32), jax.ShapeDtypeStruct((SWA_QH, 128), F32)],
        scratch_shapes=[pltpu.VMEM((BLK, 256), F32), pltpu.VMEM((BLK, 256), F32)],
        args=(sinks, sq, sk, sk, sk, sv, sv, sv, o, do), hook=hook)


def _mix_out(h1, ogla, gg, oswa, wgn, wsn, wout, wpost):
    t = h1.shape[0]
    tm = _row_tile(t)

    def body(h_ref, og_ref, gg_ref, os_ref, wgn_ref, wsn_ref, wout_ref, wpost_ref, h2_ref, cat_ref, m_ref):
        parts = []
        for h in range(GLA_HEADS):
            ls = slice(GLA_DV * h, GLA_DV * (h + 1))
            y, _, _ = _rms(og_ref[:, ls], wgn_ref[...])
            g = gg_ref[:, ls]
            parts.append(y * (g * _sigmoid(g)))
        ys, _, _ = _rms(os_ref[...], wsn_ref[...])
        cat = jnp.concatenate(parts + [ys], axis=1).astype(BF16)
        cat_ref[...] = cat
        m = _dot(cat, wout_ref[...])
        m_ref[...] = m
        y, _, _ = _rms(m, wpost_ref[...])
        h2_ref[...] = h_ref[...] + y

    def row(w):
        return pl.BlockSpec((tm, w), lambda i: (i, 0))

    return pl.pallas_call(
        body, name="mix_out", grid=(t // tm,),
        in_specs=[row(D_MODEL), row(512), row(512), row(512), _full((1, GLA_DV)), _full((1, SWA_W)),
                  _full((D_MODEL, D_MODEL)), _full((1, D_MODEL))],
        out_specs=[row(D_MODEL), row(D_MODEL), row(D_MODEL)],
        out_shape=[jax.ShapeDtypeStruct((t, D_MODEL), F32), jax.ShapeDtypeStruct((t, D_MODEL), BF16),
                   jax.ShapeDtypeStruct((t, D_MODEL), F32)],
        compiler_params=_cparams(1),
    )(h1, ogla, gg, oswa, wgn, wsn, wout, wpost)


def _mix_out_bwd(dh2, m, ogla, gg, oswa, wgn, wsn, wout, wpost, hook=None):
    t = dh2.shape[0]
    tm = _row_tile(t)

    def body(dh_ref, m_ref, og_ref, gg_ref, os_ref, wgn_ref, wsn_ref, wout_ref, wpost_ref,
             dog_ref, dgg_ref, dos_ref, dm_ref, dwpost_ref, dwgn_ref, dwsn_ref):
        @pl.when(pl.program_id(0) == 0)
        def _():
            dwpost_ref[...] = jnp.zeros_like(dwpost_ref)
            dwgn_ref[...] = jnp.zeros_like(dwgn_ref)
            dwsn_ref[...] = jnp.zeros_like(dwsn_ref)

        wpost = wpost_ref[...]
        _, mh, r = _rms(m_ref[...], wpost)
        dm, dw = _rms_bwd(mh, r, wpost, dh_ref[...])
        dwpost_ref[...] += dw
        dmb = dm.astype(BF16)
        dm_ref[...] = dmb
        dcat = _dg(dmb, wout_ref[...], NT)
        wgn = wgn_ref[...]
        for h in range(GLA_HEADS):
            ls = slice(GLA_DV * h, GLA_DV * (h + 1))
            dog = dcat[:, ls]
            g = gg_ref[:, ls]
            sg = _sigmoid(g)
            y, xh, r = _rms(og_ref[:, ls], wgn)
            dgg_ref[:, ls] = dog * y * (sg * (1.0 + g * (1.0 - sg)))
            dx, dw = _rms_bwd(xh, r, wgn, dog * (g * sg))
            dog_ref[:, ls] = dx
            dwgn_ref[...] += dw
        wsn = wsn_ref[...]
        _, xh, r = _rms(os_ref[...], wsn)
        dx, dw = _rms_bwd(xh, r, wsn, dcat[:, GLA_W:])
        dos_ref[...] = dx
        dwsn_ref[...] += dw

    def row(w):
        return pl.BlockSpec((tm, w), lambda i: (i, 0))

    def rshape(w, dt=F32):
        return jax.ShapeDtypeStruct((t, w), dt)

    return _pallas(
        body, name="mix_out_bwd", grid=(t // tm,),
        in_specs=[row(D_MODEL), row(D_MODEL), row(512), row(512), row(512), _full((1, GLA_DV)), _full((1, SWA_W)),
                  _full((D_MODEL, D_MODEL)), _full((1, D_MODEL))],
        out_specs=[row(512), row(512), row(512), row(D_MODEL), _full((1, D_MODEL)), _full((1, GLA_DV)),
                   _full((1, SWA_W))],
        out_shape=[rshape(512), rshape(512), rshape(512), rshape(D_MODEL, BF16),
                   jax.ShapeDtypeStruct((1, D_MODEL), F32), jax.ShapeDtypeStruct((1, GLA_DV), F32),
                   jax.ShapeDtypeStruct((1, SWA_W), F32)],
        args=(dh2, m, ogla, gg, oswa, wgn, wsn, wout, wpost), hook=hook)


def _mix_in_bwd(dh2, h1, wmixpre, winp, wa2p, bap, cos, sin, ga, dgq, dgk, dgv, dgg, dla, dsq, dsk, dsv, dkm, dvm):
    t = h1.shape[0]
    tm = _row_tile(t)

    def body(dh2_ref, h_ref, w_ref, win_ref, wa2_ref, ba_ref, cos_ref, sin_ref, ga_ref, dgq_ref, dgk_ref, dgv_ref,
             dgg_ref, dla_ref, dsq_ref, dsk_ref, dsv_ref, dkm_ref, dvm_ref,
             dh1_ref, dproj_ref, dw_ref, dwa2_ref, dba_ref):
        i = pl.program_id(0)

        @pl.when(i == 0)
        def _():
            dw_ref[...] = jnp.zeros_like(dw_ref)
            dwa2_ref[...] = jnp.zeros_like(dwa2_ref)
            dba_ref[...] = jnp.zeros_like(dba_ref)

        first = (i == 0).astype(F32)
        c = cos_ref[...]
        s = -sin_ref[...]
        fh = _first_half_mask(tm)
        dproj_ref[:, P_GQ:P_GK] = dgq_ref[...].astype(BF16)
        dproj_ref[:, P_GK:P_GV] = dgk_ref[...].astype(BF16)
        dproj_ref[:, P_GV:P_GG] = dgv_ref[...].astype(BF16)
        dproj_ref[:, P_GG:P_GA] = dgg_ref[...].astype(BF16)
        gab = ga_ref[...].astype(BF16)
        z = _dot(gab, wa2_ref[...]) + ba_ref[...]
        row_id = i * tm + lax.broadcasted_iota(jnp.int32, (tm, 1), 0)
        dz = jnp.where(row_id >= PAD, dla_ref[...] * (1.0 / GLA_TAU) * (1.0 - _sigmoid(z)), 0.0)
        dzb = dz.astype(BF16)
        dba_ref[...] += jnp.sum(dz, axis=0, keepdims=True)
        dwa2_ref[...] += _dg(gab, dzb, TN)
        dproj_ref[:, P_GA:P_SQ] = _dg(dzb, wa2_ref[...], NT).astype(BF16)
        for k in range(4):
            dy = dsq_ref[:, 128 * k:128 * (k + 1)]
            dproj_ref[:, P_SQ + 128 * k:P_SQ + 128 * (k + 1)] = (dy * c + _rot_half(dy, fh) * s).astype(BF16)
        for k in range(2):
            ls = slice(128 * k, 128 * (k + 1))
            dy = dsk_ref[:, ls]
            dy = jnp.concatenate([dy[:BLK] + first * dkm_ref[:, ls], dy[BLK:]], axis=0) if tm > BLK else (
                dy + first * dkm_ref[:, ls])
            dproj_ref[:, P_SK + 128 * k:P_SK + 128 * (k + 1)] = (dy * c + _rot_half(dy, fh) * s).astype(BF16)
            dv = dsv_ref[:, ls]
            dv = jnp.concatenate([dv[:BLK] + first * dvm_ref[:, ls], dv[BLK:]], axis=0) if tm > BLK else (
                dv + first * dvm_ref[:, ls])
            dproj_ref[:, P_SV + 128 * k:P_SV + 128 * (k + 1)] = dv.astype(BF16)
        dn = _dg(dproj_ref[...], win_ref[...], NT)
        w = w_ref[...]
        _, hh, r = _rms(h_ref[...], w)
        dx, dw = _rms_bwd(hh, r, w, dn)
        dw_ref[...] += dw
        dh1_ref[...] = dh2_ref[...] + dx

    def row(w):
        return pl.BlockSpec((tm, w), lambda i: (i, 0))

    return pl.pallas_call(
        body, name="mix_in_bwd", grid=(t // tm,),
        in_specs=[row(D_MODEL), row(D_MODEL), _full((1, D_MODEL)), _full((D_MODEL, P_END)), _full((128, GLA_KW)),
                  _full((1, GLA_KW)), row(128), row(128), row(128), row(256), row(256), row(512), row(512), row(256),
                  row(512), row(256), row(256), _full((BLK, 256)), _full((BLK, 256))],
        out_specs=[row(D_MODEL), row(P_END), _full((1, D_MODEL)), _full((128, GLA_KW)), _full((1, GLA_KW))],
        out_shape=[jax.ShapeDtypeStruct((t, D_MODEL), F32), jax.ShapeDtypeStruct((t, P_END), BF16),
                   jax.ShapeDtypeStruct((1, D_MODEL), F32), jax.ShapeDtypeStruct((128, GLA_KW), F32),
                   jax.ShapeDtypeStruct((1, GLA_KW), F32)],
        compiler_params=_cparams(1),
    )(dh2, h1, wmixpre, winp, wa2p, bap, cos, sin, ga, dgq, dgk, dgv, dgg, dla, dsq, dsk, dsv, dkm, dvm)


def _loss_head(h3, target):
    t = h3.shape[0]
    nb = t // BLK

    def body(h_ref, t_ref, dy_ref, loss_ref):
        n = pl.program_id(0)

        @pl.when(n == 0)
        def _():
            loss_ref[...] = jnp.zeros_like(loss_ref)
            dy_ref[...] = jnp.zeros_like(dy_ref)

        @pl.when(n > 0)
        def _():
            err = h_ref[...] - t_ref[...]
            dy_ref[...] = err * (1.0 / D_MODEL)
            part = jnp.sum(jnp.sum(err * err, axis=1, keepdims=True), axis=0, keepdims=True)
            loss_ref[...] += jnp.broadcast_to(part, (1, 128))

    return pl.pallas_call(
        body, name="loss_head", grid=(nb,),
        in_specs=[pl.BlockSpec((BLK, D_MODEL), lambda i: (i, 0)),
                  pl.BlockSpec((BLK, D_MODEL), lambda i: (jnp.maximum(i - 1, 0), 0))],
        out_specs=[pl.BlockSpec((BLK, D_MODEL), lambda i: (i, 0)), _full((1, 128))],
        out_shape=[jax.ShapeDtypeStruct((t, D_MODEL), F32), jax.ShapeDtypeStruct((1, 128), F32)],
        compiler_params=_cparams(1),
    )(h3, target)


def _adamw_update(w, g, m, v):
    m = ADAM_B1 * m + (1.0 - ADAM_B1) * g
    v = ADAM_B2 * v + (1.0 - ADAM_B2) * (g * g)
    m_hat = m / (1.0 - ADAM_B1 ** ADAM_STEP)
    v_hat = v / (1.0 - ADAM_B2 ** ADAM_STEP)
    return -ADAM_LR * (m_hat / (jnp.sqrt(v_hat) + ADAM_EPS) + ADAM_WD * w), m, v


def _adamw(w, g, m, v):
    r, c = w.shape
    tr = _div_tile(r)

    def body(w_ref, g_ref, m_ref, v_ref, d_ref, nm_ref, nv_ref):
        d_ref[...], nm_ref[...], nv_ref[...] = _adamw_update(w_ref[...], g_ref[...], m_ref[...], v_ref[...])

    spec = pl.BlockSpec((tr, c), lambda i: (i, 0))
    shape = jax.ShapeDtypeStruct((r, c), F32)
    return pl.pallas_call(
        body, name="adamw", grid=(r // tr,), in_specs=[spec] * 4, out_specs=[spec] * 3, out_shape=[shape] * 3,
        compiler_params=_cparams(1),
    )(w, g, m, v)


def _adamw_halves(w, g_mine, g_other, m, v, c_idx, half=None):
    r, c = w.shape
    tr = _div_tile(r // 2 if half is None else r)
    nth = (r // 2) // tr if half is None else None

    def body(c_ref, w_ref, gm_ref, go_ref, m_ref, v_ref, g_ref, d_ref, nm_ref, nv_ref):
        hh = pl.program_id(0) // nth if half is None else half
        g = jnp.where(hh == c_ref[0], gm_ref[...], go_ref[...])
        g_ref[...] = g
        d_ref[...], nm_ref[...], nv_ref[...] = _adamw_update(w_ref[...], g, m_ref[...], v_ref[...])

    spec = pl.BlockSpec((tr, c), lambda i, c_ref: (i, 0))
    gspec = pl.BlockSpec((tr, c), (lambda i, c_ref: (i % nth, 0)) if half is None else (lambda i, c_ref: (i, 0)))
    shape = jax.ShapeDtypeStruct((r, c), F32)
    return pl.pallas_call(
        body, name="adamw_halves",
        grid_spec=pltpu.PrefetchScalarGridSpec(
            num_scalar_prefetch=1, grid=(r // tr,), in_specs=[spec, gspec, gspec, spec, spec], out_specs=[spec] * 4),
        out_shape=[shape] * 4, compiler_params=_cparams(1),
    )(c_idx, w, g_mine, g_other, m, v)


def _place():
    x, y, c = lax.axis_index("x"), lax.axis_index("y"), lax.axis_index("c")
    chips = [(1 - x, y), (x, 1 - y), (1 - x, 1 - y)]
    return x, y, c, chips


def _remote(send_sem, recv_sem, src, dst, to):
    return pltpu.make_async_remote_copy(src_ref=src, dst_ref=dst, send_sem=send_sem, recv_sem=recv_sem,
                                        device_id=to, device_id_type=MESH)


def _half(ref_rows, c):
    h = ref_rows // 2
    return pl.ds(pl.multiple_of(c * h, 8), h)


def _own_slot(shard, q):
    return lax.dynamic_update_slice(jnp.zeros((N_CHIPS,) + shard.shape, shard.dtype), shard[None], (q, 0, 0))


class _GatherChips:
    has_mid = True

    def __init__(self, bufs):
        n = len(bufs)
        self.inputs = list(bufs)
        self.out_shape = [jax.ShapeDtypeStruct(b.shape, b.dtype) for b in bufs]
        self.aliases = [(t, t) for t in range(n)]
        self.scratch = [pltpu.SemaphoreType.DMA((n, 6)), pltpu.SemaphoreType.DMA((n, 6))]

    def start(self, ins, outs, scr):
        send, recv = scr
        x, y, c, chips = _place()
        q = 2 * x + y
        for t, (i_ref, o_ref) in enumerate(zip(ins, outs)):
            rows = _half(i_ref.shape[1], c)
            for j, (cx, cy) in enumerate(chips):
                _remote(send.at[t, j], recv.at[t, j], i_ref.at[q, rows], o_ref.at[q, rows], (cx, cy, c)).start()

    def mid(self, ins, outs, scr):
        send, recv = scr
        x, y, c, chips = _place()
        for t, o_ref in enumerate(outs):
            rows = _half(o_ref.shape[1], c)
            for j, (cx, cy) in enumerate(chips):
                slot = o_ref.at[2 * cx + cy, rows]
                _remote(send.at[t, j], recv.at[t, j], slot, slot, (cx, cy, c)).wait_recv()
                _remote(send.at[t, 3 + j], recv.at[t, 3 + j], slot, slot, (x, y, 1 - c)).start()

    def finish(self, ins, outs, scr):
        send, recv = scr
        x, y, c, chips = _place()
        for t, o_ref in enumerate(outs):
            mine, other = _half(o_ref.shape[1], c), _half(o_ref.shape[1], 1 - c)
            for j, (cx, cy) in enumerate(chips):
                slot = o_ref.at[2 * cx + cy, other]
                _remote(send.at[t, 3 + j], recv.at[t, 3 + j], slot, slot, (x, y, 1 - c)).wait_recv()
            for j, (cx, cy) in enumerate(chips):
                sent = o_ref.at[2 * cx + cy, mine]
                _remote(send.at[t, j], recv.at[t, j], sent, sent, (cx, cy, c)).wait_send()
                _remote(send.at[t, 3 + j], recv.at[t, 3 + j], sent, sent, (x, y, 1 - c)).wait_send()


class _PairExchange:
    has_mid = False
    aliases = ()

    def __init__(self, arrs):
        n = len(arrs)
        self.inputs = list(arrs)
        self.out_shape = [jax.ShapeDtypeStruct((a.shape[0], a.shape[1] // 2, a.shape[2]), a.dtype) for a in arrs]
        self.scratch = [pltpu.SemaphoreType.DMA((n,)), pltpu.SemaphoreType.DMA((n,))]

    def _copies(self, ins, outs, scr):
        send, recv = scr
        x, y, c, _ = _place()
        return [_remote(send.at[t], recv.at[t], i_ref.at[:, _half(i_ref.shape[1], 1 - c)], o_ref, (x, y, 1 - c))
                for t, (i_ref, o_ref) in enumerate(zip(ins, outs))]

    def start(self, ins, outs, scr):
        for cp in self._copies(ins, outs, scr):
            cp.start()

    def finish(self, ins, outs, scr):
        for cp in self._copies(ins, outs, scr):
            cp.wait()


class _ChipScatter:
    has_mid = False
    aliases = ()

    def __init__(self, arrs):
        n = len(arrs)
        self.inputs = list(arrs)
        self.out_shape = [jax.ShapeDtypeStruct((3,) + a.shape[1:], a.dtype) for a in arrs]
        self.scratch = [pltpu.SemaphoreType.DMA((n, 3)), pltpu.SemaphoreType.DMA((n, 3))]

    def _copies(self, ins, outs, scr):
        send, recv = scr
        x, y, c, chips = _place()
        return [_remote(send.at[t, j], recv.at[t, j], i_ref.at[2 * cx + cy], o_ref.at[j], (cx, cy, c))
                for t, (i_ref, o_ref) in enumerate(zip(ins, outs)) for j, (cx, cy) in enumerate(chips)]

    def start(self, ins, outs, scr):
        for cp in self._copies(ins, outs, scr):
            cp.start()

    def finish(self, ins, outs, scr):
        for cp in self._copies(ins, outs, scr):
            cp.wait()


class _PairShare:
    has_mid = False
    aliases = ()

    def __init__(self, arrs):
        n = len(arrs)
        self.inputs = list(arrs)
        self.out_shape = [jax.ShapeDtypeStruct(a.shape, a.dtype) for a in arrs]
        self.scratch = [pltpu.SemaphoreType.DMA((n,)), pltpu.SemaphoreType.DMA((n,))]

    def _copies(self, ins, outs, scr):
        send, recv = scr
        x, y, c, _ = _place()
        return [_remote(send.at[t], recv.at[t], i_ref, o_ref, (x, y, 1 - c))
                for t, (i_ref, o_ref) in enumerate(zip(ins, outs))]

    def start(self, ins, outs, scr):
        for cp in self._copies(ins, outs, scr):
            cp.start()

    def finish(self, ins, outs, scr):
        for cp in self._copies(ins, outs, scr):
            cp.wait()


def _comm_call(hook, name):
    n_in, n_out = len(hook.inputs), len(hook.out_shape)

    def body(*refs):
        ins, outs, scr = refs[:n_in], refs[n_in:n_in + n_out], refs[n_in + n_out:]
        hook.start(ins, outs, scr)
        if hook.has_mid:
            hook.mid(ins, outs, scr)
        hook.finish(ins, outs, scr)

    return pl.pallas_call(body, name=name, in_specs=[ANY] * n_in, out_specs=[ANY] * n_out,
                          out_shape=list(hook.out_shape), scratch_shapes=list(hook.scratch),
                          input_output_aliases=dict(hook.aliases))(*hook.inputs)


def _all_gather_devices(vec):
    r, w = vec.shape

    def body(x_ref, out_ref, send_sems, recv_sems, local_sem):
        x, y, c, chips = _place()
        me, sibling = (x, y, c), (x, y, 1 - c)

        def rows(px, py, pc):
            return out_ref.at[4 * px + 2 * py + pc]

        def copy(k, block, to, src=None):
            return pltpu.make_async_remote_copy(
                src_ref=rows(*block) if src is None else src, dst_ref=rows(*block), send_sem=send_sems.at[k],
                recv_sem=recv_sems.at[k], device_id=to, device_id_type=MESH)

        mine = pltpu.make_async_copy(x_ref, rows(*me), local_sem)
        mine.start()
        first = [copy(0, me, sibling, src=x_ref)]
        first += [copy(1 + j, me, (*chip, c), src=x_ref) for j, chip in enumerate(chips)]
        for cp in first:
            cp.start()
        passed = [copy(4 + j, (*chip, c), sibling) for j, chip in enumerate(chips)]
        for j, chip in enumerate(chips):
            copy(1 + j, (*chip, c), me).wait_recv()
            passed[j].start()
        copy(0, sibling, me).wait_recv()
        for j, chip in enumerate(chips):
            copy(4 + j, (*chip, 1 - c), me).wait_recv()
        for cp in first + passed:
            cp.wait_send()
        mine.wait()

    return pl.pallas_call(
        body, name="all_gather_devices",
        in_specs=[pl.BlockSpec(memory_space=pltpu.VMEM)], out_specs=pl.BlockSpec(memory_space=pltpu.VMEM),
        out_shape=jax.ShapeDtypeStruct((N_DEV, r, w), vec.dtype),
        scratch_shapes=[pltpu.SemaphoreType.DMA((7,)), pltpu.SemaphoreType.DMA((7,)), pltpu.SemaphoreType.DMA],
    )(vec)


def _pair_sum(g, other, c_idx):
    nq, r, w = g.shape
    h = r // 2
    tr = _div_tile(h)
    nt = h // tr

    def body(c_ref, g_ref, o_ref, s_ref):
        s_ref[...] = (g_ref[...].astype(F32) + o_ref[...].astype(F32)).astype(s_ref.dtype)

    return pl.pallas_call(
        body, name="pair_sum",
        grid_spec=pltpu.PrefetchScalarGridSpec(
            num_scalar_prefetch=1, grid=(nq, nt),
            in_specs=[pl.BlockSpec((None, tr, w), lambda k, i, c_ref: (k, c_ref[0] * nt + i, 0)),
                      pl.BlockSpec((None, tr, w), lambda k, i, c_ref: (k, i, 0))],
            out_specs=pl.BlockSpec((None, tr, w), lambda k, i, c_ref: (k, i, 0))),
        out_shape=jax.ShapeDtypeStruct((nq, h, w), g.dtype),
        compiler_params=_cparams(2),
    )(c_idx, g, other)


def _chip_sum(s, others, q_idx):
    _, h, w = s.shape
    tr = _div_tile(h)

    def body(q_ref, s_ref, o_ref, out_ref):
        out_ref[...] = ((s_ref[...].astype(F32) + o_ref[0].astype(F32)) + o_ref[1].astype(F32)) + o_ref[2].astype(F32)

    return pl.pallas_call(
        body, name="chip_sum",
        grid_spec=pltpu.PrefetchScalarGridSpec(
            num_scalar_prefetch=1, grid=(h // tr,),
            in_specs=[pl.BlockSpec((None, tr, w), lambda i, q_ref: (q_ref[0], i, 0)),
                      pl.BlockSpec((3, tr, w), lambda i, q_ref: (0, i, 0))],
            out_specs=pl.BlockSpec((tr, w), lambda i, q_ref: (i, 0))),
        out_shape=jax.ShapeDtypeStruct((h, w), F32),
        compiler_params=_cparams(1),
    )(q_idx, s, others)


def _sum_devices(parts):
    nd, r, w = parts.shape

    def body(p_ref, o_ref):
        acc = p_ref[0]
        for k in range(1, nd):
            acc = acc + p_ref[k]
        o_ref[...] = acc

    return pl.pallas_call(
        body, name="sum_devices", in_specs=[_full((nd, r, w))], out_specs=_full((r, w)),
        out_shape=jax.ShapeDtypeStruct((r, w), parts.dtype), grid=(1,), compiler_params=_cparams(1),
    )(parts)


def _pack_win(w_in):
    o = np.cumsum((0,) + IN_SPLITS)
    gq, gk, gv, gg, ga, sq, sk, sv = [w_in[:, o[i]:o[i + 1]] for i in range(8)]
    z = jnp.zeros((w_in.shape[0], 128 - GLA_RANK), w_in.dtype)
    dup = lambda a: jnp.concatenate([a[:, :64], a[:, :64], a[:, 64:], a[:, 64:]], axis=1)
    return jnp.concatenate([gq, gk, gv, gg, ga, z, sq, dup(sk), dup(sv)], axis=1)


def _unpack_dwin(d):
    und = lambda a: jnp.concatenate([a[:, 0:64] + a[:, 64:128], a[:, 128:192] + a[:, 192:256]], axis=1)
    return jnp.concatenate([d[:, :P_GA], d[:, P_GA:P_GA + GLA_RANK], d[:, P_SQ:P_SK], und(d[:, P_SK:P_SV]),
                            und(d[:, P_SV:P_END])], axis=1)


def _local_step(x, target, meta, p):
    s = x.shape[0]
    t = s + BLK
    h0 = jnp.concatenate([jnp.zeros((PAD, D_MODEL), F32), meta, x], axis=0)
    cos, sin = _rope_tables(t)

    h1, n1, g1, u1, f1 = _ffn_fwd(h0, p["ffn1_pre_norm"], p["ffn1_w_gate"], p["ffn1_w_up"], p["ffn1_w_down"],
                                  p["ffn1_post_norm"])
    n2, gq, gk, gv, gg, ga, la, sq, sk, sv = _mix_proj(h1, p["mix_pre_norm"], p["w_in"], p["gla_w_a2"], p["gla_b_a"],
                                                       cos, sin)
    ogla, ss = _gla_fwd(gq, gk, gv, la)
    oswa = _swa_fwd(p["swa_sinks"], sq, sk, sv)
    h2, cat, m = _mix_out(h1, ogla, gg, oswa, p["gla_out_norm"], p["swa_out_norm"], p["w_out"], p["mix_post_norm"])
    h3, n3, g3, u3, f3 = _ffn_fwd(h2, p["ffn2_pre_norm"], p["ffn2_w_gate"], p["ffn2_w_up"], p["ffn2_w_down"],
                                  p["ffn2_post_norm"])
    dy, sse = _loss_head(h3, target)

    grads = {}
    dh2, df3, dg3, du3, a3, grads["ffn2_pre_norm"], grads["ffn2_post_norm"] = _ffn_bwd(
        dy, h2, f3, g3, u3, p["ffn2_pre_norm"], p["ffn2_w_gate"], p["ffn2_w_up"], p["ffn2_w_down"],
        p["ffn2_post_norm"])
    gu, grads["ffn2_w_down"] = _ffn_wgrad(n3, df3, dg3, du3, a3)
    grads["ffn2_w_gate"], grads["ffn2_w_up"] = gu[:, :D_MODEL], gu[:, D_MODEL:]

    dogla, dgg, doswa, dm, grads["mix_post_norm"], grads["gla_out_norm"], grads["swa_out_norm"] = _mix_out_bwd(
        dh2, m, ogla, gg, oswa, p["gla_out_norm"], p["swa_out_norm"], p["w_out"], p["mix_post_norm"])
    grads["w_out"] = _xty(cat, dm)
    dsq, dsk, dsv, dkm, dvm, dsinks = _swa_bwd(p["swa_sinks"], sq, sk, sv, oswa, doswa)
    grads["swa_sinks"] = dsinks[:, 0]
    dgq, dgk, dgv, dla = _gla_bwd(gq, gk, gv, la, ss, dogla)
    dh1, dproj, grads["mix_pre_norm"], dwa2p, grads["gla_b_a"] = _mix_in_bwd(
        dh2, h1, p["mix_pre_norm"], p["w_in"], p["gla_w_a2"], p["gla_b_a"], cos, sin, ga, dgq, dgk, dgv, dgg, dla,
        dsq, dsk, dsv, dkm, dvm)
    grads["gla_w_a2"] = dwa2p[:GLA_RANK]
    grads["w_in"] = _unpack_dwin(_xty(n2, dproj))

    dh0, df1, dg1, du1, a1, grads["ffn1_pre_norm"], grads["ffn1_post_norm"] = _ffn_bwd(
        dh1, h0, f1, g1, u1, p["ffn1_pre_norm"], p["ffn1_w_gate"], p["ffn1_w_up"], p["ffn1_w_down"],
        p["ffn1_post_norm"])
    gu, grads["ffn1_w_down"] = _ffn_wgrad(n1, df1, dg1, du1, a1)
    grads["ffn1_w_gate"], grads["ffn1_w_up"] = gu[:, :D_MODEL], gu[:, D_MODEL:]
    grads["meta_tokens"] = dh0[PAD:BLK]
    return sse[0, 0], dh0[BLK:], grads


WEIGHTS = ['meta_tokens', 'ffn1_pre_norm', 'ffn1_w_gate', 'ffn1_w_up', 'ffn1_w_down', 'ffn1_post_norm',
           'mix_pre_norm', 'w_in', 'gla_w_a2', 'gla_b_a', 'gla_out_norm', 'swa_sinks', 'swa_out_norm', 'w_out',
           'mix_post_norm', 'ffn2_pre_norm', 'ffn2_w_gate', 'ffn2_w_up', 'ffn2_w_down', 'ffn2_post_norm']
BIG = ['ffn1_w_gate', 'ffn1_w_up', 'ffn1_w_down', 'w_in', 'w_out', 'ffn2_w_gate', 'ffn2_w_up', 'ffn2_w_down']
SMALL = [n for n in WEIGHTS if n not in BIG]
FJ = D_FF // N_CHIPS
D_IN_J = D_IN // N_CHIPS
D_OUT_J = D_MODEL // N_CHIPS
BIG_SHARD = {'ffn1_w_gate': (D_MODEL, FJ), 'ffn1_w_up': (D_MODEL, FJ), 'ffn1_w_down': (FJ, D_MODEL),
             'w_in': (D_MODEL, D_IN_J), 'w_out': (D_OUT_J, D_MODEL),
             'ffn2_w_gate': (D_MODEL, FJ), 'ffn2_w_up': (D_MODEL, FJ), 'ffn2_w_down': (FJ, D_MODEL)}


def _small_rows(name, a):
    flat = a.reshape(-1)
    rows = -(-flat.shape[0] // 1024) * 8
    return jnp.pad(flat, (0, rows * 128 - flat.shape[0])).reshape(rows, 128)


def kernel(x, meta_tokens, ffn1_pre_norm, ffn1_w_gate, ffn1_w_up, ffn1_w_down, ffn1_post_norm, mix_pre_norm, w_in, gla_w_a2, gla_b_a, gla_out_norm, swa_sinks, swa_out_norm, w_out, mix_post_norm, ffn2_pre_norm, ffn2_w_gate, ffn2_w_up, ffn2_w_down, ffn2_post_norm, loss_target, m_meta_tokens, m_ffn1_pre_norm, m_ffn1_w_gate, m_ffn1_w_up, m_ffn1_w_down, m_ffn1_post_norm, m_mix_pre_norm, m_w_in, m_gla_w_a2, m_gla_b_a, m_gla_out_norm, m_swa_sinks, m_swa_out_norm, m_w_out, m_mix_post_norm, m_ffn2_pre_norm, m_ffn2_w_gate, m_ffn2_w_up, m_ffn2_w_down, m_ffn2_post_norm, v_meta_tokens, v_ffn1_pre_norm, v_ffn1_w_gate, v_ffn1_w_up, v_ffn1_w_down, v_ffn1_post_norm, v_mix_pre_norm, v_w_in, v_gla_w_a2, v_gla_b_a, v_gla_out_norm, v_swa_sinks, v_swa_out_norm, v_w_out, v_mix_post_norm, v_ffn2_pre_norm, v_ffn2_w_gate, v_ffn2_w_up, v_ffn2_w_down, v_ffn2_post_norm):
    args = dict(locals())
    w = {n: args[n] for n in WEIGHTS}
    mom = {n: args["m_" + n] for n in WEIGHTS}
    var = {n: args["v_" + n] for n in WEIGHTS}
    cx, cy, cc = lax.axis_index("x"), lax.axis_index("y"), lax.axis_index("c")
    q_idx = (2 * cx + cy).astype(jnp.int32).reshape(1)
    c_idx = cc.astype(jnp.int32).reshape(1)

    q_chip = 2 * cx + cy
    bf = {n: _own_slot(w[n].reshape(BIG_SHARD[n]).astype(BF16), q_chip) for n in BIG}
    early = _GatherChips([bf["ffn1_w_gate"], bf["ffn1_w_up"], bf["ffn1_w_down"], _own_slot(w["meta_tokens"], q_chip),
                          _own_slot(w["gla_w_a2"].reshape(GLA_RANK, GLA_KW // N_CHIPS), q_chip)])
    wg1, wu1, wd1, meta4, wa24 = _comm_call(early, "gather_ffn1")
    meta_full = meta4.transpose(1, 0, 2).reshape(N_META, D_MODEL)
    wa2p = jnp.pad(wa24.transpose(1, 0, 2).reshape(GLA_RANK, GLA_KW), ((0, 128 - GLA_RANK), (0, 0))).astype(BF16)
    sinks = w["swa_sinks"].reshape(SWA_QH)

    seq, target = x[0], loss_target[0]
    t = seq.shape[0] + BLK
    h0 = jnp.concatenate([jnp.zeros((PAD, D_MODEL), F32), meta_full, seq], axis=0)
    cos, sin = _rope_tables(t)
    late = _GatherChips([bf["w_in"], bf["w_out"], bf["ffn2_w_gate"], bf["ffn2_w_up"], bf["ffn2_w_down"]])
    (h1, n1, g1, u1, f1), (win4, wout4, wg2, wu2, wd2) = _ffn_fwd(
        h0, w["ffn1_pre_norm"], wg1, wu1, wd1, w["ffn1_post_norm"], hook=late)
    winp = _pack_win(win4.transpose(1, 0, 2).reshape(D_MODEL, D_IN))
    wout = wout4.reshape(D_MODEL, D_MODEL)
    n2, gq, gk, gv, gg, ga, la, sq, sk, sv = _mix_proj(h1, w["mix_pre_norm"], winp, wa2p, w["gla_b_a"], cos, sin)
    ogla, ss = _gla_fwd(gq, gk, gv, la)
    oswa = _swa_fwd(sinks, sq, sk, sv)
    h2, cat, m = _mix_out(h1, ogla, gg, oswa, w["gla_out_norm"], w["swa_out_norm"], wout, w["mix_post_norm"])
    h3, n3, g3, u3, f3 = _ffn_fwd(h2, w["ffn2_pre_norm"], wg2, wu2, wd2, w["ffn2_post_norm"])
    dy, sse = _loss_head(h3, target)
    loss = lax.psum(sse[0, 0] * (0.5 / D_MODEL), ("x", "y", "c"))

    g = {}
    dh2, df3, dg3, du3, a3, g["ffn2_pre_norm"], g["ffn2_post_norm"] = _ffn_bwd(
        dy, h2, f3, g3, u3, w["ffn2_pre_norm"], wg2, wu2, wd2, w["ffn2_post_norm"])
    gu2, gd2 = _ffn_wgrad(n3, df3, dg3, du3, a3)
    (dogla, dgg, doswa, dm, g["mix_post_norm"], g["gla_out_norm"], g["swa_out_norm"]), (rgu2, rgd2) = _mix_out_bwd(
        dh2, m, ogla, gg, oswa, w["gla_out_norm"], w["swa_out_norm"], wout, w["mix_post_norm"],
        hook=_PairExchange([gu2, gd2]))
    sgu2, sgd2 = _pair_sum(gu2, rgu2, c_idx), _pair_sum(gd2, rgd2, c_idx)
    gout = _xty(cat, dm).reshape(N_CHIPS, D_OUT_J, D_MODEL).astype(BF16)
    (dsq, dsk, dsv, dkm, dvm, dsinks), (ogu2, ogd2) = _swa_bwd(sinks, sq, sk, sv, oswa, doswa,
                                                               hook=_ChipScatter([sgu2, sgd2]))
    g["swa_sinks"] = dsinks[:, 0].reshape(1, SWA_QH)
    dgq, dgk, dgv, dla = _gla_bwd(gq, gk, gv, la, ss, dogla)
    dh1, dproj, g["mix_pre_norm"], dwa2p, g["gla_b_a"] = _mix_in_bwd(
        dh2, h1, w["mix_pre_norm"], winp, wa2p, w["gla_b_a"], cos, sin, ga, dgq, dgk, dgv, dgg, dla,
        dsq, dsk, dsv, dkm, dvm)
    g["gla_w_a2"] = dwa2p[:GLA_RANK]
    gin = _unpack_dwin(_xty(n2, dproj)).reshape(D_MODEL, N_CHIPS, D_IN_J).transpose(1, 0, 2).astype(BF16)
    (dh0, df1, dg1, du1, a1, g["ffn1_pre_norm"], g["ffn1_post_norm"]), (rgin, rgout) = _ffn_bwd(
        dh1, h0, f1, g1, u1, w["ffn1_pre_norm"], wg1, wu1, wd1, w["ffn1_post_norm"],
        hook=_PairExchange([gin, gout]))
    sgin, sgout = _pair_sum(gin, rgin, c_idx), _pair_sum(gout, rgout, c_idx)
    (gu1, gd1), (ogin, ogout) = _ffn_wgrad(n1, df1, dg1, du1, a1, hook=_ChipScatter([sgin, sgout]))
    g["meta_tokens"] = dh0[PAD:BLK]
    grad_x = dh0[BLK:]
    rgu1, rgd1 = _comm_call(_PairExchange([gu1, gd1]), "pair_exchange_ffn1")
    sgu1, sgd1 = _pair_sum(gu1, rgu1, c_idx), _pair_sum(gd1, rgd1, c_idx)
    ogu1, ogd1 = _comm_call(_ChipScatter([sgu1, sgd1]), "chip_scatter_ffn1")
    halves = [_chip_sum(s, o, q_idx) for s, o in ((sgu1, ogu1), (sgd1, ogd1), (sgin, ogin), (sgout, ogout),
                                                   (sgu2, ogu2), (sgd2, ogd2))]
    others = _comm_call(_PairShare(halves), "pair_share")
    reduced = {"ffn1_w_gate": (0, 0), "ffn1_w_up": (0, 1), "ffn1_w_down": (1, None), "w_in": (2, None),
               "w_out": (3, None), "ffn2_w_gate": (4, 0), "ffn2_w_up": (4, 1), "ffn2_w_down": (5, None)}
    grad, delta, new_m, new_v = {}, {}, {}, {}
    for n in BIG:
        shard = BIG_SHARD[n]
        k, half = reduced[n]
        outs = _adamw_halves(w[n].reshape(shard), halves[k], others[k], mom[n].reshape(shard), var[n].reshape(shard),
                             c_idx, half)
        grad[n], delta[n], new_m[n], new_v[n] = [a.reshape(w[n].shape) for a in outs]

    small_rows = [_small_rows(n, g[n]) for n in SMALL]
    ssizes = [a.shape[0] for a in small_rows]
    gsmall = _sum_devices(_all_gather_devices(jnp.concatenate(small_rows, axis=0)))
    col0 = {"meta_tokens": D_MODEL // N_CHIPS, "gla_w_a2": GLA_KW // N_CHIPS}
    gs, ws, ms, vs = [], [], [], []
    off = 0
    for n, sz in zip(SMALL, ssizes):
        full_shape = g[n].shape
        gn = gsmall[off:off + sz].reshape(-1)[:math.prod(full_shape)].reshape(full_shape)
        off += sz
        if n in col0:
            gn = lax.dynamic_slice_in_dim(gn, (2 * cx + cy) * col0[n], col0[n], axis=1)
        grad[n] = gn.reshape(w[n].shape)
        gs.append(_small_rows(n, grad[n]))
        ws.append(_small_rows(n, w[n]))
        ms.append(_small_rows(n, mom[n]))
        vs.append(_small_rows(n, var[n]))
    psizes = [a.shape[0] for a in gs]
    d, nm, nv = _adamw(*[jnp.concatenate(a, axis=0) for a in (ws, gs, ms, vs)])
    off = 0
    for n, sz in zip(SMALL, psizes):
        cnt = math.prod(w[n].shape)
        delta[n], new_m[n], new_v[n] = [a[off:off + sz].reshape(-1)[:cnt].reshape(w[n].shape) for a in (d, nm, nv)]
        off += sz

    return (loss, grad_x[None], *[grad[n] for n in WEIGHTS], *[delta[n] for n in WEIGHTS],
            *[new_m[n] for n in WEIGHTS], *[new_v[n] for n in WEIGHTS])
```

```python
import functools
import math

import numpy as np
import jax
import jax.numpy as jnp
from jax import lax
from jax.experimental import pallas as pl
from jax.experimental.pallas import tpu as pltpu

F32 = jnp.float32
BF16 = jnp.bfloat16
MESH = pl.DeviceIdType.MESH

D_MODEL = 1024
D_FF = 2816
N_CHIPS = 4
N_DEV = 8
N_META = 16
BLK = 128
PAD = BLK - N_META
GLA_CHUNK = 64
GLA_HEADS = 4
GLA_DV = 128
GLA_DK = 64
GLA_KW = GLA_HEADS * GLA_DK
GLA_W = GLA_HEADS * GLA_DV
GLA_RANK = 16
GLA_TAU = 16.0
SWA_HD = 64
SWA_QH = 8
SWA_KVH = 2
SWA_W = SWA_QH * SWA_HD
WINDOW = 128
ROPE_THETA = 10000.0
EPS = 1e-6
NEG_INF = -1e30
IN_SPLITS = (256, 256, 512, 512, 16, 512, 128, 128)
D_IN = sum(IN_SPLITS)
P_GQ, P_GK, P_GV, P_GG, P_GA, P_SQ, P_SK, P_SV, P_END = 0, 256, 512, 1024, 1536, 1664, 2176, 2432, 2688
ADAM_LR, ADAM_B1, ADAM_B2, ADAM_EPS, ADAM_WD, ADAM_STEP = 0.001, 0.9, 0.999, 1e-08, 0.01, 10
VMEM_LIMIT = 56 * 1024 * 1024

NT = (((1,), (1,)), ((), ()))
TN = (((0,), (0,)), ((), ()))


def _cparams(n_axes):
    return pltpu.CompilerParams(dimension_semantics=("arbitrary",) * n_axes, vmem_limit_bytes=VMEM_LIMIT)


def _row_tile(t):
    for tm in (640, 512, 384, 256, 128):
        if t % tm == 0:
            return tm
    raise ValueError(t)


def _div_tile(r, cap=512):
    best = None
    for tr in range(8, min(r, cap) + 1, 8):
        if r % tr == 0:
            best = tr
    return best if best is not None else r


def _dot(a, b):
    return jnp.dot(a, b, preferred_element_type=F32)


def _dg(a, b, dims):
    return lax.dot_general(a, b, dims, preferred_element_type=F32)


def _rms(x, w):
    r = lax.rsqrt(jnp.mean(x * x, axis=-1, keepdims=True) + EPS)
    xh = x * r
    return xh * w, xh, r


def _rms_bwd(xh, r, w, dy):
    wdy = dy * w
    dx = r * (wdy - xh * jnp.mean(wdy * xh, axis=-1, keepdims=True))
    dw = jnp.sum(dy * xh, axis=0, keepdims=True)
    return dx, dw


def _sigmoid(x):
    return 1.0 / (1.0 + jnp.exp(-x))


def _full(shape):
    nd = len(shape)
    return pl.BlockSpec(shape, lambda *_: (0,) * nd)


ANY = pl.BlockSpec(memory_space=pl.ANY)


def _pallas(body, *, name, grid, in_specs, out_specs, out_shape, args, scratch_shapes=(), hook=None):
    n_axes = len(grid)
    if hook is None:
        return pl.pallas_call(body, name=name, grid=grid, in_specs=list(in_specs), out_specs=list(out_specs),
                              out_shape=list(out_shape), scratch_shapes=list(scratch_shapes),
                              compiler_params=_cparams(n_axes))(*args)
    n_in, n_out, n_scr = len(in_specs), len(out_specs), len(scratch_shapes)
    h_in, h_out = len(hook.inputs), len(hook.out_shape)
    total = math.prod(grid)

    def wrapped(*refs):
        ins, hins = refs[:n_in], refs[n_in:n_in + h_in]
        o0 = n_in + h_in
        outs, houts = refs[o0:o0 + n_out], refs[o0 + n_out:o0 + n_out + h_out]
        s0 = o0 + n_out + h_out
        scr, hscr = refs[s0:s0 + n_scr], refs[s0 + n_scr:]
        step = pl.program_id(0)
        for a in range(1, n_axes):
            step = step * grid[a] + pl.program_id(a)

        @pl.when(step == 0)
        def _():
            hook.start(hins, houts, hscr)

        body(*ins, *outs, *scr)

        if hook.has_mid:
            @pl.when(step == (3 * total) // 4)
            def _():
                hook.mid(hins, houts, hscr)

        @pl.when(step == total - 1)
        def _():
            hook.finish(hins, houts, hscr)

    res = pl.pallas_call(
        wrapped, name=name, grid=grid, in_specs=list(in_specs) + [ANY] * h_in,
        out_specs=list(out_specs) + [ANY] * h_out, out_shape=list(out_shape) + list(hook.out_shape),
        scratch_shapes=list(scratch_shapes) + list(hook.scratch), compiler_params=_cparams(n_axes),
        input_output_aliases={n_in + a: n_out + b for a, b in hook.aliases},
    )(*args, *hook.inputs)
    return res[:n_out], res[n_out:]


def _ffn_fwd(h, wpre, wg4, wu4, wd4, wpost, hook=None):
    t = h.shape[0]
    tm = _row_tile(t)
    nj, _, fj = wg4.shape

    def body(h_ref, wpre_ref, wg_ref, wu_ref, wd_ref, wpost_ref, hout_ref, n_ref, p1_ref, p2_ref, a_ref, f_ref,
             acc_ref):
        j = pl.program_id(1)

        @pl.when(j == 0)
        def _():
            y, _, _ = _rms(h_ref[...], wpre_ref[...])
            n_ref[...] = y.astype(BF16)
            acc_ref[...] = jnp.zeros_like(acc_ref)

        n = n_ref[...]
        g = _dot(n, wg_ref[...])
        u = _dot(n, wu_ref[...])
        sg = _sigmoid(g)
        silu = g * sg
        p1_ref[...] = (u * (sg + silu * (1.0 - sg))).astype(BF16)
        p2_ref[...] = silu.astype(BF16)
        a = (silu * u).astype(BF16)
        a_ref[...] = a
        acc_ref[...] += _dot(a, wd_ref[...])

        @pl.when(j == nj - 1)
        def _():
            f = acc_ref[...]
            f_ref[...] = f
            y, _, _ = _rms(f, wpost_ref[...])
            hout_ref[...] = h_ref[...] + 0.5 * y

    row = pl.BlockSpec((tm, D_MODEL), lambda i, j: (i, 0))
    vec = pl.BlockSpec((1, D_MODEL), lambda i, j: (0, 0))
    wcol = pl.BlockSpec((None, D_MODEL, fj), lambda i, j: (j, 0, 0))
    wrow = pl.BlockSpec((None, fj, D_MODEL), lambda i, j: (j, 0, 0))
    act = pl.BlockSpec((None, tm, fj), lambda i, j: (j, i, 0))
    return _pallas(
        body, name="ffn_fwd", grid=(t // tm, nj),
        in_specs=[row, vec, wcol, wcol, wrow, vec],
        out_specs=[row, row, act, act, act, row],
        out_shape=[jax.ShapeDtypeStruct((t, D_MODEL), F32), jax.ShapeDtypeStruct((t, D_MODEL), BF16),
                   jax.ShapeDtypeStruct((nj, t, fj), BF16), jax.ShapeDtypeStruct((nj, t, fj), BF16),
                   jax.ShapeDtypeStruct((nj, t, fj), BF16), jax.ShapeDtypeStruct((t, D_MODEL), F32)],
        scratch_shapes=[pltpu.VMEM((tm, D_MODEL), F32)],
        args=(h, wpre, wg4, wu4, wd4, wpost), hook=hook)


def _ffn_bwd(dhout, h, f, p14, p24, wpre, wg4, wu4, wd4, wpost, hook=None):
    t = h.shape[0]
    tm = _row_tile(t)
    nj, _, fj = wg4.shape

    def body(dhout_ref, h_ref, f_ref, p1_ref, p2_ref, wpre_ref, wg_ref, wu_ref, wd_ref, wpost_ref,
             dh_ref, df_ref, dg_ref, du_ref, dwpre_ref, dwpost_ref, dn_ref):
        i = pl.program_id(0)
        j = pl.program_id(1)

        @pl.when((i == 0) & (j == 0))
        def _():
            dwpre_ref[...] = jnp.zeros_like(dwpre_ref)
            dwpost_ref[...] = jnp.zeros_like(dwpost_ref)

        @pl.when(j == 0)
        def _():
            wpost = wpost_ref[...]
            _, fh, r = _rms(f_ref[...], wpost)
            df, dw = _rms_bwd(fh, r, wpost, 0.5 * dhout_ref[...])
            dwpost_ref[...] += dw
            df_ref[...] = df.astype(BF16)
            dn_ref[...] = jnp.zeros_like(dn_ref)

        da = _dg(df_ref[...], wd_ref[...], NT)
        dg = (da * p1_ref[...].astype(F32)).astype(BF16)
        du = (da * p2_ref[...].astype(F32)).astype(BF16)
        dg_ref[...] = dg
        du_ref[...] = du
        dn_ref[...] += _dg(dg, wg_ref[...], NT) + _dg(du, wu_ref[...], NT)

        @pl.when(j == nj - 1)
        def _():
            wpre = wpre_ref[...]
            _, hh, r = _rms(h_ref[...], wpre)
            dx, dw = _rms_bwd(hh, r, wpre, dn_ref[...])
            dwpre_ref[...] += dw
            dh_ref[...] = dhout_ref[...] + dx

    row = pl.BlockSpec((tm, D_MODEL), lambda i, j: (i, 0))
    vec = pl.BlockSpec((1, D_MODEL), lambda i, j: (0, 0))
    wcol = pl.BlockSpec((None, D_MODEL, fj), lambda i, j: (j, 0, 0))
    wrow = pl.BlockSpec((None, fj, D_MODEL), lambda i, j: (j, 0, 0))
    act = pl.BlockSpec((None, tm, fj), lambda i, j: (j, i, 0))
    actshape = jax.ShapeDtypeStruct((nj, t, fj), BF16)
    return _pallas(
        body, name="ffn_bwd", grid=(t // tm, nj),
        in_specs=[row, row, row, act, act, vec, wcol, wcol, wrow, vec],
        out_specs=[row, row, act, act, vec, vec],
        out_shape=[jax.ShapeDtypeStruct((t, D_MODEL), F32), jax.ShapeDtypeStruct((t, D_MODEL), BF16),
                   actshape, actshape,
                   jax.ShapeDtypeStruct((1, D_MODEL), F32), jax.ShapeDtypeStruct((1, D_MODEL), F32)],
        scratch_shapes=[pltpu.VMEM((tm, D_MODEL), F32)],
        args=(dhout, h, f, p14, p24, wpre, wg4, wu4, wd4, wpost), hook=hook)


def _ffn_wgrad(n, df, dg4, du4, a4, hook=None):
    t = n.shape[0]
    tm = _row_tile(t)
    ni = t // tm
    nj, _, fj = dg4.shape

    def body(n_ref, df_ref, dg_ref, du_ref, a_ref, dwgu_ref, dwd_ref, gu_acc, d_acc):
        i = pl.program_id(1)

        @pl.when(i == 0)
        def _():
            gu_acc[...] = jnp.zeros_like(gu_acc)
            d_acc[...] = jnp.zeros_like(d_acc)

        nn = n_ref[...]
        gu_acc[0:D_MODEL, :] += _dg(nn, dg_ref[...], TN)
        gu_acc[D_MODEL:2 * D_MODEL, :] += _dg(nn, du_ref[...], TN)
        d_acc[...] += _dg(a_ref[...], df_ref[...], TN)

        @pl.when(i == ni - 1)
        def _():
            dwgu_ref[...] = gu_acc[...].astype(BF16)
            dwd_ref[...] = d_acc[...].astype(BF16)

    row = pl.BlockSpec((tm, D_MODEL), lambda j, i: (i, 0))
    act = pl.BlockSpec((None, tm, fj), lambda j, i: (j, i, 0))
    wcol = pl.BlockSpec((None, 2 * D_MODEL, fj), lambda j, i: (j, 0, 0))
    wrow = pl.BlockSpec((None, fj, D_MODEL), lambda j, i: (j, 0, 0))
    return _pallas(
        body, name="ffn_wgrad", grid=(nj, ni),
        in_specs=[row, row, act, act, act],
        out_specs=[wcol, wrow],
        out_shape=[jax.ShapeDtypeStruct((nj, 2 * D_MODEL, fj), BF16), jax.ShapeDtypeStruct((nj, fj, D_MODEL), BF16)],
        scratch_shapes=[pltpu.VMEM((2 * D_MODEL, fj), F32), pltpu.VMEM((fj, D_MODEL), F32)],
        args=(n, df, dg4, du4, a4), hook=hook)


def _xty(x, y):
    t, k = x.shape
    n = y.shape[1]
    tm = _row_tile(t)
    tn = n if n <= 1024 else (896 if n % 896 == 0 else 128)

    def body(x_ref, y_ref, o_ref):
        @pl.when(pl.program_id(1) == 0)
        def _():
            o_ref[...] = jnp.zeros_like(o_ref)

        o_ref[...] += _dg(x_ref[...], y_ref[...], TN)

    return pl.pallas_call(
        body, name="xty", grid=(n // tn, t // tm),
        in_specs=[pl.BlockSpec((tm, k), lambda j, i: (i, 0)), pl.BlockSpec((tm, tn), lambda j, i: (i, j))],
        out_specs=pl.BlockSpec((k, tn), lambda j, i: (0, j)),
        out_shape=jax.ShapeDtypeStruct((k, n), F32),
        compiler_params=_cparams(2),
    )(x, y)


def _rope_tables(t):
    pos = (jnp.arange(t, dtype=jnp.int32) - PAD).astype(F32)
    inv_freq = 1.0 / (ROPE_THETA ** (jnp.arange(0, SWA_HD, 2, dtype=F32) / SWA_HD))
    ang = pos[:, None] * inv_freq[None, :]
    cos = jnp.cos(ang)
    sin = jnp.sin(ang)
    return jnp.concatenate([cos, cos, cos, cos], axis=1), jnp.concatenate([-sin, sin, -sin, sin], axis=1)


def _rot_half(x, first_half):
    return jnp.where(first_half, pltpu.roll(x, 96, 1), pltpu.roll(x, 32, 1))


def _first_half_mask(rows):
    lane = lax.broadcasted_iota(jnp.int32, (rows, 128), 1)
    return (lane % 64) < 32


def _log_sigmoid(z):
    return jnp.minimum(z, 0.0) - jnp.log(1.0 + jnp.exp(-jnp.abs(z)))


def _mix_proj(h1, wmixpre, winp, wa2p, bap, cos, sin):
    t = h1.shape[0]
    tm = _row_tile(t)

    def body(h_ref, w_ref, win_ref, wa2_ref, ba_ref, cos_ref, sin_ref,
             n_ref, gq_ref, gk_ref, gv_ref, gg_ref, ga_ref, la_ref, sq_ref, sk_ref, sv_ref):
        y, _, _ = _rms(h_ref[...], w_ref[...])
        n = y.astype(BF16)
        n_ref[...] = n
        proj = _dot(n, win_ref[...])
        gq_ref[...] = proj[:, P_GQ:P_GK]
        gk_ref[...] = proj[:, P_GK:P_GV]
        gv_ref[...] = proj[:, P_GV:P_GG]
        gg_ref[...] = proj[:, P_GG:P_GA]
        ga = proj[:, P_GA:P_SQ]
        ga_ref[...] = ga
        z = _dot(ga.astype(BF16), wa2_ref[...]) + ba_ref[...]
        la_ref[...] = _log_sigmoid(z) * (1.0 / GLA_TAU)
        c = cos_ref[...]
        s = sin_ref[...]
        fh = _first_half_mask(tm)
        for k in range(4):
            x = proj[:, P_SQ + 128 * k:P_SQ + 128 * (k + 1)]
            sq_ref[:, 128 * k:128 * (k + 1)] = (x * c + _rot_half(x, fh) * s).astype(BF16)
        for k in range(2):
            x = proj[:, P_SK + 128 * k:P_SK + 128 * (k + 1)]
            sk_ref[:, 128 * k:128 * (k + 1)] = (x * c + _rot_half(x, fh) * s).astype(BF16)
        sv_ref[...] = proj[:, P_SV:P_END].astype(BF16)

    def row(w):
        return pl.BlockSpec((tm, w), lambda i: (i, 0))

    def rshape(w, dt):
        return jax.ShapeDtypeStruct((t, w), dt)

    return pl.pallas_call(
        body, name="mix_proj", grid=(t // tm,),
        in_specs=[row(D_MODEL), _full((1, D_MODEL)), _full((D_MODEL, P_END)), _full((128, GLA_KW)),
                  _full((1, GLA_KW)), row(128), row(128)],
        out_specs=[row(D_MODEL), row(256), row(256), row(512), row(512), row(128), row(256), row(512), row(256),
                   row(256)],
        out_shape=[rshape(D_MODEL, BF16), rshape(256, F32), rshape(256, F32), rshape(512, F32), rshape(512, F32),
                   rshape(128, F32), rshape(256, F32), rshape(512, BF16), rshape(256, BF16), rshape(256, BF16)],
        compiler_params=_cparams(1),
    )(h1, wmixpre, winp, wa2p, bap, cos, sin)


def _gla_cumsum(la, tril_f):
    b = jnp.dot(tril_f, la, precision=lax.Precision.HIGHEST, preferred_element_type=F32)
    row = lax.broadcasted_iota(jnp.int32, b.shape, 0)
    bm = jnp.sum(jnp.where(row == GLA_CHUNK // 2 - 1, b, 0.0), axis=0, keepdims=True)
    bl = jnp.sum(jnp.where(row == GLA_CHUNK - 1, b, 0.0), axis=0, keepdims=True)
    return b, bm, bl


def _gla_chunk_terms(la, q, k, tril_f):
    b, bm, bl = _gla_cumsum(la, tril_f)
    qs = q * (GLA_DK ** -0.5)
    qt = qs * jnp.exp(b - bm)
    kt = k * jnp.exp(bm - b)
    qh = qs * jnp.exp(b)
    kh = k * jnp.exp(bl - b)
    ebl = jnp.exp(bl)
    return qt, kt, qh, kh, ebl


def _gla_fwd(gq, gk, gv, la):
    t = gq.shape[0]
    nb = t // BLK
    ncb = BLK // GLA_CHUNK

    def body(q_ref, k_ref, v_ref, la_ref, o_ref, ss_ref, st_ref):
        @pl.when(pl.program_id(0) == 0)
        def _():
            st_ref[...] = jnp.zeros_like(st_ref)

        r = lax.broadcasted_iota(jnp.int32, (GLA_CHUNK, GLA_CHUNK), 0)
        c = lax.broadcasted_iota(jnp.int32, (GLA_CHUNK, GLA_CHUNK), 1)
        tril = r >= c
        tril_f = tril.astype(F32)
        lane = lax.broadcasted_iota(jnp.int32, (GLA_CHUNK, 128), 1)
        halves = (lane < 64, lane >= 64)
        for ch in range(ncb):
            rows = slice(ch * GLA_CHUNK, (ch + 1) * GLA_CHUNK)
            qt, kt, qh, kh, ebl = _gla_chunk_terms(la_ref[rows, :], q_ref[rows, :], k_ref[rows, :], tril_f)
            for hp in range(2):
                ls = slice(128 * hp, 128 * (hp + 1))
                kt2 = kt[:, ls].astype(BF16)
                kh2 = kh[:, ls].astype(BF16)
                for e in range(2):
                    h = 2 * hp + e
                    vs = slice(GLA_DV * h, GLA_DV * (h + 1))
                    st = st_ref[h]
                    ss_ref[ch, h] = st
                    v = v_ref[rows, vs].astype(BF16)
                    qtm = jnp.where(halves[e], qt[:, ls], 0.0).astype(BF16)
                    qhm = jnp.where(halves[e], qh[:, ls], 0.0).astype(BF16)
                    a = jnp.where(tril, _dg(qtm, kt2, NT), 0.0)
                    o_ref[rows, vs] = _dot(a.astype(BF16), v) + _dg(qhm, st.astype(BF16), NT)
                    st_ref[h] = st * ebl[:, ls] + _dg(v, kh2, TN)

    def row(w):
        return pl.BlockSpec((BLK, w), lambda i: (i, 0))

    return pl.pallas_call(
        body, name="gla_fwd", grid=(nb,),
        in_specs=[row(256), row(256), row(512), row(256)],
        out_specs=[row(512), pl.BlockSpec((ncb, GLA_HEADS, GLA_DV, 128), lambda i: (i, 0, 0, 0))],
        out_shape=[jax.ShapeDtypeStruct((t, GLA_W), F32),
                   jax.ShapeDtypeStruct((nb * ncb, GLA_HEADS, GLA_DV, 128), F32)],
        scratch_shapes=[pltpu.VMEM((GLA_HEADS, GLA_DV, 128), F32)],
        compiler_params=_cparams(1),
    )(gq, gk, gv, la)


def _gla_bwd(gq, gk, gv, la, ss, do):
    t = gq.shape[0]
    nb = t // BLK
    ncb = BLK // GLA_CHUNK

    def body(q_ref, k_ref, v_ref, la_ref, ss_ref, do_ref, dq_ref, dk_ref, dv_ref, dla_ref, dst_ref):
        @pl.when(pl.program_id(0) == 0)
        def _():
            dst_ref[...] = jnp.zeros_like(dst_ref)

        r = lax.broadcasted_iota(jnp.int32, (GLA_CHUNK, GLA_CHUNK), 0)
        c = lax.broadcasted_iota(jnp.int32, (GLA_CHUNK, GLA_CHUNK), 1)
        tril = r >= c
        tril_f = tril.astype(F32)
        triu_f = (r <= c).astype(F32)
        lane = lax.broadcasted_iota(jnp.int32, (GLA_CHUNK, 128), 1)
        halves = (lane < 64, lane >= 64)
        last_row = lax.broadcasted_iota(jnp.int32, (GLA_CHUNK, 128), 0) == GLA_CHUNK - 1
        lane1 = lax.broadcasted_iota(jnp.int32, (1, 128), 1)
        halves1 = (lane1 < 64, lane1 >= 64)
        scale = GLA_DK ** -0.5
        for ch in reversed(range(ncb)):
            rows = slice(ch * GLA_CHUNK, (ch + 1) * GLA_CHUNK)
            la = la_ref[rows, :]
            b, bm, bl = _gla_cumsum(la, tril_f)
            eq = jnp.exp(b - bm)
            ek = jnp.exp(bm - b)
            eb = jnp.exp(b)
            ekl = jnp.exp(bl - b)
            ebl = jnp.exp(bl)
            qs = q_ref[rows, :] * scale
            kk = k_ref[rows, :]
            qt, kt, qh, kh = qs * eq, kk * ek, qs * eb, kk * ekl
            for hp in range(2):
                ls = slice(128 * hp, 128 * (hp + 1))
                kt2 = kt[:, ls].astype(BF16)
                kh2 = kh[:, ls].astype(BF16)
                dqt = jnp.zeros((GLA_CHUNK, 128), F32)
                dkt = jnp.zeros((GLA_CHUNK, 128), F32)
                dqh = jnp.zeros((GLA_CHUNK, 128), F32)
                dkh = jnp.zeros((GLA_CHUNK, 128), F32)
                dbl = jnp.zeros((1, 128), F32)
                for e in range(2):
                    h = 2 * hp + e
                    vs = slice(GLA_DV * h, GLA_DV * (h + 1))
                    m = halves[e]
                    st = ss_ref[ch, h]
                    dstn = dst_ref[h]
                    v = v_ref[rows, vs].astype(BF16)
                    dov = do_ref[rows, vs].astype(BF16)
                    qtm = jnp.where(m, qt[:, ls], 0.0).astype(BF16)
                    qhm = jnp.where(m, qh[:, ls], 0.0).astype(BF16)
                    khm = jnp.where(m, kh[:, ls], 0.0).astype(BF16)
                    a = jnp.where(tril, _dg(qtm, kt2, NT), 0.0).astype(BF16)
                    da = jnp.where(tril, _dg(dov, v, NT), 0.0).astype(BF16)
                    dstn_b = dstn.astype(BF16)
                    dv_ref[rows, vs] = _dg(a, dov, TN) + _dg(khm, dstn_b, NT)
                    dqt = dqt + jnp.where(m, _dot(da, kt2), 0.0)
                    dkt = dkt + _dg(da, qtm, TN)
                    dqh = dqh + jnp.where(m, _dot(dov, st.astype(BF16)), 0.0)
                    dkh = dkh + jnp.where(m, _dot(v, dstn_b), 0.0)
                    dbl = dbl + jnp.where(halves1[e], jnp.sum(dstn * st, axis=0, keepdims=True), 0.0)
                    dst_ref[h] = dstn * ebl[:, ls] + _dg(dov, qhm, TN)
                dq_ref[rows, ls] = scale * (dqt * eq[:, ls] + dqh * eb[:, ls])
                dk_ref[rows, ls] = dkt * ek[:, ls] + dkh * ekl[:, ls]
                dkk = dkh * kh[:, ls]
                db = dqt * qt[:, ls] - dkt * kt[:, ls] + dqh * qh[:, ls] - dkk
                db_last = jnp.sum(dkk, axis=0, keepdims=True) + ebl[:, ls] * dbl
                db = db + jnp.where(last_row, db_last, 0.0)
                dla_ref[rows, ls] = jnp.dot(triu_f, db, precision=lax.Precision.HIGHEST,
                                            preferred_element_type=F32)

    def row(w):
        return pl.BlockSpec((BLK, w), lambda i: (nb - 1 - i, 0))

    def rshape(w):
        return jax.ShapeDtypeStruct((t, w), F32)

    return pl.pallas_call(
        body, name="gla_bwd", grid=(nb,),
        in_specs=[row(256), row(256), row(512), row(256),
                  pl.BlockSpec((ncb, GLA_HEADS, GLA_DV, 128), lambda i: (nb - 1 - i, 0, 0, 0)), row(512)],
        out_specs=[row(256), row(256), row(512), row(256)],
        out_shape=[rshape(256), rshape(256), rshape(512), rshape(256)],
        scratch_shapes=[pltpu.VMEM((GLA_HEADS, GLA_DV, 128), F32)],
        compiler_params=_cparams(1),
    )(gq, gk, gv, la, ss, do)


SWA_G = SWA_QH // SWA_KVH


def _swa_mask(n):
    r = lax.broadcasted_iota(jnp.int32, (SWA_G * BLK, 3 * BLK), 0) % BLK
    c = lax.broadcasted_iota(jnp.int32, (SWA_G * BLK, 3 * BLK), 1)
    seg = c // BLK
    cc = c % BLK
    qpos = n * BLK + r - PAD
    kpos = jnp.where(seg == 0, (n - 1) * BLK, jnp.where(seg == 1, n * BLK, 0)) + cc - PAD
    band = (seg < 2) & (kpos >= N_META) & (kpos <= qpos) & (qpos - kpos < WINDOW)
    meta = (seg == 2) & (kpos >= 0) & (kpos < N_META) & (kpos <= qpos)
    return band | meta


def _swa_stack(ref, kh, lo, dtype):
    parts = []
    for g in range(2):
        pair = ref[:, 128 * (2 * kh + g):128 * (2 * kh + g + 1)]
        zero = jnp.zeros_like(pair)
        parts += [jnp.where(lo, pair, zero), jnp.where(lo, zero, pair)]
    return jnp.concatenate(parts, axis=0).astype(dtype)


def _swa_unstack(x4, lo):
    return [jnp.where(lo, x4[2 * g * BLK:(2 * g + 1) * BLK], x4[(2 * g + 1) * BLK:(2 * g + 2) * BLK])
            for g in range(2)]


def _swa_sink_col(sink_ref, kh):
    blk = lax.broadcasted_iota(jnp.int32, (SWA_G * BLK, 1), 0) // BLK
    col = jnp.full((SWA_G * BLK, 1), sink_ref[SWA_G * kh + SWA_G - 1], F32)
    for e in reversed(range(SWA_G - 1)):
        col = jnp.where(blk == e, sink_ref[SWA_G * kh + e], col)
    return col


def _swa_probs(q4, kall, mask, sink):
    s = _dg(q4, kall, NT) * (SWA_HD ** -0.5)
    s = jnp.where(mask, s, NEG_INF)
    m = jnp.maximum(jnp.max(s, axis=-1, keepdims=True), sink)
    p = jnp.exp(s - m)
    es = jnp.exp(sink - m)
    inv = 1.0 / (jnp.sum(p, axis=-1, keepdims=True) + es)
    return p * inv, es * inv


def _swa_fwd(sinks, sq, sk, sv):
    t = sq.shape[0]
    nb = t // BLK

    def body(sink_ref, q_ref, kp_ref, kc_ref, km_ref, vp_ref, vc_ref, vm_ref, o_ref):
        n = pl.program_id(0)
        mask = _swa_mask(n)
        lo = lax.broadcasted_iota(jnp.int32, (BLK, 128), 1) < 64
        for kh in range(SWA_KVH):
            ls = slice(128 * kh, 128 * (kh + 1))
            kall = jnp.concatenate([kp_ref[:, ls], kc_ref[:, ls], km_ref[:, ls]], axis=0)
            vall = jnp.concatenate([vp_ref[:, ls], vc_ref[:, ls], vm_ref[:, ls]], axis=0)
            p, _ = _swa_probs(_swa_stack(q_ref, kh, lo, BF16), kall, mask, _swa_sink_col(sink_ref, kh))
            for g, pair in enumerate(_swa_unstack(_dot(p.astype(BF16), vall), lo)):
                o_ref[:, 128 * (2 * kh + g):128 * (2 * kh + g + 1)] = pair

    cur = lambda w: pl.BlockSpec((BLK, w), lambda i: (i, 0))
    prev = lambda w: pl.BlockSpec((BLK, w), lambda i: (jnp.maximum(i - 1, 0), 0))
    first = lambda w: pl.BlockSpec((BLK, w), lambda i: (0, 0))
    return pl.pallas_call(
        body, name="swa_fwd", grid=(nb,),
        in_specs=[pl.BlockSpec(memory_space=pltpu.SMEM), cur(512), prev(256), cur(256), first(256),
                  prev(256), cur(256), first(256)],
        out_specs=cur(512),
        out_shape=jax.ShapeDtypeStruct((t, SWA_W), F32),
        compiler_params=_cparams(1),
    )(sinks, sq, sk, sk, sk, sv, sv, sv)


def _swa_bwd(sinks, sq, sk, sv, o, do, hook=None):
    t = sq.shape[0]
    nb = t // BLK

    def body(sink_ref, q_ref, kp_ref, kc_ref, km_ref, vp_ref, vc_ref, vm_ref, o_ref, do_ref,
             dq_ref, dk_ref, dv_ref, dkm_ref, dvm_ref, dsink_ref, ck_ref, cv_ref):
        n = pl.program_id(0)

        @pl.when(n == 0)
        def _():
            ck_ref[...] = jnp.zeros_like(ck_ref)
            cv_ref[...] = jnp.zeros_like(cv_ref)
            dkm_ref[...] = jnp.zeros_like(dkm_ref)
            dvm_ref[...] = jnp.zeros_like(dvm_ref)
            dsink_ref[...] = jnp.zeros_like(dsink_ref)

        @pl.when(n == nb)
        def _():
            dk_ref[...] = ck_ref[...]
            dv_ref[...] = cv_ref[...]

        @pl.when(n < nb)
        def _():
            mask = _swa_mask(n)
            lo = lax.broadcasted_iota(jnp.int32, (BLK, 128), 1) < 64
            scale = SWA_HD ** -0.5
            for kh in range(SWA_KVH):
                ls = slice(128 * kh, 128 * (kh + 1))
                kall = jnp.concatenate([kp_ref[:, ls], kc_ref[:, ls], km_ref[:, ls]], axis=0)
                vall = jnp.concatenate([vp_ref[:, ls], vc_ref[:, ls], vm_ref[:, ls]], axis=0)
                q4 = _swa_stack(q_ref, kh, lo, BF16)
                do4 = _swa_stack(do_ref, kh, lo, F32)
                p, psink = _swa_probs(q4, kall, mask, _swa_sink_col(sink_ref, kh))
                delta = jnp.sum(do4 * _swa_stack(o_ref, kh, lo, F32), axis=-1, keepdims=True)
                do4b = do4.astype(BF16)
                ds = (p * (_dg(do4b, vall, NT) - delta) * scale).astype(BF16)
                for g, pair in enumerate(_swa_unstack(_dot(ds, kall), lo)):
                    dq_ref[:, 128 * (2 * kh + g):128 * (2 * kh + g + 1)] = pair
                dkall = _dg(ds, q4, TN)
                dvall = _dg(p.astype(BF16), do4b, TN)
                dsk = psink * delta
                for e in range(SWA_G):
                    h = SWA_G * kh + e
                    dsink_ref[h:h + 1, :] += jnp.broadcast_to(
                        -jnp.sum(dsk[e * BLK:(e + 1) * BLK], axis=0, keepdims=True), (1, 128))
                dk_ref[:, ls] = ck_ref[:, ls] + dkall[0:BLK]
                dv_ref[:, ls] = cv_ref[:, ls] + dvall[0:BLK]
                ck_ref[:, ls] = dkall[BLK:2 * BLK]
                cv_ref[:, ls] = dvall[BLK:2 * BLK]
                dkm_ref[:, ls] += dkall[2 * BLK:3 * BLK]
                dvm_ref[:, ls] += dvall[2 * BLK:3 * BLK]

    cur = lambda w: pl.BlockSpec((BLK, w), lambda i: (jnp.minimum(i, nb - 1), 0))
    prev = lambda w: pl.BlockSpec((BLK, w), lambda i: (jnp.maximum(i - 1, 0), 0))
    first = lambda w: pl.BlockSpec((BLK, w), lambda i: (0, 0))
    return _pallas(
        body, name="swa_bwd", grid=(nb + 1,),
        in_specs=[pl.BlockSpec(memory_space=pltpu.SMEM), cur(512), prev(256), cur(256), first(256),
                  prev(256), cur(256), first(256), cur(512), cur(512)],
        out_specs=[cur(512), prev(256), prev(256), first(256), first(256), _full((SWA_QH, 128))],
        out_shape=[jax.ShapeDtypeStruct((t, SWA_W), F32), jax.ShapeDtypeStruct((t, 256), F32),
                   jax.ShapeDtypeStruct((t, 256), F32), jax.ShapeDtypeStruct((BLK, 256), F32),
                   jax.ShapeDtypeStruct((BLK, 256), F32), jax.ShapeDtypeStruct((SWA_QH, 128), F32)],
        scratch_shapes=[pltpu.VMEM((BLK, 256), F32), pltpu.VMEM((BLK, 256), F32)],
        args=(sinks, sq, sk, sk, sk, sv, sv, sv, o, do), hook=hook)


def _mix_out(h1, ogla, gg, oswa, wgn, wsn, wout, wpost):
    t = h1.shape[0]
    tm = _row_tile(t)

    def body(h_ref, og_ref, gg_ref, os_ref, wgn_ref, wsn_ref, wout_ref, wpost_ref, h2_ref, cat_ref, m_ref):
        parts = []
        for h in range(GLA_HEADS):
            ls = slice(GLA_DV * h, GLA_DV * (h + 1))
            y, _, _ = _rms(og_ref[:, ls], wgn_ref[...])
            g = gg_ref[:, ls]
            parts.append(y * (g * _sigmoid(g)))
        ys, _, _ = _rms(os_ref[...], wsn_ref[...])
        cat = jnp.concatenate(parts + [ys], axis=1).astype(BF16)
        cat_ref[...] = cat
        m = _dot(cat, wout_ref[...])
        m_ref[...] = m
        y, _, _ = _rms(m, wpost_ref[...])
        h2_ref[...] = h_ref[...] + y

    def row(w):
        return pl.BlockSpec((tm, w), lambda i: (i, 0))

    return pl.pallas_call(
        body, name="mix_out", grid=(t // tm,),
        in_specs=[row(D_MODEL), row(512), row(512), row(512), _full((1, GLA_DV)), _full((1, SWA_W)),
                  _full((D_MODEL, D_MODEL)), _full((1, D_MODEL))],
        out_specs=[row(D_MODEL), row(D_MODEL), row(D_MODEL)],
        out_shape=[jax.ShapeDtypeStruct((t, D_MODEL), F32), jax.ShapeDtypeStruct((t, D_MODEL), BF16),
                   jax.ShapeDtypeStruct((t, D_MODEL), F32)],
        compiler_params=_cparams(1),
    )(h1, ogla, gg, oswa, wgn, wsn, wout, wpost)


def _mix_out_bwd(dh2, m, ogla, gg, oswa, wgn, wsn, wout, wpost, hook=None):
    t = dh2.shape[0]
    tm = _row_tile(t)

    def body(dh_ref, m_ref, og_ref, gg_ref, os_ref, wgn_ref, wsn_ref, wout_ref, wpost_ref,
             dog_ref, dgg_ref, dos_ref, dm_ref, dwpost_ref, dwgn_ref, dwsn_ref):
        @pl.when(pl.program_id(0) == 0)
        def _():
            dwpost_ref[...] = jnp.zeros_like(dwpost_ref)
            dwgn_ref[...] = jnp.zeros_like(dwgn_ref)
            dwsn_ref[...] = jnp.zeros_like(dwsn_ref)

        wpost = wpost_ref[...]
        _, mh, r = _rms(m_ref[...], wpost)
        dm, dw = _rms_bwd(mh, r, wpost, dh_ref[...])
        dwpost_ref[...] += dw
        dmb = dm.astype(BF16)
        dm_ref[...] = dmb
        dcat = _dg(dmb, wout_ref[...], NT)
        wgn = wgn_ref[...]
        for h in range(GLA_HEADS):
            ls = slice(GLA_DV * h, GLA_DV * (h + 1))
            dog = dcat[:, ls]
            g = gg_ref[:, ls]
            sg = _sigmoid(g)
            y, xh, r = _rms(og_ref[:, ls], wgn)
            dgg_ref[:, ls] = dog * y * (sg * (1.0 + g * (1.0 - sg)))
            dx, dw = _rms_bwd(xh, r, wgn, dog * (g * sg))
            dog_ref[:, ls] = dx
            dwgn_ref[...] += dw
        wsn = wsn_ref[...]
        _, xh, r = _rms(os_ref[...], wsn)
        dx, dw = _rms_bwd(xh, r, wsn, dcat[:, GLA_W:])
        dos_ref[...] = dx
        dwsn_ref[...] += dw

    def row(w):
        return pl.BlockSpec((tm, w), lambda i: (i, 0))

    def rshape(w, dt=F32):
        return jax.ShapeDtypeStruct((t, w), dt)

    return _pallas(
        body, name="mix_out_bwd", grid=(t // tm,),
        in_specs=[row(D_MODEL), row(D_MODEL), row(512), row(512), row(512), _full((1, GLA_DV)), _full((1, SWA_W)),
                  _full((D_MODEL, D_MODEL)), _full((1, D_MODEL))],
        out_specs=[row(512), row(512), row(512), row(D_MODEL), _full((1, D_MODEL)), _full((1, GLA_DV)),
                   _full((1, SWA_W))],
        out_shape=[rshape(512), rshape(512), rshape(512), rshape(D_MODEL, BF16),
                   jax.ShapeDtypeStruct((1, D_MODEL), F32), jax.ShapeDtypeStruct((1, GLA_DV), F32),
                   jax.ShapeDtypeStruct((1, SWA_W), F32)],
        args=(dh2, m, ogla, gg, oswa, wgn, wsn, wout, wpost), hook=hook)


def _mix_in_bwd(dh2, h1, wmixpre, winp, wa2p, bap, cos, sin, ga, dgq, dgk, dgv, dgg, dla, dsq, dsk, dsv, dkm, dvm):
    t = h1.shape[0]
    tm = _row_tile(t)

    def body(dh2_ref, h_ref, w_ref, win_ref, wa2_ref, ba_ref, cos_ref, sin_ref, ga_ref, dgq_ref, dgk_ref, dgv_ref,
             dgg_ref, dla_ref, dsq_ref, dsk_ref, dsv_ref, dkm_ref, dvm_ref,
             dh1_ref, dproj_ref, dw_ref, dwa2_ref, dba_ref):
        i = pl.program_id(0)

        @pl.when(i == 0)
        def _():
            dw_ref[...] = jnp.zeros_like(dw_ref)
            dwa2_ref[...] = jnp.zeros_like(dwa2_ref)
            dba_ref[...] = jnp.zeros_like(dba_ref)

        first = (i == 0).astype(F32)
        c = cos_ref[...]
        s = -sin_ref[...]
        fh = _first_half_mask(tm)
        dproj_ref[:, P_GQ:P_GK] = dgq_ref[...].astype(BF16)
        dproj_ref[:, P_GK:P_GV] = dgk_ref[...].astype(BF16)
        dproj_ref[:, P_GV:P_GG] = dgv_ref[...].astype(BF16)
        dproj_ref[:, P_GG:P_GA] = dgg_ref[...].astype(BF16)
        gab = ga_ref[...].astype(BF16)
        z = _dot(gab, wa2_ref[...]) + ba_ref[...]
        row_id = i * tm + lax.broadcasted_iota(jnp.int32, (tm, 1), 0)
        dz = jnp.where(row_id >= PAD, dla_ref[...] * (1.0 / GLA_TAU) * (1.0 - _sigmoid(z)), 0.0)
        dzb = dz.astype(BF16)
        dba_ref[...] += jnp.sum(dz, axis=0, keepdims=True)
        dwa2_ref[...] += _dg(gab, dzb, TN)
        dproj_ref[:, P_GA:P_SQ] = _dg(dzb, wa2_ref[...], NT).astype(BF16)
        for k in range(4):
            dy = dsq_ref[:, 128 * k:128 * (k + 1)]
            dproj_ref[:, P_SQ + 128 * k:P_SQ + 128 * (k + 1)] = (dy * c + _rot_half(dy, fh) * s).astype(BF16)
        for k in range(2):
            ls = slice(128 * k, 128 * (k + 1))
            dy = dsk_ref[:, ls]
            dy = jnp.concatenate([dy[:BLK] + first * dkm_ref[:, ls], dy[BLK:]], axis=0) if tm > BLK else (
                dy + first * dkm_ref[:, ls])
            dproj_ref[:, P_SK + 128 * k:P_SK + 128 * (k + 1)] = (dy * c + _rot_half(dy, fh) * s).astype(BF16)
            dv = dsv_ref[:, ls]
            dv = jnp.concatenate([dv[:BLK] + first * dvm_ref[:, ls], dv[BLK:]], axis=0) if tm > BLK else (
                dv + first * dvm_ref[:, ls])
            dproj_ref[:, P_SV + 128 * k:P_SV + 128 * (k + 1)] = dv.astype(BF16)
        dn = _dg(dproj_ref[...], win_ref[...], NT)
        w = w_ref[...]
        _, hh, r = _rms(h_ref[...], w)
        dx, dw = _rms_bwd(hh, r, w, dn)
        dw_ref[...] += dw
        dh1_ref[...] = dh2_ref[...] + dx

    def row(w):
        return pl.BlockSpec((tm, w), lambda i: (i, 0))

    return pl.pallas_call(
        body, name="mix_in_bwd", grid=(t // tm,),
        in_specs=[row(D_MODEL), row(D_MODEL), _full((1, D_MODEL)), _full((D_MODEL, P_END)), _full((128, GLA_KW)),
                  _full((1, GLA_KW)), row(128), row(128), row(128), row(256), row(256), row(512), row(512), row(256),
                  row(512), row(256), row(256), _full((BLK, 256)), _full((BLK, 256))],
        out_specs=[row(D_MODEL), row(P_END), _full((1, D_MODEL)), _full((128, GLA_KW)), _full((1, GLA_KW))],
        out_shape=[jax.ShapeDtypeStruct((t, D_MODEL), F32), jax.ShapeDtypeStruct((t, P_END), BF16),
                   jax.ShapeDtypeStruct((1, D_MODEL), F32), jax.ShapeDtypeStruct((128, GLA_KW), F32),
                   jax.ShapeDtypeStruct((1, GLA_KW), F32)],
        compiler_params=_cparams(1),
    )(dh2, h1, wmixpre, winp, wa2p, bap, cos, sin, ga, dgq, dgk, dgv, dgg, dla, dsq, dsk, dsv, dkm, dvm)


def _loss_head(h3, target):
    t = h3.shape[0]
    nb = t // BLK

    def body(h_ref, t_ref, dy_ref, loss_ref):
        n = pl.program_id(0)

        @pl.when(n == 0)
        def _():
            loss_ref[...] = jnp.zeros_like(loss_ref)
            dy_ref[...] = jnp.zeros_like(dy_ref)

        @pl.when(n > 0)
        def _():
            err = h_ref[...] - t_ref[...]
            dy_ref[...] = err * (1.0 / D_MODEL)
            part = jnp.sum(jnp.sum(err * err, axis=1, keepdims=True), axis=0, keepdims=True)
            loss_ref[...] += jnp.broadcast_to(part, (1, 128))

    return pl.pallas_call(
        body, name="loss_head", grid=(nb,),
        in_specs=[pl.BlockSpec((BLK, D_MODEL), lambda i: (i, 0)),
                  pl.BlockSpec((BLK, D_MODEL), lambda i: (jnp.maximum(i - 1, 0), 0))],
        out_specs=[pl.BlockSpec((BLK, D_MODEL), lambda i: (i, 0)), _full((1, 128))],
        out_shape=[jax.ShapeDtypeStruct((t, D_MODEL), F32), jax.ShapeDtypeStruct((1, 128), F32)],
        compiler_params=_cparams(1),
    )(h3, target)


def _adamw_update(w, g, m, v):
    m = ADAM_B1 * m + (1.0 - ADAM_B1) * g
    v = ADAM_B2 * v + (1.0 - ADAM_B2) * (g * g)
    m_hat = m / (1.0 - ADAM_B1 ** ADAM_STEP)
    v_hat = v / (1.0 - ADAM_B2 ** ADAM_STEP)
    return -ADAM_LR * (m_hat / (jnp.sqrt(v_hat) + ADAM_EPS) + ADAM_WD * w), m, v


def _adamw(w, g, m, v):
    r, c = w.shape
    tr = _div_tile(r)

    def body(w_ref, g_ref, m_ref, v_ref, d_ref, nm_ref, nv_ref):
        d_ref[...], nm_ref[...], nv_ref[...] = _adamw_update(w_ref[...], g_ref[...], m_ref[...], v_ref[...])

    spec = pl.BlockSpec((tr, c), lambda i: (i, 0))
    shape = jax.ShapeDtypeStruct((r, c), F32)
    return pl.pallas_call(
        body, name="adamw", grid=(r // tr,), in_specs=[spec] * 4, out_specs=[spec] * 3, out_shape=[shape] * 3,
        compiler_params=_cparams(1),
    )(w, g, m, v)


def _adamw_halves(w, g_mine, g_other, m, v, c_idx, half=None):
    r, c = w.shape
    tr = _div_tile(r // 2 if half is None else r)
    nth = (r // 2) // tr if half is None else None

    def body(c_ref, w_ref, gm_ref, go_ref, m_ref, v_ref, g_ref, d_ref, nm_ref, nv_ref):
        hh = pl.program_id(0) // nth if half is None else half
        g = jnp.where(hh == c_ref[0], gm_ref[...], go_ref[...])
        g_ref[...] = g
        d_ref[...], nm_ref[...], nv_ref[...] = _adamw_update(w_ref[...], g, m_ref[...], v_ref[...])

    spec = pl.BlockSpec((tr, c), lambda i, c_ref: (i, 0))
    gspec = pl.BlockSpec((tr, c), (lambda i, c_ref: (i % nth, 0)) if half is None else (lambda i, c_ref: (i, 0)))
    shape = jax.ShapeDtypeStruct((r, c), F32)
    return pl.pallas_call(
        body, name="adamw_halves",
        grid_spec=pltpu.PrefetchScalarGridSpec(
            num_scalar_prefetch=1, grid=(r // tr,), in_specs=[spec, gspec, gspec, spec, spec], out_specs=[spec] * 4),
        out_shape=[shape] * 4, compiler_params=_cparams(1),
    )(c_idx, w, g_mine, g_other, m, v)


def _place():
    x, y, c = lax.axis_index("x"), lax.axis_index("y"), lax.axis_index("c")
    chips = [(1 - x, y), (x, 1 - y), (1 - x, 1 - y)]
    return x, y, c, chips


def _remote(send_sem, recv_sem, src, dst, to):
    return pltpu.make_async_remote_copy(src_ref=src, dst_ref=dst, send_sem=send_sem, recv_sem=recv_sem,
                                        device_id=to, device_id_type=MESH)


def _half(ref_rows, c):
    h = ref_rows // 2
    return pl.ds(pl.multiple_of(c * h, 8), h)


def _own_slot(shard, q):
    return lax.dynamic_update_slice(jnp.zeros((N_CHIPS,) + shard.shape, shard.dtype), shard[None], (q, 0, 0))


class _GatherChips:
    has_mid = True

    def __init__(self, bufs):
        n = len(bufs)
        self.inputs = list(bufs)
        self.out_shape = [jax.ShapeDtypeStruct(b.shape, b.dtype) for b in bufs]
        self.aliases = [(t, t) for t in range(n)]
        self.scratch = [pltpu.SemaphoreType.DMA((n, 6)), pltpu.SemaphoreType.DMA((n, 6))]

    def start(self, ins, outs, scr):
        send, recv = scr
        x, y, c, chips = _place()
        q = 2 * x + y
        for t, (i_ref, o_ref) in enumerate(zip(ins, outs)):
            rows = _half(i_ref.shape[1], c)
            for j, (cx, cy) in enumerate(chips):
                _remote(send.at[t, j], recv.at[t, j], i_ref.at[q, rows], o_ref.at[q, rows], (cx, cy, c)).start()

    def mid(self, ins, outs, scr):
        send, recv = scr
        x, y, c, chips = _place()
        for t, o_ref in enumerate(outs):
            rows = _half(o_ref.shape[1], c)
            for j, (cx, cy) in enumerate(chips):
                slot = o_ref.at[2 * cx + cy, rows]
                _remote(send.at[t, j], recv.at[t, j], slot, slot, (cx, cy, c)).wait_recv()
                _remote(send.at[t, 3 + j], recv.at[t, 3 + j], slot, slot, (x, y, 1 - c)).start()

    def finish(self, ins, outs, scr):
        send, recv = scr
        x, y, c, chips = _place()
        for t, o_ref in enumerate(outs):
            mine, other = _half(o_ref.shape[1], c), _half(o_ref.shape[1], 1 - c)
            for j, (cx, cy) in enumerate(chips):
                slot = o_ref.at[2 * cx + cy, other]
                _remote(send.at[t, 3 + j], recv.at[t, 3 + j], slot, slot, (x, y, 1 - c)).wait_recv()
            for j, (cx, cy) in enumerate(chips):
                sent = o_ref.at[2 * cx + cy, mine]
                _remote(send.at[t, j], recv.at[t, j], sent, sent, (cx, cy, c)).wait_send()
                _remote(send.at[t, 3 + j], recv.at[t, 3 + j], sent, sent, (x, y, 1 - c)).wait_send()


class _PairExchange:
    has_mid = False
    aliases = ()

    def __init__(self, arrs):
        n = len(arrs)
        self.inputs = list(arrs)
        self.out_shape = [jax.ShapeDtypeStruct((a.shape[0], a.shape[1] // 2, a.shape[2]), a.dtype) for a in arrs]
        self.scratch = [pltpu.SemaphoreType.DMA((n,)), pltpu.SemaphoreType.DMA((n,))]

    def _copies(self, ins, outs, scr):
        send, recv = scr
        x, y, c, _ = _place()
        return [_remote(send.at[t], recv.at[t], i_ref.at[:, _half(i_ref.shape[1], 1 - c)], o_ref, (x, y, 1 - c))
                for t, (i_ref, o_ref) in enumerate(zip(ins, outs))]

    def start(self, ins, outs, scr):
        for cp in self._copies(ins, outs, scr):
            cp.start()

    def finish(self, ins, outs, scr):
        for cp in self._copies(ins, outs, scr):
            cp.wait()


class _ChipScatter:
    has_mid = False
    aliases = ()

    def __init__(self, arrs):
        n = len(arrs)
        self.inputs = list(arrs)
        self.out_shape = [jax.ShapeDtypeStruct((3,) + a.shape[1:], a.dtype) for a in arrs]
        self.scratch = [pltpu.SemaphoreType.DMA((n, 3)), pltpu.SemaphoreType.DMA((n, 3))]

    def _copies(self, ins, outs, scr):
        send, recv = scr
        x, y, c, chips = _place()
        return [_remote(send.at[t, j], recv.at[t, j], i_ref.at[2 * cx + cy], o_ref.at[j], (cx, cy, c))
                for t, (i_ref, o_ref) in enumerate(zip(ins, outs)) for j, (cx, cy) in enumerate(chips)]

    def start(self, ins, outs, scr):
        for cp in self._copies(ins, outs, scr):
            cp.start()

    def finish(self, ins, outs, scr):
        for cp in self._copies(ins, outs, scr):
            cp.wait()


class _PairShare:
    has_mid = False
    aliases = ()

    def __init__(self, arrs):
        n = len(arrs)
        self.inputs = list(arrs)
        self.out_shape = [jax.ShapeDtypeStruct(a.shape, a.dtype) for a in arrs]
        self.scratch = [pltpu.SemaphoreType.DMA((n,)), pltpu.SemaphoreType.DMA((n,))]

    def _copies(self, ins, outs, scr):
        send, recv = scr
        x, y, c, _ = _place()
        return [_remote(send.at[t], recv.at[t], i_ref, o_ref, (x, y, 1 - c))
                for t, (i_ref, o_ref) in enumerate(zip(ins, outs))]

    def start(self, ins, outs, scr):
        for cp in self._copies(ins, outs, scr):
            cp.start()

    def finish(self, ins, outs, scr):
        for cp in self._copies(ins, outs, scr):
            cp.wait()


def _comm_call(hook, name):
    n_in, n_out = len(hook.inputs), len(hook.out_shape)

    def body(*refs):
        ins, outs, scr = refs[:n_in], refs[n_in:n_in + n_out], refs[n_in + n_out:]
        hook.start(ins, outs, scr)
        if hook.has_mid:
            hook.mid(ins, outs, scr)
        hook.finish(ins, outs, scr)

    return pl.pallas_call(body, name=name, in_specs=[ANY] * n_in, out_specs=[ANY] * n_out,
                          out_shape=list(hook.out_shape), scratch_shapes=list(hook.scratch),
                          input_output_aliases=dict(hook.aliases))(*hook.inputs)


def _all_gather_devices(vec):
    r, w = vec.shape

    def body(x_ref, out_ref, send_sems, recv_sems, local_sem):
        x, y, c, chips = _place()
        me, sibling = (x, y, c), (x, y, 1 - c)

        def rows(px, py, pc):
            return out_ref.at[4 * px + 2 * py + pc]

        def copy(k, block, to, src=None):
            return pltpu.make_async_remote_copy(
                src_ref=rows(*block) if src is None else src, dst_ref=rows(*block), send_sem=send_sems.at[k],
                recv_sem=recv_sems.at[k], device_id=to, device_id_type=MESH)

        mine = pltpu.make_async_copy(x_ref, rows(*me), local_sem)
        mine.start()
        first = [copy(0, me, sibling, src=x_ref)]
        first += [copy(1 + j, me, (*chip, c), src=x_ref) for j, chip in enumerate(chips)]
        for cp in first:
            cp.start()
        passed = [copy(4 + j, (*chip, c), sibling) for j, chip in enumerate(chips)]
        for j, chip in enumerate(chips):
            copy(1 + j, (*chip, c), me).wait_recv()
            passed[j].start()
        copy(0, sibling, me).wait_recv()
        for j, chip in enumerate(chips):
            copy(4 + j, (*chip, 1 - c), me).wait_recv()
        for cp in first + passed:
            cp.wait_send()
        mine.wait()

    return pl.pallas_call(
        body, name="all_gather_devices",
        in_specs=[pl.BlockSpec(memory_space=pltpu.VMEM)], out_specs=pl.BlockSpec(memory_space=pltpu.VMEM),
        out_shape=jax.ShapeDtypeStruct((N_DEV, r, w), vec.dtype),
        scratch_shapes=[pltpu.SemaphoreType.DMA((7,)), pltpu.SemaphoreType.DMA((7,)), pltpu.SemaphoreType.DMA],
    )(vec)


def _pair_sum(g, other, c_idx):
    nq, r, w = g.shape
    h = r // 2
    tr = _div_tile(h)
    nt = h // tr

    def body(c_ref, g_ref, o_ref, s_ref):
        s_ref[...] = (g_ref[...].astype(F32) + o_ref[...].astype(F32)).astype(s_ref.dtype)

    return pl.pallas_call(
        body, name="pair_sum",
        grid_spec=pltpu.PrefetchScalarGridSpec(
            num_scalar_prefetch=1, grid=(nq, nt),
            in_specs=[pl.BlockSpec((None, tr, w), lambda k, i, c_ref: (k, c_ref[0] * nt + i, 0)),
                      pl.BlockSpec((None, tr, w), lambda k, i, c_ref: (k, i, 0))],
            out_specs=pl.BlockSpec((None, tr, w), lambda k, i, c_ref: (k, i, 0))),
        out_shape=jax.ShapeDtypeStruct((nq, h, w), g.dtype),
        compiler_params=_cparams(2),
    )(c_idx, g, other)


def _chip_sum(s, others, q_idx):
    _, h, w = s.shape
    tr = _div_tile(h)

    def body(q_ref, s_ref, o_ref, out_ref):
        out_ref[...] = ((s_ref[...].astype(F32) + o_ref[0].astype(F32)) + o_ref[1].astype(F32)) + o_ref[2].astype(F32)

    return pl.pallas_call(
        body, name="chip_sum",
        grid_spec=pltpu.PrefetchScalarGridSpec(
            num_scalar_prefetch=1, grid=(h // tr,),
            in_specs=[pl.BlockSpec((None, tr, w), lambda i, q_ref: (q_ref[0], i, 0)),
                      pl.BlockSpec((3, tr, w), lambda i, q_ref: (0, i, 0))],
            out_specs=pl.BlockSpec((tr, w), lambda i, q_ref: (i, 0))),
        out_shape=jax.ShapeDtypeStruct((h, w), F32),
        compiler_params=_cparams(1),
    )(q_idx, s, others)


def _sum_devices(parts):
    nd, r, w = parts.shape

    def body(p_ref, o_ref):
        acc = p_ref[0]
        for k in range(1, nd):
            acc = acc + p_ref[k]
        o_ref[...] = acc

    return pl.pallas_call(
        body, name="sum_devices", in_specs=[_full((nd, r, w))], out_specs=_full((r, w)),
        out_shape=jax.ShapeDtypeStruct((r, w), parts.dtype), grid=(1,), compiler_params=_cparams(1),
    )(parts)


def _pack_win(w_in):
    o = np.cumsum((0,) + IN_SPLITS)
    gq, gk, gv, gg, ga, sq, sk, sv = [w_in[:, o[i]:o[i + 1]] for i in range(8)]
    z = jnp.zeros((w_in.shape[0], 128 - GLA_RANK), w_in.dtype)
    dup = lambda a: jnp.concatenate([a[:, :64], a[:, :64], a[:, 64:], a[:, 64:]], axis=1)
    return jnp.concatenate([gq, gk, gv, gg, ga, z, sq, dup(sk), dup(sv)], axis=1)


def _unpack_dwin(d):
    und = lambda a: jnp.concatenate([a[:, 0:64] + a[:, 64:128], a[:, 128:192] + a[:, 192:256]], axis=1)
    return jnp.concatenate([d[:, :P_GA], d[:, P_GA:P_GA + GLA_RANK], d[:, P_SQ:P_SK], und(d[:, P_SK:P_SV]),
                            und(d[:, P_SV:P_END])], axis=1)


def _local_step(x, target, meta, p):
    s = x.shape[0]
    t = s + BLK
    h0 = jnp.concatenate([jnp.zeros((PAD, D_MODEL), F32), meta, x], axis=0)
    cos, sin = _rope_tables(t)

    h1, n1, g1, u1, a1, f1 = _ffn_fwd(h0, p["ffn1_pre_norm"], p["ffn1_w_gate"], p["ffn1_w_up"], p["ffn1_w_down"],
                                      p["ffn1_post_norm"])
    n2, gq, gk, gv, gg, ga, la, sq, sk, sv = _mix_proj(h1, p["mix_pre_norm"], p["w_in"], p["gla_w_a2"], p["gla_b_a"],
                                                       cos, sin)
    ogla, ss = _gla_fwd(gq, gk, gv, la)
    oswa = _swa_fwd(p["swa_sinks"], sq, sk, sv)
    h2, cat, m = _mix_out(h1, ogla, gg, oswa, p["gla_out_norm"], p["swa_out_norm"], p["w_out"], p["mix_post_norm"])
    h3, n3, g3, u3, a3, f3 = _ffn_fwd(h2, p["ffn2_pre_norm"], p["ffn2_w_gate"], p["ffn2_w_up"], p["ffn2_w_down"],
                                      p["ffn2_post_norm"])
    dy, sse = _loss_head(h3, target)

    grads = {}
    dh2, df3, dg3, du3, grads["ffn2_pre_norm"], grads["ffn2_post_norm"] = _ffn_bwd(
        dy, h2, f3, g3, u3, p["ffn2_pre_norm"], p["ffn2_w_gate"], p["ffn2_w_up"], p["ffn2_w_down"],
        p["ffn2_post_norm"])
    gu, grads["ffn2_w_down"] = _ffn_wgrad(n3, df3, dg3, du3, a3)
    grads["ffn2_w_gate"], grads["ffn2_w_up"] = gu[:, :D_MODEL], gu[:, D_MODEL:]

    dogla, dgg, doswa, dm, grads["mix_post_norm"], grads["gla_out_norm"], grads["swa_out_norm"] = _mix_out_bwd(
        dh2, m, ogla, gg, oswa, p["gla_out_norm"], p["swa_out_norm"], p["w_out"], p["mix_post_norm"])
    grads["w_out"] = _xty(cat, dm)
    dsq, dsk, dsv, dkm, dvm, dsinks = _swa_bwd(p["swa_sinks"], sq, sk, sv, oswa, doswa)
    grads["swa_sinks"] = dsinks[:, 0]
    dgq, dgk, dgv, dla = _gla_bwd(gq, gk, gv, la, ss, dogla)
    dh1, dproj, grads["mix_pre_norm"], dwa2p, grads["gla_b_a"] = _mix_in_bwd(
        dh2, h1, p["mix_pre_norm"], p["w_in"], p["gla_w_a2"], p["gla_b_a"], cos, sin, ga, dgq, dgk, dgv, dgg, dla,
        dsq, dsk, dsv, dkm, dvm)
    grads["gla_w_a2"] = dwa2p[:GLA_RANK]
    grads["w_in"] = _unpack_dwin(_xty(n2, dproj))

    dh0, df1, dg1, du1, grads["ffn1_pre_norm"], grads["ffn1_post_norm"] = _ffn_bwd(
        dh1, h0, f1, g1, u1, p["ffn1_pre_norm"], p["ffn1_w_gate"], p["ffn1_w_up"], p["ffn1_w_down"],
        p["ffn1_post_norm"])
    gu, grads["ffn1_w_down"] = _ffn_wgrad(n1, df1, dg1, du1, a1)
    grads["ffn1_w_gate"], grads["ffn1_w_up"] = gu[:, :D_MODEL], gu[:, D_MODEL:]
    grads["meta_tokens"] = dh0[PAD:BLK]
    return sse[0, 0], dh0[BLK:], grads


WEIGHTS = ['meta_tokens', 'ffn1_pre_norm', 'ffn1_w_gate', 'ffn1_w_up', 'ffn1_w_down', 'ffn1_post_norm',
           'mix_pre_norm', 'w_in', 'gla_w_a2', 'gla_b_a', 'gla_out_norm', 'swa_sinks', 'swa_out_norm', 'w_out',
           'mix_post_norm', 'ffn2_pre_norm', 'ffn2_w_gate', 'ffn2_w_up', 'ffn2_w_down', 'ffn2_post_norm']
BIG = ['ffn1_w_gate', 'ffn1_w_up', 'ffn1_w_down', 'w_in', 'w_out', 'ffn2_w_gate', 'ffn2_w_up', 'ffn2_w_down']
SMALL = [n for n in WEIGHTS if n not in BIG]
FJ = D_FF // N_CHIPS
D_IN_J = D_IN // N_CHIPS
D_OUT_J = D_MODEL // N_CHIPS
BIG_SHARD = {'ffn1_w_gate': (D_MODEL, FJ), 'ffn1_w_up': (D_MODEL, FJ), 'ffn1_w_down': (FJ, D_MODEL),
             'w_in': (D_MODEL, D_IN_J), 'w_out': (D_OUT_J, D_MODEL),
             'ffn2_w_gate': (D_MODEL, FJ), 'ffn2_w_up': (D_MODEL, FJ), 'ffn2_w_down': (FJ, D_MODEL)}


def _small_rows(name, a):
    flat = a.reshape(-1)
    rows = -(-flat.shape[0] // 1024) * 8
    return jnp.pad(flat, (0, rows * 128 - flat.shape[0])).reshape(rows, 128)


def kernel(x, meta_tokens, ffn1_pre_norm, ffn1_w_gate, ffn1_w_up, ffn1_w_down, ffn1_post_norm, mix_pre_norm, w_in, gla_w_a2, gla_b_a, gla_out_norm, swa_sinks, swa_out_norm, w_out, mix_post_norm, ffn2_pre_norm, ffn2_w_gate, ffn2_w_up, ffn2_w_down, ffn2_post_norm, loss_target, m_meta_tokens, m_ffn1_pre_norm, m_ffn1_w_gate, m_ffn1_w_up, m_ffn1_w_down, m_ffn1_post_norm, m_mix_pre_norm, m_w_in, m_gla_w_a2, m_gla_b_a, m_gla_out_norm, m_swa_sinks, m_swa_out_norm, m_w_out, m_mix_post_norm, m_ffn2_pre_norm, m_ffn2_w_gate, m_ffn2_w_up, m_ffn2_w_down, m_ffn2_post_norm, v_meta_tokens, v_ffn1_pre_norm, v_ffn1_w_gate, v_ffn1_w_up, v_ffn1_w_down, v_ffn1_post_norm, v_mix_pre_norm, v_w_in, v_gla_w_a2, v_gla_b_a, v_gla_out_norm, v_swa_sinks, v_swa_out_norm, v_w_out, v_mix_post_norm, v_ffn2_pre_norm, v_ffn2_w_gate, v_ffn2_w_up, v_ffn2_w_down, v_ffn2_post_norm):
    args = dict(locals())
    w = {n: args[n] for n in WEIGHTS}
    mom = {n: args["m_" + n] for n in WEIGHTS}
    var = {n: args["v_" + n] for n in WEIGHTS}
    cx, cy, cc = lax.axis_index("x"), lax.axis_index("y"), lax.axis_index("c")
    q_idx = (2 * cx + cy).astype(jnp.int32).reshape(1)
    c_idx = cc.astype(jnp.int32).reshape(1)

    q_chip = 2 * cx + cy
    bf = {n: _own_slot(w[n].reshape(BIG_SHARD[n]).astype(BF16), q_chip) for n in BIG}
    early = _GatherChips([bf["ffn1_w_gate"], bf["ffn1_w_up"], bf["ffn1_w_down"], _own_slot(w["meta_tokens"], q_chip),
                          _own_slot(w["gla_w_a2"].reshape(GLA_RANK, GLA_KW // N_CHIPS), q_chip)])
    wg1, wu1, wd1, meta4, wa24 = _comm_call(early, "gather_ffn1")
    meta_full = meta4.transpose(1, 0, 2).reshape(N_META, D_MODEL)
    wa2p = jnp.pad(wa24.transpose(1, 0, 2).reshape(GLA_RANK, GLA_KW), ((0, 128 - GLA_RANK), (0, 0))).astype(BF16)
    sinks = w["swa_sinks"].reshape(SWA_QH)

    seq, target = x[0], loss_target[0]
    t = seq.shape[0] + BLK
    h0 = jnp.concatenate([jnp.zeros((PAD, D_MODEL), F32), meta_full, seq], axis=0)
    cos, sin = _rope_tables(t)
    late = _GatherChips([bf["w_in"], bf["w_out"], bf["ffn2_w_gate"], bf["ffn2_w_up"], bf["ffn2_w_down"]])
    (h1, n1, g1, u1, a1, f1), (win4, wout4, wg2, wu2, wd2) = _ffn_fwd(
        h0, w["ffn1_pre_norm"], wg1, wu1, wd1, w["ffn1_post_norm"], hook=late)
    winp = _pack_win(win4.transpose(1, 0, 2).reshape(D_MODEL, D_IN))
    wout = wout4.reshape(D_MODEL, D_MODEL)
    n2, gq, gk, gv, gg, ga, la, sq, sk, sv = _mix_proj(h1, w["mix_pre_norm"], winp, wa2p, w["gla_b_a"], cos, sin)
    ogla, ss = _gla_fwd(gq, gk, gv, la)
    oswa = _swa_fwd(sinks, sq, sk, sv)
    h2, cat, m = _mix_out(h1, ogla, gg, oswa, w["gla_out_norm"], w["swa_out_norm"], wout, w["mix_post_norm"])
    h3, n3, g3, u3, a3, f3 = _ffn_fwd(h2, w["ffn2_pre_norm"], wg2, wu2, wd2, w["ffn2_post_norm"])
    dy, sse = _loss_head(h3, target)
    loss = lax.psum(sse[0, 0] * (0.5 / D_MODEL), ("x", "y", "c"))

    g = {}
    dh2, df3, dg3, du3, g["ffn2_pre_norm"], g["ffn2_post_norm"] = _ffn_bwd(
        dy, h2, f3, g3, u3, w["ffn2_pre_norm"], wg2, wu2, wd2, w["ffn2_post_norm"])
    gu2, gd2 = _ffn_wgrad(n3, df3, dg3, du3, a3)
    (dogla, dgg, doswa, dm, g["mix_post_norm"], g["gla_out_norm"], g["swa_out_norm"]), (rgu2, rgd2) = _mix_out_bwd(
        dh2, m, ogla, gg, oswa, w["gla_out_norm"], w["swa_out_norm"], wout, w["mix_post_norm"],
        hook=_PairExchange([gu2, gd2]))
    sgu2, sgd2 = _pair_sum(gu2, rgu2, c_idx), _pair_sum(gd2, rgd2, c_idx)
    gout = _xty(cat, dm).reshape(N_CHIPS, D_OUT_J, D_MODEL).astype(BF16)
    (dsq, dsk, dsv, dkm, dvm, dsinks), (ogu2, ogd2) = _swa_bwd(sinks, sq, sk, sv, oswa, doswa,
                                                               hook=_ChipScatter([sgu2, sgd2]))
    g["swa_sinks"] = dsinks[:, 0].reshape(1, SWA_QH)
    dgq, dgk, dgv, dla = _gla_bwd(gq, gk, gv, la, ss, dogla)
    dh1, dproj, g["mix_pre_norm"], dwa2p, g["gla_b_a"] = _mix_in_bwd(
        dh2, h1, w["mix_pre_norm"], winp, wa2p, w["gla_b_a"], cos, sin, ga, dgq, dgk, dgv, dgg, dla,
        dsq, dsk, dsv, dkm, dvm)
    g["gla_w_a2"] = dwa2p[:GLA_RANK]
    gin = _unpack_dwin(_xty(n2, dproj)).reshape(D_MODEL, N_CHIPS, D_IN_J).transpose(1, 0, 2).astype(BF16)
    (dh0, df1, dg1, du1, g["ffn1_pre_norm"], g["ffn1_post_norm"]), (rgin, rgout) = _ffn_bwd(
        dh1, h0, f1, g1, u1, w["ffn1_pre_norm"], wg1, wu1, wd1, w["ffn1_post_norm"],
        hook=_PairExchange([gin, gout]))
    sgin, sgout = _pair_sum(gin, rgin, c_idx), _pair_sum(gout, rgout, c_idx)
    (gu1, gd1), (ogin, ogout) = _ffn_wgrad(n1, df1, dg1, du1, a1, hook=_ChipScatter([sgin, sgout]))
    g["meta_tokens"] = dh0[PAD:BLK]
    grad_x = dh0[BLK:]
    rgu1, rgd1 = _comm_call(_PairExchange([gu1, gd1]), "pair_exchange_ffn1")
    sgu1, sgd1 = _pair_sum(gu1, rgu1, c_idx), _pair_sum(gd1, rgd1, c_idx)
    ogu1, ogd1 = _comm_call(_ChipScatter([sgu1, sgd1]), "chip_scatter_ffn1")
    halves = [_chip_sum(s, o, q_idx) for s, o in ((sgu1, ogu1), (sgd1, ogd1), (sgin, ogin), (sgout, ogout),
                                                   (sgu2, ogu2), (sgd2, ogd2))]
    others = _comm_call(_PairShare(halves), "pair_share")
    reduced = {"ffn1_w_gate": (0, 0), "ffn1_w_up": (0, 1), "ffn1_w_down": (1, None), "w_in": (2, None),
               "w_out": (3, None), "ffn2_w_gate": (4, 0), "ffn2_w_up": (4, 1), "ffn2_w_down": (5, None)}
    grad, delta, new_m, new_v = {}, {}, {}, {}
    for n in BIG:
        shard = BIG_SHARD[n]
        k, half = reduced[n]
        outs = _adamw_halves(w[n].reshape(shard), halves[k], others[k], mom[n].reshape(shard), var[n].reshape(shard),
                             c_idx, half)
        grad[n], delta[n], new_m[n], new_v[n] = [a.reshape(w[n].shape) for a in outs]

    small_rows = [_small_rows(n, g[n]) for n in SMALL]
    ssizes = [a.shape[0] for a in small_rows]
    gsmall = _sum_devices(_all_gather_devices(jnp.concatenate(small_rows, axis=0)))
    col0 = {"meta_tokens": D_MODEL // N_CHIPS, "gla_w_a2": GLA_KW // N_CHIPS}
    gs, ws, ms, vs = [], [], [], []
    off = 0
    for n, sz in zip(SMALL, ssizes):
        full_shape = g[n].shape
        gn = gsmall[off:off + sz].reshape(-1)[:math.prod(full_shape)].reshape(full_shape)
        off += sz
        if n in col0:
            gn = lax.dynamic_slice_in_dim(gn, (2 * cx + cy) * col0[n], col0[n], axis=1)
        grad[n] = gn.reshape(w[n].shape)
        gs.append(_small_rows(n, grad[n]))
        ws.append(_small_rows(n, w[n]))
        ms.append(_small_rows(n, mom[n]))
        vs.append(_small_rows(n, var[n]))
    psizes = [a.shape[0] for a in gs]
    d, nm, nv = _adamw(*[jnp.concatenate(a, axis=0) for a in (ws, gs, ms, vs)])
    off = 0
    for n, sz in zip(SMALL, psizes):
        cnt = math.prod(w[n].shape)
        delta[n], new_m[n], new_v[n] = [a[off:off + sz].reshape(-1)[:cnt].reshape(w[n].shape) for a in (d, nm, nv)]
        off += sz

    return (loss, grad_x[None], *[grad[n] for n in WEIGHTS], *[delta[n] for n in WEIGHTS],
            *[new_m[n] for n in WEIGHTS], *[new_v[n] for n in WEIGHTS])
```

```python
import functools
import math

import numpy as np
import jax
import jax.numpy as jnp
from jax import lax
from jax.experimental import pallas as pl
from jax.experimental.pallas import tpu as pltpu

F32 = jnp.float32
BF16 = jnp.bfloat16
MESH = pl.DeviceIdType.MESH

D_MODEL = 1024
D_FF = 2816
N_CHIPS = 4
N_DEV = 8
N_META = 16
BLK = 128
PAD = BLK - N_META
GLA_CHUNK = 64
GLA_HEADS = 4
GLA_DV = 128
GLA_DK = 64
GLA_KW = GLA_HEADS * GLA_DK
GLA_W = GLA_HEADS * GLA_DV
GLA_RANK = 16
GLA_TAU = 16.0
SWA_HD = 64
SWA_QH = 8
SWA_KVH = 2
SWA_W = SWA_QH * SWA_HD
WINDOW = 128
ROPE_THETA = 10000.0
EPS = 1e-6
NEG_INF = -1e30
IN_SPLITS = (256, 256, 512, 512, 16, 512, 128, 128)
D_IN = sum(IN_SPLITS)
P_GQ, P_GK, P_GV, P_GG, P_GA, P_SQ, P_SK, P_SV, P_END = 0, 256, 512, 1024, 1536, 1664, 2176, 2432, 2688
ADAM_LR, ADAM_B1, ADAM_B2, ADAM_EPS, ADAM_WD, ADAM_STEP = 0.001, 0.9, 0.999, 1e-08, 0.01, 10
VMEM_LIMIT = 56 * 1024 * 1024

NT = (((1,), (1,)), ((), ()))
TN = (((0,), (0,)), ((), ()))


def _cparams(n_axes):
    return pltpu.CompilerParams(dimension_semantics=("arbitrary",) * n_axes, vmem_limit_bytes=VMEM_LIMIT)


def _row_tile(t):
    for tm in (640, 512, 384, 256, 128):
        if t % tm == 0:
            return tm
    raise ValueError(t)


def _div_tile(r, cap=512):
    best = None
    for tr in range(8, min(r, cap) + 1, 8):
        if r % tr == 0:
            best = tr
    return best if best is not None else r


def _dot(a, b):
    return jnp.dot(a, b, preferred_element_type=F32)


def _dg(a, b, dims):
    return lax.dot_general(a, b, dims, preferred_element_type=F32)


def _rms(x, w):
    r = lax.rsqrt(jnp.mean(x * x, axis=-1, keepdims=True) + EPS)
    xh = x * r
    return xh * w, xh, r


def _rms_bwd(xh, r, w, dy):
    wdy = dy * w
    dx = r * (wdy - xh * jnp.mean(wdy * xh, axis=-1, keepdims=True))
    dw = jnp.sum(dy * xh, axis=0, keepdims=True)
    return dx, dw


def _sigmoid(x):
    return 1.0 / (1.0 + jnp.exp(-x))


def _full(shape):
    nd = len(shape)
    return pl.BlockSpec(shape, lambda *_: (0,) * nd)


ANY = pl.BlockSpec(memory_space=pl.ANY)


def _pallas(body, *, name, grid, in_specs, out_specs, out_shape, args, scratch_shapes=(), hook=None):
    n_axes = len(grid)
    if hook is None:
        return pl.pallas_call(body, name=name, grid=grid, in_specs=list(in_specs), out_specs=list(out_specs),
                              out_shape=list(out_shape), scratch_shapes=list(scratch_shapes),
                              compiler_params=_cparams(n_axes))(*args)
    n_in, n_out, n_scr = len(in_specs), len(out_specs), len(scratch_shapes)
    h_in, h_out = len(hook.inputs), len(hook.out_shape)
    total = math.prod(grid)

    def wrapped(*refs):
        ins, hins = refs[:n_in], refs[n_in:n_in + h_in]
        o0 = n_in + h_in
        outs, houts = refs[o0:o0 + n_out], refs[o0 + n_out:o0 + n_out + h_out]
        s0 = o0 + n_out + h_out
        scr, hscr = refs[s0:s0 + n_scr], refs[s0 + n_scr:]
        step = pl.program_id(0)
        for a in range(1, n_axes):
            step = step * grid[a] + pl.program_id(a)

        @pl.when(step == 0)
        def _():
            hook.start(hins, houts, hscr)

        body(*ins, *outs, *scr)

        if hook.has_mid:
            @pl.when(step == (3 * total) // 4)
            def _():
                hook.mid(hins, houts, hscr)

        @pl.when(step == total - 1)
        def _():
            hook.finish(hins, houts, hscr)

    res = pl.pallas_call(
        wrapped, name=name, grid=grid, in_specs=list(in_specs) + [ANY] * h_in,
        out_specs=list(out_specs) + [ANY] * h_out, out_shape=list(out_shape) + list(hook.out_shape),
        scratch_shapes=list(scratch_shapes) + list(hook.scratch), compiler_params=_cparams(n_axes),
        input_output_aliases={n_in + a: n_out + b for a, b in hook.aliases},
    )(*args, *hook.inputs)
    return res[:n_out], res[n_out:]


def _ffn_fwd(h, wpre, wg4, wu4, wd4, wpost, hook=None):
    t = h.shape[0]
    tm = _row_tile(t)
    nj, fj, _ = wg4.shape

    def body(h_ref, wpre_ref, wg_ref, wu_ref, wd_ref, wpost_ref, hout_ref, n_ref, p1_ref, p2_ref, a_ref, f_ref,
             acc_ref):
        j = pl.program_id(1)

        @pl.when(j == 0)
        def _():
            y, _, _ = _rms(h_ref[...], wpre_ref[...])
            n_ref[...] = y.astype(BF16)
            acc_ref[...] = jnp.zeros_like(acc_ref)

        n = n_ref[...]
        g = _dg(n, wg_ref[...], NT)
        u = _dg(n, wu_ref[...], NT)
        sg = _sigmoid(g)
        silu = g * sg
        p1_ref[...] = (u * (sg + silu * (1.0 - sg))).astype(BF16)
        p2_ref[...] = silu.astype(BF16)
        a = (silu * u).astype(BF16)
        a_ref[...] = a
        acc_ref[...] += _dot(a, wd_ref[...])

        @pl.when(j == nj - 1)
        def _():
            f = acc_ref[...]
            f_ref[...] = f
            y, _, _ = _rms(f, wpost_ref[...])
            hout_ref[...] = h_ref[...] + 0.5 * y

    row = pl.BlockSpec((tm, D_MODEL), lambda i, j: (i, 0))
    vec = pl.BlockSpec((1, D_MODEL), lambda i, j: (0, 0))
    wrow = pl.BlockSpec((None, fj, D_MODEL), lambda i, j: (j, 0, 0))
    act = pl.BlockSpec((None, tm, fj), lambda i, j: (j, i, 0))
    return _pallas(
        body, name="ffn_fwd", grid=(t // tm, nj),
        in_specs=[row, vec, wrow, wrow, wrow, vec],
        out_specs=[row, row, act, act, act, row],
        out_shape=[jax.ShapeDtypeStruct((t, D_MODEL), F32), jax.ShapeDtypeStruct((t, D_MODEL), BF16),
                   jax.ShapeDtypeStruct((nj, t, fj), BF16), jax.ShapeDtypeStruct((nj, t, fj), BF16),
                   jax.ShapeDtypeStruct((nj, t, fj), BF16), jax.ShapeDtypeStruct((t, D_MODEL), F32)],
        scratch_shapes=[pltpu.VMEM((tm, D_MODEL), F32)],
        args=(h, wpre, wg4, wu4, wd4, wpost), hook=hook)


def _ffn_bwd(dhout, h, f, p14, p24, wpre, wg4, wu4, wd4, wpost, hook=None):
    t = h.shape[0]
    tm = _row_tile(t)
    nj, fj, _ = wg4.shape

    def body(dhout_ref, h_ref, f_ref, p1_ref, p2_ref, wpre_ref, wg_ref, wu_ref, wd_ref, wpost_ref,
             dh_ref, df_ref, dg_ref, du_ref, dwpre_ref, dwpost_ref, dn_ref):
        i = pl.program_id(0)
        j = pl.program_id(1)

        @pl.when((i == 0) & (j == 0))
        def _():
            dwpre_ref[...] = jnp.zeros_like(dwpre_ref)
            dwpost_ref[...] = jnp.zeros_like(dwpost_ref)

        @pl.when(j == 0)
        def _():
            wpost = wpost_ref[...]
            _, fh, r = _rms(f_ref[...], wpost)
            df, dw = _rms_bwd(fh, r, wpost, 0.5 * dhout_ref[...])
            dwpost_ref[...] += dw
            df_ref[...] = df.astype(BF16)
            dn_ref[...] = jnp.zeros_like(dn_ref)

        da = _dg(df_ref[...], wd_ref[...], NT)
        dg = (da * p1_ref[...].astype(F32)).astype(BF16)
        du = (da * p2_ref[...].astype(F32)).astype(BF16)
        dg_ref[...] = dg
        du_ref[...] = du
        dn_ref[...] += _dot(dg, wg_ref[...]) + _dot(du, wu_ref[...])

        @pl.when(j == nj - 1)
        def _():
            wpre = wpre_ref[...]
            _, hh, r = _rms(h_ref[...], wpre)
            dx, dw = _rms_bwd(hh, r, wpre, dn_ref[...])
            dwpre_ref[...] += dw
            dh_ref[...] = dhout_ref[...] + dx

    row = pl.BlockSpec((tm, D_MODEL), lambda i, j: (i, 0))
    vec = pl.BlockSpec((1, D_MODEL), lambda i, j: (0, 0))
    wrow = pl.BlockSpec((None, fj, D_MODEL), lambda i, j: (j, 0, 0))
    act = pl.BlockSpec((None, tm, fj), lambda i, j: (j, i, 0))
    actshape = jax.ShapeDtypeStruct((nj, t, fj), BF16)
    return _pallas(
        body, name="ffn_bwd", grid=(t // tm, nj),
        in_specs=[row, row, row, act, act, vec, wrow, wrow, wrow, vec],
        out_specs=[row, row, act, act, vec, vec],
        out_shape=[jax.ShapeDtypeStruct((t, D_MODEL), F32), jax.ShapeDtypeStruct((t, D_MODEL), BF16),
                   actshape, actshape,
                   jax.ShapeDtypeStruct((1, D_MODEL), F32), jax.ShapeDtypeStruct((1, D_MODEL), F32)],
        scratch_shapes=[pltpu.VMEM((tm, D_MODEL), F32)],
        args=(dhout, h, f, p14, p24, wpre, wg4, wu4, wd4, wpost), hook=hook)


def _ffn_wgrad(n, df, dg4, du4, a4, hook=None):
    t = n.shape[0]
    tm = _row_tile(t)
    ni = t // tm
    nj, _, fj = dg4.shape

    def body(n_ref, df_ref, dg_ref, du_ref, a_ref, dw_ref, acc):
        i = pl.program_id(1)

        @pl.when(i == 0)
        def _():
            acc[...] = jnp.zeros_like(acc)

        nn = n_ref[...]
        acc[0:fj, :] += _dg(dg_ref[...], nn, TN)
        acc[fj:2 * fj, :] += _dg(du_ref[...], nn, TN)
        acc[2 * fj:3 * fj, :] += _dg(a_ref[...], df_ref[...], TN)

        @pl.when(i == ni - 1)
        def _():
            dw_ref[...] = acc[...].astype(BF16)

    row = pl.BlockSpec((tm, D_MODEL), lambda j, i: (i, 0))
    act = pl.BlockSpec((None, tm, fj), lambda j, i: (j, i, 0))
    return _pallas(
        body, name="ffn_wgrad", grid=(nj, ni),
        in_specs=[row, row, act, act, act],
        out_specs=[pl.BlockSpec((None, 3 * fj, D_MODEL), lambda j, i: (j, 0, 0))],
        out_shape=[jax.ShapeDtypeStruct((nj, 3 * fj, D_MODEL), BF16)],
        scratch_shapes=[pltpu.VMEM((3 * fj, D_MODEL), F32)],
        args=(n, df, dg4, du4, a4), hook=hook)


def _xty(x, y):
    t, k = x.shape
    n = y.shape[1]
    tm = _row_tile(t)
    tn = n if n <= 1024 else (896 if n % 896 == 0 else 128)

    def body(x_ref, y_ref, o_ref):
        @pl.when(pl.program_id(1) == 0)
        def _():
            o_ref[...] = jnp.zeros_like(o_ref)

        o_ref[...] += _dg(x_ref[...], y_ref[...], TN)

    return pl.pallas_call(
        body, name="xty", grid=(n // tn, t // tm),
        in_specs=[pl.BlockSpec((tm, k), lambda j, i: (i, 0)), pl.BlockSpec((tm, tn), lambda j, i: (i, j))],
        out_specs=pl.BlockSpec((k, tn), lambda j, i: (0, j)),
        out_shape=jax.ShapeDtypeStruct((k, n), F32),
        compiler_params=_cparams(2),
    )(x, y)


def _rope_tables(t):
    pos = (jnp.arange(t, dtype=jnp.int32) - PAD).astype(F32)
    inv_freq = 1.0 / (ROPE_THETA ** (jnp.arange(0, SWA_HD, 2, dtype=F32) / SWA_HD))
    ang = pos[:, None] * inv_freq[None, :]
    cos = jnp.cos(ang)
    sin = jnp.sin(ang)
    return jnp.concatenate([cos, cos, cos, cos], axis=1), jnp.concatenate([-sin, sin, -sin, sin], axis=1)


def _rot_half(x, first_half):
    return jnp.where(first_half, pltpu.roll(x, 96, 1), pltpu.roll(x, 32, 1))


def _first_half_mask(rows):
    lane = lax.broadcasted_iota(jnp.int32, (rows, 128), 1)
    return (lane % 64) < 32


def _log_sigmoid(z):
    return jnp.minimum(z, 0.0) - jnp.log(1.0 + jnp.exp(-jnp.abs(z)))


def _mix_proj(h1, wmixpre, winp, wa2p, bap, cos, sin):
    t = h1.shape[0]
    tm = _row_tile(t)

    def body(h_ref, w_ref, win_ref, wa2_ref, ba_ref, cos_ref, sin_ref,
             n_ref, gq_ref, gk_ref, gv_ref, gg_ref, ga_ref, la_ref, sq_ref, sk_ref, sv_ref):
        y, _, _ = _rms(h_ref[...], w_ref[...])
        n = y.astype(BF16)
        n_ref[...] = n
        proj = _dot(n, win_ref[...])
        gq_ref[...] = proj[:, P_GQ:P_GK]
        gk_ref[...] = proj[:, P_GK:P_GV]
        gv_ref[...] = proj[:, P_GV:P_GG]
        gg_ref[...] = proj[:, P_GG:P_GA]
        ga = proj[:, P_GA:P_SQ]
        ga_ref[...] = ga
        z = _dot(ga.astype(BF16), wa2_ref[...]) + ba_ref[...]
        la_ref[...] = _log_sigmoid(z) * (1.0 / GLA_TAU)
        c = cos_ref[...]
        s = sin_ref[...]
        fh = _first_half_mask(tm)
        for k in range(4):
            x = proj[:, P_SQ + 128 * k:P_SQ + 128 * (k + 1)]
            sq_ref[:, 128 * k:128 * (k + 1)] = (x * c + _rot_half(x, fh) * s).astype(BF16)
        for k in range(2):
            x = proj[:, P_SK + 128 * k:P_SK + 128 * (k + 1)]
            sk_ref[:, 128 * k:128 * (k + 1)] = (x * c + _rot_half(x, fh) * s).astype(BF16)
        sv_ref[...] = proj[:, P_SV:P_END].astype(BF16)

    def row(w):
        return pl.BlockSpec((tm, w), lambda i: (i, 0))

    def rshape(w, dt):
        return jax.ShapeDtypeStruct((t, w), dt)

    return pl.pallas_call(
        body, name="mix_proj", grid=(t // tm,),
        in_specs=[row(D_MODEL), _full((1, D_MODEL)), _full((D_MODEL, P_END)), _full((128, GLA_KW)),
                  _full((1, GLA_KW)), row(128), row(128)],
        out_specs=[row(D_MODEL), row(256), row(256), row(512), row(512), row(128), row(256), row(512), row(256),
                   row(256)],
        out_shape=[rshape(D_MODEL, BF16), rshape(256, F32), rshape(256, F32), rshape(512, F32), rshape(512, F32),
                   rshape(128, F32), rshape(256, F32), rshape(512, BF16), rshape(256, BF16), rshape(256, BF16)],
        compiler_params=_cparams(1),
    )(h1, wmixpre, winp, wa2p, bap, cos, sin)


def _gla_cumsum(la, tril_f):
    b = jnp.dot(tril_f, la, precision=lax.Precision.HIGHEST, preferred_element_type=F32)
    row = lax.broadcasted_iota(jnp.int32, b.shape, 0)
    bm = jnp.sum(jnp.where(row == GLA_CHUNK // 2 - 1, b, 0.0), axis=0, keepdims=True)
    bl = jnp.sum(jnp.where(row == GLA_CHUNK - 1, b, 0.0), axis=0, keepdims=True)
    return b, bm, bl


def _gla_chunk_terms(la, q, k, tril_f):
    b, bm, bl = _gla_cumsum(la, tril_f)
    qs = q * (GLA_DK ** -0.5)
    qt = qs * jnp.exp(b - bm)
    kt = k * jnp.exp(bm - b)
    qh = qs * jnp.exp(b)
    kh = k * jnp.exp(bl - b)
    ebl = jnp.exp(bl)
    return qt, kt, qh, kh, ebl


def _gla_fwd(gq, gk, gv, la):
    t = gq.shape[0]
    nb = t // BLK
    ncb = BLK // GLA_CHUNK

    def body(q_ref, k_ref, v_ref, la_ref, o_ref, ss_ref, st_ref):
        @pl.when(pl.program_id(0) == 0)
        def _():
            st_ref[...] = jnp.zeros_like(st_ref)

        r = lax.broadcasted_iota(jnp.int32, (GLA_CHUNK, GLA_CHUNK), 0)
        c = lax.broadcasted_iota(jnp.int32, (GLA_CHUNK, GLA_CHUNK), 1)
        tril = r >= c
        tril_f = tril.astype(F32)
        lane = lax.broadcasted_iota(jnp.int32, (GLA_CHUNK, 128), 1)
        halves = (lane < 64, lane >= 64)
        for ch in range(ncb):
            rows = slice(ch * GLA_CHUNK, (ch + 1) * GLA_CHUNK)
            qt, kt, qh, kh, ebl = _gla_chunk_terms(la_ref[rows, :], q_ref[rows, :], k_ref[rows, :], tril_f)
            for hp in range(2):
                ls = slice(128 * hp, 128 * (hp + 1))
                kt2 = kt[:, ls].astype(BF16)
                kh2 = kh[:, ls].astype(BF16)
                for e in range(2):
                    h = 2 * hp + e
                    vs = slice(GLA_DV * h, GLA_DV * (h + 1))
                    st = st_ref[h]
                    ss_ref[ch, h] = st
                    v = v_ref[rows, vs].astype(BF16)
                    qtm = jnp.where(halves[e], qt[:, ls], 0.0).astype(BF16)
                    qhm = jnp.where(halves[e], qh[:, ls], 0.0).astype(BF16)
                    a = jnp.where(tril, _dg(qtm, kt2, NT), 0.0)
                    o_ref[rows, vs] = _dot(a.astype(BF16), v) + _dg(qhm, st.astype(BF16), NT)
                    st_ref[h] = st * ebl[:, ls] + _dg(v, kh2, TN)

    def row(w):
        return pl.BlockSpec((BLK, w), lambda i: (i, 0))

    return pl.pallas_call(
        body, name="gla_fwd", grid=(nb,),
        in_specs=[row(256), row(256), row(512), row(256)],
        out_specs=[row(512), pl.BlockSpec((ncb, GLA_HEADS, GLA_DV, 128), lambda i: (i, 0, 0, 0))],
        out_shape=[jax.ShapeDtypeStruct((t, GLA_W), F32),
                   jax.ShapeDtypeStruct((nb * ncb, GLA_HEADS, GLA_DV, 128), F32)],
        scratch_shapes=[pltpu.VMEM((GLA_HEADS, GLA_DV, 128), F32)],
        compiler_params=_cparams(1),
    )(gq, gk, gv, la)


def _gla_bwd(gq, gk, gv, la, ss, do):
    t = gq.shape[0]
    nb = t // BLK
    ncb = BLK // GLA_CHUNK

    def body(q_ref, k_ref, v_ref, la_ref, ss_ref, do_ref, dq_ref, dk_ref, dv_ref, dla_ref, dst_ref):
        @pl.when(pl.program_id(0) == 0)
        def _():
            dst_ref[...] = jnp.zeros_like(dst_ref)

        r = lax.broadcasted_iota(jnp.int32, (GLA_CHUNK, GLA_CHUNK), 0)
        c = lax.broadcasted_iota(jnp.int32, (GLA_CHUNK, GLA_CHUNK), 1)
        tril = r >= c
        tril_f = tril.astype(F32)
        triu_f = (r <= c).astype(F32)
        lane = lax.broadcasted_iota(jnp.int32, (GLA_CHUNK, 128), 1)
        halves = (lane < 64, lane >= 64)
        last_row = lax.broadcasted_iota(jnp.int32, (GLA_CHUNK, 128), 0) == GLA_CHUNK - 1
        lane1 = lax.broadcasted_iota(jnp.int32, (1, 128), 1)
        halves1 = (lane1 < 64, lane1 >= 64)
        scale = GLA_DK ** -0.5
        for ch in reversed(range(ncb)):
            rows = slice(ch * GLA_CHUNK, (ch + 1) * GLA_CHUNK)
            la = la_ref[rows, :]
            b, bm, bl = _gla_cumsum(la, tril_f)
            eq = jnp.exp(b - bm)
            ek = jnp.exp(bm - b)
            eb = jnp.exp(b)
            ekl = jnp.exp(bl - b)
            ebl = jnp.exp(bl)
            qs = q_ref[rows, :] * scale
            kk = k_ref[rows, :]
            qt, kt, qh, kh = qs * eq, kk * ek, qs * eb, kk * ekl
            for hp in range(2):
                ls = slice(128 * hp, 128 * (hp + 1))
                kt2 = kt[:, ls].astype(BF16)
                kh2 = kh[:, ls].astype(BF16)
                dqt = jnp.zeros((GLA_CHUNK, 128), F32)
                dkt = jnp.zeros((GLA_CHUNK, 128), F32)
                dqh = jnp.zeros((GLA_CHUNK, 128), F32)
                dkh = jnp.zeros((GLA_CHUNK, 128), F32)
                dbl = jnp.zeros((1, 128), F32)
                for e in range(2):
                    h = 2 * hp + e
                    vs = slice(GLA_DV * h, GLA_DV * (h + 1))
                    m = halves[e]
                    st = ss_ref[ch, h]
                    dstn = dst_ref[h]
                    v = v_ref[rows, vs].astype(BF16)
                    dov = do_ref[rows, vs].astype(BF16)
                    qtm = jnp.where(m, qt[:, ls], 0.0).astype(BF16)
                    qhm = jnp.where(m, qh[:, ls], 0.0).astype(BF16)
                    khm = jnp.where(m, kh[:, ls], 0.0).astype(BF16)
                    a = jnp.where(tril, _dg(qtm, kt2, NT), 0.0).astype(BF16)
                    da = jnp.where(tril, _dg(dov, v, NT), 0.0).astype(BF16)
                    dstn_b = dstn.astype(BF16)
                    dv_ref[rows, vs] = _dg(a, dov, TN) + _dg(khm, dstn_b, NT)
                    dqt = dqt + jnp.where(m, _dot(da, kt2), 0.0)
                    dkt = dkt + _dg(da, qtm, TN)
                    dqh = dqh + jnp.where(m, _dot(dov, st.astype(BF16)), 0.0)
                    dkh = dkh + jnp.where(m, _dot(v, dstn_b), 0.0)
                    dbl = dbl + jnp.where(halves1[e], jnp.sum(dstn * st, axis=0, keepdims=True), 0.0)
                    dst_ref[h] = dstn * ebl[:, ls] + _dg(dov, qhm, TN)
                dq_ref[rows, ls] = scale * (dqt * eq[:, ls] + dqh * eb[:, ls])
                dk_ref[rows, ls] = dkt * ek[:, ls] + dkh * ekl[:, ls]
                dkk = dkh * kh[:, ls]
                db = dqt * qt[:, ls] - dkt * kt[:, ls] + dqh * qh[:, ls] - dkk
                db_last = jnp.sum(dkk, axis=0, keepdims=True) + ebl[:, ls] * dbl
                db = db + jnp.where(last_row, db_last, 0.0)
                dla_ref[rows, ls] = jnp.dot(triu_f, db, precision=lax.Precision.HIGHEST,
                                            preferred_element_type=F32)

    def row(w):
        return pl.BlockSpec((BLK, w), lambda i: (nb - 1 - i, 0))

    def rshape(w):
        return jax.ShapeDtypeStruct((t, w), F32)

    return pl.pallas_call(
        body, name="gla_bwd", grid=(nb,),
        in_specs=[row(256), row(256), row(512), row(256),
                  pl.BlockSpec((ncb, GLA_HEADS, GLA_DV, 128), lambda i: (nb - 1 - i, 0, 0, 0)), row(512)],
        out_specs=[row(256), row(256), row(512), row(256)],
        out_shape=[rshape(256), rshape(256), rshape(512), rshape(256)],
        scratch_shapes=[pltpu.VMEM((GLA_HEADS, GLA_DV, 128), F32)],
        compiler_params=_cparams(1),
    )(gq, gk, gv, la, ss, do)


SWA_G = SWA_QH // SWA_KVH


def _swa_mask(n):
    r = lax.broadcasted_iota(jnp.int32, (SWA_G * BLK, 3 * BLK), 0) % BLK
    c = lax.broadcasted_iota(jnp.int32, (SWA_G * BLK, 3 * BLK), 1)
    seg = c // BLK
    cc = c % BLK
    qpos = n * BLK + r - PAD
    kpos = jnp.where(seg == 0, (n - 1) * BLK, jnp.where(seg == 1, n * BLK, 0)) + cc - PAD
    band = (seg < 2) & (kpos >= N_META) & (kpos <= qpos) & (qpos - kpos < WINDOW)
    meta = (seg == 2) & (kpos >= 0) & (kpos < N_META) & (kpos <= qpos)
    return band | meta


def _swa_stack(ref, kh, lo, dtype):
    parts = []
    for g in range(2):
        pair = ref[:, 128 * (2 * kh + g):128 * (2 * kh + g + 1)]
        zero = jnp.zeros_like(pair)
        parts += [jnp.where(lo, pair, zero), jnp.where(lo, zero, pair)]
    return jnp.concatenate(parts, axis=0).astype(dtype)


def _swa_unstack(x4, lo):
    return [jnp.where(lo, x4[2 * g * BLK:(2 * g + 1) * BLK], x4[(2 * g + 1) * BLK:(2 * g + 2) * BLK])
            for g in range(2)]


def _swa_sink_col(sink_ref, kh):
    blk = lax.broadcasted_iota(jnp.int32, (SWA_G * BLK, 1), 0) // BLK
    col = jnp.full((SWA_G * BLK, 1), sink_ref[SWA_G * kh + SWA_G - 1], F32)
    for e in reversed(range(SWA_G - 1)):
        col = jnp.where(blk == e, sink_ref[SWA_G * kh + e], col)
    return col


def _swa_probs(q4, kall, mask, sink):
    s = _dg(q4, kall, NT) * (SWA_HD ** -0.5)
    s = jnp.where(mask, s, NEG_INF)
    m = jnp.maximum(jnp.max(s, axis=-1, keepdims=True), sink)
    p = jnp.exp(s - m)
    es = jnp.exp(sink - m)
    inv = 1.0 / (jnp.sum(p, axis=-1, keepdims=True) + es)
    return p * inv, es * inv


def _swa_fwd(sinks, sq, sk, sv):
    t = sq.shape[0]
    nb = t // BLK

    def body(sink_ref, q_ref, kp_ref, kc_ref, km_ref, vp_ref, vc_ref, vm_ref, o_ref):
        n = pl.program_id(0)
        mask = _swa_mask(n)
        lo = lax.broadcasted_iota(jnp.int32, (BLK, 128), 1) < 64
        for kh in range(SWA_KVH):
            ls = slice(128 * kh, 128 * (kh + 1))
            kall = jnp.concatenate([kp_ref[:, ls], kc_ref[:, ls], km_ref[:, ls]], axis=0)
            vall = jnp.concatenate([vp_ref[:, ls], vc_ref[:, ls], vm_ref[:, ls]], axis=0)
            p, _ = _swa_probs(_swa_stack(q_ref, kh, lo, BF16), kall, mask, _swa_sink_col(sink_ref, kh))
            for g, pair in enumerate(_swa_unstack(_dot(p.astype(BF16), vall), lo)):
                o_ref[:, 128 * (2 * kh + g):128 * (2 * kh + g + 1)] = pair

    cur = lambda w: pl.BlockSpec((BLK, w), lambda i: (i, 0))
    prev = lambda w: pl.BlockSpec((BLK, w), lambda i: (jnp.maximum(i - 1, 0), 0))
    first = lambda w: pl.BlockSpec((BLK, w), lambda i: (0, 0))
    return pl.pallas_call(
        body, name="swa_fwd", grid=(nb,),
        in_specs=[pl.BlockSpec(memory_space=pltpu.SMEM), cur(512), prev(256), cur(256), first(256),
                  prev(256), cur(256), first(256)],
        out_specs=cur(512),
        out_shape=jax.ShapeDtypeStruct((t, SWA_W), F32),
        compiler_params=_cparams(1),
    )(sinks, sq, sk, sk, sk, sv, sv, sv)


def _swa_bwd(sinks, sq, sk, sv, o, do, hook=None):
    t = sq.shape[0]
    nb = t // BLK

    def body(sink_ref, q_ref, kp_ref, kc_ref, km_ref, vp_ref, vc_ref, vm_ref, o_ref, do_ref,
             dq_ref, dk_ref, dv_ref, dkm_ref, dvm_ref, dsink_ref, ck_ref, cv_ref):
        n = pl.program_id(0)

        @pl.when(n == 0)
        def _():
            ck_ref[...] = jnp.zeros_like(ck_ref)
            cv_ref[...] = jnp.zeros_like(cv_ref)
            dkm_ref[...] = jnp.zeros_like(dkm_ref)
            dvm_ref[...] = jnp.zeros_like(dvm_ref)
            dsink_ref[...] = jnp.zeros_like(dsink_ref)

        @pl.when(n == nb)
        def _():
            dk_ref[...] = ck_ref[...]
            dv_ref[...] = cv_ref[...]

        @pl.when(n < nb)
        def _():
            mask = _swa_mask(n)
            lo = lax.broadcasted_iota(jnp.int32, (BLK, 128), 1) < 64
            scale = SWA_HD ** -0.5
            for kh in range(SWA_KVH):
                ls = slice(128 * kh, 128 * (kh + 1))
                kall = jnp.concatenate([kp_ref[:, ls], kc_ref[:, ls], km_ref[:, ls]], axis=0)
                vall = jnp.concatenate([vp_ref[:, ls], vc_ref[:, ls], vm_ref[:, ls]], axis=0)
                q4 = _swa_stack(q_ref, kh, lo, BF16)
                do4 = _swa_stack(do_ref, kh, lo, F32)
                p, psink = _swa_probs(q4, kall, mask, _swa_sink_col(sink_ref, kh))
                delta = jnp.sum(do4 * _swa_stack(o_ref, kh, lo, F32), axis=-1, keepdims=True)
                do4b = do4.astype(BF16)
                ds = (p * (_dg(do4b, vall, NT) - delta) * scale).astype(BF16)
                for g, pair in enumerate(_swa_unstack(_dot(ds, kall), lo)):
                    dq_ref[:, 128 * (2 * kh + g):128 * (2 * kh + g + 1)] = pair
                dkall = _dg(ds, q4, TN)
                dvall = _dg(p.astype(BF16), do4b, TN)
                dsk = psink * delta
                for e in range(SWA_G):
                    h = SWA_G * kh + e
                    dsink_ref[h:h + 1, :] += jnp.broadcast_to(
                        -jnp.sum(dsk[e * BLK:(e + 1) * BLK], axis=0, keepdims=True), (1, 128))
                dk_ref[:, ls] = ck_ref[:, ls] + dkall[0:BLK]
                dv_ref[:, ls] = cv_ref[:, ls] + dvall[0:BLK]
                ck_ref[:, ls] = dkall[BLK:2 * BLK]
                cv_ref[:, ls] = dvall[BLK:2 * BLK]
                dkm_ref[:, ls] += dkall[2 * BLK:3 * BLK]
                dvm_ref[:, ls] += dvall[2 * BLK:3 * BLK]

    cur = lambda w: pl.BlockSpec((BLK, w), lambda i: (jnp.minimum(i, nb - 1), 0))
    prev = lambda w: pl.BlockSpec((BLK, w), lambda i: (jnp.maximum(i - 1, 0), 0))
    first = lambda w: pl.BlockSpec((BLK, w), lambda i: (0, 0))
    return _pallas(
        body, name="swa_bwd", grid=(nb + 1,),
        in_specs=[pl.BlockSpec(memory_space=pltpu.SMEM), cur(512), prev(256), cur(256), first(256),
                  prev(256), cur(256), first(256), cur(512), cur(512)],
        out_specs=[cur(512), prev(256), prev(256), first(256), first(256), _full((SWA_QH, 128))],
        out_shape=[jax.ShapeDtypeStruct((t, SWA_W), F32), jax.ShapeDtypeStruct((t, 256), F32),
                   jax.ShapeDtypeStruct((t, 256), F32), jax.ShapeDtypeStruct((BLK, 256), F32),
                   jax.ShapeDtypeStruct((BLK, 256), F32), jax.ShapeDtypeStruct((SWA_QH, 128), F32)],
        scratch_shapes=[pltpu.VMEM((BLK, 256), F32), pltpu.VMEM((BLK, 256), F32)],
        args=(sinks, sq, sk, sk, sk, sv, sv, sv, o, do), hook=hook)


def _mix_out(h1, ogla, gg, oswa, wgn, wsn, wout, wpost):
    t = h1.shape[0]
    tm = _row_tile(t)

    def body(h_ref, og_ref, gg_ref, os_ref, wgn_ref, wsn_ref, wout_ref, wpost_ref, h2_ref, cat_ref, m_ref):
        parts = []
        for h in range(GLA_HEADS):
            ls = slice(GLA_DV * h, GLA_DV * (h + 1))
            y, _, _ = _rms(og_ref[:, ls], wgn_ref[...])
            g = gg_ref[:, ls]
            parts.append(y * (g * _sigmoid(g)))
        ys, _, _ = _rms(os_ref[...], wsn_ref[...])
        cat = jnp.concatenate(parts + [ys], axis=1).astype(BF16)
        cat_ref[...] = cat
        m = _dot(cat, wout_ref[...])
        m_ref[...] = m
        y, _, _ = _rms(m, wpost_ref[...])
        h2_ref[...] = h_ref[...] + y

    def row(w):
        return pl.BlockSpec((tm, w), lambda i: (i, 0))

    return pl.pallas_call(
        body, name="mix_out", grid=(t // tm,),
        in_specs=[row(D_MODEL), row(512), row(512), row(512), _full((1, GLA_DV)), _full((1, SWA_W)),
                  _full((D_MODEL, D_MODEL)), _full((1, D_MODEL))],
        out_specs=[row(D_MODEL), row(D_MODEL), row(D_MODEL)],
        out_shape=[jax.ShapeDtypeStruct((t, D_MODEL), F32), jax.ShapeDtypeStruct((t, D_MODEL), BF16),
                   jax.ShapeDtypeStruct((t, D_MODEL), F32)],
        compiler_params=_cparams(1),
    )(h1, ogla, gg, oswa, wgn, wsn, wout, wpost)


def _mix_out_bwd(dh2, m, ogla, gg, oswa, wgn, wsn, wout, wpost, hook=None):
    t = dh2.shape[0]
    tm = _row_tile(t)

    def body(dh_ref, m_ref, og_ref, gg_ref, os_ref, wgn_ref, wsn_ref, wout_ref, wpost_ref,
             dog_ref, dgg_ref, dos_ref, dm_ref, dwpost_ref, dwgn_ref, dwsn_ref):
        @pl.when(pl.program_id(0) == 0)
        def _():
            dwpost_ref[...] = jnp.zeros_like(dwpost_ref)
            dwgn_ref[...] = jnp.zeros_like(dwgn_ref)
            dwsn_ref[...] = jnp.zeros_like(dwsn_ref)

        wpost = wpost_ref[...]
        _, mh, r = _rms(m_ref[...], wpost)
        dm, dw = _rms_bwd(mh, r, wpost, dh_ref[...])
        dwpost_ref[...] += dw
        dmb = dm.astype(BF16)
        dm_ref[...] = dmb
        dcat = _dg(dmb, wout_ref[...], NT)
        wgn = wgn_ref[...]
        for h in range(GLA_HEADS):
            ls = slice(GLA_DV * h, GLA_DV * (h + 1))
            dog = dcat[:, ls]
            g = gg_ref[:, ls]
            sg = _sigmoid(g)
            y, xh, r = _rms(og_ref[:, ls], wgn)
            dgg_ref[:, ls] = dog * y * (sg * (1.0 + g * (1.0 - sg)))
            dx, dw = _rms_bwd(xh, r, wgn, dog * (g * sg))
            dog_ref[:, ls] = dx
            dwgn_ref[...] += dw
        wsn = wsn_ref[...]
        _, xh, r = _rms(os_ref[...], wsn)
        dx, dw = _rms_bwd(xh, r, wsn, dcat[:, GLA_W:])
        dos_ref[...] = dx
        dwsn_ref[...] += dw

    def row(w):
        return pl.BlockSpec((tm, w), lambda i: (i, 0))

    def rshape(w, dt=F32):
        return jax.ShapeDtypeStruct((t, w), dt)

    return _pallas(
        body, name="mix_out_bwd", grid=(t // tm,),
        in_specs=[row(D_MODEL), row(D_MODEL), row(512), row(512), row(512), _full((1, GLA_DV)), _full((1, SWA_W)),
                  _full((D_MODEL, D_MODEL)), _full((1, D_MODEL))],
        out_specs=[row(512), row(512), row(512), row(D_MODEL), _full((1, D_MODEL)), _full((1, GLA_DV)),
                   _full((1, SWA_W))],
        out_shape=[rshape(512), rshape(512), rshape(512), rshape(D_MODEL, BF16),
                   jax.ShapeDtypeStruct((1, D_MODEL), F32), jax.ShapeDtypeStruct((1, GLA_DV), F32),
                   jax.ShapeDtypeStruct((1, SWA_W), F32)],
        args=(dh2, m, ogla, gg, oswa, wgn, wsn, wout, wpost), hook=hook)


def _mix_in_bwd(dh2, h1, wmixpre, winp, wa2p, bap, cos, sin, ga, dgq, dgk, dgv, dgg, dla, dsq, dsk, dsv, dkm, dvm):
    t = h1.shape[0]
    tm = _row_tile(t)

    def body(dh2_ref, h_ref, w_ref, win_ref, wa2_ref, ba_ref, cos_ref, sin_ref, ga_ref, dgq_ref, dgk_ref, dgv_ref,
             dgg_ref, dla_ref, dsq_ref, dsk_ref, dsv_ref, dkm_ref, dvm_ref,
             dh1_ref, dproj_ref, dw_ref, dwa2_ref, dba_ref):
        i = pl.program_id(0)

        @pl.when(i == 0)
        def _():
            dw_ref[...] = jnp.zeros_like(dw_ref)
            dwa2_ref[...] = jnp.zeros_like(dwa2_ref)
            dba_ref[...] = jnp.zeros_like(dba_ref)

        first = (i == 0).astype(F32)
        c = cos_ref[...]
        s = -sin_ref[...]
        fh = _first_half_mask(tm)
        dproj_ref[:, P_GQ:P_GK] = dgq_ref[...].astype(BF16)
        dproj_ref[:, P_GK:P_GV] = dgk_ref[...].astype(BF16)
        dproj_ref[:, P_GV:P_GG] = dgv_ref[...].astype(BF16)
        dproj_ref[:, P_GG:P_GA] = dgg_ref[...].astype(BF16)
        gab = ga_ref[...].astype(BF16)
        z = _dot(gab, wa2_ref[...]) + ba_ref[...]
        row_id = i * tm + lax.broadcasted_iota(jnp.int32, (tm, 1), 0)
        dz = jnp.where(row_id >= PAD, dla_ref[...] * (1.0 / GLA_TAU) * (1.0 - _sigmoid(z)), 0.0)
        dzb = dz.astype(BF16)
        dba_ref[...] += jnp.sum(dz, axis=0, keepdims=True)
        dwa2_ref[...] += _dg(gab, dzb, TN)
        dproj_ref[:, P_GA:P_SQ] = _dg(dzb, wa2_ref[...], NT).astype(BF16)
        for k in range(4):
            dy = dsq_ref[:, 128 * k:128 * (k + 1)]
            dproj_ref[:, P_SQ + 128 * k:P_SQ + 128 * (k + 1)] = (dy * c + _rot_half(dy, fh) * s).astype(BF16)
        for k in range(2):
            ls = slice(128 * k, 128 * (k + 1))
            dy = dsk_ref[:, ls]
            dy = jnp.concatenate([dy[:BLK] + first * dkm_ref[:, ls], dy[BLK:]], axis=0) if tm > BLK else (
                dy + first * dkm_ref[:, ls])
            dproj_ref[:, P_SK + 128 * k:P_SK + 128 * (k + 1)] = (dy * c + _rot_half(dy, fh) * s).astype(BF16)
            dv = dsv_ref[:, ls]
            dv = jnp.concatenate([dv[:BLK] + first * dvm_ref[:, ls], dv[BLK:]], axis=0) if tm > BLK else (
                dv + first * dvm_ref[:, ls])
            dproj_ref[:, P_SV + 128 * k:P_SV + 128 * (k + 1)] = dv.astype(BF16)
        dn = _dg(dproj_ref[...], win_ref[...], NT)
        w = w_ref[...]
        _, hh, r = _rms(h_ref[...], w)
        dx, dw = _rms_bwd(hh, r, w, dn)
        dw_ref[...] += dw
        dh1_ref[...] = dh2_ref[...] + dx

    def row(w):
        return pl.BlockSpec((tm, w), lambda i: (i, 0))

    return pl.pallas_call(
        body, name="mix_in_bwd", grid=(t // tm,),
        in_specs=[row(D_MODEL), row(D_MODEL), _full((1, D_MODEL)), _full((D_MODEL, P_END)), _full((128, GLA_KW)),
                  _full((1, GLA_KW)), row(128), row(128), row(128), row(256), row(256), row(512), row(512), row(256),
                  row(512), row(256), row(256), _full((BLK, 256)), _full((BLK, 256))],
        out_specs=[row(D_MODEL), row(P_END), _full((1, D_MODEL)), _full((128, GLA_KW)), _full((1, GLA_KW))],
        out_shape=[jax.ShapeDtypeStruct((t, D_MODEL), F32), jax.ShapeDtypeStruct((t, P_END), BF16),
                   jax.ShapeDtypeStruct((1, D_MODEL), F32), jax.ShapeDtypeStruct((128, GLA_KW), F32),
                   jax.ShapeDtypeStruct((1, GLA_KW), F32)],
        compiler_params=_cparams(1),
    )(dh2, h1, wmixpre, winp, wa2p, bap, cos, sin, ga, dgq, dgk, dgv, dgg, dla, dsq, dsk, dsv, dkm, dvm)


def _loss_head(h3, target):
    t = h3.shape[0]
    nb = t // BLK

    def body(h_ref, t_ref, dy_ref, loss_ref):
        n = pl.program_id(0)

        @pl.when(n == 0)
        def _():
            loss_ref[...] = jnp.zeros_like(loss_ref)
            dy_ref[...] = jnp.zeros_like(dy_ref)

        @pl.when(n > 0)
        def _():
            err = h_ref[...] - t_ref[...]
            dy_ref[...] = err * (1.0 / D_MODEL)
            part = jnp.sum(jnp.sum(err * err, axis=1, keepdims=True), axis=0, keepdims=True)
            loss_ref[...] += jnp.broadcast_to(part, (1, 128))

    return pl.pallas_call(
        body, name="loss_head", grid=(nb,),
        in_specs=[pl.BlockSpec((BLK, D_MODEL), lambda i: (i, 0)),
                  pl.BlockSpec((BLK, D_MODEL), lambda i: (jnp.maximum(i - 1, 0), 0))],
        out_specs=[pl.BlockSpec((BLK, D_MODEL), lambda i: (i, 0)), _full((1, 128))],
        out_shape=[jax.ShapeDtypeStruct((t, D_MODEL), F32), jax.ShapeDtypeStruct((1, 128), F32)],
        compiler_params=_cparams(1),
    )(h3, target)


def _adamw_update(w, g, m, v):
    m = ADAM_B1 * m + (1.0 - ADAM_B1) * g
    v = ADAM_B2 * v + (1.0 - ADAM_B2) * (g * g)
    m_hat = m / (1.0 - ADAM_B1 ** ADAM_STEP)
    v_hat = v / (1.0 - ADAM_B2 ** ADAM_STEP)
    return -ADAM_LR * (m_hat / (jnp.sqrt(v_hat) + ADAM_EPS) + ADAM_WD * w), m, v


def _adamw(w, g, m, v):
    r, c = w.shape
    tr = _div_tile(r)

    def body(w_ref, g_ref, m_ref, v_ref, d_ref, nm_ref, nv_ref):
        d_ref[...], nm_ref[...], nv_ref[...] = _adamw_update(w_ref[...], g_ref[...], m_ref[...], v_ref[...])

    spec = pl.BlockSpec((tr, c), lambda i: (i, 0))
    shape = jax.ShapeDtypeStruct((r, c), F32)
    return pl.pallas_call(
        body, name="adamw", grid=(r // tr,), in_specs=[spec] * 4, out_specs=[spec] * 3, out_shape=[shape] * 3,
        compiler_params=_cparams(1),
    )(w, g, m, v)


def _adamw_halves(w, g_mine, g_other, m, v, c_idx, row0=0):
    r, c = w.shape
    h = g_mine.shape[0]
    tr = _div_tile(math.gcd(r, h))
    nth = h // tr
    t0 = row0 // tr
    assert t0 * tr == row0

    def body(c_ref, w_ref, gm_ref, go_ref, m_ref, v_ref, g_ref, d_ref, nm_ref, nv_ref):
        hh = (t0 + pl.program_id(0)) // nth
        g = jnp.where(hh == c_ref[0], gm_ref[...], go_ref[...])
        g_ref[...] = g
        d_ref[...], nm_ref[...], nv_ref[...] = _adamw_update(w_ref[...], g, m_ref[...], v_ref[...])

    spec = pl.BlockSpec((tr, c), lambda i, c_ref: (i, 0))
    gspec = pl.BlockSpec((tr, c), lambda i, c_ref: ((t0 + i) % nth, 0))
    shape = jax.ShapeDtypeStruct((r, c), F32)
    return pl.pallas_call(
        body, name="adamw_halves",
        grid_spec=pltpu.PrefetchScalarGridSpec(
            num_scalar_prefetch=1, grid=(r // tr,), in_specs=[spec, gspec, gspec, spec, spec], out_specs=[spec] * 4),
        out_shape=[shape] * 4, compiler_params=_cparams(1),
    )(c_idx, w, g_mine, g_other, m, v)


def _place():
    x, y, c = lax.axis_index("x"), lax.axis_index("y"), lax.axis_index("c")
    chips = [(1 - x, y), (x, 1 - y), (1 - x, 1 - y)]
    return x, y, c, chips


def _remote(send_sem, recv_sem, src, dst, to):
    return pltpu.make_async_remote_copy(src_ref=src, dst_ref=dst, send_sem=send_sem, recv_sem=recv_sem,
                                        device_id=to, device_id_type=MESH)


def _half(ref_rows, c):
    h = ref_rows // 2
    return pl.ds(pl.multiple_of(c * h, 8), h)


def _own_slot(shard, q):
    return lax.dynamic_update_slice(jnp.zeros((N_CHIPS,) + shard.shape, shard.dtype), shard[None], (q, 0, 0))


class _GatherChips:
    has_mid = True

    def __init__(self, bufs):
        n = len(bufs)
        self.inputs = list(bufs)
        self.out_shape = [jax.ShapeDtypeStruct(b.shape, b.dtype) for b in bufs]
        self.aliases = [(t, t) for t in range(n)]
        self.scratch = [pltpu.SemaphoreType.DMA((n, 6)), pltpu.SemaphoreType.DMA((n, 6))]

    def start(self, ins, outs, scr):
        send, recv = scr
        x, y, c, chips = _place()
        q = 2 * x + y
        for t, (i_ref, o_ref) in enumerate(zip(ins, outs)):
            rows = _half(i_ref.shape[1], c)
            for j, (cx, cy) in enumerate(chips):
                _remote(send.at[t, j], recv.at[t, j], i_ref.at[q, rows], o_ref.at[q, rows], (cx, cy, c)).start()

    def mid(self, ins, outs, scr):
        send, recv = scr
        x, y, c, chips = _place()
        for t, o_ref in enumerate(outs):
            rows = _half(o_ref.shape[1], c)
            for j, (cx, cy) in enumerate(chips):
                slot = o_ref.at[2 * cx + cy, rows]
                _remote(send.at[t, j], recv.at[t, j], slot, slot, (cx, cy, c)).wait_recv()
                _remote(send.at[t, 3 + j], recv.at[t, 3 + j], slot, slot, (x, y, 1 - c)).start()

    def finish(self, ins, outs, scr):
        send, recv = scr
        x, y, c, chips = _place()
        for t, o_ref in enumerate(outs):
            mine, other = _half(o_ref.shape[1], c), _half(o_ref.shape[1], 1 - c)
            for j, (cx, cy) in enumerate(chips):
                slot = o_ref.at[2 * cx + cy, other]
                _remote(send.at[t, 3 + j], recv.at[t, 3 + j], slot, slot, (x, y, 1 - c)).wait_recv()
            for j, (cx, cy) in enumerate(chips):
                sent = o_ref.at[2 * cx + cy, mine]
                _remote(send.at[t, j], recv.at[t, j], sent, sent, (cx, cy, c)).wait_send()
                _remote(send.at[t, 3 + j], recv.at[t, 3 + j], sent, sent, (x, y, 1 - c)).wait_send()


class _PairExchange:
    has_mid = False
    aliases = ()

    def __init__(self, arrs):
        n = len(arrs)
        self.inputs = list(arrs)
        self.out_shape = [jax.ShapeDtypeStruct((a.shape[0], a.shape[1] // 2, a.shape[2]), a.dtype) for a in arrs]
        self.scratch = [pltpu.SemaphoreType.DMA((n,)), pltpu.SemaphoreType.DMA((n,))]

    def _copies(self, ins, outs, scr):
        send, recv = scr
        x, y, c, _ = _place()
        return [_remote(send.at[t], recv.at[t], i_ref.at[:, _half(i_ref.shape[1], 1 - c)], o_ref, (x, y, 1 - c))
                for t, (i_ref, o_ref) in enumerate(zip(ins, outs))]

    def start(self, ins, outs, scr):
        for cp in self._copies(ins, outs, scr):
            cp.start()

    def finish(self, ins, outs, scr):
        for cp in self._copies(ins, outs, scr):
            cp.wait()


class _ChipScatter:
    has_mid = False
    aliases = ()

    def __init__(self, arrs):
        n = len(arrs)
        self.inputs = list(arrs)
        self.out_shape = [jax.ShapeDtypeStruct((3,) + a.shape[1:], a.dtype) for a in arrs]
        self.scratch = [pltpu.SemaphoreType.DMA((n, 3)), pltpu.SemaphoreType.DMA((n, 3))]

    def _copies(self, ins, outs, scr):
        send, recv = scr
        x, y, c, chips = _place()
        return [_remote(send.at[t, j], recv.at[t, j], i_ref.at[2 * cx + cy], o_ref.at[j], (cx, cy, c))
                for t, (i_ref, o_ref) in enumerate(zip(ins, outs)) for j, (cx, cy) in enumerate(chips)]

    def start(self, ins, outs, scr):
        for cp in self._copies(ins, outs, scr):
            cp.start()

    def finish(self, ins, outs, scr):
        for cp in self._copies(ins, outs, scr):
            cp.wait()


class _PairShare:
    has_mid = False
    aliases = ()

    def __init__(self, arrs):
        n = len(arrs)
        self.inputs = list(arrs)
        self.out_shape = [jax.ShapeDtypeStruct(a.shape, a.dtype) for a in arrs]
        self.scratch = [pltpu.SemaphoreType.DMA((n,)), pltpu.SemaphoreType.DMA((n,))]

    def _copies(self, ins, outs, scr):
        send, recv = scr
        x, y, c, _ = _place()
        return [_remote(send.at[t], recv.at[t], i_ref, o_ref, (x, y, 1 - c))
                for t, (i_ref, o_ref) in enumerate(zip(ins, outs))]

    def start(self, ins, outs, scr):
        for cp in self._copies(ins, outs, scr):
            cp.start()

    def finish(self, ins, outs, scr):
        for cp in self._copies(ins, outs, scr):
            cp.wait()


def _comm_call(hook, name):
    n_in, n_out = len(hook.inputs), len(hook.out_shape)

    def body(*refs):
        ins, outs, scr = refs[:n_in], refs[n_in:n_in + n_out], refs[n_in + n_out:]
        hook.start(ins, outs, scr)
        if hook.has_mid:
            hook.mid(ins, outs, scr)
        hook.finish(ins, outs, scr)

    return pl.pallas_call(body, name=name, in_specs=[ANY] * n_in, out_specs=[ANY] * n_out,
                          out_shape=list(hook.out_shape), scratch_shapes=list(hook.scratch),
                          input_output_aliases=dict(hook.aliases))(*hook.inputs)


def _all_gather_devices(vec):
    r, w = vec.shape

    def body(x_ref, out_ref, send_sems, recv_sems, local_sem):
        x, y, c, chips = _place()
        me, sibling = (x, y, c), (x, y, 1 - c)

        def rows(px, py, pc):
            return out_ref.at[4 * px + 2 * py + pc]

        def copy(k, block, to, src=None):
            return pltpu.make_async_remote_copy(
                src_ref=rows(*block) if src is None else src, dst_ref=rows(*block), send_sem=send_sems.at[k],
                recv_sem=recv_sems.at[k], device_id=to, device_id_type=MESH)

        mine = pltpu.make_async_copy(x_ref, rows(*me), local_sem)
        mine.start()
        first = [copy(0, me, sibling, src=x_ref)]
        first += [copy(1 + j, me, (*chip, c), src=x_ref) for j, chip in enumerate(chips)]
        for cp in first:
            cp.start()
        passed = [copy(4 + j, (*chip, c), sibling) for j, chip in enumerate(chips)]
        for j, chip in enumerate(chips):
            copy(1 + j, (*chip, c), me).wait_recv()
            passed[j].start()
        copy(0, sibling, me).wait_recv()
        for j, chip in enumerate(chips):
            copy(4 + j, (*chip, 1 - c), me).wait_recv()
        for cp in first + passed:
            cp.wait_send()
        mine.wait()

    return pl.pallas_call(
        body, name="all_gather_devices",
        in_specs=[pl.BlockSpec(memory_space=pltpu.VMEM)], out_specs=pl.BlockSpec(memory_space=pltpu.VMEM),
        out_shape=jax.ShapeDtypeStruct((N_DEV, r, w), vec.dtype),
        scratch_shapes=[pltpu.SemaphoreType.DMA((7,)), pltpu.SemaphoreType.DMA((7,)), pltpu.SemaphoreType.DMA],
    )(vec)


def _pair_sum(g, other, c_idx):
    nq, r, w = g.shape
    h = r // 2
    tr = _div_tile(h)
    nt = h // tr

    def body(c_ref, g_ref, o_ref, s_ref):
        s_ref[...] = (g_ref[...].astype(F32) + o_ref[...].astype(F32)).astype(s_ref.dtype)

    return pl.pallas_call(
        body, name="pair_sum",
        grid_spec=pltpu.PrefetchScalarGridSpec(
            num_scalar_prefetch=1, grid=(nq, nt),
            in_specs=[pl.BlockSpec((None, tr, w), lambda k, i, c_ref: (k, c_ref[0] * nt + i, 0)),
                      pl.BlockSpec((None, tr, w), lambda k, i, c_ref: (k, i, 0))],
            out_specs=pl.BlockSpec((None, tr, w), lambda k, i, c_ref: (k, i, 0))),
        out_shape=jax.ShapeDtypeStruct((nq, h, w), g.dtype),
        compiler_params=_cparams(2),
    )(c_idx, g, other)


def _chip_sum(s, others, q_idx):
    _, h, w = s.shape
    tr = _div_tile(h)

    def body(q_ref, s_ref, o_ref, out_ref):
        out_ref[...] = ((s_ref[...].astype(F32) + o_ref[0].astype(F32)) + o_ref[1].astype(F32)) + o_ref[2].astype(F32)

    return pl.pallas_call(
        body, name="chip_sum",
        grid_spec=pltpu.PrefetchScalarGridSpec(
            num_scalar_prefetch=1, grid=(h // tr,),
            in_specs=[pl.BlockSpec((None, tr, w), lambda i, q_ref: (q_ref[0], i, 0)),
                      pl.BlockSpec((3, tr, w), lambda i, q_ref: (0, i, 0))],
            out_specs=pl.BlockSpec((tr, w), lambda i, q_ref: (i, 0))),
        out_shape=jax.ShapeDtypeStruct((h, w), F32),
        compiler_params=_cparams(1),
    )(q_idx, s, others)


def _sum_devices(parts):
    nd, r, w = parts.shape

    def body(p_ref, o_ref):
        acc = p_ref[0]
        for k in range(1, nd):
            acc = acc + p_ref[k]
        o_ref[...] = acc

    return pl.pallas_call(
        body, name="sum_devices", in_specs=[_full((nd, r, w))], out_specs=_full((r, w)),
        out_shape=jax.ShapeDtypeStruct((r, w), parts.dtype), grid=(1,), compiler_params=_cparams(1),
    )(parts)


def _pack_win(w_in):
    o = np.cumsum((0,) + IN_SPLITS)
    gq, gk, gv, gg, ga, sq, sk, sv = [w_in[:, o[i]:o[i + 1]] for i in range(8)]
    z = jnp.zeros((w_in.shape[0], 128 - GLA_RANK), w_in.dtype)
    dup = lambda a: jnp.concatenate([a[:, :64], a[:, :64], a[:, 64:], a[:, 64:]], axis=1)
    return jnp.concatenate([gq, gk, gv, gg, ga, z, sq, dup(sk), dup(sv)], axis=1)


def _unpack_dwin(d):
    und = lambda a: jnp.concatenate([a[:, 0:64] + a[:, 64:128], a[:, 128:192] + a[:, 192:256]], axis=1)
    return jnp.concatenate([d[:, :P_GA], d[:, P_GA:P_GA + GLA_RANK], d[:, P_SQ:P_SK], und(d[:, P_SK:P_SV]),
                            und(d[:, P_SV:P_END])], axis=1)


def _local_step(x, target, meta, p):
    s = x.shape[0]
    t = s + BLK
    h0 = jnp.concatenate([jnp.zeros((PAD, D_MODEL), F32), meta, x], axis=0)
    cos, sin = _rope_tables(t)

    h1, n1, g1, u1, a1, f1 = _ffn_fwd(h0, p["ffn1_pre_norm"], p["ffn1_w_gate"], p["ffn1_w_up"], p["ffn1_w_down"],
                                      p["ffn1_post_norm"])
    n2, gq, gk, gv, gg, ga, la, sq, sk, sv = _mix_proj(h1, p["mix_pre_norm"], p["w_in"], p["gla_w_a2"], p["gla_b_a"],
                                                       cos, sin)
    ogla, ss = _gla_fwd(gq, gk, gv, la)
    oswa = _swa_fwd(p["swa_sinks"], sq, sk, sv)
    h2, cat, m = _mix_out(h1, ogla, gg, oswa, p["gla_out_norm"], p["swa_out_norm"], p["w_out"], p["mix_post_norm"])
    h3, n3, g3, u3, a3, f3 = _ffn_fwd(h2, p["ffn2_pre_norm"], p["ffn2_w_gate"], p["ffn2_w_up"], p["ffn2_w_down"],
                                      p["ffn2_post_norm"])
    dy, sse = _loss_head(h3, target)

    grads = {}
    dh2, df3, dg3, du3, grads["ffn2_pre_norm"], grads["ffn2_post_norm"] = _ffn_bwd(
        dy, h2, f3, g3, u3, p["ffn2_pre_norm"], p["ffn2_w_gate"], p["ffn2_w_up"], p["ffn2_w_down"],
        p["ffn2_post_norm"])
    (gud,) = _ffn_wgrad(n3, df3, dg3, du3, a3)
    grads["ffn2_w_gate"], grads["ffn2_w_up"], grads["ffn2_w_down"] = gud[:, :FJ], gud[:, FJ:2 * FJ], gud[:, 2 * FJ:]

    dogla, dgg, doswa, dm, grads["mix_post_norm"], grads["gla_out_norm"], grads["swa_out_norm"] = _mix_out_bwd(
        dh2, m, ogla, gg, oswa, p["gla_out_norm"], p["swa_out_norm"], p["w_out"], p["mix_post_norm"])
    grads["w_out"] = _xty(cat, dm)
    dsq, dsk, dsv, dkm, dvm, dsinks = _swa_bwd(p["swa_sinks"], sq, sk, sv, oswa, doswa)
    grads["swa_sinks"] = dsinks[:, 0]
    dgq, dgk, dgv, dla = _gla_bwd(gq, gk, gv, la, ss, dogla)
    dh1, dproj, grads["mix_pre_norm"], dwa2p, grads["gla_b_a"] = _mix_in_bwd(
        dh2, h1, p["mix_pre_norm"], p["w_in"], p["gla_w_a2"], p["gla_b_a"], cos, sin, ga, dgq, dgk, dgv, dgg, dla,
        dsq, dsk, dsv, dkm, dvm)
    grads["gla_w_a2"] = dwa2p[:GLA_RANK]
    grads["w_in"] = _unpack_dwin(_xty(n2, dproj))

    dh0, df1, dg1, du1, grads["ffn1_pre_norm"], grads["ffn1_post_norm"] = _ffn_bwd(
        dh1, h0, f1, g1, u1, p["ffn1_pre_norm"], p["ffn1_w_gate"], p["ffn1_w_up"], p["ffn1_w_down"],
        p["ffn1_post_norm"])
    (gud,) = _ffn_wgrad(n1, df1, dg1, du1, a1)
    grads["ffn1_w_gate"], grads["ffn1_w_up"], grads["ffn1_w_down"] = gud[:, :FJ], gud[:, FJ:2 * FJ], gud[:, 2 * FJ:]
    grads["meta_tokens"] = dh0[PAD:BLK]
    return sse[0, 0], dh0[BLK:], grads


WEIGHTS = ['meta_tokens', 'ffn1_pre_norm', 'ffn1_w_gate', 'ffn1_w_up', 'ffn1_w_down', 'ffn1_post_norm',
           'mix_pre_norm', 'w_in', 'gla_w_a2', 'gla_b_a', 'gla_out_norm', 'swa_sinks', 'swa_out_norm', 'w_out',
           'mix_post_norm', 'ffn2_pre_norm', 'ffn2_w_gate', 'ffn2_w_up', 'ffn2_w_down', 'ffn2_post_norm']
BIG = ['ffn1_w_gate', 'ffn1_w_up', 'ffn1_w_down', 'w_in', 'w_out', 'ffn2_w_gate', 'ffn2_w_up', 'ffn2_w_down']
SMALL = [n for n in WEIGHTS if n not in BIG]
FJ = D_FF // N_CHIPS
D_IN_J = D_IN // N_CHIPS
D_OUT_J = D_MODEL // N_CHIPS
TRANSPOSED = ('ffn1_w_gate', 'ffn1_w_up', 'ffn2_w_gate', 'ffn2_w_up')


def _shard2d(name, a):
    return a[0].T if name in TRANSPOSED else a[0]


def _unshard2d(name, a):
    return (a.T if name in TRANSPOSED else a)[None]


def _small_rows(name, a):
    flat = a.reshape(-1)
    rows = -(-flat.shape[0] // 1024) * 8
    return jnp.pad(flat, (0, rows * 128 - flat.shape[0])).reshape(rows, 128)


def kernel(x, meta_tokens, ffn1_pre_norm, ffn1_w_gate, ffn1_w_up, ffn1_w_down, ffn1_post_norm, mix_pre_norm, w_in, gla_w_a2, gla_b_a, gla_out_norm, swa_sinks, swa_out_norm, w_out, mix_post_norm, ffn2_pre_norm, ffn2_w_gate, ffn2_w_up, ffn2_w_down, ffn2_post_norm, loss_target, m_meta_tokens, m_ffn1_pre_norm, m_ffn1_w_gate, m_ffn1_w_up, m_ffn1_w_down, m_ffn1_post_norm, m_mix_pre_norm, m_w_in, m_gla_w_a2, m_gla_b_a, m_gla_out_norm, m_swa_sinks, m_swa_out_norm, m_w_out, m_mix_post_norm, m_ffn2_pre_norm, m_ffn2_w_gate, m_ffn2_w_up, m_ffn2_w_down, m_ffn2_post_norm, v_meta_tokens, v_ffn1_pre_norm, v_ffn1_w_gate, v_ffn1_w_up, v_ffn1_w_down, v_ffn1_post_norm, v_mix_pre_norm, v_w_in, v_gla_w_a2, v_gla_b_a, v_gla_out_norm, v_swa_sinks, v_swa_out_norm, v_w_out, v_mix_post_norm, v_ffn2_pre_norm, v_ffn2_w_gate, v_ffn2_w_up, v_ffn2_w_down, v_ffn2_post_norm):
    args = dict(locals())
    w = {n: args[n] for n in WEIGHTS}
    mom = {n: args["m_" + n] for n in WEIGHTS}
    var = {n: args["v_" + n] for n in WEIGHTS}
    cx, cy, cc = lax.axis_index("x"), lax.axis_index("y"), lax.axis_index("c")
    q_idx = (2 * cx + cy).astype(jnp.int32).reshape(1)
    c_idx = cc.astype(jnp.int32).reshape(1)

    q_chip = 2 * cx + cy
    bf = {n: _own_slot(_shard2d(n, w[n]).astype(BF16), q_chip) for n in BIG}
    early = _GatherChips([bf["ffn1_w_gate"], bf["ffn1_w_up"], bf["ffn1_w_down"], _own_slot(w["meta_tokens"], q_chip),
                          _own_slot(w["gla_w_a2"].reshape(GLA_RANK, GLA_KW // N_CHIPS), q_chip)])
    wg1, wu1, wd1, meta4, wa24 = _comm_call(early, "gather_ffn1")
    meta_full = meta4.transpose(1, 0, 2).reshape(N_META, D_MODEL)
    wa2p = jnp.pad(wa24.transpose(1, 0, 2).reshape(GLA_RANK, GLA_KW), ((0, 128 - GLA_RANK), (0, 0))).astype(BF16)
    sinks = w["swa_sinks"].reshape(SWA_QH)

    seq, target = x[0], loss_target[0]
    t = seq.shape[0] + BLK
    h0 = jnp.concatenate([jnp.zeros((PAD, D_MODEL), F32), meta_full, seq], axis=0)
    cos, sin = _rope_tables(t)
    late = _GatherChips([bf["w_in"], bf["w_out"], bf["ffn2_w_gate"], bf["ffn2_w_up"], bf["ffn2_w_down"]])
    (h1, n1, g1, u1, a1, f1), (win4, wout4, wg2, wu2, wd2) = _ffn_fwd(
        h0, w["ffn1_pre_norm"], wg1, wu1, wd1, w["ffn1_post_norm"], hook=late)
    winp = _pack_win(win4.transpose(1, 0, 2).reshape(D_MODEL, D_IN))
    wout = wout4.reshape(D_MODEL, D_MODEL)
    n2, gq, gk, gv, gg, ga, la, sq, sk, sv = _mix_proj(h1, w["mix_pre_norm"], winp, wa2p, w["gla_b_a"], cos, sin)
    ogla, ss = _gla_fwd(gq, gk, gv, la)
    oswa = _swa_fwd(sinks, sq, sk, sv)
    h2, cat, m = _mix_out(h1, ogla, gg, oswa, w["gla_out_norm"], w["swa_out_norm"], wout, w["mix_post_norm"])
    h3, n3, g3, u3, a3, f3 = _ffn_fwd(h2, w["ffn2_pre_norm"], wg2, wu2, wd2, w["ffn2_post_norm"])
    dy, sse = _loss_head(h3, target)
    loss = lax.psum(sse[0, 0] * (0.5 / D_MODEL), ("x", "y", "c"))

    g = {}
    dh2, df3, dg3, du3, g["ffn2_pre_norm"], g["ffn2_post_norm"] = _ffn_bwd(
        dy, h2, f3, g3, u3, w["ffn2_pre_norm"], wg2, wu2, wd2, w["ffn2_post_norm"])
    (gf2,) = _ffn_wgrad(n3, df3, dg3, du3, a3)
    (dogla, dgg, doswa, dm, g["mix_post_norm"], g["gla_out_norm"], g["swa_out_norm"]), (rgf2,) = _mix_out_bwd(
        dh2, m, ogla, gg, oswa, w["gla_out_norm"], w["swa_out_norm"], wout, w["mix_post_norm"],
        hook=_PairExchange([gf2]))
    sgf2 = _pair_sum(gf2, rgf2, c_idx)
    gout = _xty(cat, dm).reshape(N_CHIPS, D_OUT_J, D_MODEL).astype(BF16)
    (dsq, dsk, dsv, dkm, dvm, dsinks), (ogf2,) = _swa_bwd(sinks, sq, sk, sv, oswa, doswa,
                                                          hook=_ChipScatter([sgf2]))
    g["swa_sinks"] = dsinks[:, 0].reshape(1, SWA_QH)
    dgq, dgk, dgv, dla = _gla_bwd(gq, gk, gv, la, ss, dogla)
    dh1, dproj, g["mix_pre_norm"], dwa2p, g["gla_b_a"] = _mix_in_bwd(
        dh2, h1, w["mix_pre_norm"], winp, wa2p, w["gla_b_a"], cos, sin, ga, dgq, dgk, dgv, dgg, dla,
        dsq, dsk, dsv, dkm, dvm)
    g["gla_w_a2"] = dwa2p[:GLA_RANK]
    gin = _unpack_dwin(_xty(n2, dproj)).reshape(D_MODEL, N_CHIPS, D_IN_J).transpose(1, 0, 2).astype(BF16)
    (dh0, df1, dg1, du1, g["ffn1_pre_norm"], g["ffn1_post_norm"]), (rgin, rgout) = _ffn_bwd(
        dh1, h0, f1, g1, u1, w["ffn1_pre_norm"], wg1, wu1, wd1, w["ffn1_post_norm"],
        hook=_PairExchange([gin, gout]))
    sgin, sgout = _pair_sum(gin, rgin, c_idx), _pair_sum(gout, rgout, c_idx)
    (gf1,), (ogin, ogout) = _ffn_wgrad(n1, df1, dg1, du1, a1, hook=_ChipScatter([sgin, sgout]))
    g["meta_tokens"] = dh0[PAD:BLK]
    grad_x = dh0[BLK:]
    (rgf1,) = _comm_call(_PairExchange([gf1]), "pair_exchange_ffn1")
    sgf1 = _pair_sum(gf1, rgf1, c_idx)
    (ogf1,) = _comm_call(_ChipScatter([sgf1]), "chip_scatter_ffn1")
    halves = [_chip_sum(s, o, q_idx) for s, o in ((sgf1, ogf1), (sgin, ogin), (sgout, ogout), (sgf2, ogf2))]
    others = _comm_call(_PairShare(halves), "pair_share")
    reduced = {"ffn1_w_gate": (0, 0), "ffn1_w_up": (0, FJ), "ffn1_w_down": (0, 2 * FJ), "w_in": (1, 0),
               "w_out": (2, 0), "ffn2_w_gate": (3, 0), "ffn2_w_up": (3, FJ), "ffn2_w_down": (3, 2 * FJ)}
    grad, delta, new_m, new_v = {}, {}, {}, {}
    for n in BIG:
        k, row0 = reduced[n]
        outs = _adamw_halves(_shard2d(n, w[n]), halves[k], others[k], _shard2d(n, mom[n]), _shard2d(n, var[n]),
                             c_idx, row0)
        grad[n], delta[n], new_m[n], new_v[n] = [_unshard2d(n, a) for a in outs]

    small_rows = [_small_rows(n, g[n]) for n in SMALL]
    ssizes = [a.shape[0] for a in small_rows]
    gsmall = _sum_devices(_all_gather_devices(jnp.concatenate(small_rows, axis=0)))
    col0 = {"meta_tokens": D_MODEL // N_CHIPS, "gla_w_a2": GLA_KW // N_CHIPS}
    gs, ws, ms, vs = [], [], [], []
    off = 0
    for n, sz in zip(SMALL, ssizes):
        full_shape = g[n].shape
        gn = gsmall[off:off + sz].reshape(-1)[:math.prod(full_shape)].reshape(full_shape)
        off += sz
        if n in col0:
            gn = lax.dynamic_slice_in_dim(gn, (2 * cx + cy) * col0[n], col0[n], axis=1)
        grad[n] = gn.reshape(w[n].shape)
        gs.append(_small_rows(n, grad[n]))
        ws.append(_small_rows(n, w[n]))
        ms.append(_small_rows(n, mom[n]))
        vs.append(_small_rows(n, var[n]))
    psizes = [a.shape[0] for a in gs]
    d, nm, nv = _adamw(*[jnp.concatenate(a, axis=0) for a in (ws, gs, ms, vs)])
    off = 0
    for n, sz in zip(SMALL, psizes):
        cnt = math.prod(w[n].shape)
        delta[n], new_m[n], new_v[n] = [a[off:off + sz].reshape(-1)[:cnt].reshape(w[n].shape) for a in (d, nm, nv)]
        off += sz

    return (loss, grad_x[None], *[grad[n] for n in WEIGHTS], *[delta[n] for n in WEIGHTS],
            *[new_m[n] for n in WEIGHTS], *[new_v[n] for n in WEIGHTS])
```

```python
import functools
import math

import numpy as np
import jax
import jax.numpy as jnp
from jax import lax
from jax.experimental import pallas as pl
from jax.experimental.pallas import tpu as pltpu

F32 = jnp.float32
BF16 = jnp.bfloat16
MESH = pl.DeviceIdType.MESH

D_MODEL = 1024
D_FF = 2816
N_CHIPS = 4
N_DEV = 8
N_META = 16
BLK = 128
PAD = BLK - N_META
GLA_CHUNK = 64
GLA_HEADS = 4
GLA_DV = 128
GLA_DK = 64
GLA_KW = GLA_HEADS * GLA_DK
GLA_W = GLA_HEADS * GLA_DV
GLA_RANK = 16
GLA_TAU = 16.0
SWA_HD = 64
SWA_QH = 8
SWA_KVH = 2
SWA_W = SWA_QH * SWA_HD
WINDOW = 128
ROPE_THETA = 10000.0
EPS = 1e-6
NEG_INF = -1e30
IN_SPLITS = (256, 256, 512, 512, 16, 512, 128, 128)
D_IN = sum(IN_SPLITS)
P_GQ, P_GK, P_GV, P_GG, P_GA, P_SQ, P_SK, P_SV, P_END = 0, 256, 512, 1024, 1536, 1664, 2176, 2432, 2688
ADAM_LR, ADAM_B1, ADAM_B2, ADAM_EPS, ADAM_WD, ADAM_STEP = 0.001, 0.9, 0.999, 1e-08, 0.01, 10
VMEM_LIMIT = 56 * 1024 * 1024

NT = (((1,), (1,)), ((), ()))
TN = (((0,), (0,)), ((), ()))


def _cparams(n_axes):
    return pltpu.CompilerParams(dimension_semantics=("arbitrary",) * n_axes, vmem_limit_bytes=VMEM_LIMIT)


def _row_tile(t):
    for tm in (640, 512, 384, 256, 128):
        if t % tm == 0:
            return tm
    raise ValueError(t)


def _contract_tile(t):
    return 1664 if t % 1664 == 0 else _row_tile(t)


def _div_tile(r, cap=512):
    best = None
    for tr in range(8, min(r, cap) + 1, 8):
        if r % tr == 0:
            best = tr
    return best if best is not None else r


def _dot(a, b):
    return jnp.dot(a, b, preferred_element_type=F32)


def _dg(a, b, dims):
    return lax.dot_general(a, b, dims, preferred_element_type=F32)


def _rms(x, w):
    r = lax.rsqrt(jnp.mean(x * x, axis=-1, keepdims=True) + EPS)
    xh = x * r
    return xh * w, xh, r


def _rms_bwd(xh, r, w, dy):
    wdy = dy * w
    dx = r * (wdy - xh * jnp.mean(wdy * xh, axis=-1, keepdims=True))
    dw = jnp.sum(dy * xh, axis=0, keepdims=True)
    return dx, dw


def _sigmoid(x):
    return 1.0 / (1.0 + jnp.exp(-x))


def _full(shape):
    nd = len(shape)
    return pl.BlockSpec(shape, lambda *_: (0,) * nd)


ANY = pl.BlockSpec(memory_space=pl.ANY)


def _pallas(body, *, name, grid, in_specs, out_specs, out_shape, args, scratch_shapes=(), hook=None):
    n_axes = len(grid)
    if hook is None:
        return pl.pallas_call(body, name=name, grid=grid, in_specs=list(in_specs), out_specs=list(out_specs),
                              out_shape=list(out_shape), scratch_shapes=list(scratch_shapes),
                              compiler_params=_cparams(n_axes))(*args)
    n_in, n_out, n_scr = len(in_specs), len(out_specs), len(scratch_shapes)
    h_in, h_out = len(hook.inputs), len(hook.out_shape)
    total = math.prod(grid)

    def wrapped(*refs):
        ins, hins = refs[:n_in], refs[n_in:n_in + h_in]
        o0 = n_in + h_in
        outs, houts = refs[o0:o0 + n_out], refs[o0 + n_out:o0 + n_out + h_out]
        s0 = o0 + n_out + h_out
        scr, hscr = refs[s0:s0 + n_scr], refs[s0 + n_scr:]
        step = pl.program_id(0)
        for a in range(1, n_axes):
            step = step * grid[a] + pl.program_id(a)

        @pl.when(step == 0)
        def _():
            hook.start(hins, houts, hscr)

        body(*ins, *outs, *scr)

        if hook.has_mid:
            @pl.when(step == (3 * total) // 4)
            def _():
                hook.mid(hins, houts, hscr)

        @pl.when(step == total - 1)
        def _():
            hook.finish(hins, houts, hscr)

    res = pl.pallas_call(
        wrapped, name=name, grid=grid, in_specs=list(in_specs) + [ANY] * h_in,
        out_specs=list(out_specs) + [ANY] * h_out, out_shape=list(out_shape) + list(hook.out_shape),
        scratch_shapes=list(scratch_shapes) + list(hook.scratch), compiler_params=_cparams(n_axes),
        input_output_aliases={n_in + a: n_out + b for a, b in hook.aliases},
    )(*args, *hook.inputs)
    return res[:n_out], res[n_out:]


def _ffn_fwd(h, wpre, wg4, wu4, wd4, wpost, hook=None, target=None):
    t = h.shape[0]
    tm = _row_tile(t)
    nj, fj, _ = wg4.shape
    nblk = tm // BLK if target is not None else 0

    def body(*refs):
        h_ref, wpre_ref, wg_ref, wu_ref, wd_ref, wpost_ref = refs[:6]
        t_refs = refs[6:6 + nblk]
        hout_ref, n_ref, p1_ref, p2_ref, a_ref, f_ref = refs[6 + nblk:12 + nblk]
        acc_ref = refs[-1]
        i = pl.program_id(0)
        j = pl.program_id(1)

        @pl.when(j == 0)
        def _():
            y, _, _ = _rms(h_ref[...], wpre_ref[...])
            n_ref[...] = y.astype(BF16)
            acc_ref[...] = jnp.zeros_like(acc_ref)

        if target is not None:
            sse_ref = refs[12 + nblk]

            @pl.when((i == 0) & (j == 0))
            def _():
                sse_ref[...] = jnp.zeros_like(sse_ref)

        n = n_ref[...]
        g = _dg(n, wg_ref[...], NT)
        u = _dg(n, wu_ref[...], NT)
        sg = _sigmoid(g)
        silu = g * sg
        p1_ref[...] = (u * (sg + silu * (1.0 - sg))).astype(BF16)
        p2_ref[...] = silu.astype(BF16)
        a = (silu * u).astype(BF16)
        a_ref[...] = a
        acc_ref[...] += _dot(a, wd_ref[...])

        @pl.when(j == nj - 1)
        def _():
            f = acc_ref[...]
            f_ref[...] = f
            y, _, _ = _rms(f, wpost_ref[...])
            hout = h_ref[...] + 0.5 * y
            if target is None:
                hout_ref[...] = hout
            else:
                sse = jnp.zeros((1, 1), F32)
                for k in range(nblk):
                    rows = slice(k * BLK, (k + 1) * BLK)
                    err = hout[rows] - t_refs[k][...]
                    if k == 0:
                        err = jnp.where(i > 0, err, 0.0)
                    hout_ref[rows, :] = err * (1.0 / D_MODEL)
                    sse = sse + jnp.sum(jnp.sum(err * err, axis=1, keepdims=True), axis=0, keepdims=True)
                sse_ref[...] += jnp.broadcast_to(sse, sse_ref.shape)

    row = pl.BlockSpec((tm, D_MODEL), lambda i, j: (i, 0))
    vec = pl.BlockSpec((1, D_MODEL), lambda i, j: (0, 0))
    wrow = pl.BlockSpec((None, fj, D_MODEL), lambda i, j: (j, 0, 0))
    act = pl.BlockSpec((None, tm, fj), lambda i, j: (j, i, 0))
    t_specs = [pl.BlockSpec((BLK, D_MODEL), functools.partial(lambda i, j, k: (jnp.maximum(nblk * i + k - 1, 0), 0), k=k))
               for k in range(nblk)]
    loss_spec = [_full((1, 128))] if target is not None else []
    loss_shape = [jax.ShapeDtypeStruct((1, 128), F32)] if target is not None else []
    return _pallas(
        body, name="ffn_fwd", grid=(t // tm, nj),
        in_specs=[row, vec, wrow, wrow, wrow, vec] + t_specs,
        out_specs=[row, row, act, act, act, row] + loss_spec,
        out_shape=[jax.ShapeDtypeStruct((t, D_MODEL), F32), jax.ShapeDtypeStruct((t, D_MODEL), BF16),
                   jax.ShapeDtypeStruct((nj, t, fj), BF16), jax.ShapeDtypeStruct((nj, t, fj), BF16),
                   jax.ShapeDtypeStruct((nj, t, fj), BF16), jax.ShapeDtypeStruct((t, D_MODEL), F32)] + loss_shape,
        scratch_shapes=[pltpu.VMEM((tm, D_MODEL), F32)],
        args=(h, wpre, wg4, wu4, wd4, wpost) + (target,) * nblk, hook=hook)


def _ffn_bwd(dhout, h, f, p14, p24, wpre, wg4, wu4, wd4, wpost, hook=None):
    t = h.shape[0]
    tm = _row_tile(t)
    nj, fj, _ = wg4.shape

    def body(dhout_ref, h_ref, f_ref, p1_ref, p2_ref, wpre_ref, wg_ref, wu_ref, wd_ref, wpost_ref,
             dh_ref, df_ref, dg_ref, du_ref, dwpre_ref, dwpost_ref, dn_ref):
        i = pl.program_id(0)
        j = pl.program_id(1)

        @pl.when((i == 0) & (j == 0))
        def _():
            dwpre_ref[...] = jnp.zeros_like(dwpre_ref)
            dwpost_ref[...] = jnp.zeros_like(dwpost_ref)

        @pl.when(j == 0)
        def _():
            wpost = wpost_ref[...]
            _, fh, r = _rms(f_ref[...], wpost)
            df, dw = _rms_bwd(fh, r, wpost, 0.5 * dhout_ref[...])
            dwpost_ref[...] += dw
            df_ref[...] = df.astype(BF16)
            dn_ref[...] = jnp.zeros_like(dn_ref)

        da = _dg(df_ref[...], wd_ref[...], NT)
        dg = (da * p1_ref[...].astype(F32)).astype(BF16)
        du = (da * p2_ref[...].astype(F32)).astype(BF16)
        dg_ref[...] = dg
        du_ref[...] = du
        dn_ref[...] += _dot(dg, wg_ref[...]) + _dot(du, wu_ref[...])

        @pl.when(j == nj - 1)
        def _():
            wpre = wpre_ref[...]
            _, hh, r = _rms(h_ref[...], wpre)
            dx, dw = _rms_bwd(hh, r, wpre, dn_ref[...])
            dwpre_ref[...] += dw
            dh_ref[...] = dhout_ref[...] + dx

    row = pl.BlockSpec((tm, D_MODEL), lambda i, j: (i, 0))
    vec = pl.BlockSpec((1, D_MODEL), lambda i, j: (0, 0))
    wrow = pl.BlockSpec((None, fj, D_MODEL), lambda i, j: (j, 0, 0))
    act = pl.BlockSpec((None, tm, fj), lambda i, j: (j, i, 0))
    actshape = jax.ShapeDtypeStruct((nj, t, fj), BF16)
    return _pallas(
        body, name="ffn_bwd", grid=(t // tm, nj),
        in_specs=[row, row, row, act, act, vec, wrow, wrow, wrow, vec],
        out_specs=[row, row, act, act, vec, vec],
        out_shape=[jax.ShapeDtypeStruct((t, D_MODEL), F32), jax.ShapeDtypeStruct((t, D_MODEL), BF16),
                   actshape, actshape,
                   jax.ShapeDtypeStruct((1, D_MODEL), F32), jax.ShapeDtypeStruct((1, D_MODEL), F32)],
        scratch_shapes=[pltpu.VMEM((tm, D_MODEL), F32)],
        args=(dhout, h, f, p14, p24, wpre, wg4, wu4, wd4, wpost), hook=hook)


def _ffn_wgrad(n, df, dg4, du4, a4, hook=None):
    t = n.shape[0]
    tm = _contract_tile(t)
    ni = t // tm
    nj, _, fj = dg4.shape

    def body(n_ref, df_ref, dg_ref, du_ref, a_ref, dw_ref, acc):
        i = pl.program_id(1)

        @pl.when(i == 0)
        def _():
            acc[...] = jnp.zeros_like(acc)

        nn = n_ref[...]
        acc[0:fj, :] += _dg(dg_ref[...], nn, TN)
        acc[fj:2 * fj, :] += _dg(du_ref[...], nn, TN)
        acc[2 * fj:3 * fj, :] += _dg(a_ref[...], df_ref[...], TN)

        @pl.when(i == ni - 1)
        def _():
            dw_ref[...] = acc[...].astype(BF16)

    row = pl.BlockSpec((tm, D_MODEL), lambda j, i: (i, 0))
    act = pl.BlockSpec((None, tm, fj), lambda j, i: (j, i, 0))
    return _pallas(
        body, name="ffn_wgrad", grid=(nj, ni),
        in_specs=[row, row, act, act, act],
        out_specs=[pl.BlockSpec((None, 3 * fj, D_MODEL), lambda j, i: (j, 0, 0))],
        out_shape=[jax.ShapeDtypeStruct((nj, 3 * fj, D_MODEL), BF16)],
        scratch_shapes=[pltpu.VMEM((3 * fj, D_MODEL), F32)],
        args=(n, df, dg4, du4, a4), hook=hook)


def _xty(x, y):
    t, k = x.shape
    n = y.shape[1]
    tm = _contract_tile(t)
    tn = n if n <= 1024 else (896 if n % 896 == 0 else 128)

    def body(x_ref, y_ref, o_ref):
        @pl.when(pl.program_id(1) == 0)
        def _():
            o_ref[...] = jnp.zeros_like(o_ref)

        o_ref[...] += _dg(x_ref[...], y_ref[...], TN)

    return pl.pallas_call(
        body, name="xty", grid=(n // tn, t // tm),
        in_specs=[pl.BlockSpec((tm, k), lambda j, i: (i, 0)), pl.BlockSpec((tm, tn), lambda j, i: (i, j))],
        out_specs=pl.BlockSpec((k, tn), lambda j, i: (0, j)),
        out_shape=jax.ShapeDtypeStruct((k, n), F32),
        compiler_params=_cparams(2),
    )(x, y)


def _rope_tables(t):
    pos = (jnp.arange(t, dtype=jnp.int32) - PAD).astype(F32)
    inv_freq = 1.0 / (ROPE_THETA ** (jnp.arange(0, SWA_HD, 2, dtype=F32) / SWA_HD))
    ang = pos[:, None] * inv_freq[None, :]
    cos = jnp.cos(ang)
    sin = jnp.sin(ang)
    return jnp.concatenate([cos, cos, cos, cos], axis=1), jnp.concatenate([-sin, sin, -sin, sin], axis=1)


def _rot_half(x, first_half):
    return jnp.where(first_half, pltpu.roll(x, 96, 1), pltpu.roll(x, 32, 1))


def _first_half_mask(rows):
    lane = lax.broadcasted_iota(jnp.int32, (rows, 128), 1)
    return (lane % 64) < 32


def _log_sigmoid(z):
    return jnp.minimum(z, 0.0) - jnp.log(1.0 + jnp.exp(-jnp.abs(z)))


def _mix_proj(h1, wmixpre, winp, wa2p, bap, cos, sin):
    t = h1.shape[0]
    tm = _row_tile(t)

    def body(h_ref, w_ref, win_ref, wa2_ref, ba_ref, cos_ref, sin_ref,
             n_ref, gq_ref, gk_ref, gv_ref, gg_ref, ga_ref, la_ref, sq_ref, sk_ref, sv_ref):
        y, _, _ = _rms(h_ref[...], w_ref[...])
        n = y.astype(BF16)
        n_ref[...] = n
        proj = _dot(n, win_ref[...])
        gq_ref[...] = proj[:, P_GQ:P_GK]
        gk_ref[...] = proj[:, P_GK:P_GV]
        gv_ref[...] = proj[:, P_GV:P_GG]
        gg_ref[...] = proj[:, P_GG:P_GA]
        ga = proj[:, P_GA:P_SQ]
        ga_ref[...] = ga
        z = _dot(ga.astype(BF16), wa2_ref[...]) + ba_ref[...]
        la_ref[...] = _log_sigmoid(z) * (1.0 / GLA_TAU)
        c = cos_ref[...]
        s = sin_ref[...]
        fh = _first_half_mask(tm)
        for k in range(4):
            x = proj[:, P_SQ + 128 * k:P_SQ + 128 * (k + 1)]
            sq_ref[:, 128 * k:128 * (k + 1)] = (x * c + _rot_half(x, fh) * s).astype(BF16)
        for k in range(2):
            x = proj[:, P_SK + 128 * k:P_SK + 128 * (k + 1)]
            sk_ref[:, 128 * k:128 * (k + 1)] = (x * c + _rot_half(x, fh) * s).astype(BF16)
        sv_ref[...] = proj[:, P_SV:P_END].astype(BF16)

    def row(w):
        return pl.BlockSpec((tm, w), lambda i: (i, 0))

    def rshape(w, dt):
        return jax.ShapeDtypeStruct((t, w), dt)

    return pl.pallas_call(
        body, name="mix_proj", grid=(t // tm,),
        in_specs=[row(D_MODEL), _full((1, D_MODEL)), _full((D_MODEL, P_END)), _full((128, GLA_KW)),
                  _full((1, GLA_KW)), row(128), row(128)],
        out_specs=[row(D_MODEL), row(256), row(256), row(512), row(512), row(128), row(256), row(512), row(256),
                   row(256)],
        out_shape=[rshape(D_MODEL, BF16), rshape(256, F32), rshape(256, F32), rshape(512, F32), rshape(512, F32),
                   rshape(128, F32), rshape(256, F32), rshape(512, BF16), rshape(256, BF16), rshape(256, BF16)],
        compiler_params=_cparams(1),
    )(h1, wmixpre, winp, wa2p, bap, cos, sin)


def _gla_cumsum(la, tril_f):
    b = jnp.dot(tril_f, la, precision=lax.Precision.HIGHEST, preferred_element_type=F32)
    row = lax.broadcasted_iota(jnp.int32, b.shape, 0)
    bm = jnp.sum(jnp.where(row == GLA_CHUNK // 2 - 1, b, 0.0), axis=0, keepdims=True)
    bl = jnp.sum(jnp.where(row == GLA_CHUNK - 1, b, 0.0), axis=0, keepdims=True)
    return b, bm, bl


def _gla_decays(la, tril_f):
    b, bm, bl = _gla_cumsum(la, tril_f)
    return jnp.exp(b - bm), jnp.exp(bm - b), jnp.exp(b), jnp.exp(bl - b), jnp.exp(bl)


def _gla_masks():
    c = GLA_CHUNK
    r = lax.broadcasted_iota(jnp.int32, (c, c), 0)
    col = lax.broadcasted_iota(jnp.int32, (c, c), 1)
    r4 = lax.broadcasted_iota(jnp.int32, (GLA_HEADS * c, c), 0) % c
    c4 = lax.broadcasted_iota(jnp.int32, (GLA_HEADS * c, c), 1)
    klane = lax.broadcasted_iota(jnp.int32, (c, GLA_KW), 1) // GLA_DK
    vlane = lax.broadcasted_iota(jnp.int32, (c, GLA_W), 1) // GLA_DV
    srow = lax.broadcasted_iota(jnp.int32, (GLA_W, GLA_KW), 0) // GLA_DV
    scol = lax.broadcasted_iota(jnp.int32, (GLA_W, GLA_KW), 1) // GLA_DK
    return dict(tril_f=(r >= col).astype(F32), triu_f=(r <= col).astype(F32), tril4=r4 >= c4,
                khead=[klane == h for h in range(GLA_HEADS)], vhead=[vlane == h for h in range(GLA_HEADS)],
                diag=srow == scol)


def _stack_heads(x, head_masks):
    return jnp.concatenate([jnp.where(m, x, 0.0) for m in head_masks], axis=0)


def _gla_fwd(gq, gk, gv, la):
    t = gq.shape[0]
    nb = t // BLK
    ncb = BLK // GLA_CHUNK
    c = GLA_CHUNK

    def body(q_ref, k_ref, v_ref, la_ref, o_ref, ss_ref, st_ref):
        @pl.when(pl.program_id(0) == 0)
        def _():
            st_ref[...] = jnp.zeros_like(st_ref)

        mk = _gla_masks()
        for ch in range(ncb):
            rows = slice(ch * c, (ch + 1) * c)
            eq, ek, eb, ekl, ebl = _gla_decays(la_ref[rows, :], mk["tril_f"])
            qs = q_ref[rows, :] * (GLA_DK ** -0.5)
            k = k_ref[rows, :]
            v = v_ref[rows, :].astype(BF16)
            st = st_ref[...]
            ss_ref[ch] = st
            q4 = _stack_heads(qs * eq, mk["khead"]).astype(BF16)
            a4 = jnp.where(mk["tril4"], _dg(q4, (k * ek).astype(BF16), NT), 0.0).astype(BF16)
            r4 = _dot(a4, v)
            intra = jnp.concatenate([r4[h * c:(h + 1) * c, GLA_DV * h:GLA_DV * (h + 1)] for h in range(GLA_HEADS)],
                                    axis=1)
            o_ref[rows, :] = intra + _dg((qs * eb).astype(BF16), st.astype(BF16), NT)
            st_ref[...] = st * ebl + jnp.where(mk["diag"], _dg(v, (k * ekl).astype(BF16), TN), 0.0)

    def row(w):
        return pl.BlockSpec((BLK, w), lambda i: (i, 0))

    return pl.pallas_call(
        body, name="gla_fwd", grid=(nb,),
        in_specs=[row(256), row(256), row(512), row(256)],
        out_specs=[row(512), pl.BlockSpec((ncb, GLA_W, GLA_KW), lambda i: (i, 0, 0))],
        out_shape=[jax.ShapeDtypeStruct((t, GLA_W), F32), jax.ShapeDtypeStruct((nb * ncb, GLA_W, GLA_KW), F32)],
        scratch_shapes=[pltpu.VMEM((GLA_W, GLA_KW), F32)],
        compiler_params=_cparams(1),
    )(gq, gk, gv, la)


def _gla_bwd(gq, gk, gv, la, ss, do):
    t = gq.shape[0]
    nb = t // BLK
    ncb = BLK // GLA_CHUNK
    c = GLA_CHUNK

    def body(q_ref, k_ref, v_ref, la_ref, ss_ref, do_ref, dq_ref, dk_ref, dv_ref, dla_ref, dst_ref):
        @pl.when(pl.program_id(0) == 0)
        def _():
            dst_ref[...] = jnp.zeros_like(dst_ref)

        mk = _gla_masks()
        last_row = lax.broadcasted_iota(jnp.int32, (c, GLA_KW), 0) == c - 1
        scale = GLA_DK ** -0.5
        for ch in reversed(range(ncb)):
            rows = slice(ch * c, (ch + 1) * c)
            eq, ek, eb, ekl, ebl = _gla_decays(la_ref[rows, :], mk["tril_f"])
            qs = q_ref[rows, :] * scale
            k = k_ref[rows, :]
            qt, kt, qh, kh = qs * eq, k * ek, qs * eb, k * ekl
            ktb, khb, qhb = kt.astype(BF16), kh.astype(BF16), qh.astype(BF16)
            v = v_ref[rows, :].astype(BF16)
            do_f = do_ref[rows, :]
            dob = do_f.astype(BF16)
            st = ss_ref[ch]
            stb = st.astype(BF16)
            dstn = dst_ref[...]
            dstb = dstn.astype(BF16)
            q4 = _stack_heads(qt, mk["khead"]).astype(BF16)
            do4 = _stack_heads(do_f, mk["vhead"]).astype(BF16)
            a4 = jnp.where(mk["tril4"], _dg(q4, ktb, NT), 0.0).astype(BF16)
            da4 = jnp.where(mk["tril4"], _dg(do4, v, NT), 0.0).astype(BF16)
            dv_ref[rows, :] = _dg(a4, do4, TN) + _dg(khb, dstb, NT)
            dq4 = _dot(da4, ktb)
            dqt = jnp.zeros((c, GLA_KW), F32)
            for h in range(GLA_HEADS):
                dqt = dqt + jnp.where(mk["khead"][h], dq4[h * c:(h + 1) * c], 0.0)
            dkt = _dg(da4, q4, TN)
            dqh = _dot(dob, stb)
            dkh = _dot(v, dstb)
            dbl = jnp.sum(dstn * st, axis=0, keepdims=True)
            dst_ref[...] = dstn * ebl + jnp.where(mk["diag"], _dg(dob, qhb, TN), 0.0)
            dq_ref[rows, :] = scale * (dqt * eq + dqh * eb)
            dk_ref[rows, :] = dkt * ek + dkh * ekl
            dkk = dkh * kh
            db = dqt * qt - dkt * kt + dqh * qh - dkk
            db = db + jnp.where(last_row, jnp.sum(dkk, axis=0, keepdims=True) + ebl * dbl, 0.0)
            dla_ref[rows, :] = jnp.dot(mk["triu_f"], db, precision=lax.Precision.HIGHEST,
                                       preferred_element_type=F32)

    def row(w):
        return pl.BlockSpec((BLK, w), lambda i: (nb - 1 - i, 0))

    def rshape(w):
        return jax.ShapeDtypeStruct((t, w), F32)

    return pl.pallas_call(
        body, name="gla_bwd", grid=(nb,),
        in_specs=[row(256), row(256), row(512), row(256),
                  pl.BlockSpec((ncb, GLA_W, GLA_KW), lambda i: (nb - 1 - i, 0, 0)), row(512)],
        out_specs=[row(256), row(256), row(512), row(256)],
        out_shape=[rshape(256), rshape(256), rshape(512), rshape(256)],
        scratch_shapes=[pltpu.VMEM((GLA_W, GLA_KW), F32)],
        compiler_params=_cparams(1),
    )(gq, gk, gv, la, ss, do)


SWA_G = SWA_QH // SWA_KVH


def _swa_mask(n):
    r = lax.broadcasted_iota(jnp.int32, (SWA_G * BLK, 3 * BLK), 0) % BLK
    c = lax.broadcasted_iota(jnp.int32, (SWA_G * BLK, 3 * BLK), 1)
    seg = c // BLK
    cc = c % BLK
    qpos = n * BLK + r - PAD
    kpos = jnp.where(seg == 0, (n - 1) * BLK, jnp.where(seg == 1, n * BLK, 0)) + cc - PAD
    band = (seg < 2) & (kpos >= N_META) & (kpos <= qpos) & (qpos - kpos < WINDOW)
    meta = (seg == 2) & (kpos >= 0) & (kpos < N_META) & (kpos <= qpos)
    return band | meta


def _swa_stack(ref, kh, lo, dtype):
    parts = []
    for g in range(2):
        pair = ref[:, 128 * (2 * kh + g):128 * (2 * kh + g + 1)]
        zero = jnp.zeros_like(pair)
        parts += [jnp.where(lo, pair, zero), jnp.where(lo, zero, pair)]
    return jnp.concatenate(parts, axis=0).astype(dtype)


def _swa_unstack(x4, lo):
    return [jnp.where(lo, x4[2 * g * BLK:(2 * g + 1) * BLK], x4[(2 * g + 1) * BLK:(2 * g + 2) * BLK])
            for g in range(2)]


def _swa_sink_col(sink_ref, kh):
    blk = lax.broadcasted_iota(jnp.int32, (SWA_G * BLK, 1), 0) // BLK
    col = jnp.full((SWA_G * BLK, 1), sink_ref[SWA_G * kh + SWA_G - 1], F32)
    for e in reversed(range(SWA_G - 1)):
        col = jnp.where(blk == e, sink_ref[SWA_G * kh + e], col)
    return col


def _swa_probs(q4, kall, mask, sink):
    s = _dg(q4, kall, NT) * (SWA_HD ** -0.5)
    s = jnp.where(mask, s, NEG_INF)
    m = jnp.maximum(jnp.max(s, axis=-1, keepdims=True), sink)
    p = jnp.exp(s - m)
    es = jnp.exp(sink - m)
    inv = 1.0 / (jnp.sum(p, axis=-1, keepdims=True) + es)
    return p * inv, es * inv


def _swa_fwd(sinks, sq, sk, sv):
    t = sq.shape[0]
    nb = t // BLK

    def body(sink_ref, q_ref, kp_ref, kc_ref, km_ref, vp_ref, vc_ref, vm_ref, o_ref):
        n = pl.program_id(0)
        mask = _swa_mask(n)
        lo = lax.broadcasted_iota(jnp.int32, (BLK, 128), 1) < 64
        for kh in range(SWA_KVH):
            ls = slice(128 * kh, 128 * (kh + 1))
            kall = jnp.concatenate([kp_ref[:, ls], kc_ref[:, ls], km_ref[:, ls]], axis=0)
            vall = jnp.concatenate([vp_ref[:, ls], vc_ref[:, ls], vm_ref[:, ls]], axis=0)
            p, _ = _swa_probs(_swa_stack(q_ref, kh, lo, BF16), kall, mask, _swa_sink_col(sink_ref, kh))
            for g, pair in enumerate(_swa_unstack(_dot(p.astype(BF16), vall), lo)):
                o_ref[:, 128 * (2 * kh + g):128 * (2 * kh + g + 1)] = pair

    cur = lambda w: pl.BlockSpec((BLK, w), lambda i: (i, 0))
    prev = lambda w: pl.BlockSpec((BLK, w), lambda i: (jnp.maximum(i - 1, 0), 0))
    first = lambda w: pl.BlockSpec((BLK, w), lambda i: (0, 0))
    return pl.pallas_call(
        body, name="swa_fwd", grid=(nb,),
        in_specs=[pl.BlockSpec(memory_space=pltpu.SMEM), cur(512), prev(256), cur(256), first(256),
                  prev(256), cur(256), first(256)],
        out_specs=cur(512),
        out_shape=jax.ShapeDtypeStruct((t, SWA_W), F32),
        compiler_params=_cparams(1),
    )(sinks, sq, sk, sk, sk, sv, sv, sv)


def _swa_bwd(sinks, sq, sk, sv, o, do, hook=None):
    t = sq.shape[0]
    nb = t // BLK

    def body(sink_ref, q_ref, kp_ref, kc_ref, km_ref, vp_ref, vc_ref, vm_ref, o_ref, do_ref,
             dq_ref, dk_ref, dv_ref, dkm_ref, dvm_ref, dsink_ref, ck_ref, cv_ref):
        n = pl.program_id(0)

        @pl.when(n == 0)
        def _():
            ck_ref[...] = jnp.zeros_like(ck_ref)
            cv_ref[...] = jnp.zeros_like(cv_ref)
            dkm_ref[...] = jnp.zeros_like(dkm_ref)
            dvm_ref[...] = jnp.zeros_like(dvm_ref)
            dsink_ref[...] = jnp.zeros_like(dsink_ref)

        @pl.when(n == nb)
        def _():
            dk_ref[...] = ck_ref[...]
            dv_ref[...] = cv_ref[...]

        @pl.when(n < nb)
        def _():
            mask = _swa_mask(n)
            lo = lax.broadcasted_iota(jnp.int32, (BLK, 128), 1) < 64
            scale = SWA_HD ** -0.5
            for kh in range(SWA_KVH):
                ls = slice(128 * kh, 128 * (kh + 1))
                kall = jnp.concatenate([kp_ref[:, ls], kc_ref[:, ls], km_ref[:, ls]], axis=0)
                vall = jnp.concatenate([vp_ref[:, ls], vc_ref[:, ls], vm_ref[:, ls]], axis=0)
                q4 = _swa_stack(q_ref, kh, lo, BF16)
                do4 = _swa_stack(do_ref, kh, lo, F32)
                p, psink = _swa_probs(q4, kall, mask, _swa_sink_col(sink_ref, kh))
                delta = jnp.sum(do4 * _swa_stack(o_ref, kh, lo, F32), axis=-1, keepdims=True)
                do4b = do4.astype(BF16)
                ds = (p * (_dg(do4b, vall, NT) - delta) * scale).astype(BF16)
                for g, pair in enumerate(_swa_unstack(_dot(ds, kall), lo)):
                    dq_ref[:, 128 * (2 * kh + g):128 * (2 * kh + g + 1)] = pair
                dkall = _dg(ds, q4, TN)
                dvall = _dg(p.astype(BF16), do4b, TN)
                dsk = psink * delta
                for e in range(SWA_G):
                    h = SWA_G * kh + e
                    dsink_ref[h:h + 1, :] += jnp.broadcast_to(
                        -jnp.sum(dsk[e * BLK:(e + 1) * BLK], axis=0, keepdims=True), (1, 128))
                dk_ref[:, ls] = ck_ref[:, ls] + dkall[0:BLK]
                dv_ref[:, ls] = cv_ref[:, ls] + dvall[0:BLK]
                ck_ref[:, ls] = dkall[BLK:2 * BLK]
                cv_ref[:, ls] = dvall[BLK:2 * BLK]
                dkm_ref[:, ls] += dkall[2 * BLK:3 * BLK]
                dvm_ref[:, ls] += dvall[2 * BLK:3 * BLK]

    cur = lambda w: pl.BlockSpec((BLK, w), lambda i: (jnp.minimum(i, nb - 1), 0))
    prev = lambda w: pl.BlockSpec((BLK, w), lambda i: (jnp.maximum(i - 1, 0), 0))
    first = lambda w: pl.BlockSpec((BLK, w), lambda i: (0, 0))
    return _pallas(
        body, name="swa_bwd", grid=(nb + 1,),
        in_specs=[pl.BlockSpec(memory_space=pltpu.SMEM), cur(512), prev(256), cur(256), first(256),
                  prev(256), cur(256), first(256), cur(512), cur(512)],
        out_specs=[cur(512), prev(256), prev(256), first(256), first(256), _full((SWA_QH, 128))],
        out_shape=[jax.ShapeDtypeStruct((t, SWA_W), F32), jax.ShapeDtypeStruct((t, 256), F32),
                   jax.ShapeDtypeStruct((t, 256), F32), jax.ShapeDtypeStruct((BLK, 256), F32),
                   jax.ShapeDtypeStruct((BLK, 256), F32), jax.ShapeDtypeStruct((SWA_QH, 128), F32)],
        scratch_shapes=[pltpu.VMEM((BLK, 256), F32), pltpu.VMEM((BLK, 256), F32)],
        args=(sinks, sq, sk, sk, sk, sv, sv, sv, o, do), hook=hook)


def _mix_out(h1, ogla, gg, oswa, wgn, wsn, wout, wpost):
    t = h1.shape[0]
    tm = _row_tile(t)

    def body(h_ref, og_ref, gg_ref, os_ref, wgn_ref, wsn_ref, wout_ref, wpost_ref, h2_ref, cat_ref, m_ref):
        parts = []
        for h in range(GLA_HEADS):
            ls = slice(GLA_DV * h, GLA_DV * (h + 1))
            y, _, _ = _rms(og_ref[:, ls], wgn_ref[...])
            g = gg_ref[:, ls]
            parts.append(y * (g * _sigmoid(g)))
        ys, _, _ = _rms(os_ref[...], wsn_ref[...])
        cat = jnp.concatenate(parts + [ys], axis=1).astype(BF16)
        cat_ref[...] = cat
        m = _dot(cat, wout_ref[...])
        m_ref[...] = m
        y, _, _ = _rms(m, wpost_ref[...])
        h2_ref[...] = h_ref[...] + y

    def row(w):
        return pl.BlockSpec((tm, w), lambda i: (i, 0))

    return pl.pallas_call(
        body, name="mix_out", grid=(t // tm,),
        in_specs=[row(D_MODEL), row(512), row(512), row(512), _full((1, GLA_DV)), _full((1, SWA_W)),
                  _full((D_MODEL, D_MODEL)), _full((1, D_MODEL))],
        out_specs=[row(D_MODEL), row(D_MODEL), row(D_MODEL)],
        out_shape=[jax.ShapeDtypeStruct((t, D_MODEL), F32), jax.ShapeDtypeStruct((t, D_MODEL), BF16),
                   jax.ShapeDtypeStruct((t, D_MODEL), F32)],
        compiler_params=_cparams(1),
    )(h1, ogla, gg, oswa, wgn, wsn, wout, wpost)


def _mix_out_bwd(dh2, m, ogla, gg, oswa, wgn, wsn, wout, wpost, hook=None):
    t = dh2.shape[0]
    tm = _row_tile(t)

    def body(dh_ref, m_ref, og_ref, gg_ref, os_ref, wgn_ref, wsn_ref, wout_ref, wpost_ref,
             dog_ref, dgg_ref, dos_ref, dm_ref, dwpost_ref, dwgn_ref, dwsn_ref):
        @pl.when(pl.program_id(0) == 0)
        def _():
            dwpost_ref[...] = jnp.zeros_like(dwpost_ref)
            dwgn_ref[...] = jnp.zeros_like(dwgn_ref)
            dwsn_ref[...] = jnp.zeros_like(dwsn_ref)

        wpost = wpost_ref[...]
        _, mh, r = _rms(m_ref[...], wpost)
        dm, dw = _rms_bwd(mh, r, wpost, dh_ref[...])
        dwpost_ref[...] += dw
        dmb = dm.astype(BF16)
        dm_ref[...] = dmb
        dcat = _dg(dmb, wout_ref[...], NT)
        wgn = wgn_ref[...]
        for h in range(GLA_HEADS):
            ls = slice(GLA_DV * h, GLA_DV * (h + 1))
            dog = dcat[:, ls]
            g = gg_ref[:, ls]
            sg = _sigmoid(g)
            y, xh, r = _rms(og_ref[:, ls], wgn)
            dgg_ref[:, ls] = dog * y * (sg * (1.0 + g * (1.0 - sg)))
            dx, dw = _rms_bwd(xh, r, wgn, dog * (g * sg))
            dog_ref[:, ls] = dx
            dwgn_ref[...] += dw
        wsn = wsn_ref[...]
        _, xh, r = _rms(os_ref[...], wsn)
        dx, dw = _rms_bwd(xh, r, wsn, dcat[:, GLA_W:])
        dos_ref[...] = dx
        dwsn_ref[...] += dw

    def row(w):
        return pl.BlockSpec((tm, w), lambda i: (i, 0))

    def rshape(w, dt=F32):
        return jax.ShapeDtypeStruct((t, w), dt)

    return _pallas(
        body, name="mix_out_bwd", grid=(t // tm,),
        in_specs=[row(D_MODEL), row(D_MODEL), row(512), row(512), row(512), _full((1, GLA_DV)), _full((1, SWA_W)),
                  _full((D_MODEL, D_MODEL)), _full((1, D_MODEL))],
        out_specs=[row(512), row(512), row(512), row(D_MODEL), _full((1, D_MODEL)), _full((1, GLA_DV)),
                   _full((1, SWA_W))],
        out_shape=[rshape(512), rshape(512), rshape(512), rshape(D_MODEL, BF16),
                   jax.ShapeDtypeStruct((1, D_MODEL), F32), jax.ShapeDtypeStruct((1, GLA_DV), F32),
                   jax.ShapeDtypeStruct((1, SWA_W), F32)],
        args=(dh2, m, ogla, gg, oswa, wgn, wsn, wout, wpost), hook=hook)


def _mix_in_bwd(dh2, h1, wmixpre, winp, wa2p, bap, cos, sin, ga, dgq, dgk, dgv, dgg, dla, dsq, dsk, dsv, dkm, dvm):
    t = h1.shape[0]
    tm = _row_tile(t)

    def body(dh2_ref, h_ref, w_ref, win_ref, wa2_ref, ba_ref, cos_ref, sin_ref, ga_ref, dgq_ref, dgk_ref, dgv_ref,
             dgg_ref, dla_ref, dsq_ref, dsk_ref, dsv_ref, dkm_ref, dvm_ref,
             dh1_ref, dproj_ref, dw_ref, dwa2_ref, dba_ref):
        i = pl.program_id(0)

        @pl.when(i == 0)
        def _():
            dw_ref[...] = jnp.zeros_like(dw_ref)
            dwa2_ref[...] = jnp.zeros_like(dwa2_ref)
            dba_ref[...] = jnp.zeros_like(dba_ref)

        first = (i == 0).astype(F32)
        c = cos_ref[...]
        s = -sin_ref[...]
        fh = _first_half_mask(tm)
        dproj_ref[:, P_GQ:P_GK] = dgq_ref[...].astype(BF16)
        dproj_ref[:, P_GK:P_GV] = dgk_ref[...].astype(BF16)
        dproj_ref[:, P_GV:P_GG] = dgv_ref[...].astype(BF16)
        dproj_ref[:, P_GG:P_GA] = dgg_ref[...].astype(BF16)
        gab = ga_ref[...].astype(BF16)
        z = _dot(gab, wa2_ref[...]) + ba_ref[...]
        row_id = i * tm + lax.broadcasted_iota(jnp.int32, (tm, 1), 0)
        dz = jnp.where(row_id >= PAD, dla_ref[...] * (1.0 / GLA_TAU) * (1.0 - _sigmoid(z)), 0.0)
        dzb = dz.astype(BF16)
        dba_ref[...] += jnp.sum(dz, axis=0, keepdims=True)
        dwa2_ref[...] += _dg(gab, dzb, TN)
        dproj_ref[:, P_GA:P_SQ] = _dg(dzb, wa2_ref[...], NT).astype(BF16)
        for k in range(4):
            dy = dsq_ref[:, 128 * k:128 * (k + 1)]
            dproj_ref[:, P_SQ + 128 * k:P_SQ + 128 * (k + 1)] = (dy * c + _rot_half(dy, fh) * s).astype(BF16)
        for k in range(2):
            ls = slice(128 * k, 128 * (k + 1))
            dy = dsk_ref[:, ls]
            dy = jnp.concatenate([dy[:BLK] + first * dkm_ref[:, ls], dy[BLK:]], axis=0) if tm > BLK else (
                dy + first * dkm_ref[:, ls])
            dproj_ref[:, P_SK + 128 * k:P_SK + 128 * (k + 1)] = (dy * c + _rot_half(dy, fh) * s).astype(BF16)
            dv = dsv_ref[:, ls]
            dv = jnp.concatenate([dv[:BLK] + first * dvm_ref[:, ls], dv[BLK:]], axis=0) if tm > BLK else (
                dv + first * dvm_ref[:, ls])
            dproj_ref[:, P_SV + 128 * k:P_SV + 128 * (k + 1)] = dv.astype(BF16)
        dn = _dg(dproj_ref[...], win_ref[...], NT)
        w = w_ref[...]
        _, hh, r = _rms(h_ref[...], w)
        dx, dw = _rms_bwd(hh, r, w, dn)
        dw_ref[...] += dw
        dh1_ref[...] = dh2_ref[...] + dx

    def row(w):
        return pl.BlockSpec((tm, w), lambda i: (i, 0))

    return pl.pallas_call(
        body, name="mix_in_bwd", grid=(t // tm,),
        in_specs=[row(D_MODEL), row(D_MODEL), _full((1, D_MODEL)), _full((D_MODEL, P_END)), _full((128, GLA_KW)),
                  _full((1, GLA_KW)), row(128), row(128), row(128), row(256), row(256), row(512), row(512), row(256),
                  row(512), row(256), row(256), _full((BLK, 256)), _full((BLK, 256))],
        out_specs=[row(D_MODEL), row(P_END), _full((1, D_MODEL)), _full((128, GLA_KW)), _full((1, GLA_KW))],
        out_shape=[jax.ShapeDtypeStruct((t, D_MODEL), F32), jax.ShapeDtypeStruct((t, P_END), BF16),
                   jax.ShapeDtypeStruct((1, D_MODEL), F32), jax.ShapeDtypeStruct((128, GLA_KW), F32),
                   jax.ShapeDtypeStruct((1, GLA_KW), F32)],
        compiler_params=_cparams(1),
    )(dh2, h1, wmixpre, winp, wa2p, bap, cos, sin, ga, dgq, dgk, dgv, dgg, dla, dsq, dsk, dsv, dkm, dvm)


def _adamw_update(w, g, m, v):
    m = ADAM_B1 * m + (1.0 - ADAM_B1) * g
    v = ADAM_B2 * v + (1.0 - ADAM_B2) * (g * g)
    m_hat = m / (1.0 - ADAM_B1 ** ADAM_STEP)
    v_hat = v / (1.0 - ADAM_B2 ** ADAM_STEP)
    return -ADAM_LR * (m_hat / (jnp.sqrt(v_hat) + ADAM_EPS) + ADAM_WD * w), m, v


def _adamw(w, g, m, v):
    r, c = w.shape
    tr = _div_tile(r)

    def body(w_ref, g_ref, m_ref, v_ref, d_ref, nm_ref, nv_ref):
        d_ref[...], nm_ref[...], nv_ref[...] = _adamw_update(w_ref[...], g_ref[...], m_ref[...], v_ref[...])

    spec = pl.BlockSpec((tr, c), lambda i: (i, 0))
    shape = jax.ShapeDtypeStruct((r, c), F32)
    return pl.pallas_call(
        body, name="adamw", grid=(r // tr,), in_specs=[spec] * 4, out_specs=[spec] * 3, out_shape=[shape] * 3,
        compiler_params=_cparams(1),
    )(w, g, m, v)


def _adamw_halves(w, g_mine, g_other, m, v, c_idx, row0=0):
    r, c = w.shape
    h = g_mine.shape[0]
    tr = _div_tile(math.gcd(r, h))
    nth = h // tr
    t0 = row0 // tr
    assert t0 * tr == row0

    def body(c_ref, w_ref, gm_ref, go_ref, m_ref, v_ref, g_ref, d_ref, nm_ref, nv_ref):
        hh = (t0 + pl.program_id(0)) // nth
        g = jnp.where(hh == c_ref[0], gm_ref[...], go_ref[...])
        g_ref[...] = g
        d_ref[...], nm_ref[...], nv_ref[...] = _adamw_update(w_ref[...], g, m_ref[...], v_ref[...])

    spec = pl.BlockSpec((tr, c), lambda i, c_ref: (i, 0))
    gspec = pl.BlockSpec((tr, c), lambda i, c_ref: ((t0 + i) % nth, 0))
    shape = jax.ShapeDtypeStruct((r, c), F32)
    return pl.pallas_call(
        body, name="adamw_halves",
        grid_spec=pltpu.PrefetchScalarGridSpec(
            num_scalar_prefetch=1, grid=(r // tr,), in_specs=[spec, gspec, gspec, spec, spec], out_specs=[spec] * 4),
        out_shape=[shape] * 4, compiler_params=_cparams(1),
    )(c_idx, w, g_mine, g_other, m, v)


def _place():
    x, y, c = lax.axis_index("x"), lax.axis_index("y"), lax.axis_index("c")
    chips = [(1 - x, y), (x, 1 - y), (1 - x, 1 - y)]
    return x, y, c, chips


def _remote(send_sem, recv_sem, src, dst, to):
    return pltpu.make_async_remote_copy(src_ref=src, dst_ref=dst, send_sem=send_sem, recv_sem=recv_sem,
                                        device_id=to, device_id_type=MESH)


def _half(ref_rows, c):
    h = ref_rows // 2
    return pl.ds(pl.multiple_of(c * h, 8), h)


def _own_slot(shard, q):
    return lax.dynamic_update_slice(jnp.zeros((N_CHIPS,) + shard.shape, shard.dtype), shard[None], (q, 0, 0))


class _GatherChips:
    has_mid = True

    def __init__(self, bufs):
        n = len(bufs)
        self.inputs = list(bufs)
        self.out_shape = [jax.ShapeDtypeStruct(b.shape, b.dtype) for b in bufs]
        self.aliases = [(t, t) for t in range(n)]
        self.scratch = [pltpu.SemaphoreType.DMA((n, 6)), pltpu.SemaphoreType.DMA((n, 6))]

    def start(self, ins, outs, scr):
        send, recv = scr
        x, y, c, chips = _place()
        q = 2 * x + y
        for t, (i_ref, o_ref) in enumerate(zip(ins, outs)):
            rows = _half(i_ref.shape[1], c)
            for j, (cx, cy) in enumerate(chips):
                _remote(send.at[t, j], recv.at[t, j], i_ref.at[q, rows], o_ref.at[q, rows], (cx, cy, c)).start()

    def mid(self, ins, outs, scr):
        send, recv = scr
        x, y, c, chips = _place()
        for t, o_ref in enumerate(outs):
            rows = _half(o_ref.shape[1], c)
            for j, (cx, cy) in enumerate(chips):
                slot = o_ref.at[2 * cx + cy, rows]
                _remote(send.at[t, j], recv.at[t, j], slot, slot, (cx, cy, c)).wait_recv()
                _remote(send.at[t, 3 + j], recv.at[t, 3 + j], slot, slot, (x, y, 1 - c)).start()

    def finish(self, ins, outs, scr):
        send, recv = scr
        x, y, c, chips = _place()
        for t, o_ref in enumerate(outs):
            mine, other = _half(o_ref.shape[1], c), _half(o_ref.shape[1], 1 - c)
            for j, (cx, cy) in enumerate(chips):
                slot = o_ref.at[2 * cx + cy, other]
                _remote(send.at[t, 3 + j], recv.at[t, 3 + j], slot, slot, (x, y, 1 - c)).wait_recv()
            for j, (cx, cy) in enumerate(chips):
                sent = o_ref.at[2 * cx + cy, mine]
                _remote(send.at[t, j], recv.at[t, j], sent, sent, (cx, cy, c)).wait_send()
                _remote(send.at[t, 3 + j], recv.at[t, 3 + j], sent, sent, (x, y, 1 - c)).wait_send()


class _PairExchange:
    has_mid = False
    aliases = ()

    def __init__(self, arrs):
        n = len(arrs)
        self.inputs = list(arrs)
        self.out_shape = [jax.ShapeDtypeStruct((a.shape[0], a.shape[1] // 2, a.shape[2]), a.dtype) for a in arrs]
        self.scratch = [pltpu.SemaphoreType.DMA((n,)), pltpu.SemaphoreType.DMA((n,))]

    def _copies(self, ins, outs, scr):
        send, recv = scr
        x, y, c, _ = _place()
        return [_remote(send.at[t], recv.at[t], i_ref.at[:, _half(i_ref.shape[1], 1 - c)], o_ref, (x, y, 1 - c))
                for t, (i_ref, o_ref) in enumerate(zip(ins, outs))]

    def start(self, ins, outs, scr):
        for cp in self._copies(ins, outs, scr):
            cp.start()

    def finish(self, ins, outs, scr):
        for cp in self._copies(ins, outs, scr):
            cp.wait()


class _ChipScatter:
    has_mid = False
    aliases = ()

    def __init__(self, arrs):
        n = len(arrs)
        self.inputs = list(arrs)
        self.out_shape = [jax.ShapeDtypeStruct((3,) + a.shape[1:], a.dtype) for a in arrs]
        self.scratch = [pltpu.SemaphoreType.DMA((n, 3)), pltpu.SemaphoreType.DMA((n, 3))]

    def _copies(self, ins, outs, scr):
        send, recv = scr
        x, y, c, chips = _place()
        return [_remote(send.at[t, j], recv.at[t, j], i_ref.at[2 * cx + cy], o_ref.at[j], (cx, cy, c))
                for t, (i_ref, o_ref) in enumerate(zip(ins, outs)) for j, (cx, cy) in enumerate(chips)]

    def start(self, ins, outs, scr):
        for cp in self._copies(ins, outs, scr):
            cp.start()

    def finish(self, ins, outs, scr):
        for cp in self._copies(ins, outs, scr):
            cp.wait()


class _PairShare:
    has_mid = False
    aliases = ()

    def __init__(self, arrs):
        n = len(arrs)
        self.inputs = list(arrs)
        self.out_shape = [jax.ShapeDtypeStruct(a.shape, a.dtype) for a in arrs]
        self.scratch = [pltpu.SemaphoreType.DMA((n,)), pltpu.SemaphoreType.DMA((n,))]

    def _copies(self, ins, outs, scr):
        send, recv = scr
        x, y, c, _ = _place()
        return [_remote(send.at[t], recv.at[t], i_ref, o_ref, (x, y, 1 - c))
                for t, (i_ref, o_ref) in enumerate(zip(ins, outs))]

    def start(self, ins, outs, scr):
        for cp in self._copies(ins, outs, scr):
            cp.start()

    def finish(self, ins, outs, scr):
        for cp in self._copies(ins, outs, scr):
            cp.wait()


def _comm_call(hook, name):
    n_in, n_out = len(hook.inputs), len(hook.out_shape)

    def body(*refs):
        ins, outs, scr = refs[:n_in], refs[n_in:n_in + n_out], refs[n_in + n_out:]
        hook.start(ins, outs, scr)
        if hook.has_mid:
            hook.mid(ins, outs, scr)
        hook.finish(ins, outs, scr)

    return pl.pallas_call(body, name=name, in_specs=[ANY] * n_in, out_specs=[ANY] * n_out,
                          out_shape=list(hook.out_shape), scratch_shapes=list(hook.scratch),
                          input_output_aliases=dict(hook.aliases))(*hook.inputs)


def _all_gather_devices(vec):
    r, w = vec.shape

    def body(x_ref, out_ref, send_sems, recv_sems, local_sem):
        x, y, c, chips = _place()
        me, sibling = (x, y, c), (x, y, 1 - c)

        def rows(px, py, pc):
            return out_ref.at[4 * px + 2 * py + pc]

        def copy(k, block, to, src=None):
            return pltpu.make_async_remote_copy(
                src_ref=rows(*block) if src is None else src, dst_ref=rows(*block), send_sem=send_sems.at[k],
                recv_sem=recv_sems.at[k], device_id=to, device_id_type=MESH)

        mine = pltpu.make_async_copy(x_ref, rows(*me), local_sem)
        mine.start()
        first = [copy(0, me, sibling, src=x_ref)]
        first += [copy(1 + j, me, (*chip, c), src=x_ref) for j, chip in enumerate(chips)]
        for cp in first:
            cp.start()
        passed = [copy(4 + j, (*chip, c), sibling) for j, chip in enumerate(chips)]
        for j, chip in enumerate(chips):
            copy(1 + j, (*chip, c), me).wait_recv()
            passed[j].start()
        copy(0, sibling, me).wait_recv()
        for j, chip in enumerate(chips):
            copy(4 + j, (*chip, 1 - c), me).wait_recv()
        for cp in first + passed:
            cp.wait_send()
        mine.wait()

    return pl.pallas_call(
        body, name="all_gather_devices",
        in_specs=[pl.BlockSpec(memory_space=pltpu.VMEM)], out_specs=pl.BlockSpec(memory_space=pltpu.VMEM),
        out_shape=jax.ShapeDtypeStruct((N_DEV, r, w), vec.dtype),
        scratch_shapes=[pltpu.SemaphoreType.DMA((7,)), pltpu.SemaphoreType.DMA((7,)), pltpu.SemaphoreType.DMA],
    )(vec)


def _pair_sum(g, other, c_idx):
    nq, r, w = g.shape
    h = r // 2
    tr = _div_tile(h)
    nt = h // tr

    def body(c_ref, g_ref, o_ref, s_ref):
        s_ref[...] = (g_ref[...].astype(F32) + o_ref[...].astype(F32)).astype(s_ref.dtype)

    return pl.pallas_call(
        body, name="pair_sum",
        grid_spec=pltpu.PrefetchScalarGridSpec(
            num_scalar_prefetch=1, grid=(nq, nt),
            in_specs=[pl.BlockSpec((None, tr, w), lambda k, i, c_ref: (k, c_ref[0] * nt + i, 0)),
                      pl.BlockSpec((None, tr, w), lambda k, i, c_ref: (k, i, 0))],
            out_specs=pl.BlockSpec((None, tr, w), lambda k, i, c_ref: (k, i, 0))),
        out_shape=jax.ShapeDtypeStruct((nq, h, w), g.dtype),
        compiler_params=_cparams(2),
    )(c_idx, g, other)


def _chip_sum(s, others, q_idx):
    _, h, w = s.shape
    tr = _div_tile(h)

    def body(q_ref, s_ref, o_ref, out_ref):
        out_ref[...] = ((s_ref[...].astype(F32) + o_ref[0].astype(F32)) + o_ref[1].astype(F32)) + o_ref[2].astype(F32)

    return pl.pallas_call(
        body, name="chip_sum",
        grid_spec=pltpu.PrefetchScalarGridSpec(
            num_scalar_prefetch=1, grid=(h // tr,),
            in_specs=[pl.BlockSpec((None, tr, w), lambda i, q_ref: (q_ref[0], i, 0)),
                      pl.BlockSpec((3, tr, w), lambda i, q_ref: (0, i, 0))],
            out_specs=pl.BlockSpec((tr, w), lambda i, q_ref: (i, 0))),
        out_shape=jax.ShapeDtypeStruct((h, w), F32),
        compiler_params=_cparams(1),
    )(q_idx, s, others)


def _sum_devices(parts):
    nd, r, w = parts.shape

    def body(p_ref, o_ref):
        acc = p_ref[0]
        for k in range(1, nd):
            acc = acc + p_ref[k]
        o_ref[...] = acc

    return pl.pallas_call(
        body, name="sum_devices", in_specs=[_full((nd, r, w))], out_specs=_full((r, w)),
        out_shape=jax.ShapeDtypeStruct((r, w), parts.dtype), grid=(1,), compiler_params=_cparams(1),
    )(parts)


def _pack_win(w_in):
    o = np.cumsum((0,) + IN_SPLITS)
    gq, gk, gv, gg, ga, sq, sk, sv = [w_in[:, o[i]:o[i + 1]] for i in range(8)]
    z = jnp.zeros((w_in.shape[0], 128 - GLA_RANK), w_in.dtype)
    dup = lambda a: jnp.concatenate([a[:, :64], a[:, :64], a[:, 64:], a[:, 64:]], axis=1)
    return jnp.concatenate([gq, gk, gv, gg, ga, z, sq, dup(sk), dup(sv)], axis=1)


def _unpack_dwin(d):
    und = lambda a: jnp.concatenate([a[:, 0:64] + a[:, 64:128], a[:, 128:192] + a[:, 192:256]], axis=1)
    return jnp.concatenate([d[:, :P_GA], d[:, P_GA:P_GA + GLA_RANK], d[:, P_SQ:P_SK], und(d[:, P_SK:P_SV]),
                            und(d[:, P_SV:P_END])], axis=1)


def _local_step(x, target, meta, p):
    s = x.shape[0]
    t = s + BLK
    h0 = jnp.concatenate([jnp.zeros((PAD, D_MODEL), F32), meta, x], axis=0)
    cos, sin = _rope_tables(t)

    h1, n1, g1, u1, a1, f1 = _ffn_fwd(h0, p["ffn1_pre_norm"], p["ffn1_w_gate"], p["ffn1_w_up"], p["ffn1_w_down"],
                                      p["ffn1_post_norm"])
    n2, gq, gk, gv, gg, ga, la, sq, sk, sv = _mix_proj(h1, p["mix_pre_norm"], p["w_in"], p["gla_w_a2"], p["gla_b_a"],
                                                       cos, sin)
    ogla, ss = _gla_fwd(gq, gk, gv, la)
    oswa = _swa_fwd(p["swa_sinks"], sq, sk, sv)
    h2, cat, m = _mix_out(h1, ogla, gg, oswa, p["gla_out_norm"], p["swa_out_norm"], p["w_out"], p["mix_post_norm"])
    dy, n3, g3, u3, a3, f3, sse = _ffn_fwd(h2, p["ffn2_pre_norm"], p["ffn2_w_gate"], p["ffn2_w_up"],
                                           p["ffn2_w_down"], p["ffn2_post_norm"], target=target)

    grads = {}
    dh2, df3, dg3, du3, grads["ffn2_pre_norm"], grads["ffn2_post_norm"] = _ffn_bwd(
        dy, h2, f3, g3, u3, p["ffn2_pre_norm"], p["ffn2_w_gate"], p["ffn2_w_up"], p["ffn2_w_down"],
        p["ffn2_post_norm"])
    (gud,) = _ffn_wgrad(n3, df3, dg3, du3, a3)
    grads["ffn2_w_gate"], grads["ffn2_w_up"], grads["ffn2_w_down"] = gud[:, :FJ], gud[:, FJ:2 * FJ], gud[:, 2 * FJ:]

    dogla, dgg, doswa, dm, grads["mix_post_norm"], grads["gla_out_norm"], grads["swa_out_norm"] = _mix_out_bwd(
        dh2, m, ogla, gg, oswa, p["gla_out_norm"], p["swa_out_norm"], p["w_out"], p["mix_post_norm"])
    grads["w_out"] = _xty(cat, dm)
    dsq, dsk, dsv, dkm, dvm, dsinks = _swa_bwd(p["swa_sinks"], sq, sk, sv, oswa, doswa)
    grads["swa_sinks"] = dsinks[:, 0]
    dgq, dgk, dgv, dla = _gla_bwd(gq, gk, gv, la, ss, dogla)
    dh1, dproj, grads["mix_pre_norm"], dwa2p, grads["gla_b_a"] = _mix_in_bwd(
        dh2, h1, p["mix_pre_norm"], p["w_in"], p["gla_w_a2"], p["gla_b_a"], cos, sin, ga, dgq, dgk, dgv, dgg, dla,
        dsq, dsk, dsv, dkm, dvm)
    grads["gla_w_a2"] = dwa2p[:GLA_RANK]
    grads["w_in"] = _unpack_dwin(_xty(n2, dproj))

    dh0, df1, dg1, du1, grads["ffn1_pre_norm"], grads["ffn1_post_norm"] = _ffn_bwd(
        dh1, h0, f1, g1, u1, p["ffn1_pre_norm"], p["ffn1_w_gate"], p["ffn1_w_up"], p["ffn1_w_down"],
        p["ffn1_post_norm"])
    (gud,) = _ffn_wgrad(n1, df1, dg1, du1, a1)
    grads["ffn1_w_gate"], grads["ffn1_w_up"], grads["ffn1_w_down"] = gud[:, :FJ], gud[:, FJ:2 * FJ], gud[:, 2 * FJ:]
    grads["meta_tokens"] = dh0[PAD:BLK]
    return sse[0, 0], dh0[BLK:], grads


WEIGHTS = ['meta_tokens', 'ffn1_pre_norm', 'ffn1_w_gate', 'ffn1_w_up', 'ffn1_w_down', 'ffn1_post_norm',
           'mix_pre_norm', 'w_in', 'gla_w_a2', 'gla_b_a', 'gla_out_norm', 'swa_sinks', 'swa_out_norm', 'w_out',
           'mix_post_norm', 'ffn2_pre_norm', 'ffn2_w_gate', 'ffn2_w_up', 'ffn2_w_down', 'ffn2_post_norm']
BIG = ['ffn1_w_gate', 'ffn1_w_up', 'ffn1_w_down', 'w_in', 'w_out', 'ffn2_w_gate', 'ffn2_w_up', 'ffn2_w_down']
SMALL = [n for n in WEIGHTS if n not in BIG]
FJ = D_FF // N_CHIPS
D_IN_J = D_IN // N_CHIPS
D_OUT_J = D_MODEL // N_CHIPS
TRANSPOSED = ('ffn1_w_gate', 'ffn1_w_up', 'ffn2_w_gate', 'ffn2_w_up')


def _shard2d(name, a):
    return a[0].T if name in TRANSPOSED else a[0]


def _unshard2d(name, a):
    return (a.T if name in TRANSPOSED else a)[None]


def _small_rows(name, a):
    flat = a.reshape(-1)
    rows = -(-flat.shape[0] // 1024) * 8
    return jnp.pad(flat, (0, rows * 128 - flat.shape[0])).reshape(rows, 128)


def kernel(x, meta_tokens, ffn1_pre_norm, ffn1_w_gate, ffn1_w_up, ffn1_w_down, ffn1_post_norm, mix_pre_norm, w_in, gla_w_a2, gla_b_a, gla_out_norm, swa_sinks, swa_out_norm, w_out, mix_post_norm, ffn2_pre_norm, ffn2_w_gate, ffn2_w_up, ffn2_w_down, ffn2_post_norm, loss_target, m_meta_tokens, m_ffn1_pre_norm, m_ffn1_w_gate, m_ffn1_w_up, m_ffn1_w_down, m_ffn1_post_norm, m_mix_pre_norm, m_w_in, m_gla_w_a2, m_gla_b_a, m_gla_out_norm, m_swa_sinks, m_swa_out_norm, m_w_out, m_mix_post_norm, m_ffn2_pre_norm, m_ffn2_w_gate, m_ffn2_w_up, m_ffn2_w_down, m_ffn2_post_norm, v_meta_tokens, v_ffn1_pre_norm, v_ffn1_w_gate, v_ffn1_w_up, v_ffn1_w_down, v_ffn1_post_norm, v_mix_pre_norm, v_w_in, v_gla_w_a2, v_gla_b_a, v_gla_out_norm, v_swa_sinks, v_swa_out_norm, v_w_out, v_mix_post_norm, v_ffn2_pre_norm, v_ffn2_w_gate, v_ffn2_w_up, v_ffn2_w_down, v_ffn2_post_norm):
    args = dict(locals())
    w = {n: args[n] for n in WEIGHTS}
    mom = {n: args["m_" + n] for n in WEIGHTS}
    var = {n: args["v_" + n] for n in WEIGHTS}
    cx, cy, cc = lax.axis_index("x"), lax.axis_index("y"), lax.axis_index("c")
    q_idx = (2 * cx + cy).astype(jnp.int32).reshape(1)
    c_idx = cc.astype(jnp.int32).reshape(1)

    q_chip = 2 * cx + cy
    bf = {n: _own_slot(_shard2d(n, w[n]).astype(BF16), q_chip) for n in BIG}
    early = _GatherChips([bf["ffn1_w_gate"], bf["ffn1_w_up"], bf["ffn1_w_down"], _own_slot(w["meta_tokens"], q_chip),
                          _own_slot(w["gla_w_a2"].reshape(GLA_RANK, GLA_KW // N_CHIPS), q_chip)])
    wg1, wu1, wd1, meta4, wa24 = _comm_call(early, "gather_ffn1")
    meta_full = meta4.transpose(1, 0, 2).reshape(N_META, D_MODEL)
    wa2p = jnp.pad(wa24.transpose(1, 0, 2).reshape(GLA_RANK, GLA_KW), ((0, 128 - GLA_RANK), (0, 0))).astype(BF16)
    sinks = w["swa_sinks"].reshape(SWA_QH)

    seq, target = x[0], loss_target[0]
    t = seq.shape[0] + BLK
    h0 = jnp.concatenate([jnp.zeros((PAD, D_MODEL), F32), meta_full, seq], axis=0)
    cos, sin = _rope_tables(t)
    late = _GatherChips([bf["w_in"], bf["w_out"], bf["ffn2_w_gate"], bf["ffn2_w_up"], bf["ffn2_w_down"]])
    (h1, n1, g1, u1, a1, f1), (win4, wout4, wg2, wu2, wd2) = _ffn_fwd(
        h0, w["ffn1_pre_norm"], wg1, wu1, wd1, w["ffn1_post_norm"], hook=late)
    winp = _pack_win(win4.transpose(1, 0, 2).reshape(D_MODEL, D_IN))
    wout = wout4.reshape(D_MODEL, D_MODEL)
    n2, gq, gk, gv, gg, ga, la, sq, sk, sv = _mix_proj(h1, w["mix_pre_norm"], winp, wa2p, w["gla_b_a"], cos, sin)
    ogla, ss = _gla_fwd(gq, gk, gv, la)
    oswa = _swa_fwd(sinks, sq, sk, sv)
    h2, cat, m = _mix_out(h1, ogla, gg, oswa, w["gla_out_norm"], w["swa_out_norm"], wout, w["mix_post_norm"])
    dy, n3, g3, u3, a3, f3, sse = _ffn_fwd(h2, w["ffn2_pre_norm"], wg2, wu2, wd2, w["ffn2_post_norm"], target=target)
    loss = lax.psum(sse[0, 0] * (0.5 / D_MODEL), ("x", "y", "c"))

    g = {}
    dh2, df3, dg3, du3, g["ffn2_pre_norm"], g["ffn2_post_norm"] = _ffn_bwd(
        dy, h2, f3, g3, u3, w["ffn2_pre_norm"], wg2, wu2, wd2, w["ffn2_post_norm"])
    (gf2,) = _ffn_wgrad(n3, df3, dg3, du3, a3)
    (dogla, dgg, doswa, dm, g["mix_post_norm"], g["gla_out_norm"], g["swa_out_norm"]), (rgf2,) = _mix_out_bwd(
        dh2, m, ogla, gg, oswa, w["gla_out_norm"], w["swa_out_norm"], wout, w["mix_post_norm"],
        hook=_PairExchange([gf2]))
    sgf2 = _pair_sum(gf2, rgf2, c_idx)
    gout = _xty(cat, dm).reshape(N_CHIPS, D_OUT_J, D_MODEL).astype(BF16)
    (dsq, dsk, dsv, dkm, dvm, dsinks), (ogf2,) = _swa_bwd(sinks, sq, sk, sv, oswa, doswa,
                                                          hook=_ChipScatter([sgf2]))
    g["swa_sinks"] = dsinks[:, 0].reshape(1, SWA_QH)
    dgq, dgk, dgv, dla = _gla_bwd(gq, gk, gv, la, ss, dogla)
    dh1, dproj, g["mix_pre_norm"], dwa2p, g["gla_b_a"] = _mix_in_bwd(
        dh2, h1, w["mix_pre_norm"], winp, wa2p, w["gla_b_a"], cos, sin, ga, dgq, dgk, dgv, dgg, dla,
        dsq, dsk, dsv, dkm, dvm)
    g["gla_w_a2"] = dwa2p[:GLA_RANK]
    gin = _unpack_dwin(_xty(n2, dproj)).reshape(D_MODEL, N_CHIPS, D_IN_J).transpose(1, 0, 2).astype(BF16)
    (dh0, df1, dg1, du1, g["ffn1_pre_norm"], g["ffn1_post_norm"]), (rgin, rgout) = _ffn_bwd(
        dh1, h0, f1, g1, u1, w["ffn1_pre_norm"], wg1, wu1, wd1, w["ffn1_post_norm"],
        hook=_PairExchange([gin, gout]))
    sgin, sgout = _pair_sum(gin, rgin, c_idx), _pair_sum(gout, rgout, c_idx)
    (gf1,), (ogin, ogout) = _ffn_wgrad(n1, df1, dg1, du1, a1, hook=_ChipScatter([sgin, sgout]))
    g["meta_tokens"] = dh0[PAD:BLK]
    grad_x = dh0[BLK:]
    (rgf1,) = _comm_call(_PairExchange([gf1]), "pair_exchange_ffn1")
    sgf1 = _pair_sum(gf1, rgf1, c_idx)
    (ogf1,) = _comm_call(_ChipScatter([sgf1]), "chip_scatter_ffn1")
    halves = [_chip_sum(s, o, q_idx) for s, o in ((sgf1, ogf1), (sgin, ogin), (sgout, ogout), (sgf2, ogf2))]
    others = _comm_call(_PairShare(halves), "pair_share")
    reduced = {"ffn1_w_gate": (0, 0), "ffn1_w_up": (0, FJ), "ffn1_w_down": (0, 2 * FJ), "w_in": (1, 0),
               "w_out": (2, 0), "ffn2_w_gate": (3, 0), "ffn2_w_up": (3, FJ), "ffn2_w_down": (3, 2 * FJ)}
    grad, delta, new_m, new_v = {}, {}, {}, {}
    for n in BIG:
        k, row0 = reduced[n]
        outs = _adamw_halves(_shard2d(n, w[n]), halves[k], others[k], _shard2d(n, mom[n]), _shard2d(n, var[n]),
                             c_idx, row0)
        grad[n], delta[n], new_m[n], new_v[n] = [_unshard2d(n, a) for a in outs]

    small_rows = [_small_rows(n, g[n]) for n in SMALL]
    ssizes = [a.shape[0] for a in small_rows]
    gsmall = _sum_devices(_all_gather_devices(jnp.concatenate(small_rows, axis=0)))
    col0 = {"meta_tokens": D_MODEL // N_CHIPS, "gla_w_a2": GLA_KW // N_CHIPS}
    gs, ws, ms, vs = [], [], [], []
    off = 0
    for n, sz in zip(SMALL, ssizes):
        full_shape = g[n].shape
        gn = gsmall[off:off + sz].reshape(-1)[:math.prod(full_shape)].reshape(full_shape)
        off += sz
        if n in col0:
            gn = lax.dynamic_slice_in_dim(gn, (2 * cx + cy) * col0[n], col0[n], axis=1)
        grad[n] = gn.reshape(w[n].shape)
        gs.append(_small_rows(n, grad[n]))
        ws.append(_small_rows(n, w[n]))
        ms.append(_small_rows(n, mom[n]))
        vs.append(_small_rows(n, var[n]))
    psizes = [a.shape[0] for a in gs]
    d, nm, nv = _adamw(*[jnp.concatenate(a, axis=0) for a in (ws, gs, ms, vs)])
    off = 0
    for n, sz in zip(SMALL, psizes):
        cnt = math.prod(w[n].shape)
        delta[n], new_m[n], new_v[n] = [a[off:off + sz].reshape(-1)[:cnt].reshape(w[n].shape) for a in (d, nm, nv)]
        off += sz

    return (loss, grad_x[None], *[grad[n] for n in WEIGHTS], *[delta[n] for n in WEIGHTS],
            *[new_m[n] for n in WEIGHTS], *[new_v[n] for n in WEIGHTS])
```

```python
import functools
import math

import numpy as np
import jax
import jax.numpy as jnp
from jax import lax
from jax.experimental import pallas as pl
from jax.experimental.pallas import tpu as pltpu

F32 = jnp.float32
BF16 = jnp.bfloat16
MESH = pl.DeviceIdType.MESH

D_MODEL = 1024
D_FF = 2816
N_CHIPS = 4
N_DEV = 8
N_META = 16
BLK = 128
PAD = BLK - N_META
GLA_CHUNK = 64
GLA_HEADS = 4
GLA_DV = 128
GLA_DK = 64
GLA_KW = GLA_HEADS * GLA_DK
GLA_W = GLA_HEADS * GLA_DV
GLA_RANK = 16
GLA_TAU = 16.0
SWA_HD = 64
SWA_QH = 8
SWA_KVH = 2
SWA_W = SWA_QH * SWA_HD
WINDOW = 128
ROPE_THETA = 10000.0
EPS = 1e-6
NEG_INF = -1e30
IN_SPLITS = (256, 256, 512, 512, 16, 512, 128, 128)
D_IN = sum(IN_SPLITS)
P_GQ, P_GK, P_GV, P_GG, P_GA, P_SQ, P_SK, P_SV, P_END = 0, 256, 512, 1024, 1536, 1664, 2176, 2432, 2688
ADAM_LR, ADAM_B1, ADAM_B2, ADAM_EPS, ADAM_WD, ADAM_STEP = 0.001, 0.9, 0.999, 1e-08, 0.01, 10
VMEM_LIMIT = 56 * 1024 * 1024

NT = (((1,), (1,)), ((), ()))
TN = (((0,), (0,)), ((), ()))


def _cparams(n_axes):
    return pltpu.CompilerParams(dimension_semantics=("arbitrary",) * n_axes, vmem_limit_bytes=VMEM_LIMIT)


def _row_tile(t):
    for tm in (640, 512, 384, 256, 128):
        if t % tm == 0:
            return tm
    raise ValueError(t)


ROW_PARTS = 2


def _row_parts(tm):
    n = ROW_PARTS if tm % (16 * ROW_PARTS) == 0 else 1
    return [slice(k * (tm // n), (k + 1) * (tm // n)) for k in range(n)]


def _contract_tile(t):
    return 1664 if t % 1664 == 0 else _row_tile(t)


def _div_tile(r, cap=512):
    best = None
    for tr in range(8, min(r, cap) + 1, 8):
        if r % tr == 0:
            best = tr
    return best if best is not None else r


def _dot(a, b):
    return jnp.dot(a, b, preferred_element_type=F32)


def _dg(a, b, dims):
    return lax.dot_general(a, b, dims, preferred_element_type=F32)


def _rms(x, w):
    r = lax.rsqrt(jnp.mean(x * x, axis=-1, keepdims=True) + EPS)
    xh = x * r
    return xh * w, xh, r


def _rms_bwd(xh, r, w, dy):
    wdy = dy * w
    dx = r * (wdy - xh * jnp.mean(wdy * xh, axis=-1, keepdims=True))
    dw = jnp.sum(dy * xh, axis=0, keepdims=True)
    return dx, dw


def _sigmoid(x):
    return 1.0 / (1.0 + jnp.exp(-x))


def _full(shape):
    nd = len(shape)
    return pl.BlockSpec(shape, lambda *_: (0,) * nd)


ANY = pl.BlockSpec(memory_space=pl.ANY)


def _pallas(body, *, name, grid, in_specs, out_specs, out_shape, args, scratch_shapes=(), hook=None):
    n_axes = len(grid)
    if hook is None:
        return pl.pallas_call(body, name=name, grid=grid, in_specs=list(in_specs), out_specs=list(out_specs),
                              out_shape=list(out_shape), scratch_shapes=list(scratch_shapes),
                              compiler_params=_cparams(n_axes))(*args)
    n_in, n_out, n_scr = len(in_specs), len(out_specs), len(scratch_shapes)
    h_in, h_out = len(hook.inputs), len(hook.out_shape)
    total = math.prod(grid)

    def wrapped(*refs):
        ins, hins = refs[:n_in], refs[n_in:n_in + h_in]
        o0 = n_in + h_in
        outs, houts = refs[o0:o0 + n_out], refs[o0 + n_out:o0 + n_out + h_out]
        s0 = o0 + n_out + h_out
        scr, hscr = refs[s0:s0 + n_scr], refs[s0 + n_scr:]
        step = pl.program_id(0)
        for a in range(1, n_axes):
            step = step * grid[a] + pl.program_id(a)

        @pl.when(step == 0)
        def _():
            hook.start(hins, houts, hscr)

        body(*ins, *outs, *scr)

        if hook.has_mid:
            @pl.when(step == (3 * total) // 4)
            def _():
                hook.mid(hins, houts, hscr)

        @pl.when(step == total - 1)
        def _():
            hook.finish(hins, houts, hscr)

    res = pl.pallas_call(
        wrapped, name=name, grid=grid, in_specs=list(in_specs) + [ANY] * h_in,
        out_specs=list(out_specs) + [ANY] * h_out, out_shape=list(out_shape) + list(hook.out_shape),
        scratch_shapes=list(scratch_shapes) + list(hook.scratch), compiler_params=_cparams(n_axes),
        input_output_aliases={n_in + a: n_out + b for a, b in hook.aliases},
    )(*args, *hook.inputs)
    return res[:n_out], res[n_out:]


def _ffn_fwd(h, wpre, wg4, wu4, wd4, wpost, hook=None, target=None):
    t = h.shape[0]
    tm = _row_tile(t)
    nj, fj, _ = wg4.shape
    nblk = tm // BLK if target is not None else 0

    def body(*refs):
        h_ref, wpre_ref, wg_ref, wu_ref, wd_ref, wpost_ref = refs[:6]
        t_refs = refs[6:6 + nblk]
        hout_ref, n_ref, p1_ref, p2_ref, a_ref, f_ref = refs[6 + nblk:12 + nblk]
        acc_ref = refs[-1]
        i = pl.program_id(0)
        j = pl.program_id(1)

        @pl.when(j == 0)
        def _():
            y, _, _ = _rms(h_ref[...], wpre_ref[...])
            n_ref[...] = y.astype(BF16)
            acc_ref[...] = jnp.zeros_like(acc_ref)

        if target is not None:
            sse_ref = refs[12 + nblk]

            @pl.when((i == 0) & (j == 0))
            def _():
                sse_ref[...] = jnp.zeros_like(sse_ref)

        parts = [slice(0, tm)]
        gus = [(_dg(n_ref[rows, :], wg_ref[...], NT), _dg(n_ref[rows, :], wu_ref[...], NT)) for rows in parts]
        for rows, (g, u) in zip(parts, gus):
            sg = _sigmoid(g)
            silu = g * sg
            p1_ref[rows, :] = (u * (sg + silu * (1.0 - sg))).astype(BF16)
            p2_ref[rows, :] = silu.astype(BF16)
            a = (silu * u).astype(BF16)
            a_ref[rows, :] = a
            acc_ref[rows, :] += _dot(a, wd_ref[...])

        @pl.when(j == nj - 1)
        def _():
            f = acc_ref[...]
            f_ref[...] = f
            y, _, _ = _rms(f, wpost_ref[...])
            hout = h_ref[...] + 0.5 * y
            if target is None:
                hout_ref[...] = hout
            else:
                sse = jnp.zeros((1, 1), F32)
                for k in range(nblk):
                    rows = slice(k * BLK, (k + 1) * BLK)
                    err = hout[rows] - t_refs[k][...]
                    if k == 0:
                        err = jnp.where(i > 0, err, 0.0)
                    hout_ref[rows, :] = err * (1.0 / D_MODEL)
                    sse = sse + jnp.sum(jnp.sum(err * err, axis=1, keepdims=True), axis=0, keepdims=True)
                sse_ref[...] += jnp.broadcast_to(sse, sse_ref.shape)

    row = pl.BlockSpec((tm, D_MODEL), lambda i, j: (i, 0))
    vec = pl.BlockSpec((1, D_MODEL), lambda i, j: (0, 0))
    wrow = pl.BlockSpec((None, fj, D_MODEL), lambda i, j: (j, 0, 0))
    act = pl.BlockSpec((None, tm, fj), lambda i, j: (j, i, 0))
    t_specs = [pl.BlockSpec((BLK, D_MODEL), functools.partial(lambda i, j, k: (jnp.maximum(nblk * i + k - 1, 0), 0), k=k))
               for k in range(nblk)]
    loss_spec = [_full((1, 128))] if target is not None else []
    loss_shape = [jax.ShapeDtypeStruct((1, 128), F32)] if target is not None else []
    return _pallas(
        body, name="ffn_fwd", grid=(t // tm, nj),
        in_specs=[row, vec, wrow, wrow, wrow, vec] + t_specs,
        out_specs=[row, row, act, act, act, row] + loss_spec,
        out_shape=[jax.ShapeDtypeStruct((t, D_MODEL), F32), jax.ShapeDtypeStruct((t, D_MODEL), BF16),
                   jax.ShapeDtypeStruct((nj, t, fj), BF16), jax.ShapeDtypeStruct((nj, t, fj), BF16),
                   jax.ShapeDtypeStruct((nj, t, fj), BF16), jax.ShapeDtypeStruct((t, D_MODEL), F32)] + loss_shape,
        scratch_shapes=[pltpu.VMEM((tm, D_MODEL), F32)],
        args=(h, wpre, wg4, wu4, wd4, wpost) + (target,) * nblk, hook=hook)


def _ffn_bwd(dhout, h, f, p14, p24, wpre, wg4, wu4, wd4, wpost, hook=None):
    t = h.shape[0]
    tm = _row_tile(t)
    nj, fj, _ = wg4.shape

    def body(dhout_ref, h_ref, f_ref, p1_ref, p2_ref, wpre_ref, wg_ref, wu_ref, wd_ref, wpost_ref,
             dh_ref, df_ref, dg_ref, du_ref, dwpre_ref, dwpost_ref, dn_ref):
        i = pl.program_id(0)
        j = pl.program_id(1)

        @pl.when((i == 0) & (j == 0))
        def _():
            dwpre_ref[...] = jnp.zeros_like(dwpre_ref)
            dwpost_ref[...] = jnp.zeros_like(dwpost_ref)

        @pl.when(j == 0)
        def _():
            wpost = wpost_ref[...]
            _, fh, r = _rms(f_ref[...], wpost)
            df, dw = _rms_bwd(fh, r, wpost, 0.5 * dhout_ref[...])
            dwpost_ref[...] += dw
            df_ref[...] = df.astype(BF16)
            dn_ref[...] = jnp.zeros_like(dn_ref)

        parts = _row_parts(tm)
        das = [_dg(df_ref[rows, :], wd_ref[...], NT) for rows in parts]
        for rows, da in zip(parts, das):
            dg = (da * p1_ref[rows, :].astype(F32)).astype(BF16)
            du = (da * p2_ref[rows, :].astype(F32)).astype(BF16)
            dg_ref[rows, :] = dg
            du_ref[rows, :] = du
            dn_ref[rows, :] += _dot(dg, wg_ref[...]) + _dot(du, wu_ref[...])

        @pl.when(j == nj - 1)
        def _():
            wpre = wpre_ref[...]
            _, hh, r = _rms(h_ref[...], wpre)
            dx, dw = _rms_bwd(hh, r, wpre, dn_ref[...])
            dwpre_ref[...] += dw
            dh_ref[...] = dhout_ref[...] + dx

    row = pl.BlockSpec((tm, D_MODEL), lambda i, j: (i, 0))
    vec = pl.BlockSpec((1, D_MODEL), lambda i, j: (0, 0))
    wrow = pl.BlockSpec((None, fj, D_MODEL), lambda i, j: (j, 0, 0))
    act = pl.BlockSpec((None, tm, fj), lambda i, j: (j, i, 0))
    actshape = jax.ShapeDtypeStruct((nj, t, fj), BF16)
    return _pallas(
        body, name="ffn_bwd", grid=(t // tm, nj),
        in_specs=[row, row, row, act, act, vec, wrow, wrow, wrow, vec],
        out_specs=[row, row, act, act, vec, vec],
        out_shape=[jax.ShapeDtypeStruct((t, D_MODEL), F32), jax.ShapeDtypeStruct((t, D_MODEL), BF16),
                   actshape, actshape,
                   jax.ShapeDtypeStruct((1, D_MODEL), F32), jax.ShapeDtypeStruct((1, D_MODEL), F32)],
        scratch_shapes=[pltpu.VMEM((tm, D_MODEL), F32)],
        args=(dhout, h, f, p14, p24, wpre, wg4, wu4, wd4, wpost), hook=hook)


def _ffn_wgrad(n, df, dg4, du4, a4, hook=None):
    t = n.shape[0]
    tm = _contract_tile(t)
    ni = t // tm
    nj, _, fj = dg4.shape

    def body(n_ref, df_ref, dg_ref, du_ref, a_ref, dw_ref, acc):
        i = pl.program_id(1)

        @pl.when(i == 0)
        def _():
            acc[...] = jnp.zeros_like(acc)

        nn = n_ref[...]
        acc[0:fj, :] += _dg(dg_ref[...], nn, TN)
        acc[fj:2 * fj, :] += _dg(du_ref[...], nn, TN)
        acc[2 * fj:3 * fj, :] += _dg(a_ref[...], df_ref[...], TN)

        @pl.when(i == ni - 1)
        def _():
            dw_ref[...] = acc[...].astype(BF16)

    row = pl.BlockSpec((tm, D_MODEL), lambda j, i: (i, 0))
    act = pl.BlockSpec((None, tm, fj), lambda j, i: (j, i, 0))
    return _pallas(
        body, name="ffn_wgrad", grid=(nj, ni),
        in_specs=[row, row, act, act, act],
        out_specs=[pl.BlockSpec((None, 3 * fj, D_MODEL), lambda j, i: (j, 0, 0))],
        out_shape=[jax.ShapeDtypeStruct((nj, 3 * fj, D_MODEL), BF16)],
        scratch_shapes=[pltpu.VMEM((3 * fj, D_MODEL), F32)],
        args=(n, df, dg4, du4, a4), hook=hook)


def _xty(x, y):
    t, k = x.shape
    n = y.shape[1]
    tm = _contract_tile(t)
    tn = n if n <= 1024 else (896 if n % 896 == 0 else 128)

    def body(x_ref, y_ref, o_ref):
        @pl.when(pl.program_id(1) == 0)
        def _():
            o_ref[...] = jnp.zeros_like(o_ref)

        o_ref[...] += _dg(x_ref[...], y_ref[...], TN)

    return pl.pallas_call(
        body, name="xty", grid=(n // tn, t // tm),
        in_specs=[pl.BlockSpec((tm, k), lambda j, i: (i, 0)), pl.BlockSpec((tm, tn), lambda j, i: (i, j))],
        out_specs=pl.BlockSpec((k, tn), lambda j, i: (0, j)),
        out_shape=jax.ShapeDtypeStruct((k, n), F32),
        compiler_params=_cparams(2),
    )(x, y)


def _rope_tables(t):
    pos = (jnp.arange(t, dtype=jnp.int32) - PAD).astype(F32)
    inv_freq = 1.0 / (ROPE_THETA ** (jnp.arange(0, SWA_HD, 2, dtype=F32) / SWA_HD))
    ang = pos[:, None] * inv_freq[None, :]
    cos = jnp.cos(ang)
    sin = jnp.sin(ang)
    return jnp.concatenate([cos, cos, cos, cos], axis=1), jnp.concatenate([-sin, sin, -sin, sin], axis=1)


def _rot_half(x, first_half):
    return jnp.where(first_half, pltpu.roll(x, 96, 1), pltpu.roll(x, 32, 1))


def _first_half_mask(rows):
    lane = lax.broadcasted_iota(jnp.int32, (rows, 128), 1)
    return (lane % 64) < 32


def _log_sigmoid(z):
    return jnp.minimum(z, 0.0) - jnp.log(1.0 + jnp.exp(-jnp.abs(z)))


def _mix_proj(h1, wmixpre, winp, wa2p, bap, cos, sin):
    t = h1.shape[0]
    tm = _row_tile(t)

    def body(h_ref, w_ref, win_ref, wa2_ref, ba_ref, cos_ref, sin_ref,
             n_ref, gq_ref, gk_ref, gv_ref, gg_ref, ga_ref, la_ref, sq_ref, sk_ref, sv_ref):
        y, _, _ = _rms(h_ref[...], w_ref[...])
        n = y.astype(BF16)
        n_ref[...] = n
        proj = _dot(n, win_ref[...])
        gq_ref[...] = proj[:, P_GQ:P_GK]
        gk_ref[...] = proj[:, P_GK:P_GV]
        gv_ref[...] = proj[:, P_GV:P_GG]
        gg_ref[...] = proj[:, P_GG:P_GA]
        ga = proj[:, P_GA:P_SQ]
        ga_ref[...] = ga
        z = _dot(ga.astype(BF16), wa2_ref[...]) + ba_ref[...]
        la_ref[...] = _log_sigmoid(z) * (1.0 / GLA_TAU)
        c = cos_ref[...]
        s = sin_ref[...]
        fh = _first_half_mask(tm)
        for k in range(4):
            x = proj[:, P_SQ + 128 * k:P_SQ + 128 * (k + 1)]
            sq_ref[:, 128 * k:128 * (k + 1)] = (x * c + _rot_half(x, fh) * s).astype(BF16)
        for k in range(2):
            x = proj[:, P_SK + 128 * k:P_SK + 128 * (k + 1)]
            sk_ref[:, 128 * k:128 * (k + 1)] = (x * c + _rot_half(x, fh) * s).astype(BF16)
        sv_ref[...] = proj[:, P_SV:P_END].astype(BF16)

    def row(w):
        return pl.BlockSpec((tm, w), lambda i: (i, 0))

    def rshape(w, dt):
        return jax.ShapeDtypeStruct((t, w), dt)

    return pl.pallas_call(
        body, name="mix_proj", grid=(t // tm,),
        in_specs=[row(D_MODEL), _full((1, D_MODEL)), _full((D_MODEL, P_END)), _full((128, GLA_KW)),
                  _full((1, GLA_KW)), row(128), row(128)],
        out_specs=[row(D_MODEL), row(256), row(256), row(512), row(512), row(128), row(256), row(512), row(256),
                   row(256)],
        out_shape=[rshape(D_MODEL, BF16), rshape(256, F32), rshape(256, F32), rshape(512, F32), rshape(512, F32),
                   rshape(128, F32), rshape(256, F32), rshape(512, BF16), rshape(256, BF16), rshape(256, BF16)],
        compiler_params=_cparams(1),
    )(h1, wmixpre, winp, wa2p, bap, cos, sin)


def _gla_cumsum(la, tril_f):
    b = jnp.dot(tril_f, la, precision=lax.Precision.HIGHEST, preferred_element_type=F32)
    row = lax.broadcasted_iota(jnp.int32, b.shape, 0)
    bm = jnp.sum(jnp.where(row == GLA_CHUNK // 2 - 1, b, 0.0), axis=0, keepdims=True)
    bl = jnp.sum(jnp.where(row == GLA_CHUNK - 1, b, 0.0), axis=0, keepdims=True)
    return b, bm, bl


def _gla_decays(la, tril_f):
    b, bm, bl = _gla_cumsum(la, tril_f)
    return jnp.exp(b - bm), jnp.exp(bm - b), jnp.exp(b), jnp.exp(bl - b), jnp.exp(bl)


def _gla_masks():
    c = GLA_CHUNK
    r = lax.broadcasted_iota(jnp.int32, (c, c), 0)
    col = lax.broadcasted_iota(jnp.int32, (c, c), 1)
    r4 = lax.broadcasted_iota(jnp.int32, (GLA_HEADS * c, c), 0) % c
    c4 = lax.broadcasted_iota(jnp.int32, (GLA_HEADS * c, c), 1)
    klane = lax.broadcasted_iota(jnp.int32, (c, GLA_KW), 1) // GLA_DK
    vlane = lax.broadcasted_iota(jnp.int32, (c, GLA_W), 1) // GLA_DV
    srow = lax.broadcasted_iota(jnp.int32, (GLA_W, GLA_KW), 0) // GLA_DV
    scol = lax.broadcasted_iota(jnp.int32, (GLA_W, GLA_KW), 1) // GLA_DK
    return dict(tril_f=(r >= col).astype(F32), triu_f=(r <= col).astype(F32), tril4=r4 >= c4,
                khead=[klane == h for h in range(GLA_HEADS)], vhead=[vlane == h for h in range(GLA_HEADS)],
                diag=srow == scol)


def _stack_heads(x, head_masks):
    return jnp.concatenate([jnp.where(m, x, 0.0) for m in head_masks], axis=0)


def _gla_fwd(gq, gk, gv, la):
    t = gq.shape[0]
    nb = t // BLK
    ncb = BLK // GLA_CHUNK
    c = GLA_CHUNK

    def body(q_ref, k_ref, v_ref, la_ref, o_ref, ss_ref, st_ref):
        @pl.when(pl.program_id(0) == 0)
        def _():
            st_ref[...] = jnp.zeros_like(st_ref)

        mk = _gla_masks()
        for ch in range(ncb):
            rows = slice(ch * c, (ch + 1) * c)
            eq, ek, eb, ekl, ebl = _gla_decays(la_ref[rows, :], mk["tril_f"])
            qs = q_ref[rows, :] * (GLA_DK ** -0.5)
            k = k_ref[rows, :]
            v = v_ref[rows, :].astype(BF16)
            st = st_ref[...]
            ss_ref[ch] = st
            q4 = _stack_heads(qs * eq, mk["khead"]).astype(BF16)
            a4 = jnp.where(mk["tril4"], _dg(q4, (k * ek).astype(BF16), NT), 0.0).astype(BF16)
            r4 = _dot(a4, v)
            intra = jnp.concatenate([r4[h * c:(h + 1) * c, GLA_DV * h:GLA_DV * (h + 1)] for h in range(GLA_HEADS)],
                                    axis=1)
            o_ref[rows, :] = intra + _dg((qs * eb).astype(BF16), st.astype(BF16), NT)
            st_ref[...] = st * ebl + jnp.where(mk["diag"], _dg(v, (k * ekl).astype(BF16), TN), 0.0)

    def row(w):
        return pl.BlockSpec((BLK, w), lambda i: (i, 0))

    return pl.pallas_call(
        body, name="gla_fwd", grid=(nb,),
        in_specs=[row(256), row(256), row(512), row(256)],
        out_specs=[row(512), pl.BlockSpec((ncb, GLA_W, GLA_KW), lambda i: (i, 0, 0))],
        out_shape=[jax.ShapeDtypeStruct((t, GLA_W), F32), jax.ShapeDtypeStruct((nb * ncb, GLA_W, GLA_KW), F32)],
        scratch_shapes=[pltpu.VMEM((GLA_W, GLA_KW), F32)],
        compiler_params=_cparams(1),
    )(gq, gk, gv, la)


def _gla_bwd(gq, gk, gv, la, ss, do):
    t = gq.shape[0]
    nb = t // BLK
    ncb = BLK // GLA_CHUNK
    c = GLA_CHUNK

    def body(q_ref, k_ref, v_ref, la_ref, ss_ref, do_ref, dq_ref, dk_ref, dv_ref, dla_ref, dst_ref):
        @pl.when(pl.program_id(0) == 0)
        def _():
            dst_ref[...] = jnp.zeros_like(dst_ref)

        mk = _gla_masks()
        last_row = lax.broadcasted_iota(jnp.int32, (c, GLA_KW), 0) == c - 1
        scale = GLA_DK ** -0.5
        for ch in reversed(range(ncb)):
            rows = slice(ch * c, (ch + 1) * c)
            eq, ek, eb, ekl, ebl = _gla_decays(la_ref[rows, :], mk["tril_f"])
            qs = q_ref[rows, :] * scale
            k = k_ref[rows, :]
            qt, kt, qh, kh = qs * eq, k * ek, qs * eb, k * ekl
            ktb, khb, qhb = kt.astype(BF16), kh.astype(BF16), qh.astype(BF16)
            v = v_ref[rows, :].astype(BF16)
            do_f = do_ref[rows, :]
            dob = do_f.astype(BF16)
            st = ss_ref[ch]
            stb = st.astype(BF16)
            dstn = dst_ref[...]
            dstb = dstn.astype(BF16)
            q4 = _stack_heads(qt, mk["khead"]).astype(BF16)
            do4 = _stack_heads(do_f, mk["vhead"]).astype(BF16)
            a4 = jnp.where(mk["tril4"], _dg(q4, ktb, NT), 0.0).astype(BF16)
            da4 = jnp.where(mk["tril4"], _dg(do4, v, NT), 0.0).astype(BF16)
            dv_ref[rows, :] = _dg(a4, do4, TN) + _dg(khb, dstb, NT)
            dq4 = _dot(da4, ktb)
            dqt = jnp.zeros((c, GLA_KW), F32)
            for h in range(GLA_HEADS):
                dqt = dqt + jnp.where(mk["khead"][h], dq4[h * c:(h + 1) * c], 0.0)
            dkt = _dg(da4, q4, TN)
            dqh = _dot(dob, stb)
            dkh = _dot(v, dstb)
            dbl = jnp.sum(dstn * st, axis=0, keepdims=True)
            dst_ref[...] = dstn * ebl + jnp.where(mk["diag"], _dg(dob, qhb, TN), 0.0)
            dq_ref[rows, :] = scale * (dqt * eq + dqh * eb)
            dk_ref[rows, :] = dkt * ek + dkh * ekl
            dkk = dkh * kh
            db = dqt * qt - dkt * kt + dqh * qh - dkk
            db = db + jnp.where(last_row, jnp.sum(dkk, axis=0, keepdims=True) + ebl * dbl, 0.0)
            dla_ref[rows, :] = jnp.dot(mk["triu_f"], db, precision=lax.Precision.HIGHEST,
                                       preferred_element_type=F32)

    def row(w):
        return pl.BlockSpec((BLK, w), lambda i: (nb - 1 - i, 0))

    def rshape(w):
        return jax.ShapeDtypeStruct((t, w), F32)

    return pl.pallas_call(
        body, name="gla_bwd", grid=(nb,),
        in_specs=[row(256), row(256), row(512), row(256),
                  pl.BlockSpec((ncb, GLA_W, GLA_KW), lambda i: (nb - 1 - i, 0, 0)), row(512)],
        out_specs=[row(256), row(256), row(512), row(256)],
        out_shape=[rshape(256), rshape(256), rshape(512), rshape(256)],
        scratch_shapes=[pltpu.VMEM((GLA_W, GLA_KW), F32)],
        compiler_params=_cparams(1),
    )(gq, gk, gv, la, ss, do)


SWA_G = SWA_QH // SWA_KVH


def _swa_mask(n):
    r = lax.broadcasted_iota(jnp.int32, (SWA_G * BLK, 3 * BLK), 0) % BLK
    c = lax.broadcasted_iota(jnp.int32, (SWA_G * BLK, 3 * BLK), 1)
    seg = c // BLK
    cc = c % BLK
    qpos = n * BLK + r - PAD
    kpos = jnp.where(seg == 0, (n - 1) * BLK, jnp.where(seg == 1, n * BLK, 0)) + cc - PAD
    band = (seg < 2) & (kpos >= N_META) & (kpos <= qpos) & (qpos - kpos < WINDOW)
    meta = (seg == 2) & (kpos >= 0) & (kpos < N_META) & (kpos <= qpos)
    return band | meta


def _swa_stack(ref, kh, lo, dtype):
    parts = []
    for g in range(2):
        pair = ref[:, 128 * (2 * kh + g):128 * (2 * kh + g + 1)]
        zero = jnp.zeros_like(pair)
        parts += [jnp.where(lo, pair, zero), jnp.where(lo, zero, pair)]
    return jnp.concatenate(parts, axis=0).astype(dtype)


def _swa_unstack(x4, lo):
    return [jnp.where(lo, x4[2 * g * BLK:(2 * g + 1) * BLK], x4[(2 * g + 1) * BLK:(2 * g + 2) * BLK])
            for g in range(2)]


def _swa_sink_col(sink_ref, kh):
    blk = lax.broadcasted_iota(jnp.int32, (SWA_G * BLK, 1), 0) // BLK
    col = jnp.full((SWA_G * BLK, 1), sink_ref[SWA_G * kh + SWA_G - 1], F32)
    for e in reversed(range(SWA_G - 1)):
        col = jnp.where(blk == e, sink_ref[SWA_G * kh + e], col)
    return col


def _swa_probs(q4, kall, mask, sink):
    s = _dg(q4, kall, NT) * (SWA_HD ** -0.5)
    s = jnp.where(mask, s, NEG_INF)
    m = jnp.maximum(jnp.max(s, axis=-1, keepdims=True), sink)
    p = jnp.exp(s - m)
    es = jnp.exp(sink - m)
    inv = 1.0 / (jnp.sum(p, axis=-1, keepdims=True) + es)
    return p * inv, es * inv


def _swa_fwd(sinks, sq, sk, sv):
    t = sq.shape[0]
    nb = t // BLK

    def body(sink_ref, q_ref, kp_ref, kc_ref, km_ref, vp_ref, vc_ref, vm_ref, o_ref):
        n = pl.program_id(0)
        mask = _swa_mask(n)
        lo = lax.broadcasted_iota(jnp.int32, (BLK, 128), 1) < 64
        for kh in range(SWA_KVH):
            ls = slice(128 * kh, 128 * (kh + 1))
            kall = jnp.concatenate([kp_ref[:, ls], kc_ref[:, ls], km_ref[:, ls]], axis=0)
            vall = jnp.concatenate([vp_ref[:, ls], vc_ref[:, ls], vm_ref[:, ls]], axis=0)
            p, _ = _swa_probs(_swa_stack(q_ref, kh, lo, BF16), kall, mask, _swa_sink_col(sink_ref, kh))
            for g, pair in enumerate(_swa_unstack(_dot(p.astype(BF16), vall), lo)):
                o_ref[:, 128 * (2 * kh + g):128 * (2 * kh + g + 1)] = pair

    cur = lambda w: pl.BlockSpec((BLK, w), lambda i: (i, 0))
    prev = lambda w: pl.BlockSpec((BLK, w), lambda i: (jnp.maximum(i - 1, 0), 0))
    first = lambda w: pl.BlockSpec((BLK, w), lambda i: (0, 0))
    return pl.pallas_call(
        body, name="swa_fwd", grid=(nb,),
        in_specs=[pl.BlockSpec(memory_space=pltpu.SMEM), cur(512), prev(256), cur(256), first(256),
                  prev(256), cur(256), first(256)],
        out_specs=cur(512),
        out_shape=jax.ShapeDtypeStruct((t, SWA_W), F32),
        compiler_params=_cparams(1),
    )(sinks, sq, sk, sk, sk, sv, sv, sv)


def _swa_bwd(sinks, sq, sk, sv, o, do, hook=None):
    t = sq.shape[0]
    nb = t // BLK

    def body(sink_ref, q_ref, kp_ref, kc_ref, km_ref, vp_ref, vc_ref, vm_ref, o_ref, do_ref,
             dq_ref, dk_ref, dv_ref, dkm_ref, dvm_ref, dsink_ref, ck_ref, cv_ref):
        n = pl.program_id(0)

        @pl.when(n == 0)
        def _():
            ck_ref[...] = jnp.zeros_like(ck_ref)
            cv_ref[...] = jnp.zeros_like(cv_ref)
            dkm_ref[...] = jnp.zeros_like(dkm_ref)
            dvm_ref[...] = jnp.zeros_like(dvm_ref)
            dsink_ref[...] = jnp.zeros_like(dsink_ref)

        @pl.when(n == nb)
        def _():
            dk_ref[...] = ck_ref[...]
            dv_ref[...] = cv_ref[...]

        @pl.when(n < nb)
        def _():
            mask = _swa_mask(n)
            lo = lax.broadcasted_iota(jnp.int32, (BLK, 128), 1) < 64
            scale = SWA_HD ** -0.5
            for kh in range(SWA_KVH):
                ls = slice(128 * kh, 128 * (kh + 1))
                kall = jnp.concatenate([kp_ref[:, ls], kc_ref[:, ls], km_ref[:, ls]], axis=0)
                vall = jnp.concatenate([vp_ref[:, ls], vc_ref[:, ls], vm_ref[:, ls]], axis=0)
                q4 = _swa_stack(q_ref, kh, lo, BF16)
                do4 = _swa_stack(do_ref, kh, lo, F32)
                p, psink = _swa_probs(q4, kall, mask, _swa_sink_col(sink_ref, kh))
                delta = jnp.sum(do4 * _swa_stack(o_ref, kh, lo, F32), axis=-1, keepdims=True)
                do4b = do4.astype(BF16)
                ds = (p * (_dg(do4b, vall, NT) - delta) * scale).astype(BF16)
                for g, pair in enumerate(_swa_unstack(_dot(ds, kall), lo)):
                    dq_ref[:, 128 * (2 * kh + g):128 * (2 * kh + g + 1)] = pair
                dkall = _dg(ds, q4, TN)
                dvall = _dg(p.astype(BF16), do4b, TN)
                dsk = psink * delta
                for e in range(SWA_G):
                    h = SWA_G * kh + e
                    dsink_ref[h:h + 1, :] += jnp.broadcast_to(
                        -jnp.sum(dsk[e * BLK:(e + 1) * BLK], axis=0, keepdims=True), (1, 128))
                dk_ref[:, ls] = ck_ref[:, ls] + dkall[0:BLK]
                dv_ref[:, ls] = cv_ref[:, ls] + dvall[0:BLK]
                ck_ref[:, ls] = dkall[BLK:2 * BLK]
                cv_ref[:, ls] = dvall[BLK:2 * BLK]
                dkm_ref[:, ls] += dkall[2 * BLK:3 * BLK]
                dvm_ref[:, ls] += dvall[2 * BLK:3 * BLK]

    cur = lambda w: pl.BlockSpec((BLK, w), lambda i: (jnp.minimum(i, nb - 1), 0))
    prev = lambda w: pl.BlockSpec((BLK, w), lambda i: (jnp.maximum(i - 1, 0), 0))
    first = lambda w: pl.BlockSpec((BLK, w), lambda i: (0, 0))
    return _pallas(
        body, name="swa_bwd", grid=(nb + 1,),
        in_specs=[pl.BlockSpec(memory_space=pltpu.SMEM), cur(512), prev(256), cur(256), first(256),
                  prev(256), cur(256), first(256), cur(512), cur(512)],
        out_specs=[cur(512), prev(256), prev(256), first(256), first(256), _full((SWA_QH, 128))],
        out_shape=[jax.ShapeDtypeStruct((t, SWA_W), F32), jax.ShapeDtypeStruct((t, 256), F32),
                   jax.ShapeDtypeStruct((t, 256), F32), jax.ShapeDtypeStruct((BLK, 256), F32),
                   jax.ShapeDtypeStruct((BLK, 256), F32), jax.ShapeDtypeStruct((SWA_QH, 128), F32)],
        scratch_shapes=[pltpu.VMEM((BLK, 256), F32), pltpu.VMEM((BLK, 256), F32)],
        args=(sinks, sq, sk, sk, sk, sv, sv, sv, o, do), hook=hook)


def _mix_out(h1, ogla, gg, oswa, wgn, wsn, wout, wpost):
    t = h1.shape[0]
    tm = _row_tile(t)

    def body(h_ref, og_ref, gg_ref, os_ref, wgn_ref, wsn_ref, wout_ref, wpost_ref, h2_ref, cat_ref, m_ref):
        parts = []
        for h in range(GLA_HEADS):
            ls = slice(GLA_DV * h, GLA_DV * (h + 1))
            y, _, _ = _rms(og_ref[:, ls], wgn_ref[...])
            g = gg_ref[:, ls]
            parts.append(y * (g * _sigmoid(g)))
        ys, _, _ = _rms(os_ref[...], wsn_ref[...])
        cat = jnp.concatenate(parts + [ys], axis=1).astype(BF16)
        cat_ref[...] = cat
        m = _dot(cat, wout_ref[...])
        m_ref[...] = m
        y, _, _ = _rms(m, wpost_ref[...])
        h2_ref[...] = h_ref[...] + y

    def row(w):
        return pl.BlockSpec((tm, w), lambda i: (i, 0))

    return pl.pallas_call(
        body, name="mix_out", grid=(t // tm,),
        in_specs=[row(D_MODEL), row(512), row(512), row(512), _full((1, GLA_DV)), _full((1, SWA_W)),
                  _full((D_MODEL, D_MODEL)), _full((1, D_MODEL))],
        out_specs=[row(D_MODEL), row(D_MODEL), row(D_MODEL)],
        out_shape=[jax.ShapeDtypeStruct((t, D_MODEL), F32), jax.ShapeDtypeStruct((t, D_MODEL), BF16),
                   jax.ShapeDtypeStruct((t, D_MODEL), F32)],
        compiler_params=_cparams(1),
    )(h1, ogla, gg, oswa, wgn, wsn, wout, wpost)


def _mix_out_bwd(dh2, m, ogla, gg, oswa, wgn, wsn, wout, wpost, hook=None):
    t = dh2.shape[0]
    tm = _row_tile(t)

    def body(dh_ref, m_ref, og_ref, gg_ref, os_ref, wgn_ref, wsn_ref, wout_ref, wpost_ref,
             dog_ref, dgg_ref, dos_ref, dm_ref, dwpost_ref, dwgn_ref, dwsn_ref):
        @pl.when(pl.program_id(0) == 0)
        def _():
            dwpost_ref[...] = jnp.zeros_like(dwpost_ref)
            dwgn_ref[...] = jnp.zeros_like(dwgn_ref)
            dwsn_ref[...] = jnp.zeros_like(dwsn_ref)

        wpost = wpost_ref[...]
        _, mh, r = _rms(m_ref[...], wpost)
        dm, dw = _rms_bwd(mh, r, wpost, dh_ref[...])
        dwpost_ref[...] += dw
        dmb = dm.astype(BF16)
        dm_ref[...] = dmb
        dcat = _dg(dmb, wout_ref[...], NT)
        wgn = wgn_ref[...]
        for h in range(GLA_HEADS):
            ls = slice(GLA_DV * h, GLA_DV * (h + 1))
            dog = dcat[:, ls]
            g = gg_ref[:, ls]
            sg = _sigmoid(g)
            y, xh, r = _rms(og_ref[:, ls], wgn)
            dgg_ref[:, ls] = dog * y * (sg * (1.0 + g * (1.0 - sg)))
            dx, dw = _rms_bwd(xh, r, wgn, dog * (g * sg))
            dog_ref[:, ls] = dx
            dwgn_ref[...] += dw
        wsn = wsn_ref[...]
        _, xh, r = _rms(os_ref[...], wsn)
        dx, dw = _rms_bwd(xh, r, wsn, dcat[:, GLA_W:])
        dos_ref[...] = dx
        dwsn_ref[...] += dw

    def row(w):
        return pl.BlockSpec((tm, w), lambda i: (i, 0))

    def rshape(w, dt=F32):
        return jax.ShapeDtypeStruct((t, w), dt)

    return _pallas(
        body, name="mix_out_bwd", grid=(t // tm,),
        in_specs=[row(D_MODEL), row(D_MODEL), row(512), row(512), row(512), _full((1, GLA_DV)), _full((1, SWA_W)),
                  _full((D_MODEL, D_MODEL)), _full((1, D_MODEL))],
        out_specs=[row(512), row(512), row(512), row(D_MODEL), _full((1, D_MODEL)), _full((1, GLA_DV)),
                   _full((1, SWA_W))],
        out_shape=[rshape(512), rshape(512), rshape(512), rshape(D_MODEL, BF16),
                   jax.ShapeDtypeStruct((1, D_MODEL), F32), jax.ShapeDtypeStruct((1, GLA_DV), F32),
                   jax.ShapeDtypeStruct((1, SWA_W), F32)],
        args=(dh2, m, ogla, gg, oswa, wgn, wsn, wout, wpost), hook=hook)


def _mix_in_bwd(dh2, h1, wmixpre, winp, wa2p, bap, cos, sin, ga, dgq, dgk, dgv, dgg, dla, dsq, dsk, dsv, dkm, dvm):
    t = h1.shape[0]
    tm = _row_tile(t)

    def body(dh2_ref, h_ref, w_ref, win_ref, wa2_ref, ba_ref, cos_ref, sin_ref, ga_ref, dgq_ref, dgk_ref, dgv_ref,
             dgg_ref, dla_ref, dsq_ref, dsk_ref, dsv_ref, dkm_ref, dvm_ref,
             dh1_ref, dproj_ref, dw_ref, dwa2_ref, dba_ref):
        i = pl.program_id(0)

        @pl.when(i == 0)
        def _():
            dw_ref[...] = jnp.zeros_like(dw_ref)
            dwa2_ref[...] = jnp.zeros_like(dwa2_ref)
            dba_ref[...] = jnp.zeros_like(dba_ref)

        first = (i == 0).astype(F32)
        c = cos_ref[...]
        s = -sin_ref[...]
        fh = _first_half_mask(tm)
        dproj_ref[:, P_GQ:P_GK] = dgq_ref[...].astype(BF16)
        dproj_ref[:, P_GK:P_GV] = dgk_ref[...].astype(BF16)
        dproj_ref[:, P_GV:P_GG] = dgv_ref[...].astype(BF16)
        dproj_ref[:, P_GG:P_GA] = dgg_ref[...].astype(BF16)
        gab = ga_ref[...].astype(BF16)
        z = _dot(gab, wa2_ref[...]) + ba_ref[...]
        row_id = i * tm + lax.broadcasted_iota(jnp.int32, (tm, 1), 0)
        dz = jnp.where(row_id >= PAD, dla_ref[...] * (1.0 / GLA_TAU) * (1.0 - _sigmoid(z)), 0.0)
        dzb = dz.astype(BF16)
        dba_ref[...] += jnp.sum(dz, axis=0, keepdims=True)
        dwa2_ref[...] += _dg(gab, dzb, TN)
        dproj_ref[:, P_GA:P_SQ] = _dg(dzb, wa2_ref[...], NT).astype(BF16)
        for k in range(4):
            dy = dsq_ref[:, 128 * k:128 * (k + 1)]
            dproj_ref[:, P_SQ + 128 * k:P_SQ + 128 * (k + 1)] = (dy * c + _rot_half(dy, fh) * s).astype(BF16)
        for k in range(2):
            ls = slice(128 * k, 128 * (k + 1))
            dy = dsk_ref[:, ls]
            dy = jnp.concatenate([dy[:BLK] + first * dkm_ref[:, ls], dy[BLK:]], axis=0) if tm > BLK else (
                dy + first * dkm_ref[:, ls])
            dproj_ref[:, P_SK + 128 * k:P_SK + 128 * (k + 1)] = (dy * c + _rot_half(dy, fh) * s).astype(BF16)
            dv = dsv_ref[:, ls]
            dv = jnp.concatenate([dv[:BLK] + first * dvm_ref[:, ls], dv[BLK:]], axis=0) if tm > BLK else (
                dv + first * dvm_ref[:, ls])
            dproj_ref[:, P_SV + 128 * k:P_SV + 128 * (k + 1)] = dv.astype(BF16)
        dn = _dg(dproj_ref[...], win_ref[...], NT)
        w = w_ref[...]
        _, hh, r = _rms(h_ref[...], w)
        dx, dw = _rms_bwd(hh, r, w, dn)
        dw_ref[...] += dw
        dh1_ref[...] = dh2_ref[...] + dx

    def row(w):
        return pl.BlockSpec((tm, w), lambda i: (i, 0))

    return pl.pallas_call(
        body, name="mix_in_bwd", grid=(t // tm,),
        in_specs=[row(D_MODEL), row(D_MODEL), _full((1, D_MODEL)), _full((D_MODEL, P_END)), _full((128, GLA_KW)),
                  _full((1, GLA_KW)), row(128), row(128), row(128), row(256), row(256), row(512), row(512), row(256),
                  row(512), row(256), row(256), _full((BLK, 256)), _full((BLK, 256))],
        out_specs=[row(D_MODEL), row(P_END), _full((1, D_MODEL)), _full((128, GLA_KW)), _full((1, GLA_KW))],
        out_shape=[jax.ShapeDtypeStruct((t, D_MODEL), F32), jax.ShapeDtypeStruct((t, P_END), BF16),
                   jax.ShapeDtypeStruct((1, D_MODEL), F32), jax.ShapeDtypeStruct((128, GLA_KW), F32),
                   jax.ShapeDtypeStruct((1, GLA_KW), F32)],
        compiler_params=_cparams(1),
    )(dh2, h1, wmixpre, winp, wa2p, bap, cos, sin, ga, dgq, dgk, dgv, dgg, dla, dsq, dsk, dsv, dkm, dvm)


def _adamw_update(w, g, m, v):
    m = ADAM_B1 * m + (1.0 - ADAM_B1) * g
    v = ADAM_B2 * v + (1.0 - ADAM_B2) * (g * g)
    m_hat = m / (1.0 - ADAM_B1 ** ADAM_STEP)
    v_hat = v / (1.0 - ADAM_B2 ** ADAM_STEP)
    return -ADAM_LR * (m_hat / (jnp.sqrt(v_hat) + ADAM_EPS) + ADAM_WD * w), m, v


def _adamw(w, g, m, v):
    r, c = w.shape
    tr = _div_tile(r)

    def body(w_ref, g_ref, m_ref, v_ref, d_ref, nm_ref, nv_ref):
        d_ref[...], nm_ref[...], nv_ref[...] = _adamw_update(w_ref[...], g_ref[...], m_ref[...], v_ref[...])

    spec = pl.BlockSpec((tr, c), lambda i: (i, 0))
    shape = jax.ShapeDtypeStruct((r, c), F32)
    return pl.pallas_call(
        body, name="adamw", grid=(r // tr,), in_specs=[spec] * 4, out_specs=[spec] * 3, out_shape=[shape] * 3,
        compiler_params=_cparams(1),
    )(w, g, m, v)


def _adamw_halves(w, g_mine, g_other, m, v, c_idx, row0=0):
    r, c = w.shape
    h = g_mine.shape[0]
    tr = _div_tile(math.gcd(r, h))
    nth = h // tr
    t0 = row0 // tr
    assert t0 * tr == row0

    def body(c_ref, w_ref, gm_ref, go_ref, m_ref, v_ref, g_ref, d_ref, nm_ref, nv_ref):
        hh = (t0 + pl.program_id(0)) // nth
        g = jnp.where(hh == c_ref[0], gm_ref[...], go_ref[...])
        g_ref[...] = g
        d_ref[...], nm_ref[...], nv_ref[...] = _adamw_update(w_ref[...], g, m_ref[...], v_ref[...])

    spec = pl.BlockSpec((tr, c), lambda i, c_ref: (i, 0))
    gspec = pl.BlockSpec((tr, c), lambda i, c_ref: ((t0 + i) % nth, 0))
    shape = jax.ShapeDtypeStruct((r, c), F32)
    return pl.pallas_call(
        body, name="adamw_halves",
        grid_spec=pltpu.PrefetchScalarGridSpec(
            num_scalar_prefetch=1, grid=(r // tr,), in_specs=[spec, gspec, gspec, spec, spec], out_specs=[spec] * 4),
        out_shape=[shape] * 4, compiler_params=_cparams(1),
    )(c_idx, w, g_mine, g_other, m, v)


def _place():
    x, y, c = lax.axis_index("x"), lax.axis_index("y"), lax.axis_index("c")
    chips = [(1 - x, y), (x, 1 - y), (1 - x, 1 - y)]
    return x, y, c, chips


def _remote(send_sem, recv_sem, src, dst, to):
    return pltpu.make_async_remote_copy(src_ref=src, dst_ref=dst, send_sem=send_sem, recv_sem=recv_sem,
                                        device_id=to, device_id_type=MESH)


def _half(ref_rows, c):
    h = ref_rows // 2
    return pl.ds(pl.multiple_of(c * h, 8), h)


def _own_slot(shard, q):
    return lax.dynamic_update_slice(jnp.zeros((N_CHIPS,) + shard.shape, shard.dtype), shard[None], (q, 0, 0))


class _GatherChips:
    has_mid = True

    def __init__(self, bufs):
        n = len(bufs)
        self.inputs = list(bufs)
        self.out_shape = [jax.ShapeDtypeStruct(b.shape, b.dtype) for b in bufs]
        self.aliases = [(t, t) for t in range(n)]
        self.scratch = [pltpu.SemaphoreType.DMA((n, 6)), pltpu.SemaphoreType.DMA((n, 6))]

    def start(self, ins, outs, scr):
        send, recv = scr
        x, y, c, chips = _place()
        q = 2 * x + y
        for t, (i_ref, o_ref) in enumerate(zip(ins, outs)):
            rows = _half(i_ref.shape[1], c)
            for j, (cx, cy) in enumerate(chips):
                _remote(send.at[t, j], recv.at[t, j], i_ref.at[q, rows], o_ref.at[q, rows], (cx, cy, c)).start()

    def mid(self, ins, outs, scr):
        send, recv = scr
        x, y, c, chips = _place()
        for t, o_ref in enumerate(outs):
            rows = _half(o_ref.shape[1], c)
            for j, (cx, cy) in enumerate(chips):
                slot = o_ref.at[2 * cx + cy, rows]
                _remote(send.at[t, j], recv.at[t, j], slot, slot, (cx, cy, c)).wait_recv()
                _remote(send.at[t, 3 + j], recv.at[t, 3 + j], slot, slot, (x, y, 1 - c)).start()

    def finish(self, ins, outs, scr):
        send, recv = scr
        x, y, c, chips = _place()
        for t, o_ref in enumerate(outs):
            mine, other = _half(o_ref.shape[1], c), _half(o_ref.shape[1], 1 - c)
            for j, (cx, cy) in enumerate(chips):
                slot = o_ref.at[2 * cx + cy, other]
                _remote(send.at[t, 3 + j], recv.at[t, 3 + j], slot, slot, (x, y, 1 - c)).wait_recv()
            for j, (cx, cy) in enumerate(chips):
                sent = o_ref.at[2 * cx + cy, mine]
                _remote(send.at[t, j], recv.at[t, j], sent, sent, (cx, cy, c)).wait_send()
                _remote(send.at[t, 3 + j], recv.at[t, 3 + j], sent, sent, (x, y, 1 - c)).wait_send()


class _PairExchange:
    has_mid = False
    aliases = ()

    def __init__(self, arrs):
        n = len(arrs)
        self.inputs = list(arrs)
        self.out_shape = [jax.ShapeDtypeStruct((a.shape[0], a.shape[1] // 2, a.shape[2]), a.dtype) for a in arrs]
        self.scratch = [pltpu.SemaphoreType.DMA((n,)), pltpu.SemaphoreType.DMA((n,))]

    def _copies(self, ins, outs, scr):
        send, recv = scr
        x, y, c, _ = _place()
        return [_remote(send.at[t], recv.at[t], i_ref.at[:, _half(i_ref.shape[1], 1 - c)], o_ref, (x, y, 1 - c))
                for t, (i_ref, o_ref) in enumerate(zip(ins, outs))]

    def start(self, ins, outs, scr):
        for cp in self._copies(ins, outs, scr):
            cp.start()

    def finish(self, ins, outs, scr):
        for cp in self._copies(ins, outs, scr):
            cp.wait()


class _ChipScatter:
    has_mid = False
    aliases = ()

    def __init__(self, arrs):
        n = len(arrs)
        self.inputs = list(arrs)
        self.out_shape = [jax.ShapeDtypeStruct((3,) + a.shape[1:], a.dtype) for a in arrs]
        self.scratch = [pltpu.SemaphoreType.DMA((n, 3)), pltpu.SemaphoreType.DMA((n, 3))]

    def _copies(self, ins, outs, scr):
        send, recv = scr
        x, y, c, chips = _place()
        return [_remote(send.at[t, j], recv.at[t, j], i_ref.at[2 * cx + cy], o_ref.at[j], (cx, cy, c))
                for t, (i_ref, o_ref) in enumerate(zip(ins, outs)) for j, (cx, cy) in enumerate(chips)]

    def start(self, ins, outs, scr):
        for cp in self._copies(ins, outs, scr):
            cp.start()

    def finish(self, ins, outs, scr):
        for cp in self._copies(ins, outs, scr):
            cp.wait()


class _PairShare:
    has_mid = False
    aliases = ()

    def __init__(self, arrs):
        n = len(arrs)
        self.inputs = list(arrs)
        self.out_shape = [jax.ShapeDtypeStruct(a.shape, a.dtype) for a in arrs]
        self.scratch = [pltpu.SemaphoreType.DMA((n,)), pltpu.SemaphoreType.DMA((n,))]

    def _copies(self, ins, outs, scr):
        send, recv = scr
        x, y, c, _ = _place()
        return [_remote(send.at[t], recv.at[t], i_ref, o_ref, (x, y, 1 - c))
                for t, (i_ref, o_ref) in enumerate(zip(ins, outs))]

    def start(self, ins, outs, scr):
        for cp in self._copies(ins, outs, scr):
            cp.start()

    def finish(self, ins, outs, scr):
        for cp in self._copies(ins, outs, scr):
            cp.wait()


def _comm_call(hook, name):
    n_in, n_out = len(hook.inputs), len(hook.out_shape)

    def body(*refs):
        ins, outs, scr = refs[:n_in], refs[n_in:n_in + n_out], refs[n_in + n_out:]
        hook.start(ins, outs, scr)
        if hook.has_mid:
            hook.mid(ins, outs, scr)
        hook.finish(ins, outs, scr)

    return pl.pallas_call(body, name=name, in_specs=[ANY] * n_in, out_specs=[ANY] * n_out,
                          out_shape=list(hook.out_shape), scratch_shapes=list(hook.scratch),
                          input_output_aliases=dict(hook.aliases))(*hook.inputs)


def _all_gather_devices(vec):
    r, w = vec.shape

    def body(x_ref, out_ref, send_sems, recv_sems, local_sem):
        x, y, c, chips = _place()
        me, sibling = (x, y, c), (x, y, 1 - c)

        def rows(px, py, pc):
            return out_ref.at[4 * px + 2 * py + pc]

        def copy(k, block, to, src=None):
            return pltpu.make_async_remote_copy(
                src_ref=rows(*block) if src is None else src, dst_ref=rows(*block), send_sem=send_sems.at[k],
                recv_sem=recv_sems.at[k], device_id=to, device_id_type=MESH)

        mine = pltpu.make_async_copy(x_ref, rows(*me), local_sem)
        mine.start()
        first = [copy(0, me, sibling, src=x_ref)]
        first += [copy(1 + j, me, (*chip, c), src=x_ref) for j, chip in enumerate(chips)]
        for cp in first:
            cp.start()
        passed = [copy(4 + j, (*chip, c), sibling) for j, chip in enumerate(chips)]
        for j, chip in enumerate(chips):
            copy(1 + j, (*chip, c), me).wait_recv()
            passed[j].start()
        copy(0, sibling, me).wait_recv()
        for j, chip in enumerate(chips):
            copy(4 + j, (*chip, 1 - c), me).wait_recv()
        for cp in first + passed:
            cp.wait_send()
        mine.wait()

    return pl.pallas_call(
        body, name="all_gather_devices",
        in_specs=[pl.BlockSpec(memory_space=pltpu.VMEM)], out_specs=pl.BlockSpec(memory_space=pltpu.VMEM),
        out_shape=jax.ShapeDtypeStruct((N_DEV, r, w), vec.dtype),
        scratch_shapes=[pltpu.SemaphoreType.DMA((7,)), pltpu.SemaphoreType.DMA((7,)), pltpu.SemaphoreType.DMA],
    )(vec)


def _pair_sum(g, other, c_idx):
    nq, r, w = g.shape
    h = r // 2
    tr = _div_tile(h)
    nt = h // tr

    def body(c_ref, g_ref, o_ref, s_ref):
        s_ref[...] = (g_ref[...].astype(F32) + o_ref[...].astype(F32)).astype(s_ref.dtype)

    return pl.pallas_call(
        body, name="pair_sum",
        grid_spec=pltpu.PrefetchScalarGridSpec(
            num_scalar_prefetch=1, grid=(nq, nt),
            in_specs=[pl.BlockSpec((None, tr, w), lambda k, i, c_ref: (k, c_ref[0] * nt + i, 0)),
                      pl.BlockSpec((None, tr, w), lambda k, i, c_ref: (k, i, 0))],
            out_specs=pl.BlockSpec((None, tr, w), lambda k, i, c_ref: (k, i, 0))),
        out_shape=jax.ShapeDtypeStruct((nq, h, w), g.dtype),
        compiler_params=_cparams(2),
    )(c_idx, g, other)


def _chip_sum(s, others, q_idx):
    _, h, w = s.shape
    tr = _div_tile(h)

    def body(q_ref, s_ref, o_ref, out_ref):
        out_ref[...] = ((s_ref[...].astype(F32) + o_ref[0].astype(F32)) + o_ref[1].astype(F32)) + o_ref[2].astype(F32)

    return pl.pallas_call(
        body, name="chip_sum",
        grid_spec=pltpu.PrefetchScalarGridSpec(
            num_scalar_prefetch=1, grid=(h // tr,),
            in_specs=[pl.BlockSpec((None, tr, w), lambda i, q_ref: (q_ref[0], i, 0)),
                      pl.BlockSpec((3, tr, w), lambda i, q_ref: (0, i, 0))],
            out_specs=pl.BlockSpec((tr, w), lambda i, q_ref: (i, 0))),
        out_shape=jax.ShapeDtypeStruct((h, w), F32),
        compiler_params=_cparams(1),
    )(q_idx, s, others)


def _sum_devices(parts):
    nd, r, w = parts.shape

    def body(p_ref, o_ref):
        acc = p_ref[0]
        for k in range(1, nd):
            acc = acc + p_ref[k]
        o_ref[...] = acc

    return pl.pallas_call(
        body, name="sum_devices", in_specs=[_full((nd, r, w))], out_specs=_full((r, w)),
        out_shape=jax.ShapeDtypeStruct((r, w), parts.dtype), grid=(1,), compiler_params=_cparams(1),
    )(parts)


def _pack_win(w_in):
    o = np.cumsum((0,) + IN_SPLITS)
    gq, gk, gv, gg, ga, sq, sk, sv = [w_in[:, o[i]:o[i + 1]] for i in range(8)]
    z = jnp.zeros((w_in.shape[0], 128 - GLA_RANK), w_in.dtype)
    dup = lambda a: jnp.concatenate([a[:, :64], a[:, :64], a[:, 64:], a[:, 64:]], axis=1)
    return jnp.concatenate([gq, gk, gv, gg, ga, z, sq, dup(sk), dup(sv)], axis=1)


def _unpack_dwin(d):
    und = lambda a: jnp.concatenate([a[:, 0:64] + a[:, 64:128], a[:, 128:192] + a[:, 192:256]], axis=1)
    return jnp.concatenate([d[:, :P_GA], d[:, P_GA:P_GA + GLA_RANK], d[:, P_SQ:P_SK], und(d[:, P_SK:P_SV]),
                            und(d[:, P_SV:P_END])], axis=1)


def _local_step(x, target, meta, p):
    s = x.shape[0]
    t = s + BLK
    h0 = jnp.concatenate([jnp.zeros((PAD, D_MODEL), F32), meta, x], axis=0)
    cos, sin = _rope_tables(t)

    h1, n1, g1, u1, a1, f1 = _ffn_fwd(h0, p["ffn1_pre_norm"], p["ffn1_w_gate"], p["ffn1_w_up"], p["ffn1_w_down"],
                                      p["ffn1_post_norm"])
    n2, gq, gk, gv, gg, ga, la, sq, sk, sv = _mix_proj(h1, p["mix_pre_norm"], p["w_in"], p["gla_w_a2"], p["gla_b_a"],
                                                       cos, sin)
    ogla, ss = _gla_fwd(gq, gk, gv, la)
    oswa = _swa_fwd(p["swa_sinks"], sq, sk, sv)
    h2, cat, m = _mix_out(h1, ogla, gg, oswa, p["gla_out_norm"], p["swa_out_norm"], p["w_out"], p["mix_post_norm"])
    dy, n3, g3, u3, a3, f3, sse = _ffn_fwd(h2, p["ffn2_pre_norm"], p["ffn2_w_gate"], p["ffn2_w_up"],
                                           p["ffn2_w_down"], p["ffn2_post_norm"], target=target)

    grads = {}
    dh2, df3, dg3, du3, grads["ffn2_pre_norm"], grads["ffn2_post_norm"] = _ffn_bwd(
        dy, h2, f3, g3, u3, p["ffn2_pre_norm"], p["ffn2_w_gate"], p["ffn2_w_up"], p["ffn2_w_down"],
        p["ffn2_post_norm"])
    (gud,) = _ffn_wgrad(n3, df3, dg3, du3, a3)
    grads["ffn2_w_gate"], grads["ffn2_w_up"], grads["ffn2_w_down"] = gud[:, :FJ], gud[:, FJ:2 * FJ], gud[:, 2 * FJ:]

    dogla, dgg, doswa, dm, grads["mix_post_norm"], grads["gla_out_norm"], grads["swa_out_norm"] = _mix_out_bwd(
        dh2, m, ogla, gg, oswa, p["gla_out_norm"], p["swa_out_norm"], p["w_out"], p["mix_post_norm"])
    grads["w_out"] = _xty(cat, dm)
    dsq, dsk, dsv, dkm, dvm, dsinks = _swa_bwd(p["swa_sinks"], sq, sk, sv, oswa, doswa)
    grads["swa_sinks"] = dsinks[:, 0]
    dgq, dgk, dgv, dla = _gla_bwd(gq, gk, gv, la, ss, dogla)
    dh1, dproj, grads["mix_pre_norm"], dwa2p, grads["gla_b_a"] = _mix_in_bwd(
        dh2, h1, p["mix_pre_norm"], p["w_in"], p["gla_w_a2"], p["gla_b_a"], cos, sin, ga, dgq, dgk, dgv, dgg, dla,
        dsq, dsk, dsv, dkm, dvm)
    grads["gla_w_a2"] = dwa2p[:GLA_RANK]
    grads["w_in"] = _unpack_dwin(_xty(n2, dproj))

    dh0, df1, dg1, du1, grads["ffn1_pre_norm"], grads["ffn1_post_norm"] = _ffn_bwd(
        dh1, h0, f1, g1, u1, p["ffn1_pre_norm"], p["ffn1_w_gate"], p["ffn1_w_up"], p["ffn1_w_down"],
        p["ffn1_post_norm"])
    (gud,) = _ffn_wgrad(n1, df1, dg1, du1, a1)
    grads["ffn1_w_gate"], grads["ffn1_w_up"], grads["ffn1_w_down"] = gud[:, :FJ], gud[:, FJ:2 * FJ], gud[:, 2 * FJ:]
    grads["meta_tokens"] = dh0[PAD:BLK]
    return sse[0, 0], dh0[BLK:], grads


WEIGHTS = ['meta_tokens', 'ffn1_pre_norm', 'ffn1_w_gate', 'ffn1_w_up', 'ffn1_w_down', 'ffn1_post_norm',
           'mix_pre_norm', 'w_in', 'gla_w_a2', 'gla_b_a', 'gla_out_norm', 'swa_sinks', 'swa_out_norm', 'w_out',
           'mix_post_norm', 'ffn2_pre_norm', 'ffn2_w_gate', 'ffn2_w_up', 'ffn2_w_down', 'ffn2_post_norm']
BIG = ['ffn1_w_gate', 'ffn1_w_up', 'ffn1_w_down', 'w_in', 'w_out', 'ffn2_w_gate', 'ffn2_w_up', 'ffn2_w_down']
SMALL = [n for n in WEIGHTS if n not in BIG]
FJ = D_FF // N_CHIPS
D_IN_J = D_IN // N_CHIPS
D_OUT_J = D_MODEL // N_CHIPS
TRANSPOSED = ('ffn1_w_gate', 'ffn1_w_up', 'ffn2_w_gate', 'ffn2_w_up')


def _shard2d(name, a):
    return a[0].T if name in TRANSPOSED else a[0]


def _unshard2d(name, a):
    return (a.T if name in TRANSPOSED else a)[None]


def _small_rows(name, a):
    flat = a.reshape(-1)
    rows = -(-flat.shape[0] // 1024) * 8
    return jnp.pad(flat, (0, rows * 128 - flat.shape[0])).reshape(rows, 128)


def kernel(x, meta_tokens, ffn1_pre_norm, ffn1_w_gate, ffn1_w_up, ffn1_w_down, ffn1_post_norm, mix_pre_norm, w_in, gla_w_a2, gla_b_a, gla_out_norm, swa_sinks, swa_out_norm, w_out, mix_post_norm, ffn2_pre_norm, ffn2_w_gate, ffn2_w_up, ffn2_w_down, ffn2_post_norm, loss_target, m_meta_tokens, m_ffn1_pre_norm, m_ffn1_w_gate, m_ffn1_w_up, m_ffn1_w_down, m_ffn1_post_norm, m_mix_pre_norm, m_w_in, m_gla_w_a2, m_gla_b_a, m_gla_out_norm, m_swa_sinks, m_swa_out_norm, m_w_out, m_mix_post_norm, m_ffn2_pre_norm, m_ffn2_w_gate, m_ffn2_w_up, m_ffn2_w_down, m_ffn2_post_norm, v_meta_tokens, v_ffn1_pre_norm, v_ffn1_w_gate, v_ffn1_w_up, v_ffn1_w_down, v_ffn1_post_norm, v_mix_pre_norm, v_w_in, v_gla_w_a2, v_gla_b_a, v_gla_out_norm, v_swa_sinks, v_swa_out_norm, v_w_out, v_mix_post_norm, v_ffn2_pre_norm, v_ffn2_w_gate, v_ffn2_w_up, v_ffn2_w_down, v_ffn2_post_norm):
    args = dict(locals())
    w = {n: args[n] for n in WEIGHTS}
    mom = {n: args["m_" + n] for n in WEIGHTS}
    var = {n: args["v_" + n] for n in WEIGHTS}
    cx, cy, cc = lax.axis_index("x"), lax.axis_index("y"), lax.axis_index("c")
    q_idx = (2 * cx + cy).astype(jnp.int32).reshape(1)
    c_idx = cc.astype(jnp.int32).reshape(1)

    q_chip = 2 * cx + cy
    bf = {n: _own_slot(_shard2d(n, w[n]).astype(BF16), q_chip) for n in BIG}
    early = _GatherChips([bf["ffn1_w_gate"], bf["ffn1_w_up"], bf["ffn1_w_down"], _own_slot(w["meta_tokens"], q_chip),
                          _own_slot(w["gla_w_a2"].reshape(GLA_RANK, GLA_KW // N_CHIPS), q_chip)])
    wg1, wu1, wd1, meta4, wa24 = _comm_call(early, "gather_ffn1")
    meta_full = meta4.transpose(1, 0, 2).reshape(N_META, D_MODEL)
    wa2p = jnp.pad(wa24.transpose(1, 0, 2).reshape(GLA_RANK, GLA_KW), ((0, 128 - GLA_RANK), (0, 0))).astype(BF16)
    sinks = w["swa_sinks"].reshape(SWA_QH)

    seq, target = x[0], loss_target[0]
    t = seq.shape[0] + BLK
    h0 = jnp.concatenate([jnp.zeros((PAD, D_MODEL), F32), meta_full, seq], axis=0)
    cos, sin = _rope_tables(t)
    late = _GatherChips([bf["w_in"], bf["w_out"], bf["ffn2_w_gate"], bf["ffn2_w_up"], bf["ffn2_w_down"]])
    (h1, n1, g1, u1, a1, f1), (win4, wout4, wg2, wu2, wd2) = _ffn_fwd(
        h0, w["ffn1_pre_norm"], wg1, wu1, wd1, w["ffn1_post_norm"], hook=late)
    winp = _pack_win(win4.transpose(1, 0, 2).reshape(D_MODEL, D_IN))
    wout = wout4.reshape(D_MODEL, D_MODEL)
    n2, gq, gk, gv, gg, ga, la, sq, sk, sv = _mix_proj(h1, w["mix_pre_norm"], winp, wa2p, w["gla_b_a"], cos, sin)
    ogla, ss = _gla_fwd(gq, gk, gv, la)
    oswa = _swa_fwd(sinks, sq, sk, sv)
    h2, cat, m = _mix_out(h1, ogla, gg, oswa, w["gla_out_norm"], w["swa_out_norm"], wout, w["mix_post_norm"])
    dy, n3, g3, u3, a3, f3, sse = _ffn_fwd(h2, w["ffn2_pre_norm"], wg2, wu2, wd2, w["ffn2_post_norm"], target=target)
    loss = lax.psum(sse[0, 0] * (0.5 / D_MODEL), ("x", "y", "c"))

    g = {}
    dh2, df3, dg3, du3, g["ffn2_pre_norm"], g["ffn2_post_norm"] = _ffn_bwd(
        dy, h2, f3, g3, u3, w["ffn2_pre_norm"], wg2, wu2, wd2, w["ffn2_post_norm"])
    (gf2,) = _ffn_wgrad(n3, df3, dg3, du3, a3)
    (dogla, dgg, doswa, dm, g["mix_post_norm"], g["gla_out_norm"], g["swa_out_norm"]), (rgf2,) = _mix_out_bwd(
        dh2, m, ogla, gg, oswa, w["gla_out_norm"], w["swa_out_norm"], wout, w["mix_post_norm"],
        hook=_PairExchange([gf2]))
    sgf2 = _pair_sum(gf2, rgf2, c_idx)
    gout = _xty(cat, dm).reshape(N_CHIPS, D_OUT_J, D_MODEL).astype(BF16)
    (dsq, dsk, dsv, dkm, dvm, dsinks), (ogf2,) = _swa_bwd(sinks, sq, sk, sv, oswa, doswa,
                                                          hook=_ChipScatter([sgf2]))
    g["swa_sinks"] = dsinks[:, 0].reshape(1, SWA_QH)
    dgq, dgk, dgv, dla = _gla_bwd(gq, gk, gv, la, ss, dogla)
    dh1, dproj, g["mix_pre_norm"], dwa2p, g["gla_b_a"] = _mix_in_bwd(
        dh2, h1, w["mix_pre_norm"], winp, wa2p, w["gla_b_a"], cos, sin, ga, dgq, dgk, dgv, dgg, dla,
        dsq, dsk, dsv, dkm, dvm)
    g["gla_w_a2"] = dwa2p[:GLA_RANK]
    gin = _unpack_dwin(_xty(n2, dproj)).reshape(D_MODEL, N_CHIPS, D_IN_J).transpose(1, 0, 2).astype(BF16)
    (dh0, df1, dg1, du1, g["ffn1_pre_norm"], g["ffn1_post_norm"]), (rgin, rgout) = _ffn_bwd(
        dh1, h0, f1, g1, u1, w["ffn1_pre_norm"], wg1, wu1, wd1, w["ffn1_post_norm"],
        hook=_PairExchange([gin, gout]))
    sgin, sgout = _pair_sum(gin, rgin, c_idx), _pair_sum(gout, rgout, c_idx)
    (gf1,), (ogin, ogout) = _ffn_wgrad(n1, df1, dg1, du1, a1, hook=_ChipScatter([sgin, sgout]))
    g["meta_tokens"] = dh0[PAD:BLK]
    grad_x = dh0[BLK:]
    (rgf1,) = _comm_call(_PairExchange([gf1]), "pair_exchange_ffn1")
    sgf1 = _pair_sum(gf1, rgf1, c_idx)
    (ogf1,) = _comm_call(_ChipScatter([sgf1]), "chip_scatter_ffn1")
    halves = [_chip_sum(s, o, q_idx) for s, o in ((sgf1, ogf1), (sgin, ogin), (sgout, ogout), (sgf2, ogf2))]
    others = _comm_call(_PairShare(halves), "pair_share")
    reduced = {"ffn1_w_gate": (0, 0), "ffn1_w_up": (0, FJ), "ffn1_w_down": (0, 2 * FJ), "w_in": (1, 0),
               "w_out": (2, 0), "ffn2_w_gate": (3, 0), "ffn2_w_up": (3, FJ), "ffn2_w_down": (3, 2 * FJ)}
    grad, delta, new_m, new_v = {}, {}, {}, {}
    for n in BIG:
        k, row0 = reduced[n]
        outs = _adamw_halves(_shard2d(n, w[n]), halves[k], others[k], _shard2d(n, mom[n]), _shard2d(n, var[n]),
                             c_idx, row0)
        grad[n], delta[n], new_m[n], new_v[n] = [_unshard2d(n, a) for a in outs]

    small_rows = [_small_rows(n, g[n]) for n in SMALL]
    ssizes = [a.shape[0] for a in small_rows]
    gsmall = _sum_devices(_all_gather_devices(jnp.concatenate(small_rows, axis=0)))
    col0 = {"meta_tokens": D_MODEL // N_CHIPS, "gla_w_a2": GLA_KW // N_CHIPS}
    gs, ws, ms, vs = [], [], [], []
    off = 0
    for n, sz in zip(SMALL, ssizes):
        full_shape = g[n].shape
        gn = gsmall[off:off + sz].reshape(-1)[:math.prod(full_shape)].reshape(full_shape)
        off += sz
        if n in col0:
            gn = lax.dynamic_slice_in_dim(gn, (2 * cx + cy) * col0[n], col0[n], axis=1)
        grad[n] = gn.reshape(w[n].shape)
        gs.append(_small_rows(n, grad[n]))
        ws.append(_small_rows(n, w[n]))
        ms.append(_small_rows(n, mom[n]))
        vs.append(_small_rows(n, var[n]))
    psizes = [a.shape[0] for a in gs]
    d, nm, nv = _adamw(*[jnp.concatenate(a, axis=0) for a in (ws, gs, ms, vs)])
    off = 0
    for n, sz in zip(SMALL, psizes):
        cnt = math.prod(w[n].shape)
        delta[n], new_m[n], new_v[n] = [a[off:off + sz].reshape(-1)[:cnt].reshape(w[n].shape) for a in (d, nm, nv)]
        off += sz

    return (loss, grad_x[None], *[grad[n] for n in WEIGHTS], *[delta[n] for n in WEIGHTS],
            *[new_m[n] for n in WEIGHTS], *[new_v[n] for n in WEIGHTS])
```

```python
import functools
import math

import numpy as np
import jax
import jax.numpy as jnp
from jax import lax
from jax.experimental import pallas as pl
from jax.experimental.pallas import tpu as pltpu

F32 = jnp.float32
BF16 = jnp.bfloat16
MESH = pl.DeviceIdType.MESH

D_MODEL = 1024
D_FF = 2816
N_CHIPS = 4
N_DEV = 8
N_META = 16
BLK = 128
PAD = BLK - N_META
GLA_CHUNK = 64
GLA_HEADS = 4
GLA_DV = 128
GLA_DK = 64
GLA_KW = GLA_HEADS * GLA_DK
GLA_W = GLA_HEADS * GLA_DV
GLA_RANK = 16
GLA_TAU = 16.0
SWA_HD = 64
SWA_QH = 8
SWA_KVH = 2
SWA_W = SWA_QH * SWA_HD
WINDOW = 128
ROPE_THETA = 10000.0
EPS = 1e-6
NEG_INF = -1e30
IN_SPLITS = (256, 256, 512, 512, 16, 512, 128, 128)
D_IN = sum(IN_SPLITS)
P_GQ, P_GK, P_GV, P_GG, P_GA, P_SQ, P_SK, P_SV, P_END = 0, 256, 512, 1024, 1536, 1664, 2176, 2432, 2688
ADAM_LR, ADAM_B1, ADAM_B2, ADAM_EPS, ADAM_WD, ADAM_STEP = 0.001, 0.9, 0.999, 1e-08, 0.01, 10
VMEM_LIMIT = 56 * 1024 * 1024

NT = (((1,), (1,)), ((), ()))
TN = (((0,), (0,)), ((), ()))


def _cparams(n_axes):
    return pltpu.CompilerParams(dimension_semantics=("arbitrary",) * n_axes, vmem_limit_bytes=VMEM_LIMIT)


def _row_tile(t):
    for tm in (640, 512, 384, 256, 128):
        if t % tm == 0:
            return tm
    raise ValueError(t)


ROW_PARTS = 2


def _row_parts(tm):
    n = ROW_PARTS if tm % (16 * ROW_PARTS) == 0 else 1
    return [slice(k * (tm // n), (k + 1) * (tm // n)) for k in range(n)]


def _contract_tile(t):
    return 1664 if t % 1664 == 0 else _row_tile(t)


def _div_tile(r, cap=512):
    best = None
    for tr in range(8, min(r, cap) + 1, 8):
        if r % tr == 0:
            best = tr
    return best if best is not None else r


def _dot(a, b):
    return jnp.dot(a, b, preferred_element_type=F32)


def _dg(a, b, dims):
    return lax.dot_general(a, b, dims, preferred_element_type=F32)


def _rms(x, w):
    r = lax.rsqrt(jnp.mean(x * x, axis=-1, keepdims=True) + EPS)
    xh = x * r
    return xh * w, xh, r


def _rms_bwd(xh, r, w, dy):
    wdy = dy * w
    dx = r * (wdy - xh * jnp.mean(wdy * xh, axis=-1, keepdims=True))
    dw = jnp.sum(dy * xh, axis=0, keepdims=True)
    return dx, dw


def _sigmoid(x):
    return 1.0 / (1.0 + jnp.exp(-x))


def _full(shape):
    nd = len(shape)
    return pl.BlockSpec(shape, lambda *_: (0,) * nd)


ANY = pl.BlockSpec(memory_space=pl.ANY)


def _pallas(body, *, name, grid, in_specs, out_specs, out_shape, args, scratch_shapes=(), hook=None):
    n_axes = len(grid)
    if hook is None:
        return pl.pallas_call(body, name=name, grid=grid, in_specs=list(in_specs), out_specs=list(out_specs),
                              out_shape=list(out_shape), scratch_shapes=list(scratch_shapes),
                              compiler_params=_cparams(n_axes))(*args)
    n_in, n_out, n_scr = len(in_specs), len(out_specs), len(scratch_shapes)
    h_in, h_out = len(hook.inputs), len(hook.out_shape)
    total = math.prod(grid)

    def wrapped(*refs):
        ins, hins = refs[:n_in], refs[n_in:n_in + h_in]
        o0 = n_in + h_in
        outs, houts = refs[o0:o0 + n_out], refs[o0 + n_out:o0 + n_out + h_out]
        s0 = o0 + n_out + h_out
        scr, hscr = refs[s0:s0 + n_scr], refs[s0 + n_scr:]
        step = pl.program_id(0)
        for a in range(1, n_axes):
            step = step * grid[a] + pl.program_id(a)

        @pl.when(step == 0)
        def _():
            hook.start(hins, houts, hscr)

        body(*ins, *outs, *scr)

        if hook.has_mid:
            @pl.when(step == (3 * total) // 4)
            def _():
                hook.mid(hins, houts, hscr)

        @pl.when(step == total - 1)
        def _():
            hook.finish(hins, houts, hscr)

    res = pl.pallas_call(
        wrapped, name=name, grid=grid, in_specs=list(in_specs) + [ANY] * h_in,
        out_specs=list(out_specs) + [ANY] * h_out, out_shape=list(out_shape) + list(hook.out_shape),
        scratch_shapes=list(scratch_shapes) + list(hook.scratch), compiler_params=_cparams(n_axes),
        input_output_aliases={n_in + a: n_out + b for a, b in hook.aliases},
    )(*args, *hook.inputs)
    return res[:n_out], res[n_out:]


def _ffn_fwd(h, wpre, wg4, wu4, wd4, wpost, hook=None, target=None):
    t = h.shape[0]
    tm = _row_tile(t)
    nj, fj, _ = wg4.shape
    nblk = tm // BLK if target is not None else 0

    def body(*refs):
        h_ref, wpre_ref, wg_ref, wu_ref, wd_ref, wpost_ref = refs[:6]
        t_refs = refs[6:6 + nblk]
        hout_ref, n_ref, p1_ref, p2_ref, a_ref, f_ref = refs[6 + nblk:12 + nblk]
        acc_ref = refs[-1]
        i = pl.program_id(0)
        j = pl.program_id(1)

        @pl.when(j == 0)
        def _():
            y, _, _ = _rms(h_ref[...], wpre_ref[...])
            n_ref[...] = y.astype(BF16)
            acc_ref[...] = jnp.zeros_like(acc_ref)

        if target is not None:
            sse_ref = refs[12 + nblk]

            @pl.when((i == 0) & (j == 0))
            def _():
                sse_ref[...] = jnp.zeros_like(sse_ref)

        parts = [slice(0, tm)]
        gus = [(_dg(n_ref[rows, :], wg_ref[...], NT), _dg(n_ref[rows, :], wu_ref[...], NT)) for rows in parts]
        for rows, (g, u) in zip(parts, gus):
            sg = _sigmoid(g)
            silu = g * sg
            p1_ref[rows, :] = (u * (sg + silu * (1.0 - sg))).astype(BF16)
            p2_ref[rows, :] = silu.astype(BF16)
            a = (silu * u).astype(BF16)
            a_ref[rows, :] = a
            acc_ref[rows, :] += _dot(a, wd_ref[...])

        @pl.when(j == nj - 1)
        def _():
            f = acc_ref[...]
            f_ref[...] = f
            y, _, _ = _rms(f, wpost_ref[...])
            hout = h_ref[...] + 0.5 * y
            if target is None:
                hout_ref[...] = hout
            else:
                sse = jnp.zeros((1, 1), F32)
                for k in range(nblk):
                    rows = slice(k * BLK, (k + 1) * BLK)
                    err = hout[rows] - t_refs[k][...]
                    if k == 0:
                        err = jnp.where(i > 0, err, 0.0)
                    hout_ref[rows, :] = err * (1.0 / D_MODEL)
                    sse = sse + jnp.sum(jnp.sum(err * err, axis=1, keepdims=True), axis=0, keepdims=True)
                sse_ref[...] += jnp.broadcast_to(sse, sse_ref.shape)

    row = pl.BlockSpec((tm, D_MODEL), lambda i, j: (i, 0))
    vec = pl.BlockSpec((1, D_MODEL), lambda i, j: (0, 0))
    wrow = pl.BlockSpec((None, fj, D_MODEL), lambda i, j: (j, 0, 0))
    act = pl.BlockSpec((None, tm, fj), lambda i, j: (j, i, 0))
    t_specs = [pl.BlockSpec((BLK, D_MODEL), functools.partial(lambda i, j, k: (jnp.maximum(nblk * i + k - 1, 0), 0), k=k))
               for k in range(nblk)]
    loss_spec = [_full((1, 128))] if target is not None else []
    loss_shape = [jax.ShapeDtypeStruct((1, 128), F32)] if target is not None else []
    return _pallas(
        body, name="ffn_fwd", grid=(t // tm, nj),
        in_specs=[row, vec, wrow, wrow, wrow, vec] + t_specs,
        out_specs=[row, row, act, act, act, row] + loss_spec,
        out_shape=[jax.ShapeDtypeStruct((t, D_MODEL), F32), jax.ShapeDtypeStruct((t, D_MODEL), BF16),
                   jax.ShapeDtypeStruct((nj, t, fj), BF16), jax.ShapeDtypeStruct((nj, t, fj), BF16),
                   jax.ShapeDtypeStruct((nj, t, fj), BF16), jax.ShapeDtypeStruct((t, D_MODEL), F32)] + loss_shape,
        scratch_shapes=[pltpu.VMEM((tm, D_MODEL), F32)],
        args=(h, wpre, wg4, wu4, wd4, wpost) + (target,) * nblk, hook=hook)


def _ffn_bwd(dhout, h, f, p14, p24, wpre, wg4, wu4, wd4, wpost, hook=None):
    t = h.shape[0]
    tm = _row_tile(t)
    nj, fj, _ = wg4.shape

    def body(dhout_ref, h_ref, f_ref, p1_ref, p2_ref, wpre_ref, wg_ref, wu_ref, wd_ref, wpost_ref,
             dh_ref, df_ref, dg_ref, du_ref, dwpre_ref, dwpost_ref, dn_ref):
        i = pl.program_id(0)
        j = pl.program_id(1)

        @pl.when((i == 0) & (j == 0))
        def _():
            dwpre_ref[...] = jnp.zeros_like(dwpre_ref)
            dwpost_ref[...] = jnp.zeros_like(dwpost_ref)

        @pl.when(j == 0)
        def _():
            wpost = wpost_ref[...]
            _, fh, r = _rms(f_ref[...], wpost)
            df, dw = _rms_bwd(fh, r, wpost, 0.5 * dhout_ref[...])
            dwpost_ref[...] += dw
            df_ref[...] = df.astype(BF16)
            dn_ref[...] = jnp.zeros_like(dn_ref)

        parts = _row_parts(tm)
        das = [_dg(df_ref[rows, :], wd_ref[...], NT) for rows in parts]
        for rows, da in zip(parts, das):
            dg = (da * p1_ref[rows, :].astype(F32)).astype(BF16)
            du = (da * p2_ref[rows, :].astype(F32)).astype(BF16)
            dg_ref[rows, :] = dg
            du_ref[rows, :] = du
            dn_ref[rows, :] += _dot(dg, wg_ref[...]) + _dot(du, wu_ref[...])

        @pl.when(j == nj - 1)
        def _():
            wpre = wpre_ref[...]
            _, hh, r = _rms(h_ref[...], wpre)
            dx, dw = _rms_bwd(hh, r, wpre, dn_ref[...])
            dwpre_ref[...] += dw
            dh_ref[...] = dhout_ref[...] + dx

    row = pl.BlockSpec((tm, D_MODEL), lambda i, j: (i, 0))
    vec = pl.BlockSpec((1, D_MODEL), lambda i, j: (0, 0))
    wrow = pl.BlockSpec((None, fj, D_MODEL), lambda i, j: (j, 0, 0))
    act = pl.BlockSpec((None, tm, fj), lambda i, j: (j, i, 0))
    actshape = jax.ShapeDtypeStruct((nj, t, fj), BF16)
    return _pallas(
        body, name="ffn_bwd", grid=(t // tm, nj),
        in_specs=[row, row, row, act, act, vec, wrow, wrow, wrow, vec],
        out_specs=[row, row, act, act, vec, vec],
        out_shape=[jax.ShapeDtypeStruct((t, D_MODEL), F32), jax.ShapeDtypeStruct((t, D_MODEL), BF16),
                   actshape, actshape,
                   jax.ShapeDtypeStruct((1, D_MODEL), F32), jax.ShapeDtypeStruct((1, D_MODEL), F32)],
        scratch_shapes=[pltpu.VMEM((tm, D_MODEL), F32)],
        args=(dhout, h, f, p14, p24, wpre, wg4, wu4, wd4, wpost), hook=hook)


def _ffn_wgrad(n, df, dg4, du4, a4, hook=None):
    t = n.shape[0]
    tm = _contract_tile(t)
    ni = t // tm
    nj, _, fj = dg4.shape

    def body(n_ref, df_ref, dg_ref, du_ref, a_ref, dw_ref, acc):
        i = pl.program_id(1)

        @pl.when(i == 0)
        def _():
            acc[...] = jnp.zeros_like(acc)

        nn = n_ref[...]
        acc[0:fj, :] += _dg(dg_ref[...], nn, TN)
        acc[fj:2 * fj, :] += _dg(du_ref[...], nn, TN)
        acc[2 * fj:3 * fj, :] += _dg(a_ref[...], df_ref[...], TN)

        @pl.when(i == ni - 1)
        def _():
            dw_ref[...] = acc[...].astype(BF16)

    row = pl.BlockSpec((tm, D_MODEL), lambda j, i: (i, 0))
    act = pl.BlockSpec((None, tm, fj), lambda j, i: (j, i, 0))
    return _pallas(
        body, name="ffn_wgrad", grid=(nj, ni),
        in_specs=[row, row, act, act, act],
        out_specs=[pl.BlockSpec((None, 3 * fj, D_MODEL), lambda j, i: (j, 0, 0))],
        out_shape=[jax.ShapeDtypeStruct((nj, 3 * fj, D_MODEL), BF16)],
        scratch_shapes=[pltpu.VMEM((3 * fj, D_MODEL), F32)],
        args=(n, df, dg4, du4, a4), hook=hook)


def _xty(x, y):
    t, k = x.shape
    n = y.shape[1]
    tm = _contract_tile(t)
    tn = n if n <= 1024 else (896 if n % 896 == 0 else 128)

    def body(x_ref, y_ref, o_ref):
        @pl.when(pl.program_id(1) == 0)
        def _():
            o_ref[...] = jnp.zeros_like(o_ref)

        o_ref[...] += _dg(x_ref[...], y_ref[...], TN)

    return pl.pallas_call(
        body, name="xty", grid=(n // tn, t // tm),
        in_specs=[pl.BlockSpec((tm, k), lambda j, i: (i, 0)), pl.BlockSpec((tm, tn), lambda j, i: (i, j))],
        out_specs=pl.BlockSpec((k, tn), lambda j, i: (0, j)),
        out_shape=jax.ShapeDtypeStruct((k, n), F32),
        compiler_params=_cparams(2),
    )(x, y)


def _rope_tables(t):
    pos = (jnp.arange(t, dtype=jnp.int32) - PAD).astype(F32)
    inv_freq = 1.0 / (ROPE_THETA ** (jnp.arange(0, SWA_HD, 2, dtype=F32) / SWA_HD))
    ang = pos[:, None] * inv_freq[None, :]
    cos = jnp.cos(ang)
    sin = jnp.sin(ang)
    return jnp.concatenate([cos, cos, cos, cos], axis=1), jnp.concatenate([-sin, sin, -sin, sin], axis=1)


def _rot_half(x, first_half):
    return jnp.where(first_half, pltpu.roll(x, 96, 1), pltpu.roll(x, 32, 1))


def _first_half_mask(rows):
    lane = lax.broadcasted_iota(jnp.int32, (rows, 128), 1)
    return (lane % 64) < 32


def _log_sigmoid(z):
    return jnp.minimum(z, 0.0) - jnp.log(1.0 + jnp.exp(-jnp.abs(z)))


def _mix_proj(h1, wmixpre, winp, wa2p, bap, cos, sin):
    t = h1.shape[0]
    tm = _row_tile(t)

    def body(h_ref, w_ref, win_ref, wa2_ref, ba_ref, cos_ref, sin_ref,
             n_ref, gq_ref, gk_ref, gv_ref, gg_ref, ga_ref, la_ref, sq_ref, sk_ref, sv_ref):
        y, _, _ = _rms(h_ref[...], w_ref[...])
        n = y.astype(BF16)
        n_ref[...] = n
        proj = _dot(n, win_ref[...])
        gq_ref[...] = proj[:, P_GQ:P_GK]
        gk_ref[...] = proj[:, P_GK:P_GV]
        gv_ref[...] = proj[:, P_GV:P_GG]
        gg_ref[...] = proj[:, P_GG:P_GA]
        ga = proj[:, P_GA:P_SQ]
        ga_ref[...] = ga
        z = _dot(ga.astype(BF16), wa2_ref[...]) + ba_ref[...]
        la_ref[...] = _log_sigmoid(z) * (1.0 / GLA_TAU)
        c = cos_ref[...]
        s = sin_ref[...]
        fh = _first_half_mask(tm)
        for k in range(4):
            x = proj[:, P_SQ + 128 * k:P_SQ + 128 * (k + 1)]
            sq_ref[:, 128 * k:128 * (k + 1)] = (x * c + _rot_half(x, fh) * s).astype(BF16)
        for k in range(2):
            x = proj[:, P_SK + 128 * k:P_SK + 128 * (k + 1)]
            sk_ref[:, 128 * k:128 * (k + 1)] = (x * c + _rot_half(x, fh) * s).astype(BF16)
        sv_ref[...] = proj[:, P_SV:P_END].astype(BF16)

    def row(w):
        return pl.BlockSpec((tm, w), lambda i: (i, 0))

    def rshape(w, dt):
        return jax.ShapeDtypeStruct((t, w), dt)

    return pl.pallas_call(
        body, name="mix_proj", grid=(t // tm,),
        in_specs=[row(D_MODEL), _full((1, D_MODEL)), _full((D_MODEL, P_END)), _full((128, GLA_KW)),
                  _full((1, GLA_KW)), row(128), row(128)],
        out_specs=[row(D_MODEL), row(256), row(256), row(512), row(512), row(128), row(256), row(512), row(256),
                   row(256)],
        out_shape=[rshape(D_MODEL, BF16), rshape(256, F32), rshape(256, F32), rshape(512, F32), rshape(512, F32),
                   rshape(128, F32), rshape(256, F32), rshape(512, BF16), rshape(256, BF16), rshape(256, BF16)],
        compiler_params=_cparams(1),
    )(h1, wmixpre, winp, wa2p, bap, cos, sin)


def _gla_cumsum(la, tril_f):
    b = jnp.dot(tril_f, la, precision=lax.Precision.HIGHEST, preferred_element_type=F32)
    row = lax.broadcasted_iota(jnp.int32, b.shape, 0)
    bm = jnp.sum(jnp.where(row == GLA_CHUNK // 2 - 1, b, 0.0), axis=0, keepdims=True)
    bl = jnp.sum(jnp.where(row == GLA_CHUNK - 1, b, 0.0), axis=0, keepdims=True)
    return b, bm, bl


def _gla_decays(la, tril_f):
    b, bm, bl = _gla_cumsum(la, tril_f)
    return jnp.exp(b - bm), jnp.exp(bm - b), jnp.exp(b), jnp.exp(bl - b), jnp.exp(bl)


def _gla_masks():
    c = GLA_CHUNK
    r = lax.broadcasted_iota(jnp.int32, (c, c), 0)
    col = lax.broadcasted_iota(jnp.int32, (c, c), 1)
    r4 = lax.broadcasted_iota(jnp.int32, (GLA_HEADS * c, c), 0) % c
    c4 = lax.broadcasted_iota(jnp.int32, (GLA_HEADS * c, c), 1)
    klane = lax.broadcasted_iota(jnp.int32, (c, GLA_KW), 1) // GLA_DK
    vlane = lax.broadcasted_iota(jnp.int32, (c, GLA_W), 1) // GLA_DV
    srow = lax.broadcasted_iota(jnp.int32, (GLA_W, GLA_KW), 0) // GLA_DV
    scol = lax.broadcasted_iota(jnp.int32, (GLA_W, GLA_KW), 1) // GLA_DK
    return dict(tril_f=(r >= col).astype(F32), triu_f=(r <= col).astype(F32), tril4=r4 >= c4,
                khead=[klane == h for h in range(GLA_HEADS)], vhead=[vlane == h for h in range(GLA_HEADS)],
                diag=srow == scol)


def _stack_heads(x, head_masks):
    return jnp.concatenate([jnp.where(m, x, 0.0) for m in head_masks], axis=0)


def _gla_fwd(gq, gk, gv, la):
    t = gq.shape[0]
    nb = t // BLK
    ncb = BLK // GLA_CHUNK
    c = GLA_CHUNK

    def body(q_ref, k_ref, v_ref, la_ref, o_ref, ss_ref, st_ref):
        @pl.when(pl.program_id(0) == 0)
        def _():
            st_ref[...] = jnp.zeros_like(st_ref)

        mk = _gla_masks()
        for ch in range(ncb):
            rows = slice(ch * c, (ch + 1) * c)
            eq, ek, eb, ekl, ebl = _gla_decays(la_ref[rows, :], mk["tril_f"])
            qs = q_ref[rows, :] * (GLA_DK ** -0.5)
            k = k_ref[rows, :]
            v = v_ref[rows, :].astype(BF16)
            st = st_ref[...]
            ss_ref[ch] = st
            q4 = _stack_heads(qs * eq, mk["khead"]).astype(BF16)
            a4 = jnp.where(mk["tril4"], _dg(q4, (k * ek).astype(BF16), NT), 0.0).astype(BF16)
            r4 = _dot(a4, v)
            intra = jnp.concatenate([r4[h * c:(h + 1) * c, GLA_DV * h:GLA_DV * (h + 1)] for h in range(GLA_HEADS)],
                                    axis=1)
            o_ref[rows, :] = intra + _dg((qs * eb).astype(BF16), st.astype(BF16), NT)
            st_ref[...] = st * ebl + jnp.where(mk["diag"], _dg(v, (k * ekl).astype(BF16), TN), 0.0)

    def row(w):
        return pl.BlockSpec((BLK, w), lambda i: (i, 0))

    return pl.pallas_call(
        body, name="gla_fwd", grid=(nb,),
        in_specs=[row(256), row(256), row(512), row(256)],
        out_specs=[row(512), pl.BlockSpec((ncb, GLA_W, GLA_KW), lambda i: (i, 0, 0))],
        out_shape=[jax.ShapeDtypeStruct((t, GLA_W), F32), jax.ShapeDtypeStruct((nb * ncb, GLA_W, GLA_KW), F32)],
        scratch_shapes=[pltpu.VMEM((GLA_W, GLA_KW), F32)],
        compiler_params=_cparams(1),
    )(gq, gk, gv, la)


def _gla_bwd(gq, gk, gv, la, ss, do):
    t = gq.shape[0]
    nb = t // BLK
    ncb = BLK // GLA_CHUNK
    c = GLA_CHUNK

    def body(q_ref, k_ref, v_ref, la_ref, ss_ref, do_ref, dq_ref, dk_ref, dv_ref, dla_ref, dst_ref):
        @pl.when(pl.program_id(0) == 0)
        def _():
            dst_ref[...] = jnp.zeros_like(dst_ref)

        mk = _gla_masks()
        last_row = lax.broadcasted_iota(jnp.int32, (c, GLA_KW), 0) == c - 1
        scale = GLA_DK ** -0.5
        for ch in reversed(range(ncb)):
            rows = slice(ch * c, (ch + 1) * c)
            eq, ek, eb, ekl, ebl = _gla_decays(la_ref[rows, :], mk["tril_f"])
            qs = q_ref[rows, :] * scale
            k = k_ref[rows, :]
            qt, kt, qh, kh = qs * eq, k * ek, qs * eb, k * ekl
            ktb, khb, qhb = kt.astype(BF16), kh.astype(BF16), qh.astype(BF16)
            v = v_ref[rows, :].astype(BF16)
            do_f = do_ref[rows, :]
            dob = do_f.astype(BF16)
            st = ss_ref[ch]
            stb = st.astype(BF16)
            dstn = dst_ref[...]
            dstb = dstn.astype(BF16)
            q4 = _stack_heads(qt, mk["khead"]).astype(BF16)
            do4 = _stack_heads(do_f, mk["vhead"]).astype(BF16)
            a4 = jnp.where(mk["tril4"], _dg(q4, ktb, NT), 0.0).astype(BF16)
            da4 = jnp.where(mk["tril4"], _dg(do4, v, NT), 0.0).astype(BF16)
            dv_ref[rows, :] = _dg(a4, do4, TN) + _dg(khb, dstb, NT)
            dq4 = _dot(da4, ktb)
            dqt = jnp.zeros((c, GLA_KW), F32)
            for h in range(GLA_HEADS):
                dqt = dqt + jnp.where(mk["khead"][h], dq4[h * c:(h + 1) * c], 0.0)
            dkt = _dg(da4, q4, TN)
            dqh = _dot(dob, stb)
            dkh = _dot(v, dstb)
            dbl = jnp.sum(dstn * st, axis=0, keepdims=True)
            dst_ref[...] = dstn * ebl + jnp.where(mk["diag"], _dg(dob, qhb, TN), 0.0)
            dq_ref[rows, :] = scale * (dqt * eq + dqh * eb)
            dk_ref[rows, :] = dkt * ek + dkh * ekl
            dkk = dkh * kh
            db = dqt * qt - dkt * kt + dqh * qh - dkk
            db = db + jnp.where(last_row, jnp.sum(dkk, axis=0, keepdims=True) + ebl * dbl, 0.0)
            dla_ref[rows, :] = jnp.dot(mk["triu_f"], db, precision=lax.Precision.HIGHEST,
                                       preferred_element_type=F32)

    def row(w):
        return pl.BlockSpec((BLK, w), lambda i: (nb - 1 - i, 0))

    def rshape(w):
        return jax.ShapeDtypeStruct((t, w), F32)

    return pl.pallas_call(
        body, name="gla_bwd", grid=(nb,),
        in_specs=[row(256), row(256), row(512), row(256),
                  pl.BlockSpec((ncb, GLA_W, GLA_KW), lambda i: (nb - 1 - i, 0, 0)), row(512)],
        out_specs=[row(256), row(256), row(512), row(256)],
        out_shape=[rshape(256), rshape(256), rshape(512), rshape(256)],
        scratch_shapes=[pltpu.VMEM((GLA_W, GLA_KW), F32)],
        compiler_params=_cparams(1),
    )(gq, gk, gv, la, ss, do)


SWA_G = SWA_QH // SWA_KVH


def _swa_mask(n):
    r = lax.broadcasted_iota(jnp.int32, (SWA_G * BLK, 3 * BLK), 0) % BLK
    c = lax.broadcasted_iota(jnp.int32, (SWA_G * BLK, 3 * BLK), 1)
    seg = c // BLK
    cc = c % BLK
    qpos = n * BLK + r - PAD
    kpos = jnp.where(seg == 0, (n - 1) * BLK, jnp.where(seg == 1, n * BLK, 0)) + cc - PAD
    band = (seg < 2) & (kpos >= N_META) & (kpos <= qpos) & (qpos - kpos < WINDOW)
    meta = (seg == 2) & (kpos >= 0) & (kpos < N_META) & (kpos <= qpos)
    return band | meta


def _swa_stack(ref, kh, lo, dtype):
    parts = []
    for g in range(2):
        pair = ref[:, 128 * (2 * kh + g):128 * (2 * kh + g + 1)]
        zero = jnp.zeros_like(pair)
        parts += [jnp.where(lo, pair, zero), jnp.where(lo, zero, pair)]
    return jnp.concatenate(parts, axis=0).astype(dtype)


def _swa_unstack(x4, lo):
    return [jnp.where(lo, x4[2 * g * BLK:(2 * g + 1) * BLK], x4[(2 * g + 1) * BLK:(2 * g + 2) * BLK])
            for g in range(2)]


def _swa_sink_col(sink_ref, kh):
    blk = lax.broadcasted_iota(jnp.int32, (SWA_G * BLK, 1), 0) // BLK
    col = jnp.full((SWA_G * BLK, 1), sink_ref[SWA_G * kh + SWA_G - 1], F32)
    for e in reversed(range(SWA_G - 1)):
        col = jnp.where(blk == e, sink_ref[SWA_G * kh + e], col)
    return col


def _swa_probs(q4, kall, mask, sink):
    s = _dg(q4, kall, NT) * (SWA_HD ** -0.5)
    s = jnp.where(mask, s, NEG_INF)
    m = jnp.maximum(jnp.max(s, axis=-1, keepdims=True), sink)
    p = jnp.exp(s - m)
    es = jnp.exp(sink - m)
    inv = 1.0 / (jnp.sum(p, axis=-1, keepdims=True) + es)
    return p * inv, es * inv


def _swa_fwd(sinks, sq, sk, sv):
    t = sq.shape[0]
    nb = t // BLK

    def body(sink_ref, q_ref, kp_ref, kc_ref, km_ref, vp_ref, vc_ref, vm_ref, o_ref):
        n = pl.program_id(0)
        mask = _swa_mask(n)
        lo = lax.broadcasted_iota(jnp.int32, (BLK, 128), 1) < 64
        for kh in range(SWA_KVH):
            ls = slice(128 * kh, 128 * (kh + 1))
            kall = jnp.concatenate([kp_ref[:, ls], kc_ref[:, ls], km_ref[:, ls]], axis=0)
            vall = jnp.concatenate([vp_ref[:, ls], vc_ref[:, ls], vm_ref[:, ls]], axis=0)
            p, _ = _swa_probs(_swa_stack(q_ref, kh, lo, BF16), kall, mask, _swa_sink_col(sink_ref, kh))
            for g, pair in enumerate(_swa_unstack(_dot(p.astype(BF16), vall), lo)):
                o_ref[:, 128 * (2 * kh + g):128 * (2 * kh + g + 1)] = pair

    cur = lambda w: pl.BlockSpec((BLK, w), lambda i: (i, 0))
    prev = lambda w: pl.BlockSpec((BLK, w), lambda i: (jnp.maximum(i - 1, 0), 0))
    first = lambda w: pl.BlockSpec((BLK, w), lambda i: (0, 0))
    return pl.pallas_call(
        body, name="swa_fwd", grid=(nb,),
        in_specs=[pl.BlockSpec(memory_space=pltpu.SMEM), cur(512), prev(256), cur(256), first(256),
                  prev(256), cur(256), first(256)],
        out_specs=cur(512),
        out_shape=jax.ShapeDtypeStruct((t, SWA_W), F32),
        compiler_params=_cparams(1),
    )(sinks, sq, sk, sk, sk, sv, sv, sv)


def _swa_bwd(sinks, sq, sk, sv, o, do, hook=None):
    t = sq.shape[0]
    nb = t // BLK

    def body(sink_ref, q_ref, kp_ref, kc_ref, km_ref, vp_ref, vc_ref, vm_ref, o_ref, do_ref,
             dq_ref, dk_ref, dv_ref, dkm_ref, dvm_ref, dsink_ref, ck_ref, cv_ref):
        n = pl.program_id(0)

        @pl.when(n == 0)
        def _():
            ck_ref[...] = jnp.zeros_like(ck_ref)
            cv_ref[...] = jnp.zeros_like(cv_ref)
            dkm_ref[...] = jnp.zeros_like(dkm_ref)
            dvm_ref[...] = jnp.zeros_like(dvm_ref)
            dsink_ref[...] = jnp.zeros_like(dsink_ref)

        @pl.when(n == nb)
        def _():
            dk_ref[...] = ck_ref[...]
            dv_ref[...] = cv_ref[...]

        @pl.when(n < nb)
        def _():
            mask = _swa_mask(n)
            lo = lax.broadcasted_iota(jnp.int32, (BLK, 128), 1) < 64
            scale = SWA_HD ** -0.5
            for kh in range(SWA_KVH):
                ls = slice(128 * kh, 128 * (kh + 1))
                kall = jnp.concatenate([kp_ref[:, ls], kc_ref[:, ls], km_ref[:, ls]], axis=0)
                vall = jnp.concatenate([vp_ref[:, ls], vc_ref[:, ls], vm_ref[:, ls]], axis=0)
                q4 = _swa_stack(q_ref, kh, lo, BF16)
                do4 = _swa_stack(do_ref, kh, lo, F32)
                p, psink = _swa_probs(q4, kall, mask, _swa_sink_col(sink_ref, kh))
                delta = jnp.sum(do4 * _swa_stack(o_ref, kh, lo, F32), axis=-1, keepdims=True)
                do4b = do4.astype(BF16)
                ds = (p * (_dg(do4b, vall, NT) - delta) * scale).astype(BF16)
                for g, pair in enumerate(_swa_unstack(_dot(ds, kall), lo)):
                    dq_ref[:, 128 * (2 * kh + g):128 * (2 * kh + g + 1)] = pair
                dkall = _dg(ds, q4, TN)
                dvall = _dg(p.astype(BF16), do4b, TN)
                dsk = psink * delta
                for e in range(SWA_G):
                    h = SWA_G * kh + e
                    dsink_ref[h:h + 1, :] += jnp.broadcast_to(
                        -jnp.sum(dsk[e * BLK:(e + 1) * BLK], axis=0, keepdims=True), (1, 128))
                dk_ref[:, ls] = ck_ref[:, ls] + dkall[0:BLK]
                dv_ref[:, ls] = cv_ref[:, ls] + dvall[0:BLK]
                ck_ref[:, ls] = dkall[BLK:2 * BLK]
                cv_ref[:, ls] = dvall[BLK:2 * BLK]
                dkm_ref[:, ls] += dkall[2 * BLK:3 * BLK]
                dvm_ref[:, ls] += dvall[2 * BLK:3 * BLK]

    cur = lambda w: pl.BlockSpec((BLK, w), lambda i: (jnp.minimum(i, nb - 1), 0))
    prev = lambda w: pl.BlockSpec((BLK, w), lambda i: (jnp.maximum(i - 1, 0), 0))
    first = lambda w: pl.BlockSpec((BLK, w), lambda i: (0, 0))
    return _pallas(
        body, name="swa_bwd", grid=(nb + 1,),
        in_specs=[pl.BlockSpec(memory_space=pltpu.SMEM), cur(512), prev(256), cur(256), first(256),
                  prev(256), cur(256), first(256), cur(512), cur(512)],
        out_specs=[cur(512), prev(256), prev(256), first(256), first(256), _full((SWA_QH, 128))],
        out_shape=[jax.ShapeDtypeStruct((t, SWA_W), F32), jax.ShapeDtypeStruct((t, 256), F32),
                   jax.ShapeDtypeStruct((t, 256), F32), jax.ShapeDtypeStruct((BLK, 256), F32),
                   jax.ShapeDtypeStruct((BLK, 256), F32), jax.ShapeDtypeStruct((SWA_QH, 128), F32)],
        scratch_shapes=[pltpu.VMEM((BLK, 256), F32), pltpu.VMEM((BLK, 256), F32)],
        args=(sinks, sq, sk, sk, sk, sv, sv, sv, o, do), hook=hook)


def _mix_out(h1, ogla, gg, oswa, wgn, wsn, wout, wpost):
    t = h1.shape[0]
    tm = _row_tile(t)

    def body(h_ref, og_ref, gg_ref, os_ref, wgn_ref, wsn_ref, wout_ref, wpost_ref, h2_ref, cat_ref, m_ref):
        parts = []
        for h in range(GLA_HEADS):
            ls = slice(GLA_DV * h, GLA_DV * (h + 1))
            y, _, _ = _rms(og_ref[:, ls], wgn_ref[...])
            g = gg_ref[:, ls]
            parts.append(y * (g * _sigmoid(g)))
        ys, _, _ = _rms(os_ref[...], wsn_ref[...])
        cat = jnp.concatenate(parts + [ys], axis=1).astype(BF16)
        cat_ref[...] = cat
        m = _dot(cat, wout_ref[...])
        m_ref[...] = m
        y, _, _ = _rms(m, wpost_ref[...])
        h2_ref[...] = h_ref[...] + y

    def row(w):
        return pl.BlockSpec((tm, w), lambda i: (i, 0))

    return pl.pallas_call(
        body, name="mix_out", grid=(t // tm,),
        in_specs=[row(D_MODEL), row(512), row(512), row(512), _full((1, GLA_DV)), _full((1, SWA_W)),
                  _full((D_MODEL, D_MODEL)), _full((1, D_MODEL))],
        out_specs=[row(D_MODEL), row(D_MODEL), row(D_MODEL)],
        out_shape=[jax.ShapeDtypeStruct((t, D_MODEL), F32), jax.ShapeDtypeStruct((t, D_MODEL), BF16),
                   jax.ShapeDtypeStruct((t, D_MODEL), F32)],
        compiler_params=_cparams(1),
    )(h1, ogla, gg, oswa, wgn, wsn, wout, wpost)


def _mix_out_bwd(dh2, m, ogla, gg, oswa, wgn, wsn, wout, wpost, hook=None):
    t = dh2.shape[0]
    tm = _row_tile(t)

    def body(dh_ref, m_ref, og_ref, gg_ref, os_ref, wgn_ref, wsn_ref, wout_ref, wpost_ref,
             dog_ref, dgg_ref, dos_ref, dm_ref, dwpost_ref, dwgn_ref, dwsn_ref):
        @pl.when(pl.program_id(0) == 0)
        def _():
            dwpost_ref[...] = jnp.zeros_like(dwpost_ref)
            dwgn_ref[...] = jnp.zeros_like(dwgn_ref)
            dwsn_ref[...] = jnp.zeros_like(dwsn_ref)

        wpost = wpost_ref[...]
        _, mh, r = _rms(m_ref[...], wpost)
        dm, dw = _rms_bwd(mh, r, wpost, dh_ref[...])
        dwpost_ref[...] += dw
        dmb = dm.astype(BF16)
        dm_ref[...] = dmb
        dcat = _dg(dmb, wout_ref[...], NT)
        wgn = wgn_ref[...]
        for h in range(GLA_HEADS):
            ls = slice(GLA_DV * h, GLA_DV * (h + 1))
            dog = dcat[:, ls]
            g = gg_ref[:, ls]
            sg = _sigmoid(g)
            y, xh, r = _rms(og_ref[:, ls], wgn)
            dgg_ref[:, ls] = dog * y * (sg * (1.0 + g * (1.0 - sg)))
            dx, dw = _rms_bwd(xh, r, wgn, dog * (g * sg))
            dog_ref[:, ls] = dx
            dwgn_ref[...] += dw
        wsn = wsn_ref[...]
        _, xh, r = _rms(os_ref[...], wsn)
        dx, dw = _rms_bwd(xh, r, wsn, dcat[:, GLA_W:])
        dos_ref[...] = dx
        dwsn_ref[...] += dw

    def row(w):
        return pl.BlockSpec((tm, w), lambda i: (i, 0))

    def rshape(w, dt=F32):
        return jax.ShapeDtypeStruct((t, w), dt)

    return _pallas(
        body, name="mix_out_bwd", grid=(t // tm,),
        in_specs=[row(D_MODEL), row(D_MODEL), row(512), row(512), row(512), _full((1, GLA_DV)), _full((1, SWA_W)),
                  _full((D_MODEL, D_MODEL)), _full((1, D_MODEL))],
        out_specs=[row(512), row(512), row(512), row(D_MODEL), _full((1, D_MODEL)), _full((1, GLA_DV)),
                   _full((1, SWA_W))],
        out_shape=[rshape(512), rshape(512), rshape(512), rshape(D_MODEL, BF16),
                   jax.ShapeDtypeStruct((1, D_MODEL), F32), jax.ShapeDtypeStruct((1, GLA_DV), F32),
                   jax.ShapeDtypeStruct((1, SWA_W), F32)],
        args=(dh2, m, ogla, gg, oswa, wgn, wsn, wout, wpost), hook=hook)


def _mix_in_bwd(dh2, h1, wmixpre, winp, wa2p, bap, cos, sin, ga, dgq, dgk, dgv, dgg, dla, dsq, dsk, dsv, dkm, dvm):
    t = h1.shape[0]
    tm = _row_tile(t)

    def body(dh2_ref, h_ref, w_ref, win_ref, wa2_ref, ba_ref, cos_ref, sin_ref, ga_ref, dgq_ref, dgk_ref, dgv_ref,
             dgg_ref, dla_ref, dsq_ref, dsk_ref, dsv_ref, dkm_ref, dvm_ref,
             dh1_ref, dproj_ref, dw_ref, dwa2_ref, dba_ref):
        i = pl.program_id(0)

        @pl.when(i == 0)
        def _():
            dw_ref[...] = jnp.zeros_like(dw_ref)
            dwa2_ref[...] = jnp.zeros_like(dwa2_ref)
            dba_ref[...] = jnp.zeros_like(dba_ref)

        first = (i == 0).astype(F32)
        c = cos_ref[...]
        s = -sin_ref[...]
        fh = _first_half_mask(tm)
        dproj_ref[:, P_GQ:P_GK] = dgq_ref[...].astype(BF16)
        dproj_ref[:, P_GK:P_GV] = dgk_ref[...].astype(BF16)
        dproj_ref[:, P_GV:P_GG] = dgv_ref[...].astype(BF16)
        dproj_ref[:, P_GG:P_GA] = dgg_ref[...].astype(BF16)
        gab = ga_ref[...].astype(BF16)
        z = _dot(gab, wa2_ref[...]) + ba_ref[...]
        row_id = i * tm + lax.broadcasted_iota(jnp.int32, (tm, 1), 0)
        dz = jnp.where(row_id >= PAD, dla_ref[...] * (1.0 / GLA_TAU) * (1.0 - _sigmoid(z)), 0.0)
        dzb = dz.astype(BF16)
        dba_ref[...] += jnp.sum(dz, axis=0, keepdims=True)
        dwa2_ref[...] += _dg(gab, dzb, TN)
        dproj_ref[:, P_GA:P_SQ] = _dg(dzb, wa2_ref[...], NT).astype(BF16)
        for k in range(4):
            dy = dsq_ref[:, 128 * k:128 * (k + 1)]
            dproj_ref[:, P_SQ + 128 * k:P_SQ + 128 * (k + 1)] = (dy * c + _rot_half(dy, fh) * s).astype(BF16)
        for k in range(2):
            ls = slice(128 * k, 128 * (k + 1))
            dy = dsk_ref[:, ls]
            dy = jnp.concatenate([dy[:BLK] + first * dkm_ref[:, ls], dy[BLK:]], axis=0) if tm > BLK else (
                dy + first * dkm_ref[:, ls])
            dproj_ref[:, P_SK + 128 * k:P_SK + 128 * (k + 1)] = (dy * c + _rot_half(dy, fh) * s).astype(BF16)
            dv = dsv_ref[:, ls]
            dv = jnp.concatenate([dv[:BLK] + first * dvm_ref[:, ls], dv[BLK:]], axis=0) if tm > BLK else (
                dv + first * dvm_ref[:, ls])
            dproj_ref[:, P_SV + 128 * k:P_SV + 128 * (k + 1)] = dv.astype(BF16)
        dn = _dg(dproj_ref[...], win_ref[...], NT)
        w = w_ref[...]
        _, hh, r = _rms(h_ref[...], w)
        dx, dw = _rms_bwd(hh, r, w, dn)
        dw_ref[...] += dw
        dh1_ref[...] = dh2_ref[...] + dx

    def row(w):
        return pl.BlockSpec((tm, w), lambda i: (i, 0))

    return pl.pallas_call(
        body, name="mix_in_bwd", grid=(t // tm,),
        in_specs=[row(D_MODEL), row(D_MODEL), _full((1, D_MODEL)), _full((D_MODEL, P_END)), _full((128, GLA_KW)),
                  _full((1, GLA_KW)), row(128), row(128), row(128), row(256), row(256), row(512), row(512), row(256),
                  row(512), row(256), row(256), _full((BLK, 256)), _full((BLK, 256))],
        out_specs=[row(D_MODEL), row(P_END), _full((1, D_MODEL)), _full((128, GLA_KW)), _full((1, GLA_KW))],
        out_shape=[jax.ShapeDtypeStruct((t, D_MODEL), F32), jax.ShapeDtypeStruct((t, P_END), BF16),
                   jax.ShapeDtypeStruct((1, D_MODEL), F32), jax.ShapeDtypeStruct((128, GLA_KW), F32),
                   jax.ShapeDtypeStruct((1, GLA_KW), F32)],
        compiler_params=_cparams(1),
    )(dh2, h1, wmixpre, winp, wa2p, bap, cos, sin, ga, dgq, dgk, dgv, dgg, dla, dsq, dsk, dsv, dkm, dvm)


def _adamw_update(w, g, m, v):
    m = ADAM_B1 * m + (1.0 - ADAM_B1) * g
    v = ADAM_B2 * v + (1.0 - ADAM_B2) * (g * g)
    m_hat = m / (1.0 - ADAM_B1 ** ADAM_STEP)
    v_hat = v / (1.0 - ADAM_B2 ** ADAM_STEP)
    return -ADAM_LR * (m_hat / (jnp.sqrt(v_hat) + ADAM_EPS) + ADAM_WD * w), m, v


def _adamw(w, g, m, v):
    r, c = w.shape
    tr = _div_tile(r)

    def body(w_ref, g_ref, m_ref, v_ref, d_ref, nm_ref, nv_ref):
        d_ref[...], nm_ref[...], nv_ref[...] = _adamw_update(w_ref[...], g_ref[...], m_ref[...], v_ref[...])

    spec = pl.BlockSpec((tr, c), lambda i: (i, 0))
    shape = jax.ShapeDtypeStruct((r, c), F32)
    return pl.pallas_call(
        body, name="adamw", grid=(r // tr,), in_specs=[spec] * 4, out_specs=[spec] * 3, out_shape=[shape] * 3,
        compiler_params=_cparams(1),
    )(w, g, m, v)


def _adamw_halves(w, g_mine, g_other, m, v, c_idx, row0=0):
    r, c = w.shape
    h = g_mine.shape[0]
    tr = _div_tile(math.gcd(r, h))
    nth = h // tr
    t0 = row0 // tr
    assert t0 * tr == row0

    def body(c_ref, w_ref, gm_ref, go_ref, m_ref, v_ref, g_ref, d_ref, nm_ref, nv_ref):
        hh = (t0 + pl.program_id(0)) // nth
        g = jnp.where(hh == c_ref[0], gm_ref[...], go_ref[...])
        g_ref[...] = g
        d_ref[...], nm_ref[...], nv_ref[...] = _adamw_update(w_ref[...], g, m_ref[...], v_ref[...])

    spec = pl.BlockSpec((tr, c), lambda i, c_ref: (i, 0))
    gspec = pl.BlockSpec((tr, c), lambda i, c_ref: ((t0 + i) % nth, 0))
    shape = jax.ShapeDtypeStruct((r, c), F32)
    return pl.pallas_call(
        body, name="adamw_halves",
        grid_spec=pltpu.PrefetchScalarGridSpec(
            num_scalar_prefetch=1, grid=(r // tr,), in_specs=[spec, gspec, gspec, spec, spec], out_specs=[spec] * 4),
        out_shape=[shape] * 4, compiler_params=_cparams(1),
    )(c_idx, w, g_mine, g_other, m, v)


def _place():
    x, y, c = lax.axis_index("x"), lax.axis_index("y"), lax.axis_index("c")
    chips = [(1 - x, y), (x, 1 - y), (1 - x, 1 - y)]
    return x, y, c, chips


def _remote(send_sem, recv_sem, src, dst, to):
    return pltpu.make_async_remote_copy(src_ref=src, dst_ref=dst, send_sem=send_sem, recv_sem=recv_sem,
                                        device_id=to, device_id_type=MESH)


def _half(ref_rows, c):
    h = ref_rows // 2
    return pl.ds(pl.multiple_of(c * h, 8), h)


def _own_slot(shard, q):
    return lax.dynamic_update_slice(jnp.zeros((N_CHIPS,) + shard.shape, shard.dtype), shard[None], (q, 0, 0))


class _GatherChips:
    has_mid = True

    def __init__(self, bufs):
        n = len(bufs)
        self.inputs = list(bufs)
        self.out_shape = [jax.ShapeDtypeStruct(b.shape, b.dtype) for b in bufs]
        self.aliases = [(t, t) for t in range(n)]
        self.scratch = [pltpu.SemaphoreType.DMA((n, 6)), pltpu.SemaphoreType.DMA((n, 6))]

    def start(self, ins, outs, scr):
        send, recv = scr
        x, y, c, chips = _place()
        q = 2 * x + y
        for t, (i_ref, o_ref) in enumerate(zip(ins, outs)):
            rows = _half(i_ref.shape[1], c)
            for j, (cx, cy) in enumerate(chips):
                _remote(send.at[t, j], recv.at[t, j], i_ref.at[q, rows], o_ref.at[q, rows], (cx, cy, c)).start()

    def mid(self, ins, outs, scr):
        send, recv = scr
        x, y, c, chips = _place()
        for t, o_ref in enumerate(outs):
            rows = _half(o_ref.shape[1], c)
            for j, (cx, cy) in enumerate(chips):
                slot = o_ref.at[2 * cx + cy, rows]
                _remote(send.at[t, j], recv.at[t, j], slot, slot, (cx, cy, c)).wait_recv()
                _remote(send.at[t, 3 + j], recv.at[t, 3 + j], slot, slot, (x, y, 1 - c)).start()

    def finish(self, ins, outs, scr):
        send, recv = scr
        x, y, c, chips = _place()
        for t, o_ref in enumerate(outs):
            mine, other = _half(o_ref.shape[1], c), _half(o_ref.shape[1], 1 - c)
            for j, (cx, cy) in enumerate(chips):
                slot = o_ref.at[2 * cx + cy, other]
                _remote(send.at[t, 3 + j], recv.at[t, 3 + j], slot, slot, (x, y, 1 - c)).wait_recv()
            for j, (cx, cy) in enumerate(chips):
                sent = o_ref.at[2 * cx + cy, mine]
                _remote(send.at[t, j], recv.at[t, j], sent, sent, (cx, cy, c)).wait_send()
                _remote(send.at[t, 3 + j], recv.at[t, 3 + j], sent, sent, (x, y, 1 - c)).wait_send()


class _PairExchange:
    has_mid = False
    aliases = ()

    def __init__(self, arrs):
        n = len(arrs)
        self.inputs = list(arrs)
        self.out_shape = [jax.ShapeDtypeStruct((a.shape[0], a.shape[1] // 2, a.shape[2]), a.dtype) for a in arrs]
        self.scratch = [pltpu.SemaphoreType.DMA((n,)), pltpu.SemaphoreType.DMA((n,))]

    def _copies(self, ins, outs, scr):
        send, recv = scr
        x, y, c, _ = _place()
        return [_remote(send.at[t], recv.at[t], i_ref.at[:, _half(i_ref.shape[1], 1 - c)], o_ref, (x, y, 1 - c))
                for t, (i_ref, o_ref) in enumerate(zip(ins, outs))]

    def start(self, ins, outs, scr):
        for cp in self._copies(ins, outs, scr):
            cp.start()

    def finish(self, ins, outs, scr):
        for cp in self._copies(ins, outs, scr):
            cp.wait()


class _ChipScatter:
    has_mid = False
    aliases = ()

    def __init__(self, arrs):
        n = len(arrs)
        self.inputs = list(arrs)
        self.out_shape = [jax.ShapeDtypeStruct((3,) + a.shape[1:], a.dtype) for a in arrs]
        self.scratch = [pltpu.SemaphoreType.DMA((n, 3)), pltpu.SemaphoreType.DMA((n, 3))]

    def _copies(self, ins, outs, scr):
        send, recv = scr
        x, y, c, chips = _place()
        return [_remote(send.at[t, j], recv.at[t, j], i_ref.at[2 * cx + cy], o_ref.at[j], (cx, cy, c))
                for t, (i_ref, o_ref) in enumerate(zip(ins, outs)) for j, (cx, cy) in enumerate(chips)]

    def start(self, ins, outs, scr):
        for cp in self._copies(ins, outs, scr):
            cp.start()

    def finish(self, ins, outs, scr):
        for cp in self._copies(ins, outs, scr):
            cp.wait()


class _PairShare:
    has_mid = False
    aliases = ()

    def __init__(self, arrs):
        n = len(arrs)
        self.inputs = list(arrs)
        self.out_shape = [jax.ShapeDtypeStruct(a.shape, a.dtype) for a in arrs]
        self.scratch = [pltpu.SemaphoreType.DMA((n,)), pltpu.SemaphoreType.DMA((n,))]

    def _copies(self, ins, outs, scr):
        send, recv = scr
        x, y, c, _ = _place()
        return [_remote(send.at[t], recv.at[t], i_ref, o_ref, (x, y, 1 - c))
                for t, (i_ref, o_ref) in enumerate(zip(ins, outs))]

    def start(self, ins, outs, scr):
        for cp in self._copies(ins, outs, scr):
            cp.start()

    def finish(self, ins, outs, scr):
        for cp in self._copies(ins, outs, scr):
            cp.wait()


def _comm_call(hook, name):
    n_in, n_out = len(hook.inputs), len(hook.out_shape)

    def body(*refs):
        ins, outs, scr = refs[:n_in], refs[n_in:n_in + n_out], refs[n_in + n_out:]
        hook.start(ins, outs, scr)
        if hook.has_mid:
            hook.mid(ins, outs, scr)
        hook.finish(ins, outs, scr)

    return pl.pallas_call(body, name=name, in_specs=[ANY] * n_in, out_specs=[ANY] * n_out,
                          out_shape=list(hook.out_shape), scratch_shapes=list(hook.scratch),
                          input_output_aliases=dict(hook.aliases))(*hook.inputs)


def _all_gather_devices(vecs):
    n = len(vecs)

    def body(*refs):
        x_refs, out_refs = refs[:n], refs[n:2 * n]
        send_sems, recv_sems, local_sems = refs[2 * n:]
        x, y, c, chips = _place()
        me, sibling = (x, y, c), (x, y, 1 - c)
        waits = []
        for t, (x_ref, out_ref) in enumerate(zip(x_refs, out_refs)):
            def slot(px, py, pc, out_ref=out_ref):
                return out_ref.at[4 * px + 2 * py + pc]

            def copy(k, block, to, src=None, t=t, slot=slot):
                return pltpu.make_async_remote_copy(
                    src_ref=slot(*block) if src is None else src, dst_ref=slot(*block), send_sem=send_sems.at[t, k],
                    recv_sem=recv_sems.at[t, k], device_id=to, device_id_type=MESH)

            mine = pltpu.make_async_copy(x_ref, slot(*me), local_sems.at[t])
            mine.start()
            first = [copy(0, me, sibling, src=x_ref)]
            first += [copy(1 + j, me, (*chip, c), src=x_ref) for j, chip in enumerate(chips)]
            for cp in first:
                cp.start()
            waits.append((copy, mine, first))
        for copy, mine, first in waits:
            passed = [copy(4 + j, (*chip, c), sibling) for j, chip in enumerate(chips)]
            for j, chip in enumerate(chips):
                copy(1 + j, (*chip, c), me).wait_recv()
                passed[j].start()
            copy(0, sibling, me).wait_recv()
            for j, chip in enumerate(chips):
                copy(4 + j, (*chip, 1 - c), me).wait_recv()
            for cp in first + passed:
                cp.wait_send()
            mine.wait()

    vmem = pl.BlockSpec(memory_space=pltpu.VMEM)
    return pl.pallas_call(
        body, name="all_gather_devices", in_specs=[vmem] * n, out_specs=[vmem] * n,
        out_shape=[jax.ShapeDtypeStruct((N_DEV,) + v.shape, v.dtype) for v in vecs],
        scratch_shapes=[pltpu.SemaphoreType.DMA((n, 7)), pltpu.SemaphoreType.DMA((n, 7)),
                        pltpu.SemaphoreType.DMA((n,))],
    )(*vecs)


def _pair_sum(g, other, c_idx):
    nq, r, w = g.shape
    h = r // 2
    tr = _div_tile(h)
    nt = h // tr

    def body(c_ref, g_ref, o_ref, s_ref):
        s_ref[...] = (g_ref[...].astype(F32) + o_ref[...].astype(F32)).astype(s_ref.dtype)

    return pl.pallas_call(
        body, name="pair_sum",
        grid_spec=pltpu.PrefetchScalarGridSpec(
            num_scalar_prefetch=1, grid=(nq, nt),
            in_specs=[pl.BlockSpec((None, tr, w), lambda k, i, c_ref: (k, c_ref[0] * nt + i, 0)),
                      pl.BlockSpec((None, tr, w), lambda k, i, c_ref: (k, i, 0))],
            out_specs=pl.BlockSpec((None, tr, w), lambda k, i, c_ref: (k, i, 0))),
        out_shape=jax.ShapeDtypeStruct((nq, h, w), g.dtype),
        compiler_params=_cparams(2),
    )(c_idx, g, other)


def _chip_sum(s, others, q_idx):
    _, h, w = s.shape
    tr = _div_tile(h)

    def body(q_ref, s_ref, o_ref, out_ref):
        out_ref[...] = ((s_ref[...].astype(F32) + o_ref[0].astype(F32)) + o_ref[1].astype(F32)) + o_ref[2].astype(F32)

    return pl.pallas_call(
        body, name="chip_sum",
        grid_spec=pltpu.PrefetchScalarGridSpec(
            num_scalar_prefetch=1, grid=(h // tr,),
            in_specs=[pl.BlockSpec((None, tr, w), lambda i, q_ref: (q_ref[0], i, 0)),
                      pl.BlockSpec((3, tr, w), lambda i, q_ref: (0, i, 0))],
            out_specs=pl.BlockSpec((tr, w), lambda i, q_ref: (i, 0))),
        out_shape=jax.ShapeDtypeStruct((h, w), F32),
        compiler_params=_cparams(1),
    )(q_idx, s, others)


def _small_update(q_idx, parts, ws, ms, vs, col_block):
    n = len(parts)
    has_w = [w is not None for w in ws]

    def body(q_ref, *refs):
        pos = 0
        ins = []
        for t in range(n):
            k = 4 if has_w[t] else 1
            ins.append(refs[pos:pos + k])
            pos += k
        outs = refs[pos:]
        opos = 0
        for t in range(n):
            p_ref = ins[t][0]
            g = p_ref[0]
            for s in range(1, p_ref.shape[0]):
                g = g + p_ref[s]
            if has_w[t]:
                _, w_ref, m_ref, v_ref = ins[t]
                g_ref, d_ref, nm_ref, nv_ref = outs[opos:opos + 4]
                opos += 4
                g_ref[...] = g
                d_ref[...], nm_ref[...], nv_ref[...] = _adamw_update(w_ref[...], g, m_ref[...], v_ref[...])
            else:
                outs[opos][...] = g
                opos += 1

    def whole(shape):
        nd = len(shape)
        return pl.BlockSpec(shape, lambda i, q_ref: (0,) * nd)

    in_specs, out_specs, out_shape, args = [], [], [], []
    for t in range(n):
        k, r, wf = parts[t].shape
        if col_block[t]:
            w = wf // N_CHIPS
            in_specs.append(pl.BlockSpec((k, r, w), lambda i, q_ref: (0, 0, q_ref[0])))
        else:
            w = wf
            in_specs.append(whole((k, r, wf)))
        args.append(parts[t])
        if has_w[t]:
            assert ws[t].shape == (r, w), (ws[t].shape, r, w)
            in_specs += [whole((r, w))] * 3
            args += [ws[t], ms[t], vs[t]]
            out_specs += [whole((r, w))] * 4
            out_shape += [jax.ShapeDtypeStruct((r, w), F32)] * 4
        else:
            out_specs.append(whole((r, w)))
            out_shape.append(jax.ShapeDtypeStruct((r, w), F32))
    return pl.pallas_call(
        body, name="small_update",
        grid_spec=pltpu.PrefetchScalarGridSpec(num_scalar_prefetch=1, grid=(1,), in_specs=in_specs,
                                               out_specs=out_specs),
        out_shape=out_shape, compiler_params=_cparams(1),
    )(q_idx, *args)


def _pack_win(w_in):
    o = np.cumsum((0,) + IN_SPLITS)
    gq, gk, gv, gg, ga, sq, sk, sv = [w_in[:, o[i]:o[i + 1]] for i in range(8)]
    z = jnp.zeros((w_in.shape[0], 128 - GLA_RANK), w_in.dtype)
    dup = lambda a: jnp.concatenate([a[:, :64], a[:, :64], a[:, 64:], a[:, 64:]], axis=1)
    return jnp.concatenate([gq, gk, gv, gg, ga, z, sq, dup(sk), dup(sv)], axis=1)


def _unpack_dwin(d):
    und = lambda a: jnp.concatenate([a[:, 0:64] + a[:, 64:128], a[:, 128:192] + a[:, 192:256]], axis=1)
    return jnp.concatenate([d[:, :P_GA], d[:, P_GA:P_GA + GLA_RANK], d[:, P_SQ:P_SK], und(d[:, P_SK:P_SV]),
                            und(d[:, P_SV:P_END])], axis=1)


def _local_step(x, target, meta, p):
    s = x.shape[0]
    t = s + BLK
    h0 = jnp.concatenate([jnp.zeros((PAD, D_MODEL), F32), meta, x], axis=0)
    cos, sin = _rope_tables(t)

    h1, n1, g1, u1, a1, f1 = _ffn_fwd(h0, p["ffn1_pre_norm"], p["ffn1_w_gate"], p["ffn1_w_up"], p["ffn1_w_down"],
                                      p["ffn1_post_norm"])
    n2, gq, gk, gv, gg, ga, la, sq, sk, sv = _mix_proj(h1, p["mix_pre_norm"], p["w_in"], p["gla_w_a2"], p["gla_b_a"],
                                                       cos, sin)
    ogla, ss = _gla_fwd(gq, gk, gv, la)
    oswa = _swa_fwd(p["swa_sinks"], sq, sk, sv)
    h2, cat, m = _mix_out(h1, ogla, gg, oswa, p["gla_out_norm"], p["swa_out_norm"], p["w_out"], p["mix_post_norm"])
    dy, n3, g3, u3, a3, f3, sse = _ffn_fwd(h2, p["ffn2_pre_norm"], p["ffn2_w_gate"], p["ffn2_w_up"],
                                           p["ffn2_w_down"], p["ffn2_post_norm"], target=target)

    grads = {}
    dh2, df3, dg3, du3, grads["ffn2_pre_norm"], grads["ffn2_post_norm"] = _ffn_bwd(
        dy, h2, f3, g3, u3, p["ffn2_pre_norm"], p["ffn2_w_gate"], p["ffn2_w_up"], p["ffn2_w_down"],
        p["ffn2_post_norm"])
    (gud,) = _ffn_wgrad(n3, df3, dg3, du3, a3)
    grads["ffn2_w_gate"], grads["ffn2_w_up"], grads["ffn2_w_down"] = gud[:, :FJ], gud[:, FJ:2 * FJ], gud[:, 2 * FJ:]

    dogla, dgg, doswa, dm, grads["mix_post_norm"], grads["gla_out_norm"], grads["swa_out_norm"] = _mix_out_bwd(
        dh2, m, ogla, gg, oswa, p["gla_out_norm"], p["swa_out_norm"], p["w_out"], p["mix_post_norm"])
    grads["w_out"] = _xty(cat, dm)
    dsq, dsk, dsv, dkm, dvm, dsinks = _swa_bwd(p["swa_sinks"], sq, sk, sv, oswa, doswa)
    grads["swa_sinks"] = dsinks[:, 0]
    dgq, dgk, dgv, dla = _gla_bwd(gq, gk, gv, la, ss, dogla)
    dh1, dproj, grads["mix_pre_norm"], dwa2p, grads["gla_b_a"] = _mix_in_bwd(
        dh2, h1, p["mix_pre_norm"], p["w_in"], p["gla_w_a2"], p["gla_b_a"], cos, sin, ga, dgq, dgk, dgv, dgg, dla,
        dsq, dsk, dsv, dkm, dvm)
    grads["gla_w_a2"] = dwa2p[:GLA_RANK]
    grads["w_in"] = _unpack_dwin(_xty(n2, dproj))

    dh0, df1, dg1, du1, grads["ffn1_pre_norm"], grads["ffn1_post_norm"] = _ffn_bwd(
        dh1, h0, f1, g1, u1, p["ffn1_pre_norm"], p["ffn1_w_gate"], p["ffn1_w_up"], p["ffn1_w_down"],
        p["ffn1_post_norm"])
    (gud,) = _ffn_wgrad(n1, df1, dg1, du1, a1)
    grads["ffn1_w_gate"], grads["ffn1_w_up"], grads["ffn1_w_down"] = gud[:, :FJ], gud[:, FJ:2 * FJ], gud[:, 2 * FJ:]
    grads["meta_tokens"] = dh0[PAD:BLK]
    return sse[0, 0], dh0[BLK:], grads


WEIGHTS = ['meta_tokens', 'ffn1_pre_norm', 'ffn1_w_gate', 'ffn1_w_up', 'ffn1_w_down', 'ffn1_post_norm',
           'mix_pre_norm', 'w_in', 'gla_w_a2', 'gla_b_a', 'gla_out_norm', 'swa_sinks', 'swa_out_norm', 'w_out',
           'mix_post_norm', 'ffn2_pre_norm', 'ffn2_w_gate', 'ffn2_w_up', 'ffn2_w_down', 'ffn2_post_norm']
BIG = ['ffn1_w_gate', 'ffn1_w_up', 'ffn1_w_down', 'w_in', 'w_out', 'ffn2_w_gate', 'ffn2_w_up', 'ffn2_w_down']
SMALL = [n for n in WEIGHTS if n not in BIG]
FJ = D_FF // N_CHIPS
D_IN_J = D_IN // N_CHIPS
D_OUT_J = D_MODEL // N_CHIPS
TRANSPOSED = ('ffn1_w_gate', 'ffn1_w_up', 'ffn2_w_gate', 'ffn2_w_up')


def _shard2d(name, a):
    return a[0].T if name in TRANSPOSED else a[0]


def _unshard2d(name, a):
    return (a.T if name in TRANSPOSED else a)[None]


def _small_rows(name, a):
    flat = a.reshape(-1)
    rows = -(-flat.shape[0] // 1024) * 8
    return jnp.pad(flat, (0, rows * 128 - flat.shape[0])).reshape(rows, 128)


def kernel(x, meta_tokens, ffn1_pre_norm, ffn1_w_gate, ffn1_w_up, ffn1_w_down, ffn1_post_norm, mix_pre_norm, w_in, gla_w_a2, gla_b_a, gla_out_norm, swa_sinks, swa_out_norm, w_out, mix_post_norm, ffn2_pre_norm, ffn2_w_gate, ffn2_w_up, ffn2_w_down, ffn2_post_norm, loss_target, m_meta_tokens, m_ffn1_pre_norm, m_ffn1_w_gate, m_ffn1_w_up, m_ffn1_w_down, m_ffn1_post_norm, m_mix_pre_norm, m_w_in, m_gla_w_a2, m_gla_b_a, m_gla_out_norm, m_swa_sinks, m_swa_out_norm, m_w_out, m_mix_post_norm, m_ffn2_pre_norm, m_ffn2_w_gate, m_ffn2_w_up, m_ffn2_w_down, m_ffn2_post_norm, v_meta_tokens, v_ffn1_pre_norm, v_ffn1_w_gate, v_ffn1_w_up, v_ffn1_w_down, v_ffn1_post_norm, v_mix_pre_norm, v_w_in, v_gla_w_a2, v_gla_b_a, v_gla_out_norm, v_swa_sinks, v_swa_out_norm, v_w_out, v_mix_post_norm, v_ffn2_pre_norm, v_ffn2_w_gate, v_ffn2_w_up, v_ffn2_w_down, v_ffn2_post_norm):
    args = dict(locals())
    w = {n: args[n] for n in WEIGHTS}
    mom = {n: args["m_" + n] for n in WEIGHTS}
    var = {n: args["v_" + n] for n in WEIGHTS}
    cx, cy, cc = lax.axis_index("x"), lax.axis_index("y"), lax.axis_index("c")
    q_idx = (2 * cx + cy).astype(jnp.int32).reshape(1)
    c_idx = cc.astype(jnp.int32).reshape(1)

    q_chip = 2 * cx + cy
    bf = {n: _own_slot(_shard2d(n, w[n]).astype(BF16), q_chip) for n in BIG}
    early = _GatherChips([bf["ffn1_w_gate"], bf["ffn1_w_up"], bf["ffn1_w_down"], _own_slot(w["meta_tokens"], q_chip),
                          _own_slot(w["gla_w_a2"].reshape(GLA_RANK, GLA_KW // N_CHIPS), q_chip)])
    wg1, wu1, wd1, meta4, wa24 = _comm_call(early, "gather_ffn1")
    meta_full = meta4.transpose(1, 0, 2).reshape(N_META, D_MODEL)
    wa2p = jnp.pad(wa24.transpose(1, 0, 2).reshape(GLA_RANK, GLA_KW), ((0, 128 - GLA_RANK), (0, 0))).astype(BF16)
    sinks = w["swa_sinks"].reshape(SWA_QH)

    seq, target = x[0], loss_target[0]
    t = seq.shape[0] + BLK
    h0 = jnp.concatenate([jnp.zeros((PAD, D_MODEL), F32), meta_full, seq], axis=0)
    cos, sin = _rope_tables(t)
    late = _GatherChips([bf["w_in"], bf["w_out"], bf["ffn2_w_gate"], bf["ffn2_w_up"], bf["ffn2_w_down"]])
    (h1, n1, g1, u1, a1, f1), (win4, wout4, wg2, wu2, wd2) = _ffn_fwd(
        h0, w["ffn1_pre_norm"], wg1, wu1, wd1, w["ffn1_post_norm"], hook=late)
    winp = _pack_win(win4.transpose(1, 0, 2).reshape(D_MODEL, D_IN))
    wout = wout4.reshape(D_MODEL, D_MODEL)
    n2, gq, gk, gv, gg, ga, la, sq, sk, sv = _mix_proj(h1, w["mix_pre_norm"], winp, wa2p, w["gla_b_a"], cos, sin)
    ogla, ss = _gla_fwd(gq, gk, gv, la)
    oswa = _swa_fwd(sinks, sq, sk, sv)
    h2, cat, m = _mix_out(h1, ogla, gg, oswa, w["gla_out_norm"], w["swa_out_norm"], wout, w["mix_post_norm"])
    dy, n3, g3, u3, a3, f3, sse = _ffn_fwd(h2, w["ffn2_pre_norm"], wg2, wu2, wd2, w["ffn2_post_norm"], target=target)
    loss = lax.psum(sse[0, 0] * (0.5 / D_MODEL), ("x", "y", "c"))

    g = {}
    dh2, df3, dg3, du3, g["ffn2_pre_norm"], g["ffn2_post_norm"] = _ffn_bwd(
        dy, h2, f3, g3, u3, w["ffn2_pre_norm"], wg2, wu2, wd2, w["ffn2_post_norm"])
    (gf2,) = _ffn_wgrad(n3, df3, dg3, du3, a3)
    (dogla, dgg, doswa, dm, g["mix_post_norm"], g["gla_out_norm"], g["swa_out_norm"]), (rgf2,) = _mix_out_bwd(
        dh2, m, ogla, gg, oswa, w["gla_out_norm"], w["swa_out_norm"], wout, w["mix_post_norm"],
        hook=_PairExchange([gf2]))
    sgf2 = _pair_sum(gf2, rgf2, c_idx)
    gout = _xty(cat, dm).reshape(N_CHIPS, D_OUT_J, D_MODEL).astype(BF16)
    (dsq, dsk, dsv, dkm, dvm, dsinks), (ogf2,) = _swa_bwd(sinks, sq, sk, sv, oswa, doswa,
                                                          hook=_ChipScatter([sgf2]))
    g["swa_sinks"] = dsinks
    dgq, dgk, dgv, dla = _gla_bwd(gq, gk, gv, la, ss, dogla)
    dh1, dproj, g["mix_pre_norm"], dwa2p, g["gla_b_a"] = _mix_in_bwd(
        dh2, h1, w["mix_pre_norm"], winp, wa2p, w["gla_b_a"], cos, sin, ga, dgq, dgk, dgv, dgg, dla,
        dsq, dsk, dsv, dkm, dvm)
    g["gla_w_a2"] = dwa2p[:GLA_RANK]
    gin = _unpack_dwin(_xty(n2, dproj)).reshape(D_MODEL, N_CHIPS, D_IN_J).transpose(1, 0, 2).astype(BF16)
    (dh0, df1, dg1, du1, g["ffn1_pre_norm"], g["ffn1_post_norm"]), (rgin, rgout) = _ffn_bwd(
        dh1, h0, f1, g1, u1, w["ffn1_pre_norm"], wg1, wu1, wd1, w["ffn1_post_norm"],
        hook=_PairExchange([gin, gout]))
    sgin, sgout = _pair_sum(gin, rgin, c_idx), _pair_sum(gout, rgout, c_idx)
    (gf1,), (ogin, ogout) = _ffn_wgrad(n1, df1, dg1, du1, a1, hook=_ChipScatter([sgin, sgout]))
    g["meta_tokens"] = dh0[PAD:BLK]
    grad_x = dh0[BLK:]
    (rgf1,) = _comm_call(_PairExchange([gf1]), "pair_exchange_ffn1")
    sgf1 = _pair_sum(gf1, rgf1, c_idx)
    (ogf1,) = _comm_call(_ChipScatter([sgf1]), "chip_scatter_ffn1")
    halves = [_chip_sum(s, o, q_idx) for s, o in ((sgf1, ogf1), (sgin, ogin), (sgout, ogout), (sgf2, ogf2))]
    others = _comm_call(_PairShare(halves), "pair_share")
    reduced = {"ffn1_w_gate": (0, 0), "ffn1_w_up": (0, FJ), "ffn1_w_down": (0, 2 * FJ), "w_in": (1, 0),
               "w_out": (2, 0), "ffn2_w_gate": (3, 0), "ffn2_w_up": (3, FJ), "ffn2_w_down": (3, 2 * FJ)}
    grad, delta, new_m, new_v = {}, {}, {}, {}
    for n in BIG:
        k, row0 = reduced[n]
        outs = _adamw_halves(_shard2d(n, w[n]), halves[k], others[k], _shard2d(n, mom[n]), _shard2d(n, var[n]),
                             c_idx, row0)
        grad[n], delta[n], new_m[n], new_v[n] = [_unshard2d(n, a) for a in outs]

    late = ["gla_w_a2", "swa_sinks"]
    direct = [n for n in SMALL if n not in late]
    names = direct + late
    gathered = _all_gather_devices([g[n] for n in names])
    mat = lambda a: a.reshape(a.shape[-2:])
    none2 = [None] * len(late)
    outs = _small_update(q_idx, gathered, [mat(w[n]) for n in direct] + none2, [mat(mom[n]) for n in direct] + none2,
                         [mat(var[n]) for n in direct] + none2, [n == "meta_tokens" for n in names])
    sum_a2, sum_sinks = outs[4 * len(direct):]
    g_late = [lax.dynamic_slice_in_dim(sum_a2, q_chip * (GLA_KW // N_CHIPS), GLA_KW // N_CHIPS, axis=1)[None],
              sum_sinks[:, 0].reshape(1, 1, SWA_QH)]
    outs = list(outs[:4 * len(direct)]) + list(_small_update(
        q_idx, g_late, [mat(w[n]) for n in late], [mat(mom[n]) for n in late], [mat(var[n]) for n in late],
        [False, False]))
    for k, n in enumerate(names):
        grad[n], delta[n], new_m[n], new_v[n] = [a.reshape(w[n].shape) for a in outs[4 * k:4 * k + 4]]

    return (loss, grad_x[None], *[grad[n] for n in WEIGHTS], *[delta[n] for n in WEIGHTS],
            *[new_m[n] for n in WEIGHTS], *[new_v[n] for n in WEIGHTS])
```

```python
import functools
import math

import numpy as np
import jax
import jax.numpy as jnp
from jax import lax
from jax.experimental import pallas as pl
from jax.experimental.pallas import tpu as pltpu

F32 = jnp.float32
BF16 = jnp.bfloat16
MESH = pl.DeviceIdType.MESH

D_MODEL = 1024
D_FF = 2816
N_CHIPS = 4
N_DEV = 8
N_META = 16
BLK = 128
PAD = BLK - N_META
GLA_CHUNK = 64
GLA_HEADS = 4
GLA_DV = 128
GLA_DK = 64
GLA_KW = GLA_HEADS * GLA_DK
GLA_W = GLA_HEADS * GLA_DV
GLA_RANK = 16
GLA_TAU = 16.0
SWA_HD = 64
SWA_QH = 8
SWA_KVH = 2
SWA_W = SWA_QH * SWA_HD
WINDOW = 128
ROPE_THETA = 10000.0
EPS = 1e-6
NEG_INF = -1e30
IN_SPLITS = (256, 256, 512, 512, 16, 512, 128, 128)
D_IN = sum(IN_SPLITS)
P_GQ, P_GK, P_GV, P_GG, P_GA, P_SQ, P_SK, P_SV, P_END = 0, 256, 512, 1024, 1536, 1664, 2176, 2432, 2688
ADAM_LR, ADAM_B1, ADAM_B2, ADAM_EPS, ADAM_WD, ADAM_STEP = 0.001, 0.9, 0.999, 1e-08, 0.01, 10
VMEM_LIMIT = 56 * 1024 * 1024

NT = (((1,), (1,)), ((), ()))
TN = (((0,), (0,)), ((), ()))


def _cparams(n_axes):
    return pltpu.CompilerParams(dimension_semantics=("arbitrary",) * n_axes, vmem_limit_bytes=VMEM_LIMIT)


def _row_tile(t):
    for tm in (640, 512, 384, 256, 128):
        if t % tm == 0:
            return tm
    raise ValueError(t)


SEQ_BLOCKS_PER_STEP = 5


def _seq_tile(t):
    return SEQ_BLOCKS_PER_STEP * BLK if t % (SEQ_BLOCKS_PER_STEP * BLK) == 0 else BLK


ROW_PARTS = 2


def _row_parts(tm):
    n = ROW_PARTS if tm % (16 * ROW_PARTS) == 0 else 1
    return [slice(k * (tm // n), (k + 1) * (tm // n)) for k in range(n)]


def _contract_tile(t):
    return 1664 if t % 1664 == 0 else _row_tile(t)


def _div_tile(r, cap=512):
    best = None
    for tr in range(8, min(r, cap) + 1, 8):
        if r % tr == 0:
            best = tr
    return best if best is not None else r


def _dot(a, b):
    return jnp.dot(a, b, preferred_element_type=F32)


def _dg(a, b, dims):
    return lax.dot_general(a, b, dims, preferred_element_type=F32)


def _rms(x, w):
    r = lax.rsqrt(jnp.mean(x * x, axis=-1, keepdims=True) + EPS)
    xh = x * r
    return xh * w, xh, r


def _rms_bwd(xh, r, w, dy):
    wdy = dy * w
    dx = r * (wdy - xh * jnp.mean(wdy * xh, axis=-1, keepdims=True))
    dw = jnp.sum(dy * xh, axis=0, keepdims=True)
    return dx, dw


def _sigmoid(x):
    return 1.0 / (1.0 + jnp.exp(-x))


def _full(shape):
    nd = len(shape)
    return pl.BlockSpec(shape, lambda *_: (0,) * nd)


ANY = pl.BlockSpec(memory_space=pl.ANY)


def _pallas(body, *, name, grid, in_specs, out_specs, out_shape, args, scratch_shapes=(), hook=None):
    n_axes = len(grid)
    if hook is None:
        return pl.pallas_call(body, name=name, grid=grid, in_specs=list(in_specs), out_specs=list(out_specs),
                              out_shape=list(out_shape), scratch_shapes=list(scratch_shapes),
                              compiler_params=_cparams(n_axes))(*args)
    n_in, n_out, n_scr = len(in_specs), len(out_specs), len(scratch_shapes)
    h_in, h_out = len(hook.inputs), len(hook.out_shape)
    total = math.prod(grid)

    def wrapped(*refs):
        ins, hins = refs[:n_in], refs[n_in:n_in + h_in]
        o0 = n_in + h_in
        outs, houts = refs[o0:o0 + n_out], refs[o0 + n_out:o0 + n_out + h_out]
        s0 = o0 + n_out + h_out
        scr, hscr = refs[s0:s0 + n_scr], refs[s0 + n_scr:]
        step = pl.program_id(0)
        for a in range(1, n_axes):
            step = step * grid[a] + pl.program_id(a)

        @pl.when(step == 0)
        def _():
            hook.start(hins, houts, hscr)

        body(*ins, *outs, *scr)

        if hook.has_mid:
            @pl.when(step == (3 * total) // 4)
            def _():
                hook.mid(hins, houts, hscr)

        @pl.when(step == total - 1)
        def _():
            hook.finish(hins, houts, hscr)

    res = pl.pallas_call(
        wrapped, name=name, grid=grid, in_specs=list(in_specs) + [ANY] * h_in,
        out_specs=list(out_specs) + [ANY] * h_out, out_shape=list(out_shape) + list(hook.out_shape),
        scratch_shapes=list(scratch_shapes) + list(hook.scratch), compiler_params=_cparams(n_axes),
        input_output_aliases={n_in + a: n_out + b for a, b in hook.aliases},
    )(*args, *hook.inputs)
    return res[:n_out], res[n_out:]


def _ffn_fwd(h, wpre, wg4, wu4, wd4, wpost, hook=None, target=None):
    t = h.shape[0]
    tm = _row_tile(t)
    nj, fj, _ = wg4.shape
    nblk = tm // BLK if target is not None else 0

    def body(*refs):
        h_ref, wpre_ref, wg_ref, wu_ref, wd_ref, wpost_ref = refs[:6]
        t_refs = refs[6:6 + nblk]
        hout_ref, n_ref, p1_ref, p2_ref, a_ref, f_ref = refs[6 + nblk:12 + nblk]
        acc_ref = refs[-1]
        i = pl.program_id(0)
        j = pl.program_id(1)

        @pl.when(j == 0)
        def _():
            y, _, _ = _rms(h_ref[...], wpre_ref[...])
            n_ref[...] = y.astype(BF16)
            acc_ref[...] = jnp.zeros_like(acc_ref)

        if target is not None:
            sse_ref = refs[12 + nblk]

            @pl.when((i == 0) & (j == 0))
            def _():
                sse_ref[...] = jnp.zeros_like(sse_ref)

        parts = [slice(0, tm)]
        gus = [(_dg(n_ref[rows, :], wg_ref[...], NT), _dg(n_ref[rows, :], wu_ref[...], NT)) for rows in parts]
        for rows, (g, u) in zip(parts, gus):
            sg = _sigmoid(g)
            silu = g * sg
            p1_ref[rows, :] = (u * (sg + silu * (1.0 - sg))).astype(BF16)
            p2_ref[rows, :] = silu.astype(BF16)
            a = (silu * u).astype(BF16)
            a_ref[rows, :] = a
            acc_ref[rows, :] += _dot(a, wd_ref[...])

        @pl.when(j == nj - 1)
        def _():
            f = acc_ref[...]
            f_ref[...] = f
            y, _, _ = _rms(f, wpost_ref[...])
            hout = h_ref[...] + 0.5 * y
            if target is None:
                hout_ref[...] = hout
            else:
                sse = jnp.zeros((1, 1), F32)
                for k in range(nblk):
                    rows = slice(k * BLK, (k + 1) * BLK)
                    err = hout[rows] - t_refs[k][...]
                    if k == 0:
                        err = jnp.where(i > 0, err, 0.0)
                    hout_ref[rows, :] = err * (1.0 / D_MODEL)
                    sse = sse + jnp.sum(jnp.sum(err * err, axis=1, keepdims=True), axis=0, keepdims=True)
                sse_ref[...] += jnp.broadcast_to(sse, sse_ref.shape)

    row = pl.BlockSpec((tm, D_MODEL), lambda i, j: (i, 0))
    vec = pl.BlockSpec((1, D_MODEL), lambda i, j: (0, 0))
    wrow = pl.BlockSpec((None, fj, D_MODEL), lambda i, j: (j, 0, 0))
    act = pl.BlockSpec((None, tm, fj), lambda i, j: (j, i, 0))
    t_specs = [pl.BlockSpec((BLK, D_MODEL), functools.partial(lambda i, j, k: (jnp.maximum(nblk * i + k - 1, 0), 0), k=k))
               for k in range(nblk)]
    loss_spec = [_full((1, 128))] if target is not None else []
    loss_shape = [jax.ShapeDtypeStruct((1, 128), F32)] if target is not None else []
    return _pallas(
        body, name="ffn_fwd", grid=(t // tm, nj),
        in_specs=[row, vec, wrow, wrow, wrow, vec] + t_specs,
        out_specs=[row, row, act, act, act, row] + loss_spec,
        out_shape=[jax.ShapeDtypeStruct((t, D_MODEL), F32), jax.ShapeDtypeStruct((t, D_MODEL), BF16),
                   jax.ShapeDtypeStruct((nj, t, fj), BF16), jax.ShapeDtypeStruct((nj, t, fj), BF16),
                   jax.ShapeDtypeStruct((nj, t, fj), BF16), jax.ShapeDtypeStruct((t, D_MODEL), F32)] + loss_shape,
        scratch_shapes=[pltpu.VMEM((tm, D_MODEL), F32)],
        args=(h, wpre, wg4, wu4, wd4, wpost) + (target,) * nblk, hook=hook)


def _ffn_bwd(dhout, h, f, p14, p24, wpre, wg4, wu4, wd4, wpost, hook=None):
    t = h.shape[0]
    tm = _row_tile(t)
    nj, fj, _ = wg4.shape

    def body(dhout_ref, h_ref, f_ref, p1_ref, p2_ref, wpre_ref, wg_ref, wu_ref, wd_ref, wpost_ref,
             dh_ref, df_ref, dg_ref, du_ref, dwpre_ref, dwpost_ref, dn_ref):
        i = pl.program_id(0)
        j = pl.program_id(1)

        @pl.when((i == 0) & (j == 0))
        def _():
            dwpre_ref[...] = jnp.zeros_like(dwpre_ref)
            dwpost_ref[...] = jnp.zeros_like(dwpost_ref)

        @pl.when(j == 0)
        def _():
            wpost = wpost_ref[...]
            _, fh, r = _rms(f_ref[...], wpost)
            df, dw = _rms_bwd(fh, r, wpost, 0.5 * dhout_ref[...])
            dwpost_ref[...] += dw
            df_ref[...] = df.astype(BF16)
            dn_ref[...] = jnp.zeros_like(dn_ref)

        parts = _row_parts(tm)
        das = [_dg(df_ref[rows, :], wd_ref[...], NT) for rows in parts]
        for rows, da in zip(parts, das):
            dg = (da * p1_ref[rows, :].astype(F32)).astype(BF16)
            du = (da * p2_ref[rows, :].astype(F32)).astype(BF16)
            dg_ref[rows, :] = dg
            du_ref[rows, :] = du
            dn_ref[rows, :] += _dot(dg, wg_ref[...]) + _dot(du, wu_ref[...])

        @pl.when(j == nj - 1)
        def _():
            wpre = wpre_ref[...]
            _, hh, r = _rms(h_ref[...], wpre)
            dx, dw = _rms_bwd(hh, r, wpre, dn_ref[...])
            dwpre_ref[...] += dw
            dh_ref[...] = dhout_ref[...] + dx

    row = pl.BlockSpec((tm, D_MODEL), lambda i, j: (i, 0))
    vec = pl.BlockSpec((1, D_MODEL), lambda i, j: (0, 0))
    wrow = pl.BlockSpec((None, fj, D_MODEL), lambda i, j: (j, 0, 0))
    act = pl.BlockSpec((None, tm, fj), lambda i, j: (j, i, 0))
    actshape = jax.ShapeDtypeStruct((nj, t, fj), BF16)
    return _pallas(
        body, name="ffn_bwd", grid=(t // tm, nj),
        in_specs=[row, row, row, act, act, vec, wrow, wrow, wrow, vec],
        out_specs=[row, row, act, act, vec, vec],
        out_shape=[jax.ShapeDtypeStruct((t, D_MODEL), F32), jax.ShapeDtypeStruct((t, D_MODEL), BF16),
                   actshape, actshape,
                   jax.ShapeDtypeStruct((1, D_MODEL), F32), jax.ShapeDtypeStruct((1, D_MODEL), F32)],
        scratch_shapes=[pltpu.VMEM((tm, D_MODEL), F32)],
        args=(dhout, h, f, p14, p24, wpre, wg4, wu4, wd4, wpost), hook=hook)


def _ffn_wgrad(n, df, dg4, du4, a4, hook=None):
    t = n.shape[0]
    tm = _contract_tile(t)
    ni = t // tm
    nj, _, fj = dg4.shape

    def body(n_ref, df_ref, dg_ref, du_ref, a_ref, dw_ref, acc):
        i = pl.program_id(1)

        @pl.when(i == 0)
        def _():
            acc[...] = jnp.zeros_like(acc)

        nn = n_ref[...]
        acc[0:fj, :] += _dg(dg_ref[...], nn, TN)
        acc[fj:2 * fj, :] += _dg(du_ref[...], nn, TN)
        acc[2 * fj:3 * fj, :] += _dg(a_ref[...], df_ref[...], TN)

        @pl.when(i == ni - 1)
        def _():
            dw_ref[...] = acc[...].astype(BF16)

    row = pl.BlockSpec((tm, D_MODEL), lambda j, i: (i, 0))
    act = pl.BlockSpec((None, tm, fj), lambda j, i: (j, i, 0))
    return _pallas(
        body, name="ffn_wgrad", grid=(nj, ni),
        in_specs=[row, row, act, act, act],
        out_specs=[pl.BlockSpec((None, 3 * fj, D_MODEL), lambda j, i: (j, 0, 0))],
        out_shape=[jax.ShapeDtypeStruct((nj, 3 * fj, D_MODEL), BF16)],
        scratch_shapes=[pltpu.VMEM((3 * fj, D_MODEL), F32)],
        args=(n, df, dg4, du4, a4), hook=hook)


def _xty(x, y):
    t, k = x.shape
    n = y.shape[1]
    tm = _contract_tile(t)
    tn = n if n <= 1024 else (896 if n % 896 == 0 else 128)

    def body(x_ref, y_ref, o_ref):
        @pl.when(pl.program_id(1) == 0)
        def _():
            o_ref[...] = jnp.zeros_like(o_ref)

        o_ref[...] += _dg(x_ref[...], y_ref[...], TN)

    return pl.pallas_call(
        body, name="xty", grid=(n // tn, t // tm),
        in_specs=[pl.BlockSpec((tm, k), lambda j, i: (i, 0)), pl.BlockSpec((tm, tn), lambda j, i: (i, j))],
        out_specs=pl.BlockSpec((k, tn), lambda j, i: (0, j)),
        out_shape=jax.ShapeDtypeStruct((k, n), F32),
        compiler_params=_cparams(2),
    )(x, y)


def _rope_tables(t):
    pos = (jnp.arange(t, dtype=jnp.int32) - PAD).astype(F32)
    inv_freq = 1.0 / (ROPE_THETA ** (jnp.arange(0, SWA_HD, 2, dtype=F32) / SWA_HD))
    ang = pos[:, None] * inv_freq[None, :]
    cos = jnp.cos(ang)
    sin = jnp.sin(ang)
    return jnp.concatenate([cos, cos, cos, cos], axis=1), jnp.concatenate([-sin, sin, -sin, sin], axis=1)


def _rot_half(x, first_half):
    return jnp.where(first_half, pltpu.roll(x, 96, 1), pltpu.roll(x, 32, 1))


def _first_half_mask(rows):
    lane = lax.broadcasted_iota(jnp.int32, (rows, 128), 1)
    return (lane % 64) < 32


def _log_sigmoid(z):
    return jnp.minimum(z, 0.0) - jnp.log(1.0 + jnp.exp(-jnp.abs(z)))


def _mix_proj(h1, wmixpre, winp, wa2p, bap, cos, sin):
    t = h1.shape[0]
    tm = _row_tile(t)

    def body(h_ref, w_ref, win_ref, wa2_ref, ba_ref, cos_ref, sin_ref,
             n_ref, gq_ref, gk_ref, gv_ref, gg_ref, ga_ref, la_ref, sq_ref, sk_ref, sv_ref):
        y, _, _ = _rms(h_ref[...], w_ref[...])
        n = y.astype(BF16)
        n_ref[...] = n
        proj = _dot(n, win_ref[...])
        gq_ref[...] = proj[:, P_GQ:P_GK]
        gk_ref[...] = proj[:, P_GK:P_GV]
        gv_ref[...] = proj[:, P_GV:P_GG]
        gg_ref[...] = proj[:, P_GG:P_GA]
        ga = proj[:, P_GA:P_SQ]
        ga_ref[...] = ga
        z = _dot(ga.astype(BF16), wa2_ref[...]) + ba_ref[...]
        la_ref[...] = _log_sigmoid(z) * (1.0 / GLA_TAU)
        c = cos_ref[...]
        s = sin_ref[...]
        fh = _first_half_mask(tm)
        for k in range(4):
            x = proj[:, P_SQ + 128 * k:P_SQ + 128 * (k + 1)]
            sq_ref[:, 128 * k:128 * (k + 1)] = (x * c + _rot_half(x, fh) * s).astype(BF16)
        for k in range(2):
            x = proj[:, P_SK + 128 * k:P_SK + 128 * (k + 1)]
            sk_ref[:, 128 * k:128 * (k + 1)] = (x * c + _rot_half(x, fh) * s).astype(BF16)
        sv_ref[...] = proj[:, P_SV:P_END].astype(BF16)

    def row(w):
        return pl.BlockSpec((tm, w), lambda i: (i, 0))

    def rshape(w, dt):
        return jax.ShapeDtypeStruct((t, w), dt)

    return pl.pallas_call(
        body, name="mix_proj", grid=(t // tm,),
        in_specs=[row(D_MODEL), _full((1, D_MODEL)), _full((D_MODEL, P_END)), _full((128, GLA_KW)),
                  _full((1, GLA_KW)), row(128), row(128)],
        out_specs=[row(D_MODEL), row(256), row(256), row(512), row(512), row(128), row(256), row(512), row(256),
                   row(256)],
        out_shape=[rshape(D_MODEL, BF16), rshape(256, F32), rshape(256, F32), rshape(512, F32), rshape(512, F32),
                   rshape(128, F32), rshape(256, F32), rshape(512, BF16), rshape(256, BF16), rshape(256, BF16)],
        compiler_params=_cparams(1),
    )(h1, wmixpre, winp, wa2p, bap, cos, sin)


def _scan_rows(x, reverse=False):
    n = x.shape[0]
    row = lax.broadcasted_iota(jnp.int32, x.shape, 0)
    s = 1
    while s < n:
        if reverse:
            x = x + jnp.where(row < n - s, pltpu.roll(x, n - s, 0), 0.0)
        else:
            x = x + jnp.where(row >= s, pltpu.roll(x, s, 0), 0.0)
        s *= 2
    return x


def _gla_cumsum(la, tril_f):
    b = _scan_rows(la)
    row = lax.broadcasted_iota(jnp.int32, b.shape, 0)
    bm = jnp.sum(jnp.where(row == GLA_CHUNK // 2 - 1, b, 0.0), axis=0, keepdims=True)
    bl = jnp.sum(jnp.where(row == GLA_CHUNK - 1, b, 0.0), axis=0, keepdims=True)
    return b, bm, bl


def _gla_decays(la, tril_f):
    b, bm, bl = _gla_cumsum(la, tril_f)
    return jnp.exp(b - bm), jnp.exp(bm - b), jnp.exp(b), jnp.exp(bl - b), jnp.exp(bl)


def _gla_masks():
    c = GLA_CHUNK
    r = lax.broadcasted_iota(jnp.int32, (c, c), 0)
    col = lax.broadcasted_iota(jnp.int32, (c, c), 1)
    r4 = lax.broadcasted_iota(jnp.int32, (GLA_HEADS * c, c), 0) % c
    c4 = lax.broadcasted_iota(jnp.int32, (GLA_HEADS * c, c), 1)
    klane = lax.broadcasted_iota(jnp.int32, (c, GLA_KW), 1) // GLA_DK
    vlane = lax.broadcasted_iota(jnp.int32, (c, GLA_W), 1) // GLA_DV
    srow = lax.broadcasted_iota(jnp.int32, (GLA_W, GLA_KW), 0) // GLA_DV
    scol = lax.broadcasted_iota(jnp.int32, (GLA_W, GLA_KW), 1) // GLA_DK
    return dict(tril_f=(r >= col).astype(F32), triu_f=(r <= col).astype(F32), tril4=r4 >= c4,
                khead=[klane == h for h in range(GLA_HEADS)], vhead=[vlane == h for h in range(GLA_HEADS)],
                diag=srow == scol)


def _stack_heads(x, head_masks):
    return jnp.concatenate([jnp.where(m, x, 0.0) for m in head_masks], axis=0)


def _gla_fwd(gq, gk, gv, la):
    t = gq.shape[0]
    rg = _seq_tile(t)
    nb = t // rg
    ncb = rg // GLA_CHUNK
    c = GLA_CHUNK

    def body(q_ref, k_ref, v_ref, la_ref, o_ref, ss_ref, st_ref):
        @pl.when(pl.program_id(0) == 0)
        def _():
            st_ref[...] = jnp.zeros_like(st_ref)

        mk = _gla_masks()
        st = st_ref[...]
        for ch in range(ncb):
            rows = slice(ch * c, (ch + 1) * c)
            eq, ek, eb, ekl, ebl = _gla_decays(la_ref[rows, :], mk["tril_f"])
            qs = q_ref[rows, :] * (GLA_DK ** -0.5)
            k = k_ref[rows, :]
            v = v_ref[rows, :].astype(BF16)
            ss_ref[ch] = st
            q4 = _stack_heads(qs * eq, mk["khead"]).astype(BF16)
            a4 = jnp.where(mk["tril4"], _dg(q4, (k * ek).astype(BF16), NT), 0.0).astype(BF16)
            r4 = _dot(a4, v)
            intra = jnp.concatenate([r4[h * c:(h + 1) * c, GLA_DV * h:GLA_DV * (h + 1)] for h in range(GLA_HEADS)],
                                    axis=1)
            o_ref[rows, :] = intra + _dg((qs * eb).astype(BF16), st.astype(BF16), NT)
            st = st * ebl + jnp.where(mk["diag"], _dg(v, (k * ekl).astype(BF16), TN), 0.0)
        st_ref[...] = st

    def row(w):
        return pl.BlockSpec((rg, w), lambda i: (i, 0))

    return pl.pallas_call(
        body, name="gla_fwd", grid=(nb,),
        in_specs=[row(256), row(256), row(512), row(256)],
        out_specs=[row(512), pl.BlockSpec((ncb, GLA_W, GLA_KW), lambda i: (i, 0, 0))],
        out_shape=[jax.ShapeDtypeStruct((t, GLA_W), F32), jax.ShapeDtypeStruct((nb * ncb, GLA_W, GLA_KW), F32)],
        scratch_shapes=[pltpu.VMEM((GLA_W, GLA_KW), F32)],
        compiler_params=_cparams(1),
    )(gq, gk, gv, la)


def _gla_bwd(gq, gk, gv, la, ss, do):
    t = gq.shape[0]
    rg = _seq_tile(t)
    nb = t // rg
    ncb = rg // GLA_CHUNK
    c = GLA_CHUNK

    def body(q_ref, k_ref, v_ref, la_ref, ss_ref, do_ref, dq_ref, dk_ref, dv_ref, dla_ref, dst_ref):
        @pl.when(pl.program_id(0) == 0)
        def _():
            dst_ref[...] = jnp.zeros_like(dst_ref)

        mk = _gla_masks()
        last_row = lax.broadcasted_iota(jnp.int32, (c, GLA_KW), 0) == c - 1
        scale = GLA_DK ** -0.5
        dstn = dst_ref[...]
        for ch in reversed(range(ncb)):
            rows = slice(ch * c, (ch + 1) * c)
            eq, ek, eb, ekl, ebl = _gla_decays(la_ref[rows, :], mk["tril_f"])
            qs = q_ref[rows, :] * scale
            k = k_ref[rows, :]
            qt, kt, qh, kh = qs * eq, k * ek, qs * eb, k * ekl
            ktb, khb, qhb = kt.astype(BF16), kh.astype(BF16), qh.astype(BF16)
            v = v_ref[rows, :].astype(BF16)
            do_f = do_ref[rows, :]
            dob = do_f.astype(BF16)
            st = ss_ref[ch]
            stb = st.astype(BF16)
            dstb = dstn.astype(BF16)
            q4 = _stack_heads(qt, mk["khead"]).astype(BF16)
            do4 = _stack_heads(do_f, mk["vhead"]).astype(BF16)
            a4 = jnp.where(mk["tril4"], _dg(q4, ktb, NT), 0.0).astype(BF16)
            da4 = jnp.where(mk["tril4"], _dg(do4, v, NT), 0.0).astype(BF16)
            dv_ref[rows, :] = _dg(a4, do4, TN) + _dg(khb, dstb, NT)
            dq4 = _dot(da4, ktb)
            dqt = jnp.zeros((c, GLA_KW), F32)
            for h in range(GLA_HEADS):
                dqt = dqt + jnp.where(mk["khead"][h], dq4[h * c:(h + 1) * c], 0.0)
            dkt = _dg(da4, q4, TN)
            dqh = _dot(dob, stb)
            dkh = _dot(v, dstb)
            dbl = jnp.sum(dstn * st, axis=0, keepdims=True)
            dstn = dstn * ebl + jnp.where(mk["diag"], _dg(dob, qhb, TN), 0.0)
            dq_ref[rows, :] = scale * (dqt * eq + dqh * eb)
            dk_ref[rows, :] = dkt * ek + dkh * ekl
            dkk = dkh * kh
            db = dqt * qt - dkt * kt + dqh * qh - dkk
            db = db + jnp.where(last_row, jnp.sum(dkk, axis=0, keepdims=True) + ebl * dbl, 0.0)
            dla_ref[rows, :] = _scan_rows(db, reverse=True)
        dst_ref[...] = dstn

    def row(w):
        return pl.BlockSpec((rg, w), lambda i: (nb - 1 - i, 0))

    def rshape(w):
        return jax.ShapeDtypeStruct((t, w), F32)

    return pl.pallas_call(
        body, name="gla_bwd", grid=(nb,),
        in_specs=[row(256), row(256), row(512), row(256),
                  pl.BlockSpec((ncb, GLA_W, GLA_KW), lambda i: (nb - 1 - i, 0, 0)), row(512)],
        out_specs=[row(256), row(256), row(512), row(256)],
        out_shape=[rshape(256), rshape(256), rshape(512), rshape(256)],
        scratch_shapes=[pltpu.VMEM((GLA_W, GLA_KW), F32)],
        compiler_params=_cparams(1),
    )(gq, gk, gv, la, ss, do)


SWA_G = SWA_QH // SWA_KVH


def _swa_mask(n):
    r = lax.broadcasted_iota(jnp.int32, (SWA_G * BLK, 3 * BLK), 0) % BLK
    c = lax.broadcasted_iota(jnp.int32, (SWA_G * BLK, 3 * BLK), 1)
    seg = c // BLK
    cc = c % BLK
    qpos = n * BLK + r - PAD
    kpos = jnp.where(seg == 0, (n - 1) * BLK, jnp.where(seg == 1, n * BLK, 0)) + cc - PAD
    band = (seg < 2) & (kpos >= N_META) & (kpos <= qpos) & (qpos - kpos < WINDOW)
    meta = (seg == 2) & (kpos >= 0) & (kpos < N_META) & (kpos <= qpos)
    return band | meta


def _swa_stack(ref, kh, lo, dtype):
    parts = []
    for g in range(2):
        pair = ref[:, 128 * (2 * kh + g):128 * (2 * kh + g + 1)]
        zero = jnp.zeros_like(pair)
        parts += [jnp.where(lo, pair, zero), jnp.where(lo, zero, pair)]
    return jnp.concatenate(parts, axis=0).astype(dtype)


def _swa_unstack(x4, lo):
    return [jnp.where(lo, x4[2 * g * BLK:(2 * g + 1) * BLK], x4[(2 * g + 1) * BLK:(2 * g + 2) * BLK])
            for g in range(2)]


def _swa_sink_col(sink_ref, kh):
    blk = lax.broadcasted_iota(jnp.int32, (SWA_G * BLK, 1), 0) // BLK
    col = jnp.full((SWA_G * BLK, 1), sink_ref[SWA_G * kh + SWA_G - 1], F32)
    for e in reversed(range(SWA_G - 1)):
        col = jnp.where(blk == e, sink_ref[SWA_G * kh + e], col)
    return col


def _swa_probs(q4, kall, mask, sink):
    s = _dg(q4, kall, NT) * (SWA_HD ** -0.5)
    s = jnp.where(mask, s, NEG_INF)
    m = jnp.maximum(jnp.max(s, axis=-1, keepdims=True), sink)
    p = jnp.exp(s - m)
    es = jnp.exp(sink - m)
    inv = 1.0 / (jnp.sum(p, axis=-1, keepdims=True) + es)
    return p * inv, es * inv


def _swa_fwd(sinks, sq, sk, sv):
    t = sq.shape[0]
    nb = t // BLK

    def body(sink_ref, q_ref, kp_ref, kc_ref, km_ref, vp_ref, vc_ref, vm_ref, o_ref):
        n = pl.program_id(0)
        mask = _swa_mask(n)
        lo = lax.broadcasted_iota(jnp.int32, (BLK, 128), 1) < 64
        for kh in range(SWA_KVH):
            ls = slice(128 * kh, 128 * (kh + 1))
            kall = jnp.concatenate([kp_ref[:, ls], kc_ref[:, ls], km_ref[:, ls]], axis=0)
            vall = jnp.concatenate([vp_ref[:, ls], vc_ref[:, ls], vm_ref[:, ls]], axis=0)
            p, _ = _swa_probs(_swa_stack(q_ref, kh, lo, BF16), kall, mask, _swa_sink_col(sink_ref, kh))
            for g, pair in enumerate(_swa_unstack(_dot(p.astype(BF16), vall), lo)):
                o_ref[:, 128 * (2 * kh + g):128 * (2 * kh + g + 1)] = pair

    cur = lambda w: pl.BlockSpec((BLK, w), lambda i: (i, 0))
    prev = lambda w: pl.BlockSpec((BLK, w), lambda i: (jnp.maximum(i - 1, 0), 0))
    first = lambda w: pl.BlockSpec((BLK, w), lambda i: (0, 0))
    return pl.pallas_call(
        body, name="swa_fwd", grid=(nb,),
        in_specs=[pl.BlockSpec(memory_space=pltpu.SMEM), cur(512), prev(256), cur(256), first(256),
                  prev(256), cur(256), first(256)],
        out_specs=cur(512),
        out_shape=jax.ShapeDtypeStruct((t, SWA_W), F32),
        compiler_params=_cparams(1),
    )(sinks, sq, sk, sk, sk, sv, sv, sv)


def _swa_bwd(sinks, sq, sk, sv, o, do, hook=None):
    t = sq.shape[0]
    nb = t // BLK

    def body(sink_ref, q_ref, kp_ref, kc_ref, km_ref, vp_ref, vc_ref, vm_ref, o_ref, do_ref,
             dq_ref, dk_ref, dv_ref, dkm_ref, dvm_ref, dsink_ref, ck_ref, cv_ref):
        n = pl.program_id(0)

        @pl.when(n == 0)
        def _():
            ck_ref[...] = jnp.zeros_like(ck_ref)
            cv_ref[...] = jnp.zeros_like(cv_ref)
            dkm_ref[...] = jnp.zeros_like(dkm_ref)
            dvm_ref[...] = jnp.zeros_like(dvm_ref)
            dsink_ref[...] = jnp.zeros_like(dsink_ref)

        @pl.when(n == nb)
        def _():
            dk_ref[...] = ck_ref[...]
            dv_ref[...] = cv_ref[...]

        @pl.when(n < nb)
        def _():
            mask = _swa_mask(n)
            lo = lax.broadcasted_iota(jnp.int32, (BLK, 128), 1) < 64
            scale = SWA_HD ** -0.5
            for kh in range(SWA_KVH):
                ls = slice(128 * kh, 128 * (kh + 1))
                kall = jnp.concatenate([kp_ref[:, ls], kc_ref[:, ls], km_ref[:, ls]], axis=0)
                vall = jnp.concatenate([vp_ref[:, ls], vc_ref[:, ls], vm_ref[:, ls]], axis=0)
                q4 = _swa_stack(q_ref, kh, lo, BF16)
                do4 = _swa_stack(do_ref, kh, lo, F32)
                p, psink = _swa_probs(q4, kall, mask, _swa_sink_col(sink_ref, kh))
                delta = jnp.sum(do4 * _swa_stack(o_ref, kh, lo, F32), axis=-1, keepdims=True)
                do4b = do4.astype(BF16)
                ds = (p * (_dg(do4b, vall, NT) - delta) * scale).astype(BF16)
                for g, pair in enumerate(_swa_unstack(_dot(ds, kall), lo)):
                    dq_ref[:, 128 * (2 * kh + g):128 * (2 * kh + g + 1)] = pair
                dkall = _dg(ds, q4, TN)
                dvall = _dg(p.astype(BF16), do4b, TN)
                dsk = psink * delta
                for e in range(SWA_G):
                    h = SWA_G * kh + e
                    dsink_ref[h:h + 1, :] += jnp.broadcast_to(
                        -jnp.sum(dsk[e * BLK:(e + 1) * BLK], axis=0, keepdims=True), (1, 128))
                dk_ref[:, ls] = ck_ref[:, ls] + dkall[0:BLK]
                dv_ref[:, ls] = cv_ref[:, ls] + dvall[0:BLK]
                ck_ref[:, ls] = dkall[BLK:2 * BLK]
                cv_ref[:, ls] = dvall[BLK:2 * BLK]
                dkm_ref[:, ls] += dkall[2 * BLK:3 * BLK]
                dvm_ref[:, ls] += dvall[2 * BLK:3 * BLK]

    cur = lambda w: pl.BlockSpec((BLK, w), lambda i: (jnp.minimum(i, nb - 1), 0))
    prev = lambda w: pl.BlockSpec((BLK, w), lambda i: (jnp.maximum(i - 1, 0), 0))
    first = lambda w: pl.BlockSpec((BLK, w), lambda i: (0, 0))
    return _pallas(
        body, name="swa_bwd", grid=(nb + 1,),
        in_specs=[pl.BlockSpec(memory_space=pltpu.SMEM), cur(512), prev(256), cur(256), first(256),
                  prev(256), cur(256), first(256), cur(512), cur(512)],
        out_specs=[cur(512), prev(256), prev(256), first(256), first(256), _full((SWA_QH, 128))],
        out_shape=[jax.ShapeDtypeStruct((t, SWA_W), F32), jax.ShapeDtypeStruct((t, 256), F32),
                   jax.ShapeDtypeStruct((t, 256), F32), jax.ShapeDtypeStruct((BLK, 256), F32),
                   jax.ShapeDtypeStruct((BLK, 256), F32), jax.ShapeDtypeStruct((SWA_QH, 128), F32)],
        scratch_shapes=[pltpu.VMEM((BLK, 256), F32), pltpu.VMEM((BLK, 256), F32)],
        args=(sinks, sq, sk, sk, sk, sv, sv, sv, o, do), hook=hook)


def _mix_out(h1, ogla, gg, oswa, wgn, wsn, wout, wpost):
    t = h1.shape[0]
    tm = _row_tile(t)

    def body(h_ref, og_ref, gg_ref, os_ref, wgn_ref, wsn_ref, wout_ref, wpost_ref, h2_ref, cat_ref, m_ref):
        parts = []
        for h in range(GLA_HEADS):
            ls = slice(GLA_DV * h, GLA_DV * (h + 1))
            y, _, _ = _rms(og_ref[:, ls], wgn_ref[...])
            g = gg_ref[:, ls]
            parts.append(y * (g * _sigmoid(g)))
        ys, _, _ = _rms(os_ref[...], wsn_ref[...])
        cat = jnp.concatenate(parts + [ys], axis=1).astype(BF16)
        cat_ref[...] = cat
        m = _dot(cat, wout_ref[...])
        m_ref[...] = m
        y, _, _ = _rms(m, wpost_ref[...])
        h2_ref[...] = h_ref[...] + y

    def row(w):
        return pl.BlockSpec((tm, w), lambda i: (i, 0))

    return pl.pallas_call(
        body, name="mix_out", grid=(t // tm,),
        in_specs=[row(D_MODEL), row(512), row(512), row(512), _full((1, GLA_DV)), _full((1, SWA_W)),
                  _full((D_MODEL, D_MODEL)), _full((1, D_MODEL))],
        out_specs=[row(D_MODEL), row(D_MODEL), row(D_MODEL)],
        out_shape=[jax.ShapeDtypeStruct((t, D_MODEL), F32), jax.ShapeDtypeStruct((t, D_MODEL), BF16),
                   jax.ShapeDtypeStruct((t, D_MODEL), F32)],
        compiler_params=_cparams(1),
    )(h1, ogla, gg, oswa, wgn, wsn, wout, wpost)


def _mix_out_bwd(dh2, m, ogla, gg, oswa, wgn, wsn, wout, wpost, hook=None):
    t = dh2.shape[0]
    tm = _row_tile(t)

    def body(dh_ref, m_ref, og_ref, gg_ref, os_ref, wgn_ref, wsn_ref, wout_ref, wpost_ref,
             dog_ref, dgg_ref, dos_ref, dm_ref, dwpost_ref, dwgn_ref, dwsn_ref):
        @pl.when(pl.program_id(0) == 0)
        def _():
            dwpost_ref[...] = jnp.zeros_like(dwpost_ref)
            dwgn_ref[...] = jnp.zeros_like(dwgn_ref)
            dwsn_ref[...] = jnp.zeros_like(dwsn_ref)

        wpost = wpost_ref[...]
        _, mh, r = _rms(m_ref[...], wpost)
        dm, dw = _rms_bwd(mh, r, wpost, dh_ref[...])
        dwpost_ref[...] += dw
        dmb = dm.astype(BF16)
        dm_ref[...] = dmb
        dcat = _dg(dmb, wout_ref[...], NT)
        wgn = wgn_ref[...]
        for h in range(GLA_HEADS):
            ls = slice(GLA_DV * h, GLA_DV * (h + 1))
            dog = dcat[:, ls]
            g = gg_ref[:, ls]
            sg = _sigmoid(g)
            y, xh, r = _rms(og_ref[:, ls], wgn)
            dgg_ref[:, ls] = dog * y * (sg * (1.0 + g * (1.0 - sg)))
            dx, dw = _rms_bwd(xh, r, wgn, dog * (g * sg))
            dog_ref[:, ls] = dx
            dwgn_ref[...] += dw
        wsn = wsn_ref[...]
        _, xh, r = _rms(os_ref[...], wsn)
        dx, dw = _rms_bwd(xh, r, wsn, dcat[:, GLA_W:])
        dos_ref[...] = dx
        dwsn_ref[...] += dw

    def row(w):
        return pl.BlockSpec((tm, w), lambda i: (i, 0))

    def rshape(w, dt=F32):
        return jax.ShapeDtypeStruct((t, w), dt)

    return _pallas(
        body, name="mix_out_bwd", grid=(t // tm,),
        in_specs=[row(D_MODEL), row(D_MODEL), row(512), row(512), row(512), _full((1, GLA_DV)), _full((1, SWA_W)),
                  _full((D_MODEL, D_MODEL)), _full((1, D_MODEL))],
        out_specs=[row(512), row(512), row(512), row(D_MODEL), _full((1, D_MODEL)), _full((1, GLA_DV)),
                   _full((1, SWA_W))],
        out_shape=[rshape(512), rshape(512), rshape(512), rshape(D_MODEL, BF16),
                   jax.ShapeDtypeStruct((1, D_MODEL), F32), jax.ShapeDtypeStruct((1, GLA_DV), F32),
                   jax.ShapeDtypeStruct((1, SWA_W), F32)],
        args=(dh2, m, ogla, gg, oswa, wgn, wsn, wout, wpost), hook=hook)


def _mix_in_bwd(dh2, h1, wmixpre, winp, wa2p, bap, cos, sin, ga, dgq, dgk, dgv, dgg, dla, dsq, dsk, dsv, dkm, dvm):
    t = h1.shape[0]
    tm = _row_tile(t)

    def body(dh2_ref, h_ref, w_ref, win_ref, wa2_ref, ba_ref, cos_ref, sin_ref, ga_ref, dgq_ref, dgk_ref, dgv_ref,
             dgg_ref, dla_ref, dsq_ref, dsk_ref, dsv_ref, dkm_ref, dvm_ref,
             dh1_ref, dproj_ref, dw_ref, dwa2_ref, dba_ref):
        i = pl.program_id(0)

        @pl.when(i == 0)
        def _():
            dw_ref[...] = jnp.zeros_like(dw_ref)
            dwa2_ref[...] = jnp.zeros_like(dwa2_ref)
            dba_ref[...] = jnp.zeros_like(dba_ref)

        first = (i == 0).astype(F32)
        c = cos_ref[...]
        s = -sin_ref[...]
        fh = _first_half_mask(tm)
        dproj_ref[:, P_GQ:P_GK] = dgq_ref[...].astype(BF16)
        dproj_ref[:, P_GK:P_GV] = dgk_ref[...].astype(BF16)
        dproj_ref[:, P_GV:P_GG] = dgv_ref[...].astype(BF16)
        dproj_ref[:, P_GG:P_GA] = dgg_ref[...].astype(BF16)
        gab = ga_ref[...].astype(BF16)
        z = _dot(gab, wa2_ref[...]) + ba_ref[...]
        row_id = i * tm + lax.broadcasted_iota(jnp.int32, (tm, 1), 0)
        dz = jnp.where(row_id >= PAD, dla_ref[...] * (1.0 / GLA_TAU) * (1.0 - _sigmoid(z)), 0.0)
        dzb = dz.astype(BF16)
        dba_ref[...] += jnp.sum(dz, axis=0, keepdims=True)
        dwa2_ref[...] += _dg(gab, dzb, TN)
        dproj_ref[:, P_GA:P_SQ] = _dg(dzb, wa2_ref[...], NT).astype(BF16)
        for k in range(4):
            dy = dsq_ref[:, 128 * k:128 * (k + 1)]
            dproj_ref[:, P_SQ + 128 * k:P_SQ + 128 * (k + 1)] = (dy * c + _rot_half(dy, fh) * s).astype(BF16)
        for k in range(2):
            ls = slice(128 * k, 128 * (k + 1))
            dy = dsk_ref[:, ls]
            dy = jnp.concatenate([dy[:BLK] + first * dkm_ref[:, ls], dy[BLK:]], axis=0) if tm > BLK else (
                dy + first * dkm_ref[:, ls])
            dproj_ref[:, P_SK + 128 * k:P_SK + 128 * (k + 1)] = (dy * c + _rot_half(dy, fh) * s).astype(BF16)
            dv = dsv_ref[:, ls]
            dv = jnp.concatenate([dv[:BLK] + first * dvm_ref[:, ls], dv[BLK:]], axis=0) if tm > BLK else (
                dv + first * dvm_ref[:, ls])
            dproj_ref[:, P_SV + 128 * k:P_SV + 128 * (k + 1)] = dv.astype(BF16)
        dn = _dg(dproj_ref[...], win_ref[...], NT)
        w = w_ref[...]
        _, hh, r = _rms(h_ref[...], w)
        dx, dw = _rms_bwd(hh, r, w, dn)
        dw_ref[...] += dw
        dh1_ref[...] = dh2_ref[...] + dx

    def row(w):
        return pl.BlockSpec((tm, w), lambda i: (i, 0))

    return pl.pallas_call(
        body, name="mix_in_bwd", grid=(t // tm,),
        in_specs=[row(D_MODEL), row(D_MODEL), _full((1, D_MODEL)), _full((D_MODEL, P_END)), _full((128, GLA_KW)),
                  _full((1, GLA_KW)), row(128), row(128), row(128), row(256), row(256), row(512), row(512), row(256),
                  row(512), row(256), row(256), _full((BLK, 256)), _full((BLK, 256))],
        out_specs=[row(D_MODEL), row(P_END), _full((1, D_MODEL)), _full((128, GLA_KW)), _full((1, GLA_KW))],
        out_shape=[jax.ShapeDtypeStruct((t, D_MODEL), F32), jax.ShapeDtypeStruct((t, P_END), BF16),
                   jax.ShapeDtypeStruct((1, D_MODEL), F32), jax.ShapeDtypeStruct((128, GLA_KW), F32),
                   jax.ShapeDtypeStruct((1, GLA_KW), F32)],
        compiler_params=_cparams(1),
    )(dh2, h1, wmixpre, winp, wa2p, bap, cos, sin, ga, dgq, dgk, dgv, dgg, dla, dsq, dsk, dsv, dkm, dvm)


def _adamw_update(w, g, m, v):
    m = ADAM_B1 * m + (1.0 - ADAM_B1) * g
    v = ADAM_B2 * v + (1.0 - ADAM_B2) * (g * g)
    m_hat = m / (1.0 - ADAM_B1 ** ADAM_STEP)
    v_hat = v / (1.0 - ADAM_B2 ** ADAM_STEP)
    return -ADAM_LR * (m_hat / (jnp.sqrt(v_hat) + ADAM_EPS) + ADAM_WD * w), m, v


def _adamw(w, g, m, v):
    r, c = w.shape
    tr = _div_tile(r)

    def body(w_ref, g_ref, m_ref, v_ref, d_ref, nm_ref, nv_ref):
        d_ref[...], nm_ref[...], nv_ref[...] = _adamw_update(w_ref[...], g_ref[...], m_ref[...], v_ref[...])

    spec = pl.BlockSpec((tr, c), lambda i: (i, 0))
    shape = jax.ShapeDtypeStruct((r, c), F32)
    return pl.pallas_call(
        body, name="adamw", grid=(r // tr,), in_specs=[spec] * 4, out_specs=[spec] * 3, out_shape=[shape] * 3,
        compiler_params=_cparams(1),
    )(w, g, m, v)


def _adamw_halves(w, g_mine, g_other, m, v, c_idx, row0=0):
    r, c = w.shape
    h = g_mine.shape[0]
    tr = _div_tile(math.gcd(r, h))
    nth = h // tr
    t0 = row0 // tr
    assert t0 * tr == row0

    def body(c_ref, w_ref, gm_ref, go_ref, m_ref, v_ref, g_ref, d_ref, nm_ref, nv_ref):
        hh = (t0 + pl.program_id(0)) // nth
        g = jnp.where(hh == c_ref[0], gm_ref[...], go_ref[...])
        g_ref[...] = g
        d_ref[...], nm_ref[...], nv_ref[...] = _adamw_update(w_ref[...], g, m_ref[...], v_ref[...])

    spec = pl.BlockSpec((tr, c), lambda i, c_ref: (i, 0))
    gspec = pl.BlockSpec((tr, c), lambda i, c_ref: ((t0 + i) % nth, 0))
    shape = jax.ShapeDtypeStruct((r, c), F32)
    return pl.pallas_call(
        body, name="adamw_halves",
        grid_spec=pltpu.PrefetchScalarGridSpec(
            num_scalar_prefetch=1, grid=(r // tr,), in_specs=[spec, gspec, gspec, spec, spec], out_specs=[spec] * 4),
        out_shape=[shape] * 4, compiler_params=_cparams(1),
    )(c_idx, w, g_mine, g_other, m, v)


def _place():
    x, y, c = lax.axis_index("x"), lax.axis_index("y"), lax.axis_index("c")
    chips = [(1 - x, y), (x, 1 - y), (1 - x, 1 - y)]
    return x, y, c, chips


def _remote(send_sem, recv_sem, src, dst, to):
    return pltpu.make_async_remote_copy(src_ref=src, dst_ref=dst, send_sem=send_sem, recv_sem=recv_sem,
                                        device_id=to, device_id_type=MESH)


def _half(ref_rows, c):
    h = ref_rows // 2
    return pl.ds(pl.multiple_of(c * h, 8), h)


def _own_slot(shard, q):
    return lax.dynamic_update_slice(jnp.zeros((N_CHIPS,) + shard.shape, shard.dtype), shard[None], (q, 0, 0))


class _GatherChips:
    has_mid = True

    def __init__(self, bufs):
        n = len(bufs)
        self.inputs = list(bufs)
        self.out_shape = [jax.ShapeDtypeStruct(b.shape, b.dtype) for b in bufs]
        self.aliases = [(t, t) for t in range(n)]
        self.scratch = [pltpu.SemaphoreType.DMA((n, 6)), pltpu.SemaphoreType.DMA((n, 6))]

    def start(self, ins, outs, scr):
        send, recv = scr
        x, y, c, chips = _place()
        q = 2 * x + y
        for t, (i_ref, o_ref) in enumerate(zip(ins, outs)):
            rows = _half(i_ref.shape[1], c)
            for j, (cx, cy) in enumerate(chips):
                _remote(send.at[t, j], recv.at[t, j], i_ref.at[q, rows], o_ref.at[q, rows], (cx, cy, c)).start()

    def mid(self, ins, outs, scr):
        send, recv = scr
        x, y, c, chips = _place()
        for t, o_ref in enumerate(outs):
            rows = _half(o_ref.shape[1], c)
            for j, (cx, cy) in enumerate(chips):
                slot = o_ref.at[2 * cx + cy, rows]
                _remote(send.at[t, j], recv.at[t, j], slot, slot, (cx, cy, c)).wait_recv()
                _remote(send.at[t, 3 + j], recv.at[t, 3 + j], slot, slot, (x, y, 1 - c)).start()

    def finish(self, ins, outs, scr):
        send, recv = scr
        x, y, c, chips = _place()
        for t, o_ref in enumerate(outs):
            mine, other = _half(o_ref.shape[1], c), _half(o_ref.shape[1], 1 - c)
            for j, (cx, cy) in enumerate(chips):
                slot = o_ref.at[2 * cx + cy, other]
                _remote(send.at[t, 3 + j], recv.at[t, 3 + j], slot, slot, (x, y, 1 - c)).wait_recv()
            for j, (cx, cy) in enumerate(chips):
                sent = o_ref.at[2 * cx + cy, mine]
                _remote(send.at[t, j], recv.at[t, j], sent, sent, (cx, cy, c)).wait_send()
                _remote(send.at[t, 3 + j], recv.at[t, 3 + j], sent, sent, (x, y, 1 - c)).wait_send()


class _PairExchange:
    has_mid = False
    aliases = ()

    def __init__(self, arrs):
        n = len(arrs)
        self.inputs = list(arrs)
        self.out_shape = [jax.ShapeDtypeStruct((a.shape[0], a.shape[1] // 2, a.shape[2]), a.dtype) for a in arrs]
        self.scratch = [pltpu.SemaphoreType.DMA((n,)), pltpu.SemaphoreType.DMA((n,))]

    def _copies(self, ins, outs, scr):
        send, recv = scr
        x, y, c, _ = _place()
        return [_remote(send.at[t], recv.at[t], i_ref.at[:, _half(i_ref.shape[1], 1 - c)], o_ref, (x, y, 1 - c))
                for t, (i_ref, o_ref) in enumerate(zip(ins, outs))]

    def start(self, ins, outs, scr):
        for cp in self._copies(ins, outs, scr):
            cp.start()

    def finish(self, ins, outs, scr):
        for cp in self._copies(ins, outs, scr):
            cp.wait()


class _ChipScatter:
    has_mid = False
    aliases = ()

    def __init__(self, arrs):
        n = len(arrs)
        self.inputs = list(arrs)
        self.out_shape = [jax.ShapeDtypeStruct((3,) + a.shape[1:], a.dtype) for a in arrs]
        self.scratch = [pltpu.SemaphoreType.DMA((n, 3)), pltpu.SemaphoreType.DMA((n, 3))]

    def _copies(self, ins, outs, scr):
        send, recv = scr
        x, y, c, chips = _place()
        return [_remote(send.at[t, j], recv.at[t, j], i_ref.at[2 * cx + cy], o_ref.at[j], (cx, cy, c))
                for t, (i_ref, o_ref) in enumerate(zip(ins, outs)) for j, (cx, cy) in enumerate(chips)]

    def start(self, ins, outs, scr):
        for cp in self._copies(ins, outs, scr):
            cp.start()

    def finish(self, ins, outs, scr):
        for cp in self._copies(ins, outs, scr):
            cp.wait()


class _PairShare:
    has_mid = False
    aliases = ()

    def __init__(self, arrs):
        n = len(arrs)
        self.inputs = list(arrs)
        self.out_shape = [jax.ShapeDtypeStruct(a.shape, a.dtype) for a in arrs]
        self.scratch = [pltpu.SemaphoreType.DMA((n,)), pltpu.SemaphoreType.DMA((n,))]

    def _copies(self, ins, outs, scr):
        send, recv = scr
        x, y, c, _ = _place()
        return [_remote(send.at[t], recv.at[t], i_ref, o_ref, (x, y, 1 - c))
                for t, (i_ref, o_ref) in enumerate(zip(ins, outs))]

    def start(self, ins, outs, scr):
        for cp in self._copies(ins, outs, scr):
            cp.start()

    def finish(self, ins, outs, scr):
        for cp in self._copies(ins, outs, scr):
            cp.wait()


def _comm_call(hook, name):
    n_in, n_out = len(hook.inputs), len(hook.out_shape)

    def body(*refs):
        ins, outs, scr = refs[:n_in], refs[n_in:n_in + n_out], refs[n_in + n_out:]
        hook.start(ins, outs, scr)
        if hook.has_mid:
            hook.mid(ins, outs, scr)
        hook.finish(ins, outs, scr)

    return pl.pallas_call(body, name=name, in_specs=[ANY] * n_in, out_specs=[ANY] * n_out,
                          out_shape=list(hook.out_shape), scratch_shapes=list(hook.scratch),
                          input_output_aliases=dict(hook.aliases))(*hook.inputs)


def _all_gather_devices(vecs):
    n = len(vecs)

    def body(*refs):
        x_refs, out_refs = refs[:n], refs[n:2 * n]
        send_sems, recv_sems, local_sems = refs[2 * n:]
        x, y, c, chips = _place()
        me, sibling = (x, y, c), (x, y, 1 - c)
        waits = []
        for t, (x_ref, out_ref) in enumerate(zip(x_refs, out_refs)):
            def slot(px, py, pc, out_ref=out_ref):
                return out_ref.at[4 * px + 2 * py + pc]

            def copy(k, block, to, src=None, t=t, slot=slot):
                return pltpu.make_async_remote_copy(
                    src_ref=slot(*block) if src is None else src, dst_ref=slot(*block), send_sem=send_sems.at[t, k],
                    recv_sem=recv_sems.at[t, k], device_id=to, device_id_type=MESH)

            mine = pltpu.make_async_copy(x_ref, slot(*me), local_sems.at[t])
            mine.start()
            first = [copy(0, me, sibling, src=x_ref)]
            first += [copy(1 + j, me, (*chip, c), src=x_ref) for j, chip in enumerate(chips)]
            for cp in first:
                cp.start()
            waits.append((copy, mine, first))
        for copy, mine, first in waits:
            passed = [copy(4 + j, (*chip, c), sibling) for j, chip in enumerate(chips)]
            for j, chip in enumerate(chips):
                copy(1 + j, (*chip, c), me).wait_recv()
                passed[j].start()
            copy(0, sibling, me).wait_recv()
            for j, chip in enumerate(chips):
                copy(4 + j, (*chip, 1 - c), me).wait_recv()
            for cp in first + passed:
                cp.wait_send()
            mine.wait()

    vmem = pl.BlockSpec(memory_space=pltpu.VMEM)
    return pl.pallas_call(
        body, name="all_gather_devices", in_specs=[vmem] * n, out_specs=[vmem] * n,
        out_shape=[jax.ShapeDtypeStruct((N_DEV,) + v.shape, v.dtype) for v in vecs],
        scratch_shapes=[pltpu.SemaphoreType.DMA((n, 7)), pltpu.SemaphoreType.DMA((n, 7)),
                        pltpu.SemaphoreType.DMA((n,))],
    )(*vecs)


def _pair_sum(g, other, c_idx):
    nq, r, w = g.shape
    h = r // 2
    tr = _div_tile(h)
    nt = h // tr

    def body(c_ref, g_ref, o_ref, s_ref):
        s_ref[...] = (g_ref[...].astype(F32) + o_ref[...].astype(F32)).astype(s_ref.dtype)

    return pl.pallas_call(
        body, name="pair_sum",
        grid_spec=pltpu.PrefetchScalarGridSpec(
            num_scalar_prefetch=1, grid=(nq, nt),
            in_specs=[pl.BlockSpec((None, tr, w), lambda k, i, c_ref: (k, c_ref[0] * nt + i, 0)),
                      pl.BlockSpec((None, tr, w), lambda k, i, c_ref: (k, i, 0))],
            out_specs=pl.BlockSpec((None, tr, w), lambda k, i, c_ref: (k, i, 0))),
        out_shape=jax.ShapeDtypeStruct((nq, h, w), g.dtype),
        compiler_params=_cparams(2),
    )(c_idx, g, other)


def _chip_sum(s, others, q_idx):
    _, h, w = s.shape
    tr = _div_tile(h)

    def body(q_ref, s_ref, o_ref, out_ref):
        out_ref[...] = ((s_ref[...].astype(F32) + o_ref[0].astype(F32)) + o_ref[1].astype(F32)) + o_ref[2].astype(F32)

    return pl.pallas_call(
        body, name="chip_sum",
        grid_spec=pltpu.PrefetchScalarGridSpec(
            num_scalar_prefetch=1, grid=(h // tr,),
            in_specs=[pl.BlockSpec((None, tr, w), lambda i, q_ref: (q_ref[0], i, 0)),
                      pl.BlockSpec((3, tr, w), lambda i, q_ref: (0, i, 0))],
            out_specs=pl.BlockSpec((tr, w), lambda i, q_ref: (i, 0))),
        out_shape=jax.ShapeDtypeStruct((h, w), F32),
        compiler_params=_cparams(1),
    )(q_idx, s, others)


def _small_update(q_idx, parts, ws, ms, vs, col_block):
    n = len(parts)
    has_w = [w is not None for w in ws]

    def body(q_ref, *refs):
        pos = 0
        ins = []
        for t in range(n):
            k = 4 if has_w[t] else 1
            ins.append(refs[pos:pos + k])
            pos += k
        outs = refs[pos:]
        opos = 0
        for t in range(n):
            p_ref = ins[t][0]
            g = p_ref[0]
            for s in range(1, p_ref.shape[0]):
                g = g + p_ref[s]
            if has_w[t]:
                _, w_ref, m_ref, v_ref = ins[t]
                g_ref, d_ref, nm_ref, nv_ref = outs[opos:opos + 4]
                opos += 4
                g_ref[...] = g
                d_ref[...], nm_ref[...], nv_ref[...] = _adamw_update(w_ref[...], g, m_ref[...], v_ref[...])
            else:
                outs[opos][...] = g
                opos += 1

    def whole(shape):
        nd = len(shape)
        return pl.BlockSpec(shape, lambda i, q_ref: (0,) * nd)

    in_specs, out_specs, out_shape, args = [], [], [], []
    for t in range(n):
        k, r, wf = parts[t].shape
        if col_block[t]:
            w = wf // N_CHIPS
            in_specs.append(pl.BlockSpec((k, r, w), lambda i, q_ref: (0, 0, q_ref[0])))
        else:
            w = wf
            in_specs.append(whole((k, r, wf)))
        args.append(parts[t])
        if has_w[t]:
            assert ws[t].shape == (r, w), (ws[t].shape, r, w)
            in_specs += [whole((r, w))] * 3
            args += [ws[t], ms[t], vs[t]]
            out_specs += [whole((r, w))] * 4
            out_shape += [jax.ShapeDtypeStruct((r, w), F32)] * 4
        else:
            out_specs.append(whole((r, w)))
            out_shape.append(jax.ShapeDtypeStruct((r, w), F32))
    return pl.pallas_call(
        body, name="small_update",
        grid_spec=pltpu.PrefetchScalarGridSpec(num_scalar_prefetch=1, grid=(1,), in_specs=in_specs,
                                               out_specs=out_specs),
        out_shape=out_shape, compiler_params=_cparams(1),
    )(q_idx, *args)


def _pack_win(w_in):
    o = np.cumsum((0,) + IN_SPLITS)
    gq, gk, gv, gg, ga, sq, sk, sv = [w_in[:, o[i]:o[i + 1]] for i in range(8)]
    z = jnp.zeros((w_in.shape[0], 128 - GLA_RANK), w_in.dtype)
    dup = lambda a: jnp.concatenate([a[:, :64], a[:, :64], a[:, 64:], a[:, 64:]], axis=1)
    return jnp.concatenate([gq, gk, gv, gg, ga, z, sq, dup(sk), dup(sv)], axis=1)


def _unpack_dwin(d):
    und = lambda a: jnp.concatenate([a[:, 0:64] + a[:, 64:128], a[:, 128:192] + a[:, 192:256]], axis=1)
    return jnp.concatenate([d[:, :P_GA], d[:, P_GA:P_GA + GLA_RANK], d[:, P_SQ:P_SK], und(d[:, P_SK:P_SV]),
                            und(d[:, P_SV:P_END])], axis=1)


def _local_step(x, target, meta, p):
    s = x.shape[0]
    t = s + BLK
    h0 = jnp.concatenate([jnp.zeros((PAD, D_MODEL), F32), meta, x], axis=0)
    cos, sin = _rope_tables(t)

    h1, n1, g1, u1, a1, f1 = _ffn_fwd(h0, p["ffn1_pre_norm"], p["ffn1_w_gate"], p["ffn1_w_up"], p["ffn1_w_down"],
                                      p["ffn1_post_norm"])
    n2, gq, gk, gv, gg, ga, la, sq, sk, sv = _mix_proj(h1, p["mix_pre_norm"], p["w_in"], p["gla_w_a2"], p["gla_b_a"],
                                                       cos, sin)
    ogla, ss = _gla_fwd(gq, gk, gv, la)
    oswa = _swa_fwd(p["swa_sinks"], sq, sk, sv)
    h2, cat, m = _mix_out(h1, ogla, gg, oswa, p["gla_out_norm"], p["swa_out_norm"], p["w_out"], p["mix_post_norm"])
    dy, n3, g3, u3, a3, f3, sse = _ffn_fwd(h2, p["ffn2_pre_norm"], p["ffn2_w_gate"], p["ffn2_w_up"],
                                           p["ffn2_w_down"], p["ffn2_post_norm"], target=target)

    grads = {}
    dh2, df3, dg3, du3, grads["ffn2_pre_norm"], grads["ffn2_post_norm"] = _ffn_bwd(
        dy, h2, f3, g3, u3, p["ffn2_pre_norm"], p["ffn2_w_gate"], p["ffn2_w_up"], p["ffn2_w_down"],
        p["ffn2_post_norm"])
    (gud,) = _ffn_wgrad(n3, df3, dg3, du3, a3)
    grads["ffn2_w_gate"], grads["ffn2_w_up"], grads["ffn2_w_down"] = gud[:, :FJ], gud[:, FJ:2 * FJ], gud[:, 2 * FJ:]

    dogla, dgg, doswa, dm, grads["mix_post_norm"], grads["gla_out_norm"], grads["swa_out_norm"] = _mix_out_bwd(
        dh2, m, ogla, gg, oswa, p["gla_out_norm"], p["swa_out_norm"], p["w_out"], p["mix_post_norm"])
    grads["w_out"] = _xty(cat, dm)
    dsq, dsk, dsv, dkm, dvm, dsinks = _swa_bwd(p["swa_sinks"], sq, sk, sv, oswa, doswa)
    grads["swa_sinks"] = dsinks[:, 0]
    dgq, dgk, dgv, dla = _gla_bwd(gq, gk, gv, la, ss, dogla)
    dh1, dproj, grads["mix_pre_norm"], dwa2p, grads["gla_b_a"] = _mix_in_bwd(
        dh2, h1, p["mix_pre_norm"], p["w_in"], p["gla_w_a2"], p["gla_b_a"], cos, sin, ga, dgq, dgk, dgv, dgg, dla,
        dsq, dsk, dsv, dkm, dvm)
    grads["gla_w_a2"] = dwa2p[:GLA_RANK]
    grads["w_in"] = _unpack_dwin(_xty(n2, dproj))

    dh0, df1, dg1, du1, grads["ffn1_pre_norm"], grads["ffn1_post_norm"] = _ffn_bwd(
        dh1, h0, f1, g1, u1, p["ffn1_pre_norm"], p["ffn1_w_gate"], p["ffn1_w_up"], p["ffn1_w_down"],
        p["ffn1_post_norm"])
    (gud,) = _ffn_wgrad(n1, df1, dg1, du1, a1)
    grads["ffn1_w_gate"], grads["ffn1_w_up"], grads["ffn1_w_down"] = gud[:, :FJ], gud[:, FJ:2 * FJ], gud[:, 2 * FJ:]
    grads["meta_tokens"] = dh0[PAD:BLK]
    return sse[0, 0], dh0[BLK:], grads


WEIGHTS = ['meta_tokens', 'ffn1_pre_norm', 'ffn1_w_gate', 'ffn1_w_up', 'ffn1_w_down', 'ffn1_post_norm',
           'mix_pre_norm', 'w_in', 'gla_w_a2', 'gla_b_a', 'gla_out_norm', 'swa_sinks', 'swa_out_norm', 'w_out',
           'mix_post_norm', 'ffn2_pre_norm', 'ffn2_w_gate', 'ffn2_w_up', 'ffn2_w_down', 'ffn2_post_norm']
BIG = ['ffn1_w_gate', 'ffn1_w_up', 'ffn1_w_down', 'w_in', 'w_out', 'ffn2_w_gate', 'ffn2_w_up', 'ffn2_w_down']
SMALL = [n for n in WEIGHTS if n not in BIG]
FJ = D_FF // N_CHIPS
D_IN_J = D_IN // N_CHIPS
D_OUT_J = D_MODEL // N_CHIPS
TRANSPOSED = ('ffn1_w_gate', 'ffn1_w_up', 'ffn2_w_gate', 'ffn2_w_up')


def _shard2d(name, a):
    return a[0].T if name in TRANSPOSED else a[0]


def _unshard2d(name, a):
    return (a.T if name in TRANSPOSED else a)[None]


def _small_rows(name, a):
    flat = a.reshape(-1)
    rows = -(-flat.shape[0] // 1024) * 8
    return jnp.pad(flat, (0, rows * 128 - flat.shape[0])).reshape(rows, 128)


def kernel(x, meta_tokens, ffn1_pre_norm, ffn1_w_gate, ffn1_w_up, ffn1_w_down, ffn1_post_norm, mix_pre_norm, w_in, gla_w_a2, gla_b_a, gla_out_norm, swa_sinks, swa_out_norm, w_out, mix_post_norm, ffn2_pre_norm, ffn2_w_gate, ffn2_w_up, ffn2_w_down, ffn2_post_norm, loss_target, m_meta_tokens, m_ffn1_pre_norm, m_ffn1_w_gate, m_ffn1_w_up, m_ffn1_w_down, m_ffn1_post_norm, m_mix_pre_norm, m_w_in, m_gla_w_a2, m_gla_b_a, m_gla_out_norm, m_swa_sinks, m_swa_out_norm, m_w_out, m_mix_post_norm, m_ffn2_pre_norm, m_ffn2_w_gate, m_ffn2_w_up, m_ffn2_w_down, m_ffn2_post_norm, v_meta_tokens, v_ffn1_pre_norm, v_ffn1_w_gate, v_ffn1_w_up, v_ffn1_w_down, v_ffn1_post_norm, v_mix_pre_norm, v_w_in, v_gla_w_a2, v_gla_b_a, v_gla_out_norm, v_swa_sinks, v_swa_out_norm, v_w_out, v_mix_post_norm, v_ffn2_pre_norm, v_ffn2_w_gate, v_ffn2_w_up, v_ffn2_w_down, v_ffn2_post_norm):
    args = dict(locals())
    w = {n: args[n] for n in WEIGHTS}
    mom = {n: args["m_" + n] for n in WEIGHTS}
    var = {n: args["v_" + n] for n in WEIGHTS}
    cx, cy, cc = lax.axis_index("x"), lax.axis_index("y"), lax.axis_index("c")
    q_idx = (2 * cx + cy).astype(jnp.int32).reshape(1)
    c_idx = cc.astype(jnp.int32).reshape(1)

    q_chip = 2 * cx + cy
    bf = {n: _own_slot(_shard2d(n, w[n]).astype(BF16), q_chip) for n in BIG}
    early = _GatherChips([bf["ffn1_w_gate"], bf["ffn1_w_up"], bf["ffn1_w_down"], _own_slot(w["meta_tokens"], q_chip),
                          _own_slot(w["gla_w_a2"].reshape(GLA_RANK, GLA_KW // N_CHIPS), q_chip)])
    wg1, wu1, wd1, meta4, wa24 = _comm_call(early, "gather_ffn1")
    meta_full = meta4.transpose(1, 0, 2).reshape(N_META, D_MODEL)
    wa2p = jnp.pad(wa24.transpose(1, 0, 2).reshape(GLA_RANK, GLA_KW), ((0, 128 - GLA_RANK), (0, 0))).astype(BF16)
    sinks = w["swa_sinks"].reshape(SWA_QH)

    seq, target = x[0], loss_target[0]
    t = seq.shape[0] + BLK
    h0 = jnp.concatenate([jnp.zeros((PAD, D_MODEL), F32), meta_full, seq], axis=0)
    cos, sin = _rope_tables(t)
    late = _GatherChips([bf["w_in"], bf["w_out"], bf["ffn2_w_gate"], bf["ffn2_w_up"], bf["ffn2_w_down"]])
    (h1, n1, g1, u1, a1, f1), (win4, wout4, wg2, wu2, wd2) = _ffn_fwd(
        h0, w["ffn1_pre_norm"], wg1, wu1, wd1, w["ffn1_post_norm"], hook=late)
    winp = _pack_win(win4.transpose(1, 0, 2).reshape(D_MODEL, D_IN))
    wout = wout4.reshape(D_MODEL, D_MODEL)
    n2, gq, gk, gv, gg, ga, la, sq, sk, sv = _mix_proj(h1, w["mix_pre_norm"], winp, wa2p, w["gla_b_a"], cos, sin)
    ogla, ss = _gla_fwd(gq, gk, gv, la)
    oswa = _swa_fwd(sinks, sq, sk, sv)
    h2, cat, m = _mix_out(h1, ogla, gg, oswa, w["gla_out_norm"], w["swa_out_norm"], wout, w["mix_post_norm"])
    dy, n3, g3, u3, a3, f3, sse = _ffn_fwd(h2, w["ffn2_pre_norm"], wg2, wu2, wd2, w["ffn2_post_norm"], target=target)
    loss = lax.psum(sse[0, 0] * (0.5 / D_MODEL), ("x", "y", "c"))

    g = {}
    dh2, df3, dg3, du3, g["ffn2_pre_norm"], g["ffn2_post_norm"] = _ffn_bwd(
        dy, h2, f3, g3, u3, w["ffn2_pre_norm"], wg2, wu2, wd2, w["ffn2_post_norm"])
    (gf2,) = _ffn_wgrad(n3, df3, dg3, du3, a3)
    (dogla, dgg, doswa, dm, g["mix_post_norm"], g["gla_out_norm"], g["swa_out_norm"]), (rgf2,) = _mix_out_bwd(
        dh2, m, ogla, gg, oswa, w["gla_out_norm"], w["swa_out_norm"], wout, w["mix_post_norm"],
        hook=_PairExchange([gf2]))
    sgf2 = _pair_sum(gf2, rgf2, c_idx)
    gout = _xty(cat, dm).reshape(N_CHIPS, D_OUT_J, D_MODEL).astype(BF16)
    (dsq, dsk, dsv, dkm, dvm, dsinks), (ogf2,) = _swa_bwd(sinks, sq, sk, sv, oswa, doswa,
                                                          hook=_ChipScatter([sgf2]))
    g["swa_sinks"] = dsinks
    dgq, dgk, dgv, dla = _gla_bwd(gq, gk, gv, la, ss, dogla)
    dh1, dproj, g["mix_pre_norm"], dwa2p, g["gla_b_a"] = _mix_in_bwd(
        dh2, h1, w["mix_pre_norm"], winp, wa2p, w["gla_b_a"], cos, sin, ga, dgq, dgk, dgv, dgg, dla,
        dsq, dsk, dsv, dkm, dvm)
    g["gla_w_a2"] = dwa2p[:GLA_RANK]
    gin = _unpack_dwin(_xty(n2, dproj)).reshape(D_MODEL, N_CHIPS, D_IN_J).transpose(1, 0, 2).astype(BF16)
    (dh0, df1, dg1, du1, g["ffn1_pre_norm"], g["ffn1_post_norm"]), (rgin, rgout) = _ffn_bwd(
        dh1, h0, f1, g1, u1, w["ffn1_pre_norm"], wg1, wu1, wd1, w["ffn1_post_norm"],
        hook=_PairExchange([gin, gout]))
    sgin, sgout = _pair_sum(gin, rgin, c_idx), _pair_sum(gout, rgout, c_idx)
    (gf1,), (ogin, ogout) = _ffn_wgrad(n1, df1, dg1, du1, a1, hook=_ChipScatter([sgin, sgout]))
    g["meta_tokens"] = dh0[PAD:BLK]
    grad_x = dh0[BLK:]
    (rgf1,) = _comm_call(_PairExchange([gf1]), "pair_exchange_ffn1")
    sgf1 = _pair_sum(gf1, rgf1, c_idx)
    (ogf1,) = _comm_call(_ChipScatter([sgf1]), "chip_scatter_ffn1")
    halves = [_chip_sum(s, o, q_idx) for s, o in ((sgf1, ogf1), (sgin, ogin), (sgout, ogout), (sgf2, ogf2))]
    others = _comm_call(_PairShare(halves), "pair_share")
    reduced = {"ffn1_w_gate": (0, 0), "ffn1_w_up": (0, FJ), "ffn1_w_down": (0, 2 * FJ), "w_in": (1, 0),
               "w_out": (2, 0), "ffn2_w_gate": (3, 0), "ffn2_w_up": (3, FJ), "ffn2_w_down": (3, 2 * FJ)}
    grad, delta, new_m, new_v = {}, {}, {}, {}
    for n in BIG:
        k, row0 = reduced[n]
        outs = _adamw_halves(_shard2d(n, w[n]), halves[k], others[k], _shard2d(n, mom[n]), _shard2d(n, var[n]),
                             c_idx, row0)
        grad[n], delta[n], new_m[n], new_v[n] = [_unshard2d(n, a) for a in outs]

    late = ["gla_w_a2", "swa_sinks"]
    direct = [n for n in SMALL if n not in late]
    names = direct + late
    gathered = _all_gather_devices([g[n] for n in names])
    mat = lambda a: a.reshape(a.shape[-2:])
    none2 = [None] * len(late)
    outs = _small_update(q_idx, gathered, [mat(w[n]) for n in direct] + none2, [mat(mom[n]) for n in direct] + none2,
                         [mat(var[n]) for n in direct] + none2, [n == "meta_tokens" for n in names])
    sum_a2, sum_sinks = outs[4 * len(direct):]
    g_late = [lax.dynamic_slice_in_dim(sum_a2, q_chip * (GLA_KW // N_CHIPS), GLA_KW // N_CHIPS, axis=1)[None],
              sum_sinks[:, 0].reshape(1, 1, SWA_QH)]
    outs = list(outs[:4 * len(direct)]) + list(_small_update(
        q_idx, g_late, [mat(w[n]) for n in late], [mat(mom[n]) for n in late], [mat(var[n]) for n in late],
        [False, False]))
    for k, n in enumerate(names):
        grad[n], delta[n], new_m[n], new_v[n] = [a.reshape(w[n].shape) for a in outs[4 * k:4 * k + 4]]

    return (loss, grad_x[None], *[grad[n] for n in WEIGHTS], *[delta[n] for n in WEIGHTS],
            *[new_m[n] for n in WEIGHTS], *[new_v[n] for n in WEIGHTS])
```

```python
import functools
import math

import numpy as np
import jax
import jax.numpy as jnp
from jax import lax
from jax.experimental import pallas as pl
from jax.experimental.pallas import tpu as pltpu

F32 = jnp.float32
BF16 = jnp.bfloat16
MESH = pl.DeviceIdType.MESH

D_MODEL = 1024
D_FF = 2816
N_CHIPS = 4
N_DEV = 8
N_META = 16
BLK = 128
PAD = BLK - N_META
GLA_CHUNK = 64
GLA_HEADS = 4
GLA_DV = 128
GLA_DK = 64
GLA_KW = GLA_HEADS * GLA_DK
GLA_W = GLA_HEADS * GLA_DV
GLA_RANK = 16
GLA_TAU = 16.0
SWA_HD = 64
SWA_QH = 8
SWA_KVH = 2
SWA_W = SWA_QH * SWA_HD
WINDOW = 128
ROPE_THETA = 10000.0
EPS = 1e-6
NEG_INF = -1e30
IN_SPLITS = (256, 256, 512, 512, 16, 512, 128, 128)
D_IN = sum(IN_SPLITS)
P_GQ, P_GK, P_GV, P_GG, P_GA, P_SQ, P_SK, P_SV, P_END = 0, 256, 512, 1024, 1536, 1664, 2176, 2432, 2688
ADAM_LR, ADAM_B1, ADAM_B2, ADAM_EPS, ADAM_WD, ADAM_STEP = 0.001, 0.9, 0.999, 1e-08, 0.01, 10
VMEM_LIMIT = 56 * 1024 * 1024

NT = (((1,), (1,)), ((), ()))
TN = (((0,), (0,)), ((), ()))


def _cparams(n_axes):
    return pltpu.CompilerParams(dimension_semantics=("arbitrary",) * n_axes, vmem_limit_bytes=VMEM_LIMIT)


def _row_tile(t):
    for tm in (640, 512, 384, 256, 128):
        if t % tm == 0:
            return tm
    raise ValueError(t)


SEQ_BLOCKS_PER_STEP = 5


def _seq_tile(t):
    return SEQ_BLOCKS_PER_STEP * BLK if t % (SEQ_BLOCKS_PER_STEP * BLK) == 0 else BLK


ROW_PARTS = 2


def _row_parts(tm):
    n = ROW_PARTS if tm % (16 * ROW_PARTS) == 0 else 1
    return [slice(k * (tm // n), (k + 1) * (tm // n)) for k in range(n)]


def _contract_tile(t):
    return 1664 if t % 1664 == 0 else _row_tile(t)


def _div_tile(r, cap=512):
    best = None
    for tr in range(8, min(r, cap) + 1, 8):
        if r % tr == 0:
            best = tr
    return best if best is not None else r


def _dot(a, b):
    return jnp.dot(a, b, preferred_element_type=F32)


def _dg(a, b, dims):
    return lax.dot_general(a, b, dims, preferred_element_type=F32)


def _rms(x, w):
    r = lax.rsqrt(jnp.mean(x * x, axis=-1, keepdims=True) + EPS)
    xh = x * r
    return xh * w, xh, r


def _rms_bwd(xh, r, w, dy):
    wdy = dy * w
    dx = r * (wdy - xh * jnp.mean(wdy * xh, axis=-1, keepdims=True))
    dw = jnp.sum(dy * xh, axis=0, keepdims=True)
    return dx, dw


def _sigmoid(x):
    return 1.0 / (1.0 + jnp.exp(-x))


def _full(shape):
    nd = len(shape)
    return pl.BlockSpec(shape, lambda *_: (0,) * nd)


ANY = pl.BlockSpec(memory_space=pl.ANY)


def _pallas(body, *, name, grid, in_specs, out_specs, out_shape, args, scratch_shapes=(), hook=None):
    n_axes = len(grid)
    if hook is None:
        return pl.pallas_call(body, name=name, grid=grid, in_specs=list(in_specs), out_specs=list(out_specs),
                              out_shape=list(out_shape), scratch_shapes=list(scratch_shapes),
                              compiler_params=_cparams(n_axes))(*args)
    n_in, n_out, n_scr = len(in_specs), len(out_specs), len(scratch_shapes)
    h_in, h_out = len(hook.inputs), len(hook.out_shape)
    total = math.prod(grid)

    def wrapped(*refs):
        ins, hins = refs[:n_in], refs[n_in:n_in + h_in]
        o0 = n_in + h_in
        outs, houts = refs[o0:o0 + n_out], refs[o0 + n_out:o0 + n_out + h_out]
        s0 = o0 + n_out + h_out
        scr, hscr = refs[s0:s0 + n_scr], refs[s0 + n_scr:]
        step = pl.program_id(0)
        for a in range(1, n_axes):
            step = step * grid[a] + pl.program_id(a)

        @pl.when(step == 0)
        def _():
            hook.start(hins, houts, hscr)

        body(*ins, *outs, *scr)

        if hook.has_mid:
            @pl.when(step == (3 * total) // 4)
            def _():
                hook.mid(hins, houts, hscr)

        @pl.when(step == total - 1)
        def _():
            hook.finish(hins, houts, hscr)

    res = pl.pallas_call(
        wrapped, name=name, grid=grid, in_specs=list(in_specs) + [ANY] * h_in,
        out_specs=list(out_specs) + [ANY] * h_out, out_shape=list(out_shape) + list(hook.out_shape),
        scratch_shapes=list(scratch_shapes) + list(hook.scratch), compiler_params=_cparams(n_axes),
        input_output_aliases={n_in + a: n_out + b for a, b in hook.aliases},
    )(*args, *hook.inputs)
    return res[:n_out], res[n_out:]


def _ffn_fwd(h, wpre, wg4, wu4, wd4, wpost, hook=None, target=None):
    t = h.shape[0]
    tm = _row_tile(t)
    nj, fj, _ = wg4.shape
    nblk = tm // BLK if target is not None else 0

    def body(*refs):
        h_ref, wpre_ref, wg_ref, wu_ref, wd_ref, wpost_ref = refs[:6]
        t_refs = refs[6:6 + nblk]
        hout_ref, n_ref, p1_ref, p2_ref, a_ref, f_ref = refs[6 + nblk:12 + nblk]
        acc_ref = refs[-1]
        i = pl.program_id(0)
        j = pl.program_id(1)

        @pl.when(j == 0)
        def _():
            y, _, _ = _rms(h_ref[...], wpre_ref[...])
            n_ref[...] = y.astype(BF16)
            acc_ref[...] = jnp.zeros_like(acc_ref)

        if target is not None:
            sse_ref = refs[12 + nblk]

            @pl.when((i == 0) & (j == 0))
            def _():
                sse_ref[...] = jnp.zeros_like(sse_ref)

        parts = [slice(0, tm)]
        gus = [(_dg(n_ref[rows, :], wg_ref[...], NT), _dg(n_ref[rows, :], wu_ref[...], NT)) for rows in parts]
        for rows, (g, u) in zip(parts, gus):
            sg = _sigmoid(g)
            silu = g * sg
            p1_ref[rows, :] = (u * (sg + silu * (1.0 - sg))).astype(BF16)
            p2_ref[rows, :] = silu.astype(BF16)
            a = (silu * u).astype(BF16)
            a_ref[rows, :] = a
            acc_ref[rows, :] += _dot(a, wd_ref[...])

        @pl.when(j == nj - 1)
        def _():
            f = acc_ref[...]
            f_ref[...] = f
            y, _, _ = _rms(f, wpost_ref[...])
            hout = h_ref[...] + 0.5 * y
            if target is None:
                hout_ref[...] = hout
            else:
                sse = jnp.zeros((1, 1), F32)
                for k in range(nblk):
                    rows = slice(k * BLK, (k + 1) * BLK)
                    err = hout[rows] - t_refs[k][...]
                    if k == 0:
                        err = jnp.where(i > 0, err, 0.0)
                    hout_ref[rows, :] = err * (1.0 / D_MODEL)
                    sse = sse + jnp.sum(jnp.sum(err * err, axis=1, keepdims=True), axis=0, keepdims=True)
                sse_ref[...] += jnp.broadcast_to(sse, sse_ref.shape)

    row = pl.BlockSpec((tm, D_MODEL), lambda i, j: (i, 0))
    vec = pl.BlockSpec((1, D_MODEL), lambda i, j: (0, 0))
    wrow = pl.BlockSpec((None, fj, D_MODEL), lambda i, j: (j, 0, 0))
    act = pl.BlockSpec((None, tm, fj), lambda i, j: (j, i, 0))
    t_specs = [pl.BlockSpec((BLK, D_MODEL), functools.partial(lambda i, j, k: (jnp.maximum(nblk * i + k - 1, 0), 0), k=k))
               for k in range(nblk)]
    loss_spec = [_full((1, 128))] if target is not None else []
    loss_shape = [jax.ShapeDtypeStruct((1, 128), F32)] if target is not None else []
    return _pallas(
        body, name="ffn_fwd", grid=(t // tm, nj),
        in_specs=[row, vec, wrow, wrow, wrow, vec] + t_specs,
        out_specs=[row, row, act, act, act, row] + loss_spec,
        out_shape=[jax.ShapeDtypeStruct((t, D_MODEL), F32), jax.ShapeDtypeStruct((t, D_MODEL), BF16),
                   jax.ShapeDtypeStruct((nj, t, fj), BF16), jax.ShapeDtypeStruct((nj, t, fj), BF16),
                   jax.ShapeDtypeStruct((nj, t, fj), BF16), jax.ShapeDtypeStruct((t, D_MODEL), F32)] + loss_shape,
        scratch_shapes=[pltpu.VMEM((tm, D_MODEL), F32)],
        args=(h, wpre, wg4, wu4, wd4, wpost) + (target,) * nblk, hook=hook)


def _ffn_bwd(dhout, h, f, p14, p24, wpre, wg4, wu4, wd4, wpost, hook=None):
    t = h.shape[0]
    tm = _row_tile(t)
    nj, fj, _ = wg4.shape

    def body(dhout_ref, h_ref, f_ref, p1_ref, p2_ref, wpre_ref, wg_ref, wu_ref, wd_ref, wpost_ref,
             dh_ref, df_ref, dg_ref, du_ref, dwpre_ref, dwpost_ref, dn_ref):
        i = pl.program_id(0)
        j = pl.program_id(1)

        @pl.when((i == 0) & (j == 0))
        def _():
            dwpre_ref[...] = jnp.zeros_like(dwpre_ref)
            dwpost_ref[...] = jnp.zeros_like(dwpost_ref)

        @pl.when(j == 0)
        def _():
            wpost = wpost_ref[...]
            _, fh, r = _rms(f_ref[...], wpost)
            df, dw = _rms_bwd(fh, r, wpost, 0.5 * dhout_ref[...])
            dwpost_ref[...] += dw
            df_ref[...] = df.astype(BF16)
            dn_ref[...] = jnp.zeros_like(dn_ref)

        parts = _row_parts(tm)
        das = [_dg(df_ref[rows, :], wd_ref[...], NT) for rows in parts]
        for rows, da in zip(parts, das):
            dg = (da * p1_ref[rows, :].astype(F32)).astype(BF16)
            du = (da * p2_ref[rows, :].astype(F32)).astype(BF16)
            dg_ref[rows, :] = dg
            du_ref[rows, :] = du
            dn_ref[rows, :] += _dot(dg, wg_ref[...]) + _dot(du, wu_ref[...])

        @pl.when(j == nj - 1)
        def _():
            wpre = wpre_ref[...]
            _, hh, r = _rms(h_ref[...], wpre)
            dx, dw = _rms_bwd(hh, r, wpre, dn_ref[...])
            dwpre_ref[...] += dw
            dh_ref[...] = dhout_ref[...] + dx

    row = pl.BlockSpec((tm, D_MODEL), lambda i, j: (i, 0))
    vec = pl.BlockSpec((1, D_MODEL), lambda i, j: (0, 0))
    wrow = pl.BlockSpec((None, fj, D_MODEL), lambda i, j: (j, 0, 0))
    act = pl.BlockSpec((None, tm, fj), lambda i, j: (j, i, 0))
    actshape = jax.ShapeDtypeStruct((nj, t, fj), BF16)
    return _pallas(
        body, name="ffn_bwd", grid=(t // tm, nj),
        in_specs=[row, row, row, act, act, vec, wrow, wrow, wrow, vec],
        out_specs=[row, row, act, act, vec, vec],
        out_shape=[jax.ShapeDtypeStruct((t, D_MODEL), F32), jax.ShapeDtypeStruct((t, D_MODEL), BF16),
                   actshape, actshape,
                   jax.ShapeDtypeStruct((1, D_MODEL), F32), jax.ShapeDtypeStruct((1, D_MODEL), F32)],
        scratch_shapes=[pltpu.VMEM((tm, D_MODEL), F32)],
        args=(dhout, h, f, p14, p24, wpre, wg4, wu4, wd4, wpost), hook=hook)


def _ffn_wgrad(n, df, dg4, du4, a4, hook=None):
    t = n.shape[0]
    tm = _contract_tile(t)
    ni = t // tm
    nj, _, fj = dg4.shape

    def body(n_ref, df_ref, dg_ref, du_ref, a_ref, dw_ref, acc):
        i = pl.program_id(1)

        @pl.when(i == 0)
        def _():
            acc[...] = jnp.zeros_like(acc)

        nn = n_ref[...]
        acc[0:fj, :] += _dg(dg_ref[...], nn, TN)
        acc[fj:2 * fj, :] += _dg(du_ref[...], nn, TN)
        acc[2 * fj:3 * fj, :] += _dg(a_ref[...], df_ref[...], TN)

        @pl.when(i == ni - 1)
        def _():
            dw_ref[...] = acc[...].astype(BF16)

    row = pl.BlockSpec((tm, D_MODEL), lambda j, i: (i, 0))
    act = pl.BlockSpec((None, tm, fj), lambda j, i: (j, i, 0))
    return _pallas(
        body, name="ffn_wgrad", grid=(nj, ni),
        in_specs=[row, row, act, act, act],
        out_specs=[pl.BlockSpec((None, 3 * fj, D_MODEL), lambda j, i: (j, 0, 0))],
        out_shape=[jax.ShapeDtypeStruct((nj, 3 * fj, D_MODEL), BF16)],
        scratch_shapes=[pltpu.VMEM((3 * fj, D_MODEL), F32)],
        args=(n, df, dg4, du4, a4), hook=hook)


def _xty(x, y):
    t, k = x.shape
    n = y.shape[1]
    tm = _contract_tile(t)
    tn = n if n <= 1024 else (896 if n % 896 == 0 else 128)

    def body(x_ref, y_ref, o_ref):
        @pl.when(pl.program_id(1) == 0)
        def _():
            o_ref[...] = jnp.zeros_like(o_ref)

        o_ref[...] += _dg(x_ref[...], y_ref[...], TN)

    return pl.pallas_call(
        body, name="xty", grid=(n // tn, t // tm),
        in_specs=[pl.BlockSpec((tm, k), lambda j, i: (i, 0)), pl.BlockSpec((tm, tn), lambda j, i: (i, j))],
        out_specs=pl.BlockSpec((k, tn), lambda j, i: (0, j)),
        out_shape=jax.ShapeDtypeStruct((k, n), F32),
        compiler_params=_cparams(2),
    )(x, y)


def _rope_tables(t):
    pos = (jnp.arange(t, dtype=jnp.int32) - PAD).astype(F32)
    inv_freq = 1.0 / (ROPE_THETA ** (jnp.arange(0, SWA_HD, 2, dtype=F32) / SWA_HD))
    ang = pos[:, None] * inv_freq[None, :]
    cos = jnp.cos(ang)
    sin = jnp.sin(ang)
    return jnp.concatenate([cos, cos, cos, cos], axis=1), jnp.concatenate([-sin, sin, -sin, sin], axis=1)


def _rot_half(x, first_half):
    return jnp.where(first_half, pltpu.roll(x, 96, 1), pltpu.roll(x, 32, 1))


def _first_half_mask(rows):
    lane = lax.broadcasted_iota(jnp.int32, (rows, 128), 1)
    return (lane % 64) < 32


def _log_sigmoid(z):
    return jnp.minimum(z, 0.0) - jnp.log(1.0 + jnp.exp(-jnp.abs(z)))


def _mix_proj(h1, wmixpre, winp, wa2p, bap, cos, sin):
    t = h1.shape[0]
    tm = _row_tile(t)

    def body(h_ref, w_ref, win_ref, wa2_ref, ba_ref, cos_ref, sin_ref,
             n_ref, gq_ref, gk_ref, gv_ref, gg_ref, ga_ref, la_ref, sq_ref, sk_ref, sv_ref):
        y, _, _ = _rms(h_ref[...], w_ref[...])
        n = y.astype(BF16)
        n_ref[...] = n
        proj = _dot(n, win_ref[...])
        gq_ref[...] = proj[:, P_GQ:P_GK]
        gk_ref[...] = proj[:, P_GK:P_GV]
        gv_ref[...] = proj[:, P_GV:P_GG]
        gg_ref[...] = proj[:, P_GG:P_GA]
        ga = proj[:, P_GA:P_SQ]
        ga_ref[...] = ga
        z = _dot(ga.astype(BF16), wa2_ref[...]) + ba_ref[...]
        la_ref[...] = _log_sigmoid(z) * (1.0 / GLA_TAU)
        c = cos_ref[...]
        s = sin_ref[...]
        fh = _first_half_mask(tm)
        for k in range(4):
            x = proj[:, P_SQ + 128 * k:P_SQ + 128 * (k + 1)]
            sq_ref[:, 128 * k:128 * (k + 1)] = (x * c + _rot_half(x, fh) * s).astype(BF16)
        for k in range(2):
            x = proj[:, P_SK + 128 * k:P_SK + 128 * (k + 1)]
            sk_ref[:, 128 * k:128 * (k + 1)] = (x * c + _rot_half(x, fh) * s).astype(BF16)
        sv_ref[...] = proj[:, P_SV:P_END].astype(BF16)

    def row(w):
        return pl.BlockSpec((tm, w), lambda i: (i, 0))

    def rshape(w, dt):
        return jax.ShapeDtypeStruct((t, w), dt)

    return pl.pallas_call(
        body, name="mix_proj", grid=(t // tm,),
        in_specs=[row(D_MODEL), _full((1, D_MODEL)), _full((D_MODEL, P_END)), _full((128, GLA_KW)),
                  _full((1, GLA_KW)), row(128), row(128)],
        out_specs=[row(D_MODEL), row(256), row(256), row(512), row(512), row(128), row(256), row(512), row(256),
                   row(256)],
        out_shape=[rshape(D_MODEL, BF16), rshape(256, F32), rshape(256, F32), rshape(512, F32), rshape(512, F32),
                   rshape(128, F32), rshape(256, F32), rshape(512, BF16), rshape(256, BF16), rshape(256, BF16)],
        compiler_params=_cparams(1),
    )(h1, wmixpre, winp, wa2p, bap, cos, sin)


def _scan_rows(x, reverse=False):
    n = x.shape[0]
    row = lax.broadcasted_iota(jnp.int32, x.shape, 0)
    s = 1
    while s < n:
        if reverse:
            x = x + jnp.where(row < n - s, pltpu.roll(x, n - s, 0), 0.0)
        else:
            x = x + jnp.where(row >= s, pltpu.roll(x, s, 0), 0.0)
        s *= 2
    return x


def _gla_cumsum(la, tril_f):
    b = _scan_rows(la)
    row = lax.broadcasted_iota(jnp.int32, b.shape, 0)
    bm = jnp.sum(jnp.where(row == GLA_CHUNK // 2 - 1, b, 0.0), axis=0, keepdims=True)
    bl = jnp.sum(jnp.where(row == GLA_CHUNK - 1, b, 0.0), axis=0, keepdims=True)
    return b, bm, bl


def _gla_decays(la, tril_f):
    b, bm, bl = _gla_cumsum(la, tril_f)
    return jnp.exp(b - bm), jnp.exp(bm - b), jnp.exp(b), jnp.exp(bl - b), jnp.exp(bl)


def _gla_masks():
    c = GLA_CHUNK
    r = lax.broadcasted_iota(jnp.int32, (c, c), 0)
    col = lax.broadcasted_iota(jnp.int32, (c, c), 1)
    r4 = lax.broadcasted_iota(jnp.int32, (GLA_HEADS * c, c), 0) % c
    c4 = lax.broadcasted_iota(jnp.int32, (GLA_HEADS * c, c), 1)
    klane = lax.broadcasted_iota(jnp.int32, (c, GLA_KW), 1) // GLA_DK
    vlane = lax.broadcasted_iota(jnp.int32, (c, GLA_W), 1) // GLA_DV
    srow = lax.broadcasted_iota(jnp.int32, (GLA_W, GLA_KW), 0) // GLA_DV
    scol = lax.broadcasted_iota(jnp.int32, (GLA_W, GLA_KW), 1) // GLA_DK
    return dict(tril_f=(r >= col).astype(F32), triu_f=(r <= col).astype(F32), tril4=r4 >= c4,
                khead=[klane == h for h in range(GLA_HEADS)], vhead=[vlane == h for h in range(GLA_HEADS)],
                diag=srow == scol)


def _stack_heads(x, head_masks):
    return jnp.concatenate([jnp.where(m, x, 0.0) for m in head_masks], axis=0)


def _gla_fwd(gq, gk, gv, la):
    t = gq.shape[0]
    rg = _seq_tile(t)
    nb = t // rg
    ncb = rg // GLA_CHUNK
    c = GLA_CHUNK

    def body(q_ref, k_ref, v_ref, la_ref, o_ref, ss_ref, st_ref):
        @pl.when(pl.program_id(0) == 0)
        def _():
            st_ref[...] = jnp.zeros_like(st_ref)

        mk = _gla_masks()
        st = st_ref[...]
        for ch in range(ncb):
            rows = slice(ch * c, (ch + 1) * c)
            eq, ek, eb, ekl, ebl = _gla_decays(la_ref[rows, :], mk["tril_f"])
            qs = q_ref[rows, :] * (GLA_DK ** -0.5)
            k = k_ref[rows, :]
            v = v_ref[rows, :].astype(BF16)
            ss_ref[ch] = st
            q4 = _stack_heads(qs * eq, mk["khead"]).astype(BF16)
            a4 = jnp.where(mk["tril4"], _dg(q4, (k * ek).astype(BF16), NT), 0.0).astype(BF16)
            r4 = _dot(a4, v)
            intra = jnp.concatenate([r4[h * c:(h + 1) * c, GLA_DV * h:GLA_DV * (h + 1)] for h in range(GLA_HEADS)],
                                    axis=1)
            o_ref[rows, :] = intra + _dg((qs * eb).astype(BF16), st.astype(BF16), NT)
            st = st * ebl + jnp.where(mk["diag"], _dg(v, (k * ekl).astype(BF16), TN), 0.0)
        st_ref[...] = st

    def row(w):
        return pl.BlockSpec((rg, w), lambda i: (i, 0))

    return pl.pallas_call(
        body, name="gla_fwd", grid=(nb,),
        in_specs=[row(256), row(256), row(512), row(256)],
        out_specs=[row(512), pl.BlockSpec((ncb, GLA_W, GLA_KW), lambda i: (i, 0, 0))],
        out_shape=[jax.ShapeDtypeStruct((t, GLA_W), F32), jax.ShapeDtypeStruct((nb * ncb, GLA_W, GLA_KW), F32)],
        scratch_shapes=[pltpu.VMEM((GLA_W, GLA_KW), F32)],
        compiler_params=_cparams(1),
    )(gq, gk, gv, la)


def _gla_bwd(gq, gk, gv, la, ss, do):
    t = gq.shape[0]
    rg = _seq_tile(t)
    nb = t // rg
    ncb = rg // GLA_CHUNK
    c = GLA_CHUNK

    def body(q_ref, k_ref, v_ref, la_ref, ss_ref, do_ref, dq_ref, dk_ref, dv_ref, dla_ref, dst_ref):
        @pl.when(pl.program_id(0) == 0)
        def _():
            dst_ref[...] = jnp.zeros_like(dst_ref)

        mk = _gla_masks()
        last_row = lax.broadcasted_iota(jnp.int32, (c, GLA_KW), 0) == c - 1
        scale = GLA_DK ** -0.5
        dstn = dst_ref[...]
        for ch in reversed(range(ncb)):
            rows = slice(ch * c, (ch + 1) * c)
            eq, ek, eb, ekl, ebl = _gla_decays(la_ref[rows, :], mk["tril_f"])
            qs = q_ref[rows, :] * scale
            k = k_ref[rows, :]
            qt, kt, qh, kh = qs * eq, k * ek, qs * eb, k * ekl
            ktb, khb, qhb = kt.astype(BF16), kh.astype(BF16), qh.astype(BF16)
            v = v_ref[rows, :].astype(BF16)
            do_f = do_ref[rows, :]
            dob = do_f.astype(BF16)
            st = ss_ref[ch]
            stb = st.astype(BF16)
            dstb = dstn.astype(BF16)
            q4 = _stack_heads(qt, mk["khead"]).astype(BF16)
            do4 = _stack_heads(do_f, mk["vhead"]).astype(BF16)
            a4 = jnp.where(mk["tril4"], _dg(q4, ktb, NT), 0.0).astype(BF16)
            da4 = jnp.where(mk["tril4"], _dg(do4, v, NT), 0.0).astype(BF16)
            dv_ref[rows, :] = _dg(a4, do4, TN) + _dg(khb, dstb, NT)
            dq4 = _dot(da4, ktb)
            dqt = jnp.zeros((c, GLA_KW), F32)
            for h in range(GLA_HEADS):
                dqt = dqt + jnp.where(mk["khead"][h], dq4[h * c:(h + 1) * c], 0.0)
            dkt = _dg(da4, q4, TN)
            dqh = _dot(dob, stb)
            dkh = _dot(v, dstb)
            dbl = jnp.sum(dstn * st, axis=0, keepdims=True)
            dstn = dstn * ebl + jnp.where(mk["diag"], _dg(dob, qhb, TN), 0.0)
            dq_ref[rows, :] = scale * (dqt * eq + dqh * eb)
            dk_ref[rows, :] = dkt * ek + dkh * ekl
            dkk = dkh * kh
            db = dqt * qt - dkt * kt + dqh * qh - dkk
            db = db + jnp.where(last_row, jnp.sum(dkk, axis=0, keepdims=True) + ebl * dbl, 0.0)
            dla_ref[rows, :] = _scan_rows(db, reverse=True)
        dst_ref[...] = dstn

    def row(w):
        return pl.BlockSpec((rg, w), lambda i: (nb - 1 - i, 0))

    def rshape(w):
        return jax.ShapeDtypeStruct((t, w), F32)

    return pl.pallas_call(
        body, name="gla_bwd", grid=(nb,),
        in_specs=[row(256), row(256), row(512), row(256),
                  pl.BlockSpec((ncb, GLA_W, GLA_KW), lambda i: (nb - 1 - i, 0, 0)), row(512)],
        out_specs=[row(256), row(256), row(512), row(256)],
        out_shape=[rshape(256), rshape(256), rshape(512), rshape(256)],
        scratch_shapes=[pltpu.VMEM((GLA_W, GLA_KW), F32)],
        compiler_params=_cparams(1),
    )(gq, gk, gv, la, ss, do)


SWA_G = SWA_QH // SWA_KVH


def _swa_bias():
    n = jnp.arange(3, dtype=jnp.int32)[:, None, None]
    r = (jnp.arange(SWA_G * BLK, dtype=jnp.int32) % BLK)[None, :, None]
    c = jnp.arange(3 * BLK, dtype=jnp.int32)[None, None, :]
    seg = c // BLK
    cc = c % BLK
    qpos = n * BLK + r - PAD
    kpos = jnp.where(seg == 0, (n - 1) * BLK, jnp.where(seg == 1, n * BLK, 0)) + cc - PAD
    band = (seg < 2) & (kpos >= N_META) & (kpos <= qpos) & (qpos - kpos < WINDOW)
    meta = (seg == 2) & (kpos >= 0) & (kpos < N_META) & (kpos <= qpos)
    return jnp.where(band | meta, 0.0, NEG_INF).astype(F32)


def _swa_stack(ref, rows, kh, lo, dtype):
    parts = []
    for g in range(2):
        pair = ref[rows, 128 * (2 * kh + g):128 * (2 * kh + g + 1)]
        zero = jnp.zeros_like(pair)
        parts += [jnp.where(lo, pair, zero), jnp.where(lo, zero, pair)]
    return jnp.concatenate(parts, axis=0).astype(dtype)


def _swa_unstack(x4, lo):
    return [jnp.where(lo, x4[2 * g * BLK:(2 * g + 1) * BLK], x4[(2 * g + 1) * BLK:(2 * g + 2) * BLK])
            for g in range(2)]


def _swa_sink_col(sink_ref, kh):
    blk = lax.broadcasted_iota(jnp.int32, (SWA_G * BLK, 1), 0) // BLK
    col = jnp.full((SWA_G * BLK, 1), sink_ref[SWA_G * kh + SWA_G - 1], F32)
    for e in reversed(range(SWA_G - 1)):
        col = jnp.where(blk == e, sink_ref[SWA_G * kh + e], col)
    return col


def _swa_probs(q4, kall, bias, sink):
    s = _dg(q4, kall, NT) * (SWA_HD ** -0.5) + bias
    m = jnp.maximum(jnp.max(s, axis=-1, keepdims=True), sink)
    p = jnp.exp(s - m)
    es = jnp.exp(sink - m)
    inv = 1.0 / (jnp.sum(p, axis=-1, keepdims=True) + es)
    return p * inv, es * inv


def _swa_keys(prev_ref, cur_ref, first_ref, b, ls):
    before = prev_ref[:, ls] if b == 0 else cur_ref[(b - 1) * BLK:b * BLK, ls]
    return jnp.concatenate([before, cur_ref[b * BLK:(b + 1) * BLK, ls], first_ref[:, ls]], axis=0)


def _swa_specs(rs, ns):
    bps = rs // BLK
    cur = lambda w: pl.BlockSpec((rs, w), lambda i: (jnp.minimum(i, ns - 1), 0))
    prev = lambda w: pl.BlockSpec((BLK, w), lambda i: (jnp.maximum(jnp.minimum(i, ns - 1) * bps - 1, 0), 0))
    first = lambda w: pl.BlockSpec((BLK, w), lambda i: (0, 0))
    return cur, prev, first


def _swa_fwd(sinks, sq, sk, sv):
    t = sq.shape[0]
    rs = _seq_tile(t)
    bps, ns = rs // BLK, t // rs

    def body(sink_ref, bias_ref, q_ref, kp_ref, kc_ref, km_ref, vp_ref, vc_ref, vm_ref, o_ref):
        i = pl.program_id(0)
        lo = lax.broadcasted_iota(jnp.int32, (BLK, 128), 1) < 64
        sink_cols = [_swa_sink_col(sink_ref, kh) for kh in range(SWA_KVH)]
        for b in range(bps):
            rows = slice(b * BLK, (b + 1) * BLK)
            bias = bias_ref[jnp.minimum(i * bps + b, 2)]
            for kh in range(SWA_KVH):
                ls = slice(128 * kh, 128 * (kh + 1))
                kall = _swa_keys(kp_ref, kc_ref, km_ref, b, ls)
                vall = _swa_keys(vp_ref, vc_ref, vm_ref, b, ls)
                p, _ = _swa_probs(_swa_stack(q_ref, rows, kh, lo, BF16), kall, bias, sink_cols[kh])
                for g, pair in enumerate(_swa_unstack(_dot(p.astype(BF16), vall), lo)):
                    o_ref[rows, 128 * (2 * kh + g):128 * (2 * kh + g + 1)] = pair

    cur, prev, first = _swa_specs(rs, ns)
    bias = _swa_bias()
    return pl.pallas_call(
        body, name="swa_fwd", grid=(ns,),
        in_specs=[pl.BlockSpec(memory_space=pltpu.SMEM), _full(bias.shape), cur(512), prev(256), cur(256), first(256),
                  prev(256), cur(256), first(256)],
        out_specs=cur(512),
        out_shape=jax.ShapeDtypeStruct((t, SWA_W), F32),
        compiler_params=_cparams(1),
    )(sinks, bias, sq, sk, sk, sk, sv, sv, sv)


def _swa_bwd(sinks, sq, sk, sv, o, do, hook=None):
    t = sq.shape[0]
    rs = _seq_tile(t)
    bps, ns = rs // BLK, t // rs

    def body(sink_ref, bias_ref, q_ref, kp_ref, kc_ref, km_ref, vp_ref, vc_ref, vm_ref, o_ref, do_ref,
             dq_ref, dk_ref, dv_ref, dkm_ref, dvm_ref, dsink_ref, pk_ref, pv_ref):
        i = pl.program_id(0)

        @pl.when(i == 0)
        def _():
            pk_ref[...] = jnp.zeros_like(pk_ref)
            pv_ref[...] = jnp.zeros_like(pv_ref)
            dkm_ref[...] = jnp.zeros_like(dkm_ref)
            dvm_ref[...] = jnp.zeros_like(dvm_ref)
            dsink_ref[...] = jnp.zeros_like(dsink_ref)

        @pl.when(i == ns)
        def _():
            dk_ref[...] = pk_ref[...]
            dv_ref[...] = pv_ref[...]

        @pl.when(i < ns)
        def _():
            lo = lax.broadcasted_iota(jnp.int32, (BLK, 128), 1) < 64
            scale = SWA_HD ** -0.5
            sink_cols = [_swa_sink_col(sink_ref, kh) for kh in range(SWA_KVH)]
            parts_k = [[None] * SWA_KVH for _ in range(bps)]
            parts_v = [[None] * SWA_KVH for _ in range(bps)]
            dsinks = [jnp.zeros((1, 1), F32) for _ in range(SWA_QH)]
            for b in range(bps):
                rows = slice(b * BLK, (b + 1) * BLK)
                bias = bias_ref[jnp.minimum(i * bps + b, 2)]
                for kh in range(SWA_KVH):
                    ls = slice(128 * kh, 128 * (kh + 1))
                    kall = _swa_keys(kp_ref, kc_ref, km_ref, b, ls)
                    vall = _swa_keys(vp_ref, vc_ref, vm_ref, b, ls)
                    q4 = _swa_stack(q_ref, rows, kh, lo, BF16)
                    do4 = _swa_stack(do_ref, rows, kh, lo, F32)
                    p, psink = _swa_probs(q4, kall, bias, sink_cols[kh])
                    delta = jnp.sum(do4 * _swa_stack(o_ref, rows, kh, lo, F32), axis=-1, keepdims=True)
                    do4b = do4.astype(BF16)
                    ds = (p * (_dg(do4b, vall, NT) - delta) * scale).astype(BF16)
                    for g, pair in enumerate(_swa_unstack(_dot(ds, kall), lo)):
                        dq_ref[rows, 128 * (2 * kh + g):128 * (2 * kh + g + 1)] = pair
                    parts_k[b][kh] = _dg(ds, q4, TN)
                    parts_v[b][kh] = _dg(p.astype(BF16), do4b, TN)
                    dsk = psink * delta
                    for e in range(SWA_G):
                        h = SWA_G * kh + e
                        dsinks[h] = dsinks[h] - jnp.sum(dsk[e * BLK:(e + 1) * BLK], axis=0, keepdims=True)
            last = slice(rs - BLK, rs)
            for parts, out_ref, pend_ref, meta_ref in ((parts_k, dk_ref, pk_ref, dkm_ref),
                                                       (parts_v, dv_ref, pv_ref, dvm_ref)):
                for kh in range(SWA_KVH):
                    ls = slice(128 * kh, 128 * (kh + 1))
                    if bps > 1:
                        out_ref[0:rs - BLK, ls] = pend_ref[0:rs - BLK, ls]
                    out_ref[last, ls] = pend_ref[last, ls] + parts[0][kh][0:BLK]
                    meta = parts[0][kh][2 * BLK:3 * BLK]
                    for b in range(bps):
                        own = parts[b][kh][BLK:2 * BLK]
                        if b + 1 < bps:
                            own = own + parts[b + 1][kh][0:BLK]
                            meta = meta + parts[b + 1][kh][2 * BLK:3 * BLK]
                        pend_ref[b * BLK:(b + 1) * BLK, ls] = own
                    meta_ref[:, ls] += meta
            for h in range(SWA_QH):
                dsink_ref[h:h + 1, :] += jnp.broadcast_to(dsinks[h], (1, 128))

    cur, prev, first = _swa_specs(rs, ns)
    late = lambda w: pl.BlockSpec((rs, w), lambda i: (jnp.maximum(i - 1, 0), 0))
    bias = _swa_bias()
    return _pallas(
        body, name="swa_bwd", grid=(ns + 1,),
        in_specs=[pl.BlockSpec(memory_space=pltpu.SMEM), _full(bias.shape), cur(512), prev(256), cur(256), first(256),
                  prev(256), cur(256), first(256), cur(512), cur(512)],
        out_specs=[cur(512), late(256), late(256), first(256), first(256), _full((SWA_QH, 128))],
        out_shape=[jax.ShapeDtypeStruct((t, SWA_W), F32), jax.ShapeDtypeStruct((t, 256), F32),
                   jax.ShapeDtypeStruct((t, 256), F32), jax.ShapeDtypeStruct((BLK, 256), F32),
                   jax.ShapeDtypeStruct((BLK, 256), F32), jax.ShapeDtypeStruct((SWA_QH, 128), F32)],
        scratch_shapes=[pltpu.VMEM((rs, 256), F32), pltpu.VMEM((rs, 256), F32)],
        args=(sinks, bias, sq, sk, sk, sk, sv, sv, sv, o, do), hook=hook)


def _mix_out(h1, ogla, gg, oswa, wgn, wsn, wout, wpost):
    t = h1.shape[0]
    tm = _row_tile(t)

    def body(h_ref, og_ref, gg_ref, os_ref, wgn_ref, wsn_ref, wout_ref, wpost_ref, h2_ref, cat_ref, m_ref):
        parts = []
        for h in range(GLA_HEADS):
            ls = slice(GLA_DV * h, GLA_DV * (h + 1))
            y, _, _ = _rms(og_ref[:, ls], wgn_ref[...])
            g = gg_ref[:, ls]
            parts.append(y * (g * _sigmoid(g)))
        ys, _, _ = _rms(os_ref[...], wsn_ref[...])
        cat = jnp.concatenate(parts + [ys], axis=1).astype(BF16)
        cat_ref[...] = cat
        m = _dot(cat, wout_ref[...])
        m_ref[...] = m
        y, _, _ = _rms(m, wpost_ref[...])
        h2_ref[...] = h_ref[...] + y

    def row(w):
        return pl.BlockSpec((tm, w), lambda i: (i, 0))

    return pl.pallas_call(
        body, name="mix_out", grid=(t // tm,),
        in_specs=[row(D_MODEL), row(512), row(512), row(512), _full((1, GLA_DV)), _full((1, SWA_W)),
                  _full((D_MODEL, D_MODEL)), _full((1, D_MODEL))],
        out_specs=[row(D_MODEL), row(D_MODEL), row(D_MODEL)],
        out_shape=[jax.ShapeDtypeStruct((t, D_MODEL), F32), jax.ShapeDtypeStruct((t, D_MODEL), BF16),
                   jax.ShapeDtypeStruct((t, D_MODEL), F32)],
        compiler_params=_cparams(1),
    )(h1, ogla, gg, oswa, wgn, wsn, wout, wpost)


def _mix_out_bwd(dh2, m, ogla, gg, oswa, wgn, wsn, wout, wpost, hook=None):
    t = dh2.shape[0]
    tm = _row_tile(t)

    def body(dh_ref, m_ref, og_ref, gg_ref, os_ref, wgn_ref, wsn_ref, wout_ref, wpost_ref,
             dog_ref, dgg_ref, dos_ref, dm_ref, dwpost_ref, dwgn_ref, dwsn_ref):
        @pl.when(pl.program_id(0) == 0)
        def _():
            dwpost_ref[...] = jnp.zeros_like(dwpost_ref)
            dwgn_ref[...] = jnp.zeros_like(dwgn_ref)
            dwsn_ref[...] = jnp.zeros_like(dwsn_ref)

        wpost = wpost_ref[...]
        _, mh, r = _rms(m_ref[...], wpost)
        dm, dw = _rms_bwd(mh, r, wpost, dh_ref[...])
        dwpost_ref[...] += dw
        dmb = dm.astype(BF16)
        dm_ref[...] = dmb
        dcat = _dg(dmb, wout_ref[...], NT)
        wgn = wgn_ref[...]
        for h in range(GLA_HEADS):
            ls = slice(GLA_DV * h, GLA_DV * (h + 1))
            dog = dcat[:, ls]
            g = gg_ref[:, ls]
            sg = _sigmoid(g)
            y, xh, r = _rms(og_ref[:, ls], wgn)
            dgg_ref[:, ls] = dog * y * (sg * (1.0 + g * (1.0 - sg)))
            dx, dw = _rms_bwd(xh, r, wgn, dog * (g * sg))
            dog_ref[:, ls] = dx
            dwgn_ref[...] += dw
        wsn = wsn_ref[...]
        _, xh, r = _rms(os_ref[...], wsn)
        dx, dw = _rms_bwd(xh, r, wsn, dcat[:, GLA_W:])
        dos_ref[...] = dx
        dwsn_ref[...] += dw

    def row(w):
        return pl.BlockSpec((tm, w), lambda i: (i, 0))

    def rshape(w, dt=F32):
        return jax.ShapeDtypeStruct((t, w), dt)

    return _pallas(
        body, name="mix_out_bwd", grid=(t // tm,),
        in_specs=[row(D_MODEL), row(D_MODEL), row(512), row(512), row(512), _full((1, GLA_DV)), _full((1, SWA_W)),
                  _full((D_MODEL, D_MODEL)), _full((1, D_MODEL))],
        out_specs=[row(512), row(512), row(512), row(D_MODEL), _full((1, D_MODEL)), _full((1, GLA_DV)),
                   _full((1, SWA_W))],
        out_shape=[rshape(512), rshape(512), rshape(512), rshape(D_MODEL, BF16),
                   jax.ShapeDtypeStruct((1, D_MODEL), F32), jax.ShapeDtypeStruct((1, GLA_DV), F32),
                   jax.ShapeDtypeStruct((1, SWA_W), F32)],
        args=(dh2, m, ogla, gg, oswa, wgn, wsn, wout, wpost), hook=hook)


def _mix_in_bwd(dh2, h1, wmixpre, winp, wa2p, bap, cos, sin, ga, dgq, dgk, dgv, dgg, dla, dsq, dsk, dsv, dkm, dvm):
    t = h1.shape[0]
    tm = _row_tile(t)

    def body(dh2_ref, h_ref, w_ref, win_ref, wa2_ref, ba_ref, cos_ref, sin_ref, ga_ref, dgq_ref, dgk_ref, dgv_ref,
             dgg_ref, dla_ref, dsq_ref, dsk_ref, dsv_ref, dkm_ref, dvm_ref,
             dh1_ref, dproj_ref, dw_ref, dwa2_ref, dba_ref):
        i = pl.program_id(0)

        @pl.when(i == 0)
        def _():
            dw_ref[...] = jnp.zeros_like(dw_ref)
            dwa2_ref[...] = jnp.zeros_like(dwa2_ref)
            dba_ref[...] = jnp.zeros_like(dba_ref)

        first = (i == 0).astype(F32)
        c = cos_ref[...]
        s = -sin_ref[...]
        fh = _first_half_mask(tm)
        dproj_ref[:, P_GQ:P_GK] = dgq_ref[...].astype(BF16)
        dproj_ref[:, P_GK:P_GV] = dgk_ref[...].astype(BF16)
        dproj_ref[:, P_GV:P_GG] = dgv_ref[...].astype(BF16)
        dproj_ref[:, P_GG:P_GA] = dgg_ref[...].astype(BF16)
        gab = ga_ref[...].astype(BF16)
        z = _dot(gab, wa2_ref[...]) + ba_ref[...]
        row_id = i * tm + lax.broadcasted_iota(jnp.int32, (tm, 1), 0)
        dz = jnp.where(row_id >= PAD, dla_ref[...] * (1.0 / GLA_TAU) * (1.0 - _sigmoid(z)), 0.0)
        dzb = dz.astype(BF16)
        dba_ref[...] += jnp.sum(dz, axis=0, keepdims=True)
        dwa2_ref[...] += _dg(gab, dzb, TN)
        dproj_ref[:, P_GA:P_SQ] = _dg(dzb, wa2_ref[...], NT).astype(BF16)
        for k in range(4):
            dy = dsq_ref[:, 128 * k:128 * (k + 1)]
            dproj_ref[:, P_SQ + 128 * k:P_SQ + 128 * (k + 1)] = (dy * c + _rot_half(dy, fh) * s).astype(BF16)
        for k in range(2):
            ls = slice(128 * k, 128 * (k + 1))
            dy = dsk_ref[:, ls]
            dy = jnp.concatenate([dy[:BLK] + first * dkm_ref[:, ls], dy[BLK:]], axis=0) if tm > BLK else (
                dy + first * dkm_ref[:, ls])
            dproj_ref[:, P_SK + 128 * k:P_SK + 128 * (k + 1)] = (dy * c + _rot_half(dy, fh) * s).astype(BF16)
            dv = dsv_ref[:, ls]
            dv = jnp.concatenate([dv[:BLK] + first * dvm_ref[:, ls], dv[BLK:]], axis=0) if tm > BLK else (
                dv + first * dvm_ref[:, ls])
            dproj_ref[:, P_SV + 128 * k:P_SV + 128 * (k + 1)] = dv.astype(BF16)
        dn = _dg(dproj_ref[...], win_ref[...], NT)
        w = w_ref[...]
        _, hh, r = _rms(h_ref[...], w)
        dx, dw = _rms_bwd(hh, r, w, dn)
        dw_ref[...] += dw
        dh1_ref[...] = dh2_ref[...] + dx

    def row(w):
        return pl.BlockSpec((tm, w), lambda i: (i, 0))

    return pl.pallas_call(
        body, name="mix_in_bwd", grid=(t // tm,),
        in_specs=[row(D_MODEL), row(D_MODEL), _full((1, D_MODEL)), _full((D_MODEL, P_END)), _full((128, GLA_KW)),
                  _full((1, GLA_KW)), row(128), row(128), row(128), row(256), row(256), row(512), row(512), row(256),
                  row(512), row(256), row(256), _full((BLK, 256)), _full((BLK, 256))],
        out_specs=[row(D_MODEL), row(P_END), _full((1, D_MODEL)), _full((128, GLA_KW)), _full((1, GLA_KW))],
        out_shape=[jax.ShapeDtypeStruct((t, D_MODEL), F32), jax.ShapeDtypeStruct((t, P_END), BF16),
                   jax.ShapeDtypeStruct((1, D_MODEL), F32), jax.ShapeDtypeStruct((128, GLA_KW), F32),
                   jax.ShapeDtypeStruct((1, GLA_KW), F32)],
        compiler_params=_cparams(1),
    )(dh2, h1, wmixpre, winp, wa2p, bap, cos, sin, ga, dgq, dgk, dgv, dgg, dla, dsq, dsk, dsv, dkm, dvm)


def _adamw_update(w, g, m, v):
    m = ADAM_B1 * m + (1.0 - ADAM_B1) * g
    v = ADAM_B2 * v + (1.0 - ADAM_B2) * (g * g)
    m_hat = m / (1.0 - ADAM_B1 ** ADAM_STEP)
    v_hat = v / (1.0 - ADAM_B2 ** ADAM_STEP)
    return -ADAM_LR * (m_hat / (jnp.sqrt(v_hat) + ADAM_EPS) + ADAM_WD * w), m, v


def _adamw(w, g, m, v):
    r, c = w.shape
    tr = _div_tile(r)

    def body(w_ref, g_ref, m_ref, v_ref, d_ref, nm_ref, nv_ref):
        d_ref[...], nm_ref[...], nv_ref[...] = _adamw_update(w_ref[...], g_ref[...], m_ref[...], v_ref[...])

    spec = pl.BlockSpec((tr, c), lambda i: (i, 0))
    shape = jax.ShapeDtypeStruct((r, c), F32)
    return pl.pallas_call(
        body, name="adamw", grid=(r // tr,), in_specs=[spec] * 4, out_specs=[spec] * 3, out_shape=[shape] * 3,
        compiler_params=_cparams(1),
    )(w, g, m, v)


def _adamw_halves(w, g_mine, g_other, m, v, c_idx, row0=0):
    r, c = w.shape
    h = g_mine.shape[0]
    tr = _div_tile(math.gcd(r, h))
    nth = h // tr
    t0 = row0 // tr
    assert t0 * tr == row0

    def body(c_ref, w_ref, gm_ref, go_ref, m_ref, v_ref, g_ref, d_ref, nm_ref, nv_ref):
        hh = (t0 + pl.program_id(0)) // nth
        g = jnp.where(hh == c_ref[0], gm_ref[...], go_ref[...])
        g_ref[...] = g
        d_ref[...], nm_ref[...], nv_ref[...] = _adamw_update(w_ref[...], g, m_ref[...], v_ref[...])

    spec = pl.BlockSpec((tr, c), lambda i, c_ref: (i, 0))
    gspec = pl.BlockSpec((tr, c), lambda i, c_ref: ((t0 + i) % nth, 0))
    shape = jax.ShapeDtypeStruct((r, c), F32)
    return pl.pallas_call(
        body, name="adamw_halves",
        grid_spec=pltpu.PrefetchScalarGridSpec(
            num_scalar_prefetch=1, grid=(r // tr,), in_specs=[spec, gspec, gspec, spec, spec], out_specs=[spec] * 4),
        out_shape=[shape] * 4, compiler_params=_cparams(1),
    )(c_idx, w, g_mine, g_other, m, v)


def _place():
    x, y, c = lax.axis_index("x"), lax.axis_index("y"), lax.axis_index("c")
    chips = [(1 - x, y), (x, 1 - y), (1 - x, 1 - y)]
    return x, y, c, chips


def _remote(send_sem, recv_sem, src, dst, to):
    return pltpu.make_async_remote_copy(src_ref=src, dst_ref=dst, send_sem=send_sem, recv_sem=recv_sem,
                                        device_id=to, device_id_type=MESH)


def _half(ref_rows, c):
    h = ref_rows // 2
    return pl.ds(pl.multiple_of(c * h, 8), h)


def _own_slot(shard, q):
    return lax.dynamic_update_slice(jnp.zeros((N_CHIPS,) + shard.shape, shard.dtype), shard[None], (q, 0, 0))


class _GatherChips:
    has_mid = True

    def __init__(self, bufs):
        n = len(bufs)
        self.inputs = list(bufs)
        self.out_shape = [jax.ShapeDtypeStruct(b.shape, b.dtype) for b in bufs]
        self.aliases = [(t, t) for t in range(n)]
        self.scratch = [pltpu.SemaphoreType.DMA((n, 6)), pltpu.SemaphoreType.DMA((n, 6))]

    def start(self, ins, outs, scr):
        send, recv = scr
        x, y, c, chips = _place()
        q = 2 * x + y
        for t, (i_ref, o_ref) in enumerate(zip(ins, outs)):
            rows = _half(i_ref.shape[1], c)
            for j, (cx, cy) in enumerate(chips):
                _remote(send.at[t, j], recv.at[t, j], i_ref.at[q, rows], o_ref.at[q, rows], (cx, cy, c)).start()

    def mid(self, ins, outs, scr):
        send, recv = scr
        x, y, c, chips = _place()
        for t, o_ref in enumerate(outs):
            rows = _half(o_ref.shape[1], c)
            for j, (cx, cy) in enumerate(chips):
                slot = o_ref.at[2 * cx + cy, rows]
                _remote(send.at[t, j], recv.at[t, j], slot, slot, (cx, cy, c)).wait_recv()
                _remote(send.at[t, 3 + j], recv.at[t, 3 + j], slot, slot, (x, y, 1 - c)).start()

    def finish(self, ins, outs, scr):
        send, recv = scr
        x, y, c, chips = _place()
        for t, o_ref in enumerate(outs):
            mine, other = _half(o_ref.shape[1], c), _half(o_ref.shape[1], 1 - c)
            for j, (cx, cy) in enumerate(chips):
                slot = o_ref.at[2 * cx + cy, other]
                _remote(send.at[t, 3 + j], recv.at[t, 3 + j], slot, slot, (x, y, 1 - c)).wait_recv()
            for j, (cx, cy) in enumerate(chips):
                sent = o_ref.at[2 * cx + cy, mine]
                _remote(send.at[t, j], recv.at[t, j], sent, sent, (cx, cy, c)).wait_send()
                _remote(send.at[t, 3 + j], recv.at[t, 3 + j], sent, sent, (x, y, 1 - c)).wait_send()


class _PairExchange:
    has_mid = False
    aliases = ()

    def __init__(self, arrs):
        n = len(arrs)
        self.inputs = list(arrs)
        self.out_shape = [jax.ShapeDtypeStruct((a.shape[0], a.shape[1] // 2, a.shape[2]), a.dtype) for a in arrs]
        self.scratch = [pltpu.SemaphoreType.DMA((n,)), pltpu.SemaphoreType.DMA((n,))]

    def _copies(self, ins, outs, scr):
        send, recv = scr
        x, y, c, _ = _place()
        return [_remote(send.at[t], recv.at[t], i_ref.at[:, _half(i_ref.shape[1], 1 - c)], o_ref, (x, y, 1 - c))
                for t, (i_ref, o_ref) in enumerate(zip(ins, outs))]

    def start(self, ins, outs, scr):
        for cp in self._copies(ins, outs, scr):
            cp.start()

    def finish(self, ins, outs, scr):
        for cp in self._copies(ins, outs, scr):
            cp.wait()


class _ChipScatter:
    has_mid = False
    aliases = ()

    def __init__(self, arrs):
        n = len(arrs)
        self.inputs = list(arrs)
        self.out_shape = [jax.ShapeDtypeStruct((3,) + a.shape[1:], a.dtype) for a in arrs]
        self.scratch = [pltpu.SemaphoreType.DMA((n, 3)), pltpu.SemaphoreType.DMA((n, 3))]

    def _copies(self, ins, outs, scr):
        send, recv = scr
        x, y, c, chips = _place()
        return [_remote(send.at[t, j], recv.at[t, j], i_ref.at[2 * cx + cy], o_ref.at[j], (cx, cy, c))
                for t, (i_ref, o_ref) in enumerate(zip(ins, outs)) for j, (cx, cy) in enumerate(chips)]

    def start(self, ins, outs, scr):
        for cp in self._copies(ins, outs, scr):
            cp.start()

    def finish(self, ins, outs, scr):
        for cp in self._copies(ins, outs, scr):
            cp.wait()


class _PairShare:
    has_mid = False
    aliases = ()

    def __init__(self, arrs):
        n = len(arrs)
        self.inputs = list(arrs)
        self.out_shape = [jax.ShapeDtypeStruct(a.shape, a.dtype) for a in arrs]
        self.scratch = [pltpu.SemaphoreType.DMA((n,)), pltpu.SemaphoreType.DMA((n,))]

    def _copies(self, ins, outs, scr):
        send, recv = scr
        x, y, c, _ = _place()
        return [_remote(send.at[t], recv.at[t], i_ref, o_ref, (x, y, 1 - c))
                for t, (i_ref, o_ref) in enumerate(zip(ins, outs))]

    def start(self, ins, outs, scr):
        for cp in self._copies(ins, outs, scr):
            cp.start()

    def finish(self, ins, outs, scr):
        for cp in self._copies(ins, outs, scr):
            cp.wait()


def _comm_call(hook, name):
    n_in, n_out = len(hook.inputs), len(hook.out_shape)

    def body(*refs):
        ins, outs, scr = refs[:n_in], refs[n_in:n_in + n_out], refs[n_in + n_out:]
        hook.start(ins, outs, scr)
        if hook.has_mid:
            hook.mid(ins, outs, scr)
        hook.finish(ins, outs, scr)

    return pl.pallas_call(body, name=name, in_specs=[ANY] * n_in, out_specs=[ANY] * n_out,
                          out_shape=list(hook.out_shape), scratch_shapes=list(hook.scratch),
                          input_output_aliases=dict(hook.aliases))(*hook.inputs)


def _all_gather_devices(vecs):
    n = len(vecs)

    def body(*refs):
        x_refs, out_refs = refs[:n], refs[n:2 * n]
        send_sems, recv_sems, local_sems = refs[2 * n:]
        x, y, c, chips = _place()
        me, sibling = (x, y, c), (x, y, 1 - c)
        waits = []
        for t, (x_ref, out_ref) in enumerate(zip(x_refs, out_refs)):
            def slot(px, py, pc, out_ref=out_ref):
                return out_ref.at[4 * px + 2 * py + pc]

            def copy(k, block, to, src=None, t=t, slot=slot):
                return pltpu.make_async_remote_copy(
                    src_ref=slot(*block) if src is None else src, dst_ref=slot(*block), send_sem=send_sems.at[t, k],
                    recv_sem=recv_sems.at[t, k], device_id=to, device_id_type=MESH)

            mine = pltpu.make_async_copy(x_ref, slot(*me), local_sems.at[t])
            mine.start()
            first = [copy(0, me, sibling, src=x_ref)]
            first += [copy(1 + j, me, (*chip, c), src=x_ref) for j, chip in enumerate(chips)]
            for cp in first:
                cp.start()
            waits.append((copy, mine, first))
        for copy, mine, first in waits:
            passed = [copy(4 + j, (*chip, c), sibling) for j, chip in enumerate(chips)]
            for j, chip in enumerate(chips):
                copy(1 + j, (*chip, c), me).wait_recv()
                passed[j].start()
            copy(0, sibling, me).wait_recv()
            for j, chip in enumerate(chips):
                copy(4 + j, (*chip, 1 - c), me).wait_recv()
            for cp in first + passed:
                cp.wait_send()
            mine.wait()

    vmem = pl.BlockSpec(memory_space=pltpu.VMEM)
    return pl.pallas_call(
        body, name="all_gather_devices", in_specs=[vmem] * n, out_specs=[vmem] * n,
        out_shape=[jax.ShapeDtypeStruct((N_DEV,) + v.shape, v.dtype) for v in vecs],
        scratch_shapes=[pltpu.SemaphoreType.DMA((n, 7)), pltpu.SemaphoreType.DMA((n, 7)),
                        pltpu.SemaphoreType.DMA((n,))],
    )(*vecs)


def _pair_sum(g, other, c_idx):
    nq, r, w = g.shape
    h = r // 2
    tr = _div_tile(h)
    nt = h // tr

    def body(c_ref, g_ref, o_ref, s_ref):
        s_ref[...] = (g_ref[...].astype(F32) + o_ref[...].astype(F32)).astype(s_ref.dtype)

    return pl.pallas_call(
        body, name="pair_sum",
        grid_spec=pltpu.PrefetchScalarGridSpec(
            num_scalar_prefetch=1, grid=(nq, nt),
            in_specs=[pl.BlockSpec((None, tr, w), lambda k, i, c_ref: (k, c_ref[0] * nt + i, 0)),
                      pl.BlockSpec((None, tr, w), lambda k, i, c_ref: (k, i, 0))],
            out_specs=pl.BlockSpec((None, tr, w), lambda k, i, c_ref: (k, i, 0))),
        out_shape=jax.ShapeDtypeStruct((nq, h, w), g.dtype),
        compiler_params=_cparams(2),
    )(c_idx, g, other)


def _chip_sum(s, others, q_idx):
    _, h, w = s.shape
    tr = _div_tile(h)

    def body(q_ref, s_ref, o_ref, out_ref):
        out_ref[...] = ((s_ref[...].astype(F32) + o_ref[0].astype(F32)) + o_ref[1].astype(F32)) + o_ref[2].astype(F32)

    return pl.pallas_call(
        body, name="chip_sum",
        grid_spec=pltpu.PrefetchScalarGridSpec(
            num_scalar_prefetch=1, grid=(h // tr,),
            in_specs=[pl.BlockSpec((None, tr, w), lambda i, q_ref: (q_ref[0], i, 0)),
                      pl.BlockSpec((3, tr, w), lambda i, q_ref: (0, i, 0))],
            out_specs=pl.BlockSpec((tr, w), lambda i, q_ref: (i, 0))),
        out_shape=jax.ShapeDtypeStruct((h, w), F32),
        compiler_params=_cparams(1),
    )(q_idx, s, others)


def _small_update(q_idx, parts, ws, ms, vs, col_block):
    n = len(parts)
    has_w = [w is not None for w in ws]

    def body(q_ref, *refs):
        pos = 0
        ins = []
        for t in range(n):
            k = 4 if has_w[t] else 1
            ins.append(refs[pos:pos + k])
            pos += k
        outs = refs[pos:]
        opos = 0
        for t in range(n):
            p_ref = ins[t][0]
            g = p_ref[0]
            for s in range(1, p_ref.shape[0]):
                g = g + p_ref[s]
            if has_w[t]:
                _, w_ref, m_ref, v_ref = ins[t]
                g_ref, d_ref, nm_ref, nv_ref = outs[opos:opos + 4]
                opos += 4
                g_ref[...] = g
                d_ref[...], nm_ref[...], nv_ref[...] = _adamw_update(w_ref[...], g, m_ref[...], v_ref[...])
            else:
                outs[opos][...] = g
                opos += 1

    def whole(shape):
        nd = len(shape)
        return pl.BlockSpec(shape, lambda i, q_ref: (0,) * nd)

    in_specs, out_specs, out_shape, args = [], [], [], []
    for t in range(n):
        k, r, wf = parts[t].shape
        if col_block[t]:
            w = wf // N_CHIPS
            in_specs.append(pl.BlockSpec((k, r, w), lambda i, q_ref: (0, 0, q_ref[0])))
        else:
            w = wf
            in_specs.append(whole((k, r, wf)))
        args.append(parts[t])
        if has_w[t]:
            assert ws[t].shape == (r, w), (ws[t].shape, r, w)
            in_specs += [whole((r, w))] * 3
            args += [ws[t], ms[t], vs[t]]
            out_specs += [whole((r, w))] * 4
            out_shape += [jax.ShapeDtypeStruct((r, w), F32)] * 4
        else:
            out_specs.append(whole((r, w)))
            out_shape.append(jax.ShapeDtypeStruct((r, w), F32))
    return pl.pallas_call(
        body, name="small_update",
        grid_spec=pltpu.PrefetchScalarGridSpec(num_scalar_prefetch=1, grid=(1,), in_specs=in_specs,
                                               out_specs=out_specs),
        out_shape=out_shape, compiler_params=_cparams(1),
    )(q_idx, *args)


def _pack_win(w_in):
    o = np.cumsum((0,) + IN_SPLITS)
    gq, gk, gv, gg, ga, sq, sk, sv = [w_in[:, o[i]:o[i + 1]] for i in range(8)]
    z = jnp.zeros((w_in.shape[0], 128 - GLA_RANK), w_in.dtype)
    dup = lambda a: jnp.concatenate([a[:, :64], a[:, :64], a[:, 64:], a[:, 64:]], axis=1)
    return jnp.concatenate([gq, gk, gv, gg, ga, z, sq, dup(sk), dup(sv)], axis=1)


def _unpack_dwin(d):
    und = lambda a: jnp.concatenate([a[:, 0:64] + a[:, 64:128], a[:, 128:192] + a[:, 192:256]], axis=1)
    return jnp.concatenate([d[:, :P_GA], d[:, P_GA:P_GA + GLA_RANK], d[:, P_SQ:P_SK], und(d[:, P_SK:P_SV]),
                            und(d[:, P_SV:P_END])], axis=1)


def _local_step(x, target, meta, p):
    s = x.shape[0]
    t = s + BLK
    h0 = jnp.concatenate([jnp.zeros((PAD, D_MODEL), F32), meta, x], axis=0)
    cos, sin = _rope_tables(t)

    h1, n1, g1, u1, a1, f1 = _ffn_fwd(h0, p["ffn1_pre_norm"], p["ffn1_w_gate"], p["ffn1_w_up"], p["ffn1_w_down"],
                                      p["ffn1_post_norm"])
    n2, gq, gk, gv, gg, ga, la, sq, sk, sv = _mix_proj(h1, p["mix_pre_norm"], p["w_in"], p["gla_w_a2"], p["gla_b_a"],
                                                       cos, sin)
    ogla, ss = _gla_fwd(gq, gk, gv, la)
    oswa = _swa_fwd(p["swa_sinks"], sq, sk, sv)
    h2, cat, m = _mix_out(h1, ogla, gg, oswa, p["gla_out_norm"], p["swa_out_norm"], p["w_out"], p["mix_post_norm"])
    dy, n3, g3, u3, a3, f3, sse = _ffn_fwd(h2, p["ffn2_pre_norm"], p["ffn2_w_gate"], p["ffn2_w_up"],
                                           p["ffn2_w_down"], p["ffn2_post_norm"], target=target)

    grads = {}
    dh2, df3, dg3, du3, grads["ffn2_pre_norm"], grads["ffn2_post_norm"] = _ffn_bwd(
        dy, h2, f3, g3, u3, p["ffn2_pre_norm"], p["ffn2_w_gate"], p["ffn2_w_up"], p["ffn2_w_down"],
        p["ffn2_post_norm"])
    (gud,) = _ffn_wgrad(n3, df3, dg3, du3, a3)
    grads["ffn2_w_gate"], grads["ffn2_w_up"], grads["ffn2_w_down"] = gud[:, :FJ], gud[:, FJ:2 * FJ], gud[:, 2 * FJ:]

    dogla, dgg, doswa, dm, grads["mix_post_norm"], grads["gla_out_norm"], grads["swa_out_norm"] = _mix_out_bwd(
        dh2, m, ogla, gg, oswa, p["gla_out_norm"], p["swa_out_norm"], p["w_out"], p["mix_post_norm"])
    grads["w_out"] = _xty(cat, dm)
    dsq, dsk, dsv, dkm, dvm, dsinks = _swa_bwd(p["swa_sinks"], sq, sk, sv, oswa, doswa)
    grads["swa_sinks"] = dsinks[:, 0]
    dgq, dgk, dgv, dla = _gla_bwd(gq, gk, gv, la, ss, dogla)
    dh1, dproj, grads["mix_pre_norm"], dwa2p, grads["gla_b_a"] = _mix_in_bwd(
        dh2, h1, p["mix_pre_norm"], p["w_in"], p["gla_w_a2"], p["gla_b_a"], cos, sin, ga, dgq, dgk, dgv, dgg, dla,
        dsq, dsk, dsv, dkm, dvm)
    grads["gla_w_a2"] = dwa2p[:GLA_RANK]
    grads["w_in"] = _unpack_dwin(_xty(n2, dproj))

    dh0, df1, dg1, du1, grads["ffn1_pre_norm"], grads["ffn1_post_norm"] = _ffn_bwd(
        dh1, h0, f1, g1, u1, p["ffn1_pre_norm"], p["ffn1_w_gate"], p["ffn1_w_up"], p["ffn1_w_down"],
        p["ffn1_post_norm"])
    (gud,) = _ffn_wgrad(n1, df1, dg1, du1, a1)
    grads["ffn1_w_gate"], grads["ffn1_w_up"], grads["ffn1_w_down"] = gud[:, :FJ], gud[:, FJ:2 * FJ], gud[:, 2 * FJ:]
    grads["meta_tokens"] = dh0[PAD:BLK]
    return sse[0, 0], dh0[BLK:], grads


WEIGHTS = ['meta_tokens', 'ffn1_pre_norm', 'ffn1_w_gate', 'ffn1_w_up', 'ffn1_w_down', 'ffn1_post_norm',
           'mix_pre_norm', 'w_in', 'gla_w_a2', 'gla_b_a', 'gla_out_norm', 'swa_sinks', 'swa_out_norm', 'w_out',
           'mix_post_norm', 'ffn2_pre_norm', 'ffn2_w_gate', 'ffn2_w_up', 'ffn2_w_down', 'ffn2_post_norm']
BIG = ['ffn1_w_gate', 'ffn1_w_up', 'ffn1_w_down', 'w_in', 'w_out', 'ffn2_w_gate', 'ffn2_w_up', 'ffn2_w_down']
SMALL = [n for n in WEIGHTS if n not in BIG]
FJ = D_FF // N_CHIPS
D_IN_J = D_IN // N_CHIPS
D_OUT_J = D_MODEL // N_CHIPS
TRANSPOSED = ('ffn1_w_gate', 'ffn1_w_up', 'ffn2_w_gate', 'ffn2_w_up')


def _shard2d(name, a):
    return a[0].T if name in TRANSPOSED else a[0]


def _unshard2d(name, a):
    return (a.T if name in TRANSPOSED else a)[None]


def _small_rows(name, a):
    flat = a.reshape(-1)
    rows = -(-flat.shape[0] // 1024) * 8
    return jnp.pad(flat, (0, rows * 128 - flat.shape[0])).reshape(rows, 128)


def kernel(x, meta_tokens, ffn1_pre_norm, ffn1_w_gate, ffn1_w_up, ffn1_w_down, ffn1_post_norm, mix_pre_norm, w_in, gla_w_a2, gla_b_a, gla_out_norm, swa_sinks, swa_out_norm, w_out, mix_post_norm, ffn2_pre_norm, ffn2_w_gate, ffn2_w_up, ffn2_w_down, ffn2_post_norm, loss_target, m_meta_tokens, m_ffn1_pre_norm, m_ffn1_w_gate, m_ffn1_w_up, m_ffn1_w_down, m_ffn1_post_norm, m_mix_pre_norm, m_w_in, m_gla_w_a2, m_gla_b_a, m_gla_out_norm, m_swa_sinks, m_swa_out_norm, m_w_out, m_mix_post_norm, m_ffn2_pre_norm, m_ffn2_w_gate, m_ffn2_w_up, m_ffn2_w_down, m_ffn2_post_norm, v_meta_tokens, v_ffn1_pre_norm, v_ffn1_w_gate, v_ffn1_w_up, v_ffn1_w_down, v_ffn1_post_norm, v_mix_pre_norm, v_w_in, v_gla_w_a2, v_gla_b_a, v_gla_out_norm, v_swa_sinks, v_swa_out_norm, v_w_out, v_mix_post_norm, v_ffn2_pre_norm, v_ffn2_w_gate, v_ffn2_w_up, v_ffn2_w_down, v_ffn2_post_norm):
    args = dict(locals())
    w = {n: args[n] for n in WEIGHTS}
    mom = {n: args["m_" + n] for n in WEIGHTS}
    var = {n: args["v_" + n] for n in WEIGHTS}
    cx, cy, cc = lax.axis_index("x"), lax.axis_index("y"), lax.axis_index("c")
    q_idx = (2 * cx + cy).astype(jnp.int32).reshape(1)
    c_idx = cc.astype(jnp.int32).reshape(1)

    q_chip = 2 * cx + cy
    bf = {n: _own_slot(_shard2d(n, w[n]).astype(BF16), q_chip) for n in BIG}
    early = _GatherChips([bf["ffn1_w_gate"], bf["ffn1_w_up"], bf["ffn1_w_down"], _own_slot(w["meta_tokens"], q_chip),
                          _own_slot(w["gla_w_a2"].reshape(GLA_RANK, GLA_KW // N_CHIPS), q_chip)])
    wg1, wu1, wd1, meta4, wa24 = _comm_call(early, "gather_ffn1")
    meta_full = meta4.transpose(1, 0, 2).reshape(N_META, D_MODEL)
    wa2p = jnp.pad(wa24.transpose(1, 0, 2).reshape(GLA_RANK, GLA_KW), ((0, 128 - GLA_RANK), (0, 0))).astype(BF16)
    sinks = w["swa_sinks"].reshape(SWA_QH)

    seq, target = x[0], loss_target[0]
    t = seq.shape[0] + BLK
    h0 = jnp.concatenate([jnp.zeros((PAD, D_MODEL), F32), meta_full, seq], axis=0)
    cos, sin = _rope_tables(t)
    late = _GatherChips([bf["w_in"], bf["w_out"], bf["ffn2_w_gate"], bf["ffn2_w_up"], bf["ffn2_w_down"]])
    (h1, n1, g1, u1, a1, f1), (win4, wout4, wg2, wu2, wd2) = _ffn_fwd(
        h0, w["ffn1_pre_norm"], wg1, wu1, wd1, w["ffn1_post_norm"], hook=late)
    winp = _pack_win(win4.transpose(1, 0, 2).reshape(D_MODEL, D_IN))
    wout = wout4.reshape(D_MODEL, D_MODEL)
    n2, gq, gk, gv, gg, ga, la, sq, sk, sv = _mix_proj(h1, w["mix_pre_norm"], winp, wa2p, w["gla_b_a"], cos, sin)
    ogla, ss = _gla_fwd(gq, gk, gv, la)
    oswa = _swa_fwd(sinks, sq, sk, sv)
    h2, cat, m = _mix_out(h1, ogla, gg, oswa, w["gla_out_norm"], w["swa_out_norm"], wout, w["mix_post_norm"])
    dy, n3, g3, u3, a3, f3, sse = _ffn_fwd(h2, w["ffn2_pre_norm"], wg2, wu2, wd2, w["ffn2_post_norm"], target=target)
    loss = lax.psum(sse[0, 0] * (0.5 / D_MODEL), ("x", "y", "c"))

    g = {}
    dh2, df3, dg3, du3, g["ffn2_pre_norm"], g["ffn2_post_norm"] = _ffn_bwd(
        dy, h2, f3, g3, u3, w["ffn2_pre_norm"], wg2, wu2, wd2, w["ffn2_post_norm"])
    (gf2,) = _ffn_wgrad(n3, df3, dg3, du3, a3)
    (dogla, dgg, doswa, dm, g["mix_post_norm"], g["gla_out_norm"], g["swa_out_norm"]), (rgf2,) = _mix_out_bwd(
        dh2, m, ogla, gg, oswa, w["gla_out_norm"], w["swa_out_norm"], wout, w["mix_post_norm"],
        hook=_PairExchange([gf2]))
    sgf2 = _pair_sum(gf2, rgf2, c_idx)
    gout = _xty(cat, dm).reshape(N_CHIPS, D_OUT_J, D_MODEL).astype(BF16)
    (dsq, dsk, dsv, dkm, dvm, dsinks), (ogf2,) = _swa_bwd(sinks, sq, sk, sv, oswa, doswa,
                                                          hook=_ChipScatter([sgf2]))
    g["swa_sinks"] = dsinks
    dgq, dgk, dgv, dla = _gla_bwd(gq, gk, gv, la, ss, dogla)
    dh1, dproj, g["mix_pre_norm"], dwa2p, g["gla_b_a"] = _mix_in_bwd(
        dh2, h1, w["mix_pre_norm"], winp, wa2p, w["gla_b_a"], cos, sin, ga, dgq, dgk, dgv, dgg, dla,
        dsq, dsk, dsv, dkm, dvm)
    g["gla_w_a2"] = dwa2p[:GLA_RANK]
    gin = _unpack_dwin(_xty(n2, dproj)).reshape(D_MODEL, N_CHIPS, D_IN_J).transpose(1, 0, 2).astype(BF16)
    (dh0, df1, dg1, du1, g["ffn1_pre_norm"], g["ffn1_post_norm"]), (rgin, rgout) = _ffn_bwd(
        dh1, h0, f1, g1, u1, w["ffn1_pre_norm"], wg1, wu1, wd1, w["ffn1_post_norm"],
        hook=_PairExchange([gin, gout]))
    sgin, sgout = _pair_sum(gin, rgin, c_idx), _pair_sum(gout, rgout, c_idx)
    (gf1,), (ogin, ogout) = _ffn_wgrad(n1, df1, dg1, du1, a1, hook=_ChipScatter([sgin, sgout]))
    g["meta_tokens"] = dh0[PAD:BLK]
    grad_x = dh0[BLK:]
    (rgf1,) = _comm_call(_PairExchange([gf1]), "pair_exchange_ffn1")
    sgf1 = _pair_sum(gf1, rgf1, c_idx)
    (ogf1,) = _comm_call(_ChipScatter([sgf1]), "chip_scatter_ffn1")
    halves = [_chip_sum(s, o, q_idx) for s, o in ((sgf1, ogf1), (sgin, ogin), (sgout, ogout), (sgf2, ogf2))]
    others = _comm_call(_PairShare(halves), "pair_share")
    reduced = {"ffn1_w_gate": (0, 0), "ffn1_w_up": (0, FJ), "ffn1_w_down": (0, 2 * FJ), "w_in": (1, 0),
               "w_out": (2, 0), "ffn2_w_gate": (3, 0), "ffn2_w_up": (3, FJ), "ffn2_w_down": (3, 2 * FJ)}
    grad, delta, new_m, new_v = {}, {}, {}, {}
    for n in BIG:
        k, row0 = reduced[n]
        outs = _adamw_halves(_shard2d(n, w[n]), halves[k], others[k], _shard2d(n, mom[n]), _shard2d(n, var[n]),
                             c_idx, row0)
        grad[n], delta[n], new_m[n], new_v[n] = [_unshard2d(n, a) for a in outs]

    late = ["gla_w_a2", "swa_sinks"]
    direct = [n for n in SMALL if n not in late]
    names = direct + late
    gathered = _all_gather_devices([g[n] for n in names])
    mat = lambda a: a.reshape(a.shape[-2:])
    none2 = [None] * len(late)
    outs = _small_update(q_idx, gathered, [mat(w[n]) for n in direct] + none2, [mat(mom[n]) for n in direct] + none2,
                         [mat(var[n]) for n in direct] + none2, [n == "meta_tokens" for n in names])
    sum_a2, sum_sinks = outs[4 * len(direct):]
    g_late = [lax.dynamic_slice_in_dim(sum_a2, q_chip * (GLA_KW // N_CHIPS), GLA_KW // N_CHIPS, axis=1)[None],
              sum_sinks[:, 0].reshape(1, 1, SWA_QH)]
    outs = list(outs[:4 * len(direct)]) + list(_small_update(
        q_idx, g_late, [mat(w[n]) for n in late], [mat(mom[n]) for n in late], [mat(var[n]) for n in late],
        [False, False]))
    for k, n in enumerate(names):
        grad[n], delta[n], new_m[n], new_v[n] = [a.reshape(w[n].shape) for a in outs[4 * k:4 * k + 4]]

    return (loss, grad_x[None], *[grad[n] for n in WEIGHTS], *[delta[n] for n in WEIGHTS],
            *[new_m[n] for n in WEIGHTS], *[new_v[n] for n in WEIGHTS])
```

```python
import functools
import math

import numpy as np
import jax
import jax.numpy as jnp
from jax import lax
from jax.experimental import pallas as pl
from jax.experimental.pallas import tpu as pltpu

F32 = jnp.float32
BF16 = jnp.bfloat16
MESH = pl.DeviceIdType.MESH

D_MODEL = 1024
D_FF = 2816
N_CHIPS = 4
N_DEV = 8
N_META = 16
BLK = 128
PAD = BLK - N_META
GLA_CHUNK = 64
GLA_HEADS = 4
GLA_DV = 128
GLA_DK = 64
GLA_KW = GLA_HEADS * GLA_DK
GLA_W = GLA_HEADS * GLA_DV
GLA_RANK = 16
GLA_TAU = 16.0
SWA_HD = 64
SWA_QH = 8
SWA_KVH = 2
SWA_W = SWA_QH * SWA_HD
WINDOW = 128
ROPE_THETA = 10000.0
EPS = 1e-6
NEG_INF = -1e30
IN_SPLITS = (256, 256, 512, 512, 16, 512, 128, 128)
D_IN = sum(IN_SPLITS)
P_GQ, P_GK, P_GV, P_GG, P_GA, P_SQ, P_SK, P_SV, P_END = 0, 256, 512, 1024, 1536, 1664, 2176, 2432, 2688
ADAM_LR, ADAM_B1, ADAM_B2, ADAM_EPS, ADAM_WD, ADAM_STEP = 0.001, 0.9, 0.999, 1e-08, 0.01, 10
VMEM_LIMIT = 56 * 1024 * 1024

NT = (((1,), (1,)), ((), ()))
TN = (((0,), (0,)), ((), ()))


def _cparams(n_axes):
    return pltpu.CompilerParams(dimension_semantics=("arbitrary",) * n_axes, vmem_limit_bytes=VMEM_LIMIT)


def _row_tile(t):
    for tm in (640, 512, 384, 256, 128):
        if t % tm == 0:
            return tm
    raise ValueError(t)


SEQ_BLOCKS_PER_STEP = 5


def _seq_tile(t):
    return SEQ_BLOCKS_PER_STEP * BLK if t % (SEQ_BLOCKS_PER_STEP * BLK) == 0 else BLK


ROW_PARTS = 2


def _row_parts(tm):
    n = ROW_PARTS if tm % (16 * ROW_PARTS) == 0 else 1
    return [slice(k * (tm // n), (k + 1) * (tm // n)) for k in range(n)]


def _contract_tile(t):
    return 1664 if t % 1664 == 0 else _row_tile(t)


def _div_tile(r, cap=512):
    best = None
    for tr in range(8, min(r, cap) + 1, 8):
        if r % tr == 0:
            best = tr
    return best if best is not None else r


def _dot(a, b):
    return jnp.dot(a, b, preferred_element_type=F32)


def _dg(a, b, dims):
    return lax.dot_general(a, b, dims, preferred_element_type=F32)


def _rms(x, w):
    r = lax.rsqrt(jnp.mean(x * x, axis=-1, keepdims=True) + EPS)
    xh = x * r
    return xh * w, xh, r


def _rms_bwd(xh, r, w, dy):
    wdy = dy * w
    dx = r * (wdy - xh * jnp.mean(wdy * xh, axis=-1, keepdims=True))
    dw = jnp.sum(dy * xh, axis=0, keepdims=True)
    return dx, dw


def _sigmoid(x):
    return 1.0 / (1.0 + jnp.exp(-x))


def _full(shape):
    nd = len(shape)
    return pl.BlockSpec(shape, lambda *_: (0,) * nd)


ANY = pl.BlockSpec(memory_space=pl.ANY)


def _pallas(body, *, name, grid, in_specs, out_specs, out_shape, args, scratch_shapes=(), hook=None):
    n_axes = len(grid)
    if hook is None:
        return pl.pallas_call(body, name=name, grid=grid, in_specs=list(in_specs), out_specs=list(out_specs),
                              out_shape=list(out_shape), scratch_shapes=list(scratch_shapes),
                              compiler_params=_cparams(n_axes))(*args)
    n_in, n_out, n_scr = len(in_specs), len(out_specs), len(scratch_shapes)
    h_in, h_out = len(hook.inputs), len(hook.out_shape)
    total = math.prod(grid)

    def wrapped(*refs):
        ins, hins = refs[:n_in], refs[n_in:n_in + h_in]
        o0 = n_in + h_in
        outs, houts = refs[o0:o0 + n_out], refs[o0 + n_out:o0 + n_out + h_out]
        s0 = o0 + n_out + h_out
        scr, hscr = refs[s0:s0 + n_scr], refs[s0 + n_scr:]
        step = pl.program_id(0)
        for a in range(1, n_axes):
            step = step * grid[a] + pl.program_id(a)

        @pl.when(step == 0)
        def _():
            hook.start(hins, houts, hscr)

        body(*ins, *outs, *scr)

        if hook.has_mid:
            @pl.when(step == (3 * total) // 4)
            def _():
                hook.mid(hins, houts, hscr)

        @pl.when(step == total - 1)
        def _():
            hook.finish(hins, houts, hscr)

    res = pl.pallas_call(
        wrapped, name=name, grid=grid, in_specs=list(in_specs) + [ANY] * h_in,
        out_specs=list(out_specs) + [ANY] * h_out, out_shape=list(out_shape) + list(hook.out_shape),
        scratch_shapes=list(scratch_shapes) + list(hook.scratch), compiler_params=_cparams(n_axes),
        input_output_aliases={n_in + a: n_out + b for a, b in hook.aliases},
    )(*args, *hook.inputs)
    return res[:n_out], res[n_out:]


def _ffn_fwd(h, wpre, wg4, wu4, wd4, wpost, hook=None, target=None):
    t = h.shape[0]
    tm = _row_tile(t)
    nj, fj, _ = wg4.shape
    nblk = tm // BLK if target is not None else 0

    def body(*refs):
        h_ref, wpre_ref, wg_ref, wu_ref, wd_ref, wpost_ref = refs[:6]
        t_refs = refs[6:6 + nblk]
        hout_ref, n_ref, p1_ref, p2_ref, a_ref, f_ref = refs[6 + nblk:12 + nblk]
        acc_ref = refs[-1]
        i = pl.program_id(0)
        j = pl.program_id(1)

        @pl.when(j == 0)
        def _():
            y, _, _ = _rms(h_ref[...], wpre_ref[...])
            n_ref[...] = y.astype(BF16)
            acc_ref[...] = jnp.zeros_like(acc_ref)

        if target is not None:
            sse_ref = refs[12 + nblk]

            @pl.when((i == 0) & (j == 0))
            def _():
                sse_ref[...] = jnp.zeros_like(sse_ref)

        parts = [slice(0, tm)]
        gus = [(_dg(n_ref[rows, :], wg_ref[...], NT), _dg(n_ref[rows, :], wu_ref[...], NT)) for rows in parts]
        for rows, (g, u) in zip(parts, gus):
            sg = _sigmoid(g)
            silu = g * sg
            p1_ref[rows, :] = (u * (sg + silu * (1.0 - sg))).astype(BF16)
            p2_ref[rows, :] = silu.astype(BF16)
            a = (silu * u).astype(BF16)
            a_ref[rows, :] = a
            acc_ref[rows, :] += _dot(a, wd_ref[...])

        @pl.when(j == nj - 1)
        def _():
            f = acc_ref[...]
            f_ref[...] = f
            y, _, _ = _rms(f, wpost_ref[...])
            hout = h_ref[...] + 0.5 * y
            if target is None:
                hout_ref[...] = hout
            else:
                sse = jnp.zeros((1, 1), F32)
                for k in range(nblk):
                    rows = slice(k * BLK, (k + 1) * BLK)
                    err = hout[rows] - t_refs[k][...]
                    if k == 0:
                        err = jnp.where(i > 0, err, 0.0)
                    hout_ref[rows, :] = err * (1.0 / D_MODEL)
                    sse = sse + jnp.sum(jnp.sum(err * err, axis=1, keepdims=True), axis=0, keepdims=True)
                sse_ref[...] += jnp.broadcast_to(sse, sse_ref.shape)

    row = pl.BlockSpec((tm, D_MODEL), lambda i, j: (i, 0))
    vec = pl.BlockSpec((1, D_MODEL), lambda i, j: (0, 0))
    wrow = pl.BlockSpec((None, fj, D_MODEL), lambda i, j: (j, 0, 0))
    act = pl.BlockSpec((None, tm, fj), lambda i, j: (j, i, 0))
    t_specs = [pl.BlockSpec((BLK, D_MODEL), functools.partial(lambda i, j, k: (jnp.maximum(nblk * i + k - 1, 0), 0), k=k))
               for k in range(nblk)]
    loss_spec = [_full((1, 128))] if target is not None else []
    loss_shape = [jax.ShapeDtypeStruct((1, 128), F32)] if target is not None else []
    return _pallas(
        body, name="ffn_fwd", grid=(t // tm, nj),
        in_specs=[row, vec, wrow, wrow, wrow, vec] + t_specs,
        out_specs=[row, row, act, act, act, row] + loss_spec,
        out_shape=[jax.ShapeDtypeStruct((t, D_MODEL), F32), jax.ShapeDtypeStruct((t, D_MODEL), BF16),
                   jax.ShapeDtypeStruct((nj, t, fj), BF16), jax.ShapeDtypeStruct((nj, t, fj), BF16),
                   jax.ShapeDtypeStruct((nj, t, fj), BF16), jax.ShapeDtypeStruct((t, D_MODEL), F32)] + loss_shape,
        scratch_shapes=[pltpu.VMEM((tm, D_MODEL), F32)],
        args=(h, wpre, wg4, wu4, wd4, wpost) + (target,) * nblk, hook=hook)


def _ffn_bwd(dhout, h, f, p14, p24, wpre, wg4, wu4, wd4, wpost, hook=None):
    t = h.shape[0]
    tm = _row_tile(t)
    nj, fj, _ = wg4.shape

    def body(dhout_ref, h_ref, f_ref, p1_ref, p2_ref, wpre_ref, wg_ref, wu_ref, wd_ref, wpost_ref,
             dh_ref, df_ref, dg_ref, du_ref, dwpre_ref, dwpost_ref, dn_ref):
        i = pl.program_id(0)
        j = pl.program_id(1)

        @pl.when((i == 0) & (j == 0))
        def _():
            dwpre_ref[...] = jnp.zeros_like(dwpre_ref)
            dwpost_ref[...] = jnp.zeros_like(dwpost_ref)

        @pl.when(j == 0)
        def _():
            wpost = wpost_ref[...]
            _, fh, r = _rms(f_ref[...], wpost)
            df, dw = _rms_bwd(fh, r, wpost, 0.5 * dhout_ref[...])
            dwpost_ref[...] += dw
            df_ref[...] = df.astype(BF16)
            dn_ref[...] = jnp.zeros_like(dn_ref)

        parts = _row_parts(tm)
        das = [_dg(df_ref[rows, :], wd_ref[...], NT) for rows in parts]
        for rows, da in zip(parts, das):
            dg = (da * p1_ref[rows, :].astype(F32)).astype(BF16)
            du = (da * p2_ref[rows, :].astype(F32)).astype(BF16)
            dg_ref[rows, :] = dg
            du_ref[rows, :] = du
            dn_ref[rows, :] += _dot(dg, wg_ref[...]) + _dot(du, wu_ref[...])

        @pl.when(j == nj - 1)
        def _():
            wpre = wpre_ref[...]
            _, hh, r = _rms(h_ref[...], wpre)
            dx, dw = _rms_bwd(hh, r, wpre, dn_ref[...])
            dwpre_ref[...] += dw
            dh_ref[...] = dhout_ref[...] + dx

    row = pl.BlockSpec((tm, D_MODEL), lambda i, j: (i, 0))
    vec = pl.BlockSpec((1, D_MODEL), lambda i, j: (0, 0))
    wrow = pl.BlockSpec((None, fj, D_MODEL), lambda i, j: (j, 0, 0))
    act = pl.BlockSpec((None, tm, fj), lambda i, j: (j, i, 0))
    actshape = jax.ShapeDtypeStruct((nj, t, fj), BF16)
    return _pallas(
        body, name="ffn_bwd", grid=(t // tm, nj),
        in_specs=[row, row, row, act, act, vec, wrow, wrow, wrow, vec],
        out_specs=[row, row, act, act, vec, vec],
        out_shape=[jax.ShapeDtypeStruct((t, D_MODEL), F32), jax.ShapeDtypeStruct((t, D_MODEL), BF16),
                   actshape, actshape,
                   jax.ShapeDtypeStruct((1, D_MODEL), F32), jax.ShapeDtypeStruct((1, D_MODEL), F32)],
        scratch_shapes=[pltpu.VMEM((tm, D_MODEL), F32)],
        args=(dhout, h, f, p14, p24, wpre, wg4, wu4, wd4, wpost), hook=hook)


def _ffn_wgrad(n, df, dg4, du4, a4, hook=None):
    t = n.shape[0]
    tm = _contract_tile(t)
    ni = t // tm
    nj, _, fj = dg4.shape

    def body(n_ref, df_ref, dg_ref, du_ref, a_ref, dw_ref, acc):
        i = pl.program_id(1)

        @pl.when(i == 0)
        def _():
            acc[...] = jnp.zeros_like(acc)

        nn = n_ref[...]
        acc[0:fj, :] += _dg(dg_ref[...], nn, TN)
        acc[fj:2 * fj, :] += _dg(du_ref[...], nn, TN)
        acc[2 * fj:3 * fj, :] += _dg(a_ref[...], df_ref[...], TN)

        @pl.when(i == ni - 1)
        def _():
            dw_ref[...] = acc[...].astype(BF16)

    row = pl.BlockSpec((tm, D_MODEL), lambda j, i: (i, 0))
    act = pl.BlockSpec((None, tm, fj), lambda j, i: (j, i, 0))
    return _pallas(
        body, name="ffn_wgrad", grid=(nj, ni),
        in_specs=[row, row, act, act, act],
        out_specs=[pl.BlockSpec((None, 3 * fj, D_MODEL), lambda j, i: (j, 0, 0))],
        out_shape=[jax.ShapeDtypeStruct((nj, 3 * fj, D_MODEL), BF16)],
        scratch_shapes=[pltpu.VMEM((3 * fj, D_MODEL), F32)],
        args=(n, df, dg4, du4, a4), hook=hook)


PASS_RELATION = (2, 0, 1)


def _ffn_wgrad_reduce(n, df, dg4, du4, a4, qc_idx, hook):
    t = n.shape[0]
    tm = _contract_tile(t)
    ni = t // tm
    nj, _, fj = dg4.shape
    assert nj == N_CHIPS
    hrows = 3 * fj // 2
    n_hin, n_hout = len(hook.inputs), len(hook.out_shape)

    def body(qc_ref, n_ref, df_ref, dg_ref, du_ref, a_ref, *rest):
        hins = rest[:n_hin]
        own_ref, others_ref = rest[n_hin:n_hin + 2]
        houts = rest[n_hin + 2:n_hin + 2 + n_hout]
        s0 = n_hin + 2 + n_hout
        acc, stage, land, sumbuf, px_send, px_recv, cs_send, cs_recv, own_sem = rest[s0:s0 + 9]
        hscr = rest[s0 + 9:]
        k_pass = pl.program_id(0)
        i = pl.program_id(1)
        x, y, c, chips = _place()
        mine = pl.ds(pl.multiple_of(c * hrows, 8), hrows)
        other = pl.ds(pl.multiple_of((1 - c) * hrows, 8), hrows)

        def to_owner(k):
            j = PASS_RELATION[k]
            return _remote(cs_send.at[j], cs_recv.at[j], sumbuf.at[k % 2], others_ref.at[j], (*chips[j], c))

        @pl.when((k_pass == 0) & (i == 0))
        def _():
            hook.start(hins, houts, hscr)

        @pl.when(i == 0)
        def _():
            acc[...] = jnp.zeros_like(acc)

        nn = n_ref[...]
        acc[0:fj, :] += _dg(dg_ref[...], nn, TN)
        acc[fj:2 * fj, :] += _dg(du_ref[...], nn, TN)
        acc[2 * fj:3 * fj, :] += _dg(a_ref[...], df_ref[...], TN)

        for k in range(N_CHIPS):
            @pl.when((k_pass == k) & (i == ni - 1))
            def _(k=k):
                slot = k % 2
                stage[...] = acc[other, :].astype(BF16)
                swap = _remote(px_send.at[k], px_recv.at[k], stage, land.at[slot], (x, y, 1 - c))
                swap.start()
                swap.wait_recv()
                pair = acc[mine, :] + land[slot].astype(F32)
                if k >= 2:
                    to_owner(k - 2).wait_send()
                sumbuf[slot] = pair.astype(BF16)
                swap.wait_send()
                if k < N_CHIPS - 1:
                    to_owner(k).start()
                else:
                    keep = pltpu.make_async_copy(sumbuf.at[slot], own_ref, own_sem)
                    keep.start()
                    for j in range(N_CHIPS - 1):
                        _remote(cs_send.at[j], cs_recv.at[j], sumbuf.at[0], others_ref.at[j], (*chips[j], c)).wait_recv()
                    to_owner(k - 1).wait_send()
                    keep.wait()
                    hook.finish(hins, houts, hscr)

    def chunk(k_pass, i, qc_ref):
        return (jnp.bitwise_xor(qc_ref[0], N_CHIPS - 1 - k_pass), i, 0)

    row = pl.BlockSpec((tm, D_MODEL), lambda k_pass, i, qc_ref: (i, 0))
    act = pl.BlockSpec((None, tm, fj), chunk)
    res = pl.pallas_call(
        body, name="ffn_wgrad_reduce",
        grid_spec=pltpu.PrefetchScalarGridSpec(
            num_scalar_prefetch=1, grid=(N_CHIPS, ni),
            in_specs=[row, row, act, act, act] + [ANY] * n_hin,
            out_specs=[ANY, ANY] + [ANY] * n_hout,
            scratch_shapes=[pltpu.VMEM((3 * fj, D_MODEL), F32), pltpu.VMEM((hrows, D_MODEL), BF16),
                            pltpu.VMEM((2, hrows, D_MODEL), BF16), pltpu.VMEM((2, hrows, D_MODEL), BF16),
                            pltpu.SemaphoreType.DMA((N_CHIPS,)), pltpu.SemaphoreType.DMA((N_CHIPS,)),
                            pltpu.SemaphoreType.DMA((N_CHIPS - 1,)), pltpu.SemaphoreType.DMA((N_CHIPS - 1,)),
                            pltpu.SemaphoreType.DMA] + list(hook.scratch)),
        out_shape=[jax.ShapeDtypeStruct((hrows, D_MODEL), BF16),
                   jax.ShapeDtypeStruct((N_CHIPS - 1, hrows, D_MODEL), BF16)] + list(hook.out_shape),
        compiler_params=_cparams(2),
    )(qc_idx, n, df, dg4, du4, a4, *hook.inputs)
    return res[0], res[1], res[2:]


def _xty(x, y):
    t, k = x.shape
    n = y.shape[1]
    tm = _contract_tile(t)
    tn = n if n <= 1024 else (896 if n % 896 == 0 else 128)

    def body(x_ref, y_ref, o_ref):
        @pl.when(pl.program_id(1) == 0)
        def _():
            o_ref[...] = jnp.zeros_like(o_ref)

        o_ref[...] += _dg(x_ref[...], y_ref[...], TN)

    return pl.pallas_call(
        body, name="xty", grid=(n // tn, t // tm),
        in_specs=[pl.BlockSpec((tm, k), lambda j, i: (i, 0)), pl.BlockSpec((tm, tn), lambda j, i: (i, j))],
        out_specs=pl.BlockSpec((k, tn), lambda j, i: (0, j)),
        out_shape=jax.ShapeDtypeStruct((k, n), F32),
        compiler_params=_cparams(2),
    )(x, y)


def _rope_tables(t):
    pos = (jnp.arange(t, dtype=jnp.int32) - PAD).astype(F32)
    inv_freq = 1.0 / (ROPE_THETA ** (jnp.arange(0, SWA_HD, 2, dtype=F32) / SWA_HD))
    ang = pos[:, None] * inv_freq[None, :]
    cos = jnp.cos(ang)
    sin = jnp.sin(ang)
    return jnp.concatenate([cos, cos, cos, cos], axis=1), jnp.concatenate([-sin, sin, -sin, sin], axis=1)


def _rot_half(x, first_half):
    return jnp.where(first_half, pltpu.roll(x, 96, 1), pltpu.roll(x, 32, 1))


def _first_half_mask(rows):
    lane = lax.broadcasted_iota(jnp.int32, (rows, 128), 1)
    return (lane % 64) < 32


def _log_sigmoid(z):
    return jnp.minimum(z, 0.0) - jnp.log(1.0 + jnp.exp(-jnp.abs(z)))


def _mix_proj(h1, wmixpre, winp, wa2p, bap, cos, sin):
    t = h1.shape[0]
    tm = _row_tile(t)

    def body(h_ref, w_ref, win_ref, wa2_ref, ba_ref, cos_ref, sin_ref,
             n_ref, gq_ref, gk_ref, gv_ref, gg_ref, ga_ref, la_ref, sq_ref, sk_ref, sv_ref):
        y, _, _ = _rms(h_ref[...], w_ref[...])
        n = y.astype(BF16)
        n_ref[...] = n
        proj = _dot(n, win_ref[...])
        gq_ref[...] = proj[:, P_GQ:P_GK]
        gk_ref[...] = proj[:, P_GK:P_GV]
        gv_ref[...] = proj[:, P_GV:P_GG]
        gg_ref[...] = proj[:, P_GG:P_GA]
        ga = proj[:, P_GA:P_SQ]
        ga_ref[...] = ga
        z = _dot(ga.astype(BF16), wa2_ref[...]) + ba_ref[...]
        la_ref[...] = _log_sigmoid(z) * (1.0 / GLA_TAU)
        c = cos_ref[...]
        s = sin_ref[...]
        fh = _first_half_mask(tm)
        for k in range(4):
            x = proj[:, P_SQ + 128 * k:P_SQ + 128 * (k + 1)]
            sq_ref[:, 128 * k:128 * (k + 1)] = (x * c + _rot_half(x, fh) * s).astype(BF16)
        for k in range(2):
            x = proj[:, P_SK + 128 * k:P_SK + 128 * (k + 1)]
            sk_ref[:, 128 * k:128 * (k + 1)] = (x * c + _rot_half(x, fh) * s).astype(BF16)
        sv_ref[...] = proj[:, P_SV:P_END].astype(BF16)

    def row(w):
        return pl.BlockSpec((tm, w), lambda i: (i, 0))

    def rshape(w, dt):
        return jax.ShapeDtypeStruct((t, w), dt)

    return pl.pallas_call(
        body, name="mix_proj", grid=(t // tm,),
        in_specs=[row(D_MODEL), _full((1, D_MODEL)), _full((D_MODEL, P_END)), _full((128, GLA_KW)),
                  _full((1, GLA_KW)), row(128), row(128)],
        out_specs=[row(D_MODEL), row(256), row(256), row(512), row(512), row(128), row(256), row(512), row(256),
                   row(256)],
        out_shape=[rshape(D_MODEL, BF16), rshape(256, F32), rshape(256, F32), rshape(512, F32), rshape(512, F32),
                   rshape(128, F32), rshape(256, F32), rshape(512, BF16), rshape(256, BF16), rshape(256, BF16)],
        compiler_params=_cparams(1),
    )(h1, wmixpre, winp, wa2p, bap, cos, sin)


def _scan_rows(x, reverse=False):
    n = x.shape[0]
    row = lax.broadcasted_iota(jnp.int32, x.shape, 0)
    s = 1
    while s < n:
        if reverse:
            x = x + jnp.where(row < n - s, pltpu.roll(x, n - s, 0), 0.0)
        else:
            x = x + jnp.where(row >= s, pltpu.roll(x, s, 0), 0.0)
        s *= 2
    return x


def _gla_cumsum(la, tril_f):
    b = _scan_rows(la)
    row = lax.broadcasted_iota(jnp.int32, b.shape, 0)
    bm = jnp.sum(jnp.where(row == GLA_CHUNK // 2 - 1, b, 0.0), axis=0, keepdims=True)
    bl = jnp.sum(jnp.where(row == GLA_CHUNK - 1, b, 0.0), axis=0, keepdims=True)
    return b, bm, bl


def _gla_decays(la, tril_f):
    b, bm, bl = _gla_cumsum(la, tril_f)
    return jnp.exp(b - bm), jnp.exp(bm - b), jnp.exp(b), jnp.exp(bl - b), jnp.exp(bl)


def _gla_masks():
    c = GLA_CHUNK
    r = lax.broadcasted_iota(jnp.int32, (c, c), 0)
    col = lax.broadcasted_iota(jnp.int32, (c, c), 1)
    r4 = lax.broadcasted_iota(jnp.int32, (GLA_HEADS * c, c), 0) % c
    c4 = lax.broadcasted_iota(jnp.int32, (GLA_HEADS * c, c), 1)
    klane = lax.broadcasted_iota(jnp.int32, (c, GLA_KW), 1) // GLA_DK
    vlane = lax.broadcasted_iota(jnp.int32, (c, GLA_W), 1) // GLA_DV
    srow = lax.broadcasted_iota(jnp.int32, (GLA_W, GLA_KW), 0) // GLA_DV
    scol = lax.broadcasted_iota(jnp.int32, (GLA_W, GLA_KW), 1) // GLA_DK
    return dict(tril_f=(r >= col).astype(F32), triu_f=(r <= col).astype(F32), tril4=r4 >= c4,
                khead=[klane == h for h in range(GLA_HEADS)], vhead=[vlane == h for h in range(GLA_HEADS)],
                diag=srow == scol)


def _stack_heads(x, head_masks):
    return jnp.concatenate([jnp.where(m, x, 0.0) for m in head_masks], axis=0)


def _gla_fwd(gq, gk, gv, la):
    t = gq.shape[0]
    rg = _seq_tile(t)
    nb = t // rg
    ncb = rg // GLA_CHUNK
    c = GLA_CHUNK

    def body(q_ref, k_ref, v_ref, la_ref, o_ref, ss_ref, st_ref):
        @pl.when(pl.program_id(0) == 0)
        def _():
            st_ref[...] = jnp.zeros_like(st_ref)

        mk = _gla_masks()
        st = st_ref[...]
        for ch in range(ncb):
            rows = slice(ch * c, (ch + 1) * c)
            eq, ek, eb, ekl, ebl = _gla_decays(la_ref[rows, :], mk["tril_f"])
            qs = q_ref[rows, :] * (GLA_DK ** -0.5)
            k = k_ref[rows, :]
            v = v_ref[rows, :].astype(BF16)
            ss_ref[ch] = st
            q4 = _stack_heads(qs * eq, mk["khead"]).astype(BF16)
            a4 = jnp.where(mk["tril4"], _dg(q4, (k * ek).astype(BF16), NT), 0.0).astype(BF16)
            r4 = _dot(a4, v)
            intra = jnp.concatenate([r4[h * c:(h + 1) * c, GLA_DV * h:GLA_DV * (h + 1)] for h in range(GLA_HEADS)],
                                    axis=1)
            o_ref[rows, :] = intra + _dg((qs * eb).astype(BF16), st.astype(BF16), NT)
            st = st * ebl + jnp.where(mk["diag"], _dg(v, (k * ekl).astype(BF16), TN), 0.0)
        st_ref[...] = st

    def row(w):
        return pl.BlockSpec((rg, w), lambda i: (i, 0))

    return pl.pallas_call(
        body, name="gla_fwd", grid=(nb,),
        in_specs=[row(256), row(256), row(512), row(256)],
        out_specs=[row(512), pl.BlockSpec((ncb, GLA_W, GLA_KW), lambda i: (i, 0, 0))],
        out_shape=[jax.ShapeDtypeStruct((t, GLA_W), F32), jax.ShapeDtypeStruct((nb * ncb, GLA_W, GLA_KW), F32)],
        scratch_shapes=[pltpu.VMEM((GLA_W, GLA_KW), F32)],
        compiler_params=_cparams(1),
    )(gq, gk, gv, la)


def _gla_bwd(gq, gk, gv, la, ss, do):
    t = gq.shape[0]
    rg = _seq_tile(t)
    nb = t // rg
    ncb = rg // GLA_CHUNK
    c = GLA_CHUNK

    def body(q_ref, k_ref, v_ref, la_ref, ss_ref, do_ref, dq_ref, dk_ref, dv_ref, dla_ref, dst_ref):
        @pl.when(pl.program_id(0) == 0)
        def _():
            dst_ref[...] = jnp.zeros_like(dst_ref)

        mk = _gla_masks()
        last_row = lax.broadcasted_iota(jnp.int32, (c, GLA_KW), 0) == c - 1
        scale = GLA_DK ** -0.5
        dstn = dst_ref[...]
        for ch in reversed(range(ncb)):
            rows = slice(ch * c, (ch + 1) * c)
            eq, ek, eb, ekl, ebl = _gla_decays(la_ref[rows, :], mk["tril_f"])
            qs = q_ref[rows, :] * scale
            k = k_ref[rows, :]
            qt, kt, qh, kh = qs * eq, k * ek, qs * eb, k * ekl
            ktb, khb, qhb = kt.astype(BF16), kh.astype(BF16), qh.astype(BF16)
            v = v_ref[rows, :].astype(BF16)
            do_f = do_ref[rows, :]
            dob = do_f.astype(BF16)
            st = ss_ref[ch]
            stb = st.astype(BF16)
            dstb = dstn.astype(BF16)
            q4 = _stack_heads(qt, mk["khead"]).astype(BF16)
            do4 = _stack_heads(do_f, mk["vhead"]).astype(BF16)
            a4 = jnp.where(mk["tril4"], _dg(q4, ktb, NT), 0.0).astype(BF16)
            da4 = jnp.where(mk["tril4"], _dg(do4, v, NT), 0.0).astype(BF16)
            dv_ref[rows, :] = _dg(a4, do4, TN) + _dg(khb, dstb, NT)
            dq4 = _dot(da4, ktb)
            dqt = jnp.zeros((c, GLA_KW), F32)
            for h in range(GLA_HEADS):
                dqt = dqt + jnp.where(mk["khead"][h], dq4[h * c:(h + 1) * c], 0.0)
            dkt = _dg(da4, q4, TN)
            dqh = _dot(dob, stb)
            dkh = _dot(v, dstb)
            dbl = jnp.sum(dstn * st, axis=0, keepdims=True)
            dstn = dstn * ebl + jnp.where(mk["diag"], _dg(dob, qhb, TN), 0.0)
            dq_ref[rows, :] = scale * (dqt * eq + dqh * eb)
            dk_ref[rows, :] = dkt * ek + dkh * ekl
            dkk = dkh * kh
            db = dqt * qt - dkt * kt + dqh * qh - dkk
            db = db + jnp.where(last_row, jnp.sum(dkk, axis=0, keepdims=True) + ebl * dbl, 0.0)
            dla_ref[rows, :] = _scan_rows(db, reverse=True)
        dst_ref[...] = dstn

    def row(w):
        return pl.BlockSpec((rg, w), lambda i: (nb - 1 - i, 0))

    def rshape(w):
        return jax.ShapeDtypeStruct((t, w), F32)

    return pl.pallas_call(
        body, name="gla_bwd", grid=(nb,),
        in_specs=[row(256), row(256), row(512), row(256),
                  pl.BlockSpec((ncb, GLA_W, GLA_KW), lambda i: (nb - 1 - i, 0, 0)), row(512)],
        out_specs=[row(256), row(256), row(512), row(256)],
        out_shape=[rshape(256), rshape(256), rshape(512), rshape(256)],
        scratch_shapes=[pltpu.VMEM((GLA_W, GLA_KW), F32)],
        compiler_params=_cparams(1),
    )(gq, gk, gv, la, ss, do)


SWA_G = SWA_QH // SWA_KVH


def _swa_bias():
    n = jnp.arange(3, dtype=jnp.int32)[:, None, None]
    r = (jnp.arange(SWA_G * BLK, dtype=jnp.int32) % BLK)[None, :, None]
    c = jnp.arange(3 * BLK, dtype=jnp.int32)[None, None, :]
    seg = c // BLK
    cc = c % BLK
    qpos = n * BLK + r - PAD
    kpos = jnp.where(seg == 0, (n - 1) * BLK, jnp.where(seg == 1, n * BLK, 0)) + cc - PAD
    band = (seg < 2) & (kpos >= N_META) & (kpos <= qpos) & (qpos - kpos < WINDOW)
    meta = (seg == 2) & (kpos >= 0) & (kpos < N_META) & (kpos <= qpos)
    return jnp.where(band | meta, 0.0, NEG_INF).astype(F32)


def _swa_stack(ref, rows, kh, lo, dtype):
    parts = []
    for g in range(2):
        pair = ref[rows, 128 * (2 * kh + g):128 * (2 * kh + g + 1)]
        zero = jnp.zeros_like(pair)
        parts += [jnp.where(lo, pair, zero), jnp.where(lo, zero, pair)]
    return jnp.concatenate(parts, axis=0).astype(dtype)


def _swa_unstack(x4, lo):
    return [jnp.where(lo, x4[2 * g * BLK:(2 * g + 1) * BLK], x4[(2 * g + 1) * BLK:(2 * g + 2) * BLK])
            for g in range(2)]


def _swa_sink_col(sink_ref, kh):
    blk = lax.broadcasted_iota(jnp.int32, (SWA_G * BLK, 1), 0) // BLK
    col = jnp.full((SWA_G * BLK, 1), sink_ref[SWA_G * kh + SWA_G - 1], F32)
    for e in reversed(range(SWA_G - 1)):
        col = jnp.where(blk == e, sink_ref[SWA_G * kh + e], col)
    return col


def _swa_probs(q4, kall, bias, sink):
    s = _dg(q4, kall, NT) * (SWA_HD ** -0.5) + bias
    m = jnp.maximum(jnp.max(s, axis=-1, keepdims=True), sink)
    p = jnp.exp(s - m)
    es = jnp.exp(sink - m)
    inv = 1.0 / (jnp.sum(p, axis=-1, keepdims=True) + es)
    return p * inv, es * inv


def _swa_keys(prev_ref, cur_ref, first_ref, b, ls):
    before = prev_ref[:, ls] if b == 0 else cur_ref[(b - 1) * BLK:b * BLK, ls]
    return jnp.concatenate([before, cur_ref[b * BLK:(b + 1) * BLK, ls], first_ref[:, ls]], axis=0)


def _swa_specs(rs, ns):
    bps = rs // BLK
    cur = lambda w: pl.BlockSpec((rs, w), lambda i: (jnp.minimum(i, ns - 1), 0))
    prev = lambda w: pl.BlockSpec((BLK, w), lambda i: (jnp.maximum(jnp.minimum(i, ns - 1) * bps - 1, 0), 0))
    first = lambda w: pl.BlockSpec((BLK, w), lambda i: (0, 0))
    return cur, prev, first


def _swa_fwd(sinks, sq, sk, sv):
    t = sq.shape[0]
    rs = _seq_tile(t)
    bps, ns = rs // BLK, t // rs

    def body(sink_ref, bias_ref, q_ref, kp_ref, kc_ref, km_ref, vp_ref, vc_ref, vm_ref, o_ref):
        i = pl.program_id(0)
        lo = lax.broadcasted_iota(jnp.int32, (BLK, 128), 1) < 64
        sink_cols = [_swa_sink_col(sink_ref, kh) for kh in range(SWA_KVH)]
        for b in range(bps):
            rows = slice(b * BLK, (b + 1) * BLK)
            bias = bias_ref[jnp.minimum(i * bps + b, 2)]
            for kh in range(SWA_KVH):
                ls = slice(128 * kh, 128 * (kh + 1))
                kall = _swa_keys(kp_ref, kc_ref, km_ref, b, ls)
                vall = _swa_keys(vp_ref, vc_ref, vm_ref, b, ls)
                p, _ = _swa_probs(_swa_stack(q_ref, rows, kh, lo, BF16), kall, bias, sink_cols[kh])
                for g, pair in enumerate(_swa_unstack(_dot(p.astype(BF16), vall), lo)):
                    o_ref[rows, 128 * (2 * kh + g):128 * (2 * kh + g + 1)] = pair

    cur, prev, first = _swa_specs(rs, ns)
    bias = _swa_bias()
    return pl.pallas_call(
        body, name="swa_fwd", grid=(ns,),
        in_specs=[pl.BlockSpec(memory_space=pltpu.SMEM), _full(bias.shape), cur(512), prev(256), cur(256), first(256),
                  prev(256), cur(256), first(256)],
        out_specs=cur(512),
        out_shape=jax.ShapeDtypeStruct((t, SWA_W), F32),
        compiler_params=_cparams(1),
    )(sinks, bias, sq, sk, sk, sk, sv, sv, sv)


def _swa_bwd(sinks, sq, sk, sv, o, do, hook=None):
    t = sq.shape[0]
    rs = _seq_tile(t)
    bps, ns = rs // BLK, t // rs

    def body(sink_ref, bias_ref, q_ref, kp_ref, kc_ref, km_ref, vp_ref, vc_ref, vm_ref, o_ref, do_ref,
             dq_ref, dk_ref, dv_ref, dkm_ref, dvm_ref, dsink_ref, pk_ref, pv_ref):
        i = pl.program_id(0)

        @pl.when(i == 0)
        def _():
            pk_ref[...] = jnp.zeros_like(pk_ref)
            pv_ref[...] = jnp.zeros_like(pv_ref)
            dkm_ref[...] = jnp.zeros_like(dkm_ref)
            dvm_ref[...] = jnp.zeros_like(dvm_ref)
            dsink_ref[...] = jnp.zeros_like(dsink_ref)

        @pl.when(i == ns)
        def _():
            dk_ref[...] = pk_ref[...]
            dv_ref[...] = pv_ref[...]

        @pl.when(i < ns)
        def _():
            lo = lax.broadcasted_iota(jnp.int32, (BLK, 128), 1) < 64
            scale = SWA_HD ** -0.5
            sink_cols = [_swa_sink_col(sink_ref, kh) for kh in range(SWA_KVH)]
            parts_k = [[None] * SWA_KVH for _ in range(bps)]
            parts_v = [[None] * SWA_KVH for _ in range(bps)]
            dsinks = [jnp.zeros((1, 1), F32) for _ in range(SWA_QH)]
            for b in range(bps):
                rows = slice(b * BLK, (b + 1) * BLK)
                bias = bias_ref[jnp.minimum(i * bps + b, 2)]
                for kh in range(SWA_KVH):
                    ls = slice(128 * kh, 128 * (kh + 1))
                    kall = _swa_keys(kp_ref, kc_ref, km_ref, b, ls)
                    vall = _swa_keys(vp_ref, vc_ref, vm_ref, b, ls)
                    q4 = _swa_stack(q_ref, rows, kh, lo, BF16)
                    do4 = _swa_stack(do_ref, rows, kh, lo, F32)
                    p, psink = _swa_probs(q4, kall, bias, sink_cols[kh])
                    delta = jnp.sum(do4 * _swa_stack(o_ref, rows, kh, lo, F32), axis=-1, keepdims=True)
                    do4b = do4.astype(BF16)
                    ds = (p * (_dg(do4b, vall, NT) - delta) * scale).astype(BF16)
                    for g, pair in enumerate(_swa_unstack(_dot(ds, kall), lo)):
                        dq_ref[rows, 128 * (2 * kh + g):128 * (2 * kh + g + 1)] = pair
                    parts_k[b][kh] = _dg(ds, q4, TN)
                    parts_v[b][kh] = _dg(p.astype(BF16), do4b, TN)
                    dsk = psink * delta
                    for e in range(SWA_G):
                        h = SWA_G * kh + e
                        dsinks[h] = dsinks[h] - jnp.sum(dsk[e * BLK:(e + 1) * BLK], axis=0, keepdims=True)
            last = slice(rs - BLK, rs)
            for parts, out_ref, pend_ref, meta_ref in ((parts_k, dk_ref, pk_ref, dkm_ref),
                                                       (parts_v, dv_ref, pv_ref, dvm_ref)):
                for kh in range(SWA_KVH):
                    ls = slice(128 * kh, 128 * (kh + 1))
                    if bps > 1:
                        out_ref[0:rs - BLK, ls] = pend_ref[0:rs - BLK, ls]
                    out_ref[last, ls] = pend_ref[last, ls] + parts[0][kh][0:BLK]
                    meta = parts[0][kh][2 * BLK:3 * BLK]
                    for b in range(bps):
                        own = parts[b][kh][BLK:2 * BLK]
                        if b + 1 < bps:
                            own = own + parts[b + 1][kh][0:BLK]
                            meta = meta + parts[b + 1][kh][2 * BLK:3 * BLK]
                        pend_ref[b * BLK:(b + 1) * BLK, ls] = own
                    meta_ref[:, ls] += meta
            for h in range(SWA_QH):
                dsink_ref[h:h + 1, :] += jnp.broadcast_to(dsinks[h], (1, 128))

    cur, prev, first = _swa_specs(rs, ns)
    late = lambda w: pl.BlockSpec((rs, w), lambda i: (jnp.maximum(i - 1, 0), 0))
    bias = _swa_bias()
    return _pallas(
        body, name="swa_bwd", grid=(ns + 1,),
        in_specs=[pl.BlockSpec(memory_space=pltpu.SMEM), _full(bias.shape), cur(512), prev(256), cur(256), first(256),
                  prev(256), cur(256), first(256), cur(512), cur(512)],
        out_specs=[cur(512), late(256), late(256), first(256), first(256), _full((SWA_QH, 128))],
        out_shape=[jax.ShapeDtypeStruct((t, SWA_W), F32), jax.ShapeDtypeStruct((t, 256), F32),
                   jax.ShapeDtypeStruct((t, 256), F32), jax.ShapeDtypeStruct((BLK, 256), F32),
                   jax.ShapeDtypeStruct((BLK, 256), F32), jax.ShapeDtypeStruct((SWA_QH, 128), F32)],
        scratch_shapes=[pltpu.VMEM((rs, 256), F32), pltpu.VMEM((rs, 256), F32)],
        args=(sinks, bias, sq, sk, sk, sk, sv, sv, sv, o, do), hook=hook)


def _mix_out(h1, ogla, gg, oswa, wgn, wsn, wout, wpost):
    t = h1.shape[0]
    tm = _row_tile(t)

    def body(h_ref, og_ref, gg_ref, os_ref, wgn_ref, wsn_ref, wout_ref, wpost_ref, h2_ref, cat_ref, m_ref):
        parts = []
        for h in range(GLA_HEADS):
            ls = slice(GLA_DV * h, GLA_DV * (h + 1))
            y, _, _ = _rms(og_ref[:, ls], wgn_ref[...])
            g = gg_ref[:, ls]
            parts.append(y * (g * _sigmoid(g)))
        ys, _, _ = _rms(os_ref[...], wsn_ref[...])
        cat = jnp.concatenate(parts + [ys], axis=1).astype(BF16)
        cat_ref[...] = cat
        m = _dot(cat, wout_ref[...])
        m_ref[...] = m
        y, _, _ = _rms(m, wpost_ref[...])
        h2_ref[...] = h_ref[...] + y

    def row(w):
        return pl.BlockSpec((tm, w), lambda i: (i, 0))

    return pl.pallas_call(
        body, name="mix_out", grid=(t // tm,),
        in_specs=[row(D_MODEL), row(512), row(512), row(512), _full((1, GLA_DV)), _full((1, SWA_W)),
                  _full((D_MODEL, D_MODEL)), _full((1, D_MODEL))],
        out_specs=[row(D_MODEL), row(D_MODEL), row(D_MODEL)],
        out_shape=[jax.ShapeDtypeStruct((t, D_MODEL), F32), jax.ShapeDtypeStruct((t, D_MODEL), BF16),
                   jax.ShapeDtypeStruct((t, D_MODEL), F32)],
        compiler_params=_cparams(1),
    )(h1, ogla, gg, oswa, wgn, wsn, wout, wpost)


def _mix_out_bwd(dh2, m, ogla, gg, oswa, wgn, wsn, wout, wpost, hook=None):
    t = dh2.shape[0]
    tm = _row_tile(t)

    def body(dh_ref, m_ref, og_ref, gg_ref, os_ref, wgn_ref, wsn_ref, wout_ref, wpost_ref,
             dog_ref, dgg_ref, dos_ref, dm_ref, dwpost_ref, dwgn_ref, dwsn_ref):
        @pl.when(pl.program_id(0) == 0)
        def _():
            dwpost_ref[...] = jnp.zeros_like(dwpost_ref)
            dwgn_ref[...] = jnp.zeros_like(dwgn_ref)
            dwsn_ref[...] = jnp.zeros_like(dwsn_ref)

        wpost = wpost_ref[...]
        _, mh, r = _rms(m_ref[...], wpost)
        dm, dw = _rms_bwd(mh, r, wpost, dh_ref[...])
        dwpost_ref[...] += dw
        dmb = dm.astype(BF16)
        dm_ref[...] = dmb
        dcat = _dg(dmb, wout_ref[...], NT)
        wgn = wgn_ref[...]
        for h in range(GLA_HEADS):
            ls = slice(GLA_DV * h, GLA_DV * (h + 1))
            dog = dcat[:, ls]
            g = gg_ref[:, ls]
            sg = _sigmoid(g)
            y, xh, r = _rms(og_ref[:, ls], wgn)
            dgg_ref[:, ls] = dog * y * (sg * (1.0 + g * (1.0 - sg)))
            dx, dw = _rms_bwd(xh, r, wgn, dog * (g * sg))
            dog_ref[:, ls] = dx
            dwgn_ref[...] += dw
        wsn = wsn_ref[...]
        _, xh, r = _rms(os_ref[...], wsn)
        dx, dw = _rms_bwd(xh, r, wsn, dcat[:, GLA_W:])
        dos_ref[...] = dx
        dwsn_ref[...] += dw

    def row(w):
        return pl.BlockSpec((tm, w), lambda i: (i, 0))

    def rshape(w, dt=F32):
        return jax.ShapeDtypeStruct((t, w), dt)

    return _pallas(
        body, name="mix_out_bwd", grid=(t // tm,),
        in_specs=[row(D_MODEL), row(D_MODEL), row(512), row(512), row(512), _full((1, GLA_DV)), _full((1, SWA_W)),
                  _full((D_MODEL, D_MODEL)), _full((1, D_MODEL))],
        out_specs=[row(512), row(512), row(512), row(D_MODEL), _full((1, D_MODEL)), _full((1, GLA_DV)),
                   _full((1, SWA_W))],
        out_shape=[rshape(512), rshape(512), rshape(512), rshape(D_MODEL, BF16),
                   jax.ShapeDtypeStruct((1, D_MODEL), F32), jax.ShapeDtypeStruct((1, GLA_DV), F32),
                   jax.ShapeDtypeStruct((1, SWA_W), F32)],
        args=(dh2, m, ogla, gg, oswa, wgn, wsn, wout, wpost), hook=hook)


def _mix_in_bwd(dh2, h1, wmixpre, winp, wa2p, bap, cos, sin, ga, dgq, dgk, dgv, dgg, dla, dsq, dsk, dsv, dkm, dvm):
    t = h1.shape[0]
    tm = _row_tile(t)

    def body(dh2_ref, h_ref, w_ref, win_ref, wa2_ref, ba_ref, cos_ref, sin_ref, ga_ref, dgq_ref, dgk_ref, dgv_ref,
             dgg_ref, dla_ref, dsq_ref, dsk_ref, dsv_ref, dkm_ref, dvm_ref,
             dh1_ref, dproj_ref, dw_ref, dwa2_ref, dba_ref):
        i = pl.program_id(0)

        @pl.when(i == 0)
        def _():
            dw_ref[...] = jnp.zeros_like(dw_ref)
            dwa2_ref[...] = jnp.zeros_like(dwa2_ref)
            dba_ref[...] = jnp.zeros_like(dba_ref)

        first = (i == 0).astype(F32)
        c = cos_ref[...]
        s = -sin_ref[...]
        fh = _first_half_mask(tm)
        dproj_ref[:, P_GQ:P_GK] = dgq_ref[...].astype(BF16)
        dproj_ref[:, P_GK:P_GV] = dgk_ref[...].astype(BF16)
        dproj_ref[:, P_GV:P_GG] = dgv_ref[...].astype(BF16)
        dproj_ref[:, P_GG:P_GA] = dgg_ref[...].astype(BF16)
        gab = ga_ref[...].astype(BF16)
        z = _dot(gab, wa2_ref[...]) + ba_ref[...]
        row_id = i * tm + lax.broadcasted_iota(jnp.int32, (tm, 1), 0)
        dz = jnp.where(row_id >= PAD, dla_ref[...] * (1.0 / GLA_TAU) * (1.0 - _sigmoid(z)), 0.0)
        dzb = dz.astype(BF16)
        dba_ref[...] += jnp.sum(dz, axis=0, keepdims=True)
        dwa2_ref[...] += _dg(gab, dzb, TN)
        dproj_ref[:, P_GA:P_SQ] = _dg(dzb, wa2_ref[...], NT).astype(BF16)
        for k in range(4):
            dy = dsq_ref[:, 128 * k:128 * (k + 1)]
            dproj_ref[:, P_SQ + 128 * k:P_SQ + 128 * (k + 1)] = (dy * c + _rot_half(dy, fh) * s).astype(BF16)
        for k in range(2):
            ls = slice(128 * k, 128 * (k + 1))
            dy = dsk_ref[:, ls]
            dy = jnp.concatenate([dy[:BLK] + first * dkm_ref[:, ls], dy[BLK:]], axis=0) if tm > BLK else (
                dy + first * dkm_ref[:, ls])
            dproj_ref[:, P_SK + 128 * k:P_SK + 128 * (k + 1)] = (dy * c + _rot_half(dy, fh) * s).astype(BF16)
            dv = dsv_ref[:, ls]
            dv = jnp.concatenate([dv[:BLK] + first * dvm_ref[:, ls], dv[BLK:]], axis=0) if tm > BLK else (
                dv + first * dvm_ref[:, ls])
            dproj_ref[:, P_SV + 128 * k:P_SV + 128 * (k + 1)] = dv.astype(BF16)
        dn = _dg(dproj_ref[...], win_ref[...], NT)
        w = w_ref[...]
        _, hh, r = _rms(h_ref[...], w)
        dx, dw = _rms_bwd(hh, r, w, dn)
        dw_ref[...] += dw
        dh1_ref[...] = dh2_ref[...] + dx

    def row(w):
        return pl.BlockSpec((tm, w), lambda i: (i, 0))

    return pl.pallas_call(
        body, name="mix_in_bwd", grid=(t // tm,),
        in_specs=[row(D_MODEL), row(D_MODEL), _full((1, D_MODEL)), _full((D_MODEL, P_END)), _full((128, GLA_KW)),
                  _full((1, GLA_KW)), row(128), row(128), row(128), row(256), row(256), row(512), row(512), row(256),
                  row(512), row(256), row(256), _full((BLK, 256)), _full((BLK, 256))],
        out_specs=[row(D_MODEL), row(P_END), _full((1, D_MODEL)), _full((128, GLA_KW)), _full((1, GLA_KW))],
        out_shape=[jax.ShapeDtypeStruct((t, D_MODEL), F32), jax.ShapeDtypeStruct((t, P_END), BF16),
                   jax.ShapeDtypeStruct((1, D_MODEL), F32), jax.ShapeDtypeStruct((128, GLA_KW), F32),
                   jax.ShapeDtypeStruct((1, GLA_KW), F32)],
        compiler_params=_cparams(1),
    )(dh2, h1, wmixpre, winp, wa2p, bap, cos, sin, ga, dgq, dgk, dgv, dgg, dla, dsq, dsk, dsv, dkm, dvm)


def _adamw_update(w, g, m, v):
    m = ADAM_B1 * m + (1.0 - ADAM_B1) * g
    v = ADAM_B2 * v + (1.0 - ADAM_B2) * (g * g)
    m_hat = m / (1.0 - ADAM_B1 ** ADAM_STEP)
    v_hat = v / (1.0 - ADAM_B2 ** ADAM_STEP)
    return -ADAM_LR * (m_hat / (jnp.sqrt(v_hat) + ADAM_EPS) + ADAM_WD * w), m, v


def _adamw(w, g, m, v):
    r, c = w.shape
    tr = _div_tile(r)

    def body(w_ref, g_ref, m_ref, v_ref, d_ref, nm_ref, nv_ref):
        d_ref[...], nm_ref[...], nv_ref[...] = _adamw_update(w_ref[...], g_ref[...], m_ref[...], v_ref[...])

    spec = pl.BlockSpec((tr, c), lambda i: (i, 0))
    shape = jax.ShapeDtypeStruct((r, c), F32)
    return pl.pallas_call(
        body, name="adamw", grid=(r // tr,), in_specs=[spec] * 4, out_specs=[spec] * 3, out_shape=[shape] * 3,
        compiler_params=_cparams(1),
    )(w, g, m, v)


def _adamw_halves(w, g_mine, g_other, m, v, c_idx, row0=0):
    r, c = w.shape
    h = g_mine.shape[0]
    tr = _div_tile(math.gcd(r, h))
    nth = h // tr
    t0 = row0 // tr
    assert t0 * tr == row0

    def body(c_ref, w_ref, gm_ref, go_ref, m_ref, v_ref, g_ref, d_ref, nm_ref, nv_ref):
        hh = (t0 + pl.program_id(0)) // nth
        g = jnp.where(hh == c_ref[0], gm_ref[...], go_ref[...])
        g_ref[...] = g
        d_ref[...], nm_ref[...], nv_ref[...] = _adamw_update(w_ref[...], g, m_ref[...], v_ref[...])

    spec = pl.BlockSpec((tr, c), lambda i, c_ref: (i, 0))
    gspec = pl.BlockSpec((tr, c), lambda i, c_ref: ((t0 + i) % nth, 0))
    shape = jax.ShapeDtypeStruct((r, c), F32)
    return pl.pallas_call(
        body, name="adamw_halves",
        grid_spec=pltpu.PrefetchScalarGridSpec(
            num_scalar_prefetch=1, grid=(r // tr,), in_specs=[spec, gspec, gspec, spec, spec], out_specs=[spec] * 4),
        out_shape=[shape] * 4, compiler_params=_cparams(1),
    )(c_idx, w, g_mine, g_other, m, v)


def _place():
    x, y, c = lax.axis_index("x"), lax.axis_index("y"), lax.axis_index("c")
    chips = [(1 - x, y), (x, 1 - y), (1 - x, 1 - y)]
    return x, y, c, chips


def _remote(send_sem, recv_sem, src, dst, to):
    return pltpu.make_async_remote_copy(src_ref=src, dst_ref=dst, send_sem=send_sem, recv_sem=recv_sem,
                                        device_id=to, device_id_type=MESH)


def _half(ref_rows, c):
    h = ref_rows // 2
    return pl.ds(pl.multiple_of(c * h, 8), h)


def _own_slot(shard, q):
    return lax.dynamic_update_slice(jnp.zeros((N_CHIPS,) + shard.shape, shard.dtype), shard[None], (q, 0, 0))


class _GatherChips:
    has_mid = True

    def __init__(self, bufs):
        n = len(bufs)
        self.inputs = list(bufs)
        self.out_shape = [jax.ShapeDtypeStruct(b.shape, b.dtype) for b in bufs]
        self.aliases = [(t, t) for t in range(n)]
        self.scratch = [pltpu.SemaphoreType.DMA((n, 6)), pltpu.SemaphoreType.DMA((n, 6))]

    def start(self, ins, outs, scr):
        send, recv = scr
        x, y, c, chips = _place()
        q = 2 * x + y
        for t, (i_ref, o_ref) in enumerate(zip(ins, outs)):
            rows = _half(i_ref.shape[1], c)
            for j, (cx, cy) in enumerate(chips):
                _remote(send.at[t, j], recv.at[t, j], i_ref.at[q, rows], o_ref.at[q, rows], (cx, cy, c)).start()

    def mid(self, ins, outs, scr):
        send, recv = scr
        x, y, c, chips = _place()
        for t, o_ref in enumerate(outs):
            rows = _half(o_ref.shape[1], c)
            for j, (cx, cy) in enumerate(chips):
                slot = o_ref.at[2 * cx + cy, rows]
                _remote(send.at[t, j], recv.at[t, j], slot, slot, (cx, cy, c)).wait_recv()
                _remote(send.at[t, 3 + j], recv.at[t, 3 + j], slot, slot, (x, y, 1 - c)).start()

    def finish(self, ins, outs, scr):
        send, recv = scr
        x, y, c, chips = _place()
        for t, o_ref in enumerate(outs):
            mine, other = _half(o_ref.shape[1], c), _half(o_ref.shape[1], 1 - c)
            for j, (cx, cy) in enumerate(chips):
                slot = o_ref.at[2 * cx + cy, other]
                _remote(send.at[t, 3 + j], recv.at[t, 3 + j], slot, slot, (x, y, 1 - c)).wait_recv()
            for j, (cx, cy) in enumerate(chips):
                sent = o_ref.at[2 * cx + cy, mine]
                _remote(send.at[t, j], recv.at[t, j], sent, sent, (cx, cy, c)).wait_send()
                _remote(send.at[t, 3 + j], recv.at[t, 3 + j], sent, sent, (x, y, 1 - c)).wait_send()


class _PairExchange:
    has_mid = False
    aliases = ()

    def __init__(self, arrs):
        n = len(arrs)
        self.inputs = list(arrs)
        self.out_shape = [jax.ShapeDtypeStruct((a.shape[0], a.shape[1] // 2, a.shape[2]), a.dtype) for a in arrs]
        self.scratch = [pltpu.SemaphoreType.DMA((n,)), pltpu.SemaphoreType.DMA((n,))]

    def _copies(self, ins, outs, scr):
        send, recv = scr
        x, y, c, _ = _place()
        return [_remote(send.at[t], recv.at[t], i_ref.at[:, _half(i_ref.shape[1], 1 - c)], o_ref, (x, y, 1 - c))
                for t, (i_ref, o_ref) in enumerate(zip(ins, outs))]

    def start(self, ins, outs, scr):
        for cp in self._copies(ins, outs, scr):
            cp.start()

    def finish(self, ins, outs, scr):
        for cp in self._copies(ins, outs, scr):
            cp.wait()


class _ChipScatter:
    has_mid = False
    aliases = ()

    def __init__(self, arrs):
        n = len(arrs)
        self.inputs = list(arrs)
        self.out_shape = [jax.ShapeDtypeStruct((3,) + a.shape[1:], a.dtype) for a in arrs]
        self.scratch = [pltpu.SemaphoreType.DMA((n, 3)), pltpu.SemaphoreType.DMA((n, 3))]

    def _copies(self, ins, outs, scr):
        send, recv = scr
        x, y, c, chips = _place()
        return [_remote(send.at[t, j], recv.at[t, j], i_ref.at[2 * cx + cy], o_ref.at[j], (cx, cy, c))
                for t, (i_ref, o_ref) in enumerate(zip(ins, outs)) for j, (cx, cy) in enumerate(chips)]

    def start(self, ins, outs, scr):
        for cp in self._copies(ins, outs, scr):
            cp.start()

    def finish(self, ins, outs, scr):
        for cp in self._copies(ins, outs, scr):
            cp.wait()


class _PairShare:
    has_mid = False
    aliases = ()

    def __init__(self, arrs):
        n = len(arrs)
        self.inputs = list(arrs)
        self.out_shape = [jax.ShapeDtypeStruct(a.shape, a.dtype) for a in arrs]
        self.scratch = [pltpu.SemaphoreType.DMA((n,)), pltpu.SemaphoreType.DMA((n,))]

    def _copies(self, ins, outs, scr):
        send, recv = scr
        x, y, c, _ = _place()
        return [_remote(send.at[t], recv.at[t], i_ref, o_ref, (x, y, 1 - c))
                for t, (i_ref, o_ref) in enumerate(zip(ins, outs))]

    def start(self, ins, outs, scr):
        for cp in self._copies(ins, outs, scr):
            cp.start()

    def finish(self, ins, outs, scr):
        for cp in self._copies(ins, outs, scr):
            cp.wait()


def _comm_call(hook, name):
    n_in, n_out = len(hook.inputs), len(hook.out_shape)

    def body(*refs):
        ins, outs, scr = refs[:n_in], refs[n_in:n_in + n_out], refs[n_in + n_out:]
        hook.start(ins, outs, scr)
        if hook.has_mid:
            hook.mid(ins, outs, scr)
        hook.finish(ins, outs, scr)

    return pl.pallas_call(body, name=name, in_specs=[ANY] * n_in, out_specs=[ANY] * n_out,
                          out_shape=list(hook.out_shape), scratch_shapes=list(hook.scratch),
                          input_output_aliases=dict(hook.aliases))(*hook.inputs)


def _all_gather_devices(vecs):
    n = len(vecs)

    def body(*refs):
        x_refs, out_refs = refs[:n], refs[n:2 * n]
        send_sems, recv_sems, local_sems = refs[2 * n:]
        x, y, c, chips = _place()
        me, sibling = (x, y, c), (x, y, 1 - c)
        waits = []
        for t, (x_ref, out_ref) in enumerate(zip(x_refs, out_refs)):
            def slot(px, py, pc, out_ref=out_ref):
                return out_ref.at[4 * px + 2 * py + pc]

            def copy(k, block, to, src=None, t=t, slot=slot):
                return pltpu.make_async_remote_copy(
                    src_ref=slot(*block) if src is None else src, dst_ref=slot(*block), send_sem=send_sems.at[t, k],
                    recv_sem=recv_sems.at[t, k], device_id=to, device_id_type=MESH)

            mine = pltpu.make_async_copy(x_ref, slot(*me), local_sems.at[t])
            mine.start()
            first = [copy(0, me, sibling, src=x_ref)]
            first += [copy(1 + j, me, (*chip, c), src=x_ref) for j, chip in enumerate(chips)]
            for cp in first:
                cp.start()
            waits.append((copy, mine, first))
        for copy, mine, first in waits:
            passed = [copy(4 + j, (*chip, c), sibling) for j, chip in enumerate(chips)]
            for j, chip in enumerate(chips):
                copy(1 + j, (*chip, c), me).wait_recv()
                passed[j].start()
            copy(0, sibling, me).wait_recv()
            for j, chip in enumerate(chips):
                copy(4 + j, (*chip, 1 - c), me).wait_recv()
            for cp in first + passed:
                cp.wait_send()
            mine.wait()

    vmem = pl.BlockSpec(memory_space=pltpu.VMEM)
    return pl.pallas_call(
        body, name="all_gather_devices", in_specs=[vmem] * n, out_specs=[vmem] * n,
        out_shape=[jax.ShapeDtypeStruct((N_DEV,) + v.shape, v.dtype) for v in vecs],
        scratch_shapes=[pltpu.SemaphoreType.DMA((n, 7)), pltpu.SemaphoreType.DMA((n, 7)),
                        pltpu.SemaphoreType.DMA((n,))],
    )(*vecs)


def _pair_sum(g, other, c_idx):
    nq, r, w = g.shape
    h = r // 2
    tr = _div_tile(h)
    nt = h // tr

    def body(c_ref, g_ref, o_ref, s_ref):
        s_ref[...] = (g_ref[...].astype(F32) + o_ref[...].astype(F32)).astype(s_ref.dtype)

    return pl.pallas_call(
        body, name="pair_sum",
        grid_spec=pltpu.PrefetchScalarGridSpec(
            num_scalar_prefetch=1, grid=(nq, nt),
            in_specs=[pl.BlockSpec((None, tr, w), lambda k, i, c_ref: (k, c_ref[0] * nt + i, 0)),
                      pl.BlockSpec((None, tr, w), lambda k, i, c_ref: (k, i, 0))],
            out_specs=pl.BlockSpec((None, tr, w), lambda k, i, c_ref: (k, i, 0))),
        out_shape=jax.ShapeDtypeStruct((nq, h, w), g.dtype),
        compiler_params=_cparams(2),
    )(c_idx, g, other)


def _chip_sum(s, others, q_idx):
    _, h, w = s.shape
    tr = _div_tile(h)

    def body(q_ref, s_ref, o_ref, out_ref):
        out_ref[...] = ((s_ref[...].astype(F32) + o_ref[0].astype(F32)) + o_ref[1].astype(F32)) + o_ref[2].astype(F32)

    return pl.pallas_call(
        body, name="chip_sum",
        grid_spec=pltpu.PrefetchScalarGridSpec(
            num_scalar_prefetch=1, grid=(h // tr,),
            in_specs=[pl.BlockSpec((None, tr, w), lambda i, q_ref: (q_ref[0], i, 0)),
                      pl.BlockSpec((3, tr, w), lambda i, q_ref: (0, i, 0))],
            out_specs=pl.BlockSpec((tr, w), lambda i, q_ref: (i, 0))),
        out_shape=jax.ShapeDtypeStruct((h, w), F32),
        compiler_params=_cparams(1),
    )(q_idx, s, others)


def _small_update(q_idx, parts, ws, ms, vs, col_block):
    n = len(parts)
    has_w = [w is not None for w in ws]

    def body(q_ref, *refs):
        pos = 0
        ins = []
        for t in range(n):
            k = 4 if has_w[t] else 1
            ins.append(refs[pos:pos + k])
            pos += k
        outs = refs[pos:]
        opos = 0
        for t in range(n):
            p_ref = ins[t][0]
            g = p_ref[0]
            for s in range(1, p_ref.shape[0]):
                g = g + p_ref[s]
            if has_w[t]:
                _, w_ref, m_ref, v_ref = ins[t]
                g_ref, d_ref, nm_ref, nv_ref = outs[opos:opos + 4]
                opos += 4
                g_ref[...] = g
                d_ref[...], nm_ref[...], nv_ref[...] = _adamw_update(w_ref[...], g, m_ref[...], v_ref[...])
            else:
                outs[opos][...] = g
                opos += 1

    def whole(shape):
        nd = len(shape)
        return pl.BlockSpec(shape, lambda i, q_ref: (0,) * nd)

    in_specs, out_specs, out_shape, args = [], [], [], []
    for t in range(n):
        k, r, wf = parts[t].shape
        if col_block[t]:
            w = wf // N_CHIPS
            in_specs.append(pl.BlockSpec((k, r, w), lambda i, q_ref: (0, 0, q_ref[0])))
        else:
            w = wf
            in_specs.append(whole((k, r, wf)))
        args.append(parts[t])
        if has_w[t]:
            assert ws[t].shape == (r, w), (ws[t].shape, r, w)
            in_specs += [whole((r, w))] * 3
            args += [ws[t], ms[t], vs[t]]
            out_specs += [whole((r, w))] * 4
            out_shape += [jax.ShapeDtypeStruct((r, w), F32)] * 4
        else:
            out_specs.append(whole((r, w)))
            out_shape.append(jax.ShapeDtypeStruct((r, w), F32))
    return pl.pallas_call(
        body, name="small_update",
        grid_spec=pltpu.PrefetchScalarGridSpec(num_scalar_prefetch=1, grid=(1,), in_specs=in_specs,
                                               out_specs=out_specs),
        out_shape=out_shape, compiler_params=_cparams(1),
    )(q_idx, *args)


def _pack_win(w_in):
    o = np.cumsum((0,) + IN_SPLITS)
    gq, gk, gv, gg, ga, sq, sk, sv = [w_in[:, o[i]:o[i + 1]] for i in range(8)]
    z = jnp.zeros((w_in.shape[0], 128 - GLA_RANK), w_in.dtype)
    dup = lambda a: jnp.concatenate([a[:, :64], a[:, :64], a[:, 64:], a[:, 64:]], axis=1)
    return jnp.concatenate([gq, gk, gv, gg, ga, z, sq, dup(sk), dup(sv)], axis=1)


def _unpack_dwin(d):
    und = lambda a: jnp.concatenate([a[:, 0:64] + a[:, 64:128], a[:, 128:192] + a[:, 192:256]], axis=1)
    return jnp.concatenate([d[:, :P_GA], d[:, P_GA:P_GA + GLA_RANK], d[:, P_SQ:P_SK], und(d[:, P_SK:P_SV]),
                            und(d[:, P_SV:P_END])], axis=1)


def _local_step(x, target, meta, p):
    s = x.shape[0]
    t = s + BLK
    h0 = jnp.concatenate([jnp.zeros((PAD, D_MODEL), F32), meta, x], axis=0)
    cos, sin = _rope_tables(t)

    h1, n1, g1, u1, a1, f1 = _ffn_fwd(h0, p["ffn1_pre_norm"], p["ffn1_w_gate"], p["ffn1_w_up"], p["ffn1_w_down"],
                                      p["ffn1_post_norm"])
    n2, gq, gk, gv, gg, ga, la, sq, sk, sv = _mix_proj(h1, p["mix_pre_norm"], p["w_in"], p["gla_w_a2"], p["gla_b_a"],
                                                       cos, sin)
    ogla, ss = _gla_fwd(gq, gk, gv, la)
    oswa = _swa_fwd(p["swa_sinks"], sq, sk, sv)
    h2, cat, m = _mix_out(h1, ogla, gg, oswa, p["gla_out_norm"], p["swa_out_norm"], p["w_out"], p["mix_post_norm"])
    dy, n3, g3, u3, a3, f3, sse = _ffn_fwd(h2, p["ffn2_pre_norm"], p["ffn2_w_gate"], p["ffn2_w_up"],
                                           p["ffn2_w_down"], p["ffn2_post_norm"], target=target)

    grads = {}
    dh2, df3, dg3, du3, grads["ffn2_pre_norm"], grads["ffn2_post_norm"] = _ffn_bwd(
        dy, h2, f3, g3, u3, p["ffn2_pre_norm"], p["ffn2_w_gate"], p["ffn2_w_up"], p["ffn2_w_down"],
        p["ffn2_post_norm"])
    (gud,) = _ffn_wgrad(n3, df3, dg3, du3, a3)
    grads["ffn2_w_gate"], grads["ffn2_w_up"], grads["ffn2_w_down"] = gud[:, :FJ], gud[:, FJ:2 * FJ], gud[:, 2 * FJ:]

    dogla, dgg, doswa, dm, grads["mix_post_norm"], grads["gla_out_norm"], grads["swa_out_norm"] = _mix_out_bwd(
        dh2, m, ogla, gg, oswa, p["gla_out_norm"], p["swa_out_norm"], p["w_out"], p["mix_post_norm"])
    grads["w_out"] = _xty(cat, dm)
    dsq, dsk, dsv, dkm, dvm, dsinks = _swa_bwd(p["swa_sinks"], sq, sk, sv, oswa, doswa)
    grads["swa_sinks"] = dsinks[:, 0]
    dgq, dgk, dgv, dla = _gla_bwd(gq, gk, gv, la, ss, dogla)
    dh1, dproj, grads["mix_pre_norm"], dwa2p, grads["gla_b_a"] = _mix_in_bwd(
        dh2, h1, p["mix_pre_norm"], p["w_in"], p["gla_w_a2"], p["gla_b_a"], cos, sin, ga, dgq, dgk, dgv, dgg, dla,
        dsq, dsk, dsv, dkm, dvm)
    grads["gla_w_a2"] = dwa2p[:GLA_RANK]
    grads["w_in"] = _unpack_dwin(_xty(n2, dproj))

    dh0, df1, dg1, du1, grads["ffn1_pre_norm"], grads["ffn1_post_norm"] = _ffn_bwd(
        dh1, h0, f1, g1, u1, p["ffn1_pre_norm"], p["ffn1_w_gate"], p["ffn1_w_up"], p["ffn1_w_down"],
        p["ffn1_post_norm"])
    (gud,) = _ffn_wgrad(n1, df1, dg1, du1, a1)
    grads["ffn1_w_gate"], grads["ffn1_w_up"], grads["ffn1_w_down"] = gud[:, :FJ], gud[:, FJ:2 * FJ], gud[:, 2 * FJ:]
    grads["meta_tokens"] = dh0[PAD:BLK]
    return sse[0, 0], dh0[BLK:], grads


WEIGHTS = ['meta_tokens', 'ffn1_pre_norm', 'ffn1_w_gate', 'ffn1_w_up', 'ffn1_w_down', 'ffn1_post_norm',
           'mix_pre_norm', 'w_in', 'gla_w_a2', 'gla_b_a', 'gla_out_norm', 'swa_sinks', 'swa_out_norm', 'w_out',
           'mix_post_norm', 'ffn2_pre_norm', 'ffn2_w_gate', 'ffn2_w_up', 'ffn2_w_down', 'ffn2_post_norm']
BIG = ['ffn1_w_gate', 'ffn1_w_up', 'ffn1_w_down', 'w_in', 'w_out', 'ffn2_w_gate', 'ffn2_w_up', 'ffn2_w_down']
SMALL = [n for n in WEIGHTS if n not in BIG]
FJ = D_FF // N_CHIPS
D_IN_J = D_IN // N_CHIPS
D_OUT_J = D_MODEL // N_CHIPS
TRANSPOSED = ('ffn1_w_gate', 'ffn1_w_up', 'ffn2_w_gate', 'ffn2_w_up')


def _shard2d(name, a):
    return a[0].T if name in TRANSPOSED else a[0]


def _unshard2d(name, a):
    return (a.T if name in TRANSPOSED else a)[None]


def _small_rows(name, a):
    flat = a.reshape(-1)
    rows = -(-flat.shape[0] // 1024) * 8
    return jnp.pad(flat, (0, rows * 128 - flat.shape[0])).reshape(rows, 128)


def kernel(x, meta_tokens, ffn1_pre_norm, ffn1_w_gate, ffn1_w_up, ffn1_w_down, ffn1_post_norm, mix_pre_norm, w_in, gla_w_a2, gla_b_a, gla_out_norm, swa_sinks, swa_out_norm, w_out, mix_post_norm, ffn2_pre_norm, ffn2_w_gate, ffn2_w_up, ffn2_w_down, ffn2_post_norm, loss_target, m_meta_tokens, m_ffn1_pre_norm, m_ffn1_w_gate, m_ffn1_w_up, m_ffn1_w_down, m_ffn1_post_norm, m_mix_pre_norm, m_w_in, m_gla_w_a2, m_gla_b_a, m_gla_out_norm, m_swa_sinks, m_swa_out_norm, m_w_out, m_mix_post_norm, m_ffn2_pre_norm, m_ffn2_w_gate, m_ffn2_w_up, m_ffn2_w_down, m_ffn2_post_norm, v_meta_tokens, v_ffn1_pre_norm, v_ffn1_w_gate, v_ffn1_w_up, v_ffn1_w_down, v_ffn1_post_norm, v_mix_pre_norm, v_w_in, v_gla_w_a2, v_gla_b_a, v_gla_out_norm, v_swa_sinks, v_swa_out_norm, v_w_out, v_mix_post_norm, v_ffn2_pre_norm, v_ffn2_w_gate, v_ffn2_w_up, v_ffn2_w_down, v_ffn2_post_norm):
    args = dict(locals())
    w = {n: args[n] for n in WEIGHTS}
    mom = {n: args["m_" + n] for n in WEIGHTS}
    var = {n: args["v_" + n] for n in WEIGHTS}
    cx, cy, cc = lax.axis_index("x"), lax.axis_index("y"), lax.axis_index("c")
    q_idx = (2 * cx + cy).astype(jnp.int32).reshape(1)
    c_idx = cc.astype(jnp.int32).reshape(1)

    q_chip = 2 * cx + cy
    bf = {n: _own_slot(_shard2d(n, w[n]).astype(BF16), q_chip) for n in BIG}
    early = _GatherChips([bf["ffn1_w_gate"], bf["ffn1_w_up"], bf["ffn1_w_down"], _own_slot(w["meta_tokens"], q_chip),
                          _own_slot(w["gla_w_a2"].reshape(GLA_RANK, GLA_KW // N_CHIPS), q_chip)])
    wg1, wu1, wd1, meta4, wa24 = _comm_call(early, "gather_ffn1")
    meta_full = meta4.transpose(1, 0, 2).reshape(N_META, D_MODEL)
    wa2p = jnp.pad(wa24.transpose(1, 0, 2).reshape(GLA_RANK, GLA_KW), ((0, 128 - GLA_RANK), (0, 0))).astype(BF16)
    sinks = w["swa_sinks"].reshape(SWA_QH)

    seq, target = x[0], loss_target[0]
    t = seq.shape[0] + BLK
    h0 = jnp.concatenate([jnp.zeros((PAD, D_MODEL), F32), meta_full, seq], axis=0)
    cos, sin = _rope_tables(t)
    late = _GatherChips([bf["w_in"], bf["w_out"], bf["ffn2_w_gate"], bf["ffn2_w_up"], bf["ffn2_w_down"]])
    (h1, n1, g1, u1, a1, f1), (win4, wout4, wg2, wu2, wd2) = _ffn_fwd(
        h0, w["ffn1_pre_norm"], wg1, wu1, wd1, w["ffn1_post_norm"], hook=late)
    winp = _pack_win(win4.transpose(1, 0, 2).reshape(D_MODEL, D_IN))
    wout = wout4.reshape(D_MODEL, D_MODEL)
    n2, gq, gk, gv, gg, ga, la, sq, sk, sv = _mix_proj(h1, w["mix_pre_norm"], winp, wa2p, w["gla_b_a"], cos, sin)
    ogla, ss = _gla_fwd(gq, gk, gv, la)
    oswa = _swa_fwd(sinks, sq, sk, sv)
    h2, cat, m = _mix_out(h1, ogla, gg, oswa, w["gla_out_norm"], w["swa_out_norm"], wout, w["mix_post_norm"])
    dy, n3, g3, u3, a3, f3, sse = _ffn_fwd(h2, w["ffn2_pre_norm"], wg2, wu2, wd2, w["ffn2_post_norm"], target=target)
    loss = lax.psum(sse[0, 0] * (0.5 / D_MODEL), ("x", "y", "c"))

    g = {}
    dh2, df3, dg3, du3, g["ffn2_pre_norm"], g["ffn2_post_norm"] = _ffn_bwd(
        dy, h2, f3, g3, u3, w["ffn2_pre_norm"], wg2, wu2, wd2, w["ffn2_post_norm"])
    (gf2,) = _ffn_wgrad(n3, df3, dg3, du3, a3)
    (dogla, dgg, doswa, dm, g["mix_post_norm"], g["gla_out_norm"], g["swa_out_norm"]), (rgf2,) = _mix_out_bwd(
        dh2, m, ogla, gg, oswa, w["gla_out_norm"], w["swa_out_norm"], wout, w["mix_post_norm"],
        hook=_PairExchange([gf2]))
    sgf2 = _pair_sum(gf2, rgf2, c_idx)
    gout = _xty(cat, dm).reshape(N_CHIPS, D_OUT_J, D_MODEL).astype(BF16)
    (dsq, dsk, dsv, dkm, dvm, dsinks), (ogf2,) = _swa_bwd(sinks, sq, sk, sv, oswa, doswa,
                                                          hook=_ChipScatter([sgf2]))
    g["swa_sinks"] = dsinks
    dgq, dgk, dgv, dla = _gla_bwd(gq, gk, gv, la, ss, dogla)
    dh1, dproj, g["mix_pre_norm"], dwa2p, g["gla_b_a"] = _mix_in_bwd(
        dh2, h1, w["mix_pre_norm"], winp, wa2p, w["gla_b_a"], cos, sin, ga, dgq, dgk, dgv, dgg, dla,
        dsq, dsk, dsv, dkm, dvm)
    g["gla_w_a2"] = dwa2p[:GLA_RANK]
    gin = _unpack_dwin(_xty(n2, dproj)).reshape(D_MODEL, N_CHIPS, D_IN_J).transpose(1, 0, 2).astype(BF16)
    (dh0, df1, dg1, du1, g["ffn1_pre_norm"], g["ffn1_post_norm"]), (rgin, rgout) = _ffn_bwd(
        dh1, h0, f1, g1, u1, w["ffn1_pre_norm"], wg1, wu1, wd1, w["ffn1_post_norm"],
        hook=_PairExchange([gin, gout]))
    sgin, sgout = _pair_sum(gin, rgin, c_idx), _pair_sum(gout, rgout, c_idx)
    qc_idx = jnp.stack([q_chip, cc]).astype(jnp.int32)
    own1, others1, (ogin, ogout) = _ffn_wgrad_reduce(n1, df1, dg1, du1, a1, qc_idx, _ChipScatter([sgin, sgout]))
    g["meta_tokens"] = dh0[PAD:BLK]
    grad_x = dh0[BLK:]
    halves = [_chip_sum(own1[None], others1, jnp.zeros((1,), jnp.int32))]
    halves += [_chip_sum(s, o, q_idx) for s, o in ((sgin, ogin), (sgout, ogout), (sgf2, ogf2))]
    others = _comm_call(_PairShare(halves), "pair_share")
    reduced = {"ffn1_w_gate": (0, 0), "ffn1_w_up": (0, FJ), "ffn1_w_down": (0, 2 * FJ), "w_in": (1, 0),
               "w_out": (2, 0), "ffn2_w_gate": (3, 0), "ffn2_w_up": (3, FJ), "ffn2_w_down": (3, 2 * FJ)}
    grad, delta, new_m, new_v = {}, {}, {}, {}
    for n in BIG:
        k, row0 = reduced[n]
        outs = _adamw_halves(_shard2d(n, w[n]), halves[k], others[k], _shard2d(n, mom[n]), _shard2d(n, var[n]),
                             c_idx, row0)
        grad[n], delta[n], new_m[n], new_v[n] = [_unshard2d(n, a) for a in outs]

    late = ["gla_w_a2", "swa_sinks"]
    direct = [n for n in SMALL if n not in late]
    names = direct + late
    gathered = _all_gather_devices([g[n] for n in names])
    mat = lambda a: a.reshape(a.shape[-2:])
    none2 = [None] * len(late)
    outs = _small_update(q_idx, gathered, [mat(w[n]) for n in direct] + none2, [mat(mom[n]) for n in direct] + none2,
                         [mat(var[n]) for n in direct] + none2, [n == "meta_tokens" for n in names])
    sum_a2, sum_sinks = outs[4 * len(direct):]
    g_late = [lax.dynamic_slice_in_dim(sum_a2, q_chip * (GLA_KW // N_CHIPS), GLA_KW // N_CHIPS, axis=1)[None],
              sum_sinks[:, 0].reshape(1, 1, SWA_QH)]
    outs = list(outs[:4 * len(direct)]) + list(_small_update(
        q_idx, g_late, [mat(w[n]) for n in late], [mat(mom[n]) for n in late], [mat(var[n]) for n in late],
        [False, False]))
    for k, n in enumerate(names):
        grad[n], delta[n], new_m[n], new_v[n] = [a.reshape(w[n].shape) for a in outs[4 * k:4 * k + 4]]

    return (loss, grad_x[None], *[grad[n] for n in WEIGHTS], *[delta[n] for n in WEIGHTS],
            *[new_m[n] for n in WEIGHTS], *[new_v[n] for n in WEIGHTS])
```

```python
import functools
import math

import numpy as np
import jax
import jax.numpy as jnp
from jax import lax
from jax.experimental import pallas as pl
from jax.experimental.pallas import tpu as pltpu

F32 = jnp.float32
BF16 = jnp.bfloat16
MESH = pl.DeviceIdType.MESH

D_MODEL = 1024
D_FF = 2816
N_CHIPS = 4
N_DEV = 8
N_META = 16
BLK = 128
PAD = BLK - N_META
GLA_CHUNK = 64
GLA_HEADS = 4
GLA_DV = 128
GLA_DK = 64
GLA_KW = GLA_HEADS * GLA_DK
GLA_W = GLA_HEADS * GLA_DV
GLA_RANK = 16
GLA_TAU = 16.0
SWA_HD = 64
SWA_QH = 8
SWA_KVH = 2
SWA_W = SWA_QH * SWA_HD
WINDOW = 128
ROPE_THETA = 10000.0
EPS = 1e-6
NEG_INF = -1e30
IN_SPLITS = (256, 256, 512, 512, 16, 512, 128, 128)
D_IN = sum(IN_SPLITS)
P_GQ, P_GK, P_GV, P_GG, P_GA, P_SQ, P_SK, P_SV, P_END = 0, 256, 512, 1024, 1536, 1664, 2176, 2432, 2688
ADAM_LR, ADAM_B1, ADAM_B2, ADAM_EPS, ADAM_WD, ADAM_STEP = 0.001, 0.9, 0.999, 1e-08, 0.01, 10
VMEM_LIMIT = 56 * 1024 * 1024

NT = (((1,), (1,)), ((), ()))
TN = (((0,), (0,)), ((), ()))


def _cparams(n_axes):
    return pltpu.CompilerParams(dimension_semantics=("arbitrary",) * n_axes, vmem_limit_bytes=VMEM_LIMIT)


def _row_tile(t):
    for tm in (640, 512, 384, 256, 128):
        if t % tm == 0:
            return tm
    raise ValueError(t)


SEQ_BLOCKS_PER_STEP = 5


def _seq_tile(t):
    return SEQ_BLOCKS_PER_STEP * BLK if t % (SEQ_BLOCKS_PER_STEP * BLK) == 0 else BLK


ROW_PARTS = 2


def _row_parts(tm):
    n = ROW_PARTS if tm % (16 * ROW_PARTS) == 0 else 1
    return [slice(k * (tm // n), (k + 1) * (tm // n)) for k in range(n)]


def _contract_tile(t):
    return 1664 if t % 1664 == 0 else _row_tile(t)


def _div_tile(r, cap=512):
    best = None
    for tr in range(8, min(r, cap) + 1, 8):
        if r % tr == 0:
            best = tr
    return best if best is not None else r


def _dot(a, b):
    return jnp.dot(a, b, preferred_element_type=F32)


def _dg(a, b, dims):
    return lax.dot_general(a, b, dims, preferred_element_type=F32)


def _rms(x, w):
    r = lax.rsqrt(jnp.mean(x * x, axis=-1, keepdims=True) + EPS)
    xh = x * r
    return xh * w, xh, r


def _rms_bwd(xh, r, w, dy):
    wdy = dy * w
    dx = r * (wdy - xh * jnp.mean(wdy * xh, axis=-1, keepdims=True))
    dw = jnp.sum(dy * xh, axis=0, keepdims=True)
    return dx, dw


def _sigmoid(x):
    return 1.0 / (1.0 + jnp.exp(-x))


def _full(shape):
    nd = len(shape)
    return pl.BlockSpec(shape, lambda *_: (0,) * nd)


ANY = pl.BlockSpec(memory_space=pl.ANY)


def _pallas(body, *, name, grid, in_specs, out_specs, out_shape, args, scratch_shapes=(), hook=None):
    n_axes = len(grid)
    if hook is None:
        return pl.pallas_call(body, name=name, grid=grid, in_specs=list(in_specs), out_specs=list(out_specs),
                              out_shape=list(out_shape), scratch_shapes=list(scratch_shapes),
                              compiler_params=_cparams(n_axes))(*args)
    n_in, n_out, n_scr = len(in_specs), len(out_specs), len(scratch_shapes)
    h_in, h_out = len(hook.inputs), len(hook.out_shape)
    total = math.prod(grid)

    def wrapped(*refs):
        ins, hins = refs[:n_in], refs[n_in:n_in + h_in]
        o0 = n_in + h_in
        outs, houts = refs[o0:o0 + n_out], refs[o0 + n_out:o0 + n_out + h_out]
        s0 = o0 + n_out + h_out
        scr, hscr = refs[s0:s0 + n_scr], refs[s0 + n_scr:]
        step = pl.program_id(0)
        for a in range(1, n_axes):
            step = step * grid[a] + pl.program_id(a)

        @pl.when(step == 0)
        def _():
            hook.start(hins, houts, hscr)

        body(*ins, *outs, *scr)

        if hook.has_mid:
            @pl.when(step == (3 * total) // 4)
            def _():
                hook.mid(hins, houts, hscr)

        @pl.when(step == total - 1)
        def _():
            hook.finish(hins, houts, hscr)

    res = pl.pallas_call(
        wrapped, name=name, grid=grid, in_specs=list(in_specs) + [ANY] * h_in,
        out_specs=list(out_specs) + [ANY] * h_out, out_shape=list(out_shape) + list(hook.out_shape),
        scratch_shapes=list(scratch_shapes) + list(hook.scratch), compiler_params=_cparams(n_axes),
        input_output_aliases={n_in + a: n_out + b for a, b in hook.aliases},
    )(*args, *hook.inputs)
    return res[:n_out], res[n_out:]


def _ffn_fwd(h, wpre, wg4, wu4, wd4, wpost, hook=None, target=None):
    t = h.shape[0]
    tm = _row_tile(t)
    nj, fj, _ = wg4.shape
    nblk = tm // BLK if target is not None else 0

    def body(*refs):
        h_ref, wpre_ref, wg_ref, wu_ref, wd_ref, wpost_ref = refs[:6]
        t_refs = refs[6:6 + nblk]
        hout_ref, n_ref, p1_ref, p2_ref, a_ref, f_ref = refs[6 + nblk:12 + nblk]
        acc_ref = refs[-1]
        i = pl.program_id(0)
        j = pl.program_id(1)

        @pl.when(j == 0)
        def _():
            y, _, _ = _rms(h_ref[...], wpre_ref[...])
            n_ref[...] = y.astype(BF16)
            acc_ref[...] = jnp.zeros_like(acc_ref)

        if target is not None:
            sse_ref = refs[12 + nblk]

            @pl.when((i == 0) & (j == 0))
            def _():
                sse_ref[...] = jnp.zeros_like(sse_ref)

        parts = [slice(0, tm)]
        gus = [(_dg(n_ref[rows, :], wg_ref[...], NT), _dg(n_ref[rows, :], wu_ref[...], NT)) for rows in parts]
        for rows, (g, u) in zip(parts, gus):
            sg = _sigmoid(g)
            silu = g * sg
            p1_ref[rows, :] = (u * (sg + silu * (1.0 - sg))).astype(BF16)
            p2_ref[rows, :] = silu.astype(BF16)
            a = (silu * u).astype(BF16)
            a_ref[rows, :] = a
            acc_ref[rows, :] += _dot(a, wd_ref[...])

        @pl.when(j == nj - 1)
        def _():
            f = acc_ref[...]
            f_ref[...] = f
            y, _, _ = _rms(f, wpost_ref[...])
            hout = h_ref[...] + 0.5 * y
            if target is None:
                hout_ref[...] = hout
            else:
                sse = jnp.zeros((1, 1), F32)
                for k in range(nblk):
                    rows = slice(k * BLK, (k + 1) * BLK)
                    err = hout[rows] - t_refs[k][...]
                    if k == 0:
                        err = jnp.where(i > 0, err, 0.0)
                    hout_ref[rows, :] = err * (1.0 / D_MODEL)
                    sse = sse + jnp.sum(jnp.sum(err * err, axis=1, keepdims=True), axis=0, keepdims=True)
                sse_ref[...] += jnp.broadcast_to(sse, sse_ref.shape)

    row = pl.BlockSpec((tm, D_MODEL), lambda i, j: (i, 0))
    vec = pl.BlockSpec((1, D_MODEL), lambda i, j: (0, 0))
    wrow = pl.BlockSpec((None, fj, D_MODEL), lambda i, j: (j, 0, 0))
    act = pl.BlockSpec((None, tm, fj), lambda i, j: (j, i, 0))
    t_specs = [pl.BlockSpec((BLK, D_MODEL), functools.partial(lambda i, j, k: (jnp.maximum(nblk * i + k - 1, 0), 0), k=k))
               for k in range(nblk)]
    loss_spec = [_full((1, 128))] if target is not None else []
    loss_shape = [jax.ShapeDtypeStruct((1, 128), F32)] if target is not None else []
    return _pallas(
        body, name="ffn_fwd", grid=(t // tm, nj),
        in_specs=[row, vec, wrow, wrow, wrow, vec] + t_specs,
        out_specs=[row, row, act, act, act, row] + loss_spec,
        out_shape=[jax.ShapeDtypeStruct((t, D_MODEL), F32), jax.ShapeDtypeStruct((t, D_MODEL), BF16),
                   jax.ShapeDtypeStruct((nj, t, fj), BF16), jax.ShapeDtypeStruct((nj, t, fj), BF16),
                   jax.ShapeDtypeStruct((nj, t, fj), BF16), jax.ShapeDtypeStruct((t, D_MODEL), F32)] + loss_shape,
        scratch_shapes=[pltpu.VMEM((tm, D_MODEL), F32)],
        args=(h, wpre, wg4, wu4, wd4, wpost) + (target,) * nblk, hook=hook)


def _ffn_bwd(dhout, h, f, p14, p24, wpre, wg4, wu4, wd4, wpost, hook=None):
    t = h.shape[0]
    tm = _row_tile(t)
    nj, fj, _ = wg4.shape

    def body(dhout_ref, h_ref, f_ref, p1_ref, p2_ref, wpre_ref, wg_ref, wu_ref, wd_ref, wpost_ref,
             dh_ref, df_ref, dg_ref, du_ref, dwpre_ref, dwpost_ref, dn_ref):
        i = pl.program_id(0)
        j = pl.program_id(1)

        @pl.when((i == 0) & (j == 0))
        def _():
            dwpre_ref[...] = jnp.zeros_like(dwpre_ref)
            dwpost_ref[...] = jnp.zeros_like(dwpost_ref)

        @pl.when(j == 0)
        def _():
            wpost = wpost_ref[...]
            _, fh, r = _rms(f_ref[...], wpost)
            df, dw = _rms_bwd(fh, r, wpost, 0.5 * dhout_ref[...])
            dwpost_ref[...] += dw
            df_ref[...] = df.astype(BF16)
            dn_ref[...] = jnp.zeros_like(dn_ref)

        parts = _row_parts(tm)
        das = [_dg(df_ref[rows, :], wd_ref[...], NT) for rows in parts]
        for rows, da in zip(parts, das):
            dg = (da * p1_ref[rows, :].astype(F32)).astype(BF16)
            du = (da * p2_ref[rows, :].astype(F32)).astype(BF16)
            dg_ref[rows, :] = dg
            du_ref[rows, :] = du
            dn_ref[rows, :] += _dot(dg, wg_ref[...]) + _dot(du, wu_ref[...])

        @pl.when(j == nj - 1)
        def _():
            wpre = wpre_ref[...]
            _, hh, r = _rms(h_ref[...], wpre)
            dx, dw = _rms_bwd(hh, r, wpre, dn_ref[...])
            dwpre_ref[...] += dw
            dh_ref[...] = dhout_ref[...] + dx

    row = pl.BlockSpec((tm, D_MODEL), lambda i, j: (i, 0))
    vec = pl.BlockSpec((1, D_MODEL), lambda i, j: (0, 0))
    wrow = pl.BlockSpec((None, fj, D_MODEL), lambda i, j: (j, 0, 0))
    act = pl.BlockSpec((None, tm, fj), lambda i, j: (j, i, 0))
    actshape = jax.ShapeDtypeStruct((nj, t, fj), BF16)
    return _pallas(
        body, name="ffn_bwd", grid=(t // tm, nj),
        in_specs=[row, row, row, act, act, vec, wrow, wrow, wrow, vec],
        out_specs=[row, row, act, act, vec, vec],
        out_shape=[jax.ShapeDtypeStruct((t, D_MODEL), F32), jax.ShapeDtypeStruct((t, D_MODEL), BF16),
                   actshape, actshape,
                   jax.ShapeDtypeStruct((1, D_MODEL), F32), jax.ShapeDtypeStruct((1, D_MODEL), F32)],
        scratch_shapes=[pltpu.VMEM((tm, D_MODEL), F32)],
        args=(dhout, h, f, p14, p24, wpre, wg4, wu4, wd4, wpost), hook=hook)


def _ffn_wgrad(n, df, dg4, du4, a4, hook=None):
    t = n.shape[0]
    tm = _contract_tile(t)
    ni = t // tm
    nj, _, fj = dg4.shape

    def body(n_ref, df_ref, dg_ref, du_ref, a_ref, dw_ref, acc):
        i = pl.program_id(1)

        @pl.when(i == 0)
        def _():
            acc[...] = jnp.zeros_like(acc)

        nn = n_ref[...]
        acc[0:fj, :] += _dg(dg_ref[...], nn, TN)
        acc[fj:2 * fj, :] += _dg(du_ref[...], nn, TN)
        acc[2 * fj:3 * fj, :] += _dg(a_ref[...], df_ref[...], TN)

        @pl.when(i == ni - 1)
        def _():
            dw_ref[...] = acc[...].astype(BF16)

    row = pl.BlockSpec((tm, D_MODEL), lambda j, i: (i, 0))
    act = pl.BlockSpec((None, tm, fj), lambda j, i: (j, i, 0))
    return _pallas(
        body, name="ffn_wgrad", grid=(nj, ni),
        in_specs=[row, row, act, act, act],
        out_specs=[pl.BlockSpec((None, 3 * fj, D_MODEL), lambda j, i: (j, 0, 0))],
        out_shape=[jax.ShapeDtypeStruct((nj, 3 * fj, D_MODEL), BF16)],
        scratch_shapes=[pltpu.VMEM((3 * fj, D_MODEL), F32)],
        args=(n, df, dg4, du4, a4), hook=hook)


def _rms_cast(h, w):
    t = h.shape[0]
    tm = _row_tile(t)

    def body(h_ref, w_ref, n_ref):
        y, _, _ = _rms(h_ref[...], w_ref[...])
        n_ref[...] = y.astype(BF16)

    row = pl.BlockSpec((tm, D_MODEL), lambda i: (i, 0))
    return pl.pallas_call(
        body, name="rms_cast", grid=(t // tm,), in_specs=[row, _full((1, D_MODEL))], out_specs=row,
        out_shape=jax.ShapeDtypeStruct((t, D_MODEL), BF16), compiler_params=_cparams(1),
    )(h, w)


FWD_RELATION = (None, 0, 1, 2)


def _ffn_fwd_gather(h, n, wbufs, wpost, qc_idx, late):
    t = h.shape[0]
    tm = _row_tile(t)
    ni = t // tm
    nj, fj, _ = wbufs[0].shape
    assert nj == N_CHIPS and ni >= 4
    nw = len(wbufs)
    n_lin, n_lout = len(late.inputs), len(late.out_shape)
    wait_step = ni - 3

    def body(qc_ref, h_ref, n_ref, wpost_ref, *rest):
        wb_in = rest[:nw]
        lins = rest[nw:nw + n_lin]
        o0 = nw + n_lin
        hout_ref, p1_ref, p2_ref, a_ref, f_hbm = rest[o0:o0 + 5]
        wb = rest[o0 + 5:o0 + 5 + nw]
        louts = rest[o0 + 5 + nw:o0 + 5 + nw + n_lout]
        s0 = o0 + 5 + nw + n_lout
        wv, wsem, send, recv, fbuf, fr_sem, fw_sem = rest[s0:s0 + 7]
        lscr = rest[s0 + 7:]
        p = pl.program_id(0)
        i = pl.program_id(1)
        step = p * ni + i
        fslot = step % 2

        def f_tile(tile):
            return f_hbm.at[pl.ds(pl.multiple_of(tile * tm, 8), tm)]

        @pl.when(step > 0)
        def _():
            pltpu.make_async_copy(fbuf.at[1 - fslot], f_tile(i), fw_sem.at[1 - fslot]).wait()

        nxt = step + 1

        @pl.when((nxt < N_CHIPS * ni) & (nxt >= ni))
        def _():
            pltpu.make_async_copy(f_tile(nxt % ni), fbuf.at[1 - fslot], fr_sem.at[1 - fslot]).start()

        @pl.when(p > 0)
        def _():
            pltpu.make_async_copy(f_tile(i), fbuf.at[fslot], fr_sem.at[fslot]).wait()
        x, y, c, chips = _place()
        q = 2 * x + y
        sibling = (x, y, 1 - c)
        mine, other = _half(fj, c), _half(fj, 1 - c)

        def load(chunk, slot, src):
            return [pltpu.make_async_copy(src[t].at[chunk], wv.at[slot, t], wsem.at[slot, t]) for t in range(nw)]

        @pl.when((p == 0) & (i == 0))
        def _():
            for t in range(nw):
                for j, (cx, cy) in enumerate(chips):
                    _remote(send.at[t, j], recv.at[t, j], wb_in[t].at[q, mine], wb[t].at[q, mine], (cx, cy, c)).start()
            late.start(lins, louts, lscr)
            for cp in load(q, 0, wb_in):
                cp.start()
            for cp in load(q, 0, wb_in):
                cp.wait()

        for pp in range(1, N_CHIPS):
            j = FWD_RELATION[pp]
            cx, cy = chips[j]
            chunk = 2 * cx + cy

            @pl.when((p == pp - 1) & (i == wait_step))
            def _(j=j, cx=cx, cy=cy, chunk=chunk, pp=pp):
                for t in range(nw):
                    got = wb[t].at[chunk, mine]
                    _remote(send.at[t, j], recv.at[t, j], got, got, (cx, cy, c)).wait_recv()
                    _remote(send.at[t, 3 + j], recv.at[t, 3 + j], got, got, sibling).start()
                for t in range(nw):
                    rest_half = wb[t].at[chunk, other]
                    _remote(send.at[t, 3 + j], recv.at[t, 3 + j], rest_half, rest_half, sibling).wait_recv()
                for cp in load(chunk, pp % 2, wb):
                    cp.start()

            @pl.when((p == pp) & (i == 0))
            def _(chunk=chunk, pp=pp):
                for cp in load(chunk, pp % 2, wb):
                    cp.wait()

        @pl.when((p == N_CHIPS - 1) & (i == 0))
        def _():
            late.mid(lins, louts, lscr)

        slot = p % 2
        nn = n_ref[...]
        g = _dg(nn, wv[slot, 0], NT)
        u = _dg(nn, wv[slot, 1], NT)
        sg = _sigmoid(g)
        silu = g * sg
        p1_ref[...] = (u * (sg + silu * (1.0 - sg))).astype(BF16)
        p2_ref[...] = silu.astype(BF16)
        a = (silu * u).astype(BF16)
        a_ref[...] = a
        part = _dot(a, wv[slot, 2])

        @pl.when(p == 0)
        def _():
            fbuf[fslot] = part

        @pl.when(p > 0)
        def _():
            fbuf[fslot] = fbuf[fslot] + part

        pltpu.make_async_copy(fbuf.at[fslot], f_tile(i), fw_sem.at[fslot]).start()

        @pl.when(p == N_CHIPS - 1)
        def _():
            yv, _, _ = _rms(fbuf[fslot], wpost_ref[...])
            hout_ref[...] = h_ref[...] + 0.5 * yv

        @pl.when((p == N_CHIPS - 1) & (i == ni - 1))
        def _():
            pltpu.make_async_copy(fbuf.at[fslot], f_tile(i), fw_sem.at[fslot]).wait()
            for t in range(nw):
                for j, (cx, cy) in enumerate(chips):
                    sent = wb[t].at[2 * cx + cy, mine]
                    _remote(send.at[t, j], recv.at[t, j], sent, sent, (cx, cy, c)).wait_send()
                    _remote(send.at[t, 3 + j], recv.at[t, 3 + j], sent, sent, sibling).wait_send()
            late.finish(lins, louts, lscr)

    def last_pass_rows(p, i, qc_ref):
        return (jnp.where(p == N_CHIPS - 1, i, 0), 0)

    def chunk_rows(p, i, qc_ref):
        order = ((p & 1) << 1) | (p >> 1)
        return (jnp.bitwise_xor(qc_ref[0], order), i, 0)

    row = pl.BlockSpec((tm, D_MODEL), lambda p, i, qc_ref: (i, 0))
    last_row = pl.BlockSpec((tm, D_MODEL), last_pass_rows)
    act = pl.BlockSpec((None, tm, fj), chunk_rows)
    act_shape = jax.ShapeDtypeStruct((nj, t, fj), BF16)
    res = pl.pallas_call(
        body, name="ffn_fwd_gather",
        grid_spec=pltpu.PrefetchScalarGridSpec(
            num_scalar_prefetch=1, grid=(N_CHIPS, ni),
            in_specs=[last_row, row, pl.BlockSpec((1, D_MODEL), lambda p, i, qc_ref: (0, 0))]
            + [ANY] * (nw + n_lin),
            out_specs=[last_row, act, act, act, ANY] + [ANY] * (nw + n_lout),
            scratch_shapes=[pltpu.VMEM((2, nw, fj, D_MODEL), BF16), pltpu.SemaphoreType.DMA((2, nw)),
                            pltpu.SemaphoreType.DMA((nw, 6)), pltpu.SemaphoreType.DMA((nw, 6)),
                            pltpu.VMEM((2, tm, D_MODEL), F32), pltpu.SemaphoreType.DMA((2,)),
                            pltpu.SemaphoreType.DMA((2,))] + list(late.scratch)),
        out_shape=[jax.ShapeDtypeStruct((t, D_MODEL), F32), act_shape, act_shape, act_shape,
                   jax.ShapeDtypeStruct((t, D_MODEL), F32)]
        + [jax.ShapeDtypeStruct(b.shape, b.dtype) for b in wbufs] + list(late.out_shape),
        input_output_aliases={**{4 + t: 5 + t for t in range(nw)},
                              **{4 + nw + a: 5 + nw + b for a, b in late.aliases}},
        compiler_params=_cparams(2),
    )(qc_idx, h, n, wpost, *wbufs, *late.inputs)
    return res[:5], res[5:5 + nw], res[5 + nw:]


PASS_RELATION = (2, 0, 1)


def _ffn_wgrad_reduce(n, df, dg4, du4, a4, qc_idx, hook):
    t = n.shape[0]
    tm = _contract_tile(t)
    ni = t // tm
    nj, _, fj = dg4.shape
    assert nj == N_CHIPS
    hrows = 3 * fj // 2
    n_hin, n_hout = len(hook.inputs), len(hook.out_shape)

    def body(qc_ref, n_ref, df_ref, dg_ref, du_ref, a_ref, *rest):
        hins = rest[:n_hin]
        own_ref, others_ref = rest[n_hin:n_hin + 2]
        houts = rest[n_hin + 2:n_hin + 2 + n_hout]
        s0 = n_hin + 2 + n_hout
        acc, stage, land, sumbuf, px_send, px_recv, cs_send, cs_recv, own_sem = rest[s0:s0 + 9]
        hscr = rest[s0 + 9:]
        k_pass = pl.program_id(0)
        i = pl.program_id(1)
        x, y, c, chips = _place()
        mine = pl.ds(pl.multiple_of(c * hrows, 8), hrows)
        other = pl.ds(pl.multiple_of((1 - c) * hrows, 8), hrows)

        def to_owner(k):
            j = PASS_RELATION[k]
            return _remote(cs_send.at[j], cs_recv.at[j], sumbuf.at[k % 2], others_ref.at[j], (*chips[j], c))

        @pl.when((k_pass == 0) & (i == 0))
        def _():
            hook.start(hins, houts, hscr)

        @pl.when(i == 0)
        def _():
            acc[...] = jnp.zeros_like(acc)

        nn = n_ref[...]
        acc[0:fj, :] += _dg(dg_ref[...], nn, TN)
        acc[fj:2 * fj, :] += _dg(du_ref[...], nn, TN)
        acc[2 * fj:3 * fj, :] += _dg(a_ref[...], df_ref[...], TN)

        for k in range(N_CHIPS):
            @pl.when((k_pass == k) & (i == ni - 1))
            def _(k=k):
                slot = k % 2
                stage[...] = acc[other, :].astype(BF16)
                swap = _remote(px_send.at[k], px_recv.at[k], stage, land.at[slot], (x, y, 1 - c))
                swap.start()
                swap.wait_recv()
                pair = acc[mine, :] + land[slot].astype(F32)
                if k >= 2:
                    to_owner(k - 2).wait_send()
                sumbuf[slot] = pair.astype(BF16)
                swap.wait_send()
                if k < N_CHIPS - 1:
                    to_owner(k).start()
                else:
                    keep = pltpu.make_async_copy(sumbuf.at[slot], own_ref, own_sem)
                    keep.start()
                    for j in range(N_CHIPS - 1):
                        _remote(cs_send.at[j], cs_recv.at[j], sumbuf.at[0], others_ref.at[j], (*chips[j], c)).wait_recv()
                    to_owner(k - 1).wait_send()
                    keep.wait()
                    hook.finish(hins, houts, hscr)

    def chunk(k_pass, i, qc_ref):
        return (jnp.bitwise_xor(qc_ref[0], N_CHIPS - 1 - k_pass), i, 0)

    row = pl.BlockSpec((tm, D_MODEL), lambda k_pass, i, qc_ref: (i, 0))
    act = pl.BlockSpec((None, tm, fj), chunk)
    res = pl.pallas_call(
        body, name="ffn_wgrad_reduce",
        grid_spec=pltpu.PrefetchScalarGridSpec(
            num_scalar_prefetch=1, grid=(N_CHIPS, ni),
            in_specs=[row, row, act, act, act] + [ANY] * n_hin,
            out_specs=[ANY, ANY] + [ANY] * n_hout,
            scratch_shapes=[pltpu.VMEM((3 * fj, D_MODEL), F32), pltpu.VMEM((hrows, D_MODEL), BF16),
                            pltpu.VMEM((2, hrows, D_MODEL), BF16), pltpu.VMEM((2, hrows, D_MODEL), BF16),
                            pltpu.SemaphoreType.DMA((N_CHIPS,)), pltpu.SemaphoreType.DMA((N_CHIPS,)),
                            pltpu.SemaphoreType.DMA((N_CHIPS - 1,)), pltpu.SemaphoreType.DMA((N_CHIPS - 1,)),
                            pltpu.SemaphoreType.DMA] + list(hook.scratch)),
        out_shape=[jax.ShapeDtypeStruct((hrows, D_MODEL), BF16),
                   jax.ShapeDtypeStruct((N_CHIPS - 1, hrows, D_MODEL), BF16)] + list(hook.out_shape),
        compiler_params=_cparams(2),
    )(qc_idx, n, df, dg4, du4, a4, *hook.inputs)
    return res[0], res[1], res[2:]


def _xty(x, y):
    t, k = x.shape
    n = y.shape[1]
    tm = _contract_tile(t)
    tn = n if n <= 1024 else (896 if n % 896 == 0 else 128)

    def body(x_ref, y_ref, o_ref):
        @pl.when(pl.program_id(1) == 0)
        def _():
            o_ref[...] = jnp.zeros_like(o_ref)

        o_ref[...] += _dg(x_ref[...], y_ref[...], TN)

    return pl.pallas_call(
        body, name="xty", grid=(n // tn, t // tm),
        in_specs=[pl.BlockSpec((tm, k), lambda j, i: (i, 0)), pl.BlockSpec((tm, tn), lambda j, i: (i, j))],
        out_specs=pl.BlockSpec((k, tn), lambda j, i: (0, j)),
        out_shape=jax.ShapeDtypeStruct((k, n), F32),
        compiler_params=_cparams(2),
    )(x, y)


def _rope_tables(t):
    pos = (jnp.arange(t, dtype=jnp.int32) - PAD).astype(F32)
    inv_freq = 1.0 / (ROPE_THETA ** (jnp.arange(0, SWA_HD, 2, dtype=F32) / SWA_HD))
    ang = pos[:, None] * inv_freq[None, :]
    cos = jnp.cos(ang)
    sin = jnp.sin(ang)
    return jnp.concatenate([cos, cos, cos, cos], axis=1), jnp.concatenate([-sin, sin, -sin, sin], axis=1)


def _rot_half(x, first_half):
    return jnp.where(first_half, pltpu.roll(x, 96, 1), pltpu.roll(x, 32, 1))


def _first_half_mask(rows):
    lane = lax.broadcasted_iota(jnp.int32, (rows, 128), 1)
    return (lane % 64) < 32


def _log_sigmoid(z):
    return jnp.minimum(z, 0.0) - jnp.log(1.0 + jnp.exp(-jnp.abs(z)))


def _mix_proj(h1, wmixpre, winp, wa2p, bap, cos, sin):
    t = h1.shape[0]
    tm = _row_tile(t)

    def body(h_ref, w_ref, win_ref, wa2_ref, ba_ref, cos_ref, sin_ref,
             n_ref, gq_ref, gk_ref, gv_ref, gg_ref, ga_ref, la_ref, sq_ref, sk_ref, sv_ref):
        y, _, _ = _rms(h_ref[...], w_ref[...])
        n = y.astype(BF16)
        n_ref[...] = n
        proj = _dot(n, win_ref[...])
        gq_ref[...] = proj[:, P_GQ:P_GK]
        gk_ref[...] = proj[:, P_GK:P_GV]
        gv_ref[...] = proj[:, P_GV:P_GG]
        gg_ref[...] = proj[:, P_GG:P_GA]
        ga = proj[:, P_GA:P_SQ]
        ga_ref[...] = ga
        z = _dot(ga.astype(BF16), wa2_ref[...]) + ba_ref[...]
        la_ref[...] = _log_sigmoid(z) * (1.0 / GLA_TAU)
        c = cos_ref[...]
        s = sin_ref[...]
        fh = _first_half_mask(tm)
        for k in range(4):
            x = proj[:, P_SQ + 128 * k:P_SQ + 128 * (k + 1)]
            sq_ref[:, 128 * k:128 * (k + 1)] = (x * c + _rot_half(x, fh) * s).astype(BF16)
        for k in range(2):
            x = proj[:, P_SK + 128 * k:P_SK + 128 * (k + 1)]
            sk_ref[:, 128 * k:128 * (k + 1)] = (x * c + _rot_half(x, fh) * s).astype(BF16)
        sv_ref[...] = proj[:, P_SV:P_END].astype(BF16)

    def row(w):
        return pl.BlockSpec((tm, w), lambda i: (i, 0))

    def rshape(w, dt):
        return jax.ShapeDtypeStruct((t, w), dt)

    return pl.pallas_call(
        body, name="mix_proj", grid=(t // tm,),
        in_specs=[row(D_MODEL), _full((1, D_MODEL)), _full((D_MODEL, P_END)), _full((128, GLA_KW)),
                  _full((1, GLA_KW)), row(128), row(128)],
        out_specs=[row(D_MODEL), row(256), row(256), row(512), row(512), row(128), row(256), row(512), row(256),
                   row(256)],
        out_shape=[rshape(D_MODEL, BF16), rshape(256, F32), rshape(256, F32), rshape(512, F32), rshape(512, F32),
                   rshape(128, F32), rshape(256, F32), rshape(512, BF16), rshape(256, BF16), rshape(256, BF16)],
        compiler_params=_cparams(1),
    )(h1, wmixpre, winp, wa2p, bap, cos, sin)


def _scan_rows(x, reverse=False):
    n = x.shape[0]
    row = lax.broadcasted_iota(jnp.int32, x.shape, 0)
    s = 1
    while s < n:
        if reverse:
            x = x + jnp.where(row < n - s, pltpu.roll(x, n - s, 0), 0.0)
        else:
            x = x + jnp.where(row >= s, pltpu.roll(x, s, 0), 0.0)
        s *= 2
    return x


def _gla_cumsum(la, tril_f):
    b = _scan_rows(la)
    row = lax.broadcasted_iota(jnp.int32, b.shape, 0)
    bm = jnp.sum(jnp.where(row == GLA_CHUNK // 2 - 1, b, 0.0), axis=0, keepdims=True)
    bl = jnp.sum(jnp.where(row == GLA_CHUNK - 1, b, 0.0), axis=0, keepdims=True)
    return b, bm, bl


def _gla_decays(la, tril_f):
    b, bm, bl = _gla_cumsum(la, tril_f)
    return jnp.exp(b - bm), jnp.exp(bm - b), jnp.exp(b), jnp.exp(bl - b), jnp.exp(bl)


def _gla_masks():
    c = GLA_CHUNK
    r = lax.broadcasted_iota(jnp.int32, (c, c), 0)
    col = lax.broadcasted_iota(jnp.int32, (c, c), 1)
    r4 = lax.broadcasted_iota(jnp.int32, (GLA_HEADS * c, c), 0) % c
    c4 = lax.broadcasted_iota(jnp.int32, (GLA_HEADS * c, c), 1)
    klane = lax.broadcasted_iota(jnp.int32, (c, GLA_KW), 1) // GLA_DK
    vlane = lax.broadcasted_iota(jnp.int32, (c, GLA_W), 1) // GLA_DV
    srow = lax.broadcasted_iota(jnp.int32, (GLA_W, GLA_KW), 0) // GLA_DV
    scol = lax.broadcasted_iota(jnp.int32, (GLA_W, GLA_KW), 1) // GLA_DK
    return dict(tril_f=(r >= col).astype(F32), triu_f=(r <= col).astype(F32), tril4=r4 >= c4,
                khead=[klane == h for h in range(GLA_HEADS)], vhead=[vlane == h for h in range(GLA_HEADS)],
                diag=srow == scol)


def _stack_heads(x, head_masks):
    return jnp.concatenate([jnp.where(m, x, 0.0) for m in head_masks], axis=0)


def _gla_fwd(gq, gk, gv, la):
    t = gq.shape[0]
    rg = _seq_tile(t)
    nb = t // rg
    ncb = rg // GLA_CHUNK
    c = GLA_CHUNK

    def body(q_ref, k_ref, v_ref, la_ref, o_ref, ss_ref, st_ref):
        @pl.when(pl.program_id(0) == 0)
        def _():
            st_ref[...] = jnp.zeros_like(st_ref)

        mk = _gla_masks()
        st = st_ref[...]
        for ch in range(ncb):
            rows = slice(ch * c, (ch + 1) * c)
            eq, ek, eb, ekl, ebl = _gla_decays(la_ref[rows, :], mk["tril_f"])
            qs = q_ref[rows, :] * (GLA_DK ** -0.5)
            k = k_ref[rows, :]
            v = v_ref[rows, :].astype(BF16)
            ss_ref[ch] = st
            q4 = _stack_heads(qs * eq, mk["khead"]).astype(BF16)
            a4 = jnp.where(mk["tril4"], _dg(q4, (k * ek).astype(BF16), NT), 0.0).astype(BF16)
            r4 = _dot(a4, v)
            intra = jnp.concatenate([r4[h * c:(h + 1) * c, GLA_DV * h:GLA_DV * (h + 1)] for h in range(GLA_HEADS)],
                                    axis=1)
            o_ref[rows, :] = intra + _dg((qs * eb).astype(BF16), st.astype(BF16), NT)
            st = st * ebl + jnp.where(mk["diag"], _dg(v, (k * ekl).astype(BF16), TN), 0.0)
        st_ref[...] = st

    def row(w):
        return pl.BlockSpec((rg, w), lambda i: (i, 0))

    return pl.pallas_call(
        body, name="gla_fwd", grid=(nb,),
        in_specs=[row(256), row(256), row(512), row(256)],
        out_specs=[row(512), pl.BlockSpec((ncb, GLA_W, GLA_KW), lambda i: (i, 0, 0))],
        out_shape=[jax.ShapeDtypeStruct((t, GLA_W), F32), jax.ShapeDtypeStruct((nb * ncb, GLA_W, GLA_KW), F32)],
        scratch_shapes=[pltpu.VMEM((GLA_W, GLA_KW), F32)],
        compiler_params=_cparams(1),
    )(gq, gk, gv, la)


def _gla_bwd(gq, gk, gv, la, ss, do):
    t = gq.shape[0]
    rg = _seq_tile(t)
    nb = t // rg
    ncb = rg // GLA_CHUNK
    c = GLA_CHUNK

    def body(q_ref, k_ref, v_ref, la_ref, ss_ref, do_ref, dq_ref, dk_ref, dv_ref, dla_ref, dst_ref):
        @pl.when(pl.program_id(0) == 0)
        def _():
            dst_ref[...] = jnp.zeros_like(dst_ref)

        mk = _gla_masks()
        last_row = lax.broadcasted_iota(jnp.int32, (c, GLA_KW), 0) == c - 1
        scale = GLA_DK ** -0.5
        dstn = dst_ref[...]
        for ch in reversed(range(ncb)):
            rows = slice(ch * c, (ch + 1) * c)
            eq, ek, eb, ekl, ebl = _gla_decays(la_ref[rows, :], mk["tril_f"])
            qs = q_ref[rows, :] * scale
            k = k_ref[rows, :]
            qt, kt, qh, kh = qs * eq, k * ek, qs * eb, k * ekl
            ktb, khb, qhb = kt.astype(BF16), kh.astype(BF16), qh.astype(BF16)
            v = v_ref[rows, :].astype(BF16)
            do_f = do_ref[rows, :]
            dob = do_f.astype(BF16)
            st = ss_ref[ch]
            stb = st.astype(BF16)
            dstb = dstn.astype(BF16)
            q4 = _stack_heads(qt, mk["khead"]).astype(BF16)
            do4 = _stack_heads(do_f, mk["vhead"]).astype(BF16)
            a4 = jnp.where(mk["tril4"], _dg(q4, ktb, NT), 0.0).astype(BF16)
            da4 = jnp.where(mk["tril4"], _dg(do4, v, NT), 0.0).astype(BF16)
            dv_ref[rows, :] = _dg(a4, do4, TN) + _dg(khb, dstb, NT)
            dq4 = _dot(da4, ktb)
            dqt = jnp.zeros((c, GLA_KW), F32)
            for h in range(GLA_HEADS):
                dqt = dqt + jnp.where(mk["khead"][h], dq4[h * c:(h + 1) * c], 0.0)
            dkt = _dg(da4, q4, TN)
            dqh = _dot(dob, stb)
            dkh = _dot(v, dstb)
            dbl = jnp.sum(dstn * st, axis=0, keepdims=True)
            dstn = dstn * ebl + jnp.where(mk["diag"], _dg(dob, qhb, TN), 0.0)
            dq_ref[rows, :] = scale * (dqt * eq + dqh * eb)
            dk_ref[rows, :] = dkt * ek + dkh * ekl
            dkk = dkh * kh
            db = dqt * qt - dkt * kt + dqh * qh - dkk
            db = db + jnp.where(last_row, jnp.sum(dkk, axis=0, keepdims=True) + ebl * dbl, 0.0)
            dla_ref[rows, :] = _scan_rows(db, reverse=True)
        dst_ref[...] = dstn

    def row(w):
        return pl.BlockSpec((rg, w), lambda i: (nb - 1 - i, 0))

    def rshape(w):
        return jax.ShapeDtypeStruct((t, w), F32)

    return pl.pallas_call(
        body, name="gla_bwd", grid=(nb,),
        in_specs=[row(256), row(256), row(512), row(256),
                  pl.BlockSpec((ncb, GLA_W, GLA_KW), lambda i: (nb - 1 - i, 0, 0)), row(512)],
        out_specs=[row(256), row(256), row(512), row(256)],
        out_shape=[rshape(256), rshape(256), rshape(512), rshape(256)],
        scratch_shapes=[pltpu.VMEM((GLA_W, GLA_KW), F32)],
        compiler_params=_cparams(1),
    )(gq, gk, gv, la, ss, do)


SWA_G = SWA_QH // SWA_KVH


def _swa_bias():
    n = jnp.arange(3, dtype=jnp.int32)[:, None, None]
    r = (jnp.arange(SWA_G * BLK, dtype=jnp.int32) % BLK)[None, :, None]
    c = jnp.arange(3 * BLK, dtype=jnp.int32)[None, None, :]
    seg = c // BLK
    cc = c % BLK
    qpos = n * BLK + r - PAD
    kpos = jnp.where(seg == 0, (n - 1) * BLK, jnp.where(seg == 1, n * BLK, 0)) + cc - PAD
    band = (seg < 2) & (kpos >= N_META) & (kpos <= qpos) & (qpos - kpos < WINDOW)
    meta = (seg == 2) & (kpos >= 0) & (kpos < N_META) & (kpos <= qpos)
    return jnp.where(band | meta, 0.0, NEG_INF).astype(F32)


def _swa_stack(ref, rows, kh, lo, dtype):
    parts = []
    for g in range(2):
        pair = ref[rows, 128 * (2 * kh + g):128 * (2 * kh + g + 1)]
        zero = jnp.zeros_like(pair)
        parts += [jnp.where(lo, pair, zero), jnp.where(lo, zero, pair)]
    return jnp.concatenate(parts, axis=0).astype(dtype)


def _swa_unstack(x4, lo):
    return [jnp.where(lo, x4[2 * g * BLK:(2 * g + 1) * BLK], x4[(2 * g + 1) * BLK:(2 * g + 2) * BLK])
            for g in range(2)]


def _swa_sink_col(sink_ref, kh):
    blk = lax.broadcasted_iota(jnp.int32, (SWA_G * BLK, 1), 0) // BLK
    col = jnp.full((SWA_G * BLK, 1), sink_ref[SWA_G * kh + SWA_G - 1], F32)
    for e in reversed(range(SWA_G - 1)):
        col = jnp.where(blk == e, sink_ref[SWA_G * kh + e], col)
    return col


def _swa_probs(q4, kall, bias, sink):
    s = _dg(q4, kall, NT) * (SWA_HD ** -0.5) + bias
    m = jnp.maximum(jnp.max(s, axis=-1, keepdims=True), sink)
    p = jnp.exp(s - m)
    es = jnp.exp(sink - m)
    inv = 1.0 / (jnp.sum(p, axis=-1, keepdims=True) + es)
    return p * inv, es * inv


def _swa_keys(prev_ref, cur_ref, first_ref, b, ls):
    before = prev_ref[:, ls] if b == 0 else cur_ref[(b - 1) * BLK:b * BLK, ls]
    return jnp.concatenate([before, cur_ref[b * BLK:(b + 1) * BLK, ls], first_ref[:, ls]], axis=0)


def _swa_specs(rs, ns):
    bps = rs // BLK
    cur = lambda w: pl.BlockSpec((rs, w), lambda i: (jnp.minimum(i, ns - 1), 0))
    prev = lambda w: pl.BlockSpec((BLK, w), lambda i: (jnp.maximum(jnp.minimum(i, ns - 1) * bps - 1, 0), 0))
    first = lambda w: pl.BlockSpec((BLK, w), lambda i: (0, 0))
    return cur, prev, first


def _swa_fwd(sinks, sq, sk, sv):
    t = sq.shape[0]
    rs = _seq_tile(t)
    bps, ns = rs // BLK, t // rs

    def body(sink_ref, bias_ref, q_ref, kp_ref, kc_ref, km_ref, vp_ref, vc_ref, vm_ref, o_ref):
        i = pl.program_id(0)
        lo = lax.broadcasted_iota(jnp.int32, (BLK, 128), 1) < 64
        sink_cols = [_swa_sink_col(sink_ref, kh) for kh in range(SWA_KVH)]
        for b in range(bps):
            rows = slice(b * BLK, (b + 1) * BLK)
            bias = bias_ref[jnp.minimum(i * bps + b, 2)]
            for kh in range(SWA_KVH):
                ls = slice(128 * kh, 128 * (kh + 1))
                kall = _swa_keys(kp_ref, kc_ref, km_ref, b, ls)
                vall = _swa_keys(vp_ref, vc_ref, vm_ref, b, ls)
                p, _ = _swa_probs(_swa_stack(q_ref, rows, kh, lo, BF16), kall, bias, sink_cols[kh])
                for g, pair in enumerate(_swa_unstack(_dot(p.astype(BF16), vall), lo)):
                    o_ref[rows, 128 * (2 * kh + g):128 * (2 * kh + g + 1)] = pair

    cur, prev, first = _swa_specs(rs, ns)
    bias = _swa_bias()
    return pl.pallas_call(
        body, name="swa_fwd", grid=(ns,),
        in_specs=[pl.BlockSpec(memory_space=pltpu.SMEM), _full(bias.shape), cur(512), prev(256), cur(256), first(256),
                  prev(256), cur(256), first(256)],
        out_specs=cur(512),
        out_shape=jax.ShapeDtypeStruct((t, SWA_W), F32),
        compiler_params=_cparams(1),
    )(sinks, bias, sq, sk, sk, sk, sv, sv, sv)


def _swa_bwd(sinks, sq, sk, sv, o, do, hook=None):
    t = sq.shape[0]
    rs = _seq_tile(t)
    bps, ns = rs // BLK, t // rs

    def body(sink_ref, bias_ref, q_ref, kp_ref, kc_ref, km_ref, vp_ref, vc_ref, vm_ref, o_ref, do_ref,
             dq_ref, dk_ref, dv_ref, dkm_ref, dvm_ref, dsink_ref, pk_ref, pv_ref):
        i = pl.program_id(0)

        @pl.when(i == 0)
        def _():
            pk_ref[...] = jnp.zeros_like(pk_ref)
            pv_ref[...] = jnp.zeros_like(pv_ref)
            dkm_ref[...] = jnp.zeros_like(dkm_ref)
            dvm_ref[...] = jnp.zeros_like(dvm_ref)
            dsink_ref[...] = jnp.zeros_like(dsink_ref)

        @pl.when(i == ns)
        def _():
            dk_ref[...] = pk_ref[...]
            dv_ref[...] = pv_ref[...]

        @pl.when(i < ns)
        def _():
            lo = lax.broadcasted_iota(jnp.int32, (BLK, 128), 1) < 64
            scale = SWA_HD ** -0.5
            sink_cols = [_swa_sink_col(sink_ref, kh) for kh in range(SWA_KVH)]
            parts_k = [[None] * SWA_KVH for _ in range(bps)]
            parts_v = [[None] * SWA_KVH for _ in range(bps)]
            dsinks = [jnp.zeros((1, 1), F32) for _ in range(SWA_QH)]
            for b in range(bps):
                rows = slice(b * BLK, (b + 1) * BLK)
                bias = bias_ref[jnp.minimum(i * bps + b, 2)]
                for kh in range(SWA_KVH):
                    ls = slice(128 * kh, 128 * (kh + 1))
                    kall = _swa_keys(kp_ref, kc_ref, km_ref, b, ls)
                    vall = _swa_keys(vp_ref, vc_ref, vm_ref, b, ls)
                    q4 = _swa_stack(q_ref, rows, kh, lo, BF16)
                    do4 = _swa_stack(do_ref, rows, kh, lo, F32)
                    p, psink = _swa_probs(q4, kall, bias, sink_cols[kh])
                    delta = jnp.sum(do4 * _swa_stack(o_ref, rows, kh, lo, F32), axis=-1, keepdims=True)
                    do4b = do4.astype(BF16)
                    ds = (p * (_dg(do4b, vall, NT) - delta) * scale).astype(BF16)
                    for g, pair in enumerate(_swa_unstack(_dot(ds, kall), lo)):
                        dq_ref[rows, 128 * (2 * kh + g):128 * (2 * kh + g + 1)] = pair
                    parts_k[b][kh] = _dg(ds, q4, TN)
                    parts_v[b][kh] = _dg(p.astype(BF16), do4b, TN)
                    dsk = psink * delta
                    for e in range(SWA_G):
                        h = SWA_G * kh + e
                        dsinks[h] = dsinks[h] - jnp.sum(dsk[e * BLK:(e + 1) * BLK], axis=0, keepdims=True)
            last = slice(rs - BLK, rs)
            for parts, out_ref, pend_ref, meta_ref in ((parts_k, dk_ref, pk_ref, dkm_ref),
                                                       (parts_v, dv_ref, pv_ref, dvm_ref)):
                for kh in range(SWA_KVH):
                    ls = slice(128 * kh, 128 * (kh + 1))
                    if bps > 1:
                        out_ref[0:rs - BLK, ls] = pend_ref[0:rs - BLK, ls]
                    out_ref[last, ls] = pend_ref[last, ls] + parts[0][kh][0:BLK]
                    meta = parts[0][kh][2 * BLK:3 * BLK]
                    for b in range(bps):
                        own = parts[b][kh][BLK:2 * BLK]
                        if b + 1 < bps:
                            own = own + parts[b + 1][kh][0:BLK]
                            meta = meta + parts[b + 1][kh][2 * BLK:3 * BLK]
                        pend_ref[b * BLK:(b + 1) * BLK, ls] = own
                    meta_ref[:, ls] += meta
            for h in range(SWA_QH):
                dsink_ref[h:h + 1, :] += jnp.broadcast_to(dsinks[h], (1, 128))

    cur, prev, first = _swa_specs(rs, ns)
    late = lambda w: pl.BlockSpec((rs, w), lambda i: (jnp.maximum(i - 1, 0), 0))
    bias = _swa_bias()
    return _pallas(
        body, name="swa_bwd", grid=(ns + 1,),
        in_specs=[pl.BlockSpec(memory_space=pltpu.SMEM), _full(bias.shape), cur(512), prev(256), cur(256), first(256),
                  prev(256), cur(256), first(256), cur(512), cur(512)],
        out_specs=[cur(512), late(256), late(256), first(256), first(256), _full((SWA_QH, 128))],
        out_shape=[jax.ShapeDtypeStruct((t, SWA_W), F32), jax.ShapeDtypeStruct((t, 256), F32),
                   jax.ShapeDtypeStruct((t, 256), F32), jax.ShapeDtypeStruct((BLK, 256), F32),
                   jax.ShapeDtypeStruct((BLK, 256), F32), jax.ShapeDtypeStruct((SWA_QH, 128), F32)],
        scratch_shapes=[pltpu.VMEM((rs, 256), F32), pltpu.VMEM((rs, 256), F32)],
        args=(sinks, bias, sq, sk, sk, sk, sv, sv, sv, o, do), hook=hook)


def _mix_out(h1, ogla, gg, oswa, wgn, wsn, wout, wpost):
    t = h1.shape[0]
    tm = _row_tile(t)

    def body(h_ref, og_ref, gg_ref, os_ref, wgn_ref, wsn_ref, wout_ref, wpost_ref, h2_ref, cat_ref, m_ref):
        parts = []
        for h in range(GLA_HEADS):
            ls = slice(GLA_DV * h, GLA_DV * (h + 1))
            y, _, _ = _rms(og_ref[:, ls], wgn_ref[...])
            g = gg_ref[:, ls]
            parts.append(y * (g * _sigmoid(g)))
        ys, _, _ = _rms(os_ref[...], wsn_ref[...])
        cat = jnp.concatenate(parts + [ys], axis=1).astype(BF16)
        cat_ref[...] = cat
        m = _dot(cat, wout_ref[...])
        m_ref[...] = m
        y, _, _ = _rms(m, wpost_ref[...])
        h2_ref[...] = h_ref[...] + y

    def row(w):
        return pl.BlockSpec((tm, w), lambda i: (i, 0))

    return pl.pallas_call(
        body, name="mix_out", grid=(t // tm,),
        in_specs=[row(D_MODEL), row(512), row(512), row(512), _full((1, GLA_DV)), _full((1, SWA_W)),
                  _full((D_MODEL, D_MODEL)), _full((1, D_MODEL))],
        out_specs=[row(D_MODEL), row(D_MODEL), row(D_MODEL)],
        out_shape=[jax.ShapeDtypeStruct((t, D_MODEL), F32), jax.ShapeDtypeStruct((t, D_MODEL), BF16),
                   jax.ShapeDtypeStruct((t, D_MODEL), F32)],
        compiler_params=_cparams(1),
    )(h1, ogla, gg, oswa, wgn, wsn, wout, wpost)


def _mix_out_bwd(dh2, m, ogla, gg, oswa, wgn, wsn, wout, wpost, hook=None):
    t = dh2.shape[0]
    tm = _row_tile(t)

    def body(dh_ref, m_ref, og_ref, gg_ref, os_ref, wgn_ref, wsn_ref, wout_ref, wpost_ref,
             dog_ref, dgg_ref, dos_ref, dm_ref, dwpost_ref, dwgn_ref, dwsn_ref):
        @pl.when(pl.program_id(0) == 0)
        def _():
            dwpost_ref[...] = jnp.zeros_like(dwpost_ref)
            dwgn_ref[...] = jnp.zeros_like(dwgn_ref)
            dwsn_ref[...] = jnp.zeros_like(dwsn_ref)

        wpost = wpost_ref[...]
        _, mh, r = _rms(m_ref[...], wpost)
        dm, dw = _rms_bwd(mh, r, wpost, dh_ref[...])
        dwpost_ref[...] += dw
        dmb = dm.astype(BF16)
        dm_ref[...] = dmb
        dcat = _dg(dmb, wout_ref[...], NT)
        wgn = wgn_ref[...]
        for h in range(GLA_HEADS):
            ls = slice(GLA_DV * h, GLA_DV * (h + 1))
            dog = dcat[:, ls]
            g = gg_ref[:, ls]
            sg = _sigmoid(g)
            y, xh, r = _rms(og_ref[:, ls], wgn)
            dgg_ref[:, ls] = dog * y * (sg * (1.0 + g * (1.0 - sg)))
            dx, dw = _rms_bwd(xh, r, wgn, dog * (g * sg))
            dog_ref[:, ls] = dx
            dwgn_ref[...] += dw
        wsn = wsn_ref[...]
        _, xh, r = _rms(os_ref[...], wsn)
        dx, dw = _rms_bwd(xh, r, wsn, dcat[:, GLA_W:])
        dos_ref[...] = dx
        dwsn_ref[...] += dw

    def row(w):
        return pl.BlockSpec((tm, w), lambda i: (i, 0))

    def rshape(w, dt=F32):
        return jax.ShapeDtypeStruct((t, w), dt)

    return _pallas(
        body, name="mix_out_bwd", grid=(t // tm,),
        in_specs=[row(D_MODEL), row(D_MODEL), row(512), row(512), row(512), _full((1, GLA_DV)), _full((1, SWA_W)),
                  _full((D_MODEL, D_MODEL)), _full((1, D_MODEL))],
        out_specs=[row(512), row(512), row(512), row(D_MODEL), _full((1, D_MODEL)), _full((1, GLA_DV)),
                   _full((1, SWA_W))],
        out_shape=[rshape(512), rshape(512), rshape(512), rshape(D_MODEL, BF16),
                   jax.ShapeDtypeStruct((1, D_MODEL), F32), jax.ShapeDtypeStruct((1, GLA_DV), F32),
                   jax.ShapeDtypeStruct((1, SWA_W), F32)],
        args=(dh2, m, ogla, gg, oswa, wgn, wsn, wout, wpost), hook=hook)


def _mix_in_bwd(dh2, h1, wmixpre, winp, wa2p, bap, cos, sin, ga, dgq, dgk, dgv, dgg, dla, dsq, dsk, dsv, dkm, dvm):
    t = h1.shape[0]
    tm = _row_tile(t)

    def body(dh2_ref, h_ref, w_ref, win_ref, wa2_ref, ba_ref, cos_ref, sin_ref, ga_ref, dgq_ref, dgk_ref, dgv_ref,
             dgg_ref, dla_ref, dsq_ref, dsk_ref, dsv_ref, dkm_ref, dvm_ref,
             dh1_ref, dproj_ref, dw_ref, dwa2_ref, dba_ref):
        i = pl.program_id(0)

        @pl.when(i == 0)
        def _():
            dw_ref[...] = jnp.zeros_like(dw_ref)
            dwa2_ref[...] = jnp.zeros_like(dwa2_ref)
            dba_ref[...] = jnp.zeros_like(dba_ref)

        first = (i == 0).astype(F32)
        c = cos_ref[...]
        s = -sin_ref[...]
        fh = _first_half_mask(tm)
        dproj_ref[:, P_GQ:P_GK] = dgq_ref[...].astype(BF16)
        dproj_ref[:, P_GK:P_GV] = dgk_ref[...].astype(BF16)
        dproj_ref[:, P_GV:P_GG] = dgv_ref[...].astype(BF16)
        dproj_ref[:, P_GG:P_GA] = dgg_ref[...].astype(BF16)
        gab = ga_ref[...].astype(BF16)
        z = _dot(gab, wa2_ref[...]) + ba_ref[...]
        row_id = i * tm + lax.broadcasted_iota(jnp.int32, (tm, 1), 0)
        dz = jnp.where(row_id >= PAD, dla_ref[...] * (1.0 / GLA_TAU) * (1.0 - _sigmoid(z)), 0.0)
        dzb = dz.astype(BF16)
        dba_ref[...] += jnp.sum(dz, axis=0, keepdims=True)
        dwa2_ref[...] += _dg(gab, dzb, TN)
        dproj_ref[:, P_GA:P_SQ] = _dg(dzb, wa2_ref[...], NT).astype(BF16)
        for k in range(4):
            dy = dsq_ref[:, 128 * k:128 * (k + 1)]
            dproj_ref[:, P_SQ + 128 * k:P_SQ + 128 * (k + 1)] = (dy * c + _rot_half(dy, fh) * s).astype(BF16)
        for k in range(2):
            ls = slice(128 * k, 128 * (k + 1))
            dy = dsk_ref[:, ls]
            dy = jnp.concatenate([dy[:BLK] + first * dkm_ref[:, ls], dy[BLK:]], axis=0) if tm > BLK else (
                dy + first * dkm_ref[:, ls])
            dproj_ref[:, P_SK + 128 * k:P_SK + 128 * (k + 1)] = (dy * c + _rot_half(dy, fh) * s).astype(BF16)
            dv = dsv_ref[:, ls]
            dv = jnp.concatenate([dv[:BLK] + first * dvm_ref[:, ls], dv[BLK:]], axis=0) if tm > BLK else (
                dv + first * dvm_ref[:, ls])
            dproj_ref[:, P_SV + 128 * k:P_SV + 128 * (k + 1)] = dv.astype(BF16)
        dn = _dg(dproj_ref[...], win_ref[...], NT)
        w = w_ref[...]
        _, hh, r = _rms(h_ref[...], w)
        dx, dw = _rms_bwd(hh, r, w, dn)
        dw_ref[...] += dw
        dh1_ref[...] = dh2_ref[...] + dx

    def row(w):
        return pl.BlockSpec((tm, w), lambda i: (i, 0))

    return pl.pallas_call(
        body, name="mix_in_bwd", grid=(t // tm,),
        in_specs=[row(D_MODEL), row(D_MODEL), _full((1, D_MODEL)), _full((D_MODEL, P_END)), _full((128, GLA_KW)),
                  _full((1, GLA_KW)), row(128), row(128), row(128), row(256), row(256), row(512), row(512), row(256),
                  row(512), row(256), row(256), _full((BLK, 256)), _full((BLK, 256))],
        out_specs=[row(D_MODEL), row(P_END), _full((1, D_MODEL)), _full((128, GLA_KW)), _full((1, GLA_KW))],
        out_shape=[jax.ShapeDtypeStruct((t, D_MODEL), F32), jax.ShapeDtypeStruct((t, P_END), BF16),
                   jax.ShapeDtypeStruct((1, D_MODEL), F32), jax.ShapeDtypeStruct((128, GLA_KW), F32),
                   jax.ShapeDtypeStruct((1, GLA_KW), F32)],
        compiler_params=_cparams(1),
    )(dh2, h1, wmixpre, winp, wa2p, bap, cos, sin, ga, dgq, dgk, dgv, dgg, dla, dsq, dsk, dsv, dkm, dvm)


def _adamw_update(w, g, m, v):
    m = ADAM_B1 * m + (1.0 - ADAM_B1) * g
    v = ADAM_B2 * v + (1.0 - ADAM_B2) * (g * g)
    m_hat = m / (1.0 - ADAM_B1 ** ADAM_STEP)
    v_hat = v / (1.0 - ADAM_B2 ** ADAM_STEP)
    return -ADAM_LR * (m_hat / (jnp.sqrt(v_hat) + ADAM_EPS) + ADAM_WD * w), m, v


def _adamw(w, g, m, v):
    r, c = w.shape
    tr = _div_tile(r)

    def body(w_ref, g_ref, m_ref, v_ref, d_ref, nm_ref, nv_ref):
        d_ref[...], nm_ref[...], nv_ref[...] = _adamw_update(w_ref[...], g_ref[...], m_ref[...], v_ref[...])

    spec = pl.BlockSpec((tr, c), lambda i: (i, 0))
    shape = jax.ShapeDtypeStruct((r, c), F32)
    return pl.pallas_call(
        body, name="adamw", grid=(r // tr,), in_specs=[spec] * 4, out_specs=[spec] * 3, out_shape=[shape] * 3,
        compiler_params=_cparams(1),
    )(w, g, m, v)


def _adamw_halves(w, g_mine, g_other, m, v, c_idx, row0=0):
    r, c = w.shape
    h = g_mine.shape[0]
    tr = _div_tile(math.gcd(r, h))
    nth = h // tr
    t0 = row0 // tr
    assert t0 * tr == row0

    def body(c_ref, w_ref, gm_ref, go_ref, m_ref, v_ref, g_ref, d_ref, nm_ref, nv_ref):
        hh = (t0 + pl.program_id(0)) // nth
        g = jnp.where(hh == c_ref[0], gm_ref[...], go_ref[...])
        g_ref[...] = g
        d_ref[...], nm_ref[...], nv_ref[...] = _adamw_update(w_ref[...], g, m_ref[...], v_ref[...])

    spec = pl.BlockSpec((tr, c), lambda i, c_ref: (i, 0))
    gspec = pl.BlockSpec((tr, c), lambda i, c_ref: ((t0 + i) % nth, 0))
    shape = jax.ShapeDtypeStruct((r, c), F32)
    return pl.pallas_call(
        body, name="adamw_halves",
        grid_spec=pltpu.PrefetchScalarGridSpec(
            num_scalar_prefetch=1, grid=(r // tr,), in_specs=[spec, gspec, gspec, spec, spec], out_specs=[spec] * 4),
        out_shape=[shape] * 4, compiler_params=_cparams(1),
    )(c_idx, w, g_mine, g_other, m, v)


def _place():
    x, y, c = lax.axis_index("x"), lax.axis_index("y"), lax.axis_index("c")
    chips = [(1 - x, y), (x, 1 - y), (1 - x, 1 - y)]
    return x, y, c, chips


def _remote(send_sem, recv_sem, src, dst, to):
    return pltpu.make_async_remote_copy(src_ref=src, dst_ref=dst, send_sem=send_sem, recv_sem=recv_sem,
                                        device_id=to, device_id_type=MESH)


def _half(ref_rows, c):
    h = ref_rows // 2
    return pl.ds(pl.multiple_of(c * h, 8), h)


def _own_slot(shard, q):
    return lax.dynamic_update_slice(jnp.zeros((N_CHIPS,) + shard.shape, shard.dtype), shard[None], (q, 0, 0))


class _GatherChips:
    has_mid = True

    def __init__(self, bufs):
        n = len(bufs)
        self.inputs = list(bufs)
        self.out_shape = [jax.ShapeDtypeStruct(b.shape, b.dtype) for b in bufs]
        self.aliases = [(t, t) for t in range(n)]
        self.scratch = [pltpu.SemaphoreType.DMA((n, 6)), pltpu.SemaphoreType.DMA((n, 6))]

    def start(self, ins, outs, scr):
        send, recv = scr
        x, y, c, chips = _place()
        q = 2 * x + y
        for t, (i_ref, o_ref) in enumerate(zip(ins, outs)):
            rows = _half(i_ref.shape[1], c)
            for j, (cx, cy) in enumerate(chips):
                _remote(send.at[t, j], recv.at[t, j], i_ref.at[q, rows], o_ref.at[q, rows], (cx, cy, c)).start()

    def mid(self, ins, outs, scr):
        send, recv = scr
        x, y, c, chips = _place()
        for t, o_ref in enumerate(outs):
            rows = _half(o_ref.shape[1], c)
            for j, (cx, cy) in enumerate(chips):
                slot = o_ref.at[2 * cx + cy, rows]
                _remote(send.at[t, j], recv.at[t, j], slot, slot, (cx, cy, c)).wait_recv()
                _remote(send.at[t, 3 + j], recv.at[t, 3 + j], slot, slot, (x, y, 1 - c)).start()

    def finish(self, ins, outs, scr):
        send, recv = scr
        x, y, c, chips = _place()
        for t, o_ref in enumerate(outs):
            mine, other = _half(o_ref.shape[1], c), _half(o_ref.shape[1], 1 - c)
            for j, (cx, cy) in enumerate(chips):
                slot = o_ref.at[2 * cx + cy, other]
                _remote(send.at[t, 3 + j], recv.at[t, 3 + j], slot, slot, (x, y, 1 - c)).wait_recv()
            for j, (cx, cy) in enumerate(chips):
                sent = o_ref.at[2 * cx + cy, mine]
                _remote(send.at[t, j], recv.at[t, j], sent, sent, (cx, cy, c)).wait_send()
                _remote(send.at[t, 3 + j], recv.at[t, 3 + j], sent, sent, (x, y, 1 - c)).wait_send()


class _PairExchange:
    has_mid = False
    aliases = ()

    def __init__(self, arrs):
        n = len(arrs)
        self.inputs = list(arrs)
        self.out_shape = [jax.ShapeDtypeStruct((a.shape[0], a.shape[1] // 2, a.shape[2]), a.dtype) for a in arrs]
        self.scratch = [pltpu.SemaphoreType.DMA((n,)), pltpu.SemaphoreType.DMA((n,))]

    def _copies(self, ins, outs, scr):
        send, recv = scr
        x, y, c, _ = _place()
        return [_remote(send.at[t], recv.at[t], i_ref.at[:, _half(i_ref.shape[1], 1 - c)], o_ref, (x, y, 1 - c))
                for t, (i_ref, o_ref) in enumerate(zip(ins, outs))]

    def start(self, ins, outs, scr):
        for cp in self._copies(ins, outs, scr):
            cp.start()

    def finish(self, ins, outs, scr):
        for cp in self._copies(ins, outs, scr):
            cp.wait()


class _ChipScatter:
    has_mid = False
    aliases = ()

    def __init__(self, arrs):
        n = len(arrs)
        self.inputs = list(arrs)
        self.out_shape = [jax.ShapeDtypeStruct((3,) + a.shape[1:], a.dtype) for a in arrs]
        self.scratch = [pltpu.SemaphoreType.DMA((n, 3)), pltpu.SemaphoreType.DMA((n, 3))]

    def _copies(self, ins, outs, scr):
        send, recv = scr
        x, y, c, chips = _place()
        return [_remote(send.at[t, j], recv.at[t, j], i_ref.at[2 * cx + cy], o_ref.at[j], (cx, cy, c))
                for t, (i_ref, o_ref) in enumerate(zip(ins, outs)) for j, (cx, cy) in enumerate(chips)]

    def start(self, ins, outs, scr):
        for cp in self._copies(ins, outs, scr):
            cp.start()

    def finish(self, ins, outs, scr):
        for cp in self._copies(ins, outs, scr):
            cp.wait()


class _PairShare:
    has_mid = False
    aliases = ()

    def __init__(self, arrs):
        n = len(arrs)
        self.inputs = list(arrs)
        self.out_shape = [jax.ShapeDtypeStruct(a.shape, a.dtype) for a in arrs]
        self.scratch = [pltpu.SemaphoreType.DMA((n,)), pltpu.SemaphoreType.DMA((n,))]

    def _copies(self, ins, outs, scr):
        send, recv = scr
        x, y, c, _ = _place()
        return [_remote(send.at[t], recv.at[t], i_ref, o_ref, (x, y, 1 - c))
                for t, (i_ref, o_ref) in enumerate(zip(ins, outs))]

    def start(self, ins, outs, scr):
        for cp in self._copies(ins, outs, scr):
            cp.start()

    def finish(self, ins, outs, scr):
        for cp in self._copies(ins, outs, scr):
            cp.wait()


def _comm_call(hook, name):
    n_in, n_out = len(hook.inputs), len(hook.out_shape)

    def body(*refs):
        ins, outs, scr = refs[:n_in], refs[n_in:n_in + n_out], refs[n_in + n_out:]
        hook.start(ins, outs, scr)
        if hook.has_mid:
            hook.mid(ins, outs, scr)
        hook.finish(ins, outs, scr)

    return pl.pallas_call(body, name=name, in_specs=[ANY] * n_in, out_specs=[ANY] * n_out,
                          out_shape=list(hook.out_shape), scratch_shapes=list(hook.scratch),
                          input_output_aliases=dict(hook.aliases))(*hook.inputs)


def _all_gather_devices(vecs):
    n = len(vecs)

    def body(*refs):
        x_refs, out_refs = refs[:n], refs[n:2 * n]
        send_sems, recv_sems, local_sems = refs[2 * n:]
        x, y, c, chips = _place()
        me, sibling = (x, y, c), (x, y, 1 - c)
        waits = []
        for t, (x_ref, out_ref) in enumerate(zip(x_refs, out_refs)):
            def slot(px, py, pc, out_ref=out_ref):
                return out_ref.at[4 * px + 2 * py + pc]

            def copy(k, block, to, src=None, t=t, slot=slot):
                return pltpu.make_async_remote_copy(
                    src_ref=slot(*block) if src is None else src, dst_ref=slot(*block), send_sem=send_sems.at[t, k],
                    recv_sem=recv_sems.at[t, k], device_id=to, device_id_type=MESH)

            mine = pltpu.make_async_copy(x_ref, slot(*me), local_sems.at[t])
            mine.start()
            first = [copy(0, me, sibling, src=x_ref)]
            first += [copy(1 + j, me, (*chip, c), src=x_ref) for j, chip in enumerate(chips)]
            for cp in first:
                cp.start()
            waits.append((copy, mine, first))
        for copy, mine, first in waits:
            passed = [copy(4 + j, (*chip, c), sibling) for j, chip in enumerate(chips)]
            for j, chip in enumerate(chips):
                copy(1 + j, (*chip, c), me).wait_recv()
                passed[j].start()
            copy(0, sibling, me).wait_recv()
            for j, chip in enumerate(chips):
                copy(4 + j, (*chip, 1 - c), me).wait_recv()
            for cp in first + passed:
                cp.wait_send()
            mine.wait()

    vmem = pl.BlockSpec(memory_space=pltpu.VMEM)
    return pl.pallas_call(
        body, name="all_gather_devices", in_specs=[vmem] * n, out_specs=[vmem] * n,
        out_shape=[jax.ShapeDtypeStruct((N_DEV,) + v.shape, v.dtype) for v in vecs],
        scratch_shapes=[pltpu.SemaphoreType.DMA((n, 7)), pltpu.SemaphoreType.DMA((n, 7)),
                        pltpu.SemaphoreType.DMA((n,))],
    )(*vecs)


def _pair_sum(g, other, c_idx):
    nq, r, w = g.shape
    h = r // 2
    tr = _div_tile(h)
    nt = h // tr

    def body(c_ref, g_ref, o_ref, s_ref):
        s_ref[...] = (g_ref[...].astype(F32) + o_ref[...].astype(F32)).astype(s_ref.dtype)

    return pl.pallas_call(
        body, name="pair_sum",
        grid_spec=pltpu.PrefetchScalarGridSpec(
            num_scalar_prefetch=1, grid=(nq, nt),
            in_specs=[pl.BlockSpec((None, tr, w), lambda k, i, c_ref: (k, c_ref[0] * nt + i, 0)),
                      pl.BlockSpec((None, tr, w), lambda k, i, c_ref: (k, i, 0))],
            out_specs=pl.BlockSpec((None, tr, w), lambda k, i, c_ref: (k, i, 0))),
        out_shape=jax.ShapeDtypeStruct((nq, h, w), g.dtype),
        compiler_params=_cparams(2),
    )(c_idx, g, other)


def _chip_sum(s, others, q_idx):
    _, h, w = s.shape
    tr = _div_tile(h)

    def body(q_ref, s_ref, o_ref, out_ref):
        out_ref[...] = ((s_ref[...].astype(F32) + o_ref[0].astype(F32)) + o_ref[1].astype(F32)) + o_ref[2].astype(F32)

    return pl.pallas_call(
        body, name="chip_sum",
        grid_spec=pltpu.PrefetchScalarGridSpec(
            num_scalar_prefetch=1, grid=(h // tr,),
            in_specs=[pl.BlockSpec((None, tr, w), lambda i, q_ref: (q_ref[0], i, 0)),
                      pl.BlockSpec((3, tr, w), lambda i, q_ref: (0, i, 0))],
            out_specs=pl.BlockSpec((tr, w), lambda i, q_ref: (i, 0))),
        out_shape=jax.ShapeDtypeStruct((h, w), F32),
        compiler_params=_cparams(1),
    )(q_idx, s, others)


def _small_update(q_idx, parts, ws, ms, vs, col_block):
    n = len(parts)
    has_w = [w is not None for w in ws]

    def body(q_ref, *refs):
        pos = 0
        ins = []
        for t in range(n):
            k = 4 if has_w[t] else 1
            ins.append(refs[pos:pos + k])
            pos += k
        outs = refs[pos:]
        opos = 0
        for t in range(n):
            p_ref = ins[t][0]
            g = p_ref[0]
            for s in range(1, p_ref.shape[0]):
                g = g + p_ref[s]
            if has_w[t]:
                _, w_ref, m_ref, v_ref = ins[t]
                g_ref, d_ref, nm_ref, nv_ref = outs[opos:opos + 4]
                opos += 4
                g_ref[...] = g
                d_ref[...], nm_ref[...], nv_ref[...] = _adamw_update(w_ref[...], g, m_ref[...], v_ref[...])
            else:
                outs[opos][...] = g
                opos += 1

    def whole(shape):
        nd = len(shape)
        return pl.BlockSpec(shape, lambda i, q_ref: (0,) * nd)

    in_specs, out_specs, out_shape, args = [], [], [], []
    for t in range(n):
        k, r, wf = parts[t].shape
        if col_block[t]:
            w = wf // N_CHIPS
            in_specs.append(pl.BlockSpec((k, r, w), lambda i, q_ref: (0, 0, q_ref[0])))
        else:
            w = wf
            in_specs.append(whole((k, r, wf)))
        args.append(parts[t])
        if has_w[t]:
            assert ws[t].shape == (r, w), (ws[t].shape, r, w)
            in_specs += [whole((r, w))] * 3
            args += [ws[t], ms[t], vs[t]]
            out_specs += [whole((r, w))] * 4
            out_shape += [jax.ShapeDtypeStruct((r, w), F32)] * 4
        else:
            out_specs.append(whole((r, w)))
            out_shape.append(jax.ShapeDtypeStruct((r, w), F32))
    return pl.pallas_call(
        body, name="small_update",
        grid_spec=pltpu.PrefetchScalarGridSpec(num_scalar_prefetch=1, grid=(1,), in_specs=in_specs,
                                               out_specs=out_specs),
        out_shape=out_shape, compiler_params=_cparams(1),
    )(q_idx, *args)


def _pack_win(w_in):
    o = np.cumsum((0,) + IN_SPLITS)
    gq, gk, gv, gg, ga, sq, sk, sv = [w_in[:, o[i]:o[i + 1]] for i in range(8)]
    z = jnp.zeros((w_in.shape[0], 128 - GLA_RANK), w_in.dtype)
    dup = lambda a: jnp.concatenate([a[:, :64], a[:, :64], a[:, 64:], a[:, 64:]], axis=1)
    return jnp.concatenate([gq, gk, gv, gg, ga, z, sq, dup(sk), dup(sv)], axis=1)


def _unpack_dwin(d):
    und = lambda a: jnp.concatenate([a[:, 0:64] + a[:, 64:128], a[:, 128:192] + a[:, 192:256]], axis=1)
    return jnp.concatenate([d[:, :P_GA], d[:, P_GA:P_GA + GLA_RANK], d[:, P_SQ:P_SK], und(d[:, P_SK:P_SV]),
                            und(d[:, P_SV:P_END])], axis=1)


def _local_step(x, target, meta, p):
    s = x.shape[0]
    t = s + BLK
    h0 = jnp.concatenate([jnp.zeros((PAD, D_MODEL), F32), meta, x], axis=0)
    cos, sin = _rope_tables(t)

    h1, n1, g1, u1, a1, f1 = _ffn_fwd(h0, p["ffn1_pre_norm"], p["ffn1_w_gate"], p["ffn1_w_up"], p["ffn1_w_down"],
                                      p["ffn1_post_norm"])
    n2, gq, gk, gv, gg, ga, la, sq, sk, sv = _mix_proj(h1, p["mix_pre_norm"], p["w_in"], p["gla_w_a2"], p["gla_b_a"],
                                                       cos, sin)
    ogla, ss = _gla_fwd(gq, gk, gv, la)
    oswa = _swa_fwd(p["swa_sinks"], sq, sk, sv)
    h2, cat, m = _mix_out(h1, ogla, gg, oswa, p["gla_out_norm"], p["swa_out_norm"], p["w_out"], p["mix_post_norm"])
    dy, n3, g3, u3, a3, f3, sse = _ffn_fwd(h2, p["ffn2_pre_norm"], p["ffn2_w_gate"], p["ffn2_w_up"],
                                           p["ffn2_w_down"], p["ffn2_post_norm"], target=target)

    grads = {}
    dh2, df3, dg3, du3, grads["ffn2_pre_norm"], grads["ffn2_post_norm"] = _ffn_bwd(
        dy, h2, f3, g3, u3, p["ffn2_pre_norm"], p["ffn2_w_gate"], p["ffn2_w_up"], p["ffn2_w_down"],
        p["ffn2_post_norm"])
    (gud,) = _ffn_wgrad(n3, df3, dg3, du3, a3)
    grads["ffn2_w_gate"], grads["ffn2_w_up"], grads["ffn2_w_down"] = gud[:, :FJ], gud[:, FJ:2 * FJ], gud[:, 2 * FJ:]

    dogla, dgg, doswa, dm, grads["mix_post_norm"], grads["gla_out_norm"], grads["swa_out_norm"] = _mix_out_bwd(
        dh2, m, ogla, gg, oswa, p["gla_out_norm"], p["swa_out_norm"], p["w_out"], p["mix_post_norm"])
    grads["w_out"] = _xty(cat, dm)
    dsq, dsk, dsv, dkm, dvm, dsinks = _swa_bwd(p["swa_sinks"], sq, sk, sv, oswa, doswa)
    grads["swa_sinks"] = dsinks[:, 0]
    dgq, dgk, dgv, dla = _gla_bwd(gq, gk, gv, la, ss, dogla)
    dh1, dproj, grads["mix_pre_norm"], dwa2p, grads["gla_b_a"] = _mix_in_bwd(
        dh2, h1, p["mix_pre_norm"], p["w_in"], p["gla_w_a2"], p["gla_b_a"], cos, sin, ga, dgq, dgk, dgv, dgg, dla,
        dsq, dsk, dsv, dkm, dvm)
    grads["gla_w_a2"] = dwa2p[:GLA_RANK]
    grads["w_in"] = _unpack_dwin(_xty(n2, dproj))

    dh0, df1, dg1, du1, grads["ffn1_pre_norm"], grads["ffn1_post_norm"] = _ffn_bwd(
        dh1, h0, f1, g1, u1, p["ffn1_pre_norm"], p["ffn1_w_gate"], p["ffn1_w_up"], p["ffn1_w_down"],
        p["ffn1_post_norm"])
    (gud,) = _ffn_wgrad(n1, df1, dg1, du1, a1)
    grads["ffn1_w_gate"], grads["ffn1_w_up"], grads["ffn1_w_down"] = gud[:, :FJ], gud[:, FJ:2 * FJ], gud[:, 2 * FJ:]
    grads["meta_tokens"] = dh0[PAD:BLK]
    return sse[0, 0], dh0[BLK:], grads


WEIGHTS = ['meta_tokens', 'ffn1_pre_norm', 'ffn1_w_gate', 'ffn1_w_up', 'ffn1_w_down', 'ffn1_post_norm',
           'mix_pre_norm', 'w_in', 'gla_w_a2', 'gla_b_a', 'gla_out_norm', 'swa_sinks', 'swa_out_norm', 'w_out',
           'mix_post_norm', 'ffn2_pre_norm', 'ffn2_w_gate', 'ffn2_w_up', 'ffn2_w_down', 'ffn2_post_norm']
BIG = ['ffn1_w_gate', 'ffn1_w_up', 'ffn1_w_down', 'w_in', 'w_out', 'ffn2_w_gate', 'ffn2_w_up', 'ffn2_w_down']
SMALL = [n for n in WEIGHTS if n not in BIG]
FJ = D_FF // N_CHIPS
D_IN_J = D_IN // N_CHIPS
D_OUT_J = D_MODEL // N_CHIPS
TRANSPOSED = ('ffn1_w_gate', 'ffn1_w_up', 'ffn2_w_gate', 'ffn2_w_up')


def _shard2d(name, a):
    return a[0].T if name in TRANSPOSED else a[0]


def _unshard2d(name, a):
    return (a.T if name in TRANSPOSED else a)[None]


def _small_rows(name, a):
    flat = a.reshape(-1)
    rows = -(-flat.shape[0] // 1024) * 8
    return jnp.pad(flat, (0, rows * 128 - flat.shape[0])).reshape(rows, 128)


def kernel(x, meta_tokens, ffn1_pre_norm, ffn1_w_gate, ffn1_w_up, ffn1_w_down, ffn1_post_norm, mix_pre_norm, w_in, gla_w_a2, gla_b_a, gla_out_norm, swa_sinks, swa_out_norm, w_out, mix_post_norm, ffn2_pre_norm, ffn2_w_gate, ffn2_w_up, ffn2_w_down, ffn2_post_norm, loss_target, m_meta_tokens, m_ffn1_pre_norm, m_ffn1_w_gate, m_ffn1_w_up, m_ffn1_w_down, m_ffn1_post_norm, m_mix_pre_norm, m_w_in, m_gla_w_a2, m_gla_b_a, m_gla_out_norm, m_swa_sinks, m_swa_out_norm, m_w_out, m_mix_post_norm, m_ffn2_pre_norm, m_ffn2_w_gate, m_ffn2_w_up, m_ffn2_w_down, m_ffn2_post_norm, v_meta_tokens, v_ffn1_pre_norm, v_ffn1_w_gate, v_ffn1_w_up, v_ffn1_w_down, v_ffn1_post_norm, v_mix_pre_norm, v_w_in, v_gla_w_a2, v_gla_b_a, v_gla_out_norm, v_swa_sinks, v_swa_out_norm, v_w_out, v_mix_post_norm, v_ffn2_pre_norm, v_ffn2_w_gate, v_ffn2_w_up, v_ffn2_w_down, v_ffn2_post_norm):
    args = dict(locals())
    w = {n: args[n] for n in WEIGHTS}
    mom = {n: args["m_" + n] for n in WEIGHTS}
    var = {n: args["v_" + n] for n in WEIGHTS}
    cx, cy, cc = lax.axis_index("x"), lax.axis_index("y"), lax.axis_index("c")
    q_idx = (2 * cx + cy).astype(jnp.int32).reshape(1)
    c_idx = cc.astype(jnp.int32).reshape(1)

    q_chip = 2 * cx + cy
    bf = {n: _own_slot(_shard2d(n, w[n]).astype(BF16), q_chip) for n in BIG}
    qc_idx = jnp.stack([q_chip, cc]).astype(jnp.int32)
    early = _GatherChips([_own_slot(w["meta_tokens"], q_chip),
                          _own_slot(w["gla_w_a2"].reshape(GLA_RANK, GLA_KW // N_CHIPS), q_chip)])
    meta4, wa24 = _comm_call(early, "gather_small")
    meta_full = meta4.transpose(1, 0, 2).reshape(N_META, D_MODEL)
    wa2p = jnp.pad(wa24.transpose(1, 0, 2).reshape(GLA_RANK, GLA_KW), ((0, 128 - GLA_RANK), (0, 0))).astype(BF16)
    sinks = w["swa_sinks"].reshape(SWA_QH)

    seq, target = x[0], loss_target[0]
    t = seq.shape[0] + BLK
    h0 = jnp.concatenate([jnp.zeros((PAD, D_MODEL), F32), meta_full, seq], axis=0)
    cos, sin = _rope_tables(t)
    late = _GatherChips([bf["w_in"], bf["w_out"], bf["ffn2_w_gate"], bf["ffn2_w_up"], bf["ffn2_w_down"]])
    n1 = _rms_cast(h0, w["ffn1_pre_norm"])
    (h1, g1, u1, a1, f1), (wg1, wu1, wd1), (win4, wout4, wg2, wu2, wd2) = _ffn_fwd_gather(
        h0, n1, [bf["ffn1_w_gate"], bf["ffn1_w_up"], bf["ffn1_w_down"]], w["ffn1_post_norm"], qc_idx, late)
    winp = _pack_win(win4.transpose(1, 0, 2).reshape(D_MODEL, D_IN))
    wout = wout4.reshape(D_MODEL, D_MODEL)
    n2, gq, gk, gv, gg, ga, la, sq, sk, sv = _mix_proj(h1, w["mix_pre_norm"], winp, wa2p, w["gla_b_a"], cos, sin)
    ogla, ss = _gla_fwd(gq, gk, gv, la)
    oswa = _swa_fwd(sinks, sq, sk, sv)
    h2, cat, m = _mix_out(h1, ogla, gg, oswa, w["gla_out_norm"], w["swa_out_norm"], wout, w["mix_post_norm"])
    dy, n3, g3, u3, a3, f3, sse = _ffn_fwd(h2, w["ffn2_pre_norm"], wg2, wu2, wd2, w["ffn2_post_norm"], target=target)
    loss = lax.psum(sse[0, 0] * (0.5 / D_MODEL), ("x", "y", "c"))

    g = {}
    dh2, df3, dg3, du3, g["ffn2_pre_norm"], g["ffn2_post_norm"] = _ffn_bwd(
        dy, h2, f3, g3, u3, w["ffn2_pre_norm"], wg2, wu2, wd2, w["ffn2_post_norm"])
    (gf2,) = _ffn_wgrad(n3, df3, dg3, du3, a3)
    (dogla, dgg, doswa, dm, g["mix_post_norm"], g["gla_out_norm"], g["swa_out_norm"]), (rgf2,) = _mix_out_bwd(
        dh2, m, ogla, gg, oswa, w["gla_out_norm"], w["swa_out_norm"], wout, w["mix_post_norm"],
        hook=_PairExchange([gf2]))
    sgf2 = _pair_sum(gf2, rgf2, c_idx)
    gout = _xty(cat, dm).reshape(N_CHIPS, D_OUT_J, D_MODEL).astype(BF16)
    (dsq, dsk, dsv, dkm, dvm, dsinks), (ogf2,) = _swa_bwd(sinks, sq, sk, sv, oswa, doswa,
                                                          hook=_ChipScatter([sgf2]))
    g["swa_sinks"] = dsinks
    dgq, dgk, dgv, dla = _gla_bwd(gq, gk, gv, la, ss, dogla)
    dh1, dproj, g["mix_pre_norm"], dwa2p, g["gla_b_a"] = _mix_in_bwd(
        dh2, h1, w["mix_pre_norm"], winp, wa2p, w["gla_b_a"], cos, sin, ga, dgq, dgk, dgv, dgg, dla,
        dsq, dsk, dsv, dkm, dvm)
    g["gla_w_a2"] = dwa2p[:GLA_RANK]
    gin = _unpack_dwin(_xty(n2, dproj)).reshape(D_MODEL, N_CHIPS, D_IN_J).transpose(1, 0, 2).astype(BF16)
    (dh0, df1, dg1, du1, g["ffn1_pre_norm"], g["ffn1_post_norm"]), (rgin, rgout) = _ffn_bwd(
        dh1, h0, f1, g1, u1, w["ffn1_pre_norm"], wg1, wu1, wd1, w["ffn1_post_norm"],
        hook=_PairExchange([gin, gout]))
    sgin, sgout = _pair_sum(gin, rgin, c_idx), _pair_sum(gout, rgout, c_idx)
    own1, others1, (ogin, ogout) = _ffn_wgrad_reduce(n1, df1, dg1, du1, a1, qc_idx, _ChipScatter([sgin, sgout]))
    g["meta_tokens"] = dh0[PAD:BLK]
    grad_x = dh0[BLK:]
    halves = [_chip_sum(own1[None], others1, jnp.zeros((1,), jnp.int32))]
    halves += [_chip_sum(s, o, q_idx) for s, o in ((sgin, ogin), (sgout, ogout), (sgf2, ogf2))]
    others = _comm_call(_PairShare(halves), "pair_share")
    reduced = {"ffn1_w_gate": (0, 0), "ffn1_w_up": (0, FJ), "ffn1_w_down": (0, 2 * FJ), "w_in": (1, 0),
               "w_out": (2, 0), "ffn2_w_gate": (3, 0), "ffn2_w_up": (3, FJ), "ffn2_w_down": (3, 2 * FJ)}
    grad, delta, new_m, new_v = {}, {}, {}, {}
    for n in BIG:
        k, row0 = reduced[n]
        outs = _adamw_halves(_shard2d(n, w[n]), halves[k], others[k], _shard2d(n, mom[n]), _shard2d(n, var[n]),
                             c_idx, row0)
        grad[n], delta[n], new_m[n], new_v[n] = [_unshard2d(n, a) for a in outs]

    late = ["gla_w_a2", "swa_sinks"]
    direct = [n for n in SMALL if n not in late]
    names = direct + late
    gathered = _all_gather_devices([g[n] for n in names])
    mat = lambda a: a.reshape(a.shape[-2:])
    none2 = [None] * len(late)
    outs = _small_update(q_idx, gathered, [mat(w[n]) for n in direct] + none2, [mat(mom[n]) for n in direct] + none2,
                         [mat(var[n]) for n in direct] + none2, [n == "meta_tokens" for n in names])
    sum_a2, sum_sinks = outs[4 * len(direct):]
    g_late = [lax.dynamic_slice_in_dim(sum_a2, q_chip * (GLA_KW // N_CHIPS), GLA_KW // N_CHIPS, axis=1)[None],
              sum_sinks[:, 0].reshape(1, 1, SWA_QH)]
    outs = list(outs[:4 * len(direct)]) + list(_small_update(
        q_idx, g_late, [mat(w[n]) for n in late], [mat(mom[n]) for n in late], [mat(var[n]) for n in late],
        [False, False]))
    for k, n in enumerate(names):
        grad[n], delta[n], new_m[n], new_v[n] = [a.reshape(w[n].shape) for a in outs[4 * k:4 * k + 4]]

    return (loss, grad_x[None], *[grad[n] for n in WEIGHTS], *[delta[n] for n in WEIGHTS],
            *[new_m[n] for n in WEIGHTS], *[new_v[n] for n in WEIGHTS])
```

```python
import functools
import math

import numpy as np
import jax
import jax.numpy as jnp
from jax import lax
from jax.experimental import pallas as pl
from jax.experimental.pallas import tpu as pltpu

F32 = jnp.float32
BF16 = jnp.bfloat16
MESH = pl.DeviceIdType.MESH

D_MODEL = 1024
D_FF = 2816
N_CHIPS = 4
N_DEV = 8
N_META = 16
BLK = 128
PAD = BLK - N_META
GLA_CHUNK = 64
GLA_HEADS = 4
GLA_DV = 128
GLA_DK = 64
GLA_KW = GLA_HEADS * GLA_DK
GLA_W = GLA_HEADS * GLA_DV
GLA_RANK = 16
GLA_TAU = 16.0
SWA_HD = 64
SWA_QH = 8
SWA_KVH = 2
SWA_W = SWA_QH * SWA_HD
WINDOW = 128
ROPE_THETA = 10000.0
EPS = 1e-6
NEG_INF = -1e30
IN_SPLITS = (256, 256, 512, 512, 16, 512, 128, 128)
D_IN = sum(IN_SPLITS)
P_GQ, P_GK, P_GV, P_GG, P_GA, P_SQ, P_SK, P_SV, P_END = 0, 256, 512, 1024, 1536, 1664, 2176, 2432, 2688
ADAM_LR, ADAM_B1, ADAM_B2, ADAM_EPS, ADAM_WD, ADAM_STEP = 0.001, 0.9, 0.999, 1e-08, 0.01, 10
VMEM_LIMIT = 56 * 1024 * 1024

NT = (((1,), (1,)), ((), ()))
TN = (((0,), (0,)), ((), ()))


def _cparams(n_axes):
    return pltpu.CompilerParams(dimension_semantics=("arbitrary",) * n_axes, vmem_limit_bytes=VMEM_LIMIT)


def _row_tile(t):
    for tm in (640, 512, 384, 256, 128):
        if t % tm == 0:
            return tm
    raise ValueError(t)


SEQ_BLOCKS_PER_STEP = 5


def _seq_tile(t):
    return SEQ_BLOCKS_PER_STEP * BLK if t % (SEQ_BLOCKS_PER_STEP * BLK) == 0 else BLK


ROW_PARTS = 2


def _row_parts(tm):
    n = ROW_PARTS if tm % (16 * ROW_PARTS) == 0 else 1
    return [slice(k * (tm // n), (k + 1) * (tm // n)) for k in range(n)]


def _contract_tile(t):
    return 1664 if t % 1664 == 0 else _row_tile(t)


def _div_tile(r, cap=512):
    best = None
    for tr in range(8, min(r, cap) + 1, 8):
        if r % tr == 0:
            best = tr
    return best if best is not None else r


def _dot(a, b):
    return jnp.dot(a, b, preferred_element_type=F32)


def _dg(a, b, dims):
    return lax.dot_general(a, b, dims, preferred_element_type=F32)


def _rms(x, w):
    r = lax.rsqrt(jnp.mean(x * x, axis=-1, keepdims=True) + EPS)
    xh = x * r
    return xh * w, xh, r


def _rms_bwd(xh, r, w, dy):
    wdy = dy * w
    dx = r * (wdy - xh * jnp.mean(wdy * xh, axis=-1, keepdims=True))
    dw = jnp.sum(dy * xh, axis=0, keepdims=True)
    return dx, dw


def _sigmoid(x):
    return 1.0 / (1.0 + jnp.exp(-x))


def _full(shape):
    nd = len(shape)
    return pl.BlockSpec(shape, lambda *_: (0,) * nd)


ANY = pl.BlockSpec(memory_space=pl.ANY)


def _pallas(body, *, name, grid, in_specs, out_specs, out_shape, args, scratch_shapes=(), hook=None):
    n_axes = len(grid)
    if hook is None:
        return pl.pallas_call(body, name=name, grid=grid, in_specs=list(in_specs), out_specs=list(out_specs),
                              out_shape=list(out_shape), scratch_shapes=list(scratch_shapes),
                              compiler_params=_cparams(n_axes))(*args)
    n_in, n_out, n_scr = len(in_specs), len(out_specs), len(scratch_shapes)
    h_in, h_out = len(hook.inputs), len(hook.out_shape)
    total = math.prod(grid)

    def wrapped(*refs):
        ins, hins = refs[:n_in], refs[n_in:n_in + h_in]
        o0 = n_in + h_in
        outs, houts = refs[o0:o0 + n_out], refs[o0 + n_out:o0 + n_out + h_out]
        s0 = o0 + n_out + h_out
        scr, hscr = refs[s0:s0 + n_scr], refs[s0 + n_scr:]
        step = pl.program_id(0)
        for a in range(1, n_axes):
            step = step * grid[a] + pl.program_id(a)

        @pl.when(step == 0)
        def _():
            hook.start(hins, houts, hscr)

        body(*ins, *outs, *scr)

        if hook.has_mid:
            @pl.when(step == (3 * total) // 4)
            def _():
                hook.mid(hins, houts, hscr)

        @pl.when(step == total - 1)
        def _():
            hook.finish(hins, houts, hscr)

    res = pl.pallas_call(
        wrapped, name=name, grid=grid, in_specs=list(in_specs) + [ANY] * h_in,
        out_specs=list(out_specs) + [ANY] * h_out, out_shape=list(out_shape) + list(hook.out_shape),
        scratch_shapes=list(scratch_shapes) + list(hook.scratch), compiler_params=_cparams(n_axes),
        input_output_aliases={n_in + a: n_out + b for a, b in hook.aliases},
    )(*args, *hook.inputs)
    return res[:n_out], res[n_out:]


def _ffn_fwd(h, wpre, wg4, wu4, wd4, wpost, hook=None, target=None):
    t = h.shape[0]
    tm = _row_tile(t)
    nj, fj, _ = wg4.shape
    nblk = tm // BLK if target is not None else 0

    def body(*refs):
        h_ref, wpre_ref, wg_ref, wu_ref, wd_ref, wpost_ref = refs[:6]
        t_refs = refs[6:6 + nblk]
        hout_ref, n_ref, p1_ref, p2_ref, a_ref, f_ref = refs[6 + nblk:12 + nblk]
        acc_ref = refs[-1]
        i = pl.program_id(0)
        j = pl.program_id(1)

        @pl.when(j == 0)
        def _():
            y, _, _ = _rms(h_ref[...], wpre_ref[...])
            n_ref[...] = y.astype(BF16)
            acc_ref[...] = jnp.zeros_like(acc_ref)

        if target is not None:
            sse_ref = refs[12 + nblk]

            @pl.when((i == 0) & (j == 0))
            def _():
                sse_ref[...] = jnp.zeros_like(sse_ref)

        parts = [slice(0, tm)]
        gus = [(_dg(n_ref[rows, :], wg_ref[...], NT), _dg(n_ref[rows, :], wu_ref[...], NT)) for rows in parts]
        for rows, (g, u) in zip(parts, gus):
            sg = _sigmoid(g)
            silu = g * sg
            p1_ref[rows, :] = (u * (sg + silu * (1.0 - sg))).astype(BF16)
            p2_ref[rows, :] = silu.astype(BF16)
            a = (silu * u).astype(BF16)
            a_ref[rows, :] = a
            acc_ref[rows, :] += _dot(a, wd_ref[...])

        @pl.when(j == nj - 1)
        def _():
            f = acc_ref[...]
            f_ref[...] = f
            y, _, _ = _rms(f, wpost_ref[...])
            hout = h_ref[...] + 0.5 * y
            if target is None:
                hout_ref[...] = hout
            else:
                sse = jnp.zeros((1, 1), F32)
                for k in range(nblk):
                    rows = slice(k * BLK, (k + 1) * BLK)
                    err = hout[rows] - t_refs[k][...]
                    if k == 0:
                        err = jnp.where(i > 0, err, 0.0)
                    hout_ref[rows, :] = err * (1.0 / D_MODEL)
                    sse = sse + jnp.sum(jnp.sum(err * err, axis=1, keepdims=True), axis=0, keepdims=True)
                sse_ref[...] += jnp.broadcast_to(sse, sse_ref.shape)

    row = pl.BlockSpec((tm, D_MODEL), lambda i, j: (i, 0))
    vec = pl.BlockSpec((1, D_MODEL), lambda i, j: (0, 0))
    wrow = pl.BlockSpec((None, fj, D_MODEL), lambda i, j: (j, 0, 0))
    act = pl.BlockSpec((None, tm, fj), lambda i, j: (j, i, 0))
    t_specs = [pl.BlockSpec((BLK, D_MODEL), functools.partial(lambda i, j, k: (jnp.maximum(nblk * i + k - 1, 0), 0), k=k))
               for k in range(nblk)]
    loss_spec = [_full((1, 128))] if target is not None else []
    loss_shape = [jax.ShapeDtypeStruct((1, 128), F32)] if target is not None else []
    return _pallas(
        body, name="ffn_fwd", grid=(t // tm, nj),
        in_specs=[row, vec, wrow, wrow, wrow, vec] + t_specs,
        out_specs=[row, row, act, act, act, row] + loss_spec,
        out_shape=[jax.ShapeDtypeStruct((t, D_MODEL), F32), jax.ShapeDtypeStruct((t, D_MODEL), BF16),
                   jax.ShapeDtypeStruct((nj, t, fj), BF16), jax.ShapeDtypeStruct((nj, t, fj), BF16),
                   jax.ShapeDtypeStruct((nj, t, fj), BF16), jax.ShapeDtypeStruct((t, D_MODEL), F32)] + loss_shape,
        scratch_shapes=[pltpu.VMEM((tm, D_MODEL), F32)],
        args=(h, wpre, wg4, wu4, wd4, wpost) + (target,) * nblk, hook=hook)


def _ffn_bwd(dhout, h, f, p14, p24, wpre, wg4, wu4, wd4, wpost, hook=None):
    t = h.shape[0]
    tm = _row_tile(t)
    nj, fj, _ = wg4.shape

    def body(dhout_ref, h_ref, f_ref, p1_ref, p2_ref, wpre_ref, wg_ref, wu_ref, wd_ref, wpost_ref,
             dh_ref, df_ref, dg_ref, du_ref, dwpre_ref, dwpost_ref, dn_ref):
        i = pl.program_id(0)
        j = pl.program_id(1)

        @pl.when((i == 0) & (j == 0))
        def _():
            dwpre_ref[...] = jnp.zeros_like(dwpre_ref)
            dwpost_ref[...] = jnp.zeros_like(dwpost_ref)

        @pl.when(j == 0)
        def _():
            wpost = wpost_ref[...]
            _, fh, r = _rms(f_ref[...], wpost)
            df, dw = _rms_bwd(fh, r, wpost, 0.5 * dhout_ref[...])
            dwpost_ref[...] += dw
            df_ref[...] = df.astype(BF16)
            dn_ref[...] = jnp.zeros_like(dn_ref)

        parts = _row_parts(tm)
        das = [_dg(df_ref[rows, :], wd_ref[...], NT) for rows in parts]
        for rows, da in zip(parts, das):
            dg = (da * p1_ref[rows, :].astype(F32)).astype(BF16)
            du = (da * p2_ref[rows, :].astype(F32)).astype(BF16)
            dg_ref[rows, :] = dg
            du_ref[rows, :] = du
            dn_ref[rows, :] += _dot(dg, wg_ref[...]) + _dot(du, wu_ref[...])

        @pl.when(j == nj - 1)
        def _():
            wpre = wpre_ref[...]
            _, hh, r = _rms(h_ref[...], wpre)
            dx, dw = _rms_bwd(hh, r, wpre, dn_ref[...])
            dwpre_ref[...] += dw
            dh_ref[...] = dhout_ref[...] + dx

    row = pl.BlockSpec((tm, D_MODEL), lambda i, j: (i, 0))
    vec = pl.BlockSpec((1, D_MODEL), lambda i, j: (0, 0))
    wrow = pl.BlockSpec((None, fj, D_MODEL), lambda i, j: (j, 0, 0))
    act = pl.BlockSpec((None, tm, fj), lambda i, j: (j, i, 0))
    actshape = jax.ShapeDtypeStruct((nj, t, fj), BF16)
    return _pallas(
        body, name="ffn_bwd", grid=(t // tm, nj),
        in_specs=[row, row, row, act, act, vec, wrow, wrow, wrow, vec],
        out_specs=[row, row, act, act, vec, vec],
        out_shape=[jax.ShapeDtypeStruct((t, D_MODEL), F32), jax.ShapeDtypeStruct((t, D_MODEL), BF16),
                   actshape, actshape,
                   jax.ShapeDtypeStruct((1, D_MODEL), F32), jax.ShapeDtypeStruct((1, D_MODEL), F32)],
        scratch_shapes=[pltpu.VMEM((tm, D_MODEL), F32)],
        args=(dhout, h, f, p14, p24, wpre, wg4, wu4, wd4, wpost), hook=hook)


def _ffn_wgrad(n, df, dg4, du4, a4, hook=None):
    t = n.shape[0]
    tm = _contract_tile(t)
    ni = t // tm
    nj, _, fj = dg4.shape

    def body(n_ref, df_ref, dg_ref, du_ref, a_ref, dw_ref, acc):
        i = pl.program_id(1)

        @pl.when(i == 0)
        def _():
            acc[...] = jnp.zeros_like(acc)

        nn = n_ref[...]
        acc[0:fj, :] += _dg(dg_ref[...], nn, TN)
        acc[fj:2 * fj, :] += _dg(du_ref[...], nn, TN)
        acc[2 * fj:3 * fj, :] += _dg(a_ref[...], df_ref[...], TN)

        @pl.when(i == ni - 1)
        def _():
            dw_ref[...] = acc[...].astype(BF16)

    row = pl.BlockSpec((tm, D_MODEL), lambda j, i: (i, 0))
    act = pl.BlockSpec((None, tm, fj), lambda j, i: (j, i, 0))
    return _pallas(
        body, name="ffn_wgrad", grid=(nj, ni),
        in_specs=[row, row, act, act, act],
        out_specs=[pl.BlockSpec((None, 3 * fj, D_MODEL), lambda j, i: (j, 0, 0))],
        out_shape=[jax.ShapeDtypeStruct((nj, 3 * fj, D_MODEL), BF16)],
        scratch_shapes=[pltpu.VMEM((3 * fj, D_MODEL), F32)],
        args=(n, df, dg4, du4, a4), hook=hook)


def _rms_cast(h, w):
    t = h.shape[0]
    tm = _row_tile(t)

    def body(h_ref, w_ref, n_ref):
        y, _, _ = _rms(h_ref[...], w_ref[...])
        n_ref[...] = y.astype(BF16)

    row = pl.BlockSpec((tm, D_MODEL), lambda i: (i, 0))
    return pl.pallas_call(
        body, name="rms_cast", grid=(t // tm,), in_specs=[row, _full((1, D_MODEL))], out_specs=row,
        out_shape=jax.ShapeDtypeStruct((t, D_MODEL), BF16), compiler_params=_cparams(1),
    )(h, w)


FWD_RELATION = (None, 0, 1, 2)


def _ffn_fwd_gather(h, n, wbufs, wpost, qc_idx, late):
    t = h.shape[0]
    tm = _row_tile(t)
    ni = t // tm
    nj, fj, _ = wbufs[0].shape
    assert nj == N_CHIPS and ni >= 4
    nw = len(wbufs)
    n_lin, n_lout = len(late.inputs), len(late.out_shape)
    wait_step = ni - 3

    def body(qc_ref, h_ref, n_ref, wpost_ref, *rest):
        wb_in = rest[:nw]
        lins = rest[nw:nw + n_lin]
        o0 = nw + n_lin
        hout_ref, p1_ref, p2_ref, a_ref, f_hbm = rest[o0:o0 + 5]
        wb = rest[o0 + 5:o0 + 5 + nw]
        louts = rest[o0 + 5 + nw:o0 + 5 + nw + n_lout]
        s0 = o0 + 5 + nw + n_lout
        wv, wsem, send, recv, fbuf, fr_sem, fw_sem = rest[s0:s0 + 7]
        lscr = rest[s0 + 7:]
        p = pl.program_id(0)
        i = pl.program_id(1)
        step = p * ni + i
        fslot = step % 2

        def f_tile(tile):
            return f_hbm.at[pl.ds(pl.multiple_of(tile * tm, 8), tm)]

        @pl.when(step > 0)
        def _():
            pltpu.make_async_copy(fbuf.at[1 - fslot], f_tile(i), fw_sem.at[1 - fslot]).wait()

        nxt = step + 1

        @pl.when((nxt < N_CHIPS * ni) & (nxt >= ni))
        def _():
            pltpu.make_async_copy(f_tile(nxt % ni), fbuf.at[1 - fslot], fr_sem.at[1 - fslot]).start()

        @pl.when(p > 0)
        def _():
            pltpu.make_async_copy(f_tile(i), fbuf.at[fslot], fr_sem.at[fslot]).wait()
        x, y, c, chips = _place()
        q = 2 * x + y
        sibling = (x, y, 1 - c)
        mine, other = _half(fj, c), _half(fj, 1 - c)

        def load(chunk, slot, src):
            return [pltpu.make_async_copy(src[t].at[chunk], wv.at[slot, t], wsem.at[slot, t]) for t in range(nw)]

        @pl.when((p == 0) & (i == 0))
        def _():
            for j, (cx, cy) in enumerate(chips):
                for t in range(nw):
                    _remote(send.at[t, j], recv.at[t, j], wb_in[t].at[q, mine], wb[t].at[q, mine], (cx, cy, c)).start()
            for cp in load(q, 0, wb_in):
                cp.start()
            for cp in load(q, 0, wb_in):
                cp.wait()

        @pl.when((p == 1) & (i == 0))
        def _():
            late.start(lins, louts, lscr)

        for pp in range(1, N_CHIPS):
            j = FWD_RELATION[pp]
            cx, cy = chips[j]
            chunk = 2 * cx + cy

            @pl.when((p == pp - 1) & (i == wait_step))
            def _(j=j, cx=cx, cy=cy, chunk=chunk, pp=pp):
                for t in range(nw):
                    got = wb[t].at[chunk, mine]
                    _remote(send.at[t, j], recv.at[t, j], got, got, (cx, cy, c)).wait_recv()
                    _remote(send.at[t, 3 + j], recv.at[t, 3 + j], got, got, sibling).start()
                for t in range(nw):
                    rest_half = wb[t].at[chunk, other]
                    _remote(send.at[t, 3 + j], recv.at[t, 3 + j], rest_half, rest_half, sibling).wait_recv()
                for cp in load(chunk, pp % 2, wb):
                    cp.start()

            @pl.when((p == pp) & (i == 0))
            def _(chunk=chunk, pp=pp):
                for cp in load(chunk, pp % 2, wb):
                    cp.wait()

        @pl.when((p == N_CHIPS - 1) & (i == ni // 2))
        def _():
            late.mid(lins, louts, lscr)

        slot = p % 2
        nn = n_ref[...]
        g = _dg(nn, wv[slot, 0], NT)
        u = _dg(nn, wv[slot, 1], NT)
        sg = _sigmoid(g)
        silu = g * sg
        p1_ref[...] = (u * (sg + silu * (1.0 - sg))).astype(BF16)
        p2_ref[...] = silu.astype(BF16)
        a = (silu * u).astype(BF16)
        a_ref[...] = a
        part = _dot(a, wv[slot, 2])

        @pl.when(p == 0)
        def _():
            fbuf[fslot] = part

        @pl.when(p > 0)
        def _():
            fbuf[fslot] = fbuf[fslot] + part

        pltpu.make_async_copy(fbuf.at[fslot], f_tile(i), fw_sem.at[fslot]).start()

        @pl.when(p == N_CHIPS - 1)
        def _():
            yv, _, _ = _rms(fbuf[fslot], wpost_ref[...])
            hout_ref[...] = h_ref[...] + 0.5 * yv

        @pl.when((p == N_CHIPS - 1) & (i == ni - 1))
        def _():
            pltpu.make_async_copy(fbuf.at[fslot], f_tile(i), fw_sem.at[fslot]).wait()
            for t in range(nw):
                for j, (cx, cy) in enumerate(chips):
                    sent = wb[t].at[2 * cx + cy, mine]
                    _remote(send.at[t, j], recv.at[t, j], sent, sent, (cx, cy, c)).wait_send()
                    _remote(send.at[t, 3 + j], recv.at[t, 3 + j], sent, sent, sibling).wait_send()
            late.finish(lins, louts, lscr)

    def last_pass_rows(p, i, qc_ref):
        return (jnp.where(p == N_CHIPS - 1, i, 0), 0)

    def chunk_rows(p, i, qc_ref):
        order = ((p & 1) << 1) | (p >> 1)
        return (jnp.bitwise_xor(qc_ref[0], order), i, 0)

    row = pl.BlockSpec((tm, D_MODEL), lambda p, i, qc_ref: (i, 0))
    last_row = pl.BlockSpec((tm, D_MODEL), last_pass_rows)
    act = pl.BlockSpec((None, tm, fj), chunk_rows)
    act_shape = jax.ShapeDtypeStruct((nj, t, fj), BF16)
    res = pl.pallas_call(
        body, name="ffn_fwd_gather",
        grid_spec=pltpu.PrefetchScalarGridSpec(
            num_scalar_prefetch=1, grid=(N_CHIPS, ni),
            in_specs=[last_row, row, pl.BlockSpec((1, D_MODEL), lambda p, i, qc_ref: (0, 0))]
            + [ANY] * (nw + n_lin),
            out_specs=[last_row, act, act, act, ANY] + [ANY] * (nw + n_lout),
            scratch_shapes=[pltpu.VMEM((2, nw, fj, D_MODEL), BF16), pltpu.SemaphoreType.DMA((2, nw)),
                            pltpu.SemaphoreType.DMA((nw, 6)), pltpu.SemaphoreType.DMA((nw, 6)),
                            pltpu.VMEM((2, tm, D_MODEL), F32), pltpu.SemaphoreType.DMA((2,)),
                            pltpu.SemaphoreType.DMA((2,))] + list(late.scratch)),
        out_shape=[jax.ShapeDtypeStruct((t, D_MODEL), F32), act_shape, act_shape, act_shape,
                   jax.ShapeDtypeStruct((t, D_MODEL), F32)]
        + [jax.ShapeDtypeStruct(b.shape, b.dtype) for b in wbufs] + list(late.out_shape),
        input_output_aliases={**{4 + t: 5 + t for t in range(nw)},
                              **{4 + nw + a: 5 + nw + b for a, b in late.aliases}},
        compiler_params=_cparams(2),
    )(qc_idx, h, n, wpost, *wbufs, *late.inputs)
    return res[:5], res[5:5 + nw], res[5 + nw:]


PASS_RELATION = (2, 0, 1)


def _ffn_wgrad_reduce(n, df, dg4, du4, a4, qc_idx, hook):
    t = n.shape[0]
    tm = _contract_tile(t)
    ni = t // tm
    nj, _, fj = dg4.shape
    assert nj == N_CHIPS
    hrows = 3 * fj // 2
    n_hin, n_hout = len(hook.inputs), len(hook.out_shape)

    def body(qc_ref, n_ref, df_ref, dg_ref, du_ref, a_ref, *rest):
        hins = rest[:n_hin]
        own_ref, others_ref = rest[n_hin:n_hin + 2]
        houts = rest[n_hin + 2:n_hin + 2 + n_hout]
        s0 = n_hin + 2 + n_hout
        acc, stage, land, sumbuf, px_send, px_recv, cs_send, cs_recv, own_sem = rest[s0:s0 + 9]
        hscr = rest[s0 + 9:]
        k_pass = pl.program_id(0)
        i = pl.program_id(1)
        x, y, c, chips = _place()
        mine = pl.ds(pl.multiple_of(c * hrows, 8), hrows)
        other = pl.ds(pl.multiple_of((1 - c) * hrows, 8), hrows)

        def to_owner(k):
            j = PASS_RELATION[k]
            return _remote(cs_send.at[j], cs_recv.at[j], sumbuf.at[k % 2], others_ref.at[j], (*chips[j], c))

        @pl.when((k_pass == 0) & (i == 0))
        def _():
            hook.start(hins, houts, hscr)

        @pl.when(i == 0)
        def _():
            acc[...] = jnp.zeros_like(acc)

        nn = n_ref[...]
        acc[0:fj, :] += _dg(dg_ref[...], nn, TN)
        acc[fj:2 * fj, :] += _dg(du_ref[...], nn, TN)
        acc[2 * fj:3 * fj, :] += _dg(a_ref[...], df_ref[...], TN)

        for k in range(N_CHIPS):
            @pl.when((k_pass == k) & (i == ni - 1))
            def _(k=k):
                slot = k % 2
                stage[...] = acc[other, :].astype(BF16)
                swap = _remote(px_send.at[k], px_recv.at[k], stage, land.at[slot], (x, y, 1 - c))
                swap.start()
                swap.wait_recv()
                pair = acc[mine, :] + land[slot].astype(F32)
                if k >= 2:
                    to_owner(k - 2).wait_send()
                sumbuf[slot] = pair.astype(BF16)
                swap.wait_send()
                if k < N_CHIPS - 1:
                    to_owner(k).start()
                else:
                    keep = pltpu.make_async_copy(sumbuf.at[slot], own_ref, own_sem)
                    keep.start()
                    for j in range(N_CHIPS - 1):
                        _remote(cs_send.at[j], cs_recv.at[j], sumbuf.at[0], others_ref.at[j], (*chips[j], c)).wait_recv()
                    to_owner(k - 1).wait_send()
                    keep.wait()
                    hook.finish(hins, houts, hscr)

    def chunk(k_pass, i, qc_ref):
        return (jnp.bitwise_xor(qc_ref[0], N_CHIPS - 1 - k_pass), i, 0)

    row = pl.BlockSpec((tm, D_MODEL), lambda k_pass, i, qc_ref: (i, 0))
    act = pl.BlockSpec((None, tm, fj), chunk)
    res = pl.pallas_call(
        body, name="ffn_wgrad_reduce",
        grid_spec=pltpu.PrefetchScalarGridSpec(
            num_scalar_prefetch=1, grid=(N_CHIPS, ni),
            in_specs=[row, row, act, act, act] + [ANY] * n_hin,
            out_specs=[ANY, ANY] + [ANY] * n_hout,
            scratch_shapes=[pltpu.VMEM((3 * fj, D_MODEL), F32), pltpu.VMEM((hrows, D_MODEL), BF16),
                            pltpu.VMEM((2, hrows, D_MODEL), BF16), pltpu.VMEM((2, hrows, D_MODEL), BF16),
                            pltpu.SemaphoreType.DMA((N_CHIPS,)), pltpu.SemaphoreType.DMA((N_CHIPS,)),
                            pltpu.SemaphoreType.DMA((N_CHIPS - 1,)), pltpu.SemaphoreType.DMA((N_CHIPS - 1,)),
                            pltpu.SemaphoreType.DMA] + list(hook.scratch)),
        out_shape=[jax.ShapeDtypeStruct((hrows, D_MODEL), BF16),
                   jax.ShapeDtypeStruct((N_CHIPS - 1, hrows, D_MODEL), BF16)] + list(hook.out_shape),
        compiler_params=_cparams(2),
    )(qc_idx, n, df, dg4, du4, a4, *hook.inputs)
    return res[0], res[1], res[2:]


def _xty(x, y):
    t, k = x.shape
    n = y.shape[1]
    tm = _contract_tile(t)
    tn = n if n <= 1024 else (896 if n % 896 == 0 else 128)

    def body(x_ref, y_ref, o_ref):
        @pl.when(pl.program_id(1) == 0)
        def _():
            o_ref[...] = jnp.zeros_like(o_ref)

        o_ref[...] += _dg(x_ref[...], y_ref[...], TN)

    return pl.pallas_call(
        body, name="xty", grid=(n // tn, t // tm),
        in_specs=[pl.BlockSpec((tm, k), lambda j, i: (i, 0)), pl.BlockSpec((tm, tn), lambda j, i: (i, j))],
        out_specs=pl.BlockSpec((k, tn), lambda j, i: (0, j)),
        out_shape=jax.ShapeDtypeStruct((k, n), F32),
        compiler_params=_cparams(2),
    )(x, y)


def _rope_tables(t):
    pos = (jnp.arange(t, dtype=jnp.int32) - PAD).astype(F32)
    inv_freq = 1.0 / (ROPE_THETA ** (jnp.arange(0, SWA_HD, 2, dtype=F32) / SWA_HD))
    ang = pos[:, None] * inv_freq[None, :]
    cos = jnp.cos(ang)
    sin = jnp.sin(ang)
    return jnp.concatenate([cos, cos, cos, cos], axis=1), jnp.concatenate([-sin, sin, -sin, sin], axis=1)


def _rot_half(x, first_half):
    return jnp.where(first_half, pltpu.roll(x, 96, 1), pltpu.roll(x, 32, 1))


def _first_half_mask(rows):
    lane = lax.broadcasted_iota(jnp.int32, (rows, 128), 1)
    return (lane % 64) < 32


def _log_sigmoid(z):
    return jnp.minimum(z, 0.0) - jnp.log(1.0 + jnp.exp(-jnp.abs(z)))


def _mix_proj(h1, wmixpre, winp, wa2p, bap, cos, sin):
    t = h1.shape[0]
    tm = _row_tile(t)

    def body(h_ref, w_ref, win_ref, wa2_ref, ba_ref, cos_ref, sin_ref,
             n_ref, gq_ref, gk_ref, gv_ref, gg_ref, ga_ref, la_ref, sq_ref, sk_ref, sv_ref):
        y, _, _ = _rms(h_ref[...], w_ref[...])
        n = y.astype(BF16)
        n_ref[...] = n
        proj = _dot(n, win_ref[...])
        gq_ref[...] = proj[:, P_GQ:P_GK]
        gk_ref[...] = proj[:, P_GK:P_GV]
        gv_ref[...] = proj[:, P_GV:P_GG]
        gg_ref[...] = proj[:, P_GG:P_GA]
        ga = proj[:, P_GA:P_SQ]
        ga_ref[...] = ga
        z = _dot(ga.astype(BF16), wa2_ref[...]) + ba_ref[...]
        la_ref[...] = _log_sigmoid(z) * (1.0 / GLA_TAU)
        c = cos_ref[...]
        s = sin_ref[...]
        fh = _first_half_mask(tm)
        for k in range(4):
            x = proj[:, P_SQ + 128 * k:P_SQ + 128 * (k + 1)]
            sq_ref[:, 128 * k:128 * (k + 1)] = (x * c + _rot_half(x, fh) * s).astype(BF16)
        for k in range(2):
            x = proj[:, P_SK + 128 * k:P_SK + 128 * (k + 1)]
            sk_ref[:, 128 * k:128 * (k + 1)] = (x * c + _rot_half(x, fh) * s).astype(BF16)
        sv_ref[...] = proj[:, P_SV:P_END].astype(BF16)

    def row(w):
        return pl.BlockSpec((tm, w), lambda i: (i, 0))

    def rshape(w, dt):
        return jax.ShapeDtypeStruct((t, w), dt)

    return pl.pallas_call(
        body, name="mix_proj", grid=(t // tm,),
        in_specs=[row(D_MODEL), _full((1, D_MODEL)), _full((D_MODEL, P_END)), _full((128, GLA_KW)),
                  _full((1, GLA_KW)), row(128), row(128)],
        out_specs=[row(D_MODEL), row(256), row(256), row(512), row(512), row(128), row(256), row(512), row(256),
                   row(256)],
        out_shape=[rshape(D_MODEL, BF16), rshape(256, F32), rshape(256, F32), rshape(512, F32), rshape(512, F32),
                   rshape(128, F32), rshape(256, F32), rshape(512, BF16), rshape(256, BF16), rshape(256, BF16)],
        compiler_params=_cparams(1),
    )(h1, wmixpre, winp, wa2p, bap, cos, sin)


def _scan_rows(x, reverse=False):
    n = x.shape[0]
    row = lax.broadcasted_iota(jnp.int32, x.shape, 0)
    s = 1
    while s < n:
        if reverse:
            x = x + jnp.where(row < n - s, pltpu.roll(x, n - s, 0), 0.0)
        else:
            x = x + jnp.where(row >= s, pltpu.roll(x, s, 0), 0.0)
        s *= 2
    return x


def _gla_cumsum(la, tril_f):
    b = _scan_rows(la)
    row = lax.broadcasted_iota(jnp.int32, b.shape, 0)
    bm = jnp.sum(jnp.where(row == GLA_CHUNK // 2 - 1, b, 0.0), axis=0, keepdims=True)
    bl = jnp.sum(jnp.where(row == GLA_CHUNK - 1, b, 0.0), axis=0, keepdims=True)
    return b, bm, bl


def _gla_decays(la, tril_f):
    b, bm, bl = _gla_cumsum(la, tril_f)
    return jnp.exp(b - bm), jnp.exp(bm - b), jnp.exp(b), jnp.exp(bl - b), jnp.exp(bl)


def _gla_masks():
    c = GLA_CHUNK
    r = lax.broadcasted_iota(jnp.int32, (c, c), 0)
    col = lax.broadcasted_iota(jnp.int32, (c, c), 1)
    r4 = lax.broadcasted_iota(jnp.int32, (GLA_HEADS * c, c), 0) % c
    c4 = lax.broadcasted_iota(jnp.int32, (GLA_HEADS * c, c), 1)
    klane = lax.broadcasted_iota(jnp.int32, (c, GLA_KW), 1) // GLA_DK
    vlane = lax.broadcasted_iota(jnp.int32, (c, GLA_W), 1) // GLA_DV
    srow = lax.broadcasted_iota(jnp.int32, (GLA_W, GLA_KW), 0) // GLA_DV
    scol = lax.broadcasted_iota(jnp.int32, (GLA_W, GLA_KW), 1) // GLA_DK
    return dict(tril_f=(r >= col).astype(F32), triu_f=(r <= col).astype(F32), tril4=r4 >= c4,
                khead=[klane == h for h in range(GLA_HEADS)], vhead=[vlane == h for h in range(GLA_HEADS)],
                diag=srow == scol)


def _stack_heads(x, head_masks):
    return jnp.concatenate([jnp.where(m, x, 0.0) for m in head_masks], axis=0)


def _gla_fwd(gq, gk, gv, la):
    t = gq.shape[0]
    rg = _seq_tile(t)
    nb = t // rg
    ncb = rg // GLA_CHUNK
    c = GLA_CHUNK

    def body(q_ref, k_ref, v_ref, la_ref, o_ref, ss_ref, st_ref):
        @pl.when(pl.program_id(0) == 0)
        def _():
            st_ref[...] = jnp.zeros_like(st_ref)

        mk = _gla_masks()
        st = st_ref[...]
        for ch in range(ncb):
            rows = slice(ch * c, (ch + 1) * c)
            eq, ek, eb, ekl, ebl = _gla_decays(la_ref[rows, :], mk["tril_f"])
            qs = q_ref[rows, :] * (GLA_DK ** -0.5)
            k = k_ref[rows, :]
            v = v_ref[rows, :].astype(BF16)
            ss_ref[ch] = st
            q4 = _stack_heads(qs * eq, mk["khead"]).astype(BF16)
            a4 = jnp.where(mk["tril4"], _dg(q4, (k * ek).astype(BF16), NT), 0.0).astype(BF16)
            r4 = _dot(a4, v)
            intra = jnp.concatenate([r4[h * c:(h + 1) * c, GLA_DV * h:GLA_DV * (h + 1)] for h in range(GLA_HEADS)],
                                    axis=1)
            o_ref[rows, :] = intra + _dg((qs * eb).astype(BF16), st.astype(BF16), NT)
            st = st * ebl + jnp.where(mk["diag"], _dg(v, (k * ekl).astype(BF16), TN), 0.0)
        st_ref[...] = st

    def row(w):
        return pl.BlockSpec((rg, w), lambda i: (i, 0))

    return pl.pallas_call(
        body, name="gla_fwd", grid=(nb,),
        in_specs=[row(256), row(256), row(512), row(256)],
        out_specs=[row(512), pl.BlockSpec((ncb, GLA_W, GLA_KW), lambda i: (i, 0, 0))],
        out_shape=[jax.ShapeDtypeStruct((t, GLA_W), F32), jax.ShapeDtypeStruct((nb * ncb, GLA_W, GLA_KW), F32)],
        scratch_shapes=[pltpu.VMEM((GLA_W, GLA_KW), F32)],
        compiler_params=_cparams(1),
    )(gq, gk, gv, la)


def _gla_bwd(gq, gk, gv, la, ss, do):
    t = gq.shape[0]
    rg = _seq_tile(t)
    nb = t // rg
    ncb = rg // GLA_CHUNK
    c = GLA_CHUNK

    def body(q_ref, k_ref, v_ref, la_ref, ss_ref, do_ref, dq_ref, dk_ref, dv_ref, dla_ref, dst_ref):
        @pl.when(pl.program_id(0) == 0)
        def _():
            dst_ref[...] = jnp.zeros_like(dst_ref)

        mk = _gla_masks()
        last_row = lax.broadcasted_iota(jnp.int32, (c, GLA_KW), 0) == c - 1
        scale = GLA_DK ** -0.5
        dstn = dst_ref[...]
        for ch in reversed(range(ncb)):
            rows = slice(ch * c, (ch + 1) * c)
            eq, ek, eb, ekl, ebl = _gla_decays(la_ref[rows, :], mk["tril_f"])
            qs = q_ref[rows, :] * scale
            k = k_ref[rows, :]
            qt, kt, qh, kh = qs * eq, k * ek, qs * eb, k * ekl
            ktb, khb, qhb = kt.astype(BF16), kh.astype(BF16), qh.astype(BF16)
            v = v_ref[rows, :].astype(BF16)
            do_f = do_ref[rows, :]
            dob = do_f.astype(BF16)
            st = ss_ref[ch]
            stb = st.astype(BF16)
            dstb = dstn.astype(BF16)
            q4 = _stack_heads(qt, mk["khead"]).astype(BF16)
            do4 = _stack_heads(do_f, mk["vhead"]).astype(BF16)
            a4 = jnp.where(mk["tril4"], _dg(q4, ktb, NT), 0.0).astype(BF16)
            da4 = jnp.where(mk["tril4"], _dg(do4, v, NT), 0.0).astype(BF16)
            dv_ref[rows, :] = _dg(a4, do4, TN) + _dg(khb, dstb, NT)
            dq4 = _dot(da4, ktb)
            dqt = jnp.zeros((c, GLA_KW), F32)
            for h in range(GLA_HEADS):
                dqt = dqt + jnp.where(mk["khead"][h], dq4[h * c:(h + 1) * c], 0.0)
            dkt = _dg(da4, q4, TN)
            dqh = _dot(dob, stb)
            dkh = _dot(v, dstb)
            dbl = jnp.sum(dstn * st, axis=0, keepdims=True)
            dstn = dstn * ebl + jnp.where(mk["diag"], _dg(dob, qhb, TN), 0.0)
            dq_ref[rows, :] = scale * (dqt * eq + dqh * eb)
            dk_ref[rows, :] = dkt * ek + dkh * ekl
            dkk = dkh * kh
            db = dqt * qt - dkt * kt + dqh * qh - dkk
            db = db + jnp.where(last_row, jnp.sum(dkk, axis=0, keepdims=True) + ebl * dbl, 0.0)
            dla_ref[rows, :] = _scan_rows(db, reverse=True)
        dst_ref[...] = dstn

    def row(w):
        return pl.BlockSpec((rg, w), lambda i: (nb - 1 - i, 0))

    def rshape(w):
        return jax.ShapeDtypeStruct((t, w), F32)

    return pl.pallas_call(
        body, name="gla_bwd", grid=(nb,),
        in_specs=[row(256), row(256), row(512), row(256),
                  pl.BlockSpec((ncb, GLA_W, GLA_KW), lambda i: (nb - 1 - i, 0, 0)), row(512)],
        out_specs=[row(256), row(256), row(512), row(256)],
        out_shape=[rshape(256), rshape(256), rshape(512), rshape(256)],
        scratch_shapes=[pltpu.VMEM((GLA_W, GLA_KW), F32)],
        compiler_params=_cparams(1),
    )(gq, gk, gv, la, ss, do)


SWA_G = SWA_QH // SWA_KVH


def _swa_bias():
    n = jnp.arange(3, dtype=jnp.int32)[:, None, None]
    r = (jnp.arange(SWA_G * BLK, dtype=jnp.int32) % BLK)[None, :, None]
    c = jnp.arange(3 * BLK, dtype=jnp.int32)[None, None, :]
    seg = c // BLK
    cc = c % BLK
    qpos = n * BLK + r - PAD
    kpos = jnp.where(seg == 0, (n - 1) * BLK, jnp.where(seg == 1, n * BLK, 0)) + cc - PAD
    band = (seg < 2) & (kpos >= N_META) & (kpos <= qpos) & (qpos - kpos < WINDOW)
    meta = (seg == 2) & (kpos >= 0) & (kpos < N_META) & (kpos <= qpos)
    return jnp.where(band | meta, 0.0, NEG_INF).astype(F32)


def _swa_stack(ref, rows, kh, lo, dtype):
    parts = []
    for g in range(2):
        pair = ref[rows, 128 * (2 * kh + g):128 * (2 * kh + g + 1)]
        zero = jnp.zeros_like(pair)
        parts += [jnp.where(lo, pair, zero), jnp.where(lo, zero, pair)]
    return jnp.concatenate(parts, axis=0).astype(dtype)


def _swa_unstack(x4, lo):
    return [jnp.where(lo, x4[2 * g * BLK:(2 * g + 1) * BLK], x4[(2 * g + 1) * BLK:(2 * g + 2) * BLK])
            for g in range(2)]


def _swa_sink_col(sink_ref, kh):
    blk = lax.broadcasted_iota(jnp.int32, (SWA_G * BLK, 1), 0) // BLK
    col = jnp.full((SWA_G * BLK, 1), sink_ref[SWA_G * kh + SWA_G - 1], F32)
    for e in reversed(range(SWA_G - 1)):
        col = jnp.where(blk == e, sink_ref[SWA_G * kh + e], col)
    return col


def _swa_probs(q4, kall, bias, sink):
    s = _dg(q4, kall, NT) * (SWA_HD ** -0.5) + bias
    m = jnp.maximum(jnp.max(s, axis=-1, keepdims=True), sink)
    p = jnp.exp(s - m)
    es = jnp.exp(sink - m)
    inv = 1.0 / (jnp.sum(p, axis=-1, keepdims=True) + es)
    return p * inv, es * inv


def _swa_keys(prev_ref, cur_ref, first_ref, b, ls):
    before = prev_ref[:, ls] if b == 0 else cur_ref[(b - 1) * BLK:b * BLK, ls]
    return jnp.concatenate([before, cur_ref[b * BLK:(b + 1) * BLK, ls], first_ref[:, ls]], axis=0)


def _swa_specs(rs, ns):
    bps = rs // BLK
    cur = lambda w: pl.BlockSpec((rs, w), lambda i: (jnp.minimum(i, ns - 1), 0))
    prev = lambda w: pl.BlockSpec((BLK, w), lambda i: (jnp.maximum(jnp.minimum(i, ns - 1) * bps - 1, 0), 0))
    first = lambda w: pl.BlockSpec((BLK, w), lambda i: (0, 0))
    return cur, prev, first


def _swa_fwd(sinks, sq, sk, sv):
    t = sq.shape[0]
    rs = _seq_tile(t)
    bps, ns = rs // BLK, t // rs

    def body(sink_ref, bias_ref, q_ref, kp_ref, kc_ref, km_ref, vp_ref, vc_ref, vm_ref, o_ref):
        i = pl.program_id(0)
        lo = lax.broadcasted_iota(jnp.int32, (BLK, 128), 1) < 64
        sink_cols = [_swa_sink_col(sink_ref, kh) for kh in range(SWA_KVH)]
        for b in range(bps):
            rows = slice(b * BLK, (b + 1) * BLK)
            bias = bias_ref[jnp.minimum(i * bps + b, 2)]
            for kh in range(SWA_KVH):
                ls = slice(128 * kh, 128 * (kh + 1))
                kall = _swa_keys(kp_ref, kc_ref, km_ref, b, ls)
                vall = _swa_keys(vp_ref, vc_ref, vm_ref, b, ls)
                p, _ = _swa_probs(_swa_stack(q_ref, rows, kh, lo, BF16), kall, bias, sink_cols[kh])
                for g, pair in enumerate(_swa_unstack(_dot(p.astype(BF16), vall), lo)):
                    o_ref[rows, 128 * (2 * kh + g):128 * (2 * kh + g + 1)] = pair

    cur, prev, first = _swa_specs(rs, ns)
    bias = _swa_bias()
    return pl.pallas_call(
        body, name="swa_fwd", grid=(ns,),
        in_specs=[pl.BlockSpec(memory_space=pltpu.SMEM), _full(bias.shape), cur(512), prev(256), cur(256), first(256),
                  prev(256), cur(256), first(256)],
        out_specs=cur(512),
        out_shape=jax.ShapeDtypeStruct((t, SWA_W), F32),
        compiler_params=_cparams(1),
    )(sinks, bias, sq, sk, sk, sk, sv, sv, sv)


def _swa_bwd(sinks, sq, sk, sv, o, do, hook=None):
    t = sq.shape[0]
    rs = _seq_tile(t)
    bps, ns = rs // BLK, t // rs

    def body(sink_ref, bias_ref, q_ref, kp_ref, kc_ref, km_ref, vp_ref, vc_ref, vm_ref, o_ref, do_ref,
             dq_ref, dk_ref, dv_ref, dkm_ref, dvm_ref, dsink_ref, pk_ref, pv_ref):
        i = pl.program_id(0)

        @pl.when(i == 0)
        def _():
            pk_ref[...] = jnp.zeros_like(pk_ref)
            pv_ref[...] = jnp.zeros_like(pv_ref)
            dkm_ref[...] = jnp.zeros_like(dkm_ref)
            dvm_ref[...] = jnp.zeros_like(dvm_ref)
            dsink_ref[...] = jnp.zeros_like(dsink_ref)

        @pl.when(i == ns)
        def _():
            dk_ref[...] = pk_ref[...]
            dv_ref[...] = pv_ref[...]

        @pl.when(i < ns)
        def _():
            lo = lax.broadcasted_iota(jnp.int32, (BLK, 128), 1) < 64
            scale = SWA_HD ** -0.5
            sink_cols = [_swa_sink_col(sink_ref, kh) for kh in range(SWA_KVH)]
            parts_k = [[None] * SWA_KVH for _ in range(bps)]
            parts_v = [[None] * SWA_KVH for _ in range(bps)]
            dsinks = [jnp.zeros((1, 1), F32) for _ in range(SWA_QH)]
            for b in range(bps):
                rows = slice(b * BLK, (b + 1) * BLK)
                bias = bias_ref[jnp.minimum(i * bps + b, 2)]
                for kh in range(SWA_KVH):
                    ls = slice(128 * kh, 128 * (kh + 1))
                    kall = _swa_keys(kp_ref, kc_ref, km_ref, b, ls)
                    vall = _swa_keys(vp_ref, vc_ref, vm_ref, b, ls)
                    q4 = _swa_stack(q_ref, rows, kh, lo, BF16)
                    do4 = _swa_stack(do_ref, rows, kh, lo, F32)
                    p, psink = _swa_probs(q4, kall, bias, sink_cols[kh])
                    delta = jnp.sum(do4 * _swa_stack(o_ref, rows, kh, lo, F32), axis=-1, keepdims=True)
                    do4b = do4.astype(BF16)
                    ds = (p * (_dg(do4b, vall, NT) - delta) * scale).astype(BF16)
                    for g, pair in enumerate(_swa_unstack(_dot(ds, kall), lo)):
                        dq_ref[rows, 128 * (2 * kh + g):128 * (2 * kh + g + 1)] = pair
                    parts_k[b][kh] = _dg(ds, q4, TN)
                    parts_v[b][kh] = _dg(p.astype(BF16), do4b, TN)
                    dsk = psink * delta
                    for e in range(SWA_G):
                        h = SWA_G * kh + e
                        dsinks[h] = dsinks[h] - jnp.sum(dsk[e * BLK:(e + 1) * BLK], axis=0, keepdims=True)
            last = slice(rs - BLK, rs)
            for parts, out_ref, pend_ref, meta_ref in ((parts_k, dk_ref, pk_ref, dkm_ref),
                                                       (parts_v, dv_ref, pv_ref, dvm_ref)):
                for kh in range(SWA_KVH):
                    ls = slice(128 * kh, 128 * (kh + 1))
                    if bps > 1:
                        out_ref[0:rs - BLK, ls] = pend_ref[0:rs - BLK, ls]
                    out_ref[last, ls] = pend_ref[last, ls] + parts[0][kh][0:BLK]
                    meta = parts[0][kh][2 * BLK:3 * BLK]
                    for b in range(bps):
                        own = parts[b][kh][BLK:2 * BLK]
                        if b + 1 < bps:
                            own = own + parts[b + 1][kh][0:BLK]
                            meta = meta + parts[b + 1][kh][2 * BLK:3 * BLK]
                        pend_ref[b * BLK:(b + 1) * BLK, ls] = own
                    meta_ref[:, ls] += meta
            for h in range(SWA_QH):
                dsink_ref[h:h + 1, :] += jnp.broadcast_to(dsinks[h], (1, 128))

    cur, prev, first = _swa_specs(rs, ns)
    late = lambda w: pl.BlockSpec((rs, w), lambda i: (jnp.maximum(i - 1, 0), 0))
    bias = _swa_bias()
    return _pallas(
        body, name="swa_bwd", grid=(ns + 1,),
        in_specs=[pl.BlockSpec(memory_space=pltpu.SMEM), _full(bias.shape), cur(512), prev(256), cur(256), first(256),
                  prev(256), cur(256), first(256), cur(512), cur(512)],
        out_specs=[cur(512), late(256), late(256), first(256), first(256), _full((SWA_QH, 128))],
        out_shape=[jax.ShapeDtypeStruct((t, SWA_W), F32), jax.ShapeDtypeStruct((t, 256), F32),
                   jax.ShapeDtypeStruct((t, 256), F32), jax.ShapeDtypeStruct((BLK, 256), F32),
                   jax.ShapeDtypeStruct((BLK, 256), F32), jax.ShapeDtypeStruct((SWA_QH, 128), F32)],
        scratch_shapes=[pltpu.VMEM((rs, 256), F32), pltpu.VMEM((rs, 256), F32)],
        args=(sinks, bias, sq, sk, sk, sk, sv, sv, sv, o, do), hook=hook)


def _mix_out(h1, ogla, gg, oswa, wgn, wsn, wout, wpost):
    t = h1.shape[0]
    tm = _row_tile(t)

    def body(h_ref, og_ref, gg_ref, os_ref, wgn_ref, wsn_ref, wout_ref, wpost_ref, h2_ref, cat_ref, m_ref):
        parts = []
        for h in range(GLA_HEADS):
            ls = slice(GLA_DV * h, GLA_DV * (h + 1))
            y, _, _ = _rms(og_ref[:, ls], wgn_ref[...])
            g = gg_ref[:, ls]
            parts.append(y * (g * _sigmoid(g)))
        ys, _, _ = _rms(os_ref[...], wsn_ref[...])
        cat = jnp.concatenate(parts + [ys], axis=1).astype(BF16)
        cat_ref[...] = cat
        m = _dot(cat, wout_ref[...])
        m_ref[...] = m
        y, _, _ = _rms(m, wpost_ref[...])
        h2_ref[...] = h_ref[...] + y

    def row(w):
        return pl.BlockSpec((tm, w), lambda i: (i, 0))

    return pl.pallas_call(
        body, name="mix_out", grid=(t // tm,),
        in_specs=[row(D_MODEL), row(512), row(512), row(512), _full((1, GLA_DV)), _full((1, SWA_W)),
                  _full((D_MODEL, D_MODEL)), _full((1, D_MODEL))],
        out_specs=[row(D_MODEL), row(D_MODEL), row(D_MODEL)],
        out_shape=[jax.ShapeDtypeStruct((t, D_MODEL), F32), jax.ShapeDtypeStruct((t, D_MODEL), BF16),
                   jax.ShapeDtypeStruct((t, D_MODEL), F32)],
        compiler_params=_cparams(1),
    )(h1, ogla, gg, oswa, wgn, wsn, wout, wpost)


def _mix_out_bwd(dh2, m, ogla, gg, oswa, wgn, wsn, wout, wpost, hook=None):
    t = dh2.shape[0]
    tm = _row_tile(t)

    def body(dh_ref, m_ref, og_ref, gg_ref, os_ref, wgn_ref, wsn_ref, wout_ref, wpost_ref,
             dog_ref, dgg_ref, dos_ref, dm_ref, dwpost_ref, dwgn_ref, dwsn_ref):
        @pl.when(pl.program_id(0) == 0)
        def _():
            dwpost_ref[...] = jnp.zeros_like(dwpost_ref)
            dwgn_ref[...] = jnp.zeros_like(dwgn_ref)
            dwsn_ref[...] = jnp.zeros_like(dwsn_ref)

        wpost = wpost_ref[...]
        _, mh, r = _rms(m_ref[...], wpost)
        dm, dw = _rms_bwd(mh, r, wpost, dh_ref[...])
        dwpost_ref[...] += dw
        dmb = dm.astype(BF16)
        dm_ref[...] = dmb
        dcat = _dg(dmb, wout_ref[...], NT)
        wgn = wgn_ref[...]
        for h in range(GLA_HEADS):
            ls = slice(GLA_DV * h, GLA_DV * (h + 1))
            dog = dcat[:, ls]
            g = gg_ref[:, ls]
            sg = _sigmoid(g)
            y, xh, r = _rms(og_ref[:, ls], wgn)
            dgg_ref[:, ls] = dog * y * (sg * (1.0 + g * (1.0 - sg)))
            dx, dw = _rms_bwd(xh, r, wgn, dog * (g * sg))
            dog_ref[:, ls] = dx
            dwgn_ref[...] += dw
        wsn = wsn_ref[...]
        _, xh, r = _rms(os_ref[...], wsn)
        dx, dw = _rms_bwd(xh, r, wsn, dcat[:, GLA_W:])
        dos_ref[...] = dx
        dwsn_ref[...] += dw

    def row(w):
        return pl.BlockSpec((tm, w), lambda i: (i, 0))

    def rshape(w, dt=F32):
        return jax.ShapeDtypeStruct((t, w), dt)

    return _pallas(
        body, name="mix_out_bwd", grid=(t // tm,),
        in_specs=[row(D_MODEL), row(D_MODEL), row(512), row(512), row(512), _full((1, GLA_DV)), _full((1, SWA_W)),
                  _full((D_MODEL, D_MODEL)), _full((1, D_MODEL))],
        out_specs=[row(512), row(512), row(512), row(D_MODEL), _full((1, D_MODEL)), _full((1, GLA_DV)),
                   _full((1, SWA_W))],
        out_shape=[rshape(512), rshape(512), rshape(512), rshape(D_MODEL, BF16),
                   jax.ShapeDtypeStruct((1, D_MODEL), F32), jax.ShapeDtypeStruct((1, GLA_DV), F32),
                   jax.ShapeDtypeStruct((1, SWA_W), F32)],
        args=(dh2, m, ogla, gg, oswa, wgn, wsn, wout, wpost), hook=hook)


def _mix_in_bwd(dh2, h1, wmixpre, winp, wa2p, bap, cos, sin, ga, dgq, dgk, dgv, dgg, dla, dsq, dsk, dsv, dkm, dvm):
    t = h1.shape[0]
    tm = _row_tile(t)

    def body(dh2_ref, h_ref, w_ref, win_ref, wa2_ref, ba_ref, cos_ref, sin_ref, ga_ref, dgq_ref, dgk_ref, dgv_ref,
             dgg_ref, dla_ref, dsq_ref, dsk_ref, dsv_ref, dkm_ref, dvm_ref,
             dh1_ref, dproj_ref, dw_ref, dwa2_ref, dba_ref):
        i = pl.program_id(0)

        @pl.when(i == 0)
        def _():
            dw_ref[...] = jnp.zeros_like(dw_ref)
            dwa2_ref[...] = jnp.zeros_like(dwa2_ref)
            dba_ref[...] = jnp.zeros_like(dba_ref)

        first = (i == 0).astype(F32)
        c = cos_ref[...]
        s = -sin_ref[...]
        fh = _first_half_mask(tm)
        dproj_ref[:, P_GQ:P_GK] = dgq_ref[...].astype(BF16)
        dproj_ref[:, P_GK:P_GV] = dgk_ref[...].astype(BF16)
        dproj_ref[:, P_GV:P_GG] = dgv_ref[...].astype(BF16)
        dproj_ref[:, P_GG:P_GA] = dgg_ref[...].astype(BF16)
        gab = ga_ref[...].astype(BF16)
        z = _dot(gab, wa2_ref[...]) + ba_ref[...]
        row_id = i * tm + lax.broadcasted_iota(jnp.int32, (tm, 1), 0)
        dz = jnp.where(row_id >= PAD, dla_ref[...] * (1.0 / GLA_TAU) * (1.0 - _sigmoid(z)), 0.0)
        dzb = dz.astype(BF16)
        dba_ref[...] += jnp.sum(dz, axis=0, keepdims=True)
        dwa2_ref[...] += _dg(gab, dzb, TN)
        dproj_ref[:, P_GA:P_SQ] = _dg(dzb, wa2_ref[...], NT).astype(BF16)
        for k in range(4):
            dy = dsq_ref[:, 128 * k:128 * (k + 1)]
            dproj_ref[:, P_SQ + 128 * k:P_SQ + 128 * (k + 1)] = (dy * c + _rot_half(dy, fh) * s).astype(BF16)
        for k in range(2):
            ls = slice(128 * k, 128 * (k + 1))
            dy = dsk_ref[:, ls]
            dy = jnp.concatenate([dy[:BLK] + first * dkm_ref[:, ls], dy[BLK:]], axis=0) if tm > BLK else (
                dy + first * dkm_ref[:, ls])
            dproj_ref[:, P_SK + 128 * k:P_SK + 128 * (k + 1)] = (dy * c + _rot_half(dy, fh) * s).astype(BF16)
            dv = dsv_ref[:, ls]
            dv = jnp.concatenate([dv[:BLK] + first * dvm_ref[:, ls], dv[BLK:]], axis=0) if tm > BLK else (
                dv + first * dvm_ref[:, ls])
            dproj_ref[:, P_SV + 128 * k:P_SV + 128 * (k + 1)] = dv.astype(BF16)
        dn = _dg(dproj_ref[...], win_ref[...], NT)
        w = w_ref[...]
        _, hh, r = _rms(h_ref[...], w)
        dx, dw = _rms_bwd(hh, r, w, dn)
        dw_ref[...] += dw
        dh1_ref[...] = dh2_ref[...] + dx

    def row(w):
        return pl.BlockSpec((tm, w), lambda i: (i, 0))

    return pl.pallas_call(
        body, name="mix_in_bwd", grid=(t // tm,),
        in_specs=[row(D_MODEL), row(D_MODEL), _full((1, D_MODEL)), _full((D_MODEL, P_END)), _full((128, GLA_KW)),
                  _full((1, GLA_KW)), row(128), row(128), row(128), row(256), row(256), row(512), row(512), row(256),
                  row(512), row(256), row(256), _full((BLK, 256)), _full((BLK, 256))],
        out_specs=[row(D_MODEL), row(P_END), _full((1, D_MODEL)), _full((128, GLA_KW)), _full((1, GLA_KW))],
        out_shape=[jax.ShapeDtypeStruct((t, D_MODEL), F32), jax.ShapeDtypeStruct((t, P_END), BF16),
                   jax.ShapeDtypeStruct((1, D_MODEL), F32), jax.ShapeDtypeStruct((128, GLA_KW), F32),
                   jax.ShapeDtypeStruct((1, GLA_KW), F32)],
        compiler_params=_cparams(1),
    )(dh2, h1, wmixpre, winp, wa2p, bap, cos, sin, ga, dgq, dgk, dgv, dgg, dla, dsq, dsk, dsv, dkm, dvm)


def _adamw_update(w, g, m, v):
    m = ADAM_B1 * m + (1.0 - ADAM_B1) * g
    v = ADAM_B2 * v + (1.0 - ADAM_B2) * (g * g)
    m_hat = m / (1.0 - ADAM_B1 ** ADAM_STEP)
    v_hat = v / (1.0 - ADAM_B2 ** ADAM_STEP)
    return -ADAM_LR * (m_hat / (jnp.sqrt(v_hat) + ADAM_EPS) + ADAM_WD * w), m, v


def _adamw(w, g, m, v):
    r, c = w.shape
    tr = _div_tile(r)

    def body(w_ref, g_ref, m_ref, v_ref, d_ref, nm_ref, nv_ref):
        d_ref[...], nm_ref[...], nv_ref[...] = _adamw_update(w_ref[...], g_ref[...], m_ref[...], v_ref[...])

    spec = pl.BlockSpec((tr, c), lambda i: (i, 0))
    shape = jax.ShapeDtypeStruct((r, c), F32)
    return pl.pallas_call(
        body, name="adamw", grid=(r // tr,), in_specs=[spec] * 4, out_specs=[spec] * 3, out_shape=[shape] * 3,
        compiler_params=_cparams(1),
    )(w, g, m, v)


def _adamw_halves(w, g_mine, g_other, m, v, c_idx, row0=0):
    r, c = w.shape
    h = g_mine.shape[0]
    tr = _div_tile(math.gcd(r, h))
    nth = h // tr
    t0 = row0 // tr
    assert t0 * tr == row0

    def body(c_ref, w_ref, gm_ref, go_ref, m_ref, v_ref, g_ref, d_ref, nm_ref, nv_ref):
        hh = (t0 + pl.program_id(0)) // nth
        g = jnp.where(hh == c_ref[0], gm_ref[...], go_ref[...])
        g_ref[...] = g
        d_ref[...], nm_ref[...], nv_ref[...] = _adamw_update(w_ref[...], g, m_ref[...], v_ref[...])

    spec = pl.BlockSpec((tr, c), lambda i, c_ref: (i, 0))
    gspec = pl.BlockSpec((tr, c), lambda i, c_ref: ((t0 + i) % nth, 0))
    shape = jax.ShapeDtypeStruct((r, c), F32)
    return pl.pallas_call(
        body, name="adamw_halves",
        grid_spec=pltpu.PrefetchScalarGridSpec(
            num_scalar_prefetch=1, grid=(r // tr,), in_specs=[spec, gspec, gspec, spec, spec], out_specs=[spec] * 4),
        out_shape=[shape] * 4, compiler_params=_cparams(1),
    )(c_idx, w, g_mine, g_other, m, v)


def _place():
    x, y, c = lax.axis_index("x"), lax.axis_index("y"), lax.axis_index("c")
    chips = [(1 - x, y), (x, 1 - y), (1 - x, 1 - y)]
    return x, y, c, chips


def _remote(send_sem, recv_sem, src, dst, to):
    return pltpu.make_async_remote_copy(src_ref=src, dst_ref=dst, send_sem=send_sem, recv_sem=recv_sem,
                                        device_id=to, device_id_type=MESH)


def _half(ref_rows, c):
    h = ref_rows // 2
    return pl.ds(pl.multiple_of(c * h, 8), h)


def _own_slot(shard, q):
    return lax.dynamic_update_slice(jnp.zeros((N_CHIPS,) + shard.shape, shard.dtype), shard[None], (q, 0, 0))


class _GatherChips:
    has_mid = True

    def __init__(self, bufs):
        n = len(bufs)
        self.inputs = list(bufs)
        self.out_shape = [jax.ShapeDtypeStruct(b.shape, b.dtype) for b in bufs]
        self.aliases = [(t, t) for t in range(n)]
        self.scratch = [pltpu.SemaphoreType.DMA((n, 6)), pltpu.SemaphoreType.DMA((n, 6))]

    def start(self, ins, outs, scr):
        send, recv = scr
        x, y, c, chips = _place()
        q = 2 * x + y
        for t, (i_ref, o_ref) in enumerate(zip(ins, outs)):
            rows = _half(i_ref.shape[1], c)
            for j, (cx, cy) in enumerate(chips):
                _remote(send.at[t, j], recv.at[t, j], i_ref.at[q, rows], o_ref.at[q, rows], (cx, cy, c)).start()

    def mid(self, ins, outs, scr):
        send, recv = scr
        x, y, c, chips = _place()
        for t, o_ref in enumerate(outs):
            rows = _half(o_ref.shape[1], c)
            for j, (cx, cy) in enumerate(chips):
                slot = o_ref.at[2 * cx + cy, rows]
                _remote(send.at[t, j], recv.at[t, j], slot, slot, (cx, cy, c)).wait_recv()
                _remote(send.at[t, 3 + j], recv.at[t, 3 + j], slot, slot, (x, y, 1 - c)).start()

    def finish(self, ins, outs, scr):
        send, recv = scr
        x, y, c, chips = _place()
        for t, o_ref in enumerate(outs):
            mine, other = _half(o_ref.shape[1], c), _half(o_ref.shape[1], 1 - c)
            for j, (cx, cy) in enumerate(chips):
                slot = o_ref.at[2 * cx + cy, other]
                _remote(send.at[t, 3 + j], recv.at[t, 3 + j], slot, slot, (x, y, 1 - c)).wait_recv()
            for j, (cx, cy) in enumerate(chips):
                sent = o_ref.at[2 * cx + cy, mine]
                _remote(send.at[t, j], recv.at[t, j], sent, sent, (cx, cy, c)).wait_send()
                _remote(send.at[t, 3 + j], recv.at[t, 3 + j], sent, sent, (x, y, 1 - c)).wait_send()


class _PairExchange:
    has_mid = False
    aliases = ()

    def __init__(self, arrs):
        n = len(arrs)
        self.inputs = list(arrs)
        self.out_shape = [jax.ShapeDtypeStruct((a.shape[0], a.shape[1] // 2, a.shape[2]), a.dtype) for a in arrs]
        self.scratch = [pltpu.SemaphoreType.DMA((n,)), pltpu.SemaphoreType.DMA((n,))]

    def _copies(self, ins, outs, scr):
        send, recv = scr
        x, y, c, _ = _place()
        return [_remote(send.at[t], recv.at[t], i_ref.at[:, _half(i_ref.shape[1], 1 - c)], o_ref, (x, y, 1 - c))
                for t, (i_ref, o_ref) in enumerate(zip(ins, outs))]

    def start(self, ins, outs, scr):
        for cp in self._copies(ins, outs, scr):
            cp.start()

    def finish(self, ins, outs, scr):
        for cp in self._copies(ins, outs, scr):
            cp.wait()


class _ChipScatter:
    has_mid = False
    aliases = ()

    def __init__(self, arrs):
        n = len(arrs)
        self.inputs = list(arrs)
        self.out_shape = [jax.ShapeDtypeStruct((3,) + a.shape[1:], a.dtype) for a in arrs]
        self.scratch = [pltpu.SemaphoreType.DMA((n, 3)), pltpu.SemaphoreType.DMA((n, 3))]

    def _copies(self, ins, outs, scr):
        send, recv = scr
        x, y, c, chips = _place()
        return [_remote(send.at[t, j], recv.at[t, j], i_ref.at[2 * cx + cy], o_ref.at[j], (cx, cy, c))
                for t, (i_ref, o_ref) in enumerate(zip(ins, outs)) for j, (cx, cy) in enumerate(chips)]

    def start(self, ins, outs, scr):
        for cp in self._copies(ins, outs, scr):
            cp.start()

    def finish(self, ins, outs, scr):
        for cp in self._copies(ins, outs, scr):
            cp.wait()


class _PairShare:
    has_mid = False
    aliases = ()

    def __init__(self, arrs):
        n = len(arrs)
        self.inputs = list(arrs)
        self.out_shape = [jax.ShapeDtypeStruct(a.shape, a.dtype) for a in arrs]
        self.scratch = [pltpu.SemaphoreType.DMA((n,)), pltpu.SemaphoreType.DMA((n,))]

    def _copies(self, ins, outs, scr):
        send, recv = scr
        x, y, c, _ = _place()
        return [_remote(send.at[t], recv.at[t], i_ref, o_ref, (x, y, 1 - c))
                for t, (i_ref, o_ref) in enumerate(zip(ins, outs))]

    def start(self, ins, outs, scr):
        for cp in self._copies(ins, outs, scr):
            cp.start()

    def finish(self, ins, outs, scr):
        for cp in self._copies(ins, outs, scr):
            cp.wait()


def _comm_call(hook, name):
    n_in, n_out = len(hook.inputs), len(hook.out_shape)

    def body(*refs):
        ins, outs, scr = refs[:n_in], refs[n_in:n_in + n_out], refs[n_in + n_out:]
        hook.start(ins, outs, scr)
        if hook.has_mid:
            hook.mid(ins, outs, scr)
        hook.finish(ins, outs, scr)

    return pl.pallas_call(body, name=name, in_specs=[ANY] * n_in, out_specs=[ANY] * n_out,
                          out_shape=list(hook.out_shape), scratch_shapes=list(hook.scratch),
                          input_output_aliases=dict(hook.aliases))(*hook.inputs)


def _all_gather_devices(vecs):
    n = len(vecs)

    def body(*refs):
        x_refs, out_refs = refs[:n], refs[n:2 * n]
        send_sems, recv_sems, local_sems = refs[2 * n:]
        x, y, c, chips = _place()
        me, sibling = (x, y, c), (x, y, 1 - c)
        waits = []
        for t, (x_ref, out_ref) in enumerate(zip(x_refs, out_refs)):
            def slot(px, py, pc, out_ref=out_ref):
                return out_ref.at[4 * px + 2 * py + pc]

            def copy(k, block, to, src=None, t=t, slot=slot):
                return pltpu.make_async_remote_copy(
                    src_ref=slot(*block) if src is None else src, dst_ref=slot(*block), send_sem=send_sems.at[t, k],
                    recv_sem=recv_sems.at[t, k], device_id=to, device_id_type=MESH)

            mine = pltpu.make_async_copy(x_ref, slot(*me), local_sems.at[t])
            mine.start()
            first = [copy(0, me, sibling, src=x_ref)]
            first += [copy(1 + j, me, (*chip, c), src=x_ref) for j, chip in enumerate(chips)]
            for cp in first:
                cp.start()
            waits.append((copy, mine, first))
        for copy, mine, first in waits:
            passed = [copy(4 + j, (*chip, c), sibling) for j, chip in enumerate(chips)]
            for j, chip in enumerate(chips):
                copy(1 + j, (*chip, c), me).wait_recv()
                passed[j].start()
            copy(0, sibling, me).wait_recv()
            for j, chip in enumerate(chips):
                copy(4 + j, (*chip, 1 - c), me).wait_recv()
            for cp in first + passed:
                cp.wait_send()
            mine.wait()

    vmem = pl.BlockSpec(memory_space=pltpu.VMEM)
    return pl.pallas_call(
        body, name="all_gather_devices", in_specs=[vmem] * n, out_specs=[vmem] * n,
        out_shape=[jax.ShapeDtypeStruct((N_DEV,) + v.shape, v.dtype) for v in vecs],
        scratch_shapes=[pltpu.SemaphoreType.DMA((n, 7)), pltpu.SemaphoreType.DMA((n, 7)),
                        pltpu.SemaphoreType.DMA((n,))],
    )(*vecs)


def _pair_sum(g, other, c_idx):
    nq, r, w = g.shape
    h = r // 2
    tr = _div_tile(h)
    nt = h // tr

    def body(c_ref, g_ref, o_ref, s_ref):
        s_ref[...] = (g_ref[...].astype(F32) + o_ref[...].astype(F32)).astype(s_ref.dtype)

    return pl.pallas_call(
        body, name="pair_sum",
        grid_spec=pltpu.PrefetchScalarGridSpec(
            num_scalar_prefetch=1, grid=(nq, nt),
            in_specs=[pl.BlockSpec((None, tr, w), lambda k, i, c_ref: (k, c_ref[0] * nt + i, 0)),
                      pl.BlockSpec((None, tr, w), lambda k, i, c_ref: (k, i, 0))],
            out_specs=pl.BlockSpec((None, tr, w), lambda k, i, c_ref: (k, i, 0))),
        out_shape=jax.ShapeDtypeStruct((nq, h, w), g.dtype),
        compiler_params=_cparams(2),
    )(c_idx, g, other)


def _chip_sum(s, others, q_idx):
    _, h, w = s.shape
    tr = _div_tile(h)

    def body(q_ref, s_ref, o_ref, out_ref):
        out_ref[...] = ((s_ref[...].astype(F32) + o_ref[0].astype(F32)) + o_ref[1].astype(F32)) + o_ref[2].astype(F32)

    return pl.pallas_call(
        body, name="chip_sum",
        grid_spec=pltpu.PrefetchScalarGridSpec(
            num_scalar_prefetch=1, grid=(h // tr,),
            in_specs=[pl.BlockSpec((None, tr, w), lambda i, q_ref: (q_ref[0], i, 0)),
                      pl.BlockSpec((3, tr, w), lambda i, q_ref: (0, i, 0))],
            out_specs=pl.BlockSpec((tr, w), lambda i, q_ref: (i, 0))),
        out_shape=jax.ShapeDtypeStruct((h, w), F32),
        compiler_params=_cparams(1),
    )(q_idx, s, others)


def _small_update(q_idx, parts, ws, ms, vs, col_block):
    n = len(parts)
    has_w = [w is not None for w in ws]

    def body(q_ref, *refs):
        pos = 0
        ins = []
        for t in range(n):
            k = 4 if has_w[t] else 1
            ins.append(refs[pos:pos + k])
            pos += k
        outs = refs[pos:]
        opos = 0
        for t in range(n):
            p_ref = ins[t][0]
            g = p_ref[0]
            for s in range(1, p_ref.shape[0]):
                g = g + p_ref[s]
            if has_w[t]:
                _, w_ref, m_ref, v_ref = ins[t]
                g_ref, d_ref, nm_ref, nv_ref = outs[opos:opos + 4]
                opos += 4
                g_ref[...] = g
                d_ref[...], nm_ref[...], nv_ref[...] = _adamw_update(w_ref[...], g, m_ref[...], v_ref[...])
            else:
                outs[opos][...] = g
                opos += 1

    def whole(shape):
        nd = len(shape)
        return pl.BlockSpec(shape, lambda i, q_ref: (0,) * nd)

    in_specs, out_specs, out_shape, args = [], [], [], []
    for t in range(n):
        k, r, wf = parts[t].shape
        if col_block[t]:
            w = wf // N_CHIPS
            in_specs.append(pl.BlockSpec((k, r, w), lambda i, q_ref: (0, 0, q_ref[0])))
        else:
            w = wf
            in_specs.append(whole((k, r, wf)))
        args.append(parts[t])
        if has_w[t]:
            assert ws[t].shape == (r, w), (ws[t].shape, r, w)
            in_specs += [whole((r, w))] * 3
            args += [ws[t], ms[t], vs[t]]
            out_specs += [whole((r, w))] * 4
            out_shape += [jax.ShapeDtypeStruct((r, w), F32)] * 4
        else:
            out_specs.append(whole((r, w)))
            out_shape.append(jax.ShapeDtypeStruct((r, w), F32))
    return pl.pallas_call(
        body, name="small_update",
        grid_spec=pltpu.PrefetchScalarGridSpec(num_scalar_prefetch=1, grid=(1,), in_specs=in_specs,
                                               out_specs=out_specs),
        out_shape=out_shape, compiler_params=_cparams(1),
    )(q_idx, *args)


def _pack_win(w_in):
    o = np.cumsum((0,) + IN_SPLITS)
    gq, gk, gv, gg, ga, sq, sk, sv = [w_in[:, o[i]:o[i + 1]] for i in range(8)]
    z = jnp.zeros((w_in.shape[0], 128 - GLA_RANK), w_in.dtype)
    dup = lambda a: jnp.concatenate([a[:, :64], a[:, :64], a[:, 64:], a[:, 64:]], axis=1)
    return jnp.concatenate([gq, gk, gv, gg, ga, z, sq, dup(sk), dup(sv)], axis=1)


def _unpack_dwin(d):
    und = lambda a: jnp.concatenate([a[:, 0:64] + a[:, 64:128], a[:, 128:192] + a[:, 192:256]], axis=1)
    return jnp.concatenate([d[:, :P_GA], d[:, P_GA:P_GA + GLA_RANK], d[:, P_SQ:P_SK], und(d[:, P_SK:P_SV]),
                            und(d[:, P_SV:P_END])], axis=1)


def _local_step(x, target, meta, p):
    s = x.shape[0]
    t = s + BLK
    h0 = jnp.concatenate([jnp.zeros((PAD, D_MODEL), F32), meta, x], axis=0)
    cos, sin = _rope_tables(t)

    h1, n1, g1, u1, a1, f1 = _ffn_fwd(h0, p["ffn1_pre_norm"], p["ffn1_w_gate"], p["ffn1_w_up"], p["ffn1_w_down"],
                                      p["ffn1_post_norm"])
    n2, gq, gk, gv, gg, ga, la, sq, sk, sv = _mix_proj(h1, p["mix_pre_norm"], p["w_in"], p["gla_w_a2"], p["gla_b_a"],
                                                       cos, sin)
    ogla, ss = _gla_fwd(gq, gk, gv, la)
    oswa = _swa_fwd(p["swa_sinks"], sq, sk, sv)
    h2, cat, m = _mix_out(h1, ogla, gg, oswa, p["gla_out_norm"], p["swa_out_norm"], p["w_out"], p["mix_post_norm"])
    dy, n3, g3, u3, a3, f3, sse = _ffn_fwd(h2, p["ffn2_pre_norm"], p["ffn2_w_gate"], p["ffn2_w_up"],
                                           p["ffn2_w_down"], p["ffn2_post_norm"], target=target)

    grads = {}
    dh2, df3, dg3, du3, grads["ffn2_pre_norm"], grads["ffn2_post_norm"] = _ffn_bwd(
        dy, h2, f3, g3, u3, p["ffn2_pre_norm"], p["ffn2_w_gate"], p["ffn2_w_up"], p["ffn2_w_down"],
        p["ffn2_post_norm"])
    (gud,) = _ffn_wgrad(n3, df3, dg3, du3, a3)
    grads["ffn2_w_gate"], grads["ffn2_w_up"], grads["ffn2_w_down"] = gud[:, :FJ], gud[:, FJ:2 * FJ], gud[:, 2 * FJ:]

    dogla, dgg, doswa, dm, grads["mix_post_norm"], grads["gla_out_norm"], grads["swa_out_norm"] = _mix_out_bwd(
        dh2, m, ogla, gg, oswa, p["gla_out_norm"], p["swa_out_norm"], p["w_out"], p["mix_post_norm"])
    grads["w_out"] = _xty(cat, dm)
    dsq, dsk, dsv, dkm, dvm, dsinks = _swa_bwd(p["swa_sinks"], sq, sk, sv, oswa, doswa)
    grads["swa_sinks"] = dsinks[:, 0]
    dgq, dgk, dgv, dla = _gla_bwd(gq, gk, gv, la, ss, dogla)
    dh1, dproj, grads["mix_pre_norm"], dwa2p, grads["gla_b_a"] = _mix_in_bwd(
        dh2, h1, p["mix_pre_norm"], p["w_in"], p["gla_w_a2"], p["gla_b_a"], cos, sin, ga, dgq, dgk, dgv, dgg, dla,
        dsq, dsk, dsv, dkm, dvm)
    grads["gla_w_a2"] = dwa2p[:GLA_RANK]
    grads["w_in"] = _unpack_dwin(_xty(n2, dproj))

    dh0, df1, dg1, du1, grads["ffn1_pre_norm"], grads["ffn1_post_norm"] = _ffn_bwd(
        dh1, h0, f1, g1, u1, p["ffn1_pre_norm"], p["ffn1_w_gate"], p["ffn1_w_up"], p["ffn1_w_down"],
        p["ffn1_post_norm"])
    (gud,) = _ffn_wgrad(n1, df1, dg1, du1, a1)
    grads["ffn1_w_gate"], grads["ffn1_w_up"], grads["ffn1_w_down"] = gud[:, :FJ], gud[:, FJ:2 * FJ], gud[:, 2 * FJ:]
    grads["meta_tokens"] = dh0[PAD:BLK]
    return sse[0, 0], dh0[BLK:], grads


WEIGHTS = ['meta_tokens', 'ffn1_pre_norm', 'ffn1_w_gate', 'ffn1_w_up', 'ffn1_w_down', 'ffn1_post_norm',
           'mix_pre_norm', 'w_in', 'gla_w_a2', 'gla_b_a', 'gla_out_norm', 'swa_sinks', 'swa_out_norm', 'w_out',
           'mix_post_norm', 'ffn2_pre_norm', 'ffn2_w_gate', 'ffn2_w_up', 'ffn2_w_down', 'ffn2_post_norm']
BIG = ['ffn1_w_gate', 'ffn1_w_up', 'ffn1_w_down', 'w_in', 'w_out', 'ffn2_w_gate', 'ffn2_w_up', 'ffn2_w_down']
SMALL = [n for n in WEIGHTS if n not in BIG]
FJ = D_FF // N_CHIPS
D_IN_J = D_IN // N_CHIPS
D_OUT_J = D_MODEL // N_CHIPS
TRANSPOSED = ('ffn1_w_gate', 'ffn1_w_up', 'ffn2_w_gate', 'ffn2_w_up')


def _shard2d(name, a):
    return a[0].T if name in TRANSPOSED else a[0]


def _unshard2d(name, a):
    return (a.T if name in TRANSPOSED else a)[None]


def _small_rows(name, a):
    flat = a.reshape(-1)
    rows = -(-flat.shape[0] // 1024) * 8
    return jnp.pad(flat, (0, rows * 128 - flat.shape[0])).reshape(rows, 128)


def kernel(x, meta_tokens, ffn1_pre_norm, ffn1_w_gate, ffn1_w_up, ffn1_w_down, ffn1_post_norm, mix_pre_norm, w_in, gla_w_a2, gla_b_a, gla_out_norm, swa_sinks, swa_out_norm, w_out, mix_post_norm, ffn2_pre_norm, ffn2_w_gate, ffn2_w_up, ffn2_w_down, ffn2_post_norm, loss_target, m_meta_tokens, m_ffn1_pre_norm, m_ffn1_w_gate, m_ffn1_w_up, m_ffn1_w_down, m_ffn1_post_norm, m_mix_pre_norm, m_w_in, m_gla_w_a2, m_gla_b_a, m_gla_out_norm, m_swa_sinks, m_swa_out_norm, m_w_out, m_mix_post_norm, m_ffn2_pre_norm, m_ffn2_w_gate, m_ffn2_w_up, m_ffn2_w_down, m_ffn2_post_norm, v_meta_tokens, v_ffn1_pre_norm, v_ffn1_w_gate, v_ffn1_w_up, v_ffn1_w_down, v_ffn1_post_norm, v_mix_pre_norm, v_w_in, v_gla_w_a2, v_gla_b_a, v_gla_out_norm, v_swa_sinks, v_swa_out_norm, v_w_out, v_mix_post_norm, v_ffn2_pre_norm, v_ffn2_w_gate, v_ffn2_w_up, v_ffn2_w_down, v_ffn2_post_norm):
    args = dict(locals())
    w = {n: args[n] for n in WEIGHTS}
    mom = {n: args["m_" + n] for n in WEIGHTS}
    var = {n: args["v_" + n] for n in WEIGHTS}
    cx, cy, cc = lax.axis_index("x"), lax.axis_index("y"), lax.axis_index("c")
    q_idx = (2 * cx + cy).astype(jnp.int32).reshape(1)
    c_idx = cc.astype(jnp.int32).reshape(1)

    q_chip = 2 * cx + cy
    bf = {n: _own_slot(_shard2d(n, w[n]).astype(BF16), q_chip) for n in BIG}
    qc_idx = jnp.stack([q_chip, cc]).astype(jnp.int32)
    early = _GatherChips([_own_slot(w["meta_tokens"], q_chip),
                          _own_slot(w["gla_w_a2"].reshape(GLA_RANK, GLA_KW // N_CHIPS), q_chip)])
    meta4, wa24 = _comm_call(early, "gather_small")
    meta_full = meta4.transpose(1, 0, 2).reshape(N_META, D_MODEL)
    wa2p = jnp.pad(wa24.transpose(1, 0, 2).reshape(GLA_RANK, GLA_KW), ((0, 128 - GLA_RANK), (0, 0))).astype(BF16)
    sinks = w["swa_sinks"].reshape(SWA_QH)

    seq, target = x[0], loss_target[0]
    t = seq.shape[0] + BLK
    h0 = jnp.concatenate([jnp.zeros((PAD, D_MODEL), F32), meta_full, seq], axis=0)
    cos, sin = _rope_tables(t)
    late = _GatherChips([bf["w_in"], bf["w_out"], bf["ffn2_w_gate"], bf["ffn2_w_up"], bf["ffn2_w_down"]])
    n1 = _rms_cast(h0, w["ffn1_pre_norm"])
    (h1, g1, u1, a1, f1), (wg1, wu1, wd1), (win4, wout4, wg2, wu2, wd2) = _ffn_fwd_gather(
        h0, n1, [bf["ffn1_w_gate"], bf["ffn1_w_up"], bf["ffn1_w_down"]], w["ffn1_post_norm"], qc_idx, late)
    winp = _pack_win(win4.transpose(1, 0, 2).reshape(D_MODEL, D_IN))
    wout = wout4.reshape(D_MODEL, D_MODEL)
    n2, gq, gk, gv, gg, ga, la, sq, sk, sv = _mix_proj(h1, w["mix_pre_norm"], winp, wa2p, w["gla_b_a"], cos, sin)
    ogla, ss = _gla_fwd(gq, gk, gv, la)
    oswa = _swa_fwd(sinks, sq, sk, sv)
    h2, cat, m = _mix_out(h1, ogla, gg, oswa, w["gla_out_norm"], w["swa_out_norm"], wout, w["mix_post_norm"])
    dy, n3, g3, u3, a3, f3, sse = _ffn_fwd(h2, w["ffn2_pre_norm"], wg2, wu2, wd2, w["ffn2_post_norm"], target=target)
    loss = lax.psum(sse[0, 0] * (0.5 / D_MODEL), ("x", "y", "c"))

    g = {}
    dh2, df3, dg3, du3, g["ffn2_pre_norm"], g["ffn2_post_norm"] = _ffn_bwd(
        dy, h2, f3, g3, u3, w["ffn2_pre_norm"], wg2, wu2, wd2, w["ffn2_post_norm"])
    (gf2,) = _ffn_wgrad(n3, df3, dg3, du3, a3)
    (dogla, dgg, doswa, dm, g["mix_post_norm"], g["gla_out_norm"], g["swa_out_norm"]), (rgf2,) = _mix_out_bwd(
        dh2, m, ogla, gg, oswa, w["gla_out_norm"], w["swa_out_norm"], wout, w["mix_post_norm"],
        hook=_PairExchange([gf2]))
    sgf2 = _pair_sum(gf2, rgf2, c_idx)
    gout = _xty(cat, dm).reshape(N_CHIPS, D_OUT_J, D_MODEL).astype(BF16)
    (dsq, dsk, dsv, dkm, dvm, dsinks), (ogf2,) = _swa_bwd(sinks, sq, sk, sv, oswa, doswa,
                                                          hook=_ChipScatter([sgf2]))
    g["swa_sinks"] = dsinks
    dgq, dgk, dgv, dla = _gla_bwd(gq, gk, gv, la, ss, dogla)
    dh1, dproj, g["mix_pre_norm"], dwa2p, g["gla_b_a"] = _mix_in_bwd(
        dh2, h1, w["mix_pre_norm"], winp, wa2p, w["gla_b_a"], cos, sin, ga, dgq, dgk, dgv, dgg, dla,
        dsq, dsk, dsv, dkm, dvm)
    g["gla_w_a2"] = dwa2p[:GLA_RANK]
    gin = _unpack_dwin(_xty(n2, dproj)).reshape(D_MODEL, N_CHIPS, D_IN_J).transpose(1, 0, 2).astype(BF16)
    (dh0, df1, dg1, du1, g["ffn1_pre_norm"], g["ffn1_post_norm"]), (rgin, rgout) = _ffn_bwd(
        dh1, h0, f1, g1, u1, w["ffn1_pre_norm"], wg1, wu1, wd1, w["ffn1_post_norm"],
        hook=_PairExchange([gin, gout]))
    sgin, sgout = _pair_sum(gin, rgin, c_idx), _pair_sum(gout, rgout, c_idx)
    own1, others1, (ogin, ogout) = _ffn_wgrad_reduce(n1, df1, dg1, du1, a1, qc_idx, _ChipScatter([sgin, sgout]))
    g["meta_tokens"] = dh0[PAD:BLK]
    grad_x = dh0[BLK:]
    halves = [_chip_sum(own1[None], others1, jnp.zeros((1,), jnp.int32))]
    halves += [_chip_sum(s, o, q_idx) for s, o in ((sgin, ogin), (sgout, ogout), (sgf2, ogf2))]
    others = _comm_call(_PairShare(halves), "pair_share")
    reduced = {"ffn1_w_gate": (0, 0), "ffn1_w_up": (0, FJ), "ffn1_w_down": (0, 2 * FJ), "w_in": (1, 0),
               "w_out": (2, 0), "ffn2_w_gate": (3, 0), "ffn2_w_up": (3, FJ), "ffn2_w_down": (3, 2 * FJ)}
    grad, delta, new_m, new_v = {}, {}, {}, {}
    for n in BIG:
        k, row0 = reduced[n]
        outs = _adamw_halves(_shard2d(n, w[n]), halves[k], others[k], _shard2d(n, mom[n]), _shard2d(n, var[n]),
                             c_idx, row0)
        grad[n], delta[n], new_m[n], new_v[n] = [_unshard2d(n, a) for a in outs]

    late = ["gla_w_a2", "swa_sinks"]
    direct = [n for n in SMALL if n not in late]
    names = direct + late
    gathered = _all_gather_devices([g[n] for n in names])
    mat = lambda a: a.reshape(a.shape[-2:])
    none2 = [None] * len(late)
    outs = _small_update(q_idx, gathered, [mat(w[n]) for n in direct] + none2, [mat(mom[n]) for n in direct] + none2,
                         [mat(var[n]) for n in direct] + none2, [n == "meta_tokens" for n in names])
    sum_a2, sum_sinks = outs[4 * len(direct):]
    g_late = [lax.dynamic_slice_in_dim(sum_a2, q_chip * (GLA_KW // N_CHIPS), GLA_KW // N_CHIPS, axis=1)[None],
              sum_sinks[:, 0].reshape(1, 1, SWA_QH)]
    outs = list(outs[:4 * len(direct)]) + list(_small_update(
        q_idx, g_late, [mat(w[n]) for n in late], [mat(mom[n]) for n in late], [mat(var[n]) for n in late],
        [False, False]))
    for k, n in enumerate(names):
        grad[n], delta[n], new_m[n], new_v[n] = [a.reshape(w[n].shape) for a in outs[4 * k:4 * k + 4]]

    return (loss, grad_x[None], *[grad[n] for n in WEIGHTS], *[delta[n] for n in WEIGHTS],
            *[new_m[n] for n in WEIGHTS], *[new_v[n] for n in WEIGHTS])
```

```python
import functools
import math

import numpy as np
import jax
import jax.numpy as jnp
from jax import lax
from jax.experimental import pallas as pl
from jax.experimental.pallas import tpu as pltpu

F32 = jnp.float32
BF16 = jnp.bfloat16
MESH = pl.DeviceIdType.MESH

D_MODEL = 1024
D_FF = 2816
N_CHIPS = 4
N_DEV = 8
N_META = 16
BLK = 128
PAD = BLK - N_META
GLA_CHUNK = 64
GLA_HEADS = 4
GLA_DV = 128
GLA_DK = 64
GLA_KW = GLA_HEADS * GLA_DK
GLA_W = GLA_HEADS * GLA_DV
GLA_RANK = 16
GLA_TAU = 16.0
SWA_HD = 64
SWA_QH = 8
SWA_KVH = 2
SWA_W = SWA_QH * SWA_HD
WINDOW = 128
ROPE_THETA = 10000.0
EPS = 1e-6
NEG_INF = -1e30
IN_SPLITS = (256, 256, 512, 512, 16, 512, 128, 128)
D_IN = sum(IN_SPLITS)
P_GQ, P_GK, P_GV, P_GG, P_GA, P_SQ, P_SK, P_SV, P_END = 0, 256, 512, 1024, 1536, 1664, 2176, 2432, 2688
ADAM_LR, ADAM_B1, ADAM_B2, ADAM_EPS, ADAM_WD, ADAM_STEP = 0.001, 0.9, 0.999, 1e-08, 0.01, 10
VMEM_LIMIT = 56 * 1024 * 1024

NT = (((1,), (1,)), ((), ()))
TN = (((0,), (0,)), ((), ()))


def _cparams(n_axes):
    return pltpu.CompilerParams(dimension_semantics=("arbitrary",) * n_axes, vmem_limit_bytes=VMEM_LIMIT)


def _row_tile(t):
    for tm in (640, 512, 384, 256, 128):
        if t % tm == 0:
            return tm
    raise ValueError(t)


SEQ_BLOCKS_PER_STEP = 5


def _seq_tile(t):
    return SEQ_BLOCKS_PER_STEP * BLK if t % (SEQ_BLOCKS_PER_STEP * BLK) == 0 else BLK


ROW_PARTS = 2


def _row_parts(tm):
    n = ROW_PARTS if tm % (16 * ROW_PARTS) == 0 else 1
    return [slice(k * (tm // n), (k + 1) * (tm // n)) for k in range(n)]


def _contract_tile(t):
    return 1664 if t % 1664 == 0 else _row_tile(t)


def _div_tile(r, cap=512):
    best = None
    for tr in range(8, min(r, cap) + 1, 8):
        if r % tr == 0:
            best = tr
    return best if best is not None else r


def _dot(a, b):
    return jnp.dot(a, b, preferred_element_type=F32)


def _dg(a, b, dims):
    return lax.dot_general(a, b, dims, preferred_element_type=F32)


def _rms(x, w):
    r = lax.rsqrt(jnp.mean(x * x, axis=-1, keepdims=True) + EPS)
    xh = x * r
    return xh * w, xh, r


def _rms_bwd(xh, r, w, dy):
    wdy = dy * w
    dx = r * (wdy - xh * jnp.mean(wdy * xh, axis=-1, keepdims=True))
    dw = jnp.sum(dy * xh, axis=0, keepdims=True)
    return dx, dw


def _sigmoid(x):
    return 1.0 / (1.0 + jnp.exp(-x))


def _full(shape):
    nd = len(shape)
    return pl.BlockSpec(shape, lambda *_: (0,) * nd)


ANY = pl.BlockSpec(memory_space=pl.ANY)


def _pallas(body, *, name, grid, in_specs, out_specs, out_shape, args, scratch_shapes=(), hook=None):
    n_axes = len(grid)
    if hook is None:
        return pl.pallas_call(body, name=name, grid=grid, in_specs=list(in_specs), out_specs=list(out_specs),
                              out_shape=list(out_shape), scratch_shapes=list(scratch_shapes),
                              compiler_params=_cparams(n_axes))(*args)
    n_in, n_out, n_scr = len(in_specs), len(out_specs), len(scratch_shapes)
    h_in, h_out = len(hook.inputs), len(hook.out_shape)
    total = math.prod(grid)

    def wrapped(*refs):
        ins, hins = refs[:n_in], refs[n_in:n_in + h_in]
        o0 = n_in + h_in
        outs, houts = refs[o0:o0 + n_out], refs[o0 + n_out:o0 + n_out + h_out]
        s0 = o0 + n_out + h_out
        scr, hscr = refs[s0:s0 + n_scr], refs[s0 + n_scr:]
        step = pl.program_id(0)
        for a in range(1, n_axes):
            step = step * grid[a] + pl.program_id(a)

        @pl.when(step == 0)
        def _():
            hook.start(hins, houts, hscr)

        body(*ins, *outs, *scr)

        if hook.has_mid:
            @pl.when(step == (3 * total) // 4)
            def _():
                hook.mid(hins, houts, hscr)

        @pl.when(step == total - 1)
        def _():
            hook.finish(hins, houts, hscr)

    res = pl.pallas_call(
        wrapped, name=name, grid=grid, in_specs=list(in_specs) + [ANY] * h_in,
        out_specs=list(out_specs) + [ANY] * h_out, out_shape=list(out_shape) + list(hook.out_shape),
        scratch_shapes=list(scratch_shapes) + list(hook.scratch), compiler_params=_cparams(n_axes),
        input_output_aliases={n_in + a: n_out + b for a, b in hook.aliases},
    )(*args, *hook.inputs)
    return res[:n_out], res[n_out:]


def _ffn_fwd(h, wpre, wg4, wu4, wd4, wpost, hook=None, target=None):
    t = h.shape[0]
    tm = _row_tile(t)
    nj, fj, _ = wg4.shape
    nblk = tm // BLK if target is not None else 0

    def body(*refs):
        h_ref, wpre_ref, wg_ref, wu_ref, wd_ref, wpost_ref = refs[:6]
        t_refs = refs[6:6 + nblk]
        hout_ref, n_ref, p1_ref, p2_ref, a_ref, f_ref = refs[6 + nblk:12 + nblk]
        acc_ref = refs[-1]
        i = pl.program_id(0)
        j = pl.program_id(1)

        @pl.when(j == 0)
        def _():
            y, _, _ = _rms(h_ref[...], wpre_ref[...])
            n_ref[...] = y.astype(BF16)
            acc_ref[...] = jnp.zeros_like(acc_ref)

        if target is not None:
            sse_ref = refs[12 + nblk]

            @pl.when((i == 0) & (j == 0))
            def _():
                sse_ref[...] = jnp.zeros_like(sse_ref)

        parts = [slice(0, tm)]
        gus = [(_dg(n_ref[rows, :], wg_ref[...], NT), _dg(n_ref[rows, :], wu_ref[...], NT)) for rows in parts]
        for rows, (g, u) in zip(parts, gus):
            sg = _sigmoid(g)
            silu = g * sg
            p1_ref[rows, :] = (u * (sg + silu * (1.0 - sg))).astype(BF16)
            p2_ref[rows, :] = silu.astype(BF16)
            a = (silu * u).astype(BF16)
            a_ref[rows, :] = a
            acc_ref[rows, :] += _dot(a, wd_ref[...])

        @pl.when(j == nj - 1)
        def _():
            f = acc_ref[...]
            f_ref[...] = f
            y, _, _ = _rms(f, wpost_ref[...])
            hout = h_ref[...] + 0.5 * y
            if target is None:
                hout_ref[...] = hout
            else:
                sse = jnp.zeros((1, 1), F32)
                for k in range(nblk):
                    rows = slice(k * BLK, (k + 1) * BLK)
                    err = hout[rows] - t_refs[k][...]
                    if k == 0:
                        err = jnp.where(i > 0, err, 0.0)
                    hout_ref[rows, :] = err * (1.0 / D_MODEL)
                    sse = sse + jnp.sum(jnp.sum(err * err, axis=1, keepdims=True), axis=0, keepdims=True)
                sse_ref[...] += jnp.broadcast_to(sse, sse_ref.shape)

    row = pl.BlockSpec((tm, D_MODEL), lambda i, j: (i, 0))
    vec = pl.BlockSpec((1, D_MODEL), lambda i, j: (0, 0))
    wrow = pl.BlockSpec((None, fj, D_MODEL), lambda i, j: (j, 0, 0))
    act = pl.BlockSpec((None, tm, fj), lambda i, j: (j, i, 0))
    t_specs = [pl.BlockSpec((BLK, D_MODEL), functools.partial(lambda i, j, k: (jnp.maximum(nblk * i + k - 1, 0), 0), k=k))
               for k in range(nblk)]
    loss_spec = [_full((1, 128))] if target is not None else []
    loss_shape = [jax.ShapeDtypeStruct((1, 128), F32)] if target is not None else []
    return _pallas(
        body, name="ffn_fwd", grid=(t // tm, nj),
        in_specs=[row, vec, wrow, wrow, wrow, vec] + t_specs,
        out_specs=[row, row, act, act, act, row] + loss_spec,
        out_shape=[jax.ShapeDtypeStruct((t, D_MODEL), F32), jax.ShapeDtypeStruct((t, D_MODEL), BF16),
                   jax.ShapeDtypeStruct((nj, t, fj), BF16), jax.ShapeDtypeStruct((nj, t, fj), BF16),
                   jax.ShapeDtypeStruct((nj, t, fj), BF16), jax.ShapeDtypeStruct((t, D_MODEL), F32)] + loss_shape,
        scratch_shapes=[pltpu.VMEM((tm, D_MODEL), F32)],
        args=(h, wpre, wg4, wu4, wd4, wpost) + (target,) * nblk, hook=hook)


def _ffn_bwd(dhout, h, f, p14, p24, wpre, wg4, wu4, wd4, wpost, hook=None):
    t = h.shape[0]
    tm = _row_tile(t)
    nj, fj, _ = wg4.shape

    def body(dhout_ref, h_ref, f_ref, p1_ref, p2_ref, wpre_ref, wg_ref, wu_ref, wd_ref, wpost_ref,
             dh_ref, df_ref, dg_ref, du_ref, dwpre_ref, dwpost_ref, dn_ref):
        i = pl.program_id(0)
        j = pl.program_id(1)

        @pl.when((i == 0) & (j == 0))
        def _():
            dwpre_ref[...] = jnp.zeros_like(dwpre_ref)
            dwpost_ref[...] = jnp.zeros_like(dwpost_ref)

        @pl.when(j == 0)
        def _():
            wpost = wpost_ref[...]
            _, fh, r = _rms(f_ref[...], wpost)
            df, dw = _rms_bwd(fh, r, wpost, 0.5 * dhout_ref[...])
            dwpost_ref[...] += dw
            df_ref[...] = df.astype(BF16)
            dn_ref[...] = jnp.zeros_like(dn_ref)

        parts = _row_parts(tm)
        das = [_dg(df_ref[rows, :], wd_ref[...], NT) for rows in parts]
        for rows, da in zip(parts, das):
            dg = (da * p1_ref[rows, :].astype(F32)).astype(BF16)
            du = (da * p2_ref[rows, :].astype(F32)).astype(BF16)
            dg_ref[rows, :] = dg
            du_ref[rows, :] = du
            dn_ref[rows, :] += _dot(dg, wg_ref[...]) + _dot(du, wu_ref[...])

        @pl.when(j == nj - 1)
        def _():
            wpre = wpre_ref[...]
            _, hh, r = _rms(h_ref[...], wpre)
            dx, dw = _rms_bwd(hh, r, wpre, dn_ref[...])
            dwpre_ref[...] += dw
            dh_ref[...] = dhout_ref[...] + dx

    row = pl.BlockSpec((tm, D_MODEL), lambda i, j: (i, 0))
    vec = pl.BlockSpec((1, D_MODEL), lambda i, j: (0, 0))
    wrow = pl.BlockSpec((None, fj, D_MODEL), lambda i, j: (j, 0, 0))
    act = pl.BlockSpec((None, tm, fj), lambda i, j: (j, i, 0))
    actshape = jax.ShapeDtypeStruct((nj, t, fj), BF16)
    return _pallas(
        body, name="ffn_bwd", grid=(t // tm, nj),
        in_specs=[row, row, row, act, act, vec, wrow, wrow, wrow, vec],
        out_specs=[row, row, act, act, vec, vec],
        out_shape=[jax.ShapeDtypeStruct((t, D_MODEL), F32), jax.ShapeDtypeStruct((t, D_MODEL), BF16),
                   actshape, actshape,
                   jax.ShapeDtypeStruct((1, D_MODEL), F32), jax.ShapeDtypeStruct((1, D_MODEL), F32)],
        scratch_shapes=[pltpu.VMEM((tm, D_MODEL), F32)],
        args=(dhout, h, f, p14, p24, wpre, wg4, wu4, wd4, wpost), hook=hook)


def _ffn_wgrad(n, df, dg4, du4, a4, hook=None):
    t = n.shape[0]
    tm = _contract_tile(t)
    ni = t // tm
    nj, _, fj = dg4.shape

    def body(n_ref, df_ref, dg_ref, du_ref, a_ref, dw_ref, acc):
        i = pl.program_id(1)

        @pl.when(i == 0)
        def _():
            acc[...] = jnp.zeros_like(acc)

        nn = n_ref[...]
        acc[0:fj, :] += _dg(dg_ref[...], nn, TN)
        acc[fj:2 * fj, :] += _dg(du_ref[...], nn, TN)
        acc[2 * fj:3 * fj, :] += _dg(a_ref[...], df_ref[...], TN)

        @pl.when(i == ni - 1)
        def _():
            dw_ref[...] = acc[...].astype(BF16)

    row = pl.BlockSpec((tm, D_MODEL), lambda j, i: (i, 0))
    act = pl.BlockSpec((None, tm, fj), lambda j, i: (j, i, 0))
    return _pallas(
        body, name="ffn_wgrad", grid=(nj, ni),
        in_specs=[row, row, act, act, act],
        out_specs=[pl.BlockSpec((None, 3 * fj, D_MODEL), lambda j, i: (j, 0, 0))],
        out_shape=[jax.ShapeDtypeStruct((nj, 3 * fj, D_MODEL), BF16)],
        scratch_shapes=[pltpu.VMEM((3 * fj, D_MODEL), F32)],
        args=(n, df, dg4, du4, a4), hook=hook)


def _rms_cast(h, w):
    t = h.shape[0]
    tm = _row_tile(t)

    def body(h_ref, w_ref, n_ref):
        y, _, _ = _rms(h_ref[...], w_ref[...])
        n_ref[...] = y.astype(BF16)

    row = pl.BlockSpec((tm, D_MODEL), lambda i: (i, 0))
    return pl.pallas_call(
        body, name="rms_cast", grid=(t // tm,), in_specs=[row, _full((1, D_MODEL))], out_specs=row,
        out_shape=jax.ShapeDtypeStruct((t, D_MODEL), BF16), compiler_params=_cparams(1),
    )(h, w)


FWD_RELATION = (None, 0, 1, 2)


def _ffn_fwd_gather(h, n, wbufs, wpost, qc_idx, late):
    t = h.shape[0]
    tm = _row_tile(t)
    ni = t // tm
    nj, fj, _ = wbufs[0].shape
    assert nj == N_CHIPS and ni >= 4
    nw = len(wbufs)
    n_lin, n_lout = len(late.inputs), len(late.out_shape)
    wait_step = ni - 3

    def body(qc_ref, h_ref, n_ref, wpost_ref, *rest):
        wb_in = rest[:nw]
        lins = rest[nw:nw + n_lin]
        o0 = nw + n_lin
        hout_ref, p1_ref, p2_ref, a_ref, f_hbm = rest[o0:o0 + 5]
        wb = rest[o0 + 5:o0 + 5 + nw]
        louts = rest[o0 + 5 + nw:o0 + 5 + nw + n_lout]
        s0 = o0 + 5 + nw + n_lout
        wv, wsem, send, recv, fbuf, fr_sem, fw_sem = rest[s0:s0 + 7]
        lscr = rest[s0 + 7:]
        p = pl.program_id(0)
        i = pl.program_id(1)
        step = p * ni + i
        fslot = step % 3
        nslot = (step + 1) % 3

        def f_tile(tile):
            return f_hbm.at[pl.ds(pl.multiple_of(tile * tm, 8), tm)]

        @pl.when(step > 1)
        def _():
            pltpu.make_async_copy(fbuf.at[nslot], f_tile(i), fw_sem.at[nslot]).wait()

        nxt = step + 1

        @pl.when((nxt < N_CHIPS * ni) & (nxt >= ni))
        def _():
            pltpu.make_async_copy(f_tile(nxt % ni), fbuf.at[nslot], fr_sem.at[nslot]).start()

        @pl.when(p > 0)
        def _():
            pltpu.make_async_copy(f_tile(i), fbuf.at[fslot], fr_sem.at[fslot]).wait()
        x, y, c, chips = _place()
        q = 2 * x + y
        sibling = (x, y, 1 - c)
        mine, other = _half(fj, c), _half(fj, 1 - c)

        def load(chunk, slot, src):
            return [pltpu.make_async_copy(src[t].at[chunk], wv.at[slot, t], wsem.at[slot, t]) for t in range(nw)]

        @pl.when((p == 0) & (i == 0))
        def _():
            for j, (cx, cy) in enumerate(chips):
                for t in range(nw):
                    _remote(send.at[t, j], recv.at[t, j], wb_in[t].at[q, mine], wb[t].at[q, mine], (cx, cy, c)).start()
            for cp in load(q, 0, wb_in):
                cp.start()
            for cp in load(q, 0, wb_in):
                cp.wait()

        @pl.when((p == 1) & (i == 0))
        def _():
            late.start(lins, louts, lscr)

        for pp in range(1, N_CHIPS):
            j = FWD_RELATION[pp]
            cx, cy = chips[j]
            chunk = 2 * cx + cy

            @pl.when((p == pp - 1) & (i == wait_step))
            def _(j=j, cx=cx, cy=cy, chunk=chunk, pp=pp):
                for t in range(nw):
                    got = wb[t].at[chunk, mine]
                    _remote(send.at[t, j], recv.at[t, j], got, got, (cx, cy, c)).wait_recv()
                    _remote(send.at[t, 3 + j], recv.at[t, 3 + j], got, got, sibling).start()
                for t in range(nw):
                    rest_half = wb[t].at[chunk, other]
                    _remote(send.at[t, 3 + j], recv.at[t, 3 + j], rest_half, rest_half, sibling).wait_recv()
                for cp in load(chunk, pp % 2, wb):
                    cp.start()

            @pl.when((p == pp) & (i == 0))
            def _(chunk=chunk, pp=pp):
                for cp in load(chunk, pp % 2, wb):
                    cp.wait()

        @pl.when((p == N_CHIPS - 1) & (i == ni // 2))
        def _():
            late.mid(lins, louts, lscr)

        slot = p % 2
        nn = n_ref[...]
        g = _dg(nn, wv[slot, 0], NT)
        u = _dg(nn, wv[slot, 1], NT)
        sg = _sigmoid(g)
        silu = g * sg
        p1_ref[...] = (u * (sg + silu * (1.0 - sg))).astype(BF16)
        p2_ref[...] = silu.astype(BF16)
        a = (silu * u).astype(BF16)
        a_ref[...] = a
        part = _dot(a, wv[slot, 2])

        @pl.when(p == 0)
        def _():
            fbuf[fslot] = part

        @pl.when(p > 0)
        def _():
            fbuf[fslot] = fbuf[fslot] + part

        pltpu.make_async_copy(fbuf.at[fslot], f_tile(i), fw_sem.at[fslot]).start()

        @pl.when(p == N_CHIPS - 1)
        def _():
            yv, _, _ = _rms(fbuf[fslot], wpost_ref[...])
            hout_ref[...] = h_ref[...] + 0.5 * yv

        @pl.when((p == N_CHIPS - 1) & (i == ni - 1))
        def _():
            pslot = (step + 2) % 3
            pltpu.make_async_copy(fbuf.at[pslot], f_tile(i), fw_sem.at[pslot]).wait()
            pltpu.make_async_copy(fbuf.at[fslot], f_tile(i), fw_sem.at[fslot]).wait()
            for t in range(nw):
                for j, (cx, cy) in enumerate(chips):
                    sent = wb[t].at[2 * cx + cy, mine]
                    _remote(send.at[t, j], recv.at[t, j], sent, sent, (cx, cy, c)).wait_send()
                    _remote(send.at[t, 3 + j], recv.at[t, 3 + j], sent, sent, sibling).wait_send()
            late.finish(lins, louts, lscr)

    def last_pass_rows(p, i, qc_ref):
        return (jnp.where(p == N_CHIPS - 1, i, 0), 0)

    def chunk_rows(p, i, qc_ref):
        order = ((p & 1) << 1) | (p >> 1)
        return (jnp.bitwise_xor(qc_ref[0], order), i, 0)

    row = pl.BlockSpec((tm, D_MODEL), lambda p, i, qc_ref: (i, 0))
    last_row = pl.BlockSpec((tm, D_MODEL), last_pass_rows)
    act = pl.BlockSpec((None, tm, fj), chunk_rows)
    act_shape = jax.ShapeDtypeStruct((nj, t, fj), BF16)
    res = pl.pallas_call(
        body, name="ffn_fwd_gather",
        grid_spec=pltpu.PrefetchScalarGridSpec(
            num_scalar_prefetch=1, grid=(N_CHIPS, ni),
            in_specs=[last_row, row, pl.BlockSpec((1, D_MODEL), lambda p, i, qc_ref: (0, 0))]
            + [ANY] * (nw + n_lin),
            out_specs=[last_row, act, act, act, ANY] + [ANY] * (nw + n_lout),
            scratch_shapes=[pltpu.VMEM((2, nw, fj, D_MODEL), BF16), pltpu.SemaphoreType.DMA((2, nw)),
                            pltpu.SemaphoreType.DMA((nw, 6)), pltpu.SemaphoreType.DMA((nw, 6)),
                            pltpu.VMEM((3, tm, D_MODEL), F32), pltpu.SemaphoreType.DMA((3,)),
                            pltpu.SemaphoreType.DMA((3,))] + list(late.scratch)),
        out_shape=[jax.ShapeDtypeStruct((t, D_MODEL), F32), act_shape, act_shape, act_shape,
                   jax.ShapeDtypeStruct((t, D_MODEL), F32)]
        + [jax.ShapeDtypeStruct(b.shape, b.dtype) for b in wbufs] + list(late.out_shape),
        input_output_aliases={**{4 + t: 5 + t for t in range(nw)},
                              **{4 + nw + a: 5 + nw + b for a, b in late.aliases}},
        compiler_params=_cparams(2),
    )(qc_idx, h, n, wpost, *wbufs, *late.inputs)
    return res[:5], res[5:5 + nw], res[5 + nw:]


PASS_RELATION = (2, 0, 1)


def _ffn_wgrad_reduce(n, df, dg4, du4, a4, qc_idx, hook):
    t = n.shape[0]
    tm = _contract_tile(t)
    ni = t // tm
    nj, _, fj = dg4.shape
    assert nj == N_CHIPS
    hrows = 3 * fj // 2
    n_hin, n_hout = len(hook.inputs), len(hook.out_shape)

    def body(qc_ref, n_ref, df_ref, dg_ref, du_ref, a_ref, *rest):
        hins = rest[:n_hin]
        own_ref, others_ref = rest[n_hin:n_hin + 2]
        houts = rest[n_hin + 2:n_hin + 2 + n_hout]
        s0 = n_hin + 2 + n_hout
        acc, stage, land, sumbuf, px_send, px_recv, cs_send, cs_recv, own_sem = rest[s0:s0 + 9]
        hscr = rest[s0 + 9:]
        k_pass = pl.program_id(0)
        i = pl.program_id(1)
        x, y, c, chips = _place()
        mine = pl.ds(pl.multiple_of(c * hrows, 8), hrows)
        other = pl.ds(pl.multiple_of((1 - c) * hrows, 8), hrows)

        def to_owner(k):
            j = PASS_RELATION[k]
            return _remote(cs_send.at[j], cs_recv.at[j], sumbuf.at[k % 2], others_ref.at[j], (*chips[j], c))

        @pl.when((k_pass == 0) & (i == 0))
        def _():
            hook.start(hins, houts, hscr)

        @pl.when(i == 0)
        def _():
            acc[...] = jnp.zeros_like(acc)

        nn = n_ref[...]
        acc[0:fj, :] += _dg(dg_ref[...], nn, TN)
        acc[fj:2 * fj, :] += _dg(du_ref[...], nn, TN)
        acc[2 * fj:3 * fj, :] += _dg(a_ref[...], df_ref[...], TN)

        for k in range(N_CHIPS):
            @pl.when((k_pass == k) & (i == ni - 1))
            def _(k=k):
                slot = k % 2
                stage[...] = acc[other, :].astype(BF16)
                swap = _remote(px_send.at[k], px_recv.at[k], stage, land.at[slot], (x, y, 1 - c))
                swap.start()
                swap.wait_recv()
                pair = acc[mine, :] + land[slot].astype(F32)
                if k >= 2:
                    to_owner(k - 2).wait_send()
                sumbuf[slot] = pair.astype(BF16)
                swap.wait_send()
                if k < N_CHIPS - 1:
                    to_owner(k).start()
                else:
                    keep = pltpu.make_async_copy(sumbuf.at[slot], own_ref, own_sem)
                    keep.start()
                    for j in range(N_CHIPS - 1):
                        _remote(cs_send.at[j], cs_recv.at[j], sumbuf.at[0], others_ref.at[j], (*chips[j], c)).wait_recv()
                    to_owner(k - 1).wait_send()
                    keep.wait()
                    hook.finish(hins, houts, hscr)

    def chunk(k_pass, i, qc_ref):
        return (jnp.bitwise_xor(qc_ref[0], N_CHIPS - 1 - k_pass), i, 0)

    row = pl.BlockSpec((tm, D_MODEL), lambda k_pass, i, qc_ref: (i, 0))
    act = pl.BlockSpec((None, tm, fj), chunk)
    res = pl.pallas_call(
        body, name="ffn_wgrad_reduce",
        grid_spec=pltpu.PrefetchScalarGridSpec(
            num_scalar_prefetch=1, grid=(N_CHIPS, ni),
            in_specs=[row, row, act, act, act] + [ANY] * n_hin,
            out_specs=[ANY, ANY] + [ANY] * n_hout,
            scratch_shapes=[pltpu.VMEM((3 * fj, D_MODEL), F32), pltpu.VMEM((hrows, D_MODEL), BF16),
                            pltpu.VMEM((2, hrows, D_MODEL), BF16), pltpu.VMEM((2, hrows, D_MODEL), BF16),
                            pltpu.SemaphoreType.DMA((N_CHIPS,)), pltpu.SemaphoreType.DMA((N_CHIPS,)),
                            pltpu.SemaphoreType.DMA((N_CHIPS - 1,)), pltpu.SemaphoreType.DMA((N_CHIPS - 1,)),
                            pltpu.SemaphoreType.DMA] + list(hook.scratch)),
        out_shape=[jax.ShapeDtypeStruct((hrows, D_MODEL), BF16),
                   jax.ShapeDtypeStruct((N_CHIPS - 1, hrows, D_MODEL), BF16)] + list(hook.out_shape),
        compiler_params=_cparams(2),
    )(qc_idx, n, df, dg4, du4, a4, *hook.inputs)
    return res[0], res[1], res[2:]


def _xty(x, y):
    t, k = x.shape
    n = y.shape[1]
    tm = _contract_tile(t)
    tn = n if n <= 1024 else (896 if n % 896 == 0 else 128)

    def body(x_ref, y_ref, o_ref):
        @pl.when(pl.program_id(1) == 0)
        def _():
            o_ref[...] = jnp.zeros_like(o_ref)

        o_ref[...] += _dg(x_ref[...], y_ref[...], TN)

    return pl.pallas_call(
        body, name="xty", grid=(n // tn, t // tm),
        in_specs=[pl.BlockSpec((tm, k), lambda j, i: (i, 0)), pl.BlockSpec((tm, tn), lambda j, i: (i, j))],
        out_specs=pl.BlockSpec((k, tn), lambda j, i: (0, j)),
        out_shape=jax.ShapeDtypeStruct((k, n), F32),
        compiler_params=_cparams(2),
    )(x, y)


def _rope_tables(t):
    pos = (jnp.arange(t, dtype=jnp.int32) - PAD).astype(F32)
    inv_freq = 1.0 / (ROPE_THETA ** (jnp.arange(0, SWA_HD, 2, dtype=F32) / SWA_HD))
    ang = pos[:, None] * inv_freq[None, :]
    cos = jnp.cos(ang)
    sin = jnp.sin(ang)
    return jnp.concatenate([cos, cos, cos, cos], axis=1), jnp.concatenate([-sin, sin, -sin, sin], axis=1)


def _rot_half(x, first_half):
    return jnp.where(first_half, pltpu.roll(x, 96, 1), pltpu.roll(x, 32, 1))


def _first_half_mask(rows):
    lane = lax.broadcasted_iota(jnp.int32, (rows, 128), 1)
    return (lane % 64) < 32


def _log_sigmoid(z):
    return jnp.minimum(z, 0.0) - jnp.log(1.0 + jnp.exp(-jnp.abs(z)))


def _mix_proj(h1, wmixpre, winp, wa2p, bap, cos, sin):
    t = h1.shape[0]
    tm = _row_tile(t)

    def body(h_ref, w_ref, win_ref, wa2_ref, ba_ref, cos_ref, sin_ref,
             n_ref, gq_ref, gk_ref, gv_ref, gg_ref, ga_ref, la_ref, sq_ref, sk_ref, sv_ref):
        y, _, _ = _rms(h_ref[...], w_ref[...])
        n = y.astype(BF16)
        n_ref[...] = n
        proj = _dot(n, win_ref[...])
        gq_ref[...] = proj[:, P_GQ:P_GK]
        gk_ref[...] = proj[:, P_GK:P_GV]
        gv_ref[...] = proj[:, P_GV:P_GG]
        gg_ref[...] = proj[:, P_GG:P_GA]
        ga = proj[:, P_GA:P_SQ]
        ga_ref[...] = ga
        z = _dot(ga.astype(BF16), wa2_ref[...]) + ba_ref[...]
        la_ref[...] = _log_sigmoid(z) * (1.0 / GLA_TAU)
        c = cos_ref[...]
        s = sin_ref[...]
        fh = _first_half_mask(tm)
        for k in range(4):
            x = proj[:, P_SQ + 128 * k:P_SQ + 128 * (k + 1)]
            sq_ref[:, 128 * k:128 * (k + 1)] = (x * c + _rot_half(x, fh) * s).astype(BF16)
        for k in range(2):
            x = proj[:, P_SK + 128 * k:P_SK + 128 * (k + 1)]
            sk_ref[:, 128 * k:128 * (k + 1)] = (x * c + _rot_half(x, fh) * s).astype(BF16)
        sv_ref[...] = proj[:, P_SV:P_END].astype(BF16)

    def row(w):
        return pl.BlockSpec((tm, w), lambda i: (i, 0))

    def rshape(w, dt):
        return jax.ShapeDtypeStruct((t, w), dt)

    return pl.pallas_call(
        body, name="mix_proj", grid=(t // tm,),
        in_specs=[row(D_MODEL), _full((1, D_MODEL)), _full((D_MODEL, P_END)), _full((128, GLA_KW)),
                  _full((1, GLA_KW)), row(128), row(128)],
        out_specs=[row(D_MODEL), row(256), row(256), row(512), row(512), row(128), row(256), row(512), row(256),
                   row(256)],
        out_shape=[rshape(D_MODEL, BF16), rshape(256, F32), rshape(256, F32), rshape(512, F32), rshape(512, F32),
                   rshape(128, F32), rshape(256, F32), rshape(512, BF16), rshape(256, BF16), rshape(256, BF16)],
        compiler_params=_cparams(1),
    )(h1, wmixpre, winp, wa2p, bap, cos, sin)


def _scan_rows(x, reverse=False):
    n = x.shape[0]
    row = lax.broadcasted_iota(jnp.int32, x.shape, 0)
    s = 1
    while s < n:
        if reverse:
            x = x + jnp.where(row < n - s, pltpu.roll(x, n - s, 0), 0.0)
        else:
            x = x + jnp.where(row >= s, pltpu.roll(x, s, 0), 0.0)
        s *= 2
    return x


def _gla_cumsum(la, tril_f):
    b = _scan_rows(la)
    row = lax.broadcasted_iota(jnp.int32, b.shape, 0)
    bm = jnp.sum(jnp.where(row == GLA_CHUNK // 2 - 1, b, 0.0), axis=0, keepdims=True)
    bl = jnp.sum(jnp.where(row == GLA_CHUNK - 1, b, 0.0), axis=0, keepdims=True)
    return b, bm, bl


def _gla_decays(la, tril_f):
    b, bm, bl = _gla_cumsum(la, tril_f)
    return jnp.exp(b - bm), jnp.exp(bm - b), jnp.exp(b), jnp.exp(bl - b), jnp.exp(bl)


def _gla_masks():
    c = GLA_CHUNK
    r = lax.broadcasted_iota(jnp.int32, (c, c), 0)
    col = lax.broadcasted_iota(jnp.int32, (c, c), 1)
    r4 = lax.broadcasted_iota(jnp.int32, (GLA_HEADS * c, c), 0) % c
    c4 = lax.broadcasted_iota(jnp.int32, (GLA_HEADS * c, c), 1)
    klane = lax.broadcasted_iota(jnp.int32, (c, GLA_KW), 1) // GLA_DK
    vlane = lax.broadcasted_iota(jnp.int32, (c, GLA_W), 1) // GLA_DV
    srow = lax.broadcasted_iota(jnp.int32, (GLA_W, GLA_KW), 0) // GLA_DV
    scol = lax.broadcasted_iota(jnp.int32, (GLA_W, GLA_KW), 1) // GLA_DK
    return dict(tril_f=(r >= col).astype(F32), triu_f=(r <= col).astype(F32), tril4=r4 >= c4,
                khead=[klane == h for h in range(GLA_HEADS)], vhead=[vlane == h for h in range(GLA_HEADS)],
                diag=srow == scol)


def _stack_heads(x, head_masks):
    return jnp.concatenate([jnp.where(m, x, 0.0) for m in head_masks], axis=0)


def _gla_fwd(gq, gk, gv, la):
    t = gq.shape[0]
    rg = _seq_tile(t)
    nb = t // rg
    ncb = rg // GLA_CHUNK
    c = GLA_CHUNK

    def body(q_ref, k_ref, v_ref, la_ref, o_ref, ss_ref, st_ref):
        @pl.when(pl.program_id(0) == 0)
        def _():
            st_ref[...] = jnp.zeros_like(st_ref)

        mk = _gla_masks()
        st = st_ref[...]
        for ch in range(ncb):
            rows = slice(ch * c, (ch + 1) * c)
            eq, ek, eb, ekl, ebl = _gla_decays(la_ref[rows, :], mk["tril_f"])
            qs = q_ref[rows, :] * (GLA_DK ** -0.5)
            k = k_ref[rows, :]
            v = v_ref[rows, :].astype(BF16)
            ss_ref[ch] = st
            q4 = _stack_heads(qs * eq, mk["khead"]).astype(BF16)
            a4 = jnp.where(mk["tril4"], _dg(q4, (k * ek).astype(BF16), NT), 0.0).astype(BF16)
            r4 = _dot(a4, v)
            intra = jnp.concatenate([r4[h * c:(h + 1) * c, GLA_DV * h:GLA_DV * (h + 1)] for h in range(GLA_HEADS)],
                                    axis=1)
            o_ref[rows, :] = intra + _dg((qs * eb).astype(BF16), st.astype(BF16), NT)
            st = st * ebl + jnp.where(mk["diag"], _dg(v, (k * ekl).astype(BF16), TN), 0.0)
        st_ref[...] = st

    def row(w):
        return pl.BlockSpec((rg, w), lambda i: (i, 0))

    return pl.pallas_call(
        body, name="gla_fwd", grid=(nb,),
        in_specs=[row(256), row(256), row(512), row(256)],
        out_specs=[row(512), pl.BlockSpec((ncb, GLA_W, GLA_KW), lambda i: (i, 0, 0))],
        out_shape=[jax.ShapeDtypeStruct((t, GLA_W), F32), jax.ShapeDtypeStruct((nb * ncb, GLA_W, GLA_KW), F32)],
        scratch_shapes=[pltpu.VMEM((GLA_W, GLA_KW), F32)],
        compiler_params=_cparams(1),
    )(gq, gk, gv, la)


def _gla_bwd(gq, gk, gv, la, ss, do):
    t = gq.shape[0]
    rg = _seq_tile(t)
    nb = t // rg
    ncb = rg // GLA_CHUNK
    c = GLA_CHUNK

    def body(q_ref, k_ref, v_ref, la_ref, ss_ref, do_ref, dq_ref, dk_ref, dv_ref, dla_ref, dst_ref):
        @pl.when(pl.program_id(0) == 0)
        def _():
            dst_ref[...] = jnp.zeros_like(dst_ref)

        mk = _gla_masks()
        last_row = lax.broadcasted_iota(jnp.int32, (c, GLA_KW), 0) == c - 1
        scale = GLA_DK ** -0.5
        dstn = dst_ref[...]
        for ch in reversed(range(ncb)):
            rows = slice(ch * c, (ch + 1) * c)
            eq, ek, eb, ekl, ebl = _gla_decays(la_ref[rows, :], mk["tril_f"])
            qs = q_ref[rows, :] * scale
            k = k_ref[rows, :]
            qt, kt, qh, kh = qs * eq, k * ek, qs * eb, k * ekl
            ktb, khb, qhb = kt.astype(BF16), kh.astype(BF16), qh.astype(BF16)
            v = v_ref[rows, :].astype(BF16)
            do_f = do_ref[rows, :]
            dob = do_f.astype(BF16)
            st = ss_ref[ch]
            stb = st.astype(BF16)
            dstb = dstn.astype(BF16)
            q4 = _stack_heads(qt, mk["khead"]).astype(BF16)
            do4 = _stack_heads(do_f, mk["vhead"]).astype(BF16)
            a4 = jnp.where(mk["tril4"], _dg(q4, ktb, NT), 0.0).astype(BF16)
            da4 = jnp.where(mk["tril4"], _dg(do4, v, NT), 0.0).astype(BF16)
            dv_ref[rows, :] = _dg(a4, do4, TN) + _dg(khb, dstb, NT)
            dq4 = _dot(da4, ktb)
            dqt = jnp.zeros((c, GLA_KW), F32)
            for h in range(GLA_HEADS):
                dqt = dqt + jnp.where(mk["khead"][h], dq4[h * c:(h + 1) * c], 0.0)
            dkt = _dg(da4, q4, TN)
            dqh = _dot(dob, stb)
            dkh = _dot(v, dstb)
            dbl = jnp.sum(dstn * st, axis=0, keepdims=True)
            dstn = dstn * ebl + jnp.where(mk["diag"], _dg(dob, qhb, TN), 0.0)
            dq_ref[rows, :] = scale * (dqt * eq + dqh * eb)
            dk_ref[rows, :] = dkt * ek + dkh * ekl
            dkk = dkh * kh
            db = dqt * qt - dkt * kt + dqh * qh - dkk
            db = db + jnp.where(last_row, jnp.sum(dkk, axis=0, keepdims=True) + ebl * dbl, 0.0)
            dla_ref[rows, :] = _scan_rows(db, reverse=True)
        dst_ref[...] = dstn

    def row(w):
        return pl.BlockSpec((rg, w), lambda i: (nb - 1 - i, 0))

    def rshape(w):
        return jax.ShapeDtypeStruct((t, w), F32)

    return pl.pallas_call(
        body, name="gla_bwd", grid=(nb,),
        in_specs=[row(256), row(256), row(512), row(256),
                  pl.BlockSpec((ncb, GLA_W, GLA_KW), lambda i: (nb - 1 - i, 0, 0)), row(512)],
        out_specs=[row(256), row(256), row(512), row(256)],
        out_shape=[rshape(256), rshape(256), rshape(512), rshape(256)],
        scratch_shapes=[pltpu.VMEM((GLA_W, GLA_KW), F32)],
        compiler_params=_cparams(1),
    )(gq, gk, gv, la, ss, do)


SWA_G = SWA_QH // SWA_KVH


def _swa_bias():
    n = jnp.arange(3, dtype=jnp.int32)[:, None, None]
    r = (jnp.arange(SWA_G * BLK, dtype=jnp.int32) % BLK)[None, :, None]
    c = jnp.arange(3 * BLK, dtype=jnp.int32)[None, None, :]
    seg = c // BLK
    cc = c % BLK
    qpos = n * BLK + r - PAD
    kpos = jnp.where(seg == 0, (n - 1) * BLK, jnp.where(seg == 1, n * BLK, 0)) + cc - PAD
    band = (seg < 2) & (kpos >= N_META) & (kpos <= qpos) & (qpos - kpos < WINDOW)
    meta = (seg == 2) & (kpos >= 0) & (kpos < N_META) & (kpos <= qpos)
    return jnp.where(band | meta, 0.0, NEG_INF).astype(F32)


def _swa_stack(ref, rows, kh, lo, dtype):
    parts = []
    for g in range(2):
        pair = ref[rows, 128 * (2 * kh + g):128 * (2 * kh + g + 1)]
        zero = jnp.zeros_like(pair)
        parts += [jnp.where(lo, pair, zero), jnp.where(lo, zero, pair)]
    return jnp.concatenate(parts, axis=0).astype(dtype)


def _swa_unstack(x4, lo):
    return [jnp.where(lo, x4[2 * g * BLK:(2 * g + 1) * BLK], x4[(2 * g + 1) * BLK:(2 * g + 2) * BLK])
            for g in range(2)]


def _swa_sink_col(sink_ref, kh):
    blk = lax.broadcasted_iota(jnp.int32, (SWA_G * BLK, 1), 0) // BLK
    col = jnp.full((SWA_G * BLK, 1), sink_ref[SWA_G * kh + SWA_G - 1], F32)
    for e in reversed(range(SWA_G - 1)):
        col = jnp.where(blk == e, sink_ref[SWA_G * kh + e], col)
    return col


def _swa_probs(q4, kall, bias, sink):
    s = _dg(q4, kall, NT) * (SWA_HD ** -0.5) + bias
    m = jnp.maximum(jnp.max(s, axis=-1, keepdims=True), sink)
    p = jnp.exp(s - m)
    es = jnp.exp(sink - m)
    inv = 1.0 / (jnp.sum(p, axis=-1, keepdims=True) + es)
    return p * inv, es * inv


def _swa_keys(prev_ref, cur_ref, first_ref, b, ls):
    before = prev_ref[:, ls] if b == 0 else cur_ref[(b - 1) * BLK:b * BLK, ls]
    return jnp.concatenate([before, cur_ref[b * BLK:(b + 1) * BLK, ls], first_ref[:, ls]], axis=0)


def _swa_specs(rs, ns):
    bps = rs // BLK
    cur = lambda w: pl.BlockSpec((rs, w), lambda i: (jnp.minimum(i, ns - 1), 0))
    prev = lambda w: pl.BlockSpec((BLK, w), lambda i: (jnp.maximum(jnp.minimum(i, ns - 1) * bps - 1, 0), 0))
    first = lambda w: pl.BlockSpec((BLK, w), lambda i: (0, 0))
    return cur, prev, first


def _swa_fwd(sinks, sq, sk, sv):
    t = sq.shape[0]
    rs = _seq_tile(t)
    bps, ns = rs // BLK, t // rs

    def body(sink_ref, bias_ref, q_ref, kp_ref, kc_ref, km_ref, vp_ref, vc_ref, vm_ref, o_ref):
        i = pl.program_id(0)
        lo = lax.broadcasted_iota(jnp.int32, (BLK, 128), 1) < 64
        sink_cols = [_swa_sink_col(sink_ref, kh) for kh in range(SWA_KVH)]
        for b in range(bps):
            rows = slice(b * BLK, (b + 1) * BLK)
            bias = bias_ref[jnp.minimum(i * bps + b, 2)]
            for kh in range(SWA_KVH):
                ls = slice(128 * kh, 128 * (kh + 1))
                kall = _swa_keys(kp_ref, kc_ref, km_ref, b, ls)
                vall = _swa_keys(vp_ref, vc_ref, vm_ref, b, ls)
                p, _ = _swa_probs(_swa_stack(q_ref, rows, kh, lo, BF16), kall, bias, sink_cols[kh])
                for g, pair in enumerate(_swa_unstack(_dot(p.astype(BF16), vall), lo)):
                    o_ref[rows, 128 * (2 * kh + g):128 * (2 * kh + g + 1)] = pair

    cur, prev, first = _swa_specs(rs, ns)
    bias = _swa_bias()
    return pl.pallas_call(
        body, name="swa_fwd", grid=(ns,),
        in_specs=[pl.BlockSpec(memory_space=pltpu.SMEM), _full(bias.shape), cur(512), prev(256), cur(256), first(256),
                  prev(256), cur(256), first(256)],
        out_specs=cur(512),
        out_shape=jax.ShapeDtypeStruct((t, SWA_W), F32),
        compiler_params=_cparams(1),
    )(sinks, bias, sq, sk, sk, sk, sv, sv, sv)


def _swa_bwd(sinks, sq, sk, sv, o, do, hook=None):
    t = sq.shape[0]
    rs = _seq_tile(t)
    bps, ns = rs // BLK, t // rs

    def body(sink_ref, bias_ref, q_ref, kp_ref, kc_ref, km_ref, vp_ref, vc_ref, vm_ref, o_ref, do_ref,
             dq_ref, dk_ref, dv_ref, dkm_ref, dvm_ref, dsink_ref, pk_ref, pv_ref):
        i = pl.program_id(0)

        @pl.when(i == 0)
        def _():
            pk_ref[...] = jnp.zeros_like(pk_ref)
            pv_ref[...] = jnp.zeros_like(pv_ref)
            dkm_ref[...] = jnp.zeros_like(dkm_ref)
            dvm_ref[...] = jnp.zeros_like(dvm_ref)
            dsink_ref[...] = jnp.zeros_like(dsink_ref)

        @pl.when(i == ns)
        def _():
            dk_ref[...] = pk_ref[...]
            dv_ref[...] = pv_ref[...]

        @pl.when(i < ns)
        def _():
            lo = lax.broadcasted_iota(jnp.int32, (BLK, 128), 1) < 64
            scale = SWA_HD ** -0.5
            sink_cols = [_swa_sink_col(sink_ref, kh) for kh in range(SWA_KVH)]
            parts_k = [[None] * SWA_KVH for _ in range(bps)]
            parts_v = [[None] * SWA_KVH for _ in range(bps)]
            dsinks = [jnp.zeros((1, 1), F32) for _ in range(SWA_QH)]
            for b in range(bps):
                rows = slice(b * BLK, (b + 1) * BLK)
                bias = bias_ref[jnp.minimum(i * bps + b, 2)]
                for kh in range(SWA_KVH):
                    ls = slice(128 * kh, 128 * (kh + 1))
                    kall = _swa_keys(kp_ref, kc_ref, km_ref, b, ls)
                    vall = _swa_keys(vp_ref, vc_ref, vm_ref, b, ls)
                    q4 = _swa_stack(q_ref, rows, kh, lo, BF16)
                    do4 = _swa_stack(do_ref, rows, kh, lo, F32)
                    p, psink = _swa_probs(q4, kall, bias, sink_cols[kh])
                    delta = jnp.sum(do4 * _swa_stack(o_ref, rows, kh, lo, F32), axis=-1, keepdims=True)
                    do4b = do4.astype(BF16)
                    ds = (p * (_dg(do4b, vall, NT) - delta) * scale).astype(BF16)
                    for g, pair in enumerate(_swa_unstack(_dot(ds, kall), lo)):
                        dq_ref[rows, 128 * (2 * kh + g):128 * (2 * kh + g + 1)] = pair
                    parts_k[b][kh] = _dg(ds, q4, TN)
                    parts_v[b][kh] = _dg(p.astype(BF16), do4b, TN)
                    dsk = psink * delta
                    for e in range(SWA_G):
                        h = SWA_G * kh + e
                        dsinks[h] = dsinks[h] - jnp.sum(dsk[e * BLK:(e + 1) * BLK], axis=0, keepdims=True)
            last = slice(rs - BLK, rs)
            for parts, out_ref, pend_ref, meta_ref in ((parts_k, dk_ref, pk_ref, dkm_ref),
                                                       (parts_v, dv_ref, pv_ref, dvm_ref)):
                for kh in range(SWA_KVH):
                    ls = slice(128 * kh, 128 * (kh + 1))
                    if bps > 1:
                        out_ref[0:rs - BLK, ls] = pend_ref[0:rs - BLK, ls]
                    out_ref[last, ls] = pend_ref[last, ls] + parts[0][kh][0:BLK]
                    meta = parts[0][kh][2 * BLK:3 * BLK]
                    for b in range(bps):
                        own = parts[b][kh][BLK:2 * BLK]
                        if b + 1 < bps:
                            own = own + parts[b + 1][kh][0:BLK]
                            meta = meta + parts[b + 1][kh][2 * BLK:3 * BLK]
                        pend_ref[b * BLK:(b + 1) * BLK, ls] = own
                    meta_ref[:, ls] += meta
            for h in range(SWA_QH):
                dsink_ref[h:h + 1, :] += jnp.broadcast_to(dsinks[h], (1, 128))

    cur, prev, first = _swa_specs(rs, ns)
    late = lambda w: pl.BlockSpec((rs, w), lambda i: (jnp.maximum(i - 1, 0), 0))
    bias = _swa_bias()
    return _pallas(
        body, name="swa_bwd", grid=(ns + 1,),
        in_specs=[pl.BlockSpec(memory_space=pltpu.SMEM), _full(bias.shape), cur(512), prev(256), cur(256), first(256),
                  prev(256), cur(256), first(256), cur(512), cur(512)],
        out_specs=[cur(512), late(256), late(256), first(256), first(256), _full((SWA_QH, 128))],
        out_shape=[jax.ShapeDtypeStruct((t, SWA_W), F32), jax.ShapeDtypeStruct((t, 256), F32),
                   jax.ShapeDtypeStruct((t, 256), F32), jax.ShapeDtypeStruct((BLK, 256), F32),
                   jax.ShapeDtypeStruct((BLK, 256), F32), jax.ShapeDtypeStruct((SWA_QH, 128), F32)],
        scratch_shapes=[pltpu.VMEM((rs, 256), F32), pltpu.VMEM((rs, 256), F32)],
        args=(sinks, bias, sq, sk, sk, sk, sv, sv, sv, o, do), hook=hook)


def _mix_out(h1, ogla, gg, oswa, wgn, wsn, wout, wpost):
    t = h1.shape[0]
    tm = _row_tile(t)

    def body(h_ref, og_ref, gg_ref, os_ref, wgn_ref, wsn_ref, wout_ref, wpost_ref, h2_ref, cat_ref, m_ref):
        parts = []
        for h in range(GLA_HEADS):
            ls = slice(GLA_DV * h, GLA_DV * (h + 1))
            y, _, _ = _rms(og_ref[:, ls], wgn_ref[...])
            g = gg_ref[:, ls]
            parts.append(y * (g * _sigmoid(g)))
        ys, _, _ = _rms(os_ref[...], wsn_ref[...])
        cat = jnp.concatenate(parts + [ys], axis=1).astype(BF16)
        cat_ref[...] = cat
        m = _dot(cat, wout_ref[...])
        m_ref[...] = m
        y, _, _ = _rms(m, wpost_ref[...])
        h2_ref[...] = h_ref[...] + y

    def row(w):
        return pl.BlockSpec((tm, w), lambda i: (i, 0))

    return pl.pallas_call(
        body, name="mix_out", grid=(t // tm,),
        in_specs=[row(D_MODEL), row(512), row(512), row(512), _full((1, GLA_DV)), _full((1, SWA_W)),
                  _full((D_MODEL, D_MODEL)), _full((1, D_MODEL))],
        out_specs=[row(D_MODEL), row(D_MODEL), row(D_MODEL)],
        out_shape=[jax.ShapeDtypeStruct((t, D_MODEL), F32), jax.ShapeDtypeStruct((t, D_MODEL), BF16),
                   jax.ShapeDtypeStruct((t, D_MODEL), F32)],
        compiler_params=_cparams(1),
    )(h1, ogla, gg, oswa, wgn, wsn, wout, wpost)


def _mix_out_bwd(dh2, m, ogla, gg, oswa, wgn, wsn, wout, wpost, hook=None):
    t = dh2.shape[0]
    tm = _row_tile(t)

    def body(dh_ref, m_ref, og_ref, gg_ref, os_ref, wgn_ref, wsn_ref, wout_ref, wpost_ref,
             dog_ref, dgg_ref, dos_ref, dm_ref, dwpost_ref, dwgn_ref, dwsn_ref):
        @pl.when(pl.program_id(0) == 0)
        def _():
            dwpost_ref[...] = jnp.zeros_like(dwpost_ref)
            dwgn_ref[...] = jnp.zeros_like(dwgn_ref)
            dwsn_ref[...] = jnp.zeros_like(dwsn_ref)

        wpost = wpost_ref[...]
        _, mh, r = _rms(m_ref[...], wpost)
        dm, dw = _rms_bwd(mh, r, wpost, dh_ref[...])
        dwpost_ref[...] += dw
        dmb = dm.astype(BF16)
        dm_ref[...] = dmb
        dcat = _dg(dmb, wout_ref[...], NT)
        wgn = wgn_ref[...]
        for h in range(GLA_HEADS):
            ls = slice(GLA_DV * h, GLA_DV * (h + 1))
            dog = dcat[:, ls]
            g = gg_ref[:, ls]
            sg = _sigmoid(g)
            y, xh, r = _rms(og_ref[:, ls], wgn)
            dgg_ref[:, ls] = dog * y * (sg * (1.0 + g * (1.0 - sg)))
            dx, dw = _rms_bwd(xh, r, wgn, dog * (g * sg))
            dog_ref[:, ls] = dx
            dwgn_ref[...] += dw
        wsn = wsn_ref[...]
        _, xh, r = _rms(os_ref[...], wsn)
        dx, dw = _rms_bwd(xh, r, wsn, dcat[:, GLA_W:])
        dos_ref[...] = dx
        dwsn_ref[...] += dw

    def row(w):
        return pl.BlockSpec((tm, w), lambda i: (i, 0))

    def rshape(w, dt=F32):
        return jax.ShapeDtypeStruct((t, w), dt)

    return _pallas(
        body, name="mix_out_bwd", grid=(t // tm,),
        in_specs=[row(D_MODEL), row(D_MODEL), row(512), row(512), row(512), _full((1, GLA_DV)), _full((1, SWA_W)),
                  _full((D_MODEL, D_MODEL)), _full((1, D_MODEL))],
        out_specs=[row(512), row(512), row(512), row(D_MODEL), _full((1, D_MODEL)), _full((1, GLA_DV)),
                   _full((1, SWA_W))],
        out_shape=[rshape(512), rshape(512), rshape(512), rshape(D_MODEL, BF16),
                   jax.ShapeDtypeStruct((1, D_MODEL), F32), jax.ShapeDtypeStruct((1, GLA_DV), F32),
                   jax.ShapeDtypeStruct((1, SWA_W), F32)],
        args=(dh2, m, ogla, gg, oswa, wgn, wsn, wout, wpost), hook=hook)


def _mix_in_bwd(dh2, h1, wmixpre, winp, wa2p, bap, cos, sin, ga, dgq, dgk, dgv, dgg, dla, dsq, dsk, dsv, dkm, dvm):
    t = h1.shape[0]
    tm = _row_tile(t)

    def body(dh2_ref, h_ref, w_ref, win_ref, wa2_ref, ba_ref, cos_ref, sin_ref, ga_ref, dgq_ref, dgk_ref, dgv_ref,
             dgg_ref, dla_ref, dsq_ref, dsk_ref, dsv_ref, dkm_ref, dvm_ref,
             dh1_ref, dproj_ref, dw_ref, dwa2_ref, dba_ref):
        i = pl.program_id(0)

        @pl.when(i == 0)
        def _():
            dw_ref[...] = jnp.zeros_like(dw_ref)
            dwa2_ref[...] = jnp.zeros_like(dwa2_ref)
            dba_ref[...] = jnp.zeros_like(dba_ref)

        first = (i == 0).astype(F32)
        c = cos_ref[...]
        s = -sin_ref[...]
        fh = _first_half_mask(tm)
        dproj_ref[:, P_GQ:P_GK] = dgq_ref[...].astype(BF16)
        dproj_ref[:, P_GK:P_GV] = dgk_ref[...].astype(BF16)
        dproj_ref[:, P_GV:P_GG] = dgv_ref[...].astype(BF16)
        dproj_ref[:, P_GG:P_GA] = dgg_ref[...].astype(BF16)
        gab = ga_ref[...].astype(BF16)
        z = _dot(gab, wa2_ref[...]) + ba_ref[...]
        row_id = i * tm + lax.broadcasted_iota(jnp.int32, (tm, 1), 0)
        dz = jnp.where(row_id >= PAD, dla_ref[...] * (1.0 / GLA_TAU) * (1.0 - _sigmoid(z)), 0.0)
        dzb = dz.astype(BF16)
        dba_ref[...] += jnp.sum(dz, axis=0, keepdims=True)
        dwa2_ref[...] += _dg(gab, dzb, TN)
        dproj_ref[:, P_GA:P_SQ] = _dg(dzb, wa2_ref[...], NT).astype(BF16)
        for k in range(4):
            dy = dsq_ref[:, 128 * k:128 * (k + 1)]
            dproj_ref[:, P_SQ + 128 * k:P_SQ + 128 * (k + 1)] = (dy * c + _rot_half(dy, fh) * s).astype(BF16)
        for k in range(2):
            ls = slice(128 * k, 128 * (k + 1))
            dy = dsk_ref[:, ls]
            dy = jnp.concatenate([dy[:BLK] + first * dkm_ref[:, ls], dy[BLK:]], axis=0) if tm > BLK else (
                dy + first * dkm_ref[:, ls])
            dproj_ref[:, P_SK + 128 * k:P_SK + 128 * (k + 1)] = (dy * c + _rot_half(dy, fh) * s).astype(BF16)
            dv = dsv_ref[:, ls]
            dv = jnp.concatenate([dv[:BLK] + first * dvm_ref[:, ls], dv[BLK:]], axis=0) if tm > BLK else (
                dv + first * dvm_ref[:, ls])
            dproj_ref[:, P_SV + 128 * k:P_SV + 128 * (k + 1)] = dv.astype(BF16)
        dn = _dg(dproj_ref[...], win_ref[...], NT)
        w = w_ref[...]
        _, hh, r = _rms(h_ref[...], w)
        dx, dw = _rms_bwd(hh, r, w, dn)
        dw_ref[...] += dw
        dh1_ref[...] = dh2_ref[...] + dx

    def row(w):
        return pl.BlockSpec((tm, w), lambda i: (i, 0))

    return pl.pallas_call(
        body, name="mix_in_bwd", grid=(t // tm,),
        in_specs=[row(D_MODEL), row(D_MODEL), _full((1, D_MODEL)), _full((D_MODEL, P_END)), _full((128, GLA_KW)),
                  _full((1, GLA_KW)), row(128), row(128), row(128), row(256), row(256), row(512), row(512), row(256),
                  row(512), row(256), row(256), _full((BLK, 256)), _full((BLK, 256))],
        out_specs=[row(D_MODEL), row(P_END), _full((1, D_MODEL)), _full((128, GLA_KW)), _full((1, GLA_KW))],
        out_shape=[jax.ShapeDtypeStruct((t, D_MODEL), F32), jax.ShapeDtypeStruct((t, P_END), BF16),
                   jax.ShapeDtypeStruct((1, D_MODEL), F32), jax.ShapeDtypeStruct((128, GLA_KW), F32),
                   jax.ShapeDtypeStruct((1, GLA_KW), F32)],
        compiler_params=_cparams(1),
    )(dh2, h1, wmixpre, winp, wa2p, bap, cos, sin, ga, dgq, dgk, dgv, dgg, dla, dsq, dsk, dsv, dkm, dvm)


def _adamw_update(w, g, m, v):
    m = ADAM_B1 * m + (1.0 - ADAM_B1) * g
    v = ADAM_B2 * v + (1.0 - ADAM_B2) * (g * g)
    m_hat = m / (1.0 - ADAM_B1 ** ADAM_STEP)
    v_hat = v / (1.0 - ADAM_B2 ** ADAM_STEP)
    return -ADAM_LR * (m_hat / (jnp.sqrt(v_hat) + ADAM_EPS) + ADAM_WD * w), m, v


def _adamw(w, g, m, v):
    r, c = w.shape
    tr = _div_tile(r)

    def body(w_ref, g_ref, m_ref, v_ref, d_ref, nm_ref, nv_ref):
        d_ref[...], nm_ref[...], nv_ref[...] = _adamw_update(w_ref[...], g_ref[...], m_ref[...], v_ref[...])

    spec = pl.BlockSpec((tr, c), lambda i: (i, 0))
    shape = jax.ShapeDtypeStruct((r, c), F32)
    return pl.pallas_call(
        body, name="adamw", grid=(r // tr,), in_specs=[spec] * 4, out_specs=[spec] * 3, out_shape=[shape] * 3,
        compiler_params=_cparams(1),
    )(w, g, m, v)


def _adamw_halves(w, g_mine, g_other, m, v, c_idx, row0=0):
    r, c = w.shape
    h = g_mine.shape[0]
    tr = _div_tile(math.gcd(r, h))
    nth = h // tr
    t0 = row0 // tr
    assert t0 * tr == row0

    def body(c_ref, w_ref, gm_ref, go_ref, m_ref, v_ref, g_ref, d_ref, nm_ref, nv_ref):
        hh = (t0 + pl.program_id(0)) // nth
        g = jnp.where(hh == c_ref[0], gm_ref[...], go_ref[...])
        g_ref[...] = g
        d_ref[...], nm_ref[...], nv_ref[...] = _adamw_update(w_ref[...], g, m_ref[...], v_ref[...])

    spec = pl.BlockSpec((tr, c), lambda i, c_ref: (i, 0))
    gspec = pl.BlockSpec((tr, c), lambda i, c_ref: ((t0 + i) % nth, 0))
    shape = jax.ShapeDtypeStruct((r, c), F32)
    return pl.pallas_call(
        body, name="adamw_halves",
        grid_spec=pltpu.PrefetchScalarGridSpec(
            num_scalar_prefetch=1, grid=(r // tr,), in_specs=[spec, gspec, gspec, spec, spec], out_specs=[spec] * 4),
        out_shape=[shape] * 4, compiler_params=_cparams(1),
    )(c_idx, w, g_mine, g_other, m, v)


def _place():
    x, y, c = lax.axis_index("x"), lax.axis_index("y"), lax.axis_index("c")
    chips = [(1 - x, y), (x, 1 - y), (1 - x, 1 - y)]
    return x, y, c, chips


def _remote(send_sem, recv_sem, src, dst, to):
    return pltpu.make_async_remote_copy(src_ref=src, dst_ref=dst, send_sem=send_sem, recv_sem=recv_sem,
                                        device_id=to, device_id_type=MESH)


def _half(ref_rows, c):
    h = ref_rows // 2
    return pl.ds(pl.multiple_of(c * h, 8), h)


def _own_slot(shard, q):
    return lax.dynamic_update_slice(jnp.zeros((N_CHIPS,) + shard.shape, shard.dtype), shard[None], (q, 0, 0))


class _GatherChips:
    has_mid = True

    def __init__(self, bufs):
        n = len(bufs)
        self.inputs = list(bufs)
        self.out_shape = [jax.ShapeDtypeStruct(b.shape, b.dtype) for b in bufs]
        self.aliases = [(t, t) for t in range(n)]
        self.scratch = [pltpu.SemaphoreType.DMA((n, 6)), pltpu.SemaphoreType.DMA((n, 6))]

    def start(self, ins, outs, scr):
        send, recv = scr
        x, y, c, chips = _place()
        q = 2 * x + y
        for t, (i_ref, o_ref) in enumerate(zip(ins, outs)):
            rows = _half(i_ref.shape[1], c)
            for j, (cx, cy) in enumerate(chips):
                _remote(send.at[t, j], recv.at[t, j], i_ref.at[q, rows], o_ref.at[q, rows], (cx, cy, c)).start()

    def mid(self, ins, outs, scr):
        send, recv = scr
        x, y, c, chips = _place()
        for t, o_ref in enumerate(outs):
            rows = _half(o_ref.shape[1], c)
            for j, (cx, cy) in enumerate(chips):
                slot = o_ref.at[2 * cx + cy, rows]
                _remote(send.at[t, j], recv.at[t, j], slot, slot, (cx, cy, c)).wait_recv()
                _remote(send.at[t, 3 + j], recv.at[t, 3 + j], slot, slot, (x, y, 1 - c)).start()

    def finish(self, ins, outs, scr):
        send, recv = scr
        x, y, c, chips = _place()
        for t, o_ref in enumerate(outs):
            mine, other = _half(o_ref.shape[1], c), _half(o_ref.shape[1], 1 - c)
            for j, (cx, cy) in enumerate(chips):
                slot = o_ref.at[2 * cx + cy, other]
                _remote(send.at[t, 3 + j], recv.at[t, 3 + j], slot, slot, (x, y, 1 - c)).wait_recv()
            for j, (cx, cy) in enumerate(chips):
                sent = o_ref.at[2 * cx + cy, mine]
                _remote(send.at[t, j], recv.at[t, j], sent, sent, (cx, cy, c)).wait_send()
                _remote(send.at[t, 3 + j], recv.at[t, 3 + j], sent, sent, (x, y, 1 - c)).wait_send()


class _PairExchange:
    has_mid = False
    aliases = ()

    def __init__(self, arrs):
        n = len(arrs)
        self.inputs = list(arrs)
        self.out_shape = [jax.ShapeDtypeStruct((a.shape[0], a.shape[1] // 2, a.shape[2]), a.dtype) for a in arrs]
        self.scratch = [pltpu.SemaphoreType.DMA((n,)), pltpu.SemaphoreType.DMA((n,))]

    def _copies(self, ins, outs, scr):
        send, recv = scr
        x, y, c, _ = _place()
        return [_remote(send.at[t], recv.at[t], i_ref.at[:, _half(i_ref.shape[1], 1 - c)], o_ref, (x, y, 1 - c))
                for t, (i_ref, o_ref) in enumerate(zip(ins, outs))]

    def start(self, ins, outs, scr):
        for cp in self._copies(ins, outs, scr):
            cp.start()

    def finish(self, ins, outs, scr):
        for cp in self._copies(ins, outs, scr):
            cp.wait()


class _ChipScatter:
    has_mid = False
    aliases = ()

    def __init__(self, arrs):
        n = len(arrs)
        self.inputs = list(arrs)
        self.out_shape = [jax.ShapeDtypeStruct((3,) + a.shape[1:], a.dtype) for a in arrs]
        self.scratch = [pltpu.SemaphoreType.DMA((n, 3)), pltpu.SemaphoreType.DMA((n, 3))]

    def _copies(self, ins, outs, scr):
        send, recv = scr
        x, y, c, chips = _place()
        return [_remote(send.at[t, j], recv.at[t, j], i_ref.at[2 * cx + cy], o_ref.at[j], (cx, cy, c))
                for t, (i_ref, o_ref) in enumerate(zip(ins, outs)) for j, (cx, cy) in enumerate(chips)]

    def start(self, ins, outs, scr):
        for cp in self._copies(ins, outs, scr):
            cp.start()

    def finish(self, ins, outs, scr):
        for cp in self._copies(ins, outs, scr):
            cp.wait()


class _PairShare:
    has_mid = False
    aliases = ()

    def __init__(self, arrs):
        n = len(arrs)
        self.inputs = list(arrs)
        self.out_shape = [jax.ShapeDtypeStruct(a.shape, a.dtype) for a in arrs]
        self.scratch = [pltpu.SemaphoreType.DMA((n,)), pltpu.SemaphoreType.DMA((n,))]

    def _copies(self, ins, outs, scr):
        send, recv = scr
        x, y, c, _ = _place()
        return [_remote(send.at[t], recv.at[t], i_ref, o_ref, (x, y, 1 - c))
                for t, (i_ref, o_ref) in enumerate(zip(ins, outs))]

    def start(self, ins, outs, scr):
        for cp in self._copies(ins, outs, scr):
            cp.start()

    def finish(self, ins, outs, scr):
        for cp in self._copies(ins, outs, scr):
            cp.wait()


def _comm_call(hook, name):
    n_in, n_out = len(hook.inputs), len(hook.out_shape)

    def body(*refs):
        ins, outs, scr = refs[:n_in], refs[n_in:n_in + n_out], refs[n_in + n_out:]
        hook.start(ins, outs, scr)
        if hook.has_mid:
            hook.mid(ins, outs, scr)
        hook.finish(ins, outs, scr)

    return pl.pallas_call(body, name=name, in_specs=[ANY] * n_in, out_specs=[ANY] * n_out,
                          out_shape=list(hook.out_shape), scratch_shapes=list(hook.scratch),
                          input_output_aliases=dict(hook.aliases))(*hook.inputs)


def _all_gather_devices(vecs):
    n = len(vecs)

    def body(*refs):
        x_refs, out_refs = refs[:n], refs[n:2 * n]
        send_sems, recv_sems, local_sems = refs[2 * n:]
        x, y, c, chips = _place()
        me, sibling = (x, y, c), (x, y, 1 - c)
        waits = []
        for t, (x_ref, out_ref) in enumerate(zip(x_refs, out_refs)):
            def slot(px, py, pc, out_ref=out_ref):
                return out_ref.at[4 * px + 2 * py + pc]

            def copy(k, block, to, src=None, t=t, slot=slot):
                return pltpu.make_async_remote_copy(
                    src_ref=slot(*block) if src is None else src, dst_ref=slot(*block), send_sem=send_sems.at[t, k],
                    recv_sem=recv_sems.at[t, k], device_id=to, device_id_type=MESH)

            mine = pltpu.make_async_copy(x_ref, slot(*me), local_sems.at[t])
            mine.start()
            first = [copy(0, me, sibling, src=x_ref)]
            first += [copy(1 + j, me, (*chip, c), src=x_ref) for j, chip in enumerate(chips)]
            for cp in first:
                cp.start()
            waits.append((copy, mine, first))
        for copy, mine, first in waits:
            passed = [copy(4 + j, (*chip, c), sibling) for j, chip in enumerate(chips)]
            for j, chip in enumerate(chips):
                copy(1 + j, (*chip, c), me).wait_recv()
                passed[j].start()
            copy(0, sibling, me).wait_recv()
            for j, chip in enumerate(chips):
                copy(4 + j, (*chip, 1 - c), me).wait_recv()
            for cp in first + passed:
                cp.wait_send()
            mine.wait()

    vmem = pl.BlockSpec(memory_space=pltpu.VMEM)
    return pl.pallas_call(
        body, name="all_gather_devices", in_specs=[vmem] * n, out_specs=[vmem] * n,
        out_shape=[jax.ShapeDtypeStruct((N_DEV,) + v.shape, v.dtype) for v in vecs],
        scratch_shapes=[pltpu.SemaphoreType.DMA((n, 7)), pltpu.SemaphoreType.DMA((n, 7)),
                        pltpu.SemaphoreType.DMA((n,))],
    )(*vecs)


def _pair_sum(g, other, c_idx):
    nq, r, w = g.shape
    h = r // 2
    tr = _div_tile(h)
    nt = h // tr

    def body(c_ref, g_ref, o_ref, s_ref):
        s_ref[...] = (g_ref[...].astype(F32) + o_ref[...].astype(F32)).astype(s_ref.dtype)

    return pl.pallas_call(
        body, name="pair_sum",
        grid_spec=pltpu.PrefetchScalarGridSpec(
            num_scalar_prefetch=1, grid=(nq, nt),
            in_specs=[pl.BlockSpec((None, tr, w), lambda k, i, c_ref: (k, c_ref[0] * nt + i, 0)),
                      pl.BlockSpec((None, tr, w), lambda k, i, c_ref: (k, i, 0))],
            out_specs=pl.BlockSpec((None, tr, w), lambda k, i, c_ref: (k, i, 0))),
        out_shape=jax.ShapeDtypeStruct((nq, h, w), g.dtype),
        compiler_params=_cparams(2),
    )(c_idx, g, other)


def _chip_sum(s, others, q_idx):
    _, h, w = s.shape
    tr = _div_tile(h)

    def body(q_ref, s_ref, o_ref, out_ref):
        out_ref[...] = ((s_ref[...].astype(F32) + o_ref[0].astype(F32)) + o_ref[1].astype(F32)) + o_ref[2].astype(F32)

    return pl.pallas_call(
        body, name="chip_sum",
        grid_spec=pltpu.PrefetchScalarGridSpec(
            num_scalar_prefetch=1, grid=(h // tr,),
            in_specs=[pl.BlockSpec((None, tr, w), lambda i, q_ref: (q_ref[0], i, 0)),
                      pl.BlockSpec((3, tr, w), lambda i, q_ref: (0, i, 0))],
            out_specs=pl.BlockSpec((tr, w), lambda i, q_ref: (i, 0))),
        out_shape=jax.ShapeDtypeStruct((h, w), F32),
        compiler_params=_cparams(1),
    )(q_idx, s, others)


def _small_update(q_idx, parts, ws, ms, vs, col_block):
    n = len(parts)
    has_w = [w is not None for w in ws]

    def body(q_ref, *refs):
        pos = 0
        ins = []
        for t in range(n):
            k = 4 if has_w[t] else 1
            ins.append(refs[pos:pos + k])
            pos += k
        outs = refs[pos:]
        opos = 0
        for t in range(n):
            p_ref = ins[t][0]
            g = p_ref[0]
            for s in range(1, p_ref.shape[0]):
                g = g + p_ref[s]
            if has_w[t]:
                _, w_ref, m_ref, v_ref = ins[t]
                g_ref, d_ref, nm_ref, nv_ref = outs[opos:opos + 4]
                opos += 4
                g_ref[...] = g
                d_ref[...], nm_ref[...], nv_ref[...] = _adamw_update(w_ref[...], g, m_ref[...], v_ref[...])
            else:
                outs[opos][...] = g
                opos += 1

    def whole(shape):
        nd = len(shape)
        return pl.BlockSpec(shape, lambda i, q_ref: (0,) * nd)

    in_specs, out_specs, out_shape, args = [], [], [], []
    for t in range(n):
        k, r, wf = parts[t].shape
        if col_block[t]:
            w = wf // N_CHIPS
            in_specs.append(pl.BlockSpec((k, r, w), lambda i, q_ref: (0, 0, q_ref[0])))
        else:
            w = wf
            in_specs.append(whole((k, r, wf)))
        args.append(parts[t])
        if has_w[t]:
            assert ws[t].shape == (r, w), (ws[t].shape, r, w)
            in_specs += [whole((r, w))] * 3
            args += [ws[t], ms[t], vs[t]]
            out_specs += [whole((r, w))] * 4
            out_shape += [jax.ShapeDtypeStruct((r, w), F32)] * 4
        else:
            out_specs.append(whole((r, w)))
            out_shape.append(jax.ShapeDtypeStruct((r, w), F32))
    return pl.pallas_call(
        body, name="small_update",
        grid_spec=pltpu.PrefetchScalarGridSpec(num_scalar_prefetch=1, grid=(1,), in_specs=in_specs,
                                               out_specs=out_specs),
        out_shape=out_shape, compiler_params=_cparams(1),
    )(q_idx, *args)


def _pack_win(w_in):
    o = np.cumsum((0,) + IN_SPLITS)
    gq, gk, gv, gg, ga, sq, sk, sv = [w_in[:, o[i]:o[i + 1]] for i in range(8)]
    z = jnp.zeros((w_in.shape[0], 128 - GLA_RANK), w_in.dtype)
    dup = lambda a: jnp.concatenate([a[:, :64], a[:, :64], a[:, 64:], a[:, 64:]], axis=1)
    return jnp.concatenate([gq, gk, gv, gg, ga, z, sq, dup(sk), dup(sv)], axis=1)


def _unpack_dwin(d):
    und = lambda a: jnp.concatenate([a[:, 0:64] + a[:, 64:128], a[:, 128:192] + a[:, 192:256]], axis=1)
    return jnp.concatenate([d[:, :P_GA], d[:, P_GA:P_GA + GLA_RANK], d[:, P_SQ:P_SK], und(d[:, P_SK:P_SV]),
                            und(d[:, P_SV:P_END])], axis=1)


def _local_step(x, target, meta, p):
    s = x.shape[0]
    t = s + BLK
    h0 = jnp.concatenate([jnp.zeros((PAD, D_MODEL), F32), meta, x], axis=0)
    cos, sin = _rope_tables(t)

    h1, n1, g1, u1, a1, f1 = _ffn_fwd(h0, p["ffn1_pre_norm"], p["ffn1_w_gate"], p["ffn1_w_up"], p["ffn1_w_down"],
                                      p["ffn1_post_norm"])
    n2, gq, gk, gv, gg, ga, la, sq, sk, sv = _mix_proj(h1, p["mix_pre_norm"], p["w_in"], p["gla_w_a2"], p["gla_b_a"],
                                                       cos, sin)
    ogla, ss = _gla_fwd(gq, gk, gv, la)
    oswa = _swa_fwd(p["swa_sinks"], sq, sk, sv)
    h2, cat, m = _mix_out(h1, ogla, gg, oswa, p["gla_out_norm"], p["swa_out_norm"], p["w_out"], p["mix_post_norm"])
    dy, n3, g3, u3, a3, f3, sse = _ffn_fwd(h2, p["ffn2_pre_norm"], p["ffn2_w_gate"], p["ffn2_w_up"],
                                           p["ffn2_w_down"], p["ffn2_post_norm"], target=target)

    grads = {}
    dh2, df3, dg3, du3, grads["ffn2_pre_norm"], grads["ffn2_post_norm"] = _ffn_bwd(
        dy, h2, f3, g3, u3, p["ffn2_pre_norm"], p["ffn2_w_gate"], p["ffn2_w_up"], p["ffn2_w_down"],
        p["ffn2_post_norm"])
    (gud,) = _ffn_wgrad(n3, df3, dg3, du3, a3)
    grads["ffn2_w_gate"], grads["ffn2_w_up"], grads["ffn2_w_down"] = gud[:, :FJ], gud[:, FJ:2 * FJ], gud[:, 2 * FJ:]

    dogla, dgg, doswa, dm, grads["mix_post_norm"], grads["gla_out_norm"], grads["swa_out_norm"] = _mix_out_bwd(
        dh2, m, ogla, gg, oswa, p["gla_out_norm"], p["swa_out_norm"], p["w_out"], p["mix_post_norm"])
    grads["w_out"] = _xty(cat, dm)
    dsq, dsk, dsv, dkm, dvm, dsinks = _swa_bwd(p["swa_sinks"], sq, sk, sv, oswa, doswa)
    grads["swa_sinks"] = dsinks[:, 0]
    dgq, dgk, dgv, dla = _gla_bwd(gq, gk, gv, la, ss, dogla)
    dh1, dproj, grads["mix_pre_norm"], dwa2p, grads["gla_b_a"] = _mix_in_bwd(
        dh2, h1, p["mix_pre_norm"], p["w_in"], p["gla_w_a2"], p["gla_b_a"], cos, sin, ga, dgq, dgk, dgv, dgg, dla,
        dsq, dsk, dsv, dkm, dvm)
    grads["gla_w_a2"] = dwa2p[:GLA_RANK]
    grads["w_in"] = _unpack_dwin(_xty(n2, dproj))

    dh0, df1, dg1, du1, grads["ffn1_pre_norm"], grads["ffn1_post_norm"] = _ffn_bwd(
        dh1, h0, f1, g1, u1, p["ffn1_pre_norm"], p["ffn1_w_gate"], p["ffn1_w_up"], p["ffn1_w_down"],
        p["ffn1_post_norm"])
    (gud,) = _ffn_wgrad(n1, df1, dg1, du1, a1)
    grads["ffn1_w_gate"], grads["ffn1_w_up"], grads["ffn1_w_down"] = gud[:, :FJ], gud[:, FJ:2 * FJ], gud[:, 2 * FJ:]
    grads["meta_tokens"] = dh0[PAD:BLK]
    return sse[0, 0], dh0[BLK:], grads


WEIGHTS = ['meta_tokens', 'ffn1_pre_norm', 'ffn1_w_gate', 'ffn1_w_up', 'ffn1_w_down', 'ffn1_post_norm',
           'mix_pre_norm', 'w_in', 'gla_w_a2', 'gla_b_a', 'gla_out_norm', 'swa_sinks', 'swa_out_norm', 'w_out',
           'mix_post_norm', 'ffn2_pre_norm', 'ffn2_w_gate', 'ffn2_w_up', 'ffn2_w_down', 'ffn2_post_norm']
BIG = ['ffn1_w_gate', 'ffn1_w_up', 'ffn1_w_down', 'w_in', 'w_out', 'ffn2_w_gate', 'ffn2_w_up', 'ffn2_w_down']
SMALL = [n for n in WEIGHTS if n not in BIG]
FJ = D_FF // N_CHIPS
D_IN_J = D_IN // N_CHIPS
D_OUT_J = D_MODEL // N_CHIPS
TRANSPOSED = ('ffn1_w_gate', 'ffn1_w_up', 'ffn2_w_gate', 'ffn2_w_up')


def _shard2d(name, a):
    return a[0].T if name in TRANSPOSED else a[0]


def _unshard2d(name, a):
    return (a.T if name in TRANSPOSED else a)[None]


def _small_rows(name, a):
    flat = a.reshape(-1)
    rows = -(-flat.shape[0] // 1024) * 8
    return jnp.pad(flat, (0, rows * 128 - flat.shape[0])).reshape(rows, 128)


def kernel(x, meta_tokens, ffn1_pre_norm, ffn1_w_gate, ffn1_w_up, ffn1_w_down, ffn1_post_norm, mix_pre_norm, w_in, gla_w_a2, gla_b_a, gla_out_norm, swa_sinks, swa_out_norm, w_out, mix_post_norm, ffn2_pre_norm, ffn2_w_gate, ffn2_w_up, ffn2_w_down, ffn2_post_norm, loss_target, m_meta_tokens, m_ffn1_pre_norm, m_ffn1_w_gate, m_ffn1_w_up, m_ffn1_w_down, m_ffn1_post_norm, m_mix_pre_norm, m_w_in, m_gla_w_a2, m_gla_b_a, m_gla_out_norm, m_swa_sinks, m_swa_out_norm, m_w_out, m_mix_post_norm, m_ffn2_pre_norm, m_ffn2_w_gate, m_ffn2_w_up, m_ffn2_w_down, m_ffn2_post_norm, v_meta_tokens, v_ffn1_pre_norm, v_ffn1_w_gate, v_ffn1_w_up, v_ffn1_w_down, v_ffn1_post_norm, v_mix_pre_norm, v_w_in, v_gla_w_a2, v_gla_b_a, v_gla_out_norm, v_swa_sinks, v_swa_out_norm, v_w_out, v_mix_post_norm, v_ffn2_pre_norm, v_ffn2_w_gate, v_ffn2_w_up, v_ffn2_w_down, v_ffn2_post_norm):
    args = dict(locals())
    w = {n: args[n] for n in WEIGHTS}
    mom = {n: args["m_" + n] for n in WEIGHTS}
    var = {n: args["v_" + n] for n in WEIGHTS}
    cx, cy, cc = lax.axis_index("x"), lax.axis_index("y"), lax.axis_index("c")
    q_idx = (2 * cx + cy).astype(jnp.int32).reshape(1)
    c_idx = cc.astype(jnp.int32).reshape(1)

    q_chip = 2 * cx + cy
    bf = {n: _own_slot(_shard2d(n, w[n]).astype(BF16), q_chip) for n in BIG}
    qc_idx = jnp.stack([q_chip, cc]).astype(jnp.int32)
    early = _GatherChips([_own_slot(w["meta_tokens"], q_chip),
                          _own_slot(w["gla_w_a2"].reshape(GLA_RANK, GLA_KW // N_CHIPS), q_chip)])
    meta4, wa24 = _comm_call(early, "gather_small")
    meta_full = meta4.transpose(1, 0, 2).reshape(N_META, D_MODEL)
    wa2p = jnp.pad(wa24.transpose(1, 0, 2).reshape(GLA_RANK, GLA_KW), ((0, 128 - GLA_RANK), (0, 0))).astype(BF16)
    sinks = w["swa_sinks"].reshape(SWA_QH)

    seq, target = x[0], loss_target[0]
    t = seq.shape[0] + BLK
    h0 = jnp.concatenate([jnp.zeros((PAD, D_MODEL), F32), meta_full, seq], axis=0)
    cos, sin = _rope_tables(t)
    late = _GatherChips([bf["w_in"], bf["w_out"], bf["ffn2_w_gate"], bf["ffn2_w_up"], bf["ffn2_w_down"]])
    n1 = _rms_cast(h0, w["ffn1_pre_norm"])
    (h1, g1, u1, a1, f1), (wg1, wu1, wd1), (win4, wout4, wg2, wu2, wd2) = _ffn_fwd_gather(
        h0, n1, [bf["ffn1_w_gate"], bf["ffn1_w_up"], bf["ffn1_w_down"]], w["ffn1_post_norm"], qc_idx, late)
    winp = _pack_win(win4.transpose(1, 0, 2).reshape(D_MODEL, D_IN))
    wout = wout4.reshape(D_MODEL, D_MODEL)
    n2, gq, gk, gv, gg, ga, la, sq, sk, sv = _mix_proj(h1, w["mix_pre_norm"], winp, wa2p, w["gla_b_a"], cos, sin)
    ogla, ss = _gla_fwd(gq, gk, gv, la)
    oswa = _swa_fwd(sinks, sq, sk, sv)
    h2, cat, m = _mix_out(h1, ogla, gg, oswa, w["gla_out_norm"], w["swa_out_norm"], wout, w["mix_post_norm"])
    dy, n3, g3, u3, a3, f3, sse = _ffn_fwd(h2, w["ffn2_pre_norm"], wg2, wu2, wd2, w["ffn2_post_norm"], target=target)
    loss = lax.psum(sse[0, 0] * (0.5 / D_MODEL), ("x", "y", "c"))

    g = {}
    dh2, df3, dg3, du3, g["ffn2_pre_norm"], g["ffn2_post_norm"] = _ffn_bwd(
        dy, h2, f3, g3, u3, w["ffn2_pre_norm"], wg2, wu2, wd2, w["ffn2_post_norm"])
    (gf2,) = _ffn_wgrad(n3, df3, dg3, du3, a3)
    (dogla, dgg, doswa, dm, g["mix_post_norm"], g["gla_out_norm"], g["swa_out_norm"]), (rgf2,) = _mix_out_bwd(
        dh2, m, ogla, gg, oswa, w["gla_out_norm"], w["swa_out_norm"], wout, w["mix_post_norm"],
        hook=_PairExchange([gf2]))
    sgf2 = _pair_sum(gf2, rgf2, c_idx)
    gout = _xty(cat, dm).reshape(N_CHIPS, D_OUT_J, D_MODEL).astype(BF16)
    (dsq, dsk, dsv, dkm, dvm, dsinks), (ogf2,) = _swa_bwd(sinks, sq, sk, sv, oswa, doswa,
                                                          hook=_ChipScatter([sgf2]))
    g["swa_sinks"] = dsinks
    dgq, dgk, dgv, dla = _gla_bwd(gq, gk, gv, la, ss, dogla)
    dh1, dproj, g["mix_pre_norm"], dwa2p, g["gla_b_a"] = _mix_in_bwd(
        dh2, h1, w["mix_pre_norm"], winp, wa2p, w["gla_b_a"], cos, sin, ga, dgq, dgk, dgv, dgg, dla,
        dsq, dsk, dsv, dkm, dvm)
    g["gla_w_a2"] = dwa2p[:GLA_RANK]
    gin = _unpack_dwin(_xty(n2, dproj)).reshape(D_MODEL, N_CHIPS, D_IN_J).transpose(1, 0, 2).astype(BF16)
    (dh0, df1, dg1, du1, g["ffn1_pre_norm"], g["ffn1_post_norm"]), (rgin, rgout) = _ffn_bwd(
        dh1, h0, f1, g1, u1, w["ffn1_pre_norm"], wg1, wu1, wd1, w["ffn1_post_norm"],
        hook=_PairExchange([gin, gout]))
    sgin, sgout = _pair_sum(gin, rgin, c_idx), _pair_sum(gout, rgout, c_idx)
    own1, others1, (ogin, ogout) = _ffn_wgrad_reduce(n1, df1, dg1, du1, a1, qc_idx, _ChipScatter([sgin, sgout]))
    g["meta_tokens"] = dh0[PAD:BLK]
    grad_x = dh0[BLK:]
    halves = [_chip_sum(own1[None], others1, jnp.zeros((1,), jnp.int32))]
    halves += [_chip_sum(s, o, q_idx) for s, o in ((sgin, ogin), (sgout, ogout), (sgf2, ogf2))]
    others = _comm_call(_PairShare(halves), "pair_share")
    reduced = {"ffn1_w_gate": (0, 0), "ffn1_w_up": (0, FJ), "ffn1_w_down": (0, 2 * FJ), "w_in": (1, 0),
               "w_out": (2, 0), "ffn2_w_gate": (3, 0), "ffn2_w_up": (3, FJ), "ffn2_w_down": (3, 2 * FJ)}
    grad, delta, new_m, new_v = {}, {}, {}, {}
    for n in BIG:
        k, row0 = reduced[n]
        outs = _adamw_halves(_shard2d(n, w[n]), halves[k], others[k], _shard2d(n, mom[n]), _shard2d(n, var[n]),
                             c_idx, row0)
        grad[n], delta[n], new_m[n], new_v[n] = [_unshard2d(n, a) for a in outs]

    late = ["gla_w_a2", "swa_sinks"]
    direct = [n for n in SMALL if n not in late]
    names = direct + late
    gathered = _all_gather_devices([g[n] for n in names])
    mat = lambda a: a.reshape(a.shape[-2:])
    none2 = [None] * len(late)
    outs = _small_update(q_idx, gathered, [mat(w[n]) for n in direct] + none2, [mat(mom[n]) for n in direct] + none2,
                         [mat(var[n]) for n in direct] + none2, [n == "meta_tokens" for n in names])
    sum_a2, sum_sinks = outs[4 * len(direct):]
    g_late = [lax.dynamic_slice_in_dim(sum_a2, q_chip * (GLA_KW // N_CHIPS), GLA_KW // N_CHIPS, axis=1)[None],
              sum_sinks[:, 0].reshape(1, 1, SWA_QH)]
    outs = list(outs[:4 * len(direct)]) + list(_small_update(
        q_idx, g_late, [mat(w[n]) for n in late], [mat(mom[n]) for n in late], [mat(var[n]) for n in late],
        [False, False]))
    for k, n in enumerate(names):
        grad[n], delta[n], new_m[n], new_v[n] = [a.reshape(w[n].shape) for a in outs[4 * k:4 * k + 4]]

    return (loss, grad_x[None], *[grad[n] for n in WEIGHTS], *[delta[n] for n in WEIGHTS],
            *[new_m[n] for n in WEIGHTS], *[new_v[n] for n in WEIGHTS])
```

```python
import functools
import math

import numpy as np
import jax
import jax.numpy as jnp
from jax import lax
from jax.experimental import pallas as pl
from jax.experimental.pallas import tpu as pltpu

F32 = jnp.float32
BF16 = jnp.bfloat16
MESH = pl.DeviceIdType.MESH

D_MODEL = 1024
D_FF = 2816
N_CHIPS = 4
N_DEV = 8
N_META = 16
BLK = 128
PAD = BLK - N_META
GLA_CHUNK = 64
GLA_HEADS = 4
GLA_DV = 128
GLA_DK = 64
GLA_KW = GLA_HEADS * GLA_DK
GLA_W = GLA_HEADS * GLA_DV
GLA_RANK = 16
GLA_TAU = 16.0
SWA_HD = 64
SWA_QH = 8
SWA_KVH = 2
SWA_W = SWA_QH * SWA_HD
WINDOW = 128
ROPE_THETA = 10000.0
EPS = 1e-6
NEG_INF = -1e30
IN_SPLITS = (256, 256, 512, 512, 16, 512, 128, 128)
D_IN = sum(IN_SPLITS)
P_GQ, P_GK, P_GV, P_GG, P_GA, P_SQ, P_SK, P_SV, P_END = 0, 256, 512, 1024, 1536, 1664, 2176, 2432, 2688
ADAM_LR, ADAM_B1, ADAM_B2, ADAM_EPS, ADAM_WD, ADAM_STEP = 0.001, 0.9, 0.999, 1e-08, 0.01, 10
VMEM_LIMIT = 56 * 1024 * 1024

NT = (((1,), (1,)), ((), ()))
TN = (((0,), (0,)), ((), ()))


def _cparams(n_axes):
    return pltpu.CompilerParams(dimension_semantics=("arbitrary",) * n_axes, vmem_limit_bytes=VMEM_LIMIT)


def _row_tile(t):
    for tm in (640, 512, 384, 256, 128):
        if t % tm == 0:
            return tm
    raise ValueError(t)


SEQ_BLOCKS_PER_STEP = 5


def _seq_tile(t):
    return SEQ_BLOCKS_PER_STEP * BLK if t % (SEQ_BLOCKS_PER_STEP * BLK) == 0 else BLK


ROW_PARTS = 2


def _row_parts(tm):
    n = ROW_PARTS if tm % (16 * ROW_PARTS) == 0 else 1
    return [slice(k * (tm // n), (k + 1) * (tm // n)) for k in range(n)]


def _contract_tile(t):
    return 1664 if t % 1664 == 0 else _row_tile(t)


def _div_tile(r, cap=512):
    best = None
    for tr in range(8, min(r, cap) + 1, 8):
        if r % tr == 0:
            best = tr
    return best if best is not None else r


def _dot(a, b):
    return jnp.dot(a, b, preferred_element_type=F32)


def _dg(a, b, dims):
    return lax.dot_general(a, b, dims, preferred_element_type=F32)


def _rms(x, w):
    r = lax.rsqrt(jnp.mean(x * x, axis=-1, keepdims=True) + EPS)
    xh = x * r
    return xh * w, xh, r


def _rms_bwd(xh, r, w, dy):
    wdy = dy * w
    dx = r * (wdy - xh * jnp.mean(wdy * xh, axis=-1, keepdims=True))
    dw = jnp.sum(dy * xh, axis=0, keepdims=True)
    return dx, dw


def _sigmoid(x):
    return 1.0 / (1.0 + jnp.exp(-x))


def _full(shape):
    nd = len(shape)
    return pl.BlockSpec(shape, lambda *_: (0,) * nd)


ANY = pl.BlockSpec(memory_space=pl.ANY)


def _pallas(body, *, name, grid, in_specs, out_specs, out_shape, args, scratch_shapes=(), hook=None):
    n_axes = len(grid)
    if hook is None:
        return pl.pallas_call(body, name=name, grid=grid, in_specs=list(in_specs), out_specs=list(out_specs),
                              out_shape=list(out_shape), scratch_shapes=list(scratch_shapes),
                              compiler_params=_cparams(n_axes))(*args)
    n_in, n_out, n_scr = len(in_specs), len(out_specs), len(scratch_shapes)
    h_in, h_out = len(hook.inputs), len(hook.out_shape)
    total = math.prod(grid)

    def wrapped(*refs):
        ins, hins = refs[:n_in], refs[n_in:n_in + h_in]
        o0 = n_in + h_in
        outs, houts = refs[o0:o0 + n_out], refs[o0 + n_out:o0 + n_out + h_out]
        s0 = o0 + n_out + h_out
        scr, hscr = refs[s0:s0 + n_scr], refs[s0 + n_scr:]
        step = pl.program_id(0)
        for a in range(1, n_axes):
            step = step * grid[a] + pl.program_id(a)

        @pl.when(step == 0)
        def _():
            hook.start(hins, houts, hscr)

        body(*ins, *outs, *scr)

        if hook.has_mid:
            @pl.when(step == (3 * total) // 4)
            def _():
                hook.mid(hins, houts, hscr)

        @pl.when(step == total - 1)
        def _():
            hook.finish(hins, houts, hscr)

    res = pl.pallas_call(
        wrapped, name=name, grid=grid, in_specs=list(in_specs) + [ANY] * h_in,
        out_specs=list(out_specs) + [ANY] * h_out, out_shape=list(out_shape) + list(hook.out_shape),
        scratch_shapes=list(scratch_shapes) + list(hook.scratch), compiler_params=_cparams(n_axes),
        input_output_aliases={n_in + a: n_out + b for a, b in hook.aliases},
    )(*args, *hook.inputs)
    return res[:n_out], res[n_out:]


def _ffn_fwd(h, wpre, wg4, wu4, wd4, wpost, hook=None, target=None):
    t = h.shape[0]
    tm = _row_tile(t)
    nj, fj, _ = wg4.shape
    nblk = tm // BLK if target is not None else 0

    def body(*refs):
        h_ref, wpre_ref, wg_ref, wu_ref, wd_ref, wpost_ref = refs[:6]
        t_refs = refs[6:6 + nblk]
        hout_ref, n_ref, p1_ref, p2_ref, a_ref, f_ref = refs[6 + nblk:12 + nblk]
        acc_ref = refs[-1]
        i = pl.program_id(0)
        j = pl.program_id(1)

        @pl.when(j == 0)
        def _():
            y, _, _ = _rms(h_ref[...], wpre_ref[...])
            n_ref[...] = y.astype(BF16)
            acc_ref[...] = jnp.zeros_like(acc_ref)

        if target is not None:
            sse_ref = refs[12 + nblk]

            @pl.when((i == 0) & (j == 0))
            def _():
                sse_ref[...] = jnp.zeros_like(sse_ref)

        parts = [slice(0, tm)]
        gus = [(_dg(n_ref[rows, :], wg_ref[...], NT), _dg(n_ref[rows, :], wu_ref[...], NT)) for rows in parts]
        for rows, (g, u) in zip(parts, gus):
            sg = _sigmoid(g)
            silu = g * sg
            p1_ref[rows, :] = (u * (sg + silu * (1.0 - sg))).astype(BF16)
            p2_ref[rows, :] = silu.astype(BF16)
            a = (silu * u).astype(BF16)
            a_ref[rows, :] = a
            acc_ref[rows, :] += _dot(a, wd_ref[...])

        @pl.when(j == nj - 1)
        def _():
            f = acc_ref[...]
            f_ref[...] = f
            y, _, _ = _rms(f, wpost_ref[...])
            hout = h_ref[...] + 0.5 * y
            if target is None:
                hout_ref[...] = hout
            else:
                sse = jnp.zeros((1, 1), F32)
                for k in range(nblk):
                    rows = slice(k * BLK, (k + 1) * BLK)
                    err = hout[rows] - t_refs[k][...]
                    if k == 0:
                        err = jnp.where(i > 0, err, 0.0)
                    hout_ref[rows, :] = err * (1.0 / D_MODEL)
                    sse = sse + jnp.sum(jnp.sum(err * err, axis=1, keepdims=True), axis=0, keepdims=True)
                sse_ref[...] += jnp.broadcast_to(sse, sse_ref.shape)

    row = pl.BlockSpec((tm, D_MODEL), lambda i, j: (i, 0))
    vec = pl.BlockSpec((1, D_MODEL), lambda i, j: (0, 0))
    wrow = pl.BlockSpec((None, fj, D_MODEL), lambda i, j: (j, 0, 0))
    act = pl.BlockSpec((None, tm, fj), lambda i, j: (j, i, 0))
    t_specs = [pl.BlockSpec((BLK, D_MODEL), functools.partial(lambda i, j, k: (jnp.maximum(nblk * i + k - 1, 0), 0), k=k))
               for k in range(nblk)]
    loss_spec = [_full((1, 128))] if target is not None else []
    loss_shape = [jax.ShapeDtypeStruct((1, 128), F32)] if target is not None else []
    return _pallas(
        body, name="ffn_fwd", grid=(t // tm, nj),
        in_specs=[row, vec, wrow, wrow, wrow, vec] + t_specs,
        out_specs=[row, row, act, act, act, row] + loss_spec,
        out_shape=[jax.ShapeDtypeStruct((t, D_MODEL), F32), jax.ShapeDtypeStruct((t, D_MODEL), BF16),
                   jax.ShapeDtypeStruct((nj, t, fj), BF16), jax.ShapeDtypeStruct((nj, t, fj), BF16),
                   jax.ShapeDtypeStruct((nj, t, fj), BF16), jax.ShapeDtypeStruct((t, D_MODEL), F32)] + loss_shape,
        scratch_shapes=[pltpu.VMEM((tm, D_MODEL), F32)],
        args=(h, wpre, wg4, wu4, wd4, wpost) + (target,) * nblk, hook=hook)


def _ffn_bwd(dhout, h, f, p14, p24, wpre, wg4, wu4, wd4, wpost, hook=None):
    t = h.shape[0]
    tm = _row_tile(t)
    nj, fj, _ = wg4.shape

    def body(dhout_ref, h_ref, f_ref, p1_ref, p2_ref, wpre_ref, wg_ref, wu_ref, wd_ref, wpost_ref,
             dh_ref, df_ref, dg_ref, du_ref, dwpre_ref, dwpost_ref, dn_ref):
        i = pl.program_id(0)
        j = pl.program_id(1)

        @pl.when((i == 0) & (j == 0))
        def _():
            dwpre_ref[...] = jnp.zeros_like(dwpre_ref)
            dwpost_ref[...] = jnp.zeros_like(dwpost_ref)

        @pl.when(j == 0)
        def _():
            wpost = wpost_ref[...]
            _, fh, r = _rms(f_ref[...], wpost)
            df, dw = _rms_bwd(fh, r, wpost, 0.5 * dhout_ref[...])
            dwpost_ref[...] += dw
            df_ref[...] = df.astype(BF16)
            dn_ref[...] = jnp.zeros_like(dn_ref)

        parts = _row_parts(tm)
        das = [_dg(df_ref[rows, :], wd_ref[...], NT) for rows in parts]
        for rows, da in zip(parts, das):
            dg = (da * p1_ref[rows, :].astype(F32)).astype(BF16)
            du = (da * p2_ref[rows, :].astype(F32)).astype(BF16)
            dg_ref[rows, :] = dg
            du_ref[rows, :] = du
            dn_ref[rows, :] += _dot(dg, wg_ref[...]) + _dot(du, wu_ref[...])

        @pl.when(j == nj - 1)
        def _():
            wpre = wpre_ref[...]
            _, hh, r = _rms(h_ref[...], wpre)
            dx, dw = _rms_bwd(hh, r, wpre, dn_ref[...])
            dwpre_ref[...] += dw
            dh_ref[...] = dhout_ref[...] + dx

    row = pl.BlockSpec((tm, D_MODEL), lambda i, j: (i, 0))
    vec = pl.BlockSpec((1, D_MODEL), lambda i, j: (0, 0))
    wrow = pl.BlockSpec((None, fj, D_MODEL), lambda i, j: (j, 0, 0))
    act = pl.BlockSpec((None, tm, fj), lambda i, j: (j, i, 0))
    actshape = jax.ShapeDtypeStruct((nj, t, fj), BF16)
    return _pallas(
        body, name="ffn_bwd", grid=(t // tm, nj),
        in_specs=[row, row, row, act, act, vec, wrow, wrow, wrow, vec],
        out_specs=[row, row, act, act, vec, vec],
        out_shape=[jax.ShapeDtypeStruct((t, D_MODEL), F32), jax.ShapeDtypeStruct((t, D_MODEL), BF16),
                   actshape, actshape,
                   jax.ShapeDtypeStruct((1, D_MODEL), F32), jax.ShapeDtypeStruct((1, D_MODEL), F32)],
        scratch_shapes=[pltpu.VMEM((tm, D_MODEL), F32)],
        args=(dhout, h, f, p14, p24, wpre, wg4, wu4, wd4, wpost), hook=hook)


def _ffn_wgrad(n, df, dg4, du4, a4, hook=None):
    t = n.shape[0]
    tm = _contract_tile(t)
    ni = t // tm
    nj, _, fj = dg4.shape

    def body(n_ref, df_ref, dg_ref, du_ref, a_ref, dw_ref, acc):
        i = pl.program_id(1)

        @pl.when(i == 0)
        def _():
            acc[...] = jnp.zeros_like(acc)

        nn = n_ref[...]
        acc[0:fj, :] += _dg(dg_ref[...], nn, TN)
        acc[fj:2 * fj, :] += _dg(du_ref[...], nn, TN)
        acc[2 * fj:3 * fj, :] += _dg(a_ref[...], df_ref[...], TN)

        @pl.when(i == ni - 1)
        def _():
            dw_ref[...] = acc[...].astype(BF16)

    row = pl.BlockSpec((tm, D_MODEL), lambda j, i: (i, 0))
    act = pl.BlockSpec((None, tm, fj), lambda j, i: (j, i, 0))
    return _pallas(
        body, name="ffn_wgrad", grid=(nj, ni),
        in_specs=[row, row, act, act, act],
        out_specs=[pl.BlockSpec((None, 3 * fj, D_MODEL), lambda j, i: (j, 0, 0))],
        out_shape=[jax.ShapeDtypeStruct((nj, 3 * fj, D_MODEL), BF16)],
        scratch_shapes=[pltpu.VMEM((3 * fj, D_MODEL), F32)],
        args=(n, df, dg4, du4, a4), hook=hook)


def _embed_norm(x, meta, w):
    t = x.shape[0] + BLK
    tm = _row_tile(t)
    nblk = tm // BLK

    def body(*refs):
        x_refs = refs[:nblk]
        meta_ref, w_ref, h_ref, n_ref = refs[nblk:]
        i = pl.program_id(0)
        first = jnp.concatenate([jnp.zeros((PAD, D_MODEL), F32), meta_ref[...]], axis=0)
        blocks = [jnp.where(i == 0, first, x_refs[0][...])] + [r[...] for r in x_refs[1:]]
        h = jnp.concatenate(blocks, axis=0) if nblk > 1 else blocks[0]
        h_ref[...] = h
        y, _, _ = _rms(h, w_ref[...])
        n_ref[...] = y.astype(BF16)

    x_specs = [pl.BlockSpec((BLK, D_MODEL), functools.partial(lambda i, k: (jnp.maximum(nblk * i + k - 1, 0), 0), k=k))
               for k in range(nblk)]
    row = pl.BlockSpec((tm, D_MODEL), lambda i: (i, 0))
    return pl.pallas_call(
        body, name="embed_norm", grid=(t // tm,),
        in_specs=x_specs + [_full((N_META, D_MODEL)), _full((1, D_MODEL))], out_specs=[row, row],
        out_shape=[jax.ShapeDtypeStruct((t, D_MODEL), F32), jax.ShapeDtypeStruct((t, D_MODEL), BF16)],
        compiler_params=_cparams(1),
    )(*([x] * nblk), meta, w)


FWD_RELATION = (None, 0, 1, 2)


def _ffn_fwd_gather(h, n, wbufs, wpost, qc_idx, late):
    t = h.shape[0]
    tm = _row_tile(t)
    ni = t // tm
    nj, fj, _ = wbufs[0].shape
    assert nj == N_CHIPS and ni >= 4
    nw = len(wbufs)
    n_lin, n_lout = len(late.inputs), len(late.out_shape)
    wait_step = ni - 3

    def body(qc_ref, h_ref, n_ref, wpost_ref, *rest):
        wb_in = rest[:nw]
        lins = rest[nw:nw + n_lin]
        o0 = nw + n_lin
        hout_ref, p1_ref, p2_ref, a_ref, f_hbm = rest[o0:o0 + 5]
        wb = rest[o0 + 5:o0 + 5 + nw]
        louts = rest[o0 + 5 + nw:o0 + 5 + nw + n_lout]
        s0 = o0 + 5 + nw + n_lout
        wv, wsem, send, recv, fbuf, fr_sem, fw_sem = rest[s0:s0 + 7]
        lscr = rest[s0 + 7:]
        p = pl.program_id(0)
        i = pl.program_id(1)
        step = p * ni + i
        fslot = step % 3
        nslot = (step + 1) % 3

        def f_tile(tile):
            return f_hbm.at[pl.ds(pl.multiple_of(tile * tm, 8), tm)]

        @pl.when(step > 1)
        def _():
            pltpu.make_async_copy(fbuf.at[nslot], f_tile(i), fw_sem.at[nslot]).wait()

        nxt = step + 1

        @pl.when((nxt < N_CHIPS * ni) & (nxt >= ni))
        def _():
            pltpu.make_async_copy(f_tile(nxt % ni), fbuf.at[nslot], fr_sem.at[nslot]).start()

        @pl.when(p > 0)
        def _():
            pltpu.make_async_copy(f_tile(i), fbuf.at[fslot], fr_sem.at[fslot]).wait()
        x, y, c, chips = _place()
        q = 2 * x + y
        sibling = (x, y, 1 - c)
        mine, other = _half(fj, c), _half(fj, 1 - c)

        def load(chunk, slot, src):
            return [pltpu.make_async_copy(src[t].at[chunk], wv.at[slot, t], wsem.at[slot, t]) for t in range(nw)]

        @pl.when((p == 0) & (i == 0))
        def _():
            for j, (cx, cy) in enumerate(chips):
                for t in range(nw):
                    _remote(send.at[t, j], recv.at[t, j], wb_in[t].at[q, mine], wb[t].at[q, mine], (cx, cy, c)).start()
            for cp in load(q, 0, wb_in):
                cp.start()
            for cp in load(q, 0, wb_in):
                cp.wait()

        @pl.when((p == 1) & (i == 0))
        def _():
            late.start(lins, louts, lscr)

        for pp in range(1, N_CHIPS):
            j = FWD_RELATION[pp]
            cx, cy = chips[j]
            chunk = 2 * cx + cy

            @pl.when((p == pp - 1) & (i == wait_step))
            def _(j=j, cx=cx, cy=cy, chunk=chunk, pp=pp):
                for t in range(nw):
                    got = wb[t].at[chunk, mine]
                    _remote(send.at[t, j], recv.at[t, j], got, got, (cx, cy, c)).wait_recv()
                    _remote(send.at[t, 3 + j], recv.at[t, 3 + j], got, got, sibling).start()
                for t in range(nw):
                    rest_half = wb[t].at[chunk, other]
                    _remote(send.at[t, 3 + j], recv.at[t, 3 + j], rest_half, rest_half, sibling).wait_recv()
                for cp in load(chunk, pp % 2, wb):
                    cp.start()

            @pl.when((p == pp) & (i == 0))
            def _(chunk=chunk, pp=pp):
                for cp in load(chunk, pp % 2, wb):
                    cp.wait()

        @pl.when((p == N_CHIPS - 1) & (i == ni // 2))
        def _():
            late.mid(lins, louts, lscr)

        slot = p % 2
        nn = n_ref[...]
        g = _dg(nn, wv[slot, 0], NT)
        u = _dg(nn, wv[slot, 1], NT)
        sg = _sigmoid(g)
        silu = g * sg
        p1_ref[...] = (u * (sg + silu * (1.0 - sg))).astype(BF16)
        p2_ref[...] = silu.astype(BF16)
        a = (silu * u).astype(BF16)
        a_ref[...] = a
        part = _dot(a, wv[slot, 2])

        @pl.when(p == 0)
        def _():
            fbuf[fslot] = part

        @pl.when(p > 0)
        def _():
            fbuf[fslot] = fbuf[fslot] + part

        pltpu.make_async_copy(fbuf.at[fslot], f_tile(i), fw_sem.at[fslot]).start()

        @pl.when(p == N_CHIPS - 1)
        def _():
            yv, _, _ = _rms(fbuf[fslot], wpost_ref[...])
            hout_ref[...] = h_ref[...] + 0.5 * yv

        @pl.when((p == N_CHIPS - 1) & (i == ni - 1))
        def _():
            pslot = (step + 2) % 3
            pltpu.make_async_copy(fbuf.at[pslot], f_tile(i), fw_sem.at[pslot]).wait()
            pltpu.make_async_copy(fbuf.at[fslot], f_tile(i), fw_sem.at[fslot]).wait()
            for t in range(nw):
                for j, (cx, cy) in enumerate(chips):
                    sent = wb[t].at[2 * cx + cy, mine]
                    _remote(send.at[t, j], recv.at[t, j], sent, sent, (cx, cy, c)).wait_send()
                    _remote(send.at[t, 3 + j], recv.at[t, 3 + j], sent, sent, sibling).wait_send()
            late.finish(lins, louts, lscr)

    def last_pass_rows(p, i, qc_ref):
        return (jnp.where(p == N_CHIPS - 1, i, 0), 0)

    def chunk_rows(p, i, qc_ref):
        order = ((p & 1) << 1) | (p >> 1)
        return (jnp.bitwise_xor(qc_ref[0], order), i, 0)

    row = pl.BlockSpec((tm, D_MODEL), lambda p, i, qc_ref: (i, 0))
    last_row = pl.BlockSpec((tm, D_MODEL), last_pass_rows)
    act = pl.BlockSpec((None, tm, fj), chunk_rows)
    act_shape = jax.ShapeDtypeStruct((nj, t, fj), BF16)
    res = pl.pallas_call(
        body, name="ffn_fwd_gather",
        grid_spec=pltpu.PrefetchScalarGridSpec(
            num_scalar_prefetch=1, grid=(N_CHIPS, ni),
            in_specs=[last_row, row, pl.BlockSpec((1, D_MODEL), lambda p, i, qc_ref: (0, 0))]
            + [ANY] * (nw + n_lin),
            out_specs=[last_row, act, act, act, ANY] + [ANY] * (nw + n_lout),
            scratch_shapes=[pltpu.VMEM((2, nw, fj, D_MODEL), BF16), pltpu.SemaphoreType.DMA((2, nw)),
                            pltpu.SemaphoreType.DMA((nw, 6)), pltpu.SemaphoreType.DMA((nw, 6)),
                            pltpu.VMEM((3, tm, D_MODEL), F32), pltpu.SemaphoreType.DMA((3,)),
                            pltpu.SemaphoreType.DMA((3,))] + list(late.scratch)),
        out_shape=[jax.ShapeDtypeStruct((t, D_MODEL), F32), act_shape, act_shape, act_shape,
                   jax.ShapeDtypeStruct((t, D_MODEL), F32)]
        + [jax.ShapeDtypeStruct(b.shape, b.dtype) for b in wbufs] + list(late.out_shape),
        input_output_aliases={**{4 + t: 5 + t for t in range(nw)},
                              **{4 + nw + a: 5 + nw + b for a, b in late.aliases}},
        compiler_params=_cparams(2),
    )(qc_idx, h, n, wpost, *wbufs, *late.inputs)
    return res[:5], res[5:5 + nw], res[5 + nw:]


PASS_RELATION = (2, 0, 1)


def _ffn_wgrad_reduce(n, df, dg4, du4, a4, qc_idx, hook):
    t = n.shape[0]
    tm = _contract_tile(t)
    ni = t // tm
    nj, _, fj = dg4.shape
    assert nj == N_CHIPS
    hrows = 3 * fj // 2
    n_hin, n_hout = len(hook.inputs), len(hook.out_shape)

    def body(qc_ref, n_ref, df_ref, dg_ref, du_ref, a_ref, *rest):
        hins = rest[:n_hin]
        own_ref, others_ref = rest[n_hin:n_hin + 2]
        houts = rest[n_hin + 2:n_hin + 2 + n_hout]
        s0 = n_hin + 2 + n_hout
        acc, stage, land, sumbuf, px_send, px_recv, cs_send, cs_recv, own_sem = rest[s0:s0 + 9]
        hscr = rest[s0 + 9:]
        k_pass = pl.program_id(0)
        i = pl.program_id(1)
        x, y, c, chips = _place()
        mine = pl.ds(pl.multiple_of(c * hrows, 8), hrows)
        other = pl.ds(pl.multiple_of((1 - c) * hrows, 8), hrows)

        def to_owner(k):
            j = PASS_RELATION[k]
            return _remote(cs_send.at[j], cs_recv.at[j], sumbuf.at[k % 2], others_ref.at[j], (*chips[j], c))

        @pl.when((k_pass == 0) & (i == 0))
        def _():
            hook.start(hins, houts, hscr)

        @pl.when(i == 0)
        def _():
            acc[...] = jnp.zeros_like(acc)

        nn = n_ref[...]
        acc[0:fj, :] += _dg(dg_ref[...], nn, TN)
        acc[fj:2 * fj, :] += _dg(du_ref[...], nn, TN)
        acc[2 * fj:3 * fj, :] += _dg(a_ref[...], df_ref[...], TN)

        for k in range(N_CHIPS):
            @pl.when((k_pass == k) & (i == ni - 1))
            def _(k=k):
                slot = k % 2
                stage[...] = acc[other, :].astype(BF16)
                swap = _remote(px_send.at[k], px_recv.at[k], stage, land.at[slot], (x, y, 1 - c))
                swap.start()
                swap.wait_recv()
                pair = acc[mine, :] + land[slot].astype(F32)
                if k >= 2:
                    to_owner(k - 2).wait_send()
                sumbuf[slot] = pair.astype(BF16)
                swap.wait_send()
                if k < N_CHIPS - 1:
                    to_owner(k).start()
                else:
                    keep = pltpu.make_async_copy(sumbuf.at[slot], own_ref, own_sem)
                    keep.start()
                    for j in range(N_CHIPS - 1):
                        _remote(cs_send.at[j], cs_recv.at[j], sumbuf.at[0], others_ref.at[j], (*chips[j], c)).wait_recv()
                    to_owner(k - 1).wait_send()
                    keep.wait()
                    hook.finish(hins, houts, hscr)

    def chunk(k_pass, i, qc_ref):
        return (jnp.bitwise_xor(qc_ref[0], N_CHIPS - 1 - k_pass), i, 0)

    row = pl.BlockSpec((tm, D_MODEL), lambda k_pass, i, qc_ref: (i, 0))
    act = pl.BlockSpec((None, tm, fj), chunk)
    res = pl.pallas_call(
        body, name="ffn_wgrad_reduce",
        grid_spec=pltpu.PrefetchScalarGridSpec(
            num_scalar_prefetch=1, grid=(N_CHIPS, ni),
            in_specs=[row, row, act, act, act] + [ANY] * n_hin,
            out_specs=[ANY, ANY] + [ANY] * n_hout,
            scratch_shapes=[pltpu.VMEM((3 * fj, D_MODEL), F32), pltpu.VMEM((hrows, D_MODEL), BF16),
                            pltpu.VMEM((2, hrows, D_MODEL), BF16), pltpu.VMEM((2, hrows, D_MODEL), BF16),
                            pltpu.SemaphoreType.DMA((N_CHIPS,)), pltpu.SemaphoreType.DMA((N_CHIPS,)),
                            pltpu.SemaphoreType.DMA((N_CHIPS - 1,)), pltpu.SemaphoreType.DMA((N_CHIPS - 1,)),
                            pltpu.SemaphoreType.DMA] + list(hook.scratch)),
        out_shape=[jax.ShapeDtypeStruct((hrows, D_MODEL), BF16),
                   jax.ShapeDtypeStruct((N_CHIPS - 1, hrows, D_MODEL), BF16)] + list(hook.out_shape),
        compiler_params=_cparams(2),
    )(qc_idx, n, df, dg4, du4, a4, *hook.inputs)
    return res[0], res[1], res[2:]


def _xty(x, y):
    t, k = x.shape
    n = y.shape[1]
    tm = _contract_tile(t)
    tn = n if n <= 1024 else (896 if n % 896 == 0 else 128)

    def body(x_ref, y_ref, o_ref):
        @pl.when(pl.program_id(1) == 0)
        def _():
            o_ref[...] = jnp.zeros_like(o_ref)

        o_ref[...] += _dg(x_ref[...], y_ref[...], TN)

    return pl.pallas_call(
        body, name="xty", grid=(n // tn, t // tm),
        in_specs=[pl.BlockSpec((tm, k), lambda j, i: (i, 0)), pl.BlockSpec((tm, tn), lambda j, i: (i, j))],
        out_specs=pl.BlockSpec((k, tn), lambda j, i: (0, j)),
        out_shape=jax.ShapeDtypeStruct((k, n), F32),
        compiler_params=_cparams(2),
    )(x, y)


def _rope_tables(t):
    pos = (jnp.arange(t, dtype=jnp.int32) - PAD).astype(F32)
    inv_freq = 1.0 / (ROPE_THETA ** (jnp.arange(0, SWA_HD, 2, dtype=F32) / SWA_HD))
    ang = pos[:, None] * inv_freq[None, :]
    cos = jnp.cos(ang)
    sin = jnp.sin(ang)
    return jnp.concatenate([cos, cos, cos, cos], axis=1), jnp.concatenate([-sin, sin, -sin, sin], axis=1)


def _rot_half(x, first_half):
    return jnp.where(first_half, pltpu.roll(x, 96, 1), pltpu.roll(x, 32, 1))


def _first_half_mask(rows):
    lane = lax.broadcasted_iota(jnp.int32, (rows, 128), 1)
    return (lane % 64) < 32


def _log_sigmoid(z):
    return jnp.minimum(z, 0.0) - jnp.log(1.0 + jnp.exp(-jnp.abs(z)))


def _mix_proj(h1, wmixpre, winp, wa2p, bap, cos, sin):
    t = h1.shape[0]
    tm = _row_tile(t)

    def body(h_ref, w_ref, win_ref, wa2_ref, ba_ref, cos_ref, sin_ref,
             n_ref, gq_ref, gk_ref, gv_ref, gg_ref, ga_ref, la_ref, sq_ref, sk_ref, sv_ref):
        y, _, _ = _rms(h_ref[...], w_ref[...])
        n = y.astype(BF16)
        n_ref[...] = n
        proj = _dot(n, win_ref[...])
        gq_ref[...] = proj[:, P_GQ:P_GK]
        gk_ref[...] = proj[:, P_GK:P_GV]
        gv_ref[...] = proj[:, P_GV:P_GG]
        gg_ref[...] = proj[:, P_GG:P_GA]
        ga = proj[:, P_GA:P_SQ]
        ga_ref[...] = ga
        z = _dot(ga.astype(BF16), wa2_ref[...]) + ba_ref[...]
        la_ref[...] = _log_sigmoid(z) * (1.0 / GLA_TAU)
        c = cos_ref[...]
        s = sin_ref[...]
        fh = _first_half_mask(tm)
        for k in range(4):
            x = proj[:, P_SQ + 128 * k:P_SQ + 128 * (k + 1)]
            sq_ref[:, 128 * k:128 * (k + 1)] = (x * c + _rot_half(x, fh) * s).astype(BF16)
        for k in range(2):
            x = proj[:, P_SK + 128 * k:P_SK + 128 * (k + 1)]
            sk_ref[:, 128 * k:128 * (k + 1)] = (x * c + _rot_half(x, fh) * s).astype(BF16)
        sv_ref[...] = proj[:, P_SV:P_END].astype(BF16)

    def row(w):
        return pl.BlockSpec((tm, w), lambda i: (i, 0))

    def rshape(w, dt):
        return jax.ShapeDtypeStruct((t, w), dt)

    return pl.pallas_call(
        body, name="mix_proj", grid=(t // tm,),
        in_specs=[row(D_MODEL), _full((1, D_MODEL)), _full((D_MODEL, P_END)), _full((128, GLA_KW)),
                  _full((1, GLA_KW)), row(128), row(128)],
        out_specs=[row(D_MODEL), row(256), row(256), row(512), row(512), row(128), row(256), row(512), row(256),
                   row(256)],
        out_shape=[rshape(D_MODEL, BF16), rshape(256, F32), rshape(256, F32), rshape(512, F32), rshape(512, F32),
                   rshape(128, F32), rshape(256, F32), rshape(512, BF16), rshape(256, BF16), rshape(256, BF16)],
        compiler_params=_cparams(1),
    )(h1, wmixpre, winp, wa2p, bap, cos, sin)


def _scan_rows(x, reverse=False):
    n = x.shape[0]
    row = lax.broadcasted_iota(jnp.int32, x.shape, 0)
    s = 1
    while s < n:
        if reverse:
            x = x + jnp.where(row < n - s, pltpu.roll(x, n - s, 0), 0.0)
        else:
            x = x + jnp.where(row >= s, pltpu.roll(x, s, 0), 0.0)
        s *= 2
    return x


def _gla_cumsum(la, tril_f):
    b = _scan_rows(la)
    row = lax.broadcasted_iota(jnp.int32, b.shape, 0)
    bm = jnp.sum(jnp.where(row == GLA_CHUNK // 2 - 1, b, 0.0), axis=0, keepdims=True)
    bl = jnp.sum(jnp.where(row == GLA_CHUNK - 1, b, 0.0), axis=0, keepdims=True)
    return b, bm, bl


def _gla_decays(la, tril_f):
    b, bm, bl = _gla_cumsum(la, tril_f)
    return jnp.exp(b - bm), jnp.exp(bm - b), jnp.exp(b), jnp.exp(bl - b), jnp.exp(bl)


def _gla_masks():
    c = GLA_CHUNK
    r = lax.broadcasted_iota(jnp.int32, (c, c), 0)
    col = lax.broadcasted_iota(jnp.int32, (c, c), 1)
    r4 = lax.broadcasted_iota(jnp.int32, (GLA_HEADS * c, c), 0) % c
    c4 = lax.broadcasted_iota(jnp.int32, (GLA_HEADS * c, c), 1)
    klane = lax.broadcasted_iota(jnp.int32, (c, GLA_KW), 1) // GLA_DK
    vlane = lax.broadcasted_iota(jnp.int32, (c, GLA_W), 1) // GLA_DV
    srow = lax.broadcasted_iota(jnp.int32, (GLA_W, GLA_KW), 0) // GLA_DV
    scol = lax.broadcasted_iota(jnp.int32, (GLA_W, GLA_KW), 1) // GLA_DK
    return dict(tril_f=(r >= col).astype(F32), triu_f=(r <= col).astype(F32), tril4=r4 >= c4,
                khead=[klane == h for h in range(GLA_HEADS)], vhead=[vlane == h for h in range(GLA_HEADS)],
                diag=srow == scol)


def _stack_heads(x, head_masks):
    return jnp.concatenate([jnp.where(m, x, 0.0) for m in head_masks], axis=0)


def _gla_fwd(gq, gk, gv, la):
    t = gq.shape[0]
    rg = _seq_tile(t)
    nb = t // rg
    ncb = rg // GLA_CHUNK
    c = GLA_CHUNK

    def body(q_ref, k_ref, v_ref, la_ref, o_ref, ss_ref, st_ref):
        @pl.when(pl.program_id(0) == 0)
        def _():
            st_ref[...] = jnp.zeros_like(st_ref)

        mk = _gla_masks()
        st = st_ref[...]
        for ch in range(ncb):
            rows = slice(ch * c, (ch + 1) * c)
            eq, ek, eb, ekl, ebl = _gla_decays(la_ref[rows, :], mk["tril_f"])
            qs = q_ref[rows, :] * (GLA_DK ** -0.5)
            k = k_ref[rows, :]
            v = v_ref[rows, :].astype(BF16)
            ss_ref[ch] = st
            q4 = _stack_heads(qs * eq, mk["khead"]).astype(BF16)
            a4 = jnp.where(mk["tril4"], _dg(q4, (k * ek).astype(BF16), NT), 0.0).astype(BF16)
            r4 = _dot(a4, v)
            intra = jnp.concatenate([r4[h * c:(h + 1) * c, GLA_DV * h:GLA_DV * (h + 1)] for h in range(GLA_HEADS)],
                                    axis=1)
            o_ref[rows, :] = intra + _dg((qs * eb).astype(BF16), st.astype(BF16), NT)
            st = st * ebl + jnp.where(mk["diag"], _dg(v, (k * ekl).astype(BF16), TN), 0.0)
        st_ref[...] = st

    def row(w):
        return pl.BlockSpec((rg, w), lambda i: (i, 0))

    return pl.pallas_call(
        body, name="gla_fwd", grid=(nb,),
        in_specs=[row(256), row(256), row(512), row(256)],
        out_specs=[row(512), pl.BlockSpec((ncb, GLA_W, GLA_KW), lambda i: (i, 0, 0))],
        out_shape=[jax.ShapeDtypeStruct((t, GLA_W), F32), jax.ShapeDtypeStruct((nb * ncb, GLA_W, GLA_KW), F32)],
        scratch_shapes=[pltpu.VMEM((GLA_W, GLA_KW), F32)],
        compiler_params=_cparams(1),
    )(gq, gk, gv, la)


def _gla_bwd(gq, gk, gv, la, ss, do):
    t = gq.shape[0]
    rg = _seq_tile(t)
    nb = t // rg
    ncb = rg // GLA_CHUNK
    c = GLA_CHUNK

    def body(q_ref, k_ref, v_ref, la_ref, ss_ref, do_ref, dq_ref, dk_ref, dv_ref, dla_ref, dst_ref):
        @pl.when(pl.program_id(0) == 0)
        def _():
            dst_ref[...] = jnp.zeros_like(dst_ref)

        mk = _gla_masks()
        last_row = lax.broadcasted_iota(jnp.int32, (c, GLA_KW), 0) == c - 1
        scale = GLA_DK ** -0.5
        dstn = dst_ref[...]
        for ch in reversed(range(ncb)):
            rows = slice(ch * c, (ch + 1) * c)
            eq, ek, eb, ekl, ebl = _gla_decays(la_ref[rows, :], mk["tril_f"])
            qs = q_ref[rows, :] * scale
            k = k_ref[rows, :]
            qt, kt, qh, kh = qs * eq, k * ek, qs * eb, k * ekl
            ktb, khb, qhb = kt.astype(BF16), kh.astype(BF16), qh.astype(BF16)
            v = v_ref[rows, :].astype(BF16)
            do_f = do_ref[rows, :]
            dob = do_f.astype(BF16)
            st = ss_ref[ch]
            stb = st.astype(BF16)
            dstb = dstn.astype(BF16)
            q4 = _stack_heads(qt, mk["khead"]).astype(BF16)
            do4 = _stack_heads(do_f, mk["vhead"]).astype(BF16)
            a4 = jnp.where(mk["tril4"], _dg(q4, ktb, NT), 0.0).astype(BF16)
            da4 = jnp.where(mk["tril4"], _dg(do4, v, NT), 0.0).astype(BF16)
            dv_ref[rows, :] = _dg(a4, do4, TN) + _dg(khb, dstb, NT)
            dq4 = _dot(da4, ktb)
            dqt = jnp.zeros((c, GLA_KW), F32)
            for h in range(GLA_HEADS):
                dqt = dqt + jnp.where(mk["khead"][h], dq4[h * c:(h + 1) * c], 0.0)
            dkt = _dg(da4, q4, TN)
            dqh = _dot(dob, stb)
            dkh = _dot(v, dstb)
            dbl = jnp.sum(dstn * st, axis=0, keepdims=True)
            dstn = dstn * ebl + jnp.where(mk["diag"], _dg(dob, qhb, TN), 0.0)
            dq_ref[rows, :] = scale * (dqt * eq + dqh * eb)
            dk_ref[rows, :] = dkt * ek + dkh * ekl
            dkk = dkh * kh
            db = dqt * qt - dkt * kt + dqh * qh - dkk
            db = db + jnp.where(last_row, jnp.sum(dkk, axis=0, keepdims=True) + ebl * dbl, 0.0)
            dla_ref[rows, :] = _scan_rows(db, reverse=True)
        dst_ref[...] = dstn

    def row(w):
        return pl.BlockSpec((rg, w), lambda i: (nb - 1 - i, 0))

    def rshape(w):
        return jax.ShapeDtypeStruct((t, w), F32)

    return pl.pallas_call(
        body, name="gla_bwd", grid=(nb,),
        in_specs=[row(256), row(256), row(512), row(256),
                  pl.BlockSpec((ncb, GLA_W, GLA_KW), lambda i: (nb - 1 - i, 0, 0)), row(512)],
        out_specs=[row(256), row(256), row(512), row(256)],
        out_shape=[rshape(256), rshape(256), rshape(512), rshape(256)],
        scratch_shapes=[pltpu.VMEM((GLA_W, GLA_KW), F32)],
        compiler_params=_cparams(1),
    )(gq, gk, gv, la, ss, do)


SWA_G = SWA_QH // SWA_KVH


def _swa_bias():
    n = jnp.arange(3, dtype=jnp.int32)[:, None, None]
    r = (jnp.arange(SWA_G * BLK, dtype=jnp.int32) % BLK)[None, :, None]
    c = jnp.arange(3 * BLK, dtype=jnp.int32)[None, None, :]
    seg = c // BLK
    cc = c % BLK
    qpos = n * BLK + r - PAD
    kpos = jnp.where(seg == 0, (n - 1) * BLK, jnp.where(seg == 1, n * BLK, 0)) + cc - PAD
    band = (seg < 2) & (kpos >= N_META) & (kpos <= qpos) & (qpos - kpos < WINDOW)
    meta = (seg == 2) & (kpos >= 0) & (kpos < N_META) & (kpos <= qpos)
    return jnp.where(band | meta, 0.0, NEG_INF).astype(F32)


def _swa_stack(ref, rows, kh, lo, dtype):
    parts = []
    for g in range(2):
        pair = ref[rows, 128 * (2 * kh + g):128 * (2 * kh + g + 1)]
        zero = jnp.zeros_like(pair)
        parts += [jnp.where(lo, pair, zero), jnp.where(lo, zero, pair)]
    return jnp.concatenate(parts, axis=0).astype(dtype)


def _swa_unstack(x4, lo):
    return [jnp.where(lo, x4[2 * g * BLK:(2 * g + 1) * BLK], x4[(2 * g + 1) * BLK:(2 * g + 2) * BLK])
            for g in range(2)]


def _swa_sink_col(sink_ref, kh):
    blk = lax.broadcasted_iota(jnp.int32, (SWA_G * BLK, 1), 0) // BLK
    col = jnp.full((SWA_G * BLK, 1), sink_ref[SWA_G * kh + SWA_G - 1], F32)
    for e in reversed(range(SWA_G - 1)):
        col = jnp.where(blk == e, sink_ref[SWA_G * kh + e], col)
    return col


def _swa_probs(q4, kall, bias, sink):
    return _swa_softmax(_dg(q4, kall, NT), bias, sink)


def _swa_softmax(qk, bias, sink):
    s = qk * (SWA_HD ** -0.5) + bias
    m = jnp.maximum(jnp.max(s, axis=-1, keepdims=True), sink)
    p = jnp.exp(s - m)
    es = jnp.exp(sink - m)
    inv = 1.0 / (jnp.sum(p, axis=-1, keepdims=True) + es)
    return p * inv, es * inv


def _swa_keys(prev_ref, cur_ref, first_ref, b, ls):
    before = prev_ref[:, ls] if b == 0 else cur_ref[(b - 1) * BLK:b * BLK, ls]
    return jnp.concatenate([before, cur_ref[b * BLK:(b + 1) * BLK, ls], first_ref[:, ls]], axis=0)


def _swa_specs(rs, ns):
    bps = rs // BLK
    cur = lambda w: pl.BlockSpec((rs, w), lambda i: (jnp.minimum(i, ns - 1), 0))
    prev = lambda w: pl.BlockSpec((BLK, w), lambda i: (jnp.maximum(jnp.minimum(i, ns - 1) * bps - 1, 0), 0))
    first = lambda w: pl.BlockSpec((BLK, w), lambda i: (0, 0))
    return cur, prev, first


def _swa_fwd(sinks, sq, sk, sv):
    t = sq.shape[0]
    rs = _seq_tile(t)
    bps, ns = rs // BLK, t // rs

    def body(sink_ref, bias_ref, q_ref, kp_ref, kc_ref, km_ref, vp_ref, vc_ref, vm_ref, o_ref):
        i = pl.program_id(0)
        lo = lax.broadcasted_iota(jnp.int32, (BLK, 128), 1) < 64
        sink_cols = [_swa_sink_col(sink_ref, kh) for kh in range(SWA_KVH)]
        chains = [(b, kh) for b in range(bps) for kh in range(SWA_KVH)]
        scores = []
        for b, kh in chains:
            ls = slice(128 * kh, 128 * (kh + 1))
            q4 = _swa_stack(q_ref, slice(b * BLK, (b + 1) * BLK), kh, lo, BF16)
            scores.append(_dg(q4, _swa_keys(kp_ref, kc_ref, km_ref, b, ls), NT))
        probs = []
        for (b, kh), s in zip(chains, scores):
            p, _ = _swa_softmax(s, bias_ref[jnp.minimum(i * bps + b, 2)], sink_cols[kh])
            probs.append(p.astype(BF16))
        for (b, kh), p in zip(chains, probs):
            ls = slice(128 * kh, 128 * (kh + 1))
            rows = slice(b * BLK, (b + 1) * BLK)
            for g, pair in enumerate(_swa_unstack(_dot(p, _swa_keys(vp_ref, vc_ref, vm_ref, b, ls)), lo)):
                o_ref[rows, 128 * (2 * kh + g):128 * (2 * kh + g + 1)] = pair

    cur, prev, first = _swa_specs(rs, ns)
    bias = _swa_bias()
    return pl.pallas_call(
        body, name="swa_fwd", grid=(ns,),
        in_specs=[pl.BlockSpec(memory_space=pltpu.SMEM), _full(bias.shape), cur(512), prev(256), cur(256), first(256),
                  prev(256), cur(256), first(256)],
        out_specs=cur(512),
        out_shape=jax.ShapeDtypeStruct((t, SWA_W), F32),
        compiler_params=_cparams(1),
    )(sinks, bias, sq, sk, sk, sk, sv, sv, sv)


def _swa_bwd(sinks, sq, sk, sv, o, do, hook=None):
    t = sq.shape[0]
    rs = _seq_tile(t)
    bps, ns = rs // BLK, t // rs

    def body(sink_ref, bias_ref, q_ref, kp_ref, kc_ref, km_ref, vp_ref, vc_ref, vm_ref, o_ref, do_ref,
             dq_ref, dk_ref, dv_ref, dkm_ref, dvm_ref, dsink_ref, pk_ref, pv_ref):
        i = pl.program_id(0)

        @pl.when(i == 0)
        def _():
            pk_ref[...] = jnp.zeros_like(pk_ref)
            pv_ref[...] = jnp.zeros_like(pv_ref)
            dkm_ref[...] = jnp.zeros_like(dkm_ref)
            dvm_ref[...] = jnp.zeros_like(dvm_ref)
            dsink_ref[...] = jnp.zeros_like(dsink_ref)

        @pl.when(i == ns)
        def _():
            dk_ref[...] = pk_ref[...]
            dv_ref[...] = pv_ref[...]

        @pl.when(i < ns)
        def _():
            lo = lax.broadcasted_iota(jnp.int32, (BLK, 128), 1) < 64
            scale = SWA_HD ** -0.5
            sink_cols = [_swa_sink_col(sink_ref, kh) for kh in range(SWA_KVH)]
            parts_k = [[None] * SWA_KVH for _ in range(bps)]
            parts_v = [[None] * SWA_KVH for _ in range(bps)]
            dsinks = [jnp.zeros((1, 1), F32) for _ in range(SWA_QH)]
            chains = [(b, kh) for b in range(bps) for kh in range(SWA_KVH)]
            lanes = lambda kh: slice(128 * kh, 128 * (kh + 1))
            block = lambda b: slice(b * BLK, (b + 1) * BLK)
            q4s = [_swa_stack(q_ref, block(b), kh, lo, BF16) for b, kh in chains]
            scores = [_dg(q4, _swa_keys(kp_ref, kc_ref, km_ref, b, lanes(kh)), NT)
                      for (b, kh), q4 in zip(chains, q4s)]
            do4s = [_swa_stack(do_ref, block(b), kh, lo, F32) for b, kh in chains]
            do4bs = [d.astype(BF16) for d in do4s]
            dps = [_dg(d, _swa_keys(vp_ref, vc_ref, vm_ref, b, lanes(kh)), NT) for (b, kh), d in zip(chains, do4bs)]
            pbs, dss = [], []
            for n_chain, (b, kh) in enumerate(chains):
                p, psink = _swa_softmax(scores[n_chain], bias_ref[jnp.minimum(i * bps + b, 2)], sink_cols[kh])
                delta = jnp.sum(do4s[n_chain] * _swa_stack(o_ref, block(b), kh, lo, F32), axis=-1, keepdims=True)
                dss.append((p * (dps[n_chain] - delta) * scale).astype(BF16))
                pbs.append(p.astype(BF16))
                dsk = psink * delta
                for e in range(SWA_G):
                    h = SWA_G * kh + e
                    dsinks[h] = dsinks[h] - jnp.sum(dsk[e * BLK:(e + 1) * BLK], axis=0, keepdims=True)
            for n_chain, (b, kh) in enumerate(chains):
                kall = _swa_keys(kp_ref, kc_ref, km_ref, b, lanes(kh))
                for g, pair in enumerate(_swa_unstack(_dot(dss[n_chain], kall), lo)):
                    dq_ref[block(b), 128 * (2 * kh + g):128 * (2 * kh + g + 1)] = pair
                parts_k[b][kh] = _dg(dss[n_chain], q4s[n_chain], TN)
                parts_v[b][kh] = _dg(pbs[n_chain], do4bs[n_chain], TN)
            last = slice(rs - BLK, rs)
            for parts, out_ref, pend_ref, meta_ref in ((parts_k, dk_ref, pk_ref, dkm_ref),
                                                       (parts_v, dv_ref, pv_ref, dvm_ref)):
                for kh in range(SWA_KVH):
                    ls = slice(128 * kh, 128 * (kh + 1))
                    if bps > 1:
                        out_ref[0:rs - BLK, ls] = pend_ref[0:rs - BLK, ls]
                    out_ref[last, ls] = pend_ref[last, ls] + parts[0][kh][0:BLK]
                    meta = parts[0][kh][2 * BLK:3 * BLK]
                    for b in range(bps):
                        own = parts[b][kh][BLK:2 * BLK]
                        if b + 1 < bps:
                            own = own + parts[b + 1][kh][0:BLK]
                            meta = meta + parts[b + 1][kh][2 * BLK:3 * BLK]
                        pend_ref[b * BLK:(b + 1) * BLK, ls] = own
                    meta_ref[:, ls] += meta
            for h in range(SWA_QH):
                dsink_ref[h:h + 1, :] += jnp.broadcast_to(dsinks[h], (1, 128))

    cur, prev, first = _swa_specs(rs, ns)
    late = lambda w: pl.BlockSpec((rs, w), lambda i: (jnp.maximum(i - 1, 0), 0))
    bias = _swa_bias()
    return _pallas(
        body, name="swa_bwd", grid=(ns + 1,),
        in_specs=[pl.BlockSpec(memory_space=pltpu.SMEM), _full(bias.shape), cur(512), prev(256), cur(256), first(256),
                  prev(256), cur(256), first(256), cur(512), cur(512)],
        out_specs=[cur(512), late(256), late(256), first(256), first(256), _full((SWA_QH, 128))],
        out_shape=[jax.ShapeDtypeStruct((t, SWA_W), F32), jax.ShapeDtypeStruct((t, 256), F32),
                   jax.ShapeDtypeStruct((t, 256), F32), jax.ShapeDtypeStruct((BLK, 256), F32),
                   jax.ShapeDtypeStruct((BLK, 256), F32), jax.ShapeDtypeStruct((SWA_QH, 128), F32)],
        scratch_shapes=[pltpu.VMEM((rs, 256), F32), pltpu.VMEM((rs, 256), F32)],
        args=(sinks, bias, sq, sk, sk, sk, sv, sv, sv, o, do), hook=hook)


def _mix_out(h1, ogla, gg, oswa, wgn, wsn, wout, wpost):
    t = h1.shape[0]
    tm = _row_tile(t)

    def body(h_ref, og_ref, gg_ref, os_ref, wgn_ref, wsn_ref, wout_ref, wpost_ref, h2_ref, cat_ref, m_ref):
        parts = []
        for h in range(GLA_HEADS):
            ls = slice(GLA_DV * h, GLA_DV * (h + 1))
            y, _, _ = _rms(og_ref[:, ls], wgn_ref[...])
            g = gg_ref[:, ls]
            parts.append(y * (g * _sigmoid(g)))
        ys, _, _ = _rms(os_ref[...], wsn_ref[...])
        cat = jnp.concatenate(parts + [ys], axis=1).astype(BF16)
        cat_ref[...] = cat
        m = _dot(cat, wout_ref[...])
        m_ref[...] = m
        y, _, _ = _rms(m, wpost_ref[...])
        h2_ref[...] = h_ref[...] + y

    def row(w):
        return pl.BlockSpec((tm, w), lambda i: (i, 0))

    return pl.pallas_call(
        body, name="mix_out", grid=(t // tm,),
        in_specs=[row(D_MODEL), row(512), row(512), row(512), _full((1, GLA_DV)), _full((1, SWA_W)),
                  _full((D_MODEL, D_MODEL)), _full((1, D_MODEL))],
        out_specs=[row(D_MODEL), row(D_MODEL), row(D_MODEL)],
        out_shape=[jax.ShapeDtypeStruct((t, D_MODEL), F32), jax.ShapeDtypeStruct((t, D_MODEL), BF16),
                   jax.ShapeDtypeStruct((t, D_MODEL), F32)],
        compiler_params=_cparams(1),
    )(h1, ogla, gg, oswa, wgn, wsn, wout, wpost)


def _mix_out_bwd(dh2, m, ogla, gg, oswa, wgn, wsn, wout, wpost, hook=None):
    t = dh2.shape[0]
    tm = _row_tile(t)

    def body(dh_ref, m_ref, og_ref, gg_ref, os_ref, wgn_ref, wsn_ref, wout_ref, wpost_ref,
             dog_ref, dgg_ref, dos_ref, dm_ref, dwpost_ref, dwgn_ref, dwsn_ref):
        @pl.when(pl.program_id(0) == 0)
        def _():
            dwpost_ref[...] = jnp.zeros_like(dwpost_ref)
            dwgn_ref[...] = jnp.zeros_like(dwgn_ref)
            dwsn_ref[...] = jnp.zeros_like(dwsn_ref)

        wpost = wpost_ref[...]
        _, mh, r = _rms(m_ref[...], wpost)
        dm, dw = _rms_bwd(mh, r, wpost, dh_ref[...])
        dwpost_ref[...] += dw
        dmb = dm.astype(BF16)
        dm_ref[...] = dmb
        dcat = _dg(dmb, wout_ref[...], NT)
        wgn = wgn_ref[...]
        for h in range(GLA_HEADS):
            ls = slice(GLA_DV * h, GLA_DV * (h + 1))
            dog = dcat[:, ls]
            g = gg_ref[:, ls]
            sg = _sigmoid(g)
            y, xh, r = _rms(og_ref[:, ls], wgn)
            dgg_ref[:, ls] = dog * y * (sg * (1.0 + g * (1.0 - sg)))
            dx, dw = _rms_bwd(xh, r, wgn, dog * (g * sg))
            dog_ref[:, ls] = dx
            dwgn_ref[...] += dw
        wsn = wsn_ref[...]
        _, xh, r = _rms(os_ref[...], wsn)
        dx, dw = _rms_bwd(xh, r, wsn, dcat[:, GLA_W:])
        dos_ref[...] = dx
        dwsn_ref[...] += dw

    def row(w):
        return pl.BlockSpec((tm, w), lambda i: (i, 0))

    def rshape(w, dt=F32):
        return jax.ShapeDtypeStruct((t, w), dt)

    return _pallas(
        body, name="mix_out_bwd", grid=(t // tm,),
        in_specs=[row(D_MODEL), row(D_MODEL), row(512), row(512), row(512), _full((1, GLA_DV)), _full((1, SWA_W)),
                  _full((D_MODEL, D_MODEL)), _full((1, D_MODEL))],
        out_specs=[row(512), row(512), row(512), row(D_MODEL), _full((1, D_MODEL)), _full((1, GLA_DV)),
                   _full((1, SWA_W))],
        out_shape=[rshape(512), rshape(512), rshape(512), rshape(D_MODEL, BF16),
                   jax.ShapeDtypeStruct((1, D_MODEL), F32), jax.ShapeDtypeStruct((1, GLA_DV), F32),
                   jax.ShapeDtypeStruct((1, SWA_W), F32)],
        args=(dh2, m, ogla, gg, oswa, wgn, wsn, wout, wpost), hook=hook)


def _mix_in_bwd(dh2, h1, wmixpre, winp, wa2p, bap, cos, sin, ga, dgq, dgk, dgv, dgg, dla, dsq, dsk, dsv, dkm, dvm):
    t = h1.shape[0]
    tm = _row_tile(t)

    def body(dh2_ref, h_ref, w_ref, win_ref, wa2_ref, ba_ref, cos_ref, sin_ref, ga_ref, dgq_ref, dgk_ref, dgv_ref,
             dgg_ref, dla_ref, dsq_ref, dsk_ref, dsv_ref, dkm_ref, dvm_ref,
             dh1_ref, dproj_ref, dw_ref, dwa2_ref, dba_ref):
        i = pl.program_id(0)

        @pl.when(i == 0)
        def _():
            dw_ref[...] = jnp.zeros_like(dw_ref)
            dwa2_ref[...] = jnp.zeros_like(dwa2_ref)
            dba_ref[...] = jnp.zeros_like(dba_ref)

        first = (i == 0).astype(F32)
        c = cos_ref[...]
        s = -sin_ref[...]
        fh = _first_half_mask(tm)
        dproj_ref[:, P_GQ:P_GK] = dgq_ref[...].astype(BF16)
        dproj_ref[:, P_GK:P_GV] = dgk_ref[...].astype(BF16)
        dproj_ref[:, P_GV:P_GG] = dgv_ref[...].astype(BF16)
        dproj_ref[:, P_GG:P_GA] = dgg_ref[...].astype(BF16)
        gab = ga_ref[...].astype(BF16)
        z = _dot(gab, wa2_ref[...]) + ba_ref[...]
        row_id = i * tm + lax.broadcasted_iota(jnp.int32, (tm, 1), 0)
        dz = jnp.where(row_id >= PAD, dla_ref[...] * (1.0 / GLA_TAU) * (1.0 - _sigmoid(z)), 0.0)
        dzb = dz.astype(BF16)
        dba_ref[...] += jnp.sum(dz, axis=0, keepdims=True)
        dwa2_ref[...] += _dg(gab, dzb, TN)
        dproj_ref[:, P_GA:P_SQ] = _dg(dzb, wa2_ref[...], NT).astype(BF16)
        for k in range(4):
            dy = dsq_ref[:, 128 * k:128 * (k + 1)]
            dproj_ref[:, P_SQ + 128 * k:P_SQ + 128 * (k + 1)] = (dy * c + _rot_half(dy, fh) * s).astype(BF16)
        for k in range(2):
            ls = slice(128 * k, 128 * (k + 1))
            dy = dsk_ref[:, ls]
            dy = jnp.concatenate([dy[:BLK] + first * dkm_ref[:, ls], dy[BLK:]], axis=0) if tm > BLK else (
                dy + first * dkm_ref[:, ls])
            dproj_ref[:, P_SK + 128 * k:P_SK + 128 * (k + 1)] = (dy * c + _rot_half(dy, fh) * s).astype(BF16)
            dv = dsv_ref[:, ls]
            dv = jnp.concatenate([dv[:BLK] + first * dvm_ref[:, ls], dv[BLK:]], axis=0) if tm > BLK else (
                dv + first * dvm_ref[:, ls])
            dproj_ref[:, P_SV + 128 * k:P_SV + 128 * (k + 1)] = dv.astype(BF16)
        dn = _dg(dproj_ref[...], win_ref[...], NT)
        w = w_ref[...]
        _, hh, r = _rms(h_ref[...], w)
        dx, dw = _rms_bwd(hh, r, w, dn)
        dw_ref[...] += dw
        dh1_ref[...] = dh2_ref[...] + dx

    def row(w):
        return pl.BlockSpec((tm, w), lambda i: (i, 0))

    return pl.pallas_call(
        body, name="mix_in_bwd", grid=(t // tm,),
        in_specs=[row(D_MODEL), row(D_MODEL), _full((1, D_MODEL)), _full((D_MODEL, P_END)), _full((128, GLA_KW)),
                  _full((1, GLA_KW)), row(128), row(128), row(128), row(256), row(256), row(512), row(512), row(256),
                  row(512), row(256), row(256), _full((BLK, 256)), _full((BLK, 256))],
        out_specs=[row(D_MODEL), row(P_END), _full((1, D_MODEL)), _full((128, GLA_KW)), _full((1, GLA_KW))],
        out_shape=[jax.ShapeDtypeStruct((t, D_MODEL), F32), jax.ShapeDtypeStruct((t, P_END), BF16),
                   jax.ShapeDtypeStruct((1, D_MODEL), F32), jax.ShapeDtypeStruct((128, GLA_KW), F32),
                   jax.ShapeDtypeStruct((1, GLA_KW), F32)],
        compiler_params=_cparams(1),
    )(dh2, h1, wmixpre, winp, wa2p, bap, cos, sin, ga, dgq, dgk, dgv, dgg, dla, dsq, dsk, dsv, dkm, dvm)


def _adamw_update(w, g, m, v):
    m = ADAM_B1 * m + (1.0 - ADAM_B1) * g
    v = ADAM_B2 * v + (1.0 - ADAM_B2) * (g * g)
    m_hat = m / (1.0 - ADAM_B1 ** ADAM_STEP)
    v_hat = v / (1.0 - ADAM_B2 ** ADAM_STEP)
    return -ADAM_LR * (m_hat / (jnp.sqrt(v_hat) + ADAM_EPS) + ADAM_WD * w), m, v


def _adamw(w, g, m, v):
    r, c = w.shape
    tr = _div_tile(r)

    def body(w_ref, g_ref, m_ref, v_ref, d_ref, nm_ref, nv_ref):
        d_ref[...], nm_ref[...], nv_ref[...] = _adamw_update(w_ref[...], g_ref[...], m_ref[...], v_ref[...])

    spec = pl.BlockSpec((tr, c), lambda i: (i, 0))
    shape = jax.ShapeDtypeStruct((r, c), F32)
    return pl.pallas_call(
        body, name="adamw", grid=(r // tr,), in_specs=[spec] * 4, out_specs=[spec] * 3, out_shape=[shape] * 3,
        compiler_params=_cparams(1),
    )(w, g, m, v)


def _adamw_halves(w, g_mine, g_other, m, v, c_idx, row0=0):
    r, c = w.shape
    h = g_mine.shape[0]
    tr = _div_tile(math.gcd(r, h))
    nth = h // tr
    t0 = row0 // tr
    assert t0 * tr == row0

    def body(c_ref, w_ref, gm_ref, go_ref, m_ref, v_ref, g_ref, d_ref, nm_ref, nv_ref):
        hh = (t0 + pl.program_id(0)) // nth
        g = jnp.where(hh == c_ref[0], gm_ref[...], go_ref[...])
        g_ref[...] = g
        d_ref[...], nm_ref[...], nv_ref[...] = _adamw_update(w_ref[...], g, m_ref[...], v_ref[...])

    spec = pl.BlockSpec((tr, c), lambda i, c_ref: (i, 0))
    gspec = pl.BlockSpec((tr, c), lambda i, c_ref: ((t0 + i) % nth, 0))
    shape = jax.ShapeDtypeStruct((r, c), F32)
    return pl.pallas_call(
        body, name="adamw_halves",
        grid_spec=pltpu.PrefetchScalarGridSpec(
            num_scalar_prefetch=1, grid=(r // tr,), in_specs=[spec, gspec, gspec, spec, spec], out_specs=[spec] * 4),
        out_shape=[shape] * 4, compiler_params=_cparams(1),
    )(c_idx, w, g_mine, g_other, m, v)


def _place():
    x, y, c = lax.axis_index("x"), lax.axis_index("y"), lax.axis_index("c")
    chips = [(1 - x, y), (x, 1 - y), (1 - x, 1 - y)]
    return x, y, c, chips


def _remote(send_sem, recv_sem, src, dst, to):
    return pltpu.make_async_remote_copy(src_ref=src, dst_ref=dst, send_sem=send_sem, recv_sem=recv_sem,
                                        device_id=to, device_id_type=MESH)


def _half(ref_rows, c):
    h = ref_rows // 2
    return pl.ds(pl.multiple_of(c * h, 8), h)


def _own_slot(shard, q):
    return lax.dynamic_update_slice(jnp.zeros((N_CHIPS,) + shard.shape, shard.dtype), shard[None], (q, 0, 0))


class _GatherChips:
    has_mid = True

    def __init__(self, bufs):
        n = len(bufs)
        self.inputs = list(bufs)
        self.out_shape = [jax.ShapeDtypeStruct(b.shape, b.dtype) for b in bufs]
        self.aliases = [(t, t) for t in range(n)]
        self.scratch = [pltpu.SemaphoreType.DMA((n, 6)), pltpu.SemaphoreType.DMA((n, 6))]

    def start(self, ins, outs, scr):
        send, recv = scr
        x, y, c, chips = _place()
        q = 2 * x + y
        for t, (i_ref, o_ref) in enumerate(zip(ins, outs)):
            rows = _half(i_ref.shape[1], c)
            for j, (cx, cy) in enumerate(chips):
                _remote(send.at[t, j], recv.at[t, j], i_ref.at[q, rows], o_ref.at[q, rows], (cx, cy, c)).start()

    def mid(self, ins, outs, scr):
        send, recv = scr
        x, y, c, chips = _place()
        for t, o_ref in enumerate(outs):
            rows = _half(o_ref.shape[1], c)
            for j, (cx, cy) in enumerate(chips):
                slot = o_ref.at[2 * cx + cy, rows]
                _remote(send.at[t, j], recv.at[t, j], slot, slot, (cx, cy, c)).wait_recv()
                _remote(send.at[t, 3 + j], recv.at[t, 3 + j], slot, slot, (x, y, 1 - c)).start()

    def finish(self, ins, outs, scr):
        send, recv = scr
        x, y, c, chips = _place()
        for t, o_ref in enumerate(outs):
            mine, other = _half(o_ref.shape[1], c), _half(o_ref.shape[1], 1 - c)
            for j, (cx, cy) in enumerate(chips):
                slot = o_ref.at[2 * cx + cy, other]
                _remote(send.at[t, 3 + j], recv.at[t, 3 + j], slot, slot, (x, y, 1 - c)).wait_recv()
            for j, (cx, cy) in enumerate(chips):
                sent = o_ref.at[2 * cx + cy, mine]
                _remote(send.at[t, j], recv.at[t, j], sent, sent, (cx, cy, c)).wait_send()
                _remote(send.at[t, 3 + j], recv.at[t, 3 + j], sent, sent, (x, y, 1 - c)).wait_send()


class _PairExchange:
    has_mid = False
    aliases = ()

    def __init__(self, arrs):
        n = len(arrs)
        self.inputs = list(arrs)
        self.out_shape = [jax.ShapeDtypeStruct((a.shape[0], a.shape[1] // 2, a.shape[2]), a.dtype) for a in arrs]
        self.scratch = [pltpu.SemaphoreType.DMA((n,)), pltpu.SemaphoreType.DMA((n,))]

    def _copies(self, ins, outs, scr):
        send, recv = scr
        x, y, c, _ = _place()
        return [_remote(send.at[t], recv.at[t], i_ref.at[:, _half(i_ref.shape[1], 1 - c)], o_ref, (x, y, 1 - c))
                for t, (i_ref, o_ref) in enumerate(zip(ins, outs))]

    def start(self, ins, outs, scr):
        for cp in self._copies(ins, outs, scr):
            cp.start()

    def finish(self, ins, outs, scr):
        for cp in self._copies(ins, outs, scr):
            cp.wait()


class _ChipScatter:
    has_mid = False
    aliases = ()

    def __init__(self, arrs):
        n = len(arrs)
        self.inputs = list(arrs)
        self.out_shape = [jax.ShapeDtypeStruct((3,) + a.shape[1:], a.dtype) for a in arrs]
        self.scratch = [pltpu.SemaphoreType.DMA((n, 3)), pltpu.SemaphoreType.DMA((n, 3))]

    def _copies(self, ins, outs, scr):
        send, recv = scr
        x, y, c, chips = _place()
        return [_remote(send.at[t, j], recv.at[t, j], i_ref.at[2 * cx + cy], o_ref.at[j], (cx, cy, c))
                for t, (i_ref, o_ref) in enumerate(zip(ins, outs)) for j, (cx, cy) in enumerate(chips)]

    def start(self, ins, outs, scr):
        for cp in self._copies(ins, outs, scr):
            cp.start()

    def finish(self, ins, outs, scr):
        for cp in self._copies(ins, outs, scr):
            cp.wait()


class _PairShare:
    has_mid = False
    aliases = ()

    def __init__(self, arrs):
        n = len(arrs)
        self.inputs = list(arrs)
        self.out_shape = [jax.ShapeDtypeStruct(a.shape, a.dtype) for a in arrs]
        self.scratch = [pltpu.SemaphoreType.DMA((n,)), pltpu.SemaphoreType.DMA((n,))]

    def _copies(self, ins, outs, scr):
        send, recv = scr
        x, y, c, _ = _place()
        return [_remote(send.at[t], recv.at[t], i_ref, o_ref, (x, y, 1 - c))
                for t, (i_ref, o_ref) in enumerate(zip(ins, outs))]

    def start(self, ins, outs, scr):
        for cp in self._copies(ins, outs, scr):
            cp.start()

    def finish(self, ins, outs, scr):
        for cp in self._copies(ins, outs, scr):
            cp.wait()


def _comm_call(hook, name):
    n_in, n_out = len(hook.inputs), len(hook.out_shape)

    def body(*refs):
        ins, outs, scr = refs[:n_in], refs[n_in:n_in + n_out], refs[n_in + n_out:]
        hook.start(ins, outs, scr)
        if hook.has_mid:
            hook.mid(ins, outs, scr)
        hook.finish(ins, outs, scr)

    return pl.pallas_call(body, name=name, in_specs=[ANY] * n_in, out_specs=[ANY] * n_out,
                          out_shape=list(hook.out_shape), scratch_shapes=list(hook.scratch),
                          input_output_aliases=dict(hook.aliases))(*hook.inputs)


def _all_gather_devices(vecs):
    n = len(vecs)

    def body(*refs):
        x_refs, out_refs = refs[:n], refs[n:2 * n]
        send_sems, recv_sems, local_sems = refs[2 * n:]
        x, y, c, chips = _place()
        me, sibling = (x, y, c), (x, y, 1 - c)
        waits = []
        for t, (x_ref, out_ref) in enumerate(zip(x_refs, out_refs)):
            def slot(px, py, pc, out_ref=out_ref):
                return out_ref.at[4 * px + 2 * py + pc]

            def copy(k, block, to, src=None, t=t, slot=slot):
                return pltpu.make_async_remote_copy(
                    src_ref=slot(*block) if src is None else src, dst_ref=slot(*block), send_sem=send_sems.at[t, k],
                    recv_sem=recv_sems.at[t, k], device_id=to, device_id_type=MESH)

            mine = pltpu.make_async_copy(x_ref, slot(*me), local_sems.at[t])
            mine.start()
            first = [copy(0, me, sibling, src=x_ref)]
            first += [copy(1 + j, me, (*chip, c), src=x_ref) for j, chip in enumerate(chips)]
            for cp in first:
                cp.start()
            waits.append((copy, mine, first))
        for copy, mine, first in waits:
            passed = [copy(4 + j, (*chip, c), sibling) for j, chip in enumerate(chips)]
            for j, chip in enumerate(chips):
                copy(1 + j, (*chip, c), me).wait_recv()
                passed[j].start()
            copy(0, sibling, me).wait_recv()
            for j, chip in enumerate(chips):
                copy(4 + j, (*chip, 1 - c), me).wait_recv()
            for cp in first + passed:
                cp.wait_send()
            mine.wait()

    vmem = pl.BlockSpec(memory_space=pltpu.VMEM)
    return pl.pallas_call(
        body, name="all_gather_devices", in_specs=[vmem] * n, out_specs=[vmem] * n,
        out_shape=[jax.ShapeDtypeStruct((N_DEV,) + v.shape, v.dtype) for v in vecs],
        scratch_shapes=[pltpu.SemaphoreType.DMA((n, 7)), pltpu.SemaphoreType.DMA((n, 7)),
                        pltpu.SemaphoreType.DMA((n,))],
    )(*vecs)


def _pair_sum(g, other, c_idx):
    nq, r, w = g.shape
    h = r // 2
    tr = _div_tile(h)
    nt = h // tr

    def body(c_ref, g_ref, o_ref, s_ref):
        s_ref[...] = (g_ref[...].astype(F32) + o_ref[...].astype(F32)).astype(s_ref.dtype)

    return pl.pallas_call(
        body, name="pair_sum",
        grid_spec=pltpu.PrefetchScalarGridSpec(
            num_scalar_prefetch=1, grid=(nq, nt),
            in_specs=[pl.BlockSpec((None, tr, w), lambda k, i, c_ref: (k, c_ref[0] * nt + i, 0)),
                      pl.BlockSpec((None, tr, w), lambda k, i, c_ref: (k, i, 0))],
            out_specs=pl.BlockSpec((None, tr, w), lambda k, i, c_ref: (k, i, 0))),
        out_shape=jax.ShapeDtypeStruct((nq, h, w), g.dtype),
        compiler_params=_cparams(2),
    )(c_idx, g, other)


def _chip_sum(s, others, q_idx):
    _, h, w = s.shape
    tr = _div_tile(h)

    def body(q_ref, s_ref, o_ref, out_ref):
        out_ref[...] = ((s_ref[...].astype(F32) + o_ref[0].astype(F32)) + o_ref[1].astype(F32)) + o_ref[2].astype(F32)

    return pl.pallas_call(
        body, name="chip_sum",
        grid_spec=pltpu.PrefetchScalarGridSpec(
            num_scalar_prefetch=1, grid=(h // tr,),
            in_specs=[pl.BlockSpec((None, tr, w), lambda i, q_ref: (q_ref[0], i, 0)),
                      pl.BlockSpec((3, tr, w), lambda i, q_ref: (0, i, 0))],
            out_specs=pl.BlockSpec((tr, w), lambda i, q_ref: (i, 0))),
        out_shape=jax.ShapeDtypeStruct((h, w), F32),
        compiler_params=_cparams(1),
    )(q_idx, s, others)


def _small_update(q_idx, parts, ws, ms, vs, col_block):
    n = len(parts)
    has_w = [w is not None for w in ws]

    def body(q_ref, *refs):
        pos = 0
        ins = []
        for t in range(n):
            k = 4 if has_w[t] else 1
            ins.append(refs[pos:pos + k])
            pos += k
        outs = refs[pos:]
        opos = 0
        for t in range(n):
            p_ref = ins[t][0]
            g = p_ref[0]
            for s in range(1, p_ref.shape[0]):
                g = g + p_ref[s]
            if has_w[t]:
                _, w_ref, m_ref, v_ref = ins[t]
                g_ref, d_ref, nm_ref, nv_ref = outs[opos:opos + 4]
                opos += 4
                g_ref[...] = g
                d_ref[...], nm_ref[...], nv_ref[...] = _adamw_update(w_ref[...], g, m_ref[...], v_ref[...])
            else:
                outs[opos][...] = g
                opos += 1

    def whole(shape):
        nd = len(shape)
        return pl.BlockSpec(shape, lambda i, q_ref: (0,) * nd)

    in_specs, out_specs, out_shape, args = [], [], [], []
    for t in range(n):
        k, r, wf = parts[t].shape
        if col_block[t]:
            w = wf // N_CHIPS
            in_specs.append(pl.BlockSpec((k, r, w), lambda i, q_ref: (0, 0, q_ref[0])))
        else:
            w = wf
            in_specs.append(whole((k, r, wf)))
        args.append(parts[t])
        if has_w[t]:
            assert ws[t].shape == (r, w), (ws[t].shape, r, w)
            in_specs += [whole((r, w))] * 3
            args += [ws[t], ms[t], vs[t]]
            out_specs += [whole((r, w))] * 4
            out_shape += [jax.ShapeDtypeStruct((r, w), F32)] * 4
        else:
            out_specs.append(whole((r, w)))
            out_shape.append(jax.ShapeDtypeStruct((r, w), F32))
    return pl.pallas_call(
        body, name="small_update",
        grid_spec=pltpu.PrefetchScalarGridSpec(num_scalar_prefetch=1, grid=(1,), in_specs=in_specs,
                                               out_specs=out_specs),
        out_shape=out_shape, compiler_params=_cparams(1),
    )(q_idx, *args)


def _pack_win(w_in):
    o = np.cumsum((0,) + IN_SPLITS)
    gq, gk, gv, gg, ga, sq, sk, sv = [w_in[:, o[i]:o[i + 1]] for i in range(8)]
    z = jnp.zeros((w_in.shape[0], 128 - GLA_RANK), w_in.dtype)
    dup = lambda a: jnp.concatenate([a[:, :64], a[:, :64], a[:, 64:], a[:, 64:]], axis=1)
    return jnp.concatenate([gq, gk, gv, gg, ga, z, sq, dup(sk), dup(sv)], axis=1)


def _unpack_dwin(d):
    und = lambda a: jnp.concatenate([a[:, 0:64] + a[:, 64:128], a[:, 128:192] + a[:, 192:256]], axis=1)
    return jnp.concatenate([d[:, :P_GA], d[:, P_GA:P_GA + GLA_RANK], d[:, P_SQ:P_SK], und(d[:, P_SK:P_SV]),
                            und(d[:, P_SV:P_END])], axis=1)


def _local_step(x, target, meta, p):
    s = x.shape[0]
    t = s + BLK
    h0 = jnp.concatenate([jnp.zeros((PAD, D_MODEL), F32), meta, x], axis=0)
    cos, sin = _rope_tables(t)

    h1, n1, g1, u1, a1, f1 = _ffn_fwd(h0, p["ffn1_pre_norm"], p["ffn1_w_gate"], p["ffn1_w_up"], p["ffn1_w_down"],
                                      p["ffn1_post_norm"])
    n2, gq, gk, gv, gg, ga, la, sq, sk, sv = _mix_proj(h1, p["mix_pre_norm"], p["w_in"], p["gla_w_a2"], p["gla_b_a"],
                                                       cos, sin)
    ogla, ss = _gla_fwd(gq, gk, gv, la)
    oswa = _swa_fwd(p["swa_sinks"], sq, sk, sv)
    h2, cat, m = _mix_out(h1, ogla, gg, oswa, p["gla_out_norm"], p["swa_out_norm"], p["w_out"], p["mix_post_norm"])
    dy, n3, g3, u3, a3, f3, sse = _ffn_fwd(h2, p["ffn2_pre_norm"], p["ffn2_w_gate"], p["ffn2_w_up"],
                                           p["ffn2_w_down"], p["ffn2_post_norm"], target=target)

    grads = {}
    dh2, df3, dg3, du3, grads["ffn2_pre_norm"], grads["ffn2_post_norm"] = _ffn_bwd(
        dy, h2, f3, g3, u3, p["ffn2_pre_norm"], p["ffn2_w_gate"], p["ffn2_w_up"], p["ffn2_w_down"],
        p["ffn2_post_norm"])
    (gud,) = _ffn_wgrad(n3, df3, dg3, du3, a3)
    grads["ffn2_w_gate"], grads["ffn2_w_up"], grads["ffn2_w_down"] = gud[:, :FJ], gud[:, FJ:2 * FJ], gud[:, 2 * FJ:]

    dogla, dgg, doswa, dm, grads["mix_post_norm"], grads["gla_out_norm"], grads["swa_out_norm"] = _mix_out_bwd(
        dh2, m, ogla, gg, oswa, p["gla_out_norm"], p["swa_out_norm"], p["w_out"], p["mix_post_norm"])
    grads["w_out"] = _xty(cat, dm)
    dsq, dsk, dsv, dkm, dvm, dsinks = _swa_bwd(p["swa_sinks"], sq, sk, sv, oswa, doswa)
    grads["swa_sinks"] = dsinks[:, 0]
    dgq, dgk, dgv, dla = _gla_bwd(gq, gk, gv, la, ss, dogla)
    dh1, dproj, grads["mix_pre_norm"], dwa2p, grads["gla_b_a"] = _mix_in_bwd(
        dh2, h1, p["mix_pre_norm"], p["w_in"], p["gla_w_a2"], p["gla_b_a"], cos, sin, ga, dgq, dgk, dgv, dgg, dla,
        dsq, dsk, dsv, dkm, dvm)
    grads["gla_w_a2"] = dwa2p[:GLA_RANK]
    grads["w_in"] = _unpack_dwin(_xty(n2, dproj))

    dh0, df1, dg1, du1, grads["ffn1_pre_norm"], grads["ffn1_post_norm"] = _ffn_bwd(
        dh1, h0, f1, g1, u1, p["ffn1_pre_norm"], p["ffn1_w_gate"], p["ffn1_w_up"], p["ffn1_w_down"],
        p["ffn1_post_norm"])
    (gud,) = _ffn_wgrad(n1, df1, dg1, du1, a1)
    grads["ffn1_w_gate"], grads["ffn1_w_up"], grads["ffn1_w_down"] = gud[:, :FJ], gud[:, FJ:2 * FJ], gud[:, 2 * FJ:]
    grads["meta_tokens"] = dh0[PAD:BLK]
    return sse[0, 0], dh0[BLK:], grads


WEIGHTS = ['meta_tokens', 'ffn1_pre_norm', 'ffn1_w_gate', 'ffn1_w_up', 'ffn1_w_down', 'ffn1_post_norm',
           'mix_pre_norm', 'w_in', 'gla_w_a2', 'gla_b_a', 'gla_out_norm', 'swa_sinks', 'swa_out_norm', 'w_out',
           'mix_post_norm', 'ffn2_pre_norm', 'ffn2_w_gate', 'ffn2_w_up', 'ffn2_w_down', 'ffn2_post_norm']
BIG = ['ffn1_w_gate', 'ffn1_w_up', 'ffn1_w_down', 'w_in', 'w_out', 'ffn2_w_gate', 'ffn2_w_up', 'ffn2_w_down']
SMALL = [n for n in WEIGHTS if n not in BIG]
FJ = D_FF // N_CHIPS
D_IN_J = D_IN // N_CHIPS
D_OUT_J = D_MODEL // N_CHIPS
TRANSPOSED = ('ffn1_w_gate', 'ffn1_w_up', 'ffn2_w_gate', 'ffn2_w_up')


def _shard2d(name, a):
    return a[0].T if name in TRANSPOSED else a[0]


def _unshard2d(name, a):
    return (a.T if name in TRANSPOSED else a)[None]


def _small_rows(name, a):
    flat = a.reshape(-1)
    rows = -(-flat.shape[0] // 1024) * 8
    return jnp.pad(flat, (0, rows * 128 - flat.shape[0])).reshape(rows, 128)


def kernel(x, meta_tokens, ffn1_pre_norm, ffn1_w_gate, ffn1_w_up, ffn1_w_down, ffn1_post_norm, mix_pre_norm, w_in, gla_w_a2, gla_b_a, gla_out_norm, swa_sinks, swa_out_norm, w_out, mix_post_norm, ffn2_pre_norm, ffn2_w_gate, ffn2_w_up, ffn2_w_down, ffn2_post_norm, loss_target, m_meta_tokens, m_ffn1_pre_norm, m_ffn1_w_gate, m_ffn1_w_up, m_ffn1_w_down, m_ffn1_post_norm, m_mix_pre_norm, m_w_in, m_gla_w_a2, m_gla_b_a, m_gla_out_norm, m_swa_sinks, m_swa_out_norm, m_w_out, m_mix_post_norm, m_ffn2_pre_norm, m_ffn2_w_gate, m_ffn2_w_up, m_ffn2_w_down, m_ffn2_post_norm, v_meta_tokens, v_ffn1_pre_norm, v_ffn1_w_gate, v_ffn1_w_up, v_ffn1_w_down, v_ffn1_post_norm, v_mix_pre_norm, v_w_in, v_gla_w_a2, v_gla_b_a, v_gla_out_norm, v_swa_sinks, v_swa_out_norm, v_w_out, v_mix_post_norm, v_ffn2_pre_norm, v_ffn2_w_gate, v_ffn2_w_up, v_ffn2_w_down, v_ffn2_post_norm):
    args = dict(locals())
    w = {n: args[n] for n in WEIGHTS}
    mom = {n: args["m_" + n] for n in WEIGHTS}
    var = {n: args["v_" + n] for n in WEIGHTS}
    cx, cy, cc = lax.axis_index("x"), lax.axis_index("y"), lax.axis_index("c")
    q_idx = (2 * cx + cy).astype(jnp.int32).reshape(1)
    c_idx = cc.astype(jnp.int32).reshape(1)

    q_chip = 2 * cx + cy
    bf = {n: _own_slot(_shard2d(n, w[n]).astype(BF16), q_chip) for n in BIG}
    qc_idx = jnp.stack([q_chip, cc]).astype(jnp.int32)
    early = _GatherChips([_own_slot(w["meta_tokens"], q_chip),
                          _own_slot(w["gla_w_a2"].reshape(GLA_RANK, GLA_KW // N_CHIPS), q_chip)])
    meta4, wa24 = _comm_call(early, "gather_small")
    meta_full = meta4.transpose(1, 0, 2).reshape(N_META, D_MODEL)
    wa2p = jnp.pad(wa24.transpose(1, 0, 2).reshape(GLA_RANK, GLA_KW), ((0, 128 - GLA_RANK), (0, 0))).astype(BF16)
    sinks = w["swa_sinks"].reshape(SWA_QH)

    seq, target = x[0], loss_target[0]
    t = seq.shape[0] + BLK
    h0, n1 = _embed_norm(seq, meta_full, w["ffn1_pre_norm"])
    cos, sin = _rope_tables(t)
    late = _GatherChips([bf["w_in"], bf["w_out"], bf["ffn2_w_gate"], bf["ffn2_w_up"], bf["ffn2_w_down"]])
    (h1, g1, u1, a1, f1), (wg1, wu1, wd1), (win4, wout4, wg2, wu2, wd2) = _ffn_fwd_gather(
        h0, n1, [bf["ffn1_w_gate"], bf["ffn1_w_up"], bf["ffn1_w_down"]], w["ffn1_post_norm"], qc_idx, late)
    winp = _pack_win(win4.transpose(1, 0, 2).reshape(D_MODEL, D_IN))
    wout = wout4.reshape(D_MODEL, D_MODEL)
    n2, gq, gk, gv, gg, ga, la, sq, sk, sv = _mix_proj(h1, w["mix_pre_norm"], winp, wa2p, w["gla_b_a"], cos, sin)
    ogla, ss = _gla_fwd(gq, gk, gv, la)
    oswa = _swa_fwd(sinks, sq, sk, sv)
    h2, cat, m = _mix_out(h1, ogla, gg, oswa, w["gla_out_norm"], w["swa_out_norm"], wout, w["mix_post_norm"])
    dy, n3, g3, u3, a3, f3, sse = _ffn_fwd(h2, w["ffn2_pre_norm"], wg2, wu2, wd2, w["ffn2_post_norm"], target=target)
    loss = lax.psum(sse[0, 0] * (0.5 / D_MODEL), ("x", "y", "c"))

    g = {}
    dh2, df3, dg3, du3, g["ffn2_pre_norm"], g["ffn2_post_norm"] = _ffn_bwd(
        dy, h2, f3, g3, u3, w["ffn2_pre_norm"], wg2, wu2, wd2, w["ffn2_post_norm"])
    (gf2,) = _ffn_wgrad(n3, df3, dg3, du3, a3)
    (dogla, dgg, doswa, dm, g["mix_post_norm"], g["gla_out_norm"], g["swa_out_norm"]), (rgf2,) = _mix_out_bwd(
        dh2, m, ogla, gg, oswa, w["gla_out_norm"], w["swa_out_norm"], wout, w["mix_post_norm"],
        hook=_PairExchange([gf2]))
    sgf2 = _pair_sum(gf2, rgf2, c_idx)
    gout = _xty(cat, dm).reshape(N_CHIPS, D_OUT_J, D_MODEL).astype(BF16)
    (dsq, dsk, dsv, dkm, dvm, dsinks), (ogf2,) = _swa_bwd(sinks, sq, sk, sv, oswa, doswa,
                                                          hook=_ChipScatter([sgf2]))
    g["swa_sinks"] = dsinks
    dgq, dgk, dgv, dla = _gla_bwd(gq, gk, gv, la, ss, dogla)
    dh1, dproj, g["mix_pre_norm"], dwa2p, g["gla_b_a"] = _mix_in_bwd(
        dh2, h1, w["mix_pre_norm"], winp, wa2p, w["gla_b_a"], cos, sin, ga, dgq, dgk, dgv, dgg, dla,
        dsq, dsk, dsv, dkm, dvm)
    g["gla_w_a2"] = dwa2p[:GLA_RANK]
    gin = _unpack_dwin(_xty(n2, dproj)).reshape(D_MODEL, N_CHIPS, D_IN_J).transpose(1, 0, 2).astype(BF16)
    (dh0, df1, dg1, du1, g["ffn1_pre_norm"], g["ffn1_post_norm"]), (rgin, rgout) = _ffn_bwd(
        dh1, h0, f1, g1, u1, w["ffn1_pre_norm"], wg1, wu1, wd1, w["ffn1_post_norm"],
        hook=_PairExchange([gin, gout]))
    sgin, sgout = _pair_sum(gin, rgin, c_idx), _pair_sum(gout, rgout, c_idx)
    own1, others1, (ogin, ogout) = _ffn_wgrad_reduce(n1, df1, dg1, du1, a1, qc_idx, _ChipScatter([sgin, sgout]))
    g["meta_tokens"] = dh0[PAD:BLK]
    grad_x = dh0[BLK:]
    halves = [_chip_sum(own1[None], others1, jnp.zeros((1,), jnp.int32))]
    halves += [_chip_sum(s, o, q_idx) for s, o in ((sgin, ogin), (sgout, ogout), (sgf2, ogf2))]
    others = _comm_call(_PairShare(halves), "pair_share")
    reduced = {"ffn1_w_gate": (0, 0), "ffn1_w_up": (0, FJ), "ffn1_w_down": (0, 2 * FJ), "w_in": (1, 0),
               "w_out": (2, 0), "ffn2_w_gate": (3, 0), "ffn2_w_up": (3, FJ), "ffn2_w_down": (3, 2 * FJ)}
    grad, delta, new_m, new_v = {}, {}, {}, {}
    for n in BIG:
        k, row0 = reduced[n]
        outs = _adamw_halves(_shard2d(n, w[n]), halves[k], others[k], _shard2d(n, mom[n]), _shard2d(n, var[n]),
                             c_idx, row0)
        grad[n], delta[n], new_m[n], new_v[n] = [_unshard2d(n, a) for a in outs]

    late = ["gla_w_a2", "swa_sinks"]
    direct = [n for n in SMALL if n not in late]
    names = direct + late
    gathered = _all_gather_devices([g[n] for n in names])
    mat = lambda a: a.reshape(a.shape[-2:])
    none2 = [None] * len(late)
    outs = _small_update(q_idx, gathered, [mat(w[n]) for n in direct] + none2, [mat(mom[n]) for n in direct] + none2,
                         [mat(var[n]) for n in direct] + none2, [n == "meta_tokens" for n in names])
    sum_a2, sum_sinks = outs[4 * len(direct):]
    g_late = [lax.dynamic_slice_in_dim(sum_a2, q_chip * (GLA_KW // N_CHIPS), GLA_KW // N_CHIPS, axis=1)[None],
              sum_sinks[:, 0].reshape(1, 1, SWA_QH)]
    outs = list(outs[:4 * len(direct)]) + list(_small_update(
        q_idx, g_late, [mat(w[n]) for n in late], [mat(mom[n]) for n in late], [mat(var[n]) for n in late],
        [False, False]))
    for k, n in enumerate(names):
        grad[n], delta[n], new_m[n], new_v[n] = [a.reshape(w[n].shape) for a in outs[4 * k:4 * k + 4]]

    return (loss, grad_x[None], *[grad[n] for n in WEIGHTS], *[delta[n] for n in WEIGHTS],
            *[new_m[n] for n in WEIGHTS], *[new_v[n] for n in WEIGHTS])
```

```python
import functools
import math

import numpy as np
import jax
import jax.numpy as jnp
from jax import lax
from jax.experimental import pallas as pl
from jax.experimental.pallas import tpu as pltpu

F32 = jnp.float32
BF16 = jnp.bfloat16
MESH = pl.DeviceIdType.MESH

D_MODEL = 1024
D_FF = 2816
N_CHIPS = 4
N_DEV = 8
N_META = 16
BLK = 128
PAD = BLK - N_META
GLA_CHUNK = 64
GLA_HEADS = 4
GLA_DV = 128
GLA_DK = 64
GLA_KW = GLA_HEADS * GLA_DK
GLA_W = GLA_HEADS * GLA_DV
GLA_RANK = 16
GLA_TAU = 16.0
SWA_HD = 64
SWA_QH = 8
SWA_KVH = 2
SWA_W = SWA_QH * SWA_HD
WINDOW = 128
ROPE_THETA = 10000.0
EPS = 1e-6
NEG_INF = -1e30
IN_SPLITS = (256, 256, 512, 512, 16, 512, 128, 128)
D_IN = sum(IN_SPLITS)
P_GQ, P_GK, P_GV, P_GG, P_GA, P_SQ, P_SK, P_SV, P_END = 0, 256, 512, 1024, 1536, 1664, 2176, 2432, 2688
ADAM_LR, ADAM_B1, ADAM_B2, ADAM_EPS, ADAM_WD, ADAM_STEP = 0.001, 0.9, 0.999, 1e-08, 0.01, 10
VMEM_LIMIT = 56 * 1024 * 1024

NT = (((1,), (1,)), ((), ()))
TN = (((0,), (0,)), ((), ()))


def _cparams(n_axes):
    return pltpu.CompilerParams(dimension_semantics=("arbitrary",) * n_axes, vmem_limit_bytes=VMEM_LIMIT)


def _row_tile(t):
    for tm in (640, 512, 384, 256, 128):
        if t % tm == 0:
            return tm
    raise ValueError(t)


SEQ_BLOCKS_PER_STEP = 5


def _seq_tile(t):
    return SEQ_BLOCKS_PER_STEP * BLK if t % (SEQ_BLOCKS_PER_STEP * BLK) == 0 else BLK


ROW_PARTS = 2


def _row_parts(tm):
    n = ROW_PARTS if tm % (16 * ROW_PARTS) == 0 else 1
    return [slice(k * (tm // n), (k + 1) * (tm // n)) for k in range(n)]


def _contract_tile(t):
    return 1664 if t % 1664 == 0 else _row_tile(t)


def _div_tile(r, cap=512):
    best = None
    for tr in range(8, min(r, cap) + 1, 8):
        if r % tr == 0:
            best = tr
    return best if best is not None else r


def _dot(a, b):
    return jnp.dot(a, b, preferred_element_type=F32)


def _dg(a, b, dims):
    return lax.dot_general(a, b, dims, preferred_element_type=F32)


def _rms(x, w):
    r = lax.rsqrt(jnp.mean(x * x, axis=-1, keepdims=True) + EPS)
    xh = x * r
    return xh * w, xh, r


def _rms_bwd(xh, r, w, dy):
    wdy = dy * w
    dx = r * (wdy - xh * jnp.mean(wdy * xh, axis=-1, keepdims=True))
    dw = jnp.sum(dy * xh, axis=0, keepdims=True)
    return dx, dw


def _sigmoid(x):
    return 1.0 / (1.0 + jnp.exp(-x))


def _full(shape):
    nd = len(shape)
    return pl.BlockSpec(shape, lambda *_: (0,) * nd)


ANY = pl.BlockSpec(memory_space=pl.ANY)


def _pallas(body, *, name, grid, in_specs, out_specs, out_shape, args, scratch_shapes=(), hook=None):
    n_axes = len(grid)
    if hook is None:
        return pl.pallas_call(body, name=name, grid=grid, in_specs=list(in_specs), out_specs=list(out_specs),
                              out_shape=list(out_shape), scratch_shapes=list(scratch_shapes),
                              compiler_params=_cparams(n_axes))(*args)
    n_in, n_out, n_scr = len(in_specs), len(out_specs), len(scratch_shapes)
    h_in, h_out = len(hook.inputs), len(hook.out_shape)
    total = math.prod(grid)

    def wrapped(*refs):
        ins, hins = refs[:n_in], refs[n_in:n_in + h_in]
        o0 = n_in + h_in
        outs, houts = refs[o0:o0 + n_out], refs[o0 + n_out:o0 + n_out + h_out]
        s0 = o0 + n_out + h_out
        scr, hscr = refs[s0:s0 + n_scr], refs[s0 + n_scr:]
        step = pl.program_id(0)
        for a in range(1, n_axes):
            step = step * grid[a] + pl.program_id(a)

        @pl.when(step == 0)
        def _():
            hook.start(hins, houts, hscr)

        body(*ins, *outs, *scr)

        if hook.has_mid:
            @pl.when(step == (3 * total) // 4)
            def _():
                hook.mid(hins, houts, hscr)

        @pl.when(step == total - 1)
        def _():
            hook.finish(hins, houts, hscr)

    res = pl.pallas_call(
        wrapped, name=name, grid=grid, in_specs=list(in_specs) + [ANY] * h_in,
        out_specs=list(out_specs) + [ANY] * h_out, out_shape=list(out_shape) + list(hook.out_shape),
        scratch_shapes=list(scratch_shapes) + list(hook.scratch), compiler_params=_cparams(n_axes),
        input_output_aliases={n_in + a: n_out + b for a, b in hook.aliases},
    )(*args, *hook.inputs)
    return res[:n_out], res[n_out:]


def _ffn_fwd(h, wpre, wg4, wu4, wd4, wpost, hook=None, target=None):
    t = h.shape[0]
    tm = _row_tile(t)
    nj, fj, _ = wg4.shape
    nblk = tm // BLK if target is not None else 0

    def body(*refs):
        h_ref, wpre_ref, wg_ref, wu_ref, wd_ref, wpost_ref = refs[:6]
        t_refs = refs[6:6 + nblk]
        hout_ref, n_ref, p1_ref, p2_ref, a_ref, f_ref = refs[6 + nblk:12 + nblk]
        acc_ref = refs[-1]
        i = pl.program_id(0)
        j = pl.program_id(1)

        @pl.when(j == 0)
        def _():
            y, _, _ = _rms(h_ref[...], wpre_ref[...])
            n_ref[...] = y.astype(BF16)
            acc_ref[...] = jnp.zeros_like(acc_ref)

        if target is not None:
            dwpost_ref, sse_ref = refs[12 + nblk:14 + nblk]

            @pl.when((i == 0) & (j == 0))
            def _():
                dwpost_ref[...] = jnp.zeros_like(dwpost_ref)
                sse_ref[...] = jnp.zeros_like(sse_ref)

        n = n_ref[...]
        g = _dg(n, wg_ref[...], NT)
        u = _dg(n, wu_ref[...], NT)
        sg = _sigmoid(g)
        silu = g * sg
        p1_ref[...] = (u * (sg + silu * (1.0 - sg))).astype(BF16)
        p2_ref[...] = silu.astype(BF16)
        a = (silu * u).astype(BF16)
        a_ref[...] = a
        acc_ref[...] += _dot(a, wd_ref[...])

        @pl.when(j == nj - 1)
        def _():
            f = acc_ref[...]
            wpost = wpost_ref[...]
            y, fh, r = _rms(f, wpost)
            hout = h_ref[...] + 0.5 * y
            if target is None:
                f_ref[...] = f
                hout_ref[...] = hout
            else:
                sse = jnp.zeros((1, 1), F32)
                errs = []
                for k in range(nblk):
                    err = hout[k * BLK:(k + 1) * BLK] - t_refs[k][...]
                    if k == 0:
                        err = jnp.where(i > 0, err, 0.0)
                    errs.append(err)
                    sse = sse + jnp.sum(jnp.sum(err * err, axis=1, keepdims=True), axis=0, keepdims=True)
                dy = (jnp.concatenate(errs, axis=0) if nblk > 1 else errs[0]) * (1.0 / D_MODEL)
                hout_ref[...] = dy
                df, dw = _rms_bwd(fh, r, wpost, 0.5 * dy)
                f_ref[...] = df.astype(BF16)
                dwpost_ref[...] += dw
                sse_ref[...] += jnp.broadcast_to(sse, sse_ref.shape)

    row = pl.BlockSpec((tm, D_MODEL), lambda i, j: (i, 0))
    vec = pl.BlockSpec((1, D_MODEL), lambda i, j: (0, 0))
    wrow = pl.BlockSpec((None, fj, D_MODEL), lambda i, j: (j, 0, 0))
    act = pl.BlockSpec((None, tm, fj), lambda i, j: (j, i, 0))
    t_specs = [pl.BlockSpec((BLK, D_MODEL), functools.partial(lambda i, j, k: (jnp.maximum(nblk * i + k - 1, 0), 0), k=k))
               for k in range(nblk)]
    loss_spec = [vec, _full((1, 128))] if target is not None else []
    loss_shape = [jax.ShapeDtypeStruct((1, D_MODEL), F32), jax.ShapeDtypeStruct((1, 128), F32)] if (
        target is not None) else []
    return _pallas(
        body, name="ffn_fwd", grid=(t // tm, nj),
        in_specs=[row, vec, wrow, wrow, wrow, vec] + t_specs,
        out_specs=[row, row, act, act, act, row] + loss_spec,
        out_shape=[jax.ShapeDtypeStruct((t, D_MODEL), F32), jax.ShapeDtypeStruct((t, D_MODEL), BF16),
                   jax.ShapeDtypeStruct((nj, t, fj), BF16), jax.ShapeDtypeStruct((nj, t, fj), BF16),
                   jax.ShapeDtypeStruct((nj, t, fj), BF16),
                   jax.ShapeDtypeStruct((t, D_MODEL), F32 if target is None else BF16)] + loss_shape,
        scratch_shapes=[pltpu.VMEM((tm, D_MODEL), F32)],
        args=(h, wpre, wg4, wu4, wd4, wpost) + (target,) * nblk, hook=hook)


def _ffn_bwd(dhout, h, f, p14, p24, wpre, wg4, wu4, wd4, wpost, hook=None, df=None):
    t = h.shape[0]
    tm = _row_tile(t)
    nj, fj, _ = wg4.shape
    have_df = df is not None

    def body(dhout_ref, h_ref, f_ref, p1_ref, p2_ref, wpre_ref, wg_ref, wu_ref, wd_ref, wpost_ref, *rest):
        if have_df:
            dh_ref, dg_ref, du_ref, dwpre_ref, dn_ref = rest
            df_ref = f_ref
        else:
            dh_ref, df_ref, dg_ref, du_ref, dwpre_ref, dwpost_ref, dn_ref = rest
        i = pl.program_id(0)
        j = pl.program_id(1)

        @pl.when((i == 0) & (j == 0))
        def _():
            dwpre_ref[...] = jnp.zeros_like(dwpre_ref)
            if not have_df:
                dwpost_ref[...] = jnp.zeros_like(dwpost_ref)

        @pl.when(j == 0)
        def _():
            if not have_df:
                wpost = wpost_ref[...]
                _, fh, r = _rms(f_ref[...], wpost)
                dfv, dw = _rms_bwd(fh, r, wpost, 0.5 * dhout_ref[...])
                dwpost_ref[...] += dw
                df_ref[...] = dfv.astype(BF16)
            dn_ref[...] = jnp.zeros_like(dn_ref)

        parts = _row_parts(tm)
        das = [_dg(df_ref[rows, :], wd_ref[...], NT) for rows in parts]
        for rows, da in zip(parts, das):
            dg = (da * p1_ref[rows, :].astype(F32)).astype(BF16)
            du = (da * p2_ref[rows, :].astype(F32)).astype(BF16)
            dg_ref[rows, :] = dg
            du_ref[rows, :] = du
            dn_ref[rows, :] += _dot(dg, wg_ref[...]) + _dot(du, wu_ref[...])

        @pl.when(j == nj - 1)
        def _():
            wpre = wpre_ref[...]
            _, hh, r = _rms(h_ref[...], wpre)
            dx, dw = _rms_bwd(hh, r, wpre, dn_ref[...])
            dwpre_ref[...] += dw
            dh_ref[...] = dhout_ref[...] + dx

    row = pl.BlockSpec((tm, D_MODEL), lambda i, j: (i, 0))
    vec = pl.BlockSpec((1, D_MODEL), lambda i, j: (0, 0))
    wrow = pl.BlockSpec((None, fj, D_MODEL), lambda i, j: (j, 0, 0))
    act = pl.BlockSpec((None, tm, fj), lambda i, j: (j, i, 0))
    actshape = jax.ShapeDtypeStruct((nj, t, fj), BF16)
    rowf, rowb, vecf = (jax.ShapeDtypeStruct((t, D_MODEL), F32), jax.ShapeDtypeStruct((t, D_MODEL), BF16),
                        jax.ShapeDtypeStruct((1, D_MODEL), F32))
    return _pallas(
        body, name="ffn_bwd", grid=(t // tm, nj),
        in_specs=[row, row, row, act, act, vec, wrow, wrow, wrow, vec],
        out_specs=[row, act, act, vec] if have_df else [row, row, act, act, vec, vec],
        out_shape=[rowf, actshape, actshape, vecf] if have_df else [rowf, rowb, actshape, actshape, vecf, vecf],
        scratch_shapes=[pltpu.VMEM((tm, D_MODEL), F32)],
        args=(dhout, h, df if have_df else f, p14, p24, wpre, wg4, wu4, wd4, wpost), hook=hook)


def _ffn_wgrad(n, df, dg4, du4, a4, hook=None):
    t = n.shape[0]
    tm = _contract_tile(t)
    ni = t // tm
    nj, _, fj = dg4.shape

    def body(n_ref, df_ref, dg_ref, du_ref, a_ref, dw_ref, acc):
        i = pl.program_id(1)

        @pl.when(i == 0)
        def _():
            acc[...] = jnp.zeros_like(acc)

        nn = n_ref[...]
        acc[0:fj, :] += _dg(dg_ref[...], nn, TN)
        acc[fj:2 * fj, :] += _dg(du_ref[...], nn, TN)
        acc[2 * fj:3 * fj, :] += _dg(a_ref[...], df_ref[...], TN)

        @pl.when(i == ni - 1)
        def _():
            dw_ref[...] = acc[...].astype(BF16)

    row = pl.BlockSpec((tm, D_MODEL), lambda j, i: (i, 0))
    act = pl.BlockSpec((None, tm, fj), lambda j, i: (j, i, 0))
    return _pallas(
        body, name="ffn_wgrad", grid=(nj, ni),
        in_specs=[row, row, act, act, act],
        out_specs=[pl.BlockSpec((None, 3 * fj, D_MODEL), lambda j, i: (j, 0, 0))],
        out_shape=[jax.ShapeDtypeStruct((nj, 3 * fj, D_MODEL), BF16)],
        scratch_shapes=[pltpu.VMEM((3 * fj, D_MODEL), F32)],
        args=(n, df, dg4, du4, a4), hook=hook)


def _embed_norm(x, meta, w):
    t = x.shape[0] + BLK
    tm = _row_tile(t)
    nblk = tm // BLK

    def body(*refs):
        x_refs = refs[:nblk]
        meta_ref, w_ref, h_ref, n_ref = refs[nblk:]
        i = pl.program_id(0)
        first = jnp.concatenate([jnp.zeros((PAD, D_MODEL), F32), meta_ref[...]], axis=0)
        blocks = [jnp.where(i == 0, first, x_refs[0][...])] + [r[...] for r in x_refs[1:]]
        h = jnp.concatenate(blocks, axis=0) if nblk > 1 else blocks[0]
        h_ref[...] = h
        y, _, _ = _rms(h, w_ref[...])
        n_ref[...] = y.astype(BF16)

    x_specs = [pl.BlockSpec((BLK, D_MODEL), functools.partial(lambda i, k: (jnp.maximum(nblk * i + k - 1, 0), 0), k=k))
               for k in range(nblk)]
    row = pl.BlockSpec((tm, D_MODEL), lambda i: (i, 0))
    return pl.pallas_call(
        body, name="embed_norm", grid=(t // tm,),
        in_specs=x_specs + [_full((N_META, D_MODEL)), _full((1, D_MODEL))], out_specs=[row, row],
        out_shape=[jax.ShapeDtypeStruct((t, D_MODEL), F32), jax.ShapeDtypeStruct((t, D_MODEL), BF16)],
        compiler_params=_cparams(1),
    )(*([x] * nblk), meta, w)


FWD_RELATION = (None, 0, 1, 2)


def _ffn_fwd_gather(h, n, wbufs, wpost, qc_idx, late):
    t = h.shape[0]
    tm = _row_tile(t)
    ni = t // tm
    nj, fj, _ = wbufs[0].shape
    assert nj == N_CHIPS and ni >= 4
    nw = len(wbufs)
    n_lin, n_lout = len(late.inputs), len(late.out_shape)
    wait_step = ni - 3

    def body(qc_ref, h_ref, n_ref, wpost_ref, *rest):
        wb_in = rest[:nw]
        lins = rest[nw:nw + n_lin]
        o0 = nw + n_lin
        hout_ref, p1_ref, p2_ref, a_ref, f_hbm = rest[o0:o0 + 5]
        wb = rest[o0 + 5:o0 + 5 + nw]
        louts = rest[o0 + 5 + nw:o0 + 5 + nw + n_lout]
        s0 = o0 + 5 + nw + n_lout
        wv, wsem, send, recv, fbuf, fr_sem, fw_sem = rest[s0:s0 + 7]
        lscr = rest[s0 + 7:]
        p = pl.program_id(0)
        i = pl.program_id(1)
        step = p * ni + i
        fslot = step % 3
        nslot = (step + 1) % 3

        def f_tile(tile):
            return f_hbm.at[pl.ds(pl.multiple_of(tile * tm, 8), tm)]

        @pl.when(step > 1)
        def _():
            pltpu.make_async_copy(fbuf.at[nslot], f_tile(i), fw_sem.at[nslot]).wait()

        nxt = step + 1

        @pl.when((nxt < N_CHIPS * ni) & (nxt >= ni))
        def _():
            pltpu.make_async_copy(f_tile(nxt % ni), fbuf.at[nslot], fr_sem.at[nslot]).start()

        @pl.when(p > 0)
        def _():
            pltpu.make_async_copy(f_tile(i), fbuf.at[fslot], fr_sem.at[fslot]).wait()
        x, y, c, chips = _place()
        q = 2 * x + y
        sibling = (x, y, 1 - c)
        mine, other = _half(fj, c), _half(fj, 1 - c)

        def load(chunk, slot, src):
            return [pltpu.make_async_copy(src[t].at[chunk], wv.at[slot, t], wsem.at[slot, t]) for t in range(nw)]

        @pl.when((p == 0) & (i == 0))
        def _():
            for j, (cx, cy) in enumerate(chips):
                for t in range(nw):
                    _remote(send.at[t, j], recv.at[t, j], wb_in[t].at[q, mine], wb[t].at[q, mine], (cx, cy, c)).start()
            for cp in load(q, 0, wb_in):
                cp.start()
            for cp in load(q, 0, wb_in):
                cp.wait()

        @pl.when((p == 1) & (i == 0))
        def _():
            late.start(lins, louts, lscr)

        for pp in range(1, N_CHIPS):
            j = FWD_RELATION[pp]
            cx, cy = chips[j]
            chunk = 2 * cx + cy

            @pl.when((p == pp - 1) & (i == wait_step))
            def _(j=j, cx=cx, cy=cy, chunk=chunk, pp=pp):
                for t in range(nw):
                    got = wb[t].at[chunk, mine]
                    _remote(send.at[t, j], recv.at[t, j], got, got, (cx, cy, c)).wait_recv()
                    _remote(send.at[t, 3 + j], recv.at[t, 3 + j], got, got, sibling).start()
                for t in range(nw):
                    rest_half = wb[t].at[chunk, other]
                    _remote(send.at[t, 3 + j], recv.at[t, 3 + j], rest_half, rest_half, sibling).wait_recv()
                for cp in load(chunk, pp % 2, wb):
                    cp.start()

            @pl.when((p == pp) & (i == 0))
            def _(chunk=chunk, pp=pp):
                for cp in load(chunk, pp % 2, wb):
                    cp.wait()

        @pl.when((p == N_CHIPS - 1) & (i == ni // 2))
        def _():
            late.mid(lins, louts, lscr)

        slot = p % 2
        nn = n_ref[...]
        g = _dg(nn, wv[slot, 0], NT)
        u = _dg(nn, wv[slot, 1], NT)
        sg = _sigmoid(g)
        silu = g * sg
        p1_ref[...] = (u * (sg + silu * (1.0 - sg))).astype(BF16)
        p2_ref[...] = silu.astype(BF16)
        a = (silu * u).astype(BF16)
        a_ref[...] = a
        part = _dot(a, wv[slot, 2])

        @pl.when(p == 0)
        def _():
            fbuf[fslot] = part

        @pl.when(p > 0)
        def _():
            fbuf[fslot] = fbuf[fslot] + part

        pltpu.make_async_copy(fbuf.at[fslot], f_tile(i), fw_sem.at[fslot]).start()

        @pl.when(p == N_CHIPS - 1)
        def _():
            yv, _, _ = _rms(fbuf[fslot], wpost_ref[...])
            hout_ref[...] = h_ref[...] + 0.5 * yv

        @pl.when((p == N_CHIPS - 1) & (i == ni - 1))
        def _():
            pslot = (step + 2) % 3
            pltpu.make_async_copy(fbuf.at[pslot], f_tile(i), fw_sem.at[pslot]).wait()
            pltpu.make_async_copy(fbuf.at[fslot], f_tile(i), fw_sem.at[fslot]).wait()
            for t in range(nw):
                for j, (cx, cy) in enumerate(chips):
                    sent = wb[t].at[2 * cx + cy, mine]
                    _remote(send.at[t, j], recv.at[t, j], sent, sent, (cx, cy, c)).wait_send()
                    _remote(send.at[t, 3 + j], recv.at[t, 3 + j], sent, sent, sibling).wait_send()
            late.finish(lins, louts, lscr)

    def last_pass_rows(p, i, qc_ref):
        return (jnp.where(p == N_CHIPS - 1, i, 0), 0)

    def chunk_rows(p, i, qc_ref):
        order = ((p & 1) << 1) | (p >> 1)
        return (jnp.bitwise_xor(qc_ref[0], order), i, 0)

    row = pl.BlockSpec((tm, D_MODEL), lambda p, i, qc_ref: (i, 0))
    last_row = pl.BlockSpec((tm, D_MODEL), last_pass_rows)
    act = pl.BlockSpec((None, tm, fj), chunk_rows)
    act_shape = jax.ShapeDtypeStruct((nj, t, fj), BF16)
    res = pl.pallas_call(
        body, name="ffn_fwd_gather",
        grid_spec=pltpu.PrefetchScalarGridSpec(
            num_scalar_prefetch=1, grid=(N_CHIPS, ni),
            in_specs=[last_row, row, pl.BlockSpec((1, D_MODEL), lambda p, i, qc_ref: (0, 0))]
            + [ANY] * (nw + n_lin),
            out_specs=[last_row, act, act, act, ANY] + [ANY] * (nw + n_lout),
            scratch_shapes=[pltpu.VMEM((2, nw, fj, D_MODEL), BF16), pltpu.SemaphoreType.DMA((2, nw)),
                            pltpu.SemaphoreType.DMA((nw, 6)), pltpu.SemaphoreType.DMA((nw, 6)),
                            pltpu.VMEM((3, tm, D_MODEL), F32), pltpu.SemaphoreType.DMA((3,)),
                            pltpu.SemaphoreType.DMA((3,))] + list(late.scratch)),
        out_shape=[jax.ShapeDtypeStruct((t, D_MODEL), F32), act_shape, act_shape, act_shape,
                   jax.ShapeDtypeStruct((t, D_MODEL), F32)]
        + [jax.ShapeDtypeStruct(b.shape, b.dtype) for b in wbufs] + list(late.out_shape),
        input_output_aliases={**{4 + t: 5 + t for t in range(nw)},
                              **{4 + nw + a: 5 + nw + b for a, b in late.aliases}},
        compiler_params=_cparams(2),
    )(qc_idx, h, n, wpost, *wbufs, *late.inputs)
    return res[:5], res[5:5 + nw], res[5 + nw:]


PASS_RELATION = (2, 0, 1)


def _ffn_wgrad_reduce(n, df, dg4, du4, a4, qc_idx, hook):
    t = n.shape[0]
    tm = _contract_tile(t)
    ni = t // tm
    nj, _, fj = dg4.shape
    assert nj == N_CHIPS
    hrows = 3 * fj // 2
    n_hin, n_hout = len(hook.inputs), len(hook.out_shape)

    def body(qc_ref, n_ref, df_ref, dg_ref, du_ref, a_ref, *rest):
        hins = rest[:n_hin]
        own_ref, others_ref = rest[n_hin:n_hin + 2]
        houts = rest[n_hin + 2:n_hin + 2 + n_hout]
        s0 = n_hin + 2 + n_hout
        acc, stage, land, sumbuf, px_send, px_recv, cs_send, cs_recv, own_sem = rest[s0:s0 + 9]
        hscr = rest[s0 + 9:]
        k_pass = pl.program_id(0)
        i = pl.program_id(1)
        x, y, c, chips = _place()
        mine = pl.ds(pl.multiple_of(c * hrows, 8), hrows)
        other = pl.ds(pl.multiple_of((1 - c) * hrows, 8), hrows)

        def to_owner(k):
            j = PASS_RELATION[k]
            return _remote(cs_send.at[j], cs_recv.at[j], sumbuf.at[k % 2], others_ref.at[j], (*chips[j], c))

        @pl.when((k_pass == 0) & (i == 0))
        def _():
            hook.start(hins, houts, hscr)

        @pl.when(i == 0)
        def _():
            acc[...] = jnp.zeros_like(acc)

        nn = n_ref[...]
        acc[0:fj, :] += _dg(dg_ref[...], nn, TN)
        acc[fj:2 * fj, :] += _dg(du_ref[...], nn, TN)
        acc[2 * fj:3 * fj, :] += _dg(a_ref[...], df_ref[...], TN)

        for k in range(N_CHIPS):
            @pl.when((k_pass == k) & (i == ni - 1))
            def _(k=k):
                slot = k % 2
                stage[...] = acc[other, :].astype(BF16)
                swap = _remote(px_send.at[k], px_recv.at[k], stage, land.at[slot], (x, y, 1 - c))
                swap.start()
                swap.wait_recv()
                pair = acc[mine, :] + land[slot].astype(F32)
                if k >= 2:
                    to_owner(k - 2).wait_send()
                sumbuf[slot] = pair.astype(BF16)
                swap.wait_send()
                if k < N_CHIPS - 1:
                    to_owner(k).start()
                else:
                    keep = pltpu.make_async_copy(sumbuf.at[slot], own_ref, own_sem)
                    keep.start()
                    for j in range(N_CHIPS - 1):
                        _remote(cs_send.at[j], cs_recv.at[j], sumbuf.at[0], others_ref.at[j], (*chips[j], c)).wait_recv()
                    to_owner(k - 1).wait_send()
                    keep.wait()
                    hook.finish(hins, houts, hscr)

    def chunk(k_pass, i, qc_ref):
        return (jnp.bitwise_xor(qc_ref[0], N_CHIPS - 1 - k_pass), i, 0)

    row = pl.BlockSpec((tm, D_MODEL), lambda k_pass, i, qc_ref: (i, 0))
    act = pl.BlockSpec((None, tm, fj), chunk)
    res = pl.pallas_call(
        body, name="ffn_wgrad_reduce",
        grid_spec=pltpu.PrefetchScalarGridSpec(
            num_scalar_prefetch=1, grid=(N_CHIPS, ni),
            in_specs=[row, row, act, act, act] + [ANY] * n_hin,
            out_specs=[ANY, ANY] + [ANY] * n_hout,
            scratch_shapes=[pltpu.VMEM((3 * fj, D_MODEL), F32), pltpu.VMEM((hrows, D_MODEL), BF16),
                            pltpu.VMEM((2, hrows, D_MODEL), BF16), pltpu.VMEM((2, hrows, D_MODEL), BF16),
                            pltpu.SemaphoreType.DMA((N_CHIPS,)), pltpu.SemaphoreType.DMA((N_CHIPS,)),
                            pltpu.SemaphoreType.DMA((N_CHIPS - 1,)), pltpu.SemaphoreType.DMA((N_CHIPS - 1,)),
                            pltpu.SemaphoreType.DMA] + list(hook.scratch)),
        out_shape=[jax.ShapeDtypeStruct((hrows, D_MODEL), BF16),
                   jax.ShapeDtypeStruct((N_CHIPS - 1, hrows, D_MODEL), BF16)] + list(hook.out_shape),
        compiler_params=_cparams(2),
    )(qc_idx, n, df, dg4, du4, a4, *hook.inputs)
    return res[0], res[1], res[2:]


def _xty(x, y):
    t, k = x.shape
    n = y.shape[1]
    tm = _contract_tile(t)
    tn = n if n <= 1024 else (896 if n % 896 == 0 else 128)

    def body(x_ref, y_ref, o_ref):
        @pl.when(pl.program_id(1) == 0)
        def _():
            o_ref[...] = jnp.zeros_like(o_ref)

        o_ref[...] += _dg(x_ref[...], y_ref[...], TN)

    return pl.pallas_call(
        body, name="xty", grid=(n // tn, t // tm),
        in_specs=[pl.BlockSpec((tm, k), lambda j, i: (i, 0)), pl.BlockSpec((tm, tn), lambda j, i: (i, j))],
        out_specs=pl.BlockSpec((k, tn), lambda j, i: (0, j)),
        out_shape=jax.ShapeDtypeStruct((k, n), F32),
        compiler_params=_cparams(2),
    )(x, y)


def _rope_tables(t):
    pos = (jnp.arange(t, dtype=jnp.int32) - PAD).astype(F32)
    inv_freq = 1.0 / (ROPE_THETA ** (jnp.arange(0, SWA_HD, 2, dtype=F32) / SWA_HD))
    ang = pos[:, None] * inv_freq[None, :]
    cos = jnp.cos(ang)
    sin = jnp.sin(ang)
    return jnp.concatenate([cos, cos, cos, cos], axis=1), jnp.concatenate([-sin, sin, -sin, sin], axis=1)


def _rot_half(x, first_half):
    return jnp.where(first_half, pltpu.roll(x, 96, 1), pltpu.roll(x, 32, 1))


def _first_half_mask(rows):
    lane = lax.broadcasted_iota(jnp.int32, (rows, 128), 1)
    return (lane % 64) < 32


def _log_sigmoid(z):
    return jnp.minimum(z, 0.0) - jnp.log(1.0 + jnp.exp(-jnp.abs(z)))


def _mix_proj(h1, wmixpre, winp, wa2p, bap, cos, sin):
    t = h1.shape[0]
    tm = _row_tile(t)

    def body(h_ref, w_ref, win_ref, wa2_ref, ba_ref, cos_ref, sin_ref,
             n_ref, gq_ref, gk_ref, gv_ref, gg_ref, ga_ref, la_ref, sq_ref, sk_ref, sv_ref):
        y, _, _ = _rms(h_ref[...], w_ref[...])
        n = y.astype(BF16)
        n_ref[...] = n
        proj = _dot(n, win_ref[...])
        gq_ref[...] = proj[:, P_GQ:P_GK]
        gk_ref[...] = proj[:, P_GK:P_GV]
        gv_ref[...] = proj[:, P_GV:P_GG]
        gg_ref[...] = proj[:, P_GG:P_GA]
        ga = proj[:, P_GA:P_SQ]
        ga_ref[...] = ga
        z = _dot(ga.astype(BF16), wa2_ref[...]) + ba_ref[...]
        la_ref[...] = _log_sigmoid(z) * (1.0 / GLA_TAU)
        c = cos_ref[...]
        s = sin_ref[...]
        fh = _first_half_mask(tm)
        for k in range(4):
            x = proj[:, P_SQ + 128 * k:P_SQ + 128 * (k + 1)]
            sq_ref[:, 128 * k:128 * (k + 1)] = (x * c + _rot_half(x, fh) * s).astype(BF16)
        for k in range(2):
            x = proj[:, P_SK + 128 * k:P_SK + 128 * (k + 1)]
            sk_ref[:, 128 * k:128 * (k + 1)] = (x * c + _rot_half(x, fh) * s).astype(BF16)
        sv_ref[...] = proj[:, P_SV:P_END].astype(BF16)

    def row(w):
        return pl.BlockSpec((tm, w), lambda i: (i, 0))

    def rshape(w, dt):
        return jax.ShapeDtypeStruct((t, w), dt)

    return pl.pallas_call(
        body, name="mix_proj", grid=(t // tm,),
        in_specs=[row(D_MODEL), _full((1, D_MODEL)), _full((D_MODEL, P_END)), _full((128, GLA_KW)),
                  _full((1, GLA_KW)), row(128), row(128)],
        out_specs=[row(D_MODEL), row(256), row(256), row(512), row(512), row(128), row(256), row(512), row(256),
                   row(256)],
        out_shape=[rshape(D_MODEL, BF16), rshape(256, F32), rshape(256, F32), rshape(512, F32), rshape(512, F32),
                   rshape(128, F32), rshape(256, F32), rshape(512, BF16), rshape(256, BF16), rshape(256, BF16)],
        compiler_params=_cparams(1),
    )(h1, wmixpre, winp, wa2p, bap, cos, sin)


def _scan_rows(x, reverse=False):
    n = x.shape[0]
    row = lax.broadcasted_iota(jnp.int32, x.shape, 0)
    s = 1
    while s < n:
        if reverse:
            x = x + jnp.where(row < n - s, pltpu.roll(x, n - s, 0), 0.0)
        else:
            x = x + jnp.where(row >= s, pltpu.roll(x, s, 0), 0.0)
        s *= 2
    return x


def _gla_cumsum(la, tril_f):
    b = _scan_rows(la)
    row = lax.broadcasted_iota(jnp.int32, b.shape, 0)
    bm = jnp.sum(jnp.where(row == GLA_CHUNK // 2 - 1, b, 0.0), axis=0, keepdims=True)
    bl = jnp.sum(jnp.where(row == GLA_CHUNK - 1, b, 0.0), axis=0, keepdims=True)
    return b, bm, bl


def _gla_decays(la, tril_f):
    b, bm, bl = _gla_cumsum(la, tril_f)
    return jnp.exp(b - bm), jnp.exp(bm - b), jnp.exp(b), jnp.exp(bl - b), jnp.exp(bl)


def _gla_masks():
    c = GLA_CHUNK
    r = lax.broadcasted_iota(jnp.int32, (c, c), 0)
    col = lax.broadcasted_iota(jnp.int32, (c, c), 1)
    r4 = lax.broadcasted_iota(jnp.int32, (GLA_HEADS * c, c), 0) % c
    c4 = lax.broadcasted_iota(jnp.int32, (GLA_HEADS * c, c), 1)
    klane = lax.broadcasted_iota(jnp.int32, (c, GLA_KW), 1) // GLA_DK
    vlane = lax.broadcasted_iota(jnp.int32, (c, GLA_W), 1) // GLA_DV
    srow = lax.broadcasted_iota(jnp.int32, (GLA_W, GLA_KW), 0) // GLA_DV
    scol = lax.broadcasted_iota(jnp.int32, (GLA_W, GLA_KW), 1) // GLA_DK
    return dict(tril_f=(r >= col).astype(F32), triu_f=(r <= col).astype(F32), tril4=r4 >= c4,
                khead=[klane == h for h in range(GLA_HEADS)], vhead=[vlane == h for h in range(GLA_HEADS)],
                diag=srow == scol)


def _stack_heads(x, head_masks):
    return jnp.concatenate([jnp.where(m, x, 0.0) for m in head_masks], axis=0)


def _gla_fwd(gq, gk, gv, la):
    t = gq.shape[0]
    rg = _seq_tile(t)
    nb = t // rg
    ncb = rg // GLA_CHUNK
    c = GLA_CHUNK

    def body(q_ref, k_ref, v_ref, la_ref, o_ref, ss_ref, st_ref):
        @pl.when(pl.program_id(0) == 0)
        def _():
            st_ref[...] = jnp.zeros_like(st_ref)

        mk = _gla_masks()
        st = st_ref[...]
        for ch in range(ncb):
            rows = slice(ch * c, (ch + 1) * c)
            eq, ek, eb, ekl, ebl = _gla_decays(la_ref[rows, :], mk["tril_f"])
            qs = q_ref[rows, :] * (GLA_DK ** -0.5)
            k = k_ref[rows, :]
            v = v_ref[rows, :].astype(BF16)
            ss_ref[ch] = st
            q4 = _stack_heads(qs * eq, mk["khead"]).astype(BF16)
            a4 = jnp.where(mk["tril4"], _dg(q4, (k * ek).astype(BF16), NT), 0.0).astype(BF16)
            r4 = _dot(a4, v)
            intra = jnp.concatenate([r4[h * c:(h + 1) * c, GLA_DV * h:GLA_DV * (h + 1)] for h in range(GLA_HEADS)],
                                    axis=1)
            o_ref[rows, :] = intra + _dg((qs * eb).astype(BF16), st.astype(BF16), NT)
            st = st * ebl + jnp.where(mk["diag"], _dg(v, (k * ekl).astype(BF16), TN), 0.0)
        st_ref[...] = st

    def row(w):
        return pl.BlockSpec((rg, w), lambda i: (i, 0))

    return pl.pallas_call(
        body, name="gla_fwd", grid=(nb,),
        in_specs=[row(256), row(256), row(512), row(256)],
        out_specs=[row(512), pl.BlockSpec((ncb, GLA_W, GLA_KW), lambda i: (i, 0, 0))],
        out_shape=[jax.ShapeDtypeStruct((t, GLA_W), F32), jax.ShapeDtypeStruct((nb * ncb, GLA_W, GLA_KW), F32)],
        scratch_shapes=[pltpu.VMEM((GLA_W, GLA_KW), F32)],
        compiler_params=_cparams(1),
    )(gq, gk, gv, la)


def _gla_bwd(gq, gk, gv, la, ss, do):
    t = gq.shape[0]
    rg = _seq_tile(t)
    nb = t // rg
    ncb = rg // GLA_CHUNK
    c = GLA_CHUNK

    def body(q_ref, k_ref, v_ref, la_ref, ss_ref, do_ref, dq_ref, dk_ref, dv_ref, dla_ref, dst_ref):
        @pl.when(pl.program_id(0) == 0)
        def _():
            dst_ref[...] = jnp.zeros_like(dst_ref)

        mk = _gla_masks()
        last_row = lax.broadcasted_iota(jnp.int32, (c, GLA_KW), 0) == c - 1
        scale = GLA_DK ** -0.5
        dstn = dst_ref[...]
        for ch in reversed(range(ncb)):
            rows = slice(ch * c, (ch + 1) * c)
            eq, ek, eb, ekl, ebl = _gla_decays(la_ref[rows, :], mk["tril_f"])
            qs = q_ref[rows, :] * scale
            k = k_ref[rows, :]
            qt, kt, qh, kh = qs * eq, k * ek, qs * eb, k * ekl
            ktb, khb, qhb = kt.astype(BF16), kh.astype(BF16), qh.astype(BF16)
            v = v_ref[rows, :].astype(BF16)
            do_f = do_ref[rows, :]
            dob = do_f.astype(BF16)
            st = ss_ref[ch]
            stb = st.astype(BF16)
            dstb = dstn.astype(BF16)
            q4 = _stack_heads(qt, mk["khead"]).astype(BF16)
            do4 = _stack_heads(do_f, mk["vhead"]).astype(BF16)
            a4 = jnp.where(mk["tril4"], _dg(q4, ktb, NT), 0.0).astype(BF16)
            da4 = jnp.where(mk["tril4"], _dg(do4, v, NT), 0.0).astype(BF16)
            dv_ref[rows, :] = _dg(a4, do4, TN) + _dg(khb, dstb, NT)
            dq4 = _dot(da4, ktb)
            dqt = jnp.zeros((c, GLA_KW), F32)
            for h in range(GLA_HEADS):
                dqt = dqt + jnp.where(mk["khead"][h], dq4[h * c:(h + 1) * c], 0.0)
            dkt = _dg(da4, q4, TN)
            dqh = _dot(dob, stb)
            dkh = _dot(v, dstb)
            dbl = jnp.sum(dstn * st, axis=0, keepdims=True)
            dstn = dstn * ebl + jnp.where(mk["diag"], _dg(dob, qhb, TN), 0.0)
            dq_ref[rows, :] = scale * (dqt * eq + dqh * eb)
            dk_ref[rows, :] = dkt * ek + dkh * ekl
            dkk = dkh * kh
            db = dqt * qt - dkt * kt + dqh * qh - dkk
            db = db + jnp.where(last_row, jnp.sum(dkk, axis=0, keepdims=True) + ebl * dbl, 0.0)
            dla_ref[rows, :] = _scan_rows(db, reverse=True)
        dst_ref[...] = dstn

    def row(w):
        return pl.BlockSpec((rg, w), lambda i: (nb - 1 - i, 0))

    def rshape(w):
        return jax.ShapeDtypeStruct((t, w), F32)

    return pl.pallas_call(
        body, name="gla_bwd", grid=(nb,),
        in_specs=[row(256), row(256), row(512), row(256),
                  pl.BlockSpec((ncb, GLA_W, GLA_KW), lambda i: (nb - 1 - i, 0, 0)), row(512)],
        out_specs=[row(256), row(256), row(512), row(256)],
        out_shape=[rshape(256), rshape(256), rshape(512), rshape(256)],
        scratch_shapes=[pltpu.VMEM((GLA_W, GLA_KW), F32)],
        compiler_params=_cparams(1),
    )(gq, gk, gv, la, ss, do)


SWA_G = SWA_QH // SWA_KVH


def _swa_bias():
    n = jnp.arange(3, dtype=jnp.int32)[:, None, None]
    r = (jnp.arange(SWA_G * BLK, dtype=jnp.int32) % BLK)[None, :, None]
    c = jnp.arange(3 * BLK, dtype=jnp.int32)[None, None, :]
    seg = c // BLK
    cc = c % BLK
    qpos = n * BLK + r - PAD
    kpos = jnp.where(seg == 0, (n - 1) * BLK, jnp.where(seg == 1, n * BLK, 0)) + cc - PAD
    band = (seg < 2) & (kpos >= N_META) & (kpos <= qpos) & (qpos - kpos < WINDOW)
    meta = (seg == 2) & (kpos >= 0) & (kpos < N_META) & (kpos <= qpos)
    return jnp.where(band | meta, 0.0, NEG_INF).astype(F32)


def _swa_stack(ref, rows, kh, lo, dtype):
    parts = []
    for g in range(2):
        pair = ref[rows, 128 * (2 * kh + g):128 * (2 * kh + g + 1)]
        zero = jnp.zeros_like(pair)
        parts += [jnp.where(lo, pair, zero), jnp.where(lo, zero, pair)]
    return jnp.concatenate(parts, axis=0).astype(dtype)


def _swa_unstack(x4, lo):
    return [jnp.where(lo, x4[2 * g * BLK:(2 * g + 1) * BLK], x4[(2 * g + 1) * BLK:(2 * g + 2) * BLK])
            for g in range(2)]


def _swa_sink_col(sink_ref, kh):
    blk = lax.broadcasted_iota(jnp.int32, (SWA_G * BLK, 1), 0) // BLK
    col = jnp.full((SWA_G * BLK, 1), sink_ref[SWA_G * kh + SWA_G - 1], F32)
    for e in reversed(range(SWA_G - 1)):
        col = jnp.where(blk == e, sink_ref[SWA_G * kh + e], col)
    return col


def _swa_softmax(qk, bias, sink):
    s = qk * (SWA_HD ** -0.5) + bias
    m = jnp.maximum(jnp.max(s, axis=-1, keepdims=True), sink)
    p = jnp.exp(s - m)
    es = jnp.exp(sink - m)
    inv = 1.0 / (jnp.sum(p, axis=-1, keepdims=True) + es)
    return p * inv, es * inv


def _swa_keys(prev_ref, cur_ref, first_ref, b, ls):
    before = prev_ref[:, ls] if b == 0 else cur_ref[(b - 1) * BLK:b * BLK, ls]
    return jnp.concatenate([before, cur_ref[b * BLK:(b + 1) * BLK, ls], first_ref[:, ls]], axis=0)


def _swa_specs(rs, ns):
    bps = rs // BLK
    cur = lambda w: pl.BlockSpec((rs, w), lambda i: (jnp.minimum(i, ns - 1), 0))
    prev = lambda w: pl.BlockSpec((BLK, w), lambda i: (jnp.maximum(jnp.minimum(i, ns - 1) * bps - 1, 0), 0))
    first = lambda w: pl.BlockSpec((BLK, w), lambda i: (0, 0))
    return cur, prev, first


def _swa_fwd(sinks, sq, sk, sv):
    t = sq.shape[0]
    rs = _seq_tile(t)
    bps, ns = rs // BLK, t // rs

    def body(sink_ref, bias_ref, q_ref, kp_ref, kc_ref, km_ref, vp_ref, vc_ref, vm_ref, o_ref):
        i = pl.program_id(0)
        lo = lax.broadcasted_iota(jnp.int32, (BLK, 128), 1) < 64
        sink_cols = [_swa_sink_col(sink_ref, kh) for kh in range(SWA_KVH)]
        chains = [(b, kh) for b in range(bps) for kh in range(SWA_KVH)]
        scores = []
        for b, kh in chains:
            ls = slice(128 * kh, 128 * (kh + 1))
            q4 = _swa_stack(q_ref, slice(b * BLK, (b + 1) * BLK), kh, lo, BF16)
            scores.append(_dg(q4, _swa_keys(kp_ref, kc_ref, km_ref, b, ls), NT))
        probs = []
        for (b, kh), s in zip(chains, scores):
            p, _ = _swa_softmax(s, bias_ref[jnp.minimum(i * bps + b, 2)], sink_cols[kh])
            probs.append(p.astype(BF16))
        for (b, kh), p in zip(chains, probs):
            ls = slice(128 * kh, 128 * (kh + 1))
            rows = slice(b * BLK, (b + 1) * BLK)
            for g, pair in enumerate(_swa_unstack(_dot(p, _swa_keys(vp_ref, vc_ref, vm_ref, b, ls)), lo)):
                o_ref[rows, 128 * (2 * kh + g):128 * (2 * kh + g + 1)] = pair

    cur, prev, first = _swa_specs(rs, ns)
    bias = _swa_bias()
    return pl.pallas_call(
        body, name="swa_fwd", grid=(ns,),
        in_specs=[pl.BlockSpec(memory_space=pltpu.SMEM), _full(bias.shape), cur(512), prev(256), cur(256), first(256),
                  prev(256), cur(256), first(256)],
        out_specs=cur(512),
        out_shape=jax.ShapeDtypeStruct((t, SWA_W), F32),
        compiler_params=_cparams(1),
    )(sinks, bias, sq, sk, sk, sk, sv, sv, sv)


def _swa_bwd(sinks, sq, sk, sv, o, do, hook=None):
    t = sq.shape[0]
    rs = _seq_tile(t)
    bps, ns = rs // BLK, t // rs

    def body(sink_ref, bias_ref, q_ref, kp_ref, kc_ref, km_ref, vp_ref, vc_ref, vm_ref, o_ref, do_ref,
             dq_ref, dk_ref, dv_ref, dkm_ref, dvm_ref, dsink_ref, pk_ref, pv_ref):
        i = pl.program_id(0)

        @pl.when(i == 0)
        def _():
            pk_ref[...] = jnp.zeros_like(pk_ref)
            pv_ref[...] = jnp.zeros_like(pv_ref)
            dkm_ref[...] = jnp.zeros_like(dkm_ref)
            dvm_ref[...] = jnp.zeros_like(dvm_ref)
            dsink_ref[...] = jnp.zeros_like(dsink_ref)

        @pl.when(i == ns)
        def _():
            dk_ref[...] = pk_ref[...]
            dv_ref[...] = pv_ref[...]

        @pl.when(i < ns)
        def _():
            lo = lax.broadcasted_iota(jnp.int32, (BLK, 128), 1) < 64
            scale = SWA_HD ** -0.5
            sink_cols = [_swa_sink_col(sink_ref, kh) for kh in range(SWA_KVH)]
            parts_k = [[None] * SWA_KVH for _ in range(bps)]
            parts_v = [[None] * SWA_KVH for _ in range(bps)]
            dsinks = [jnp.zeros((1, 1), F32) for _ in range(SWA_QH)]
            chains = [(b, kh) for b in range(bps) for kh in range(SWA_KVH)]
            lanes = lambda kh: slice(128 * kh, 128 * (kh + 1))
            block = lambda b: slice(b * BLK, (b + 1) * BLK)
            q4s = [_swa_stack(q_ref, block(b), kh, lo, BF16) for b, kh in chains]
            scores = [_dg(q4, _swa_keys(kp_ref, kc_ref, km_ref, b, lanes(kh)), NT)
                      for (b, kh), q4 in zip(chains, q4s)]
            do4s = [_swa_stack(do_ref, block(b), kh, lo, F32) for b, kh in chains]
            do4bs = [d.astype(BF16) for d in do4s]
            dps = [_dg(d, _swa_keys(vp_ref, vc_ref, vm_ref, b, lanes(kh)), NT) for (b, kh), d in zip(chains, do4bs)]
            pbs, dss = [], []
            for n_chain, (b, kh) in enumerate(chains):
                p, psink = _swa_softmax(scores[n_chain], bias_ref[jnp.minimum(i * bps + b, 2)], sink_cols[kh])
                delta = jnp.sum(do4s[n_chain] * _swa_stack(o_ref, block(b), kh, lo, F32), axis=-1, keepdims=True)
                dss.append((p * (dps[n_chain] - delta) * scale).astype(BF16))
                pbs.append(p.astype(BF16))
                dsk = psink * delta
                for e in range(SWA_G):
                    h = SWA_G * kh + e
                    dsinks[h] = dsinks[h] - jnp.sum(dsk[e * BLK:(e + 1) * BLK], axis=0, keepdims=True)
            for n_chain, (b, kh) in enumerate(chains):
                kall = _swa_keys(kp_ref, kc_ref, km_ref, b, lanes(kh))
                for g, pair in enumerate(_swa_unstack(_dot(dss[n_chain], kall), lo)):
                    dq_ref[block(b), 128 * (2 * kh + g):128 * (2 * kh + g + 1)] = pair
                parts_k[b][kh] = _dg(dss[n_chain], q4s[n_chain], TN)
                parts_v[b][kh] = _dg(pbs[n_chain], do4bs[n_chain], TN)
            last = slice(rs - BLK, rs)
            for parts, out_ref, pend_ref, meta_ref in ((parts_k, dk_ref, pk_ref, dkm_ref),
                                                       (parts_v, dv_ref, pv_ref, dvm_ref)):
                for kh in range(SWA_KVH):
                    ls = slice(128 * kh, 128 * (kh + 1))
                    if bps > 1:
                        out_ref[0:rs - BLK, ls] = pend_ref[0:rs - BLK, ls]
                    out_ref[last, ls] = pend_ref[last, ls] + parts[0][kh][0:BLK]
                    meta = parts[0][kh][2 * BLK:3 * BLK]
                    for b in range(bps):
                        own = parts[b][kh][BLK:2 * BLK]
                        if b + 1 < bps:
                            own = own + parts[b + 1][kh][0:BLK]
                            meta = meta + parts[b + 1][kh][2 * BLK:3 * BLK]
                        pend_ref[b * BLK:(b + 1) * BLK, ls] = own
                    meta_ref[:, ls] += meta
            for h in range(SWA_QH):
                dsink_ref[h:h + 1, :] += jnp.broadcast_to(dsinks[h], (1, 128))

    cur, prev, first = _swa_specs(rs, ns)
    late = lambda w: pl.BlockSpec((rs, w), lambda i: (jnp.maximum(i - 1, 0), 0))
    bias = _swa_bias()
    return _pallas(
        body, name="swa_bwd", grid=(ns + 1,),
        in_specs=[pl.BlockSpec(memory_space=pltpu.SMEM), _full(bias.shape), cur(512), prev(256), cur(256), first(256),
                  prev(256), cur(256), first(256), cur(512), cur(512)],
        out_specs=[cur(512), late(256), late(256), first(256), first(256), _full((SWA_QH, 128))],
        out_shape=[jax.ShapeDtypeStruct((t, SWA_W), F32), jax.ShapeDtypeStruct((t, 256), F32),
                   jax.ShapeDtypeStruct((t, 256), F32), jax.ShapeDtypeStruct((BLK, 256), F32),
                   jax.ShapeDtypeStruct((BLK, 256), F32), jax.ShapeDtypeStruct((SWA_QH, 128), F32)],
        scratch_shapes=[pltpu.VMEM((rs, 256), F32), pltpu.VMEM((rs, 256), F32)],
        args=(sinks, bias, sq, sk, sk, sk, sv, sv, sv, o, do), hook=hook)


def _mix_out(h1, ogla, gg, oswa, wgn, wsn, wout, wpost):
    t = h1.shape[0]
    tm = _row_tile(t)

    def body(h_ref, og_ref, gg_ref, os_ref, wgn_ref, wsn_ref, wout_ref, wpost_ref, h2_ref, cat_ref, m_ref):
        parts = []
        for h in range(GLA_HEADS):
            ls = slice(GLA_DV * h, GLA_DV * (h + 1))
            y, _, _ = _rms(og_ref[:, ls], wgn_ref[...])
            g = gg_ref[:, ls]
            parts.append(y * (g * _sigmoid(g)))
        ys, _, _ = _rms(os_ref[...], wsn_ref[...])
        cat = jnp.concatenate(parts + [ys], axis=1).astype(BF16)
        cat_ref[...] = cat
        m = _dot(cat, wout_ref[...])
        m_ref[...] = m
        y, _, _ = _rms(m, wpost_ref[...])
        h2_ref[...] = h_ref[...] + y

    def row(w):
        return pl.BlockSpec((tm, w), lambda i: (i, 0))

    return pl.pallas_call(
        body, name="mix_out", grid=(t // tm,),
        in_specs=[row(D_MODEL), row(512), row(512), row(512), _full((1, GLA_DV)), _full((1, SWA_W)),
                  _full((D_MODEL, D_MODEL)), _full((1, D_MODEL))],
        out_specs=[row(D_MODEL), row(D_MODEL), row(D_MODEL)],
        out_shape=[jax.ShapeDtypeStruct((t, D_MODEL), F32), jax.ShapeDtypeStruct((t, D_MODEL), BF16),
                   jax.ShapeDtypeStruct((t, D_MODEL), F32)],
        compiler_params=_cparams(1),
    )(h1, ogla, gg, oswa, wgn, wsn, wout, wpost)


def _mix_out_bwd(dh2, m, ogla, gg, oswa, wgn, wsn, wout, wpost, hook=None):
    t = dh2.shape[0]
    tm = _row_tile(t)

    def body(dh_ref, m_ref, og_ref, gg_ref, os_ref, wgn_ref, wsn_ref, wout_ref, wpost_ref,
             dog_ref, dgg_ref, dos_ref, dm_ref, dwpost_ref, dwgn_ref, dwsn_ref):
        @pl.when(pl.program_id(0) == 0)
        def _():
            dwpost_ref[...] = jnp.zeros_like(dwpost_ref)
            dwgn_ref[...] = jnp.zeros_like(dwgn_ref)
            dwsn_ref[...] = jnp.zeros_like(dwsn_ref)

        wpost = wpost_ref[...]
        _, mh, r = _rms(m_ref[...], wpost)
        dm, dw = _rms_bwd(mh, r, wpost, dh_ref[...])
        dwpost_ref[...] += dw
        dmb = dm.astype(BF16)
        dm_ref[...] = dmb
        dcat = _dg(dmb, wout_ref[...], NT)
        wgn = wgn_ref[...]
        for h in range(GLA_HEADS):
            ls = slice(GLA_DV * h, GLA_DV * (h + 1))
            dog = dcat[:, ls]
            g = gg_ref[:, ls]
            sg = _sigmoid(g)
            y, xh, r = _rms(og_ref[:, ls], wgn)
            dgg_ref[:, ls] = dog * y * (sg * (1.0 + g * (1.0 - sg)))
            dx, dw = _rms_bwd(xh, r, wgn, dog * (g * sg))
            dog_ref[:, ls] = dx
            dwgn_ref[...] += dw
        wsn = wsn_ref[...]
        _, xh, r = _rms(os_ref[...], wsn)
        dx, dw = _rms_bwd(xh, r, wsn, dcat[:, GLA_W:])
        dos_ref[...] = dx
        dwsn_ref[...] += dw

    def row(w):
        return pl.BlockSpec((tm, w), lambda i: (i, 0))

    def rshape(w, dt=F32):
        return jax.ShapeDtypeStruct((t, w), dt)

    return _pallas(
        body, name="mix_out_bwd", grid=(t // tm,),
        in_specs=[row(D_MODEL), row(D_MODEL), row(512), row(512), row(512), _full((1, GLA_DV)), _full((1, SWA_W)),
                  _full((D_MODEL, D_MODEL)), _full((1, D_MODEL))],
        out_specs=[row(512), row(512), row(512), row(D_MODEL), _full((1, D_MODEL)), _full((1, GLA_DV)),
                   _full((1, SWA_W))],
        out_shape=[rshape(512), rshape(512), rshape(512), rshape(D_MODEL, BF16),
                   jax.ShapeDtypeStruct((1, D_MODEL), F32), jax.ShapeDtypeStruct((1, GLA_DV), F32),
                   jax.ShapeDtypeStruct((1, SWA_W), F32)],
        args=(dh2, m, ogla, gg, oswa, wgn, wsn, wout, wpost), hook=hook)


def _mix_in_bwd(dh2, h1, wmixpre, winp, wa2p, bap, cos, sin, ga, dgq, dgk, dgv, dgg, dla, dsq, dsk, dsv, dkm, dvm):
    t = h1.shape[0]
    tm = _row_tile(t)

    def body(dh2_ref, h_ref, w_ref, win_ref, wa2_ref, ba_ref, cos_ref, sin_ref, ga_ref, dgq_ref, dgk_ref, dgv_ref,
             dgg_ref, dla_ref, dsq_ref, dsk_ref, dsv_ref, dkm_ref, dvm_ref,
             dh1_ref, dproj_ref, dw_ref, dwa2_ref, dba_ref):
        i = pl.program_id(0)

        @pl.when(i == 0)
        def _():
            dw_ref[...] = jnp.zeros_like(dw_ref)
            dwa2_ref[...] = jnp.zeros_like(dwa2_ref)
            dba_ref[...] = jnp.zeros_like(dba_ref)

        first = (i == 0).astype(F32)
        c = cos_ref[...]
        s = -sin_ref[...]
        fh = _first_half_mask(tm)
        dproj_ref[:, P_GQ:P_GK] = dgq_ref[...].astype(BF16)
        dproj_ref[:, P_GK:P_GV] = dgk_ref[...].astype(BF16)
        dproj_ref[:, P_GV:P_GG] = dgv_ref[...].astype(BF16)
        dproj_ref[:, P_GG:P_GA] = dgg_ref[...].astype(BF16)
        gab = ga_ref[...].astype(BF16)
        z = _dot(gab, wa2_ref[...]) + ba_ref[...]
        row_id = i * tm + lax.broadcasted_iota(jnp.int32, (tm, 1), 0)
        dz = jnp.where(row_id >= PAD, dla_ref[...] * (1.0 / GLA_TAU) * (1.0 - _sigmoid(z)), 0.0)
        dzb = dz.astype(BF16)
        dba_ref[...] += jnp.sum(dz, axis=0, keepdims=True)
        dwa2_ref[...] += _dg(gab, dzb, TN)
        dproj_ref[:, P_GA:P_SQ] = _dg(dzb, wa2_ref[...], NT).astype(BF16)
        for k in range(4):
            dy = dsq_ref[:, 128 * k:128 * (k + 1)]
            dproj_ref[:, P_SQ + 128 * k:P_SQ + 128 * (k + 1)] = (dy * c + _rot_half(dy, fh) * s).astype(BF16)
        for k in range(2):
            ls = slice(128 * k, 128 * (k + 1))
            dy = dsk_ref[:, ls]
            dy = jnp.concatenate([dy[:BLK] + first * dkm_ref[:, ls], dy[BLK:]], axis=0) if tm > BLK else (
                dy + first * dkm_ref[:, ls])
            dproj_ref[:, P_SK + 128 * k:P_SK + 128 * (k + 1)] = (dy * c + _rot_half(dy, fh) * s).astype(BF16)
            dv = dsv_ref[:, ls]
            dv = jnp.concatenate([dv[:BLK] + first * dvm_ref[:, ls], dv[BLK:]], axis=0) if tm > BLK else (
                dv + first * dvm_ref[:, ls])
            dproj_ref[:, P_SV + 128 * k:P_SV + 128 * (k + 1)] = dv.astype(BF16)
        dn = _dg(dproj_ref[...], win_ref[...], NT)
        w = w_ref[...]
        _, hh, r = _rms(h_ref[...], w)
        dx, dw = _rms_bwd(hh, r, w, dn)
        dw_ref[...] += dw
        dh1_ref[...] = dh2_ref[...] + dx

    def row(w):
        return pl.BlockSpec((tm, w), lambda i: (i, 0))

    return pl.pallas_call(
        body, name="mix_in_bwd", grid=(t // tm,),
        in_specs=[row(D_MODEL), row(D_MODEL), _full((1, D_MODEL)), _full((D_MODEL, P_END)), _full((128, GLA_KW)),
                  _full((1, GLA_KW)), row(128), row(128), row(128), row(256), row(256), row(512), row(512), row(256),
                  row(512), row(256), row(256), _full((BLK, 256)), _full((BLK, 256))],
        out_specs=[row(D_MODEL), row(P_END), _full((1, D_MODEL)), _full((128, GLA_KW)), _full((1, GLA_KW))],
        out_shape=[jax.ShapeDtypeStruct((t, D_MODEL), F32), jax.ShapeDtypeStruct((t, P_END), BF16),
                   jax.ShapeDtypeStruct((1, D_MODEL), F32), jax.ShapeDtypeStruct((128, GLA_KW), F32),
                   jax.ShapeDtypeStruct((1, GLA_KW), F32)],
        compiler_params=_cparams(1),
    )(dh2, h1, wmixpre, winp, wa2p, bap, cos, sin, ga, dgq, dgk, dgv, dgg, dla, dsq, dsk, dsv, dkm, dvm)


def _adamw_update(w, g, m, v):
    m = ADAM_B1 * m + (1.0 - ADAM_B1) * g
    v = ADAM_B2 * v + (1.0 - ADAM_B2) * (g * g)
    m_hat = m / (1.0 - ADAM_B1 ** ADAM_STEP)
    v_hat = v / (1.0 - ADAM_B2 ** ADAM_STEP)
    return -ADAM_LR * (m_hat / (jnp.sqrt(v_hat) + ADAM_EPS) + ADAM_WD * w), m, v


def _adamw_halves(w, g_mine, g_other, m, v, c_idx, row0=0):
    r, c = w.shape
    h = g_mine.shape[0]
    tr = _div_tile(math.gcd(r, h))
    nth = h // tr
    t0 = row0 // tr
    assert t0 * tr == row0

    def body(c_ref, w_ref, gm_ref, go_ref, m_ref, v_ref, g_ref, d_ref, nm_ref, nv_ref):
        hh = (t0 + pl.program_id(0)) // nth
        g = jnp.where(hh == c_ref[0], gm_ref[...], go_ref[...])
        g_ref[...] = g
        d_ref[...], nm_ref[...], nv_ref[...] = _adamw_update(w_ref[...], g, m_ref[...], v_ref[...])

    spec = pl.BlockSpec((tr, c), lambda i, c_ref: (i, 0))

    def gspec(is_mine):
        def index(i, c_ref):
            used = ((t0 + i) // nth == c_ref[0]) == is_mine
            return (jnp.where(used, (t0 + i) % nth, 0), 0)
        return pl.BlockSpec((tr, c), index)

    shape = jax.ShapeDtypeStruct((r, c), F32)
    return pl.pallas_call(
        body, name="adamw_halves",
        grid_spec=pltpu.PrefetchScalarGridSpec(
            num_scalar_prefetch=1, grid=(r // tr,), in_specs=[spec, gspec(True), gspec(False), spec, spec],
            out_specs=[spec] * 4),
        out_shape=[shape] * 4, compiler_params=_cparams(1),
    )(c_idx, w, g_mine, g_other, m, v)


def _place():
    x, y, c = lax.axis_index("x"), lax.axis_index("y"), lax.axis_index("c")
    chips = [(1 - x, y), (x, 1 - y), (1 - x, 1 - y)]
    return x, y, c, chips


def _remote(send_sem, recv_sem, src, dst, to):
    return pltpu.make_async_remote_copy(src_ref=src, dst_ref=dst, send_sem=send_sem, recv_sem=recv_sem,
                                        device_id=to, device_id_type=MESH)


def _half(ref_rows, c):
    h = ref_rows // 2
    return pl.ds(pl.multiple_of(c * h, 8), h)


def _own_slot(shard, q):
    return lax.dynamic_update_slice(jnp.zeros((N_CHIPS,) + shard.shape, shard.dtype), shard[None], (q, 0, 0))


class _GatherChips:
    has_mid = True

    def __init__(self, bufs):
        n = len(bufs)
        self.inputs = list(bufs)
        self.out_shape = [jax.ShapeDtypeStruct(b.shape, b.dtype) for b in bufs]
        self.aliases = [(t, t) for t in range(n)]
        self.scratch = [pltpu.SemaphoreType.DMA((n, 6)), pltpu.SemaphoreType.DMA((n, 6))]

    def start(self, ins, outs, scr):
        send, recv = scr
        x, y, c, chips = _place()
        q = 2 * x + y
        for t, (i_ref, o_ref) in enumerate(zip(ins, outs)):
            rows = _half(i_ref.shape[1], c)
            for j, (cx, cy) in enumerate(chips):
                _remote(send.at[t, j], recv.at[t, j], i_ref.at[q, rows], o_ref.at[q, rows], (cx, cy, c)).start()

    def mid(self, ins, outs, scr):
        send, recv = scr
        x, y, c, chips = _place()
        for t, o_ref in enumerate(outs):
            rows = _half(o_ref.shape[1], c)
            for j, (cx, cy) in enumerate(chips):
                slot = o_ref.at[2 * cx + cy, rows]
                _remote(send.at[t, j], recv.at[t, j], slot, slot, (cx, cy, c)).wait_recv()
                _remote(send.at[t, 3 + j], recv.at[t, 3 + j], slot, slot, (x, y, 1 - c)).start()

    def finish(self, ins, outs, scr):
        send, recv = scr
        x, y, c, chips = _place()
        for t, o_ref in enumerate(outs):
            mine, other = _half(o_ref.shape[1], c), _half(o_ref.shape[1], 1 - c)
            for j, (cx, cy) in enumerate(chips):
                slot = o_ref.at[2 * cx + cy, other]
                _remote(send.at[t, 3 + j], recv.at[t, 3 + j], slot, slot, (x, y, 1 - c)).wait_recv()
            for j, (cx, cy) in enumerate(chips):
                sent = o_ref.at[2 * cx + cy, mine]
                _remote(send.at[t, j], recv.at[t, j], sent, sent, (cx, cy, c)).wait_send()
                _remote(send.at[t, 3 + j], recv.at[t, 3 + j], sent, sent, (x, y, 1 - c)).wait_send()


class _PairExchange:
    has_mid = False
    aliases = ()

    def __init__(self, arrs):
        n = len(arrs)
        self.inputs = list(arrs)
        self.out_shape = [jax.ShapeDtypeStruct((a.shape[0], a.shape[1] // 2, a.shape[2]), a.dtype) for a in arrs]
        self.scratch = [pltpu.SemaphoreType.DMA((n,)), pltpu.SemaphoreType.DMA((n,))]

    def _copies(self, ins, outs, scr):
        send, recv = scr
        x, y, c, _ = _place()
        return [_remote(send.at[t], recv.at[t], i_ref.at[:, _half(i_ref.shape[1], 1 - c)], o_ref, (x, y, 1 - c))
                for t, (i_ref, o_ref) in enumerate(zip(ins, outs))]

    def start(self, ins, outs, scr):
        for cp in self._copies(ins, outs, scr):
            cp.start()

    def finish(self, ins, outs, scr):
        for cp in self._copies(ins, outs, scr):
            cp.wait()


class _ChipScatter:
    has_mid = False
    aliases = ()

    def __init__(self, arrs):
        n = len(arrs)
        self.inputs = list(arrs)
        self.out_shape = [jax.ShapeDtypeStruct((3,) + a.shape[1:], a.dtype) for a in arrs]
        self.scratch = [pltpu.SemaphoreType.DMA((n, 3)), pltpu.SemaphoreType.DMA((n, 3))]

    def _copies(self, ins, outs, scr):
        send, recv = scr
        x, y, c, chips = _place()
        return [_remote(send.at[t, j], recv.at[t, j], i_ref.at[2 * cx + cy], o_ref.at[j], (cx, cy, c))
                for t, (i_ref, o_ref) in enumerate(zip(ins, outs)) for j, (cx, cy) in enumerate(chips)]

    def start(self, ins, outs, scr):
        for cp in self._copies(ins, outs, scr):
            cp.start()

    def finish(self, ins, outs, scr):
        for cp in self._copies(ins, outs, scr):
            cp.wait()


class _PairShare:
    has_mid = False
    aliases = ()

    def __init__(self, arrs):
        n = len(arrs)
        self.inputs = list(arrs)
        self.out_shape = [jax.ShapeDtypeStruct(a.shape, a.dtype) for a in arrs]
        self.scratch = [pltpu.SemaphoreType.DMA((n,)), pltpu.SemaphoreType.DMA((n,))]

    def _copies(self, ins, outs, scr):
        send, recv = scr
        x, y, c, _ = _place()
        return [_remote(send.at[t], recv.at[t], i_ref, o_ref, (x, y, 1 - c))
                for t, (i_ref, o_ref) in enumerate(zip(ins, outs))]

    def start(self, ins, outs, scr):
        for cp in self._copies(ins, outs, scr):
            cp.start()

    def finish(self, ins, outs, scr):
        for cp in self._copies(ins, outs, scr):
            cp.wait()


def _comm_call(hook, name):
    n_in, n_out = len(hook.inputs), len(hook.out_shape)

    def body(*refs):
        ins, outs, scr = refs[:n_in], refs[n_in:n_in + n_out], refs[n_in + n_out:]
        hook.start(ins, outs, scr)
        if hook.has_mid:
            hook.mid(ins, outs, scr)
        hook.finish(ins, outs, scr)

    return pl.pallas_call(body, name=name, in_specs=[ANY] * n_in, out_specs=[ANY] * n_out,
                          out_shape=list(hook.out_shape), scratch_shapes=list(hook.scratch),
                          input_output_aliases=dict(hook.aliases))(*hook.inputs)


def _all_gather_devices(vecs):
    n = len(vecs)

    def body(*refs):
        x_refs, out_refs = refs[:n], refs[n:2 * n]
        send_sems, recv_sems, local_sems = refs[2 * n:]
        x, y, c, chips = _place()
        me, sibling = (x, y, c), (x, y, 1 - c)
        waits = []
        for t, (x_ref, out_ref) in enumerate(zip(x_refs, out_refs)):
            def slot(px, py, pc, out_ref=out_ref):
                return out_ref.at[4 * px + 2 * py + pc]

            def copy(k, block, to, src=None, t=t, slot=slot):
                return pltpu.make_async_remote_copy(
                    src_ref=slot(*block) if src is None else src, dst_ref=slot(*block), send_sem=send_sems.at[t, k],
                    recv_sem=recv_sems.at[t, k], device_id=to, device_id_type=MESH)

            mine = pltpu.make_async_copy(x_ref, slot(*me), local_sems.at[t])
            mine.start()
            first = [copy(0, me, sibling, src=x_ref)]
            first += [copy(1 + j, me, (*chip, c), src=x_ref) for j, chip in enumerate(chips)]
            for cp in first:
                cp.start()
            waits.append((copy, mine, first))
        for copy, mine, first in waits:
            passed = [copy(4 + j, (*chip, c), sibling) for j, chip in enumerate(chips)]
            for j, chip in enumerate(chips):
                copy(1 + j, (*chip, c), me).wait_recv()
                passed[j].start()
            copy(0, sibling, me).wait_recv()
            for j, chip in enumerate(chips):
                copy(4 + j, (*chip, 1 - c), me).wait_recv()
            for cp in first + passed:
                cp.wait_send()
            mine.wait()

    vmem = pl.BlockSpec(memory_space=pltpu.VMEM)
    return pl.pallas_call(
        body, name="all_gather_devices", in_specs=[vmem] * n, out_specs=[vmem] * n,
        out_shape=[jax.ShapeDtypeStruct((N_DEV,) + v.shape, v.dtype) for v in vecs],
        scratch_shapes=[pltpu.SemaphoreType.DMA((n, 7)), pltpu.SemaphoreType.DMA((n, 7)),
                        pltpu.SemaphoreType.DMA((n,))],
    )(*vecs)


def _pair_sum(g, other, c_idx):
    nq, r, w = g.shape
    h = r // 2
    tr = _div_tile(h)
    nt = h // tr

    def body(c_ref, g_ref, o_ref, s_ref):
        s_ref[...] = (g_ref[...].astype(F32) + o_ref[...].astype(F32)).astype(s_ref.dtype)

    return pl.pallas_call(
        body, name="pair_sum",
        grid_spec=pltpu.PrefetchScalarGridSpec(
            num_scalar_prefetch=1, grid=(nq, nt),
            in_specs=[pl.BlockSpec((None, tr, w), lambda k, i, c_ref: (k, c_ref[0] * nt + i, 0)),
                      pl.BlockSpec((None, tr, w), lambda k, i, c_ref: (k, i, 0))],
            out_specs=pl.BlockSpec((None, tr, w), lambda k, i, c_ref: (k, i, 0))),
        out_shape=jax.ShapeDtypeStruct((nq, h, w), g.dtype),
        compiler_params=_cparams(2),
    )(c_idx, g, other)


def _chip_sum(s, others, q_idx):
    _, h, w = s.shape
    tr = _div_tile(h)

    def body(q_ref, s_ref, o_ref, out_ref):
        out_ref[...] = ((s_ref[...].astype(F32) + o_ref[0].astype(F32)) + o_ref[1].astype(F32)) + o_ref[2].astype(F32)

    return pl.pallas_call(
        body, name="chip_sum",
        grid_spec=pltpu.PrefetchScalarGridSpec(
            num_scalar_prefetch=1, grid=(h // tr,),
            in_specs=[pl.BlockSpec((None, tr, w), lambda i, q_ref: (q_ref[0], i, 0)),
                      pl.BlockSpec((3, tr, w), lambda i, q_ref: (0, i, 0))],
            out_specs=pl.BlockSpec((tr, w), lambda i, q_ref: (i, 0))),
        out_shape=jax.ShapeDtypeStruct((h, w), F32),
        compiler_params=_cparams(1),
    )(q_idx, s, others)


def _small_update(q_idx, parts, ws, ms, vs, col_block):
    n = len(parts)
    has_w = [w is not None for w in ws]

    def body(q_ref, *refs):
        pos = 0
        ins = []
        for t in range(n):
            k = 4 if has_w[t] else 1
            ins.append(refs[pos:pos + k])
            pos += k
        outs = refs[pos:]
        opos = 0
        for t in range(n):
            p_ref = ins[t][0]
            g = p_ref[0]
            for s in range(1, p_ref.shape[0]):
                g = g + p_ref[s]
            if has_w[t]:
                _, w_ref, m_ref, v_ref = ins[t]
                g_ref, d_ref, nm_ref, nv_ref = outs[opos:opos + 4]
                opos += 4
                g_ref[...] = g
                d_ref[...], nm_ref[...], nv_ref[...] = _adamw_update(w_ref[...], g, m_ref[...], v_ref[...])
            else:
                outs[opos][...] = g
                opos += 1

    def whole(shape):
        nd = len(shape)
        return pl.BlockSpec(shape, lambda i, q_ref: (0,) * nd)

    in_specs, out_specs, out_shape, args = [], [], [], []
    for t in range(n):
        k, r, wf = parts[t].shape
        if col_block[t]:
            w = wf // N_CHIPS
            in_specs.append(pl.BlockSpec((k, r, w), lambda i, q_ref: (0, 0, q_ref[0])))
        else:
            w = wf
            in_specs.append(whole((k, r, wf)))
        args.append(parts[t])
        if has_w[t]:
            assert ws[t].shape == (r, w), (ws[t].shape, r, w)
            in_specs += [whole((r, w))] * 3
            args += [ws[t], ms[t], vs[t]]
            out_specs += [whole((r, w))] * 4
            out_shape += [jax.ShapeDtypeStruct((r, w), F32)] * 4
        else:
            out_specs.append(whole((r, w)))
            out_shape.append(jax.ShapeDtypeStruct((r, w), F32))
    return pl.pallas_call(
        body, name="small_update",
        grid_spec=pltpu.PrefetchScalarGridSpec(num_scalar_prefetch=1, grid=(1,), in_specs=in_specs,
                                               out_specs=out_specs),
        out_shape=out_shape, compiler_params=_cparams(1),
    )(q_idx, *args)


def _pack_win(w_in):
    o = np.cumsum((0,) + IN_SPLITS)
    gq, gk, gv, gg, ga, sq, sk, sv = [w_in[:, o[i]:o[i + 1]] for i in range(8)]
    z = jnp.zeros((w_in.shape[0], 128 - GLA_RANK), w_in.dtype)
    dup = lambda a: jnp.concatenate([a[:, :64], a[:, :64], a[:, 64:], a[:, 64:]], axis=1)
    return jnp.concatenate([gq, gk, gv, gg, ga, z, sq, dup(sk), dup(sv)], axis=1)


def _unpack_dwin(d):
    und = lambda a: jnp.concatenate([a[:, 0:64] + a[:, 64:128], a[:, 128:192] + a[:, 192:256]], axis=1)
    return jnp.concatenate([d[:, :P_GA], d[:, P_GA:P_GA + GLA_RANK], d[:, P_SQ:P_SK], und(d[:, P_SK:P_SV]),
                            und(d[:, P_SV:P_END])], axis=1)


def _local_step(x, target, meta, p):
    s = x.shape[0]
    t = s + BLK
    h0 = jnp.concatenate([jnp.zeros((PAD, D_MODEL), F32), meta, x], axis=0)
    cos, sin = _rope_tables(t)

    h1, n1, g1, u1, a1, f1 = _ffn_fwd(h0, p["ffn1_pre_norm"], p["ffn1_w_gate"], p["ffn1_w_up"], p["ffn1_w_down"],
                                      p["ffn1_post_norm"])
    n2, gq, gk, gv, gg, ga, la, sq, sk, sv = _mix_proj(h1, p["mix_pre_norm"], p["w_in"], p["gla_w_a2"], p["gla_b_a"],
                                                       cos, sin)
    ogla, ss = _gla_fwd(gq, gk, gv, la)
    oswa = _swa_fwd(p["swa_sinks"], sq, sk, sv)
    h2, cat, m = _mix_out(h1, ogla, gg, oswa, p["gla_out_norm"], p["swa_out_norm"], p["w_out"], p["mix_post_norm"])
    grads = {}
    dy, n3, g3, u3, a3, df3, grads["ffn2_post_norm"], sse = _ffn_fwd(
        h2, p["ffn2_pre_norm"], p["ffn2_w_gate"], p["ffn2_w_up"], p["ffn2_w_down"], p["ffn2_post_norm"],
        target=target)

    dh2, dg3, du3, grads["ffn2_pre_norm"] = _ffn_bwd(
        dy, h2, None, g3, u3, p["ffn2_pre_norm"], p["ffn2_w_gate"], p["ffn2_w_up"], p["ffn2_w_down"],
        p["ffn2_post_norm"], df=df3)
    (gud,) = _ffn_wgrad(n3, df3, dg3, du3, a3)
    grads["ffn2_w_gate"], grads["ffn2_w_up"], grads["ffn2_w_down"] = gud[:, :FJ], gud[:, FJ:2 * FJ], gud[:, 2 * FJ:]

    dogla, dgg, doswa, dm, grads["mix_post_norm"], grads["gla_out_norm"], grads["swa_out_norm"] = _mix_out_bwd(
        dh2, m, ogla, gg, oswa, p["gla_out_norm"], p["swa_out_norm"], p["w_out"], p["mix_post_norm"])
    grads["w_out"] = _xty(cat, dm)
    dsq, dsk, dsv, dkm, dvm, dsinks = _swa_bwd(p["swa_sinks"], sq, sk, sv, oswa, doswa)
    grads["swa_sinks"] = dsinks[:, 0]
    dgq, dgk, dgv, dla = _gla_bwd(gq, gk, gv, la, ss, dogla)
    dh1, dproj, grads["mix_pre_norm"], dwa2p, grads["gla_b_a"] = _mix_in_bwd(
        dh2, h1, p["mix_pre_norm"], p["w_in"], p["gla_w_a2"], p["gla_b_a"], cos, sin, ga, dgq, dgk, dgv, dgg, dla,
        dsq, dsk, dsv, dkm, dvm)
    grads["gla_w_a2"] = dwa2p[:GLA_RANK]
    grads["w_in"] = _unpack_dwin(_xty(n2, dproj))

    dh0, df1, dg1, du1, grads["ffn1_pre_norm"], grads["ffn1_post_norm"] = _ffn_bwd(
        dh1, h0, f1, g1, u1, p["ffn1_pre_norm"], p["ffn1_w_gate"], p["ffn1_w_up"], p["ffn1_w_down"],
        p["ffn1_post_norm"])
    (gud,) = _ffn_wgrad(n1, df1, dg1, du1, a1)
    grads["ffn1_w_gate"], grads["ffn1_w_up"], grads["ffn1_w_down"] = gud[:, :FJ], gud[:, FJ:2 * FJ], gud[:, 2 * FJ:]
    grads["meta_tokens"] = dh0[PAD:BLK]
    return sse[0, 0], dh0[BLK:], grads


WEIGHTS = ['meta_tokens', 'ffn1_pre_norm', 'ffn1_w_gate', 'ffn1_w_up', 'ffn1_w_down', 'ffn1_post_norm',
           'mix_pre_norm', 'w_in', 'gla_w_a2', 'gla_b_a', 'gla_out_norm', 'swa_sinks', 'swa_out_norm', 'w_out',
           'mix_post_norm', 'ffn2_pre_norm', 'ffn2_w_gate', 'ffn2_w_up', 'ffn2_w_down', 'ffn2_post_norm']
BIG = ['ffn1_w_gate', 'ffn1_w_up', 'ffn1_w_down', 'w_in', 'w_out', 'ffn2_w_gate', 'ffn2_w_up', 'ffn2_w_down']
SMALL = [n for n in WEIGHTS if n not in BIG]
FJ = D_FF // N_CHIPS
D_IN_J = D_IN // N_CHIPS
D_OUT_J = D_MODEL // N_CHIPS
TRANSPOSED = ('ffn1_w_gate', 'ffn1_w_up', 'ffn2_w_gate', 'ffn2_w_up')


def _shard2d(name, a):
    return a[0].T if name in TRANSPOSED else a[0]


def _unshard2d(name, a):
    return (a.T if name in TRANSPOSED else a)[None]


def kernel(x, meta_tokens, ffn1_pre_norm, ffn1_w_gate, ffn1_w_up, ffn1_w_down, ffn1_post_norm, mix_pre_norm, w_in, gla_w_a2, gla_b_a, gla_out_norm, swa_sinks, swa_out_norm, w_out, mix_post_norm, ffn2_pre_norm, ffn2_w_gate, ffn2_w_up, ffn2_w_down, ffn2_post_norm, loss_target, m_meta_tokens, m_ffn1_pre_norm, m_ffn1_w_gate, m_ffn1_w_up, m_ffn1_w_down, m_ffn1_post_norm, m_mix_pre_norm, m_w_in, m_gla_w_a2, m_gla_b_a, m_gla_out_norm, m_swa_sinks, m_swa_out_norm, m_w_out, m_mix_post_norm, m_ffn2_pre_norm, m_ffn2_w_gate, m_ffn2_w_up, m_ffn2_w_down, m_ffn2_post_norm, v_meta_tokens, v_ffn1_pre_norm, v_ffn1_w_gate, v_ffn1_w_up, v_ffn1_w_down, v_ffn1_post_norm, v_mix_pre_norm, v_w_in, v_gla_w_a2, v_gla_b_a, v_gla_out_norm, v_swa_sinks, v_swa_out_norm, v_w_out, v_mix_post_norm, v_ffn2_pre_norm, v_ffn2_w_gate, v_ffn2_w_up, v_ffn2_w_down, v_ffn2_post_norm):
    args = dict(locals())
    w = {n: args[n] for n in WEIGHTS}
    mom = {n: args["m_" + n] for n in WEIGHTS}
    var = {n: args["v_" + n] for n in WEIGHTS}
    cx, cy, cc = lax.axis_index("x"), lax.axis_index("y"), lax.axis_index("c")
    q_idx = (2 * cx + cy).astype(jnp.int32).reshape(1)
    c_idx = cc.astype(jnp.int32).reshape(1)

    q_chip = 2 * cx + cy
    bf = {n: _own_slot(_shard2d(n, w[n]).astype(BF16), q_chip) for n in BIG}
    qc_idx = jnp.stack([q_chip, cc]).astype(jnp.int32)
    early = _GatherChips([_own_slot(w["meta_tokens"], q_chip),
                          _own_slot(w["gla_w_a2"].reshape(GLA_RANK, GLA_KW // N_CHIPS), q_chip)])
    meta4, wa24 = _comm_call(early, "gather_small")
    meta_full = meta4.transpose(1, 0, 2).reshape(N_META, D_MODEL)
    wa2p = jnp.pad(wa24.transpose(1, 0, 2).reshape(GLA_RANK, GLA_KW), ((0, 128 - GLA_RANK), (0, 0))).astype(BF16)
    sinks = w["swa_sinks"].reshape(SWA_QH)

    seq, target = x[0], loss_target[0]
    t = seq.shape[0] + BLK
    h0, n1 = _embed_norm(seq, meta_full, w["ffn1_pre_norm"])
    cos, sin = _rope_tables(t)
    late = _GatherChips([bf["w_in"], bf["w_out"], bf["ffn2_w_gate"], bf["ffn2_w_up"], bf["ffn2_w_down"]])
    (h1, g1, u1, a1, f1), (wg1, wu1, wd1), (win4, wout4, wg2, wu2, wd2) = _ffn_fwd_gather(
        h0, n1, [bf["ffn1_w_gate"], bf["ffn1_w_up"], bf["ffn1_w_down"]], w["ffn1_post_norm"], qc_idx, late)
    winp = _pack_win(win4.transpose(1, 0, 2).reshape(D_MODEL, D_IN))
    wout = wout4.reshape(D_MODEL, D_MODEL)
    n2, gq, gk, gv, gg, ga, la, sq, sk, sv = _mix_proj(h1, w["mix_pre_norm"], winp, wa2p, w["gla_b_a"], cos, sin)
    ogla, ss = _gla_fwd(gq, gk, gv, la)
    oswa = _swa_fwd(sinks, sq, sk, sv)
    h2, cat, m = _mix_out(h1, ogla, gg, oswa, w["gla_out_norm"], w["swa_out_norm"], wout, w["mix_post_norm"])
    g = {}
    dy, n3, g3, u3, a3, df3, g["ffn2_post_norm"], sse = _ffn_fwd(
        h2, w["ffn2_pre_norm"], wg2, wu2, wd2, w["ffn2_post_norm"], target=target)
    loss = lax.psum(sse[0, 0] * (0.5 / D_MODEL), ("x", "y", "c"))

    dh2, dg3, du3, g["ffn2_pre_norm"] = _ffn_bwd(
        dy, h2, None, g3, u3, w["ffn2_pre_norm"], wg2, wu2, wd2, w["ffn2_post_norm"], df=df3)
    (gf2,) = _ffn_wgrad(n3, df3, dg3, du3, a3)
    (dogla, dgg, doswa, dm, g["mix_post_norm"], g["gla_out_norm"], g["swa_out_norm"]), (rgf2,) = _mix_out_bwd(
        dh2, m, ogla, gg, oswa, w["gla_out_norm"], w["swa_out_norm"], wout, w["mix_post_norm"],
        hook=_PairExchange([gf2]))
    sgf2 = _pair_sum(gf2, rgf2, c_idx)
    gout = _xty(cat, dm).reshape(N_CHIPS, D_OUT_J, D_MODEL).astype(BF16)
    (dsq, dsk, dsv, dkm, dvm, dsinks), (ogf2,) = _swa_bwd(sinks, sq, sk, sv, oswa, doswa,
                                                          hook=_ChipScatter([sgf2]))
    g["swa_sinks"] = dsinks
    dgq, dgk, dgv, dla = _gla_bwd(gq, gk, gv, la, ss, dogla)
    dh1, dproj, g["mix_pre_norm"], dwa2p, g["gla_b_a"] = _mix_in_bwd(
        dh2, h1, w["mix_pre_norm"], winp, wa2p, w["gla_b_a"], cos, sin, ga, dgq, dgk, dgv, dgg, dla,
        dsq, dsk, dsv, dkm, dvm)
    g["gla_w_a2"] = dwa2p[:GLA_RANK]
    gin = _unpack_dwin(_xty(n2, dproj)).reshape(D_MODEL, N_CHIPS, D_IN_J).transpose(1, 0, 2).astype(BF16)
    (dh0, df1, dg1, du1, g["ffn1_pre_norm"], g["ffn1_post_norm"]), (rgin, rgout) = _ffn_bwd(
        dh1, h0, f1, g1, u1, w["ffn1_pre_norm"], wg1, wu1, wd1, w["ffn1_post_norm"],
        hook=_PairExchange([gin, gout]))
    sgin, sgout = _pair_sum(gin, rgin, c_idx), _pair_sum(gout, rgout, c_idx)
    own1, others1, (ogin, ogout) = _ffn_wgrad_reduce(n1, df1, dg1, du1, a1, qc_idx, _ChipScatter([sgin, sgout]))
    g["meta_tokens"] = dh0[PAD:BLK]
    grad_x = dh0[BLK:]
    halves = [_chip_sum(own1[None], others1, jnp.zeros((1,), jnp.int32))]
    halves += [_chip_sum(s, o, q_idx) for s, o in ((sgin, ogin), (sgout, ogout), (sgf2, ogf2))]
    others = _comm_call(_PairShare(halves), "pair_share")
    reduced = {"ffn1_w_gate": (0, 0), "ffn1_w_up": (0, FJ), "ffn1_w_down": (0, 2 * FJ), "w_in": (1, 0),
               "w_out": (2, 0), "ffn2_w_gate": (3, 0), "ffn2_w_up": (3, FJ), "ffn2_w_down": (3, 2 * FJ)}
    grad, delta, new_m, new_v = {}, {}, {}, {}
    for n in BIG:
        k, row0 = reduced[n]
        outs = _adamw_halves(_shard2d(n, w[n]), halves[k], others[k], _shard2d(n, mom[n]), _shard2d(n, var[n]),
                             c_idx, row0)
        grad[n], delta[n], new_m[n], new_v[n] = [_unshard2d(n, a) for a in outs]

    late = ["gla_w_a2", "swa_sinks"]
    direct = [n for n in SMALL if n not in late]
    names = direct + late
    gathered = _all_gather_devices([g[n] for n in names])
    mat = lambda a: a.reshape(a.shape[-2:])
    none2 = [None] * len(late)
    outs = _small_update(q_idx, gathered, [mat(w[n]) for n in direct] + none2, [mat(mom[n]) for n in direct] + none2,
                         [mat(var[n]) for n in direct] + none2, [n == "meta_tokens" for n in names])
    sum_a2, sum_sinks = outs[4 * len(direct):]
    g_late = [lax.dynamic_slice_in_dim(sum_a2, q_chip * (GLA_KW // N_CHIPS), GLA_KW // N_CHIPS, axis=1)[None],
              sum_sinks[:, 0].reshape(1, 1, SWA_QH)]
    outs = list(outs[:4 * len(direct)]) + list(_small_update(
        q_idx, g_late, [mat(w[n]) for n in late], [mat(mom[n]) for n in late], [mat(var[n]) for n in late],
        [False, False]))
    for k, n in enumerate(names):
        grad[n], delta[n], new_m[n], new_v[n] = [a.reshape(w[n].shape) for a in outs[4 * k:4 * k + 4]]

    return (loss, grad_x[None], *[grad[n] for n in WEIGHTS], *[delta[n] for n in WEIGHTS],
            *[new_m[n] for n in WEIGHTS], *[new_v[n] for n in WEIGHTS])
```

```python
import functools
import math

import numpy as np
import jax
import jax.numpy as jnp
from jax import lax
from jax.experimental import pallas as pl
from jax.experimental.pallas import tpu as pltpu

F32 = jnp.float32
BF16 = jnp.bfloat16
MESH = pl.DeviceIdType.MESH

D_MODEL = 1024
D_FF = 2816
N_CHIPS = 4
N_DEV = 8
N_META = 16
BLK = 128
PAD = BLK - N_META
GLA_CHUNK = 64
GLA_HEADS = 4
GLA_DV = 128
GLA_DK = 64
GLA_KW = GLA_HEADS * GLA_DK
GLA_W = GLA_HEADS * GLA_DV
GLA_RANK = 16
GLA_TAU = 16.0
SWA_HD = 64
SWA_QH = 8
SWA_KVH = 2
SWA_W = SWA_QH * SWA_HD
WINDOW = 128
ROPE_THETA = 10000.0
EPS = 1e-6
NEG_INF = -1e30
IN_SPLITS = (256, 256, 512, 512, 16, 512, 128, 128)
D_IN = sum(IN_SPLITS)
P_GQ, P_GK, P_GV, P_GG, P_GA, P_SQ, P_SK, P_SV, P_END = 0, 256, 512, 1024, 1536, 1664, 2176, 2432, 2688
ADAM_LR, ADAM_B1, ADAM_B2, ADAM_EPS, ADAM_WD, ADAM_STEP = 0.001, 0.9, 0.999, 1e-08, 0.01, 10
VMEM_LIMIT = 56 * 1024 * 1024

NT = (((1,), (1,)), ((), ()))
TN = (((0,), (0,)), ((), ()))


def _cparams(n_axes):
    return pltpu.CompilerParams(dimension_semantics=("arbitrary",) * n_axes, vmem_limit_bytes=VMEM_LIMIT)


def _row_tile(t):
    for tm in (640, 512, 384, 256, 128):
        if t % tm == 0:
            return tm
    raise ValueError(t)


SEQ_BLOCKS_PER_STEP = 5


def _seq_tile(t):
    return SEQ_BLOCKS_PER_STEP * BLK if t % (SEQ_BLOCKS_PER_STEP * BLK) == 0 else BLK


ROW_PARTS = 2


def _row_parts(tm):
    n = ROW_PARTS if tm % (16 * ROW_PARTS) == 0 else 1
    return [slice(k * (tm // n), (k + 1) * (tm // n)) for k in range(n)]


def _contract_tile(t):
    return 1664 if t % 1664 == 0 else _row_tile(t)


def _div_tile(r, cap=512):
    best = None
    for tr in range(8, min(r, cap) + 1, 8):
        if r % tr == 0:
            best = tr
    return best if best is not None else r


def _dot(a, b):
    return jnp.dot(a, b, preferred_element_type=F32)


def _dg(a, b, dims):
    return lax.dot_general(a, b, dims, preferred_element_type=F32)


def _rms(x, w):
    r = lax.rsqrt(jnp.mean(x * x, axis=-1, keepdims=True) + EPS)
    xh = x * r
    return xh * w, xh, r


def _rms_bwd(xh, r, w, dy):
    wdy = dy * w
    dx = r * (wdy - xh * jnp.mean(wdy * xh, axis=-1, keepdims=True))
    dw = jnp.sum(dy * xh, axis=0, keepdims=True)
    return dx, dw


def _sigmoid(x):
    return 1.0 / (1.0 + jnp.exp(-x))


def _full(shape):
    nd = len(shape)
    return pl.BlockSpec(shape, lambda *_: (0,) * nd)


ANY = pl.BlockSpec(memory_space=pl.ANY)


def _pallas(body, *, name, grid, in_specs, out_specs, out_shape, args, scratch_shapes=(), hook=None):
    n_axes = len(grid)
    if hook is None:
        return pl.pallas_call(body, name=name, grid=grid, in_specs=list(in_specs), out_specs=list(out_specs),
                              out_shape=list(out_shape), scratch_shapes=list(scratch_shapes),
                              compiler_params=_cparams(n_axes))(*args)
    n_in, n_out, n_scr = len(in_specs), len(out_specs), len(scratch_shapes)
    h_in, h_out = len(hook.inputs), len(hook.out_shape)
    total = math.prod(grid)

    def wrapped(*refs):
        ins, hins = refs[:n_in], refs[n_in:n_in + h_in]
        o0 = n_in + h_in
        outs, houts = refs[o0:o0 + n_out], refs[o0 + n_out:o0 + n_out + h_out]
        s0 = o0 + n_out + h_out
        scr, hscr = refs[s0:s0 + n_scr], refs[s0 + n_scr:]
        step = pl.program_id(0)
        for a in range(1, n_axes):
            step = step * grid[a] + pl.program_id(a)

        @pl.when(step == 0)
        def _():
            hook.start(hins, houts, hscr)

        body(*ins, *outs, *scr)

        if hook.has_mid:
            @pl.when(step == (3 * total) // 4)
            def _():
                hook.mid(hins, houts, hscr)

        @pl.when(step == total - 1)
        def _():
            hook.finish(hins, houts, hscr)

    res = pl.pallas_call(
        wrapped, name=name, grid=grid, in_specs=list(in_specs) + [ANY] * h_in,
        out_specs=list(out_specs) + [ANY] * h_out, out_shape=list(out_shape) + list(hook.out_shape),
        scratch_shapes=list(scratch_shapes) + list(hook.scratch), compiler_params=_cparams(n_axes),
        input_output_aliases={n_in + a: n_out + b for a, b in hook.aliases},
    )(*args, *hook.inputs)
    return res[:n_out], res[n_out:]


def _ffn_fwd(h, wpre, wg4, wu4, wd4, wpost, hook=None, target=None):
    t = h.shape[0]
    tm = _row_tile(t)
    nj, fj, _ = wg4.shape
    nblk = tm // BLK if target is not None else 0

    def body(*refs):
        h_ref, wpre_ref, wg_ref, wu_ref, wd_ref, wpost_ref = refs[:6]
        t_refs = refs[6:6 + nblk]
        hout_ref, n_ref, p1_ref, p2_ref, a_ref, f_ref = refs[6 + nblk:12 + nblk]
        acc_ref = refs[-1]
        i = pl.program_id(0)
        j = pl.program_id(1)

        @pl.when(j == 0)
        def _():
            y, _, _ = _rms(h_ref[...], wpre_ref[...])
            n_ref[...] = y.astype(BF16)
            acc_ref[...] = jnp.zeros_like(acc_ref)

        if target is not None:
            dwpost_ref, sse_ref = refs[12 + nblk:14 + nblk]

            @pl.when((i == 0) & (j == 0))
            def _():
                dwpost_ref[...] = jnp.zeros_like(dwpost_ref)
                sse_ref[...] = jnp.zeros_like(sse_ref)

        n = n_ref[...]
        g = _dg(n, wg_ref[...], NT)
        u = _dg(n, wu_ref[...], NT)
        sg = _sigmoid(g)
        silu = g * sg
        p1_ref[...] = (u * (sg + silu * (1.0 - sg))).astype(BF16)
        p2_ref[...] = silu.astype(BF16)
        a = (silu * u).astype(BF16)
        a_ref[...] = a
        acc_ref[...] += _dot(a, wd_ref[...])

        @pl.when(j == nj - 1)
        def _():
            f = acc_ref[...]
            wpost = wpost_ref[...]
            y, fh, r = _rms(f, wpost)
            hout = h_ref[...] + 0.5 * y
            if target is None:
                f_ref[...] = f
                hout_ref[...] = hout
            else:
                sse = jnp.zeros((1, 1), F32)
                errs = []
                for k in range(nblk):
                    err = hout[k * BLK:(k + 1) * BLK] - t_refs[k][...]
                    if k == 0:
                        err = jnp.where(i > 0, err, 0.0)
                    errs.append(err)
                    sse = sse + jnp.sum(jnp.sum(err * err, axis=1, keepdims=True), axis=0, keepdims=True)
                dy = (jnp.concatenate(errs, axis=0) if nblk > 1 else errs[0]) * (1.0 / D_MODEL)
                hout_ref[...] = dy
                df, dw = _rms_bwd(fh, r, wpost, 0.5 * dy)
                f_ref[...] = df.astype(BF16)
                dwpost_ref[...] += dw
                sse_ref[...] += jnp.broadcast_to(sse, sse_ref.shape)

    row = pl.BlockSpec((tm, D_MODEL), lambda i, j: (i, 0))
    vec = pl.BlockSpec((1, D_MODEL), lambda i, j: (0, 0))
    wrow = pl.BlockSpec((None, fj, D_MODEL), lambda i, j: (j, 0, 0))
    act = pl.BlockSpec((None, tm, fj), lambda i, j: (j, i, 0))
    t_specs = [pl.BlockSpec((BLK, D_MODEL), functools.partial(lambda i, j, k: (jnp.maximum(nblk * i + k - 1, 0), 0), k=k))
               for k in range(nblk)]
    loss_spec = [vec, _full((1, 128))] if target is not None else []
    loss_shape = [jax.ShapeDtypeStruct((1, D_MODEL), F32), jax.ShapeDtypeStruct((1, 128), F32)] if (
        target is not None) else []
    return _pallas(
        body, name="ffn_fwd", grid=(t // tm, nj),
        in_specs=[row, vec, wrow, wrow, wrow, vec] + t_specs,
        out_specs=[row, row, act, act, act, row] + loss_spec,
        out_shape=[jax.ShapeDtypeStruct((t, D_MODEL), F32), jax.ShapeDtypeStruct((t, D_MODEL), BF16),
                   jax.ShapeDtypeStruct((nj, t, fj), BF16), jax.ShapeDtypeStruct((nj, t, fj), BF16),
                   jax.ShapeDtypeStruct((nj, t, fj), BF16),
                   jax.ShapeDtypeStruct((t, D_MODEL), F32 if target is None else BF16)] + loss_shape,
        scratch_shapes=[pltpu.VMEM((tm, D_MODEL), F32)],
        args=(h, wpre, wg4, wu4, wd4, wpost) + (target,) * nblk, hook=hook)


def _ffn_bwd(dhout, h, f, p14, p24, wpre, wg4, wu4, wd4, wpost, hook=None, df=None):
    t = h.shape[0]
    tm = _row_tile(t)
    nj, fj, _ = wg4.shape
    have_df = df is not None

    def body(dhout_ref, h_ref, f_ref, p1_ref, p2_ref, wpre_ref, wg_ref, wu_ref, wd_ref, wpost_ref, *rest):
        if have_df:
            dh_ref, dg_ref, du_ref, dwpre_ref, dn_ref = rest
            df_ref = f_ref
        else:
            dh_ref, df_ref, dg_ref, du_ref, dwpre_ref, dwpost_ref, dn_ref = rest
        i = pl.program_id(0)
        j = pl.program_id(1)

        @pl.when((i == 0) & (j == 0))
        def _():
            dwpre_ref[...] = jnp.zeros_like(dwpre_ref)
            if not have_df:
                dwpost_ref[...] = jnp.zeros_like(dwpost_ref)

        @pl.when(j == 0)
        def _():
            if not have_df:
                wpost = wpost_ref[...]
                _, fh, r = _rms(f_ref[...], wpost)
                dfv, dw = _rms_bwd(fh, r, wpost, 0.5 * dhout_ref[...])
                dwpost_ref[...] += dw
                df_ref[...] = dfv.astype(BF16)
            dn_ref[...] = jnp.zeros_like(dn_ref)

        parts = _row_parts(tm)
        das = [_dg(df_ref[rows, :], wd_ref[...], NT) for rows in parts]
        for rows, da in zip(parts, das):
            dg = (da * p1_ref[rows, :].astype(F32)).astype(BF16)
            du = (da * p2_ref[rows, :].astype(F32)).astype(BF16)
            dg_ref[rows, :] = dg
            du_ref[rows, :] = du
            dn_ref[rows, :] += _dot(dg, wg_ref[...]) + _dot(du, wu_ref[...])

        @pl.when(j == nj - 1)
        def _():
            wpre = wpre_ref[...]
            _, hh, r = _rms(h_ref[...], wpre)
            dx, dw = _rms_bwd(hh, r, wpre, dn_ref[...])
            dwpre_ref[...] += dw
            dh_ref[...] = dhout_ref[...] + dx

    row = pl.BlockSpec((tm, D_MODEL), lambda i, j: (i, 0))
    vec = pl.BlockSpec((1, D_MODEL), lambda i, j: (0, 0))
    wrow = pl.BlockSpec((None, fj, D_MODEL), lambda i, j: (j, 0, 0))
    act = pl.BlockSpec((None, tm, fj), lambda i, j: (j, i, 0))
    actshape = jax.ShapeDtypeStruct((nj, t, fj), BF16)
    rowf, rowb, vecf = (jax.ShapeDtypeStruct((t, D_MODEL), F32), jax.ShapeDtypeStruct((t, D_MODEL), BF16),
                        jax.ShapeDtypeStruct((1, D_MODEL), F32))
    return _pallas(
        body, name="ffn_bwd", grid=(t // tm, nj),
        in_specs=[row, row, row, act, act, vec, wrow, wrow, wrow, vec],
        out_specs=[row, act, act, vec] if have_df else [row, row, act, act, vec, vec],
        out_shape=[rowf, actshape, actshape, vecf] if have_df else [rowf, rowb, actshape, actshape, vecf, vecf],
        scratch_shapes=[pltpu.VMEM((tm, D_MODEL), F32)],
        args=(dhout, h, df if have_df else f, p14, p24, wpre, wg4, wu4, wd4, wpost), hook=hook)


def _ffn_wgrad(n, df, dg4, du4, a4, hook=None):
    t = n.shape[0]
    tm = _contract_tile(t)
    ni = t // tm
    nj, _, fj = dg4.shape

    def body(n_ref, df_ref, dg_ref, du_ref, a_ref, dw_ref, acc):
        i = pl.program_id(1)

        @pl.when(i == 0)
        def _():
            acc[...] = jnp.zeros_like(acc)

        nn = n_ref[...]
        acc[0:fj, :] += _dg(dg_ref[...], nn, TN)
        acc[fj:2 * fj, :] += _dg(du_ref[...], nn, TN)
        acc[2 * fj:3 * fj, :] += _dg(a_ref[...], df_ref[...], TN)

        @pl.when(i == ni - 1)
        def _():
            dw_ref[...] = acc[...].astype(BF16)

    row = pl.BlockSpec((tm, D_MODEL), lambda j, i: (i, 0))
    act = pl.BlockSpec((None, tm, fj), lambda j, i: (j, i, 0))
    return _pallas(
        body, name="ffn_wgrad", grid=(nj, ni),
        in_specs=[row, row, act, act, act],
        out_specs=[pl.BlockSpec((None, 3 * fj, D_MODEL), lambda j, i: (j, 0, 0))],
        out_shape=[jax.ShapeDtypeStruct((nj, 3 * fj, D_MODEL), BF16)],
        scratch_shapes=[pltpu.VMEM((3 * fj, D_MODEL), F32)],
        args=(n, df, dg4, du4, a4), hook=hook)


def _embed_norm(x, meta, w):
    t = x.shape[0] + BLK
    tm = _row_tile(t)
    nblk = tm // BLK

    def body(*refs):
        x_refs = refs[:nblk]
        meta_ref, w_ref, h_ref, n_ref = refs[nblk:]
        i = pl.program_id(0)
        first = jnp.concatenate([jnp.zeros((PAD, D_MODEL), F32), meta_ref[...]], axis=0)
        blocks = [jnp.where(i == 0, first, x_refs[0][...])] + [r[...] for r in x_refs[1:]]
        h = jnp.concatenate(blocks, axis=0) if nblk > 1 else blocks[0]
        h_ref[...] = h
        y, _, _ = _rms(h, w_ref[...])
        n_ref[...] = y.astype(BF16)

    x_specs = [pl.BlockSpec((BLK, D_MODEL), functools.partial(lambda i, k: (jnp.maximum(nblk * i + k - 1, 0), 0), k=k))
               for k in range(nblk)]
    row = pl.BlockSpec((tm, D_MODEL), lambda i: (i, 0))
    return pl.pallas_call(
        body, name="embed_norm", grid=(t // tm,),
        in_specs=x_specs + [_full((N_META, D_MODEL)), _full((1, D_MODEL))], out_specs=[row, row],
        out_shape=[jax.ShapeDtypeStruct((t, D_MODEL), F32), jax.ShapeDtypeStruct((t, D_MODEL), BF16)],
        compiler_params=_cparams(1),
    )(*([x] * nblk), meta, w)


FWD_RELATION = (None, 0, 1, 2)


def _ffn_fwd_gather(h, n, wbufs, wpost, qc_idx, late):
    t = h.shape[0]
    tm = _row_tile(t)
    ni = t // tm
    nj, fj, _ = wbufs[0].shape
    assert nj == N_CHIPS and ni >= 4
    nw = len(wbufs)
    n_lin, n_lout = len(late.inputs), len(late.out_shape)
    wait_step = ni - 3

    def body(qc_ref, h_ref, n_ref, wpost_ref, *rest):
        wb_in = rest[:nw]
        lins = rest[nw:nw + n_lin]
        o0 = nw + n_lin
        hout_ref, p1_ref, p2_ref, a_ref, f_hbm = rest[o0:o0 + 5]
        wb = rest[o0 + 5:o0 + 5 + nw]
        louts = rest[o0 + 5 + nw:o0 + 5 + nw + n_lout]
        s0 = o0 + 5 + nw + n_lout
        wv, wsem, send, recv, fbuf, fr_sem, fw_sem = rest[s0:s0 + 7]
        lscr = rest[s0 + 7:]
        p = pl.program_id(0)
        i = pl.program_id(1)
        step = p * ni + i
        fslot = step % 3
        nslot = (step + 1) % 3

        def f_tile(tile):
            return f_hbm.at[pl.ds(pl.multiple_of(tile * tm, 8), tm)]

        @pl.when(step > 1)
        def _():
            pltpu.make_async_copy(fbuf.at[nslot], f_tile(i), fw_sem.at[nslot]).wait()

        nxt = step + 1

        @pl.when((nxt < N_CHIPS * ni) & (nxt >= ni))
        def _():
            pltpu.make_async_copy(f_tile(nxt % ni), fbuf.at[nslot], fr_sem.at[nslot]).start()

        @pl.when(p > 0)
        def _():
            pltpu.make_async_copy(f_tile(i), fbuf.at[fslot], fr_sem.at[fslot]).wait()
        x, y, c, chips = _place()
        q = 2 * x + y
        sibling = (x, y, 1 - c)
        mine, other = _half(fj, c), _half(fj, 1 - c)

        def load(chunk, slot, src):
            return [pltpu.make_async_copy(src[t].at[chunk], wv.at[slot, t], wsem.at[slot, t]) for t in range(nw)]

        @pl.when((p == 0) & (i == 0))
        def _():
            for j, (cx, cy) in enumerate(chips):
                for t in range(nw):
                    _remote(send.at[t, j], recv.at[t, j], wb_in[t].at[q, mine], wb[t].at[q, mine], (cx, cy, c)).start()
            for cp in load(q, 0, wb_in):
                cp.start()
            for cp in load(q, 0, wb_in):
                cp.wait()

        @pl.when((p == 1) & (i == 0))
        def _():
            late.start(lins, louts, lscr)

        for pp in range(1, N_CHIPS):
            j = FWD_RELATION[pp]
            cx, cy = chips[j]
            chunk = 2 * cx + cy

            @pl.when((p == pp - 1) & (i == wait_step))
            def _(j=j, cx=cx, cy=cy, chunk=chunk, pp=pp):
                for t in range(nw):
                    got = wb[t].at[chunk, mine]
                    _remote(send.at[t, j], recv.at[t, j], got, got, (cx, cy, c)).wait_recv()
                    _remote(send.at[t, 3 + j], recv.at[t, 3 + j], got, got, sibling).start()
                for t in range(nw):
                    rest_half = wb[t].at[chunk, other]
                    _remote(send.at[t, 3 + j], recv.at[t, 3 + j], rest_half, rest_half, sibling).wait_recv()
                for cp in load(chunk, pp % 2, wb):
                    cp.start()

            @pl.when((p == pp) & (i == 0))
            def _(chunk=chunk, pp=pp):
                for cp in load(chunk, pp % 2, wb):
                    cp.wait()

        @pl.when((p == N_CHIPS - 1) & (i == ni // 2))
        def _():
            late.mid(lins, louts, lscr)

        slot = p % 2
        nn = n_ref[...]
        g = _dg(nn, wv[slot, 0], NT)
        u = _dg(nn, wv[slot, 1], NT)
        sg = _sigmoid(g)
        silu = g * sg
        p1_ref[...] = (u * (sg + silu * (1.0 - sg))).astype(BF16)
        p2_ref[...] = silu.astype(BF16)
        a = (silu * u).astype(BF16)
        a_ref[...] = a
        part = _dot(a, wv[slot, 2])

        @pl.when(p == 0)
        def _():
            fbuf[fslot] = part

        @pl.when(p > 0)
        def _():
            fbuf[fslot] = fbuf[fslot] + part

        pltpu.make_async_copy(fbuf.at[fslot], f_tile(i), fw_sem.at[fslot]).start()

        @pl.when(p == N_CHIPS - 1)
        def _():
            yv, _, _ = _rms(fbuf[fslot], wpost_ref[...])
            hout_ref[...] = h_ref[...] + 0.5 * yv

        @pl.when((p == N_CHIPS - 1) & (i == ni - 1))
        def _():
            pslot = (step + 2) % 3
            pltpu.make_async_copy(fbuf.at[pslot], f_tile(i), fw_sem.at[pslot]).wait()
            pltpu.make_async_copy(fbuf.at[fslot], f_tile(i), fw_sem.at[fslot]).wait()
            for t in range(nw):
                for j, (cx, cy) in enumerate(chips):
                    sent = wb[t].at[2 * cx + cy, mine]
                    _remote(send.at[t, j], recv.at[t, j], sent, sent, (cx, cy, c)).wait_send()
                    _remote(send.at[t, 3 + j], recv.at[t, 3 + j], sent, sent, sibling).wait_send()
            late.finish(lins, louts, lscr)

    def last_pass_rows(p, i, qc_ref):
        return (jnp.where(p == N_CHIPS - 1, i, 0), 0)

    def chunk_rows(p, i, qc_ref):
        order = ((p & 1) << 1) | (p >> 1)
        return (jnp.bitwise_xor(qc_ref[0], order), i, 0)

    row = pl.BlockSpec((tm, D_MODEL), lambda p, i, qc_ref: (i, 0))
    last_row = pl.BlockSpec((tm, D_MODEL), last_pass_rows)
    act = pl.BlockSpec((None, tm, fj), chunk_rows)
    act_shape = jax.ShapeDtypeStruct((nj, t, fj), BF16)
    res = pl.pallas_call(
        body, name="ffn_fwd_gather",
        grid_spec=pltpu.PrefetchScalarGridSpec(
            num_scalar_prefetch=1, grid=(N_CHIPS, ni),
            in_specs=[last_row, row, pl.BlockSpec((1, D_MODEL), lambda p, i, qc_ref: (0, 0))]
            + [ANY] * (nw + n_lin),
            out_specs=[last_row, act, act, act, ANY] + [ANY] * (nw + n_lout),
            scratch_shapes=[pltpu.VMEM((2, nw, fj, D_MODEL), BF16), pltpu.SemaphoreType.DMA((2, nw)),
                            pltpu.SemaphoreType.DMA((nw, 6)), pltpu.SemaphoreType.DMA((nw, 6)),
                            pltpu.VMEM((3, tm, D_MODEL), F32), pltpu.SemaphoreType.DMA((3,)),
                            pltpu.SemaphoreType.DMA((3,))] + list(late.scratch)),
        out_shape=[jax.ShapeDtypeStruct((t, D_MODEL), F32), act_shape, act_shape, act_shape,
                   jax.ShapeDtypeStruct((t, D_MODEL), F32)]
        + [jax.ShapeDtypeStruct(b.shape, b.dtype) for b in wbufs] + list(late.out_shape),
        input_output_aliases={**{4 + t: 5 + t for t in range(nw)},
                              **{4 + nw + a: 5 + nw + b for a, b in late.aliases}},
        compiler_params=_cparams(2),
    )(qc_idx, h, n, wpost, *wbufs, *late.inputs)
    return res[:5], res[5:5 + nw], res[5 + nw:]


PASS_RELATION = (2, 0, 1)


def _ffn_wgrad_reduce(n, df, dg4, du4, a4, qc_idx, hook):
    t = n.shape[0]
    tm = _contract_tile(t)
    ni = t // tm
    nj, _, fj = dg4.shape
    assert nj == N_CHIPS
    hrows = 3 * fj // 2
    n_hin, n_hout = len(hook.inputs), len(hook.out_shape)

    def body(qc_ref, n_ref, df_ref, dg_ref, du_ref, a_ref, *rest):
        hins = rest[:n_hin]
        own_ref, others_ref = rest[n_hin:n_hin + 2]
        houts = rest[n_hin + 2:n_hin + 2 + n_hout]
        s0 = n_hin + 2 + n_hout
        acc, stage, land, sumbuf, px_send, px_recv, cs_send, cs_recv, own_sem = rest[s0:s0 + 9]
        hscr = rest[s0 + 9:]
        k_pass = pl.program_id(0)
        i = pl.program_id(1)
        x, y, c, chips = _place()
        mine = pl.ds(pl.multiple_of(c * hrows, 8), hrows)
        other = pl.ds(pl.multiple_of((1 - c) * hrows, 8), hrows)

        def to_owner(k):
            j = PASS_RELATION[k]
            return _remote(cs_send.at[j], cs_recv.at[j], sumbuf.at[k % 2], others_ref.at[j], (*chips[j], c))

        @pl.when((k_pass == 0) & (i == 0))
        def _():
            hook.start(hins, houts, hscr)

        @pl.when(i == 0)
        def _():
            acc[...] = jnp.zeros_like(acc)

        nn = n_ref[...]
        acc[0:fj, :] += _dg(dg_ref[...], nn, TN)
        acc[fj:2 * fj, :] += _dg(du_ref[...], nn, TN)
        acc[2 * fj:3 * fj, :] += _dg(a_ref[...], df_ref[...], TN)

        for k in range(N_CHIPS):
            @pl.when((k_pass == k) & (i == ni - 1))
            def _(k=k):
                slot = k % 2
                stage[...] = acc[other, :].astype(BF16)
                swap = _remote(px_send.at[k], px_recv.at[k], stage, land.at[slot], (x, y, 1 - c))
                swap.start()
                swap.wait_recv()
                pair = acc[mine, :] + land[slot].astype(F32)
                if k >= 2:
                    to_owner(k - 2).wait_send()
                sumbuf[slot] = pair.astype(BF16)
                swap.wait_send()
                if k < N_CHIPS - 1:
                    to_owner(k).start()
                else:
                    keep = pltpu.make_async_copy(sumbuf.at[slot], own_ref, own_sem)
                    keep.start()
                    for j in range(N_CHIPS - 1):
                        _remote(cs_send.at[j], cs_recv.at[j], sumbuf.at[0], others_ref.at[j], (*chips[j], c)).wait_recv()
                    to_owner(k - 1).wait_send()
                    keep.wait()
                    hook.finish(hins, houts, hscr)

    def chunk(k_pass, i, qc_ref):
        return (jnp.bitwise_xor(qc_ref[0], N_CHIPS - 1 - k_pass), i, 0)

    row = pl.BlockSpec((tm, D_MODEL), lambda k_pass, i, qc_ref: (i, 0))
    act = pl.BlockSpec((None, tm, fj), chunk)
    res = pl.pallas_call(
        body, name="ffn_wgrad_reduce",
        grid_spec=pltpu.PrefetchScalarGridSpec(
            num_scalar_prefetch=1, grid=(N_CHIPS, ni),
            in_specs=[row, row, act, act, act] + [ANY] * n_hin,
            out_specs=[ANY, ANY] + [ANY] * n_hout,
            scratch_shapes=[pltpu.VMEM((3 * fj, D_MODEL), F32), pltpu.VMEM((hrows, D_MODEL), BF16),
                            pltpu.VMEM((2, hrows, D_MODEL), BF16), pltpu.VMEM((2, hrows, D_MODEL), BF16),
                            pltpu.SemaphoreType.DMA((N_CHIPS,)), pltpu.SemaphoreType.DMA((N_CHIPS,)),
                            pltpu.SemaphoreType.DMA((N_CHIPS - 1,)), pltpu.SemaphoreType.DMA((N_CHIPS - 1,)),
                            pltpu.SemaphoreType.DMA] + list(hook.scratch)),
        out_shape=[jax.ShapeDtypeStruct((hrows, D_MODEL), BF16),
                   jax.ShapeDtypeStruct((N_CHIPS - 1, hrows, D_MODEL), BF16)] + list(hook.out_shape),
        compiler_params=_cparams(2),
    )(qc_idx, n, df, dg4, du4, a4, *hook.inputs)
    return res[0], res[1], res[2:]


def _xty(x, y):
    t, k = x.shape
    n = y.shape[1]
    tm = _contract_tile(t)
    tn = n if n <= 1024 else (896 if n % 896 == 0 else 128)

    def body(x_ref, y_ref, o_ref):
        @pl.when(pl.program_id(1) == 0)
        def _():
            o_ref[...] = jnp.zeros_like(o_ref)

        o_ref[...] += _dg(x_ref[...], y_ref[...], TN)

    return pl.pallas_call(
        body, name="xty", grid=(n // tn, t // tm),
        in_specs=[pl.BlockSpec((tm, k), lambda j, i: (i, 0)), pl.BlockSpec((tm, tn), lambda j, i: (i, j))],
        out_specs=pl.BlockSpec((k, tn), lambda j, i: (0, j)),
        out_shape=jax.ShapeDtypeStruct((k, n), F32),
        compiler_params=_cparams(2),
    )(x, y)


def _rope_tables(t):
    pos = (jnp.arange(t, dtype=jnp.int32) - PAD).astype(F32)
    inv_freq = 1.0 / (ROPE_THETA ** (jnp.arange(0, SWA_HD, 2, dtype=F32) / SWA_HD))
    ang = pos[:, None] * inv_freq[None, :]
    cos = jnp.cos(ang)
    sin = jnp.sin(ang)
    return jnp.concatenate([cos, cos, cos, cos], axis=1), jnp.concatenate([-sin, sin, -sin, sin], axis=1)


def _rot_half(x, first_half):
    return jnp.where(first_half, pltpu.roll(x, 96, 1), pltpu.roll(x, 32, 1))


def _first_half_mask(rows):
    lane = lax.broadcasted_iota(jnp.int32, (rows, 128), 1)
    return (lane % 64) < 32


def _log_sigmoid(z):
    return jnp.minimum(z, 0.0) - jnp.log(1.0 + jnp.exp(-jnp.abs(z)))


def _mix_proj(h1, wmixpre, winp, wa2p, bap, cos, sin):
    t = h1.shape[0]
    tm = _row_tile(t)

    def body(h_ref, w_ref, win_ref, wa2_ref, ba_ref, cos_ref, sin_ref,
             n_ref, gq_ref, gk_ref, gv_ref, gg_ref, ga_ref, la_ref, sq_ref, sk_ref, sv_ref):
        y, _, _ = _rms(h_ref[...], w_ref[...])
        n = y.astype(BF16)
        n_ref[...] = n
        proj = _dot(n, win_ref[...])
        gq_ref[...] = proj[:, P_GQ:P_GK]
        gk_ref[...] = proj[:, P_GK:P_GV]
        gv_ref[...] = proj[:, P_GV:P_GG]
        gg_ref[...] = proj[:, P_GG:P_GA]
        ga = proj[:, P_GA:P_SQ]
        ga_ref[...] = ga
        z = _dot(ga.astype(BF16), wa2_ref[...]) + ba_ref[...]
        la_ref[...] = _log_sigmoid(z) * (1.0 / GLA_TAU)
        c = cos_ref[...]
        s = sin_ref[...]
        fh = _first_half_mask(tm)
        for k in range(4):
            x = proj[:, P_SQ + 128 * k:P_SQ + 128 * (k + 1)]
            sq_ref[:, 128 * k:128 * (k + 1)] = (x * c + _rot_half(x, fh) * s).astype(BF16)
        for k in range(2):
            x = proj[:, P_SK + 128 * k:P_SK + 128 * (k + 1)]
            sk_ref[:, 128 * k:128 * (k + 1)] = (x * c + _rot_half(x, fh) * s).astype(BF16)
        sv_ref[...] = proj[:, P_SV:P_END].astype(BF16)

    def row(w):
        return pl.BlockSpec((tm, w), lambda i: (i, 0))

    def rshape(w, dt):
        return jax.ShapeDtypeStruct((t, w), dt)

    return pl.pallas_call(
        body, name="mix_proj", grid=(t // tm,),
        in_specs=[row(D_MODEL), _full((1, D_MODEL)), _full((D_MODEL, P_END)), _full((128, GLA_KW)),
                  _full((1, GLA_KW)), row(128), row(128)],
        out_specs=[row(D_MODEL), row(256), row(256), row(512), row(512), row(128), row(256), row(512), row(256),
                   row(256)],
        out_shape=[rshape(D_MODEL, BF16), rshape(256, F32), rshape(256, F32), rshape(512, F32), rshape(512, F32),
                   rshape(128, F32), rshape(256, F32), rshape(512, BF16), rshape(256, BF16), rshape(256, BF16)],
        compiler_params=_cparams(1),
    )(h1, wmixpre, winp, wa2p, bap, cos, sin)


def _scan_rows(x, reverse=False):
    n = x.shape[0]
    row = lax.broadcasted_iota(jnp.int32, x.shape, 0)
    s = 1
    while s < n:
        if reverse:
            x = x + jnp.where(row < n - s, pltpu.roll(x, n - s, 0), 0.0)
        else:
            x = x + jnp.where(row >= s, pltpu.roll(x, s, 0), 0.0)
        s *= 2
    return x


def _gla_cumsum(la, tril_f):
    b = _scan_rows(la)
    row = lax.broadcasted_iota(jnp.int32, b.shape, 0)
    bm = jnp.sum(jnp.where(row == GLA_CHUNK // 2 - 1, b, 0.0), axis=0, keepdims=True)
    bl = jnp.sum(jnp.where(row == GLA_CHUNK - 1, b, 0.0), axis=0, keepdims=True)
    return b, bm, bl


def _gla_decays(la, tril_f):
    b, bm, bl = _gla_cumsum(la, tril_f)
    return jnp.exp(b - bm), jnp.exp(bm - b), jnp.exp(b), jnp.exp(bl - b), jnp.exp(bl)


def _gla_masks():
    c = GLA_CHUNK
    r = lax.broadcasted_iota(jnp.int32, (c, c), 0)
    col = lax.broadcasted_iota(jnp.int32, (c, c), 1)
    r4 = lax.broadcasted_iota(jnp.int32, (GLA_HEADS * c, c), 0) % c
    c4 = lax.broadcasted_iota(jnp.int32, (GLA_HEADS * c, c), 1)
    klane = lax.broadcasted_iota(jnp.int32, (c, GLA_KW), 1) // GLA_DK
    vlane = lax.broadcasted_iota(jnp.int32, (c, GLA_W), 1) // GLA_DV
    srow = lax.broadcasted_iota(jnp.int32, (GLA_W, GLA_KW), 0) // GLA_DV
    scol = lax.broadcasted_iota(jnp.int32, (GLA_W, GLA_KW), 1) // GLA_DK
    return dict(tril_f=(r >= col).astype(F32), triu_f=(r <= col).astype(F32), tril4=r4 >= c4,
                khead=[klane == h for h in range(GLA_HEADS)], vhead=[vlane == h for h in range(GLA_HEADS)],
                diag=srow == scol)


def _stack_heads(x, head_masks):
    return jnp.concatenate([jnp.where(m, x, 0.0) for m in head_masks], axis=0)


def _gla_fwd(gq, gk, gv, la):
    t = gq.shape[0]
    rg = _seq_tile(t)
    nb = t // rg
    ncb = rg // GLA_CHUNK
    c = GLA_CHUNK

    def body(q_ref, k_ref, v_ref, la_ref, o_ref, ss_ref, st_ref):
        @pl.when(pl.program_id(0) == 0)
        def _():
            st_ref[...] = jnp.zeros_like(st_ref)

        mk = _gla_masks()
        st = st_ref[...]
        for ch in range(ncb):
            rows = slice(ch * c, (ch + 1) * c)
            eq, ek, eb, ekl, ebl = _gla_decays(la_ref[rows, :], mk["tril_f"])
            qs = q_ref[rows, :] * (GLA_DK ** -0.5)
            k = k_ref[rows, :]
            v = v_ref[rows, :].astype(BF16)
            ss_ref[ch] = st
            q4 = _stack_heads(qs * eq, mk["khead"]).astype(BF16)
            a4 = jnp.where(mk["tril4"], _dg(q4, (k * ek).astype(BF16), NT), 0.0).astype(BF16)
            r4 = _dot(a4, v)
            intra = jnp.concatenate([r4[h * c:(h + 1) * c, GLA_DV * h:GLA_DV * (h + 1)] for h in range(GLA_HEADS)],
                                    axis=1)
            o_ref[rows, :] = intra + _dg((qs * eb).astype(BF16), st.astype(BF16), NT)
            st = st * ebl + jnp.where(mk["diag"], _dg(v, (k * ekl).astype(BF16), TN), 0.0)
        st_ref[...] = st

    def row(w):
        return pl.BlockSpec((rg, w), lambda i: (i, 0))

    return pl.pallas_call(
        body, name="gla_fwd", grid=(nb,),
        in_specs=[row(256), row(256), row(512), row(256)],
        out_specs=[row(512), pl.BlockSpec((ncb, GLA_W, GLA_KW), lambda i: (i, 0, 0))],
        out_shape=[jax.ShapeDtypeStruct((t, GLA_W), F32), jax.ShapeDtypeStruct((nb * ncb, GLA_W, GLA_KW), F32)],
        scratch_shapes=[pltpu.VMEM((GLA_W, GLA_KW), F32)],
        compiler_params=_cparams(1),
    )(gq, gk, gv, la)


def _gla_bwd(gq, gk, gv, la, ss, do):
    t = gq.shape[0]
    rg = _seq_tile(t)
    nb = t // rg
    ncb = rg // GLA_CHUNK
    c = GLA_CHUNK

    def body(q_ref, k_ref, v_ref, la_ref, ss_ref, do_ref, dq_ref, dk_ref, dv_ref, dla_ref, dst_ref):
        @pl.when(pl.program_id(0) == 0)
        def _():
            dst_ref[...] = jnp.zeros_like(dst_ref)

        mk = _gla_masks()
        last_row = lax.broadcasted_iota(jnp.int32, (c, GLA_KW), 0) == c - 1
        scale = GLA_DK ** -0.5
        dstn = dst_ref[...]
        for ch in reversed(range(ncb)):
            rows = slice(ch * c, (ch + 1) * c)
            eq, ek, eb, ekl, ebl = _gla_decays(la_ref[rows, :], mk["tril_f"])
            qs = q_ref[rows, :] * scale
            k = k_ref[rows, :]
            qt, kt, qh, kh = qs * eq, k * ek, qs * eb, k * ekl
            ktb, khb, qhb = kt.astype(BF16), kh.astype(BF16), qh.astype(BF16)
            v = v_ref[rows, :].astype(BF16)
            do_f = do_ref[rows, :]
            dob = do_f.astype(BF16)
            st = ss_ref[ch]
            stb = st.astype(BF16)
            dstb = dstn.astype(BF16)
            q4 = _stack_heads(qt, mk["khead"]).astype(BF16)
            do4 = _stack_heads(do_f, mk["vhead"]).astype(BF16)
            a4 = jnp.where(mk["tril4"], _dg(q4, ktb, NT), 0.0).astype(BF16)
            da4 = jnp.where(mk["tril4"], _dg(do4, v, NT), 0.0).astype(BF16)
            dv_ref[rows, :] = _dg(a4, do4, TN) + _dg(khb, dstb, NT)
            dq4 = _dot(da4, ktb)
            dqt = jnp.zeros((c, GLA_KW), F32)
            for h in range(GLA_HEADS):
                dqt = dqt + jnp.where(mk["khead"][h], dq4[h * c:(h + 1) * c], 0.0)
            dkt = _dg(da4, q4, TN)
            dqh = _dot(dob, stb)
            dkh = _dot(v, dstb)
            dbl = jnp.sum(dstn * st, axis=0, keepdims=True)
            dstn = dstn * ebl + jnp.where(mk["diag"], _dg(dob, qhb, TN), 0.0)
            dq_ref[rows, :] = scale * (dqt * eq + dqh * eb)
            dk_ref[rows, :] = dkt * ek + dkh * ekl
            dkk = dkh * kh
            db = dqt * qt - dkt * kt + dqh * qh - dkk
            db = db + jnp.where(last_row, jnp.sum(dkk, axis=0, keepdims=True) + ebl * dbl, 0.0)
            dla_ref[rows, :] = _scan_rows(db, reverse=True)
        dst_ref[...] = dstn

    def row(w):
        return pl.BlockSpec((rg, w), lambda i: (nb - 1 - i, 0))

    def rshape(w):
        return jax.ShapeDtypeStruct((t, w), F32)

    return pl.pallas_call(
        body, name="gla_bwd", grid=(nb,),
        in_specs=[row(256), row(256), row(512), row(256),
                  pl.BlockSpec((ncb, GLA_W, GLA_KW), lambda i: (nb - 1 - i, 0, 0)), row(512)],
        out_specs=[row(256), row(256), row(512), row(256)],
        out_shape=[rshape(256), rshape(256), rshape(512), rshape(256)],
        scratch_shapes=[pltpu.VMEM((GLA_W, GLA_KW), F32)],
        compiler_params=_cparams(1),
    )(gq, gk, gv, la, ss, do)


SWA_G = SWA_QH // SWA_KVH


def _swa_bias():
    n = jnp.arange(3, dtype=jnp.int32)[:, None, None]
    r = (jnp.arange(SWA_G * BLK, dtype=jnp.int32) % BLK)[None, :, None]
    c = jnp.arange(3 * BLK, dtype=jnp.int32)[None, None, :]
    seg = c // BLK
    cc = c % BLK
    qpos = n * BLK + r - PAD
    kpos = jnp.where(seg == 0, (n - 1) * BLK, jnp.where(seg == 1, n * BLK, 0)) + cc - PAD
    band = (seg < 2) & (kpos >= N_META) & (kpos <= qpos) & (qpos - kpos < WINDOW)
    meta = (seg == 2) & (kpos >= 0) & (kpos < N_META) & (kpos <= qpos)
    return jnp.where(band | meta, 0.0, NEG_INF).astype(F32)


def _swa_stack(ref, rows, kh, lo, dtype):
    parts = []
    for g in range(2):
        pair = ref[rows, 128 * (2 * kh + g):128 * (2 * kh + g + 1)]
        zero = jnp.zeros_like(pair)
        parts += [jnp.where(lo, pair, zero), jnp.where(lo, zero, pair)]
    return jnp.concatenate(parts, axis=0).astype(dtype)


def _swa_unstack(x4, lo):
    return [jnp.where(lo, x4[2 * g * BLK:(2 * g + 1) * BLK], x4[(2 * g + 1) * BLK:(2 * g + 2) * BLK])
            for g in range(2)]


def _swa_sink_col(sink_ref, kh):
    blk = lax.broadcasted_iota(jnp.int32, (SWA_G * BLK, 1), 0) // BLK
    col = jnp.full((SWA_G * BLK, 1), sink_ref[SWA_G * kh + SWA_G - 1], F32)
    for e in reversed(range(SWA_G - 1)):
        col = jnp.where(blk == e, sink_ref[SWA_G * kh + e], col)
    return col


def _swa_softmax(qk, bias, sink):
    s = qk * (SWA_HD ** -0.5) + bias
    m = jnp.maximum(jnp.max(s, axis=-1, keepdims=True), sink)
    p = jnp.exp(s - m)
    es = jnp.exp(sink - m)
    inv = 1.0 / (jnp.sum(p, axis=-1, keepdims=True) + es)
    return p * inv, es * inv


def _swa_keys(prev_ref, cur_ref, first_ref, b, ls):
    before = prev_ref[:, ls] if b == 0 else cur_ref[(b - 1) * BLK:b * BLK, ls]
    return jnp.concatenate([before, cur_ref[b * BLK:(b + 1) * BLK, ls], first_ref[:, ls]], axis=0)


def _swa_specs(rs, ns):
    bps = rs // BLK
    cur = lambda w: pl.BlockSpec((rs, w), lambda i: (jnp.minimum(i, ns - 1), 0))
    prev = lambda w: pl.BlockSpec((BLK, w), lambda i: (jnp.maximum(jnp.minimum(i, ns - 1) * bps - 1, 0), 0))
    first = lambda w: pl.BlockSpec((BLK, w), lambda i: (0, 0))
    return cur, prev, first


def _swa_fwd(sinks, sq, sk, sv):
    t = sq.shape[0]
    rs = _seq_tile(t)
    bps, ns = rs // BLK, t // rs

    def body(sink_ref, bias_ref, q_ref, kp_ref, kc_ref, km_ref, vp_ref, vc_ref, vm_ref, o_ref):
        i = pl.program_id(0)
        lo = lax.broadcasted_iota(jnp.int32, (BLK, 128), 1) < 64
        sink_cols = [_swa_sink_col(sink_ref, kh) for kh in range(SWA_KVH)]
        chains = [(b, kh) for b in range(bps) for kh in range(SWA_KVH)]
        scores = []
        for b, kh in chains:
            ls = slice(128 * kh, 128 * (kh + 1))
            q4 = _swa_stack(q_ref, slice(b * BLK, (b + 1) * BLK), kh, lo, BF16)
            scores.append(_dg(q4, _swa_keys(kp_ref, kc_ref, km_ref, b, ls), NT))
        probs = []
        for (b, kh), s in zip(chains, scores):
            p, _ = _swa_softmax(s, bias_ref[jnp.minimum(i * bps + b, 2)], sink_cols[kh])
            probs.append(p.astype(BF16))
        for (b, kh), p in zip(chains, probs):
            ls = slice(128 * kh, 128 * (kh + 1))
            rows = slice(b * BLK, (b + 1) * BLK)
            for g, pair in enumerate(_swa_unstack(_dot(p, _swa_keys(vp_ref, vc_ref, vm_ref, b, ls)), lo)):
                o_ref[rows, 128 * (2 * kh + g):128 * (2 * kh + g + 1)] = pair

    cur, prev, first = _swa_specs(rs, ns)
    bias = _swa_bias()
    return pl.pallas_call(
        body, name="swa_fwd", grid=(ns,),
        in_specs=[pl.BlockSpec(memory_space=pltpu.SMEM), _full(bias.shape), cur(512), prev(256), cur(256), first(256),
                  prev(256), cur(256), first(256)],
        out_specs=cur(512),
        out_shape=jax.ShapeDtypeStruct((t, SWA_W), F32),
        compiler_params=_cparams(1),
    )(sinks, bias, sq, sk, sk, sk, sv, sv, sv)


def _swa_bwd(sinks, sq, sk, sv, o, do, hook=None):
    t = sq.shape[0]
    rs = _seq_tile(t)
    bps, ns = rs // BLK, t // rs

    def body(sink_ref, bias_ref, q_ref, kp_ref, kc_ref, km_ref, vp_ref, vc_ref, vm_ref, o_ref, do_ref,
             dq_ref, dk_ref, dv_ref, dkm_ref, dvm_ref, dsink_ref, pk_ref, pv_ref):
        i = pl.program_id(0)

        @pl.when(i == 0)
        def _():
            pk_ref[...] = jnp.zeros_like(pk_ref)
            pv_ref[...] = jnp.zeros_like(pv_ref)
            dkm_ref[...] = jnp.zeros_like(dkm_ref)
            dvm_ref[...] = jnp.zeros_like(dvm_ref)
            dsink_ref[...] = jnp.zeros_like(dsink_ref)

        @pl.when(i == ns)
        def _():
            dk_ref[...] = pk_ref[...]
            dv_ref[...] = pv_ref[...]

        @pl.when(i < ns)
        def _():
            lo = lax.broadcasted_iota(jnp.int32, (BLK, 128), 1) < 64
            scale = SWA_HD ** -0.5
            sink_cols = [_swa_sink_col(sink_ref, kh) for kh in range(SWA_KVH)]
            parts_k = [[None] * SWA_KVH for _ in range(bps)]
            parts_v = [[None] * SWA_KVH for _ in range(bps)]
            dsinks = [jnp.zeros((1, 1), F32) for _ in range(SWA_QH)]
            chains = [(b, kh) for b in range(bps) for kh in range(SWA_KVH)]
            lanes = lambda kh: slice(128 * kh, 128 * (kh + 1))
            block = lambda b: slice(b * BLK, (b + 1) * BLK)
            q4s = [_swa_stack(q_ref, block(b), kh, lo, BF16) for b, kh in chains]
            scores = [_dg(q4, _swa_keys(kp_ref, kc_ref, km_ref, b, lanes(kh)), NT)
                      for (b, kh), q4 in zip(chains, q4s)]
            do4s = [_swa_stack(do_ref, block(b), kh, lo, F32) for b, kh in chains]
            do4bs = [d.astype(BF16) for d in do4s]
            dps = [_dg(d, _swa_keys(vp_ref, vc_ref, vm_ref, b, lanes(kh)), NT) for (b, kh), d in zip(chains, do4bs)]
            pbs, dss = [], []
            for n_chain, (b, kh) in enumerate(chains):
                p, psink = _swa_softmax(scores[n_chain], bias_ref[jnp.minimum(i * bps + b, 2)], sink_cols[kh])
                delta = jnp.sum(do4s[n_chain] * _swa_stack(o_ref, block(b), kh, lo, F32), axis=-1, keepdims=True)
                dss.append((p * (dps[n_chain] - delta) * scale).astype(BF16))
                pbs.append(p.astype(BF16))
                dsk = psink * delta
                for e in range(SWA_G):
                    h = SWA_G * kh + e
                    dsinks[h] = dsinks[h] - jnp.sum(dsk[e * BLK:(e + 1) * BLK], axis=0, keepdims=True)
            for n_chain, (b, kh) in enumerate(chains):
                kall = _swa_keys(kp_ref, kc_ref, km_ref, b, lanes(kh))
                for g, pair in enumerate(_swa_unstack(_dot(dss[n_chain], kall), lo)):
                    dq_ref[block(b), 128 * (2 * kh + g):128 * (2 * kh + g + 1)] = pair
                parts_k[b][kh] = _dg(dss[n_chain], q4s[n_chain], TN)
                parts_v[b][kh] = _dg(pbs[n_chain], do4bs[n_chain], TN)
            last = slice(rs - BLK, rs)
            for parts, out_ref, pend_ref, meta_ref in ((parts_k, dk_ref, pk_ref, dkm_ref),
                                                       (parts_v, dv_ref, pv_ref, dvm_ref)):
                for kh in range(SWA_KVH):
                    ls = slice(128 * kh, 128 * (kh + 1))
                    if bps > 1:
                        out_ref[0:rs - BLK, ls] = pend_ref[0:rs - BLK, ls]
                    out_ref[last, ls] = pend_ref[last, ls] + parts[0][kh][0:BLK]
                    meta = parts[0][kh][2 * BLK:3 * BLK]
                    for b in range(bps):
                        own = parts[b][kh][BLK:2 * BLK]
                        if b + 1 < bps:
                            own = own + parts[b + 1][kh][0:BLK]
                            meta = meta + parts[b + 1][kh][2 * BLK:3 * BLK]
                        pend_ref[b * BLK:(b + 1) * BLK, ls] = own
                    meta_ref[:, ls] += meta
            for h in range(SWA_QH):
                dsink_ref[h:h + 1, :] += jnp.broadcast_to(dsinks[h], (1, 128))

    cur, prev, first = _swa_specs(rs, ns)
    late = lambda w: pl.BlockSpec((rs, w), lambda i: (jnp.maximum(i - 1, 0), 0))
    bias = _swa_bias()
    return _pallas(
        body, name="swa_bwd", grid=(ns + 1,),
        in_specs=[pl.BlockSpec(memory_space=pltpu.SMEM), _full(bias.shape), cur(512), prev(256), cur(256), first(256),
                  prev(256), cur(256), first(256), cur(512), cur(512)],
        out_specs=[cur(512), late(256), late(256), first(256), first(256), _full((SWA_QH, 128))],
        out_shape=[jax.ShapeDtypeStruct((t, SWA_W), F32), jax.ShapeDtypeStruct((t, 256), F32),
                   jax.ShapeDtypeStruct((t, 256), F32), jax.ShapeDtypeStruct((BLK, 256), F32),
                   jax.ShapeDtypeStruct((BLK, 256), F32), jax.ShapeDtypeStruct((SWA_QH, 128), F32)],
        scratch_shapes=[pltpu.VMEM((rs, 256), F32), pltpu.VMEM((rs, 256), F32)],
        args=(sinks, bias, sq, sk, sk, sk, sv, sv, sv, o, do), hook=hook)


def _mix_out(h1, ogla, gg, oswa, wgn, wsn, wout, wpost):
    t = h1.shape[0]
    tm = _row_tile(t)

    def body(h_ref, og_ref, gg_ref, os_ref, wgn_ref, wsn_ref, wout_ref, wpost_ref, h2_ref, cat_ref, m_ref):
        parts = []
        for h in range(GLA_HEADS):
            ls = slice(GLA_DV * h, GLA_DV * (h + 1))
            y, _, _ = _rms(og_ref[:, ls], wgn_ref[...])
            g = gg_ref[:, ls]
            parts.append(y * (g * _sigmoid(g)))
        ys, _, _ = _rms(os_ref[...], wsn_ref[...])
        cat = jnp.concatenate(parts + [ys], axis=1).astype(BF16)
        cat_ref[...] = cat
        m = _dot(cat, wout_ref[...])
        m_ref[...] = m
        y, _, _ = _rms(m, wpost_ref[...])
        h2_ref[...] = h_ref[...] + y

    def row(w):
        return pl.BlockSpec((tm, w), lambda i: (i, 0))

    return pl.pallas_call(
        body, name="mix_out", grid=(t // tm,),
        in_specs=[row(D_MODEL), row(512), row(512), row(512), _full((1, GLA_DV)), _full((1, SWA_W)),
                  _full((D_MODEL, D_MODEL)), _full((1, D_MODEL))],
        out_specs=[row(D_MODEL), row(D_MODEL), row(D_MODEL)],
        out_shape=[jax.ShapeDtypeStruct((t, D_MODEL), F32), jax.ShapeDtypeStruct((t, D_MODEL), BF16),
                   jax.ShapeDtypeStruct((t, D_MODEL), F32)],
        compiler_params=_cparams(1),
    )(h1, ogla, gg, oswa, wgn, wsn, wout, wpost)


def _mix_out_bwd(dh2, m, ogla, gg, oswa, wgn, wsn, wout, wpost, hook=None):
    t = dh2.shape[0]
    tm = _row_tile(t)

    def body(dh_ref, m_ref, og_ref, gg_ref, os_ref, wgn_ref, wsn_ref, wout_ref, wpost_ref,
             dog_ref, dgg_ref, dos_ref, dm_ref, dwpost_ref, dwgn_ref, dwsn_ref):
        @pl.when(pl.program_id(0) == 0)
        def _():
            dwpost_ref[...] = jnp.zeros_like(dwpost_ref)
            dwgn_ref[...] = jnp.zeros_like(dwgn_ref)
            dwsn_ref[...] = jnp.zeros_like(dwsn_ref)

        wpost = wpost_ref[...]
        _, mh, r = _rms(m_ref[...], wpost)
        dm, dw = _rms_bwd(mh, r, wpost, dh_ref[...])
        dwpost_ref[...] += dw
        dmb = dm.astype(BF16)
        dm_ref[...] = dmb
        dcat = _dg(dmb, wout_ref[...], NT)
        wgn = wgn_ref[...]
        for h in range(GLA_HEADS):
            ls = slice(GLA_DV * h, GLA_DV * (h + 1))
            dog = dcat[:, ls]
            g = gg_ref[:, ls]
            sg = _sigmoid(g)
            y, xh, r = _rms(og_ref[:, ls], wgn)
            dgg_ref[:, ls] = dog * y * (sg * (1.0 + g * (1.0 - sg)))
            dx, dw = _rms_bwd(xh, r, wgn, dog * (g * sg))
            dog_ref[:, ls] = dx
            dwgn_ref[...] += dw
        wsn = wsn_ref[...]
        _, xh, r = _rms(os_ref[...], wsn)
        dx, dw = _rms_bwd(xh, r, wsn, dcat[:, GLA_W:])
        dos_ref[...] = dx
        dwsn_ref[...] += dw

    def row(w):
        return pl.BlockSpec((tm, w), lambda i: (i, 0))

    def rshape(w, dt=F32):
        return jax.ShapeDtypeStruct((t, w), dt)

    return _pallas(
        body, name="mix_out_bwd", grid=(t // tm,),
        in_specs=[row(D_MODEL), row(D_MODEL), row(512), row(512), row(512), _full((1, GLA_DV)), _full((1, SWA_W)),
                  _full((D_MODEL, D_MODEL)), _full((1, D_MODEL))],
        out_specs=[row(512), row(512), row(512), row(D_MODEL), _full((1, D_MODEL)), _full((1, GLA_DV)),
                   _full((1, SWA_W))],
        out_shape=[rshape(512), rshape(512), rshape(512), rshape(D_MODEL, BF16),
                   jax.ShapeDtypeStruct((1, D_MODEL), F32), jax.ShapeDtypeStruct((1, GLA_DV), F32),
                   jax.ShapeDtypeStruct((1, SWA_W), F32)],
        args=(dh2, m, ogla, gg, oswa, wgn, wsn, wout, wpost), hook=hook)


def _mix_in_bwd(dh2, h1, wmixpre, winp, wa2p, bap, cos, sin, ga, dgq, dgk, dgv, dgg, dla, dsq, dsk, dsv, dkm, dvm):
    t = h1.shape[0]
    tm = _row_tile(t)

    def body(dh2_ref, h_ref, w_ref, win_ref, wa2_ref, ba_ref, cos_ref, sin_ref, ga_ref, dgq_ref, dgk_ref, dgv_ref,
             dgg_ref, dla_ref, dsq_ref, dsk_ref, dsv_ref, dkm_ref, dvm_ref,
             dh1_ref, dproj_ref, dw_ref, dwa2_ref, dba_ref):
        i = pl.program_id(0)

        @pl.when(i == 0)
        def _():
            dw_ref[...] = jnp.zeros_like(dw_ref)
            dwa2_ref[...] = jnp.zeros_like(dwa2_ref)
            dba_ref[...] = jnp.zeros_like(dba_ref)

        first = (i == 0).astype(F32)
        c = cos_ref[...]
        s = -sin_ref[...]
        fh = _first_half_mask(tm)
        dproj_ref[:, P_GQ:P_GK] = dgq_ref[...].astype(BF16)
        dproj_ref[:, P_GK:P_GV] = dgk_ref[...].astype(BF16)
        dproj_ref[:, P_GV:P_GG] = dgv_ref[...].astype(BF16)
        dproj_ref[:, P_GG:P_GA] = dgg_ref[...].astype(BF16)
        gab = ga_ref[...].astype(BF16)
        z = _dot(gab, wa2_ref[...]) + ba_ref[...]
        row_id = i * tm + lax.broadcasted_iota(jnp.int32, (tm, 1), 0)
        dz = jnp.where(row_id >= PAD, dla_ref[...] * (1.0 / GLA_TAU) * (1.0 - _sigmoid(z)), 0.0)
        dzb = dz.astype(BF16)
        dba_ref[...] += jnp.sum(dz, axis=0, keepdims=True)
        dwa2_ref[...] += _dg(gab, dzb, TN)
        dproj_ref[:, P_GA:P_SQ] = _dg(dzb, wa2_ref[...], NT).astype(BF16)
        for k in range(4):
            dy = dsq_ref[:, 128 * k:128 * (k + 1)]
            dproj_ref[:, P_SQ + 128 * k:P_SQ + 128 * (k + 1)] = (dy * c + _rot_half(dy, fh) * s).astype(BF16)
        for k in range(2):
            ls = slice(128 * k, 128 * (k + 1))
            dy = dsk_ref[:, ls]
            dy = jnp.concatenate([dy[:BLK] + first * dkm_ref[:, ls], dy[BLK:]], axis=0) if tm > BLK else (
                dy + first * dkm_ref[:, ls])
            dproj_ref[:, P_SK + 128 * k:P_SK + 128 * (k + 1)] = (dy * c + _rot_half(dy, fh) * s).astype(BF16)
            dv = dsv_ref[:, ls]
            dv = jnp.concatenate([dv[:BLK] + first * dvm_ref[:, ls], dv[BLK:]], axis=0) if tm > BLK else (
                dv + first * dvm_ref[:, ls])
            dproj_ref[:, P_SV + 128 * k:P_SV + 128 * (k + 1)] = dv.astype(BF16)
        dn = _dg(dproj_ref[...], win_ref[...], NT)
        w = w_ref[...]
        _, hh, r = _rms(h_ref[...], w)
        dx, dw = _rms_bwd(hh, r, w, dn)
        dw_ref[...] += dw
        dh1_ref[...] = dh2_ref[...] + dx

    def row(w):
        return pl.BlockSpec((tm, w), lambda i: (i, 0))

    return pl.pallas_call(
        body, name="mix_in_bwd", grid=(t // tm,),
        in_specs=[row(D_MODEL), row(D_MODEL), _full((1, D_MODEL)), _full((D_MODEL, P_END)), _full((128, GLA_KW)),
                  _full((1, GLA_KW)), row(128), row(128), row(128), row(256), row(256), row(512), row(512), row(256),
                  row(512), row(256), row(256), _full((BLK, 256)), _full((BLK, 256))],
        out_specs=[row(D_MODEL), row(P_END), _full((1, D_MODEL)), _full((128, GLA_KW)), _full((1, GLA_KW))],
        out_shape=[jax.ShapeDtypeStruct((t, D_MODEL), F32), jax.ShapeDtypeStruct((t, P_END), BF16),
                   jax.ShapeDtypeStruct((1, D_MODEL), F32), jax.ShapeDtypeStruct((128, GLA_KW), F32),
                   jax.ShapeDtypeStruct((1, GLA_KW), F32)],
        compiler_params=_cparams(1),
    )(dh2, h1, wmixpre, winp, wa2p, bap, cos, sin, ga, dgq, dgk, dgv, dgg, dla, dsq, dsk, dsv, dkm, dvm)


def _adamw_update(w, g, m, v):
    m = ADAM_B1 * m + (1.0 - ADAM_B1) * g
    v = ADAM_B2 * v + (1.0 - ADAM_B2) * (g * g)
    m_hat = m / (1.0 - ADAM_B1 ** ADAM_STEP)
    v_hat = v / (1.0 - ADAM_B2 ** ADAM_STEP)
    return -ADAM_LR * (m_hat / (jnp.sqrt(v_hat) + ADAM_EPS) + ADAM_WD * w), m, v


def _adamw_halves(w, g_mine, g_other, m, v, c_idx, row0=0):
    r, c = w.shape
    h = g_mine.shape[0]
    tr = _div_tile(math.gcd(r, h))
    nth = h // tr
    t0 = row0 // tr
    assert t0 * tr == row0

    def body(c_ref, w_ref, gm_ref, go_ref, m_ref, v_ref, g_ref, d_ref, nm_ref, nv_ref):
        hh = (t0 + pl.program_id(0)) // nth
        g = jnp.where(hh == c_ref[0], gm_ref[...], go_ref[...])
        g_ref[...] = g
        d_ref[...], nm_ref[...], nv_ref[...] = _adamw_update(w_ref[...], g, m_ref[...], v_ref[...])

    spec = pl.BlockSpec((tr, c), lambda i, c_ref: (i, 0))

    def gspec(is_mine):
        def index(i, c_ref):
            used = ((t0 + i) // nth == c_ref[0]) == is_mine
            return (jnp.where(used, (t0 + i) % nth, 0), 0)
        return pl.BlockSpec((tr, c), index)

    shape = jax.ShapeDtypeStruct((r, c), F32)
    return pl.pallas_call(
        body, name="adamw_halves",
        grid_spec=pltpu.PrefetchScalarGridSpec(
            num_scalar_prefetch=1, grid=(r // tr,), in_specs=[spec, gspec(True), gspec(False), spec, spec],
            out_specs=[spec] * 4),
        out_shape=[shape] * 4, compiler_params=_cparams(1),
    )(c_idx, w, g_mine, g_other, m, v)


def _place():
    x, y, c = lax.axis_index("x"), lax.axis_index("y"), lax.axis_index("c")
    chips = [(1 - x, y), (x, 1 - y), (1 - x, 1 - y)]
    return x, y, c, chips


def _remote(send_sem, recv_sem, src, dst, to):
    return pltpu.make_async_remote_copy(src_ref=src, dst_ref=dst, send_sem=send_sem, recv_sem=recv_sem,
                                        device_id=to, device_id_type=MESH)


def _half(ref_rows, c):
    h = ref_rows // 2
    return pl.ds(pl.multiple_of(c * h, 8), h)


def _own_slot(shard, q):
    return lax.dynamic_update_slice(jnp.zeros((N_CHIPS,) + shard.shape, shard.dtype), shard[None], (q, 0, 0))


class _GatherChips:
    has_mid = True

    def __init__(self, bufs):
        n = len(bufs)
        self.inputs = list(bufs)
        self.out_shape = [jax.ShapeDtypeStruct(b.shape, b.dtype) for b in bufs]
        self.aliases = [(t, t) for t in range(n)]
        self.scratch = [pltpu.SemaphoreType.DMA((n, 6)), pltpu.SemaphoreType.DMA((n, 6))]

    def start(self, ins, outs, scr):
        send, recv = scr
        x, y, c, chips = _place()
        q = 2 * x + y
        for t, (i_ref, o_ref) in enumerate(zip(ins, outs)):
            rows = _half(i_ref.shape[1], c)
            for j, (cx, cy) in enumerate(chips):
                _remote(send.at[t, j], recv.at[t, j], i_ref.at[q, rows], o_ref.at[q, rows], (cx, cy, c)).start()

    def mid(self, ins, outs, scr):
        send, recv = scr
        x, y, c, chips = _place()
        for t, o_ref in enumerate(outs):
            rows = _half(o_ref.shape[1], c)
            for j, (cx, cy) in enumerate(chips):
                slot = o_ref.at[2 * cx + cy, rows]
                _remote(send.at[t, j], recv.at[t, j], slot, slot, (cx, cy, c)).wait_recv()
                _remote(send.at[t, 3 + j], recv.at[t, 3 + j], slot, slot, (x, y, 1 - c)).start()

    def finish(self, ins, outs, scr):
        send, recv = scr
        x, y, c, chips = _place()
        for t, o_ref in enumerate(outs):
            mine, other = _half(o_ref.shape[1], c), _half(o_ref.shape[1], 1 - c)
            for j, (cx, cy) in enumerate(chips):
                slot = o_ref.at[2 * cx + cy, other]
                _remote(send.at[t, 3 + j], recv.at[t, 3 + j], slot, slot, (x, y, 1 - c)).wait_recv()
            for j, (cx, cy) in enumerate(chips):
                sent = o_ref.at[2 * cx + cy, mine]
                _remote(send.at[t, j], recv.at[t, j], sent, sent, (cx, cy, c)).wait_send()
                _remote(send.at[t, 3 + j], recv.at[t, 3 + j], sent, sent, (x, y, 1 - c)).wait_send()


class _PairExchange:
    has_mid = False
    aliases = ()

    def __init__(self, arrs):
        n = len(arrs)
        self.inputs = list(arrs)
        self.out_shape = [jax.ShapeDtypeStruct((a.shape[0], a.shape[1] // 2, a.shape[2]), a.dtype) for a in arrs]
        self.scratch = [pltpu.SemaphoreType.DMA((n,)), pltpu.SemaphoreType.DMA((n,))]

    def _copies(self, ins, outs, scr):
        send, recv = scr
        x, y, c, _ = _place()
        return [_remote(send.at[t], recv.at[t], i_ref.at[:, _half(i_ref.shape[1], 1 - c)], o_ref, (x, y, 1 - c))
                for t, (i_ref, o_ref) in enumerate(zip(ins, outs))]

    def start(self, ins, outs, scr):
        for cp in self._copies(ins, outs, scr):
            cp.start()

    def finish(self, ins, outs, scr):
        for cp in self._copies(ins, outs, scr):
            cp.wait()


class _ChipScatter:
    has_mid = False
    aliases = ()

    def __init__(self, arrs):
        n = len(arrs)
        self.inputs = list(arrs)
        self.out_shape = [jax.ShapeDtypeStruct((3,) + a.shape[1:], a.dtype) for a in arrs]
        self.scratch = [pltpu.SemaphoreType.DMA((n, 3)), pltpu.SemaphoreType.DMA((n, 3))]

    def _copies(self, ins, outs, scr):
        send, recv = scr
        x, y, c, chips = _place()
        return [_remote(send.at[t, j], recv.at[t, j], i_ref.at[2 * cx + cy], o_ref.at[j], (cx, cy, c))
                for t, (i_ref, o_ref) in enumerate(zip(ins, outs)) for j, (cx, cy) in enumerate(chips)]

    def start(self, ins, outs, scr):
        for cp in self._copies(ins, outs, scr):
            cp.start()

    def finish(self, ins, outs, scr):
        for cp in self._copies(ins, outs, scr):
            cp.wait()


class _PairShare:
    has_mid = False
    aliases = ()

    def __init__(self, arrs):
        n = len(arrs)
        self.inputs = list(arrs)
        self.out_shape = [jax.ShapeDtypeStruct(a.shape, a.dtype) for a in arrs]
        self.scratch = [pltpu.SemaphoreType.DMA((n,)), pltpu.SemaphoreType.DMA((n,))]

    def _copies(self, ins, outs, scr):
        send, recv = scr
        x, y, c, _ = _place()
        return [_remote(send.at[t], recv.at[t], i_ref, o_ref, (x, y, 1 - c))
                for t, (i_ref, o_ref) in enumerate(zip(ins, outs))]

    def start(self, ins, outs, scr):
        for cp in self._copies(ins, outs, scr):
            cp.start()

    def finish(self, ins, outs, scr):
        for cp in self._copies(ins, outs, scr):
            cp.wait()


def _comm_call(hook, name):
    n_in, n_out = len(hook.inputs), len(hook.out_shape)

    def body(*refs):
        ins, outs, scr = refs[:n_in], refs[n_in:n_in + n_out], refs[n_in + n_out:]
        hook.start(ins, outs, scr)
        if hook.has_mid:
            hook.mid(ins, outs, scr)
        hook.finish(ins, outs, scr)

    return pl.pallas_call(body, name=name, in_specs=[ANY] * n_in, out_specs=[ANY] * n_out,
                          out_shape=list(hook.out_shape), scratch_shapes=list(hook.scratch),
                          input_output_aliases=dict(hook.aliases))(*hook.inputs)


def _all_gather_devices(vecs):
    n = len(vecs)

    def body(*refs):
        x_refs, out_refs = refs[:n], refs[n:2 * n]
        send_sems, recv_sems, local_sems = refs[2 * n:]
        x, y, c, chips = _place()
        me, sibling = (x, y, c), (x, y, 1 - c)
        waits = []
        for t, (x_ref, out_ref) in enumerate(zip(x_refs, out_refs)):
            def slot(px, py, pc, out_ref=out_ref):
                return out_ref.at[4 * px + 2 * py + pc]

            def copy(k, block, to, src=None, t=t, slot=slot):
                return pltpu.make_async_remote_copy(
                    src_ref=slot(*block) if src is None else src, dst_ref=slot(*block), send_sem=send_sems.at[t, k],
                    recv_sem=recv_sems.at[t, k], device_id=to, device_id_type=MESH)

            mine = pltpu.make_async_copy(x_ref, slot(*me), local_sems.at[t])
            mine.start()
            first = [copy(0, me, sibling, src=x_ref)]
            first += [copy(1 + j, me, (*chip, c), src=x_ref) for j, chip in enumerate(chips)]
            for cp in first:
                cp.start()
            waits.append((copy, mine, first))
        for copy, mine, first in waits:
            passed = [copy(4 + j, (*chip, c), sibling) for j, chip in enumerate(chips)]
            for j, chip in enumerate(chips):
                copy(1 + j, (*chip, c), me).wait_recv()
                passed[j].start()
            copy(0, sibling, me).wait_recv()
            for j, chip in enumerate(chips):
                copy(4 + j, (*chip, 1 - c), me).wait_recv()
            for cp in first + passed:
                cp.wait_send()
            mine.wait()

    vmem = pl.BlockSpec(memory_space=pltpu.VMEM)
    return pl.pallas_call(
        body, name="all_gather_devices", in_specs=[vmem] * n, out_specs=[vmem] * n,
        out_shape=[jax.ShapeDtypeStruct((N_DEV,) + v.shape, v.dtype) for v in vecs],
        scratch_shapes=[pltpu.SemaphoreType.DMA((n, 7)), pltpu.SemaphoreType.DMA((n, 7)),
                        pltpu.SemaphoreType.DMA((n,))],
    )(*vecs)


def _pair_sum(g, other, c_idx):
    nq, r, w = g.shape
    h = r // 2
    tr = _div_tile(h)
    nt = h // tr

    def body(c_ref, g_ref, o_ref, s_ref):
        s_ref[...] = (g_ref[...].astype(F32) + o_ref[...].astype(F32)).astype(s_ref.dtype)

    return pl.pallas_call(
        body, name="pair_sum",
        grid_spec=pltpu.PrefetchScalarGridSpec(
            num_scalar_prefetch=1, grid=(nq, nt),
            in_specs=[pl.BlockSpec((None, tr, w), lambda k, i, c_ref: (k, c_ref[0] * nt + i, 0)),
                      pl.BlockSpec((None, tr, w), lambda k, i, c_ref: (k, i, 0))],
            out_specs=pl.BlockSpec((None, tr, w), lambda k, i, c_ref: (k, i, 0))),
        out_shape=jax.ShapeDtypeStruct((nq, h, w), g.dtype),
        compiler_params=_cparams(2),
    )(c_idx, g, other)


def _chip_sum(s, others, q_idx):
    _, h, w = s.shape
    tr = _div_tile(h)

    def body(q_ref, s_ref, o_ref, out_ref):
        out_ref[...] = ((s_ref[...].astype(F32) + o_ref[0].astype(F32)) + o_ref[1].astype(F32)) + o_ref[2].astype(F32)

    return pl.pallas_call(
        body, name="chip_sum",
        grid_spec=pltpu.PrefetchScalarGridSpec(
            num_scalar_prefetch=1, grid=(h // tr,),
            in_specs=[pl.BlockSpec((None, tr, w), lambda i, q_ref: (q_ref[0], i, 0)),
                      pl.BlockSpec((3, tr, w), lambda i, q_ref: (0, i, 0))],
            out_specs=pl.BlockSpec((tr, w), lambda i, q_ref: (i, 0))),
        out_shape=jax.ShapeDtypeStruct((h, w), F32),
        compiler_params=_cparams(1),
    )(q_idx, s, others)


def _small_update(q_idx, parts, ws, ms, vs, col_block):
    n = len(parts)
    has_w = [w is not None for w in ws]

    def body(q_ref, *refs):
        pos = 0
        ins = []
        for t in range(n):
            k = 4 if has_w[t] else 1
            ins.append(refs[pos:pos + k])
            pos += k
        outs = refs[pos:]
        opos = 0
        for t in range(n):
            p_ref = ins[t][0]
            g = p_ref[0]
            for s in range(1, p_ref.shape[0]):
                g = g + p_ref[s]
            if has_w[t]:
                _, w_ref, m_ref, v_ref = ins[t]
                g_ref, d_ref, nm_ref, nv_ref = outs[opos:opos + 4]
                opos += 4
                g_ref[...] = g
                d_ref[...], nm_ref[...], nv_ref[...] = _adamw_update(w_ref[...], g, m_ref[...], v_ref[...])
            else:
                outs[opos][...] = g
                opos += 1

    def whole(shape):
        nd = len(shape)
        return pl.BlockSpec(shape, lambda i, q_ref: (0,) * nd)

    in_specs, out_specs, out_shape, args = [], [], [], []
    for t in range(n):
        k, r, wf = parts[t].shape
        if col_block[t]:
            w = wf // N_CHIPS
            in_specs.append(pl.BlockSpec((k, r, w), lambda i, q_ref: (0, 0, q_ref[0])))
        else:
            w = wf
            in_specs.append(whole((k, r, wf)))
        args.append(parts[t])
        if has_w[t]:
            assert ws[t].shape == (r, w), (ws[t].shape, r, w)
            in_specs += [whole((r, w))] * 3
            args += [ws[t], ms[t], vs[t]]
            out_specs += [whole((r, w))] * 4
            out_shape += [jax.ShapeDtypeStruct((r, w), F32)] * 4
        else:
            out_specs.append(whole((r, w)))
            out_shape.append(jax.ShapeDtypeStruct((r, w), F32))
    return pl.pallas_call(
        body, name="small_update",
        grid_spec=pltpu.PrefetchScalarGridSpec(num_scalar_prefetch=1, grid=(1,), in_specs=in_specs,
                                               out_specs=out_specs),
        out_shape=out_shape, compiler_params=_cparams(1),
    )(q_idx, *args)


_PACK_SEGMENTS = ((0, 1552), None, (1552, 2064), (2064, 2128), (2064, 2128), (2128, 2192), (2128, 2192),
                  (2192, 2256), (2192, 2256), (2256, 2320), (2256, 2320))
_UNPACK_SEGMENTS = (((0, 1552), (0,)), ((1552, 2064), (P_SQ,)), ((2064, 2128), (P_SK, P_SK + 64)),
                    ((2128, 2192), (P_SK + 128, P_SK + 192)), ((2192, 2256), (P_SV, P_SV + 64)),
                    ((2256, 2320), (P_SV + 128, P_SV + 192)))


def _pack_win(w4):
    per = w4.shape[2]
    pieces = []
    for seg in _PACK_SEGMENTS:
        if seg is None:
            pieces.append(jnp.zeros((w4.shape[1], 128 - GLA_RANK), w4.dtype))
            continue
        for q in range(w4.shape[0]):
            lo, hi = max(seg[0], q * per), min(seg[1], (q + 1) * per)
            if lo < hi:
                pieces.append(w4[q][:, lo - q * per:hi - q * per])
    return jnp.concatenate(pieces, axis=1)


def _unpack_dwin(d):
    per = D_IN // N_CHIPS
    chips = []
    for q in range(N_CHIPS):
        pieces = []
        for (a, b), starts in _UNPACK_SEGMENTS:
            lo, hi = max(a, q * per), min(b, (q + 1) * per)
            if lo < hi:
                copies = [d[:, s + lo - a:s + hi - a] for s in starts]
                pieces.append(copies[0] if len(copies) == 1 else copies[0] + copies[1])
        chips.append(jnp.concatenate(pieces, axis=1))
    return jnp.stack(chips).astype(BF16)


def _local_step(x, target, meta, p):
    s = x.shape[0]
    t = s + BLK
    h0 = jnp.concatenate([jnp.zeros((PAD, D_MODEL), F32), meta, x], axis=0)
    cos, sin = _rope_tables(t)

    h1, n1, g1, u1, a1, f1 = _ffn_fwd(h0, p["ffn1_pre_norm"], p["ffn1_w_gate"], p["ffn1_w_up"], p["ffn1_w_down"],
                                      p["ffn1_post_norm"])
    n2, gq, gk, gv, gg, ga, la, sq, sk, sv = _mix_proj(h1, p["mix_pre_norm"], p["w_in"], p["gla_w_a2"], p["gla_b_a"],
                                                       cos, sin)
    ogla, ss = _gla_fwd(gq, gk, gv, la)
    oswa = _swa_fwd(p["swa_sinks"], sq, sk, sv)
    h2, cat, m = _mix_out(h1, ogla, gg, oswa, p["gla_out_norm"], p["swa_out_norm"], p["w_out"], p["mix_post_norm"])
    grads = {}
    dy, n3, g3, u3, a3, df3, grads["ffn2_post_norm"], sse = _ffn_fwd(
        h2, p["ffn2_pre_norm"], p["ffn2_w_gate"], p["ffn2_w_up"], p["ffn2_w_down"], p["ffn2_post_norm"],
        target=target)

    dh2, dg3, du3, grads["ffn2_pre_norm"] = _ffn_bwd(
        dy, h2, None, g3, u3, p["ffn2_pre_norm"], p["ffn2_w_gate"], p["ffn2_w_up"], p["ffn2_w_down"],
        p["ffn2_post_norm"], df=df3)
    (gud,) = _ffn_wgrad(n3, df3, dg3, du3, a3)
    grads["ffn2_w_gate"], grads["ffn2_w_up"], grads["ffn2_w_down"] = gud[:, :FJ], gud[:, FJ:2 * FJ], gud[:, 2 * FJ:]

    dogla, dgg, doswa, dm, grads["mix_post_norm"], grads["gla_out_norm"], grads["swa_out_norm"] = _mix_out_bwd(
        dh2, m, ogla, gg, oswa, p["gla_out_norm"], p["swa_out_norm"], p["w_out"], p["mix_post_norm"])
    grads["w_out"] = _xty(cat, dm)
    dsq, dsk, dsv, dkm, dvm, dsinks = _swa_bwd(p["swa_sinks"], sq, sk, sv, oswa, doswa)
    grads["swa_sinks"] = dsinks[:, 0]
    dgq, dgk, dgv, dla = _gla_bwd(gq, gk, gv, la, ss, dogla)
    dh1, dproj, grads["mix_pre_norm"], dwa2p, grads["gla_b_a"] = _mix_in_bwd(
        dh2, h1, p["mix_pre_norm"], p["w_in"], p["gla_w_a2"], p["gla_b_a"], cos, sin, ga, dgq, dgk, dgv, dgg, dla,
        dsq, dsk, dsv, dkm, dvm)
    grads["gla_w_a2"] = dwa2p[:GLA_RANK]
    grads["w_in"] = _unpack_dwin(_xty(n2, dproj))

    dh0, df1, dg1, du1, grads["ffn1_pre_norm"], grads["ffn1_post_norm"] = _ffn_bwd(
        dh1, h0, f1, g1, u1, p["ffn1_pre_norm"], p["ffn1_w_gate"], p["ffn1_w_up"], p["ffn1_w_down"],
        p["ffn1_post_norm"])
    (gud,) = _ffn_wgrad(n1, df1, dg1, du1, a1)
    grads["ffn1_w_gate"], grads["ffn1_w_up"], grads["ffn1_w_down"] = gud[:, :FJ], gud[:, FJ:2 * FJ], gud[:, 2 * FJ:]
    grads["meta_tokens"] = dh0[PAD:BLK]
    return sse[0, 0], dh0[BLK:], grads


WEIGHTS = ['meta_tokens', 'ffn1_pre_norm', 'ffn1_w_gate', 'ffn1_w_up', 'ffn1_w_down', 'ffn1_post_norm',
           'mix_pre_norm', 'w_in', 'gla_w_a2', 'gla_b_a', 'gla_out_norm', 'swa_sinks', 'swa_out_norm', 'w_out',
           'mix_post_norm', 'ffn2_pre_norm', 'ffn2_w_gate', 'ffn2_w_up', 'ffn2_w_down', 'ffn2_post_norm']
BIG = ['ffn1_w_gate', 'ffn1_w_up', 'ffn1_w_down', 'w_in', 'w_out', 'ffn2_w_gate', 'ffn2_w_up', 'ffn2_w_down']
SMALL = [n for n in WEIGHTS if n not in BIG]
FJ = D_FF // N_CHIPS
D_IN_J = D_IN // N_CHIPS
D_OUT_J = D_MODEL // N_CHIPS
TRANSPOSED = ('ffn1_w_gate', 'ffn1_w_up', 'ffn2_w_gate', 'ffn2_w_up')


def _shard2d(name, a):
    return a[0].T if name in TRANSPOSED else a[0]


def _unshard2d(name, a):
    return (a.T if name in TRANSPOSED else a)[None]


def kernel(x, meta_tokens, ffn1_pre_norm, ffn1_w_gate, ffn1_w_up, ffn1_w_down, ffn1_post_norm, mix_pre_norm, w_in, gla_w_a2, gla_b_a, gla_out_norm, swa_sinks, swa_out_norm, w_out, mix_post_norm, ffn2_pre_norm, ffn2_w_gate, ffn2_w_up, ffn2_w_down, ffn2_post_norm, loss_target, m_meta_tokens, m_ffn1_pre_norm, m_ffn1_w_gate, m_ffn1_w_up, m_ffn1_w_down, m_ffn1_post_norm, m_mix_pre_norm, m_w_in, m_gla_w_a2, m_gla_b_a, m_gla_out_norm, m_swa_sinks, m_swa_out_norm, m_w_out, m_mix_post_norm, m_ffn2_pre_norm, m_ffn2_w_gate, m_ffn2_w_up, m_ffn2_w_down, m_ffn2_post_norm, v_meta_tokens, v_ffn1_pre_norm, v_ffn1_w_gate, v_ffn1_w_up, v_ffn1_w_down, v_ffn1_post_norm, v_mix_pre_norm, v_w_in, v_gla_w_a2, v_gla_b_a, v_gla_out_norm, v_swa_sinks, v_swa_out_norm, v_w_out, v_mix_post_norm, v_ffn2_pre_norm, v_ffn2_w_gate, v_ffn2_w_up, v_ffn2_w_down, v_ffn2_post_norm):
    args = dict(locals())
    w = {n: args[n] for n in WEIGHTS}
    mom = {n: args["m_" + n] for n in WEIGHTS}
    var = {n: args["v_" + n] for n in WEIGHTS}
    cx, cy, cc = lax.axis_index("x"), lax.axis_index("y"), lax.axis_index("c")
    q_idx = (2 * cx + cy).astype(jnp.int32).reshape(1)
    c_idx = cc.astype(jnp.int32).reshape(1)

    q_chip = 2 * cx + cy
    bf = {n: _own_slot(_shard2d(n, w[n]).astype(BF16), q_chip) for n in BIG}
    qc_idx = jnp.stack([q_chip, cc]).astype(jnp.int32)
    early = _GatherChips([_own_slot(w["meta_tokens"], q_chip),
                          _own_slot(w["gla_w_a2"].reshape(GLA_RANK, GLA_KW // N_CHIPS), q_chip)])
    meta4, wa24 = _comm_call(early, "gather_small")
    meta_full = meta4.transpose(1, 0, 2).reshape(N_META, D_MODEL)
    wa2p = jnp.pad(wa24.transpose(1, 0, 2).reshape(GLA_RANK, GLA_KW), ((0, 128 - GLA_RANK), (0, 0))).astype(BF16)
    sinks = w["swa_sinks"].reshape(SWA_QH)

    seq, target = x[0], loss_target[0]
    t = seq.shape[0] + BLK
    h0, n1 = _embed_norm(seq, meta_full, w["ffn1_pre_norm"])
    cos, sin = _rope_tables(t)
    late = _GatherChips([bf["w_in"], bf["w_out"], bf["ffn2_w_gate"], bf["ffn2_w_up"], bf["ffn2_w_down"]])
    (h1, g1, u1, a1, f1), (wg1, wu1, wd1), (win4, wout4, wg2, wu2, wd2) = _ffn_fwd_gather(
        h0, n1, [bf["ffn1_w_gate"], bf["ffn1_w_up"], bf["ffn1_w_down"]], w["ffn1_post_norm"], qc_idx, late)
    winp = _pack_win(win4)
    wout = wout4.reshape(D_MODEL, D_MODEL)
    n2, gq, gk, gv, gg, ga, la, sq, sk, sv = _mix_proj(h1, w["mix_pre_norm"], winp, wa2p, w["gla_b_a"], cos, sin)
    ogla, ss = _gla_fwd(gq, gk, gv, la)
    oswa = _swa_fwd(sinks, sq, sk, sv)
    h2, cat, m = _mix_out(h1, ogla, gg, oswa, w["gla_out_norm"], w["swa_out_norm"], wout, w["mix_post_norm"])
    g = {}
    dy, n3, g3, u3, a3, df3, g["ffn2_post_norm"], sse = _ffn_fwd(
        h2, w["ffn2_pre_norm"], wg2, wu2, wd2, w["ffn2_post_norm"], target=target)
    loss = lax.psum(sse[0, 0] * (0.5 / D_MODEL), ("x", "y", "c"))

    dh2, dg3, du3, g["ffn2_pre_norm"] = _ffn_bwd(
        dy, h2, None, g3, u3, w["ffn2_pre_norm"], wg2, wu2, wd2, w["ffn2_post_norm"], df=df3)
    (gf2,) = _ffn_wgrad(n3, df3, dg3, du3, a3)
    (dogla, dgg, doswa, dm, g["mix_post_norm"], g["gla_out_norm"], g["swa_out_norm"]), (rgf2,) = _mix_out_bwd(
        dh2, m, ogla, gg, oswa, w["gla_out_norm"], w["swa_out_norm"], wout, w["mix_post_norm"],
        hook=_PairExchange([gf2]))
    sgf2 = _pair_sum(gf2, rgf2, c_idx)
    gout = _xty(cat, dm).reshape(N_CHIPS, D_OUT_J, D_MODEL).astype(BF16)
    (dsq, dsk, dsv, dkm, dvm, dsinks), (ogf2,) = _swa_bwd(sinks, sq, sk, sv, oswa, doswa,
                                                          hook=_ChipScatter([sgf2]))
    g["swa_sinks"] = dsinks
    dgq, dgk, dgv, dla = _gla_bwd(gq, gk, gv, la, ss, dogla)
    dh1, dproj, g["mix_pre_norm"], dwa2p, g["gla_b_a"] = _mix_in_bwd(
        dh2, h1, w["mix_pre_norm"], winp, wa2p, w["gla_b_a"], cos, sin, ga, dgq, dgk, dgv, dgg, dla,
        dsq, dsk, dsv, dkm, dvm)
    g["gla_w_a2"] = dwa2p[:GLA_RANK]
    gin = _unpack_dwin(_xty(n2, dproj))
    (dh0, df1, dg1, du1, g["ffn1_pre_norm"], g["ffn1_post_norm"]), (rgin, rgout) = _ffn_bwd(
        dh1, h0, f1, g1, u1, w["ffn1_pre_norm"], wg1, wu1, wd1, w["ffn1_post_norm"],
        hook=_PairExchange([gin, gout]))
    sgin, sgout = _pair_sum(gin, rgin, c_idx), _pair_sum(gout, rgout, c_idx)
    own1, others1, (ogin, ogout) = _ffn_wgrad_reduce(n1, df1, dg1, du1, a1, qc_idx, _ChipScatter([sgin, sgout]))
    g["meta_tokens"] = dh0[PAD:BLK]
    grad_x = dh0[BLK:]
    halves = [_chip_sum(own1[None], others1, jnp.zeros((1,), jnp.int32))]
    halves += [_chip_sum(s, o, q_idx) for s, o in ((sgin, ogin), (sgout, ogout), (sgf2, ogf2))]
    others = _comm_call(_PairShare(halves), "pair_share")
    reduced = {"ffn1_w_gate": (0, 0), "ffn1_w_up": (0, FJ), "ffn1_w_down": (0, 2 * FJ), "w_in": (1, 0),
               "w_out": (2, 0), "ffn2_w_gate": (3, 0), "ffn2_w_up": (3, FJ), "ffn2_w_down": (3, 2 * FJ)}
    grad, delta, new_m, new_v = {}, {}, {}, {}
    for n in BIG:
        k, row0 = reduced[n]
        outs = _adamw_halves(_shard2d(n, w[n]), halves[k], others[k], _shard2d(n, mom[n]), _shard2d(n, var[n]),
                             c_idx, row0)
        grad[n], delta[n], new_m[n], new_v[n] = [_unshard2d(n, a) for a in outs]

    late = ["gla_w_a2", "swa_sinks"]
    direct = [n for n in SMALL if n not in late]
    names = direct + late
    gathered = _all_gather_devices([g[n] for n in names])
    mat = lambda a: a.reshape(a.shape[-2:])
    none2 = [None] * len(late)
    outs = _small_update(q_idx, gathered, [mat(w[n]) for n in direct] + none2, [mat(mom[n]) for n in direct] + none2,
                         [mat(var[n]) for n in direct] + none2, [n == "meta_tokens" for n in names])
    sum_a2, sum_sinks = outs[4 * len(direct):]
    g_late = [lax.dynamic_slice_in_dim(sum_a2, q_chip * (GLA_KW // N_CHIPS), GLA_KW // N_CHIPS, axis=1)[None],
              sum_sinks[:, 0].reshape(1, 1, SWA_QH)]
    outs = list(outs[:4 * len(direct)]) + list(_small_update(
        q_idx, g_late, [mat(w[n]) for n in late], [mat(mom[n]) for n in late], [mat(var[n]) for n in late],
        [False, False]))
    for k, n in enumerate(names):
        grad[n], delta[n], new_m[n], new_v[n] = [a.reshape(w[n].shape) for a in outs[4 * k:4 * k + 4]]

    return (loss, grad_x[None], *[grad[n] for n in WEIGHTS], *[delta[n] for n in WEIGHTS],
            *[new_m[n] for n in WEIGHTS], *[new_v[n] for n in WEIGHTS])
```

```python
import functools
import math

import numpy as np
import jax
import jax.numpy as jnp
from jax import lax
from jax.experimental import pallas as pl
from jax.experimental.pallas import tpu as pltpu

F32 = jnp.float32
BF16 = jnp.bfloat16
MESH = pl.DeviceIdType.MESH

D_MODEL = 1024
D_FF = 2816
N_CHIPS = 4
N_DEV = 8
N_META = 16
BLK = 128
PAD = BLK - N_META
GLA_CHUNK = 64
GLA_HEADS = 4
GLA_DV = 128
GLA_DK = 64
GLA_KW = GLA_HEADS * GLA_DK
GLA_W = GLA_HEADS * GLA_DV
GLA_RANK = 16
GLA_TAU = 16.0
SWA_HD = 64
SWA_QH = 8
SWA_KVH = 2
SWA_W = SWA_QH * SWA_HD
WINDOW = 128
ROPE_THETA = 10000.0
EPS = 1e-6
NEG_INF = -1e30
IN_SPLITS = (256, 256, 512, 512, 16, 512, 128, 128)
D_IN = sum(IN_SPLITS)
P_GQ, P_GK, P_GV, P_GG, P_GA, P_SQ, P_SK, P_SV, P_END = 0, 256, 512, 1024, 1536, 1664, 2176, 2432, 2688
ADAM_LR, ADAM_B1, ADAM_B2, ADAM_EPS, ADAM_WD, ADAM_STEP = 0.001, 0.9, 0.999, 1e-08, 0.01, 10
VMEM_LIMIT = 56 * 1024 * 1024

NT = (((1,), (1,)), ((), ()))
TN = (((0,), (0,)), ((), ()))


def _cparams(n_axes):
    return pltpu.CompilerParams(dimension_semantics=("arbitrary",) * n_axes, vmem_limit_bytes=VMEM_LIMIT)


def _row_tile(t):
    for tm in (640, 512, 384, 256, 128):
        if t % tm == 0:
            return tm
    raise ValueError(t)


SEQ_BLOCKS_PER_STEP = 5


def _seq_tile(t):
    return SEQ_BLOCKS_PER_STEP * BLK if t % (SEQ_BLOCKS_PER_STEP * BLK) == 0 else BLK


ROW_PARTS = 2


def _row_parts(tm):
    n = ROW_PARTS if tm % (16 * ROW_PARTS) == 0 else 1
    return [slice(k * (tm // n), (k + 1) * (tm // n)) for k in range(n)]


def _contract_tile(t):
    return 1664 if t % 1664 == 0 else _row_tile(t)


def _div_tile(r, cap=512):
    best = None
    for tr in range(8, min(r, cap) + 1, 8):
        if r % tr == 0:
            best = tr
    return best if best is not None else r


def _dot(a, b):
    return jnp.dot(a, b, preferred_element_type=F32)


def _dg(a, b, dims):
    return lax.dot_general(a, b, dims, preferred_element_type=F32)


def _rms(x, w):
    r = lax.rsqrt(jnp.mean(x * x, axis=-1, keepdims=True) + EPS)
    xh = x * r
    return xh * w, xh, r


def _rms_bwd(xh, r, w, dy):
    wdy = dy * w
    dx = r * (wdy - xh * jnp.mean(wdy * xh, axis=-1, keepdims=True))
    dw = jnp.sum(dy * xh, axis=0, keepdims=True)
    return dx, dw


def _sigmoid(x):
    return 1.0 / (1.0 + jnp.exp(-x))


def _full(shape):
    nd = len(shape)
    return pl.BlockSpec(shape, lambda *_: (0,) * nd)


ANY = pl.BlockSpec(memory_space=pl.ANY)


def _pallas(body, *, name, grid, in_specs, out_specs, out_shape, args, scratch_shapes=(), hook=None):
    n_axes = len(grid)
    if hook is None:
        return pl.pallas_call(body, name=name, grid=grid, in_specs=list(in_specs), out_specs=list(out_specs),
                              out_shape=list(out_shape), scratch_shapes=list(scratch_shapes),
                              compiler_params=_cparams(n_axes))(*args)
    n_in, n_out, n_scr = len(in_specs), len(out_specs), len(scratch_shapes)
    h_in, h_out = len(hook.inputs), len(hook.out_shape)
    total = math.prod(grid)

    def wrapped(*refs):
        ins, hins = refs[:n_in], refs[n_in:n_in + h_in]
        o0 = n_in + h_in
        outs, houts = refs[o0:o0 + n_out], refs[o0 + n_out:o0 + n_out + h_out]
        s0 = o0 + n_out + h_out
        scr, hscr = refs[s0:s0 + n_scr], refs[s0 + n_scr:]
        step = pl.program_id(0)
        for a in range(1, n_axes):
            step = step * grid[a] + pl.program_id(a)

        @pl.when(step == 0)
        def _():
            hook.start(hins, houts, hscr)

        body(*ins, *outs, *scr)

        if hook.has_mid:
            @pl.when(step == (3 * total) // 4)
            def _():
                hook.mid(hins, houts, hscr)

        @pl.when(step == total - 1)
        def _():
            hook.finish(hins, houts, hscr)

    res = pl.pallas_call(
        wrapped, name=name, grid=grid, in_specs=list(in_specs) + [ANY] * h_in,
        out_specs=list(out_specs) + [ANY] * h_out, out_shape=list(out_shape) + list(hook.out_shape),
        scratch_shapes=list(scratch_shapes) + list(hook.scratch), compiler_params=_cparams(n_axes),
        input_output_aliases={n_in + a: n_out + b for a, b in hook.aliases},
    )(*args, *hook.inputs)
    return res[:n_out], res[n_out:]


def _ffn_fwd(h, wpre, wg4, wu4, wd4, wpost, hook=None, target=None):
    t = h.shape[0]
    tm = _row_tile(t)
    nj, fj, _ = wg4.shape
    nblk = tm // BLK if target is not None else 0

    def body(*refs):
        h_ref, wpre_ref, wg_ref, wu_ref, wd_ref, wpost_ref = refs[:6]
        t_refs = refs[6:6 + nblk]
        hout_ref, n_ref, p1_ref, p2_ref, a_ref, f_ref = refs[6 + nblk:12 + nblk]
        acc_ref = refs[-1]
        i = pl.program_id(0)
        j = pl.program_id(1)

        @pl.when(j == 0)
        def _():
            y, _, _ = _rms(h_ref[...], wpre_ref[...])
            n_ref[...] = y.astype(BF16)
            acc_ref[...] = jnp.zeros_like(acc_ref)

        if target is not None:
            dwpost_ref, sse_ref = refs[12 + nblk:14 + nblk]

            @pl.when((i == 0) & (j == 0))
            def _():
                dwpost_ref[...] = jnp.zeros_like(dwpost_ref)
                sse_ref[...] = jnp.zeros_like(sse_ref)

        n = n_ref[...]
        g = _dg(n, wg_ref[...], NT)
        u = _dg(n, wu_ref[...], NT)
        sg = _sigmoid(g)
        silu = g * sg
        p1_ref[...] = (u * (sg + silu * (1.0 - sg))).astype(BF16)
        p2_ref[...] = silu.astype(BF16)
        a = (silu * u).astype(BF16)
        a_ref[...] = a
        acc_ref[...] += _dot(a, wd_ref[...])

        @pl.when(j == nj - 1)
        def _():
            f = acc_ref[...]
            wpost = wpost_ref[...]
            y, fh, r = _rms(f, wpost)
            hout = h_ref[...] + 0.5 * y
            if target is None:
                f_ref[...] = f
                hout_ref[...] = hout
            else:
                sse = jnp.zeros((1, 1), F32)
                errs = []
                for k in range(nblk):
                    err = hout[k * BLK:(k + 1) * BLK] - t_refs[k][...]
                    if k == 0:
                        err = jnp.where(i > 0, err, 0.0)
                    errs.append(err)
                    sse = sse + jnp.sum(jnp.sum(err * err, axis=1, keepdims=True), axis=0, keepdims=True)
                dy = (jnp.concatenate(errs, axis=0) if nblk > 1 else errs[0]) * (1.0 / D_MODEL)
                hout_ref[...] = dy
                df, dw = _rms_bwd(fh, r, wpost, 0.5 * dy)
                f_ref[...] = df.astype(BF16)
                dwpost_ref[...] += dw
                sse_ref[...] += jnp.broadcast_to(sse, sse_ref.shape)

    row = pl.BlockSpec((tm, D_MODEL), lambda i, j: (i, 0))
    vec = pl.BlockSpec((1, D_MODEL), lambda i, j: (0, 0))
    wrow = pl.BlockSpec((None, fj, D_MODEL), lambda i, j: (j, 0, 0))
    act = pl.BlockSpec((None, tm, fj), lambda i, j: (j, i, 0))
    t_specs = [pl.BlockSpec((BLK, D_MODEL), functools.partial(lambda i, j, k: (jnp.maximum(nblk * i + k - 1, 0), 0), k=k))
               for k in range(nblk)]
    loss_spec = [vec, _full((1, 128))] if target is not None else []
    loss_shape = [jax.ShapeDtypeStruct((1, D_MODEL), F32), jax.ShapeDtypeStruct((1, 128), F32)] if (
        target is not None) else []
    return _pallas(
        body, name="ffn_fwd", grid=(t // tm, nj),
        in_specs=[row, vec, wrow, wrow, wrow, vec] + t_specs,
        out_specs=[row, row, act, act, act, row] + loss_spec,
        out_shape=[jax.ShapeDtypeStruct((t, D_MODEL), F32), jax.ShapeDtypeStruct((t, D_MODEL), BF16),
                   jax.ShapeDtypeStruct((nj, t, fj), BF16), jax.ShapeDtypeStruct((nj, t, fj), BF16),
                   jax.ShapeDtypeStruct((nj, t, fj), BF16),
                   jax.ShapeDtypeStruct((t, D_MODEL), F32 if target is None else BF16)] + loss_shape,
        scratch_shapes=[pltpu.VMEM((tm, D_MODEL), F32)],
        args=(h, wpre, wg4, wu4, wd4, wpost) + (target,) * nblk, hook=hook)


def _ffn_bwd(dhout, h, f, p14, p24, wpre, wg4, wu4, wd4, wpost, hook=None, df=None):
    t = h.shape[0]
    tm = _row_tile(t)
    nj, fj, _ = wg4.shape
    have_df = df is not None

    def body(dhout_ref, h_ref, f_ref, p1_ref, p2_ref, wpre_ref, wg_ref, wu_ref, wd_ref, wpost_ref, *rest):
        if have_df:
            dh_ref, dg_ref, du_ref, dwpre_ref, dn_ref = rest
            df_ref = f_ref
        else:
            dh_ref, df_ref, dg_ref, du_ref, dwpre_ref, dwpost_ref, dn_ref = rest
        i = pl.program_id(0)
        j = pl.program_id(1)

        @pl.when((i == 0) & (j == 0))
        def _():
            dwpre_ref[...] = jnp.zeros_like(dwpre_ref)
            if not have_df:
                dwpost_ref[...] = jnp.zeros_like(dwpost_ref)

        @pl.when(j == 0)
        def _():
            if not have_df:
                wpost = wpost_ref[...]
                _, fh, r = _rms(f_ref[...], wpost)
                dfv, dw = _rms_bwd(fh, r, wpost, 0.5 * dhout_ref[...])
                dwpost_ref[...] += dw
                df_ref[...] = dfv.astype(BF16)
            dn_ref[...] = jnp.zeros_like(dn_ref)

        parts = _row_parts(tm)
        das = [_dg(df_ref[rows, :], wd_ref[...], NT) for rows in parts]
        for rows, da in zip(parts, das):
            dg = (da * p1_ref[rows, :].astype(F32)).astype(BF16)
            du = (da * p2_ref[rows, :].astype(F32)).astype(BF16)
            dg_ref[rows, :] = dg
            du_ref[rows, :] = du
            dn_ref[rows, :] += _dot(dg, wg_ref[...]) + _dot(du, wu_ref[...])

        @pl.when(j == nj - 1)
        def _():
            wpre = wpre_ref[...]
            _, hh, r = _rms(h_ref[...], wpre)
            dx, dw = _rms_bwd(hh, r, wpre, dn_ref[...])
            dwpre_ref[...] += dw
            dh_ref[...] = dhout_ref[...] + dx

    row = pl.BlockSpec((tm, D_MODEL), lambda i, j: (i, 0))
    vec = pl.BlockSpec((1, D_MODEL), lambda i, j: (0, 0))
    wrow = pl.BlockSpec((None, fj, D_MODEL), lambda i, j: (j, 0, 0))
    act = pl.BlockSpec((None, tm, fj), lambda i, j: (j, i, 0))
    actshape = jax.ShapeDtypeStruct((nj, t, fj), BF16)
    rowf, rowb, vecf = (jax.ShapeDtypeStruct((t, D_MODEL), F32), jax.ShapeDtypeStruct((t, D_MODEL), BF16),
                        jax.ShapeDtypeStruct((1, D_MODEL), F32))
    return _pallas(
        body, name="ffn_bwd", grid=(t // tm, nj),
        in_specs=[row, row, row, act, act, vec, wrow, wrow, wrow, vec],
        out_specs=[row, act, act, vec] if have_df else [row, row, act, act, vec, vec],
        out_shape=[rowf, actshape, actshape, vecf] if have_df else [rowf, rowb, actshape, actshape, vecf, vecf],
        scratch_shapes=[pltpu.VMEM((tm, D_MODEL), F32)],
        args=(dhout, h, df if have_df else f, p14, p24, wpre, wg4, wu4, wd4, wpost), hook=hook)


def _ffn_wgrad(n, df, dg4, du4, a4, hook=None):
    t = n.shape[0]
    tm = _contract_tile(t)
    ni = t // tm
    nj, _, fj = dg4.shape

    def body(n_ref, df_ref, dg_ref, du_ref, a_ref, dw_ref, acc):
        i = pl.program_id(1)

        @pl.when(i == 0)
        def _():
            acc[...] = jnp.zeros_like(acc)

        nn = n_ref[...]
        acc[0:fj, :] += _dg(dg_ref[...], nn, TN)
        acc[fj:2 * fj, :] += _dg(du_ref[...], nn, TN)
        acc[2 * fj:3 * fj, :] += _dg(a_ref[...], df_ref[...], TN)

        @pl.when(i == ni - 1)
        def _():
            dw_ref[...] = acc[...].astype(BF16)

    row = pl.BlockSpec((tm, D_MODEL), lambda j, i: (i, 0))
    act = pl.BlockSpec((None, tm, fj), lambda j, i: (j, i, 0))
    return _pallas(
        body, name="ffn_wgrad", grid=(nj, ni),
        in_specs=[row, row, act, act, act],
        out_specs=[pl.BlockSpec((None, 3 * fj, D_MODEL), lambda j, i: (j, 0, 0))],
        out_shape=[jax.ShapeDtypeStruct((nj, 3 * fj, D_MODEL), BF16)],
        scratch_shapes=[pltpu.VMEM((3 * fj, D_MODEL), F32)],
        args=(n, df, dg4, du4, a4), hook=hook)


def _embed_norm(x, meta, w):
    t = x.shape[0] + BLK
    tm = _row_tile(t)
    nblk = tm // BLK

    def body(*refs):
        x_refs = refs[:nblk]
        meta_ref, w_ref, h_ref, n_ref = refs[nblk:]
        i = pl.program_id(0)
        first = jnp.concatenate([jnp.zeros((PAD, D_MODEL), F32), meta_ref[...]], axis=0)
        blocks = [jnp.where(i == 0, first, x_refs[0][...])] + [r[...] for r in x_refs[1:]]
        h = jnp.concatenate(blocks, axis=0) if nblk > 1 else blocks[0]
        h_ref[...] = h
        y, _, _ = _rms(h, w_ref[...])
        n_ref[...] = y.astype(BF16)

    x_specs = [pl.BlockSpec((BLK, D_MODEL), functools.partial(lambda i, k: (jnp.maximum(nblk * i + k - 1, 0), 0), k=k))
               for k in range(nblk)]
    row = pl.BlockSpec((tm, D_MODEL), lambda i: (i, 0))
    return pl.pallas_call(
        body, name="embed_norm", grid=(t // tm,),
        in_specs=x_specs + [_full((N_META, D_MODEL)), _full((1, D_MODEL))], out_specs=[row, row],
        out_shape=[jax.ShapeDtypeStruct((t, D_MODEL), F32), jax.ShapeDtypeStruct((t, D_MODEL), BF16)],
        compiler_params=_cparams(1),
    )(*([x] * nblk), meta, w)


FWD_RELATION = (None, 0, 1, 2)


def _ffn_fwd_gather(h, n, wbufs, wpost, qc_idx, late):
    t = h.shape[0]
    tm = _row_tile(t)
    ni = t // tm
    nj, fj, _ = wbufs[0].shape
    assert nj == N_CHIPS and ni >= 4
    nw = len(wbufs)
    n_lin, n_lout = len(late.inputs), len(late.out_shape)
    wait_step = ni - 3

    def body(qc_ref, h_ref, n_ref, wpost_ref, *rest):
        wb_in = rest[:nw]
        lins = rest[nw:nw + n_lin]
        o0 = nw + n_lin
        hout_ref, p1_ref, p2_ref, a_ref, f_hbm = rest[o0:o0 + 5]
        wb = rest[o0 + 5:o0 + 5 + nw]
        louts = rest[o0 + 5 + nw:o0 + 5 + nw + n_lout]
        s0 = o0 + 5 + nw + n_lout
        wv, wsem, send, recv, fbuf, fr_sem, fw_sem = rest[s0:s0 + 7]
        lscr = rest[s0 + 7:]
        p = pl.program_id(0)
        i = pl.program_id(1)
        step = p * ni + i
        fslot = step % 3
        nslot = (step + 1) % 3

        def f_tile(tile):
            return f_hbm.at[pl.ds(pl.multiple_of(tile * tm, 8), tm)]

        @pl.when(step > 1)
        def _():
            pltpu.make_async_copy(fbuf.at[nslot], f_tile(i), fw_sem.at[nslot]).wait()

        nxt = step + 1

        @pl.when((nxt < N_CHIPS * ni) & (nxt >= ni))
        def _():
            pltpu.make_async_copy(f_tile(nxt % ni), fbuf.at[nslot], fr_sem.at[nslot]).start()

        @pl.when(p > 0)
        def _():
            pltpu.make_async_copy(f_tile(i), fbuf.at[fslot], fr_sem.at[fslot]).wait()
        x, y, c, chips = _place()
        q = 2 * x + y
        sibling = (x, y, 1 - c)
        mine, other = _half(fj, c), _half(fj, 1 - c)

        def load(chunk, slot, src):
            return [pltpu.make_async_copy(src[t].at[chunk], wv.at[slot, t], wsem.at[slot, t]) for t in range(nw)]

        @pl.when((p == 0) & (i == 0))
        def _():
            for j, (cx, cy) in enumerate(chips):
                for t in range(nw):
                    _remote(send.at[t, j], recv.at[t, j], wb_in[t].at[q, mine], wb[t].at[q, mine], (cx, cy, c)).start()
            for cp in load(q, 0, wb_in):
                cp.start()
            for cp in load(q, 0, wb_in):
                cp.wait()

        @pl.when((p == 1) & (i == 0))
        def _():
            late.start(lins, louts, lscr)

        for pp in range(1, N_CHIPS):
            j = FWD_RELATION[pp]
            cx, cy = chips[j]
            chunk = 2 * cx + cy

            @pl.when((p == pp - 1) & (i == wait_step))
            def _(j=j, cx=cx, cy=cy, chunk=chunk, pp=pp):
                for t in range(nw):
                    got = wb[t].at[chunk, mine]
                    _remote(send.at[t, j], recv.at[t, j], got, got, (cx, cy, c)).wait_recv()
                    _remote(send.at[t, 3 + j], recv.at[t, 3 + j], got, got, sibling).start()
                for t in range(nw):
                    rest_half = wb[t].at[chunk, other]
                    _remote(send.at[t, 3 + j], recv.at[t, 3 + j], rest_half, rest_half, sibling).wait_recv()
                for cp in load(chunk, pp % 2, wb):
                    cp.start()

            @pl.when((p == pp) & (i == 0))
            def _(chunk=chunk, pp=pp):
                for cp in load(chunk, pp % 2, wb):
                    cp.wait()

        @pl.when((p == N_CHIPS - 1) & (i == ni // 2))
        def _():
            late.mid(lins, louts, lscr)

        slot = p % 2
        nn = n_ref[...]
        g = _dg(nn, wv[slot, 0], NT)
        u = _dg(nn, wv[slot, 1], NT)
        sg = _sigmoid(g)
        silu = g * sg
        p1_ref[...] = (u * (sg + silu * (1.0 - sg))).astype(BF16)
        p2_ref[...] = silu.astype(BF16)
        a = (silu * u).astype(BF16)
        a_ref[...] = a
        part = _dot(a, wv[slot, 2])

        @pl.when(p == 0)
        def _():
            fbuf[fslot] = part

        @pl.when(p > 0)
        def _():
            fbuf[fslot] = fbuf[fslot] + part

        pltpu.make_async_copy(fbuf.at[fslot], f_tile(i), fw_sem.at[fslot]).start()

        @pl.when(p == N_CHIPS - 1)
        def _():
            yv, _, _ = _rms(fbuf[fslot], wpost_ref[...])
            hout_ref[...] = h_ref[...] + 0.5 * yv

        @pl.when((p == N_CHIPS - 1) & (i == ni - 1))
        def _():
            pslot = (step + 2) % 3
            pltpu.make_async_copy(fbuf.at[pslot], f_tile(i), fw_sem.at[pslot]).wait()
            pltpu.make_async_copy(fbuf.at[fslot], f_tile(i), fw_sem.at[fslot]).wait()
            for t in range(nw):
                for j, (cx, cy) in enumerate(chips):
                    sent = wb[t].at[2 * cx + cy, mine]
                    _remote(send.at[t, j], recv.at[t, j], sent, sent, (cx, cy, c)).wait_send()
                    _remote(send.at[t, 3 + j], recv.at[t, 3 + j], sent, sent, sibling).wait_send()
            late.finish(lins, louts, lscr)

    def last_pass_rows(p, i, qc_ref):
        return (jnp.where(p == N_CHIPS - 1, i, 0), 0)

    def chunk_rows(p, i, qc_ref):
        order = ((p & 1) << 1) | (p >> 1)
        return (jnp.bitwise_xor(qc_ref[0], order), i, 0)

    row = pl.BlockSpec((tm, D_MODEL), lambda p, i, qc_ref: (i, 0))
    last_row = pl.BlockSpec((tm, D_MODEL), last_pass_rows)
    act = pl.BlockSpec((None, tm, fj), chunk_rows)
    act_shape = jax.ShapeDtypeStruct((nj, t, fj), BF16)
    res = pl.pallas_call(
        body, name="ffn_fwd_gather",
        grid_spec=pltpu.PrefetchScalarGridSpec(
            num_scalar_prefetch=1, grid=(N_CHIPS, ni),
            in_specs=[last_row, row, pl.BlockSpec((1, D_MODEL), lambda p, i, qc_ref: (0, 0))]
            + [ANY] * (nw + n_lin),
            out_specs=[last_row, act, act, act, ANY] + [ANY] * (nw + n_lout),
            scratch_shapes=[pltpu.VMEM((2, nw, fj, D_MODEL), BF16), pltpu.SemaphoreType.DMA((2, nw)),
                            pltpu.SemaphoreType.DMA((nw, 6)), pltpu.SemaphoreType.DMA((nw, 6)),
                            pltpu.VMEM((3, tm, D_MODEL), F32), pltpu.SemaphoreType.DMA((3,)),
                            pltpu.SemaphoreType.DMA((3,))] + list(late.scratch)),
        out_shape=[jax.ShapeDtypeStruct((t, D_MODEL), F32), act_shape, act_shape, act_shape,
                   jax.ShapeDtypeStruct((t, D_MODEL), F32)]
        + [jax.ShapeDtypeStruct(b.shape, b.dtype) for b in wbufs] + list(late.out_shape),
        input_output_aliases={**{4 + t: 5 + t for t in range(nw)},
                              **{4 + nw + a: 5 + nw + b for a, b in late.aliases}},
        compiler_params=_cparams(2),
    )(qc_idx, h, n, wpost, *wbufs, *late.inputs)
    return res[:5], res[5:5 + nw], res[5 + nw:]


PASS_RELATION = (2, 0, 1)


def _ffn_wgrad_reduce(n, df, dg4, du4, a4, qc_idx, hook):
    t = n.shape[0]
    tm = _contract_tile(t)
    ni = t // tm
    nj, _, fj = dg4.shape
    assert nj == N_CHIPS
    hrows = 3 * fj // 2
    n_hin, n_hout = len(hook.inputs), len(hook.out_shape)

    def body(qc_ref, n_ref, df_ref, dg_ref, du_ref, a_ref, *rest):
        hins = rest[:n_hin]
        own_ref, others_ref = rest[n_hin:n_hin + 2]
        houts = rest[n_hin + 2:n_hin + 2 + n_hout]
        s0 = n_hin + 2 + n_hout
        acc, stage, land, sumbuf, px_send, px_recv, cs_send, cs_recv, own_sem = rest[s0:s0 + 9]
        hscr = rest[s0 + 9:]
        k_pass = pl.program_id(0)
        i = pl.program_id(1)
        x, y, c, chips = _place()
        mine = pl.ds(pl.multiple_of(c * hrows, 8), hrows)
        other = pl.ds(pl.multiple_of((1 - c) * hrows, 8), hrows)

        def to_owner(k):
            j = PASS_RELATION[k]
            return _remote(cs_send.at[j], cs_recv.at[j], sumbuf.at[k % 2], others_ref.at[j], (*chips[j], c))

        @pl.when((k_pass == 0) & (i == 0))
        def _():
            hook.start(hins, houts, hscr)

        @pl.when(i == 0)
        def _():
            acc[...] = jnp.zeros_like(acc)

        nn = n_ref[...]
        acc[0:fj, :] += _dg(dg_ref[...], nn, TN)
        acc[fj:2 * fj, :] += _dg(du_ref[...], nn, TN)
        acc[2 * fj:3 * fj, :] += _dg(a_ref[...], df_ref[...], TN)

        for k in range(N_CHIPS):
            @pl.when((k_pass == k) & (i == ni - 1))
            def _(k=k):
                slot = k % 2
                stage[...] = acc[other, :].astype(BF16)
                swap = _remote(px_send.at[k], px_recv.at[k], stage, land.at[slot], (x, y, 1 - c))
                swap.start()
                swap.wait_recv()
                pair = acc[mine, :] + land[slot].astype(F32)
                if k >= 2:
                    to_owner(k - 2).wait_send()
                sumbuf[slot] = pair.astype(BF16)
                swap.wait_send()
                if k < N_CHIPS - 1:
                    to_owner(k).start()
                else:
                    keep = pltpu.make_async_copy(sumbuf.at[slot], own_ref, own_sem)
                    keep.start()
                    for j in range(N_CHIPS - 1):
                        _remote(cs_send.at[j], cs_recv.at[j], sumbuf.at[0], others_ref.at[j], (*chips[j], c)).wait_recv()
                    to_owner(k - 1).wait_send()
                    keep.wait()
                    hook.finish(hins, houts, hscr)

    def chunk(k_pass, i, qc_ref):
        return (jnp.bitwise_xor(qc_ref[0], N_CHIPS - 1 - k_pass), i, 0)

    row = pl.BlockSpec((tm, D_MODEL), lambda k_pass, i, qc_ref: (i, 0))
    act = pl.BlockSpec((None, tm, fj), chunk)
    res = pl.pallas_call(
        body, name="ffn_wgrad_reduce",
        grid_spec=pltpu.PrefetchScalarGridSpec(
            num_scalar_prefetch=1, grid=(N_CHIPS, ni),
            in_specs=[row, row, act, act, act] + [ANY] * n_hin,
            out_specs=[ANY, ANY] + [ANY] * n_hout,
            scratch_shapes=[pltpu.VMEM((3 * fj, D_MODEL), F32), pltpu.VMEM((hrows, D_MODEL), BF16),
                            pltpu.VMEM((2, hrows, D_MODEL), BF16), pltpu.VMEM((2, hrows, D_MODEL), BF16),
                            pltpu.SemaphoreType.DMA((N_CHIPS,)), pltpu.SemaphoreType.DMA((N_CHIPS,)),
                            pltpu.SemaphoreType.DMA((N_CHIPS - 1,)), pltpu.SemaphoreType.DMA((N_CHIPS - 1,)),
                            pltpu.SemaphoreType.DMA] + list(hook.scratch)),
        out_shape=[jax.ShapeDtypeStruct((hrows, D_MODEL), BF16),
                   jax.ShapeDtypeStruct((N_CHIPS - 1, hrows, D_MODEL), BF16)] + list(hook.out_shape),
        compiler_params=_cparams(2),
    )(qc_idx, n, df, dg4, du4, a4, *hook.inputs)
    return res[0], res[1], res[2:]


def _xty(x, y):
    t, k = x.shape
    n = y.shape[1]
    tm = _contract_tile(t)
    tn = n if n <= 1024 else (896 if n % 896 == 0 else 128)

    def body(x_ref, y_ref, o_ref):
        @pl.when(pl.program_id(1) == 0)
        def _():
            o_ref[...] = jnp.zeros_like(o_ref)

        o_ref[...] += _dg(x_ref[...], y_ref[...], TN)

    return pl.pallas_call(
        body, name="xty", grid=(n // tn, t // tm),
        in_specs=[pl.BlockSpec((tm, k), lambda j, i: (i, 0)), pl.BlockSpec((tm, tn), lambda j, i: (i, j))],
        out_specs=pl.BlockSpec((k, tn), lambda j, i: (0, j)),
        out_shape=jax.ShapeDtypeStruct((k, n), F32),
        compiler_params=_cparams(2),
    )(x, y)


def _rope_tables(t):
    pos = (jnp.arange(t, dtype=jnp.int32) - PAD).astype(F32)
    inv_freq = 1.0 / (ROPE_THETA ** (jnp.arange(0, SWA_HD, 2, dtype=F32) / SWA_HD))
    ang = pos[:, None] * inv_freq[None, :]
    cos = jnp.cos(ang)
    sin = jnp.sin(ang)
    return jnp.concatenate([cos, cos, cos, cos], axis=1), jnp.concatenate([-sin, sin, -sin, sin], axis=1)


def _rot_half(x, first_half):
    return jnp.where(first_half, pltpu.roll(x, 96, 1), pltpu.roll(x, 32, 1))


def _first_half_mask(rows):
    lane = lax.broadcasted_iota(jnp.int32, (rows, 128), 1)
    return (lane % 64) < 32


def _log_sigmoid(z):
    return jnp.minimum(z, 0.0) - jnp.log(1.0 + jnp.exp(-jnp.abs(z)))


def _mix_proj(h1, wmixpre, winp, wa2p, bap, cos, sin):
    t = h1.shape[0]
    tm = _row_tile(t)

    def body(h_ref, w_ref, win_ref, wa2_ref, ba_ref, cos_ref, sin_ref,
             n_ref, gq_ref, gk_ref, gv_ref, gg_ref, ga_ref, la_ref, sq_ref, sk_ref, sv_ref):
        y, _, _ = _rms(h_ref[...], w_ref[...])
        n = y.astype(BF16)
        n_ref[...] = n
        proj = _dot(n, win_ref[...])
        gq_ref[...] = proj[:, P_GQ:P_GK]
        gk_ref[...] = proj[:, P_GK:P_GV]
        gv_ref[...] = proj[:, P_GV:P_GG]
        gg_ref[...] = proj[:, P_GG:P_GA]
        ga = proj[:, P_GA:P_SQ]
        ga_ref[...] = ga
        z = _dot(ga.astype(BF16), wa2_ref[...]) + ba_ref[...]
        la_ref[...] = _log_sigmoid(z) * (1.0 / GLA_TAU)
        c = cos_ref[...]
        s = sin_ref[...]
        fh = _first_half_mask(tm)
        for k in range(4):
            x = proj[:, P_SQ + 128 * k:P_SQ + 128 * (k + 1)]
            sq_ref[:, 128 * k:128 * (k + 1)] = (x * c + _rot_half(x, fh) * s).astype(BF16)
        for k in range(2):
            x = proj[:, P_SK + 128 * k:P_SK + 128 * (k + 1)]
            sk_ref[:, 128 * k:128 * (k + 1)] = (x * c + _rot_half(x, fh) * s).astype(BF16)
        sv_ref[...] = proj[:, P_SV:P_END].astype(BF16)

    def row(w):
        return pl.BlockSpec((tm, w), lambda i: (i, 0))

    def rshape(w, dt):
        return jax.ShapeDtypeStruct((t, w), dt)

    return pl.pallas_call(
        body, name="mix_proj", grid=(t // tm,),
        in_specs=[row(D_MODEL), _full((1, D_MODEL)), _full((D_MODEL, P_END)), _full((128, GLA_KW)),
                  _full((1, GLA_KW)), row(128), row(128)],
        out_specs=[row(D_MODEL), row(256), row(256), row(512), row(512), row(128), row(256), row(512), row(256),
                   row(256)],
        out_shape=[rshape(D_MODEL, BF16), rshape(256, F32), rshape(256, F32), rshape(512, F32), rshape(512, F32),
                   rshape(128, F32), rshape(256, F32), rshape(512, BF16), rshape(256, BF16), rshape(256, BF16)],
        compiler_params=_cparams(1),
    )(h1, wmixpre, winp, wa2p, bap, cos, sin)


def _scan_rows(x, reverse=False):
    n = x.shape[0]
    row = lax.broadcasted_iota(jnp.int32, x.shape, 0)
    s = 1
    while s < n:
        if reverse:
            x = x + jnp.where(row < n - s, pltpu.roll(x, n - s, 0), 0.0)
        else:
            x = x + jnp.where(row >= s, pltpu.roll(x, s, 0), 0.0)
        s *= 2
    return x


def _gla_cumsum(la, tril_f):
    b = _scan_rows(la)
    row = lax.broadcasted_iota(jnp.int32, b.shape, 0)
    bm = jnp.sum(jnp.where(row == GLA_CHUNK // 2 - 1, b, 0.0), axis=0, keepdims=True)
    bl = jnp.sum(jnp.where(row == GLA_CHUNK - 1, b, 0.0), axis=0, keepdims=True)
    return b, bm, bl


def _gla_decays(la, tril_f):
    b, bm, bl = _gla_cumsum(la, tril_f)
    return jnp.exp(b - bm), jnp.exp(bm - b), jnp.exp(b), jnp.exp(bl - b), jnp.exp(bl)


def _gla_masks():
    c = GLA_CHUNK
    r = lax.broadcasted_iota(jnp.int32, (c, c), 0)
    col = lax.broadcasted_iota(jnp.int32, (c, c), 1)
    r4 = lax.broadcasted_iota(jnp.int32, (GLA_HEADS * c, c), 0) % c
    c4 = lax.broadcasted_iota(jnp.int32, (GLA_HEADS * c, c), 1)
    klane = lax.broadcasted_iota(jnp.int32, (c, GLA_KW), 1) // GLA_DK
    vlane = lax.broadcasted_iota(jnp.int32, (c, GLA_W), 1) // GLA_DV
    srow = lax.broadcasted_iota(jnp.int32, (GLA_W, GLA_KW), 0) // GLA_DV
    scol = lax.broadcasted_iota(jnp.int32, (GLA_W, GLA_KW), 1) // GLA_DK
    return dict(tril_f=(r >= col).astype(F32), triu_f=(r <= col).astype(F32), tril4=r4 >= c4,
                khead=[klane == h for h in range(GLA_HEADS)], vhead=[vlane == h for h in range(GLA_HEADS)],
                diag=srow == scol)


def _stack_heads(x, head_masks):
    return jnp.concatenate([jnp.where(m, x, 0.0) for m in head_masks], axis=0)


def _gla_fwd(gq, gk, gv, la):
    t = gq.shape[0]
    rg = _seq_tile(t)
    nb = t // rg
    ncb = rg // GLA_CHUNK
    c = GLA_CHUNK

    def body(q_ref, k_ref, v_ref, la_ref, o_ref, ss_ref, st_ref):
        @pl.when(pl.program_id(0) == 0)
        def _():
            st_ref[...] = jnp.zeros_like(st_ref)

        mk = _gla_masks()
        st = st_ref[...]
        for ch in range(ncb):
            rows = slice(ch * c, (ch + 1) * c)
            eq, ek, eb, ekl, ebl = _gla_decays(la_ref[rows, :], mk["tril_f"])
            qs = q_ref[rows, :] * (GLA_DK ** -0.5)
            k = k_ref[rows, :]
            v = v_ref[rows, :].astype(BF16)
            ss_ref[ch] = st
            q4 = _stack_heads(qs * eq, mk["khead"]).astype(BF16)
            a4 = jnp.where(mk["tril4"], _dg(q4, (k * ek).astype(BF16), NT), 0.0).astype(BF16)
            r4 = _dot(a4, v)
            intra = jnp.concatenate([r4[h * c:(h + 1) * c, GLA_DV * h:GLA_DV * (h + 1)] for h in range(GLA_HEADS)],
                                    axis=1)
            o_ref[rows, :] = intra + _dg((qs * eb).astype(BF16), st.astype(BF16), NT)
            st = st * ebl + jnp.where(mk["diag"], _dg(v, (k * ekl).astype(BF16), TN), 0.0)
        st_ref[...] = st

    def row(w):
        return pl.BlockSpec((rg, w), lambda i: (i, 0))

    return pl.pallas_call(
        body, name="gla_fwd", grid=(nb,),
        in_specs=[row(256), row(256), row(512), row(256)],
        out_specs=[row(512), pl.BlockSpec((ncb, GLA_W, GLA_KW), lambda i: (i, 0, 0))],
        out_shape=[jax.ShapeDtypeStruct((t, GLA_W), F32), jax.ShapeDtypeStruct((nb * ncb, GLA_W, GLA_KW), F32)],
        scratch_shapes=[pltpu.VMEM((GLA_W, GLA_KW), F32)],
        compiler_params=_cparams(1),
    )(gq, gk, gv, la)


def _gla_bwd(gq, gk, gv, la, ss, do):
    t = gq.shape[0]
    rg = _seq_tile(t)
    nb = t // rg
    ncb = rg // GLA_CHUNK
    c = GLA_CHUNK

    def body(q_ref, k_ref, v_ref, la_ref, ss_ref, do_ref, dq_ref, dk_ref, dv_ref, dla_ref, dst_ref):
        @pl.when(pl.program_id(0) == 0)
        def _():
            dst_ref[...] = jnp.zeros_like(dst_ref)

        mk = _gla_masks()
        last_row = lax.broadcasted_iota(jnp.int32, (c, GLA_KW), 0) == c - 1
        scale = GLA_DK ** -0.5
        dstn = dst_ref[...]
        for ch in reversed(range(ncb)):
            rows = slice(ch * c, (ch + 1) * c)
            eq, ek, eb, ekl, ebl = _gla_decays(la_ref[rows, :], mk["tril_f"])
            qs = q_ref[rows, :] * scale
            k = k_ref[rows, :]
            qt, kt, qh, kh = qs * eq, k * ek, qs * eb, k * ekl
            ktb, khb, qhb = kt.astype(BF16), kh.astype(BF16), qh.astype(BF16)
            v = v_ref[rows, :].astype(BF16)
            do_f = do_ref[rows, :]
            dob = do_f.astype(BF16)
            st = ss_ref[ch]
            stb = st.astype(BF16)
            dstb = dstn.astype(BF16)
            q4 = _stack_heads(qt, mk["khead"]).astype(BF16)
            do4 = _stack_heads(do_f, mk["vhead"]).astype(BF16)
            a4 = jnp.where(mk["tril4"], _dg(q4, ktb, NT), 0.0).astype(BF16)
            da4 = jnp.where(mk["tril4"], _dg(do4, v, NT), 0.0).astype(BF16)
            dv_ref[rows, :] = _dg(a4, do4, TN) + _dg(khb, dstb, NT)
            dq4 = _dot(da4, ktb)
            dqt = jnp.zeros((c, GLA_KW), F32)
            for h in range(GLA_HEADS):
                dqt = dqt + jnp.where(mk["khead"][h], dq4[h * c:(h + 1) * c], 0.0)
            dkt = _dg(da4, q4, TN)
            dqh = _dot(dob, stb)
            dkh = _dot(v, dstb)
            dbl = jnp.sum(dstn * st, axis=0, keepdims=True)
            dstn = dstn * ebl + jnp.where(mk["diag"], _dg(dob, qhb, TN), 0.0)
            dq_ref[rows, :] = scale * (dqt * eq + dqh * eb)
            dk_ref[rows, :] = dkt * ek + dkh * ekl
            dkk = dkh * kh
            db = dqt * qt - dkt * kt + dqh * qh - dkk
            db = db + jnp.where(last_row, jnp.sum(dkk, axis=0, keepdims=True) + ebl * dbl, 0.0)
            dla_ref[rows, :] = _scan_rows(db, reverse=True)
        dst_ref[...] = dstn

    def row(w):
        return pl.BlockSpec((rg, w), lambda i: (nb - 1 - i, 0))

    def rshape(w):
        return jax.ShapeDtypeStruct((t, w), F32)

    return pl.pallas_call(
        body, name="gla_bwd", grid=(nb,),
        in_specs=[row(256), row(256), row(512), row(256),
                  pl.BlockSpec((ncb, GLA_W, GLA_KW), lambda i: (nb - 1 - i, 0, 0)), row(512)],
        out_specs=[row(256), row(256), row(512), row(256)],
        out_shape=[rshape(256), rshape(256), rshape(512), rshape(256)],
        scratch_shapes=[pltpu.VMEM((GLA_W, GLA_KW), F32)],
        compiler_params=_cparams(1),
    )(gq, gk, gv, la, ss, do)


SWA_G = SWA_QH // SWA_KVH


def _swa_bias():
    n = jnp.arange(3, dtype=jnp.int32)[:, None, None]
    r = (jnp.arange(SWA_G * BLK, dtype=jnp.int32) % BLK)[None, :, None]
    c = jnp.arange(3 * BLK, dtype=jnp.int32)[None, None, :]
    seg = c // BLK
    cc = c % BLK
    qpos = n * BLK + r - PAD
    kpos = jnp.where(seg == 0, (n - 1) * BLK, jnp.where(seg == 1, n * BLK, 0)) + cc - PAD
    band = (seg < 2) & (kpos >= N_META) & (kpos <= qpos) & (qpos - kpos < WINDOW)
    meta = (seg == 2) & (kpos >= 0) & (kpos < N_META) & (kpos <= qpos)
    return jnp.where(band | meta, 0.0, NEG_INF).astype(F32)


def _swa_stack(ref, rows, kh, lo, dtype):
    parts = []
    for g in range(2):
        pair = ref[rows, 128 * (2 * kh + g):128 * (2 * kh + g + 1)]
        zero = jnp.zeros_like(pair)
        parts += [jnp.where(lo, pair, zero), jnp.where(lo, zero, pair)]
    return jnp.concatenate(parts, axis=0).astype(dtype)


def _swa_unstack(x4, lo):
    return [jnp.where(lo, x4[2 * g * BLK:(2 * g + 1) * BLK], x4[(2 * g + 1) * BLK:(2 * g + 2) * BLK])
            for g in range(2)]


def _swa_sink_col(sink_ref, kh):
    blk = lax.broadcasted_iota(jnp.int32, (SWA_G * BLK, 1), 0) // BLK
    col = jnp.full((SWA_G * BLK, 1), sink_ref[SWA_G * kh + SWA_G - 1], F32)
    for e in reversed(range(SWA_G - 1)):
        col = jnp.where(blk == e, sink_ref[SWA_G * kh + e], col)
    return col


def _swa_softmax(qk, bias, sink):
    s = qk * (SWA_HD ** -0.5) + bias
    m = jnp.maximum(jnp.max(s, axis=-1, keepdims=True), sink)
    p = jnp.exp(s - m)
    es = jnp.exp(sink - m)
    inv = 1.0 / (jnp.sum(p, axis=-1, keepdims=True) + es)
    return p * inv, es * inv


def _swa_keys(prev_ref, cur_ref, first_ref, b, ls):
    before = prev_ref[:, ls] if b == 0 else cur_ref[(b - 1) * BLK:b * BLK, ls]
    return jnp.concatenate([before, cur_ref[b * BLK:(b + 1) * BLK, ls], first_ref[:, ls]], axis=0)


def _swa_specs(rs, ns):
    bps = rs // BLK
    cur = lambda w: pl.BlockSpec((rs, w), lambda i: (jnp.minimum(i, ns - 1), 0))
    prev = lambda w: pl.BlockSpec((BLK, w), lambda i: (jnp.maximum(jnp.minimum(i, ns - 1) * bps - 1, 0), 0))
    first = lambda w: pl.BlockSpec((BLK, w), lambda i: (0, 0))
    return cur, prev, first


def _swa_fwd(sinks, sq, sk, sv):
    t = sq.shape[0]
    rs = _seq_tile(t)
    bps, ns = rs // BLK, t // rs

    def body(sink_ref, bias_ref, q_ref, kp_ref, kc_ref, km_ref, vp_ref, vc_ref, vm_ref, o_ref):
        i = pl.program_id(0)
        lo = lax.broadcasted_iota(jnp.int32, (BLK, 128), 1) < 64
        sink_cols = [_swa_sink_col(sink_ref, kh) for kh in range(SWA_KVH)]
        chains = [(b, kh) for b in range(bps) for kh in range(SWA_KVH)]
        scores = []
        for b, kh in chains:
            ls = slice(128 * kh, 128 * (kh + 1))
            q4 = _swa_stack(q_ref, slice(b * BLK, (b + 1) * BLK), kh, lo, BF16)
            scores.append(_dg(q4, _swa_keys(kp_ref, kc_ref, km_ref, b, ls), NT))
        probs = []
        for (b, kh), s in zip(chains, scores):
            p, _ = _swa_softmax(s, bias_ref[jnp.minimum(i * bps + b, 2)], sink_cols[kh])
            probs.append(p.astype(BF16))
        for (b, kh), p in zip(chains, probs):
            ls = slice(128 * kh, 128 * (kh + 1))
            rows = slice(b * BLK, (b + 1) * BLK)
            for g, pair in enumerate(_swa_unstack(_dot(p, _swa_keys(vp_ref, vc_ref, vm_ref, b, ls)), lo)):
                o_ref[rows, 128 * (2 * kh + g):128 * (2 * kh + g + 1)] = pair

    cur, prev, first = _swa_specs(rs, ns)
    bias = _swa_bias()
    return pl.pallas_call(
        body, name="swa_fwd", grid=(ns,),
        in_specs=[pl.BlockSpec(memory_space=pltpu.SMEM), _full(bias.shape), cur(512), prev(256), cur(256), first(256),
                  prev(256), cur(256), first(256)],
        out_specs=cur(512),
        out_shape=jax.ShapeDtypeStruct((t, SWA_W), F32),
        compiler_params=_cparams(1),
    )(sinks, bias, sq, sk, sk, sk, sv, sv, sv)


def _swa_bwd(sinks, sq, sk, sv, o, do, hook=None):
    t = sq.shape[0]
    rs = _seq_tile(t)
    bps, ns = rs // BLK, t // rs

    def body(sink_ref, bias_ref, q_ref, kp_ref, kc_ref, km_ref, vp_ref, vc_ref, vm_ref, o_ref, do_ref,
             dq_ref, dk_ref, dv_ref, dkm_ref, dvm_ref, dsink_ref, pk_ref, pv_ref):
        i = pl.program_id(0)

        @pl.when(i == 0)
        def _():
            pk_ref[...] = jnp.zeros_like(pk_ref)
            pv_ref[...] = jnp.zeros_like(pv_ref)
            dkm_ref[...] = jnp.zeros_like(dkm_ref)
            dvm_ref[...] = jnp.zeros_like(dvm_ref)
            dsink_ref[...] = jnp.zeros_like(dsink_ref)

        @pl.when(i == ns)
        def _():
            dk_ref[...] = pk_ref[...]
            dv_ref[...] = pv_ref[...]

        @pl.when(i < ns)
        def _():
            lo = lax.broadcasted_iota(jnp.int32, (BLK, 128), 1) < 64
            scale = SWA_HD ** -0.5
            sink_cols = [_swa_sink_col(sink_ref, kh) for kh in range(SWA_KVH)]
            parts_k = [[None] * SWA_KVH for _ in range(bps)]
            parts_v = [[None] * SWA_KVH for _ in range(bps)]
            dsinks = [jnp.zeros((1, 1), F32) for _ in range(SWA_QH)]
            chains = [(b, kh) for b in range(bps) for kh in range(SWA_KVH)]
            lanes = lambda kh: slice(128 * kh, 128 * (kh + 1))
            block = lambda b: slice(b * BLK, (b + 1) * BLK)
            q4s = [_swa_stack(q_ref, block(b), kh, lo, BF16) for b, kh in chains]
            scores = [_dg(q4, _swa_keys(kp_ref, kc_ref, km_ref, b, lanes(kh)), NT)
                      for (b, kh), q4 in zip(chains, q4s)]
            do4s = [_swa_stack(do_ref, block(b), kh, lo, F32) for b, kh in chains]
            do4bs = [d.astype(BF16) for d in do4s]
            dps = [_dg(d, _swa_keys(vp_ref, vc_ref, vm_ref, b, lanes(kh)), NT) for (b, kh), d in zip(chains, do4bs)]
            pbs, dss = [], []
            for n_chain, (b, kh) in enumerate(chains):
                p, psink = _swa_softmax(scores[n_chain], bias_ref[jnp.minimum(i * bps + b, 2)], sink_cols[kh])
                delta = jnp.sum(do4s[n_chain] * _swa_stack(o_ref, block(b), kh, lo, F32), axis=-1, keepdims=True)
                dss.append((p * (dps[n_chain] - delta) * scale).astype(BF16))
                pbs.append(p.astype(BF16))
                dsk = psink * delta
                for e in range(SWA_G):
                    h = SWA_G * kh + e
                    dsinks[h] = dsinks[h] - jnp.sum(dsk[e * BLK:(e + 1) * BLK], axis=0, keepdims=True)
            for n_chain, (b, kh) in enumerate(chains):
                kall = _swa_keys(kp_ref, kc_ref, km_ref, b, lanes(kh))
                for g, pair in enumerate(_swa_unstack(_dot(dss[n_chain], kall), lo)):
                    dq_ref[block(b), 128 * (2 * kh + g):128 * (2 * kh + g + 1)] = pair
                parts_k[b][kh] = _dg(dss[n_chain], q4s[n_chain], TN)
                parts_v[b][kh] = _dg(pbs[n_chain], do4bs[n_chain], TN)
            last = slice(rs - BLK, rs)
            for parts, out_ref, pend_ref, meta_ref in ((parts_k, dk_ref, pk_ref, dkm_ref),
                                                       (parts_v, dv_ref, pv_ref, dvm_ref)):
                for kh in range(SWA_KVH):
                    ls = slice(128 * kh, 128 * (kh + 1))
                    if bps > 1:
                        out_ref[0:rs - BLK, ls] = pend_ref[0:rs - BLK, ls]
                    out_ref[last, ls] = pend_ref[last, ls] + parts[0][kh][0:BLK]
                    meta = parts[0][kh][2 * BLK:3 * BLK]
                    for b in range(bps):
                        own = parts[b][kh][BLK:2 * BLK]
                        if b + 1 < bps:
                            own = own + parts[b + 1][kh][0:BLK]
                            meta = meta + parts[b + 1][kh][2 * BLK:3 * BLK]
                        pend_ref[b * BLK:(b + 1) * BLK, ls] = own
                    meta_ref[:, ls] += meta
            for h in range(SWA_QH):
                dsink_ref[h:h + 1, :] += jnp.broadcast_to(dsinks[h], (1, 128))

    cur, prev, first = _swa_specs(rs, ns)
    late = lambda w: pl.BlockSpec((rs, w), lambda i: (jnp.maximum(i - 1, 0), 0))
    bias = _swa_bias()
    return _pallas(
        body, name="swa_bwd", grid=(ns + 1,),
        in_specs=[pl.BlockSpec(memory_space=pltpu.SMEM), _full(bias.shape), cur(512), prev(256), cur(256), first(256),
                  prev(256), cur(256), first(256), cur(512), cur(512)],
        out_specs=[cur(512), late(256), late(256), first(256), first(256), _full((SWA_QH, 128))],
        out_shape=[jax.ShapeDtypeStruct((t, SWA_W), F32), jax.ShapeDtypeStruct((t, 256), F32),
                   jax.ShapeDtypeStruct((t, 256), F32), jax.ShapeDtypeStruct((BLK, 256), F32),
                   jax.ShapeDtypeStruct((BLK, 256), F32), jax.ShapeDtypeStruct((SWA_QH, 128), F32)],
        scratch_shapes=[pltpu.VMEM((rs, 256), F32), pltpu.VMEM((rs, 256), F32)],
        args=(sinks, bias, sq, sk, sk, sk, sv, sv, sv, o, do), hook=hook)


def _mix_out(h1, ogla, gg, oswa, wgn, wsn, wout, wpost):
    t = h1.shape[0]
    tm = _row_tile(t)

    def body(h_ref, og_ref, gg_ref, os_ref, wgn_ref, wsn_ref, wout_ref, wpost_ref, h2_ref, cat_ref, m_ref):
        parts = []
        for h in range(GLA_HEADS):
            ls = slice(GLA_DV * h, GLA_DV * (h + 1))
            y, _, _ = _rms(og_ref[:, ls], wgn_ref[...])
            g = gg_ref[:, ls]
            parts.append(y * (g * _sigmoid(g)))
        ys, _, _ = _rms(os_ref[...], wsn_ref[...])
        cat = jnp.concatenate(parts + [ys], axis=1).astype(BF16)
        cat_ref[...] = cat
        m = _dot(cat, wout_ref[...])
        m_ref[...] = m
        y, _, _ = _rms(m, wpost_ref[...])
        h2_ref[...] = h_ref[...] + y

    def row(w):
        return pl.BlockSpec((tm, w), lambda i: (i, 0))

    return pl.pallas_call(
        body, name="mix_out", grid=(t // tm,),
        in_specs=[row(D_MODEL), row(512), row(512), row(512), _full((1, GLA_DV)), _full((1, SWA_W)),
                  _full((D_MODEL, D_MODEL)), _full((1, D_MODEL))],
        out_specs=[row(D_MODEL), row(D_MODEL), row(D_MODEL)],
        out_shape=[jax.ShapeDtypeStruct((t, D_MODEL), F32), jax.ShapeDtypeStruct((t, D_MODEL), BF16),
                   jax.ShapeDtypeStruct((t, D_MODEL), F32)],
        compiler_params=_cparams(1),
    )(h1, ogla, gg, oswa, wgn, wsn, wout, wpost)


def _mix_out_bwd(dh2, m, ogla, gg, oswa, wgn, wsn, wout, wpost, hook=None):
    t = dh2.shape[0]
    tm = _row_tile(t)

    def body(dh_ref, m_ref, og_ref, gg_ref, os_ref, wgn_ref, wsn_ref, wout_ref, wpost_ref,
             dog_ref, dgg_ref, dos_ref, dm_ref, dwpost_ref, dwgn_ref, dwsn_ref):
        @pl.when(pl.program_id(0) == 0)
        def _():
            dwpost_ref[...] = jnp.zeros_like(dwpost_ref)
            dwgn_ref[...] = jnp.zeros_like(dwgn_ref)
            dwsn_ref[...] = jnp.zeros_like(dwsn_ref)

        wpost = wpost_ref[...]
        _, mh, r = _rms(m_ref[...], wpost)
        dm, dw = _rms_bwd(mh, r, wpost, dh_ref[...])
        dwpost_ref[...] += dw
        dmb = dm.astype(BF16)
        dm_ref[...] = dmb
        dcat = _dg(dmb, wout_ref[...], NT)
        wgn = wgn_ref[...]
        for h in range(GLA_HEADS):
            ls = slice(GLA_DV * h, GLA_DV * (h + 1))
            dog = dcat[:, ls]
            g = gg_ref[:, ls]
            sg = _sigmoid(g)
            y, xh, r = _rms(og_ref[:, ls], wgn)
            dgg_ref[:, ls] = dog * y * (sg * (1.0 + g * (1.0 - sg)))
            dx, dw = _rms_bwd(xh, r, wgn, dog * (g * sg))
            dog_ref[:, ls] = dx
            dwgn_ref[...] += dw
        wsn = wsn_ref[...]
        _, xh, r = _rms(os_ref[...], wsn)
        dx, dw = _rms_bwd(xh, r, wsn, dcat[:, GLA_W:])
        dos_ref[...] = dx
        dwsn_ref[...] += dw

    def row(w):
        return pl.BlockSpec((tm, w), lambda i: (i, 0))

    def rshape(w, dt=F32):
        return jax.ShapeDtypeStruct((t, w), dt)

    return _pallas(
        body, name="mix_out_bwd", grid=(t // tm,),
        in_specs=[row(D_MODEL), row(D_MODEL), row(512), row(512), row(512), _full((1, GLA_DV)), _full((1, SWA_W)),
                  _full((D_MODEL, D_MODEL)), _full((1, D_MODEL))],
        out_specs=[row(512), row(512), row(512), row(D_MODEL), _full((1, D_MODEL)), _full((1, GLA_DV)),
                   _full((1, SWA_W))],
        out_shape=[rshape(512), rshape(512), rshape(512), rshape(D_MODEL, BF16),
                   jax.ShapeDtypeStruct((1, D_MODEL), F32), jax.ShapeDtypeStruct((1, GLA_DV), F32),
                   jax.ShapeDtypeStruct((1, SWA_W), F32)],
        args=(dh2, m, ogla, gg, oswa, wgn, wsn, wout, wpost), hook=hook)


def _mix_in_bwd(dh2, h1, wmixpre, winp, wa2p, bap, cos, sin, ga, dgq, dgk, dgv, dgg, dla, dsq, dsk, dsv, dkm, dvm):
    t = h1.shape[0]
    tm = _row_tile(t)

    def body(dh2_ref, h_ref, w_ref, win_ref, wa2_ref, ba_ref, cos_ref, sin_ref, ga_ref, dgq_ref, dgk_ref, dgv_ref,
             dgg_ref, dla_ref, dsq_ref, dsk_ref, dsv_ref, dkm_ref, dvm_ref,
             dh1_ref, dproj_ref, dw_ref, dwa2_ref, dba_ref):
        i = pl.program_id(0)

        @pl.when(i == 0)
        def _():
            dw_ref[...] = jnp.zeros_like(dw_ref)
            dwa2_ref[...] = jnp.zeros_like(dwa2_ref)
            dba_ref[...] = jnp.zeros_like(dba_ref)

        first = (i == 0).astype(F32)
        c = cos_ref[...]
        s = -sin_ref[...]
        fh = _first_half_mask(tm)
        dproj_ref[:, P_GQ:P_GK] = dgq_ref[...].astype(BF16)
        dproj_ref[:, P_GK:P_GV] = dgk_ref[...].astype(BF16)
        dproj_ref[:, P_GV:P_GG] = dgv_ref[...].astype(BF16)
        dproj_ref[:, P_GG:P_GA] = dgg_ref[...].astype(BF16)
        gab = ga_ref[...].astype(BF16)
        z = _dot(gab, wa2_ref[...]) + ba_ref[...]
        row_id = i * tm + lax.broadcasted_iota(jnp.int32, (tm, 1), 0)
        dz = jnp.where(row_id >= PAD, dla_ref[...] * (1.0 / GLA_TAU) * (1.0 - _sigmoid(z)), 0.0)
        dzb = dz.astype(BF16)
        dba_ref[...] += jnp.sum(dz, axis=0, keepdims=True)
        dwa2_ref[...] += _dg(gab, dzb, TN)
        dproj_ref[:, P_GA:P_SQ] = _dg(dzb, wa2_ref[...], NT).astype(BF16)
        for k in range(4):
            dy = dsq_ref[:, 128 * k:128 * (k + 1)]
            dproj_ref[:, P_SQ + 128 * k:P_SQ + 128 * (k + 1)] = (dy * c + _rot_half(dy, fh) * s).astype(BF16)
        for k in range(2):
            ls = slice(128 * k, 128 * (k + 1))
            dy = dsk_ref[:, ls]
            dy = jnp.concatenate([dy[:BLK] + first * dkm_ref[:, ls], dy[BLK:]], axis=0) if tm > BLK else (
                dy + first * dkm_ref[:, ls])
            dproj_ref[:, P_SK + 128 * k:P_SK + 128 * (k + 1)] = (dy * c + _rot_half(dy, fh) * s).astype(BF16)
            dv = dsv_ref[:, ls]
            dv = jnp.concatenate([dv[:BLK] + first * dvm_ref[:, ls], dv[BLK:]], axis=0) if tm > BLK else (
                dv + first * dvm_ref[:, ls])
            dproj_ref[:, P_SV + 128 * k:P_SV + 128 * (k + 1)] = dv.astype(BF16)
        dn = _dg(dproj_ref[...], win_ref[...], NT)
        w = w_ref[...]
        _, hh, r = _rms(h_ref[...], w)
        dx, dw = _rms_bwd(hh, r, w, dn)
        dw_ref[...] += dw
        dh1_ref[...] = dh2_ref[...] + dx

    def row(w):
        return pl.BlockSpec((tm, w), lambda i: (i, 0))

    return pl.pallas_call(
        body, name="mix_in_bwd", grid=(t // tm,),
        in_specs=[row(D_MODEL), row(D_MODEL), _full((1, D_MODEL)), _full((D_MODEL, P_END)), _full((128, GLA_KW)),
                  _full((1, GLA_KW)), row(128), row(128), row(128), row(256), row(256), row(512), row(512), row(256),
                  row(512), row(256), row(256), _full((BLK, 256)), _full((BLK, 256))],
        out_specs=[row(D_MODEL), row(P_END), _full((1, D_MODEL)), _full((128, GLA_KW)), _full((1, GLA_KW))],
        out_shape=[jax.ShapeDtypeStruct((t, D_MODEL), F32), jax.ShapeDtypeStruct((t, P_END), BF16),
                   jax.ShapeDtypeStruct((1, D_MODEL), F32), jax.ShapeDtypeStruct((128, GLA_KW), F32),
                   jax.ShapeDtypeStruct((1, GLA_KW), F32)],
        compiler_params=_cparams(1),
    )(dh2, h1, wmixpre, winp, wa2p, bap, cos, sin, ga, dgq, dgk, dgv, dgg, dla, dsq, dsk, dsv, dkm, dvm)


def _adamw_update(w, g, m, v):
    m = ADAM_B1 * m + (1.0 - ADAM_B1) * g
    v = ADAM_B2 * v + (1.0 - ADAM_B2) * (g * g)
    m_hat = m / (1.0 - ADAM_B1 ** ADAM_STEP)
    v_hat = v / (1.0 - ADAM_B2 ** ADAM_STEP)
    return -ADAM_LR * (m_hat / (jnp.sqrt(v_hat) + ADAM_EPS) + ADAM_WD * w), m, v


def _adamw_halves(w, g_mine, g_other, m, v, c_idx, row0=0):
    r, c = w.shape
    h = g_mine.shape[0]
    tr = _div_tile(math.gcd(r, h))
    nth = h // tr
    t0 = row0 // tr
    assert t0 * tr == row0

    def body(c_ref, w_ref, gm_ref, go_ref, m_ref, v_ref, g_ref, d_ref, nm_ref, nv_ref):
        hh = (t0 + pl.program_id(0)) // nth
        g = jnp.where(hh == c_ref[0], gm_ref[...], go_ref[...])
        g_ref[...] = g
        d_ref[...], nm_ref[...], nv_ref[...] = _adamw_update(w_ref[...], g, m_ref[...], v_ref[...])

    spec = pl.BlockSpec((tr, c), lambda i, c_ref: (i, 0))

    def gspec(is_mine):
        def index(i, c_ref):
            used = ((t0 + i) // nth == c_ref[0]) == is_mine
            return (jnp.where(used, (t0 + i) % nth, 0), 0)
        return pl.BlockSpec((tr, c), index)

    shape = jax.ShapeDtypeStruct((r, c), F32)
    return pl.pallas_call(
        body, name="adamw_halves",
        grid_spec=pltpu.PrefetchScalarGridSpec(
            num_scalar_prefetch=1, grid=(r // tr,), in_specs=[spec, gspec(True), gspec(False), spec, spec],
            out_specs=[spec] * 4),
        out_shape=[shape] * 4, compiler_params=_cparams(1),
    )(c_idx, w, g_mine, g_other, m, v)


def _place():
    x, y, c = lax.axis_index("x"), lax.axis_index("y"), lax.axis_index("c")
    chips = [(1 - x, y), (x, 1 - y), (1 - x, 1 - y)]
    return x, y, c, chips


def _remote(send_sem, recv_sem, src, dst, to):
    return pltpu.make_async_remote_copy(src_ref=src, dst_ref=dst, send_sem=send_sem, recv_sem=recv_sem,
                                        device_id=to, device_id_type=MESH)


def _half(ref_rows, c):
    h = ref_rows // 2
    return pl.ds(pl.multiple_of(c * h, 8), h)


def _own_slot(shard, q):
    return lax.dynamic_update_slice(jnp.zeros((N_CHIPS,) + shard.shape, shard.dtype), shard[None], (q, 0, 0))


class _GatherChips:
    has_mid = True

    def __init__(self, bufs):
        n = len(bufs)
        self.inputs = list(bufs)
        self.out_shape = [jax.ShapeDtypeStruct(b.shape, b.dtype) for b in bufs]
        self.aliases = [(t, t) for t in range(n)]
        self.scratch = [pltpu.SemaphoreType.DMA((n, 6)), pltpu.SemaphoreType.DMA((n, 6))]

    def start(self, ins, outs, scr):
        send, recv = scr
        x, y, c, chips = _place()
        q = 2 * x + y
        for t, (i_ref, o_ref) in enumerate(zip(ins, outs)):
            rows = _half(i_ref.shape[1], c)
            for j, (cx, cy) in enumerate(chips):
                _remote(send.at[t, j], recv.at[t, j], i_ref.at[q, rows], o_ref.at[q, rows], (cx, cy, c)).start()

    def mid(self, ins, outs, scr):
        send, recv = scr
        x, y, c, chips = _place()
        for t, o_ref in enumerate(outs):
            rows = _half(o_ref.shape[1], c)
            for j, (cx, cy) in enumerate(chips):
                slot = o_ref.at[2 * cx + cy, rows]
                _remote(send.at[t, j], recv.at[t, j], slot, slot, (cx, cy, c)).wait_recv()
                _remote(send.at[t, 3 + j], recv.at[t, 3 + j], slot, slot, (x, y, 1 - c)).start()

    def finish(self, ins, outs, scr):
        send, recv = scr
        x, y, c, chips = _place()
        for t, o_ref in enumerate(outs):
            mine, other = _half(o_ref.shape[1], c), _half(o_ref.shape[1], 1 - c)
            for j, (cx, cy) in enumerate(chips):
                slot = o_ref.at[2 * cx + cy, other]
                _remote(send.at[t, 3 + j], recv.at[t, 3 + j], slot, slot, (x, y, 1 - c)).wait_recv()
            for j, (cx, cy) in enumerate(chips):
                sent = o_ref.at[2 * cx + cy, mine]
                _remote(send.at[t, j], recv.at[t, j], sent, sent, (cx, cy, c)).wait_send()
                _remote(send.at[t, 3 + j], recv.at[t, 3 + j], sent, sent, (x, y, 1 - c)).wait_send()


class _PairExchange:
    has_mid = False
    aliases = ()

    def __init__(self, arrs):
        n = len(arrs)
        self.inputs = list(arrs)
        self.out_shape = [jax.ShapeDtypeStruct((a.shape[0], a.shape[1] // 2, a.shape[2]), a.dtype) for a in arrs]
        self.scratch = [pltpu.SemaphoreType.DMA((n,)), pltpu.SemaphoreType.DMA((n,))]

    def _copies(self, ins, outs, scr):
        send, recv = scr
        x, y, c, _ = _place()
        return [_remote(send.at[t], recv.at[t], i_ref.at[:, _half(i_ref.shape[1], 1 - c)], o_ref, (x, y, 1 - c))
                for t, (i_ref, o_ref) in enumerate(zip(ins, outs))]

    def start(self, ins, outs, scr):
        for cp in self._copies(ins, outs, scr):
            cp.start()

    def finish(self, ins, outs, scr):
        for cp in self._copies(ins, outs, scr):
            cp.wait()


class _ChipScatter:
    has_mid = False
    aliases = ()

    def __init__(self, arrs):
        n = len(arrs)
        self.inputs = list(arrs)
        self.out_shape = [jax.ShapeDtypeStruct((3,) + a.shape[1:], a.dtype) for a in arrs]
        self.scratch = [pltpu.SemaphoreType.DMA((n, 3)), pltpu.SemaphoreType.DMA((n, 3))]

    def _copies(self, ins, outs, scr):
        send, recv = scr
        x, y, c, chips = _place()
        return [_remote(send.at[t, j], recv.at[t, j], i_ref.at[2 * cx + cy], o_ref.at[j], (cx, cy, c))
                for t, (i_ref, o_ref) in enumerate(zip(ins, outs)) for j, (cx, cy) in enumerate(chips)]

    def start(self, ins, outs, scr):
        for cp in self._copies(ins, outs, scr):
            cp.start()

    def finish(self, ins, outs, scr):
        for cp in self._copies(ins, outs, scr):
            cp.wait()


class _PairShare:
    has_mid = False
    aliases = ()

    def __init__(self, arrs):
        n = len(arrs)
        self.inputs = list(arrs)
        self.out_shape = [jax.ShapeDtypeStruct(a.shape, a.dtype) for a in arrs]
        self.scratch = [pltpu.SemaphoreType.DMA((n,)), pltpu.SemaphoreType.DMA((n,))]

    def _copies(self, ins, outs, scr):
        send, recv = scr
        x, y, c, _ = _place()
        return [_remote(send.at[t], recv.at[t], i_ref, o_ref, (x, y, 1 - c))
                for t, (i_ref, o_ref) in enumerate(zip(ins, outs))]

    def start(self, ins, outs, scr):
        for cp in self._copies(ins, outs, scr):
            cp.start()

    def finish(self, ins, outs, scr):
        for cp in self._copies(ins, outs, scr):
            cp.wait()


def _comm_call(hook, name):
    n_in, n_out = len(hook.inputs), len(hook.out_shape)

    def body(*refs):
        ins, outs, scr = refs[:n_in], refs[n_in:n_in + n_out], refs[n_in + n_out:]
        hook.start(ins, outs, scr)
        if hook.has_mid:
            hook.mid(ins, outs, scr)
        hook.finish(ins, outs, scr)

    return pl.pallas_call(body, name=name, in_specs=[ANY] * n_in, out_specs=[ANY] * n_out,
                          out_shape=list(hook.out_shape), scratch_shapes=list(hook.scratch),
                          input_output_aliases=dict(hook.aliases))(*hook.inputs)


def _all_gather_devices(vecs):
    n = len(vecs)

    def body(*refs):
        x_refs, out_refs = refs[:n], refs[n:2 * n]
        send_sems, recv_sems, local_sems = refs[2 * n:]
        x, y, c, chips = _place()
        me, sibling = (x, y, c), (x, y, 1 - c)
        waits = []
        for t, (x_ref, out_ref) in enumerate(zip(x_refs, out_refs)):
            def slot(px, py, pc, out_ref=out_ref):
                return out_ref.at[4 * px + 2 * py + pc]

            def copy(k, block, to, src=None, t=t, slot=slot):
                return pltpu.make_async_remote_copy(
                    src_ref=slot(*block) if src is None else src, dst_ref=slot(*block), send_sem=send_sems.at[t, k],
                    recv_sem=recv_sems.at[t, k], device_id=to, device_id_type=MESH)

            mine = pltpu.make_async_copy(x_ref, slot(*me), local_sems.at[t])
            mine.start()
            first = [copy(0, me, sibling, src=x_ref)]
            first += [copy(1 + j, me, (*chip, c), src=x_ref) for j, chip in enumerate(chips)]
            for cp in first:
                cp.start()
            waits.append((copy, mine, first))
        for copy, mine, first in waits:
            passed = [copy(4 + j, (*chip, c), sibling) for j, chip in enumerate(chips)]
            for j, chip in enumerate(chips):
                copy(1 + j, (*chip, c), me).wait_recv()
                passed[j].start()
            copy(0, sibling, me).wait_recv()
            for j, chip in enumerate(chips):
                copy(4 + j, (*chip, 1 - c), me).wait_recv()
            for cp in first + passed:
                cp.wait_send()
            mine.wait()

    vmem = pl.BlockSpec(memory_space=pltpu.VMEM)
    return pl.pallas_call(
        body, name="all_gather_devices", in_specs=[vmem] * n, out_specs=[vmem] * n,
        out_shape=[jax.ShapeDtypeStruct((N_DEV,) + v.shape, v.dtype) for v in vecs],
        scratch_shapes=[pltpu.SemaphoreType.DMA((n, 7)), pltpu.SemaphoreType.DMA((n, 7)),
                        pltpu.SemaphoreType.DMA((n,))],
    )(*vecs)


def _pair_sum(g, other, c_idx):
    nq, r, w = g.shape
    h = r // 2
    tr = _div_tile(h)
    nt = h // tr

    def body(c_ref, g_ref, o_ref, s_ref):
        s_ref[...] = (g_ref[...].astype(F32) + o_ref[...].astype(F32)).astype(s_ref.dtype)

    return pl.pallas_call(
        body, name="pair_sum",
        grid_spec=pltpu.PrefetchScalarGridSpec(
            num_scalar_prefetch=1, grid=(nq, nt),
            in_specs=[pl.BlockSpec((None, tr, w), lambda k, i, c_ref: (k, c_ref[0] * nt + i, 0)),
                      pl.BlockSpec((None, tr, w), lambda k, i, c_ref: (k, i, 0))],
            out_specs=pl.BlockSpec((None, tr, w), lambda k, i, c_ref: (k, i, 0))),
        out_shape=jax.ShapeDtypeStruct((nq, h, w), g.dtype),
        compiler_params=_cparams(2),
    )(c_idx, g, other)


def _chip_sum(s, others, q_idx):
    _, h, w = s.shape
    tr = _div_tile(h)

    def body(q_ref, s_ref, o_ref, out_ref):
        out_ref[...] = ((s_ref[...].astype(F32) + o_ref[0].astype(F32)) + o_ref[1].astype(F32)) + o_ref[2].astype(F32)

    return pl.pallas_call(
        body, name="chip_sum",
        grid_spec=pltpu.PrefetchScalarGridSpec(
            num_scalar_prefetch=1, grid=(h // tr,),
            in_specs=[pl.BlockSpec((None, tr, w), lambda i, q_ref: (q_ref[0], i, 0)),
                      pl.BlockSpec((3, tr, w), lambda i, q_ref: (0, i, 0))],
            out_specs=pl.BlockSpec((tr, w), lambda i, q_ref: (i, 0))),
        out_shape=jax.ShapeDtypeStruct((h, w), F32),
        compiler_params=_cparams(1),
    )(q_idx, s, others)


def _small_update(q_idx, parts, ws, ms, vs, col_block):
    n = len(parts)
    has_w = [w is not None for w in ws]

    def body(q_ref, *refs):
        pos = 0
        ins = []
        for t in range(n):
            k = 4 if has_w[t] else 1
            ins.append(refs[pos:pos + k])
            pos += k
        outs = refs[pos:]
        opos = 0
        for t in range(n):
            p_ref = ins[t][0]
            g = p_ref[0]
            for s in range(1, p_ref.shape[0]):
                g = g + p_ref[s]
            if has_w[t]:
                _, w_ref, m_ref, v_ref = ins[t]
                g_ref, d_ref, nm_ref, nv_ref = outs[opos:opos + 4]
                opos += 4
                g_ref[...] = g
                d_ref[...], nm_ref[...], nv_ref[...] = _adamw_update(w_ref[...], g, m_ref[...], v_ref[...])
            else:
                outs[opos][...] = g
                opos += 1

    def whole(shape):
        nd = len(shape)
        return pl.BlockSpec(shape, lambda i, q_ref: (0,) * nd)

    in_specs, out_specs, out_shape, args = [], [], [], []
    for t in range(n):
        k, r, wf = parts[t].shape
        if col_block[t]:
            w = wf // N_CHIPS
            in_specs.append(pl.BlockSpec((k, r, w), lambda i, q_ref: (0, 0, q_ref[0])))
        else:
            w = wf
            in_specs.append(whole((k, r, wf)))
        args.append(parts[t])
        if has_w[t]:
            assert ws[t].shape == (r, w), (ws[t].shape, r, w)
            in_specs += [whole((r, w))] * 3
            args += [ws[t], ms[t], vs[t]]
            out_specs += [whole((r, w))] * 4
            out_shape += [jax.ShapeDtypeStruct((r, w), F32)] * 4
        else:
            out_specs.append(whole((r, w)))
            out_shape.append(jax.ShapeDtypeStruct((r, w), F32))
    return pl.pallas_call(
        body, name="small_update",
        grid_spec=pltpu.PrefetchScalarGridSpec(num_scalar_prefetch=1, grid=(1,), in_specs=in_specs,
                                               out_specs=out_specs),
        out_shape=out_shape, compiler_params=_cparams(1),
    )(q_idx, *args)


_PACK_SEGMENTS = ((0, 1552), None, (1552, 2064), (2064, 2128), (2064, 2128), (2128, 2192), (2128, 2192),
                  (2192, 2256), (2192, 2256), (2256, 2320), (2256, 2320))
_UNPACK_SEGMENTS = (((0, 1552), (0,)), ((1552, 2064), (P_SQ,)), ((2064, 2128), (P_SK, P_SK + 64)),
                    ((2128, 2192), (P_SK + 128, P_SK + 192)), ((2192, 2256), (P_SV, P_SV + 64)),
                    ((2256, 2320), (P_SV + 128, P_SV + 192)))


def _pack_win(w4):
    per = w4.shape[2]
    pieces = []
    for seg in _PACK_SEGMENTS:
        if seg is None:
            pieces.append(jnp.zeros((w4.shape[1], 128 - GLA_RANK), w4.dtype))
            continue
        for q in range(w4.shape[0]):
            lo, hi = max(seg[0], q * per), min(seg[1], (q + 1) * per)
            if lo < hi:
                pieces.append(w4[q][:, lo - q * per:hi - q * per])
    return jnp.concatenate(pieces, axis=1)


def _unpack_dwin(d):
    per = D_IN // N_CHIPS
    chips = []
    for q in range(N_CHIPS):
        pieces = []
        for (a, b), starts in _UNPACK_SEGMENTS:
            lo, hi = max(a, q * per), min(b, (q + 1) * per)
            if lo < hi:
                copies = [d[:, s + lo - a:s + hi - a] for s in starts]
                pieces.append(copies[0] if len(copies) == 1 else copies[0] + copies[1])
        chips.append(jnp.concatenate(pieces, axis=1))
    return jnp.stack(chips).astype(BF16)


def _local_step(x, target, meta, p):
    s = x.shape[0]
    t = s + BLK
    h0 = jnp.concatenate([jnp.zeros((PAD, D_MODEL), F32), meta, x], axis=0)
    cos, sin = _rope_tables(t)

    h1, n1, g1, u1, a1, f1 = _ffn_fwd(h0, p["ffn1_pre_norm"], p["ffn1_w_gate"], p["ffn1_w_up"], p["ffn1_w_down"],
                                      p["ffn1_post_norm"])
    n2, gq, gk, gv, gg, ga, la, sq, sk, sv = _mix_proj(h1, p["mix_pre_norm"], p["w_in"], p["gla_w_a2"], p["gla_b_a"],
                                                       cos, sin)
    ogla, ss = _gla_fwd(gq, gk, gv, la)
    oswa = _swa_fwd(p["swa_sinks"], sq, sk, sv)
    h2, cat, m = _mix_out(h1, ogla, gg, oswa, p["gla_out_norm"], p["swa_out_norm"], p["w_out"], p["mix_post_norm"])
    grads = {}
    dy, n3, g3, u3, a3, df3, grads["ffn2_post_norm"], sse = _ffn_fwd(
        h2, p["ffn2_pre_norm"], p["ffn2_w_gate"], p["ffn2_w_up"], p["ffn2_w_down"], p["ffn2_post_norm"],
        target=target)

    dh2, dg3, du3, grads["ffn2_pre_norm"] = _ffn_bwd(
        dy, h2, None, g3, u3, p["ffn2_pre_norm"], p["ffn2_w_gate"], p["ffn2_w_up"], p["ffn2_w_down"],
        p["ffn2_post_norm"], df=df3)
    (gud,) = _ffn_wgrad(n3, df3, dg3, du3, a3)
    grads["ffn2_w_gate"], grads["ffn2_w_up"], grads["ffn2_w_down"] = gud[:, :FJ], gud[:, FJ:2 * FJ], gud[:, 2 * FJ:]

    dogla, dgg, doswa, dm, grads["mix_post_norm"], grads["gla_out_norm"], grads["swa_out_norm"] = _mix_out_bwd(
        dh2, m, ogla, gg, oswa, p["gla_out_norm"], p["swa_out_norm"], p["w_out"], p["mix_post_norm"])
    grads["w_out"] = _xty(cat, dm)
    dsq, dsk, dsv, dkm, dvm, dsinks = _swa_bwd(p["swa_sinks"], sq, sk, sv, oswa, doswa)
    grads["swa_sinks"] = dsinks[:, 0]
    dgq, dgk, dgv, dla = _gla_bwd(gq, gk, gv, la, ss, dogla)
    dh1, dproj, grads["mix_pre_norm"], dwa2p, grads["gla_b_a"] = _mix_in_bwd(
        dh2, h1, p["mix_pre_norm"], p["w_in"], p["gla_w_a2"], p["gla_b_a"], cos, sin, ga, dgq, dgk, dgv, dgg, dla,
        dsq, dsk, dsv, dkm, dvm)
    grads["gla_w_a2"] = dwa2p[:GLA_RANK]
    grads["w_in"] = _unpack_dwin(_xty(n2, dproj))

    dh0, df1, dg1, du1, grads["ffn1_pre_norm"], grads["ffn1_post_norm"] = _ffn_bwd(
        dh1, h0, f1, g1, u1, p["ffn1_pre_norm"], p["ffn1_w_gate"], p["ffn1_w_up"], p["ffn1_w_down"],
        p["ffn1_post_norm"])
    (gud,) = _ffn_wgrad(n1, df1, dg1, du1, a1)
    grads["ffn1_w_gate"], grads["ffn1_w_up"], grads["ffn1_w_down"] = gud[:, :FJ], gud[:, FJ:2 * FJ], gud[:, 2 * FJ:]
    grads["meta_tokens"] = dh0[PAD:BLK]
    return sse[0, 0], dh0[BLK:], grads


WEIGHTS = ['meta_tokens', 'ffn1_pre_norm', 'ffn1_w_gate', 'ffn1_w_up', 'ffn1_w_down', 'ffn1_post_norm',
           'mix_pre_norm', 'w_in', 'gla_w_a2', 'gla_b_a', 'gla_out_norm', 'swa_sinks', 'swa_out_norm', 'w_out',
           'mix_post_norm', 'ffn2_pre_norm', 'ffn2_w_gate', 'ffn2_w_up', 'ffn2_w_down', 'ffn2_post_norm']
BIG = ['ffn1_w_gate', 'ffn1_w_up', 'ffn1_w_down', 'w_in', 'w_out', 'ffn2_w_gate', 'ffn2_w_up', 'ffn2_w_down']
SMALL = [n for n in WEIGHTS if n not in BIG]
FJ = D_FF // N_CHIPS
D_IN_J = D_IN // N_CHIPS
D_OUT_J = D_MODEL // N_CHIPS
TRANSPOSED = ('ffn1_w_gate', 'ffn1_w_up', 'ffn2_w_gate', 'ffn2_w_up')


def _shard2d(name, a):
    return a[0].T if name in TRANSPOSED else a[0]


def _unshard2d(name, a):
    return (a.T if name in TRANSPOSED else a)[None]


def kernel(x, meta_tokens, ffn1_pre_norm, ffn1_w_gate, ffn1_w_up, ffn1_w_down, ffn1_post_norm, mix_pre_norm, w_in, gla_w_a2, gla_b_a, gla_out_norm, swa_sinks, swa_out_norm, w_out, mix_post_norm, ffn2_pre_norm, ffn2_w_gate, ffn2_w_up, ffn2_w_down, ffn2_post_norm, loss_target, m_meta_tokens, m_ffn1_pre_norm, m_ffn1_w_gate, m_ffn1_w_up, m_ffn1_w_down, m_ffn1_post_norm, m_mix_pre_norm, m_w_in, m_gla_w_a2, m_gla_b_a, m_gla_out_norm, m_swa_sinks, m_swa_out_norm, m_w_out, m_mix_post_norm, m_ffn2_pre_norm, m_ffn2_w_gate, m_ffn2_w_up, m_ffn2_w_down, m_ffn2_post_norm, v_meta_tokens, v_ffn1_pre_norm, v_ffn1_w_gate, v_ffn1_w_up, v_ffn1_w_down, v_ffn1_post_norm, v_mix_pre_norm, v_w_in, v_gla_w_a2, v_gla_b_a, v_gla_out_norm, v_swa_sinks, v_swa_out_norm, v_w_out, v_mix_post_norm, v_ffn2_pre_norm, v_ffn2_w_gate, v_ffn2_w_up, v_ffn2_w_down, v_ffn2_post_norm):
    args = dict(locals())
    w = {n: args[n] for n in WEIGHTS}
    mom = {n: args["m_" + n] for n in WEIGHTS}
    var = {n: args["v_" + n] for n in WEIGHTS}
    cx, cy, cc = lax.axis_index("x"), lax.axis_index("y"), lax.axis_index("c")
    q_idx = (2 * cx + cy).astype(jnp.int32).reshape(1)
    c_idx = cc.astype(jnp.int32).reshape(1)

    q_chip = 2 * cx + cy
    bf = {n: _own_slot(_shard2d(n, w[n]).astype(BF16), q_chip) for n in BIG}
    qc_idx = jnp.stack([q_chip, cc]).astype(jnp.int32)
    early = _GatherChips([_own_slot(w["meta_tokens"], q_chip),
                          _own_slot(w["gla_w_a2"].reshape(GLA_RANK, GLA_KW // N_CHIPS), q_chip)])
    meta4, wa24 = _comm_call(early, "gather_small")
    meta_full = meta4.transpose(1, 0, 2).reshape(N_META, D_MODEL)
    wa2p = jnp.pad(wa24.transpose(1, 0, 2).reshape(GLA_RANK, GLA_KW), ((0, 128 - GLA_RANK), (0, 0))).astype(BF16)
    sinks = w["swa_sinks"].reshape(SWA_QH)

    seq, target = x[0], loss_target[0]
    t = seq.shape[0] + BLK
    h0, n1 = _embed_norm(seq, meta_full, w["ffn1_pre_norm"])
    cos, sin = _rope_tables(t)
    late = _GatherChips([bf["w_in"], bf["w_out"], bf["ffn2_w_gate"], bf["ffn2_w_up"], bf["ffn2_w_down"]])
    (h1, g1, u1, a1, f1), (wg1, wu1, wd1), (win4, wout4, wg2, wu2, wd2) = _ffn_fwd_gather(
        h0, n1, [bf["ffn1_w_gate"], bf["ffn1_w_up"], bf["ffn1_w_down"]], w["ffn1_post_norm"], qc_idx, late)
    winp = _pack_win(win4)
    wout = wout4.reshape(D_MODEL, D_MODEL)
    n2, gq, gk, gv, gg, ga, la, sq, sk, sv = _mix_proj(h1, w["mix_pre_norm"], winp, wa2p, w["gla_b_a"], cos, sin)
    ogla, ss = _gla_fwd(gq, gk, gv, la)
    oswa = _swa_fwd(sinks, sq, sk, sv)
    h2, cat, m = _mix_out(h1, ogla, gg, oswa, w["gla_out_norm"], w["swa_out_norm"], wout, w["mix_post_norm"])
    g = {}
    dy, n3, g3, u3, a3, df3, g["ffn2_post_norm"], sse = _ffn_fwd(
        h2, w["ffn2_pre_norm"], wg2, wu2, wd2, w["ffn2_post_norm"], target=target)

    dh2, dg3, du3, g["ffn2_pre_norm"] = _ffn_bwd(
        dy, h2, None, g3, u3, w["ffn2_pre_norm"], wg2, wu2, wd2, w["ffn2_post_norm"], df=df3)
    (gf2,) = _ffn_wgrad(n3, df3, dg3, du3, a3)
    (dogla, dgg, doswa, dm, g["mix_post_norm"], g["gla_out_norm"], g["swa_out_norm"]), (rgf2,) = _mix_out_bwd(
        dh2, m, ogla, gg, oswa, w["gla_out_norm"], w["swa_out_norm"], wout, w["mix_post_norm"],
        hook=_PairExchange([gf2]))
    sgf2 = _pair_sum(gf2, rgf2, c_idx)
    gout = _xty(cat, dm).reshape(N_CHIPS, D_OUT_J, D_MODEL).astype(BF16)
    (dsq, dsk, dsv, dkm, dvm, dsinks), (ogf2,) = _swa_bwd(sinks, sq, sk, sv, oswa, doswa,
                                                          hook=_ChipScatter([sgf2]))
    g["swa_sinks"] = dsinks
    dgq, dgk, dgv, dla = _gla_bwd(gq, gk, gv, la, ss, dogla)
    dh1, dproj, g["mix_pre_norm"], dwa2p, g["gla_b_a"] = _mix_in_bwd(
        dh2, h1, w["mix_pre_norm"], winp, wa2p, w["gla_b_a"], cos, sin, ga, dgq, dgk, dgv, dgg, dla,
        dsq, dsk, dsv, dkm, dvm)
    g["gla_w_a2"] = dwa2p[:GLA_RANK]
    gin = _unpack_dwin(_xty(n2, dproj))
    (dh0, df1, dg1, du1, g["ffn1_pre_norm"], g["ffn1_post_norm"]), (rgin, rgout) = _ffn_bwd(
        dh1, h0, f1, g1, u1, w["ffn1_pre_norm"], wg1, wu1, wd1, w["ffn1_post_norm"],
        hook=_PairExchange([gin, gout]))
    sgin, sgout = _pair_sum(gin, rgin, c_idx), _pair_sum(gout, rgout, c_idx)
    own1, others1, (ogin, ogout) = _ffn_wgrad_reduce(n1, df1, dg1, du1, a1, qc_idx, _ChipScatter([sgin, sgout]))
    g["meta_tokens"] = dh0[PAD:BLK]
    grad_x = dh0[BLK:]
    halves = [_chip_sum(own1[None], others1, jnp.zeros((1,), jnp.int32))]
    halves += [_chip_sum(s, o, q_idx) for s, o in ((sgin, ogin), (sgout, ogout), (sgf2, ogf2))]
    others = _comm_call(_PairShare(halves), "pair_share")
    reduced = {"ffn1_w_gate": (0, 0), "ffn1_w_up": (0, FJ), "ffn1_w_down": (0, 2 * FJ), "w_in": (1, 0),
               "w_out": (2, 0), "ffn2_w_gate": (3, 0), "ffn2_w_up": (3, FJ), "ffn2_w_down": (3, 2 * FJ)}
    grad, delta, new_m, new_v = {}, {}, {}, {}
    for n in BIG:
        k, row0 = reduced[n]
        outs = _adamw_halves(_shard2d(n, w[n]), halves[k], others[k], _shard2d(n, mom[n]), _shard2d(n, var[n]),
                             c_idx, row0)
        grad[n], delta[n], new_m[n], new_v[n] = [_unshard2d(n, a) for a in outs]

    late = ["gla_w_a2", "swa_sinks"]
    direct = [n for n in SMALL if n not in late]
    names = direct + late
    gathered = _all_gather_devices([g[n] for n in names] + [sse])
    mat = lambda a: a.reshape(a.shape[-2:])
    none3 = [None] * (len(late) + 1)
    outs = _small_update(q_idx, gathered, [mat(w[n]) for n in direct] + none3, [mat(mom[n]) for n in direct] + none3,
                         [mat(var[n]) for n in direct] + none3, [n == "meta_tokens" for n in names] + [False])
    sum_a2, sum_sinks, sum_sse = outs[4 * len(direct):]
    loss = sum_sse[0, 0] * (0.5 / D_MODEL)
    g_late = [lax.dynamic_slice_in_dim(sum_a2, q_chip * (GLA_KW // N_CHIPS), GLA_KW // N_CHIPS, axis=1)[None],
              sum_sinks[:, 0].reshape(1, 1, SWA_QH)]
    outs = list(outs[:4 * len(direct)]) + list(_small_update(
        q_idx, g_late, [mat(w[n]) for n in late], [mat(mom[n]) for n in late], [mat(var[n]) for n in late],
        [False, False]))
    for k, n in enumerate(names):
        grad[n], delta[n], new_m[n], new_v[n] = [a.reshape(w[n].shape) for a in outs[4 * k:4 * k + 4]]

    return (loss, grad_x[None], *[grad[n] for n in WEIGHTS], *[delta[n] for n in WEIGHTS],
            *[new_m[n] for n in WEIGHTS], *[new_v[n] for n in WEIGHTS])
```

```python
import functools
import math

import numpy as np
import jax
import jax.numpy as jnp
from jax import lax
from jax.experimental import pallas as pl
from jax.experimental.pallas import tpu as pltpu

F32 = jnp.float32
BF16 = jnp.bfloat16
MESH = pl.DeviceIdType.MESH

D_MODEL = 1024
D_FF = 2816
N_CHIPS = 4
N_DEV = 8
N_META = 16
BLK = 128
PAD = BLK - N_META
GLA_CHUNK = 64
GLA_HEADS = 4
GLA_DV = 128
GLA_DK = 64
GLA_KW = GLA_HEADS * GLA_DK
GLA_W = GLA_HEADS * GLA_DV
GLA_RANK = 16
GLA_TAU = 16.0
SWA_HD = 64
SWA_QH = 8
SWA_KVH = 2
SWA_W = SWA_QH * SWA_HD
WINDOW = 128
ROPE_THETA = 10000.0
EPS = 1e-6
NEG_INF = -1e30
IN_SPLITS = (256, 256, 512, 512, 16, 512, 128, 128)
D_IN = sum(IN_SPLITS)
P_GQ, P_GK, P_GV, P_GG, P_GA, P_SQ, P_SK, P_SV, P_END = 0, 256, 512, 1024, 1536, 1664, 2176, 2432, 2688
ADAM_LR, ADAM_B1, ADAM_B2, ADAM_EPS, ADAM_WD, ADAM_STEP = 0.001, 0.9, 0.999, 1e-08, 0.01, 10
VMEM_LIMIT = 56 * 1024 * 1024

NT = (((1,), (1,)), ((), ()))
TN = (((0,), (0,)), ((), ()))


def _cparams(n_axes):
    return pltpu.CompilerParams(dimension_semantics=("arbitrary",) * n_axes, vmem_limit_bytes=VMEM_LIMIT)


def _row_tile(t):
    for tm in (640, 512, 384, 256, 128):
        if t % tm == 0:
            return tm
    raise ValueError(t)


SEQ_BLOCKS_PER_STEP = 5


def _seq_tile(t):
    return SEQ_BLOCKS_PER_STEP * BLK if t % (SEQ_BLOCKS_PER_STEP * BLK) == 0 else BLK


ROW_PARTS = 2


def _row_parts(tm):
    n = ROW_PARTS if tm % (16 * ROW_PARTS) == 0 else 1
    return [slice(k * (tm // n), (k + 1) * (tm // n)) for k in range(n)]


def _contract_tile(t):
    return 1664 if t % 1664 == 0 else _row_tile(t)


def _div_tile(r, cap=512):
    best = None
    for tr in range(8, min(r, cap) + 1, 8):
        if r % tr == 0:
            best = tr
    return best if best is not None else r


def _dot(a, b):
    return jnp.dot(a, b, preferred_element_type=F32)


def _dg(a, b, dims):
    return lax.dot_general(a, b, dims, preferred_element_type=F32)


def _rms(x, w):
    r = lax.rsqrt(jnp.mean(x * x, axis=-1, keepdims=True) + EPS)
    xh = x * r
    return xh * w, xh, r


def _rms_bwd(xh, r, w, dy):
    wdy = dy * w
    dx = r * (wdy - xh * jnp.mean(wdy * xh, axis=-1, keepdims=True))
    dw = jnp.sum(dy * xh, axis=0, keepdims=True)
    return dx, dw


def _sigmoid(x):
    return 1.0 / (1.0 + jnp.exp(-x))


def _full(shape):
    nd = len(shape)
    return pl.BlockSpec(shape, lambda *_: (0,) * nd)


ANY = pl.BlockSpec(memory_space=pl.ANY)


def _pallas(body, *, name, grid, in_specs, out_specs, out_shape, args, scratch_shapes=(), hook=None):
    n_axes = len(grid)
    if hook is None:
        return pl.pallas_call(body, name=name, grid=grid, in_specs=list(in_specs), out_specs=list(out_specs),
                              out_shape=list(out_shape), scratch_shapes=list(scratch_shapes),
                              compiler_params=_cparams(n_axes))(*args)
    n_in, n_out, n_scr = len(in_specs), len(out_specs), len(scratch_shapes)
    h_in, h_out = len(hook.inputs), len(hook.out_shape)
    total = math.prod(grid)

    def wrapped(*refs):
        ins, hins = refs[:n_in], refs[n_in:n_in + h_in]
        o0 = n_in + h_in
        outs, houts = refs[o0:o0 + n_out], refs[o0 + n_out:o0 + n_out + h_out]
        s0 = o0 + n_out + h_out
        scr, hscr = refs[s0:s0 + n_scr], refs[s0 + n_scr:]
        step = pl.program_id(0)
        for a in range(1, n_axes):
            step = step * grid[a] + pl.program_id(a)

        @pl.when(step == 0)
        def _():
            hook.start(hins, houts, hscr)

        body(*ins, *outs, *scr)

        if hook.has_mid:
            @pl.when(step == (3 * total) // 4)
            def _():
                hook.mid(hins, houts, hscr)

        @pl.when(step == total - 1)
        def _():
            hook.finish(hins, houts, hscr)

    res = pl.pallas_call(
        wrapped, name=name, grid=grid, in_specs=list(in_specs) + [ANY] * h_in,
        out_specs=list(out_specs) + [ANY] * h_out, out_shape=list(out_shape) + list(hook.out_shape),
        scratch_shapes=list(scratch_shapes) + list(hook.scratch), compiler_params=_cparams(n_axes),
        input_output_aliases={n_in + a: n_out + b for a, b in hook.aliases},
    )(*args, *hook.inputs)
    return res[:n_out], res[n_out:]


def _ffn_weight_specs(w3):
    fj = w3.shape[1] // 3
    return fj, [pl.BlockSpec((None, fj, D_MODEL), functools.partial(lambda i, j, k: (j, k, 0), k=k)) for k in range(3)]


def _ffn_fwd(h, wpre, w3, wpost, hook=None, target=None):
    t = h.shape[0]
    tm = _row_tile(t)
    nj = w3.shape[0]
    fj, wspecs = _ffn_weight_specs(w3)
    nblk = tm // BLK if target is not None else 0

    def body(*refs):
        h_ref, wpre_ref, wg_ref, wu_ref, wd_ref, wpost_ref = refs[:6]
        t_refs = refs[6:6 + nblk]
        hout_ref, n_ref, p1_ref, p2_ref, a_ref, f_ref = refs[6 + nblk:12 + nblk]
        acc_ref = refs[-1]
        i = pl.program_id(0)
        j = pl.program_id(1)

        @pl.when(j == 0)
        def _():
            y, _, _ = _rms(h_ref[...], wpre_ref[...])
            n_ref[...] = y.astype(BF16)
            acc_ref[...] = jnp.zeros_like(acc_ref)

        if target is not None:
            dwpost_ref, sse_ref = refs[12 + nblk:14 + nblk]

            @pl.when((i == 0) & (j == 0))
            def _():
                dwpost_ref[...] = jnp.zeros_like(dwpost_ref)
                sse_ref[...] = jnp.zeros_like(sse_ref)

        n = n_ref[...]
        g = _dg(n, wg_ref[...], NT)
        u = _dg(n, wu_ref[...], NT)
        sg = _sigmoid(g)
        silu = g * sg
        p1_ref[...] = (u * (sg + silu * (1.0 - sg))).astype(BF16)
        p2_ref[...] = silu.astype(BF16)
        a = (silu * u).astype(BF16)
        a_ref[...] = a
        acc_ref[...] += _dot(a, wd_ref[...])

        @pl.when(j == nj - 1)
        def _():
            f = acc_ref[...]
            wpost = wpost_ref[...]
            y, fh, r = _rms(f, wpost)
            hout = h_ref[...] + 0.5 * y
            if target is None:
                f_ref[...] = f
                hout_ref[...] = hout
            else:
                sse = jnp.zeros((1, 1), F32)
                errs = []
                for k in range(nblk):
                    err = hout[k * BLK:(k + 1) * BLK] - t_refs[k][...]
                    if k == 0:
                        err = jnp.where(i > 0, err, 0.0)
                    errs.append(err)
                    sse = sse + jnp.sum(jnp.sum(err * err, axis=1, keepdims=True), axis=0, keepdims=True)
                dy = (jnp.concatenate(errs, axis=0) if nblk > 1 else errs[0]) * (1.0 / D_MODEL)
                hout_ref[...] = dy
                df, dw = _rms_bwd(fh, r, wpost, 0.5 * dy)
                f_ref[...] = df.astype(BF16)
                dwpost_ref[...] += dw
                sse_ref[...] += jnp.broadcast_to(sse, sse_ref.shape)

    row = pl.BlockSpec((tm, D_MODEL), lambda i, j: (i, 0))
    vec = pl.BlockSpec((1, D_MODEL), lambda i, j: (0, 0))
    act = pl.BlockSpec((None, tm, fj), lambda i, j: (j, i, 0))
    t_specs = [pl.BlockSpec((BLK, D_MODEL), functools.partial(lambda i, j, k: (jnp.maximum(nblk * i + k - 1, 0), 0), k=k))
               for k in range(nblk)]
    loss_spec = [vec, _full((1, 128))] if target is not None else []
    loss_shape = [jax.ShapeDtypeStruct((1, D_MODEL), F32), jax.ShapeDtypeStruct((1, 128), F32)] if (
        target is not None) else []
    return _pallas(
        body, name="ffn_fwd", grid=(t // tm, nj),
        in_specs=[row, vec] + wspecs + [vec] + t_specs,
        out_specs=[row, row, act, act, act, row] + loss_spec,
        out_shape=[jax.ShapeDtypeStruct((t, D_MODEL), F32), jax.ShapeDtypeStruct((t, D_MODEL), BF16),
                   jax.ShapeDtypeStruct((nj, t, fj), BF16), jax.ShapeDtypeStruct((nj, t, fj), BF16),
                   jax.ShapeDtypeStruct((nj, t, fj), BF16),
                   jax.ShapeDtypeStruct((t, D_MODEL), F32 if target is None else BF16)] + loss_shape,
        scratch_shapes=[pltpu.VMEM((tm, D_MODEL), F32)],
        args=(h, wpre, w3, w3, w3, wpost) + (target,) * nblk, hook=hook)


def _ffn_bwd(dhout, h, f, p14, p24, wpre, w3, wpost, hook=None, df=None):
    t = h.shape[0]
    tm = _row_tile(t)
    nj = w3.shape[0]
    fj, wspecs = _ffn_weight_specs(w3)
    have_df = df is not None

    def body(dhout_ref, h_ref, f_ref, p1_ref, p2_ref, wpre_ref, wg_ref, wu_ref, wd_ref, wpost_ref, *rest):
        if have_df:
            dh_ref, dg_ref, du_ref, dwpre_ref, dn_ref = rest
            df_ref = f_ref
        else:
            dh_ref, df_ref, dg_ref, du_ref, dwpre_ref, dwpost_ref, dn_ref = rest
        i = pl.program_id(0)
        j = pl.program_id(1)

        @pl.when((i == 0) & (j == 0))
        def _():
            dwpre_ref[...] = jnp.zeros_like(dwpre_ref)
            if not have_df:
                dwpost_ref[...] = jnp.zeros_like(dwpost_ref)

        @pl.when(j == 0)
        def _():
            if not have_df:
                wpost = wpost_ref[...]
                _, fh, r = _rms(f_ref[...], wpost)
                dfv, dw = _rms_bwd(fh, r, wpost, 0.5 * dhout_ref[...])
                dwpost_ref[...] += dw
                df_ref[...] = dfv.astype(BF16)
            dn_ref[...] = jnp.zeros_like(dn_ref)

        parts = _row_parts(tm)
        das = [_dg(df_ref[rows, :], wd_ref[...], NT) for rows in parts]
        for rows, da in zip(parts, das):
            dg = (da * p1_ref[rows, :].astype(F32)).astype(BF16)
            du = (da * p2_ref[rows, :].astype(F32)).astype(BF16)
            dg_ref[rows, :] = dg
            du_ref[rows, :] = du
            dn_ref[rows, :] += _dot(dg, wg_ref[...]) + _dot(du, wu_ref[...])

        @pl.when(j == nj - 1)
        def _():
            wpre = wpre_ref[...]
            _, hh, r = _rms(h_ref[...], wpre)
            dx, dw = _rms_bwd(hh, r, wpre, dn_ref[...])
            dwpre_ref[...] += dw
            dh_ref[...] = dhout_ref[...] + dx

    row = pl.BlockSpec((tm, D_MODEL), lambda i, j: (i, 0))
    vec = pl.BlockSpec((1, D_MODEL), lambda i, j: (0, 0))
    act = pl.BlockSpec((None, tm, fj), lambda i, j: (j, i, 0))
    actshape = jax.ShapeDtypeStruct((nj, t, fj), BF16)
    rowf, rowb, vecf = (jax.ShapeDtypeStruct((t, D_MODEL), F32), jax.ShapeDtypeStruct((t, D_MODEL), BF16),
                        jax.ShapeDtypeStruct((1, D_MODEL), F32))
    return _pallas(
        body, name="ffn_bwd", grid=(t // tm, nj),
        in_specs=[row, row, row, act, act, vec] + wspecs + [vec],
        out_specs=[row, act, act, vec] if have_df else [row, row, act, act, vec, vec],
        out_shape=[rowf, actshape, actshape, vecf] if have_df else [rowf, rowb, actshape, actshape, vecf, vecf],
        scratch_shapes=[pltpu.VMEM((tm, D_MODEL), F32)],
        args=(dhout, h, df if have_df else f, p14, p24, wpre, w3, w3, w3, wpost), hook=hook)


def _ffn_wgrad(n, df, dg4, du4, a4, hook=None):
    t = n.shape[0]
    tm = _contract_tile(t)
    ni = t // tm
    nj, _, fj = dg4.shape

    def body(n_ref, df_ref, dg_ref, du_ref, a_ref, dw_ref, acc):
        i = pl.program_id(1)

        @pl.when(i == 0)
        def _():
            acc[...] = jnp.zeros_like(acc)

        nn = n_ref[...]
        acc[0:fj, :] += _dg(dg_ref[...], nn, TN)
        acc[fj:2 * fj, :] += _dg(du_ref[...], nn, TN)
        acc[2 * fj:3 * fj, :] += _dg(a_ref[...], df_ref[...], TN)

        @pl.when(i == ni - 1)
        def _():
            dw_ref[...] = acc[...].astype(BF16)

    row = pl.BlockSpec((tm, D_MODEL), lambda j, i: (i, 0))
    act = pl.BlockSpec((None, tm, fj), lambda j, i: (j, i, 0))
    return _pallas(
        body, name="ffn_wgrad", grid=(nj, ni),
        in_specs=[row, row, act, act, act],
        out_specs=[pl.BlockSpec((None, 3 * fj, D_MODEL), lambda j, i: (j, 0, 0))],
        out_shape=[jax.ShapeDtypeStruct((nj, 3 * fj, D_MODEL), BF16)],
        scratch_shapes=[pltpu.VMEM((3 * fj, D_MODEL), F32)],
        args=(n, df, dg4, du4, a4), hook=hook)


def _embed_norm(x, meta, w):
    t = x.shape[0] + BLK
    tm = _row_tile(t)
    nblk = tm // BLK

    def body(*refs):
        x_refs = refs[:nblk]
        meta_ref, w_ref, h_ref, n_ref = refs[nblk:]
        i = pl.program_id(0)
        first = jnp.concatenate([jnp.zeros((PAD, D_MODEL), F32), meta_ref[...]], axis=0)
        blocks = [jnp.where(i == 0, first, x_refs[0][...])] + [r[...] for r in x_refs[1:]]
        h = jnp.concatenate(blocks, axis=0) if nblk > 1 else blocks[0]
        h_ref[...] = h
        y, _, _ = _rms(h, w_ref[...])
        n_ref[...] = y.astype(BF16)

    x_specs = [pl.BlockSpec((BLK, D_MODEL), functools.partial(lambda i, k: (jnp.maximum(nblk * i + k - 1, 0), 0), k=k))
               for k in range(nblk)]
    row = pl.BlockSpec((tm, D_MODEL), lambda i: (i, 0))
    return pl.pallas_call(
        body, name="embed_norm", grid=(t // tm,),
        in_specs=x_specs + [_full((N_META, D_MODEL)), _full((1, D_MODEL))], out_specs=[row, row],
        out_shape=[jax.ShapeDtypeStruct((t, D_MODEL), F32), jax.ShapeDtypeStruct((t, D_MODEL), BF16)],
        compiler_params=_cparams(1),
    )(*([x] * nblk), meta, w)


FWD_RELATION = (None, 0, 1, 2)


def _ffn_fwd_gather(h, n, wbuf, wpost, qc_idx, late):
    t = h.shape[0]
    tm = _row_tile(t)
    ni = t // tm
    nj, rows3, _ = wbuf.shape
    fj = rows3 // 3
    assert nj == N_CHIPS and ni >= 4
    wbufs = [wbuf]
    nw = 1
    n_lin, n_lout = len(late.inputs), len(late.out_shape)
    wait_step = ni - 3

    def body(qc_ref, h_ref, n_ref, wpost_ref, *rest):
        wb_in = rest[:nw]
        lins = rest[nw:nw + n_lin]
        o0 = nw + n_lin
        hout_ref, p1_ref, p2_ref, a_ref, f_hbm = rest[o0:o0 + 5]
        wb = rest[o0 + 5:o0 + 5 + nw]
        louts = rest[o0 + 5 + nw:o0 + 5 + nw + n_lout]
        s0 = o0 + 5 + nw + n_lout
        wv, wsem, send, recv, fbuf, fr_sem, fw_sem = rest[s0:s0 + 7]
        lscr = rest[s0 + 7:]
        p = pl.program_id(0)
        i = pl.program_id(1)
        step = p * ni + i
        fslot = step % 3
        nslot = (step + 1) % 3

        def f_tile(tile):
            return f_hbm.at[pl.ds(pl.multiple_of(tile * tm, 8), tm)]

        @pl.when(step > 1)
        def _():
            pltpu.make_async_copy(fbuf.at[nslot], f_tile(i), fw_sem.at[nslot]).wait()

        nxt = step + 1

        @pl.when((nxt < N_CHIPS * ni) & (nxt >= ni))
        def _():
            pltpu.make_async_copy(f_tile(nxt % ni), fbuf.at[nslot], fr_sem.at[nslot]).start()

        @pl.when(p > 0)
        def _():
            pltpu.make_async_copy(f_tile(i), fbuf.at[fslot], fr_sem.at[fslot]).wait()
        x, y, c, chips = _place()
        q = 2 * x + y
        sibling = (x, y, 1 - c)
        mine, other = _half(rows3, c), _half(rows3, 1 - c)

        def load(chunk, slot, src):
            return [pltpu.make_async_copy(src[t].at[chunk], wv.at[slot, t], wsem.at[slot, t]) for t in range(nw)]

        @pl.when((p == 0) & (i == 0))
        def _():
            for j, (cx, cy) in enumerate(chips):
                for t in range(nw):
                    _remote(send.at[t, j], recv.at[t, j], wb_in[t].at[q, mine], wb[t].at[q, mine], (cx, cy, c)).start()
            for cp in load(q, 0, wb_in):
                cp.start()
            for cp in load(q, 0, wb_in):
                cp.wait()

        @pl.when((p == 1) & (i == 0))
        def _():
            late.start(lins, louts, lscr)

        for pp in range(1, N_CHIPS):
            j = FWD_RELATION[pp]
            cx, cy = chips[j]
            chunk = 2 * cx + cy

            @pl.when((p == pp - 1) & (i == wait_step))
            def _(j=j, cx=cx, cy=cy, chunk=chunk, pp=pp):
                for t in range(nw):
                    got = wb[t].at[chunk, mine]
                    _remote(send.at[t, j], recv.at[t, j], got, got, (cx, cy, c)).wait_recv()
                    _remote(send.at[t, 3 + j], recv.at[t, 3 + j], got, got, sibling).start()
                for t in range(nw):
                    rest_half = wb[t].at[chunk, other]
                    _remote(send.at[t, 3 + j], recv.at[t, 3 + j], rest_half, rest_half, sibling).wait_recv()
                for cp in load(chunk, pp % 2, wb):
                    cp.start()

            @pl.when((p == pp) & (i == 0))
            def _(chunk=chunk, pp=pp):
                for cp in load(chunk, pp % 2, wb):
                    cp.wait()

        @pl.when((p == N_CHIPS - 1) & (i == ni // 2))
        def _():
            late.mid(lins, louts, lscr)

        slot = p % 2
        nn = n_ref[...]
        g = _dg(nn, wv[slot, 0, 0:fj], NT)
        u = _dg(nn, wv[slot, 0, fj:2 * fj], NT)
        sg = _sigmoid(g)
        silu = g * sg
        p1_ref[...] = (u * (sg + silu * (1.0 - sg))).astype(BF16)
        p2_ref[...] = silu.astype(BF16)
        a = (silu * u).astype(BF16)
        a_ref[...] = a
        part = _dot(a, wv[slot, 0, 2 * fj:3 * fj])

        @pl.when(p == 0)
        def _():
            fbuf[fslot] = part

        @pl.when(p > 0)
        def _():
            fbuf[fslot] = fbuf[fslot] + part

        pltpu.make_async_copy(fbuf.at[fslot], f_tile(i), fw_sem.at[fslot]).start()

        @pl.when(p == N_CHIPS - 1)
        def _():
            yv, _, _ = _rms(fbuf[fslot], wpost_ref[...])
            hout_ref[...] = h_ref[...] + 0.5 * yv

        @pl.when((p == N_CHIPS - 1) & (i == ni - 1))
        def _():
            pslot = (step + 2) % 3
            pltpu.make_async_copy(fbuf.at[pslot], f_tile(i), fw_sem.at[pslot]).wait()
            pltpu.make_async_copy(fbuf.at[fslot], f_tile(i), fw_sem.at[fslot]).wait()
            for t in range(nw):
                for j, (cx, cy) in enumerate(chips):
                    sent = wb[t].at[2 * cx + cy, mine]
                    _remote(send.at[t, j], recv.at[t, j], sent, sent, (cx, cy, c)).wait_send()
                    _remote(send.at[t, 3 + j], recv.at[t, 3 + j], sent, sent, sibling).wait_send()
            late.finish(lins, louts, lscr)

    def last_pass_rows(p, i, qc_ref):
        return (jnp.where(p == N_CHIPS - 1, i, 0), 0)

    def chunk_rows(p, i, qc_ref):
        order = ((p & 1) << 1) | (p >> 1)
        return (jnp.bitwise_xor(qc_ref[0], order), i, 0)

    row = pl.BlockSpec((tm, D_MODEL), lambda p, i, qc_ref: (i, 0))
    last_row = pl.BlockSpec((tm, D_MODEL), last_pass_rows)
    act = pl.BlockSpec((None, tm, fj), chunk_rows)
    act_shape = jax.ShapeDtypeStruct((nj, t, fj), BF16)
    res = pl.pallas_call(
        body, name="ffn_fwd_gather",
        grid_spec=pltpu.PrefetchScalarGridSpec(
            num_scalar_prefetch=1, grid=(N_CHIPS, ni),
            in_specs=[last_row, row, pl.BlockSpec((1, D_MODEL), lambda p, i, qc_ref: (0, 0))]
            + [ANY] * (nw + n_lin),
            out_specs=[last_row, act, act, act, ANY] + [ANY] * (nw + n_lout),
            scratch_shapes=[pltpu.VMEM((2, nw, rows3, D_MODEL), BF16), pltpu.SemaphoreType.DMA((2, nw)),
                            pltpu.SemaphoreType.DMA((nw, 6)), pltpu.SemaphoreType.DMA((nw, 6)),
                            pltpu.VMEM((3, tm, D_MODEL), F32), pltpu.SemaphoreType.DMA((3,)),
                            pltpu.SemaphoreType.DMA((3,))] + list(late.scratch)),
        out_shape=[jax.ShapeDtypeStruct((t, D_MODEL), F32), act_shape, act_shape, act_shape,
                   jax.ShapeDtypeStruct((t, D_MODEL), F32)]
        + [jax.ShapeDtypeStruct(b.shape, b.dtype) for b in wbufs] + list(late.out_shape),
        input_output_aliases={**{4 + t: 5 + t for t in range(nw)},
                              **{4 + nw + a: 5 + nw + b for a, b in late.aliases}},
        compiler_params=_cparams(2),
    )(qc_idx, h, n, wpost, *wbufs, *late.inputs)
    return res[:5], res[5:5 + nw], res[5 + nw:]


PASS_RELATION = (2, 0, 1)


def _ffn_wgrad_reduce(n, df, dg4, du4, a4, qc_idx, hook):
    t = n.shape[0]
    tm = _contract_tile(t)
    ni = t // tm
    nj, _, fj = dg4.shape
    assert nj == N_CHIPS
    hrows = 3 * fj // 2
    n_hin, n_hout = len(hook.inputs), len(hook.out_shape)

    def body(qc_ref, n_ref, df_ref, dg_ref, du_ref, a_ref, *rest):
        hins = rest[:n_hin]
        own_ref, others_ref = rest[n_hin:n_hin + 2]
        houts = rest[n_hin + 2:n_hin + 2 + n_hout]
        s0 = n_hin + 2 + n_hout
        acc, stage, land, sumbuf, px_send, px_recv, cs_send, cs_recv, own_sem = rest[s0:s0 + 9]
        hscr = rest[s0 + 9:]
        k_pass = pl.program_id(0)
        i = pl.program_id(1)
        x, y, c, chips = _place()
        mine = pl.ds(pl.multiple_of(c * hrows, 8), hrows)
        other = pl.ds(pl.multiple_of((1 - c) * hrows, 8), hrows)

        def to_owner(k):
            j = PASS_RELATION[k]
            return _remote(cs_send.at[j], cs_recv.at[j], sumbuf.at[k % 2], others_ref.at[j], (*chips[j], c))

        @pl.when((k_pass == 0) & (i == 0))
        def _():
            hook.start(hins, houts, hscr)

        @pl.when(i == 0)
        def _():
            acc[...] = jnp.zeros_like(acc)

        nn = n_ref[...]
        acc[0:fj, :] += _dg(dg_ref[...], nn, TN)
        acc[fj:2 * fj, :] += _dg(du_ref[...], nn, TN)
        acc[2 * fj:3 * fj, :] += _dg(a_ref[...], df_ref[...], TN)

        for k in range(N_CHIPS):
            @pl.when((k_pass == k) & (i == ni - 1))
            def _(k=k):
                slot = k % 2
                stage[...] = acc[other, :].astype(BF16)
                swap = _remote(px_send.at[k], px_recv.at[k], stage, land.at[slot], (x, y, 1 - c))
                swap.start()
                swap.wait_recv()
                pair = acc[mine, :] + land[slot].astype(F32)
                if k >= 2:
                    to_owner(k - 2).wait_send()
                sumbuf[slot] = pair.astype(BF16)
                swap.wait_send()
                if k < N_CHIPS - 1:
                    to_owner(k).start()
                else:
                    keep = pltpu.make_async_copy(sumbuf.at[slot], own_ref, own_sem)
                    keep.start()
                    for j in range(N_CHIPS - 1):
                        _remote(cs_send.at[j], cs_recv.at[j], sumbuf.at[0], others_ref.at[j], (*chips[j], c)).wait_recv()
                    to_owner(k - 1).wait_send()
                    keep.wait()
                    hook.finish(hins, houts, hscr)

    def chunk(k_pass, i, qc_ref):
        return (jnp.bitwise_xor(qc_ref[0], N_CHIPS - 1 - k_pass), i, 0)

    row = pl.BlockSpec((tm, D_MODEL), lambda k_pass, i, qc_ref: (i, 0))
    act = pl.BlockSpec((None, tm, fj), chunk)
    res = pl.pallas_call(
        body, name="ffn_wgrad_reduce",
        grid_spec=pltpu.PrefetchScalarGridSpec(
            num_scalar_prefetch=1, grid=(N_CHIPS, ni),
            in_specs=[row, row, act, act, act] + [ANY] * n_hin,
            out_specs=[ANY, ANY] + [ANY] * n_hout,
            scratch_shapes=[pltpu.VMEM((3 * fj, D_MODEL), F32), pltpu.VMEM((hrows, D_MODEL), BF16),
                            pltpu.VMEM((2, hrows, D_MODEL), BF16), pltpu.VMEM((2, hrows, D_MODEL), BF16),
                            pltpu.SemaphoreType.DMA((N_CHIPS,)), pltpu.SemaphoreType.DMA((N_CHIPS,)),
                            pltpu.SemaphoreType.DMA((N_CHIPS - 1,)), pltpu.SemaphoreType.DMA((N_CHIPS - 1,)),
                            pltpu.SemaphoreType.DMA] + list(hook.scratch)),
        out_shape=[jax.ShapeDtypeStruct((hrows, D_MODEL), BF16),
                   jax.ShapeDtypeStruct((N_CHIPS - 1, hrows, D_MODEL), BF16)] + list(hook.out_shape),
        compiler_params=_cparams(2),
    )(qc_idx, n, df, dg4, du4, a4, *hook.inputs)
    return res[0], res[1], res[2:]


def _xty(x, y):
    t, k = x.shape
    n = y.shape[1]
    tm = _contract_tile(t)
    tn = n if n <= 1024 else (896 if n % 896 == 0 else 128)

    def body(x_ref, y_ref, o_ref):
        @pl.when(pl.program_id(1) == 0)
        def _():
            o_ref[...] = jnp.zeros_like(o_ref)

        o_ref[...] += _dg(x_ref[...], y_ref[...], TN)

    return pl.pallas_call(
        body, name="xty", grid=(n // tn, t // tm),
        in_specs=[pl.BlockSpec((tm, k), lambda j, i: (i, 0)), pl.BlockSpec((tm, tn), lambda j, i: (i, j))],
        out_specs=pl.BlockSpec((k, tn), lambda j, i: (0, j)),
        out_shape=jax.ShapeDtypeStruct((k, n), F32),
        compiler_params=_cparams(2),
    )(x, y)


def _rope_tables(t):
    pos = (jnp.arange(t, dtype=jnp.int32) - PAD).astype(F32)
    inv_freq = 1.0 / (ROPE_THETA ** (jnp.arange(0, SWA_HD, 2, dtype=F32) / SWA_HD))
    ang = pos[:, None] * inv_freq[None, :]
    cos = jnp.cos(ang)
    sin = jnp.sin(ang)
    return jnp.concatenate([cos, cos, cos, cos], axis=1), jnp.concatenate([-sin, sin, -sin, sin], axis=1)


def _rot_half(x, first_half):
    return jnp.where(first_half, pltpu.roll(x, 96, 1), pltpu.roll(x, 32, 1))


def _first_half_mask(rows):
    lane = lax.broadcasted_iota(jnp.int32, (rows, 128), 1)
    return (lane % 64) < 32


def _log_sigmoid(z):
    return jnp.minimum(z, 0.0) - jnp.log(1.0 + jnp.exp(-jnp.abs(z)))


def _mix_proj(h1, wmixpre, winp, wa2p, bap, cos, sin):
    t = h1.shape[0]
    tm = _row_tile(t)

    def body(h_ref, w_ref, win_ref, wa2_ref, ba_ref, cos_ref, sin_ref,
             n_ref, gq_ref, gk_ref, gv_ref, gg_ref, ga_ref, la_ref, sq_ref, sk_ref, sv_ref):
        y, _, _ = _rms(h_ref[...], w_ref[...])
        n = y.astype(BF16)
        n_ref[...] = n
        proj = _dot(n, win_ref[...])
        gq_ref[...] = proj[:, P_GQ:P_GK]
        gk_ref[...] = proj[:, P_GK:P_GV]
        gv_ref[...] = proj[:, P_GV:P_GG]
        gg_ref[...] = proj[:, P_GG:P_GA]
        ga = proj[:, P_GA:P_SQ]
        ga_ref[...] = ga
        z = _dot(ga.astype(BF16), wa2_ref[...]) + ba_ref[...]
        la_ref[...] = _log_sigmoid(z) * (1.0 / GLA_TAU)
        c = cos_ref[...]
        s = sin_ref[...]
        fh = _first_half_mask(tm)
        for k in range(4):
            x = proj[:, P_SQ + 128 * k:P_SQ + 128 * (k + 1)]
            sq_ref[:, 128 * k:128 * (k + 1)] = (x * c + _rot_half(x, fh) * s).astype(BF16)
        for k in range(2):
            x = proj[:, P_SK + 128 * k:P_SK + 128 * (k + 1)]
            sk_ref[:, 128 * k:128 * (k + 1)] = (x * c + _rot_half(x, fh) * s).astype(BF16)
        sv_ref[...] = proj[:, P_SV:P_END].astype(BF16)

    def row(w):
        return pl.BlockSpec((tm, w), lambda i: (i, 0))

    def rshape(w, dt):
        return jax.ShapeDtypeStruct((t, w), dt)

    return pl.pallas_call(
        body, name="mix_proj", grid=(t // tm,),
        in_specs=[row(D_MODEL), _full((1, D_MODEL)), _full((D_MODEL, P_END)), _full((128, GLA_KW)),
                  _full((1, GLA_KW)), row(128), row(128)],
        out_specs=[row(D_MODEL), row(256), row(256), row(512), row(512), row(128), row(256), row(512), row(256),
                   row(256)],
        out_shape=[rshape(D_MODEL, BF16), rshape(256, F32), rshape(256, F32), rshape(512, F32), rshape(512, F32),
                   rshape(128, F32), rshape(256, F32), rshape(512, BF16), rshape(256, BF16), rshape(256, BF16)],
        compiler_params=_cparams(1),
    )(h1, wmixpre, winp, wa2p, bap, cos, sin)


def _scan_rows(x, reverse=False):
    n = x.shape[0]
    row = lax.broadcasted_iota(jnp.int32, x.shape, 0)
    s = 1
    while s < n:
        if reverse:
            x = x + jnp.where(row < n - s, pltpu.roll(x, n - s, 0), 0.0)
        else:
            x = x + jnp.where(row >= s, pltpu.roll(x, s, 0), 0.0)
        s *= 2
    return x


def _gla_cumsum(la, tril_f):
    b = _scan_rows(la)
    row = lax.broadcasted_iota(jnp.int32, b.shape, 0)
    bm = jnp.sum(jnp.where(row == GLA_CHUNK // 2 - 1, b, 0.0), axis=0, keepdims=True)
    bl = jnp.sum(jnp.where(row == GLA_CHUNK - 1, b, 0.0), axis=0, keepdims=True)
    return b, bm, bl


def _gla_decays(la, tril_f):
    b, bm, bl = _gla_cumsum(la, tril_f)
    return jnp.exp(b - bm), jnp.exp(bm - b), jnp.exp(b), jnp.exp(bl - b), jnp.exp(bl)


def _gla_masks():
    c = GLA_CHUNK
    r = lax.broadcasted_iota(jnp.int32, (c, c), 0)
    col = lax.broadcasted_iota(jnp.int32, (c, c), 1)
    r4 = lax.broadcasted_iota(jnp.int32, (GLA_HEADS * c, c), 0) % c
    c4 = lax.broadcasted_iota(jnp.int32, (GLA_HEADS * c, c), 1)
    klane = lax.broadcasted_iota(jnp.int32, (c, GLA_KW), 1) // GLA_DK
    vlane = lax.broadcasted_iota(jnp.int32, (c, GLA_W), 1) // GLA_DV
    srow = lax.broadcasted_iota(jnp.int32, (GLA_W, GLA_KW), 0) // GLA_DV
    scol = lax.broadcasted_iota(jnp.int32, (GLA_W, GLA_KW), 1) // GLA_DK
    return dict(tril_f=(r >= col).astype(F32), triu_f=(r <= col).astype(F32), tril4=r4 >= c4,
                khead=[klane == h for h in range(GLA_HEADS)], vhead=[vlane == h for h in range(GLA_HEADS)],
                diag=srow == scol)


def _stack_heads(x, head_masks):
    return jnp.concatenate([jnp.where(m, x, 0.0) for m in head_masks], axis=0)


def _gla_fwd(gq, gk, gv, la):
    t = gq.shape[0]
    rg = _seq_tile(t)
    nb = t // rg
    ncb = rg // GLA_CHUNK
    c = GLA_CHUNK

    def body(q_ref, k_ref, v_ref, la_ref, o_ref, ss_ref, st_ref):
        @pl.when(pl.program_id(0) == 0)
        def _():
            st_ref[...] = jnp.zeros_like(st_ref)

        mk = _gla_masks()
        st = st_ref[...]
        for ch in range(ncb):
            rows = slice(ch * c, (ch + 1) * c)
            eq, ek, eb, ekl, ebl = _gla_decays(la_ref[rows, :], mk["tril_f"])
            qs = q_ref[rows, :] * (GLA_DK ** -0.5)
            k = k_ref[rows, :]
            v = v_ref[rows, :].astype(BF16)
            ss_ref[ch] = st
            q4 = _stack_heads(qs * eq, mk["khead"]).astype(BF16)
            a4 = jnp.where(mk["tril4"], _dg(q4, (k * ek).astype(BF16), NT), 0.0).astype(BF16)
            r4 = _dot(a4, v)
            intra = jnp.concatenate([r4[h * c:(h + 1) * c, GLA_DV * h:GLA_DV * (h + 1)] for h in range(GLA_HEADS)],
                                    axis=1)
            o_ref[rows, :] = intra + _dg((qs * eb).astype(BF16), st.astype(BF16), NT)
            st = st * ebl + jnp.where(mk["diag"], _dg(v, (k * ekl).astype(BF16), TN), 0.0)
        st_ref[...] = st

    def row(w):
        return pl.BlockSpec((rg, w), lambda i: (i, 0))

    return pl.pallas_call(
        body, name="gla_fwd", grid=(nb,),
        in_specs=[row(256), row(256), row(512), row(256)],
        out_specs=[row(512), pl.BlockSpec((ncb, GLA_W, GLA_KW), lambda i: (i, 0, 0))],
        out_shape=[jax.ShapeDtypeStruct((t, GLA_W), F32), jax.ShapeDtypeStruct((nb * ncb, GLA_W, GLA_KW), F32)],
        scratch_shapes=[pltpu.VMEM((GLA_W, GLA_KW), F32)],
        compiler_params=_cparams(1),
    )(gq, gk, gv, la)


def _gla_bwd(gq, gk, gv, la, ss, do):
    t = gq.shape[0]
    rg = _seq_tile(t)
    nb = t // rg
    ncb = rg // GLA_CHUNK
    c = GLA_CHUNK

    def body(q_ref, k_ref, v_ref, la_ref, ss_ref, do_ref, dq_ref, dk_ref, dv_ref, dla_ref, dst_ref):
        @pl.when(pl.program_id(0) == 0)
        def _():
            dst_ref[...] = jnp.zeros_like(dst_ref)

        mk = _gla_masks()
        last_row = lax.broadcasted_iota(jnp.int32, (c, GLA_KW), 0) == c - 1
        scale = GLA_DK ** -0.5
        dstn = dst_ref[...]
        for ch in reversed(range(ncb)):
            rows = slice(ch * c, (ch + 1) * c)
            eq, ek, eb, ekl, ebl = _gla_decays(la_ref[rows, :], mk["tril_f"])
            qs = q_ref[rows, :] * scale
            k = k_ref[rows, :]
            qt, kt, qh, kh = qs * eq, k * ek, qs * eb, k * ekl
            ktb, khb, qhb = kt.astype(BF16), kh.astype(BF16), qh.astype(BF16)
            v = v_ref[rows, :].astype(BF16)
            do_f = do_ref[rows, :]
            dob = do_f.astype(BF16)
            st = ss_ref[ch]
            stb = st.astype(BF16)
            dstb = dstn.astype(BF16)
            q4 = _stack_heads(qt, mk["khead"]).astype(BF16)
            do4 = _stack_heads(do_f, mk["vhead"]).astype(BF16)
            a4 = jnp.where(mk["tril4"], _dg(q4, ktb, NT), 0.0).astype(BF16)
            da4 = jnp.where(mk["tril4"], _dg(do4, v, NT), 0.0).astype(BF16)
            dv_ref[rows, :] = _dg(a4, do4, TN) + _dg(khb, dstb, NT)
            dq4 = _dot(da4, ktb)
            dqt = jnp.zeros((c, GLA_KW), F32)
            for h in range(GLA_HEADS):
                dqt = dqt + jnp.where(mk["khead"][h], dq4[h * c:(h + 1) * c], 0.0)
            dkt = _dg(da4, q4, TN)
            dqh = _dot(dob, stb)
            dkh = _dot(v, dstb)
            dbl = jnp.sum(dstn * st, axis=0, keepdims=True)
            dstn = dstn * ebl + jnp.where(mk["diag"], _dg(dob, qhb, TN), 0.0)
            dq_ref[rows, :] = scale * (dqt * eq + dqh * eb)
            dk_ref[rows, :] = dkt * ek + dkh * ekl
            dkk = dkh * kh
            db = dqt * qt - dkt * kt + dqh * qh - dkk
            db = db + jnp.where(last_row, jnp.sum(dkk, axis=0, keepdims=True) + ebl * dbl, 0.0)
            dla_ref[rows, :] = _scan_rows(db, reverse=True)
        dst_ref[...] = dstn

    def row(w):
        return pl.BlockSpec((rg, w), lambda i: (nb - 1 - i, 0))

    def rshape(w):
        return jax.ShapeDtypeStruct((t, w), F32)

    return pl.pallas_call(
        body, name="gla_bwd", grid=(nb,),
        in_specs=[row(256), row(256), row(512), row(256),
                  pl.BlockSpec((ncb, GLA_W, GLA_KW), lambda i: (nb - 1 - i, 0, 0)), row(512)],
        out_specs=[row(256), row(256), row(512), row(256)],
        out_shape=[rshape(256), rshape(256), rshape(512), rshape(256)],
        scratch_shapes=[pltpu.VMEM((GLA_W, GLA_KW), F32)],
        compiler_params=_cparams(1),
    )(gq, gk, gv, la, ss, do)


SWA_G = SWA_QH // SWA_KVH


def _swa_bias():
    n = jnp.arange(3, dtype=jnp.int32)[:, None, None]
    r = (jnp.arange(SWA_G * BLK, dtype=jnp.int32) % BLK)[None, :, None]
    c = jnp.arange(3 * BLK, dtype=jnp.int32)[None, None, :]
    seg = c // BLK
    cc = c % BLK
    qpos = n * BLK + r - PAD
    kpos = jnp.where(seg == 0, (n - 1) * BLK, jnp.where(seg == 1, n * BLK, 0)) + cc - PAD
    band = (seg < 2) & (kpos >= N_META) & (kpos <= qpos) & (qpos - kpos < WINDOW)
    meta = (seg == 2) & (kpos >= 0) & (kpos < N_META) & (kpos <= qpos)
    return jnp.where(band | meta, 0.0, NEG_INF).astype(F32)


def _swa_stack(ref, rows, kh, lo, dtype):
    parts = []
    for g in range(2):
        pair = ref[rows, 128 * (2 * kh + g):128 * (2 * kh + g + 1)]
        zero = jnp.zeros_like(pair)
        parts += [jnp.where(lo, pair, zero), jnp.where(lo, zero, pair)]
    return jnp.concatenate(parts, axis=0).astype(dtype)


def _swa_unstack(x4, lo):
    return [jnp.where(lo, x4[2 * g * BLK:(2 * g + 1) * BLK], x4[(2 * g + 1) * BLK:(2 * g + 2) * BLK])
            for g in range(2)]


def _swa_sink_col(sink_ref, kh):
    blk = lax.broadcasted_iota(jnp.int32, (SWA_G * BLK, 1), 0) // BLK
    col = jnp.full((SWA_G * BLK, 1), sink_ref[SWA_G * kh + SWA_G - 1], F32)
    for e in reversed(range(SWA_G - 1)):
        col = jnp.where(blk == e, sink_ref[SWA_G * kh + e], col)
    return col


def _swa_softmax(qk, bias, sink):
    s = qk * (SWA_HD ** -0.5) + bias
    m = jnp.maximum(jnp.max(s, axis=-1, keepdims=True), sink)
    p = jnp.exp(s - m)
    es = jnp.exp(sink - m)
    inv = 1.0 / (jnp.sum(p, axis=-1, keepdims=True) + es)
    return p * inv, es * inv


def _swa_keys(prev_ref, cur_ref, first_ref, b, ls):
    before = prev_ref[:, ls] if b == 0 else cur_ref[(b - 1) * BLK:b * BLK, ls]
    return jnp.concatenate([before, cur_ref[b * BLK:(b + 1) * BLK, ls], first_ref[:, ls]], axis=0)


def _swa_specs(rs, ns):
    bps = rs // BLK
    cur = lambda w: pl.BlockSpec((rs, w), lambda i: (jnp.minimum(i, ns - 1), 0))
    prev = lambda w: pl.BlockSpec((BLK, w), lambda i: (jnp.maximum(jnp.minimum(i, ns - 1) * bps - 1, 0), 0))
    first = lambda w: pl.BlockSpec((BLK, w), lambda i: (0, 0))
    return cur, prev, first


def _swa_fwd(sinks, sq, sk, sv):
    t = sq.shape[0]
    rs = _seq_tile(t)
    bps, ns = rs // BLK, t // rs

    def body(sink_ref, bias_ref, q_ref, kp_ref, kc_ref, km_ref, vp_ref, vc_ref, vm_ref, o_ref):
        i = pl.program_id(0)
        lo = lax.broadcasted_iota(jnp.int32, (BLK, 128), 1) < 64
        sink_cols = [_swa_sink_col(sink_ref, kh) for kh in range(SWA_KVH)]
        chains = [(b, kh) for b in range(bps) for kh in range(SWA_KVH)]
        scores = []
        for b, kh in chains:
            ls = slice(128 * kh, 128 * (kh + 1))
            q4 = _swa_stack(q_ref, slice(b * BLK, (b + 1) * BLK), kh, lo, BF16)
            scores.append(_dg(q4, _swa_keys(kp_ref, kc_ref, km_ref, b, ls), NT))
        probs = []
        for (b, kh), s in zip(chains, scores):
            p, _ = _swa_softmax(s, bias_ref[jnp.minimum(i * bps + b, 2)], sink_cols[kh])
            probs.append(p.astype(BF16))
        for (b, kh), p in zip(chains, probs):
            ls = slice(128 * kh, 128 * (kh + 1))
            rows = slice(b * BLK, (b + 1) * BLK)
            for g, pair in enumerate(_swa_unstack(_dot(p, _swa_keys(vp_ref, vc_ref, vm_ref, b, ls)), lo)):
                o_ref[rows, 128 * (2 * kh + g):128 * (2 * kh + g + 1)] = pair

    cur, prev, first = _swa_specs(rs, ns)
    bias = _swa_bias()
    return pl.pallas_call(
        body, name="swa_fwd", grid=(ns,),
        in_specs=[pl.BlockSpec(memory_space=pltpu.SMEM), _full(bias.shape), cur(512), prev(256), cur(256), first(256),
                  prev(256), cur(256), first(256)],
        out_specs=cur(512),
        out_shape=jax.ShapeDtypeStruct((t, SWA_W), F32),
        compiler_params=_cparams(1),
    )(sinks, bias, sq, sk, sk, sk, sv, sv, sv)


def _swa_bwd(sinks, sq, sk, sv, o, do, hook=None):
    t = sq.shape[0]
    rs = _seq_tile(t)
    bps, ns = rs // BLK, t // rs

    def body(sink_ref, bias_ref, q_ref, kp_ref, kc_ref, km_ref, vp_ref, vc_ref, vm_ref, o_ref, do_ref,
             dq_ref, dk_ref, dv_ref, dkm_ref, dvm_ref, dsink_ref, pk_ref, pv_ref):
        i = pl.program_id(0)

        @pl.when(i == 0)
        def _():
            pk_ref[...] = jnp.zeros_like(pk_ref)
            pv_ref[...] = jnp.zeros_like(pv_ref)
            dkm_ref[...] = jnp.zeros_like(dkm_ref)
            dvm_ref[...] = jnp.zeros_like(dvm_ref)
            dsink_ref[...] = jnp.zeros_like(dsink_ref)

        @pl.when(i == ns)
        def _():
            dk_ref[...] = pk_ref[...]
            dv_ref[...] = pv_ref[...]

        @pl.when(i < ns)
        def _():
            lo = lax.broadcasted_iota(jnp.int32, (BLK, 128), 1) < 64
            scale = SWA_HD ** -0.5
            sink_cols = [_swa_sink_col(sink_ref, kh) for kh in range(SWA_KVH)]
            parts_k = [[None] * SWA_KVH for _ in range(bps)]
            parts_v = [[None] * SWA_KVH for _ in range(bps)]
            dsinks = [jnp.zeros((1, 1), F32) for _ in range(SWA_QH)]
            chains = [(b, kh) for b in range(bps) for kh in range(SWA_KVH)]
            lanes = lambda kh: slice(128 * kh, 128 * (kh + 1))
            block = lambda b: slice(b * BLK, (b + 1) * BLK)
            q4s = [_swa_stack(q_ref, block(b), kh, lo, BF16) for b, kh in chains]
            scores = [_dg(q4, _swa_keys(kp_ref, kc_ref, km_ref, b, lanes(kh)), NT)
                      for (b, kh), q4 in zip(chains, q4s)]
            do4s = [_swa_stack(do_ref, block(b), kh, lo, F32) for b, kh in chains]
            do4bs = [d.astype(BF16) for d in do4s]
            dps = [_dg(d, _swa_keys(vp_ref, vc_ref, vm_ref, b, lanes(kh)), NT) for (b, kh), d in zip(chains, do4bs)]
            pbs, dss = [], []
            for n_chain, (b, kh) in enumerate(chains):
                p, psink = _swa_softmax(scores[n_chain], bias_ref[jnp.minimum(i * bps + b, 2)], sink_cols[kh])
                delta = jnp.sum(do4s[n_chain] * _swa_stack(o_ref, block(b), kh, lo, F32), axis=-1, keepdims=True)
                dss.append((p * (dps[n_chain] - delta) * scale).astype(BF16))
                pbs.append(p.astype(BF16))
                dsk = psink * delta
                for e in range(SWA_G):
                    h = SWA_G * kh + e
                    dsinks[h] = dsinks[h] - jnp.sum(dsk[e * BLK:(e + 1) * BLK], axis=0, keepdims=True)
            for n_chain, (b, kh) in enumerate(chains):
                kall = _swa_keys(kp_ref, kc_ref, km_ref, b, lanes(kh))
                for g, pair in enumerate(_swa_unstack(_dot(dss[n_chain], kall), lo)):
                    dq_ref[block(b), 128 * (2 * kh + g):128 * (2 * kh + g + 1)] = pair
                parts_k[b][kh] = _dg(dss[n_chain], q4s[n_chain], TN)
                parts_v[b][kh] = _dg(pbs[n_chain], do4bs[n_chain], TN)
            last = slice(rs - BLK, rs)
            for parts, out_ref, pend_ref, meta_ref in ((parts_k, dk_ref, pk_ref, dkm_ref),
                                                       (parts_v, dv_ref, pv_ref, dvm_ref)):
                for kh in range(SWA_KVH):
                    ls = slice(128 * kh, 128 * (kh + 1))
                    if bps > 1:
                        out_ref[0:rs - BLK, ls] = pend_ref[0:rs - BLK, ls]
                    out_ref[last, ls] = pend_ref[last, ls] + parts[0][kh][0:BLK]
                    meta = parts[0][kh][2 * BLK:3 * BLK]
                    for b in range(bps):
                        own = parts[b][kh][BLK:2 * BLK]
                        if b + 1 < bps:
                            own = own + parts[b + 1][kh][0:BLK]
                            meta = meta + parts[b + 1][kh][2 * BLK:3 * BLK]
                        pend_ref[b * BLK:(b + 1) * BLK, ls] = own
                    meta_ref[:, ls] += meta
            for h in range(SWA_QH):
                dsink_ref[h:h + 1, :] += jnp.broadcast_to(dsinks[h], (1, 128))

    cur, prev, first = _swa_specs(rs, ns)
    late = lambda w: pl.BlockSpec((rs, w), lambda i: (jnp.maximum(i - 1, 0), 0))
    bias = _swa_bias()
    return _pallas(
        body, name="swa_bwd", grid=(ns + 1,),
        in_specs=[pl.BlockSpec(memory_space=pltpu.SMEM), _full(bias.shape), cur(512), prev(256), cur(256), first(256),
                  prev(256), cur(256), first(256), cur(512), cur(512)],
        out_specs=[cur(512), late(256), late(256), first(256), first(256), _full((SWA_QH, 128))],
        out_shape=[jax.ShapeDtypeStruct((t, SWA_W), F32), jax.ShapeDtypeStruct((t, 256), F32),
                   jax.ShapeDtypeStruct((t, 256), F32), jax.ShapeDtypeStruct((BLK, 256), F32),
                   jax.ShapeDtypeStruct((BLK, 256), F32), jax.ShapeDtypeStruct((SWA_QH, 128), F32)],
        scratch_shapes=[pltpu.VMEM((rs, 256), F32), pltpu.VMEM((rs, 256), F32)],
        args=(sinks, bias, sq, sk, sk, sk, sv, sv, sv, o, do), hook=hook)


def _mix_out(h1, ogla, gg, oswa, wgn, wsn, wout, wpost):
    t = h1.shape[0]
    tm = _row_tile(t)

    def body(h_ref, og_ref, gg_ref, os_ref, wgn_ref, wsn_ref, wout_ref, wpost_ref, h2_ref, cat_ref, m_ref):
        parts = []
        for h in range(GLA_HEADS):
            ls = slice(GLA_DV * h, GLA_DV * (h + 1))
            y, _, _ = _rms(og_ref[:, ls], wgn_ref[...])
            g = gg_ref[:, ls]
            parts.append(y * (g * _sigmoid(g)))
        ys, _, _ = _rms(os_ref[...], wsn_ref[...])
        cat = jnp.concatenate(parts + [ys], axis=1).astype(BF16)
        cat_ref[...] = cat
        m = _dot(cat, wout_ref[...])
        m_ref[...] = m
        y, _, _ = _rms(m, wpost_ref[...])
        h2_ref[...] = h_ref[...] + y

    def row(w):
        return pl.BlockSpec((tm, w), lambda i: (i, 0))

    return pl.pallas_call(
        body, name="mix_out", grid=(t // tm,),
        in_specs=[row(D_MODEL), row(512), row(512), row(512), _full((1, GLA_DV)), _full((1, SWA_W)),
                  _full((D_MODEL, D_MODEL)), _full((1, D_MODEL))],
        out_specs=[row(D_MODEL), row(D_MODEL), row(D_MODEL)],
        out_shape=[jax.ShapeDtypeStruct((t, D_MODEL), F32), jax.ShapeDtypeStruct((t, D_MODEL), BF16),
                   jax.ShapeDtypeStruct((t, D_MODEL), F32)],
        compiler_params=_cparams(1),
    )(h1, ogla, gg, oswa, wgn, wsn, wout, wpost)


def _mix_out_bwd(dh2, m, ogla, gg, oswa, wgn, wsn, wout, wpost, hook=None):
    t = dh2.shape[0]
    tm = _row_tile(t)

    def body(dh_ref, m_ref, og_ref, gg_ref, os_ref, wgn_ref, wsn_ref, wout_ref, wpost_ref,
             dog_ref, dgg_ref, dos_ref, dm_ref, dwpost_ref, dwgn_ref, dwsn_ref):
        @pl.when(pl.program_id(0) == 0)
        def _():
            dwpost_ref[...] = jnp.zeros_like(dwpost_ref)
            dwgn_ref[...] = jnp.zeros_like(dwgn_ref)
            dwsn_ref[...] = jnp.zeros_like(dwsn_ref)

        wpost = wpost_ref[...]
        _, mh, r = _rms(m_ref[...], wpost)
        dm, dw = _rms_bwd(mh, r, wpost, dh_ref[...])
        dwpost_ref[...] += dw
        dmb = dm.astype(BF16)
        dm_ref[...] = dmb
        dcat = _dg(dmb, wout_ref[...], NT)
        wgn = wgn_ref[...]
        for h in range(GLA_HEADS):
            ls = slice(GLA_DV * h, GLA_DV * (h + 1))
            dog = dcat[:, ls]
            g = gg_ref[:, ls]
            sg = _sigmoid(g)
            y, xh, r = _rms(og_ref[:, ls], wgn)
            dgg_ref[:, ls] = dog * y * (sg * (1.0 + g * (1.0 - sg)))
            dx, dw = _rms_bwd(xh, r, wgn, dog * (g * sg))
            dog_ref[:, ls] = dx
            dwgn_ref[...] += dw
        wsn = wsn_ref[...]
        _, xh, r = _rms(os_ref[...], wsn)
        dx, dw = _rms_bwd(xh, r, wsn, dcat[:, GLA_W:])
        dos_ref[...] = dx
        dwsn_ref[...] += dw

    def row(w):
        return pl.BlockSpec((tm, w), lambda i: (i, 0))

    def rshape(w, dt=F32):
        return jax.ShapeDtypeStruct((t, w), dt)

    return _pallas(
        body, name="mix_out_bwd", grid=(t // tm,),
        in_specs=[row(D_MODEL), row(D_MODEL), row(512), row(512), row(512), _full((1, GLA_DV)), _full((1, SWA_W)),
                  _full((D_MODEL, D_MODEL)), _full((1, D_MODEL))],
        out_specs=[row(512), row(512), row(512), row(D_MODEL), _full((1, D_MODEL)), _full((1, GLA_DV)),
                   _full((1, SWA_W))],
        out_shape=[rshape(512), rshape(512), rshape(512), rshape(D_MODEL, BF16),
                   jax.ShapeDtypeStruct((1, D_MODEL), F32), jax.ShapeDtypeStruct((1, GLA_DV), F32),
                   jax.ShapeDtypeStruct((1, SWA_W), F32)],
        args=(dh2, m, ogla, gg, oswa, wgn, wsn, wout, wpost), hook=hook)


def _mix_in_bwd(dh2, h1, wmixpre, winp, wa2p, bap, cos, sin, ga, dgq, dgk, dgv, dgg, dla, dsq, dsk, dsv, dkm, dvm):
    t = h1.shape[0]
    tm = _row_tile(t)

    def body(dh2_ref, h_ref, w_ref, win_ref, wa2_ref, ba_ref, cos_ref, sin_ref, ga_ref, dgq_ref, dgk_ref, dgv_ref,
             dgg_ref, dla_ref, dsq_ref, dsk_ref, dsv_ref, dkm_ref, dvm_ref,
             dh1_ref, dproj_ref, dw_ref, dwa2_ref, dba_ref):
        i = pl.program_id(0)

        @pl.when(i == 0)
        def _():
            dw_ref[...] = jnp.zeros_like(dw_ref)
            dwa2_ref[...] = jnp.zeros_like(dwa2_ref)
            dba_ref[...] = jnp.zeros_like(dba_ref)

        first = (i == 0).astype(F32)
        c = cos_ref[...]
        s = -sin_ref[...]
        fh = _first_half_mask(tm)
        dproj_ref[:, P_GQ:P_GK] = dgq_ref[...].astype(BF16)
        dproj_ref[:, P_GK:P_GV] = dgk_ref[...].astype(BF16)
        dproj_ref[:, P_GV:P_GG] = dgv_ref[...].astype(BF16)
        dproj_ref[:, P_GG:P_GA] = dgg_ref[...].astype(BF16)
        gab = ga_ref[...].astype(BF16)
        z = _dot(gab, wa2_ref[...]) + ba_ref[...]
        row_id = i * tm + lax.broadcasted_iota(jnp.int32, (tm, 1), 0)
        dz = jnp.where(row_id >= PAD, dla_ref[...] * (1.0 / GLA_TAU) * (1.0 - _sigmoid(z)), 0.0)
        dzb = dz.astype(BF16)
        dba_ref[...] += jnp.sum(dz, axis=0, keepdims=True)
        dwa2_ref[...] += _dg(gab, dzb, TN)
        dproj_ref[:, P_GA:P_SQ] = _dg(dzb, wa2_ref[...], NT).astype(BF16)
        for k in range(4):
            dy = dsq_ref[:, 128 * k:128 * (k + 1)]
            dproj_ref[:, P_SQ + 128 * k:P_SQ + 128 * (k + 1)] = (dy * c + _rot_half(dy, fh) * s).astype(BF16)
        for k in range(2):
            ls = slice(128 * k, 128 * (k + 1))
            dy = dsk_ref[:, ls]
            dy = jnp.concatenate([dy[:BLK] + first * dkm_ref[:, ls], dy[BLK:]], axis=0) if tm > BLK else (
                dy + first * dkm_ref[:, ls])
            dproj_ref[:, P_SK + 128 * k:P_SK + 128 * (k + 1)] = (dy * c + _rot_half(dy, fh) * s).astype(BF16)
            dv = dsv_ref[:, ls]
            dv = jnp.concatenate([dv[:BLK] + first * dvm_ref[:, ls], dv[BLK:]], axis=0) if tm > BLK else (
                dv + first * dvm_ref[:, ls])
            dproj_ref[:, P_SV + 128 * k:P_SV + 128 * (k + 1)] = dv.astype(BF16)
        dn = _dg(dproj_ref[...], win_ref[...], NT)
        w = w_ref[...]
        _, hh, r = _rms(h_ref[...], w)
        dx, dw = _rms_bwd(hh, r, w, dn)
        dw_ref[...] += dw
        dh1_ref[...] = dh2_ref[...] + dx

    def row(w):
        return pl.BlockSpec((tm, w), lambda i: (i, 0))

    return pl.pallas_call(
        body, name="mix_in_bwd", grid=(t // tm,),
        in_specs=[row(D_MODEL), row(D_MODEL), _full((1, D_MODEL)), _full((D_MODEL, P_END)), _full((128, GLA_KW)),
                  _full((1, GLA_KW)), row(128), row(128), row(128), row(256), row(256), row(512), row(512), row(256),
                  row(512), row(256), row(256), _full((BLK, 256)), _full((BLK, 256))],
        out_specs=[row(D_MODEL), row(P_END), _full((1, D_MODEL)), _full((128, GLA_KW)), _full((1, GLA_KW))],
        out_shape=[jax.ShapeDtypeStruct((t, D_MODEL), F32), jax.ShapeDtypeStruct((t, P_END), BF16),
                   jax.ShapeDtypeStruct((1, D_MODEL), F32), jax.ShapeDtypeStruct((128, GLA_KW), F32),
                   jax.ShapeDtypeStruct((1, GLA_KW), F32)],
        compiler_params=_cparams(1),
    )(dh2, h1, wmixpre, winp, wa2p, bap, cos, sin, ga, dgq, dgk, dgv, dgg, dla, dsq, dsk, dsv, dkm, dvm)


def _adamw_update(w, g, m, v):
    m = ADAM_B1 * m + (1.0 - ADAM_B1) * g
    v = ADAM_B2 * v + (1.0 - ADAM_B2) * (g * g)
    m_hat = m / (1.0 - ADAM_B1 ** ADAM_STEP)
    v_hat = v / (1.0 - ADAM_B2 ** ADAM_STEP)
    return -ADAM_LR * (m_hat / (jnp.sqrt(v_hat) + ADAM_EPS) + ADAM_WD * w), m, v


def _adamw_halves(w, g_mine, g_other, m, v, c_idx, row0=0):
    r, c = w.shape
    h = g_mine.shape[0]
    tr = _div_tile(math.gcd(r, h))
    nth = h // tr
    t0 = row0 // tr
    assert t0 * tr == row0

    def body(c_ref, w_ref, gm_ref, go_ref, m_ref, v_ref, g_ref, d_ref, nm_ref, nv_ref):
        hh = (t0 + pl.program_id(0)) // nth
        g = jnp.where(hh == c_ref[0], gm_ref[...], go_ref[...])
        g_ref[...] = g
        d_ref[...], nm_ref[...], nv_ref[...] = _adamw_update(w_ref[...], g, m_ref[...], v_ref[...])

    spec = pl.BlockSpec((tr, c), lambda i, c_ref: (i, 0))

    def gspec(is_mine):
        def index(i, c_ref):
            used = ((t0 + i) // nth == c_ref[0]) == is_mine
            return (jnp.where(used, (t0 + i) % nth, 0), 0)
        return pl.BlockSpec((tr, c), index)

    shape = jax.ShapeDtypeStruct((r, c), F32)
    return pl.pallas_call(
        body, name="adamw_halves",
        grid_spec=pltpu.PrefetchScalarGridSpec(
            num_scalar_prefetch=1, grid=(r // tr,), in_specs=[spec, gspec(True), gspec(False), spec, spec],
            out_specs=[spec] * 4),
        out_shape=[shape] * 4, compiler_params=_cparams(1),
    )(c_idx, w, g_mine, g_other, m, v)


def _place():
    x, y, c = lax.axis_index("x"), lax.axis_index("y"), lax.axis_index("c")
    chips = [(1 - x, y), (x, 1 - y), (1 - x, 1 - y)]
    return x, y, c, chips


def _remote(send_sem, recv_sem, src, dst, to):
    return pltpu.make_async_remote_copy(src_ref=src, dst_ref=dst, send_sem=send_sem, recv_sem=recv_sem,
                                        device_id=to, device_id_type=MESH)


def _half(ref_rows, c):
    h = ref_rows // 2
    return pl.ds(pl.multiple_of(c * h, 8), h)


def _own_slot(shard, q):
    return lax.dynamic_update_slice(jnp.zeros((N_CHIPS,) + shard.shape, shard.dtype), shard[None], (q, 0, 0))


class _GatherChips:
    has_mid = True

    def __init__(self, bufs):
        n = len(bufs)
        self.inputs = list(bufs)
        self.out_shape = [jax.ShapeDtypeStruct(b.shape, b.dtype) for b in bufs]
        self.aliases = [(t, t) for t in range(n)]
        self.scratch = [pltpu.SemaphoreType.DMA((n, 6)), pltpu.SemaphoreType.DMA((n, 6))]

    def start(self, ins, outs, scr):
        send, recv = scr
        x, y, c, chips = _place()
        q = 2 * x + y
        for t, (i_ref, o_ref) in enumerate(zip(ins, outs)):
            rows = _half(i_ref.shape[1], c)
            for j, (cx, cy) in enumerate(chips):
                _remote(send.at[t, j], recv.at[t, j], i_ref.at[q, rows], o_ref.at[q, rows], (cx, cy, c)).start()

    def mid(self, ins, outs, scr):
        send, recv = scr
        x, y, c, chips = _place()
        for t, o_ref in enumerate(outs):
            rows = _half(o_ref.shape[1], c)
            for j, (cx, cy) in enumerate(chips):
                slot = o_ref.at[2 * cx + cy, rows]
                _remote(send.at[t, j], recv.at[t, j], slot, slot, (cx, cy, c)).wait_recv()
                _remote(send.at[t, 3 + j], recv.at[t, 3 + j], slot, slot, (x, y, 1 - c)).start()

    def finish(self, ins, outs, scr):
        send, recv = scr
        x, y, c, chips = _place()
        for t, o_ref in enumerate(outs):
            mine, other = _half(o_ref.shape[1], c), _half(o_ref.shape[1], 1 - c)
            for j, (cx, cy) in enumerate(chips):
                slot = o_ref.at[2 * cx + cy, other]
                _remote(send.at[t, 3 + j], recv.at[t, 3 + j], slot, slot, (x, y, 1 - c)).wait_recv()
            for j, (cx, cy) in enumerate(chips):
                sent = o_ref.at[2 * cx + cy, mine]
                _remote(send.at[t, j], recv.at[t, j], sent, sent, (cx, cy, c)).wait_send()
                _remote(send.at[t, 3 + j], recv.at[t, 3 + j], sent, sent, (x, y, 1 - c)).wait_send()


class _PairExchange:
    has_mid = False
    aliases = ()

    def __init__(self, arrs):
        n = len(arrs)
        self.inputs = list(arrs)
        self.out_shape = [jax.ShapeDtypeStruct((a.shape[0], a.shape[1] // 2, a.shape[2]), a.dtype) for a in arrs]
        self.scratch = [pltpu.SemaphoreType.DMA((n,)), pltpu.SemaphoreType.DMA((n,))]

    def _copies(self, ins, outs, scr):
        send, recv = scr
        x, y, c, _ = _place()
        return [_remote(send.at[t], recv.at[t], i_ref.at[:, _half(i_ref.shape[1], 1 - c)], o_ref, (x, y, 1 - c))
                for t, (i_ref, o_ref) in enumerate(zip(ins, outs))]

    def start(self, ins, outs, scr):
        for cp in self._copies(ins, outs, scr):
            cp.start()

    def finish(self, ins, outs, scr):
        for cp in self._copies(ins, outs, scr):
            cp.wait()


class _ChipScatter:
    has_mid = False
    aliases = ()

    def __init__(self, arrs):
        n = len(arrs)
        self.inputs = list(arrs)
        self.out_shape = [jax.ShapeDtypeStruct((3,) + a.shape[1:], a.dtype) for a in arrs]
        self.scratch = [pltpu.SemaphoreType.DMA((n, 3)), pltpu.SemaphoreType.DMA((n, 3))]

    def _copies(self, ins, outs, scr):
        send, recv = scr
        x, y, c, chips = _place()
        return [_remote(send.at[t, j], recv.at[t, j], i_ref.at[2 * cx + cy], o_ref.at[j], (cx, cy, c))
                for t, (i_ref, o_ref) in enumerate(zip(ins, outs)) for j, (cx, cy) in enumerate(chips)]

    def start(self, ins, outs, scr):
        for cp in self._copies(ins, outs, scr):
            cp.start()

    def finish(self, ins, outs, scr):
        for cp in self._copies(ins, outs, scr):
            cp.wait()


class _PairShare:
    has_mid = False
    aliases = ()

    def __init__(self, arrs):
        n = len(arrs)
        self.inputs = list(arrs)
        self.out_shape = [jax.ShapeDtypeStruct(a.shape, a.dtype) for a in arrs]
        self.scratch = [pltpu.SemaphoreType.DMA((n,)), pltpu.SemaphoreType.DMA((n,))]

    def _copies(self, ins, outs, scr):
        send, recv = scr
        x, y, c, _ = _place()
        return [_remote(send.at[t], recv.at[t], i_ref, o_ref, (x, y, 1 - c))
                for t, (i_ref, o_ref) in enumerate(zip(ins, outs))]

    def start(self, ins, outs, scr):
        for cp in self._copies(ins, outs, scr):
            cp.start()

    def finish(self, ins, outs, scr):
        for cp in self._copies(ins, outs, scr):
            cp.wait()


def _comm_call(hook, name):
    n_in, n_out = len(hook.inputs), len(hook.out_shape)

    def body(*refs):
        ins, outs, scr = refs[:n_in], refs[n_in:n_in + n_out], refs[n_in + n_out:]
        hook.start(ins, outs, scr)
        if hook.has_mid:
            hook.mid(ins, outs, scr)
        hook.finish(ins, outs, scr)

    return pl.pallas_call(body, name=name, in_specs=[ANY] * n_in, out_specs=[ANY] * n_out,
                          out_shape=list(hook.out_shape), scratch_shapes=list(hook.scratch),
                          input_output_aliases=dict(hook.aliases))(*hook.inputs)


def _all_gather_devices(vecs):
    n = len(vecs)

    def body(*refs):
        x_refs, out_refs = refs[:n], refs[n:2 * n]
        send_sems, recv_sems, local_sems = refs[2 * n:]
        x, y, c, chips = _place()
        me, sibling = (x, y, c), (x, y, 1 - c)
        waits = []
        for t, (x_ref, out_ref) in enumerate(zip(x_refs, out_refs)):
            def slot(px, py, pc, out_ref=out_ref):
                return out_ref.at[4 * px + 2 * py + pc]

            def copy(k, block, to, src=None, t=t, slot=slot):
                return pltpu.make_async_remote_copy(
                    src_ref=slot(*block) if src is None else src, dst_ref=slot(*block), send_sem=send_sems.at[t, k],
                    recv_sem=recv_sems.at[t, k], device_id=to, device_id_type=MESH)

            mine = pltpu.make_async_copy(x_ref, slot(*me), local_sems.at[t])
            mine.start()
            first = [copy(0, me, sibling, src=x_ref)]
            first += [copy(1 + j, me, (*chip, c), src=x_ref) for j, chip in enumerate(chips)]
            for cp in first:
                cp.start()
            waits.append((copy, mine, first))
        for copy, mine, first in waits:
            passed = [copy(4 + j, (*chip, c), sibling) for j, chip in enumerate(chips)]
            for j, chip in enumerate(chips):
                copy(1 + j, (*chip, c), me).wait_recv()
                passed[j].start()
            copy(0, sibling, me).wait_recv()
            for j, chip in enumerate(chips):
                copy(4 + j, (*chip, 1 - c), me).wait_recv()
            for cp in first + passed:
                cp.wait_send()
            mine.wait()

    vmem = pl.BlockSpec(memory_space=pltpu.VMEM)
    return pl.pallas_call(
        body, name="all_gather_devices", in_specs=[vmem] * n, out_specs=[vmem] * n,
        out_shape=[jax.ShapeDtypeStruct((N_DEV,) + v.shape, v.dtype) for v in vecs],
        scratch_shapes=[pltpu.SemaphoreType.DMA((n, 7)), pltpu.SemaphoreType.DMA((n, 7)),
                        pltpu.SemaphoreType.DMA((n,))],
    )(*vecs)


def _pair_sum(g, other, c_idx):
    nq, r, w = g.shape
    h = r // 2
    tr = _div_tile(h)
    nt = h // tr

    def body(c_ref, g_ref, o_ref, s_ref):
        s_ref[...] = (g_ref[...].astype(F32) + o_ref[...].astype(F32)).astype(s_ref.dtype)

    return pl.pallas_call(
        body, name="pair_sum",
        grid_spec=pltpu.PrefetchScalarGridSpec(
            num_scalar_prefetch=1, grid=(nq, nt),
            in_specs=[pl.BlockSpec((None, tr, w), lambda k, i, c_ref: (k, c_ref[0] * nt + i, 0)),
                      pl.BlockSpec((None, tr, w), lambda k, i, c_ref: (k, i, 0))],
            out_specs=pl.BlockSpec((None, tr, w), lambda k, i, c_ref: (k, i, 0))),
        out_shape=jax.ShapeDtypeStruct((nq, h, w), g.dtype),
        compiler_params=_cparams(2),
    )(c_idx, g, other)


def _chip_sum(s, others, q_idx):
    _, h, w = s.shape
    tr = _div_tile(h)

    def body(q_ref, s_ref, o_ref, out_ref):
        out_ref[...] = ((s_ref[...].astype(F32) + o_ref[0].astype(F32)) + o_ref[1].astype(F32)) + o_ref[2].astype(F32)

    return pl.pallas_call(
        body, name="chip_sum",
        grid_spec=pltpu.PrefetchScalarGridSpec(
            num_scalar_prefetch=1, grid=(h // tr,),
            in_specs=[pl.BlockSpec((None, tr, w), lambda i, q_ref: (q_ref[0], i, 0)),
                      pl.BlockSpec((3, tr, w), lambda i, q_ref: (0, i, 0))],
            out_specs=pl.BlockSpec((tr, w), lambda i, q_ref: (i, 0))),
        out_shape=jax.ShapeDtypeStruct((h, w), F32),
        compiler_params=_cparams(1),
    )(q_idx, s, others)


def _small_update(q_idx, parts, ws, ms, vs, col_block):
    n = len(parts)
    has_w = [w is not None for w in ws]

    def body(q_ref, *refs):
        pos = 0
        ins = []
        for t in range(n):
            k = 4 if has_w[t] else 1
            ins.append(refs[pos:pos + k])
            pos += k
        outs = refs[pos:]
        opos = 0
        for t in range(n):
            p_ref = ins[t][0]
            g = p_ref[0]
            for s in range(1, p_ref.shape[0]):
                g = g + p_ref[s]
            if has_w[t]:
                _, w_ref, m_ref, v_ref = ins[t]
                g_ref, d_ref, nm_ref, nv_ref = outs[opos:opos + 4]
                opos += 4
                g_ref[...] = g
                d_ref[...], nm_ref[...], nv_ref[...] = _adamw_update(w_ref[...], g, m_ref[...], v_ref[...])
            else:
                outs[opos][...] = g
                opos += 1

    def whole(shape):
        nd = len(shape)
        return pl.BlockSpec(shape, lambda i, q_ref: (0,) * nd)

    in_specs, out_specs, out_shape, args = [], [], [], []
    for t in range(n):
        k, r, wf = parts[t].shape
        if col_block[t]:
            w = wf // N_CHIPS
            in_specs.append(pl.BlockSpec((k, r, w), lambda i, q_ref: (0, 0, q_ref[0])))
        else:
            w = wf
            in_specs.append(whole((k, r, wf)))
        args.append(parts[t])
        if has_w[t]:
            assert ws[t].shape == (r, w), (ws[t].shape, r, w)
            in_specs += [whole((r, w))] * 3
            args += [ws[t], ms[t], vs[t]]
            out_specs += [whole((r, w))] * 4
            out_shape += [jax.ShapeDtypeStruct((r, w), F32)] * 4
        else:
            out_specs.append(whole((r, w)))
            out_shape.append(jax.ShapeDtypeStruct((r, w), F32))
    return pl.pallas_call(
        body, name="small_update",
        grid_spec=pltpu.PrefetchScalarGridSpec(num_scalar_prefetch=1, grid=(1,), in_specs=in_specs,
                                               out_specs=out_specs),
        out_shape=out_shape, compiler_params=_cparams(1),
    )(q_idx, *args)


_PACK_SEGMENTS = ((0, 1552), None, (1552, 2064), (2064, 2128), (2064, 2128), (2128, 2192), (2128, 2192),
                  (2192, 2256), (2192, 2256), (2256, 2320), (2256, 2320))
_UNPACK_SEGMENTS = (((0, 1552), (0,)), ((1552, 2064), (P_SQ,)), ((2064, 2128), (P_SK, P_SK + 64)),
                    ((2128, 2192), (P_SK + 128, P_SK + 192)), ((2192, 2256), (P_SV, P_SV + 64)),
                    ((2256, 2320), (P_SV + 128, P_SV + 192)))


def _pack_win(w4):
    per = w4.shape[2]
    pieces = []
    for seg in _PACK_SEGMENTS:
        if seg is None:
            pieces.append(jnp.zeros((w4.shape[1], 128 - GLA_RANK), w4.dtype))
            continue
        for q in range(w4.shape[0]):
            lo, hi = max(seg[0], q * per), min(seg[1], (q + 1) * per)
            if lo < hi:
                pieces.append(w4[q][:, lo - q * per:hi - q * per])
    return jnp.concatenate(pieces, axis=1)


def _unpack_dwin(d):
    per = D_IN // N_CHIPS
    chips = []
    for q in range(N_CHIPS):
        pieces = []
        for (a, b), starts in _UNPACK_SEGMENTS:
            lo, hi = max(a, q * per), min(b, (q + 1) * per)
            if lo < hi:
                copies = [d[:, s + lo - a:s + hi - a] for s in starts]
                pieces.append(copies[0] if len(copies) == 1 else copies[0] + copies[1])
        chips.append(jnp.concatenate(pieces, axis=1))
    return jnp.stack(chips).astype(BF16)


def _local_step(x, target, meta, p):
    s = x.shape[0]
    t = s + BLK
    h0 = jnp.concatenate([jnp.zeros((PAD, D_MODEL), F32), meta, x], axis=0)
    cos, sin = _rope_tables(t)

    h1, n1, g1, u1, a1, f1 = _ffn_fwd(h0, p["ffn1_pre_norm"], p["ffn1_w"], p["ffn1_post_norm"])
    n2, gq, gk, gv, gg, ga, la, sq, sk, sv = _mix_proj(h1, p["mix_pre_norm"], p["w_in"], p["gla_w_a2"], p["gla_b_a"],
                                                       cos, sin)
    ogla, ss = _gla_fwd(gq, gk, gv, la)
    oswa = _swa_fwd(p["swa_sinks"], sq, sk, sv)
    h2, cat, m = _mix_out(h1, ogla, gg, oswa, p["gla_out_norm"], p["swa_out_norm"], p["w_out"], p["mix_post_norm"])
    grads = {}
    dy, n3, g3, u3, a3, df3, grads["ffn2_post_norm"], sse = _ffn_fwd(
        h2, p["ffn2_pre_norm"], p["ffn2_w"], p["ffn2_post_norm"], target=target)

    dh2, dg3, du3, grads["ffn2_pre_norm"] = _ffn_bwd(
        dy, h2, None, g3, u3, p["ffn2_pre_norm"], p["ffn2_w"], p["ffn2_post_norm"], df=df3)
    (gud,) = _ffn_wgrad(n3, df3, dg3, du3, a3)
    grads["ffn2_w_gate"], grads["ffn2_w_up"], grads["ffn2_w_down"] = gud[:, :FJ], gud[:, FJ:2 * FJ], gud[:, 2 * FJ:]

    dogla, dgg, doswa, dm, grads["mix_post_norm"], grads["gla_out_norm"], grads["swa_out_norm"] = _mix_out_bwd(
        dh2, m, ogla, gg, oswa, p["gla_out_norm"], p["swa_out_norm"], p["w_out"], p["mix_post_norm"])
    grads["w_out"] = _xty(cat, dm)
    dsq, dsk, dsv, dkm, dvm, dsinks = _swa_bwd(p["swa_sinks"], sq, sk, sv, oswa, doswa)
    grads["swa_sinks"] = dsinks[:, 0]
    dgq, dgk, dgv, dla = _gla_bwd(gq, gk, gv, la, ss, dogla)
    dh1, dproj, grads["mix_pre_norm"], dwa2p, grads["gla_b_a"] = _mix_in_bwd(
        dh2, h1, p["mix_pre_norm"], p["w_in"], p["gla_w_a2"], p["gla_b_a"], cos, sin, ga, dgq, dgk, dgv, dgg, dla,
        dsq, dsk, dsv, dkm, dvm)
    grads["gla_w_a2"] = dwa2p[:GLA_RANK]
    grads["w_in"] = _unpack_dwin(_xty(n2, dproj))

    dh0, df1, dg1, du1, grads["ffn1_pre_norm"], grads["ffn1_post_norm"] = _ffn_bwd(
        dh1, h0, f1, g1, u1, p["ffn1_pre_norm"], p["ffn1_w"], p["ffn1_post_norm"])
    (gud,) = _ffn_wgrad(n1, df1, dg1, du1, a1)
    grads["ffn1_w_gate"], grads["ffn1_w_up"], grads["ffn1_w_down"] = gud[:, :FJ], gud[:, FJ:2 * FJ], gud[:, 2 * FJ:]
    grads["meta_tokens"] = dh0[PAD:BLK]
    return sse[0, 0], dh0[BLK:], grads


WEIGHTS = ['meta_tokens', 'ffn1_pre_norm', 'ffn1_w_gate', 'ffn1_w_up', 'ffn1_w_down', 'ffn1_post_norm',
           'mix_pre_norm', 'w_in', 'gla_w_a2', 'gla_b_a', 'gla_out_norm', 'swa_sinks', 'swa_out_norm', 'w_out',
           'mix_post_norm', 'ffn2_pre_norm', 'ffn2_w_gate', 'ffn2_w_up', 'ffn2_w_down', 'ffn2_post_norm']
BIG = ['ffn1_w_gate', 'ffn1_w_up', 'ffn1_w_down', 'w_in', 'w_out', 'ffn2_w_gate', 'ffn2_w_up', 'ffn2_w_down']
SMALL = [n for n in WEIGHTS if n not in BIG]
FJ = D_FF // N_CHIPS
D_IN_J = D_IN // N_CHIPS
D_OUT_J = D_MODEL // N_CHIPS
TRANSPOSED = ('ffn1_w_gate', 'ffn1_w_up', 'ffn2_w_gate', 'ffn2_w_up')


def _shard2d(name, a):
    return a[0].T if name in TRANSPOSED else a[0]


def _unshard2d(name, a):
    return (a.T if name in TRANSPOSED else a)[None]


def kernel(x, meta_tokens, ffn1_pre_norm, ffn1_w_gate, ffn1_w_up, ffn1_w_down, ffn1_post_norm, mix_pre_norm, w_in, gla_w_a2, gla_b_a, gla_out_norm, swa_sinks, swa_out_norm, w_out, mix_post_norm, ffn2_pre_norm, ffn2_w_gate, ffn2_w_up, ffn2_w_down, ffn2_post_norm, loss_target, m_meta_tokens, m_ffn1_pre_norm, m_ffn1_w_gate, m_ffn1_w_up, m_ffn1_w_down, m_ffn1_post_norm, m_mix_pre_norm, m_w_in, m_gla_w_a2, m_gla_b_a, m_gla_out_norm, m_swa_sinks, m_swa_out_norm, m_w_out, m_mix_post_norm, m_ffn2_pre_norm, m_ffn2_w_gate, m_ffn2_w_up, m_ffn2_w_down, m_ffn2_post_norm, v_meta_tokens, v_ffn1_pre_norm, v_ffn1_w_gate, v_ffn1_w_up, v_ffn1_w_down, v_ffn1_post_norm, v_mix_pre_norm, v_w_in, v_gla_w_a2, v_gla_b_a, v_gla_out_norm, v_swa_sinks, v_swa_out_norm, v_w_out, v_mix_post_norm, v_ffn2_pre_norm, v_ffn2_w_gate, v_ffn2_w_up, v_ffn2_w_down, v_ffn2_post_norm):
    args = dict(locals())
    w = {n: args[n] for n in WEIGHTS}
    mom = {n: args["m_" + n] for n in WEIGHTS}
    var = {n: args["v_" + n] for n in WEIGHTS}
    cx, cy, cc = lax.axis_index("x"), lax.axis_index("y"), lax.axis_index("c")
    q_idx = (2 * cx + cy).astype(jnp.int32).reshape(1)
    c_idx = cc.astype(jnp.int32).reshape(1)

    q_chip = 2 * cx + cy
    bf = {n: _own_slot(_shard2d(n, w[n]).astype(BF16), q_chip) for n in ("w_in", "w_out")}
    for ffn in ("ffn1", "ffn2"):
        stacked = jnp.concatenate([_shard2d(ffn + s, w[ffn + s]) for s in ("_w_gate", "_w_up", "_w_down")], axis=0)
        bf[ffn] = _own_slot(stacked.astype(BF16), q_chip)
    qc_idx = jnp.stack([q_chip, cc]).astype(jnp.int32)
    early = _GatherChips([_own_slot(w["meta_tokens"], q_chip),
                          _own_slot(w["gla_w_a2"].reshape(GLA_RANK, GLA_KW // N_CHIPS), q_chip)])
    meta4, wa24 = _comm_call(early, "gather_small")
    meta_full = meta4.transpose(1, 0, 2).reshape(N_META, D_MODEL)
    wa2p = jnp.pad(wa24.transpose(1, 0, 2).reshape(GLA_RANK, GLA_KW), ((0, 128 - GLA_RANK), (0, 0))).astype(BF16)
    sinks = w["swa_sinks"].reshape(SWA_QH)

    seq, target = x[0], loss_target[0]
    t = seq.shape[0] + BLK
    h0, n1 = _embed_norm(seq, meta_full, w["ffn1_pre_norm"])
    cos, sin = _rope_tables(t)
    late = _GatherChips([bf["w_in"], bf["w_out"], bf["ffn2"]])
    (h1, g1, u1, a1, f1), (w31,), (win4, wout4, w32) = _ffn_fwd_gather(
        h0, n1, bf["ffn1"], w["ffn1_post_norm"], qc_idx, late)
    winp = _pack_win(win4)
    wout = wout4.reshape(D_MODEL, D_MODEL)
    n2, gq, gk, gv, gg, ga, la, sq, sk, sv = _mix_proj(h1, w["mix_pre_norm"], winp, wa2p, w["gla_b_a"], cos, sin)
    ogla, ss = _gla_fwd(gq, gk, gv, la)
    oswa = _swa_fwd(sinks, sq, sk, sv)
    h2, cat, m = _mix_out(h1, ogla, gg, oswa, w["gla_out_norm"], w["swa_out_norm"], wout, w["mix_post_norm"])
    g = {}
    dy, n3, g3, u3, a3, df3, g["ffn2_post_norm"], sse = _ffn_fwd(
        h2, w["ffn2_pre_norm"], w32, w["ffn2_post_norm"], target=target)

    dh2, dg3, du3, g["ffn2_pre_norm"] = _ffn_bwd(
        dy, h2, None, g3, u3, w["ffn2_pre_norm"], w32, w["ffn2_post_norm"], df=df3)
    (gf2,) = _ffn_wgrad(n3, df3, dg3, du3, a3)
    (dogla, dgg, doswa, dm, g["mix_post_norm"], g["gla_out_norm"], g["swa_out_norm"]), (rgf2,) = _mix_out_bwd(
        dh2, m, ogla, gg, oswa, w["gla_out_norm"], w["swa_out_norm"], wout, w["mix_post_norm"],
        hook=_PairExchange([gf2]))
    sgf2 = _pair_sum(gf2, rgf2, c_idx)
    gout = _xty(cat, dm).reshape(N_CHIPS, D_OUT_J, D_MODEL).astype(BF16)
    (dsq, dsk, dsv, dkm, dvm, dsinks), (ogf2,) = _swa_bwd(sinks, sq, sk, sv, oswa, doswa,
                                                          hook=_ChipScatter([sgf2]))
    g["swa_sinks"] = dsinks
    dgq, dgk, dgv, dla = _gla_bwd(gq, gk, gv, la, ss, dogla)
    dh1, dproj, g["mix_pre_norm"], dwa2p, g["gla_b_a"] = _mix_in_bwd(
        dh2, h1, w["mix_pre_norm"], winp, wa2p, w["gla_b_a"], cos, sin, ga, dgq, dgk, dgv, dgg, dla,
        dsq, dsk, dsv, dkm, dvm)
    g["gla_w_a2"] = dwa2p[:GLA_RANK]
    gin = _unpack_dwin(_xty(n2, dproj))
    (dh0, df1, dg1, du1, g["ffn1_pre_norm"], g["ffn1_post_norm"]), (rgin, rgout) = _ffn_bwd(
        dh1, h0, f1, g1, u1, w["ffn1_pre_norm"], w31, w["ffn1_post_norm"],
        hook=_PairExchange([gin, gout]))
    sgin, sgout = _pair_sum(gin, rgin, c_idx), _pair_sum(gout, rgout, c_idx)
    own1, others1, (ogin, ogout) = _ffn_wgrad_reduce(n1, df1, dg1, du1, a1, qc_idx, _ChipScatter([sgin, sgout]))
    g["meta_tokens"] = dh0[PAD:BLK]
    grad_x = dh0[BLK:]
    halves = [_chip_sum(own1[None], others1, jnp.zeros((1,), jnp.int32))]
    halves += [_chip_sum(s, o, q_idx) for s, o in ((sgin, ogin), (sgout, ogout), (sgf2, ogf2))]
    others = _comm_call(_PairShare(halves), "pair_share")
    reduced = {"ffn1_w_gate": (0, 0), "ffn1_w_up": (0, FJ), "ffn1_w_down": (0, 2 * FJ), "w_in": (1, 0),
               "w_out": (2, 0), "ffn2_w_gate": (3, 0), "ffn2_w_up": (3, FJ), "ffn2_w_down": (3, 2 * FJ)}
    grad, delta, new_m, new_v = {}, {}, {}, {}
    for n in BIG:
        k, row0 = reduced[n]
        outs = _adamw_halves(_shard2d(n, w[n]), halves[k], others[k], _shard2d(n, mom[n]), _shard2d(n, var[n]),
                             c_idx, row0)
        grad[n], delta[n], new_m[n], new_v[n] = [_unshard2d(n, a) for a in outs]

    late = ["gla_w_a2", "swa_sinks"]
    direct = [n for n in SMALL if n not in late]
    names = direct + late
    gathered = _all_gather_devices([g[n] for n in names] + [sse])
    mat = lambda a: a.reshape(a.shape[-2:])
    none3 = [None] * (len(late) + 1)
    outs = _small_update(q_idx, gathered, [mat(w[n]) for n in direct] + none3, [mat(mom[n]) for n in direct] + none3,
                         [mat(var[n]) for n in direct] + none3, [n == "meta_tokens" for n in names] + [False])
    sum_a2, sum_sinks, sum_sse = outs[4 * len(direct):]
    loss = sum_sse[0, 0] * (0.5 / D_MODEL)
    g_late = [lax.dynamic_slice_in_dim(sum_a2, q_chip * (GLA_KW // N_CHIPS), GLA_KW // N_CHIPS, axis=1)[None],
              sum_sinks[:, 0].reshape(1, 1, SWA_QH)]
    outs = list(outs[:4 * len(direct)]) + list(_small_update(
        q_idx, g_late, [mat(w[n]) for n in late], [mat(mom[n]) for n in late], [mat(var[n]) for n in late],
        [False, False]))
    for k, n in enumerate(names):
        grad[n], delta[n], new_m[n], new_v[n] = [a.reshape(w[n].shape) for a in outs[4 * k:4 * k + 4]]

    return (loss, grad_x[None], *[grad[n] for n in WEIGHTS], *[delta[n] for n in WEIGHTS],
            *[new_m[n] for n in WEIGHTS], *[new_v[n] for n in WEIGHTS])
```

```python
import functools
import math

import numpy as np
import jax
import jax.numpy as jnp
from jax import lax
from jax.experimental import pallas as pl
from jax.experimental.pallas import tpu as pltpu

F32 = jnp.float32
BF16 = jnp.bfloat16
MESH = pl.DeviceIdType.MESH

D_MODEL = 1024
D_FF = 2816
N_CHIPS = 4
N_DEV = 8
N_META = 16
BLK = 128
PAD = BLK - N_META
GLA_CHUNK = 64
GLA_HEADS = 4
GLA_DV = 128
GLA_DK = 64
GLA_KW = GLA_HEADS * GLA_DK
GLA_W = GLA_HEADS * GLA_DV
GLA_RANK = 16
GLA_TAU = 16.0
SWA_HD = 64
SWA_QH = 8
SWA_KVH = 2
SWA_W = SWA_QH * SWA_HD
WINDOW = 128
ROPE_THETA = 10000.0
EPS = 1e-6
NEG_INF = -1e30
IN_SPLITS = (256, 256, 512, 512, 16, 512, 128, 128)
D_IN = sum(IN_SPLITS)
P_GQ, P_GK, P_GV, P_GG, P_GA, P_SQ, P_SK, P_SV, P_END = 0, 256, 512, 1024, 1536, 1664, 2176, 2432, 2688
ADAM_LR, ADAM_B1, ADAM_B2, ADAM_EPS, ADAM_WD, ADAM_STEP = 0.001, 0.9, 0.999, 1e-08, 0.01, 10
VMEM_LIMIT = 56 * 1024 * 1024

NT = (((1,), (1,)), ((), ()))
TN = (((0,), (0,)), ((), ()))


def _cparams(n_axes):
    return pltpu.CompilerParams(dimension_semantics=("arbitrary",) * n_axes, vmem_limit_bytes=VMEM_LIMIT)


def _row_tile(t):
    for tm in (640, 512, 384, 256, 128):
        if t % tm == 0:
            return tm
    raise ValueError(t)


SEQ_BLOCKS_PER_STEP = 5


def _seq_tile(t):
    return SEQ_BLOCKS_PER_STEP * BLK if t % (SEQ_BLOCKS_PER_STEP * BLK) == 0 else BLK


ROW_PARTS = 2


def _row_parts(tm):
    n = ROW_PARTS if tm % (16 * ROW_PARTS) == 0 else 1
    return [slice(k * (tm // n), (k + 1) * (tm // n)) for k in range(n)]


def _contract_tile(t):
    return 1664 if t % 1664 == 0 else _row_tile(t)


def _div_tile(r, cap=512):
    best = None
    for tr in range(8, min(r, cap) + 1, 8):
        if r % tr == 0:
            best = tr
    return best if best is not None else r


def _dot(a, b):
    return jnp.dot(a, b, preferred_element_type=F32)


def _dg(a, b, dims):
    return lax.dot_general(a, b, dims, preferred_element_type=F32)


def _rms(x, w):
    r = lax.rsqrt(jnp.mean(x * x, axis=-1, keepdims=True) + EPS)
    xh = x * r
    return xh * w, xh, r


def _rms_bwd(xh, r, w, dy):
    wdy = dy * w
    dx = r * (wdy - xh * jnp.mean(wdy * xh, axis=-1, keepdims=True))
    dw = jnp.sum(dy * xh, axis=0, keepdims=True)
    return dx, dw


def _sigmoid(x):
    return 1.0 / (1.0 + jnp.exp(-x))


def _full(shape):
    nd = len(shape)
    return pl.BlockSpec(shape, lambda *_: (0,) * nd)


ANY = pl.BlockSpec(memory_space=pl.ANY)


def _pallas(body, *, name, grid, in_specs, out_specs, out_shape, args, scratch_shapes=(), hook=None):
    n_axes = len(grid)
    if hook is None:
        return pl.pallas_call(body, name=name, grid=grid, in_specs=list(in_specs), out_specs=list(out_specs),
                              out_shape=list(out_shape), scratch_shapes=list(scratch_shapes),
                              compiler_params=_cparams(n_axes))(*args)
    n_in, n_out, n_scr = len(in_specs), len(out_specs), len(scratch_shapes)
    h_in, h_out = len(hook.inputs), len(hook.out_shape)
    total = math.prod(grid)

    def wrapped(*refs):
        ins, hins = refs[:n_in], refs[n_in:n_in + h_in]
        o0 = n_in + h_in
        outs, houts = refs[o0:o0 + n_out], refs[o0 + n_out:o0 + n_out + h_out]
        s0 = o0 + n_out + h_out
        scr, hscr = refs[s0:s0 + n_scr], refs[s0 + n_scr:]
        step = pl.program_id(0)
        for a in range(1, n_axes):
            step = step * grid[a] + pl.program_id(a)

        @pl.when(step == 0)
        def _():
            hook.start(hins, houts, hscr)

        body(*ins, *outs, *scr)

        if hook.has_mid:
            @pl.when(step == (3 * total) // 4)
            def _():
                hook.mid(hins, houts, hscr)

        @pl.when(step == total - 1)
        def _():
            hook.finish(hins, houts, hscr)

    res = pl.pallas_call(
        wrapped, name=name, grid=grid, in_specs=list(in_specs) + [ANY] * h_in,
        out_specs=list(out_specs) + [ANY] * h_out, out_shape=list(out_shape) + list(hook.out_shape),
        scratch_shapes=list(scratch_shapes) + list(hook.scratch), compiler_params=_cparams(n_axes),
        input_output_aliases={n_in + a: n_out + b for a, b in hook.aliases},
    )(*args, *hook.inputs)
    return res[:n_out], res[n_out:]


def _ffn_weight_specs(w3):
    fj = w3.shape[1] // 3
    return fj, [pl.BlockSpec((None, fj, D_MODEL), functools.partial(lambda i, j, k: (j, k, 0), k=k)) for k in range(3)]


def _ffn_fwd(h, wpre, w3, wpost, hook=None, target=None):
    t = h.shape[0]
    tm = _row_tile(t)
    nj = w3.shape[0]
    fj, wspecs = _ffn_weight_specs(w3)
    nblk = tm // BLK if target is not None else 0

    def body(*refs):
        h_ref, wpre_ref, wg_ref, wu_ref, wd_ref, wpost_ref = refs[:6]
        t_refs = refs[6:6 + nblk]
        hout_ref, n_ref, p1_ref, p2_ref, a_ref, f_ref = refs[6 + nblk:12 + nblk]
        acc_ref = refs[-1]
        i = pl.program_id(0)
        j = pl.program_id(1)

        @pl.when(j == 0)
        def _():
            y, _, _ = _rms(h_ref[...], wpre_ref[...])
            n_ref[...] = y.astype(BF16)
            acc_ref[...] = jnp.zeros_like(acc_ref)

        if target is not None:
            dwpost_ref, sse_ref = refs[12 + nblk:14 + nblk]

            @pl.when((i == 0) & (j == 0))
            def _():
                dwpost_ref[...] = jnp.zeros_like(dwpost_ref)
                sse_ref[...] = jnp.zeros_like(sse_ref)

        n = n_ref[...]
        g = _dg(n, wg_ref[...], NT)
        u = _dg(n, wu_ref[...], NT)
        sg = _sigmoid(g)
        silu = g * sg
        p1_ref[...] = (u * (sg + silu * (1.0 - sg))).astype(BF16)
        p2_ref[...] = silu.astype(BF16)
        a = (silu * u).astype(BF16)
        a_ref[...] = a
        acc_ref[...] += _dot(a, wd_ref[...])

        @pl.when(j == nj - 1)
        def _():
            f = acc_ref[...]
            wpost = wpost_ref[...]
            y, fh, r = _rms(f, wpost)
            hout = h_ref[...] + 0.5 * y
            if target is None:
                f_ref[...] = f
                hout_ref[...] = hout
            else:
                sse = jnp.zeros((1, 1), F32)
                errs = []
                for k in range(nblk):
                    err = hout[k * BLK:(k + 1) * BLK] - t_refs[k][...]
                    if k == 0:
                        err = jnp.where(i > 0, err, 0.0)
                    errs.append(err)
                    sse = sse + jnp.sum(jnp.sum(err * err, axis=1, keepdims=True), axis=0, keepdims=True)
                dy = (jnp.concatenate(errs, axis=0) if nblk > 1 else errs[0]) * (1.0 / D_MODEL)
                hout_ref[...] = dy
                df, dw = _rms_bwd(fh, r, wpost, 0.5 * dy)
                f_ref[...] = df.astype(BF16)
                dwpost_ref[...] += dw
                sse_ref[...] += jnp.broadcast_to(sse, sse_ref.shape)

    row = pl.BlockSpec((tm, D_MODEL), lambda i, j: (i, 0))
    row_ahead = pl.BlockSpec((tm, D_MODEL), lambda i, j: (i, 0), pipeline_mode=pl.Buffered(2, use_lookahead=True))
    vec = pl.BlockSpec((1, D_MODEL), lambda i, j: (0, 0))
    act = pl.BlockSpec((None, tm, fj), lambda i, j: (j, i, 0))
    t_specs = [pl.BlockSpec((BLK, D_MODEL), functools.partial(lambda i, j, k: (jnp.maximum(nblk * i + k - 1, 0), 0), k=k))
               for k in range(nblk)]
    loss_spec = [vec, _full((1, 128))] if target is not None else []
    loss_shape = [jax.ShapeDtypeStruct((1, D_MODEL), F32), jax.ShapeDtypeStruct((1, 128), F32)] if (
        target is not None) else []
    return _pallas(
        body, name="ffn_fwd", grid=(t // tm, nj),
        in_specs=[row, vec] + wspecs + [vec] + t_specs,
        out_specs=[row, row, act, act, act, row] + loss_spec,
        out_shape=[jax.ShapeDtypeStruct((t, D_MODEL), F32), jax.ShapeDtypeStruct((t, D_MODEL), BF16),
                   jax.ShapeDtypeStruct((nj, t, fj), BF16), jax.ShapeDtypeStruct((nj, t, fj), BF16),
                   jax.ShapeDtypeStruct((nj, t, fj), BF16),
                   jax.ShapeDtypeStruct((t, D_MODEL), F32 if target is None else BF16)] + loss_shape,
        scratch_shapes=[pltpu.VMEM((tm, D_MODEL), F32)],
        args=(h, wpre, w3, w3, w3, wpost) + (target,) * nblk, hook=hook)


def _ffn_bwd(dhout, h, f, p14, p24, wpre, w3, wpost, hook=None, df=None):
    t = h.shape[0]
    tm = _row_tile(t)
    nj = w3.shape[0]
    fj, wspecs = _ffn_weight_specs(w3)
    have_df = df is not None

    ni = t // tm

    def body(dhout_hbm, h_hbm, f_hbm, p1_ref, p2_ref, wpre_ref, wg_ref, wu_ref, wd_ref, wpost_ref, *rest):
        if have_df:
            dh_ref, dg_ref, du_ref, dwpre_ref, dn_ref, dhout_buf, h_buf, f_buf, rsem = rest
        else:
            dh_ref, df_ref, dg_ref, du_ref, dwpre_ref, dwpost_ref, dn_ref, dhout_buf, h_buf, f_buf, rsem = rest
        i = pl.program_id(0)
        j = pl.program_id(1)
        slot = i % 2

        def fetch(tile, sl):
            rows = pl.ds(pl.multiple_of(tile * tm, 8), tm)
            return [pltpu.make_async_copy(src.at[rows], buf.at[sl], rsem.at[k, sl])
                    for k, (src, buf) in enumerate(((dhout_hbm, dhout_buf), (h_hbm, h_buf), (f_hbm, f_buf)))]

        @pl.when((i == 0) & (j == 0))
        def _():
            for cp in fetch(0, 0):
                cp.start()
            dwpre_ref[...] = jnp.zeros_like(dwpre_ref)
            if not have_df:
                dwpost_ref[...] = jnp.zeros_like(dwpost_ref)

        @pl.when(j == 0)
        def _():
            for cp in fetch(i, slot):
                cp.wait()

        @pl.when((j == 0) & (i + 1 < ni))
        def _():
            for cp in fetch(i + 1, 1 - slot):
                cp.start()

        dhout_ref, h_ref, f_ref = dhout_buf.at[slot], h_buf.at[slot], f_buf.at[slot]
        if have_df:
            df_ref = f_ref

        @pl.when(j == 0)
        def _():
            if not have_df:
                wpost = wpost_ref[...]
                _, fh, r = _rms(f_ref[...], wpost)
                dfv, dw = _rms_bwd(fh, r, wpost, 0.5 * dhout_ref[...])
                dwpost_ref[...] += dw
                df_ref[...] = dfv.astype(BF16)
            dn_ref[...] = jnp.zeros_like(dn_ref)

        parts = _row_parts(tm)
        das = [_dg(df_ref[rows, :], wd_ref[...], NT) for rows in parts]
        for rows, da in zip(parts, das):
            dg = (da * p1_ref[rows, :].astype(F32)).astype(BF16)
            du = (da * p2_ref[rows, :].astype(F32)).astype(BF16)
            dg_ref[rows, :] = dg
            du_ref[rows, :] = du
            dn_ref[rows, :] += _dot(dg, wg_ref[...]) + _dot(du, wu_ref[...])

        @pl.when(j == nj - 1)
        def _():
            wpre = wpre_ref[...]
            _, hh, r = _rms(h_ref[...], wpre)
            dx, dw = _rms_bwd(hh, r, wpre, dn_ref[...])
            dwpre_ref[...] += dw
            dh_ref[...] = dhout_ref[...] + dx

    row = pl.BlockSpec((tm, D_MODEL), lambda i, j: (i, 0))
    vec = pl.BlockSpec((1, D_MODEL), lambda i, j: (0, 0))
    act = pl.BlockSpec((None, tm, fj), lambda i, j: (j, i, 0))
    actshape = jax.ShapeDtypeStruct((nj, t, fj), BF16)
    rowf, rowb, vecf = (jax.ShapeDtypeStruct((t, D_MODEL), F32), jax.ShapeDtypeStruct((t, D_MODEL), BF16),
                        jax.ShapeDtypeStruct((1, D_MODEL), F32))
    f_in = df if have_df else f
    return _pallas(
        body, name="ffn_bwd", grid=(ni, nj),
        in_specs=[ANY, ANY, ANY, act, act, vec] + wspecs + [vec],
        out_specs=[row, act, act, vec] if have_df else [row, row, act, act, vec, vec],
        out_shape=[rowf, actshape, actshape, vecf] if have_df else [rowf, rowb, actshape, actshape, vecf, vecf],
        scratch_shapes=[pltpu.VMEM((tm, D_MODEL), F32), pltpu.VMEM((2, tm, D_MODEL), F32),
                        pltpu.VMEM((2, tm, D_MODEL), F32), pltpu.VMEM((2, tm, D_MODEL), f_in.dtype),
                        pltpu.SemaphoreType.DMA((3, 2))],
        args=(dhout, h, f_in, p14, p24, wpre, w3, w3, w3, wpost), hook=hook)


def _ffn_wgrad(n, df, dg4, du4, a4, hook=None):
    t = n.shape[0]
    tm = _contract_tile(t)
    ni = t // tm
    nj, _, fj = dg4.shape

    def body(n_ref, df_ref, dg_ref, du_ref, a_ref, dw_ref, acc):
        i = pl.program_id(1)

        @pl.when(i == 0)
        def _():
            acc[...] = jnp.zeros_like(acc)

        nn = n_ref[...]
        acc[0:fj, :] += _dg(dg_ref[...], nn, TN)
        acc[fj:2 * fj, :] += _dg(du_ref[...], nn, TN)
        acc[2 * fj:3 * fj, :] += _dg(a_ref[...], df_ref[...], TN)

        @pl.when(i == ni - 1)
        def _():
            dw_ref[...] = acc[...].astype(BF16)

    row = pl.BlockSpec((tm, D_MODEL), lambda j, i: (i, 0))
    act = pl.BlockSpec((None, tm, fj), lambda j, i: (j, i, 0))
    return _pallas(
        body, name="ffn_wgrad", grid=(nj, ni),
        in_specs=[row, row, act, act, act],
        out_specs=[pl.BlockSpec((None, 3 * fj, D_MODEL), lambda j, i: (j, 0, 0))],
        out_shape=[jax.ShapeDtypeStruct((nj, 3 * fj, D_MODEL), BF16)],
        scratch_shapes=[pltpu.VMEM((3 * fj, D_MODEL), F32)],
        args=(n, df, dg4, du4, a4), hook=hook)


def _embed_norm(x, meta, w):
    t = x.shape[0] + BLK
    tm = _row_tile(t)
    nblk = tm // BLK

    def body(*refs):
        x_refs = refs[:nblk]
        meta_ref, w_ref, h_ref, n_ref = refs[nblk:]
        i = pl.program_id(0)
        first = jnp.concatenate([jnp.zeros((PAD, D_MODEL), F32), meta_ref[...]], axis=0)
        blocks = [jnp.where(i == 0, first, x_refs[0][...])] + [r[...] for r in x_refs[1:]]
        h = jnp.concatenate(blocks, axis=0) if nblk > 1 else blocks[0]
        h_ref[...] = h
        y, _, _ = _rms(h, w_ref[...])
        n_ref[...] = y.astype(BF16)

    x_specs = [pl.BlockSpec((BLK, D_MODEL), functools.partial(lambda i, k: (jnp.maximum(nblk * i + k - 1, 0), 0), k=k))
               for k in range(nblk)]
    row = pl.BlockSpec((tm, D_MODEL), lambda i: (i, 0))
    return pl.pallas_call(
        body, name="embed_norm", grid=(t // tm,),
        in_specs=x_specs + [_full((N_META, D_MODEL)), _full((1, D_MODEL))], out_specs=[row, row],
        out_shape=[jax.ShapeDtypeStruct((t, D_MODEL), F32), jax.ShapeDtypeStruct((t, D_MODEL), BF16)],
        compiler_params=_cparams(1),
    )(*([x] * nblk), meta, w)


FWD_RELATION = (None, 0, 1, 2)


def _ffn_fwd_gather(h, n, wbuf, wpost, qc_idx, late):
    t = h.shape[0]
    tm = _row_tile(t)
    ni = t // tm
    nj, rows3, _ = wbuf.shape
    fj = rows3 // 3
    assert nj == N_CHIPS and ni >= 4
    wbufs = [wbuf]
    nw = 1
    n_lin, n_lout = len(late.inputs), len(late.out_shape)
    wait_step = ni - 3

    def body(qc_ref, h_ref, n_ref, wpost_ref, *rest):
        wb_in = rest[:nw]
        lins = rest[nw:nw + n_lin]
        o0 = nw + n_lin
        hout_ref, p1_ref, p2_ref, a_ref, f_hbm = rest[o0:o0 + 5]
        wb = rest[o0 + 5:o0 + 5 + nw]
        louts = rest[o0 + 5 + nw:o0 + 5 + nw + n_lout]
        s0 = o0 + 5 + nw + n_lout
        wv, wsem, send, recv, fbuf, fr_sem, fw_sem = rest[s0:s0 + 7]
        lscr = rest[s0 + 7:]
        p = pl.program_id(0)
        i = pl.program_id(1)
        step = p * ni + i
        fslot = step % 3
        nslot = (step + 1) % 3

        def f_tile(tile):
            return f_hbm.at[pl.ds(pl.multiple_of(tile * tm, 8), tm)]

        @pl.when(step > 1)
        def _():
            pltpu.make_async_copy(fbuf.at[nslot], f_tile(i), fw_sem.at[nslot]).wait()

        nxt = step + 1

        @pl.when((nxt < N_CHIPS * ni) & (nxt >= ni))
        def _():
            pltpu.make_async_copy(f_tile(nxt % ni), fbuf.at[nslot], fr_sem.at[nslot]).start()

        @pl.when(p > 0)
        def _():
            pltpu.make_async_copy(f_tile(i), fbuf.at[fslot], fr_sem.at[fslot]).wait()
        x, y, c, chips = _place()
        q = 2 * x + y
        sibling = (x, y, 1 - c)
        mine, other = _half(rows3, c), _half(rows3, 1 - c)

        def load(chunk, slot, src):
            return [pltpu.make_async_copy(src[t].at[chunk], wv.at[slot, t], wsem.at[slot, t]) for t in range(nw)]

        @pl.when((p == 0) & (i == 0))
        def _():
            for j, (cx, cy) in enumerate(chips):
                for t in range(nw):
                    _remote(send.at[t, j], recv.at[t, j], wb_in[t].at[q, mine], wb[t].at[q, mine], (cx, cy, c)).start()
            for cp in load(q, 0, wb_in):
                cp.start()
            for cp in load(q, 0, wb_in):
                cp.wait()

        @pl.when((p == 1) & (i == 0))
        def _():
            late.start(lins, louts, lscr)

        for pp in range(1, N_CHIPS):
            j = FWD_RELATION[pp]
            cx, cy = chips[j]
            chunk = 2 * cx + cy

            @pl.when((p == pp - 1) & (i == wait_step))
            def _(j=j, cx=cx, cy=cy, chunk=chunk, pp=pp):
                for t in range(nw):
                    got = wb[t].at[chunk, mine]
                    _remote(send.at[t, j], recv.at[t, j], got, got, (cx, cy, c)).wait_recv()
                    _remote(send.at[t, 3 + j], recv.at[t, 3 + j], got, got, sibling).start()
                for t in range(nw):
                    rest_half = wb[t].at[chunk, other]
                    _remote(send.at[t, 3 + j], recv.at[t, 3 + j], rest_half, rest_half, sibling).wait_recv()
                for cp in load(chunk, pp % 2, wb):
                    cp.start()

            @pl.when((p == pp) & (i == 0))
            def _(chunk=chunk, pp=pp):
                for cp in load(chunk, pp % 2, wb):
                    cp.wait()

        @pl.when((p == N_CHIPS - 1) & (i == ni // 2))
        def _():
            late.mid(lins, louts, lscr)

        slot = p % 2
        nn = n_ref[...]
        g = _dg(nn, wv[slot, 0, 0:fj], NT)
        u = _dg(nn, wv[slot, 0, fj:2 * fj], NT)
        sg = _sigmoid(g)
        silu = g * sg
        p1_ref[...] = (u * (sg + silu * (1.0 - sg))).astype(BF16)
        p2_ref[...] = silu.astype(BF16)
        a = (silu * u).astype(BF16)
        a_ref[...] = a
        part = _dot(a, wv[slot, 0, 2 * fj:3 * fj])

        @pl.when(p == 0)
        def _():
            fbuf[fslot] = part

        @pl.when(p > 0)
        def _():
            fbuf[fslot] = fbuf[fslot] + part

        pltpu.make_async_copy(fbuf.at[fslot], f_tile(i), fw_sem.at[fslot]).start()

        @pl.when(p == N_CHIPS - 1)
        def _():
            yv, _, _ = _rms(fbuf[fslot], wpost_ref[...])
            hout_ref[...] = h_ref[...] + 0.5 * yv

        @pl.when((p == N_CHIPS - 1) & (i == ni - 1))
        def _():
            pslot = (step + 2) % 3
            pltpu.make_async_copy(fbuf.at[pslot], f_tile(i), fw_sem.at[pslot]).wait()
            pltpu.make_async_copy(fbuf.at[fslot], f_tile(i), fw_sem.at[fslot]).wait()
            for t in range(nw):
                for j, (cx, cy) in enumerate(chips):
                    sent = wb[t].at[2 * cx + cy, mine]
                    _remote(send.at[t, j], recv.at[t, j], sent, sent, (cx, cy, c)).wait_send()
                    _remote(send.at[t, 3 + j], recv.at[t, 3 + j], sent, sent, sibling).wait_send()
            late.finish(lins, louts, lscr)

    def last_pass_rows(p, i, qc_ref):
        return (jnp.where(p == N_CHIPS - 1, i, 0), 0)

    def chunk_rows(p, i, qc_ref):
        order = ((p & 1) << 1) | (p >> 1)
        return (jnp.bitwise_xor(qc_ref[0], order), i, 0)

    row = pl.BlockSpec((tm, D_MODEL), lambda p, i, qc_ref: (i, 0))
    last_row = pl.BlockSpec((tm, D_MODEL), last_pass_rows)
    act = pl.BlockSpec((None, tm, fj), chunk_rows)
    act_shape = jax.ShapeDtypeStruct((nj, t, fj), BF16)
    res = pl.pallas_call(
        body, name="ffn_fwd_gather",
        grid_spec=pltpu.PrefetchScalarGridSpec(
            num_scalar_prefetch=1, grid=(N_CHIPS, ni),
            in_specs=[last_row, row, pl.BlockSpec((1, D_MODEL), lambda p, i, qc_ref: (0, 0))]
            + [ANY] * (nw + n_lin),
            out_specs=[last_row, act, act, act, ANY] + [ANY] * (nw + n_lout),
            scratch_shapes=[pltpu.VMEM((2, nw, rows3, D_MODEL), BF16), pltpu.SemaphoreType.DMA((2, nw)),
                            pltpu.SemaphoreType.DMA((nw, 6)), pltpu.SemaphoreType.DMA((nw, 6)),
                            pltpu.VMEM((3, tm, D_MODEL), F32), pltpu.SemaphoreType.DMA((3,)),
                            pltpu.SemaphoreType.DMA((3,))] + list(late.scratch)),
        out_shape=[jax.ShapeDtypeStruct((t, D_MODEL), F32), act_shape, act_shape, act_shape,
                   jax.ShapeDtypeStruct((t, D_MODEL), F32)]
        + [jax.ShapeDtypeStruct(b.shape, b.dtype) for b in wbufs] + list(late.out_shape),
        input_output_aliases={**{4 + t: 5 + t for t in range(nw)},
                              **{4 + nw + a: 5 + nw + b for a, b in late.aliases}},
        compiler_params=_cparams(2),
    )(qc_idx, h, n, wpost, *wbufs, *late.inputs)
    return res[:5], res[5:5 + nw], res[5 + nw:]


PASS_RELATION = (2, 0, 1)


def _ffn_wgrad_reduce(n, df, dg4, du4, a4, qc_idx, hook):
    t = n.shape[0]
    tm = _contract_tile(t)
    ni = t // tm
    nj, _, fj = dg4.shape
    assert nj == N_CHIPS
    hrows = 3 * fj // 2
    n_hin, n_hout = len(hook.inputs), len(hook.out_shape)

    def body(qc_ref, n_ref, df_ref, dg_ref, du_ref, a_ref, *rest):
        hins = rest[:n_hin]
        own_ref, others_ref = rest[n_hin:n_hin + 2]
        houts = rest[n_hin + 2:n_hin + 2 + n_hout]
        s0 = n_hin + 2 + n_hout
        acc, stage, land, sumbuf, px_send, px_recv, cs_send, cs_recv, own_sem = rest[s0:s0 + 9]
        hscr = rest[s0 + 9:]
        k_pass = pl.program_id(0)
        i = pl.program_id(1)
        x, y, c, chips = _place()
        mine = pl.ds(pl.multiple_of(c * hrows, 8), hrows)
        other = pl.ds(pl.multiple_of((1 - c) * hrows, 8), hrows)

        def to_owner(k):
            j = PASS_RELATION[k]
            return _remote(cs_send.at[j], cs_recv.at[j], sumbuf.at[k % 2], others_ref.at[j], (*chips[j], c))

        @pl.when((k_pass == 0) & (i == 0))
        def _():
            hook.start(hins, houts, hscr)

        @pl.when(i == 0)
        def _():
            acc[...] = jnp.zeros_like(acc)

        nn = n_ref[...]
        acc[0:fj, :] += _dg(dg_ref[...], nn, TN)
        acc[fj:2 * fj, :] += _dg(du_ref[...], nn, TN)
        acc[2 * fj:3 * fj, :] += _dg(a_ref[...], df_ref[...], TN)

        for k in range(N_CHIPS):
            @pl.when((k_pass == k) & (i == ni - 1))
            def _(k=k):
                slot = k % 2
                stage[...] = acc[other, :].astype(BF16)
                swap = _remote(px_send.at[k], px_recv.at[k], stage, land.at[slot], (x, y, 1 - c))
                swap.start()
                swap.wait_recv()
                pair = acc[mine, :] + land[slot].astype(F32)
                if k >= 2:
                    to_owner(k - 2).wait_send()
                sumbuf[slot] = pair.astype(BF16)
                swap.wait_send()
                if k < N_CHIPS - 1:
                    to_owner(k).start()
                else:
                    keep = pltpu.make_async_copy(sumbuf.at[slot], own_ref, own_sem)
                    keep.start()
                    for j in range(N_CHIPS - 1):
                        _remote(cs_send.at[j], cs_recv.at[j], sumbuf.at[0], others_ref.at[j], (*chips[j], c)).wait_recv()
                    to_owner(k - 1).wait_send()
                    keep.wait()
                    hook.finish(hins, houts, hscr)

    def chunk(k_pass, i, qc_ref):
        return (jnp.bitwise_xor(qc_ref[0], N_CHIPS - 1 - k_pass), i, 0)

    row = pl.BlockSpec((tm, D_MODEL), lambda k_pass, i, qc_ref: (i, 0))
    act = pl.BlockSpec((None, tm, fj), chunk)
    res = pl.pallas_call(
        body, name="ffn_wgrad_reduce",
        grid_spec=pltpu.PrefetchScalarGridSpec(
            num_scalar_prefetch=1, grid=(N_CHIPS, ni),
            in_specs=[row, row, act, act, act] + [ANY] * n_hin,
            out_specs=[ANY, ANY] + [ANY] * n_hout,
            scratch_shapes=[pltpu.VMEM((3 * fj, D_MODEL), F32), pltpu.VMEM((hrows, D_MODEL), BF16),
                            pltpu.VMEM((2, hrows, D_MODEL), BF16), pltpu.VMEM((2, hrows, D_MODEL), BF16),
                            pltpu.SemaphoreType.DMA((N_CHIPS,)), pltpu.SemaphoreType.DMA((N_CHIPS,)),
                            pltpu.SemaphoreType.DMA((N_CHIPS - 1,)), pltpu.SemaphoreType.DMA((N_CHIPS - 1,)),
                            pltpu.SemaphoreType.DMA] + list(hook.scratch)),
        out_shape=[jax.ShapeDtypeStruct((hrows, D_MODEL), BF16),
                   jax.ShapeDtypeStruct((N_CHIPS - 1, hrows, D_MODEL), BF16)] + list(hook.out_shape),
        compiler_params=_cparams(2),
    )(qc_idx, n, df, dg4, du4, a4, *hook.inputs)
    return res[0], res[1], res[2:]


def _xty(x, y):
    t, k = x.shape
    n = y.shape[1]
    tm = _contract_tile(t)
    tn = n if n <= 1024 else (896 if n % 896 == 0 else 128)

    def body(x_ref, y_ref, o_ref):
        @pl.when(pl.program_id(1) == 0)
        def _():
            o_ref[...] = jnp.zeros_like(o_ref)

        o_ref[...] += _dg(x_ref[...], y_ref[...], TN)

    return pl.pallas_call(
        body, name="xty", grid=(n // tn, t // tm),
        in_specs=[pl.BlockSpec((tm, k), lambda j, i: (i, 0)), pl.BlockSpec((tm, tn), lambda j, i: (i, j))],
        out_specs=pl.BlockSpec((k, tn), lambda j, i: (0, j)),
        out_shape=jax.ShapeDtypeStruct((k, n), F32),
        compiler_params=_cparams(2),
    )(x, y)


def _rope_tables(t):
    pos = (jnp.arange(t, dtype=jnp.int32) - PAD).astype(F32)
    inv_freq = 1.0 / (ROPE_THETA ** (jnp.arange(0, SWA_HD, 2, dtype=F32) / SWA_HD))
    ang = pos[:, None] * inv_freq[None, :]
    cos = jnp.cos(ang)
    sin = jnp.sin(ang)
    return jnp.concatenate([cos, cos, cos, cos], axis=1), jnp.concatenate([-sin, sin, -sin, sin], axis=1)


def _rot_half(x, first_half):
    return jnp.where(first_half, pltpu.roll(x, 96, 1), pltpu.roll(x, 32, 1))


def _first_half_mask(rows):
    lane = lax.broadcasted_iota(jnp.int32, (rows, 128), 1)
    return (lane % 64) < 32


def _log_sigmoid(z):
    return jnp.minimum(z, 0.0) - jnp.log(1.0 + jnp.exp(-jnp.abs(z)))


def _mix_proj(h1, wmixpre, winp, wa2p, bap, cos, sin):
    t = h1.shape[0]
    tm = _row_tile(t)

    def body(h_ref, w_ref, win_ref, wa2_ref, ba_ref, cos_ref, sin_ref,
             n_ref, gq_ref, gk_ref, gv_ref, gg_ref, ga_ref, la_ref, sq_ref, sk_ref, sv_ref):
        y, _, _ = _rms(h_ref[...], w_ref[...])
        n = y.astype(BF16)
        n_ref[...] = n
        proj = _dot(n, win_ref[...])
        gq_ref[...] = proj[:, P_GQ:P_GK]
        gk_ref[...] = proj[:, P_GK:P_GV]
        gv_ref[...] = proj[:, P_GV:P_GG]
        gg_ref[...] = proj[:, P_GG:P_GA]
        ga = proj[:, P_GA:P_SQ]
        ga_ref[...] = ga
        z = _dot(ga.astype(BF16), wa2_ref[...]) + ba_ref[...]
        la_ref[...] = _log_sigmoid(z) * (1.0 / GLA_TAU)
        c = cos_ref[...]
        s = sin_ref[...]
        fh = _first_half_mask(tm)
        for k in range(4):
            x = proj[:, P_SQ + 128 * k:P_SQ + 128 * (k + 1)]
            sq_ref[:, 128 * k:128 * (k + 1)] = (x * c + _rot_half(x, fh) * s).astype(BF16)
        for k in range(2):
            x = proj[:, P_SK + 128 * k:P_SK + 128 * (k + 1)]
            sk_ref[:, 128 * k:128 * (k + 1)] = (x * c + _rot_half(x, fh) * s).astype(BF16)
        sv_ref[...] = proj[:, P_SV:P_END].astype(BF16)

    def row(w):
        return pl.BlockSpec((tm, w), lambda i: (i, 0))

    def rshape(w, dt):
        return jax.ShapeDtypeStruct((t, w), dt)

    return pl.pallas_call(
        body, name="mix_proj", grid=(t // tm,),
        in_specs=[row(D_MODEL), _full((1, D_MODEL)), _full((D_MODEL, P_END)), _full((128, GLA_KW)),
                  _full((1, GLA_KW)), row(128), row(128)],
        out_specs=[row(D_MODEL), row(256), row(256), row(512), row(512), row(128), row(256), row(512), row(256),
                   row(256)],
        out_shape=[rshape(D_MODEL, BF16), rshape(256, F32), rshape(256, F32), rshape(512, F32), rshape(512, F32),
                   rshape(128, F32), rshape(256, F32), rshape(512, BF16), rshape(256, BF16), rshape(256, BF16)],
        compiler_params=_cparams(1),
    )(h1, wmixpre, winp, wa2p, bap, cos, sin)


def _scan_rows(x, reverse=False):
    n = x.shape[0]
    row = lax.broadcasted_iota(jnp.int32, x.shape, 0)
    s = 1
    while s < n:
        if reverse:
            x = x + jnp.where(row < n - s, pltpu.roll(x, n - s, 0), 0.0)
        else:
            x = x + jnp.where(row >= s, pltpu.roll(x, s, 0), 0.0)
        s *= 2
    return x


def _gla_cumsum(la, tril_f):
    b = _scan_rows(la)
    row = lax.broadcasted_iota(jnp.int32, b.shape, 0)
    bm = jnp.sum(jnp.where(row == GLA_CHUNK // 2 - 1, b, 0.0), axis=0, keepdims=True)
    bl = jnp.sum(jnp.where(row == GLA_CHUNK - 1, b, 0.0), axis=0, keepdims=True)
    return b, bm, bl


def _gla_decays(la, tril_f):
    b, bm, bl = _gla_cumsum(la, tril_f)
    return jnp.exp(b - bm), jnp.exp(bm - b), jnp.exp(b), jnp.exp(bl - b), jnp.exp(bl)


def _gla_masks():
    c = GLA_CHUNK
    r = lax.broadcasted_iota(jnp.int32, (c, c), 0)
    col = lax.broadcasted_iota(jnp.int32, (c, c), 1)
    r4 = lax.broadcasted_iota(jnp.int32, (GLA_HEADS * c, c), 0) % c
    c4 = lax.broadcasted_iota(jnp.int32, (GLA_HEADS * c, c), 1)
    klane = lax.broadcasted_iota(jnp.int32, (c, GLA_KW), 1) // GLA_DK
    vlane = lax.broadcasted_iota(jnp.int32, (c, GLA_W), 1) // GLA_DV
    srow = lax.broadcasted_iota(jnp.int32, (GLA_W, GLA_KW), 0) // GLA_DV
    scol = lax.broadcasted_iota(jnp.int32, (GLA_W, GLA_KW), 1) // GLA_DK
    return dict(tril_f=(r >= col).astype(F32), triu_f=(r <= col).astype(F32), tril4=r4 >= c4,
                khead=[klane == h for h in range(GLA_HEADS)], vhead=[vlane == h for h in range(GLA_HEADS)],
                diag=srow == scol)


def _stack_heads(x, head_masks):
    return jnp.concatenate([jnp.where(m, x, 0.0) for m in head_masks], axis=0)


def _gla_fwd(gq, gk, gv, la):
    t = gq.shape[0]
    rg = _seq_tile(t)
    nb = t // rg
    ncb = rg // GLA_CHUNK
    c = GLA_CHUNK

    def body(q_ref, k_ref, v_ref, la_ref, o_ref, ss_ref, st_ref):
        @pl.when(pl.program_id(0) == 0)
        def _():
            st_ref[...] = jnp.zeros_like(st_ref)

        mk = _gla_masks()
        st = st_ref[...]
        for ch in range(ncb):
            rows = slice(ch * c, (ch + 1) * c)
            eq, ek, eb, ekl, ebl = _gla_decays(la_ref[rows, :], mk["tril_f"])
            qs = q_ref[rows, :] * (GLA_DK ** -0.5)
            k = k_ref[rows, :]
            v = v_ref[rows, :].astype(BF16)
            ss_ref[ch] = st
            q4 = _stack_heads(qs * eq, mk["khead"]).astype(BF16)
            a4 = jnp.where(mk["tril4"], _dg(q4, (k * ek).astype(BF16), NT), 0.0).astype(BF16)
            r4 = _dot(a4, v)
            intra = jnp.concatenate([r4[h * c:(h + 1) * c, GLA_DV * h:GLA_DV * (h + 1)] for h in range(GLA_HEADS)],
                                    axis=1)
            o_ref[rows, :] = intra + _dg((qs * eb).astype(BF16), st.astype(BF16), NT)
            st = st * ebl + jnp.where(mk["diag"], _dg(v, (k * ekl).astype(BF16), TN), 0.0)
        st_ref[...] = st

    def row(w):
        return pl.BlockSpec((rg, w), lambda i: (i, 0))

    return pl.pallas_call(
        body, name="gla_fwd", grid=(nb,),
        in_specs=[row(256), row(256), row(512), row(256)],
        out_specs=[row(512), pl.BlockSpec((ncb, GLA_W, GLA_KW), lambda i: (i, 0, 0))],
        out_shape=[jax.ShapeDtypeStruct((t, GLA_W), F32), jax.ShapeDtypeStruct((nb * ncb, GLA_W, GLA_KW), F32)],
        scratch_shapes=[pltpu.VMEM((GLA_W, GLA_KW), F32)],
        compiler_params=_cparams(1),
    )(gq, gk, gv, la)


def _gla_bwd(gq, gk, gv, la, ss, do):
    t = gq.shape[0]
    rg = _seq_tile(t)
    nb = t // rg
    ncb = rg // GLA_CHUNK
    c = GLA_CHUNK

    def body(q_ref, k_ref, v_ref, la_ref, ss_ref, do_ref, dq_ref, dk_ref, dv_ref, dla_ref, dst_ref):
        @pl.when(pl.program_id(0) == 0)
        def _():
            dst_ref[...] = jnp.zeros_like(dst_ref)

        mk = _gla_masks()
        last_row = lax.broadcasted_iota(jnp.int32, (c, GLA_KW), 0) == c - 1
        scale = GLA_DK ** -0.5
        dstn = dst_ref[...]
        for ch in reversed(range(ncb)):
            rows = slice(ch * c, (ch + 1) * c)
            eq, ek, eb, ekl, ebl = _gla_decays(la_ref[rows, :], mk["tril_f"])
            qs = q_ref[rows, :] * scale
            k = k_ref[rows, :]
            qt, kt, qh, kh = qs * eq, k * ek, qs * eb, k * ekl
            ktb, khb, qhb = kt.astype(BF16), kh.astype(BF16), qh.astype(BF16)
            v = v_ref[rows, :].astype(BF16)
            do_f = do_ref[rows, :]
            dob = do_f.astype(BF16)
            st = ss_ref[ch]
            stb = st.astype(BF16)
            dstb = dstn.astype(BF16)
            q4 = _stack_heads(qt, mk["khead"]).astype(BF16)
            do4 = _stack_heads(do_f, mk["vhead"]).astype(BF16)
            a4 = jnp.where(mk["tril4"], _dg(q4, ktb, NT), 0.0).astype(BF16)
            da4 = jnp.where(mk["tril4"], _dg(do4, v, NT), 0.0).astype(BF16)
            dv_ref[rows, :] = _dg(a4, do4, TN) + _dg(khb, dstb, NT)
            dq4 = _dot(da4, ktb)
            dqt = jnp.zeros((c, GLA_KW), F32)
            for h in range(GLA_HEADS):
                dqt = dqt + jnp.where(mk["khead"][h], dq4[h * c:(h + 1) * c], 0.0)
            dkt = _dg(da4, q4, TN)
            dqh = _dot(dob, stb)
            dkh = _dot(v, dstb)
            dbl = jnp.sum(dstn * st, axis=0, keepdims=True)
            dstn = dstn * ebl + jnp.where(mk["diag"], _dg(dob, qhb, TN), 0.0)
            dq_ref[rows, :] = scale * (dqt * eq + dqh * eb)
            dk_ref[rows, :] = dkt * ek + dkh * ekl
            dkk = dkh * kh
            db = dqt * qt - dkt * kt + dqh * qh - dkk
            db = db + jnp.where(last_row, jnp.sum(dkk, axis=0, keepdims=True) + ebl * dbl, 0.0)
            dla_ref[rows, :] = _scan_rows(db, reverse=True)
        dst_ref[...] = dstn

    def row(w):
        return pl.BlockSpec((rg, w), lambda i: (nb - 1 - i, 0))

    def rshape(w):
        return jax.ShapeDtypeStruct((t, w), F32)

    return pl.pallas_call(
        body, name="gla_bwd", grid=(nb,),
        in_specs=[row(256), row(256), row(512), row(256),
                  pl.BlockSpec((ncb, GLA_W, GLA_KW), lambda i: (nb - 1 - i, 0, 0)), row(512)],
        out_specs=[row(256), row(256), row(512), row(256)],
        out_shape=[rshape(256), rshape(256), rshape(512), rshape(256)],
        scratch_shapes=[pltpu.VMEM((GLA_W, GLA_KW), F32)],
        compiler_params=_cparams(1),
    )(gq, gk, gv, la, ss, do)


SWA_G = SWA_QH // SWA_KVH


def _swa_bias():
    n = jnp.arange(3, dtype=jnp.int32)[:, None, None]
    r = (jnp.arange(SWA_G * BLK, dtype=jnp.int32) % BLK)[None, :, None]
    c = jnp.arange(3 * BLK, dtype=jnp.int32)[None, None, :]
    seg = c // BLK
    cc = c % BLK
    qpos = n * BLK + r - PAD
    kpos = jnp.where(seg == 0, (n - 1) * BLK, jnp.where(seg == 1, n * BLK, 0)) + cc - PAD
    band = (seg < 2) & (kpos >= N_META) & (kpos <= qpos) & (qpos - kpos < WINDOW)
    meta = (seg == 2) & (kpos >= 0) & (kpos < N_META) & (kpos <= qpos)
    return jnp.where(band | meta, 0.0, NEG_INF).astype(F32)


def _swa_stack(ref, rows, kh, lo, dtype):
    parts = []
    for g in range(2):
        pair = ref[rows, 128 * (2 * kh + g):128 * (2 * kh + g + 1)]
        zero = jnp.zeros_like(pair)
        parts += [jnp.where(lo, pair, zero), jnp.where(lo, zero, pair)]
    return jnp.concatenate(parts, axis=0).astype(dtype)


def _swa_unstack(x4, lo):
    return [jnp.where(lo, x4[2 * g * BLK:(2 * g + 1) * BLK], x4[(2 * g + 1) * BLK:(2 * g + 2) * BLK])
            for g in range(2)]


def _swa_sink_col(sink_ref, kh):
    blk = lax.broadcasted_iota(jnp.int32, (SWA_G * BLK, 1), 0) // BLK
    col = jnp.full((SWA_G * BLK, 1), sink_ref[SWA_G * kh + SWA_G - 1], F32)
    for e in reversed(range(SWA_G - 1)):
        col = jnp.where(blk == e, sink_ref[SWA_G * kh + e], col)
    return col


def _swa_softmax(qk, bias, sink):
    s = qk * (SWA_HD ** -0.5) + bias
    m = jnp.maximum(jnp.max(s, axis=-1, keepdims=True), sink)
    p = jnp.exp(s - m)
    es = jnp.exp(sink - m)
    inv = 1.0 / (jnp.sum(p, axis=-1, keepdims=True) + es)
    return p * inv, es * inv


def _swa_keys(prev_ref, cur_ref, first_ref, b, ls):
    before = prev_ref[:, ls] if b == 0 else cur_ref[(b - 1) * BLK:b * BLK, ls]
    return jnp.concatenate([before, cur_ref[b * BLK:(b + 1) * BLK, ls], first_ref[:, ls]], axis=0)


def _swa_specs(rs, ns):
    bps = rs // BLK
    cur = lambda w: pl.BlockSpec((rs, w), lambda i: (jnp.minimum(i, ns - 1), 0))
    prev = lambda w: pl.BlockSpec((BLK, w), lambda i: (jnp.maximum(jnp.minimum(i, ns - 1) * bps - 1, 0), 0))
    first = lambda w: pl.BlockSpec((BLK, w), lambda i: (0, 0))
    return cur, prev, first


def _swa_fwd(sinks, sq, sk, sv):
    t = sq.shape[0]
    rs = _seq_tile(t)
    bps, ns = rs // BLK, t // rs

    def body(sink_ref, bias_ref, q_ref, kp_ref, kc_ref, km_ref, vp_ref, vc_ref, vm_ref, o_ref):
        i = pl.program_id(0)
        lo = lax.broadcasted_iota(jnp.int32, (BLK, 128), 1) < 64
        sink_cols = [_swa_sink_col(sink_ref, kh) for kh in range(SWA_KVH)]
        chains = [(b, kh) for b in range(bps) for kh in range(SWA_KVH)]
        scores = []
        for b, kh in chains:
            ls = slice(128 * kh, 128 * (kh + 1))
            q4 = _swa_stack(q_ref, slice(b * BLK, (b + 1) * BLK), kh, lo, BF16)
            scores.append(_dg(q4, _swa_keys(kp_ref, kc_ref, km_ref, b, ls), NT))
        probs = []
        for (b, kh), s in zip(chains, scores):
            p, _ = _swa_softmax(s, bias_ref[jnp.minimum(i * bps + b, 2)], sink_cols[kh])
            probs.append(p.astype(BF16))
        for (b, kh), p in zip(chains, probs):
            ls = slice(128 * kh, 128 * (kh + 1))
            rows = slice(b * BLK, (b + 1) * BLK)
            for g, pair in enumerate(_swa_unstack(_dot(p, _swa_keys(vp_ref, vc_ref, vm_ref, b, ls)), lo)):
                o_ref[rows, 128 * (2 * kh + g):128 * (2 * kh + g + 1)] = pair

    cur, prev, first = _swa_specs(rs, ns)
    bias = _swa_bias()
    return pl.pallas_call(
        body, name="swa_fwd", grid=(ns,),
        in_specs=[pl.BlockSpec(memory_space=pltpu.SMEM), _full(bias.shape), cur(512), prev(256), cur(256), first(256),
                  prev(256), cur(256), first(256)],
        out_specs=cur(512),
        out_shape=jax.ShapeDtypeStruct((t, SWA_W), F32),
        compiler_params=_cparams(1),
    )(sinks, bias, sq, sk, sk, sk, sv, sv, sv)


def _swa_bwd(sinks, sq, sk, sv, o, do, hook=None):
    t = sq.shape[0]
    rs = _seq_tile(t)
    bps, ns = rs // BLK, t // rs

    def body(sink_ref, bias_ref, q_ref, kp_ref, kc_ref, km_ref, vp_ref, vc_ref, vm_ref, o_ref, do_ref,
             dq_ref, dk_ref, dv_ref, dkm_ref, dvm_ref, dsink_ref, pk_ref, pv_ref):
        i = pl.program_id(0)

        @pl.when(i == 0)
        def _():
            pk_ref[...] = jnp.zeros_like(pk_ref)
            pv_ref[...] = jnp.zeros_like(pv_ref)
            dkm_ref[...] = jnp.zeros_like(dkm_ref)
            dvm_ref[...] = jnp.zeros_like(dvm_ref)
            dsink_ref[...] = jnp.zeros_like(dsink_ref)

        @pl.when(i == ns)
        def _():
            dk_ref[...] = pk_ref[...]
            dv_ref[...] = pv_ref[...]

        @pl.when(i < ns)
        def _():
            lo = lax.broadcasted_iota(jnp.int32, (BLK, 128), 1) < 64
            scale = SWA_HD ** -0.5
            sink_cols = [_swa_sink_col(sink_ref, kh) for kh in range(SWA_KVH)]
            parts_k = [[None] * SWA_KVH for _ in range(bps)]
            parts_v = [[None] * SWA_KVH for _ in range(bps)]
            dsinks = [jnp.zeros((1, 1), F32) for _ in range(SWA_QH)]
            chains = [(b, kh) for b in range(bps) for kh in range(SWA_KVH)]
            lanes = lambda kh: slice(128 * kh, 128 * (kh + 1))
            block = lambda b: slice(b * BLK, (b + 1) * BLK)
            q4s = [_swa_stack(q_ref, block(b), kh, lo, BF16) for b, kh in chains]
            scores = [_dg(q4, _swa_keys(kp_ref, kc_ref, km_ref, b, lanes(kh)), NT)
                      for (b, kh), q4 in zip(chains, q4s)]
            do4s = [_swa_stack(do_ref, block(b), kh, lo, F32) for b, kh in chains]
            do4bs = [d.astype(BF16) for d in do4s]
            dps = [_dg(d, _swa_keys(vp_ref, vc_ref, vm_ref, b, lanes(kh)), NT) for (b, kh), d in zip(chains, do4bs)]
            pbs, dss = [], []
            for n_chain, (b, kh) in enumerate(chains):
                p, psink = _swa_softmax(scores[n_chain], bias_ref[jnp.minimum(i * bps + b, 2)], sink_cols[kh])
                delta = jnp.sum(do4s[n_chain] * _swa_stack(o_ref, block(b), kh, lo, F32), axis=-1, keepdims=True)
                dss.append((p * (dps[n_chain] - delta) * scale).astype(BF16))
                pbs.append(p.astype(BF16))
                dsk = psink * delta
                for e in range(SWA_G):
                    h = SWA_G * kh + e
                    dsinks[h] = dsinks[h] - jnp.sum(dsk[e * BLK:(e + 1) * BLK], axis=0, keepdims=True)
            for n_chain, (b, kh) in enumerate(chains):
                kall = _swa_keys(kp_ref, kc_ref, km_ref, b, lanes(kh))
                for g, pair in enumerate(_swa_unstack(_dot(dss[n_chain], kall), lo)):
                    dq_ref[block(b), 128 * (2 * kh + g):128 * (2 * kh + g + 1)] = pair
                parts_k[b][kh] = _dg(dss[n_chain], q4s[n_chain], TN)
                parts_v[b][kh] = _dg(pbs[n_chain], do4bs[n_chain], TN)
            last = slice(rs - BLK, rs)
            for parts, out_ref, pend_ref, meta_ref in ((parts_k, dk_ref, pk_ref, dkm_ref),
                                                       (parts_v, dv_ref, pv_ref, dvm_ref)):
                for kh in range(SWA_KVH):
                    ls = slice(128 * kh, 128 * (kh + 1))
                    if bps > 1:
                        out_ref[0:rs - BLK, ls] = pend_ref[0:rs - BLK, ls]
                    out_ref[last, ls] = pend_ref[last, ls] + parts[0][kh][0:BLK]
                    meta = parts[0][kh][2 * BLK:3 * BLK]
                    for b in range(bps):
                        own = parts[b][kh][BLK:2 * BLK]
                        if b + 1 < bps:
                            own = own + parts[b + 1][kh][0:BLK]
                            meta = meta + parts[b + 1][kh][2 * BLK:3 * BLK]
                        pend_ref[b * BLK:(b + 1) * BLK, ls] = own
                    meta_ref[:, ls] += meta
            for h in range(SWA_QH):
                dsink_ref[h:h + 1, :] += jnp.broadcast_to(dsinks[h], (1, 128))

    cur, prev, first = _swa_specs(rs, ns)
    late = lambda w: pl.BlockSpec((rs, w), lambda i: (jnp.maximum(i - 1, 0), 0))
    bias = _swa_bias()
    return _pallas(
        body, name="swa_bwd", grid=(ns + 1,),
        in_specs=[pl.BlockSpec(memory_space=pltpu.SMEM), _full(bias.shape), cur(512), prev(256), cur(256), first(256),
                  prev(256), cur(256), first(256), cur(512), cur(512)],
        out_specs=[cur(512), late(256), late(256), first(256), first(256), _full((SWA_QH, 128))],
        out_shape=[jax.ShapeDtypeStruct((t, SWA_W), F32), jax.ShapeDtypeStruct((t, 256), F32),
                   jax.ShapeDtypeStruct((t, 256), F32), jax.ShapeDtypeStruct((BLK, 256), F32),
                   jax.ShapeDtypeStruct((BLK, 256), F32), jax.ShapeDtypeStruct((SWA_QH, 128), F32)],
        scratch_shapes=[pltpu.VMEM((rs, 256), F32), pltpu.VMEM((rs, 256), F32)],
        args=(sinks, bias, sq, sk, sk, sk, sv, sv, sv, o, do), hook=hook)


def _mix_out(h1, ogla, gg, oswa, wgn, wsn, wout, wpost):
    t = h1.shape[0]
    tm = _row_tile(t)

    def body(h_ref, og_ref, gg_ref, os_ref, wgn_ref, wsn_ref, wout_ref, wpost_ref, h2_ref, cat_ref, m_ref):
        parts = []
        for h in range(GLA_HEADS):
            ls = slice(GLA_DV * h, GLA_DV * (h + 1))
            y, _, _ = _rms(og_ref[:, ls], wgn_ref[...])
            g = gg_ref[:, ls]
            parts.append(y * (g * _sigmoid(g)))
        ys, _, _ = _rms(os_ref[...], wsn_ref[...])
        cat = jnp.concatenate(parts + [ys], axis=1).astype(BF16)
        cat_ref[...] = cat
        m = _dot(cat, wout_ref[...])
        m_ref[...] = m
        y, _, _ = _rms(m, wpost_ref[...])
        h2_ref[...] = h_ref[...] + y

    def row(w):
        return pl.BlockSpec((tm, w), lambda i: (i, 0))

    return pl.pallas_call(
        body, name="mix_out", grid=(t // tm,),
        in_specs=[row(D_MODEL), row(512), row(512), row(512), _full((1, GLA_DV)), _full((1, SWA_W)),
                  _full((D_MODEL, D_MODEL)), _full((1, D_MODEL))],
        out_specs=[row(D_MODEL), row(D_MODEL), row(D_MODEL)],
        out_shape=[jax.ShapeDtypeStruct((t, D_MODEL), F32), jax.ShapeDtypeStruct((t, D_MODEL), BF16),
                   jax.ShapeDtypeStruct((t, D_MODEL), F32)],
        compiler_params=_cparams(1),
    )(h1, ogla, gg, oswa, wgn, wsn, wout, wpost)


def _mix_out_bwd(dh2, m, ogla, gg, oswa, wgn, wsn, wout, wpost, hook=None):
    t = dh2.shape[0]
    tm = _row_tile(t)

    def body(dh_ref, m_ref, og_ref, gg_ref, os_ref, wgn_ref, wsn_ref, wout_ref, wpost_ref,
             dog_ref, dgg_ref, dos_ref, dm_ref, dwpost_ref, dwgn_ref, dwsn_ref):
        @pl.when(pl.program_id(0) == 0)
        def _():
            dwpost_ref[...] = jnp.zeros_like(dwpost_ref)
            dwgn_ref[...] = jnp.zeros_like(dwgn_ref)
            dwsn_ref[...] = jnp.zeros_like(dwsn_ref)

        wpost = wpost_ref[...]
        _, mh, r = _rms(m_ref[...], wpost)
        dm, dw = _rms_bwd(mh, r, wpost, dh_ref[...])
        dwpost_ref[...] += dw
        dmb = dm.astype(BF16)
        dm_ref[...] = dmb
        dcat = _dg(dmb, wout_ref[...], NT)
        wgn = wgn_ref[...]
        for h in range(GLA_HEADS):
            ls = slice(GLA_DV * h, GLA_DV * (h + 1))
            dog = dcat[:, ls]
            g = gg_ref[:, ls]
            sg = _sigmoid(g)
            y, xh, r = _rms(og_ref[:, ls], wgn)
            dgg_ref[:, ls] = dog * y * (sg * (1.0 + g * (1.0 - sg)))
            dx, dw = _rms_bwd(xh, r, wgn, dog * (g * sg))
            dog_ref[:, ls] = dx
            dwgn_ref[...] += dw
        wsn = wsn_ref[...]
        _, xh, r = _rms(os_ref[...], wsn)
        dx, dw = _rms_bwd(xh, r, wsn, dcat[:, GLA_W:])
        dos_ref[...] = dx
        dwsn_ref[...] += dw

    def row(w):
        return pl.BlockSpec((tm, w), lambda i: (i, 0))

    def rshape(w, dt=F32):
        return jax.ShapeDtypeStruct((t, w), dt)

    return _pallas(
        body, name="mix_out_bwd", grid=(t // tm,),
        in_specs=[row(D_MODEL), row(D_MODEL), row(512), row(512), row(512), _full((1, GLA_DV)), _full((1, SWA_W)),
                  _full((D_MODEL, D_MODEL)), _full((1, D_MODEL))],
        out_specs=[row(512), row(512), row(512), row(D_MODEL), _full((1, D_MODEL)), _full((1, GLA_DV)),
                   _full((1, SWA_W))],
        out_shape=[rshape(512), rshape(512), rshape(512), rshape(D_MODEL, BF16),
                   jax.ShapeDtypeStruct((1, D_MODEL), F32), jax.ShapeDtypeStruct((1, GLA_DV), F32),
                   jax.ShapeDtypeStruct((1, SWA_W), F32)],
        args=(dh2, m, ogla, gg, oswa, wgn, wsn, wout, wpost), hook=hook)


def _mix_in_bwd(dh2, h1, wmixpre, winp, wa2p, bap, cos, sin, ga, dgq, dgk, dgv, dgg, dla, dsq, dsk, dsv, dkm, dvm):
    t = h1.shape[0]
    tm = _row_tile(t)

    def body(dh2_ref, h_ref, w_ref, win_ref, wa2_ref, ba_ref, cos_ref, sin_ref, ga_ref, dgq_ref, dgk_ref, dgv_ref,
             dgg_ref, dla_ref, dsq_ref, dsk_ref, dsv_ref, dkm_ref, dvm_ref,
             dh1_ref, dproj_ref, dw_ref, dwa2_ref, dba_ref):
        i = pl.program_id(0)

        @pl.when(i == 0)
        def _():
            dw_ref[...] = jnp.zeros_like(dw_ref)
            dwa2_ref[...] = jnp.zeros_like(dwa2_ref)
            dba_ref[...] = jnp.zeros_like(dba_ref)

        first = (i == 0).astype(F32)
        c = cos_ref[...]
        s = -sin_ref[...]
        fh = _first_half_mask(tm)
        dproj_ref[:, P_GQ:P_GK] = dgq_ref[...].astype(BF16)
        dproj_ref[:, P_GK:P_GV] = dgk_ref[...].astype(BF16)
        dproj_ref[:, P_GV:P_GG] = dgv_ref[...].astype(BF16)
        dproj_ref[:, P_GG:P_GA] = dgg_ref[...].astype(BF16)
        gab = ga_ref[...].astype(BF16)
        z = _dot(gab, wa2_ref[...]) + ba_ref[...]
        row_id = i * tm + lax.broadcasted_iota(jnp.int32, (tm, 1), 0)
        dz = jnp.where(row_id >= PAD, dla_ref[...] * (1.0 / GLA_TAU) * (1.0 - _sigmoid(z)), 0.0)
        dzb = dz.astype(BF16)
        dba_ref[...] += jnp.sum(dz, axis=0, keepdims=True)
        dwa2_ref[...] += _dg(gab, dzb, TN)
        dproj_ref[:, P_GA:P_SQ] = _dg(dzb, wa2_ref[...], NT).astype(BF16)
        for k in range(4):
            dy = dsq_ref[:, 128 * k:128 * (k + 1)]
            dproj_ref[:, P_SQ + 128 * k:P_SQ + 128 * (k + 1)] = (dy * c + _rot_half(dy, fh) * s).astype(BF16)
        for k in range(2):
            ls = slice(128 * k, 128 * (k + 1))
            dy = dsk_ref[:, ls]
            dy = jnp.concatenate([dy[:BLK] + first * dkm_ref[:, ls], dy[BLK:]], axis=0) if tm > BLK else (
                dy + first * dkm_ref[:, ls])
            dproj_ref[:, P_SK + 128 * k:P_SK + 128 * (k + 1)] = (dy * c + _rot_half(dy, fh) * s).astype(BF16)
            dv = dsv_ref[:, ls]
            dv = jnp.concatenate([dv[:BLK] + first * dvm_ref[:, ls], dv[BLK:]], axis=0) if tm > BLK else (
                dv + first * dvm_ref[:, ls])
            dproj_ref[:, P_SV + 128 * k:P_SV + 128 * (k + 1)] = dv.astype(BF16)
        dn = _dg(dproj_ref[...], win_ref[...], NT)
        w = w_ref[...]
        _, hh, r = _rms(h_ref[...], w)
        dx, dw = _rms_bwd(hh, r, w, dn)
        dw_ref[...] += dw
        dh1_ref[...] = dh2_ref[...] + dx

    def row(w):
        return pl.BlockSpec((tm, w), lambda i: (i, 0))

    return pl.pallas_call(
        body, name="mix_in_bwd", grid=(t // tm,),
        in_specs=[row(D_MODEL), row(D_MODEL), _full((1, D_MODEL)), _full((D_MODEL, P_END)), _full((128, GLA_KW)),
                  _full((1, GLA_KW)), row(128), row(128), row(128), row(256), row(256), row(512), row(512), row(256),
                  row(512), row(256), row(256), _full((BLK, 256)), _full((BLK, 256))],
        out_specs=[row(D_MODEL), row(P_END), _full((1, D_MODEL)), _full((128, GLA_KW)), _full((1, GLA_KW))],
        out_shape=[jax.ShapeDtypeStruct((t, D_MODEL), F32), jax.ShapeDtypeStruct((t, P_END), BF16),
                   jax.ShapeDtypeStruct((1, D_MODEL), F32), jax.ShapeDtypeStruct((128, GLA_KW), F32),
                   jax.ShapeDtypeStruct((1, GLA_KW), F32)],
        compiler_params=_cparams(1),
    )(dh2, h1, wmixpre, winp, wa2p, bap, cos, sin, ga, dgq, dgk, dgv, dgg, dla, dsq, dsk, dsv, dkm, dvm)


def _adamw_update(w, g, m, v):
    m = ADAM_B1 * m + (1.0 - ADAM_B1) * g
    v = ADAM_B2 * v + (1.0 - ADAM_B2) * (g * g)
    m_hat = m / (1.0 - ADAM_B1 ** ADAM_STEP)
    v_hat = v / (1.0 - ADAM_B2 ** ADAM_STEP)
    return -ADAM_LR * (m_hat / (jnp.sqrt(v_hat) + ADAM_EPS) + ADAM_WD * w), m, v


def _adamw_halves(w, g_mine, g_other, m, v, c_idx, row0=0):
    r, c = w.shape
    h = g_mine.shape[0]
    tr = _div_tile(math.gcd(r, h))
    nth = h // tr
    t0 = row0 // tr
    assert t0 * tr == row0

    def body(c_ref, w_ref, gm_ref, go_ref, m_ref, v_ref, g_ref, d_ref, nm_ref, nv_ref):
        hh = (t0 + pl.program_id(0)) // nth
        g = jnp.where(hh == c_ref[0], gm_ref[...], go_ref[...])
        g_ref[...] = g
        d_ref[...], nm_ref[...], nv_ref[...] = _adamw_update(w_ref[...], g, m_ref[...], v_ref[...])

    spec = pl.BlockSpec((tr, c), lambda i, c_ref: (i, 0))

    def gspec(is_mine):
        def index(i, c_ref):
            used = ((t0 + i) // nth == c_ref[0]) == is_mine
            return (jnp.where(used, (t0 + i) % nth, 0), 0)
        return pl.BlockSpec((tr, c), index)

    shape = jax.ShapeDtypeStruct((r, c), F32)
    return pl.pallas_call(
        body, name="adamw_halves",
        grid_spec=pltpu.PrefetchScalarGridSpec(
            num_scalar_prefetch=1, grid=(r // tr,), in_specs=[spec, gspec(True), gspec(False), spec, spec],
            out_specs=[spec] * 4),
        out_shape=[shape] * 4, compiler_params=_cparams(1),
    )(c_idx, w, g_mine, g_other, m, v)


def _place():
    x, y, c = lax.axis_index("x"), lax.axis_index("y"), lax.axis_index("c")
    chips = [(1 - x, y), (x, 1 - y), (1 - x, 1 - y)]
    return x, y, c, chips


def _remote(send_sem, recv_sem, src, dst, to):
    return pltpu.make_async_remote_copy(src_ref=src, dst_ref=dst, send_sem=send_sem, recv_sem=recv_sem,
                                        device_id=to, device_id_type=MESH)


def _half(ref_rows, c):
    h = ref_rows // 2
    return pl.ds(pl.multiple_of(c * h, 8), h)


def _own_slot(shard, q):
    return lax.dynamic_update_slice(jnp.zeros((N_CHIPS,) + shard.shape, shard.dtype), shard[None], (q, 0, 0))


class _GatherChips:
    has_mid = True

    def __init__(self, bufs):
        n = len(bufs)
        self.inputs = list(bufs)
        self.out_shape = [jax.ShapeDtypeStruct(b.shape, b.dtype) for b in bufs]
        self.aliases = [(t, t) for t in range(n)]
        self.scratch = [pltpu.SemaphoreType.DMA((n, 6)), pltpu.SemaphoreType.DMA((n, 6))]

    def start(self, ins, outs, scr):
        send, recv = scr
        x, y, c, chips = _place()
        q = 2 * x + y
        for t, (i_ref, o_ref) in enumerate(zip(ins, outs)):
            rows = _half(i_ref.shape[1], c)
            for j, (cx, cy) in enumerate(chips):
                _remote(send.at[t, j], recv.at[t, j], i_ref.at[q, rows], o_ref.at[q, rows], (cx, cy, c)).start()

    def mid(self, ins, outs, scr):
        send, recv = scr
        x, y, c, chips = _place()
        for t, o_ref in enumerate(outs):
            rows = _half(o_ref.shape[1], c)
            for j, (cx, cy) in enumerate(chips):
                slot = o_ref.at[2 * cx + cy, rows]
                _remote(send.at[t, j], recv.at[t, j], slot, slot, (cx, cy, c)).wait_recv()
                _remote(send.at[t, 3 + j], recv.at[t, 3 + j], slot, slot, (x, y, 1 - c)).start()

    def finish(self, ins, outs, scr):
        send, recv = scr
        x, y, c, chips = _place()
        for t, o_ref in enumerate(outs):
            mine, other = _half(o_ref.shape[1], c), _half(o_ref.shape[1], 1 - c)
            for j, (cx, cy) in enumerate(chips):
                slot = o_ref.at[2 * cx + cy, other]
                _remote(send.at[t, 3 + j], recv.at[t, 3 + j], slot, slot, (x, y, 1 - c)).wait_recv()
            for j, (cx, cy) in enumerate(chips):
                sent = o_ref.at[2 * cx + cy, mine]
                _remote(send.at[t, j], recv.at[t, j], sent, sent, (cx, cy, c)).wait_send()
                _remote(send.at[t, 3 + j], recv.at[t, 3 + j], sent, sent, (x, y, 1 - c)).wait_send()


class _PairExchange:
    has_mid = False
    aliases = ()

    def __init__(self, arrs):
        n = len(arrs)
        self.inputs = list(arrs)
        self.out_shape = [jax.ShapeDtypeStruct((a.shape[0], a.shape[1] // 2, a.shape[2]), a.dtype) for a in arrs]
        self.scratch = [pltpu.SemaphoreType.DMA((n,)), pltpu.SemaphoreType.DMA((n,))]

    def _copies(self, ins, outs, scr):
        send, recv = scr
        x, y, c, _ = _place()
        return [_remote(send.at[t], recv.at[t], i_ref.at[:, _half(i_ref.shape[1], 1 - c)], o_ref, (x, y, 1 - c))
                for t, (i_ref, o_ref) in enumerate(zip(ins, outs))]

    def start(self, ins, outs, scr):
        for cp in self._copies(ins, outs, scr):
            cp.start()

    def finish(self, ins, outs, scr):
        for cp in self._copies(ins, outs, scr):
            cp.wait()


class _ChipScatter:
    has_mid = False
    aliases = ()

    def __init__(self, arrs):
        n = len(arrs)
        self.inputs = list(arrs)
        self.out_shape = [jax.ShapeDtypeStruct((3,) + a.shape[1:], a.dtype) for a in arrs]
        self.scratch = [pltpu.SemaphoreType.DMA((n, 3)), pltpu.SemaphoreType.DMA((n, 3))]

    def _copies(self, ins, outs, scr):
        send, recv = scr
        x, y, c, chips = _place()
        return [_remote(send.at[t, j], recv.at[t, j], i_ref.at[2 * cx + cy], o_ref.at[j], (cx, cy, c))
                for t, (i_ref, o_ref) in enumerate(zip(ins, outs)) for j, (cx, cy) in enumerate(chips)]

    def start(self, ins, outs, scr):
        for cp in self._copies(ins, outs, scr):
            cp.start()

    def finish(self, ins, outs, scr):
        for cp in self._copies(ins, outs, scr):
            cp.wait()


class _PairShare:
    has_mid = False
    aliases = ()

    def __init__(self, arrs):
        n = len(arrs)
        self.inputs = list(arrs)
        self.out_shape = [jax.ShapeDtypeStruct(a.shape, a.dtype) for a in arrs]
        self.scratch = [pltpu.SemaphoreType.DMA((n,)), pltpu.SemaphoreType.DMA((n,))]

    def _copies(self, ins, outs, scr):
        send, recv = scr
        x, y, c, _ = _place()
        return [_remote(send.at[t], recv.at[t], i_ref, o_ref, (x, y, 1 - c))
                for t, (i_ref, o_ref) in enumerate(zip(ins, outs))]

    def start(self, ins, outs, scr):
        for cp in self._copies(ins, outs, scr):
            cp.start()

    def finish(self, ins, outs, scr):
        for cp in self._copies(ins, outs, scr):
            cp.wait()


def _comm_call(hook, name):
    n_in, n_out = len(hook.inputs), len(hook.out_shape)

    def body(*refs):
        ins, outs, scr = refs[:n_in], refs[n_in:n_in + n_out], refs[n_in + n_out:]
        hook.start(ins, outs, scr)
        if hook.has_mid:
            hook.mid(ins, outs, scr)
        hook.finish(ins, outs, scr)

    return pl.pallas_call(body, name=name, in_specs=[ANY] * n_in, out_specs=[ANY] * n_out,
                          out_shape=list(hook.out_shape), scratch_shapes=list(hook.scratch),
                          input_output_aliases=dict(hook.aliases))(*hook.inputs)


def _all_gather_devices(vecs):
    n = len(vecs)

    def body(*refs):
        x_refs, out_refs = refs[:n], refs[n:2 * n]
        send_sems, recv_sems, local_sems = refs[2 * n:]
        x, y, c, chips = _place()
        me, sibling = (x, y, c), (x, y, 1 - c)
        waits = []
        for t, (x_ref, out_ref) in enumerate(zip(x_refs, out_refs)):
            def slot(px, py, pc, out_ref=out_ref):
                return out_ref.at[4 * px + 2 * py + pc]

            def copy(k, block, to, src=None, t=t, slot=slot):
                return pltpu.make_async_remote_copy(
                    src_ref=slot(*block) if src is None else src, dst_ref=slot(*block), send_sem=send_sems.at[t, k],
                    recv_sem=recv_sems.at[t, k], device_id=to, device_id_type=MESH)

            mine = pltpu.make_async_copy(x_ref, slot(*me), local_sems.at[t])
            mine.start()
            first = [copy(0, me, sibling, src=x_ref)]
            first += [copy(1 + j, me, (*chip, c), src=x_ref) for j, chip in enumerate(chips)]
            for cp in first:
                cp.start()
            waits.append((copy, mine, first))
        for copy, mine, first in waits:
            passed = [copy(4 + j, (*chip, c), sibling) for j, chip in enumerate(chips)]
            for j, chip in enumerate(chips):
                copy(1 + j, (*chip, c), me).wait_recv()
                passed[j].start()
            copy(0, sibling, me).wait_recv()
            for j, chip in enumerate(chips):
                copy(4 + j, (*chip, 1 - c), me).wait_recv()
            for cp in first + passed:
                cp.wait_send()
            mine.wait()

    vmem = pl.BlockSpec(memory_space=pltpu.VMEM)
    return pl.pallas_call(
        body, name="all_gather_devices", in_specs=[vmem] * n, out_specs=[vmem] * n,
        out_shape=[jax.ShapeDtypeStruct((N_DEV,) + v.shape, v.dtype) for v in vecs],
        scratch_shapes=[pltpu.SemaphoreType.DMA((n, 7)), pltpu.SemaphoreType.DMA((n, 7)),
                        pltpu.SemaphoreType.DMA((n,))],
    )(*vecs)


def _pair_sum(g, other, c_idx):
    nq, r, w = g.shape
    h = r // 2
    tr = _div_tile(h)
    nt = h // tr

    def body(c_ref, g_ref, o_ref, s_ref):
        s_ref[...] = (g_ref[...].astype(F32) + o_ref[...].astype(F32)).astype(s_ref.dtype)

    return pl.pallas_call(
        body, name="pair_sum",
        grid_spec=pltpu.PrefetchScalarGridSpec(
            num_scalar_prefetch=1, grid=(nq, nt),
            in_specs=[pl.BlockSpec((None, tr, w), lambda k, i, c_ref: (k, c_ref[0] * nt + i, 0)),
                      pl.BlockSpec((None, tr, w), lambda k, i, c_ref: (k, i, 0))],
            out_specs=pl.BlockSpec((None, tr, w), lambda k, i, c_ref: (k, i, 0))),
        out_shape=jax.ShapeDtypeStruct((nq, h, w), g.dtype),
        compiler_params=_cparams(2),
    )(c_idx, g, other)


def _chip_sum(s, others, q_idx):
    _, h, w = s.shape
    tr = _div_tile(h)

    def body(q_ref, s_ref, o_ref, out_ref):
        out_ref[...] = ((s_ref[...].astype(F32) + o_ref[0].astype(F32)) + o_ref[1].astype(F32)) + o_ref[2].astype(F32)

    return pl.pallas_call(
        body, name="chip_sum",
        grid_spec=pltpu.PrefetchScalarGridSpec(
            num_scalar_prefetch=1, grid=(h // tr,),
            in_specs=[pl.BlockSpec((None, tr, w), lambda i, q_ref: (q_ref[0], i, 0)),
                      pl.BlockSpec((3, tr, w), lambda i, q_ref: (0, i, 0))],
            out_specs=pl.BlockSpec((tr, w), lambda i, q_ref: (i, 0))),
        out_shape=jax.ShapeDtypeStruct((h, w), F32),
        compiler_params=_cparams(1),
    )(q_idx, s, others)


def _small_update(q_idx, parts, ws, ms, vs, col_block):
    n = len(parts)
    has_w = [w is not None for w in ws]

    def body(q_ref, *refs):
        pos = 0
        ins = []
        for t in range(n):
            k = 4 if has_w[t] else 1
            ins.append(refs[pos:pos + k])
            pos += k
        outs = refs[pos:]
        opos = 0
        for t in range(n):
            p_ref = ins[t][0]
            g = p_ref[0]
            for s in range(1, p_ref.shape[0]):
                g = g + p_ref[s]
            if has_w[t]:
                _, w_ref, m_ref, v_ref = ins[t]
                g_ref, d_ref, nm_ref, nv_ref = outs[opos:opos + 4]
                opos += 4
                g_ref[...] = g
                d_ref[...], nm_ref[...], nv_ref[...] = _adamw_update(w_ref[...], g, m_ref[...], v_ref[...])
            else:
                outs[opos][...] = g
                opos += 1

    def whole(shape):
        nd = len(shape)
        return pl.BlockSpec(shape, lambda i, q_ref: (0,) * nd)

    in_specs, out_specs, out_shape, args = [], [], [], []
    for t in range(n):
        k, r, wf = parts[t].shape
        if col_block[t]:
            w = wf // N_CHIPS
            in_specs.append(pl.BlockSpec((k, r, w), lambda i, q_ref: (0, 0, q_ref[0])))
        else:
            w = wf
            in_specs.append(whole((k, r, wf)))
        args.append(parts[t])
        if has_w[t]:
            assert ws[t].shape == (r, w), (ws[t].shape, r, w)
            in_specs += [whole((r, w))] * 3
            args += [ws[t], ms[t], vs[t]]
            out_specs += [whole((r, w))] * 4
            out_shape += [jax.ShapeDtypeStruct((r, w), F32)] * 4
        else:
            out_specs.append(whole((r, w)))
            out_shape.append(jax.ShapeDtypeStruct((r, w), F32))
    return pl.pallas_call(
        body, name="small_update",
        grid_spec=pltpu.PrefetchScalarGridSpec(num_scalar_prefetch=1, grid=(1,), in_specs=in_specs,
                                               out_specs=out_specs),
        out_shape=out_shape, compiler_params=_cparams(1),
    )(q_idx, *args)


_PACK_SEGMENTS = ((0, 1552), None, (1552, 2064), (2064, 2128), (2064, 2128), (2128, 2192), (2128, 2192),
                  (2192, 2256), (2192, 2256), (2256, 2320), (2256, 2320))
_UNPACK_SEGMENTS = (((0, 1552), (0,)), ((1552, 2064), (P_SQ,)), ((2064, 2128), (P_SK, P_SK + 64)),
                    ((2128, 2192), (P_SK + 128, P_SK + 192)), ((2192, 2256), (P_SV, P_SV + 64)),
                    ((2256, 2320), (P_SV + 128, P_SV + 192)))


def _pack_win(w4):
    per = w4.shape[2]
    pieces = []
    for seg in _PACK_SEGMENTS:
        if seg is None:
            pieces.append(jnp.zeros((w4.shape[1], 128 - GLA_RANK), w4.dtype))
            continue
        for q in range(w4.shape[0]):
            lo, hi = max(seg[0], q * per), min(seg[1], (q + 1) * per)
            if lo < hi:
                pieces.append(w4[q][:, lo - q * per:hi - q * per])
    return jnp.concatenate(pieces, axis=1)


def _unpack_dwin(d):
    per = D_IN // N_CHIPS
    chips = []
    for q in range(N_CHIPS):
        pieces = []
        for (a, b), starts in _UNPACK_SEGMENTS:
            lo, hi = max(a, q * per), min(b, (q + 1) * per)
            if lo < hi:
                copies = [d[:, s + lo - a:s + hi - a] for s in starts]
                pieces.append(copies[0] if len(copies) == 1 else copies[0] + copies[1])
        chips.append(jnp.concatenate(pieces, axis=1))
    return jnp.stack(chips).astype(BF16)


def _local_step(x, target, meta, p):
    s = x.shape[0]
    t = s + BLK
    h0 = jnp.concatenate([jnp.zeros((PAD, D_MODEL), F32), meta, x], axis=0)
    cos, sin = _rope_tables(t)

    h1, n1, g1, u1, a1, f1 = _ffn_fwd(h0, p["ffn1_pre_norm"], p["ffn1_w"], p["ffn1_post_norm"])
    n2, gq, gk, gv, gg, ga, la, sq, sk, sv = _mix_proj(h1, p["mix_pre_norm"], p["w_in"], p["gla_w_a2"], p["gla_b_a"],
                                                       cos, sin)
    ogla, ss = _gla_fwd(gq, gk, gv, la)
    oswa = _swa_fwd(p["swa_sinks"], sq, sk, sv)
    h2, cat, m = _mix_out(h1, ogla, gg, oswa, p["gla_out_norm"], p["swa_out_norm"], p["w_out"], p["mix_post_norm"])
    grads = {}
    dy, n3, g3, u3, a3, df3, grads["ffn2_post_norm"], sse = _ffn_fwd(
        h2, p["ffn2_pre_norm"], p["ffn2_w"], p["ffn2_post_norm"], target=target)

    dh2, dg3, du3, grads["ffn2_pre_norm"] = _ffn_bwd(
        dy, h2, None, g3, u3, p["ffn2_pre_norm"], p["ffn2_w"], p["ffn2_post_norm"], df=df3)
    (gud,) = _ffn_wgrad(n3, df3, dg3, du3, a3)
    grads["ffn2_w_gate"], grads["ffn2_w_up"], grads["ffn2_w_down"] = gud[:, :FJ], gud[:, FJ:2 * FJ], gud[:, 2 * FJ:]

    dogla, dgg, doswa, dm, grads["mix_post_norm"], grads["gla_out_norm"], grads["swa_out_norm"] = _mix_out_bwd(
        dh2, m, ogla, gg, oswa, p["gla_out_norm"], p["swa_out_norm"], p["w_out"], p["mix_post_norm"])
    grads["w_out"] = _xty(cat, dm)
    dsq, dsk, dsv, dkm, dvm, dsinks = _swa_bwd(p["swa_sinks"], sq, sk, sv, oswa, doswa)
    grads["swa_sinks"] = dsinks[:, 0]
    dgq, dgk, dgv, dla = _gla_bwd(gq, gk, gv, la, ss, dogla)
    dh1, dproj, grads["mix_pre_norm"], dwa2p, grads["gla_b_a"] = _mix_in_bwd(
        dh2, h1, p["mix_pre_norm"], p["w_in"], p["gla_w_a2"], p["gla_b_a"], cos, sin, ga, dgq, dgk, dgv, dgg, dla,
        dsq, dsk, dsv, dkm, dvm)
    grads["gla_w_a2"] = dwa2p[:GLA_RANK]
    grads["w_in"] = _unpack_dwin(_xty(n2, dproj))

    dh0, df1, dg1, du1, grads["ffn1_pre_norm"], grads["ffn1_post_norm"] = _ffn_bwd(
        dh1, h0, f1, g1, u1, p["ffn1_pre_norm"], p["ffn1_w"], p["ffn1_post_norm"])
    (gud,) = _ffn_wgrad(n1, df1, dg1, du1, a1)
    grads["ffn1_w_gate"], grads["ffn1_w_up"], grads["ffn1_w_down"] = gud[:, :FJ], gud[:, FJ:2 * FJ], gud[:, 2 * FJ:]
    grads["meta_tokens"] = dh0[PAD:BLK]
    return sse[0, 0], dh0[BLK:], grads


WEIGHTS = ['meta_tokens', 'ffn1_pre_norm', 'ffn1_w_gate', 'ffn1_w_up', 'ffn1_w_down', 'ffn1_post_norm',
           'mix_pre_norm', 'w_in', 'gla_w_a2', 'gla_b_a', 'gla_out_norm', 'swa_sinks', 'swa_out_norm', 'w_out',
           'mix_post_norm', 'ffn2_pre_norm', 'ffn2_w_gate', 'ffn2_w_up', 'ffn2_w_down', 'ffn2_post_norm']
BIG = ['ffn1_w_gate', 'ffn1_w_up', 'ffn1_w_down', 'w_in', 'w_out', 'ffn2_w_gate', 'ffn2_w_up', 'ffn2_w_down']
SMALL = [n for n in WEIGHTS if n not in BIG]
FJ = D_FF // N_CHIPS
D_IN_J = D_IN // N_CHIPS
D_OUT_J = D_MODEL // N_CHIPS
TRANSPOSED = ('ffn1_w_gate', 'ffn1_w_up', 'ffn2_w_gate', 'ffn2_w_up')


def _shard2d(name, a):
    return a[0].T if name in TRANSPOSED else a[0]


def _unshard2d(name, a):
    return (a.T if name in TRANSPOSED else a)[None]


def kernel(x, meta_tokens, ffn1_pre_norm, ffn1_w_gate, ffn1_w_up, ffn1_w_down, ffn1_post_norm, mix_pre_norm, w_in, gla_w_a2, gla_b_a, gla_out_norm, swa_sinks, swa_out_norm, w_out, mix_post_norm, ffn2_pre_norm, ffn2_w_gate, ffn2_w_up, ffn2_w_down, ffn2_post_norm, loss_target, m_meta_tokens, m_ffn1_pre_norm, m_ffn1_w_gate, m_ffn1_w_up, m_ffn1_w_down, m_ffn1_post_norm, m_mix_pre_norm, m_w_in, m_gla_w_a2, m_gla_b_a, m_gla_out_norm, m_swa_sinks, m_swa_out_norm, m_w_out, m_mix_post_norm, m_ffn2_pre_norm, m_ffn2_w_gate, m_ffn2_w_up, m_ffn2_w_down, m_ffn2_post_norm, v_meta_tokens, v_ffn1_pre_norm, v_ffn1_w_gate, v_ffn1_w_up, v_ffn1_w_down, v_ffn1_post_norm, v_mix_pre_norm, v_w_in, v_gla_w_a2, v_gla_b_a, v_gla_out_norm, v_swa_sinks, v_swa_out_norm, v_w_out, v_mix_post_norm, v_ffn2_pre_norm, v_ffn2_w_gate, v_ffn2_w_up, v_ffn2_w_down, v_ffn2_post_norm):
    args = dict(locals())
    w = {n: args[n] for n in WEIGHTS}
    mom = {n: args["m_" + n] for n in WEIGHTS}
    var = {n: args["v_" + n] for n in WEIGHTS}
    cx, cy, cc = lax.axis_index("x"), lax.axis_index("y"), lax.axis_index("c")
    q_idx = (2 * cx + cy).astype(jnp.int32).reshape(1)
    c_idx = cc.astype(jnp.int32).reshape(1)

    q_chip = 2 * cx + cy
    bf = {n: _own_slot(_shard2d(n, w[n]).astype(BF16), q_chip) for n in ("w_in", "w_out")}
    for ffn in ("ffn1", "ffn2"):
        stacked = jnp.concatenate([_shard2d(ffn + s, w[ffn + s]) for s in ("_w_gate", "_w_up", "_w_down")], axis=0)
        bf[ffn] = _own_slot(stacked.astype(BF16), q_chip)
    qc_idx = jnp.stack([q_chip, cc]).astype(jnp.int32)
    early = _GatherChips([_own_slot(w["meta_tokens"], q_chip),
                          _own_slot(w["gla_w_a2"].reshape(GLA_RANK, GLA_KW // N_CHIPS), q_chip)])
    meta4, wa24 = _comm_call(early, "gather_small")
    meta_full = meta4.transpose(1, 0, 2).reshape(N_META, D_MODEL)
    wa2p = jnp.pad(wa24.transpose(1, 0, 2).reshape(GLA_RANK, GLA_KW), ((0, 128 - GLA_RANK), (0, 0))).astype(BF16)
    sinks = w["swa_sinks"].reshape(SWA_QH)

    seq, target = x[0], loss_target[0]
    t = seq.shape[0] + BLK
    h0, n1 = _embed_norm(seq, meta_full, w["ffn1_pre_norm"])
    cos, sin = _rope_tables(t)
    late = _GatherChips([bf["w_in"], bf["w_out"], bf["ffn2"]])
    (h1, g1, u1, a1, f1), (w31,), (win4, wout4, w32) = _ffn_fwd_gather(
        h0, n1, bf["ffn1"], w["ffn1_post_norm"], qc_idx, late)
    winp = _pack_win(win4)
    wout = wout4.reshape(D_MODEL, D_MODEL)
    n2, gq, gk, gv, gg, ga, la, sq, sk, sv = _mix_proj(h1, w["mix_pre_norm"], winp, wa2p, w["gla_b_a"], cos, sin)
    ogla, ss = _gla_fwd(gq, gk, gv, la)
    oswa = _swa_fwd(sinks, sq, sk, sv)
    h2, cat, m = _mix_out(h1, ogla, gg, oswa, w["gla_out_norm"], w["swa_out_norm"], wout, w["mix_post_norm"])
    g = {}
    dy, n3, g3, u3, a3, df3, g["ffn2_post_norm"], sse = _ffn_fwd(
        h2, w["ffn2_pre_norm"], w32, w["ffn2_post_norm"], target=target)

    dh2, dg3, du3, g["ffn2_pre_norm"] = _ffn_bwd(
        dy, h2, None, g3, u3, w["ffn2_pre_norm"], w32, w["ffn2_post_norm"], df=df3)
    (gf2,) = _ffn_wgrad(n3, df3, dg3, du3, a3)
    (dogla, dgg, doswa, dm, g["mix_post_norm"], g["gla_out_norm"], g["swa_out_norm"]), (rgf2,) = _mix_out_bwd(
        dh2, m, ogla, gg, oswa, w["gla_out_norm"], w["swa_out_norm"], wout, w["mix_post_norm"],
        hook=_PairExchange([gf2]))
    sgf2 = _pair_sum(gf2, rgf2, c_idx)
    gout = _xty(cat, dm).reshape(N_CHIPS, D_OUT_J, D_MODEL).astype(BF16)
    (dsq, dsk, dsv, dkm, dvm, dsinks), (ogf2,) = _swa_bwd(sinks, sq, sk, sv, oswa, doswa,
                                                          hook=_ChipScatter([sgf2]))
    g["swa_sinks"] = dsinks
    dgq, dgk, dgv, dla = _gla_bwd(gq, gk, gv, la, ss, dogla)
    dh1, dproj, g["mix_pre_norm"], dwa2p, g["gla_b_a"] = _mix_in_bwd(
        dh2, h1, w["mix_pre_norm"], winp, wa2p, w["gla_b_a"], cos, sin, ga, dgq, dgk, dgv, dgg, dla,
        dsq, dsk, dsv, dkm, dvm)
    g["gla_w_a2"] = dwa2p[:GLA_RANK]
    gin = _unpack_dwin(_xty(n2, dproj))
    (dh0, df1, dg1, du1, g["ffn1_pre_norm"], g["ffn1_post_norm"]), (rgin, rgout) = _ffn_bwd(
        dh1, h0, f1, g1, u1, w["ffn1_pre_norm"], w31, w["ffn1_post_norm"],
        hook=_PairExchange([gin, gout]))
    sgin, sgout = _pair_sum(gin, rgin, c_idx), _pair_sum(gout, rgout, c_idx)
    own1, others1, (ogin, ogout) = _ffn_wgrad_reduce(n1, df1, dg1, du1, a1, qc_idx, _ChipScatter([sgin, sgout]))
    g["meta_tokens"] = dh0[PAD:BLK]
    grad_x = dh0[BLK:]
    halves = [_chip_sum(own1[None], others1, jnp.zeros((1,), jnp.int32))]
    halves += [_chip_sum(s, o, q_idx) for s, o in ((sgin, ogin), (sgout, ogout), (sgf2, ogf2))]
    others = _comm_call(_PairShare(halves), "pair_share")
    reduced = {"ffn1_w_gate": (0, 0), "ffn1_w_up": (0, FJ), "ffn1_w_down": (0, 2 * FJ), "w_in": (1, 0),
               "w_out": (2, 0), "ffn2_w_gate": (3, 0), "ffn2_w_up": (3, FJ), "ffn2_w_down": (3, 2 * FJ)}
    grad, delta, new_m, new_v = {}, {}, {}, {}
    for n in BIG:
        k, row0 = reduced[n]
        outs = _adamw_halves(_shard2d(n, w[n]), halves[k], others[k], _shard2d(n, mom[n]), _shard2d(n, var[n]),
                             c_idx, row0)
        grad[n], delta[n], new_m[n], new_v[n] = [_unshard2d(n, a) for a in outs]

    late = ["gla_w_a2", "swa_sinks"]
    direct = [n for n in SMALL if n not in late]
    names = direct + late
    gathered = _all_gather_devices([g[n] for n in names] + [sse])
    mat = lambda a: a.reshape(a.shape[-2:])
    none3 = [None] * (len(late) + 1)
    outs = _small_update(q_idx, gathered, [mat(w[n]) for n in direct] + none3, [mat(mom[n]) for n in direct] + none3,
                         [mat(var[n]) for n in direct] + none3, [n == "meta_tokens" for n in names] + [False])
    sum_a2, sum_sinks, sum_sse = outs[4 * len(direct):]
    loss = sum_sse[0, 0] * (0.5 / D_MODEL)
    g_late = [lax.dynamic_slice_in_dim(sum_a2, q_chip * (GLA_KW // N_CHIPS), GLA_KW // N_CHIPS, axis=1)[None],
              sum_sinks[:, 0].reshape(1, 1, SWA_QH)]
    outs = list(outs[:4 * len(direct)]) + list(_small_update(
        q_idx, g_late, [mat(w[n]) for n in late], [mat(mom[n]) for n in late], [mat(var[n]) for n in late],
        [False, False]))
    for k, n in enumerate(names):
        grad[n], delta[n], new_m[n], new_v[n] = [a.reshape(w[n].shape) for a in outs[4 * k:4 * k + 4]]

    return (loss, grad_x[None], *[grad[n] for n in WEIGHTS], *[delta[n] for n in WEIGHTS],
            *[new_m[n] for n in WEIGHTS], *[new_v[n] for n in WEIGHTS])
```

```python
import functools
import math

import numpy as np
import jax
import jax.numpy as jnp
from jax import lax
from jax.experimental import pallas as pl
from jax.experimental.pallas import tpu as pltpu

F32 = jnp.float32
BF16 = jnp.bfloat16
MESH = pl.DeviceIdType.MESH

D_MODEL = 1024
D_FF = 2816
N_CHIPS = 4
N_DEV = 8
N_META = 16
BLK = 128
PAD = BLK - N_META
GLA_CHUNK = 64
GLA_HEADS = 4
GLA_DV = 128
GLA_DK = 64
GLA_KW = GLA_HEADS * GLA_DK
GLA_W = GLA_HEADS * GLA_DV
GLA_RANK = 16
GLA_TAU = 16.0
SWA_HD = 64
SWA_QH = 8
SWA_KVH = 2
SWA_W = SWA_QH * SWA_HD
WINDOW = 128
ROPE_THETA = 10000.0
EPS = 1e-6
NEG_INF = -1e30
IN_SPLITS = (256, 256, 512, 512, 16, 512, 128, 128)
D_IN = sum(IN_SPLITS)
P_GQ, P_GK, P_GV, P_GG, P_GA, P_SQ, P_SK, P_SV, P_END = 0, 256, 512, 1024, 1536, 1664, 2176, 2432, 2688
ADAM_LR, ADAM_B1, ADAM_B2, ADAM_EPS, ADAM_WD, ADAM_STEP = 0.001, 0.9, 0.999, 1e-08, 0.01, 10
VMEM_LIMIT = 56 * 1024 * 1024

NT = (((1,), (1,)), ((), ()))
TN = (((0,), (0,)), ((), ()))


def _cparams(n_axes):
    return pltpu.CompilerParams(dimension_semantics=("arbitrary",) * n_axes, vmem_limit_bytes=VMEM_LIMIT)


def _row_tile(t):
    for tm in (640, 512, 384, 256, 128):
        if t % tm == 0:
            return tm
    raise ValueError(t)


SEQ_BLOCKS_PER_STEP = 5


def _seq_tile(t):
    return SEQ_BLOCKS_PER_STEP * BLK if t % (SEQ_BLOCKS_PER_STEP * BLK) == 0 else BLK


ROW_PARTS = 2


def _row_parts(tm):
    n = ROW_PARTS if tm % (16 * ROW_PARTS) == 0 else 1
    return [slice(k * (tm // n), (k + 1) * (tm // n)) for k in range(n)]


def _contract_tile(t):
    return 1664 if t % 1664 == 0 else _row_tile(t)


def _div_tile(r, cap=512):
    best = None
    for tr in range(8, min(r, cap) + 1, 8):
        if r % tr == 0:
            best = tr
    return best if best is not None else r


def _dot(a, b):
    return jnp.dot(a, b, preferred_element_type=F32)


def _dg(a, b, dims):
    return lax.dot_general(a, b, dims, preferred_element_type=F32)


def _rms(x, w):
    r = lax.rsqrt(jnp.mean(x * x, axis=-1, keepdims=True) + EPS)
    xh = x * r
    return xh * w, xh, r


def _rms_bwd(xh, r, w, dy):
    wdy = dy * w
    dx = r * (wdy - xh * jnp.mean(wdy * xh, axis=-1, keepdims=True))
    dw = jnp.sum(dy * xh, axis=0, keepdims=True)
    return dx, dw


def _sigmoid(x):
    return 1.0 / (1.0 + jnp.exp(-x))


def _full(shape):
    nd = len(shape)
    return pl.BlockSpec(shape, lambda *_: (0,) * nd)


ANY = pl.BlockSpec(memory_space=pl.ANY)


def _pallas(body, *, name, grid, in_specs, out_specs, out_shape, args, scratch_shapes=(), hook=None):
    n_axes = len(grid)
    if hook is None:
        return pl.pallas_call(body, name=name, grid=grid, in_specs=list(in_specs), out_specs=list(out_specs),
                              out_shape=list(out_shape), scratch_shapes=list(scratch_shapes),
                              compiler_params=_cparams(n_axes))(*args)
    n_in, n_out, n_scr = len(in_specs), len(out_specs), len(scratch_shapes)
    h_in, h_out = len(hook.inputs), len(hook.out_shape)
    total = math.prod(grid)

    def wrapped(*refs):
        ins, hins = refs[:n_in], refs[n_in:n_in + h_in]
        o0 = n_in + h_in
        outs, houts = refs[o0:o0 + n_out], refs[o0 + n_out:o0 + n_out + h_out]
        s0 = o0 + n_out + h_out
        scr, hscr = refs[s0:s0 + n_scr], refs[s0 + n_scr:]
        step = pl.program_id(0)
        for a in range(1, n_axes):
            step = step * grid[a] + pl.program_id(a)

        @pl.when(step == 0)
        def _():
            hook.start(hins, houts, hscr)

        body(*ins, *outs, *scr)

        if hook.has_mid:
            @pl.when(step == (3 * total) // 4)
            def _():
                hook.mid(hins, houts, hscr)

        @pl.when(step == total - 1)
        def _():
            hook.finish(hins, houts, hscr)

    res = pl.pallas_call(
        wrapped, name=name, grid=grid, in_specs=list(in_specs) + [ANY] * h_in,
        out_specs=list(out_specs) + [ANY] * h_out, out_shape=list(out_shape) + list(hook.out_shape),
        scratch_shapes=list(scratch_shapes) + list(hook.scratch), compiler_params=_cparams(n_axes),
        input_output_aliases={n_in + a: n_out + b for a, b in hook.aliases},
    )(*args, *hook.inputs)
    return res[:n_out], res[n_out:]


def _ffn_weight_specs(w3):
    fj = w3.shape[1] // 3
    return fj, [pl.BlockSpec((None, fj, D_MODEL), functools.partial(lambda i, j, k: (j, k, 0), k=k)) for k in range(3)]


def _ffn_fwd(h, wpre, w3, wpost, hook=None, target=None):
    t = h.shape[0]
    tm = _row_tile(t)
    nj, rows3, _ = w3.shape
    fj = rows3 // 3
    nblk = tm // BLK if target is not None else 0

    def body(*refs):
        h_ref, wpre_ref, w_hbm, wpost_ref = refs[:4]
        t_refs = refs[4:4 + nblk]
        hout_ref, n_ref, p1_ref, p2_ref, a_ref, f_ref = refs[4 + nblk:10 + nblk]
        acc_ref, wv, wsem = refs[-3:]
        i = pl.program_id(0)
        j = pl.program_id(1)

        @pl.when((i == 0) & (j == 0))
        def _():
            for k in range(nj):
                pltpu.make_async_copy(w_hbm.at[k], wv.at[k], wsem.at[k]).start()

        @pl.when(i == 0)
        def _():
            pltpu.make_async_copy(w_hbm.at[j], wv.at[j], wsem.at[j]).wait()

        @pl.when(j == 0)
        def _():
            y, _, _ = _rms(h_ref[...], wpre_ref[...])
            n_ref[...] = y.astype(BF16)
            acc_ref[...] = jnp.zeros_like(acc_ref)

        if target is not None:
            dwpost_ref, sse_ref = refs[10 + nblk:12 + nblk]

            @pl.when((i == 0) & (j == 0))
            def _():
                dwpost_ref[...] = jnp.zeros_like(dwpost_ref)
                sse_ref[...] = jnp.zeros_like(sse_ref)

        n = n_ref[...]
        g = _dg(n, wv[j, 0:fj], NT)
        u = _dg(n, wv[j, fj:2 * fj], NT)
        sg = _sigmoid(g)
        silu = g * sg
        p1_ref[...] = (u * (sg + silu * (1.0 - sg))).astype(BF16)
        p2_ref[...] = silu.astype(BF16)
        a = (silu * u).astype(BF16)
        a_ref[...] = a
        acc_ref[...] += _dot(a, wv[j, 2 * fj:3 * fj])

        @pl.when(j == nj - 1)
        def _():
            f = acc_ref[...]
            wpost = wpost_ref[...]
            y, fh, r = _rms(f, wpost)
            hout = h_ref[...] + 0.5 * y
            if target is None:
                f_ref[...] = f
                hout_ref[...] = hout
            else:
                sse = jnp.zeros((1, 1), F32)
                errs = []
                for k in range(nblk):
                    err = hout[k * BLK:(k + 1) * BLK] - t_refs[k][...]
                    if k == 0:
                        err = jnp.where(i > 0, err, 0.0)
                    errs.append(err)
                    sse = sse + jnp.sum(jnp.sum(err * err, axis=1, keepdims=True), axis=0, keepdims=True)
                dy = (jnp.concatenate(errs, axis=0) if nblk > 1 else errs[0]) * (1.0 / D_MODEL)
                hout_ref[...] = dy
                df, dw = _rms_bwd(fh, r, wpost, 0.5 * dy)
                f_ref[...] = df.astype(BF16)
                dwpost_ref[...] += dw
                sse_ref[...] += jnp.broadcast_to(sse, sse_ref.shape)

    row = pl.BlockSpec((tm, D_MODEL), lambda i, j: (i, 0))
    vec = pl.BlockSpec((1, D_MODEL), lambda i, j: (0, 0))
    act = pl.BlockSpec((None, tm, fj), lambda i, j: (j, i, 0))
    t_specs = [pl.BlockSpec((BLK, D_MODEL), functools.partial(lambda i, j, k: (jnp.maximum(nblk * i + k - 1, 0), 0), k=k))
               for k in range(nblk)]
    loss_spec = [vec, _full((1, 128))] if target is not None else []
    loss_shape = [jax.ShapeDtypeStruct((1, D_MODEL), F32), jax.ShapeDtypeStruct((1, 128), F32)] if (
        target is not None) else []
    return _pallas(
        body, name="ffn_fwd", grid=(t // tm, nj),
        in_specs=[row, vec, ANY, vec] + t_specs,
        out_specs=[row, row, act, act, act, row] + loss_spec,
        out_shape=[jax.ShapeDtypeStruct((t, D_MODEL), F32), jax.ShapeDtypeStruct((t, D_MODEL), BF16),
                   jax.ShapeDtypeStruct((nj, t, fj), BF16), jax.ShapeDtypeStruct((nj, t, fj), BF16),
                   jax.ShapeDtypeStruct((nj, t, fj), BF16),
                   jax.ShapeDtypeStruct((t, D_MODEL), F32 if target is None else BF16)] + loss_shape,
        scratch_shapes=[pltpu.VMEM((tm, D_MODEL), F32), pltpu.VMEM((nj, rows3, D_MODEL), BF16),
                        pltpu.SemaphoreType.DMA((nj,))],
        args=(h, wpre, w3, wpost) + (target,) * nblk, hook=hook)


def _ffn_bwd(dhout, h, f, p14, p24, wpre, w3, wpost, hook=None, df=None):
    t = h.shape[0]
    tm = _row_tile(t)
    nj = w3.shape[0]
    fj, wspecs = _ffn_weight_specs(w3)
    have_df = df is not None

    def body(dhout_ref, h_ref, f_ref, p1_ref, p2_ref, wpre_ref, wg_ref, wu_ref, wd_ref, wpost_ref, *rest):
        if have_df:
            dh_ref, dg_ref, du_ref, dwpre_ref, dn_ref = rest
            df_ref = f_ref
        else:
            dh_ref, df_ref, dg_ref, du_ref, dwpre_ref, dwpost_ref, dn_ref = rest
        i = pl.program_id(0)
        j = pl.program_id(1)

        @pl.when((i == 0) & (j == 0))
        def _():
            dwpre_ref[...] = jnp.zeros_like(dwpre_ref)
            if not have_df:
                dwpost_ref[...] = jnp.zeros_like(dwpost_ref)

        @pl.when(j == 0)
        def _():
            if not have_df:
                wpost = wpost_ref[...]
                _, fh, r = _rms(f_ref[...], wpost)
                dfv, dw = _rms_bwd(fh, r, wpost, 0.5 * dhout_ref[...])
                dwpost_ref[...] += dw
                df_ref[...] = dfv.astype(BF16)
            dn_ref[...] = jnp.zeros_like(dn_ref)

        parts = _row_parts(tm)
        das = [_dg(df_ref[rows, :], wd_ref[...], NT) for rows in parts]
        for rows, da in zip(parts, das):
            dg = (da * p1_ref[rows, :].astype(F32)).astype(BF16)
            du = (da * p2_ref[rows, :].astype(F32)).astype(BF16)
            dg_ref[rows, :] = dg
            du_ref[rows, :] = du
            dn_ref[rows, :] += _dot(dg, wg_ref[...]) + _dot(du, wu_ref[...])

        @pl.when(j == nj - 1)
        def _():
            wpre = wpre_ref[...]
            _, hh, r = _rms(h_ref[...], wpre)
            dx, dw = _rms_bwd(hh, r, wpre, dn_ref[...])
            dwpre_ref[...] += dw
            dh_ref[...] = dhout_ref[...] + dx

    row = pl.BlockSpec((tm, D_MODEL), lambda i, j: (i, 0))
    vec = pl.BlockSpec((1, D_MODEL), lambda i, j: (0, 0))
    act = pl.BlockSpec((None, tm, fj), lambda i, j: (j, i, 0))
    actshape = jax.ShapeDtypeStruct((nj, t, fj), BF16)
    rowf, rowb, vecf = (jax.ShapeDtypeStruct((t, D_MODEL), F32), jax.ShapeDtypeStruct((t, D_MODEL), BF16),
                        jax.ShapeDtypeStruct((1, D_MODEL), F32))
    return _pallas(
        body, name="ffn_bwd", grid=(t // tm, nj),
        in_specs=[row, row, row, act, act, vec] + wspecs + [vec],
        out_specs=[row, act, act, vec] if have_df else [row, row, act, act, vec, vec],
        out_shape=[rowf, actshape, actshape, vecf] if have_df else [rowf, rowb, actshape, actshape, vecf, vecf],
        scratch_shapes=[pltpu.VMEM((tm, D_MODEL), F32)],
        args=(dhout, h, df if have_df else f, p14, p24, wpre, w3, w3, w3, wpost), hook=hook)


def _ffn_wgrad(n, df, dg4, du4, a4, hook=None):
    t = n.shape[0]
    tm = _contract_tile(t)
    ni = t // tm
    nj, _, fj = dg4.shape

    def body(n_ref, df_ref, dg_ref, du_ref, a_ref, dw_ref, acc):
        i = pl.program_id(1)

        @pl.when(i == 0)
        def _():
            acc[...] = jnp.zeros_like(acc)

        nn = n_ref[...]
        acc[0:fj, :] += _dg(dg_ref[...], nn, TN)
        acc[fj:2 * fj, :] += _dg(du_ref[...], nn, TN)
        acc[2 * fj:3 * fj, :] += _dg(a_ref[...], df_ref[...], TN)

        @pl.when(i == ni - 1)
        def _():
            dw_ref[...] = acc[...].astype(BF16)

    row = pl.BlockSpec((tm, D_MODEL), lambda j, i: (i, 0))
    act = pl.BlockSpec((None, tm, fj), lambda j, i: (j, i, 0))
    return _pallas(
        body, name="ffn_wgrad", grid=(nj, ni),
        in_specs=[row, row, act, act, act],
        out_specs=[pl.BlockSpec((None, 3 * fj, D_MODEL), lambda j, i: (j, 0, 0))],
        out_shape=[jax.ShapeDtypeStruct((nj, 3 * fj, D_MODEL), BF16)],
        scratch_shapes=[pltpu.VMEM((3 * fj, D_MODEL), F32)],
        args=(n, df, dg4, du4, a4), hook=hook)


def _embed_norm(x, meta, w):
    t = x.shape[0] + BLK
    tm = _row_tile(t)
    nblk = tm // BLK

    def body(*refs):
        x_refs = refs[:nblk]
        meta_ref, w_ref, h_ref, n_ref = refs[nblk:]
        i = pl.program_id(0)
        first = jnp.concatenate([jnp.zeros((PAD, D_MODEL), F32), meta_ref[...]], axis=0)
        blocks = [jnp.where(i == 0, first, x_refs[0][...])] + [r[...] for r in x_refs[1:]]
        h = jnp.concatenate(blocks, axis=0) if nblk > 1 else blocks[0]
        h_ref[...] = h
        y, _, _ = _rms(h, w_ref[...])
        n_ref[...] = y.astype(BF16)

    x_specs = [pl.BlockSpec((BLK, D_MODEL), functools.partial(lambda i, k: (jnp.maximum(nblk * i + k - 1, 0), 0), k=k))
               for k in range(nblk)]
    row = pl.BlockSpec((tm, D_MODEL), lambda i: (i, 0))
    return pl.pallas_call(
        body, name="embed_norm", grid=(t // tm,),
        in_specs=x_specs + [_full((N_META, D_MODEL)), _full((1, D_MODEL))], out_specs=[row, row],
        out_shape=[jax.ShapeDtypeStruct((t, D_MODEL), F32), jax.ShapeDtypeStruct((t, D_MODEL), BF16)],
        compiler_params=_cparams(1),
    )(*([x] * nblk), meta, w)


FWD_RELATION = (None, 0, 1, 2)


def _ffn_fwd_gather(h, n, wbuf, wpost, qc_idx, late):
    t = h.shape[0]
    tm = _row_tile(t)
    ni = t // tm
    nj, rows3, _ = wbuf.shape
    fj = rows3 // 3
    assert nj == N_CHIPS and ni >= 4
    wbufs = [wbuf]
    nw = 1
    n_lin, n_lout = len(late.inputs), len(late.out_shape)
    wait_step = ni - 3

    def body(qc_ref, h_ref, n_ref, wpost_ref, *rest):
        wb_in = rest[:nw]
        lins = rest[nw:nw + n_lin]
        o0 = nw + n_lin
        hout_ref, p1_ref, p2_ref, a_ref, f_hbm = rest[o0:o0 + 5]
        wb = rest[o0 + 5:o0 + 5 + nw]
        louts = rest[o0 + 5 + nw:o0 + 5 + nw + n_lout]
        s0 = o0 + 5 + nw + n_lout
        wv, wsem, send, recv, fbuf, fr_sem, fw_sem = rest[s0:s0 + 7]
        lscr = rest[s0 + 7:]
        p = pl.program_id(0)
        i = pl.program_id(1)
        step = p * ni + i
        fslot = step % 3
        nslot = (step + 1) % 3

        def f_tile(tile):
            return f_hbm.at[pl.ds(pl.multiple_of(tile * tm, 8), tm)]

        @pl.when(step > 1)
        def _():
            pltpu.make_async_copy(fbuf.at[nslot], f_tile(i), fw_sem.at[nslot]).wait()

        nxt = step + 1

        @pl.when((nxt < N_CHIPS * ni) & (nxt >= ni))
        def _():
            pltpu.make_async_copy(f_tile(nxt % ni), fbuf.at[nslot], fr_sem.at[nslot]).start()

        @pl.when(p > 0)
        def _():
            pltpu.make_async_copy(f_tile(i), fbuf.at[fslot], fr_sem.at[fslot]).wait()
        x, y, c, chips = _place()
        q = 2 * x + y
        sibling = (x, y, 1 - c)
        mine, other = _half(rows3, c), _half(rows3, 1 - c)

        def load(chunk, slot, src):
            return [pltpu.make_async_copy(src[t].at[chunk], wv.at[slot, t], wsem.at[slot, t]) for t in range(nw)]

        @pl.when((p == 0) & (i == 0))
        def _():
            for j, (cx, cy) in enumerate(chips):
                for t in range(nw):
                    _remote(send.at[t, j], recv.at[t, j], wb_in[t].at[q, mine], wb[t].at[q, mine], (cx, cy, c)).start()
            for cp in load(q, 0, wb_in):
                cp.start()
            for cp in load(q, 0, wb_in):
                cp.wait()

        @pl.when((p == 1) & (i == 0))
        def _():
            late.start(lins, louts, lscr)

        for pp in range(1, N_CHIPS):
            j = FWD_RELATION[pp]
            cx, cy = chips[j]
            chunk = 2 * cx + cy

            @pl.when((p == pp - 1) & (i == wait_step))
            def _(j=j, cx=cx, cy=cy, chunk=chunk, pp=pp):
                for t in range(nw):
                    got = wb[t].at[chunk, mine]
                    _remote(send.at[t, j], recv.at[t, j], got, got, (cx, cy, c)).wait_recv()
                    _remote(send.at[t, 3 + j], recv.at[t, 3 + j], got, got, sibling).start()
                for t in range(nw):
                    rest_half = wb[t].at[chunk, other]
                    _remote(send.at[t, 3 + j], recv.at[t, 3 + j], rest_half, rest_half, sibling).wait_recv()
                for cp in load(chunk, pp % 2, wb):
                    cp.start()

            @pl.when((p == pp) & (i == 0))
            def _(chunk=chunk, pp=pp):
                for cp in load(chunk, pp % 2, wb):
                    cp.wait()

        @pl.when((p == N_CHIPS - 1) & (i == ni // 2))
        def _():
            late.mid(lins, louts, lscr)

        slot = p % 2
        nn = n_ref[...]
        g = _dg(nn, wv[slot, 0, 0:fj], NT)
        u = _dg(nn, wv[slot, 0, fj:2 * fj], NT)
        sg = _sigmoid(g)
        silu = g * sg
        p1_ref[...] = (u * (sg + silu * (1.0 - sg))).astype(BF16)
        p2_ref[...] = silu.astype(BF16)
        a = (silu * u).astype(BF16)
        a_ref[...] = a
        part = _dot(a, wv[slot, 0, 2 * fj:3 * fj])

        @pl.when(p == 0)
        def _():
            fbuf[fslot] = part

        @pl.when(p > 0)
        def _():
            fbuf[fslot] = fbuf[fslot] + part

        pltpu.make_async_copy(fbuf.at[fslot], f_tile(i), fw_sem.at[fslot]).start()

        @pl.when(p == N_CHIPS - 1)
        def _():
            yv, _, _ = _rms(fbuf[fslot], wpost_ref[...])
            hout_ref[...] = h_ref[...] + 0.5 * yv

        @pl.when((p == N_CHIPS - 1) & (i == ni - 1))
        def _():
            pslot = (step + 2) % 3
            pltpu.make_async_copy(fbuf.at[pslot], f_tile(i), fw_sem.at[pslot]).wait()
            pltpu.make_async_copy(fbuf.at[fslot], f_tile(i), fw_sem.at[fslot]).wait()
            for t in range(nw):
                for j, (cx, cy) in enumerate(chips):
                    sent = wb[t].at[2 * cx + cy, mine]
                    _remote(send.at[t, j], recv.at[t, j], sent, sent, (cx, cy, c)).wait_send()
                    _remote(send.at[t, 3 + j], recv.at[t, 3 + j], sent, sent, sibling).wait_send()
            late.finish(lins, louts, lscr)

    def last_pass_rows(p, i, qc_ref):
        return (jnp.where(p == N_CHIPS - 1, i, 0), 0)

    def chunk_rows(p, i, qc_ref):
        order = ((p & 1) << 1) | (p >> 1)
        return (jnp.bitwise_xor(qc_ref[0], order), i, 0)

    row = pl.BlockSpec((tm, D_MODEL), lambda p, i, qc_ref: (i, 0))
    last_row = pl.BlockSpec((tm, D_MODEL), last_pass_rows)
    act = pl.BlockSpec((None, tm, fj), chunk_rows)
    act_shape = jax.ShapeDtypeStruct((nj, t, fj), BF16)
    res = pl.pallas_call(
        body, name="ffn_fwd_gather",
        grid_spec=pltpu.PrefetchScalarGridSpec(
            num_scalar_prefetch=1, grid=(N_CHIPS, ni),
            in_specs=[last_row, row, pl.BlockSpec((1, D_MODEL), lambda p, i, qc_ref: (0, 0))]
            + [ANY] * (nw + n_lin),
            out_specs=[last_row, act, act, act, ANY] + [ANY] * (nw + n_lout),
            scratch_shapes=[pltpu.VMEM((2, nw, rows3, D_MODEL), BF16), pltpu.SemaphoreType.DMA((2, nw)),
                            pltpu.SemaphoreType.DMA((nw, 6)), pltpu.SemaphoreType.DMA((nw, 6)),
                            pltpu.VMEM((3, tm, D_MODEL), F32), pltpu.SemaphoreType.DMA((3,)),
                            pltpu.SemaphoreType.DMA((3,))] + list(late.scratch)),
        out_shape=[jax.ShapeDtypeStruct((t, D_MODEL), F32), act_shape, act_shape, act_shape,
                   jax.ShapeDtypeStruct((t, D_MODEL), F32)]
        + [jax.ShapeDtypeStruct(b.shape, b.dtype) for b in wbufs] + list(late.out_shape),
        input_output_aliases={**{4 + t: 5 + t for t in range(nw)},
                              **{4 + nw + a: 5 + nw + b for a, b in late.aliases}},
        compiler_params=_cparams(2),
    )(qc_idx, h, n, wpost, *wbufs, *late.inputs)
    return res[:5], res[5:5 + nw], res[5 + nw:]


PASS_RELATION = (2, 0, 1)


def _ffn_wgrad_reduce(n, df, dg4, du4, a4, qc_idx, hook):
    t = n.shape[0]
    tm = _contract_tile(t)
    ni = t // tm
    nj, _, fj = dg4.shape
    assert nj == N_CHIPS
    hrows = 3 * fj // 2
    n_hin, n_hout = len(hook.inputs), len(hook.out_shape)

    def body(qc_ref, n_ref, df_ref, dg_ref, du_ref, a_ref, *rest):
        hins = rest[:n_hin]
        own_ref, others_ref = rest[n_hin:n_hin + 2]
        houts = rest[n_hin + 2:n_hin + 2 + n_hout]
        s0 = n_hin + 2 + n_hout
        acc, stage, land, sumbuf, px_send, px_recv, cs_send, cs_recv, own_sem = rest[s0:s0 + 9]
        hscr = rest[s0 + 9:]
        k_pass = pl.program_id(0)
        i = pl.program_id(1)
        x, y, c, chips = _place()
        mine = pl.ds(pl.multiple_of(c * hrows, 8), hrows)
        other = pl.ds(pl.multiple_of((1 - c) * hrows, 8), hrows)

        def to_owner(k):
            j = PASS_RELATION[k]
            return _remote(cs_send.at[j], cs_recv.at[j], sumbuf.at[k % 2], others_ref.at[j], (*chips[j], c))

        @pl.when((k_pass == 0) & (i == 0))
        def _():
            hook.start(hins, houts, hscr)

        @pl.when(i == 0)
        def _():
            acc[...] = jnp.zeros_like(acc)

        nn = n_ref[...]
        acc[0:fj, :] += _dg(dg_ref[...], nn, TN)
        acc[fj:2 * fj, :] += _dg(du_ref[...], nn, TN)
        acc[2 * fj:3 * fj, :] += _dg(a_ref[...], df_ref[...], TN)

        for k in range(N_CHIPS):
            @pl.when((k_pass == k) & (i == ni - 1))
            def _(k=k):
                slot = k % 2
                stage[...] = acc[other, :].astype(BF16)
                swap = _remote(px_send.at[k], px_recv.at[k], stage, land.at[slot], (x, y, 1 - c))
                swap.start()
                swap.wait_recv()
                pair = acc[mine, :] + land[slot].astype(F32)
                if k >= 2:
                    to_owner(k - 2).wait_send()
                sumbuf[slot] = pair.astype(BF16)
                swap.wait_send()
                if k < N_CHIPS - 1:
                    to_owner(k).start()
                else:
                    keep = pltpu.make_async_copy(sumbuf.at[slot], own_ref, own_sem)
                    keep.start()
                    for j in range(N_CHIPS - 1):
                        _remote(cs_send.at[j], cs_recv.at[j], sumbuf.at[0], others_ref.at[j], (*chips[j], c)).wait_recv()
                    to_owner(k - 1).wait_send()
                    keep.wait()
                    hook.finish(hins, houts, hscr)

    def chunk(k_pass, i, qc_ref):
        return (jnp.bitwise_xor(qc_ref[0], N_CHIPS - 1 - k_pass), i, 0)

    row = pl.BlockSpec((tm, D_MODEL), lambda k_pass, i, qc_ref: (i, 0))
    act = pl.BlockSpec((None, tm, fj), chunk)
    res = pl.pallas_call(
        body, name="ffn_wgrad_reduce",
        grid_spec=pltpu.PrefetchScalarGridSpec(
            num_scalar_prefetch=1, grid=(N_CHIPS, ni),
            in_specs=[row, row, act, act, act] + [ANY] * n_hin,
            out_specs=[ANY, ANY] + [ANY] * n_hout,
            scratch_shapes=[pltpu.VMEM((3 * fj, D_MODEL), F32), pltpu.VMEM((hrows, D_MODEL), BF16),
                            pltpu.VMEM((2, hrows, D_MODEL), BF16), pltpu.VMEM((2, hrows, D_MODEL), BF16),
                            pltpu.SemaphoreType.DMA((N_CHIPS,)), pltpu.SemaphoreType.DMA((N_CHIPS,)),
                            pltpu.SemaphoreType.DMA((N_CHIPS - 1,)), pltpu.SemaphoreType.DMA((N_CHIPS - 1,)),
                            pltpu.SemaphoreType.DMA] + list(hook.scratch)),
        out_shape=[jax.ShapeDtypeStruct((hrows, D_MODEL), BF16),
                   jax.ShapeDtypeStruct((N_CHIPS - 1, hrows, D_MODEL), BF16)] + list(hook.out_shape),
        compiler_params=_cparams(2),
    )(qc_idx, n, df, dg4, du4, a4, *hook.inputs)
    return res[0], res[1], res[2:]


def _xty(x, y):
    t, k = x.shape
    n = y.shape[1]
    tm = _contract_tile(t)
    tn = n if n <= 1024 else (896 if n % 896 == 0 else 128)

    def body(x_ref, y_ref, o_ref):
        @pl.when(pl.program_id(1) == 0)
        def _():
            o_ref[...] = jnp.zeros_like(o_ref)

        o_ref[...] += _dg(x_ref[...], y_ref[...], TN)

    return pl.pallas_call(
        body, name="xty", grid=(n // tn, t // tm),
        in_specs=[pl.BlockSpec((tm, k), lambda j, i: (i, 0)), pl.BlockSpec((tm, tn), lambda j, i: (i, j))],
        out_specs=pl.BlockSpec((k, tn), lambda j, i: (0, j)),
        out_shape=jax.ShapeDtypeStruct((k, n), F32),
        compiler_params=_cparams(2),
    )(x, y)


def _rope_tables(t):
    pos = (jnp.arange(t, dtype=jnp.int32) - PAD).astype(F32)
    inv_freq = 1.0 / (ROPE_THETA ** (jnp.arange(0, SWA_HD, 2, dtype=F32) / SWA_HD))
    ang = pos[:, None] * inv_freq[None, :]
    cos = jnp.cos(ang)
    sin = jnp.sin(ang)
    return jnp.concatenate([cos, cos, cos, cos], axis=1), jnp.concatenate([-sin, sin, -sin, sin], axis=1)


def _rot_half(x, first_half):
    return jnp.where(first_half, pltpu.roll(x, 96, 1), pltpu.roll(x, 32, 1))


def _first_half_mask(rows):
    lane = lax.broadcasted_iota(jnp.int32, (rows, 128), 1)
    return (lane % 64) < 32


def _log_sigmoid(z):
    return jnp.minimum(z, 0.0) - jnp.log(1.0 + jnp.exp(-jnp.abs(z)))


def _mix_proj(h1, wmixpre, winp, wa2p, bap, cos, sin):
    t = h1.shape[0]
    tm = _row_tile(t)

    def body(h_ref, w_ref, win_ref, wa2_ref, ba_ref, cos_ref, sin_ref,
             n_ref, gq_ref, gk_ref, gv_ref, gg_ref, ga_ref, la_ref, sq_ref, sk_ref, sv_ref):
        y, _, _ = _rms(h_ref[...], w_ref[...])
        n = y.astype(BF16)
        n_ref[...] = n
        proj = _dot(n, win_ref[...])
        gq_ref[...] = proj[:, P_GQ:P_GK]
        gk_ref[...] = proj[:, P_GK:P_GV]
        gv_ref[...] = proj[:, P_GV:P_GG]
        gg_ref[...] = proj[:, P_GG:P_GA]
        ga = proj[:, P_GA:P_SQ]
        ga_ref[...] = ga
        z = _dot(ga.astype(BF16), wa2_ref[...]) + ba_ref[...]
        la_ref[...] = _log_sigmoid(z) * (1.0 / GLA_TAU)
        c = cos_ref[...]
        s = sin_ref[...]
        fh = _first_half_mask(tm)
        for k in range(4):
            x = proj[:, P_SQ + 128 * k:P_SQ + 128 * (k + 1)]
            sq_ref[:, 128 * k:128 * (k + 1)] = (x * c + _rot_half(x, fh) * s).astype(BF16)
        for k in range(2):
            x = proj[:, P_SK + 128 * k:P_SK + 128 * (k + 1)]
            sk_ref[:, 128 * k:128 * (k + 1)] = (x * c + _rot_half(x, fh) * s).astype(BF16)
        sv_ref[...] = proj[:, P_SV:P_END].astype(BF16)

    def row(w):
        return pl.BlockSpec((tm, w), lambda i: (i, 0))

    def rshape(w, dt):
        return jax.ShapeDtypeStruct((t, w), dt)

    return pl.pallas_call(
        body, name="mix_proj", grid=(t // tm,),
        in_specs=[row(D_MODEL), _full((1, D_MODEL)), _full((D_MODEL, P_END)), _full((128, GLA_KW)),
                  _full((1, GLA_KW)), row(128), row(128)],
        out_specs=[row(D_MODEL), row(256), row(256), row(512), row(512), row(128), row(256), row(512), row(256),
                   row(256)],
        out_shape=[rshape(D_MODEL, BF16), rshape(256, F32), rshape(256, F32), rshape(512, F32), rshape(512, F32),
                   rshape(128, F32), rshape(256, F32), rshape(512, BF16), rshape(256, BF16), rshape(256, BF16)],
        compiler_params=_cparams(1),
    )(h1, wmixpre, winp, wa2p, bap, cos, sin)


def _scan_rows(x, reverse=False):
    n = x.shape[0]
    row = lax.broadcasted_iota(jnp.int32, x.shape, 0)
    s = 1
    while s < n:
        if reverse:
            x = x + jnp.where(row < n - s, pltpu.roll(x, n - s, 0), 0.0)
        else:
            x = x + jnp.where(row >= s, pltpu.roll(x, s, 0), 0.0)
        s *= 2
    return x


def _gla_cumsum(la, tril_f):
    b = _scan_rows(la)
    row = lax.broadcasted_iota(jnp.int32, b.shape, 0)
    bm = jnp.sum(jnp.where(row == GLA_CHUNK // 2 - 1, b, 0.0), axis=0, keepdims=True)
    bl = jnp.sum(jnp.where(row == GLA_CHUNK - 1, b, 0.0), axis=0, keepdims=True)
    return b, bm, bl


def _gla_decays(la, tril_f):
    b, bm, bl = _gla_cumsum(la, tril_f)
    return jnp.exp(b - bm), jnp.exp(bm - b), jnp.exp(b), jnp.exp(bl - b), jnp.exp(bl)


def _gla_masks():
    c = GLA_CHUNK
    r = lax.broadcasted_iota(jnp.int32, (c, c), 0)
    col = lax.broadcasted_iota(jnp.int32, (c, c), 1)
    r4 = lax.broadcasted_iota(jnp.int32, (GLA_HEADS * c, c), 0) % c
    c4 = lax.broadcasted_iota(jnp.int32, (GLA_HEADS * c, c), 1)
    klane = lax.broadcasted_iota(jnp.int32, (c, GLA_KW), 1) // GLA_DK
    vlane = lax.broadcasted_iota(jnp.int32, (c, GLA_W), 1) // GLA_DV
    srow = lax.broadcasted_iota(jnp.int32, (GLA_W, GLA_KW), 0) // GLA_DV
    scol = lax.broadcasted_iota(jnp.int32, (GLA_W, GLA_KW), 1) // GLA_DK
    return dict(tril_f=(r >= col).astype(F32), triu_f=(r <= col).astype(F32), tril4=r4 >= c4,
                khead=[klane == h for h in range(GLA_HEADS)], vhead=[vlane == h for h in range(GLA_HEADS)],
                diag=srow == scol)


def _stack_heads(x, head_masks):
    return jnp.concatenate([jnp.where(m, x, 0.0) for m in head_masks], axis=0)


def _gla_fwd(gq, gk, gv, la):
    t = gq.shape[0]
    rg = _seq_tile(t)
    nb = t // rg
    ncb = rg // GLA_CHUNK
    c = GLA_CHUNK

    def body(q_ref, k_ref, v_ref, la_ref, o_ref, ss_ref, st_ref):
        @pl.when(pl.program_id(0) == 0)
        def _():
            st_ref[...] = jnp.zeros_like(st_ref)

        mk = _gla_masks()
        st = st_ref[...]
        for ch in range(ncb):
            rows = slice(ch * c, (ch + 1) * c)
            eq, ek, eb, ekl, ebl = _gla_decays(la_ref[rows, :], mk["tril_f"])
            qs = q_ref[rows, :] * (GLA_DK ** -0.5)
            k = k_ref[rows, :]
            v = v_ref[rows, :].astype(BF16)
            ss_ref[ch] = st
            q4 = _stack_heads(qs * eq, mk["khead"]).astype(BF16)
            a4 = jnp.where(mk["tril4"], _dg(q4, (k * ek).astype(BF16), NT), 0.0).astype(BF16)
            r4 = _dot(a4, v)
            intra = jnp.concatenate([r4[h * c:(h + 1) * c, GLA_DV * h:GLA_DV * (h + 1)] for h in range(GLA_HEADS)],
                                    axis=1)
            o_ref[rows, :] = intra + _dg((qs * eb).astype(BF16), st.astype(BF16), NT)
            st = st * ebl + jnp.where(mk["diag"], _dg(v, (k * ekl).astype(BF16), TN), 0.0)
        st_ref[...] = st

    def row(w):
        return pl.BlockSpec((rg, w), lambda i: (i, 0))

    return pl.pallas_call(
        body, name="gla_fwd", grid=(nb,),
        in_specs=[row(256), row(256), row(512), row(256)],
        out_specs=[row(512), pl.BlockSpec((ncb, GLA_W, GLA_KW), lambda i: (i, 0, 0))],
        out_shape=[jax.ShapeDtypeStruct((t, GLA_W), F32), jax.ShapeDtypeStruct((nb * ncb, GLA_W, GLA_KW), F32)],
        scratch_shapes=[pltpu.VMEM((GLA_W, GLA_KW), F32)],
        compiler_params=_cparams(1),
    )(gq, gk, gv, la)


def _gla_bwd(gq, gk, gv, la, ss, do):
    t = gq.shape[0]
    rg = _seq_tile(t)
    nb = t // rg
    ncb = rg // GLA_CHUNK
    c = GLA_CHUNK

    def body(q_ref, k_ref, v_ref, la_ref, ss_ref, do_ref, dq_ref, dk_ref, dv_ref, dla_ref, dst_ref):
        @pl.when(pl.program_id(0) == 0)
        def _():
            dst_ref[...] = jnp.zeros_like(dst_ref)

        mk = _gla_masks()
        last_row = lax.broadcasted_iota(jnp.int32, (c, GLA_KW), 0) == c - 1
        scale = GLA_DK ** -0.5
        dstn = dst_ref[...]
        for ch in reversed(range(ncb)):
            rows = slice(ch * c, (ch + 1) * c)
            eq, ek, eb, ekl, ebl = _gla_decays(la_ref[rows, :], mk["tril_f"])
            qs = q_ref[rows, :] * scale
            k = k_ref[rows, :]
            qt, kt, qh, kh = qs * eq, k * ek, qs * eb, k * ekl
            ktb, khb, qhb = kt.astype(BF16), kh.astype(BF16), qh.astype(BF16)
            v = v_ref[rows, :].astype(BF16)
            do_f = do_ref[rows, :]
            dob = do_f.astype(BF16)
            st = ss_ref[ch]
            stb = st.astype(BF16)
            dstb = dstn.astype(BF16)
            q4 = _stack_heads(qt, mk["khead"]).astype(BF16)
            do4 = _stack_heads(do_f, mk["vhead"]).astype(BF16)
            a4 = jnp.where(mk["tril4"], _dg(q4, ktb, NT), 0.0).astype(BF16)
            da4 = jnp.where(mk["tril4"], _dg(do4, v, NT), 0.0).astype(BF16)
            dv_ref[rows, :] = _dg(a4, do4, TN) + _dg(khb, dstb, NT)
            dq4 = _dot(da4, ktb)
            dqt = jnp.zeros((c, GLA_KW), F32)
            for h in range(GLA_HEADS):
                dqt = dqt + jnp.where(mk["khead"][h], dq4[h * c:(h + 1) * c], 0.0)
            dkt = _dg(da4, q4, TN)
            dqh = _dot(dob, stb)
            dkh = _dot(v, dstb)
            dbl = jnp.sum(dstn * st, axis=0, keepdims=True)
            dstn = dstn * ebl + jnp.where(mk["diag"], _dg(dob, qhb, TN), 0.0)
            dq_ref[rows, :] = scale * (dqt * eq + dqh * eb)
            dk_ref[rows, :] = dkt * ek + dkh * ekl
            dkk = dkh * kh
            db = dqt * qt - dkt * kt + dqh * qh - dkk
            db = db + jnp.where(last_row, jnp.sum(dkk, axis=0, keepdims=True) + ebl * dbl, 0.0)
            dla_ref[rows, :] = _scan_rows(db, reverse=True)
        dst_ref[...] = dstn

    def row(w):
        return pl.BlockSpec((rg, w), lambda i: (nb - 1 - i, 0))

    def rshape(w):
        return jax.ShapeDtypeStruct((t, w), F32)

    return pl.pallas_call(
        body, name="gla_bwd", grid=(nb,),
        in_specs=[row(256), row(256), row(512), row(256),
                  pl.BlockSpec((ncb, GLA_W, GLA_KW), lambda i: (nb - 1 - i, 0, 0)), row(512)],
        out_specs=[row(256), row(256), row(512), row(256)],
        out_shape=[rshape(256), rshape(256), rshape(512), rshape(256)],
        scratch_shapes=[pltpu.VMEM((GLA_W, GLA_KW), F32)],
        compiler_params=_cparams(1),
    )(gq, gk, gv, la, ss, do)


SWA_G = SWA_QH // SWA_KVH


def _swa_bias():
    n = jnp.arange(3, dtype=jnp.int32)[:, None, None]
    r = (jnp.arange(SWA_G * BLK, dtype=jnp.int32) % BLK)[None, :, None]
    c = jnp.arange(3 * BLK, dtype=jnp.int32)[None, None, :]
    seg = c // BLK
    cc = c % BLK
    qpos = n * BLK + r - PAD
    kpos = jnp.where(seg == 0, (n - 1) * BLK, jnp.where(seg == 1, n * BLK, 0)) + cc - PAD
    band = (seg < 2) & (kpos >= N_META) & (kpos <= qpos) & (qpos - kpos < WINDOW)
    meta = (seg == 2) & (kpos >= 0) & (kpos < N_META) & (kpos <= qpos)
    return jnp.where(band | meta, 0.0, NEG_INF).astype(F32)


def _swa_stack(ref, rows, kh, lo, dtype):
    parts = []
    for g in range(2):
        pair = ref[rows, 128 * (2 * kh + g):128 * (2 * kh + g + 1)]
        zero = jnp.zeros_like(pair)
        parts += [jnp.where(lo, pair, zero), jnp.where(lo, zero, pair)]
    return jnp.concatenate(parts, axis=0).astype(dtype)


def _swa_unstack(x4, lo):
    return [jnp.where(lo, x4[2 * g * BLK:(2 * g + 1) * BLK], x4[(2 * g + 1) * BLK:(2 * g + 2) * BLK])
            for g in range(2)]


def _swa_sink_col(sink_ref, kh):
    blk = lax.broadcasted_iota(jnp.int32, (SWA_G * BLK, 1), 0) // BLK
    col = jnp.full((SWA_G * BLK, 1), sink_ref[SWA_G * kh + SWA_G - 1], F32)
    for e in reversed(range(SWA_G - 1)):
        col = jnp.where(blk == e, sink_ref[SWA_G * kh + e], col)
    return col


def _swa_softmax(qk, bias, sink):
    s = qk * (SWA_HD ** -0.5) + bias
    m = jnp.maximum(jnp.max(s, axis=-1, keepdims=True), sink)
    p = jnp.exp(s - m)
    es = jnp.exp(sink - m)
    inv = 1.0 / (jnp.sum(p, axis=-1, keepdims=True) + es)
    return p * inv, es * inv


def _swa_keys(prev_ref, cur_ref, first_ref, b, ls):
    before = prev_ref[:, ls] if b == 0 else cur_ref[(b - 1) * BLK:b * BLK, ls]
    return jnp.concatenate([before, cur_ref[b * BLK:(b + 1) * BLK, ls], first_ref[:, ls]], axis=0)


def _swa_specs(rs, ns):
    bps = rs // BLK
    cur = lambda w: pl.BlockSpec((rs, w), lambda i: (jnp.minimum(i, ns - 1), 0))
    prev = lambda w: pl.BlockSpec((BLK, w), lambda i: (jnp.maximum(jnp.minimum(i, ns - 1) * bps - 1, 0), 0))
    first = lambda w: pl.BlockSpec((BLK, w), lambda i: (0, 0))
    return cur, prev, first


def _swa_fwd(sinks, sq, sk, sv):
    t = sq.shape[0]
    rs = _seq_tile(t)
    bps, ns = rs // BLK, t // rs

    def body(sink_ref, bias_ref, q_ref, kp_ref, kc_ref, km_ref, vp_ref, vc_ref, vm_ref, o_ref):
        i = pl.program_id(0)
        lo = lax.broadcasted_iota(jnp.int32, (BLK, 128), 1) < 64
        sink_cols = [_swa_sink_col(sink_ref, kh) for kh in range(SWA_KVH)]
        chains = [(b, kh) for b in range(bps) for kh in range(SWA_KVH)]
        scores = []
        for b, kh in chains:
            ls = slice(128 * kh, 128 * (kh + 1))
            q4 = _swa_stack(q_ref, slice(b * BLK, (b + 1) * BLK), kh, lo, BF16)
            scores.append(_dg(q4, _swa_keys(kp_ref, kc_ref, km_ref, b, ls), NT))
        probs = []
        for (b, kh), s in zip(chains, scores):
            p, _ = _swa_softmax(s, bias_ref[jnp.minimum(i * bps + b, 2)], sink_cols[kh])
            probs.append(p.astype(BF16))
        for (b, kh), p in zip(chains, probs):
            ls = slice(128 * kh, 128 * (kh + 1))
            rows = slice(b * BLK, (b + 1) * BLK)
            for g, pair in enumerate(_swa_unstack(_dot(p, _swa_keys(vp_ref, vc_ref, vm_ref, b, ls)), lo)):
                o_ref[rows, 128 * (2 * kh + g):128 * (2 * kh + g + 1)] = pair

    cur, prev, first = _swa_specs(rs, ns)
    bias = _swa_bias()
    return pl.pallas_call(
        body, name="swa_fwd", grid=(ns,),
        in_specs=[pl.BlockSpec(memory_space=pltpu.SMEM), _full(bias.shape), cur(512), prev(256), cur(256), first(256),
                  prev(256), cur(256), first(256)],
        out_specs=cur(512),
        out_shape=jax.ShapeDtypeStruct((t, SWA_W), F32),
        compiler_params=_cparams(1),
    )(sinks, bias, sq, sk, sk, sk, sv, sv, sv)


def _swa_bwd(sinks, sq, sk, sv, o, do, hook=None):
    t = sq.shape[0]
    rs = _seq_tile(t)
    bps, ns = rs // BLK, t // rs

    def body(sink_ref, bias_ref, q_ref, kp_ref, kc_ref, km_ref, vp_ref, vc_ref, vm_ref, o_ref, do_ref,
             dq_ref, dk_ref, dv_ref, dkm_ref, dvm_ref, dsink_ref, pk_ref, pv_ref):
        i = pl.program_id(0)

        @pl.when(i == 0)
        def _():
            pk_ref[...] = jnp.zeros_like(pk_ref)
            pv_ref[...] = jnp.zeros_like(pv_ref)
            dkm_ref[...] = jnp.zeros_like(dkm_ref)
            dvm_ref[...] = jnp.zeros_like(dvm_ref)
            dsink_ref[...] = jnp.zeros_like(dsink_ref)

        @pl.when(i == ns)
        def _():
            dk_ref[...] = pk_ref[...]
            dv_ref[...] = pv_ref[...]

        @pl.when(i < ns)
        def _():
            lo = lax.broadcasted_iota(jnp.int32, (BLK, 128), 1) < 64
            scale = SWA_HD ** -0.5
            sink_cols = [_swa_sink_col(sink_ref, kh) for kh in range(SWA_KVH)]
            parts_k = [[None] * SWA_KVH for _ in range(bps)]
            parts_v = [[None] * SWA_KVH for _ in range(bps)]
            dsinks = [jnp.zeros((1, 1), F32) for _ in range(SWA_QH)]
            chains = [(b, kh) for b in range(bps) for kh in range(SWA_KVH)]
            lanes = lambda kh: slice(128 * kh, 128 * (kh + 1))
            block = lambda b: slice(b * BLK, (b + 1) * BLK)
            q4s = [_swa_stack(q_ref, block(b), kh, lo, BF16) for b, kh in chains]
            scores = [_dg(q4, _swa_keys(kp_ref, kc_ref, km_ref, b, lanes(kh)), NT)
                      for (b, kh), q4 in zip(chains, q4s)]
            do4s = [_swa_stack(do_ref, block(b), kh, lo, F32) for b, kh in chains]
            do4bs = [d.astype(BF16) for d in do4s]
            dps = [_dg(d, _swa_keys(vp_ref, vc_ref, vm_ref, b, lanes(kh)), NT) for (b, kh), d in zip(chains, do4bs)]
            pbs, dss = [], []
            for n_chain, (b, kh) in enumerate(chains):
                p, psink = _swa_softmax(scores[n_chain], bias_ref[jnp.minimum(i * bps + b, 2)], sink_cols[kh])
                delta = jnp.sum(do4s[n_chain] * _swa_stack(o_ref, block(b), kh, lo, F32), axis=-1, keepdims=True)
                dss.append((p * (dps[n_chain] - delta) * scale).astype(BF16))
                pbs.append(p.astype(BF16))
                dsk = psink * delta
                for e in range(SWA_G):
                    h = SWA_G * kh + e
                    dsinks[h] = dsinks[h] - jnp.sum(dsk[e * BLK:(e + 1) * BLK], axis=0, keepdims=True)
            for n_chain, (b, kh) in enumerate(chains):
                kall = _swa_keys(kp_ref, kc_ref, km_ref, b, lanes(kh))
                for g, pair in enumerate(_swa_unstack(_dot(dss[n_chain], kall), lo)):
                    dq_ref[block(b), 128 * (2 * kh + g):128 * (2 * kh + g + 1)] = pair
                parts_k[b][kh] = _dg(dss[n_chain], q4s[n_chain], TN)
                parts_v[b][kh] = _dg(pbs[n_chain], do4bs[n_chain], TN)
            last = slice(rs - BLK, rs)
            for parts, out_ref, pend_ref, meta_ref in ((parts_k, dk_ref, pk_ref, dkm_ref),
                                                       (parts_v, dv_ref, pv_ref, dvm_ref)):
                for kh in range(SWA_KVH):
                    ls = slice(128 * kh, 128 * (kh + 1))
                    if bps > 1:
                        out_ref[0:rs - BLK, ls] = pend_ref[0:rs - BLK, ls]
                    out_ref[last, ls] = pend_ref[last, ls] + parts[0][kh][0:BLK]
                    meta = parts[0][kh][2 * BLK:3 * BLK]
                    for b in range(bps):
                        own = parts[b][kh][BLK:2 * BLK]
                        if b + 1 < bps:
                            own = own + parts[b + 1][kh][0:BLK]
                            meta = meta + parts[b + 1][kh][2 * BLK:3 * BLK]
                        pend_ref[b * BLK:(b + 1) * BLK, ls] = own
                    meta_ref[:, ls] += meta
            for h in range(SWA_QH):
                dsink_ref[h:h + 1, :] += jnp.broadcast_to(dsinks[h], (1, 128))

    cur, prev, first = _swa_specs(rs, ns)
    late = lambda w: pl.BlockSpec((rs, w), lambda i: (jnp.maximum(i - 1, 0), 0))
    bias = _swa_bias()
    return _pallas(
        body, name="swa_bwd", grid=(ns + 1,),
        in_specs=[pl.BlockSpec(memory_space=pltpu.SMEM), _full(bias.shape), cur(512), prev(256), cur(256), first(256),
                  prev(256), cur(256), first(256), cur(512), cur(512)],
        out_specs=[cur(512), late(256), late(256), first(256), first(256), _full((SWA_QH, 128))],
        out_shape=[jax.ShapeDtypeStruct((t, SWA_W), F32), jax.ShapeDtypeStruct((t, 256), F32),
                   jax.ShapeDtypeStruct((t, 256), F32), jax.ShapeDtypeStruct((BLK, 256), F32),
                   jax.ShapeDtypeStruct((BLK, 256), F32), jax.ShapeDtypeStruct((SWA_QH, 128), F32)],
        scratch_shapes=[pltpu.VMEM((rs, 256), F32), pltpu.VMEM((rs, 256), F32)],
        args=(sinks, bias, sq, sk, sk, sk, sv, sv, sv, o, do), hook=hook)


def _mix_out(h1, ogla, gg, oswa, wgn, wsn, wout, wpost):
    t = h1.shape[0]
    tm = _row_tile(t)

    def body(h_ref, og_ref, gg_ref, os_ref, wgn_ref, wsn_ref, wout_ref, wpost_ref, h2_ref, cat_ref, m_ref):
        parts = []
        for h in range(GLA_HEADS):
            ls = slice(GLA_DV * h, GLA_DV * (h + 1))
            y, _, _ = _rms(og_ref[:, ls], wgn_ref[...])
            g = gg_ref[:, ls]
            parts.append(y * (g * _sigmoid(g)))
        ys, _, _ = _rms(os_ref[...], wsn_ref[...])
        cat = jnp.concatenate(parts + [ys], axis=1).astype(BF16)
        cat_ref[...] = cat
        m = _dot(cat, wout_ref[...])
        m_ref[...] = m
        y, _, _ = _rms(m, wpost_ref[...])
        h2_ref[...] = h_ref[...] + y

    def row(w):
        return pl.BlockSpec((tm, w), lambda i: (i, 0))

    return pl.pallas_call(
        body, name="mix_out", grid=(t // tm,),
        in_specs=[row(D_MODEL), row(512), row(512), row(512), _full((1, GLA_DV)), _full((1, SWA_W)),
                  _full((D_MODEL, D_MODEL)), _full((1, D_MODEL))],
        out_specs=[row(D_MODEL), row(D_MODEL), row(D_MODEL)],
        out_shape=[jax.ShapeDtypeStruct((t, D_MODEL), F32), jax.ShapeDtypeStruct((t, D_MODEL), BF16),
                   jax.ShapeDtypeStruct((t, D_MODEL), F32)],
        compiler_params=_cparams(1),
    )(h1, ogla, gg, oswa, wgn, wsn, wout, wpost)


def _mix_out_bwd(dh2, m, ogla, gg, oswa, wgn, wsn, wout, wpost, hook=None):
    t = dh2.shape[0]
    tm = _row_tile(t)

    def body(dh_ref, m_ref, og_ref, gg_ref, os_ref, wgn_ref, wsn_ref, wout_ref, wpost_ref,
             dog_ref, dgg_ref, dos_ref, dm_ref, dwpost_ref, dwgn_ref, dwsn_ref):
        @pl.when(pl.program_id(0) == 0)
        def _():
            dwpost_ref[...] = jnp.zeros_like(dwpost_ref)
            dwgn_ref[...] = jnp.zeros_like(dwgn_ref)
            dwsn_ref[...] = jnp.zeros_like(dwsn_ref)

        wpost = wpost_ref[...]
        _, mh, r = _rms(m_ref[...], wpost)
        dm, dw = _rms_bwd(mh, r, wpost, dh_ref[...])
        dwpost_ref[...] += dw
        dmb = dm.astype(BF16)
        dm_ref[...] = dmb
        dcat = _dg(dmb, wout_ref[...], NT)
        wgn = wgn_ref[...]
        for h in range(GLA_HEADS):
            ls = slice(GLA_DV * h, GLA_DV * (h + 1))
            dog = dcat[:, ls]
            g = gg_ref[:, ls]
            sg = _sigmoid(g)
            y, xh, r = _rms(og_ref[:, ls], wgn)
            dgg_ref[:, ls] = dog * y * (sg * (1.0 + g * (1.0 - sg)))
            dx, dw = _rms_bwd(xh, r, wgn, dog * (g * sg))
            dog_ref[:, ls] = dx
            dwgn_ref[...] += dw
        wsn = wsn_ref[...]
        _, xh, r = _rms(os_ref[...], wsn)
        dx, dw = _rms_bwd(xh, r, wsn, dcat[:, GLA_W:])
        dos_ref[...] = dx
        dwsn_ref[...] += dw

    def row(w):
        return pl.BlockSpec((tm, w), lambda i: (i, 0))

    def rshape(w, dt=F32):
        return jax.ShapeDtypeStruct((t, w), dt)

    return _pallas(
        body, name="mix_out_bwd", grid=(t // tm,),
        in_specs=[row(D_MODEL), row(D_MODEL), row(512), row(512), row(512), _full((1, GLA_DV)), _full((1, SWA_W)),
                  _full((D_MODEL, D_MODEL)), _full((1, D_MODEL))],
        out_specs=[row(512), row(512), row(512), row(D_MODEL), _full((1, D_MODEL)), _full((1, GLA_DV)),
                   _full((1, SWA_W))],
        out_shape=[rshape(512), rshape(512), rshape(512), rshape(D_MODEL, BF16),
                   jax.ShapeDtypeStruct((1, D_MODEL), F32), jax.ShapeDtypeStruct((1, GLA_DV), F32),
                   jax.ShapeDtypeStruct((1, SWA_W), F32)],
        args=(dh2, m, ogla, gg, oswa, wgn, wsn, wout, wpost), hook=hook)


def _mix_in_bwd(dh2, h1, wmixpre, winp, wa2p, bap, cos, sin, ga, dgq, dgk, dgv, dgg, dla, dsq, dsk, dsv, dkm, dvm):
    t = h1.shape[0]
    tm = _row_tile(t)

    def body(dh2_ref, h_ref, w_ref, win_ref, wa2_ref, ba_ref, cos_ref, sin_ref, ga_ref, dgq_ref, dgk_ref, dgv_ref,
             dgg_ref, dla_ref, dsq_ref, dsk_ref, dsv_ref, dkm_ref, dvm_ref,
             dh1_ref, dproj_ref, dw_ref, dwa2_ref, dba_ref):
        i = pl.program_id(0)

        @pl.when(i == 0)
        def _():
            dw_ref[...] = jnp.zeros_like(dw_ref)
            dwa2_ref[...] = jnp.zeros_like(dwa2_ref)
            dba_ref[...] = jnp.zeros_like(dba_ref)

        first = (i == 0).astype(F32)
        c = cos_ref[...]
        s = -sin_ref[...]
        fh = _first_half_mask(tm)
        dproj_ref[:, P_GQ:P_GK] = dgq_ref[...].astype(BF16)
        dproj_ref[:, P_GK:P_GV] = dgk_ref[...].astype(BF16)
        dproj_ref[:, P_GV:P_GG] = dgv_ref[...].astype(BF16)
        dproj_ref[:, P_GG:P_GA] = dgg_ref[...].astype(BF16)
        gab = ga_ref[...].astype(BF16)
        z = _dot(gab, wa2_ref[...]) + ba_ref[...]
        row_id = i * tm + lax.broadcasted_iota(jnp.int32, (tm, 1), 0)
        dz = jnp.where(row_id >= PAD, dla_ref[...] * (1.0 / GLA_TAU) * (1.0 - _sigmoid(z)), 0.0)
        dzb = dz.astype(BF16)
        dba_ref[...] += jnp.sum(dz, axis=0, keepdims=True)
        dwa2_ref[...] += _dg(gab, dzb, TN)
        dproj_ref[:, P_GA:P_SQ] = _dg(dzb, wa2_ref[...], NT).astype(BF16)
        for k in range(4):
            dy = dsq_ref[:, 128 * k:128 * (k + 1)]
            dproj_ref[:, P_SQ + 128 * k:P_SQ + 128 * (k + 1)] = (dy * c + _rot_half(dy, fh) * s).astype(BF16)
        for k in range(2):
            ls = slice(128 * k, 128 * (k + 1))
            dy = dsk_ref[:, ls]
            dy = jnp.concatenate([dy[:BLK] + first * dkm_ref[:, ls], dy[BLK:]], axis=0) if tm > BLK else (
                dy + first * dkm_ref[:, ls])
            dproj_ref[:, P_SK + 128 * k:P_SK + 128 * (k + 1)] = (dy * c + _rot_half(dy, fh) * s).astype(BF16)
            dv = dsv_ref[:, ls]
            dv = jnp.concatenate([dv[:BLK] + first * dvm_ref[:, ls], dv[BLK:]], axis=0) if tm > BLK else (
                dv + first * dvm_ref[:, ls])
            dproj_ref[:, P_SV + 128 * k:P_SV + 128 * (k + 1)] = dv.astype(BF16)
        dn = _dg(dproj_ref[...], win_ref[...], NT)
        w = w_ref[...]
        _, hh, r = _rms(h_ref[...], w)
        dx, dw = _rms_bwd(hh, r, w, dn)
        dw_ref[...] += dw
        dh1_ref[...] = dh2_ref[...] + dx

    def row(w):
        return pl.BlockSpec((tm, w), lambda i: (i, 0))

    return pl.pallas_call(
        body, name="mix_in_bwd", grid=(t // tm,),
        in_specs=[row(D_MODEL), row(D_MODEL), _full((1, D_MODEL)), _full((D_MODEL, P_END)), _full((128, GLA_KW)),
                  _full((1, GLA_KW)), row(128), row(128), row(128), row(256), row(256), row(512), row(512), row(256),
                  row(512), row(256), row(256), _full((BLK, 256)), _full((BLK, 256))],
        out_specs=[row(D_MODEL), row(P_END), _full((1, D_MODEL)), _full((128, GLA_KW)), _full((1, GLA_KW))],
        out_shape=[jax.ShapeDtypeStruct((t, D_MODEL), F32), jax.ShapeDtypeStruct((t, P_END), BF16),
                   jax.ShapeDtypeStruct((1, D_MODEL), F32), jax.ShapeDtypeStruct((128, GLA_KW), F32),
                   jax.ShapeDtypeStruct((1, GLA_KW), F32)],
        compiler_params=_cparams(1),
    )(dh2, h1, wmixpre, winp, wa2p, bap, cos, sin, ga, dgq, dgk, dgv, dgg, dla, dsq, dsk, dsv, dkm, dvm)


def _adamw_update(w, g, m, v):
    m = ADAM_B1 * m + (1.0 - ADAM_B1) * g
    v = ADAM_B2 * v + (1.0 - ADAM_B2) * (g * g)
    m_hat = m / (1.0 - ADAM_B1 ** ADAM_STEP)
    v_hat = v / (1.0 - ADAM_B2 ** ADAM_STEP)
    return -ADAM_LR * (m_hat / (jnp.sqrt(v_hat) + ADAM_EPS) + ADAM_WD * w), m, v


def _adamw_halves(w, g_mine, g_other, m, v, c_idx, row0=0):
    r, c = w.shape
    h = g_mine.shape[0]
    tr = _div_tile(math.gcd(r, h))
    nth = h // tr
    t0 = row0 // tr
    assert t0 * tr == row0

    def body(c_ref, w_ref, gm_ref, go_ref, m_ref, v_ref, g_ref, d_ref, nm_ref, nv_ref):
        hh = (t0 + pl.program_id(0)) // nth
        g = jnp.where(hh == c_ref[0], gm_ref[...], go_ref[...])
        g_ref[...] = g
        d_ref[...], nm_ref[...], nv_ref[...] = _adamw_update(w_ref[...], g, m_ref[...], v_ref[...])

    spec = pl.BlockSpec((tr, c), lambda i, c_ref: (i, 0))

    def gspec(is_mine):
        def index(i, c_ref):
            used = ((t0 + i) // nth == c_ref[0]) == is_mine
            return (jnp.where(used, (t0 + i) % nth, 0), 0)
        return pl.BlockSpec((tr, c), index)

    shape = jax.ShapeDtypeStruct((r, c), F32)
    return pl.pallas_call(
        body, name="adamw_halves",
        grid_spec=pltpu.PrefetchScalarGridSpec(
            num_scalar_prefetch=1, grid=(r // tr,), in_specs=[spec, gspec(True), gspec(False), spec, spec],
            out_specs=[spec] * 4),
        out_shape=[shape] * 4, compiler_params=_cparams(1),
    )(c_idx, w, g_mine, g_other, m, v)


def _place():
    x, y, c = lax.axis_index("x"), lax.axis_index("y"), lax.axis_index("c")
    chips = [(1 - x, y), (x, 1 - y), (1 - x, 1 - y)]
    return x, y, c, chips


def _remote(send_sem, recv_sem, src, dst, to):
    return pltpu.make_async_remote_copy(src_ref=src, dst_ref=dst, send_sem=send_sem, recv_sem=recv_sem,
                                        device_id=to, device_id_type=MESH)


def _half(ref_rows, c):
    h = ref_rows // 2
    return pl.ds(pl.multiple_of(c * h, 8), h)


def _own_slot(shard, q):
    return lax.dynamic_update_slice(jnp.zeros((N_CHIPS,) + shard.shape, shard.dtype), shard[None], (q, 0, 0))


class _GatherChips:
    has_mid = True

    def __init__(self, bufs):
        n = len(bufs)
        self.inputs = list(bufs)
        self.out_shape = [jax.ShapeDtypeStruct(b.shape, b.dtype) for b in bufs]
        self.aliases = [(t, t) for t in range(n)]
        self.scratch = [pltpu.SemaphoreType.DMA((n, 6)), pltpu.SemaphoreType.DMA((n, 6))]

    def start(self, ins, outs, scr):
        send, recv = scr
        x, y, c, chips = _place()
        q = 2 * x + y
        for t, (i_ref, o_ref) in enumerate(zip(ins, outs)):
            rows = _half(i_ref.shape[1], c)
            for j, (cx, cy) in enumerate(chips):
                _remote(send.at[t, j], recv.at[t, j], i_ref.at[q, rows], o_ref.at[q, rows], (cx, cy, c)).start()

    def mid(self, ins, outs, scr):
        send, recv = scr
        x, y, c, chips = _place()
        for t, o_ref in enumerate(outs):
            rows = _half(o_ref.shape[1], c)
            for j, (cx, cy) in enumerate(chips):
                slot = o_ref.at[2 * cx + cy, rows]
                _remote(send.at[t, j], recv.at[t, j], slot, slot, (cx, cy, c)).wait_recv()
                _remote(send.at[t, 3 + j], recv.at[t, 3 + j], slot, slot, (x, y, 1 - c)).start()

    def finish(self, ins, outs, scr):
        send, recv = scr
        x, y, c, chips = _place()
        for t, o_ref in enumerate(outs):
            mine, other = _half(o_ref.shape[1], c), _half(o_ref.shape[1], 1 - c)
            for j, (cx, cy) in enumerate(chips):
                slot = o_ref.at[2 * cx + cy, other]
                _remote(send.at[t, 3 + j], recv.at[t, 3 + j], slot, slot, (x, y, 1 - c)).wait_recv()
            for j, (cx, cy) in enumerate(chips):
                sent = o_ref.at[2 * cx + cy, mine]
                _remote(send.at[t, j], recv.at[t, j], sent, sent, (cx, cy, c)).wait_send()
                _remote(send.at[t, 3 + j], recv.at[t, 3 + j], sent, sent, (x, y, 1 - c)).wait_send()


class _PairExchange:
    has_mid = False
    aliases = ()

    def __init__(self, arrs):
        n = len(arrs)
        self.inputs = list(arrs)
        self.out_shape = [jax.ShapeDtypeStruct((a.shape[0], a.shape[1] // 2, a.shape[2]), a.dtype) for a in arrs]
        self.scratch = [pltpu.SemaphoreType.DMA((n,)), pltpu.SemaphoreType.DMA((n,))]

    def _copies(self, ins, outs, scr):
        send, recv = scr
        x, y, c, _ = _place()
        return [_remote(send.at[t], recv.at[t], i_ref.at[:, _half(i_ref.shape[1], 1 - c)], o_ref, (x, y, 1 - c))
                for t, (i_ref, o_ref) in enumerate(zip(ins, outs))]

    def start(self, ins, outs, scr):
        for cp in self._copies(ins, outs, scr):
            cp.start()

    def finish(self, ins, outs, scr):
        for cp in self._copies(ins, outs, scr):
            cp.wait()


class _ChipScatter:
    has_mid = False
    aliases = ()

    def __init__(self, arrs):
        n = len(arrs)
        self.inputs = list(arrs)
        self.out_shape = [jax.ShapeDtypeStruct((3,) + a.shape[1:], a.dtype) for a in arrs]
        self.scratch = [pltpu.SemaphoreType.DMA((n, 3)), pltpu.SemaphoreType.DMA((n, 3))]

    def _copies(self, ins, outs, scr):
        send, recv = scr
        x, y, c, chips = _place()
        return [_remote(send.at[t, j], recv.at[t, j], i_ref.at[2 * cx + cy], o_ref.at[j], (cx, cy, c))
                for t, (i_ref, o_ref) in enumerate(zip(ins, outs)) for j, (cx, cy) in enumerate(chips)]

    def start(self, ins, outs, scr):
        for cp in self._copies(ins, outs, scr):
            cp.start()

    def finish(self, ins, outs, scr):
        for cp in self._copies(ins, outs, scr):
            cp.wait()


class _PairShare:
    has_mid = False
    aliases = ()

    def __init__(self, arrs):
        n = len(arrs)
        self.inputs = list(arrs)
        self.out_shape = [jax.ShapeDtypeStruct(a.shape, a.dtype) for a in arrs]
        self.scratch = [pltpu.SemaphoreType.DMA((n,)), pltpu.SemaphoreType.DMA((n,))]

    def _copies(self, ins, outs, scr):
        send, recv = scr
        x, y, c, _ = _place()
        return [_remote(send.at[t], recv.at[t], i_ref, o_ref, (x, y, 1 - c))
                for t, (i_ref, o_ref) in enumerate(zip(ins, outs))]

    def start(self, ins, outs, scr):
        for cp in self._copies(ins, outs, scr):
            cp.start()

    def finish(self, ins, outs, scr):
        for cp in self._copies(ins, outs, scr):
            cp.wait()


def _comm_call(hook, name):
    n_in, n_out = len(hook.inputs), len(hook.out_shape)

    def body(*refs):
        ins, outs, scr = refs[:n_in], refs[n_in:n_in + n_out], refs[n_in + n_out:]
        hook.start(ins, outs, scr)
        if hook.has_mid:
            hook.mid(ins, outs, scr)
        hook.finish(ins, outs, scr)

    return pl.pallas_call(body, name=name, in_specs=[ANY] * n_in, out_specs=[ANY] * n_out,
                          out_shape=list(hook.out_shape), scratch_shapes=list(hook.scratch),
                          input_output_aliases=dict(hook.aliases))(*hook.inputs)


def _all_gather_devices(vecs):
    n = len(vecs)

    def body(*refs):
        x_refs, out_refs = refs[:n], refs[n:2 * n]
        send_sems, recv_sems, local_sems = refs[2 * n:]
        x, y, c, chips = _place()
        me, sibling = (x, y, c), (x, y, 1 - c)
        waits = []
        for t, (x_ref, out_ref) in enumerate(zip(x_refs, out_refs)):
            def slot(px, py, pc, out_ref=out_ref):
                return out_ref.at[4 * px + 2 * py + pc]

            def copy(k, block, to, src=None, t=t, slot=slot):
                return pltpu.make_async_remote_copy(
                    src_ref=slot(*block) if src is None else src, dst_ref=slot(*block), send_sem=send_sems.at[t, k],
                    recv_sem=recv_sems.at[t, k], device_id=to, device_id_type=MESH)

            mine = pltpu.make_async_copy(x_ref, slot(*me), local_sems.at[t])
            mine.start()
            first = [copy(0, me, sibling, src=x_ref)]
            first += [copy(1 + j, me, (*chip, c), src=x_ref) for j, chip in enumerate(chips)]
            for cp in first:
                cp.start()
            waits.append((copy, mine, first))
        for copy, mine, first in waits:
            passed = [copy(4 + j, (*chip, c), sibling) for j, chip in enumerate(chips)]
            for j, chip in enumerate(chips):
                copy(1 + j, (*chip, c), me).wait_recv()
                passed[j].start()
            copy(0, sibling, me).wait_recv()
            for j, chip in enumerate(chips):
                copy(4 + j, (*chip, 1 - c), me).wait_recv()
            for cp in first + passed:
                cp.wait_send()
            mine.wait()

    vmem = pl.BlockSpec(memory_space=pltpu.VMEM)
    return pl.pallas_call(
        body, name="all_gather_devices", in_specs=[vmem] * n, out_specs=[vmem] * n,
        out_shape=[jax.ShapeDtypeStruct((N_DEV,) + v.shape, v.dtype) for v in vecs],
        scratch_shapes=[pltpu.SemaphoreType.DMA((n, 7)), pltpu.SemaphoreType.DMA((n, 7)),
                        pltpu.SemaphoreType.DMA((n,))],
    )(*vecs)


def _pair_sum(g, other, c_idx):
    nq, r, w = g.shape
    h = r // 2
    tr = _div_tile(h)
    nt = h // tr

    def body(c_ref, g_ref, o_ref, s_ref):
        s_ref[...] = (g_ref[...].astype(F32) + o_ref[...].astype(F32)).astype(s_ref.dtype)

    return pl.pallas_call(
        body, name="pair_sum",
        grid_spec=pltpu.PrefetchScalarGridSpec(
            num_scalar_prefetch=1, grid=(nq, nt),
            in_specs=[pl.BlockSpec((None, tr, w), lambda k, i, c_ref: (k, c_ref[0] * nt + i, 0)),
                      pl.BlockSpec((None, tr, w), lambda k, i, c_ref: (k, i, 0))],
            out_specs=pl.BlockSpec((None, tr, w), lambda k, i, c_ref: (k, i, 0))),
        out_shape=jax.ShapeDtypeStruct((nq, h, w), g.dtype),
        compiler_params=_cparams(2),
    )(c_idx, g, other)


def _chip_sum(s, others, q_idx):
    _, h, w = s.shape
    tr = _div_tile(h)

    def body(q_ref, s_ref, o_ref, out_ref):
        out_ref[...] = ((s_ref[...].astype(F32) + o_ref[0].astype(F32)) + o_ref[1].astype(F32)) + o_ref[2].astype(F32)

    return pl.pallas_call(
        body, name="chip_sum",
        grid_spec=pltpu.PrefetchScalarGridSpec(
            num_scalar_prefetch=1, grid=(h // tr,),
            in_specs=[pl.BlockSpec((None, tr, w), lambda i, q_ref: (q_ref[0], i, 0)),
                      pl.BlockSpec((3, tr, w), lambda i, q_ref: (0, i, 0))],
            out_specs=pl.BlockSpec((tr, w), lambda i, q_ref: (i, 0))),
        out_shape=jax.ShapeDtypeStruct((h, w), F32),
        compiler_params=_cparams(1),
    )(q_idx, s, others)


def _small_update(q_idx, parts, ws, ms, vs, col_block):
    n = len(parts)
    has_w = [w is not None for w in ws]

    def body(q_ref, *refs):
        pos = 0
        ins = []
        for t in range(n):
            k = 4 if has_w[t] else 1
            ins.append(refs[pos:pos + k])
            pos += k
        outs = refs[pos:]
        opos = 0
        for t in range(n):
            p_ref = ins[t][0]
            g = p_ref[0]
            for s in range(1, p_ref.shape[0]):
                g = g + p_ref[s]
            if has_w[t]:
                _, w_ref, m_ref, v_ref = ins[t]
                g_ref, d_ref, nm_ref, nv_ref = outs[opos:opos + 4]
                opos += 4
                g_ref[...] = g
                d_ref[...], nm_ref[...], nv_ref[...] = _adamw_update(w_ref[...], g, m_ref[...], v_ref[...])
            else:
                outs[opos][...] = g
                opos += 1

    def whole(shape):
        nd = len(shape)
        return pl.BlockSpec(shape, lambda i, q_ref: (0,) * nd)

    in_specs, out_specs, out_shape, args = [], [], [], []
    for t in range(n):
        k, r, wf = parts[t].shape
        if col_block[t]:
            w = wf // N_CHIPS
            in_specs.append(pl.BlockSpec((k, r, w), lambda i, q_ref: (0, 0, q_ref[0])))
        else:
            w = wf
            in_specs.append(whole((k, r, wf)))
        args.append(parts[t])
        if has_w[t]:
            assert ws[t].shape == (r, w), (ws[t].shape, r, w)
            in_specs += [whole((r, w))] * 3
            args += [ws[t], ms[t], vs[t]]
            out_specs += [whole((r, w))] * 4
            out_shape += [jax.ShapeDtypeStruct((r, w), F32)] * 4
        else:
            out_specs.append(whole((r, w)))
            out_shape.append(jax.ShapeDtypeStruct((r, w), F32))
    return pl.pallas_call(
        body, name="small_update",
        grid_spec=pltpu.PrefetchScalarGridSpec(num_scalar_prefetch=1, grid=(1,), in_specs=in_specs,
                                               out_specs=out_specs),
        out_shape=out_shape, compiler_params=_cparams(1),
    )(q_idx, *args)


_PACK_SEGMENTS = ((0, 1552), None, (1552, 2064), (2064, 2128), (2064, 2128), (2128, 2192), (2128, 2192),
                  (2192, 2256), (2192, 2256), (2256, 2320), (2256, 2320))
_UNPACK_SEGMENTS = (((0, 1552), (0,)), ((1552, 2064), (P_SQ,)), ((2064, 2128), (P_SK, P_SK + 64)),
                    ((2128, 2192), (P_SK + 128, P_SK + 192)), ((2192, 2256), (P_SV, P_SV + 64)),
                    ((2256, 2320), (P_SV + 128, P_SV + 192)))


def _pack_win(w4):
    per = w4.shape[2]
    pieces = []
    for seg in _PACK_SEGMENTS:
        if seg is None:
            pieces.append(jnp.zeros((w4.shape[1], 128 - GLA_RANK), w4.dtype))
            continue
        for q in range(w4.shape[0]):
            lo, hi = max(seg[0], q * per), min(seg[1], (q + 1) * per)
            if lo < hi:
                pieces.append(w4[q][:, lo - q * per:hi - q * per])
    return jnp.concatenate(pieces, axis=1)


def _unpack_dwin(d):
    per = D_IN // N_CHIPS
    chips = []
    for q in range(N_CHIPS):
        pieces = []
        for (a, b), starts in _UNPACK_SEGMENTS:
            lo, hi = max(a, q * per), min(b, (q + 1) * per)
            if lo < hi:
                copies = [d[:, s + lo - a:s + hi - a] for s in starts]
                pieces.append(copies[0] if len(copies) == 1 else copies[0] + copies[1])
        chips.append(jnp.concatenate(pieces, axis=1))
    return jnp.stack(chips).astype(BF16)


def _local_step(x, target, meta, p):
    s = x.shape[0]
    t = s + BLK
    h0 = jnp.concatenate([jnp.zeros((PAD, D_MODEL), F32), meta, x], axis=0)
    cos, sin = _rope_tables(t)

    h1, n1, g1, u1, a1, f1 = _ffn_fwd(h0, p["ffn1_pre_norm"], p["ffn1_w"], p["ffn1_post_norm"])
    n2, gq, gk, gv, gg, ga, la, sq, sk, sv = _mix_proj(h1, p["mix_pre_norm"], p["w_in"], p["gla_w_a2"], p["gla_b_a"],
                                                       cos, sin)
    ogla, ss = _gla_fwd(gq, gk, gv, la)
    oswa = _swa_fwd(p["swa_sinks"], sq, sk, sv)
    h2, cat, m = _mix_out(h1, ogla, gg, oswa, p["gla_out_norm"], p["swa_out_norm"], p["w_out"], p["mix_post_norm"])
    grads = {}
    dy, n3, g3, u3, a3, df3, grads["ffn2_post_norm"], sse = _ffn_fwd(
        h2, p["ffn2_pre_norm"], p["ffn2_w"], p["ffn2_post_norm"], target=target)

    dh2, dg3, du3, grads["ffn2_pre_norm"] = _ffn_bwd(
        dy, h2, None, g3, u3, p["ffn2_pre_norm"], p["ffn2_w"], p["ffn2_post_norm"], df=df3)
    (gud,) = _ffn_wgrad(n3, df3, dg3, du3, a3)
    grads["ffn2_w_gate"], grads["ffn2_w_up"], grads["ffn2_w_down"] = gud[:, :FJ], gud[:, FJ:2 * FJ], gud[:, 2 * FJ:]

    dogla, dgg, doswa, dm, grads["mix_post_norm"], grads["gla_out_norm"], grads["swa_out_norm"] = _mix_out_bwd(
        dh2, m, ogla, gg, oswa, p["gla_out_norm"], p["swa_out_norm"], p["w_out"], p["mix_post_norm"])
    grads["w_out"] = _xty(cat, dm)
    dsq, dsk, dsv, dkm, dvm, dsinks = _swa_bwd(p["swa_sinks"], sq, sk, sv, oswa, doswa)
    grads["swa_sinks"] = dsinks[:, 0]
    dgq, dgk, dgv, dla = _gla_bwd(gq, gk, gv, la, ss, dogla)
    dh1, dproj, grads["mix_pre_norm"], dwa2p, grads["gla_b_a"] = _mix_in_bwd(
        dh2, h1, p["mix_pre_norm"], p["w_in"], p["gla_w_a2"], p["gla_b_a"], cos, sin, ga, dgq, dgk, dgv, dgg, dla,
        dsq, dsk, dsv, dkm, dvm)
    grads["gla_w_a2"] = dwa2p[:GLA_RANK]
    grads["w_in"] = _unpack_dwin(_xty(n2, dproj))

    dh0, df1, dg1, du1, grads["ffn1_pre_norm"], grads["ffn1_post_norm"] = _ffn_bwd(
        dh1, h0, f1, g1, u1, p["ffn1_pre_norm"], p["ffn1_w"], p["ffn1_post_norm"])
    (gud,) = _ffn_wgrad(n1, df1, dg1, du1, a1)
    grads["ffn1_w_gate"], grads["ffn1_w_up"], grads["ffn1_w_down"] = gud[:, :FJ], gud[:, FJ:2 * FJ], gud[:, 2 * FJ:]
    grads["meta_tokens"] = dh0[PAD:BLK]
    return sse[0, 0], dh0[BLK:], grads


WEIGHTS = ['meta_tokens', 'ffn1_pre_norm', 'ffn1_w_gate', 'ffn1_w_up', 'ffn1_w_down', 'ffn1_post_norm',
           'mix_pre_norm', 'w_in', 'gla_w_a2', 'gla_b_a', 'gla_out_norm', 'swa_sinks', 'swa_out_norm', 'w_out',
           'mix_post_norm', 'ffn2_pre_norm', 'ffn2_w_gate', 'ffn2_w_up', 'ffn2_w_down', 'ffn2_post_norm']
BIG = ['ffn1_w_gate', 'ffn1_w_up', 'ffn1_w_down', 'w_in', 'w_out', 'ffn2_w_gate', 'ffn2_w_up', 'ffn2_w_down']
SMALL = [n for n in WEIGHTS if n not in BIG]
FJ = D_FF // N_CHIPS
D_IN_J = D_IN // N_CHIPS
D_OUT_J = D_MODEL // N_CHIPS
TRANSPOSED = ('ffn1_w_gate', 'ffn1_w_up', 'ffn2_w_gate', 'ffn2_w_up')


def _shard2d(name, a):
    return a[0].T if name in TRANSPOSED else a[0]


def _unshard2d(name, a):
    return (a.T if name in TRANSPOSED else a)[None]


def kernel(x, meta_tokens, ffn1_pre_norm, ffn1_w_gate, ffn1_w_up, ffn1_w_down, ffn1_post_norm, mix_pre_norm, w_in, gla_w_a2, gla_b_a, gla_out_norm, swa_sinks, swa_out_norm, w_out, mix_post_norm, ffn2_pre_norm, ffn2_w_gate, ffn2_w_up, ffn2_w_down, ffn2_post_norm, loss_target, m_meta_tokens, m_ffn1_pre_norm, m_ffn1_w_gate, m_ffn1_w_up, m_ffn1_w_down, m_ffn1_post_norm, m_mix_pre_norm, m_w_in, m_gla_w_a2, m_gla_b_a, m_gla_out_norm, m_swa_sinks, m_swa_out_norm, m_w_out, m_mix_post_norm, m_ffn2_pre_norm, m_ffn2_w_gate, m_ffn2_w_up, m_ffn2_w_down, m_ffn2_post_norm, v_meta_tokens, v_ffn1_pre_norm, v_ffn1_w_gate, v_ffn1_w_up, v_ffn1_w_down, v_ffn1_post_norm, v_mix_pre_norm, v_w_in, v_gla_w_a2, v_gla_b_a, v_gla_out_norm, v_swa_sinks, v_swa_out_norm, v_w_out, v_mix_post_norm, v_ffn2_pre_norm, v_ffn2_w_gate, v_ffn2_w_up, v_ffn2_w_down, v_ffn2_post_norm):
    args = dict(locals())
    w = {n: args[n] for n in WEIGHTS}
    mom = {n: args["m_" + n] for n in WEIGHTS}
    var = {n: args["v_" + n] for n in WEIGHTS}
    cx, cy, cc = lax.axis_index("x"), lax.axis_index("y"), lax.axis_index("c")
    q_idx = (2 * cx + cy).astype(jnp.int32).reshape(1)
    c_idx = cc.astype(jnp.int32).reshape(1)

    q_chip = 2 * cx + cy
    bf = {n: _own_slot(_shard2d(n, w[n]).astype(BF16), q_chip) for n in ("w_in", "w_out")}
    for ffn in ("ffn1", "ffn2"):
        stacked = jnp.concatenate([_shard2d(ffn + s, w[ffn + s]) for s in ("_w_gate", "_w_up", "_w_down")], axis=0)
        bf[ffn] = _own_slot(stacked.astype(BF16), q_chip)
    qc_idx = jnp.stack([q_chip, cc]).astype(jnp.int32)
    early = _GatherChips([_own_slot(w["meta_tokens"], q_chip),
                          _own_slot(w["gla_w_a2"].reshape(GLA_RANK, GLA_KW // N_CHIPS), q_chip)])
    meta4, wa24 = _comm_call(early, "gather_small")
    meta_full = meta4.transpose(1, 0, 2).reshape(N_META, D_MODEL)
    wa2p = jnp.pad(wa24.transpose(1, 0, 2).reshape(GLA_RANK, GLA_KW), ((0, 128 - GLA_RANK), (0, 0))).astype(BF16)
    sinks = w["swa_sinks"].reshape(SWA_QH)

    seq, target = x[0], loss_target[0]
    t = seq.shape[0] + BLK
    h0, n1 = _embed_norm(seq, meta_full, w["ffn1_pre_norm"])
    cos, sin = _rope_tables(t)
    late = _GatherChips([bf["w_in"], bf["w_out"], bf["ffn2"]])
    (h1, g1, u1, a1, f1), (w31,), (win4, wout4, w32) = _ffn_fwd_gather(
        h0, n1, bf["ffn1"], w["ffn1_post_norm"], qc_idx, late)
    winp = _pack_win(win4)
    wout = wout4.reshape(D_MODEL, D_MODEL)
    n2, gq, gk, gv, gg, ga, la, sq, sk, sv = _mix_proj(h1, w["mix_pre_norm"], winp, wa2p, w["gla_b_a"], cos, sin)
    ogla, ss = _gla_fwd(gq, gk, gv, la)
    oswa = _swa_fwd(sinks, sq, sk, sv)
    h2, cat, m = _mix_out(h1, ogla, gg, oswa, w["gla_out_norm"], w["swa_out_norm"], wout, w["mix_post_norm"])
    g = {}
    dy, n3, g3, u3, a3, df3, g["ffn2_post_norm"], sse = _ffn_fwd(
        h2, w["ffn2_pre_norm"], w32, w["ffn2_post_norm"], target=target)

    dh2, dg3, du3, g["ffn2_pre_norm"] = _ffn_bwd(
        dy, h2, None, g3, u3, w["ffn2_pre_norm"], w32, w["ffn2_post_norm"], df=df3)
    (gf2,) = _ffn_wgrad(n3, df3, dg3, du3, a3)
    (dogla, dgg, doswa, dm, g["mix_post_norm"], g["gla_out_norm"], g["swa_out_norm"]), (rgf2,) = _mix_out_bwd(
        dh2, m, ogla, gg, oswa, w["gla_out_norm"], w["swa_out_norm"], wout, w["mix_post_norm"],
        hook=_PairExchange([gf2]))
    sgf2 = _pair_sum(gf2, rgf2, c_idx)
    gout = _xty(cat, dm).reshape(N_CHIPS, D_OUT_J, D_MODEL).astype(BF16)
    (dsq, dsk, dsv, dkm, dvm, dsinks), (ogf2,) = _swa_bwd(sinks, sq, sk, sv, oswa, doswa,
                                                          hook=_ChipScatter([sgf2]))
    g["swa_sinks"] = dsinks
    dgq, dgk, dgv, dla = _gla_bwd(gq, gk, gv, la, ss, dogla)
    dh1, dproj, g["mix_pre_norm"], dwa2p, g["gla_b_a"] = _mix_in_bwd(
        dh2, h1, w["mix_pre_norm"], winp, wa2p, w["gla_b_a"], cos, sin, ga, dgq, dgk, dgv, dgg, dla,
        dsq, dsk, dsv, dkm, dvm)
    g["gla_w_a2"] = dwa2p[:GLA_RANK]
    gin = _unpack_dwin(_xty(n2, dproj))
    (dh0, df1, dg1, du1, g["ffn1_pre_norm"], g["ffn1_post_norm"]), (rgin, rgout) = _ffn_bwd(
        dh1, h0, f1, g1, u1, w["ffn1_pre_norm"], w31, w["ffn1_post_norm"],
        hook=_PairExchange([gin, gout]))
    sgin, sgout = _pair_sum(gin, rgin, c_idx), _pair_sum(gout, rgout, c_idx)
    own1, others1, (ogin, ogout) = _ffn_wgrad_reduce(n1, df1, dg1, du1, a1, qc_idx, _ChipScatter([sgin, sgout]))
    g["meta_tokens"] = dh0[PAD:BLK]
    grad_x = dh0[BLK:]
    halves = [_chip_sum(own1[None], others1, jnp.zeros((1,), jnp.int32))]
    halves += [_chip_sum(s, o, q_idx) for s, o in ((sgin, ogin), (sgout, ogout), (sgf2, ogf2))]
    others = _comm_call(_PairShare(halves), "pair_share")
    reduced = {"ffn1_w_gate": (0, 0), "ffn1_w_up": (0, FJ), "ffn1_w_down": (0, 2 * FJ), "w_in": (1, 0),
               "w_out": (2, 0), "ffn2_w_gate": (3, 0), "ffn2_w_up": (3, FJ), "ffn2_w_down": (3, 2 * FJ)}
    grad, delta, new_m, new_v = {}, {}, {}, {}
    for n in BIG:
        k, row0 = reduced[n]
        outs = _adamw_halves(_shard2d(n, w[n]), halves[k], others[k], _shard2d(n, mom[n]), _shard2d(n, var[n]),
                             c_idx, row0)
        grad[n], delta[n], new_m[n], new_v[n] = [_unshard2d(n, a) for a in outs]

    late = ["gla_w_a2", "swa_sinks"]
    direct = [n for n in SMALL if n not in late]
    names = direct + late
    gathered = _all_gather_devices([g[n] for n in names] + [sse])
    mat = lambda a: a.reshape(a.shape[-2:])
    none3 = [None] * (len(late) + 1)
    outs = _small_update(q_idx, gathered, [mat(w[n]) for n in direct] + none3, [mat(mom[n]) for n in direct] + none3,
                         [mat(var[n]) for n in direct] + none3, [n == "meta_tokens" for n in names] + [False])
    sum_a2, sum_sinks, sum_sse = outs[4 * len(direct):]
    loss = sum_sse[0, 0] * (0.5 / D_MODEL)
    g_late = [lax.dynamic_slice_in_dim(sum_a2, q_chip * (GLA_KW // N_CHIPS), GLA_KW // N_CHIPS, axis=1)[None],
              sum_sinks[:, 0].reshape(1, 1, SWA_QH)]
    outs = list(outs[:4 * len(direct)]) + list(_small_update(
        q_idx, g_late, [mat(w[n]) for n in late], [mat(mom[n]) for n in late], [mat(var[n]) for n in late],
        [False, False]))
    for k, n in enumerate(names):
        grad[n], delta[n], new_m[n], new_v[n] = [a.reshape(w[n].shape) for a in outs[4 * k:4 * k + 4]]

    return (loss, grad_x[None], *[grad[n] for n in WEIGHTS], *[delta[n] for n in WEIGHTS],
            *[new_m[n] for n in WEIGHTS], *[new_v[n] for n in WEIGHTS])
```

```python
import functools
import math

import numpy as np
import jax
import jax.numpy as jnp
from jax import lax
from jax.experimental import pallas as pl
from jax.experimental.pallas import tpu as pltpu

F32 = jnp.float32
BF16 = jnp.bfloat16
MESH = pl.DeviceIdType.MESH

D_MODEL = 1024
D_FF = 2816
N_CHIPS = 4
N_DEV = 8
N_META = 16
BLK = 128
PAD = BLK - N_META
GLA_CHUNK = 64
GLA_HEADS = 4
GLA_DV = 128
GLA_DK = 64
GLA_KW = GLA_HEADS * GLA_DK
GLA_W = GLA_HEADS * GLA_DV
GLA_RANK = 16
GLA_TAU = 16.0
SWA_HD = 64
SWA_QH = 8
SWA_KVH = 2
SWA_W = SWA_QH * SWA_HD
WINDOW = 128
ROPE_THETA = 10000.0
EPS = 1e-6
NEG_INF = -1e30
IN_SPLITS = (256, 256, 512, 512, 16, 512, 128, 128)
D_IN = sum(IN_SPLITS)
P_GQ, P_GK, P_GV, P_GG, P_GA, P_SQ, P_SK, P_SV, P_END = 0, 256, 512, 1024, 1536, 1664, 2176, 2432, 2688
ADAM_LR, ADAM_B1, ADAM_B2, ADAM_EPS, ADAM_WD, ADAM_STEP = 0.001, 0.9, 0.999, 1e-08, 0.01, 10
VMEM_LIMIT = 56 * 1024 * 1024

NT = (((1,), (1,)), ((), ()))
TN = (((0,), (0,)), ((), ()))


def _cparams(n_axes):
    return pltpu.CompilerParams(dimension_semantics=("arbitrary",) * n_axes, vmem_limit_bytes=VMEM_LIMIT)


def _row_tile(t):
    for tm in (640, 512, 384, 256, 128):
        if t % tm == 0:
            return tm
    raise ValueError(t)


SEQ_BLOCKS_PER_STEP = 5


def _seq_tile(t):
    return SEQ_BLOCKS_PER_STEP * BLK if t % (SEQ_BLOCKS_PER_STEP * BLK) == 0 else BLK


ROW_PARTS = 2


def _row_parts(tm):
    n = ROW_PARTS if tm % (16 * ROW_PARTS) == 0 else 1
    return [slice(k * (tm // n), (k + 1) * (tm // n)) for k in range(n)]


def _contract_tile(t):
    return 1664 if t % 1664 == 0 else _row_tile(t)


def _div_tile(r, cap=512):
    best = None
    for tr in range(8, min(r, cap) + 1, 8):
        if r % tr == 0:
            best = tr
    return best if best is not None else r


def _dot(a, b):
    return jnp.dot(a, b, preferred_element_type=F32)


def _dg(a, b, dims):
    return lax.dot_general(a, b, dims, preferred_element_type=F32)


def _rms(x, w):
    r = lax.rsqrt(jnp.mean(x * x, axis=-1, keepdims=True) + EPS)
    xh = x * r
    return xh * w, xh, r


def _rms_bwd(xh, r, w, dy):
    wdy = dy * w
    dx = r * (wdy - xh * jnp.mean(wdy * xh, axis=-1, keepdims=True))
    dw = jnp.sum(dy * xh, axis=0, keepdims=True)
    return dx, dw


def _sigmoid(x):
    return 1.0 / (1.0 + jnp.exp(-x))


def _full(shape):
    nd = len(shape)
    return pl.BlockSpec(shape, lambda *_: (0,) * nd)


ANY = pl.BlockSpec(memory_space=pl.ANY)


def _pallas(body, *, name, grid, in_specs, out_specs, out_shape, args, scratch_shapes=(), hook=None):
    n_axes = len(grid)
    if hook is None:
        return pl.pallas_call(body, name=name, grid=grid, in_specs=list(in_specs), out_specs=list(out_specs),
                              out_shape=list(out_shape), scratch_shapes=list(scratch_shapes),
                              compiler_params=_cparams(n_axes))(*args)
    n_in, n_out, n_scr = len(in_specs), len(out_specs), len(scratch_shapes)
    h_in, h_out = len(hook.inputs), len(hook.out_shape)
    total = math.prod(grid)

    def wrapped(*refs):
        ins, hins = refs[:n_in], refs[n_in:n_in + h_in]
        o0 = n_in + h_in
        outs, houts = refs[o0:o0 + n_out], refs[o0 + n_out:o0 + n_out + h_out]
        s0 = o0 + n_out + h_out
        scr, hscr = refs[s0:s0 + n_scr], refs[s0 + n_scr:]
        step = pl.program_id(0)
        for a in range(1, n_axes):
            step = step * grid[a] + pl.program_id(a)

        @pl.when(step == 0)
        def _():
            hook.start(hins, houts, hscr)

        body(*ins, *outs, *scr)

        if hook.has_mid:
            @pl.when(step == (3 * total) // 4)
            def _():
                hook.mid(hins, houts, hscr)

        @pl.when(step == total - 1)
        def _():
            hook.finish(hins, houts, hscr)

    res = pl.pallas_call(
        wrapped, name=name, grid=grid, in_specs=list(in_specs) + [ANY] * h_in,
        out_specs=list(out_specs) + [ANY] * h_out, out_shape=list(out_shape) + list(hook.out_shape),
        scratch_shapes=list(scratch_shapes) + list(hook.scratch), compiler_params=_cparams(n_axes),
        input_output_aliases={n_in + a: n_out + b for a, b in hook.aliases},
    )(*args, *hook.inputs)
    return res[:n_out], res[n_out:]


def _ffn_weight_specs(w3):
    fj = w3.shape[1] // 3
    return fj, [pl.BlockSpec((None, fj, D_MODEL), functools.partial(lambda i, j, k: (j, k, 0), k=k)) for k in range(3)]


def _ffn_fwd(h, wpre, w3, wpost, hook=None, target=None):
    t = h.shape[0]
    tm = _row_tile(t)
    nj, rows3, _ = w3.shape
    fj = rows3 // 3
    nblk = tm // BLK if target is not None else 0

    def body(*refs):
        h_ref, wpre_ref, w_hbm, wpost_ref = refs[:4]
        t_refs = refs[4:4 + nblk]
        hout_ref, n_ref, p1_ref, p2_ref, a_ref, f_ref = refs[4 + nblk:10 + nblk]
        acc_ref, wv, wsem = refs[-3:]
        i = pl.program_id(0)
        j = pl.program_id(1)

        @pl.when((i == 0) & (j == 0))
        def _():
            for k in range(nj):
                pltpu.make_async_copy(w_hbm.at[k], wv.at[k], wsem.at[k]).start()

        @pl.when(i == 0)
        def _():
            pltpu.make_async_copy(w_hbm.at[j], wv.at[j], wsem.at[j]).wait()

        @pl.when(j == 0)
        def _():
            y, _, _ = _rms(h_ref[...], wpre_ref[...])
            n_ref[...] = y.astype(BF16)
            acc_ref[...] = jnp.zeros_like(acc_ref)

        if target is not None:
            dwpost_ref, sse_ref = refs[10 + nblk:12 + nblk]

            @pl.when((i == 0) & (j == 0))
            def _():
                dwpost_ref[...] = jnp.zeros_like(dwpost_ref)
                sse_ref[...] = jnp.zeros_like(sse_ref)

        n = n_ref[...]
        g = _dg(n, wv[j, 0:fj], NT)
        u = _dg(n, wv[j, fj:2 * fj], NT)
        sg = _sigmoid(g)
        silu = g * sg
        p1_ref[...] = (u * (sg + silu * (1.0 - sg))).astype(BF16)
        p2_ref[...] = silu.astype(BF16)
        a = (silu * u).astype(BF16)
        a_ref[...] = a
        acc_ref[...] += _dot(a, wv[j, 2 * fj:3 * fj])

        @pl.when(j == nj - 1)
        def _():
            f = acc_ref[...]
            wpost = wpost_ref[...]
            y, fh, r = _rms(f, wpost)
            hout = h_ref[...] + 0.5 * y
            if target is None:
                f_ref[...] = f
                hout_ref[...] = hout
            else:
                sse = jnp.zeros((1, 1), F32)
                errs = []
                for k in range(nblk):
                    err = hout[k * BLK:(k + 1) * BLK] - t_refs[k][...]
                    if k == 0:
                        err = jnp.where(i > 0, err, 0.0)
                    errs.append(err)
                    sse = sse + jnp.sum(jnp.sum(err * err, axis=1, keepdims=True), axis=0, keepdims=True)
                dy = (jnp.concatenate(errs, axis=0) if nblk > 1 else errs[0]) * (1.0 / D_MODEL)
                hout_ref[...] = dy
                df, dw = _rms_bwd(fh, r, wpost, 0.5 * dy)
                f_ref[...] = df.astype(BF16)
                dwpost_ref[...] += dw
                sse_ref[...] += jnp.broadcast_to(sse, sse_ref.shape)

    row = pl.BlockSpec((tm, D_MODEL), lambda i, j: (i, 0))
    vec = pl.BlockSpec((1, D_MODEL), lambda i, j: (0, 0))
    act = pl.BlockSpec((None, tm, fj), lambda i, j: (j, i, 0))
    t_specs = [pl.BlockSpec((BLK, D_MODEL), functools.partial(lambda i, j, k: (jnp.maximum(nblk * i + k - 1, 0), 0), k=k))
               for k in range(nblk)]
    loss_spec = [vec, _full((1, 128))] if target is not None else []
    loss_shape = [jax.ShapeDtypeStruct((1, D_MODEL), F32), jax.ShapeDtypeStruct((1, 128), F32)] if (
        target is not None) else []
    return _pallas(
        body, name="ffn_fwd", grid=(t // tm, nj),
        in_specs=[row, vec, ANY, vec] + t_specs,
        out_specs=[row, row, act, act, act, row] + loss_spec,
        out_shape=[jax.ShapeDtypeStruct((t, D_MODEL), F32), jax.ShapeDtypeStruct((t, D_MODEL), BF16),
                   jax.ShapeDtypeStruct((nj, t, fj), BF16), jax.ShapeDtypeStruct((nj, t, fj), BF16),
                   jax.ShapeDtypeStruct((nj, t, fj), BF16),
                   jax.ShapeDtypeStruct((t, D_MODEL), F32 if target is None else BF16)] + loss_shape,
        scratch_shapes=[pltpu.VMEM((tm, D_MODEL), F32), pltpu.VMEM((nj, rows3, D_MODEL), BF16),
                        pltpu.SemaphoreType.DMA((nj,))],
        args=(h, wpre, w3, wpost) + (target,) * nblk, hook=hook)


def _ffn_bwd(dhout, h, f, p14, p24, wpre, w3, wpost, hook=None, df=None):
    t = h.shape[0]
    tm = _row_tile(t)
    nj = w3.shape[0]
    fj, wspecs = _ffn_weight_specs(w3)
    have_df = df is not None

    def body(dhout_ref, h_ref, f_ref, p1_ref, p2_ref, wpre_ref, wg_ref, wu_ref, wd_ref, wpost_ref, *rest):
        if have_df:
            dh_ref, dg_ref, du_ref, dwpre_ref, dn_ref = rest
            df_ref = f_ref
        else:
            dh_ref, df_ref, dg_ref, du_ref, dwpre_ref, dwpost_ref, dn_ref = rest
        i = pl.program_id(0)
        j = pl.program_id(1)

        @pl.when((i == 0) & (j == 0))
        def _():
            dwpre_ref[...] = jnp.zeros_like(dwpre_ref)
            if not have_df:
                dwpost_ref[...] = jnp.zeros_like(dwpost_ref)

        @pl.when(j == 0)
        def _():
            if not have_df:
                wpost = wpost_ref[...]
                _, fh, r = _rms(f_ref[...], wpost)
                dfv, dw = _rms_bwd(fh, r, wpost, 0.5 * dhout_ref[...])
                dwpost_ref[...] += dw
                df_ref[...] = dfv.astype(BF16)
            dn_ref[...] = jnp.zeros_like(dn_ref)

        parts = _row_parts(tm)
        das = [_dg(df_ref[rows, :], wd_ref[...], NT) for rows in parts]
        for rows, da in zip(parts, das):
            dg = (da * p1_ref[rows, :].astype(F32)).astype(BF16)
            du = (da * p2_ref[rows, :].astype(F32)).astype(BF16)
            dg_ref[rows, :] = dg
            du_ref[rows, :] = du
            dn_ref[rows, :] += _dot(dg, wg_ref[...]) + _dot(du, wu_ref[...])

        @pl.when(j == nj - 1)
        def _():
            wpre = wpre_ref[...]
            _, hh, r = _rms(h_ref[...], wpre)
            dx, dw = _rms_bwd(hh, r, wpre, dn_ref[...])
            dwpre_ref[...] += dw
            dh_ref[...] = dhout_ref[...] + dx

    row = pl.BlockSpec((tm, D_MODEL), lambda i, j: (i, 0))
    vec = pl.BlockSpec((1, D_MODEL), lambda i, j: (0, 0))
    act = pl.BlockSpec((None, tm, fj), lambda i, j: (j, i, 0))
    actshape = jax.ShapeDtypeStruct((nj, t, fj), BF16)
    rowf, rowb, vecf = (jax.ShapeDtypeStruct((t, D_MODEL), F32), jax.ShapeDtypeStruct((t, D_MODEL), BF16),
                        jax.ShapeDtypeStruct((1, D_MODEL), F32))
    return _pallas(
        body, name="ffn_bwd", grid=(t // tm, nj),
        in_specs=[row, row, row, act, act, vec] + wspecs + [vec],
        out_specs=[row, act, act, vec] if have_df else [row, row, act, act, vec, vec],
        out_shape=[rowf, actshape, actshape, vecf] if have_df else [rowf, rowb, actshape, actshape, vecf, vecf],
        scratch_shapes=[pltpu.VMEM((tm, D_MODEL), F32)],
        args=(dhout, h, df if have_df else f, p14, p24, wpre, w3, w3, w3, wpost), hook=hook)


def _ffn_wgrad(n, df, dg4, du4, a4, hook=None):
    t = n.shape[0]
    tm = _contract_tile(t)
    ni = t // tm
    nj, _, fj = dg4.shape

    def body(n_ref, df_ref, dg_ref, du_ref, a_ref, dw_ref, acc):
        i = pl.program_id(1)

        @pl.when(i == 0)
        def _():
            acc[...] = jnp.zeros_like(acc)

        nn = n_ref[...]
        acc[0:fj, :] += _dg(dg_ref[...], nn, TN)
        acc[fj:2 * fj, :] += _dg(du_ref[...], nn, TN)
        acc[2 * fj:3 * fj, :] += _dg(a_ref[...], df_ref[...], TN)

        @pl.when(i == ni - 1)
        def _():
            dw_ref[...] = acc[...].astype(BF16)

    row = pl.BlockSpec((tm, D_MODEL), lambda j, i: (i, 0))
    act = pl.BlockSpec((None, tm, fj), lambda j, i: (j, i, 0))
    return _pallas(
        body, name="ffn_wgrad", grid=(nj, ni),
        in_specs=[row, row, act, act, act],
        out_specs=[pl.BlockSpec((None, 3 * fj, D_MODEL), lambda j, i: (j, 0, 0))],
        out_shape=[jax.ShapeDtypeStruct((nj, 3 * fj, D_MODEL), BF16)],
        scratch_shapes=[pltpu.VMEM((3 * fj, D_MODEL), F32)],
        args=(n, df, dg4, du4, a4), hook=hook)


def _embed_norm(x, meta, w):
    t = x.shape[0] + BLK
    tm = _row_tile(t)
    nblk = tm // BLK

    def body(*refs):
        x_refs = refs[:nblk]
        meta_ref, w_ref, h_ref, n_ref = refs[nblk:]
        i = pl.program_id(0)
        first = jnp.concatenate([jnp.zeros((PAD, D_MODEL), F32), meta_ref[...]], axis=0)
        blocks = [jnp.where(i == 0, first, x_refs[0][...])] + [r[...] for r in x_refs[1:]]
        h = jnp.concatenate(blocks, axis=0) if nblk > 1 else blocks[0]
        h_ref[...] = h
        y, _, _ = _rms(h, w_ref[...])
        n_ref[...] = y.astype(BF16)

    x_specs = [pl.BlockSpec((BLK, D_MODEL), functools.partial(lambda i, k: (jnp.maximum(nblk * i + k - 1, 0), 0), k=k))
               for k in range(nblk)]
    row = pl.BlockSpec((tm, D_MODEL), lambda i: (i, 0))
    return pl.pallas_call(
        body, name="embed_norm", grid=(t // tm,),
        in_specs=x_specs + [_full((N_META, D_MODEL)), _full((1, D_MODEL))], out_specs=[row, row],
        out_shape=[jax.ShapeDtypeStruct((t, D_MODEL), F32), jax.ShapeDtypeStruct((t, D_MODEL), BF16)],
        compiler_params=_cparams(1),
    )(*([x] * nblk), meta, w)


FWD_RELATION = (None, 0, 1, 2)


def _ffn_fwd_gather(h, n, wbuf, wpost, qc_idx, late):
    t = h.shape[0]
    tm = _row_tile(t)
    ni = t // tm
    nj, rows3, _ = wbuf.shape
    fj = rows3 // 3
    assert nj == N_CHIPS and ni >= 4
    wbufs = [wbuf]
    nw = 1
    n_lin, n_lout = len(late.inputs), len(late.out_shape)
    wait_step = ni - 3

    def body(qc_ref, h_ref, n_ref, wpost_ref, *rest):
        wb_in = rest[:nw]
        lins = rest[nw:nw + n_lin]
        o0 = nw + n_lin
        hout_ref, p1_ref, p2_ref, a_ref, f_hbm = rest[o0:o0 + 5]
        wb = rest[o0 + 5:o0 + 5 + nw]
        louts = rest[o0 + 5 + nw:o0 + 5 + nw + n_lout]
        s0 = o0 + 5 + nw + n_lout
        wv, wsem, send, recv, fbuf, fr_sem, fw_sem = rest[s0:s0 + 7]
        lscr = rest[s0 + 7:]
        p = pl.program_id(0)
        i = pl.program_id(1)
        step = p * ni + i
        fslot = step % 3
        nslot = (step + 1) % 3

        def f_tile(tile):
            return f_hbm.at[pl.ds(pl.multiple_of(tile * tm, 8), tm)]

        @pl.when(step > 1)
        def _():
            pltpu.make_async_copy(fbuf.at[nslot], f_tile(i), fw_sem.at[nslot]).wait()

        nxt = step + 1

        @pl.when((nxt < N_CHIPS * ni) & (nxt >= ni))
        def _():
            pltpu.make_async_copy(f_tile(nxt % ni), fbuf.at[nslot], fr_sem.at[nslot]).start()

        @pl.when(p > 0)
        def _():
            pltpu.make_async_copy(f_tile(i), fbuf.at[fslot], fr_sem.at[fslot]).wait()
        x, y, c, chips = _place()
        q = 2 * x + y
        sibling = (x, y, 1 - c)
        mine, other = _half(rows3, c), _half(rows3, 1 - c)

        def load(chunk, slot, src):
            return [pltpu.make_async_copy(src[t].at[chunk], wv.at[slot, t], wsem.at[slot, t]) for t in range(nw)]

        @pl.when((p == 0) & (i == 0))
        def _():
            for j, (cx, cy) in enumerate(chips):
                for t in range(nw):
                    _remote(send.at[t, j], recv.at[t, j], wb_in[t].at[q, mine], wb[t].at[q, mine], (cx, cy, c)).start()
            for cp in load(q, 0, wb_in):
                cp.start()
            for cp in load(q, 0, wb_in):
                cp.wait()

        @pl.when((p == 1) & (i == 0))
        def _():
            late.start(lins, louts, lscr)

        for pp in range(1, N_CHIPS):
            j = FWD_RELATION[pp]
            cx, cy = chips[j]
            chunk = 2 * cx + cy

            @pl.when((p == pp - 1) & (i == wait_step))
            def _(j=j, cx=cx, cy=cy, chunk=chunk, pp=pp):
                for t in range(nw):
                    got = wb[t].at[chunk, mine]
                    _remote(send.at[t, j], recv.at[t, j], got, got, (cx, cy, c)).wait_recv()
                    _remote(send.at[t, 3 + j], recv.at[t, 3 + j], got, got, sibling).start()
                for t in range(nw):
                    rest_half = wb[t].at[chunk, other]
                    _remote(send.at[t, 3 + j], recv.at[t, 3 + j], rest_half, rest_half, sibling).wait_recv()
                for cp in load(chunk, pp % 2, wb):
                    cp.start()

            @pl.when((p == pp) & (i == 0))
            def _(chunk=chunk, pp=pp):
                for cp in load(chunk, pp % 2, wb):
                    cp.wait()

        @pl.when((p == N_CHIPS - 1) & (i == ni // 2))
        def _():
            late.mid(lins, louts, lscr)

        slot = p % 2
        nn = n_ref[...]
        g = _dg(nn, wv[slot, 0, 0:fj], NT)
        u = _dg(nn, wv[slot, 0, fj:2 * fj], NT)
        sg = _sigmoid(g)
        silu = g * sg
        p1_ref[...] = (u * (sg + silu * (1.0 - sg))).astype(BF16)
        p2_ref[...] = silu.astype(BF16)
        a = (silu * u).astype(BF16)
        a_ref[...] = a
        part = _dot(a, wv[slot, 0, 2 * fj:3 * fj])

        @pl.when(p == 0)
        def _():
            fbuf[fslot] = part

        @pl.when(p > 0)
        def _():
            fbuf[fslot] = fbuf[fslot] + part

        pltpu.make_async_copy(fbuf.at[fslot], f_tile(i), fw_sem.at[fslot]).start()

        @pl.when(p == N_CHIPS - 1)
        def _():
            yv, _, _ = _rms(fbuf[fslot], wpost_ref[...])
            hout_ref[...] = h_ref[...] + 0.5 * yv

        @pl.when((p == N_CHIPS - 1) & (i == ni - 1))
        def _():
            pslot = (step + 2) % 3
            pltpu.make_async_copy(fbuf.at[pslot], f_tile(i), fw_sem.at[pslot]).wait()
            pltpu.make_async_copy(fbuf.at[fslot], f_tile(i), fw_sem.at[fslot]).wait()
            for t in range(nw):
                for j, (cx, cy) in enumerate(chips):
                    sent = wb[t].at[2 * cx + cy, mine]
                    _remote(send.at[t, j], recv.at[t, j], sent, sent, (cx, cy, c)).wait_send()
                    _remote(send.at[t, 3 + j], recv.at[t, 3 + j], sent, sent, sibling).wait_send()
            late.finish(lins, louts, lscr)

    def last_pass_rows(p, i, qc_ref):
        return (jnp.where(p == N_CHIPS - 1, i, 0), 0)

    def chunk_rows(p, i, qc_ref):
        order = ((p & 1) << 1) | (p >> 1)
        return (jnp.bitwise_xor(qc_ref[0], order), i, 0)

    row = pl.BlockSpec((tm, D_MODEL), lambda p, i, qc_ref: (i, 0))
    last_row = pl.BlockSpec((tm, D_MODEL), last_pass_rows)
    act = pl.BlockSpec((None, tm, fj), chunk_rows)
    act_shape = jax.ShapeDtypeStruct((nj, t, fj), BF16)
    res = pl.pallas_call(
        body, name="ffn_fwd_gather",
        grid_spec=pltpu.PrefetchScalarGridSpec(
            num_scalar_prefetch=1, grid=(N_CHIPS, ni),
            in_specs=[last_row, row, pl.BlockSpec((1, D_MODEL), lambda p, i, qc_ref: (0, 0))]
            + [ANY] * (nw + n_lin),
            out_specs=[last_row, act, act, act, ANY] + [ANY] * (nw + n_lout),
            scratch_shapes=[pltpu.VMEM((2, nw, rows3, D_MODEL), BF16), pltpu.SemaphoreType.DMA((2, nw)),
                            pltpu.SemaphoreType.DMA((nw, 6)), pltpu.SemaphoreType.DMA((nw, 6)),
                            pltpu.VMEM((3, tm, D_MODEL), F32), pltpu.SemaphoreType.DMA((3,)),
                            pltpu.SemaphoreType.DMA((3,))] + list(late.scratch)),
        out_shape=[jax.ShapeDtypeStruct((t, D_MODEL), F32), act_shape, act_shape, act_shape,
                   jax.ShapeDtypeStruct((t, D_MODEL), F32)]
        + [jax.ShapeDtypeStruct(b.shape, b.dtype) for b in wbufs] + list(late.out_shape),
        input_output_aliases={**{4 + t: 5 + t for t in range(nw)},
                              **{4 + nw + a: 5 + nw + b for a, b in late.aliases}},
        compiler_params=_cparams(2),
    )(qc_idx, h, n, wpost, *wbufs, *late.inputs)
    return res[:5], res[5:5 + nw], res[5 + nw:]


PASS_RELATION = (2, 0, 1)


def _ffn_wgrad_reduce(n, df, dg4, du4, a4, qc_idx, hook):
    t = n.shape[0]
    tm = _contract_tile(t)
    ni = t // tm
    nj, _, fj = dg4.shape
    assert nj == N_CHIPS
    hrows = 3 * fj // 2
    n_hin, n_hout = len(hook.inputs), len(hook.out_shape)

    def body(qc_ref, n_ref, df_ref, dg_ref, du_ref, a_ref, *rest):
        hins = rest[:n_hin]
        own_ref, others_ref = rest[n_hin:n_hin + 2]
        houts = rest[n_hin + 2:n_hin + 2 + n_hout]
        s0 = n_hin + 2 + n_hout
        acc, stage, land, sumbuf, px_send, px_recv, cs_send, cs_recv, own_sem = rest[s0:s0 + 9]
        hscr = rest[s0 + 9:]
        k_pass = pl.program_id(0)
        i = pl.program_id(1)
        x, y, c, chips = _place()
        mine = pl.ds(pl.multiple_of(c * hrows, 8), hrows)
        other = pl.ds(pl.multiple_of((1 - c) * hrows, 8), hrows)

        def to_owner(k):
            j = PASS_RELATION[k]
            return _remote(cs_send.at[j], cs_recv.at[j], sumbuf.at[k % 2], others_ref.at[j], (*chips[j], c))

        @pl.when((k_pass == 0) & (i == 0))
        def _():
            hook.start(hins, houts, hscr)

        if hook.has_mid:
            @pl.when((k_pass == N_CHIPS // 2) & (i == 0))
            def _():
                hook.mid(hins, houts, hscr)

        @pl.when(i == 0)
        def _():
            acc[...] = jnp.zeros_like(acc)

        nn = n_ref[...]
        acc[0:fj, :] += _dg(dg_ref[...], nn, TN)
        acc[fj:2 * fj, :] += _dg(du_ref[...], nn, TN)
        acc[2 * fj:3 * fj, :] += _dg(a_ref[...], df_ref[...], TN)

        for k in range(N_CHIPS):
            @pl.when((k_pass == k) & (i == ni - 1))
            def _(k=k):
                slot = k % 2
                stage[...] = acc[other, :].astype(BF16)
                swap = _remote(px_send.at[k], px_recv.at[k], stage, land.at[slot], (x, y, 1 - c))
                swap.start()
                swap.wait_recv()
                pair = acc[mine, :] + land[slot].astype(F32)
                if k >= 2:
                    to_owner(k - 2).wait_send()
                sumbuf[slot] = pair.astype(BF16)
                swap.wait_send()
                if k < N_CHIPS - 1:
                    to_owner(k).start()
                else:
                    keep = pltpu.make_async_copy(sumbuf.at[slot], own_ref, own_sem)
                    keep.start()
                    for j in range(N_CHIPS - 1):
                        _remote(cs_send.at[j], cs_recv.at[j], sumbuf.at[0], others_ref.at[j], (*chips[j], c)).wait_recv()
                    to_owner(k - 1).wait_send()
                    keep.wait()
                    hook.finish(hins, houts, hscr)

    def chunk(k_pass, i, qc_ref):
        return (jnp.bitwise_xor(qc_ref[0], N_CHIPS - 1 - k_pass), i, 0)

    row = pl.BlockSpec((tm, D_MODEL), lambda k_pass, i, qc_ref: (i, 0))
    act = pl.BlockSpec((None, tm, fj), chunk)
    res = pl.pallas_call(
        body, name="ffn_wgrad_reduce",
        grid_spec=pltpu.PrefetchScalarGridSpec(
            num_scalar_prefetch=1, grid=(N_CHIPS, ni),
            in_specs=[row, row, act, act, act] + [ANY] * n_hin,
            out_specs=[ANY, ANY] + [ANY] * n_hout,
            scratch_shapes=[pltpu.VMEM((3 * fj, D_MODEL), F32), pltpu.VMEM((hrows, D_MODEL), BF16),
                            pltpu.VMEM((2, hrows, D_MODEL), BF16), pltpu.VMEM((2, hrows, D_MODEL), BF16),
                            pltpu.SemaphoreType.DMA((N_CHIPS,)), pltpu.SemaphoreType.DMA((N_CHIPS,)),
                            pltpu.SemaphoreType.DMA((N_CHIPS - 1,)), pltpu.SemaphoreType.DMA((N_CHIPS - 1,)),
                            pltpu.SemaphoreType.DMA] + list(hook.scratch)),
        out_shape=[jax.ShapeDtypeStruct((hrows, D_MODEL), BF16),
                   jax.ShapeDtypeStruct((N_CHIPS - 1, hrows, D_MODEL), BF16)] + list(hook.out_shape),
        compiler_params=_cparams(2),
    )(qc_idx, n, df, dg4, du4, a4, *hook.inputs)
    return res[0], res[1], res[2:]


def _xty(x, y):
    t, k = x.shape
    n = y.shape[1]
    tm = _contract_tile(t)
    tn = n if n <= 1024 else (896 if n % 896 == 0 else 128)

    def body(x_ref, y_ref, o_ref):
        @pl.when(pl.program_id(1) == 0)
        def _():
            o_ref[...] = jnp.zeros_like(o_ref)

        o_ref[...] += _dg(x_ref[...], y_ref[...], TN)

    return pl.pallas_call(
        body, name="xty", grid=(n // tn, t // tm),
        in_specs=[pl.BlockSpec((tm, k), lambda j, i: (i, 0)), pl.BlockSpec((tm, tn), lambda j, i: (i, j))],
        out_specs=pl.BlockSpec((k, tn), lambda j, i: (0, j)),
        out_shape=jax.ShapeDtypeStruct((k, n), F32),
        compiler_params=_cparams(2),
    )(x, y)


def _rope_tables(t):
    pos = (jnp.arange(t, dtype=jnp.int32) - PAD).astype(F32)
    inv_freq = 1.0 / (ROPE_THETA ** (jnp.arange(0, SWA_HD, 2, dtype=F32) / SWA_HD))
    ang = pos[:, None] * inv_freq[None, :]
    cos = jnp.cos(ang)
    sin = jnp.sin(ang)
    return jnp.concatenate([cos, cos, cos, cos], axis=1), jnp.concatenate([-sin, sin, -sin, sin], axis=1)


def _rot_half(x, first_half):
    return jnp.where(first_half, pltpu.roll(x, 96, 1), pltpu.roll(x, 32, 1))


def _first_half_mask(rows):
    lane = lax.broadcasted_iota(jnp.int32, (rows, 128), 1)
    return (lane % 64) < 32


def _log_sigmoid(z):
    return jnp.minimum(z, 0.0) - jnp.log(1.0 + jnp.exp(-jnp.abs(z)))


def _mix_proj(h1, wmixpre, winp, wa2p, bap, cos, sin):
    t = h1.shape[0]
    tm = _row_tile(t)

    def body(h_ref, w_ref, win_ref, wa2_ref, ba_ref, cos_ref, sin_ref,
             n_ref, gq_ref, gk_ref, gv_ref, gg_ref, ga_ref, la_ref, sq_ref, sk_ref, sv_ref):
        y, _, _ = _rms(h_ref[...], w_ref[...])
        n = y.astype(BF16)
        n_ref[...] = n
        proj = _dot(n, win_ref[...])
        gq_ref[...] = proj[:, P_GQ:P_GK]
        gk_ref[...] = proj[:, P_GK:P_GV]
        gv_ref[...] = proj[:, P_GV:P_GG]
        gg_ref[...] = proj[:, P_GG:P_GA]
        ga = proj[:, P_GA:P_SQ]
        ga_ref[...] = ga
        z = _dot(ga.astype(BF16), wa2_ref[...]) + ba_ref[...]
        la_ref[...] = _log_sigmoid(z) * (1.0 / GLA_TAU)
        c = cos_ref[...]
        s = sin_ref[...]
        fh = _first_half_mask(tm)
        for k in range(4):
            x = proj[:, P_SQ + 128 * k:P_SQ + 128 * (k + 1)]
            sq_ref[:, 128 * k:128 * (k + 1)] = (x * c + _rot_half(x, fh) * s).astype(BF16)
        for k in range(2):
            x = proj[:, P_SK + 128 * k:P_SK + 128 * (k + 1)]
            sk_ref[:, 128 * k:128 * (k + 1)] = (x * c + _rot_half(x, fh) * s).astype(BF16)
        sv_ref[...] = proj[:, P_SV:P_END].astype(BF16)

    def row(w):
        return pl.BlockSpec((tm, w), lambda i: (i, 0))

    def rshape(w, dt):
        return jax.ShapeDtypeStruct((t, w), dt)

    return pl.pallas_call(
        body, name="mix_proj", grid=(t // tm,),
        in_specs=[row(D_MODEL), _full((1, D_MODEL)), _full((D_MODEL, P_END)), _full((128, GLA_KW)),
                  _full((1, GLA_KW)), row(128), row(128)],
        out_specs=[row(D_MODEL), row(256), row(256), row(512), row(512), row(128), row(256), row(512), row(256),
                   row(256)],
        out_shape=[rshape(D_MODEL, BF16), rshape(256, F32), rshape(256, F32), rshape(512, F32), rshape(512, F32),
                   rshape(128, F32), rshape(256, F32), rshape(512, BF16), rshape(256, BF16), rshape(256, BF16)],
        compiler_params=_cparams(1),
    )(h1, wmixpre, winp, wa2p, bap, cos, sin)


def _scan_rows(x, reverse=False):
    n = x.shape[0]
    row = lax.broadcasted_iota(jnp.int32, x.shape, 0)
    s = 1
    while s < n:
        if reverse:
            x = x + jnp.where(row < n - s, pltpu.roll(x, n - s, 0), 0.0)
        else:
            x = x + jnp.where(row >= s, pltpu.roll(x, s, 0), 0.0)
        s *= 2
    return x


def _gla_cumsum(la, tril_f):
    b = _scan_rows(la)
    row = lax.broadcasted_iota(jnp.int32, b.shape, 0)
    bm = jnp.sum(jnp.where(row == GLA_CHUNK // 2 - 1, b, 0.0), axis=0, keepdims=True)
    bl = jnp.sum(jnp.where(row == GLA_CHUNK - 1, b, 0.0), axis=0, keepdims=True)
    return b, bm, bl


def _gla_decays(la, tril_f):
    b, bm, bl = _gla_cumsum(la, tril_f)
    return jnp.exp(b - bm), jnp.exp(bm - b), jnp.exp(b), jnp.exp(bl - b), jnp.exp(bl)


def _gla_masks():
    c = GLA_CHUNK
    r = lax.broadcasted_iota(jnp.int32, (c, c), 0)
    col = lax.broadcasted_iota(jnp.int32, (c, c), 1)
    r4 = lax.broadcasted_iota(jnp.int32, (GLA_HEADS * c, c), 0) % c
    c4 = lax.broadcasted_iota(jnp.int32, (GLA_HEADS * c, c), 1)
    klane = lax.broadcasted_iota(jnp.int32, (c, GLA_KW), 1) // GLA_DK
    vlane = lax.broadcasted_iota(jnp.int32, (c, GLA_W), 1) // GLA_DV
    srow = lax.broadcasted_iota(jnp.int32, (GLA_W, GLA_KW), 0) // GLA_DV
    scol = lax.broadcasted_iota(jnp.int32, (GLA_W, GLA_KW), 1) // GLA_DK
    return dict(tril_f=(r >= col).astype(F32), triu_f=(r <= col).astype(F32), tril4=r4 >= c4,
                khead=[klane == h for h in range(GLA_HEADS)], vhead=[vlane == h for h in range(GLA_HEADS)],
                diag=srow == scol)


def _stack_heads(x, head_masks):
    return jnp.concatenate([jnp.where(m, x, 0.0) for m in head_masks], axis=0)


def _gla_fwd(gq, gk, gv, la):
    t = gq.shape[0]
    rg = _seq_tile(t)
    nb = t // rg
    ncb = rg // GLA_CHUNK
    c = GLA_CHUNK

    def body(q_ref, k_ref, v_ref, la_ref, o_ref, ss_ref, st_ref):
        @pl.when(pl.program_id(0) == 0)
        def _():
            st_ref[...] = jnp.zeros_like(st_ref)

        mk = _gla_masks()
        st = st_ref[...]
        for ch in range(ncb):
            rows = slice(ch * c, (ch + 1) * c)
            eq, ek, eb, ekl, ebl = _gla_decays(la_ref[rows, :], mk["tril_f"])
            qs = q_ref[rows, :] * (GLA_DK ** -0.5)
            k = k_ref[rows, :]
            v = v_ref[rows, :].astype(BF16)
            ss_ref[ch] = st
            q4 = _stack_heads(qs * eq, mk["khead"]).astype(BF16)
            a4 = jnp.where(mk["tril4"], _dg(q4, (k * ek).astype(BF16), NT), 0.0).astype(BF16)
            r4 = _dot(a4, v)
            intra = jnp.concatenate([r4[h * c:(h + 1) * c, GLA_DV * h:GLA_DV * (h + 1)] for h in range(GLA_HEADS)],
                                    axis=1)
            o_ref[rows, :] = intra + _dg((qs * eb).astype(BF16), st.astype(BF16), NT)
            st = st * ebl + jnp.where(mk["diag"], _dg(v, (k * ekl).astype(BF16), TN), 0.0)
        st_ref[...] = st

    def row(w):
        return pl.BlockSpec((rg, w), lambda i: (i, 0))

    return pl.pallas_call(
        body, name="gla_fwd", grid=(nb,),
        in_specs=[row(256), row(256), row(512), row(256)],
        out_specs=[row(512), pl.BlockSpec((ncb, GLA_W, GLA_KW), lambda i: (i, 0, 0))],
        out_shape=[jax.ShapeDtypeStruct((t, GLA_W), F32), jax.ShapeDtypeStruct((nb * ncb, GLA_W, GLA_KW), F32)],
        scratch_shapes=[pltpu.VMEM((GLA_W, GLA_KW), F32)],
        compiler_params=_cparams(1),
    )(gq, gk, gv, la)


def _gla_bwd(gq, gk, gv, la, ss, do):
    t = gq.shape[0]
    rg = _seq_tile(t)
    nb = t // rg
    ncb = rg // GLA_CHUNK
    c = GLA_CHUNK

    def body(q_ref, k_ref, v_ref, la_ref, ss_ref, do_ref, dq_ref, dk_ref, dv_ref, dla_ref, dst_ref):
        @pl.when(pl.program_id(0) == 0)
        def _():
            dst_ref[...] = jnp.zeros_like(dst_ref)

        mk = _gla_masks()
        last_row = lax.broadcasted_iota(jnp.int32, (c, GLA_KW), 0) == c - 1
        scale = GLA_DK ** -0.5
        dstn = dst_ref[...]
        for ch in reversed(range(ncb)):
            rows = slice(ch * c, (ch + 1) * c)
            eq, ek, eb, ekl, ebl = _gla_decays(la_ref[rows, :], mk["tril_f"])
            qs = q_ref[rows, :] * scale
            k = k_ref[rows, :]
            qt, kt, qh, kh = qs * eq, k * ek, qs * eb, k * ekl
            ktb, khb, qhb = kt.astype(BF16), kh.astype(BF16), qh.astype(BF16)
            v = v_ref[rows, :].astype(BF16)
            do_f = do_ref[rows, :]
            dob = do_f.astype(BF16)
            st = ss_ref[ch]
            stb = st.astype(BF16)
            dstb = dstn.astype(BF16)
            q4 = _stack_heads(qt, mk["khead"]).astype(BF16)
            do4 = _stack_heads(do_f, mk["vhead"]).astype(BF16)
            a4 = jnp.where(mk["tril4"], _dg(q4, ktb, NT), 0.0).astype(BF16)
            da4 = jnp.where(mk["tril4"], _dg(do4, v, NT), 0.0).astype(BF16)
            dv_ref[rows, :] = _dg(a4, do4, TN) + _dg(khb, dstb, NT)
            dq4 = _dot(da4, ktb)
            dqt = jnp.zeros((c, GLA_KW), F32)
            for h in range(GLA_HEADS):
                dqt = dqt + jnp.where(mk["khead"][h], dq4[h * c:(h + 1) * c], 0.0)
            dkt = _dg(da4, q4, TN)
            dqh = _dot(dob, stb)
            dkh = _dot(v, dstb)
            dbl = jnp.sum(dstn * st, axis=0, keepdims=True)
            dstn = dstn * ebl + jnp.where(mk["diag"], _dg(dob, qhb, TN), 0.0)
            dq_ref[rows, :] = scale * (dqt * eq + dqh * eb)
            dk_ref[rows, :] = dkt * ek + dkh * ekl
            dkk = dkh * kh
            db = dqt * qt - dkt * kt + dqh * qh - dkk
            db = db + jnp.where(last_row, jnp.sum(dkk, axis=0, keepdims=True) + ebl * dbl, 0.0)
            dla_ref[rows, :] = _scan_rows(db, reverse=True)
        dst_ref[...] = dstn

    def row(w):
        return pl.BlockSpec((rg, w), lambda i: (nb - 1 - i, 0))

    def rshape(w):
        return jax.ShapeDtypeStruct((t, w), F32)

    return pl.pallas_call(
        body, name="gla_bwd", grid=(nb,),
        in_specs=[row(256), row(256), row(512), row(256),
                  pl.BlockSpec((ncb, GLA_W, GLA_KW), lambda i: (nb - 1 - i, 0, 0)), row(512)],
        out_specs=[row(256), row(256), row(512), row(256)],
        out_shape=[rshape(256), rshape(256), rshape(512), rshape(256)],
        scratch_shapes=[pltpu.VMEM((GLA_W, GLA_KW), F32)],
        compiler_params=_cparams(1),
    )(gq, gk, gv, la, ss, do)


SWA_G = SWA_QH // SWA_KVH


def _swa_bias():
    n = jnp.arange(3, dtype=jnp.int32)[:, None, None]
    r = (jnp.arange(SWA_G * BLK, dtype=jnp.int32) % BLK)[None, :, None]
    c = jnp.arange(3 * BLK, dtype=jnp.int32)[None, None, :]
    seg = c // BLK
    cc = c % BLK
    qpos = n * BLK + r - PAD
    kpos = jnp.where(seg == 0, (n - 1) * BLK, jnp.where(seg == 1, n * BLK, 0)) + cc - PAD
    band = (seg < 2) & (kpos >= N_META) & (kpos <= qpos) & (qpos - kpos < WINDOW)
    meta = (seg == 2) & (kpos >= 0) & (kpos < N_META) & (kpos <= qpos)
    return jnp.where(band | meta, 0.0, NEG_INF).astype(F32)


def _swa_stack(ref, rows, kh, lo, dtype):
    parts = []
    for g in range(2):
        pair = ref[rows, 128 * (2 * kh + g):128 * (2 * kh + g + 1)]
        zero = jnp.zeros_like(pair)
        parts += [jnp.where(lo, pair, zero), jnp.where(lo, zero, pair)]
    return jnp.concatenate(parts, axis=0).astype(dtype)


def _swa_unstack(x4, lo):
    return [jnp.where(lo, x4[2 * g * BLK:(2 * g + 1) * BLK], x4[(2 * g + 1) * BLK:(2 * g + 2) * BLK])
            for g in range(2)]


def _swa_sink_col(sink_ref, kh):
    blk = lax.broadcasted_iota(jnp.int32, (SWA_G * BLK, 1), 0) // BLK
    col = jnp.full((SWA_G * BLK, 1), sink_ref[SWA_G * kh + SWA_G - 1], F32)
    for e in reversed(range(SWA_G - 1)):
        col = jnp.where(blk == e, sink_ref[SWA_G * kh + e], col)
    return col


def _swa_softmax(qk, bias, sink):
    s = qk * (SWA_HD ** -0.5) + bias
    m = jnp.maximum(jnp.max(s, axis=-1, keepdims=True), sink)
    p = jnp.exp(s - m)
    es = jnp.exp(sink - m)
    inv = 1.0 / (jnp.sum(p, axis=-1, keepdims=True) + es)
    return p * inv, es * inv


def _swa_keys(prev_ref, cur_ref, first_ref, b, ls):
    before = prev_ref[:, ls] if b == 0 else cur_ref[(b - 1) * BLK:b * BLK, ls]
    return jnp.concatenate([before, cur_ref[b * BLK:(b + 1) * BLK, ls], first_ref[:, ls]], axis=0)


def _swa_specs(rs, ns):
    bps = rs // BLK
    cur = lambda w: pl.BlockSpec((rs, w), lambda i: (jnp.minimum(i, ns - 1), 0))
    prev = lambda w: pl.BlockSpec((BLK, w), lambda i: (jnp.maximum(jnp.minimum(i, ns - 1) * bps - 1, 0), 0))
    first = lambda w: pl.BlockSpec((BLK, w), lambda i: (0, 0))
    return cur, prev, first


def _swa_fwd(sinks, sq, sk, sv):
    t = sq.shape[0]
    rs = _seq_tile(t)
    bps, ns = rs // BLK, t // rs

    def body(sink_ref, bias_ref, q_ref, kp_ref, kc_ref, km_ref, vp_ref, vc_ref, vm_ref, o_ref):
        i = pl.program_id(0)
        lo = lax.broadcasted_iota(jnp.int32, (BLK, 128), 1) < 64
        sink_cols = [_swa_sink_col(sink_ref, kh) for kh in range(SWA_KVH)]
        chains = [(b, kh) for b in range(bps) for kh in range(SWA_KVH)]
        scores = []
        for b, kh in chains:
            ls = slice(128 * kh, 128 * (kh + 1))
            q4 = _swa_stack(q_ref, slice(b * BLK, (b + 1) * BLK), kh, lo, BF16)
            scores.append(_dg(q4, _swa_keys(kp_ref, kc_ref, km_ref, b, ls), NT))
        probs = []
        for (b, kh), s in zip(chains, scores):
            p, _ = _swa_softmax(s, bias_ref[jnp.minimum(i * bps + b, 2)], sink_cols[kh])
            probs.append(p.astype(BF16))
        for (b, kh), p in zip(chains, probs):
            ls = slice(128 * kh, 128 * (kh + 1))
            rows = slice(b * BLK, (b + 1) * BLK)
            for g, pair in enumerate(_swa_unstack(_dot(p, _swa_keys(vp_ref, vc_ref, vm_ref, b, ls)), lo)):
                o_ref[rows, 128 * (2 * kh + g):128 * (2 * kh + g + 1)] = pair

    cur, prev, first = _swa_specs(rs, ns)
    bias = _swa_bias()
    return pl.pallas_call(
        body, name="swa_fwd", grid=(ns,),
        in_specs=[pl.BlockSpec(memory_space=pltpu.SMEM), _full(bias.shape), cur(512), prev(256), cur(256), first(256),
                  prev(256), cur(256), first(256)],
        out_specs=cur(512),
        out_shape=jax.ShapeDtypeStruct((t, SWA_W), F32),
        compiler_params=_cparams(1),
    )(sinks, bias, sq, sk, sk, sk, sv, sv, sv)


def _swa_bwd(sinks, sq, sk, sv, o, do, hook=None):
    t = sq.shape[0]
    rs = _seq_tile(t)
    bps, ns = rs // BLK, t // rs

    def body(sink_ref, bias_ref, q_ref, kp_ref, kc_ref, km_ref, vp_ref, vc_ref, vm_ref, o_ref, do_ref,
             dq_ref, dk_ref, dv_ref, dkm_ref, dvm_ref, dsink_ref, pk_ref, pv_ref):
        i = pl.program_id(0)

        @pl.when(i == 0)
        def _():
            pk_ref[...] = jnp.zeros_like(pk_ref)
            pv_ref[...] = jnp.zeros_like(pv_ref)
            dkm_ref[...] = jnp.zeros_like(dkm_ref)
            dvm_ref[...] = jnp.zeros_like(dvm_ref)
            dsink_ref[...] = jnp.zeros_like(dsink_ref)

        @pl.when(i == ns)
        def _():
            dk_ref[...] = pk_ref[...]
            dv_ref[...] = pv_ref[...]

        @pl.when(i < ns)
        def _():
            lo = lax.broadcasted_iota(jnp.int32, (BLK, 128), 1) < 64
            scale = SWA_HD ** -0.5
            sink_cols = [_swa_sink_col(sink_ref, kh) for kh in range(SWA_KVH)]
            parts_k = [[None] * SWA_KVH for _ in range(bps)]
            parts_v = [[None] * SWA_KVH for _ in range(bps)]
            dsinks = [jnp.zeros((1, 1), F32) for _ in range(SWA_QH)]
            chains = [(b, kh) for b in range(bps) for kh in range(SWA_KVH)]
            lanes = lambda kh: slice(128 * kh, 128 * (kh + 1))
            block = lambda b: slice(b * BLK, (b + 1) * BLK)
            q4s = [_swa_stack(q_ref, block(b), kh, lo, BF16) for b, kh in chains]
            scores = [_dg(q4, _swa_keys(kp_ref, kc_ref, km_ref, b, lanes(kh)), NT)
                      for (b, kh), q4 in zip(chains, q4s)]
            do4s = [_swa_stack(do_ref, block(b), kh, lo, F32) for b, kh in chains]
            do4bs = [d.astype(BF16) for d in do4s]
            dps = [_dg(d, _swa_keys(vp_ref, vc_ref, vm_ref, b, lanes(kh)), NT) for (b, kh), d in zip(chains, do4bs)]
            pbs, dss = [], []
            for n_chain, (b, kh) in enumerate(chains):
                p, psink = _swa_softmax(scores[n_chain], bias_ref[jnp.minimum(i * bps + b, 2)], sink_cols[kh])
                delta = jnp.sum(do4s[n_chain] * _swa_stack(o_ref, block(b), kh, lo, F32), axis=-1, keepdims=True)
                dss.append((p * (dps[n_chain] - delta) * scale).astype(BF16))
                pbs.append(p.astype(BF16))
                dsk = psink * delta
                for e in range(SWA_G):
                    h = SWA_G * kh + e
                    dsinks[h] = dsinks[h] - jnp.sum(dsk[e * BLK:(e + 1) * BLK], axis=0, keepdims=True)
            for n_chain, (b, kh) in enumerate(chains):
                kall = _swa_keys(kp_ref, kc_ref, km_ref, b, lanes(kh))
                for g, pair in enumerate(_swa_unstack(_dot(dss[n_chain], kall), lo)):
                    dq_ref[block(b), 128 * (2 * kh + g):128 * (2 * kh + g + 1)] = pair
                parts_k[b][kh] = _dg(dss[n_chain], q4s[n_chain], TN)
                parts_v[b][kh] = _dg(pbs[n_chain], do4bs[n_chain], TN)
            last = slice(rs - BLK, rs)
            for parts, out_ref, pend_ref, meta_ref in ((parts_k, dk_ref, pk_ref, dkm_ref),
                                                       (parts_v, dv_ref, pv_ref, dvm_ref)):
                for kh in range(SWA_KVH):
                    ls = slice(128 * kh, 128 * (kh + 1))
                    if bps > 1:
                        out_ref[0:rs - BLK, ls] = pend_ref[0:rs - BLK, ls]
                    out_ref[last, ls] = pend_ref[last, ls] + parts[0][kh][0:BLK]
                    meta = parts[0][kh][2 * BLK:3 * BLK]
                    for b in range(bps):
                        own = parts[b][kh][BLK:2 * BLK]
                        if b + 1 < bps:
                            own = own + parts[b + 1][kh][0:BLK]
                            meta = meta + parts[b + 1][kh][2 * BLK:3 * BLK]
                        pend_ref[b * BLK:(b + 1) * BLK, ls] = own
                    meta_ref[:, ls] += meta
            for h in range(SWA_QH):
                dsink_ref[h:h + 1, :] += jnp.broadcast_to(dsinks[h], (1, 128))

    cur, prev, first = _swa_specs(rs, ns)
    late = lambda w: pl.BlockSpec((rs, w), lambda i: (jnp.maximum(i - 1, 0), 0))
    bias = _swa_bias()
    return _pallas(
        body, name="swa_bwd", grid=(ns + 1,),
        in_specs=[pl.BlockSpec(memory_space=pltpu.SMEM), _full(bias.shape), cur(512), prev(256), cur(256), first(256),
                  prev(256), cur(256), first(256), cur(512), cur(512)],
        out_specs=[cur(512), late(256), late(256), first(256), first(256), _full((SWA_QH, 128))],
        out_shape=[jax.ShapeDtypeStruct((t, SWA_W), F32), jax.ShapeDtypeStruct((t, 256), F32),
                   jax.ShapeDtypeStruct((t, 256), F32), jax.ShapeDtypeStruct((BLK, 256), F32),
                   jax.ShapeDtypeStruct((BLK, 256), F32), jax.ShapeDtypeStruct((SWA_QH, 128), F32)],
        scratch_shapes=[pltpu.VMEM((rs, 256), F32), pltpu.VMEM((rs, 256), F32)],
        args=(sinks, bias, sq, sk, sk, sk, sv, sv, sv, o, do), hook=hook)


def _mix_out(h1, ogla, gg, oswa, wgn, wsn, wout, wpost):
    t = h1.shape[0]
    tm = _row_tile(t)

    def body(h_ref, og_ref, gg_ref, os_ref, wgn_ref, wsn_ref, wout_ref, wpost_ref, h2_ref, cat_ref, m_ref):
        parts = []
        for h in range(GLA_HEADS):
            ls = slice(GLA_DV * h, GLA_DV * (h + 1))
            y, _, _ = _rms(og_ref[:, ls], wgn_ref[...])
            g = gg_ref[:, ls]
            parts.append(y * (g * _sigmoid(g)))
        ys, _, _ = _rms(os_ref[...], wsn_ref[...])
        cat = jnp.concatenate(parts + [ys], axis=1).astype(BF16)
        cat_ref[...] = cat
        m = _dot(cat, wout_ref[...])
        m_ref[...] = m
        y, _, _ = _rms(m, wpost_ref[...])
        h2_ref[...] = h_ref[...] + y

    def row(w):
        return pl.BlockSpec((tm, w), lambda i: (i, 0))

    return pl.pallas_call(
        body, name="mix_out", grid=(t // tm,),
        in_specs=[row(D_MODEL), row(512), row(512), row(512), _full((1, GLA_DV)), _full((1, SWA_W)),
                  _full((D_MODEL, D_MODEL)), _full((1, D_MODEL))],
        out_specs=[row(D_MODEL), row(D_MODEL), row(D_MODEL)],
        out_shape=[jax.ShapeDtypeStruct((t, D_MODEL), F32), jax.ShapeDtypeStruct((t, D_MODEL), BF16),
                   jax.ShapeDtypeStruct((t, D_MODEL), F32)],
        compiler_params=_cparams(1),
    )(h1, ogla, gg, oswa, wgn, wsn, wout, wpost)


def _mix_out_bwd(dh2, m, ogla, gg, oswa, wgn, wsn, wout, wpost, hook=None):
    t = dh2.shape[0]
    tm = _row_tile(t)

    def body(dh_ref, m_ref, og_ref, gg_ref, os_ref, wgn_ref, wsn_ref, wout_ref, wpost_ref,
             dog_ref, dgg_ref, dos_ref, dm_ref, dwpost_ref, dwgn_ref, dwsn_ref):
        @pl.when(pl.program_id(0) == 0)
        def _():
            dwpost_ref[...] = jnp.zeros_like(dwpost_ref)
            dwgn_ref[...] = jnp.zeros_like(dwgn_ref)
            dwsn_ref[...] = jnp.zeros_like(dwsn_ref)

        wpost = wpost_ref[...]
        _, mh, r = _rms(m_ref[...], wpost)
        dm, dw = _rms_bwd(mh, r, wpost, dh_ref[...])
        dwpost_ref[...] += dw
        dmb = dm.astype(BF16)
        dm_ref[...] = dmb
        dcat = _dg(dmb, wout_ref[...], NT)
        wgn = wgn_ref[...]
        for h in range(GLA_HEADS):
            ls = slice(GLA_DV * h, GLA_DV * (h + 1))
            dog = dcat[:, ls]
            g = gg_ref[:, ls]
            sg = _sigmoid(g)
            y, xh, r = _rms(og_ref[:, ls], wgn)
            dgg_ref[:, ls] = dog * y * (sg * (1.0 + g * (1.0 - sg)))
            dx, dw = _rms_bwd(xh, r, wgn, dog * (g * sg))
            dog_ref[:, ls] = dx
            dwgn_ref[...] += dw
        wsn = wsn_ref[...]
        _, xh, r = _rms(os_ref[...], wsn)
        dx, dw = _rms_bwd(xh, r, wsn, dcat[:, GLA_W:])
        dos_ref[...] = dx
        dwsn_ref[...] += dw

    def row(w):
        return pl.BlockSpec((tm, w), lambda i: (i, 0))

    def rshape(w, dt=F32):
        return jax.ShapeDtypeStruct((t, w), dt)

    return _pallas(
        body, name="mix_out_bwd", grid=(t // tm,),
        in_specs=[row(D_MODEL), row(D_MODEL), row(512), row(512), row(512), _full((1, GLA_DV)), _full((1, SWA_W)),
                  _full((D_MODEL, D_MODEL)), _full((1, D_MODEL))],
        out_specs=[row(512), row(512), row(512), row(D_MODEL), _full((1, D_MODEL)), _full((1, GLA_DV)),
                   _full((1, SWA_W))],
        out_shape=[rshape(512), rshape(512), rshape(512), rshape(D_MODEL, BF16),
                   jax.ShapeDtypeStruct((1, D_MODEL), F32), jax.ShapeDtypeStruct((1, GLA_DV), F32),
                   jax.ShapeDtypeStruct((1, SWA_W), F32)],
        args=(dh2, m, ogla, gg, oswa, wgn, wsn, wout, wpost), hook=hook)


def _mix_in_bwd(dh2, h1, wmixpre, winp, wa2p, bap, cos, sin, ga, dgq, dgk, dgv, dgg, dla, dsq, dsk, dsv, dkm, dvm):
    t = h1.shape[0]
    tm = _row_tile(t)

    def body(dh2_ref, h_ref, w_ref, win_ref, wa2_ref, ba_ref, cos_ref, sin_ref, ga_ref, dgq_ref, dgk_ref, dgv_ref,
             dgg_ref, dla_ref, dsq_ref, dsk_ref, dsv_ref, dkm_ref, dvm_ref,
             dh1_ref, dproj_ref, dw_ref, dwa2_ref, dba_ref):
        i = pl.program_id(0)

        @pl.when(i == 0)
        def _():
            dw_ref[...] = jnp.zeros_like(dw_ref)
            dwa2_ref[...] = jnp.zeros_like(dwa2_ref)
            dba_ref[...] = jnp.zeros_like(dba_ref)

        first = (i == 0).astype(F32)
        c = cos_ref[...]
        s = -sin_ref[...]
        fh = _first_half_mask(tm)
        dproj_ref[:, P_GQ:P_GK] = dgq_ref[...].astype(BF16)
        dproj_ref[:, P_GK:P_GV] = dgk_ref[...].astype(BF16)
        dproj_ref[:, P_GV:P_GG] = dgv_ref[...].astype(BF16)
        dproj_ref[:, P_GG:P_GA] = dgg_ref[...].astype(BF16)
        gab = ga_ref[...].astype(BF16)
        z = _dot(gab, wa2_ref[...]) + ba_ref[...]
        row_id = i * tm + lax.broadcasted_iota(jnp.int32, (tm, 1), 0)
        dz = jnp.where(row_id >= PAD, dla_ref[...] * (1.0 / GLA_TAU) * (1.0 - _sigmoid(z)), 0.0)
        dzb = dz.astype(BF16)
        dba_ref[...] += jnp.sum(dz, axis=0, keepdims=True)
        dwa2_ref[...] += _dg(gab, dzb, TN)
        dproj_ref[:, P_GA:P_SQ] = _dg(dzb, wa2_ref[...], NT).astype(BF16)
        for k in range(4):
            dy = dsq_ref[:, 128 * k:128 * (k + 1)]
            dproj_ref[:, P_SQ + 128 * k:P_SQ + 128 * (k + 1)] = (dy * c + _rot_half(dy, fh) * s).astype(BF16)
        for k in range(2):
            ls = slice(128 * k, 128 * (k + 1))
            dy = dsk_ref[:, ls]
            dy = jnp.concatenate([dy[:BLK] + first * dkm_ref[:, ls], dy[BLK:]], axis=0) if tm > BLK else (
                dy + first * dkm_ref[:, ls])
            dproj_ref[:, P_SK + 128 * k:P_SK + 128 * (k + 1)] = (dy * c + _rot_half(dy, fh) * s).astype(BF16)
            dv = dsv_ref[:, ls]
            dv = jnp.concatenate([dv[:BLK] + first * dvm_ref[:, ls], dv[BLK:]], axis=0) if tm > BLK else (
                dv + first * dvm_ref[:, ls])
            dproj_ref[:, P_SV + 128 * k:P_SV + 128 * (k + 1)] = dv.astype(BF16)
        dn = _dg(dproj_ref[...], win_ref[...], NT)
        w = w_ref[...]
        _, hh, r = _rms(h_ref[...], w)
        dx, dw = _rms_bwd(hh, r, w, dn)
        dw_ref[...] += dw
        dh1_ref[...] = dh2_ref[...] + dx

    def row(w):
        return pl.BlockSpec((tm, w), lambda i: (i, 0))

    return pl.pallas_call(
        body, name="mix_in_bwd", grid=(t // tm,),
        in_specs=[row(D_MODEL), row(D_MODEL), _full((1, D_MODEL)), _full((D_MODEL, P_END)), _full((128, GLA_KW)),
                  _full((1, GLA_KW)), row(128), row(128), row(128), row(256), row(256), row(512), row(512), row(256),
                  row(512), row(256), row(256), _full((BLK, 256)), _full((BLK, 256))],
        out_specs=[row(D_MODEL), row(P_END), _full((1, D_MODEL)), _full((128, GLA_KW)), _full((1, GLA_KW))],
        out_shape=[jax.ShapeDtypeStruct((t, D_MODEL), F32), jax.ShapeDtypeStruct((t, P_END), BF16),
                   jax.ShapeDtypeStruct((1, D_MODEL), F32), jax.ShapeDtypeStruct((128, GLA_KW), F32),
                   jax.ShapeDtypeStruct((1, GLA_KW), F32)],
        compiler_params=_cparams(1),
    )(dh2, h1, wmixpre, winp, wa2p, bap, cos, sin, ga, dgq, dgk, dgv, dgg, dla, dsq, dsk, dsv, dkm, dvm)


def _adamw_update(w, g, m, v):
    m = ADAM_B1 * m + (1.0 - ADAM_B1) * g
    v = ADAM_B2 * v + (1.0 - ADAM_B2) * (g * g)
    m_hat = m / (1.0 - ADAM_B1 ** ADAM_STEP)
    v_hat = v / (1.0 - ADAM_B2 ** ADAM_STEP)
    return -ADAM_LR * (m_hat / (jnp.sqrt(v_hat) + ADAM_EPS) + ADAM_WD * w), m, v


def _adamw_halves(w, g_mine, g_other, m, v, c_idx, row0=0):
    r, c = w.shape
    h = g_mine.shape[0]
    tr = _div_tile(math.gcd(r, h))
    nth = h // tr
    t0 = row0 // tr
    assert t0 * tr == row0

    def body(c_ref, w_ref, gm_ref, go_ref, m_ref, v_ref, g_ref, d_ref, nm_ref, nv_ref):
        hh = (t0 + pl.program_id(0)) // nth
        g = jnp.where(hh == c_ref[0], gm_ref[...], go_ref[...])
        g_ref[...] = g
        d_ref[...], nm_ref[...], nv_ref[...] = _adamw_update(w_ref[...], g, m_ref[...], v_ref[...])

    spec = pl.BlockSpec((tr, c), lambda i, c_ref: (i, 0))

    def gspec(is_mine):
        def index(i, c_ref):
            used = ((t0 + i) // nth == c_ref[0]) == is_mine
            return (jnp.where(used, (t0 + i) % nth, 0), 0)
        return pl.BlockSpec((tr, c), index)

    shape = jax.ShapeDtypeStruct((r, c), F32)
    return pl.pallas_call(
        body, name="adamw_halves",
        grid_spec=pltpu.PrefetchScalarGridSpec(
            num_scalar_prefetch=1, grid=(r // tr,), in_specs=[spec, gspec(True), gspec(False), spec, spec],
            out_specs=[spec] * 4),
        out_shape=[shape] * 4, compiler_params=_cparams(1),
    )(c_idx, w, g_mine, g_other, m, v)


def _place():
    x, y, c = lax.axis_index("x"), lax.axis_index("y"), lax.axis_index("c")
    chips = [(1 - x, y), (x, 1 - y), (1 - x, 1 - y)]
    return x, y, c, chips


def _remote(send_sem, recv_sem, src, dst, to):
    return pltpu.make_async_remote_copy(src_ref=src, dst_ref=dst, send_sem=send_sem, recv_sem=recv_sem,
                                        device_id=to, device_id_type=MESH)


def _half(ref_rows, c):
    h = ref_rows // 2
    return pl.ds(pl.multiple_of(c * h, 8), h)


def _own_slot(shard, q):
    return lax.dynamic_update_slice(jnp.zeros((N_CHIPS,) + shard.shape, shard.dtype), shard[None], (q, 0, 0))


class _GatherChips:
    has_mid = True

    def __init__(self, bufs):
        n = len(bufs)
        self.inputs = list(bufs)
        self.out_shape = [jax.ShapeDtypeStruct(b.shape, b.dtype) for b in bufs]
        self.aliases = [(t, t) for t in range(n)]
        self.scratch = [pltpu.SemaphoreType.DMA((n, 6)), pltpu.SemaphoreType.DMA((n, 6))]

    def start(self, ins, outs, scr):
        send, recv = scr
        x, y, c, chips = _place()
        q = 2 * x + y
        for t, (i_ref, o_ref) in enumerate(zip(ins, outs)):
            rows = _half(i_ref.shape[1], c)
            for j, (cx, cy) in enumerate(chips):
                _remote(send.at[t, j], recv.at[t, j], i_ref.at[q, rows], o_ref.at[q, rows], (cx, cy, c)).start()

    def mid(self, ins, outs, scr):
        send, recv = scr
        x, y, c, chips = _place()
        for t, o_ref in enumerate(outs):
            rows = _half(o_ref.shape[1], c)
            for j, (cx, cy) in enumerate(chips):
                slot = o_ref.at[2 * cx + cy, rows]
                _remote(send.at[t, j], recv.at[t, j], slot, slot, (cx, cy, c)).wait_recv()
                _remote(send.at[t, 3 + j], recv.at[t, 3 + j], slot, slot, (x, y, 1 - c)).start()

    def finish(self, ins, outs, scr):
        send, recv = scr
        x, y, c, chips = _place()
        for t, o_ref in enumerate(outs):
            mine, other = _half(o_ref.shape[1], c), _half(o_ref.shape[1], 1 - c)
            for j, (cx, cy) in enumerate(chips):
                slot = o_ref.at[2 * cx + cy, other]
                _remote(send.at[t, 3 + j], recv.at[t, 3 + j], slot, slot, (x, y, 1 - c)).wait_recv()
            for j, (cx, cy) in enumerate(chips):
                sent = o_ref.at[2 * cx + cy, mine]
                _remote(send.at[t, j], recv.at[t, j], sent, sent, (cx, cy, c)).wait_send()
                _remote(send.at[t, 3 + j], recv.at[t, 3 + j], sent, sent, (x, y, 1 - c)).wait_send()


class _PairExchange:
    has_mid = False
    aliases = ()

    def __init__(self, arrs):
        n = len(arrs)
        self.inputs = list(arrs)
        self.out_shape = [jax.ShapeDtypeStruct((a.shape[0], a.shape[1] // 2, a.shape[2]), a.dtype) for a in arrs]
        self.scratch = [pltpu.SemaphoreType.DMA((n,)), pltpu.SemaphoreType.DMA((n,))]

    def _copies(self, ins, outs, scr):
        send, recv = scr
        x, y, c, _ = _place()
        return [_remote(send.at[t], recv.at[t], i_ref.at[:, _half(i_ref.shape[1], 1 - c)], o_ref, (x, y, 1 - c))
                for t, (i_ref, o_ref) in enumerate(zip(ins, outs))]

    def start(self, ins, outs, scr):
        for cp in self._copies(ins, outs, scr):
            cp.start()

    def finish(self, ins, outs, scr):
        for cp in self._copies(ins, outs, scr):
            cp.wait()


class _ChipScatter:
    has_mid = False
    aliases = ()

    def __init__(self, arrs):
        n = len(arrs)
        self.inputs = list(arrs)
        self.out_shape = [jax.ShapeDtypeStruct((3,) + a.shape[1:], a.dtype) for a in arrs]
        self.scratch = [pltpu.SemaphoreType.DMA((n, 3)), pltpu.SemaphoreType.DMA((n, 3))]

    def _copies(self, ins, outs, scr):
        send, recv = scr
        x, y, c, chips = _place()
        return [_remote(send.at[t, j], recv.at[t, j], i_ref.at[2 * cx + cy], o_ref.at[j], (cx, cy, c))
                for t, (i_ref, o_ref) in enumerate(zip(ins, outs)) for j, (cx, cy) in enumerate(chips)]

    def start(self, ins, outs, scr):
        for cp in self._copies(ins, outs, scr):
            cp.start()

    def finish(self, ins, outs, scr):
        for cp in self._copies(ins, outs, scr):
            cp.wait()


class _PairShare:
    has_mid = False
    aliases = ()

    def __init__(self, arrs):
        n = len(arrs)
        self.inputs = list(arrs)
        self.out_shape = [jax.ShapeDtypeStruct(a.shape, a.dtype) for a in arrs]
        self.scratch = [pltpu.SemaphoreType.DMA((n,)), pltpu.SemaphoreType.DMA((n,))]

    def _copies(self, ins, outs, scr):
        send, recv = scr
        x, y, c, _ = _place()
        return [_remote(send.at[t], recv.at[t], i_ref, o_ref, (x, y, 1 - c))
                for t, (i_ref, o_ref) in enumerate(zip(ins, outs))]

    def start(self, ins, outs, scr):
        for cp in self._copies(ins, outs, scr):
            cp.start()

    def finish(self, ins, outs, scr):
        for cp in self._copies(ins, outs, scr):
            cp.wait()


def _comm_call(hook, name):
    n_in, n_out = len(hook.inputs), len(hook.out_shape)

    def body(*refs):
        ins, outs, scr = refs[:n_in], refs[n_in:n_in + n_out], refs[n_in + n_out:]
        hook.start(ins, outs, scr)
        if hook.has_mid:
            hook.mid(ins, outs, scr)
        hook.finish(ins, outs, scr)

    return pl.pallas_call(body, name=name, in_specs=[ANY] * n_in, out_specs=[ANY] * n_out,
                          out_shape=list(hook.out_shape), scratch_shapes=list(hook.scratch),
                          input_output_aliases=dict(hook.aliases))(*hook.inputs)


class _GatherDevices:
    has_mid = True
    aliases = ()

    def __init__(self, vecs):
        n = len(vecs)
        self.inputs = list(vecs)
        self.out_shape = [jax.ShapeDtypeStruct((N_DEV,) + v.shape, v.dtype) for v in vecs]
        self.scratch = [pltpu.SemaphoreType.DMA((n, 7)), pltpu.SemaphoreType.DMA((n, 7)),
                        pltpu.SemaphoreType.DMA((n,))]

    @staticmethod
    def _copy(scr, t, k, out_ref, block, to, src=None):
        send, recv, _ = scr
        px, py, pc = block
        slot = out_ref.at[4 * px + 2 * py + pc]
        return _remote(send.at[t, k], recv.at[t, k], slot if src is None else src, slot, to)

    def start(self, ins, outs, scr):
        x, y, c, chips = _place()
        me = (x, y, c)
        for t, (x_ref, out_ref) in enumerate(zip(ins, outs)):
            pltpu.make_async_copy(x_ref, out_ref.at[4 * x + 2 * y + c], scr[2].at[t]).start()
            self._copy(scr, t, 0, out_ref, me, (x, y, 1 - c), src=x_ref).start()
            for j, chip in enumerate(chips):
                self._copy(scr, t, 1 + j, out_ref, me, (*chip, c), src=x_ref).start()

    def mid(self, ins, outs, scr):
        x, y, c, chips = _place()
        for t, out_ref in enumerate(outs):
            for j, chip in enumerate(chips):
                self._copy(scr, t, 1 + j, out_ref, (*chip, c), (x, y, c)).wait_recv()
                self._copy(scr, t, 4 + j, out_ref, (*chip, c), (x, y, 1 - c)).start()

    def finish(self, ins, outs, scr):
        x, y, c, chips = _place()
        me = (x, y, c)
        for t, (x_ref, out_ref) in enumerate(zip(ins, outs)):
            self._copy(scr, t, 0, out_ref, (x, y, 1 - c), me).wait_recv()
            for j, chip in enumerate(chips):
                self._copy(scr, t, 4 + j, out_ref, (*chip, 1 - c), me).wait_recv()
            self._copy(scr, t, 0, out_ref, me, (x, y, 1 - c), src=x_ref).wait_send()
            for j, chip in enumerate(chips):
                self._copy(scr, t, 1 + j, out_ref, me, (*chip, c), src=x_ref).wait_send()
                self._copy(scr, t, 4 + j, out_ref, (*chip, c), (x, y, 1 - c)).wait_send()
            pltpu.make_async_copy(x_ref, out_ref.at[4 * x + 2 * y + c], scr[2].at[t]).wait()


class _Hooks:
    def __init__(self, hooks):
        self.hooks = list(hooks)
        self.has_mid = any(h.has_mid for h in hooks)
        self.inputs = [a for h in hooks for a in h.inputs]
        self.out_shape = [s for h in hooks for s in h.out_shape]
        self.scratch = [s for h in hooks for s in h.scratch]
        self.aliases = []
        i0 = o0 = 0
        for h in hooks:
            self.aliases += [(i0 + a, o0 + b) for a, b in h.aliases]
            i0 += len(h.inputs)
            o0 += len(h.out_shape)

    def _each(self, ins, outs, scr):
        i0 = o0 = s0 = 0
        for h in self.hooks:
            ni, no, ns = len(h.inputs), len(h.out_shape), len(h.scratch)
            yield h, ins[i0:i0 + ni], outs[o0:o0 + no], scr[s0:s0 + ns]
            i0, o0, s0 = i0 + ni, o0 + no, s0 + ns

    def start(self, ins, outs, scr):
        for h, i, o, s in self._each(ins, outs, scr):
            h.start(i, o, s)

    def mid(self, ins, outs, scr):
        for h, i, o, s in self._each(ins, outs, scr):
            if h.has_mid:
                h.mid(i, o, s)

    def finish(self, ins, outs, scr):
        for h, i, o, s in self._each(ins, outs, scr):
            h.finish(i, o, s)

    def split(self, outs):
        res, o0 = [], 0
        for h in self.hooks:
            res.append(list(outs[o0:o0 + len(h.out_shape)]))
            o0 += len(h.out_shape)
        return res


def _pair_sum(g, other, c_idx):
    nq, r, w = g.shape
    h = r // 2
    tr = _div_tile(h)
    nt = h // tr

    def body(c_ref, g_ref, o_ref, s_ref):
        s_ref[...] = (g_ref[...].astype(F32) + o_ref[...].astype(F32)).astype(s_ref.dtype)

    return pl.pallas_call(
        body, name="pair_sum",
        grid_spec=pltpu.PrefetchScalarGridSpec(
            num_scalar_prefetch=1, grid=(nq, nt),
            in_specs=[pl.BlockSpec((None, tr, w), lambda k, i, c_ref: (k, c_ref[0] * nt + i, 0)),
                      pl.BlockSpec((None, tr, w), lambda k, i, c_ref: (k, i, 0))],
            out_specs=pl.BlockSpec((None, tr, w), lambda k, i, c_ref: (k, i, 0))),
        out_shape=jax.ShapeDtypeStruct((nq, h, w), g.dtype),
        compiler_params=_cparams(2),
    )(c_idx, g, other)


def _chip_sum(s, others, q_idx):
    _, h, w = s.shape
    tr = _div_tile(h)

    def body(q_ref, s_ref, o_ref, out_ref):
        out_ref[...] = ((s_ref[...].astype(F32) + o_ref[0].astype(F32)) + o_ref[1].astype(F32)) + o_ref[2].astype(F32)

    return pl.pallas_call(
        body, name="chip_sum",
        grid_spec=pltpu.PrefetchScalarGridSpec(
            num_scalar_prefetch=1, grid=(h // tr,),
            in_specs=[pl.BlockSpec((None, tr, w), lambda i, q_ref: (q_ref[0], i, 0)),
                      pl.BlockSpec((3, tr, w), lambda i, q_ref: (0, i, 0))],
            out_specs=pl.BlockSpec((tr, w), lambda i, q_ref: (i, 0))),
        out_shape=jax.ShapeDtypeStruct((h, w), F32),
        compiler_params=_cparams(1),
    )(q_idx, s, others)


def _small_update(q_idx, parts, ws, ms, vs, col_block):
    n = len(parts)
    has_w = [w is not None for w in ws]

    def body(q_ref, *refs):
        pos = 0
        ins = []
        for t in range(n):
            k = 4 if has_w[t] else 1
            ins.append(refs[pos:pos + k])
            pos += k
        outs = refs[pos:]
        opos = 0
        for t in range(n):
            p_ref = ins[t][0]
            g = p_ref[0]
            for s in range(1, p_ref.shape[0]):
                g = g + p_ref[s]
            if has_w[t]:
                _, w_ref, m_ref, v_ref = ins[t]
                g_ref, d_ref, nm_ref, nv_ref = outs[opos:opos + 4]
                opos += 4
                g_ref[...] = g
                d_ref[...], nm_ref[...], nv_ref[...] = _adamw_update(w_ref[...], g, m_ref[...], v_ref[...])
            else:
                outs[opos][...] = g
                opos += 1

    def whole(shape):
        nd = len(shape)
        return pl.BlockSpec(shape, lambda i, q_ref: (0,) * nd)

    in_specs, out_specs, out_shape, args = [], [], [], []
    for t in range(n):
        k, r, wf = parts[t].shape
        if col_block[t]:
            w = wf // N_CHIPS
            in_specs.append(pl.BlockSpec((k, r, w), lambda i, q_ref: (0, 0, q_ref[0])))
        else:
            w = wf
            in_specs.append(whole((k, r, wf)))
        args.append(parts[t])
        if has_w[t]:
            assert ws[t].shape == (r, w), (ws[t].shape, r, w)
            in_specs += [whole((r, w))] * 3
            args += [ws[t], ms[t], vs[t]]
            out_specs += [whole((r, w))] * 4
            out_shape += [jax.ShapeDtypeStruct((r, w), F32)] * 4
        else:
            out_specs.append(whole((r, w)))
            out_shape.append(jax.ShapeDtypeStruct((r, w), F32))
    return pl.pallas_call(
        body, name="small_update",
        grid_spec=pltpu.PrefetchScalarGridSpec(num_scalar_prefetch=1, grid=(1,), in_specs=in_specs,
                                               out_specs=out_specs),
        out_shape=out_shape, compiler_params=_cparams(1),
    )(q_idx, *args)


_PACK_SEGMENTS = ((0, 1552), None, (1552, 2064), (2064, 2128), (2064, 2128), (2128, 2192), (2128, 2192),
                  (2192, 2256), (2192, 2256), (2256, 2320), (2256, 2320))
_UNPACK_SEGMENTS = (((0, 1552), (0,)), ((1552, 2064), (P_SQ,)), ((2064, 2128), (P_SK, P_SK + 64)),
                    ((2128, 2192), (P_SK + 128, P_SK + 192)), ((2192, 2256), (P_SV, P_SV + 64)),
                    ((2256, 2320), (P_SV + 128, P_SV + 192)))


def _pack_win(w4):
    per = w4.shape[2]
    pieces = []
    for seg in _PACK_SEGMENTS:
        if seg is None:
            pieces.append(jnp.zeros((w4.shape[1], 128 - GLA_RANK), w4.dtype))
            continue
        for q in range(w4.shape[0]):
            lo, hi = max(seg[0], q * per), min(seg[1], (q + 1) * per)
            if lo < hi:
                pieces.append(w4[q][:, lo - q * per:hi - q * per])
    return jnp.concatenate(pieces, axis=1)


def _unpack_dwin(d):
    per = D_IN // N_CHIPS
    chips = []
    for q in range(N_CHIPS):
        pieces = []
        for (a, b), starts in _UNPACK_SEGMENTS:
            lo, hi = max(a, q * per), min(b, (q + 1) * per)
            if lo < hi:
                copies = [d[:, s + lo - a:s + hi - a] for s in starts]
                pieces.append(copies[0] if len(copies) == 1 else copies[0] + copies[1])
        chips.append(jnp.concatenate(pieces, axis=1))
    return jnp.stack(chips).astype(BF16)


def _local_step(x, target, meta, p):
    s = x.shape[0]
    t = s + BLK
    h0 = jnp.concatenate([jnp.zeros((PAD, D_MODEL), F32), meta, x], axis=0)
    cos, sin = _rope_tables(t)

    h1, n1, g1, u1, a1, f1 = _ffn_fwd(h0, p["ffn1_pre_norm"], p["ffn1_w"], p["ffn1_post_norm"])
    n2, gq, gk, gv, gg, ga, la, sq, sk, sv = _mix_proj(h1, p["mix_pre_norm"], p["w_in"], p["gla_w_a2"], p["gla_b_a"],
                                                       cos, sin)
    ogla, ss = _gla_fwd(gq, gk, gv, la)
    oswa = _swa_fwd(p["swa_sinks"], sq, sk, sv)
    h2, cat, m = _mix_out(h1, ogla, gg, oswa, p["gla_out_norm"], p["swa_out_norm"], p["w_out"], p["mix_post_norm"])
    grads = {}
    dy, n3, g3, u3, a3, df3, grads["ffn2_post_norm"], sse = _ffn_fwd(
        h2, p["ffn2_pre_norm"], p["ffn2_w"], p["ffn2_post_norm"], target=target)

    dh2, dg3, du3, grads["ffn2_pre_norm"] = _ffn_bwd(
        dy, h2, None, g3, u3, p["ffn2_pre_norm"], p["ffn2_w"], p["ffn2_post_norm"], df=df3)
    (gud,) = _ffn_wgrad(n3, df3, dg3, du3, a3)
    grads["ffn2_w_gate"], grads["ffn2_w_up"], grads["ffn2_w_down"] = gud[:, :FJ], gud[:, FJ:2 * FJ], gud[:, 2 * FJ:]

    dogla, dgg, doswa, dm, grads["mix_post_norm"], grads["gla_out_norm"], grads["swa_out_norm"] = _mix_out_bwd(
        dh2, m, ogla, gg, oswa, p["gla_out_norm"], p["swa_out_norm"], p["w_out"], p["mix_post_norm"])
    grads["w_out"] = _xty(cat, dm)
    dsq, dsk, dsv, dkm, dvm, dsinks = _swa_bwd(p["swa_sinks"], sq, sk, sv, oswa, doswa)
    grads["swa_sinks"] = dsinks[:, 0]
    dgq, dgk, dgv, dla = _gla_bwd(gq, gk, gv, la, ss, dogla)
    dh1, dproj, grads["mix_pre_norm"], dwa2p, grads["gla_b_a"] = _mix_in_bwd(
        dh2, h1, p["mix_pre_norm"], p["w_in"], p["gla_w_a2"], p["gla_b_a"], cos, sin, ga, dgq, dgk, dgv, dgg, dla,
        dsq, dsk, dsv, dkm, dvm)
    grads["gla_w_a2"] = dwa2p[:GLA_RANK]
    grads["w_in"] = _unpack_dwin(_xty(n2, dproj))

    dh0, df1, dg1, du1, grads["ffn1_pre_norm"], grads["ffn1_post_norm"] = _ffn_bwd(
        dh1, h0, f1, g1, u1, p["ffn1_pre_norm"], p["ffn1_w"], p["ffn1_post_norm"])
    (gud,) = _ffn_wgrad(n1, df1, dg1, du1, a1)
    grads["ffn1_w_gate"], grads["ffn1_w_up"], grads["ffn1_w_down"] = gud[:, :FJ], gud[:, FJ:2 * FJ], gud[:, 2 * FJ:]
    grads["meta_tokens"] = dh0[PAD:BLK]
    return sse[0, 0], dh0[BLK:], grads


WEIGHTS = ['meta_tokens', 'ffn1_pre_norm', 'ffn1_w_gate', 'ffn1_w_up', 'ffn1_w_down', 'ffn1_post_norm',
           'mix_pre_norm', 'w_in', 'gla_w_a2', 'gla_b_a', 'gla_out_norm', 'swa_sinks', 'swa_out_norm', 'w_out',
           'mix_post_norm', 'ffn2_pre_norm', 'ffn2_w_gate', 'ffn2_w_up', 'ffn2_w_down', 'ffn2_post_norm']
BIG = ['ffn1_w_gate', 'ffn1_w_up', 'ffn1_w_down', 'w_in', 'w_out', 'ffn2_w_gate', 'ffn2_w_up', 'ffn2_w_down']
SMALL = [n for n in WEIGHTS if n not in BIG]
FJ = D_FF // N_CHIPS
D_IN_J = D_IN // N_CHIPS
D_OUT_J = D_MODEL // N_CHIPS
TRANSPOSED = ('ffn1_w_gate', 'ffn1_w_up', 'ffn2_w_gate', 'ffn2_w_up')


def _shard2d(name, a):
    return a[0].T if name in TRANSPOSED else a[0]


def _unshard2d(name, a):
    return (a.T if name in TRANSPOSED else a)[None]


def kernel(x, meta_tokens, ffn1_pre_norm, ffn1_w_gate, ffn1_w_up, ffn1_w_down, ffn1_post_norm, mix_pre_norm, w_in, gla_w_a2, gla_b_a, gla_out_norm, swa_sinks, swa_out_norm, w_out, mix_post_norm, ffn2_pre_norm, ffn2_w_gate, ffn2_w_up, ffn2_w_down, ffn2_post_norm, loss_target, m_meta_tokens, m_ffn1_pre_norm, m_ffn1_w_gate, m_ffn1_w_up, m_ffn1_w_down, m_ffn1_post_norm, m_mix_pre_norm, m_w_in, m_gla_w_a2, m_gla_b_a, m_gla_out_norm, m_swa_sinks, m_swa_out_norm, m_w_out, m_mix_post_norm, m_ffn2_pre_norm, m_ffn2_w_gate, m_ffn2_w_up, m_ffn2_w_down, m_ffn2_post_norm, v_meta_tokens, v_ffn1_pre_norm, v_ffn1_w_gate, v_ffn1_w_up, v_ffn1_w_down, v_ffn1_post_norm, v_mix_pre_norm, v_w_in, v_gla_w_a2, v_gla_b_a, v_gla_out_norm, v_swa_sinks, v_swa_out_norm, v_w_out, v_mix_post_norm, v_ffn2_pre_norm, v_ffn2_w_gate, v_ffn2_w_up, v_ffn2_w_down, v_ffn2_post_norm):
    args = dict(locals())
    w = {n: args[n] for n in WEIGHTS}
    mom = {n: args["m_" + n] for n in WEIGHTS}
    var = {n: args["v_" + n] for n in WEIGHTS}
    cx, cy, cc = lax.axis_index("x"), lax.axis_index("y"), lax.axis_index("c")
    q_idx = (2 * cx + cy).astype(jnp.int32).reshape(1)
    c_idx = cc.astype(jnp.int32).reshape(1)

    q_chip = 2 * cx + cy
    bf = {n: _own_slot(_shard2d(n, w[n]).astype(BF16), q_chip) for n in ("w_in", "w_out")}
    for ffn in ("ffn1", "ffn2"):
        stacked = jnp.concatenate([_shard2d(ffn + s, w[ffn + s]) for s in ("_w_gate", "_w_up", "_w_down")], axis=0)
        bf[ffn] = _own_slot(stacked.astype(BF16), q_chip)
    qc_idx = jnp.stack([q_chip, cc]).astype(jnp.int32)
    early = _GatherChips([_own_slot(w["meta_tokens"], q_chip),
                          _own_slot(w["gla_w_a2"].reshape(GLA_RANK, GLA_KW // N_CHIPS), q_chip)])
    meta4, wa24 = _comm_call(early, "gather_small")
    meta_full = meta4.transpose(1, 0, 2).reshape(N_META, D_MODEL)
    wa2p = jnp.pad(wa24.transpose(1, 0, 2).reshape(GLA_RANK, GLA_KW), ((0, 128 - GLA_RANK), (0, 0))).astype(BF16)
    sinks = w["swa_sinks"].reshape(SWA_QH)

    seq, target = x[0], loss_target[0]
    t = seq.shape[0] + BLK
    h0, n1 = _embed_norm(seq, meta_full, w["ffn1_pre_norm"])
    cos, sin = _rope_tables(t)
    late = _GatherChips([bf["w_in"], bf["w_out"], bf["ffn2"]])
    (h1, g1, u1, a1, f1), (w31,), (win4, wout4, w32) = _ffn_fwd_gather(
        h0, n1, bf["ffn1"], w["ffn1_post_norm"], qc_idx, late)
    winp = _pack_win(win4)
    wout = wout4.reshape(D_MODEL, D_MODEL)
    n2, gq, gk, gv, gg, ga, la, sq, sk, sv = _mix_proj(h1, w["mix_pre_norm"], winp, wa2p, w["gla_b_a"], cos, sin)
    ogla, ss = _gla_fwd(gq, gk, gv, la)
    oswa = _swa_fwd(sinks, sq, sk, sv)
    h2, cat, m = _mix_out(h1, ogla, gg, oswa, w["gla_out_norm"], w["swa_out_norm"], wout, w["mix_post_norm"])
    g = {}
    dy, n3, g3, u3, a3, df3, g["ffn2_post_norm"], sse = _ffn_fwd(
        h2, w["ffn2_pre_norm"], w32, w["ffn2_post_norm"], target=target)

    dh2, dg3, du3, g["ffn2_pre_norm"] = _ffn_bwd(
        dy, h2, None, g3, u3, w["ffn2_pre_norm"], w32, w["ffn2_post_norm"], df=df3)
    (gf2,) = _ffn_wgrad(n3, df3, dg3, du3, a3)
    (dogla, dgg, doswa, dm, g["mix_post_norm"], g["gla_out_norm"], g["swa_out_norm"]), (rgf2,) = _mix_out_bwd(
        dh2, m, ogla, gg, oswa, w["gla_out_norm"], w["swa_out_norm"], wout, w["mix_post_norm"],
        hook=_PairExchange([gf2]))
    sgf2 = _pair_sum(gf2, rgf2, c_idx)
    gout = _xty(cat, dm).reshape(N_CHIPS, D_OUT_J, D_MODEL).astype(BF16)
    (dsq, dsk, dsv, dkm, dvm, dsinks), (ogf2,) = _swa_bwd(sinks, sq, sk, sv, oswa, doswa,
                                                          hook=_ChipScatter([sgf2]))
    g["swa_sinks"] = dsinks
    dgq, dgk, dgv, dla = _gla_bwd(gq, gk, gv, la, ss, dogla)
    dh1, dproj, g["mix_pre_norm"], dwa2p, g["gla_b_a"] = _mix_in_bwd(
        dh2, h1, w["mix_pre_norm"], winp, wa2p, w["gla_b_a"], cos, sin, ga, dgq, dgk, dgv, dgg, dla,
        dsq, dsk, dsv, dkm, dvm)
    g["gla_w_a2"] = dwa2p[:GLA_RANK]
    gin = _unpack_dwin(_xty(n2, dproj))
    (dh0, df1, dg1, du1, g["ffn1_pre_norm"], g["ffn1_post_norm"]), (rgin, rgout) = _ffn_bwd(
        dh1, h0, f1, g1, u1, w["ffn1_pre_norm"], w31, w["ffn1_post_norm"],
        hook=_PairExchange([gin, gout]))
    sgin, sgout = _pair_sum(gin, rgin, c_idx), _pair_sum(gout, rgout, c_idx)
    g["meta_tokens"] = dh0[PAD:BLK]
    grad_x = dh0[BLK:]
    late_small = ["gla_w_a2", "swa_sinks"]
    direct = [n for n in SMALL if n not in late_small]
    names = direct + late_small
    hooks = _Hooks([_ChipScatter([sgin, sgout]), _GatherDevices([g[n] for n in names] + [sse])])
    own1, others1, houts = _ffn_wgrad_reduce(n1, df1, dg1, du1, a1, qc_idx, hooks)
    (ogin, ogout), gathered = hooks.split(houts)
    halves = [_chip_sum(own1[None], others1, jnp.zeros((1,), jnp.int32))]
    halves += [_chip_sum(s, o, q_idx) for s, o in ((sgin, ogin), (sgout, ogout), (sgf2, ogf2))]
    others = _comm_call(_PairShare(halves), "pair_share")
    reduced = {"ffn1_w_gate": (0, 0), "ffn1_w_up": (0, FJ), "ffn1_w_down": (0, 2 * FJ), "w_in": (1, 0),
               "w_out": (2, 0), "ffn2_w_gate": (3, 0), "ffn2_w_up": (3, FJ), "ffn2_w_down": (3, 2 * FJ)}
    grad, delta, new_m, new_v = {}, {}, {}, {}
    for n in BIG:
        k, row0 = reduced[n]
        outs = _adamw_halves(_shard2d(n, w[n]), halves[k], others[k], _shard2d(n, mom[n]), _shard2d(n, var[n]),
                             c_idx, row0)
        grad[n], delta[n], new_m[n], new_v[n] = [_unshard2d(n, a) for a in outs]

    late = late_small
    mat = lambda a: a.reshape(a.shape[-2:])
    none3 = [None] * (len(late) + 1)
    outs = _small_update(q_idx, gathered, [mat(w[n]) for n in direct] + none3, [mat(mom[n]) for n in direct] + none3,
                         [mat(var[n]) for n in direct] + none3, [n == "meta_tokens" for n in names] + [False])
    sum_a2, sum_sinks, sum_sse = outs[4 * len(direct):]
    loss = sum_sse[0, 0] * (0.5 / D_MODEL)
    g_late = [lax.dynamic_slice_in_dim(sum_a2, q_chip * (GLA_KW // N_CHIPS), GLA_KW // N_CHIPS, axis=1)[None],
              sum_sinks[:, 0].reshape(1, 1, SWA_QH)]
    outs = list(outs[:4 * len(direct)]) + list(_small_update(
        q_idx, g_late, [mat(w[n]) for n in late], [mat(mom[n]) for n in late], [mat(var[n]) for n in late],
        [False, False]))
    for k, n in enumerate(names):
        grad[n], delta[n], new_m[n], new_v[n] = [a.reshape(w[n].shape) for a in outs[4 * k:4 * k + 4]]

    return (loss, grad_x[None], *[grad[n] for n in WEIGHTS], *[delta[n] for n in WEIGHTS],
            *[new_m[n] for n in WEIGHTS], *[new_v[n] for n in WEIGHTS])
```

```python
import functools
import math

import numpy as np
import jax
import jax.numpy as jnp
from jax import lax
from jax.experimental import pallas as pl
from jax.experimental.pallas import tpu as pltpu

F32 = jnp.float32
BF16 = jnp.bfloat16
MESH = pl.DeviceIdType.MESH

D_MODEL = 1024
D_FF = 2816
N_CHIPS = 4
N_DEV = 8
N_META = 16
BLK = 128
PAD = BLK - N_META
GLA_CHUNK = 64
GLA_HEADS = 4
GLA_DV = 128
GLA_DK = 64
GLA_KW = GLA_HEADS * GLA_DK
GLA_W = GLA_HEADS * GLA_DV
GLA_RANK = 16
GLA_TAU = 16.0
SWA_HD = 64
SWA_QH = 8
SWA_KVH = 2
SWA_W = SWA_QH * SWA_HD
WINDOW = 128
ROPE_THETA = 10000.0
EPS = 1e-6
NEG_INF = -1e30
IN_SPLITS = (256, 256, 512, 512, 16, 512, 128, 128)
D_IN = sum(IN_SPLITS)
P_GQ, P_GK, P_GV, P_GG, P_GA, P_SQ, P_SK, P_SV, P_END = 0, 256, 512, 1024, 1536, 1664, 2176, 2432, 2688
ADAM_LR, ADAM_B1, ADAM_B2, ADAM_EPS, ADAM_WD, ADAM_STEP = 0.001, 0.9, 0.999, 1e-08, 0.01, 10
VMEM_LIMIT = 56 * 1024 * 1024

NT = (((1,), (1,)), ((), ()))
TN = (((0,), (0,)), ((), ()))


def _cparams(n_axes):
    return pltpu.CompilerParams(dimension_semantics=("arbitrary",) * n_axes, vmem_limit_bytes=VMEM_LIMIT)


def _row_tile(t):
    for tm in (640, 512, 384, 256, 128):
        if t % tm == 0:
            return tm
    raise ValueError(t)


SEQ_BLOCKS_PER_STEP = 5


def _seq_tile(t):
    return SEQ_BLOCKS_PER_STEP * BLK if t % (SEQ_BLOCKS_PER_STEP * BLK) == 0 else BLK


ROW_PARTS = 2


def _row_parts(tm):
    n = ROW_PARTS if tm % (16 * ROW_PARTS) == 0 else 1
    return [slice(k * (tm // n), (k + 1) * (tm // n)) for k in range(n)]


def _contract_tile(t):
    return 1664 if t % 1664 == 0 else _row_tile(t)


def _div_tile(r, cap=512):
    best = None
    for tr in range(8, min(r, cap) + 1, 8):
        if r % tr == 0:
            best = tr
    return best if best is not None else r


def _dot(a, b):
    return jnp.dot(a, b, preferred_element_type=F32)


def _dg(a, b, dims):
    return lax.dot_general(a, b, dims, preferred_element_type=F32)


def _rms(x, w):
    r = lax.rsqrt(jnp.mean(x * x, axis=-1, keepdims=True) + EPS)
    xh = x * r
    return xh * w, xh, r


def _rms_bwd(xh, r, w, dy):
    wdy = dy * w
    dx = r * (wdy - xh * jnp.mean(wdy * xh, axis=-1, keepdims=True))
    dw = jnp.sum(dy * xh, axis=0, keepdims=True)
    return dx, dw


def _sigmoid(x):
    return 1.0 / (1.0 + jnp.exp(-x))


def _full(shape):
    nd = len(shape)
    return pl.BlockSpec(shape, lambda *_: (0,) * nd)


ANY = pl.BlockSpec(memory_space=pl.ANY)


def _pallas(body, *, name, grid, in_specs, out_specs, out_shape, args, scratch_shapes=(), hook=None):
    n_axes = len(grid)
    if hook is None:
        return pl.pallas_call(body, name=name, grid=grid, in_specs=list(in_specs), out_specs=list(out_specs),
                              out_shape=list(out_shape), scratch_shapes=list(scratch_shapes),
                              compiler_params=_cparams(n_axes))(*args)
    n_in, n_out, n_scr = len(in_specs), len(out_specs), len(scratch_shapes)
    h_in, h_out = len(hook.inputs), len(hook.out_shape)
    total = math.prod(grid)

    def wrapped(*refs):
        ins, hins = refs[:n_in], refs[n_in:n_in + h_in]
        o0 = n_in + h_in
        outs, houts = refs[o0:o0 + n_out], refs[o0 + n_out:o0 + n_out + h_out]
        s0 = o0 + n_out + h_out
        scr, hscr = refs[s0:s0 + n_scr], refs[s0 + n_scr:]
        step = pl.program_id(0)
        for a in range(1, n_axes):
            step = step * grid[a] + pl.program_id(a)

        @pl.when(step == 0)
        def _():
            hook.start(hins, houts, hscr)

        body(*ins, *outs, *scr)

        if hook.has_mid:
            @pl.when(step == (3 * total) // 4)
            def _():
                hook.mid(hins, houts, hscr)

        @pl.when(step == total - 1)
        def _():
            hook.finish(hins, houts, hscr)

    res = pl.pallas_call(
        wrapped, name=name, grid=grid, in_specs=list(in_specs) + [ANY] * h_in,
        out_specs=list(out_specs) + [ANY] * h_out, out_shape=list(out_shape) + list(hook.out_shape),
        scratch_shapes=list(scratch_shapes) + list(hook.scratch), compiler_params=_cparams(n_axes),
        input_output_aliases={n_in + a: n_out + b for a, b in hook.aliases},
    )(*args, *hook.inputs)
    return res[:n_out], res[n_out:]


def _ffn_weight_specs(w3):
    fj = w3.shape[1] // 3
    return fj, [pl.BlockSpec((None, fj, D_MODEL), functools.partial(lambda i, j, k: (j, k, 0), k=k)) for k in range(3)]


def _ffn_fwd(h, wpre, w3, wpost, hook=None, target=None):
    t = h.shape[0]
    tm = _row_tile(t)
    nj, rows3, _ = w3.shape
    fj = rows3 // 3
    nblk = tm // BLK if target is not None else 0

    def body(*refs):
        h_ref, wpre_ref, w_hbm, wpost_ref = refs[:4]
        t_refs = refs[4:4 + nblk]
        hout_ref, n_ref, p1_ref, p2_ref, a_ref, f_ref = refs[4 + nblk:10 + nblk]
        acc_ref, wv, wsem = refs[-3:]
        i = pl.program_id(0)
        j = pl.program_id(1)

        @pl.when((i == 0) & (j == 0))
        def _():
            for k in range(nj):
                pltpu.make_async_copy(w_hbm.at[k], wv.at[k], wsem.at[k]).start()

        @pl.when(i == 0)
        def _():
            pltpu.make_async_copy(w_hbm.at[j], wv.at[j], wsem.at[j]).wait()

        @pl.when(j == 0)
        def _():
            y, _, _ = _rms(h_ref[...], wpre_ref[...])
            n_ref[...] = y.astype(BF16)
            acc_ref[...] = jnp.zeros_like(acc_ref)

        if target is not None:
            dwpost_ref, sse_ref = refs[10 + nblk:12 + nblk]

            @pl.when((i == 0) & (j == 0))
            def _():
                dwpost_ref[...] = jnp.zeros_like(dwpost_ref)
                sse_ref[...] = jnp.zeros_like(sse_ref)

        n = n_ref[...]
        g = _dg(n, wv[j, 0:fj], NT)
        u = _dg(n, wv[j, fj:2 * fj], NT)
        sg = _sigmoid(g)
        silu = g * sg
        p1_ref[...] = (u * (sg + silu * (1.0 - sg))).astype(BF16)
        p2_ref[...] = silu.astype(BF16)
        a = (silu * u).astype(BF16)
        a_ref[...] = a
        acc_ref[...] += _dot(a, wv[j, 2 * fj:3 * fj])

        @pl.when(j == nj - 1)
        def _():
            f = acc_ref[...]
            wpost = wpost_ref[...]
            y, fh, r = _rms(f, wpost)
            hout = h_ref[...] + 0.5 * y
            if target is None:
                f_ref[...] = f
                hout_ref[...] = hout
            else:
                sse = jnp.zeros((1, 1), F32)
                errs = []
                for k in range(nblk):
                    err = hout[k * BLK:(k + 1) * BLK] - t_refs[k][...]
                    if k == 0:
                        err = jnp.where(i > 0, err, 0.0)
                    errs.append(err)
                    sse = sse + jnp.sum(jnp.sum(err * err, axis=1, keepdims=True), axis=0, keepdims=True)
                dy = (jnp.concatenate(errs, axis=0) if nblk > 1 else errs[0]) * (1.0 / D_MODEL)
                hout_ref[...] = dy
                df, dw = _rms_bwd(fh, r, wpost, 0.5 * dy)
                f_ref[...] = df.astype(BF16)
                dwpost_ref[...] += dw
                sse_ref[...] += jnp.broadcast_to(sse, sse_ref.shape)

    row = pl.BlockSpec((tm, D_MODEL), lambda i, j: (i, 0))
    vec = pl.BlockSpec((1, D_MODEL), lambda i, j: (0, 0))
    act = pl.BlockSpec((None, tm, fj), lambda i, j: (j, i, 0))
    t_specs = [pl.BlockSpec((BLK, D_MODEL), functools.partial(lambda i, j, k: (jnp.maximum(nblk * i + k - 1, 0), 0), k=k))
               for k in range(nblk)]
    loss_spec = [vec, _full((1, 128))] if target is not None else []
    loss_shape = [jax.ShapeDtypeStruct((1, D_MODEL), F32), jax.ShapeDtypeStruct((1, 128), F32)] if (
        target is not None) else []
    return _pallas(
        body, name="ffn_fwd", grid=(t // tm, nj),
        in_specs=[row, vec, ANY, vec] + t_specs,
        out_specs=[row, row, act, act, act, row] + loss_spec,
        out_shape=[jax.ShapeDtypeStruct((t, D_MODEL), F32), jax.ShapeDtypeStruct((t, D_MODEL), BF16),
                   jax.ShapeDtypeStruct((nj, t, fj), BF16), jax.ShapeDtypeStruct((nj, t, fj), BF16),
                   jax.ShapeDtypeStruct((nj, t, fj), BF16),
                   jax.ShapeDtypeStruct((t, D_MODEL), F32 if target is None else BF16)] + loss_shape,
        scratch_shapes=[pltpu.VMEM((tm, D_MODEL), F32), pltpu.VMEM((nj, rows3, D_MODEL), BF16),
                        pltpu.SemaphoreType.DMA((nj,))],
        args=(h, wpre, w3, wpost) + (target,) * nblk, hook=hook)


def _ffn_bwd(dhout, h, f, p14, p24, wpre, w3, wpost, hook=None, df=None):
    t = h.shape[0]
    tm = _row_tile(t)
    nj = w3.shape[0]
    fj, wspecs = _ffn_weight_specs(w3)
    have_df = df is not None

    def body(dhout_ref, h_ref, f_ref, p1_ref, p2_ref, wpre_ref, wg_ref, wu_ref, wd_ref, wpost_ref, *rest):
        if have_df:
            dh_ref, dg_ref, du_ref, dwpre_ref, dn_ref = rest
            df_ref = f_ref
        else:
            dh_ref, df_ref, dg_ref, du_ref, dwpre_ref, dwpost_ref, dn_ref = rest
        i = pl.program_id(0)
        j = pl.program_id(1)

        @pl.when((i == 0) & (j == 0))
        def _():
            dwpre_ref[...] = jnp.zeros_like(dwpre_ref)
            if not have_df:
                dwpost_ref[...] = jnp.zeros_like(dwpost_ref)

        @pl.when(j == 0)
        def _():
            if not have_df:
                wpost = wpost_ref[...]
                _, fh, r = _rms(f_ref[...], wpost)
                dfv, dw = _rms_bwd(fh, r, wpost, 0.5 * dhout_ref[...])
                dwpost_ref[...] += dw
                df_ref[...] = dfv.astype(BF16)
            dn_ref[...] = jnp.zeros_like(dn_ref)

        parts = _row_parts(tm)
        das = [_dg(df_ref[rows, :], wd_ref[...], NT) for rows in parts]
        for rows, da in zip(parts, das):
            dg = (da * p1_ref[rows, :].astype(F32)).astype(BF16)
            du = (da * p2_ref[rows, :].astype(F32)).astype(BF16)
            dg_ref[rows, :] = dg
            du_ref[rows, :] = du
            dn_ref[rows, :] += _dot(dg, wg_ref[...]) + _dot(du, wu_ref[...])

        @pl.when(j == nj - 1)
        def _():
            wpre = wpre_ref[...]
            _, hh, r = _rms(h_ref[...], wpre)
            dx, dw = _rms_bwd(hh, r, wpre, dn_ref[...])
            dwpre_ref[...] += dw
            dh_ref[...] = dhout_ref[...] + dx

    row = pl.BlockSpec((tm, D_MODEL), lambda i, j: (i, 0))
    vec = pl.BlockSpec((1, D_MODEL), lambda i, j: (0, 0))
    act = pl.BlockSpec((None, tm, fj), lambda i, j: (j, i, 0))
    actshape = jax.ShapeDtypeStruct((nj, t, fj), BF16)
    rowf, rowb, vecf = (jax.ShapeDtypeStruct((t, D_MODEL), F32), jax.ShapeDtypeStruct((t, D_MODEL), BF16),
                        jax.ShapeDtypeStruct((1, D_MODEL), F32))
    return _pallas(
        body, name="ffn_bwd", grid=(t // tm, nj),
        in_specs=[row, row, row, act, act, vec] + wspecs + [vec],
        out_specs=[row, act, act, vec] if have_df else [row, row, act, act, vec, vec],
        out_shape=[rowf, actshape, actshape, vecf] if have_df else [rowf, rowb, actshape, actshape, vecf, vecf],
        scratch_shapes=[pltpu.VMEM((tm, D_MODEL), F32)],
        args=(dhout, h, df if have_df else f, p14, p24, wpre, w3, w3, w3, wpost), hook=hook)


def _ffn_wgrad(n, df, dg4, du4, a4, hook=None):
    t = n.shape[0]
    tm = _contract_tile(t)
    ni = t // tm
    nj, _, fj = dg4.shape

    def body(n_ref, df_ref, dg_ref, du_ref, a_ref, dw_ref, acc):
        i = pl.program_id(1)

        @pl.when(i == 0)
        def _():
            acc[...] = jnp.zeros_like(acc)

        nn = n_ref[...]
        acc[0:fj, :] += _dg(dg_ref[...], nn, TN)
        acc[fj:2 * fj, :] += _dg(du_ref[...], nn, TN)
        acc[2 * fj:3 * fj, :] += _dg(a_ref[...], df_ref[...], TN)

        @pl.when(i == ni - 1)
        def _():
            dw_ref[...] = acc[...].astype(BF16)

    row = pl.BlockSpec((tm, D_MODEL), lambda j, i: (i, 0))
    act = pl.BlockSpec((None, tm, fj), lambda j, i: (j, i, 0))
    return _pallas(
        body, name="ffn_wgrad", grid=(nj, ni),
        in_specs=[row, row, act, act, act],
        out_specs=[pl.BlockSpec((None, 3 * fj, D_MODEL), lambda j, i: (j, 0, 0))],
        out_shape=[jax.ShapeDtypeStruct((nj, 3 * fj, D_MODEL), BF16)],
        scratch_shapes=[pltpu.VMEM((3 * fj, D_MODEL), F32)],
        args=(n, df, dg4, du4, a4), hook=hook)


def _embed_norm(x, meta_buf, w):
    t = x.shape[0] + BLK
    tm = _row_tile(t)
    nblk = tm // BLK
    ni = t // tm
    gather = _GatherChips([meta_buf])

    def body(*refs):
        x_refs = refs[:nblk]
        w_ref, mb_in, h_ref, n_ref, mb_out, mv, msem, send, recv = refs[nblk:]
        step = pl.program_id(0)
        tile = (step + 1) % ni
        hook_refs = ([mb_in], [mb_out], [send, recv])
        steps = (0, 1, ni - 2) if ni >= 3 else (0, 0, 0)
        for at, phase in zip(steps, (gather.start, gather.mid, gather.finish)):
            @pl.when(step == at)
            def _(phase=phase):
                phase(*hook_refs)

        @pl.when(step == 0)
        def _():
            mv[...] = jnp.zeros_like(mv)

        @pl.when(step == ni - 1)
        def _():
            cp = pltpu.make_async_copy(mb_out, mv, msem)
            cp.start()
            cp.wait()

        meta = jnp.concatenate([mv[k] for k in range(N_CHIPS)], axis=1)
        first = jnp.concatenate([jnp.zeros((PAD, D_MODEL), F32), meta], axis=0)
        blocks = [jnp.where(tile == 0, first, x_refs[0][...])] + [r[...] for r in x_refs[1:]]
        h = jnp.concatenate(blocks, axis=0) if nblk > 1 else blocks[0]
        h_ref[...] = h
        y, _, _ = _rms(h, w_ref[...])
        n_ref[...] = y.astype(BF16)

    x_specs = [pl.BlockSpec((BLK, D_MODEL), functools.partial(
        lambda i, k: (jnp.maximum(nblk * ((i + 1) % ni) + k - 1, 0), 0), k=k)) for k in range(nblk)]
    row = pl.BlockSpec((tm, D_MODEL), lambda i: ((i + 1) % ni, 0))
    h0, n0, _ = pl.pallas_call(
        body, name="embed_norm", grid=(ni,),
        in_specs=x_specs + [_full((1, D_MODEL)), ANY], out_specs=[row, row, ANY],
        out_shape=[jax.ShapeDtypeStruct((t, D_MODEL), F32), jax.ShapeDtypeStruct((t, D_MODEL), BF16),
                   jax.ShapeDtypeStruct(meta_buf.shape, meta_buf.dtype)],
        scratch_shapes=[pltpu.VMEM(meta_buf.shape, meta_buf.dtype), pltpu.SemaphoreType.DMA] + list(gather.scratch),
        input_output_aliases={nblk + 1: 2},
        compiler_params=_cparams(1),
    )(*([x] * nblk), w, meta_buf)
    return h0, n0


FWD_RELATION = (None, 0, 1, 2)


def _ffn_fwd_gather(h, n, wbuf, wpost, qc_idx, late):
    t = h.shape[0]
    tm = _row_tile(t)
    ni = t // tm
    nj, rows3, _ = wbuf.shape
    fj = rows3 // 3
    assert nj == N_CHIPS and ni >= 4
    wbufs = [wbuf]
    nw = 1
    n_lin, n_lout = len(late.inputs), len(late.out_shape)
    wait_step = ni - 3

    def body(qc_ref, h_ref, n_ref, wpost_ref, *rest):
        wb_in = rest[:nw]
        lins = rest[nw:nw + n_lin]
        o0 = nw + n_lin
        hout_ref, p1_ref, p2_ref, a_ref, f_hbm = rest[o0:o0 + 5]
        wb = rest[o0 + 5:o0 + 5 + nw]
        louts = rest[o0 + 5 + nw:o0 + 5 + nw + n_lout]
        s0 = o0 + 5 + nw + n_lout
        wv, wsem, send, recv, fbuf, fr_sem, fw_sem = rest[s0:s0 + 7]
        lscr = rest[s0 + 7:]
        p = pl.program_id(0)
        i = pl.program_id(1)
        step = p * ni + i
        fslot = step % 3
        nslot = (step + 1) % 3

        def f_tile(tile):
            return f_hbm.at[pl.ds(pl.multiple_of(tile * tm, 8), tm)]

        @pl.when(step > 1)
        def _():
            pltpu.make_async_copy(fbuf.at[nslot], f_tile(i), fw_sem.at[nslot]).wait()

        nxt = step + 1

        @pl.when((nxt < N_CHIPS * ni) & (nxt >= ni))
        def _():
            pltpu.make_async_copy(f_tile(nxt % ni), fbuf.at[nslot], fr_sem.at[nslot]).start()

        @pl.when(p > 0)
        def _():
            pltpu.make_async_copy(f_tile(i), fbuf.at[fslot], fr_sem.at[fslot]).wait()
        x, y, c, chips = _place()
        q = 2 * x + y
        sibling = (x, y, 1 - c)
        mine, other = _half(rows3, c), _half(rows3, 1 - c)

        def load(chunk, slot, src):
            return [pltpu.make_async_copy(src[t].at[chunk], wv.at[slot, t], wsem.at[slot, t]) for t in range(nw)]

        @pl.when((p == 0) & (i == 0))
        def _():
            for j, (cx, cy) in enumerate(chips):
                for t in range(nw):
                    _remote(send.at[t, j], recv.at[t, j], wb_in[t].at[q, mine], wb[t].at[q, mine], (cx, cy, c)).start()
            for cp in load(q, 0, wb_in):
                cp.start()
            for cp in load(q, 0, wb_in):
                cp.wait()

        @pl.when((p == 1) & (i == 0))
        def _():
            late.start(lins, louts, lscr)

        for pp in range(1, N_CHIPS):
            j = FWD_RELATION[pp]
            cx, cy = chips[j]
            chunk = 2 * cx + cy

            @pl.when((p == pp - 1) & (i == wait_step))
            def _(j=j, cx=cx, cy=cy, chunk=chunk, pp=pp):
                for t in range(nw):
                    got = wb[t].at[chunk, mine]
                    _remote(send.at[t, j], recv.at[t, j], got, got, (cx, cy, c)).wait_recv()
                    _remote(send.at[t, 3 + j], recv.at[t, 3 + j], got, got, sibling).start()
                for t in range(nw):
                    rest_half = wb[t].at[chunk, other]
                    _remote(send.at[t, 3 + j], recv.at[t, 3 + j], rest_half, rest_half, sibling).wait_recv()
                for cp in load(chunk, pp % 2, wb):
                    cp.start()

            @pl.when((p == pp) & (i == 0))
            def _(chunk=chunk, pp=pp):
                for cp in load(chunk, pp % 2, wb):
                    cp.wait()

        @pl.when((p == N_CHIPS - 1) & (i == ni // 2))
        def _():
            late.mid(lins, louts, lscr)

        slot = p % 2
        nn = n_ref[...]
        g = _dg(nn, wv[slot, 0, 0:fj], NT)
        u = _dg(nn, wv[slot, 0, fj:2 * fj], NT)
        sg = _sigmoid(g)
        silu = g * sg
        p1_ref[...] = (u * (sg + silu * (1.0 - sg))).astype(BF16)
        p2_ref[...] = silu.astype(BF16)
        a = (silu * u).astype(BF16)
        a_ref[...] = a
        part = _dot(a, wv[slot, 0, 2 * fj:3 * fj])

        @pl.when(p == 0)
        def _():
            fbuf[fslot] = part

        @pl.when(p > 0)
        def _():
            fbuf[fslot] = fbuf[fslot] + part

        pltpu.make_async_copy(fbuf.at[fslot], f_tile(i), fw_sem.at[fslot]).start()

        @pl.when(p == N_CHIPS - 1)
        def _():
            yv, _, _ = _rms(fbuf[fslot], wpost_ref[...])
            hout_ref[...] = h_ref[...] + 0.5 * yv

        @pl.when((p == N_CHIPS - 1) & (i == ni - 1))
        def _():
            pslot = (step + 2) % 3
            pltpu.make_async_copy(fbuf.at[pslot], f_tile(i), fw_sem.at[pslot]).wait()
            pltpu.make_async_copy(fbuf.at[fslot], f_tile(i), fw_sem.at[fslot]).wait()
            for t in range(nw):
                for j, (cx, cy) in enumerate(chips):
                    sent = wb[t].at[2 * cx + cy, mine]
                    _remote(send.at[t, j], recv.at[t, j], sent, sent, (cx, cy, c)).wait_send()
                    _remote(send.at[t, 3 + j], recv.at[t, 3 + j], sent, sent, sibling).wait_send()
            late.finish(lins, louts, lscr)

    def last_pass_rows(p, i, qc_ref):
        return (jnp.where(p == N_CHIPS - 1, i, 0), 0)

    def chunk_rows(p, i, qc_ref):
        order = ((p & 1) << 1) | (p >> 1)
        return (jnp.bitwise_xor(qc_ref[0], order), i, 0)

    row = pl.BlockSpec((tm, D_MODEL), lambda p, i, qc_ref: (i, 0))
    last_row = pl.BlockSpec((tm, D_MODEL), last_pass_rows)
    act = pl.BlockSpec((None, tm, fj), chunk_rows)
    act_shape = jax.ShapeDtypeStruct((nj, t, fj), BF16)
    res = pl.pallas_call(
        body, name="ffn_fwd_gather",
        grid_spec=pltpu.PrefetchScalarGridSpec(
            num_scalar_prefetch=1, grid=(N_CHIPS, ni),
            in_specs=[last_row, row, pl.BlockSpec((1, D_MODEL), lambda p, i, qc_ref: (0, 0))]
            + [ANY] * (nw + n_lin),
            out_specs=[last_row, act, act, act, ANY] + [ANY] * (nw + n_lout),
            scratch_shapes=[pltpu.VMEM((2, nw, rows3, D_MODEL), BF16), pltpu.SemaphoreType.DMA((2, nw)),
                            pltpu.SemaphoreType.DMA((nw, 6)), pltpu.SemaphoreType.DMA((nw, 6)),
                            pltpu.VMEM((3, tm, D_MODEL), F32), pltpu.SemaphoreType.DMA((3,)),
                            pltpu.SemaphoreType.DMA((3,))] + list(late.scratch)),
        out_shape=[jax.ShapeDtypeStruct((t, D_MODEL), F32), act_shape, act_shape, act_shape,
                   jax.ShapeDtypeStruct((t, D_MODEL), F32)]
        + [jax.ShapeDtypeStruct(b.shape, b.dtype) for b in wbufs] + list(late.out_shape),
        input_output_aliases={**{4 + t: 5 + t for t in range(nw)},
                              **{4 + nw + a: 5 + nw + b for a, b in late.aliases}},
        compiler_params=_cparams(2),
    )(qc_idx, h, n, wpost, *wbufs, *late.inputs)
    return res[:5], res[5:5 + nw], res[5 + nw:]


PASS_RELATION = (2, 0, 1)


def _ffn_wgrad_reduce(n, df, dg4, du4, a4, qc_idx, hook):
    t = n.shape[0]
    tm = _contract_tile(t)
    ni = t // tm
    nj, _, fj = dg4.shape
    assert nj == N_CHIPS
    hrows = 3 * fj // 2
    n_hin, n_hout = len(hook.inputs), len(hook.out_shape)

    def body(qc_ref, n_ref, df_ref, dg_ref, du_ref, a_ref, *rest):
        hins = rest[:n_hin]
        own_ref, others_ref = rest[n_hin:n_hin + 2]
        houts = rest[n_hin + 2:n_hin + 2 + n_hout]
        s0 = n_hin + 2 + n_hout
        acc, stage, land, sumbuf, px_send, px_recv, cs_send, cs_recv, own_sem = rest[s0:s0 + 9]
        hscr = rest[s0 + 9:]
        k_pass = pl.program_id(0)
        i = pl.program_id(1)
        x, y, c, chips = _place()
        mine = pl.ds(pl.multiple_of(c * hrows, 8), hrows)
        other = pl.ds(pl.multiple_of((1 - c) * hrows, 8), hrows)

        def to_owner(k):
            j = PASS_RELATION[k]
            return _remote(cs_send.at[j], cs_recv.at[j], sumbuf.at[k % 2], others_ref.at[j], (*chips[j], c))

        @pl.when((k_pass == 0) & (i == 0))
        def _():
            hook.start(hins, houts, hscr)

        if hook.has_mid:
            @pl.when((k_pass == N_CHIPS // 2) & (i == 0))
            def _():
                hook.mid(hins, houts, hscr)

        @pl.when(i == 0)
        def _():
            acc[...] = jnp.zeros_like(acc)

        nn = n_ref[...]
        acc[0:fj, :] += _dg(dg_ref[...], nn, TN)
        acc[fj:2 * fj, :] += _dg(du_ref[...], nn, TN)
        acc[2 * fj:3 * fj, :] += _dg(a_ref[...], df_ref[...], TN)

        for k in range(N_CHIPS):
            @pl.when((k_pass == k) & (i == ni - 1))
            def _(k=k):
                slot = k % 2
                stage[...] = acc[other, :].astype(BF16)
                swap = _remote(px_send.at[k], px_recv.at[k], stage, land.at[slot], (x, y, 1 - c))
                swap.start()
                swap.wait_recv()
                pair = acc[mine, :] + land[slot].astype(F32)
                if k >= 2:
                    to_owner(k - 2).wait_send()
                sumbuf[slot] = pair.astype(BF16)
                swap.wait_send()
                if k < N_CHIPS - 1:
                    to_owner(k).start()
                else:
                    keep = pltpu.make_async_copy(sumbuf.at[slot], own_ref, own_sem)
                    keep.start()
                    for j in range(N_CHIPS - 1):
                        _remote(cs_send.at[j], cs_recv.at[j], sumbuf.at[0], others_ref.at[j], (*chips[j], c)).wait_recv()
                    to_owner(k - 1).wait_send()
                    keep.wait()
                    hook.finish(hins, houts, hscr)

    def chunk(k_pass, i, qc_ref):
        return (jnp.bitwise_xor(qc_ref[0], N_CHIPS - 1 - k_pass), i, 0)

    row = pl.BlockSpec((tm, D_MODEL), lambda k_pass, i, qc_ref: (i, 0))
    act = pl.BlockSpec((None, tm, fj), chunk)
    res = pl.pallas_call(
        body, name="ffn_wgrad_reduce",
        grid_spec=pltpu.PrefetchScalarGridSpec(
            num_scalar_prefetch=1, grid=(N_CHIPS, ni),
            in_specs=[row, row, act, act, act] + [ANY] * n_hin,
            out_specs=[ANY, ANY] + [ANY] * n_hout,
            scratch_shapes=[pltpu.VMEM((3 * fj, D_MODEL), F32), pltpu.VMEM((hrows, D_MODEL), BF16),
                            pltpu.VMEM((2, hrows, D_MODEL), BF16), pltpu.VMEM((2, hrows, D_MODEL), BF16),
                            pltpu.SemaphoreType.DMA((N_CHIPS,)), pltpu.SemaphoreType.DMA((N_CHIPS,)),
                            pltpu.SemaphoreType.DMA((N_CHIPS - 1,)), pltpu.SemaphoreType.DMA((N_CHIPS - 1,)),
                            pltpu.SemaphoreType.DMA] + list(hook.scratch)),
        out_shape=[jax.ShapeDtypeStruct((hrows, D_MODEL), BF16),
                   jax.ShapeDtypeStruct((N_CHIPS - 1, hrows, D_MODEL), BF16)] + list(hook.out_shape),
        compiler_params=_cparams(2),
    )(qc_idx, n, df, dg4, du4, a4, *hook.inputs)
    return res[0], res[1], res[2:]


def _xty(x, y):
    t, k = x.shape
    n = y.shape[1]
    tm = _contract_tile(t)
    tn = n if n <= 1024 else (896 if n % 896 == 0 else 128)

    def body(x_ref, y_ref, o_ref):
        @pl.when(pl.program_id(1) == 0)
        def _():
            o_ref[...] = jnp.zeros_like(o_ref)

        o_ref[...] += _dg(x_ref[...], y_ref[...], TN)

    return pl.pallas_call(
        body, name="xty", grid=(n // tn, t // tm),
        in_specs=[pl.BlockSpec((tm, k), lambda j, i: (i, 0)), pl.BlockSpec((tm, tn), lambda j, i: (i, j))],
        out_specs=pl.BlockSpec((k, tn), lambda j, i: (0, j)),
        out_shape=jax.ShapeDtypeStruct((k, n), F32),
        compiler_params=_cparams(2),
    )(x, y)


def _rope_tables(t):
    pos = (jnp.arange(t, dtype=jnp.int32) - PAD).astype(F32)
    inv_freq = 1.0 / (ROPE_THETA ** (jnp.arange(0, SWA_HD, 2, dtype=F32) / SWA_HD))
    ang = pos[:, None] * inv_freq[None, :]
    cos = jnp.cos(ang)
    sin = jnp.sin(ang)
    return jnp.concatenate([cos, cos, cos, cos], axis=1), jnp.concatenate([-sin, sin, -sin, sin], axis=1)


def _rot_half(x, first_half):
    return jnp.where(first_half, pltpu.roll(x, 96, 1), pltpu.roll(x, 32, 1))


def _first_half_mask(rows):
    lane = lax.broadcasted_iota(jnp.int32, (rows, 128), 1)
    return (lane % 64) < 32


def _log_sigmoid(z):
    return jnp.minimum(z, 0.0) - jnp.log(1.0 + jnp.exp(-jnp.abs(z)))


def _mix_proj(h1, wmixpre, winp, wa2p, bap, cos, sin):
    t = h1.shape[0]
    tm = _row_tile(t)

    def body(h_ref, w_ref, win_ref, wa2_ref, ba_ref, cos_ref, sin_ref,
             n_ref, gq_ref, gk_ref, gv_ref, gg_ref, ga_ref, la_ref, sq_ref, sk_ref, sv_ref):
        y, _, _ = _rms(h_ref[...], w_ref[...])
        n = y.astype(BF16)
        n_ref[...] = n
        proj = _dot(n, win_ref[...])
        gq_ref[...] = proj[:, P_GQ:P_GK]
        gk_ref[...] = proj[:, P_GK:P_GV]
        gv_ref[...] = proj[:, P_GV:P_GG]
        gg_ref[...] = proj[:, P_GG:P_GA]
        ga = proj[:, P_GA:P_SQ]
        ga_ref[...] = ga
        z = _dot(ga.astype(BF16), wa2_ref[...]) + ba_ref[...]
        la_ref[...] = _log_sigmoid(z) * (1.0 / GLA_TAU)
        c = cos_ref[...]
        s = sin_ref[...]
        fh = _first_half_mask(tm)
        for k in range(4):
            x = proj[:, P_SQ + 128 * k:P_SQ + 128 * (k + 1)]
            sq_ref[:, 128 * k:128 * (k + 1)] = (x * c + _rot_half(x, fh) * s).astype(BF16)
        for k in range(2):
            x = proj[:, P_SK + 128 * k:P_SK + 128 * (k + 1)]
            sk_ref[:, 128 * k:128 * (k + 1)] = (x * c + _rot_half(x, fh) * s).astype(BF16)
        sv_ref[...] = proj[:, P_SV:P_END].astype(BF16)

    def row(w):
        return pl.BlockSpec((tm, w), lambda i: (i, 0))

    def rshape(w, dt):
        return jax.ShapeDtypeStruct((t, w), dt)

    return pl.pallas_call(
        body, name="mix_proj", grid=(t // tm,),
        in_specs=[row(D_MODEL), _full((1, D_MODEL)), _full((D_MODEL, P_END)), _full((128, GLA_KW)),
                  _full((1, GLA_KW)), row(128), row(128)],
        out_specs=[row(D_MODEL), row(256), row(256), row(512), row(512), row(128), row(256), row(512), row(256),
                   row(256)],
        out_shape=[rshape(D_MODEL, BF16), rshape(256, F32), rshape(256, F32), rshape(512, F32), rshape(512, F32),
                   rshape(128, F32), rshape(256, F32), rshape(512, BF16), rshape(256, BF16), rshape(256, BF16)],
        compiler_params=_cparams(1),
    )(h1, wmixpre, winp, wa2p, bap, cos, sin)


def _scan_rows(x, reverse=False):
    n = x.shape[0]
    row = lax.broadcasted_iota(jnp.int32, x.shape, 0)
    s = 1
    while s < n:
        if reverse:
            x = x + jnp.where(row < n - s, pltpu.roll(x, n - s, 0), 0.0)
        else:
            x = x + jnp.where(row >= s, pltpu.roll(x, s, 0), 0.0)
        s *= 2
    return x


def _gla_cumsum(la, tril_f):
    b = _scan_rows(la)
    row = lax.broadcasted_iota(jnp.int32, b.shape, 0)
    bm = jnp.sum(jnp.where(row == GLA_CHUNK // 2 - 1, b, 0.0), axis=0, keepdims=True)
    bl = jnp.sum(jnp.where(row == GLA_CHUNK - 1, b, 0.0), axis=0, keepdims=True)
    return b, bm, bl


def _gla_decays(la, tril_f):
    b, bm, bl = _gla_cumsum(la, tril_f)
    return jnp.exp(b - bm), jnp.exp(bm - b), jnp.exp(b), jnp.exp(bl - b), jnp.exp(bl)


def _gla_masks():
    c = GLA_CHUNK
    r = lax.broadcasted_iota(jnp.int32, (c, c), 0)
    col = lax.broadcasted_iota(jnp.int32, (c, c), 1)
    r4 = lax.broadcasted_iota(jnp.int32, (GLA_HEADS * c, c), 0) % c
    c4 = lax.broadcasted_iota(jnp.int32, (GLA_HEADS * c, c), 1)
    klane = lax.broadcasted_iota(jnp.int32, (c, GLA_KW), 1) // GLA_DK
    vlane = lax.broadcasted_iota(jnp.int32, (c, GLA_W), 1) // GLA_DV
    srow = lax.broadcasted_iota(jnp.int32, (GLA_W, GLA_KW), 0) // GLA_DV
    scol = lax.broadcasted_iota(jnp.int32, (GLA_W, GLA_KW), 1) // GLA_DK
    return dict(tril_f=(r >= col).astype(F32), triu_f=(r <= col).astype(F32), tril4=r4 >= c4,
                khead=[klane == h for h in range(GLA_HEADS)], vhead=[vlane == h for h in range(GLA_HEADS)],
                diag=srow == scol)


def _stack_heads(x, head_masks):
    return jnp.concatenate([jnp.where(m, x, 0.0) for m in head_masks], axis=0)


def _gla_fwd(gq, gk, gv, la):
    t = gq.shape[0]
    rg = _seq_tile(t)
    nb = t // rg
    ncb = rg // GLA_CHUNK
    c = GLA_CHUNK

    def body(q_ref, k_ref, v_ref, la_ref, o_ref, ss_ref, st_ref):
        @pl.when(pl.program_id(0) == 0)
        def _():
            st_ref[...] = jnp.zeros_like(st_ref)

        mk = _gla_masks()
        st = st_ref[...]
        for ch in range(ncb):
            rows = slice(ch * c, (ch + 1) * c)
            eq, ek, eb, ekl, ebl = _gla_decays(la_ref[rows, :], mk["tril_f"])
            qs = q_ref[rows, :] * (GLA_DK ** -0.5)
            k = k_ref[rows, :]
            v = v_ref[rows, :].astype(BF16)
            ss_ref[ch] = st
            q4 = _stack_heads(qs * eq, mk["khead"]).astype(BF16)
            a4 = jnp.where(mk["tril4"], _dg(q4, (k * ek).astype(BF16), NT), 0.0).astype(BF16)
            r4 = _dot(a4, v)
            intra = jnp.concatenate([r4[h * c:(h + 1) * c, GLA_DV * h:GLA_DV * (h + 1)] for h in range(GLA_HEADS)],
                                    axis=1)
            o_ref[rows, :] = intra + _dg((qs * eb).astype(BF16), st.astype(BF16), NT)
            st = st * ebl + jnp.where(mk["diag"], _dg(v, (k * ekl).astype(BF16), TN), 0.0)
        st_ref[...] = st

    def row(w):
        return pl.BlockSpec((rg, w), lambda i: (i, 0))

    return pl.pallas_call(
        body, name="gla_fwd", grid=(nb,),
        in_specs=[row(256), row(256), row(512), row(256)],
        out_specs=[row(512), pl.BlockSpec((ncb, GLA_W, GLA_KW), lambda i: (i, 0, 0))],
        out_shape=[jax.ShapeDtypeStruct((t, GLA_W), F32), jax.ShapeDtypeStruct((nb * ncb, GLA_W, GLA_KW), F32)],
        scratch_shapes=[pltpu.VMEM((GLA_W, GLA_KW), F32)],
        compiler_params=_cparams(1),
    )(gq, gk, gv, la)


def _gla_bwd(gq, gk, gv, la, ss, do):
    t = gq.shape[0]
    rg = _seq_tile(t)
    nb = t // rg
    ncb = rg // GLA_CHUNK
    c = GLA_CHUNK

    def body(q_ref, k_ref, v_ref, la_ref, ss_ref, do_ref, dq_ref, dk_ref, dv_ref, dla_ref, dst_ref):
        @pl.when(pl.program_id(0) == 0)
        def _():
            dst_ref[...] = jnp.zeros_like(dst_ref)

        mk = _gla_masks()
        last_row = lax.broadcasted_iota(jnp.int32, (c, GLA_KW), 0) == c - 1
        scale = GLA_DK ** -0.5
        dstn = dst_ref[...]
        for ch in reversed(range(ncb)):
            rows = slice(ch * c, (ch + 1) * c)
            eq, ek, eb, ekl, ebl = _gla_decays(la_ref[rows, :], mk["tril_f"])
            qs = q_ref[rows, :] * scale
            k = k_ref[rows, :]
            qt, kt, qh, kh = qs * eq, k * ek, qs * eb, k * ekl
            ktb, khb, qhb = kt.astype(BF16), kh.astype(BF16), qh.astype(BF16)
            v = v_ref[rows, :].astype(BF16)
            do_f = do_ref[rows, :]
            dob = do_f.astype(BF16)
            st = ss_ref[ch]
            stb = st.astype(BF16)
            dstb = dstn.astype(BF16)
            q4 = _stack_heads(qt, mk["khead"]).astype(BF16)
            do4 = _stack_heads(do_f, mk["vhead"]).astype(BF16)
            a4 = jnp.where(mk["tril4"], _dg(q4, ktb, NT), 0.0).astype(BF16)
            da4 = jnp.where(mk["tril4"], _dg(do4, v, NT), 0.0).astype(BF16)
            dv_ref[rows, :] = _dg(a4, do4, TN) + _dg(khb, dstb, NT)
            dq4 = _dot(da4, ktb)
            dqt = jnp.zeros((c, GLA_KW), F32)
            for h in range(GLA_HEADS):
                dqt = dqt + jnp.where(mk["khead"][h], dq4[h * c:(h + 1) * c], 0.0)
            dkt = _dg(da4, q4, TN)
            dqh = _dot(dob, stb)
            dkh = _dot(v, dstb)
            dbl = jnp.sum(dstn * st, axis=0, keepdims=True)
            dstn = dstn * ebl + jnp.where(mk["diag"], _dg(dob, qhb, TN), 0.0)
            dq_ref[rows, :] = scale * (dqt * eq + dqh * eb)
            dk_ref[rows, :] = dkt * ek + dkh * ekl
            dkk = dkh * kh
            db = dqt * qt - dkt * kt + dqh * qh - dkk
            db = db + jnp.where(last_row, jnp.sum(dkk, axis=0, keepdims=True) + ebl * dbl, 0.0)
            dla_ref[rows, :] = _scan_rows(db, reverse=True)
        dst_ref[...] = dstn

    def row(w):
        return pl.BlockSpec((rg, w), lambda i: (nb - 1 - i, 0))

    def rshape(w):
        return jax.ShapeDtypeStruct((t, w), F32)

    return pl.pallas_call(
        body, name="gla_bwd", grid=(nb,),
        in_specs=[row(256), row(256), row(512), row(256),
                  pl.BlockSpec((ncb, GLA_W, GLA_KW), lambda i: (nb - 1 - i, 0, 0)), row(512)],
        out_specs=[row(256), row(256), row(512), row(256)],
        out_shape=[rshape(256), rshape(256), rshape(512), rshape(256)],
        scratch_shapes=[pltpu.VMEM((GLA_W, GLA_KW), F32)],
        compiler_params=_cparams(1),
    )(gq, gk, gv, la, ss, do)


SWA_G = SWA_QH // SWA_KVH


def _swa_bias():
    n = jnp.arange(3, dtype=jnp.int32)[:, None, None]
    r = (jnp.arange(SWA_G * BLK, dtype=jnp.int32) % BLK)[None, :, None]
    c = jnp.arange(3 * BLK, dtype=jnp.int32)[None, None, :]
    seg = c // BLK
    cc = c % BLK
    qpos = n * BLK + r - PAD
    kpos = jnp.where(seg == 0, (n - 1) * BLK, jnp.where(seg == 1, n * BLK, 0)) + cc - PAD
    band = (seg < 2) & (kpos >= N_META) & (kpos <= qpos) & (qpos - kpos < WINDOW)
    meta = (seg == 2) & (kpos >= 0) & (kpos < N_META) & (kpos <= qpos)
    return jnp.where(band | meta, 0.0, NEG_INF).astype(F32)


def _swa_stack(ref, rows, kh, lo, dtype):
    parts = []
    for g in range(2):
        pair = ref[rows, 128 * (2 * kh + g):128 * (2 * kh + g + 1)]
        zero = jnp.zeros_like(pair)
        parts += [jnp.where(lo, pair, zero), jnp.where(lo, zero, pair)]
    return jnp.concatenate(parts, axis=0).astype(dtype)


def _swa_unstack(x4, lo):
    return [jnp.where(lo, x4[2 * g * BLK:(2 * g + 1) * BLK], x4[(2 * g + 1) * BLK:(2 * g + 2) * BLK])
            for g in range(2)]


def _swa_sink_col(sink_ref, kh):
    blk = lax.broadcasted_iota(jnp.int32, (SWA_G * BLK, 1), 0) // BLK
    col = jnp.full((SWA_G * BLK, 1), sink_ref[SWA_G * kh + SWA_G - 1], F32)
    for e in reversed(range(SWA_G - 1)):
        col = jnp.where(blk == e, sink_ref[SWA_G * kh + e], col)
    return col


def _swa_softmax(qk, bias, sink):
    s = qk * (SWA_HD ** -0.5) + bias
    m = jnp.maximum(jnp.max(s, axis=-1, keepdims=True), sink)
    p = jnp.exp(s - m)
    es = jnp.exp(sink - m)
    inv = 1.0 / (jnp.sum(p, axis=-1, keepdims=True) + es)
    return p * inv, es * inv


def _swa_keys(prev_ref, cur_ref, first_ref, b, ls):
    before = prev_ref[:, ls] if b == 0 else cur_ref[(b - 1) * BLK:b * BLK, ls]
    return jnp.concatenate([before, cur_ref[b * BLK:(b + 1) * BLK, ls], first_ref[:, ls]], axis=0)


def _swa_specs(rs, ns):
    bps = rs // BLK
    cur = lambda w: pl.BlockSpec((rs, w), lambda i: (jnp.minimum(i, ns - 1), 0))
    prev = lambda w: pl.BlockSpec((BLK, w), lambda i: (jnp.maximum(jnp.minimum(i, ns - 1) * bps - 1, 0), 0))
    first = lambda w: pl.BlockSpec((BLK, w), lambda i: (0, 0))
    return cur, prev, first


def _swa_fwd(sinks, sq, sk, sv):
    t = sq.shape[0]
    rs = _seq_tile(t)
    bps, ns = rs // BLK, t // rs

    def body(sink_ref, bias_ref, q_ref, kp_ref, kc_ref, km_ref, vp_ref, vc_ref, vm_ref, o_ref):
        i = pl.program_id(0)
        lo = lax.broadcasted_iota(jnp.int32, (BLK, 128), 1) < 64
        sink_cols = [_swa_sink_col(sink_ref, kh) for kh in range(SWA_KVH)]
        chains = [(b, kh) for b in range(bps) for kh in range(SWA_KVH)]
        scores = []
        for b, kh in chains:
            ls = slice(128 * kh, 128 * (kh + 1))
            q4 = _swa_stack(q_ref, slice(b * BLK, (b + 1) * BLK), kh, lo, BF16)
            scores.append(_dg(q4, _swa_keys(kp_ref, kc_ref, km_ref, b, ls), NT))
        probs = []
        for (b, kh), s in zip(chains, scores):
            p, _ = _swa_softmax(s, bias_ref[jnp.minimum(i * bps + b, 2)], sink_cols[kh])
            probs.append(p.astype(BF16))
        for (b, kh), p in zip(chains, probs):
            ls = slice(128 * kh, 128 * (kh + 1))
            rows = slice(b * BLK, (b + 1) * BLK)
            for g, pair in enumerate(_swa_unstack(_dot(p, _swa_keys(vp_ref, vc_ref, vm_ref, b, ls)), lo)):
                o_ref[rows, 128 * (2 * kh + g):128 * (2 * kh + g + 1)] = pair

    cur, prev, first = _swa_specs(rs, ns)
    bias = _swa_bias()
    return pl.pallas_call(
        body, name="swa_fwd", grid=(ns,),
        in_specs=[pl.BlockSpec(memory_space=pltpu.SMEM), _full(bias.shape), cur(512), prev(256), cur(256), first(256),
                  prev(256), cur(256), first(256)],
        out_specs=cur(512),
        out_shape=jax.ShapeDtypeStruct((t, SWA_W), F32),
        compiler_params=_cparams(1),
    )(sinks, bias, sq, sk, sk, sk, sv, sv, sv)


def _swa_bwd(sinks, sq, sk, sv, o, do, hook=None):
    t = sq.shape[0]
    rs = _seq_tile(t)
    bps, ns = rs // BLK, t // rs

    def body(sink_ref, bias_ref, q_ref, kp_ref, kc_ref, km_ref, vp_ref, vc_ref, vm_ref, o_ref, do_ref,
             dq_ref, dk_ref, dv_ref, dkm_ref, dvm_ref, dsink_ref, pk_ref, pv_ref):
        i = pl.program_id(0)

        @pl.when(i == 0)
        def _():
            pk_ref[...] = jnp.zeros_like(pk_ref)
            pv_ref[...] = jnp.zeros_like(pv_ref)
            dkm_ref[...] = jnp.zeros_like(dkm_ref)
            dvm_ref[...] = jnp.zeros_like(dvm_ref)
            dsink_ref[...] = jnp.zeros_like(dsink_ref)

        @pl.when(i == ns)
        def _():
            dk_ref[...] = pk_ref[...]
            dv_ref[...] = pv_ref[...]

        @pl.when(i < ns)
        def _():
            lo = lax.broadcasted_iota(jnp.int32, (BLK, 128), 1) < 64
            scale = SWA_HD ** -0.5
            sink_cols = [_swa_sink_col(sink_ref, kh) for kh in range(SWA_KVH)]
            parts_k = [[None] * SWA_KVH for _ in range(bps)]
            parts_v = [[None] * SWA_KVH for _ in range(bps)]
            dsinks = [jnp.zeros((1, 1), F32) for _ in range(SWA_QH)]
            chains = [(b, kh) for b in range(bps) for kh in range(SWA_KVH)]
            lanes = lambda kh: slice(128 * kh, 128 * (kh + 1))
            block = lambda b: slice(b * BLK, (b + 1) * BLK)
            q4s = [_swa_stack(q_ref, block(b), kh, lo, BF16) for b, kh in chains]
            scores = [_dg(q4, _swa_keys(kp_ref, kc_ref, km_ref, b, lanes(kh)), NT)
                      for (b, kh), q4 in zip(chains, q4s)]
            do4s = [_swa_stack(do_ref, block(b), kh, lo, F32) for b, kh in chains]
            do4bs = [d.astype(BF16) for d in do4s]
            dps = [_dg(d, _swa_keys(vp_ref, vc_ref, vm_ref, b, lanes(kh)), NT) for (b, kh), d in zip(chains, do4bs)]
            pbs, dss = [], []
            for n_chain, (b, kh) in enumerate(chains):
                p, psink = _swa_softmax(scores[n_chain], bias_ref[jnp.minimum(i * bps + b, 2)], sink_cols[kh])
                delta = jnp.sum(do4s[n_chain] * _swa_stack(o_ref, block(b), kh, lo, F32), axis=-1, keepdims=True)
                dss.append((p * (dps[n_chain] - delta) * scale).astype(BF16))
                pbs.append(p.astype(BF16))
                dsk = psink * delta
                for e in range(SWA_G):
                    h = SWA_G * kh + e
                    dsinks[h] = dsinks[h] - jnp.sum(dsk[e * BLK:(e + 1) * BLK], axis=0, keepdims=True)
            for n_chain, (b, kh) in enumerate(chains):
                kall = _swa_keys(kp_ref, kc_ref, km_ref, b, lanes(kh))
                for g, pair in enumerate(_swa_unstack(_dot(dss[n_chain], kall), lo)):
                    dq_ref[block(b), 128 * (2 * kh + g):128 * (2 * kh + g + 1)] = pair
                parts_k[b][kh] = _dg(dss[n_chain], q4s[n_chain], TN)
                parts_v[b][kh] = _dg(pbs[n_chain], do4bs[n_chain], TN)
            last = slice(rs - BLK, rs)
            for parts, out_ref, pend_ref, meta_ref in ((parts_k, dk_ref, pk_ref, dkm_ref),
                                                       (parts_v, dv_ref, pv_ref, dvm_ref)):
                for kh in range(SWA_KVH):
                    ls = slice(128 * kh, 128 * (kh + 1))
                    if bps > 1:
                        out_ref[0:rs - BLK, ls] = pend_ref[0:rs - BLK, ls]
                    out_ref[last, ls] = pend_ref[last, ls] + parts[0][kh][0:BLK]
                    meta = parts[0][kh][2 * BLK:3 * BLK]
                    for b in range(bps):
                        own = parts[b][kh][BLK:2 * BLK]
                        if b + 1 < bps:
                            own = own + parts[b + 1][kh][0:BLK]
                            meta = meta + parts[b + 1][kh][2 * BLK:3 * BLK]
                        pend_ref[b * BLK:(b + 1) * BLK, ls] = own
                    meta_ref[:, ls] += meta
            for h in range(SWA_QH):
                dsink_ref[h:h + 1, :] += jnp.broadcast_to(dsinks[h], (1, 128))

    cur, prev, first = _swa_specs(rs, ns)
    late = lambda w: pl.BlockSpec((rs, w), lambda i: (jnp.maximum(i - 1, 0), 0))
    bias = _swa_bias()
    return _pallas(
        body, name="swa_bwd", grid=(ns + 1,),
        in_specs=[pl.BlockSpec(memory_space=pltpu.SMEM), _full(bias.shape), cur(512), prev(256), cur(256), first(256),
                  prev(256), cur(256), first(256), cur(512), cur(512)],
        out_specs=[cur(512), late(256), late(256), first(256), first(256), _full((SWA_QH, 128))],
        out_shape=[jax.ShapeDtypeStruct((t, SWA_W), F32), jax.ShapeDtypeStruct((t, 256), F32),
                   jax.ShapeDtypeStruct((t, 256), F32), jax.ShapeDtypeStruct((BLK, 256), F32),
                   jax.ShapeDtypeStruct((BLK, 256), F32), jax.ShapeDtypeStruct((SWA_QH, 128), F32)],
        scratch_shapes=[pltpu.VMEM((rs, 256), F32), pltpu.VMEM((rs, 256), F32)],
        args=(sinks, bias, sq, sk, sk, sk, sv, sv, sv, o, do), hook=hook)


def _mix_out(h1, ogla, gg, oswa, wgn, wsn, wout, wpost):
    t = h1.shape[0]
    tm = _row_tile(t)

    def body(h_ref, og_ref, gg_ref, os_ref, wgn_ref, wsn_ref, wout_ref, wpost_ref, h2_ref, cat_ref, m_ref):
        parts = []
        for h in range(GLA_HEADS):
            ls = slice(GLA_DV * h, GLA_DV * (h + 1))
            y, _, _ = _rms(og_ref[:, ls], wgn_ref[...])
            g = gg_ref[:, ls]
            parts.append(y * (g * _sigmoid(g)))
        ys, _, _ = _rms(os_ref[...], wsn_ref[...])
        cat = jnp.concatenate(parts + [ys], axis=1).astype(BF16)
        cat_ref[...] = cat
        m = _dot(cat, wout_ref[...])
        m_ref[...] = m
        y, _, _ = _rms(m, wpost_ref[...])
        h2_ref[...] = h_ref[...] + y

    def row(w):
        return pl.BlockSpec((tm, w), lambda i: (i, 0))

    return pl.pallas_call(
        body, name="mix_out", grid=(t // tm,),
        in_specs=[row(D_MODEL), row(512), row(512), row(512), _full((1, GLA_DV)), _full((1, SWA_W)),
                  _full((D_MODEL, D_MODEL)), _full((1, D_MODEL))],
        out_specs=[row(D_MODEL), row(D_MODEL), row(D_MODEL)],
        out_shape=[jax.ShapeDtypeStruct((t, D_MODEL), F32), jax.ShapeDtypeStruct((t, D_MODEL), BF16),
                   jax.ShapeDtypeStruct((t, D_MODEL), F32)],
        compiler_params=_cparams(1),
    )(h1, ogla, gg, oswa, wgn, wsn, wout, wpost)


def _mix_out_bwd(dh2, m, ogla, gg, oswa, wgn, wsn, wout, wpost, hook=None):
    t = dh2.shape[0]
    tm = _row_tile(t)

    def body(dh_ref, m_ref, og_ref, gg_ref, os_ref, wgn_ref, wsn_ref, wout_ref, wpost_ref,
             dog_ref, dgg_ref, dos_ref, dm_ref, dwpost_ref, dwgn_ref, dwsn_ref):
        @pl.when(pl.program_id(0) == 0)
        def _():
            dwpost_ref[...] = jnp.zeros_like(dwpost_ref)
            dwgn_ref[...] = jnp.zeros_like(dwgn_ref)
            dwsn_ref[...] = jnp.zeros_like(dwsn_ref)

        wpost = wpost_ref[...]
        _, mh, r = _rms(m_ref[...], wpost)
        dm, dw = _rms_bwd(mh, r, wpost, dh_ref[...])
        dwpost_ref[...] += dw
        dmb = dm.astype(BF16)
        dm_ref[...] = dmb
        dcat = _dg(dmb, wout_ref[...], NT)
        wgn = wgn_ref[...]
        for h in range(GLA_HEADS):
            ls = slice(GLA_DV * h, GLA_DV * (h + 1))
            dog = dcat[:, ls]
            g = gg_ref[:, ls]
            sg = _sigmoid(g)
            y, xh, r = _rms(og_ref[:, ls], wgn)
            dgg_ref[:, ls] = dog * y * (sg * (1.0 + g * (1.0 - sg)))
            dx, dw = _rms_bwd(xh, r, wgn, dog * (g * sg))
            dog_ref[:, ls] = dx
            dwgn_ref[...] += dw
        wsn = wsn_ref[...]
        _, xh, r = _rms(os_ref[...], wsn)
        dx, dw = _rms_bwd(xh, r, wsn, dcat[:, GLA_W:])
        dos_ref[...] = dx
        dwsn_ref[...] += dw

    def row(w):
        return pl.BlockSpec((tm, w), lambda i: (i, 0))

    def rshape(w, dt=F32):
        return jax.ShapeDtypeStruct((t, w), dt)

    return _pallas(
        body, name="mix_out_bwd", grid=(t // tm,),
        in_specs=[row(D_MODEL), row(D_MODEL), row(512), row(512), row(512), _full((1, GLA_DV)), _full((1, SWA_W)),
                  _full((D_MODEL, D_MODEL)), _full((1, D_MODEL))],
        out_specs=[row(512), row(512), row(512), row(D_MODEL), _full((1, D_MODEL)), _full((1, GLA_DV)),
                   _full((1, SWA_W))],
        out_shape=[rshape(512), rshape(512), rshape(512), rshape(D_MODEL, BF16),
                   jax.ShapeDtypeStruct((1, D_MODEL), F32), jax.ShapeDtypeStruct((1, GLA_DV), F32),
                   jax.ShapeDtypeStruct((1, SWA_W), F32)],
        args=(dh2, m, ogla, gg, oswa, wgn, wsn, wout, wpost), hook=hook)


def _mix_in_bwd(dh2, h1, wmixpre, winp, wa2p, bap, cos, sin, ga, dgq, dgk, dgv, dgg, dla, dsq, dsk, dsv, dkm, dvm):
    t = h1.shape[0]
    tm = _row_tile(t)

    def body(dh2_ref, h_ref, w_ref, win_ref, wa2_ref, ba_ref, cos_ref, sin_ref, ga_ref, dgq_ref, dgk_ref, dgv_ref,
             dgg_ref, dla_ref, dsq_ref, dsk_ref, dsv_ref, dkm_ref, dvm_ref,
             dh1_ref, dproj_ref, dw_ref, dwa2_ref, dba_ref):
        i = pl.program_id(0)

        @pl.when(i == 0)
        def _():
            dw_ref[...] = jnp.zeros_like(dw_ref)
            dwa2_ref[...] = jnp.zeros_like(dwa2_ref)
            dba_ref[...] = jnp.zeros_like(dba_ref)

        first = (i == 0).astype(F32)
        c = cos_ref[...]
        s = -sin_ref[...]
        fh = _first_half_mask(tm)
        dproj_ref[:, P_GQ:P_GK] = dgq_ref[...].astype(BF16)
        dproj_ref[:, P_GK:P_GV] = dgk_ref[...].astype(BF16)
        dproj_ref[:, P_GV:P_GG] = dgv_ref[...].astype(BF16)
        dproj_ref[:, P_GG:P_GA] = dgg_ref[...].astype(BF16)
        gab = ga_ref[...].astype(BF16)
        z = _dot(gab, wa2_ref[...]) + ba_ref[...]
        row_id = i * tm + lax.broadcasted_iota(jnp.int32, (tm, 1), 0)
        dz = jnp.where(row_id >= PAD, dla_ref[...] * (1.0 / GLA_TAU) * (1.0 - _sigmoid(z)), 0.0)
        dzb = dz.astype(BF16)
        dba_ref[...] += jnp.sum(dz, axis=0, keepdims=True)
        dwa2_ref[...] += _dg(gab, dzb, TN)
        dproj_ref[:, P_GA:P_SQ] = _dg(dzb, wa2_ref[...], NT).astype(BF16)
        for k in range(4):
            dy = dsq_ref[:, 128 * k:128 * (k + 1)]
            dproj_ref[:, P_SQ + 128 * k:P_SQ + 128 * (k + 1)] = (dy * c + _rot_half(dy, fh) * s).astype(BF16)
        for k in range(2):
            ls = slice(128 * k, 128 * (k + 1))
            dy = dsk_ref[:, ls]
            dy = jnp.concatenate([dy[:BLK] + first * dkm_ref[:, ls], dy[BLK:]], axis=0) if tm > BLK else (
                dy + first * dkm_ref[:, ls])
            dproj_ref[:, P_SK + 128 * k:P_SK + 128 * (k + 1)] = (dy * c + _rot_half(dy, fh) * s).astype(BF16)
            dv = dsv_ref[:, ls]
            dv = jnp.concatenate([dv[:BLK] + first * dvm_ref[:, ls], dv[BLK:]], axis=0) if tm > BLK else (
                dv + first * dvm_ref[:, ls])
            dproj_ref[:, P_SV + 128 * k:P_SV + 128 * (k + 1)] = dv.astype(BF16)
        dn = _dg(dproj_ref[...], win_ref[...], NT)
        w = w_ref[...]
        _, hh, r = _rms(h_ref[...], w)
        dx, dw = _rms_bwd(hh, r, w, dn)
        dw_ref[...] += dw
        dh1_ref[...] = dh2_ref[...] + dx

    def row(w):
        return pl.BlockSpec((tm, w), lambda i: (i, 0))

    return pl.pallas_call(
        body, name="mix_in_bwd", grid=(t // tm,),
        in_specs=[row(D_MODEL), row(D_MODEL), _full((1, D_MODEL)), _full((D_MODEL, P_END)), _full((128, GLA_KW)),
                  _full((1, GLA_KW)), row(128), row(128), row(128), row(256), row(256), row(512), row(512), row(256),
                  row(512), row(256), row(256), _full((BLK, 256)), _full((BLK, 256))],
        out_specs=[row(D_MODEL), row(P_END), _full((1, D_MODEL)), _full((128, GLA_KW)), _full((1, GLA_KW))],
        out_shape=[jax.ShapeDtypeStruct((t, D_MODEL), F32), jax.ShapeDtypeStruct((t, P_END), BF16),
                   jax.ShapeDtypeStruct((1, D_MODEL), F32), jax.ShapeDtypeStruct((128, GLA_KW), F32),
                   jax.ShapeDtypeStruct((1, GLA_KW), F32)],
        compiler_params=_cparams(1),
    )(dh2, h1, wmixpre, winp, wa2p, bap, cos, sin, ga, dgq, dgk, dgv, dgg, dla, dsq, dsk, dsv, dkm, dvm)


def _adamw_update(w, g, m, v):
    m = ADAM_B1 * m + (1.0 - ADAM_B1) * g
    v = ADAM_B2 * v + (1.0 - ADAM_B2) * (g * g)
    m_hat = m / (1.0 - ADAM_B1 ** ADAM_STEP)
    v_hat = v / (1.0 - ADAM_B2 ** ADAM_STEP)
    return -ADAM_LR * (m_hat / (jnp.sqrt(v_hat) + ADAM_EPS) + ADAM_WD * w), m, v


def _adamw_halves(w, g_mine, g_other, m, v, c_idx, row0=0):
    r, c = w.shape
    h = g_mine.shape[0]
    tr = _div_tile(math.gcd(r, h))
    nth = h // tr
    t0 = row0 // tr
    assert t0 * tr == row0

    def body(c_ref, w_ref, gm_ref, go_ref, m_ref, v_ref, g_ref, d_ref, nm_ref, nv_ref):
        hh = (t0 + pl.program_id(0)) // nth
        g = jnp.where(hh == c_ref[0], gm_ref[...], go_ref[...])
        g_ref[...] = g
        d_ref[...], nm_ref[...], nv_ref[...] = _adamw_update(w_ref[...], g, m_ref[...], v_ref[...])

    spec = pl.BlockSpec((tr, c), lambda i, c_ref: (i, 0))

    def gspec(is_mine):
        def index(i, c_ref):
            used = ((t0 + i) // nth == c_ref[0]) == is_mine
            return (jnp.where(used, (t0 + i) % nth, 0), 0)
        return pl.BlockSpec((tr, c), index)

    shape = jax.ShapeDtypeStruct((r, c), F32)
    return pl.pallas_call(
        body, name="adamw_halves",
        grid_spec=pltpu.PrefetchScalarGridSpec(
            num_scalar_prefetch=1, grid=(r // tr,), in_specs=[spec, gspec(True), gspec(False), spec, spec],
            out_specs=[spec] * 4),
        out_shape=[shape] * 4, compiler_params=_cparams(1),
    )(c_idx, w, g_mine, g_other, m, v)


def _place():
    x, y, c = lax.axis_index("x"), lax.axis_index("y"), lax.axis_index("c")
    chips = [(1 - x, y), (x, 1 - y), (1 - x, 1 - y)]
    return x, y, c, chips


def _remote(send_sem, recv_sem, src, dst, to):
    return pltpu.make_async_remote_copy(src_ref=src, dst_ref=dst, send_sem=send_sem, recv_sem=recv_sem,
                                        device_id=to, device_id_type=MESH)


def _half(ref_rows, c):
    h = ref_rows // 2
    return pl.ds(pl.multiple_of(c * h, 8), h)


def _own_slot(shard, q):
    return lax.dynamic_update_slice(jnp.zeros((N_CHIPS,) + shard.shape, shard.dtype), shard[None], (q, 0, 0))


class _GatherChips:
    has_mid = True

    def __init__(self, bufs):
        n = len(bufs)
        self.inputs = list(bufs)
        self.out_shape = [jax.ShapeDtypeStruct(b.shape, b.dtype) for b in bufs]
        self.aliases = [(t, t) for t in range(n)]
        self.scratch = [pltpu.SemaphoreType.DMA((n, 6)), pltpu.SemaphoreType.DMA((n, 6))]

    def start(self, ins, outs, scr):
        send, recv = scr
        x, y, c, chips = _place()
        q = 2 * x + y
        for t, (i_ref, o_ref) in enumerate(zip(ins, outs)):
            rows = _half(i_ref.shape[1], c)
            for j, (cx, cy) in enumerate(chips):
                _remote(send.at[t, j], recv.at[t, j], i_ref.at[q, rows], o_ref.at[q, rows], (cx, cy, c)).start()

    def mid(self, ins, outs, scr):
        send, recv = scr
        x, y, c, chips = _place()
        for t, o_ref in enumerate(outs):
            rows = _half(o_ref.shape[1], c)
            for j, (cx, cy) in enumerate(chips):
                slot = o_ref.at[2 * cx + cy, rows]
                _remote(send.at[t, j], recv.at[t, j], slot, slot, (cx, cy, c)).wait_recv()
                _remote(send.at[t, 3 + j], recv.at[t, 3 + j], slot, slot, (x, y, 1 - c)).start()

    def finish(self, ins, outs, scr):
        send, recv = scr
        x, y, c, chips = _place()
        for t, o_ref in enumerate(outs):
            mine, other = _half(o_ref.shape[1], c), _half(o_ref.shape[1], 1 - c)
            for j, (cx, cy) in enumerate(chips):
                slot = o_ref.at[2 * cx + cy, other]
                _remote(send.at[t, 3 + j], recv.at[t, 3 + j], slot, slot, (x, y, 1 - c)).wait_recv()
            for j, (cx, cy) in enumerate(chips):
                sent = o_ref.at[2 * cx + cy, mine]
                _remote(send.at[t, j], recv.at[t, j], sent, sent, (cx, cy, c)).wait_send()
                _remote(send.at[t, 3 + j], recv.at[t, 3 + j], sent, sent, (x, y, 1 - c)).wait_send()


class _PairExchange:
    has_mid = False
    aliases = ()

    def __init__(self, arrs):
        n = len(arrs)
        self.inputs = list(arrs)
        self.out_shape = [jax.ShapeDtypeStruct((a.shape[0], a.shape[1] // 2, a.shape[2]), a.dtype) for a in arrs]
        self.scratch = [pltpu.SemaphoreType.DMA((n,)), pltpu.SemaphoreType.DMA((n,))]

    def _copies(self, ins, outs, scr):
        send, recv = scr
        x, y, c, _ = _place()
        return [_remote(send.at[t], recv.at[t], i_ref.at[:, _half(i_ref.shape[1], 1 - c)], o_ref, (x, y, 1 - c))
                for t, (i_ref, o_ref) in enumerate(zip(ins, outs))]

    def start(self, ins, outs, scr):
        for cp in self._copies(ins, outs, scr):
            cp.start()

    def finish(self, ins, outs, scr):
        for cp in self._copies(ins, outs, scr):
            cp.wait()


class _ChipScatter:
    has_mid = False
    aliases = ()

    def __init__(self, arrs):
        n = len(arrs)
        self.inputs = list(arrs)
        self.out_shape = [jax.ShapeDtypeStruct((3,) + a.shape[1:], a.dtype) for a in arrs]
        self.scratch = [pltpu.SemaphoreType.DMA((n, 3)), pltpu.SemaphoreType.DMA((n, 3))]

    def _copies(self, ins, outs, scr):
        send, recv = scr
        x, y, c, chips = _place()
        return [_remote(send.at[t, j], recv.at[t, j], i_ref.at[2 * cx + cy], o_ref.at[j], (cx, cy, c))
                for t, (i_ref, o_ref) in enumerate(zip(ins, outs)) for j, (cx, cy) in enumerate(chips)]

    def start(self, ins, outs, scr):
        for cp in self._copies(ins, outs, scr):
            cp.start()

    def finish(self, ins, outs, scr):
        for cp in self._copies(ins, outs, scr):
            cp.wait()


class _PairShare:
    has_mid = False
    aliases = ()

    def __init__(self, arrs):
        n = len(arrs)
        self.inputs = list(arrs)
        self.out_shape = [jax.ShapeDtypeStruct(a.shape, a.dtype) for a in arrs]
        self.scratch = [pltpu.SemaphoreType.DMA((n,)), pltpu.SemaphoreType.DMA((n,))]

    def _copies(self, ins, outs, scr):
        send, recv = scr
        x, y, c, _ = _place()
        return [_remote(send.at[t], recv.at[t], i_ref, o_ref, (x, y, 1 - c))
                for t, (i_ref, o_ref) in enumerate(zip(ins, outs))]

    def start(self, ins, outs, scr):
        for cp in self._copies(ins, outs, scr):
            cp.start()

    def finish(self, ins, outs, scr):
        for cp in self._copies(ins, outs, scr):
            cp.wait()


def _comm_call(hook, name):
    n_in, n_out = len(hook.inputs), len(hook.out_shape)

    def body(*refs):
        ins, outs, scr = refs[:n_in], refs[n_in:n_in + n_out], refs[n_in + n_out:]
        hook.start(ins, outs, scr)
        if hook.has_mid:
            hook.mid(ins, outs, scr)
        hook.finish(ins, outs, scr)

    return pl.pallas_call(body, name=name, in_specs=[ANY] * n_in, out_specs=[ANY] * n_out,
                          out_shape=list(hook.out_shape), scratch_shapes=list(hook.scratch),
                          input_output_aliases=dict(hook.aliases))(*hook.inputs)


class _GatherDevices:
    has_mid = True
    aliases = ()

    def __init__(self, vecs):
        n = len(vecs)
        self.inputs = list(vecs)
        self.out_shape = [jax.ShapeDtypeStruct((N_DEV,) + v.shape, v.dtype) for v in vecs]
        self.scratch = [pltpu.SemaphoreType.DMA((n, 7)), pltpu.SemaphoreType.DMA((n, 7)),
                        pltpu.SemaphoreType.DMA((n,))]

    @staticmethod
    def _copy(scr, t, k, out_ref, block, to, src=None):
        send, recv, _ = scr
        px, py, pc = block
        slot = out_ref.at[4 * px + 2 * py + pc]
        return _remote(send.at[t, k], recv.at[t, k], slot if src is None else src, slot, to)

    def start(self, ins, outs, scr):
        x, y, c, chips = _place()
        me = (x, y, c)
        for t, (x_ref, out_ref) in enumerate(zip(ins, outs)):
            pltpu.make_async_copy(x_ref, out_ref.at[4 * x + 2 * y + c], scr[2].at[t]).start()
            self._copy(scr, t, 0, out_ref, me, (x, y, 1 - c), src=x_ref).start()
            for j, chip in enumerate(chips):
                self._copy(scr, t, 1 + j, out_ref, me, (*chip, c), src=x_ref).start()

    def mid(self, ins, outs, scr):
        x, y, c, chips = _place()
        for t, out_ref in enumerate(outs):
            for j, chip in enumerate(chips):
                self._copy(scr, t, 1 + j, out_ref, (*chip, c), (x, y, c)).wait_recv()
                self._copy(scr, t, 4 + j, out_ref, (*chip, c), (x, y, 1 - c)).start()

    def finish(self, ins, outs, scr):
        x, y, c, chips = _place()
        me = (x, y, c)
        for t, (x_ref, out_ref) in enumerate(zip(ins, outs)):
            self._copy(scr, t, 0, out_ref, (x, y, 1 - c), me).wait_recv()
            for j, chip in enumerate(chips):
                self._copy(scr, t, 4 + j, out_ref, (*chip, 1 - c), me).wait_recv()
            self._copy(scr, t, 0, out_ref, me, (x, y, 1 - c), src=x_ref).wait_send()
            for j, chip in enumerate(chips):
                self._copy(scr, t, 1 + j, out_ref, me, (*chip, c), src=x_ref).wait_send()
                self._copy(scr, t, 4 + j, out_ref, (*chip, c), (x, y, 1 - c)).wait_send()
            pltpu.make_async_copy(x_ref, out_ref.at[4 * x + 2 * y + c], scr[2].at[t]).wait()


class _Hooks:
    def __init__(self, hooks):
        self.hooks = list(hooks)
        self.has_mid = any(h.has_mid for h in hooks)
        self.inputs = [a for h in hooks for a in h.inputs]
        self.out_shape = [s for h in hooks for s in h.out_shape]
        self.scratch = [s for h in hooks for s in h.scratch]
        self.aliases = []
        i0 = o0 = 0
        for h in hooks:
            self.aliases += [(i0 + a, o0 + b) for a, b in h.aliases]
            i0 += len(h.inputs)
            o0 += len(h.out_shape)

    def _each(self, ins, outs, scr):
        i0 = o0 = s0 = 0
        for h in self.hooks:
            ni, no, ns = len(h.inputs), len(h.out_shape), len(h.scratch)
            yield h, ins[i0:i0 + ni], outs[o0:o0 + no], scr[s0:s0 + ns]
            i0, o0, s0 = i0 + ni, o0 + no, s0 + ns

    def start(self, ins, outs, scr):
        for h, i, o, s in self._each(ins, outs, scr):
            h.start(i, o, s)

    def mid(self, ins, outs, scr):
        for h, i, o, s in self._each(ins, outs, scr):
            if h.has_mid:
                h.mid(i, o, s)

    def finish(self, ins, outs, scr):
        for h, i, o, s in self._each(ins, outs, scr):
            h.finish(i, o, s)

    def split(self, outs):
        res, o0 = [], 0
        for h in self.hooks:
            res.append(list(outs[o0:o0 + len(h.out_shape)]))
            o0 += len(h.out_shape)
        return res


def _pair_sum(g, other, c_idx):
    nq, r, w = g.shape
    h = r // 2
    tr = _div_tile(h)
    nt = h // tr

    def body(c_ref, g_ref, o_ref, s_ref):
        s_ref[...] = (g_ref[...].astype(F32) + o_ref[...].astype(F32)).astype(s_ref.dtype)

    return pl.pallas_call(
        body, name="pair_sum",
        grid_spec=pltpu.PrefetchScalarGridSpec(
            num_scalar_prefetch=1, grid=(nq, nt),
            in_specs=[pl.BlockSpec((None, tr, w), lambda k, i, c_ref: (k, c_ref[0] * nt + i, 0)),
                      pl.BlockSpec((None, tr, w), lambda k, i, c_ref: (k, i, 0))],
            out_specs=pl.BlockSpec((None, tr, w), lambda k, i, c_ref: (k, i, 0))),
        out_shape=jax.ShapeDtypeStruct((nq, h, w), g.dtype),
        compiler_params=_cparams(2),
    )(c_idx, g, other)


def _chip_sum(s, others, q_idx):
    _, h, w = s.shape
    tr = _div_tile(h)

    def body(q_ref, s_ref, o_ref, out_ref):
        out_ref[...] = ((s_ref[...].astype(F32) + o_ref[0].astype(F32)) + o_ref[1].astype(F32)) + o_ref[2].astype(F32)

    return pl.pallas_call(
        body, name="chip_sum",
        grid_spec=pltpu.PrefetchScalarGridSpec(
            num_scalar_prefetch=1, grid=(h // tr,),
            in_specs=[pl.BlockSpec((None, tr, w), lambda i, q_ref: (q_ref[0], i, 0)),
                      pl.BlockSpec((3, tr, w), lambda i, q_ref: (0, i, 0))],
            out_specs=pl.BlockSpec((tr, w), lambda i, q_ref: (i, 0))),
        out_shape=jax.ShapeDtypeStruct((h, w), F32),
        compiler_params=_cparams(1),
    )(q_idx, s, others)


def _small_update(q_idx, parts, ws, ms, vs, col_block):
    n = len(parts)
    has_w = [w is not None for w in ws]

    def body(q_ref, *refs):
        pos = 0
        ins = []
        for t in range(n):
            k = 4 if has_w[t] else 1
            ins.append(refs[pos:pos + k])
            pos += k
        outs = refs[pos:]
        opos = 0
        for t in range(n):
            p_ref = ins[t][0]
            g = p_ref[0]
            for s in range(1, p_ref.shape[0]):
                g = g + p_ref[s]
            if has_w[t]:
                _, w_ref, m_ref, v_ref = ins[t]
                g_ref, d_ref, nm_ref, nv_ref = outs[opos:opos + 4]
                opos += 4
                g_ref[...] = g
                d_ref[...], nm_ref[...], nv_ref[...] = _adamw_update(w_ref[...], g, m_ref[...], v_ref[...])
            else:
                outs[opos][...] = g
                opos += 1

    def whole(shape):
        nd = len(shape)
        return pl.BlockSpec(shape, lambda i, q_ref: (0,) * nd)

    in_specs, out_specs, out_shape, args = [], [], [], []
    for t in range(n):
        k, r, wf = parts[t].shape
        if col_block[t]:
            w = wf // N_CHIPS
            in_specs.append(pl.BlockSpec((k, r, w), lambda i, q_ref: (0, 0, q_ref[0])))
        else:
            w = wf
            in_specs.append(whole((k, r, wf)))
        args.append(parts[t])
        if has_w[t]:
            assert ws[t].shape == (r, w), (ws[t].shape, r, w)
            in_specs += [whole((r, w))] * 3
            args += [ws[t], ms[t], vs[t]]
            out_specs += [whole((r, w))] * 4
            out_shape += [jax.ShapeDtypeStruct((r, w), F32)] * 4
        else:
            out_specs.append(whole((r, w)))
            out_shape.append(jax.ShapeDtypeStruct((r, w), F32))
    return pl.pallas_call(
        body, name="small_update",
        grid_spec=pltpu.PrefetchScalarGridSpec(num_scalar_prefetch=1, grid=(1,), in_specs=in_specs,
                                               out_specs=out_specs),
        out_shape=out_shape, compiler_params=_cparams(1),
    )(q_idx, *args)


_PACK_SEGMENTS = ((0, 1552), None, (1552, 2064), (2064, 2128), (2064, 2128), (2128, 2192), (2128, 2192),
                  (2192, 2256), (2192, 2256), (2256, 2320), (2256, 2320))
_UNPACK_SEGMENTS = (((0, 1552), (0,)), ((1552, 2064), (P_SQ,)), ((2064, 2128), (P_SK, P_SK + 64)),
                    ((2128, 2192), (P_SK + 128, P_SK + 192)), ((2192, 2256), (P_SV, P_SV + 64)),
                    ((2256, 2320), (P_SV + 128, P_SV + 192)))


def _pack_win(w4):
    per = w4.shape[2]
    pieces = []
    for seg in _PACK_SEGMENTS:
        if seg is None:
            pieces.append(jnp.zeros((w4.shape[1], 128 - GLA_RANK), w4.dtype))
            continue
        for q in range(w4.shape[0]):
            lo, hi = max(seg[0], q * per), min(seg[1], (q + 1) * per)
            if lo < hi:
                pieces.append(w4[q][:, lo - q * per:hi - q * per])
    return jnp.concatenate(pieces, axis=1)


def _unpack_dwin(d):
    per = D_IN // N_CHIPS
    chips = []
    for q in range(N_CHIPS):
        pieces = []
        for (a, b), starts in _UNPACK_SEGMENTS:
            lo, hi = max(a, q * per), min(b, (q + 1) * per)
            if lo < hi:
                copies = [d[:, s + lo - a:s + hi - a] for s in starts]
                pieces.append(copies[0] if len(copies) == 1 else copies[0] + copies[1])
        chips.append(jnp.concatenate(pieces, axis=1))
    return jnp.stack(chips).astype(BF16)


def _local_step(x, target, meta, p):
    s = x.shape[0]
    t = s + BLK
    h0 = jnp.concatenate([jnp.zeros((PAD, D_MODEL), F32), meta, x], axis=0)
    cos, sin = _rope_tables(t)

    h1, n1, g1, u1, a1, f1 = _ffn_fwd(h0, p["ffn1_pre_norm"], p["ffn1_w"], p["ffn1_post_norm"])
    n2, gq, gk, gv, gg, ga, la, sq, sk, sv = _mix_proj(h1, p["mix_pre_norm"], p["w_in"], p["gla_w_a2"], p["gla_b_a"],
                                                       cos, sin)
    ogla, ss = _gla_fwd(gq, gk, gv, la)
    oswa = _swa_fwd(p["swa_sinks"], sq, sk, sv)
    h2, cat, m = _mix_out(h1, ogla, gg, oswa, p["gla_out_norm"], p["swa_out_norm"], p["w_out"], p["mix_post_norm"])
    grads = {}
    dy, n3, g3, u3, a3, df3, grads["ffn2_post_norm"], sse = _ffn_fwd(
        h2, p["ffn2_pre_norm"], p["ffn2_w"], p["ffn2_post_norm"], target=target)

    dh2, dg3, du3, grads["ffn2_pre_norm"] = _ffn_bwd(
        dy, h2, None, g3, u3, p["ffn2_pre_norm"], p["ffn2_w"], p["ffn2_post_norm"], df=df3)
    (gud,) = _ffn_wgrad(n3, df3, dg3, du3, a3)
    grads["ffn2_w_gate"], grads["ffn2_w_up"], grads["ffn2_w_down"] = gud[:, :FJ], gud[:, FJ:2 * FJ], gud[:, 2 * FJ:]

    dogla, dgg, doswa, dm, grads["mix_post_norm"], grads["gla_out_norm"], grads["swa_out_norm"] = _mix_out_bwd(
        dh2, m, ogla, gg, oswa, p["gla_out_norm"], p["swa_out_norm"], p["w_out"], p["mix_post_norm"])
    grads["w_out"] = _xty(cat, dm)
    dsq, dsk, dsv, dkm, dvm, dsinks = _swa_bwd(p["swa_sinks"], sq, sk, sv, oswa, doswa)
    grads["swa_sinks"] = dsinks[:, 0]
    dgq, dgk, dgv, dla = _gla_bwd(gq, gk, gv, la, ss, dogla)
    dh1, dproj, grads["mix_pre_norm"], dwa2p, grads["gla_b_a"] = _mix_in_bwd(
        dh2, h1, p["mix_pre_norm"], p["w_in"], p["gla_w_a2"], p["gla_b_a"], cos, sin, ga, dgq, dgk, dgv, dgg, dla,
        dsq, dsk, dsv, dkm, dvm)
    grads["gla_w_a2"] = dwa2p[:GLA_RANK]
    grads["w_in"] = _unpack_dwin(_xty(n2, dproj))

    dh0, df1, dg1, du1, grads["ffn1_pre_norm"], grads["ffn1_post_norm"] = _ffn_bwd(
        dh1, h0, f1, g1, u1, p["ffn1_pre_norm"], p["ffn1_w"], p["ffn1_post_norm"])
    (gud,) = _ffn_wgrad(n1, df1, dg1, du1, a1)
    grads["ffn1_w_gate"], grads["ffn1_w_up"], grads["ffn1_w_down"] = gud[:, :FJ], gud[:, FJ:2 * FJ], gud[:, 2 * FJ:]
    grads["meta_tokens"] = dh0[PAD:BLK]
    return sse[0, 0], dh0[BLK:], grads


WEIGHTS = ['meta_tokens', 'ffn1_pre_norm', 'ffn1_w_gate', 'ffn1_w_up', 'ffn1_w_down', 'ffn1_post_norm',
           'mix_pre_norm', 'w_in', 'gla_w_a2', 'gla_b_a', 'gla_out_norm', 'swa_sinks', 'swa_out_norm', 'w_out',
           'mix_post_norm', 'ffn2_pre_norm', 'ffn2_w_gate', 'ffn2_w_up', 'ffn2_w_down', 'ffn2_post_norm']
BIG = ['ffn1_w_gate', 'ffn1_w_up', 'ffn1_w_down', 'w_in', 'w_out', 'ffn2_w_gate', 'ffn2_w_up', 'ffn2_w_down']
SMALL = [n for n in WEIGHTS if n not in BIG]
FJ = D_FF // N_CHIPS
D_IN_J = D_IN // N_CHIPS
D_OUT_J = D_MODEL // N_CHIPS
TRANSPOSED = ('ffn1_w_gate', 'ffn1_w_up', 'ffn2_w_gate', 'ffn2_w_up')


def _shard2d(name, a):
    return a[0].T if name in TRANSPOSED else a[0]


def _unshard2d(name, a):
    return (a.T if name in TRANSPOSED else a)[None]


def kernel(x, meta_tokens, ffn1_pre_norm, ffn1_w_gate, ffn1_w_up, ffn1_w_down, ffn1_post_norm, mix_pre_norm, w_in, gla_w_a2, gla_b_a, gla_out_norm, swa_sinks, swa_out_norm, w_out, mix_post_norm, ffn2_pre_norm, ffn2_w_gate, ffn2_w_up, ffn2_w_down, ffn2_post_norm, loss_target, m_meta_tokens, m_ffn1_pre_norm, m_ffn1_w_gate, m_ffn1_w_up, m_ffn1_w_down, m_ffn1_post_norm, m_mix_pre_norm, m_w_in, m_gla_w_a2, m_gla_b_a, m_gla_out_norm, m_swa_sinks, m_swa_out_norm, m_w_out, m_mix_post_norm, m_ffn2_pre_norm, m_ffn2_w_gate, m_ffn2_w_up, m_ffn2_w_down, m_ffn2_post_norm, v_meta_tokens, v_ffn1_pre_norm, v_ffn1_w_gate, v_ffn1_w_up, v_ffn1_w_down, v_ffn1_post_norm, v_mix_pre_norm, v_w_in, v_gla_w_a2, v_gla_b_a, v_gla_out_norm, v_swa_sinks, v_swa_out_norm, v_w_out, v_mix_post_norm, v_ffn2_pre_norm, v_ffn2_w_gate, v_ffn2_w_up, v_ffn2_w_down, v_ffn2_post_norm):
    args = dict(locals())
    w = {n: args[n] for n in WEIGHTS}
    mom = {n: args["m_" + n] for n in WEIGHTS}
    var = {n: args["v_" + n] for n in WEIGHTS}
    cx, cy, cc = lax.axis_index("x"), lax.axis_index("y"), lax.axis_index("c")
    q_idx = (2 * cx + cy).astype(jnp.int32).reshape(1)
    c_idx = cc.astype(jnp.int32).reshape(1)

    q_chip = 2 * cx + cy
    bf = {n: _own_slot(_shard2d(n, w[n]).astype(BF16), q_chip) for n in ("w_in", "w_out")}
    for ffn in ("ffn1", "ffn2"):
        stacked = jnp.concatenate([_shard2d(ffn + s, w[ffn + s]) for s in ("_w_gate", "_w_up", "_w_down")], axis=0)
        bf[ffn] = _own_slot(stacked.astype(BF16), q_chip)
    qc_idx = jnp.stack([q_chip, cc]).astype(jnp.int32)
    sinks = w["swa_sinks"].reshape(SWA_QH)

    seq, target = x[0], loss_target[0]
    t = seq.shape[0] + BLK
    h0, n1 = _embed_norm(seq, _own_slot(w["meta_tokens"], q_chip), w["ffn1_pre_norm"])
    cos, sin = _rope_tables(t)
    late = _GatherChips([bf["w_in"], bf["w_out"], bf["ffn2"],
                         _own_slot(w["gla_w_a2"].reshape(GLA_RANK, GLA_KW // N_CHIPS), q_chip)])
    (h1, g1, u1, a1, f1), (w31,), (win4, wout4, w32, wa24) = _ffn_fwd_gather(
        h0, n1, bf["ffn1"], w["ffn1_post_norm"], qc_idx, late)
    wa2p = jnp.pad(wa24.transpose(1, 0, 2).reshape(GLA_RANK, GLA_KW), ((0, 128 - GLA_RANK), (0, 0))).astype(BF16)
    winp = _pack_win(win4)
    wout = wout4.reshape(D_MODEL, D_MODEL)
    n2, gq, gk, gv, gg, ga, la, sq, sk, sv = _mix_proj(h1, w["mix_pre_norm"], winp, wa2p, w["gla_b_a"], cos, sin)
    ogla, ss = _gla_fwd(gq, gk, gv, la)
    oswa = _swa_fwd(sinks, sq, sk, sv)
    h2, cat, m = _mix_out(h1, ogla, gg, oswa, w["gla_out_norm"], w["swa_out_norm"], wout, w["mix_post_norm"])
    g = {}
    dy, n3, g3, u3, a3, df3, g["ffn2_post_norm"], sse = _ffn_fwd(
        h2, w["ffn2_pre_norm"], w32, w["ffn2_post_norm"], target=target)

    dh2, dg3, du3, g["ffn2_pre_norm"] = _ffn_bwd(
        dy, h2, None, g3, u3, w["ffn2_pre_norm"], w32, w["ffn2_post_norm"], df=df3)
    (gf2,) = _ffn_wgrad(n3, df3, dg3, du3, a3)
    (dogla, dgg, doswa, dm, g["mix_post_norm"], g["gla_out_norm"], g["swa_out_norm"]), (rgf2,) = _mix_out_bwd(
        dh2, m, ogla, gg, oswa, w["gla_out_norm"], w["swa_out_norm"], wout, w["mix_post_norm"],
        hook=_PairExchange([gf2]))
    sgf2 = _pair_sum(gf2, rgf2, c_idx)
    gout = _xty(cat, dm).reshape(N_CHIPS, D_OUT_J, D_MODEL).astype(BF16)
    (dsq, dsk, dsv, dkm, dvm, dsinks), (ogf2,) = _swa_bwd(sinks, sq, sk, sv, oswa, doswa,
                                                          hook=_ChipScatter([sgf2]))
    g["swa_sinks"] = dsinks
    dgq, dgk, dgv, dla = _gla_bwd(gq, gk, gv, la, ss, dogla)
    dh1, dproj, g["mix_pre_norm"], dwa2p, g["gla_b_a"] = _mix_in_bwd(
        dh2, h1, w["mix_pre_norm"], winp, wa2p, w["gla_b_a"], cos, sin, ga, dgq, dgk, dgv, dgg, dla,
        dsq, dsk, dsv, dkm, dvm)
    g["gla_w_a2"] = dwa2p[:GLA_RANK]
    gin = _unpack_dwin(_xty(n2, dproj))
    (dh0, df1, dg1, du1, g["ffn1_pre_norm"], g["ffn1_post_norm"]), (rgin, rgout) = _ffn_bwd(
        dh1, h0, f1, g1, u1, w["ffn1_pre_norm"], w31, w["ffn1_post_norm"],
        hook=_PairExchange([gin, gout]))
    sgin, sgout = _pair_sum(gin, rgin, c_idx), _pair_sum(gout, rgout, c_idx)
    g["meta_tokens"] = dh0[PAD:BLK]
    grad_x = dh0[BLK:]
    late_small = ["gla_w_a2", "swa_sinks"]
    direct = [n for n in SMALL if n not in late_small]
    names = direct + late_small
    half_f2 = _chip_sum(sgf2, ogf2, q_idx)
    hooks = _Hooks([_ChipScatter([sgin, sgout]), _GatherDevices([g[n] for n in names] + [sse]),
                    _PairShare([half_f2])])
    own1, others1, houts = _ffn_wgrad_reduce(n1, df1, dg1, du1, a1, qc_idx, hooks)
    (ogin, ogout), gathered, (other_f2,) = hooks.split(houts)
    halves = [_chip_sum(own1[None], others1, jnp.zeros((1,), jnp.int32))]
    halves += [_chip_sum(s, o, q_idx) for s, o in ((sgin, ogin), (sgout, ogout))]
    others = list(_comm_call(_PairShare(halves), "pair_share")) + [other_f2]
    halves.append(half_f2)
    reduced = {"ffn1_w_gate": (0, 0), "ffn1_w_up": (0, FJ), "ffn1_w_down": (0, 2 * FJ), "w_in": (1, 0),
               "w_out": (2, 0), "ffn2_w_gate": (3, 0), "ffn2_w_up": (3, FJ), "ffn2_w_down": (3, 2 * FJ)}
    grad, delta, new_m, new_v = {}, {}, {}, {}
    for n in BIG:
        k, row0 = reduced[n]
        outs = _adamw_halves(_shard2d(n, w[n]), halves[k], others[k], _shard2d(n, mom[n]), _shard2d(n, var[n]),
                             c_idx, row0)
        grad[n], delta[n], new_m[n], new_v[n] = [_unshard2d(n, a) for a in outs]

    late = late_small
    mat = lambda a: a.reshape(a.shape[-2:])
    none3 = [None] * (len(late) + 1)
    outs = _small_update(q_idx, gathered, [mat(w[n]) for n in direct] + none3, [mat(mom[n]) for n in direct] + none3,
                         [mat(var[n]) for n in direct] + none3, [n == "meta_tokens" for n in names] + [False])
    sum_a2, sum_sinks, sum_sse = outs[4 * len(direct):]
    loss = sum_sse[0, 0] * (0.5 / D_MODEL)
    g_late = [lax.dynamic_slice_in_dim(sum_a2, q_chip * (GLA_KW // N_CHIPS), GLA_KW // N_CHIPS, axis=1)[None],
              sum_sinks[:, 0].reshape(1, 1, SWA_QH)]
    outs = list(outs[:4 * len(direct)]) + list(_small_update(
        q_idx, g_late, [mat(w[n]) for n in late], [mat(mom[n]) for n in late], [mat(var[n]) for n in late],
        [False, False]))
    for k, n in enumerate(names):
        grad[n], delta[n], new_m[n], new_v[n] = [a.reshape(w[n].shape) for a in outs[4 * k:4 * k + 4]]

    return (loss, grad_x[None], *[grad[n] for n in WEIGHTS], *[delta[n] for n in WEIGHTS],
            *[new_m[n] for n in WEIGHTS], *[new_v[n] for n in WEIGHTS])
```

```python
import functools
import math

import numpy as np
import jax
import jax.numpy as jnp
from jax import lax
from jax.experimental import pallas as pl
from jax.experimental.pallas import tpu as pltpu

F32 = jnp.float32
BF16 = jnp.bfloat16
MESH = pl.DeviceIdType.MESH

D_MODEL = 1024
D_FF = 2816
N_CHIPS = 4
N_DEV = 8
N_META = 16
BLK = 128
PAD = BLK - N_META
GLA_CHUNK = 64
GLA_HEADS = 4
GLA_DV = 128
GLA_DK = 64
GLA_KW = GLA_HEADS * GLA_DK
GLA_W = GLA_HEADS * GLA_DV
GLA_RANK = 16
GLA_TAU = 16.0
SWA_HD = 64
SWA_QH = 8
SWA_KVH = 2
SWA_W = SWA_QH * SWA_HD
WINDOW = 128
ROPE_THETA = 10000.0
EPS = 1e-6
NEG_INF = -1e30
IN_SPLITS = (256, 256, 512, 512, 16, 512, 128, 128)
D_IN = sum(IN_SPLITS)
P_GQ, P_GK, P_GV, P_GG, P_GA, P_SQ, P_SK, P_SV, P_END = 0, 256, 512, 1024, 1536, 1664, 2176, 2432, 2688
ADAM_LR, ADAM_B1, ADAM_B2, ADAM_EPS, ADAM_WD, ADAM_STEP = 0.001, 0.9, 0.999, 1e-08, 0.01, 10
VMEM_LIMIT = 56 * 1024 * 1024

NT = (((1,), (1,)), ((), ()))
TN = (((0,), (0,)), ((), ()))


def _cparams(n_axes):
    return pltpu.CompilerParams(dimension_semantics=("arbitrary",) * n_axes, vmem_limit_bytes=VMEM_LIMIT)


def _row_tile(t):
    for tm in (640, 512, 384, 256, 128):
        if t % tm == 0:
            return tm
    raise ValueError(t)


SEQ_BLOCKS_PER_STEP = 5


def _seq_tile(t):
    return SEQ_BLOCKS_PER_STEP * BLK if t % (SEQ_BLOCKS_PER_STEP * BLK) == 0 else BLK


ROW_PARTS = 2


def _row_parts(tm):
    n = ROW_PARTS if tm % (16 * ROW_PARTS) == 0 else 1
    return [slice(k * (tm // n), (k + 1) * (tm // n)) for k in range(n)]


def _contract_tile(t):
    return 1664 if t % 1664 == 0 else _row_tile(t)


def _div_tile(r, cap=512):
    best = None
    for tr in range(8, min(r, cap) + 1, 8):
        if r % tr == 0:
            best = tr
    return best if best is not None else r


def _dot(a, b):
    return jnp.dot(a, b, preferred_element_type=F32)


def _dg(a, b, dims):
    return lax.dot_general(a, b, dims, preferred_element_type=F32)


def _rms(x, w):
    r = lax.rsqrt(jnp.mean(x * x, axis=-1, keepdims=True) + EPS)
    xh = x * r
    return xh * w, xh, r


def _rms_bwd(xh, r, w, dy):
    wdy = dy * w
    dx = r * (wdy - xh * jnp.mean(wdy * xh, axis=-1, keepdims=True))
    dw = jnp.sum(dy * xh, axis=0, keepdims=True)
    return dx, dw


def _sigmoid(x):
    return 1.0 / (1.0 + jnp.exp(-x))


def _full(shape):
    nd = len(shape)
    return pl.BlockSpec(shape, lambda *_: (0,) * nd)


ANY = pl.BlockSpec(memory_space=pl.ANY)


def _pallas(body, *, name, grid, in_specs, out_specs, out_shape, args, scratch_shapes=(), hook=None):
    n_axes = len(grid)
    if hook is None:
        return pl.pallas_call(body, name=name, grid=grid, in_specs=list(in_specs), out_specs=list(out_specs),
                              out_shape=list(out_shape), scratch_shapes=list(scratch_shapes),
                              compiler_params=_cparams(n_axes))(*args)
    n_in, n_out, n_scr = len(in_specs), len(out_specs), len(scratch_shapes)
    h_in, h_out = len(hook.inputs), len(hook.out_shape)
    total = math.prod(grid)

    def wrapped(*refs):
        ins, hins = refs[:n_in], refs[n_in:n_in + h_in]
        o0 = n_in + h_in
        outs, houts = refs[o0:o0 + n_out], refs[o0 + n_out:o0 + n_out + h_out]
        s0 = o0 + n_out + h_out
        scr, hscr = refs[s0:s0 + n_scr], refs[s0 + n_scr:]
        step = pl.program_id(0)
        for a in range(1, n_axes):
            step = step * grid[a] + pl.program_id(a)

        @pl.when(step == 0)
        def _():
            hook.start(hins, houts, hscr)

        body(*ins, *outs, *scr)

        if hook.has_mid:
            @pl.when(step == (3 * total) // 4)
            def _():
                hook.mid(hins, houts, hscr)

        @pl.when(step == total - 1)
        def _():
            hook.finish(hins, houts, hscr)

    res = pl.pallas_call(
        wrapped, name=name, grid=grid, in_specs=list(in_specs) + [ANY] * h_in,
        out_specs=list(out_specs) + [ANY] * h_out, out_shape=list(out_shape) + list(hook.out_shape),
        scratch_shapes=list(scratch_shapes) + list(hook.scratch), compiler_params=_cparams(n_axes),
        input_output_aliases={n_in + a: n_out + b for a, b in hook.aliases},
    )(*args, *hook.inputs)
    return res[:n_out], res[n_out:]


def _ffn_weight_specs(w3):
    fj = w3.shape[1] // 3
    return fj, [pl.BlockSpec((None, fj, D_MODEL), functools.partial(lambda i, j, k: (j, k, 0), k=k)) for k in range(3)]


def _ffn_fwd(h, wpre, w3, wpost, hook=None, target=None):
    t = h.shape[0]
    tm = _row_tile(t)
    nj, rows3, _ = w3.shape
    fj = rows3 // 3
    nblk = tm // BLK if target is not None else 0

    def body(*refs):
        h_ref, wpre_ref, w_hbm, wpost_ref = refs[:4]
        t_refs = refs[4:4 + nblk]
        hout_ref, n_ref, p1_ref, p2_ref, a_ref, f_ref = refs[4 + nblk:10 + nblk]
        acc_ref, wv, wsem = refs[-3:]
        i = pl.program_id(0)
        j = pl.program_id(1)

        @pl.when((i == 0) & (j == 0))
        def _():
            for k in range(nj):
                pltpu.make_async_copy(w_hbm.at[k], wv.at[k], wsem.at[k]).start()

        @pl.when(i == 0)
        def _():
            pltpu.make_async_copy(w_hbm.at[j], wv.at[j], wsem.at[j]).wait()

        @pl.when(j == 0)
        def _():
            y, _, _ = _rms(h_ref[...], wpre_ref[...])
            n_ref[...] = y.astype(BF16)
            acc_ref[...] = jnp.zeros_like(acc_ref)

        if target is not None:
            dwpost_ref, sse_ref = refs[10 + nblk:12 + nblk]

            @pl.when((i == 0) & (j == 0))
            def _():
                dwpost_ref[...] = jnp.zeros_like(dwpost_ref)
                sse_ref[...] = jnp.zeros_like(sse_ref)

        n = n_ref[...]
        g = _dg(n, wv[j, 0:fj], NT)
        u = _dg(n, wv[j, fj:2 * fj], NT)
        sg = _sigmoid(g)
        silu = g * sg
        p1_ref[...] = (u * (sg + silu * (1.0 - sg))).astype(BF16)
        p2_ref[...] = silu.astype(BF16)
        a = (silu * u).astype(BF16)
        a_ref[...] = a
        acc_ref[...] += _dot(a, wv[j, 2 * fj:3 * fj])

        @pl.when(j == nj - 1)
        def _():
            f = acc_ref[...]
            wpost = wpost_ref[...]
            y, fh, r = _rms(f, wpost)
            hout = h_ref[...] + 0.5 * y
            if target is None:
                f_ref[...] = f
                hout_ref[...] = hout
            else:
                sse = jnp.zeros((1, 1), F32)
                errs = []
                for k in range(nblk):
                    err = hout[k * BLK:(k + 1) * BLK] - t_refs[k][...]
                    if k == 0:
                        err = jnp.where(i > 0, err, 0.0)
                    errs.append(err)
                    sse = sse + jnp.sum(jnp.sum(err * err, axis=1, keepdims=True), axis=0, keepdims=True)
                dy = (jnp.concatenate(errs, axis=0) if nblk > 1 else errs[0]) * (1.0 / D_MODEL)
                hout_ref[...] = dy
                df, dw = _rms_bwd(fh, r, wpost, 0.5 * dy)
                f_ref[...] = df.astype(BF16)
                dwpost_ref[...] += dw
                sse_ref[...] += jnp.broadcast_to(sse, sse_ref.shape)

    row = pl.BlockSpec((tm, D_MODEL), lambda i, j: (i, 0))
    vec = pl.BlockSpec((1, D_MODEL), lambda i, j: (0, 0))
    act = pl.BlockSpec((None, tm, fj), lambda i, j: (j, i, 0))
    t_specs = [pl.BlockSpec((BLK, D_MODEL), functools.partial(lambda i, j, k: (jnp.maximum(nblk * i + k - 1, 0), 0), k=k))
               for k in range(nblk)]
    loss_spec = [vec, _full((1, 128))] if target is not None else []
    loss_shape = [jax.ShapeDtypeStruct((1, D_MODEL), F32), jax.ShapeDtypeStruct((1, 128), F32)] if (
        target is not None) else []
    return _pallas(
        body, name="ffn_fwd", grid=(t // tm, nj),
        in_specs=[row, vec, ANY, vec] + t_specs,
        out_specs=[row, row, act, act, act, row] + loss_spec,
        out_shape=[jax.ShapeDtypeStruct((t, D_MODEL), F32), jax.ShapeDtypeStruct((t, D_MODEL), BF16),
                   jax.ShapeDtypeStruct((nj, t, fj), BF16), jax.ShapeDtypeStruct((nj, t, fj), BF16),
                   jax.ShapeDtypeStruct((nj, t, fj), BF16),
                   jax.ShapeDtypeStruct((t, D_MODEL), F32 if target is None else BF16)] + loss_shape,
        scratch_shapes=[pltpu.VMEM((tm, D_MODEL), F32), pltpu.VMEM((nj, rows3, D_MODEL), BF16),
                        pltpu.SemaphoreType.DMA((nj,))],
        args=(h, wpre, w3, wpost) + (target,) * nblk, hook=hook)


def _ffn_bwd(dhout, h, f, p14, p24, wpre, w3, wpost, hook=None, df=None):
    t = h.shape[0]
    tm = _row_tile(t)
    nj = w3.shape[0]
    fj, wspecs = _ffn_weight_specs(w3)
    have_df = df is not None

    def body(dhout_ref, h_ref, f_ref, p1_ref, p2_ref, wpre_ref, wg_ref, wu_ref, wd_ref, wpost_ref, *rest):
        if have_df:
            dh_ref, dg_ref, du_ref, dwpre_ref, dn_ref = rest
            df_ref = f_ref
        else:
            dh_ref, df_ref, dg_ref, du_ref, dwpre_ref, dwpost_ref, dn_ref = rest
        i = pl.program_id(0)
        j = pl.program_id(1)

        @pl.when((i == 0) & (j == 0))
        def _():
            dwpre_ref[...] = jnp.zeros_like(dwpre_ref)
            if not have_df:
                dwpost_ref[...] = jnp.zeros_like(dwpost_ref)

        @pl.when(j == 0)
        def _():
            if not have_df:
                wpost = wpost_ref[...]
                _, fh, r = _rms(f_ref[...], wpost)
                dfv, dw = _rms_bwd(fh, r, wpost, 0.5 * dhout_ref[...])
                dwpost_ref[...] += dw
                df_ref[...] = dfv.astype(BF16)
            dn_ref[...] = jnp.zeros_like(dn_ref)

        parts = _row_parts(tm)
        das = [_dg(df_ref[rows, :], wd_ref[...], NT) for rows in parts]
        for rows, da in zip(parts, das):
            dg = (da * p1_ref[rows, :].astype(F32)).astype(BF16)
            du = (da * p2_ref[rows, :].astype(F32)).astype(BF16)
            dg_ref[rows, :] = dg
            du_ref[rows, :] = du
            dn_ref[rows, :] += _dot(dg, wg_ref[...]) + _dot(du, wu_ref[...])

        @pl.when(j == nj - 1)
        def _():
            wpre = wpre_ref[...]
            _, hh, r = _rms(h_ref[...], wpre)
            dx, dw = _rms_bwd(hh, r, wpre, dn_ref[...])
            dwpre_ref[...] += dw
            dh_ref[...] = dhout_ref[...] + dx

    row = pl.BlockSpec((tm, D_MODEL), lambda i, j: (i, 0))
    vec = pl.BlockSpec((1, D_MODEL), lambda i, j: (0, 0))
    act = pl.BlockSpec((None, tm, fj), lambda i, j: (j, i, 0))
    actshape = jax.ShapeDtypeStruct((nj, t, fj), BF16)
    rowf, rowb, vecf = (jax.ShapeDtypeStruct((t, D_MODEL), F32), jax.ShapeDtypeStruct((t, D_MODEL), BF16),
                        jax.ShapeDtypeStruct((1, D_MODEL), F32))
    return _pallas(
        body, name="ffn_bwd", grid=(t // tm, nj),
        in_specs=[row, row, row, act, act, vec] + wspecs + [vec],
        out_specs=[row, act, act, vec] if have_df else [row, row, act, act, vec, vec],
        out_shape=[rowf, actshape, actshape, vecf] if have_df else [rowf, rowb, actshape, actshape, vecf, vecf],
        scratch_shapes=[pltpu.VMEM((tm, D_MODEL), F32)],
        args=(dhout, h, df if have_df else f, p14, p24, wpre, w3, w3, w3, wpost), hook=hook)


def _ffn_wgrad(n, df, dg4, du4, a4, hook=None):
    t = n.shape[0]
    tm = _contract_tile(t)
    ni = t // tm
    nj, _, fj = dg4.shape

    def body(n_ref, df_ref, dg_ref, du_ref, a_ref, dw_ref, acc):
        i = pl.program_id(1)

        @pl.when(i == 0)
        def _():
            acc[...] = jnp.zeros_like(acc)

        nn = n_ref[...]
        acc[0:fj, :] += _dg(dg_ref[...], nn, TN)
        acc[fj:2 * fj, :] += _dg(du_ref[...], nn, TN)
        acc[2 * fj:3 * fj, :] += _dg(a_ref[...], df_ref[...], TN)

        @pl.when(i == ni - 1)
        def _():
            dw_ref[...] = acc[...].astype(BF16)

    row = pl.BlockSpec((tm, D_MODEL), lambda j, i: (i, 0))
    act = pl.BlockSpec((None, tm, fj), lambda j, i: (j, i, 0))
    return _pallas(
        body, name="ffn_wgrad", grid=(nj, ni),
        in_specs=[row, row, act, act, act],
        out_specs=[pl.BlockSpec((None, 3 * fj, D_MODEL), lambda j, i: (j, 0, 0))],
        out_shape=[jax.ShapeDtypeStruct((nj, 3 * fj, D_MODEL), BF16)],
        scratch_shapes=[pltpu.VMEM((3 * fj, D_MODEL), F32)],
        args=(n, df, dg4, du4, a4), hook=hook)


def _embed_norm(x, meta_buf, w):
    t = x.shape[0] + BLK
    tm = _row_tile(t)
    nblk = tm // BLK
    ni = t // tm
    gather = _GatherChips([meta_buf])

    def body(*refs):
        x_refs = refs[:nblk]
        w_ref, mb_in, h_ref, n_ref, mb_out, mv, msem, send, recv = refs[nblk:]
        step = pl.program_id(0)
        tile = (step + 1) % ni
        hook_refs = ([mb_in], [mb_out], [send, recv])
        steps = (0, 1, ni - 2) if ni >= 3 else (0, 0, 0)
        for at, phase in zip(steps, (gather.start, gather.mid, gather.finish)):
            @pl.when(step == at)
            def _(phase=phase):
                phase(*hook_refs)

        @pl.when(step == 0)
        def _():
            mv[...] = jnp.zeros_like(mv)

        @pl.when(step == ni - 1)
        def _():
            cp = pltpu.make_async_copy(mb_out, mv, msem)
            cp.start()
            cp.wait()

        meta = jnp.concatenate([mv[k] for k in range(N_CHIPS)], axis=1)
        first = jnp.concatenate([jnp.zeros((PAD, D_MODEL), F32), meta], axis=0)
        blocks = [jnp.where(tile == 0, first, x_refs[0][...])] + [r[...] for r in x_refs[1:]]
        h = jnp.concatenate(blocks, axis=0) if nblk > 1 else blocks[0]
        h_ref[...] = h
        y, _, _ = _rms(h, w_ref[...])
        n_ref[...] = y.astype(BF16)

    x_specs = [pl.BlockSpec((BLK, D_MODEL), functools.partial(
        lambda i, k: (jnp.maximum(nblk * ((i + 1) % ni) + k - 1, 0), 0), k=k)) for k in range(nblk)]
    row = pl.BlockSpec((tm, D_MODEL), lambda i: ((i + 1) % ni, 0))
    h0, n0, _ = pl.pallas_call(
        body, name="embed_norm", grid=(ni,),
        in_specs=x_specs + [_full((1, D_MODEL)), ANY], out_specs=[row, row, ANY],
        out_shape=[jax.ShapeDtypeStruct((t, D_MODEL), F32), jax.ShapeDtypeStruct((t, D_MODEL), BF16),
                   jax.ShapeDtypeStruct(meta_buf.shape, meta_buf.dtype)],
        scratch_shapes=[pltpu.VMEM(meta_buf.shape, meta_buf.dtype), pltpu.SemaphoreType.DMA] + list(gather.scratch),
        input_output_aliases={nblk + 1: 2},
        compiler_params=_cparams(1),
    )(*([x] * nblk), w, meta_buf)
    return h0, n0


FWD_RELATION = (None, 0, 1, 2)


def _ffn_fwd_gather(h, n, wbuf, wpost, qc_idx, late):
    t = h.shape[0]
    tm = _row_tile(t)
    ni = t // tm
    nj, rows3, _ = wbuf.shape
    fj = rows3 // 3
    assert nj == N_CHIPS and ni >= 4
    wbufs = [wbuf]
    nw = 1
    n_lin, n_lout = len(late.inputs), len(late.out_shape)
    wait_step = ni - 3

    def body(qc_ref, h_ref, n_ref, wpost_ref, *rest):
        wb_in = rest[:nw]
        lins = rest[nw:nw + n_lin]
        o0 = nw + n_lin
        hout_ref, p1_ref, p2_ref, a_ref, f_hbm = rest[o0:o0 + 5]
        wb = rest[o0 + 5:o0 + 5 + nw]
        louts = rest[o0 + 5 + nw:o0 + 5 + nw + n_lout]
        s0 = o0 + 5 + nw + n_lout
        wv, wsem, send, recv, fbuf, fr_sem, fw_sem = rest[s0:s0 + 7]
        lscr = rest[s0 + 7:]
        p = pl.program_id(0)
        i = pl.program_id(1)
        step = p * ni + i
        fslot = step % 3
        nslot = (step + 1) % 3

        def f_tile(tile):
            return f_hbm.at[pl.ds(pl.multiple_of(tile * tm, 8), tm)]

        @pl.when(step > 1)
        def _():
            pltpu.make_async_copy(fbuf.at[nslot], f_tile(i), fw_sem.at[nslot]).wait()

        nxt = step + 1

        @pl.when((nxt < N_CHIPS * ni) & (nxt >= ni))
        def _():
            pltpu.make_async_copy(f_tile(nxt % ni), fbuf.at[nslot], fr_sem.at[nslot]).start()

        @pl.when(p > 0)
        def _():
            pltpu.make_async_copy(f_tile(i), fbuf.at[fslot], fr_sem.at[fslot]).wait()
        x, y, c, chips = _place()
        q = 2 * x + y
        sibling = (x, y, 1 - c)
        mine, other = _half(rows3, c), _half(rows3, 1 - c)

        def load(chunk, slot, src):
            return [pltpu.make_async_copy(src[t].at[chunk], wv.at[slot, t], wsem.at[slot, t]) for t in range(nw)]

        @pl.when((p == 0) & (i == 0))
        def _():
            for j, (cx, cy) in enumerate(chips):
                for t in range(nw):
                    _remote(send.at[t, j], recv.at[t, j], wb_in[t].at[q, mine], wb[t].at[q, mine], (cx, cy, c)).start()
            for cp in load(q, 0, wb_in):
                cp.start()
            for cp in load(q, 0, wb_in):
                cp.wait()

        @pl.when((p == 1) & (i == 0))
        def _():
            late.start(lins, louts, lscr)

        for pp in range(1, N_CHIPS):
            j = FWD_RELATION[pp]
            cx, cy = chips[j]
            chunk = 2 * cx + cy

            @pl.when((p == pp - 1) & (i == wait_step))
            def _(j=j, cx=cx, cy=cy, chunk=chunk, pp=pp):
                for t in range(nw):
                    got = wb[t].at[chunk, mine]
                    _remote(send.at[t, j], recv.at[t, j], got, got, (cx, cy, c)).wait_recv()
                    _remote(send.at[t, 3 + j], recv.at[t, 3 + j], got, got, sibling).start()
                for t in range(nw):
                    rest_half = wb[t].at[chunk, other]
                    _remote(send.at[t, 3 + j], recv.at[t, 3 + j], rest_half, rest_half, sibling).wait_recv()
                for cp in load(chunk, pp % 2, wb):
                    cp.start()

            @pl.when((p == pp) & (i == 0))
            def _(chunk=chunk, pp=pp):
                for cp in load(chunk, pp % 2, wb):
                    cp.wait()

        @pl.when((p == N_CHIPS - 1) & (i == ni // 2))
        def _():
            late.mid(lins, louts, lscr)

        slot = p % 2
        nn = n_ref[...]
        g = _dg(nn, wv[slot, 0, 0:fj], NT)
        u = _dg(nn, wv[slot, 0, fj:2 * fj], NT)
        sg = _sigmoid(g)
        silu = g * sg
        p1_ref[...] = (u * (sg + silu * (1.0 - sg))).astype(BF16)
        p2_ref[...] = silu.astype(BF16)
        a = (silu * u).astype(BF16)
        a_ref[...] = a
        part = _dot(a, wv[slot, 0, 2 * fj:3 * fj])

        @pl.when(p == 0)
        def _():
            fbuf[fslot] = part

        @pl.when(p > 0)
        def _():
            fbuf[fslot] = fbuf[fslot] + part

        pltpu.make_async_copy(fbuf.at[fslot], f_tile(i), fw_sem.at[fslot]).start()

        @pl.when(p == N_CHIPS - 1)
        def _():
            yv, _, _ = _rms(fbuf[fslot], wpost_ref[...])
            hout_ref[...] = h_ref[...] + 0.5 * yv

        @pl.when((p == N_CHIPS - 1) & (i == ni - 1))
        def _():
            pslot = (step + 2) % 3
            pltpu.make_async_copy(fbuf.at[pslot], f_tile(i), fw_sem.at[pslot]).wait()
            pltpu.make_async_copy(fbuf.at[fslot], f_tile(i), fw_sem.at[fslot]).wait()
            for t in range(nw):
                for j, (cx, cy) in enumerate(chips):
                    sent = wb[t].at[2 * cx + cy, mine]
                    _remote(send.at[t, j], recv.at[t, j], sent, sent, (cx, cy, c)).wait_send()
                    _remote(send.at[t, 3 + j], recv.at[t, 3 + j], sent, sent, sibling).wait_send()
            late.finish(lins, louts, lscr)

    def last_pass_rows(p, i, qc_ref):
        return (jnp.where(p == N_CHIPS - 1, i, 0), 0)

    def chunk_rows(p, i, qc_ref):
        order = ((p & 1) << 1) | (p >> 1)
        return (jnp.bitwise_xor(qc_ref[0], order), i, 0)

    row = pl.BlockSpec((tm, D_MODEL), lambda p, i, qc_ref: (i, 0))
    last_row = pl.BlockSpec((tm, D_MODEL), last_pass_rows)
    act = pl.BlockSpec((None, tm, fj), chunk_rows)
    act_shape = jax.ShapeDtypeStruct((nj, t, fj), BF16)
    res = pl.pallas_call(
        body, name="ffn_fwd_gather",
        grid_spec=pltpu.PrefetchScalarGridSpec(
            num_scalar_prefetch=1, grid=(N_CHIPS, ni),
            in_specs=[last_row, row, pl.BlockSpec((1, D_MODEL), lambda p, i, qc_ref: (0, 0))]
            + [ANY] * (nw + n_lin),
            out_specs=[last_row, act, act, act, ANY] + [ANY] * (nw + n_lout),
            scratch_shapes=[pltpu.VMEM((2, nw, rows3, D_MODEL), BF16), pltpu.SemaphoreType.DMA((2, nw)),
                            pltpu.SemaphoreType.DMA((nw, 6)), pltpu.SemaphoreType.DMA((nw, 6)),
                            pltpu.VMEM((3, tm, D_MODEL), F32), pltpu.SemaphoreType.DMA((3,)),
                            pltpu.SemaphoreType.DMA((3,))] + list(late.scratch)),
        out_shape=[jax.ShapeDtypeStruct((t, D_MODEL), F32), act_shape, act_shape, act_shape,
                   jax.ShapeDtypeStruct((t, D_MODEL), F32)]
        + [jax.ShapeDtypeStruct(b.shape, b.dtype) for b in wbufs] + list(late.out_shape),
        input_output_aliases={**{4 + t: 5 + t for t in range(nw)},
                              **{4 + nw + a: 5 + nw + b for a, b in late.aliases}},
        compiler_params=_cparams(2),
    )(qc_idx, h, n, wpost, *wbufs, *late.inputs)
    return res[:5], res[5:5 + nw], res[5 + nw:]


PASS_RELATION = (2, 0, 1)


def _ffn_wgrad_reduce(n, df, dg4, du4, a4, qc_idx, hook):
    t = n.shape[0]
    tm = _contract_tile(t)
    ni = t // tm
    nj, _, fj = dg4.shape
    assert nj == N_CHIPS
    hrows = 3 * fj // 2
    n_hin, n_hout = len(hook.inputs), len(hook.out_shape)

    def body(qc_ref, n_ref, df_ref, dg_ref, du_ref, a_ref, *rest):
        hins = rest[:n_hin]
        own_ref, others_ref = rest[n_hin:n_hin + 2]
        houts = rest[n_hin + 2:n_hin + 2 + n_hout]
        s0 = n_hin + 2 + n_hout
        acc, stage, land, sumbuf, px_send, px_recv, cs_send, cs_recv, own_sem = rest[s0:s0 + 9]
        hscr = rest[s0 + 9:]
        k_pass = pl.program_id(0)
        i = pl.program_id(1)
        x, y, c, chips = _place()
        mine = pl.ds(pl.multiple_of(c * hrows, 8), hrows)
        other = pl.ds(pl.multiple_of((1 - c) * hrows, 8), hrows)

        def to_owner(k):
            j = PASS_RELATION[k]
            return _remote(cs_send.at[j], cs_recv.at[j], sumbuf.at[k % 2], others_ref.at[j], (*chips[j], c))

        @pl.when((k_pass == 0) & (i == 0))
        def _():
            hook.start(hins, houts, hscr)

        if hook.has_mid:
            @pl.when((k_pass == N_CHIPS // 2) & (i == 0))
            def _():
                hook.mid(hins, houts, hscr)

        @pl.when(i == 0)
        def _():
            acc[...] = jnp.zeros_like(acc)

        nn = n_ref[...]
        acc[0:fj, :] += _dg(dg_ref[...], nn, TN)
        acc[fj:2 * fj, :] += _dg(du_ref[...], nn, TN)
        acc[2 * fj:3 * fj, :] += _dg(a_ref[...], df_ref[...], TN)

        for k in range(N_CHIPS):
            @pl.when((k_pass == k) & (i == ni - 1))
            def _(k=k):
                slot = k % 2
                stage[...] = acc[other, :].astype(BF16)
                swap = _remote(px_send.at[k], px_recv.at[k], stage, land.at[slot], (x, y, 1 - c))
                swap.start()
                swap.wait_recv()
                pair = acc[mine, :] + land[slot].astype(F32)
                if k >= 2:
                    to_owner(k - 2).wait_send()
                sumbuf[slot] = pair.astype(BF16)
                swap.wait_send()
                if k < N_CHIPS - 1:
                    to_owner(k).start()
                else:
                    keep = pltpu.make_async_copy(sumbuf.at[slot], own_ref, own_sem)
                    keep.start()
                    for j in range(N_CHIPS - 1):
                        _remote(cs_send.at[j], cs_recv.at[j], sumbuf.at[0], others_ref.at[j], (*chips[j], c)).wait_recv()
                    to_owner(k - 1).wait_send()
                    keep.wait()
                    hook.finish(hins, houts, hscr)

    def chunk(k_pass, i, qc_ref):
        return (jnp.bitwise_xor(qc_ref[0], N_CHIPS - 1 - k_pass), i, 0)

    row = pl.BlockSpec((tm, D_MODEL), lambda k_pass, i, qc_ref: (i, 0))
    act = pl.BlockSpec((None, tm, fj), chunk)
    res = pl.pallas_call(
        body, name="ffn_wgrad_reduce",
        grid_spec=pltpu.PrefetchScalarGridSpec(
            num_scalar_prefetch=1, grid=(N_CHIPS, ni),
            in_specs=[row, row, act, act, act] + [ANY] * n_hin,
            out_specs=[ANY, ANY] + [ANY] * n_hout,
            scratch_shapes=[pltpu.VMEM((3 * fj, D_MODEL), F32), pltpu.VMEM((hrows, D_MODEL), BF16),
                            pltpu.VMEM((2, hrows, D_MODEL), BF16), pltpu.VMEM((2, hrows, D_MODEL), BF16),
                            pltpu.SemaphoreType.DMA((N_CHIPS,)), pltpu.SemaphoreType.DMA((N_CHIPS,)),
                            pltpu.SemaphoreType.DMA((N_CHIPS - 1,)), pltpu.SemaphoreType.DMA((N_CHIPS - 1,)),
                            pltpu.SemaphoreType.DMA] + list(hook.scratch)),
        out_shape=[jax.ShapeDtypeStruct((hrows, D_MODEL), BF16),
                   jax.ShapeDtypeStruct((N_CHIPS - 1, hrows, D_MODEL), BF16)] + list(hook.out_shape),
        compiler_params=_cparams(2),
    )(qc_idx, n, df, dg4, du4, a4, *hook.inputs)
    return res[0], res[1], res[2:]


def _xty(x, y):
    t, k = x.shape
    n = y.shape[1]
    tm = _contract_tile(t)
    tn = n if n <= 1024 else (896 if n % 896 == 0 else 128)

    def body(x_ref, y_ref, o_ref):
        @pl.when(pl.program_id(1) == 0)
        def _():
            o_ref[...] = jnp.zeros_like(o_ref)

        o_ref[...] += _dg(x_ref[...], y_ref[...], TN)

    return pl.pallas_call(
        body, name="xty", grid=(n // tn, t // tm),
        in_specs=[pl.BlockSpec((tm, k), lambda j, i: (i, 0)), pl.BlockSpec((tm, tn), lambda j, i: (i, j))],
        out_specs=pl.BlockSpec((k, tn), lambda j, i: (0, j)),
        out_shape=jax.ShapeDtypeStruct((k, n), F32),
        compiler_params=_cparams(2),
    )(x, y)


def _rope_tables(t):
    pos = (jnp.arange(t, dtype=jnp.int32) - PAD).astype(F32)
    inv_freq = 1.0 / (ROPE_THETA ** (jnp.arange(0, SWA_HD, 2, dtype=F32) / SWA_HD))
    ang = pos[:, None] * inv_freq[None, :]
    cos = jnp.cos(ang)
    sin = jnp.sin(ang)
    return jnp.concatenate([cos, cos, cos, cos], axis=1), jnp.concatenate([-sin, sin, -sin, sin], axis=1)


def _rot_half(x, first_half):
    return jnp.where(first_half, pltpu.roll(x, 96, 1), pltpu.roll(x, 32, 1))


def _first_half_mask(rows):
    lane = lax.broadcasted_iota(jnp.int32, (rows, 128), 1)
    return (lane % 64) < 32


def _log_sigmoid(z):
    return jnp.minimum(z, 0.0) - jnp.log(1.0 + jnp.exp(-jnp.abs(z)))


def _mix_proj(h1, wmixpre, winp, wa2p, bap, cos, sin):
    t = h1.shape[0]
    tm = _row_tile(t)

    def body(h_ref, w_ref, win_ref, wa2_ref, ba_ref, cos_ref, sin_ref,
             n_ref, gq_ref, gk_ref, gv_ref, gg_ref, ga_ref, la_ref, sq_ref, sk_ref, sv_ref):
        y, _, _ = _rms(h_ref[...], w_ref[...])
        n = y.astype(BF16)
        n_ref[...] = n
        proj = _dot(n, win_ref[...])
        gq_ref[...] = proj[:, P_GQ:P_GK]
        gk_ref[...] = proj[:, P_GK:P_GV]
        gv_ref[...] = proj[:, P_GV:P_GG]
        gg_ref[...] = proj[:, P_GG:P_GA]
        ga = proj[:, P_GA:P_SQ]
        ga_ref[...] = ga
        z = _dot(ga.astype(BF16), wa2_ref[...]) + ba_ref[...]
        la_ref[...] = _log_sigmoid(z) * (1.0 / GLA_TAU)
        c = cos_ref[...]
        s = sin_ref[...]
        fh = _first_half_mask(tm)
        for k in range(4):
            x = proj[:, P_SQ + 128 * k:P_SQ + 128 * (k + 1)]
            sq_ref[:, 128 * k:128 * (k + 1)] = (x * c + _rot_half(x, fh) * s).astype(BF16)
        for k in range(2):
            x = proj[:, P_SK + 128 * k:P_SK + 128 * (k + 1)]
            sk_ref[:, 128 * k:128 * (k + 1)] = (x * c + _rot_half(x, fh) * s).astype(BF16)
        sv_ref[...] = proj[:, P_SV:P_END].astype(BF16)

    def row(w):
        return pl.BlockSpec((tm, w), lambda i: (i, 0))

    def rshape(w, dt):
        return jax.ShapeDtypeStruct((t, w), dt)

    return pl.pallas_call(
        body, name="mix_proj", grid=(t // tm,),
        in_specs=[row(D_MODEL), _full((1, D_MODEL)), _full((D_MODEL, P_END)), _full((128, GLA_KW)),
                  _full((1, GLA_KW)), row(128), row(128)],
        out_specs=[row(D_MODEL), row(256), row(256), row(512), row(512), row(128), row(256), row(512), row(256),
                   row(256)],
        out_shape=[rshape(D_MODEL, BF16), rshape(256, F32), rshape(256, F32), rshape(512, F32), rshape(512, F32),
                   rshape(128, F32), rshape(256, F32), rshape(512, BF16), rshape(256, BF16), rshape(256, BF16)],
        compiler_params=_cparams(1),
    )(h1, wmixpre, winp, wa2p, bap, cos, sin)


def _scan_rows(x, reverse=False):
    n = x.shape[0]
    row = lax.broadcasted_iota(jnp.int32, x.shape, 0)
    s = 1
    while s < n:
        if reverse:
            x = x + jnp.where(row < n - s, pltpu.roll(x, n - s, 0), 0.0)
        else:
            x = x + jnp.where(row >= s, pltpu.roll(x, s, 0), 0.0)
        s *= 2
    return x


def _gla_cumsum(la, tril_f):
    b = _scan_rows(la)
    row = lax.broadcasted_iota(jnp.int32, b.shape, 0)
    bm = jnp.sum(jnp.where(row == GLA_CHUNK // 2 - 1, b, 0.0), axis=0, keepdims=True)
    bl = jnp.sum(jnp.where(row == GLA_CHUNK - 1, b, 0.0), axis=0, keepdims=True)
    return b, bm, bl


def _gla_decays(la, tril_f):
    b, bm, bl = _gla_cumsum(la, tril_f)
    return jnp.exp(b - bm), jnp.exp(bm - b), jnp.exp(b), jnp.exp(bl - b), jnp.exp(bl)


def _gla_masks():
    c = GLA_CHUNK
    r = lax.broadcasted_iota(jnp.int32, (c, c), 0)
    col = lax.broadcasted_iota(jnp.int32, (c, c), 1)
    r4 = lax.broadcasted_iota(jnp.int32, (GLA_HEADS * c, c), 0) % c
    c4 = lax.broadcasted_iota(jnp.int32, (GLA_HEADS * c, c), 1)
    klane = lax.broadcasted_iota(jnp.int32, (c, GLA_KW), 1) // GLA_DK
    vlane = lax.broadcasted_iota(jnp.int32, (c, GLA_W), 1) // GLA_DV
    srow = lax.broadcasted_iota(jnp.int32, (GLA_W, GLA_KW), 0) // GLA_DV
    scol = lax.broadcasted_iota(jnp.int32, (GLA_W, GLA_KW), 1) // GLA_DK
    return dict(tril_f=(r >= col).astype(F32), triu_f=(r <= col).astype(F32), tril4=r4 >= c4,
                khead=[klane == h for h in range(GLA_HEADS)], vhead=[vlane == h for h in range(GLA_HEADS)],
                diag=srow == scol)


def _stack_heads(x, head_masks):
    return jnp.concatenate([jnp.where(m, x, 0.0) for m in head_masks], axis=0)


def _gla_fwd(gq, gk, gv, la):
    t = gq.shape[0]
    rg = _seq_tile(t)
    nb = t // rg
    ncb = rg // GLA_CHUNK
    c = GLA_CHUNK

    def body(q_ref, k_ref, v_ref, la_ref, o_ref, ss_ref, st_ref):
        @pl.when(pl.program_id(0) == 0)
        def _():
            st_ref[...] = jnp.zeros_like(st_ref)

        mk = _gla_masks()
        st = st_ref[...]
        for ch in range(ncb):
            rows = slice(ch * c, (ch + 1) * c)
            eq, ek, eb, ekl, ebl = _gla_decays(la_ref[rows, :], mk["tril_f"])
            qs = q_ref[rows, :] * (GLA_DK ** -0.5)
            k = k_ref[rows, :]
            v = v_ref[rows, :].astype(BF16)
            ss_ref[ch] = st
            q4 = _stack_heads(qs * eq, mk["khead"]).astype(BF16)
            a4 = jnp.where(mk["tril4"], _dg(q4, (k * ek).astype(BF16), NT), 0.0).astype(BF16)
            r4 = _dot(a4, v)
            intra = jnp.concatenate([r4[h * c:(h + 1) * c, GLA_DV * h:GLA_DV * (h + 1)] for h in range(GLA_HEADS)],
                                    axis=1)
            o_ref[rows, :] = intra + _dg((qs * eb).astype(BF16), st.astype(BF16), NT)
            st = st * ebl + jnp.where(mk["diag"], _dg(v, (k * ekl).astype(BF16), TN), 0.0)
        st_ref[...] = st

    def row(w):
        return pl.BlockSpec((rg, w), lambda i: (i, 0))

    return pl.pallas_call(
        body, name="gla_fwd", grid=(nb,),
        in_specs=[row(256), row(256), row(512), row(256)],
        out_specs=[row(512), pl.BlockSpec((ncb, GLA_W, GLA_KW), lambda i: (i, 0, 0))],
        out_shape=[jax.ShapeDtypeStruct((t, GLA_W), F32), jax.ShapeDtypeStruct((nb * ncb, GLA_W, GLA_KW), F32)],
        scratch_shapes=[pltpu.VMEM((GLA_W, GLA_KW), F32)],
        compiler_params=_cparams(1),
    )(gq, gk, gv, la)


def _gla_bwd(gq, gk, gv, la, ss, do):
    t = gq.shape[0]
    rg = _seq_tile(t)
    nb = t // rg
    ncb = rg // GLA_CHUNK
    c = GLA_CHUNK

    def body(q_ref, k_ref, v_ref, la_ref, ss_ref, do_ref, dq_ref, dk_ref, dv_ref, dla_ref, dst_ref):
        @pl.when(pl.program_id(0) == 0)
        def _():
            dst_ref[...] = jnp.zeros_like(dst_ref)

        mk = _gla_masks()
        last_row = lax.broadcasted_iota(jnp.int32, (c, GLA_KW), 0) == c - 1
        scale = GLA_DK ** -0.5
        dstn = dst_ref[...]
        for ch in reversed(range(ncb)):
            rows = slice(ch * c, (ch + 1) * c)
            eq, ek, eb, ekl, ebl = _gla_decays(la_ref[rows, :], mk["tril_f"])
            qs = q_ref[rows, :] * scale
            k = k_ref[rows, :]
            qt, kt, qh, kh = qs * eq, k * ek, qs * eb, k * ekl
            ktb, khb, qhb = kt.astype(BF16), kh.astype(BF16), qh.astype(BF16)
            v = v_ref[rows, :].astype(BF16)
            do_f = do_ref[rows, :]
            dob = do_f.astype(BF16)
            st = ss_ref[ch]
            stb = st.astype(BF16)
            dstb = dstn.astype(BF16)
            q4 = _stack_heads(qt, mk["khead"]).astype(BF16)
            do4 = _stack_heads(do_f, mk["vhead"]).astype(BF16)
            a4 = jnp.where(mk["tril4"], _dg(q4, ktb, NT), 0.0).astype(BF16)
            da4 = jnp.where(mk["tril4"], _dg(do4, v, NT), 0.0).astype(BF16)
            dv_ref[rows, :] = _dg(a4, do4, TN) + _dg(khb, dstb, NT)
            dq4 = _dot(da4, ktb)
            dqt = jnp.zeros((c, GLA_KW), F32)
            for h in range(GLA_HEADS):
                dqt = dqt + jnp.where(mk["khead"][h], dq4[h * c:(h + 1) * c], 0.0)
            dkt = _dg(da4, q4, TN)
            dqh = _dot(dob, stb)
            dkh = _dot(v, dstb)
            dbl = jnp.sum(dstn * st, axis=0, keepdims=True)
            dstn = dstn * ebl + jnp.where(mk["diag"], _dg(dob, qhb, TN), 0.0)
            dq_ref[rows, :] = scale * (dqt * eq + dqh * eb)
            dk_ref[rows, :] = dkt * ek + dkh * ekl
            dkk = dkh * kh
            db = dqt * qt - dkt * kt + dqh * qh - dkk
            db = db + jnp.where(last_row, jnp.sum(dkk, axis=0, keepdims=True) + ebl * dbl, 0.0)
            dla_ref[rows, :] = _scan_rows(db, reverse=True)
        dst_ref[...] = dstn

    def row(w):
        return pl.BlockSpec((rg, w), lambda i: (nb - 1 - i, 0))

    def rshape(w):
        return jax.ShapeDtypeStruct((t, w), F32)

    return pl.pallas_call(
        body, name="gla_bwd", grid=(nb,),
        in_specs=[row(256), row(256), row(512), row(256),
                  pl.BlockSpec((ncb, GLA_W, GLA_KW), lambda i: (nb - 1 - i, 0, 0)), row(512)],
        out_specs=[row(256), row(256), row(512), row(256)],
        out_shape=[rshape(256), rshape(256), rshape(512), rshape(256)],
        scratch_shapes=[pltpu.VMEM((GLA_W, GLA_KW), F32)],
        compiler_params=_cparams(1),
    )(gq, gk, gv, la, ss, do)


SWA_G = SWA_QH // SWA_KVH


def _swa_bias():
    n = jnp.arange(3, dtype=jnp.int32)[:, None, None]
    r = (jnp.arange(SWA_G * BLK, dtype=jnp.int32) % BLK)[None, :, None]
    c = jnp.arange(3 * BLK, dtype=jnp.int32)[None, None, :]
    seg = c // BLK
    cc = c % BLK
    qpos = n * BLK + r - PAD
    kpos = jnp.where(seg == 0, (n - 1) * BLK, jnp.where(seg == 1, n * BLK, 0)) + cc - PAD
    band = (seg < 2) & (kpos >= N_META) & (kpos <= qpos) & (qpos - kpos < WINDOW)
    meta = (seg == 2) & (kpos >= 0) & (kpos < N_META) & (kpos <= qpos)
    return jnp.where(band | meta, 0.0, NEG_INF).astype(F32)


def _swa_stack(ref, rows, kh, lo, dtype):
    parts = []
    for g in range(2):
        pair = ref[rows, 128 * (2 * kh + g):128 * (2 * kh + g + 1)]
        zero = jnp.zeros_like(pair)
        parts += [jnp.where(lo, pair, zero), jnp.where(lo, zero, pair)]
    return jnp.concatenate(parts, axis=0).astype(dtype)


def _swa_unstack(x4, lo):
    return [jnp.where(lo, x4[2 * g * BLK:(2 * g + 1) * BLK], x4[(2 * g + 1) * BLK:(2 * g + 2) * BLK])
            for g in range(2)]


def _swa_sink_col(sink_ref, kh):
    blk = lax.broadcasted_iota(jnp.int32, (SWA_G * BLK, 1), 0) // BLK
    col = jnp.full((SWA_G * BLK, 1), sink_ref[SWA_G * kh + SWA_G - 1], F32)
    for e in reversed(range(SWA_G - 1)):
        col = jnp.where(blk == e, sink_ref[SWA_G * kh + e], col)
    return col


def _swa_softmax(qk, bias, sink):
    s = qk * (SWA_HD ** -0.5) + bias
    m = jnp.maximum(jnp.max(s, axis=-1, keepdims=True), sink)
    p = jnp.exp(s - m)
    es = jnp.exp(sink - m)
    inv = 1.0 / (jnp.sum(p, axis=-1, keepdims=True) + es)
    return p * inv, es * inv


def _swa_keys(prev_ref, cur_ref, first_ref, b, ls):
    before = prev_ref[:, ls] if b == 0 else cur_ref[(b - 1) * BLK:b * BLK, ls]
    return jnp.concatenate([before, cur_ref[b * BLK:(b + 1) * BLK, ls], first_ref[:, ls]], axis=0)


def _swa_specs(rs, ns):
    bps = rs // BLK
    cur = lambda w: pl.BlockSpec((rs, w), lambda i: (jnp.minimum(i, ns - 1), 0))
    prev = lambda w: pl.BlockSpec((BLK, w), lambda i: (jnp.maximum(jnp.minimum(i, ns - 1) * bps - 1, 0), 0))
    first = lambda w: pl.BlockSpec((BLK, w), lambda i: (0, 0))
    return cur, prev, first


def _swa_fwd(sinks, sq, sk, sv):
    t = sq.shape[0]
    rs = _seq_tile(t)
    bps, ns = rs // BLK, t // rs

    def body(sink_ref, bias_ref, q_ref, kp_ref, kc_ref, km_ref, vp_ref, vc_ref, vm_ref, o_ref):
        i = pl.program_id(0)
        lo = lax.broadcasted_iota(jnp.int32, (BLK, 128), 1) < 64
        sink_cols = [_swa_sink_col(sink_ref, kh) for kh in range(SWA_KVH)]
        chains = [(b, kh) for b in range(bps) for kh in range(SWA_KVH)]
        scores = []
        for b, kh in chains:
            ls = slice(128 * kh, 128 * (kh + 1))
            q4 = _swa_stack(q_ref, slice(b * BLK, (b + 1) * BLK), kh, lo, BF16)
            scores.append(_dg(q4, _swa_keys(kp_ref, kc_ref, km_ref, b, ls), NT))
        probs = []
        for (b, kh), s in zip(chains, scores):
            p, _ = _swa_softmax(s, bias_ref[jnp.minimum(i * bps + b, 2)], sink_cols[kh])
            probs.append(p.astype(BF16))
        for (b, kh), p in zip(chains, probs):
            ls = slice(128 * kh, 128 * (kh + 1))
            rows = slice(b * BLK, (b + 1) * BLK)
            for g, pair in enumerate(_swa_unstack(_dot(p, _swa_keys(vp_ref, vc_ref, vm_ref, b, ls)), lo)):
                o_ref[rows, 128 * (2 * kh + g):128 * (2 * kh + g + 1)] = pair

    cur, prev, first = _swa_specs(rs, ns)
    bias = _swa_bias()
    return pl.pallas_call(
        body, name="swa_fwd", grid=(ns,),
        in_specs=[pl.BlockSpec(memory_space=pltpu.SMEM), _full(bias.shape), cur(512), prev(256), cur(256), first(256),
                  prev(256), cur(256), first(256)],
        out_specs=cur(512),
        out_shape=jax.ShapeDtypeStruct((t, SWA_W), F32),
        compiler_params=_cparams(1),
    )(sinks, bias, sq, sk, sk, sk, sv, sv, sv)


def _swa_bwd(sinks, sq, sk, sv, o, do, hook=None):
    t = sq.shape[0]
    rs = _seq_tile(t)
    bps, ns = rs // BLK, t // rs

    def body(sink_ref, bias_ref, q_ref, kp_ref, kc_ref, km_ref, vp_ref, vc_ref, vm_ref, o_ref, do_ref,
             dq_ref, dk_ref, dv_ref, dkm_ref, dvm_ref, dsink_ref, pk_ref, pv_ref):
        i = pl.program_id(0)

        @pl.when(i == 0)
        def _():
            pk_ref[...] = jnp.zeros_like(pk_ref)
            pv_ref[...] = jnp.zeros_like(pv_ref)
            dkm_ref[...] = jnp.zeros_like(dkm_ref)
            dvm_ref[...] = jnp.zeros_like(dvm_ref)
            dsink_ref[...] = jnp.zeros_like(dsink_ref)

        @pl.when(i == ns)
        def _():
            dk_ref[...] = pk_ref[...]
            dv_ref[...] = pv_ref[...]

        @pl.when(i < ns)
        def _():
            lo = lax.broadcasted_iota(jnp.int32, (BLK, 128), 1) < 64
            scale = SWA_HD ** -0.5
            sink_cols = [_swa_sink_col(sink_ref, kh) for kh in range(SWA_KVH)]
            parts_k = [[None] * SWA_KVH for _ in range(bps)]
            parts_v = [[None] * SWA_KVH for _ in range(bps)]
            dsinks = [jnp.zeros((1, 1), F32) for _ in range(SWA_QH)]
            chains = [(b, kh) for b in range(bps) for kh in range(SWA_KVH)]
            lanes = lambda kh: slice(128 * kh, 128 * (kh + 1))
            block = lambda b: slice(b * BLK, (b + 1) * BLK)
            q4s = [_swa_stack(q_ref, block(b), kh, lo, BF16) for b, kh in chains]
            scores = [_dg(q4, _swa_keys(kp_ref, kc_ref, km_ref, b, lanes(kh)), NT)
                      for (b, kh), q4 in zip(chains, q4s)]
            do4s = [_swa_stack(do_ref, block(b), kh, lo, F32) for b, kh in chains]
            do4bs = [d.astype(BF16) for d in do4s]
            dps = [_dg(d, _swa_keys(vp_ref, vc_ref, vm_ref, b, lanes(kh)), NT) for (b, kh), d in zip(chains, do4bs)]
            pbs, dss = [], []
            for n_chain, (b, kh) in enumerate(chains):
                p, psink = _swa_softmax(scores[n_chain], bias_ref[jnp.minimum(i * bps + b, 2)], sink_cols[kh])
                delta = jnp.sum(do4s[n_chain] * _swa_stack(o_ref, block(b), kh, lo, F32), axis=-1, keepdims=True)
                dss.append((p * (dps[n_chain] - delta) * scale).astype(BF16))
                pbs.append(p.astype(BF16))
                dsk = psink * delta
                for e in range(SWA_G):
                    h = SWA_G * kh + e
                    dsinks[h] = dsinks[h] - jnp.sum(dsk[e * BLK:(e + 1) * BLK], axis=0, keepdims=True)
            for n_chain, (b, kh) in enumerate(chains):
                kall = _swa_keys(kp_ref, kc_ref, km_ref, b, lanes(kh))
                for g, pair in enumerate(_swa_unstack(_dot(dss[n_chain], kall), lo)):
                    dq_ref[block(b), 128 * (2 * kh + g):128 * (2 * kh + g + 1)] = pair
                parts_k[b][kh] = _dg(dss[n_chain], q4s[n_chain], TN)
                parts_v[b][kh] = _dg(pbs[n_chain], do4bs[n_chain], TN)
            last = slice(rs - BLK, rs)
            for parts, out_ref, pend_ref, meta_ref in ((parts_k, dk_ref, pk_ref, dkm_ref),
                                                       (parts_v, dv_ref, pv_ref, dvm_ref)):
                for kh in range(SWA_KVH):
                    ls = slice(128 * kh, 128 * (kh + 1))
                    if bps > 1:
                        out_ref[0:rs - BLK, ls] = pend_ref[0:rs - BLK, ls]
                    out_ref[last, ls] = pend_ref[last, ls] + parts[0][kh][0:BLK]
                    meta = parts[0][kh][2 * BLK:3 * BLK]
                    for b in range(bps):
                        own = parts[b][kh][BLK:2 * BLK]
                        if b + 1 < bps:
                            own = own + parts[b + 1][kh][0:BLK]
                            meta = meta + parts[b + 1][kh][2 * BLK:3 * BLK]
                        pend_ref[b * BLK:(b + 1) * BLK, ls] = own
                    meta_ref[:, ls] += meta
            for h in range(SWA_QH):
                dsink_ref[h:h + 1, :] += jnp.broadcast_to(dsinks[h], (1, 128))

    cur, prev, first = _swa_specs(rs, ns)
    late = lambda w: pl.BlockSpec((rs, w), lambda i: (jnp.maximum(i - 1, 0), 0))
    bias = _swa_bias()
    return _pallas(
        body, name="swa_bwd", grid=(ns + 1,),
        in_specs=[pl.BlockSpec(memory_space=pltpu.SMEM), _full(bias.shape), cur(512), prev(256), cur(256), first(256),
                  prev(256), cur(256), first(256), cur(512), cur(512)],
        out_specs=[cur(512), late(256), late(256), first(256), first(256), _full((SWA_QH, 128))],
        out_shape=[jax.ShapeDtypeStruct((t, SWA_W), F32), jax.ShapeDtypeStruct((t, 256), F32),
                   jax.ShapeDtypeStruct((t, 256), F32), jax.ShapeDtypeStruct((BLK, 256), F32),
                   jax.ShapeDtypeStruct((BLK, 256), F32), jax.ShapeDtypeStruct((SWA_QH, 128), F32)],
        scratch_shapes=[pltpu.VMEM((rs, 256), F32), pltpu.VMEM((rs, 256), F32)],
        args=(sinks, bias, sq, sk, sk, sk, sv, sv, sv, o, do), hook=hook)


def _mix_out(h1, ogla, gg, oswa, wgn, wsn, wout, wpost):
    t = h1.shape[0]
    tm = _row_tile(t)

    def body(h_ref, og_ref, gg_ref, os_ref, wgn_ref, wsn_ref, wout_ref, wpost_ref, h2_ref, cat_ref, m_ref):
        parts = []
        for h in range(GLA_HEADS):
            ls = slice(GLA_DV * h, GLA_DV * (h + 1))
            y, _, _ = _rms(og_ref[:, ls], wgn_ref[...])
            g = gg_ref[:, ls]
            parts.append(y * (g * _sigmoid(g)))
        ys, _, _ = _rms(os_ref[...], wsn_ref[...])
        cat = jnp.concatenate(parts + [ys], axis=1).astype(BF16)
        cat_ref[...] = cat
        m = _dot(cat, wout_ref[...])
        m_ref[...] = m
        y, _, _ = _rms(m, wpost_ref[...])
        h2_ref[...] = h_ref[...] + y

    def row(w):
        return pl.BlockSpec((tm, w), lambda i: (i, 0))

    return pl.pallas_call(
        body, name="mix_out", grid=(t // tm,),
        in_specs=[row(D_MODEL), row(512), row(512), row(512), _full((1, GLA_DV)), _full((1, SWA_W)),
                  _full((D_MODEL, D_MODEL)), _full((1, D_MODEL))],
        out_specs=[row(D_MODEL), row(D_MODEL), row(D_MODEL)],
        out_shape=[jax.ShapeDtypeStruct((t, D_MODEL), F32), jax.ShapeDtypeStruct((t, D_MODEL), BF16),
                   jax.ShapeDtypeStruct((t, D_MODEL), F32)],
        compiler_params=_cparams(1),
    )(h1, ogla, gg, oswa, wgn, wsn, wout, wpost)


def _mix_out_bwd(dh2, m, ogla, gg, oswa, wgn, wsn, wout, wpost, hook=None):
    t = dh2.shape[0]
    tm = _row_tile(t)

    def body(dh_ref, m_ref, og_ref, gg_ref, os_ref, wgn_ref, wsn_ref, wout_ref, wpost_ref,
             dog_ref, dgg_ref, dos_ref, dm_ref, dwpost_ref, dwgn_ref, dwsn_ref):
        @pl.when(pl.program_id(0) == 0)
        def _():
            dwpost_ref[...] = jnp.zeros_like(dwpost_ref)
            dwgn_ref[...] = jnp.zeros_like(dwgn_ref)
            dwsn_ref[...] = jnp.zeros_like(dwsn_ref)

        wpost = wpost_ref[...]
        _, mh, r = _rms(m_ref[...], wpost)
        dm, dw = _rms_bwd(mh, r, wpost, dh_ref[...])
        dwpost_ref[...] += dw
        dmb = dm.astype(BF16)
        dm_ref[...] = dmb
        dcat = _dg(dmb, wout_ref[...], NT)
        wgn = wgn_ref[...]
        for h in range(GLA_HEADS):
            ls = slice(GLA_DV * h, GLA_DV * (h + 1))
            dog = dcat[:, ls]
            g = gg_ref[:, ls]
            sg = _sigmoid(g)
            y, xh, r = _rms(og_ref[:, ls], wgn)
            dgg_ref[:, ls] = dog * y * (sg * (1.0 + g * (1.0 - sg)))
            dx, dw = _rms_bwd(xh, r, wgn, dog * (g * sg))
            dog_ref[:, ls] = dx
            dwgn_ref[...] += dw
        wsn = wsn_ref[...]
        _, xh, r = _rms(os_ref[...], wsn)
        dx, dw = _rms_bwd(xh, r, wsn, dcat[:, GLA_W:])
        dos_ref[...] = dx
        dwsn_ref[...] += dw

    def row(w):
        return pl.BlockSpec((tm, w), lambda i: (i, 0))

    def rshape(w, dt=F32):
        return jax.ShapeDtypeStruct((t, w), dt)

    return _pallas(
        body, name="mix_out_bwd", grid=(t // tm,),
        in_specs=[row(D_MODEL), row(D_MODEL), row(512), row(512), row(512), _full((1, GLA_DV)), _full((1, SWA_W)),
                  _full((D_MODEL, D_MODEL)), _full((1, D_MODEL))],
        out_specs=[row(512), row(512), row(512), row(D_MODEL), _full((1, D_MODEL)), _full((1, GLA_DV)),
                   _full((1, SWA_W))],
        out_shape=[rshape(512), rshape(512), rshape(512), rshape(D_MODEL, BF16),
                   jax.ShapeDtypeStruct((1, D_MODEL), F32), jax.ShapeDtypeStruct((1, GLA_DV), F32),
                   jax.ShapeDtypeStruct((1, SWA_W), F32)],
        args=(dh2, m, ogla, gg, oswa, wgn, wsn, wout, wpost), hook=hook)


def _mix_in_bwd(dh2, h1, wmixpre, winp, wa2p, bap, cos, sin, ga, dgq, dgk, dgv, dgg, dla, dsq, dsk, dsv, dkm, dvm):
    t = h1.shape[0]
    tm = _row_tile(t)

    def body(dh2_ref, h_ref, w_ref, win_ref, wa2_ref, ba_ref, cos_ref, sin_ref, ga_ref, dgq_ref, dgk_ref, dgv_ref,
             dgg_ref, dla_ref, dsq_ref, dsk_ref, dsv_ref, dkm_ref, dvm_ref,
             dh1_ref, dproj_ref, dw_ref, dwa2_ref, dba_ref):
        i = pl.program_id(0)

        @pl.when(i == 0)
        def _():
            dw_ref[...] = jnp.zeros_like(dw_ref)
            dwa2_ref[...] = jnp.zeros_like(dwa2_ref)
            dba_ref[...] = jnp.zeros_like(dba_ref)

        first = (i == 0).astype(F32)
        c = cos_ref[...]
        s = -sin_ref[...]
        fh = _first_half_mask(tm)
        dproj_ref[:, P_GQ:P_GK] = dgq_ref[...].astype(BF16)
        dproj_ref[:, P_GK:P_GV] = dgk_ref[...].astype(BF16)
        dproj_ref[:, P_GV:P_GG] = dgv_ref[...].astype(BF16)
        dproj_ref[:, P_GG:P_GA] = dgg_ref[...].astype(BF16)
        gab = ga_ref[...].astype(BF16)
        z = _dot(gab, wa2_ref[...]) + ba_ref[...]
        row_id = i * tm + lax.broadcasted_iota(jnp.int32, (tm, 1), 0)
        dz = jnp.where(row_id >= PAD, dla_ref[...] * (1.0 / GLA_TAU) * (1.0 - _sigmoid(z)), 0.0)
        dzb = dz.astype(BF16)
        dba_ref[...] += jnp.sum(dz, axis=0, keepdims=True)
        dwa2_ref[...] += _dg(gab, dzb, TN)
        dproj_ref[:, P_GA:P_SQ] = _dg(dzb, wa2_ref[...], NT).astype(BF16)
        for k in range(4):
            dy = dsq_ref[:, 128 * k:128 * (k + 1)]
            dproj_ref[:, P_SQ + 128 * k:P_SQ + 128 * (k + 1)] = (dy * c + _rot_half(dy, fh) * s).astype(BF16)
        for k in range(2):
            ls = slice(128 * k, 128 * (k + 1))
            dy = dsk_ref[:, ls]
            dy = jnp.concatenate([dy[:BLK] + first * dkm_ref[:, ls], dy[BLK:]], axis=0) if tm > BLK else (
                dy + first * dkm_ref[:, ls])
            dproj_ref[:, P_SK + 128 * k:P_SK + 128 * (k + 1)] = (dy * c + _rot_half(dy, fh) * s).astype(BF16)
            dv = dsv_ref[:, ls]
            dv = jnp.concatenate([dv[:BLK] + first * dvm_ref[:, ls], dv[BLK:]], axis=0) if tm > BLK else (
                dv + first * dvm_ref[:, ls])
            dproj_ref[:, P_SV + 128 * k:P_SV + 128 * (k + 1)] = dv.astype(BF16)
        dn = _dg(dproj_ref[...], win_ref[...], NT)
        w = w_ref[...]
        _, hh, r = _rms(h_ref[...], w)
        dx, dw = _rms_bwd(hh, r, w, dn)
        dw_ref[...] += dw
        dh1_ref[...] = dh2_ref[...] + dx

    def row(w):
        return pl.BlockSpec((tm, w), lambda i: (i, 0))

    return pl.pallas_call(
        body, name="mix_in_bwd", grid=(t // tm,),
        in_specs=[row(D_MODEL), row(D_MODEL), _full((1, D_MODEL)), _full((D_MODEL, P_END)), _full((128, GLA_KW)),
                  _full((1, GLA_KW)), row(128), row(128), row(128), row(256), row(256), row(512), row(512), row(256),
                  row(512), row(256), row(256), _full((BLK, 256)), _full((BLK, 256))],
        out_specs=[row(D_MODEL), row(P_END), _full((1, D_MODEL)), _full((128, GLA_KW)), _full((1, GLA_KW))],
        out_shape=[jax.ShapeDtypeStruct((t, D_MODEL), F32), jax.ShapeDtypeStruct((t, P_END), BF16),
                   jax.ShapeDtypeStruct((1, D_MODEL), F32), jax.ShapeDtypeStruct((128, GLA_KW), F32),
                   jax.ShapeDtypeStruct((1, GLA_KW), F32)],
        compiler_params=_cparams(1),
    )(dh2, h1, wmixpre, winp, wa2p, bap, cos, sin, ga, dgq, dgk, dgv, dgg, dla, dsq, dsk, dsv, dkm, dvm)


def _adamw_update(w, g, m, v):
    m = ADAM_B1 * m + (1.0 - ADAM_B1) * g
    v = ADAM_B2 * v + (1.0 - ADAM_B2) * (g * g)
    m_hat = m / (1.0 - ADAM_B1 ** ADAM_STEP)
    v_hat = v / (1.0 - ADAM_B2 ** ADAM_STEP)
    return -ADAM_LR * (m_hat / (jnp.sqrt(v_hat) + ADAM_EPS) + ADAM_WD * w), m, v


def _adamw_halves(w, g_mine, g_other, m, v, c_idx, row0=0):
    r, c = w.shape
    h = g_mine.shape[0]
    tr = _div_tile(math.gcd(r, h))
    nth = h // tr
    t0 = row0 // tr
    assert t0 * tr == row0

    def body(c_ref, w_ref, gm_ref, go_ref, m_ref, v_ref, g_ref, d_ref, nm_ref, nv_ref):
        hh = (t0 + pl.program_id(0)) // nth
        g = jnp.where(hh == c_ref[0], gm_ref[...], go_ref[...])
        g_ref[...] = g
        d_ref[...], nm_ref[...], nv_ref[...] = _adamw_update(w_ref[...], g, m_ref[...], v_ref[...])

    spec = pl.BlockSpec((tr, c), lambda i, c_ref: (i, 0))

    def gspec(is_mine):
        def index(i, c_ref):
            used = ((t0 + i) // nth == c_ref[0]) == is_mine
            return (jnp.where(used, (t0 + i) % nth, 0), 0)
        return pl.BlockSpec((tr, c), index)

    shape = jax.ShapeDtypeStruct((r, c), F32)
    return pl.pallas_call(
        body, name="adamw_halves",
        grid_spec=pltpu.PrefetchScalarGridSpec(
            num_scalar_prefetch=1, grid=(r // tr,), in_specs=[spec, gspec(True), gspec(False), spec, spec],
            out_specs=[spec] * 4),
        out_shape=[shape] * 4, compiler_params=_cparams(1),
    )(c_idx, w, g_mine, g_other, m, v)


def _place():
    x, y, c = lax.axis_index("x"), lax.axis_index("y"), lax.axis_index("c")
    chips = [(1 - x, y), (x, 1 - y), (1 - x, 1 - y)]
    return x, y, c, chips


def _remote(send_sem, recv_sem, src, dst, to):
    return pltpu.make_async_remote_copy(src_ref=src, dst_ref=dst, send_sem=send_sem, recv_sem=recv_sem,
                                        device_id=to, device_id_type=MESH)


def _half(ref_rows, c):
    h = ref_rows // 2
    return pl.ds(pl.multiple_of(c * h, 8), h)


def _own_slot(shard, q):
    return lax.dynamic_update_slice(jnp.zeros((N_CHIPS,) + shard.shape, shard.dtype), shard[None], (q, 0, 0))


def _stack_own_slot(mats, q_idx):
    r, w = mats[0].shape
    tr = _div_tile(r)
    per = r // tr
    n = len(mats)

    def body(q_ref, *refs):
        m_refs, o_ref = refs[:n], refs[n]
        s = pl.program_id(0)
        for k in range(n):
            @pl.when(s // per == k)
            def _(k=k):
                o_ref[...] = m_refs[k][...].astype(BF16)

    def rows_of(k):
        return lambda s, q_ref: (jnp.where(s // per == k, s % per, 0), 0)

    return pl.pallas_call(
        body, name="stack_own_slot",
        grid_spec=pltpu.PrefetchScalarGridSpec(
            num_scalar_prefetch=1, grid=(n * per,),
            in_specs=[pl.BlockSpec((tr, w), rows_of(k)) for k in range(n)],
            out_specs=pl.BlockSpec((None, tr, w), lambda s, q_ref: (q_ref[0], s, 0))),
        out_shape=jax.ShapeDtypeStruct((N_CHIPS, n * r, w), BF16), compiler_params=_cparams(1),
    )(q_idx, *mats)


class _GatherChips:
    has_mid = True

    def __init__(self, bufs):
        n = len(bufs)
        self.inputs = list(bufs)
        self.out_shape = [jax.ShapeDtypeStruct(b.shape, b.dtype) for b in bufs]
        self.aliases = [(t, t) for t in range(n)]
        self.scratch = [pltpu.SemaphoreType.DMA((n, 6)), pltpu.SemaphoreType.DMA((n, 6))]

    def start(self, ins, outs, scr):
        send, recv = scr
        x, y, c, chips = _place()
        q = 2 * x + y
        for t, (i_ref, o_ref) in enumerate(zip(ins, outs)):
            rows = _half(i_ref.shape[1], c)
            for j, (cx, cy) in enumerate(chips):
                _remote(send.at[t, j], recv.at[t, j], i_ref.at[q, rows], o_ref.at[q, rows], (cx, cy, c)).start()

    def mid(self, ins, outs, scr):
        send, recv = scr
        x, y, c, chips = _place()
        for t, o_ref in enumerate(outs):
            rows = _half(o_ref.shape[1], c)
            for j, (cx, cy) in enumerate(chips):
                slot = o_ref.at[2 * cx + cy, rows]
                _remote(send.at[t, j], recv.at[t, j], slot, slot, (cx, cy, c)).wait_recv()
                _remote(send.at[t, 3 + j], recv.at[t, 3 + j], slot, slot, (x, y, 1 - c)).start()

    def finish(self, ins, outs, scr):
        send, recv = scr
        x, y, c, chips = _place()
        for t, o_ref in enumerate(outs):
            mine, other = _half(o_ref.shape[1], c), _half(o_ref.shape[1], 1 - c)
            for j, (cx, cy) in enumerate(chips):
                slot = o_ref.at[2 * cx + cy, other]
                _remote(send.at[t, 3 + j], recv.at[t, 3 + j], slot, slot, (x, y, 1 - c)).wait_recv()
            for j, (cx, cy) in enumerate(chips):
                sent = o_ref.at[2 * cx + cy, mine]
                _remote(send.at[t, j], recv.at[t, j], sent, sent, (cx, cy, c)).wait_send()
                _remote(send.at[t, 3 + j], recv.at[t, 3 + j], sent, sent, (x, y, 1 - c)).wait_send()


class _PairExchange:
    has_mid = False
    aliases = ()

    def __init__(self, arrs):
        n = len(arrs)
        self.inputs = list(arrs)
        self.out_shape = [jax.ShapeDtypeStruct((a.shape[0], a.shape[1] // 2, a.shape[2]), a.dtype) for a in arrs]
        self.scratch = [pltpu.SemaphoreType.DMA((n,)), pltpu.SemaphoreType.DMA((n,))]

    def _copies(self, ins, outs, scr):
        send, recv = scr
        x, y, c, _ = _place()
        return [_remote(send.at[t], recv.at[t], i_ref.at[:, _half(i_ref.shape[1], 1 - c)], o_ref, (x, y, 1 - c))
                for t, (i_ref, o_ref) in enumerate(zip(ins, outs))]

    def start(self, ins, outs, scr):
        for cp in self._copies(ins, outs, scr):
            cp.start()

    def finish(self, ins, outs, scr):
        for cp in self._copies(ins, outs, scr):
            cp.wait()


class _ChipScatter:
    has_mid = False
    aliases = ()

    def __init__(self, arrs):
        n = len(arrs)
        self.inputs = list(arrs)
        self.out_shape = [jax.ShapeDtypeStruct((3,) + a.shape[1:], a.dtype) for a in arrs]
        self.scratch = [pltpu.SemaphoreType.DMA((n, 3)), pltpu.SemaphoreType.DMA((n, 3))]

    def _copies(self, ins, outs, scr):
        send, recv = scr
        x, y, c, chips = _place()
        return [_remote(send.at[t, j], recv.at[t, j], i_ref.at[2 * cx + cy], o_ref.at[j], (cx, cy, c))
                for t, (i_ref, o_ref) in enumerate(zip(ins, outs)) for j, (cx, cy) in enumerate(chips)]

    def start(self, ins, outs, scr):
        for cp in self._copies(ins, outs, scr):
            cp.start()

    def finish(self, ins, outs, scr):
        for cp in self._copies(ins, outs, scr):
            cp.wait()


class _PairShare:
    has_mid = False
    aliases = ()

    def __init__(self, arrs):
        n = len(arrs)
        self.inputs = list(arrs)
        self.out_shape = [jax.ShapeDtypeStruct(a.shape, a.dtype) for a in arrs]
        self.scratch = [pltpu.SemaphoreType.DMA((n,)), pltpu.SemaphoreType.DMA((n,))]

    def _copies(self, ins, outs, scr):
        send, recv = scr
        x, y, c, _ = _place()
        return [_remote(send.at[t], recv.at[t], i_ref, o_ref, (x, y, 1 - c))
                for t, (i_ref, o_ref) in enumerate(zip(ins, outs))]

    def start(self, ins, outs, scr):
        for cp in self._copies(ins, outs, scr):
            cp.start()

    def finish(self, ins, outs, scr):
        for cp in self._copies(ins, outs, scr):
            cp.wait()


def _comm_call(hook, name):
    n_in, n_out = len(hook.inputs), len(hook.out_shape)

    def body(*refs):
        ins, outs, scr = refs[:n_in], refs[n_in:n_in + n_out], refs[n_in + n_out:]
        hook.start(ins, outs, scr)
        if hook.has_mid:
            hook.mid(ins, outs, scr)
        hook.finish(ins, outs, scr)

    return pl.pallas_call(body, name=name, in_specs=[ANY] * n_in, out_specs=[ANY] * n_out,
                          out_shape=list(hook.out_shape), scratch_shapes=list(hook.scratch),
                          input_output_aliases=dict(hook.aliases))(*hook.inputs)


class _GatherDevices:
    has_mid = True
    aliases = ()

    def __init__(self, vecs):
        n = len(vecs)
        self.inputs = list(vecs)
        self.out_shape = [jax.ShapeDtypeStruct((N_DEV,) + v.shape, v.dtype) for v in vecs]
        self.scratch = [pltpu.SemaphoreType.DMA((n, 7)), pltpu.SemaphoreType.DMA((n, 7)),
                        pltpu.SemaphoreType.DMA((n,))]

    @staticmethod
    def _copy(scr, t, k, out_ref, block, to, src=None):
        send, recv, _ = scr
        px, py, pc = block
        slot = out_ref.at[4 * px + 2 * py + pc]
        return _remote(send.at[t, k], recv.at[t, k], slot if src is None else src, slot, to)

    def start(self, ins, outs, scr):
        x, y, c, chips = _place()
        me = (x, y, c)
        for t, (x_ref, out_ref) in enumerate(zip(ins, outs)):
            pltpu.make_async_copy(x_ref, out_ref.at[4 * x + 2 * y + c], scr[2].at[t]).start()
            self._copy(scr, t, 0, out_ref, me, (x, y, 1 - c), src=x_ref).start()
            for j, chip in enumerate(chips):
                self._copy(scr, t, 1 + j, out_ref, me, (*chip, c), src=x_ref).start()

    def mid(self, ins, outs, scr):
        x, y, c, chips = _place()
        for t, out_ref in enumerate(outs):
            for j, chip in enumerate(chips):
                self._copy(scr, t, 1 + j, out_ref, (*chip, c), (x, y, c)).wait_recv()
                self._copy(scr, t, 4 + j, out_ref, (*chip, c), (x, y, 1 - c)).start()

    def finish(self, ins, outs, scr):
        x, y, c, chips = _place()
        me = (x, y, c)
        for t, (x_ref, out_ref) in enumerate(zip(ins, outs)):
            self._copy(scr, t, 0, out_ref, (x, y, 1 - c), me).wait_recv()
            for j, chip in enumerate(chips):
                self._copy(scr, t, 4 + j, out_ref, (*chip, 1 - c), me).wait_recv()
            self._copy(scr, t, 0, out_ref, me, (x, y, 1 - c), src=x_ref).wait_send()
            for j, chip in enumerate(chips):
                self._copy(scr, t, 1 + j, out_ref, me, (*chip, c), src=x_ref).wait_send()
                self._copy(scr, t, 4 + j, out_ref, (*chip, c), (x, y, 1 - c)).wait_send()
            pltpu.make_async_copy(x_ref, out_ref.at[4 * x + 2 * y + c], scr[2].at[t]).wait()


class _Hooks:
    def __init__(self, hooks):
        self.hooks = list(hooks)
        self.has_mid = any(h.has_mid for h in hooks)
        self.inputs = [a for h in hooks for a in h.inputs]
        self.out_shape = [s for h in hooks for s in h.out_shape]
        self.scratch = [s for h in hooks for s in h.scratch]
        self.aliases = []
        i0 = o0 = 0
        for h in hooks:
            self.aliases += [(i0 + a, o0 + b) for a, b in h.aliases]
            i0 += len(h.inputs)
            o0 += len(h.out_shape)

    def _each(self, ins, outs, scr):
        i0 = o0 = s0 = 0
        for h in self.hooks:
            ni, no, ns = len(h.inputs), len(h.out_shape), len(h.scratch)
            yield h, ins[i0:i0 + ni], outs[o0:o0 + no], scr[s0:s0 + ns]
            i0, o0, s0 = i0 + ni, o0 + no, s0 + ns

    def start(self, ins, outs, scr):
        for h, i, o, s in self._each(ins, outs, scr):
            h.start(i, o, s)

    def mid(self, ins, outs, scr):
        for h, i, o, s in self._each(ins, outs, scr):
            if h.has_mid:
                h.mid(i, o, s)

    def finish(self, ins, outs, scr):
        for h, i, o, s in self._each(ins, outs, scr):
            h.finish(i, o, s)

    def split(self, outs):
        res, o0 = [], 0
        for h in self.hooks:
            res.append(list(outs[o0:o0 + len(h.out_shape)]))
            o0 += len(h.out_shape)
        return res


def _pair_sum(g, other, c_idx):
    nq, r, w = g.shape
    h = r // 2
    tr = _div_tile(h)
    nt = h // tr

    def body(c_ref, g_ref, o_ref, s_ref):
        s_ref[...] = (g_ref[...].astype(F32) + o_ref[...].astype(F32)).astype(s_ref.dtype)

    return pl.pallas_call(
        body, name="pair_sum",
        grid_spec=pltpu.PrefetchScalarGridSpec(
            num_scalar_prefetch=1, grid=(nq, nt),
            in_specs=[pl.BlockSpec((None, tr, w), lambda k, i, c_ref: (k, c_ref[0] * nt + i, 0)),
                      pl.BlockSpec((None, tr, w), lambda k, i, c_ref: (k, i, 0))],
            out_specs=pl.BlockSpec((None, tr, w), lambda k, i, c_ref: (k, i, 0))),
        out_shape=jax.ShapeDtypeStruct((nq, h, w), g.dtype),
        compiler_params=_cparams(2),
    )(c_idx, g, other)


def _chip_sum(s, others, q_idx):
    _, h, w = s.shape
    tr = _div_tile(h)

    def body(q_ref, s_ref, o_ref, out_ref):
        out_ref[...] = ((s_ref[...].astype(F32) + o_ref[0].astype(F32)) + o_ref[1].astype(F32)) + o_ref[2].astype(F32)

    return pl.pallas_call(
        body, name="chip_sum",
        grid_spec=pltpu.PrefetchScalarGridSpec(
            num_scalar_prefetch=1, grid=(h // tr,),
            in_specs=[pl.BlockSpec((None, tr, w), lambda i, q_ref: (q_ref[0], i, 0)),
                      pl.BlockSpec((3, tr, w), lambda i, q_ref: (0, i, 0))],
            out_specs=pl.BlockSpec((tr, w), lambda i, q_ref: (i, 0))),
        out_shape=jax.ShapeDtypeStruct((h, w), F32),
        compiler_params=_cparams(1),
    )(q_idx, s, others)


def _small_update(q_idx, parts, ws, ms, vs, col_block):
    n = len(parts)
    has_w = [w is not None for w in ws]

    def body(q_ref, *refs):
        pos = 0
        ins = []
        for t in range(n):
            k = 4 if has_w[t] else 1
            ins.append(refs[pos:pos + k])
            pos += k
        outs = refs[pos:]
        opos = 0
        for t in range(n):
            p_ref = ins[t][0]
            g = p_ref[0]
            for s in range(1, p_ref.shape[0]):
                g = g + p_ref[s]
            if has_w[t]:
                _, w_ref, m_ref, v_ref = ins[t]
                g_ref, d_ref, nm_ref, nv_ref = outs[opos:opos + 4]
                opos += 4
                g_ref[...] = g
                d_ref[...], nm_ref[...], nv_ref[...] = _adamw_update(w_ref[...], g, m_ref[...], v_ref[...])
            else:
                outs[opos][...] = g
                opos += 1

    def whole(shape):
        nd = len(shape)
        return pl.BlockSpec(shape, lambda i, q_ref: (0,) * nd)

    in_specs, out_specs, out_shape, args = [], [], [], []
    for t in range(n):
        k, r, wf = parts[t].shape
        if col_block[t]:
            w = wf // N_CHIPS
            in_specs.append(pl.BlockSpec((k, r, w), lambda i, q_ref: (0, 0, q_ref[0])))
        else:
            w = wf
            in_specs.append(whole((k, r, wf)))
        args.append(parts[t])
        if has_w[t]:
            assert ws[t].shape == (r, w), (ws[t].shape, r, w)
            in_specs += [whole((r, w))] * 3
            args += [ws[t], ms[t], vs[t]]
            out_specs += [whole((r, w))] * 4
            out_shape += [jax.ShapeDtypeStruct((r, w), F32)] * 4
        else:
            out_specs.append(whole((r, w)))
            out_shape.append(jax.ShapeDtypeStruct((r, w), F32))
    return pl.pallas_call(
        body, name="small_update",
        grid_spec=pltpu.PrefetchScalarGridSpec(num_scalar_prefetch=1, grid=(1,), in_specs=in_specs,
                                               out_specs=out_specs),
        out_shape=out_shape, compiler_params=_cparams(1),
    )(q_idx, *args)


_PACK_SEGMENTS = ((0, 1552), None, (1552, 2064), (2064, 2128), (2064, 2128), (2128, 2192), (2128, 2192),
                  (2192, 2256), (2192, 2256), (2256, 2320), (2256, 2320))
_UNPACK_SEGMENTS = (((0, 1552), (0,)), ((1552, 2064), (P_SQ,)), ((2064, 2128), (P_SK, P_SK + 64)),
                    ((2128, 2192), (P_SK + 128, P_SK + 192)), ((2192, 2256), (P_SV, P_SV + 64)),
                    ((2256, 2320), (P_SV + 128, P_SV + 192)))


def _pack_win(w4):
    per = w4.shape[2]
    pieces = []
    for seg in _PACK_SEGMENTS:
        if seg is None:
            pieces.append(jnp.zeros((w4.shape[1], 128 - GLA_RANK), w4.dtype))
            continue
        for q in range(w4.shape[0]):
            lo, hi = max(seg[0], q * per), min(seg[1], (q + 1) * per)
            if lo < hi:
                pieces.append(w4[q][:, lo - q * per:hi - q * per])
    return jnp.concatenate(pieces, axis=1)


def _unpack_dwin(d):
    per = D_IN // N_CHIPS
    chips = []
    for q in range(N_CHIPS):
        pieces = []
        for (a, b), starts in _UNPACK_SEGMENTS:
            lo, hi = max(a, q * per), min(b, (q + 1) * per)
            if lo < hi:
                copies = [d[:, s + lo - a:s + hi - a] for s in starts]
                pieces.append(copies[0] if len(copies) == 1 else copies[0] + copies[1])
        chips.append(jnp.concatenate(pieces, axis=1))
    return jnp.stack(chips).astype(BF16)


def _local_step(x, target, meta, p):
    s = x.shape[0]
    t = s + BLK
    h0 = jnp.concatenate([jnp.zeros((PAD, D_MODEL), F32), meta, x], axis=0)
    cos, sin = _rope_tables(t)

    h1, n1, g1, u1, a1, f1 = _ffn_fwd(h0, p["ffn1_pre_norm"], p["ffn1_w"], p["ffn1_post_norm"])
    n2, gq, gk, gv, gg, ga, la, sq, sk, sv = _mix_proj(h1, p["mix_pre_norm"], p["w_in"], p["gla_w_a2"], p["gla_b_a"],
                                                       cos, sin)
    ogla, ss = _gla_fwd(gq, gk, gv, la)
    oswa = _swa_fwd(p["swa_sinks"], sq, sk, sv)
    h2, cat, m = _mix_out(h1, ogla, gg, oswa, p["gla_out_norm"], p["swa_out_norm"], p["w_out"], p["mix_post_norm"])
    grads = {}
    dy, n3, g3, u3, a3, df3, grads["ffn2_post_norm"], sse = _ffn_fwd(
        h2, p["ffn2_pre_norm"], p["ffn2_w"], p["ffn2_post_norm"], target=target)

    dh2, dg3, du3, grads["ffn2_pre_norm"] = _ffn_bwd(
        dy, h2, None, g3, u3, p["ffn2_pre_norm"], p["ffn2_w"], p["ffn2_post_norm"], df=df3)
    (gud,) = _ffn_wgrad(n3, df3, dg3, du3, a3)
    grads["ffn2_w_gate"], grads["ffn2_w_up"], grads["ffn2_w_down"] = gud[:, :FJ], gud[:, FJ:2 * FJ], gud[:, 2 * FJ:]

    dogla, dgg, doswa, dm, grads["mix_post_norm"], grads["gla_out_norm"], grads["swa_out_norm"] = _mix_out_bwd(
        dh2, m, ogla, gg, oswa, p["gla_out_norm"], p["swa_out_norm"], p["w_out"], p["mix_post_norm"])
    grads["w_out"] = _xty(cat, dm)
    dsq, dsk, dsv, dkm, dvm, dsinks = _swa_bwd(p["swa_sinks"], sq, sk, sv, oswa, doswa)
    grads["swa_sinks"] = dsinks[:, 0]
    dgq, dgk, dgv, dla = _gla_bwd(gq, gk, gv, la, ss, dogla)
    dh1, dproj, grads["mix_pre_norm"], dwa2p, grads["gla_b_a"] = _mix_in_bwd(
        dh2, h1, p["mix_pre_norm"], p["w_in"], p["gla_w_a2"], p["gla_b_a"], cos, sin, ga, dgq, dgk, dgv, dgg, dla,
        dsq, dsk, dsv, dkm, dvm)
    grads["gla_w_a2"] = dwa2p[:GLA_RANK]
    grads["w_in"] = _unpack_dwin(_xty(n2, dproj))

    dh0, df1, dg1, du1, grads["ffn1_pre_norm"], grads["ffn1_post_norm"] = _ffn_bwd(
        dh1, h0, f1, g1, u1, p["ffn1_pre_norm"], p["ffn1_w"], p["ffn1_post_norm"])
    (gud,) = _ffn_wgrad(n1, df1, dg1, du1, a1)
    grads["ffn1_w_gate"], grads["ffn1_w_up"], grads["ffn1_w_down"] = gud[:, :FJ], gud[:, FJ:2 * FJ], gud[:, 2 * FJ:]
    grads["meta_tokens"] = dh0[PAD:BLK]
    return sse[0, 0], dh0[BLK:], grads


WEIGHTS = ['meta_tokens', 'ffn1_pre_norm', 'ffn1_w_gate', 'ffn1_w_up', 'ffn1_w_down', 'ffn1_post_norm',
           'mix_pre_norm', 'w_in', 'gla_w_a2', 'gla_b_a', 'gla_out_norm', 'swa_sinks', 'swa_out_norm', 'w_out',
           'mix_post_norm', 'ffn2_pre_norm', 'ffn2_w_gate', 'ffn2_w_up', 'ffn2_w_down', 'ffn2_post_norm']
BIG = ['ffn1_w_gate', 'ffn1_w_up', 'ffn1_w_down', 'w_in', 'w_out', 'ffn2_w_gate', 'ffn2_w_up', 'ffn2_w_down']
SMALL = [n for n in WEIGHTS if n not in BIG]
FJ = D_FF // N_CHIPS
D_IN_J = D_IN // N_CHIPS
D_OUT_J = D_MODEL // N_CHIPS
TRANSPOSED = ('ffn1_w_gate', 'ffn1_w_up', 'ffn2_w_gate', 'ffn2_w_up')


def _shard2d(name, a):
    return a[0].T if name in TRANSPOSED else a[0]


def _unshard2d(name, a):
    return (a.T if name in TRANSPOSED else a)[None]


def kernel(x, meta_tokens, ffn1_pre_norm, ffn1_w_gate, ffn1_w_up, ffn1_w_down, ffn1_post_norm, mix_pre_norm, w_in, gla_w_a2, gla_b_a, gla_out_norm, swa_sinks, swa_out_norm, w_out, mix_post_norm, ffn2_pre_norm, ffn2_w_gate, ffn2_w_up, ffn2_w_down, ffn2_post_norm, loss_target, m_meta_tokens, m_ffn1_pre_norm, m_ffn1_w_gate, m_ffn1_w_up, m_ffn1_w_down, m_ffn1_post_norm, m_mix_pre_norm, m_w_in, m_gla_w_a2, m_gla_b_a, m_gla_out_norm, m_swa_sinks, m_swa_out_norm, m_w_out, m_mix_post_norm, m_ffn2_pre_norm, m_ffn2_w_gate, m_ffn2_w_up, m_ffn2_w_down, m_ffn2_post_norm, v_meta_tokens, v_ffn1_pre_norm, v_ffn1_w_gate, v_ffn1_w_up, v_ffn1_w_down, v_ffn1_post_norm, v_mix_pre_norm, v_w_in, v_gla_w_a2, v_gla_b_a, v_gla_out_norm, v_swa_sinks, v_swa_out_norm, v_w_out, v_mix_post_norm, v_ffn2_pre_norm, v_ffn2_w_gate, v_ffn2_w_up, v_ffn2_w_down, v_ffn2_post_norm):
    args = dict(locals())
    w = {n: args[n] for n in WEIGHTS}
    mom = {n: args["m_" + n] for n in WEIGHTS}
    var = {n: args["v_" + n] for n in WEIGHTS}
    cx, cy, cc = lax.axis_index("x"), lax.axis_index("y"), lax.axis_index("c")
    q_idx = (2 * cx + cy).astype(jnp.int32).reshape(1)
    c_idx = cc.astype(jnp.int32).reshape(1)

    q_chip = 2 * cx + cy
    bf = {n: _own_slot(_shard2d(n, w[n]).astype(BF16), q_chip) for n in ("w_in", "w_out")}
    for ffn in ("ffn1", "ffn2"):
        bf[ffn] = _stack_own_slot([_shard2d(ffn + s, w[ffn + s]) for s in ("_w_gate", "_w_up", "_w_down")], q_idx)
    qc_idx = jnp.stack([q_chip, cc]).astype(jnp.int32)
    sinks = w["swa_sinks"].reshape(SWA_QH)

    seq, target = x[0], loss_target[0]
    t = seq.shape[0] + BLK
    h0, n1 = _embed_norm(seq, _own_slot(w["meta_tokens"], q_chip), w["ffn1_pre_norm"])
    cos, sin = _rope_tables(t)
    late = _GatherChips([bf["w_in"], bf["w_out"], bf["ffn2"],
                         _own_slot(w["gla_w_a2"].reshape(GLA_RANK, GLA_KW // N_CHIPS), q_chip)])
    (h1, g1, u1, a1, f1), (w31,), (win4, wout4, w32, wa24) = _ffn_fwd_gather(
        h0, n1, bf["ffn1"], w["ffn1_post_norm"], qc_idx, late)
    wa2p = jnp.pad(wa24.transpose(1, 0, 2).reshape(GLA_RANK, GLA_KW), ((0, 128 - GLA_RANK), (0, 0))).astype(BF16)
    winp = _pack_win(win4)
    wout = wout4.reshape(D_MODEL, D_MODEL)
    n2, gq, gk, gv, gg, ga, la, sq, sk, sv = _mix_proj(h1, w["mix_pre_norm"], winp, wa2p, w["gla_b_a"], cos, sin)
    ogla, ss = _gla_fwd(gq, gk, gv, la)
    oswa = _swa_fwd(sinks, sq, sk, sv)
    h2, cat, m = _mix_out(h1, ogla, gg, oswa, w["gla_out_norm"], w["swa_out_norm"], wout, w["mix_post_norm"])
    g = {}
    dy, n3, g3, u3, a3, df3, g["ffn2_post_norm"], sse = _ffn_fwd(
        h2, w["ffn2_pre_norm"], w32, w["ffn2_post_norm"], target=target)

    dh2, dg3, du3, g["ffn2_pre_norm"] = _ffn_bwd(
        dy, h2, None, g3, u3, w["ffn2_pre_norm"], w32, w["ffn2_post_norm"], df=df3)
    (gf2,) = _ffn_wgrad(n3, df3, dg3, du3, a3)
    (dogla, dgg, doswa, dm, g["mix_post_norm"], g["gla_out_norm"], g["swa_out_norm"]), (rgf2,) = _mix_out_bwd(
        dh2, m, ogla, gg, oswa, w["gla_out_norm"], w["swa_out_norm"], wout, w["mix_post_norm"],
        hook=_PairExchange([gf2]))
    sgf2 = _pair_sum(gf2, rgf2, c_idx)
    gout = _xty(cat, dm).reshape(N_CHIPS, D_OUT_J, D_MODEL).astype(BF16)
    (dsq, dsk, dsv, dkm, dvm, dsinks), (ogf2,) = _swa_bwd(sinks, sq, sk, sv, oswa, doswa,
                                                          hook=_ChipScatter([sgf2]))
    g["swa_sinks"] = dsinks
    dgq, dgk, dgv, dla = _gla_bwd(gq, gk, gv, la, ss, dogla)
    dh1, dproj, g["mix_pre_norm"], dwa2p, g["gla_b_a"] = _mix_in_bwd(
        dh2, h1, w["mix_pre_norm"], winp, wa2p, w["gla_b_a"], cos, sin, ga, dgq, dgk, dgv, dgg, dla,
        dsq, dsk, dsv, dkm, dvm)
    g["gla_w_a2"] = dwa2p[:GLA_RANK]
    gin = _unpack_dwin(_xty(n2, dproj))
    (dh0, df1, dg1, du1, g["ffn1_pre_norm"], g["ffn1_post_norm"]), (rgin, rgout) = _ffn_bwd(
        dh1, h0, f1, g1, u1, w["ffn1_pre_norm"], w31, w["ffn1_post_norm"],
        hook=_PairExchange([gin, gout]))
    sgin, sgout = _pair_sum(gin, rgin, c_idx), _pair_sum(gout, rgout, c_idx)
    g["meta_tokens"] = dh0[PAD:BLK]
    grad_x = dh0[BLK:]
    late_small = ["gla_w_a2", "swa_sinks"]
    direct = [n for n in SMALL if n not in late_small]
    names = direct + late_small
    half_f2 = _chip_sum(sgf2, ogf2, q_idx)
    hooks = _Hooks([_ChipScatter([sgin, sgout]), _GatherDevices([g[n] for n in names] + [sse]),
                    _PairShare([half_f2])])
    own1, others1, houts = _ffn_wgrad_reduce(n1, df1, dg1, du1, a1, qc_idx, hooks)
    (ogin, ogout), gathered, (other_f2,) = hooks.split(houts)
    halves = [_chip_sum(own1[None], others1, jnp.zeros((1,), jnp.int32))]
    halves += [_chip_sum(s, o, q_idx) for s, o in ((sgin, ogin), (sgout, ogout))]
    others = list(_comm_call(_PairShare(halves), "pair_share")) + [other_f2]
    halves.append(half_f2)
    reduced = {"ffn1_w_gate": (0, 0), "ffn1_w_up": (0, FJ), "ffn1_w_down": (0, 2 * FJ), "w_in": (1, 0),
               "w_out": (2, 0), "ffn2_w_gate": (3, 0), "ffn2_w_up": (3, FJ), "ffn2_w_down": (3, 2 * FJ)}
    grad, delta, new_m, new_v = {}, {}, {}, {}
    for n in BIG:
        k, row0 = reduced[n]
        outs = _adamw_halves(_shard2d(n, w[n]), halves[k], others[k], _shard2d(n, mom[n]), _shard2d(n, var[n]),
                             c_idx, row0)
        grad[n], delta[n], new_m[n], new_v[n] = [_unshard2d(n, a) for a in outs]

    late = late_small
    mat = lambda a: a.reshape(a.shape[-2:])
    none3 = [None] * (len(late) + 1)
    outs = _small_update(q_idx, gathered, [mat(w[n]) for n in direct] + none3, [mat(mom[n]) for n in direct] + none3,
                         [mat(var[n]) for n in direct] + none3, [n == "meta_tokens" for n in names] + [False])
    sum_a2, sum_sinks, sum_sse = outs[4 * len(direct):]
    loss = sum_sse[0, 0] * (0.5 / D_MODEL)
    g_late = [lax.dynamic_slice_in_dim(sum_a2, q_chip * (GLA_KW // N_CHIPS), GLA_KW // N_CHIPS, axis=1)[None],
              sum_sinks[:, 0].reshape(1, 1, SWA_QH)]
    outs = list(outs[:4 * len(direct)]) + list(_small_update(
        q_idx, g_late, [mat(w[n]) for n in late], [mat(mom[n]) for n in late], [mat(var[n]) for n in late],
        [False, False]))
    for k, n in enumerate(names):
        grad[n], delta[n], new_m[n], new_v[n] = [a.reshape(w[n].shape) for a in outs[4 * k:4 * k + 4]]

    return (loss, grad_x[None], *[grad[n] for n in WEIGHTS], *[delta[n] for n in WEIGHTS],
            *[new_m[n] for n in WEIGHTS], *[new_v[n] for n in WEIGHTS])
```

```python
import functools
import math

import numpy as np
import jax
import jax.numpy as jnp
from jax import lax
from jax.experimental import pallas as pl
from jax.experimental.pallas import tpu as pltpu

F32 = jnp.float32
BF16 = jnp.bfloat16
MESH = pl.DeviceIdType.MESH

D_MODEL = 1024
D_FF = 2816
N_CHIPS = 4
N_DEV = 8
N_META = 16
BLK = 128
PAD = BLK - N_META
GLA_CHUNK = 64
GLA_HEADS = 4
GLA_DV = 128
GLA_DK = 64
GLA_KW = GLA_HEADS * GLA_DK
GLA_W = GLA_HEADS * GLA_DV
GLA_RANK = 16
GLA_TAU = 16.0
SWA_HD = 64
SWA_QH = 8
SWA_KVH = 2
SWA_W = SWA_QH * SWA_HD
WINDOW = 128
ROPE_THETA = 10000.0
EPS = 1e-6
NEG_INF = -1e30
IN_SPLITS = (256, 256, 512, 512, 16, 512, 128, 128)
D_IN = sum(IN_SPLITS)
P_GQ, P_GK, P_GV, P_GG, P_GA, P_SQ, P_SK, P_SV, P_END = 0, 256, 512, 1024, 1536, 1664, 2176, 2432, 2688
ADAM_LR, ADAM_B1, ADAM_B2, ADAM_EPS, ADAM_WD, ADAM_STEP = 0.001, 0.9, 0.999, 1e-08, 0.01, 10
VMEM_LIMIT = 56 * 1024 * 1024

NT = (((1,), (1,)), ((), ()))
TN = (((0,), (0,)), ((), ()))


def _cparams(n_axes):
    return pltpu.CompilerParams(dimension_semantics=("arbitrary",) * n_axes, vmem_limit_bytes=VMEM_LIMIT)


def _row_tile(t):
    for tm in (640, 512, 384, 256, 128):
        if t % tm == 0:
            return tm
    raise ValueError(t)


SEQ_BLOCKS_PER_STEP = 5


def _seq_tile(t):
    return SEQ_BLOCKS_PER_STEP * BLK if t % (SEQ_BLOCKS_PER_STEP * BLK) == 0 else BLK


ROW_PARTS = 2


def _row_parts(tm):
    n = ROW_PARTS if tm % (16 * ROW_PARTS) == 0 else 1
    return [slice(k * (tm // n), (k + 1) * (tm // n)) for k in range(n)]


def _contract_tile(t):
    return 1664 if t % 1664 == 0 else _row_tile(t)


def _div_tile(r, cap=512):
    best = None
    for tr in range(8, min(r, cap) + 1, 8):
        if r % tr == 0:
            best = tr
    return best if best is not None else r


def _dot(a, b):
    return jnp.dot(a, b, preferred_element_type=F32)


def _dg(a, b, dims):
    return lax.dot_general(a, b, dims, preferred_element_type=F32)


def _rms(x, w):
    r = lax.rsqrt(jnp.mean(x * x, axis=-1, keepdims=True) + EPS)
    xh = x * r
    return xh * w, xh, r


def _rms_bwd(xh, r, w, dy):
    wdy = dy * w
    dx = r * (wdy - xh * jnp.mean(wdy * xh, axis=-1, keepdims=True))
    dw = jnp.sum(dy * xh, axis=0, keepdims=True)
    return dx, dw


def _sigmoid(x):
    return 1.0 / (1.0 + jnp.exp(-x))


def _full(shape):
    nd = len(shape)
    return pl.BlockSpec(shape, lambda *_: (0,) * nd)


ANY = pl.BlockSpec(memory_space=pl.ANY)


def _pallas(body, *, name, grid, in_specs, out_specs, out_shape, args, scratch_shapes=(), hook=None):
    n_axes = len(grid)
    if hook is None:
        return pl.pallas_call(body, name=name, grid=grid, in_specs=list(in_specs), out_specs=list(out_specs),
                              out_shape=list(out_shape), scratch_shapes=list(scratch_shapes),
                              compiler_params=_cparams(n_axes))(*args)
    n_in, n_out, n_scr = len(in_specs), len(out_specs), len(scratch_shapes)
    h_in, h_out = len(hook.inputs), len(hook.out_shape)
    total = math.prod(grid)

    def wrapped(*refs):
        ins, hins = refs[:n_in], refs[n_in:n_in + h_in]
        o0 = n_in + h_in
        outs, houts = refs[o0:o0 + n_out], refs[o0 + n_out:o0 + n_out + h_out]
        s0 = o0 + n_out + h_out
        scr, hscr = refs[s0:s0 + n_scr], refs[s0 + n_scr:]
        step = pl.program_id(0)
        for a in range(1, n_axes):
            step = step * grid[a] + pl.program_id(a)

        @pl.when(step == 0)
        def _():
            hook.start(hins, houts, hscr)

        body(*ins, *outs, *scr)

        if hook.has_mid:
            @pl.when(step == (3 * total) // 4)
            def _():
                hook.mid(hins, houts, hscr)

        @pl.when(step == total - 1)
        def _():
            hook.finish(hins, houts, hscr)

    res = pl.pallas_call(
        wrapped, name=name, grid=grid, in_specs=list(in_specs) + [ANY] * h_in,
        out_specs=list(out_specs) + [ANY] * h_out, out_shape=list(out_shape) + list(hook.out_shape),
        scratch_shapes=list(scratch_shapes) + list(hook.scratch), compiler_params=_cparams(n_axes),
        input_output_aliases={n_in + a: n_out + b for a, b in hook.aliases},
    )(*args, *hook.inputs)
    return res[:n_out], res[n_out:]


def _ffn_weight_specs(w3):
    fj = w3.shape[1] // 3
    return fj, [pl.BlockSpec((None, fj, D_MODEL), functools.partial(lambda i, j, k: (j, k, 0), k=k)) for k in range(3)]


def _ffn_fwd(h, wpre, w3, wpost, hook=None, target=None):
    t = h.shape[0]
    tm = _row_tile(t)
    nj, rows3, _ = w3.shape
    fj = rows3 // 3
    nblk = tm // BLK if target is not None else 0

    def body(*refs):
        h_ref, wpre_ref, w_hbm, wpost_ref = refs[:4]
        t_refs = refs[4:4 + nblk]
        hout_ref, n_ref, p1_ref, p2_ref, a_ref, f_ref = refs[4 + nblk:10 + nblk]
        acc_ref, wv, wsem = refs[-3:]
        i = pl.program_id(0)
        j = pl.program_id(1)

        @pl.when((i == 0) & (j == 0))
        def _():
            for k in range(nj):
                pltpu.make_async_copy(w_hbm.at[k], wv.at[k], wsem.at[k]).start()

        @pl.when(i == 0)
        def _():
            pltpu.make_async_copy(w_hbm.at[j], wv.at[j], wsem.at[j]).wait()

        @pl.when(j == 0)
        def _():
            y, _, _ = _rms(h_ref[...], wpre_ref[...])
            n_ref[...] = y.astype(BF16)
            acc_ref[...] = jnp.zeros_like(acc_ref)

        if target is not None:
            dwpost_ref, sse_ref = refs[10 + nblk:12 + nblk]

            @pl.when((i == 0) & (j == 0))
            def _():
                dwpost_ref[...] = jnp.zeros_like(dwpost_ref)
                sse_ref[...] = jnp.zeros_like(sse_ref)

        n = n_ref[...]
        g = _dg(n, wv[j, 0:fj], NT)
        u = _dg(n, wv[j, fj:2 * fj], NT)
        sg = _sigmoid(g)
        silu = g * sg
        p1_ref[...] = (u * (sg + silu * (1.0 - sg))).astype(BF16)
        p2_ref[...] = silu.astype(BF16)
        a = (silu * u).astype(BF16)
        a_ref[...] = a
        acc_ref[...] += _dot(a, wv[j, 2 * fj:3 * fj])

        @pl.when(j == nj - 1)
        def _():
            f = acc_ref[...]
            wpost = wpost_ref[...]
            y, fh, r = _rms(f, wpost)
            hout = h_ref[...] + 0.5 * y
            if target is None:
                f_ref[...] = f
                hout_ref[...] = hout
            else:
                sse = jnp.zeros((1, 1), F32)
                errs = []
                for k in range(nblk):
                    err = hout[k * BLK:(k + 1) * BLK] - t_refs[k][...]
                    if k == 0:
                        err = jnp.where(i > 0, err, 0.0)
                    errs.append(err)
                    sse = sse + jnp.sum(jnp.sum(err * err, axis=1, keepdims=True), axis=0, keepdims=True)
                dy = (jnp.concatenate(errs, axis=0) if nblk > 1 else errs[0]) * (1.0 / D_MODEL)
                hout_ref[...] = dy
                df, dw = _rms_bwd(fh, r, wpost, 0.5 * dy)
                f_ref[...] = df.astype(BF16)
                dwpost_ref[...] += dw
                sse_ref[...] += jnp.broadcast_to(sse, sse_ref.shape)

    row = pl.BlockSpec((tm, D_MODEL), lambda i, j: (i, 0))
    vec = pl.BlockSpec((1, D_MODEL), lambda i, j: (0, 0))
    act = pl.BlockSpec((None, tm, fj), lambda i, j: (j, i, 0))
    t_specs = [pl.BlockSpec((BLK, D_MODEL), functools.partial(lambda i, j, k: (jnp.maximum(nblk * i + k - 1, 0), 0), k=k))
               for k in range(nblk)]
    loss_spec = [vec, _full((1, 128))] if target is not None else []
    loss_shape = [jax.ShapeDtypeStruct((1, D_MODEL), F32), jax.ShapeDtypeStruct((1, 128), F32)] if (
        target is not None) else []
    return _pallas(
        body, name="ffn_fwd", grid=(t // tm, nj),
        in_specs=[row, vec, ANY, vec] + t_specs,
        out_specs=[row, row, act, act, act, row] + loss_spec,
        out_shape=[jax.ShapeDtypeStruct((t, D_MODEL), F32), jax.ShapeDtypeStruct((t, D_MODEL), BF16),
                   jax.ShapeDtypeStruct((nj, t, fj), BF16), jax.ShapeDtypeStruct((nj, t, fj), BF16),
                   jax.ShapeDtypeStruct((nj, t, fj), BF16),
                   jax.ShapeDtypeStruct((t, D_MODEL), F32 if target is None else BF16)] + loss_shape,
        scratch_shapes=[pltpu.VMEM((tm, D_MODEL), F32), pltpu.VMEM((nj, rows3, D_MODEL), BF16),
                        pltpu.SemaphoreType.DMA((nj,))],
        args=(h, wpre, w3, wpost) + (target,) * nblk, hook=hook)


def _ffn_bwd(dhout, h, f, p14, p24, wpre, w3, wpost, hook=None, df=None, split_first_block=False):
    t = h.shape[0]
    tm = _row_tile(t)
    ni = t // tm
    nj = w3.shape[0]
    fj, wspecs = _ffn_weight_specs(w3)
    have_df = df is not None
    assert not (have_df and split_first_block)

    def body(dhout_ref, h_ref, f_ref, p1_ref, p2_ref, wpre_ref, wg_ref, wu_ref, wd_ref, wpost_ref, *rest):
        if have_df:
            dh_ref, dg_ref, du_ref, dwpre_ref, dn_ref = rest
            df_ref = f_ref
        elif split_first_block:
            dh_ref, rest_hbm, df_ref, dg_ref, du_ref, dwpre_ref, dwpost_ref, dn_ref, dh_buf, dh_sem, first_sem = rest
        else:
            dh_ref, df_ref, dg_ref, du_ref, dwpre_ref, dwpost_ref, dn_ref = rest
        i = pl.program_id(0)
        j = pl.program_id(1)

        @pl.when((i == 0) & (j == 0))
        def _():
            dwpre_ref[...] = jnp.zeros_like(dwpre_ref)
            if not have_df:
                dwpost_ref[...] = jnp.zeros_like(dwpost_ref)

        @pl.when(j == 0)
        def _():
            if not have_df:
                wpost = wpost_ref[...]
                _, fh, r = _rms(f_ref[...], wpost)
                dfv, dw = _rms_bwd(fh, r, wpost, 0.5 * dhout_ref[...])
                dwpost_ref[...] += dw
                df_ref[...] = dfv.astype(BF16)
            dn_ref[...] = jnp.zeros_like(dn_ref)

        parts = _row_parts(tm)
        das = [_dg(df_ref[rows, :], wd_ref[...], NT) for rows in parts]
        for rows, da in zip(parts, das):
            dg = (da * p1_ref[rows, :].astype(F32)).astype(BF16)
            du = (da * p2_ref[rows, :].astype(F32)).astype(BF16)
            dg_ref[rows, :] = dg
            du_ref[rows, :] = du
            dn_ref[rows, :] += _dot(dg, wg_ref[...]) + _dot(du, wu_ref[...])

        @pl.when(j == nj - 1)
        def _():
            wpre = wpre_ref[...]
            _, hh, r = _rms(h_ref[...], wpre)
            dx, dw = _rms_bwd(hh, r, wpre, dn_ref[...])
            dwpre_ref[...] += dw
            dh = dhout_ref[...] + dx
            if not split_first_block:
                dh_ref[...] = dh
            else:
                slot = i % 2

                def to_rest(tile, sl):
                    rows = pl.ds(pl.multiple_of(tile * tm - BLK, 8), tm)
                    return pltpu.make_async_copy(dh_buf.at[sl], rest_hbm.at[rows], dh_sem.at[sl])

                first = pltpu.make_async_copy(dh_buf.at[0, BLK:tm], rest_hbm.at[0:tm - BLK], first_sem)

                @pl.when(i == 2)
                def _():
                    first.wait()

                @pl.when(i >= 3)
                def _():
                    to_rest(i, slot).wait()

                dh_buf[slot] = dh

                @pl.when(i == 0)
                def _():
                    dh_ref[...] = dh[0:BLK]
                    first.start()

                @pl.when(i > 0)
                def _():
                    to_rest(i, slot).start()

                @pl.when(i == ni - 1)
                def _():
                    if ni < 3:
                        first.wait()
                    if ni >= 3:
                        to_rest(i, 1 - slot).wait()
                    if ni >= 2:
                        to_rest(i, slot).wait()

    row = pl.BlockSpec((tm, D_MODEL), lambda i, j: (i, 0))
    vec = pl.BlockSpec((1, D_MODEL), lambda i, j: (0, 0))
    act = pl.BlockSpec((None, tm, fj), lambda i, j: (j, i, 0))
    actshape = jax.ShapeDtypeStruct((nj, t, fj), BF16)
    rowf, rowb, vecf = (jax.ShapeDtypeStruct((t, D_MODEL), F32), jax.ShapeDtypeStruct((t, D_MODEL), BF16),
                        jax.ShapeDtypeStruct((1, D_MODEL), F32))
    if have_df:
        out_specs, out_shape = [row, act, act, vec], [rowf, actshape, actshape, vecf]
    else:
        out_specs, out_shape = [row, row, act, act, vec, vec], [rowf, rowb, actshape, actshape, vecf, vecf]
    scratch = [pltpu.VMEM((tm, D_MODEL), F32)]
    if split_first_block:
        out_specs = [pl.BlockSpec((BLK, D_MODEL), lambda i, j: (0, 0)), ANY] + out_specs[1:]
        out_shape = [jax.ShapeDtypeStruct((BLK, D_MODEL), F32), jax.ShapeDtypeStruct((t - BLK, D_MODEL), F32)
                     ] + out_shape[1:]
        scratch += [pltpu.VMEM((2, tm, D_MODEL), F32), pltpu.SemaphoreType.DMA((2,)), pltpu.SemaphoreType.DMA]
    return _pallas(
        body, name="ffn_bwd", grid=(ni, nj),
        in_specs=[row, row, row, act, act, vec] + wspecs + [vec],
        out_specs=out_specs, out_shape=out_shape, scratch_shapes=scratch,
        args=(dhout, h, df if have_df else f, p14, p24, wpre, w3, w3, w3, wpost), hook=hook)


def _ffn_wgrad(n, df, dg4, du4, a4, hook=None):
    t = n.shape[0]
    tm = _contract_tile(t)
    ni = t // tm
    nj, _, fj = dg4.shape

    def body(n_ref, df_ref, dg_ref, du_ref, a_ref, dw_ref, acc):
        i = pl.program_id(1)

        @pl.when(i == 0)
        def _():
            acc[...] = jnp.zeros_like(acc)

        nn = n_ref[...]
        acc[0:fj, :] += _dg(dg_ref[...], nn, TN)
        acc[fj:2 * fj, :] += _dg(du_ref[...], nn, TN)
        acc[2 * fj:3 * fj, :] += _dg(a_ref[...], df_ref[...], TN)

        @pl.when(i == ni - 1)
        def _():
            dw_ref[...] = acc[...].astype(BF16)

    row = pl.BlockSpec((tm, D_MODEL), lambda j, i: (i, 0))
    act = pl.BlockSpec((None, tm, fj), lambda j, i: (j, i, 0))
    return _pallas(
        body, name="ffn_wgrad", grid=(nj, ni),
        in_specs=[row, row, act, act, act],
        out_specs=[pl.BlockSpec((None, 3 * fj, D_MODEL), lambda j, i: (j, 0, 0))],
        out_shape=[jax.ShapeDtypeStruct((nj, 3 * fj, D_MODEL), BF16)],
        scratch_shapes=[pltpu.VMEM((3 * fj, D_MODEL), F32)],
        args=(n, df, dg4, du4, a4), hook=hook)


def _embed_norm(x, meta_buf, w):
    t = x.shape[0] + BLK
    tm = _row_tile(t)
    nblk = tm // BLK
    ni = t // tm
    gather = _GatherChips([meta_buf])

    def body(*refs):
        x_refs = refs[:nblk]
        w_ref, mb_in, h_ref, n_ref, mb_out, mv, msem, send, recv = refs[nblk:]
        step = pl.program_id(0)
        tile = (step + 1) % ni
        hook_refs = ([mb_in], [mb_out], [send, recv])
        steps = (0, 1, ni - 2) if ni >= 3 else (0, 0, 0)
        for at, phase in zip(steps, (gather.start, gather.mid, gather.finish)):
            @pl.when(step == at)
            def _(phase=phase):
                phase(*hook_refs)

        @pl.when(step == 0)
        def _():
            mv[...] = jnp.zeros_like(mv)

        @pl.when(step == ni - 1)
        def _():
            cp = pltpu.make_async_copy(mb_out, mv, msem)
            cp.start()
            cp.wait()

        meta = jnp.concatenate([mv[k] for k in range(N_CHIPS)], axis=1)
        first = jnp.concatenate([jnp.zeros((PAD, D_MODEL), F32), meta], axis=0)
        blocks = [jnp.where(tile == 0, first, x_refs[0][...])] + [r[...] for r in x_refs[1:]]
        h = jnp.concatenate(blocks, axis=0) if nblk > 1 else blocks[0]
        h_ref[...] = h
        y, _, _ = _rms(h, w_ref[...])
        n_ref[...] = y.astype(BF16)

    x_specs = [pl.BlockSpec((BLK, D_MODEL), functools.partial(
        lambda i, k: (jnp.maximum(nblk * ((i + 1) % ni) + k - 1, 0), 0), k=k)) for k in range(nblk)]
    row = pl.BlockSpec((tm, D_MODEL), lambda i: ((i + 1) % ni, 0))
    h0, n0, _ = pl.pallas_call(
        body, name="embed_norm", grid=(ni,),
        in_specs=x_specs + [_full((1, D_MODEL)), ANY], out_specs=[row, row, ANY],
        out_shape=[jax.ShapeDtypeStruct((t, D_MODEL), F32), jax.ShapeDtypeStruct((t, D_MODEL), BF16),
                   jax.ShapeDtypeStruct(meta_buf.shape, meta_buf.dtype)],
        scratch_shapes=[pltpu.VMEM(meta_buf.shape, meta_buf.dtype), pltpu.SemaphoreType.DMA] + list(gather.scratch),
        input_output_aliases={nblk + 1: 2},
        compiler_params=_cparams(1),
    )(*([x] * nblk), w, meta_buf)
    return h0, n0


FWD_RELATION = (None, 0, 1, 2)


def _ffn_fwd_gather(h, n, wbuf, wpost, qc_idx, late):
    t = h.shape[0]
    tm = _row_tile(t)
    ni = t // tm
    nj, rows3, _ = wbuf.shape
    fj = rows3 // 3
    assert nj == N_CHIPS and ni >= 4
    wbufs = [wbuf]
    nw = 1
    n_lin, n_lout = len(late.inputs), len(late.out_shape)
    wait_step = ni - 3

    def body(qc_ref, h_ref, n_ref, wpost_ref, *rest):
        wb_in = rest[:nw]
        lins = rest[nw:nw + n_lin]
        o0 = nw + n_lin
        hout_ref, p1_ref, p2_ref, a_ref, f_hbm = rest[o0:o0 + 5]
        wb = rest[o0 + 5:o0 + 5 + nw]
        louts = rest[o0 + 5 + nw:o0 + 5 + nw + n_lout]
        s0 = o0 + 5 + nw + n_lout
        wv, wsem, send, recv, fbuf, fr_sem, fw_sem = rest[s0:s0 + 7]
        lscr = rest[s0 + 7:]
        p = pl.program_id(0)
        i = pl.program_id(1)
        step = p * ni + i
        fslot = step % 3
        nslot = (step + 1) % 3

        def f_tile(tile):
            return f_hbm.at[pl.ds(pl.multiple_of(tile * tm, 8), tm)]

        @pl.when(step > 1)
        def _():
            pltpu.make_async_copy(fbuf.at[nslot], f_tile(i), fw_sem.at[nslot]).wait()

        nxt = step + 1

        @pl.when((nxt < N_CHIPS * ni) & (nxt >= ni))
        def _():
            pltpu.make_async_copy(f_tile(nxt % ni), fbuf.at[nslot], fr_sem.at[nslot]).start()

        @pl.when(p > 0)
        def _():
            pltpu.make_async_copy(f_tile(i), fbuf.at[fslot], fr_sem.at[fslot]).wait()
        x, y, c, chips = _place()
        q = 2 * x + y
        sibling = (x, y, 1 - c)
        mine, other = _half(rows3, c), _half(rows3, 1 - c)

        def load(chunk, slot, src):
            return [pltpu.make_async_copy(src[t].at[chunk], wv.at[slot, t], wsem.at[slot, t]) for t in range(nw)]

        @pl.when((p == 0) & (i == 0))
        def _():
            for j, (cx, cy) in enumerate(chips):
                for t in range(nw):
                    _remote(send.at[t, j], recv.at[t, j], wb_in[t].at[q, mine], wb[t].at[q, mine], (cx, cy, c)).start()
            for cp in load(q, 0, wb_in):
                cp.start()
            for cp in load(q, 0, wb_in):
                cp.wait()

        @pl.when((p == 1) & (i == 0))
        def _():
            late.start(lins, louts, lscr)

        for pp in range(1, N_CHIPS):
            j = FWD_RELATION[pp]
            cx, cy = chips[j]
            chunk = 2 * cx + cy

            @pl.when((p == pp - 1) & (i == wait_step))
            def _(j=j, cx=cx, cy=cy, chunk=chunk, pp=pp):
                for t in range(nw):
                    got = wb[t].at[chunk, mine]
                    _remote(send.at[t, j], recv.at[t, j], got, got, (cx, cy, c)).wait_recv()
                    _remote(send.at[t, 3 + j], recv.at[t, 3 + j], got, got, sibling).start()
                for t in range(nw):
                    rest_half = wb[t].at[chunk, other]
                    _remote(send.at[t, 3 + j], recv.at[t, 3 + j], rest_half, rest_half, sibling).wait_recv()
                for cp in load(chunk, pp % 2, wb):
                    cp.start()

            @pl.when((p == pp) & (i == 0))
            def _(chunk=chunk, pp=pp):
                for cp in load(chunk, pp % 2, wb):
                    cp.wait()

        @pl.when((p == N_CHIPS - 1) & (i == ni // 2))
        def _():
            late.mid(lins, louts, lscr)

        slot = p % 2
        nn = n_ref[...]
        g = _dg(nn, wv[slot, 0, 0:fj], NT)
        u = _dg(nn, wv[slot, 0, fj:2 * fj], NT)
        sg = _sigmoid(g)
        silu = g * sg
        p1_ref[...] = (u * (sg + silu * (1.0 - sg))).astype(BF16)
        p2_ref[...] = silu.astype(BF16)
        a = (silu * u).astype(BF16)
        a_ref[...] = a
        part = _dot(a, wv[slot, 0, 2 * fj:3 * fj])

        @pl.when(p == 0)
        def _():
            fbuf[fslot] = part

        @pl.when(p > 0)
        def _():
            fbuf[fslot] = fbuf[fslot] + part

        pltpu.make_async_copy(fbuf.at[fslot], f_tile(i), fw_sem.at[fslot]).start()

        @pl.when(p == N_CHIPS - 1)
        def _():
            yv, _, _ = _rms(fbuf[fslot], wpost_ref[...])
            hout_ref[...] = h_ref[...] + 0.5 * yv

        @pl.when((p == N_CHIPS - 1) & (i == ni - 1))
        def _():
            pslot = (step + 2) % 3
            pltpu.make_async_copy(fbuf.at[pslot], f_tile(i), fw_sem.at[pslot]).wait()
            pltpu.make_async_copy(fbuf.at[fslot], f_tile(i), fw_sem.at[fslot]).wait()
            for t in range(nw):
                for j, (cx, cy) in enumerate(chips):
                    sent = wb[t].at[2 * cx + cy, mine]
                    _remote(send.at[t, j], recv.at[t, j], sent, sent, (cx, cy, c)).wait_send()
                    _remote(send.at[t, 3 + j], recv.at[t, 3 + j], sent, sent, sibling).wait_send()
            late.finish(lins, louts, lscr)

    def last_pass_rows(p, i, qc_ref):
        return (jnp.where(p == N_CHIPS - 1, i, 0), 0)

    def chunk_rows(p, i, qc_ref):
        order = ((p & 1) << 1) | (p >> 1)
        return (jnp.bitwise_xor(qc_ref[0], order), i, 0)

    row = pl.BlockSpec((tm, D_MODEL), lambda p, i, qc_ref: (i, 0))
    last_row = pl.BlockSpec((tm, D_MODEL), last_pass_rows)
    act = pl.BlockSpec((None, tm, fj), chunk_rows)
    act_shape = jax.ShapeDtypeStruct((nj, t, fj), BF16)
    res = pl.pallas_call(
        body, name="ffn_fwd_gather",
        grid_spec=pltpu.PrefetchScalarGridSpec(
            num_scalar_prefetch=1, grid=(N_CHIPS, ni),
            in_specs=[last_row, row, pl.BlockSpec((1, D_MODEL), lambda p, i, qc_ref: (0, 0))]
            + [ANY] * (nw + n_lin),
            out_specs=[last_row, act, act, act, ANY] + [ANY] * (nw + n_lout),
            scratch_shapes=[pltpu.VMEM((2, nw, rows3, D_MODEL), BF16), pltpu.SemaphoreType.DMA((2, nw)),
                            pltpu.SemaphoreType.DMA((nw, 6)), pltpu.SemaphoreType.DMA((nw, 6)),
                            pltpu.VMEM((3, tm, D_MODEL), F32), pltpu.SemaphoreType.DMA((3,)),
                            pltpu.SemaphoreType.DMA((3,))] + list(late.scratch)),
        out_shape=[jax.ShapeDtypeStruct((t, D_MODEL), F32), act_shape, act_shape, act_shape,
                   jax.ShapeDtypeStruct((t, D_MODEL), F32)]
        + [jax.ShapeDtypeStruct(b.shape, b.dtype) for b in wbufs] + list(late.out_shape),
        input_output_aliases={**{4 + t: 5 + t for t in range(nw)},
                              **{4 + nw + a: 5 + nw + b for a, b in late.aliases}},
        compiler_params=_cparams(2),
    )(qc_idx, h, n, wpost, *wbufs, *late.inputs)
    return res[:5], res[5:5 + nw], res[5 + nw:]


PASS_RELATION = (2, 0, 1)


def _ffn_wgrad_reduce(n, df, dg4, du4, a4, qc_idx, hook):
    t = n.shape[0]
    tm = _contract_tile(t)
    ni = t // tm
    nj, _, fj = dg4.shape
    assert nj == N_CHIPS
    hrows = 3 * fj // 2
    n_hin, n_hout = len(hook.inputs), len(hook.out_shape)

    def body(qc_ref, n_ref, df_ref, dg_ref, du_ref, a_ref, *rest):
        hins = rest[:n_hin]
        own_ref, others_ref = rest[n_hin:n_hin + 2]
        houts = rest[n_hin + 2:n_hin + 2 + n_hout]
        s0 = n_hin + 2 + n_hout
        acc, stage, land, sumbuf, px_send, px_recv, cs_send, cs_recv, own_sem = rest[s0:s0 + 9]
        hscr = rest[s0 + 9:]
        k_pass = pl.program_id(0)
        i = pl.program_id(1)
        x, y, c, chips = _place()
        mine = pl.ds(pl.multiple_of(c * hrows, 8), hrows)
        other = pl.ds(pl.multiple_of((1 - c) * hrows, 8), hrows)

        def to_owner(k):
            j = PASS_RELATION[k]
            return _remote(cs_send.at[j], cs_recv.at[j], sumbuf.at[k % 2], others_ref.at[j], (*chips[j], c))

        @pl.when((k_pass == 0) & (i == 0))
        def _():
            hook.start(hins, houts, hscr)

        if hook.has_mid:
            @pl.when((k_pass == N_CHIPS // 2) & (i == 0))
            def _():
                hook.mid(hins, houts, hscr)

        @pl.when(i == 0)
        def _():
            acc[...] = jnp.zeros_like(acc)

        nn = n_ref[...]
        acc[0:fj, :] += _dg(dg_ref[...], nn, TN)
        acc[fj:2 * fj, :] += _dg(du_ref[...], nn, TN)
        acc[2 * fj:3 * fj, :] += _dg(a_ref[...], df_ref[...], TN)

        for k in range(N_CHIPS):
            @pl.when((k_pass == k) & (i == ni - 1))
            def _(k=k):
                slot = k % 2
                stage[...] = acc[other, :].astype(BF16)
                swap = _remote(px_send.at[k], px_recv.at[k], stage, land.at[slot], (x, y, 1 - c))
                swap.start()
                swap.wait_recv()
                pair = acc[mine, :] + land[slot].astype(F32)
                if k >= 2:
                    to_owner(k - 2).wait_send()
                sumbuf[slot] = pair.astype(BF16)
                swap.wait_send()
                if k < N_CHIPS - 1:
                    to_owner(k).start()
                else:
                    keep = pltpu.make_async_copy(sumbuf.at[slot], own_ref, own_sem)
                    keep.start()
                    for j in range(N_CHIPS - 1):
                        _remote(cs_send.at[j], cs_recv.at[j], sumbuf.at[0], others_ref.at[j], (*chips[j], c)).wait_recv()
                    to_owner(k - 1).wait_send()
                    keep.wait()
                    hook.finish(hins, houts, hscr)

    def chunk(k_pass, i, qc_ref):
        return (jnp.bitwise_xor(qc_ref[0], N_CHIPS - 1 - k_pass), i, 0)

    row = pl.BlockSpec((tm, D_MODEL), lambda k_pass, i, qc_ref: (i, 0))
    act = pl.BlockSpec((None, tm, fj), chunk)
    res = pl.pallas_call(
        body, name="ffn_wgrad_reduce",
        grid_spec=pltpu.PrefetchScalarGridSpec(
            num_scalar_prefetch=1, grid=(N_CHIPS, ni),
            in_specs=[row, row, act, act, act] + [ANY] * n_hin,
            out_specs=[ANY, ANY] + [ANY] * n_hout,
            scratch_shapes=[pltpu.VMEM((3 * fj, D_MODEL), F32), pltpu.VMEM((hrows, D_MODEL), BF16),
                            pltpu.VMEM((2, hrows, D_MODEL), BF16), pltpu.VMEM((2, hrows, D_MODEL), BF16),
                            pltpu.SemaphoreType.DMA((N_CHIPS,)), pltpu.SemaphoreType.DMA((N_CHIPS,)),
                            pltpu.SemaphoreType.DMA((N_CHIPS - 1,)), pltpu.SemaphoreType.DMA((N_CHIPS - 1,)),
                            pltpu.SemaphoreType.DMA] + list(hook.scratch)),
        out_shape=[jax.ShapeDtypeStruct((hrows, D_MODEL), BF16),
                   jax.ShapeDtypeStruct((N_CHIPS - 1, hrows, D_MODEL), BF16)] + list(hook.out_shape),
        compiler_params=_cparams(2),
    )(qc_idx, n, df, dg4, du4, a4, *hook.inputs)
    return res[0], res[1], res[2:]


def _xty(x, y):
    t, k = x.shape
    n = y.shape[1]
    tm = _contract_tile(t)
    tn = n if n <= 1024 else (896 if n % 896 == 0 else 128)

    def body(x_ref, y_ref, o_ref):
        @pl.when(pl.program_id(1) == 0)
        def _():
            o_ref[...] = jnp.zeros_like(o_ref)

        o_ref[...] += _dg(x_ref[...], y_ref[...], TN)

    return pl.pallas_call(
        body, name="xty", grid=(n // tn, t // tm),
        in_specs=[pl.BlockSpec((tm, k), lambda j, i: (i, 0)), pl.BlockSpec((tm, tn), lambda j, i: (i, j))],
        out_specs=pl.BlockSpec((k, tn), lambda j, i: (0, j)),
        out_shape=jax.ShapeDtypeStruct((k, n), F32),
        compiler_params=_cparams(2),
    )(x, y)


def _rope_tables(t):
    pos = (jnp.arange(t, dtype=jnp.int32) - PAD).astype(F32)
    inv_freq = 1.0 / (ROPE_THETA ** (jnp.arange(0, SWA_HD, 2, dtype=F32) / SWA_HD))
    ang = pos[:, None] * inv_freq[None, :]
    cos = jnp.cos(ang)
    sin = jnp.sin(ang)
    return jnp.concatenate([cos, cos, cos, cos], axis=1), jnp.concatenate([-sin, sin, -sin, sin], axis=1)


def _rot_half(x, first_half):
    return jnp.where(first_half, pltpu.roll(x, 96, 1), pltpu.roll(x, 32, 1))


def _first_half_mask(rows):
    lane = lax.broadcasted_iota(jnp.int32, (rows, 128), 1)
    return (lane % 64) < 32


def _log_sigmoid(z):
    return jnp.minimum(z, 0.0) - jnp.log(1.0 + jnp.exp(-jnp.abs(z)))


def _mix_proj(h1, wmixpre, winp, wa2p, bap, cos, sin):
    t = h1.shape[0]
    tm = _row_tile(t)

    def body(h_ref, w_ref, win_ref, wa2_ref, ba_ref, cos_ref, sin_ref,
             n_ref, gq_ref, gk_ref, gv_ref, gg_ref, ga_ref, la_ref, sq_ref, sk_ref, sv_ref):
        y, _, _ = _rms(h_ref[...], w_ref[...])
        n = y.astype(BF16)
        n_ref[...] = n
        proj = _dot(n, win_ref[...])
        gq_ref[...] = proj[:, P_GQ:P_GK]
        gk_ref[...] = proj[:, P_GK:P_GV]
        gv_ref[...] = proj[:, P_GV:P_GG]
        gg_ref[...] = proj[:, P_GG:P_GA]
        ga = proj[:, P_GA:P_SQ]
        ga_ref[...] = ga
        z = _dot(ga.astype(BF16), wa2_ref[...]) + ba_ref[...]
        la_ref[...] = _log_sigmoid(z) * (1.0 / GLA_TAU)
        c = cos_ref[...]
        s = sin_ref[...]
        fh = _first_half_mask(tm)
        for k in range(4):
            x = proj[:, P_SQ + 128 * k:P_SQ + 128 * (k + 1)]
            sq_ref[:, 128 * k:128 * (k + 1)] = (x * c + _rot_half(x, fh) * s).astype(BF16)
        for k in range(2):
            x = proj[:, P_SK + 128 * k:P_SK + 128 * (k + 1)]
            sk_ref[:, 128 * k:128 * (k + 1)] = (x * c + _rot_half(x, fh) * s).astype(BF16)
        sv_ref[...] = proj[:, P_SV:P_END].astype(BF16)

    def row(w):
        return pl.BlockSpec((tm, w), lambda i: (i, 0))

    def rshape(w, dt):
        return jax.ShapeDtypeStruct((t, w), dt)

    return pl.pallas_call(
        body, name="mix_proj", grid=(t // tm,),
        in_specs=[row(D_MODEL), _full((1, D_MODEL)), _full((D_MODEL, P_END)), _full((128, GLA_KW)),
                  _full((1, GLA_KW)), row(128), row(128)],
        out_specs=[row(D_MODEL), row(256), row(256), row(512), row(512), row(128), row(256), row(512), row(256),
                   row(256)],
        out_shape=[rshape(D_MODEL, BF16), rshape(256, F32), rshape(256, F32), rshape(512, F32), rshape(512, F32),
                   rshape(128, F32), rshape(256, F32), rshape(512, BF16), rshape(256, BF16), rshape(256, BF16)],
        compiler_params=_cparams(1),
    )(h1, wmixpre, winp, wa2p, bap, cos, sin)


def _scan_rows(x, reverse=False):
    n = x.shape[0]
    row = lax.broadcasted_iota(jnp.int32, x.shape, 0)
    s = 1
    while s < n:
        if reverse:
            x = x + jnp.where(row < n - s, pltpu.roll(x, n - s, 0), 0.0)
        else:
            x = x + jnp.where(row >= s, pltpu.roll(x, s, 0), 0.0)
        s *= 2
    return x


def _gla_cumsum(la, tril_f):
    b = _scan_rows(la)
    row = lax.broadcasted_iota(jnp.int32, b.shape, 0)
    bm = jnp.sum(jnp.where(row == GLA_CHUNK // 2 - 1, b, 0.0), axis=0, keepdims=True)
    bl = jnp.sum(jnp.where(row == GLA_CHUNK - 1, b, 0.0), axis=0, keepdims=True)
    return b, bm, bl


def _gla_decays(la, tril_f):
    b, bm, bl = _gla_cumsum(la, tril_f)
    return jnp.exp(b - bm), jnp.exp(bm - b), jnp.exp(b), jnp.exp(bl - b), jnp.exp(bl)


def _gla_masks():
    c = GLA_CHUNK
    r = lax.broadcasted_iota(jnp.int32, (c, c), 0)
    col = lax.broadcasted_iota(jnp.int32, (c, c), 1)
    r4 = lax.broadcasted_iota(jnp.int32, (GLA_HEADS * c, c), 0) % c
    c4 = lax.broadcasted_iota(jnp.int32, (GLA_HEADS * c, c), 1)
    klane = lax.broadcasted_iota(jnp.int32, (c, GLA_KW), 1) // GLA_DK
    vlane = lax.broadcasted_iota(jnp.int32, (c, GLA_W), 1) // GLA_DV
    srow = lax.broadcasted_iota(jnp.int32, (GLA_W, GLA_KW), 0) // GLA_DV
    scol = lax.broadcasted_iota(jnp.int32, (GLA_W, GLA_KW), 1) // GLA_DK
    return dict(tril_f=(r >= col).astype(F32), triu_f=(r <= col).astype(F32), tril4=r4 >= c4,
                khead=[klane == h for h in range(GLA_HEADS)], vhead=[vlane == h for h in range(GLA_HEADS)],
                diag=srow == scol)


def _stack_heads(x, head_masks):
    return jnp.concatenate([jnp.where(m, x, 0.0) for m in head_masks], axis=0)


def _gla_fwd(gq, gk, gv, la):
    t = gq.shape[0]
    rg = _seq_tile(t)
    nb = t // rg
    ncb = rg // GLA_CHUNK
    c = GLA_CHUNK

    def body(q_ref, k_ref, v_ref, la_ref, o_ref, ss_ref, st_ref):
        @pl.when(pl.program_id(0) == 0)
        def _():
            st_ref[...] = jnp.zeros_like(st_ref)

        mk = _gla_masks()
        st = st_ref[...]
        for ch in range(ncb):
            rows = slice(ch * c, (ch + 1) * c)
            eq, ek, eb, ekl, ebl = _gla_decays(la_ref[rows, :], mk["tril_f"])
            qs = q_ref[rows, :] * (GLA_DK ** -0.5)
            k = k_ref[rows, :]
            v = v_ref[rows, :].astype(BF16)
            ss_ref[ch] = st
            q4 = _stack_heads(qs * eq, mk["khead"]).astype(BF16)
            a4 = jnp.where(mk["tril4"], _dg(q4, (k * ek).astype(BF16), NT), 0.0).astype(BF16)
            r4 = _dot(a4, v)
            intra = jnp.concatenate([r4[h * c:(h + 1) * c, GLA_DV * h:GLA_DV * (h + 1)] for h in range(GLA_HEADS)],
                                    axis=1)
            o_ref[rows, :] = intra + _dg((qs * eb).astype(BF16), st.astype(BF16), NT)
            st = st * ebl + jnp.where(mk["diag"], _dg(v, (k * ekl).astype(BF16), TN), 0.0)
        st_ref[...] = st

    def row(w):
        return pl.BlockSpec((rg, w), lambda i: (i, 0))

    return pl.pallas_call(
        body, name="gla_fwd", grid=(nb,),
        in_specs=[row(256), row(256), row(512), row(256)],
        out_specs=[row(512), pl.BlockSpec((ncb, GLA_W, GLA_KW), lambda i: (i, 0, 0))],
        out_shape=[jax.ShapeDtypeStruct((t, GLA_W), F32), jax.ShapeDtypeStruct((nb * ncb, GLA_W, GLA_KW), F32)],
        scratch_shapes=[pltpu.VMEM((GLA_W, GLA_KW), F32)],
        compiler_params=_cparams(1),
    )(gq, gk, gv, la)


def _gla_bwd(gq, gk, gv, la, ss, do):
    t = gq.shape[0]
    rg = _seq_tile(t)
    nb = t // rg
    ncb = rg // GLA_CHUNK
    c = GLA_CHUNK

    def body(q_ref, k_ref, v_ref, la_ref, ss_ref, do_ref, dq_ref, dk_ref, dv_ref, dla_ref, dst_ref):
        @pl.when(pl.program_id(0) == 0)
        def _():
            dst_ref[...] = jnp.zeros_like(dst_ref)

        mk = _gla_masks()
        last_row = lax.broadcasted_iota(jnp.int32, (c, GLA_KW), 0) == c - 1
        scale = GLA_DK ** -0.5
        dstn = dst_ref[...]
        for ch in reversed(range(ncb)):
            rows = slice(ch * c, (ch + 1) * c)
            eq, ek, eb, ekl, ebl = _gla_decays(la_ref[rows, :], mk["tril_f"])
            qs = q_ref[rows, :] * scale
            k = k_ref[rows, :]
            qt, kt, qh, kh = qs * eq, k * ek, qs * eb, k * ekl
            ktb, khb, qhb = kt.astype(BF16), kh.astype(BF16), qh.astype(BF16)
            v = v_ref[rows, :].astype(BF16)
            do_f = do_ref[rows, :]
            dob = do_f.astype(BF16)
            st = ss_ref[ch]
            stb = st.astype(BF16)
            dstb = dstn.astype(BF16)
            q4 = _stack_heads(qt, mk["khead"]).astype(BF16)
            do4 = _stack_heads(do_f, mk["vhead"]).astype(BF16)
            a4 = jnp.where(mk["tril4"], _dg(q4, ktb, NT), 0.0).astype(BF16)
            da4 = jnp.where(mk["tril4"], _dg(do4, v, NT), 0.0).astype(BF16)
            dv_ref[rows, :] = _dg(a4, do4, TN) + _dg(khb, dstb, NT)
            dq4 = _dot(da4, ktb)
            dqt = jnp.zeros((c, GLA_KW), F32)
            for h in range(GLA_HEADS):
                dqt = dqt + jnp.where(mk["khead"][h], dq4[h * c:(h + 1) * c], 0.0)
            dkt = _dg(da4, q4, TN)
            dqh = _dot(dob, stb)
            dkh = _dot(v, dstb)
            dbl = jnp.sum(dstn * st, axis=0, keepdims=True)
            dstn = dstn * ebl + jnp.where(mk["diag"], _dg(dob, qhb, TN), 0.0)
            dq_ref[rows, :] = scale * (dqt * eq + dqh * eb)
            dk_ref[rows, :] = dkt * ek + dkh * ekl
            dkk = dkh * kh
            db = dqt * qt - dkt * kt + dqh * qh - dkk
            db = db + jnp.where(last_row, jnp.sum(dkk, axis=0, keepdims=True) + ebl * dbl, 0.0)
            dla_ref[rows, :] = _scan_rows(db, reverse=True)
        dst_ref[...] = dstn

    def row(w):
        return pl.BlockSpec((rg, w), lambda i: (nb - 1 - i, 0))

    def rshape(w):
        return jax.ShapeDtypeStruct((t, w), F32)

    return pl.pallas_call(
        body, name="gla_bwd", grid=(nb,),
        in_specs=[row(256), row(256), row(512), row(256),
                  pl.BlockSpec((ncb, GLA_W, GLA_KW), lambda i: (nb - 1 - i, 0, 0)), row(512)],
        out_specs=[row(256), row(256), row(512), row(256)],
        out_shape=[rshape(256), rshape(256), rshape(512), rshape(256)],
        scratch_shapes=[pltpu.VMEM((GLA_W, GLA_KW), F32)],
        compiler_params=_cparams(1),
    )(gq, gk, gv, la, ss, do)


SWA_G = SWA_QH // SWA_KVH


def _swa_bias():
    n = jnp.arange(3, dtype=jnp.int32)[:, None, None]
    r = (jnp.arange(SWA_G * BLK, dtype=jnp.int32) % BLK)[None, :, None]
    c = jnp.arange(3 * BLK, dtype=jnp.int32)[None, None, :]
    seg = c // BLK
    cc = c % BLK
    qpos = n * BLK + r - PAD
    kpos = jnp.where(seg == 0, (n - 1) * BLK, jnp.where(seg == 1, n * BLK, 0)) + cc - PAD
    band = (seg < 2) & (kpos >= N_META) & (kpos <= qpos) & (qpos - kpos < WINDOW)
    meta = (seg == 2) & (kpos >= 0) & (kpos < N_META) & (kpos <= qpos)
    return jnp.where(band | meta, 0.0, NEG_INF).astype(F32)


def _swa_stack(ref, rows, kh, lo, dtype):
    parts = []
    for g in range(2):
        pair = ref[rows, 128 * (2 * kh + g):128 * (2 * kh + g + 1)]
        zero = jnp.zeros_like(pair)
        parts += [jnp.where(lo, pair, zero), jnp.where(lo, zero, pair)]
    return jnp.concatenate(parts, axis=0).astype(dtype)


def _swa_unstack(x4, lo):
    return [jnp.where(lo, x4[2 * g * BLK:(2 * g + 1) * BLK], x4[(2 * g + 1) * BLK:(2 * g + 2) * BLK])
            for g in range(2)]


def _swa_sink_col(sink_ref, kh):
    blk = lax.broadcasted_iota(jnp.int32, (SWA_G * BLK, 1), 0) // BLK
    col = jnp.full((SWA_G * BLK, 1), sink_ref[SWA_G * kh + SWA_G - 1], F32)
    for e in reversed(range(SWA_G - 1)):
        col = jnp.where(blk == e, sink_ref[SWA_G * kh + e], col)
    return col


def _swa_softmax(qk, bias, sink):
    s = qk * (SWA_HD ** -0.5) + bias
    m = jnp.maximum(jnp.max(s, axis=-1, keepdims=True), sink)
    p = jnp.exp(s - m)
    es = jnp.exp(sink - m)
    inv = 1.0 / (jnp.sum(p, axis=-1, keepdims=True) + es)
    return p * inv, es * inv


def _swa_keys(prev_ref, cur_ref, first_ref, b, ls):
    before = prev_ref[:, ls] if b == 0 else cur_ref[(b - 1) * BLK:b * BLK, ls]
    return jnp.concatenate([before, cur_ref[b * BLK:(b + 1) * BLK, ls], first_ref[:, ls]], axis=0)


def _swa_specs(rs, ns):
    bps = rs // BLK
    cur = lambda w: pl.BlockSpec((rs, w), lambda i: (jnp.minimum(i, ns - 1), 0))
    prev = lambda w: pl.BlockSpec((BLK, w), lambda i: (jnp.maximum(jnp.minimum(i, ns - 1) * bps - 1, 0), 0))
    first = lambda w: pl.BlockSpec((BLK, w), lambda i: (0, 0))
    return cur, prev, first


def _swa_fwd(sinks, sq, sk, sv):
    t = sq.shape[0]
    rs = _seq_tile(t)
    bps, ns = rs // BLK, t // rs

    def body(sink_ref, bias_ref, q_ref, kp_ref, kc_ref, km_ref, vp_ref, vc_ref, vm_ref, o_ref):
        i = pl.program_id(0)
        lo = lax.broadcasted_iota(jnp.int32, (BLK, 128), 1) < 64
        sink_cols = [_swa_sink_col(sink_ref, kh) for kh in range(SWA_KVH)]
        chains = [(b, kh) for b in range(bps) for kh in range(SWA_KVH)]
        scores = []
        for b, kh in chains:
            ls = slice(128 * kh, 128 * (kh + 1))
            q4 = _swa_stack(q_ref, slice(b * BLK, (b + 1) * BLK), kh, lo, BF16)
            scores.append(_dg(q4, _swa_keys(kp_ref, kc_ref, km_ref, b, ls), NT))
        probs = []
        for (b, kh), s in zip(chains, scores):
            p, _ = _swa_softmax(s, bias_ref[jnp.minimum(i * bps + b, 2)], sink_cols[kh])
            probs.append(p.astype(BF16))
        for (b, kh), p in zip(chains, probs):
            ls = slice(128 * kh, 128 * (kh + 1))
            rows = slice(b * BLK, (b + 1) * BLK)
            for g, pair in enumerate(_swa_unstack(_dot(p, _swa_keys(vp_ref, vc_ref, vm_ref, b, ls)), lo)):
                o_ref[rows, 128 * (2 * kh + g):128 * (2 * kh + g + 1)] = pair

    cur, prev, first = _swa_specs(rs, ns)
    bias = _swa_bias()
    return pl.pallas_call(
        body, name="swa_fwd", grid=(ns,),
        in_specs=[pl.BlockSpec(memory_space=pltpu.SMEM), _full(bias.shape), cur(512), prev(256), cur(256), first(256),
                  prev(256), cur(256), first(256)],
        out_specs=cur(512),
        out_shape=jax.ShapeDtypeStruct((t, SWA_W), F32),
        compiler_params=_cparams(1),
    )(sinks, bias, sq, sk, sk, sk, sv, sv, sv)


def _swa_bwd(sinks, sq, sk, sv, o, do, hook=None):
    t = sq.shape[0]
    rs = _seq_tile(t)
    bps, ns = rs // BLK, t // rs

    def body(sink_ref, bias_ref, q_ref, kp_ref, kc_ref, km_ref, vp_ref, vc_ref, vm_ref, o_ref, do_ref,
             dq_ref, dk_ref, dv_ref, dkm_ref, dvm_ref, dsink_ref, pk_ref, pv_ref):
        i = pl.program_id(0)

        @pl.when(i == 0)
        def _():
            pk_ref[...] = jnp.zeros_like(pk_ref)
            pv_ref[...] = jnp.zeros_like(pv_ref)
            dkm_ref[...] = jnp.zeros_like(dkm_ref)
            dvm_ref[...] = jnp.zeros_like(dvm_ref)
            dsink_ref[...] = jnp.zeros_like(dsink_ref)

        @pl.when(i == ns)
        def _():
            dk_ref[...] = pk_ref[...]
            dv_ref[...] = pv_ref[...]

        @pl.when(i < ns)
        def _():
            lo = lax.broadcasted_iota(jnp.int32, (BLK, 128), 1) < 64
            scale = SWA_HD ** -0.5
            sink_cols = [_swa_sink_col(sink_ref, kh) for kh in range(SWA_KVH)]
            parts_k = [[None] * SWA_KVH for _ in range(bps)]
            parts_v = [[None] * SWA_KVH for _ in range(bps)]
            dsinks = [jnp.zeros((1, 1), F32) for _ in range(SWA_QH)]
            chains = [(b, kh) for b in range(bps) for kh in range(SWA_KVH)]
            lanes = lambda kh: slice(128 * kh, 128 * (kh + 1))
            block = lambda b: slice(b * BLK, (b + 1) * BLK)
            q4s = [_swa_stack(q_ref, block(b), kh, lo, BF16) for b, kh in chains]
            scores = [_dg(q4, _swa_keys(kp_ref, kc_ref, km_ref, b, lanes(kh)), NT)
                      for (b, kh), q4 in zip(chains, q4s)]
            do4s = [_swa_stack(do_ref, block(b), kh, lo, F32) for b, kh in chains]
            do4bs = [d.astype(BF16) for d in do4s]
            dps = [_dg(d, _swa_keys(vp_ref, vc_ref, vm_ref, b, lanes(kh)), NT) for (b, kh), d in zip(chains, do4bs)]
            pbs, dss = [], []
            for n_chain, (b, kh) in enumerate(chains):
                p, psink = _swa_softmax(scores[n_chain], bias_ref[jnp.minimum(i * bps + b, 2)], sink_cols[kh])
                delta = jnp.sum(do4s[n_chain] * _swa_stack(o_ref, block(b), kh, lo, F32), axis=-1, keepdims=True)
                dss.append((p * (dps[n_chain] - delta) * scale).astype(BF16))
                pbs.append(p.astype(BF16))
                dsk = psink * delta
                for e in range(SWA_G):
                    h = SWA_G * kh + e
                    dsinks[h] = dsinks[h] - jnp.sum(dsk[e * BLK:(e + 1) * BLK], axis=0, keepdims=True)
            for n_chain, (b, kh) in enumerate(chains):
                kall = _swa_keys(kp_ref, kc_ref, km_ref, b, lanes(kh))
                for g, pair in enumerate(_swa_unstack(_dot(dss[n_chain], kall), lo)):
                    dq_ref[block(b), 128 * (2 * kh + g):128 * (2 * kh + g + 1)] = pair
                parts_k[b][kh] = _dg(dss[n_chain], q4s[n_chain], TN)
                parts_v[b][kh] = _dg(pbs[n_chain], do4bs[n_chain], TN)
            last = slice(rs - BLK, rs)
            for parts, out_ref, pend_ref, meta_ref in ((parts_k, dk_ref, pk_ref, dkm_ref),
                                                       (parts_v, dv_ref, pv_ref, dvm_ref)):
                for kh in range(SWA_KVH):
                    ls = slice(128 * kh, 128 * (kh + 1))
                    if bps > 1:
                        out_ref[0:rs - BLK, ls] = pend_ref[0:rs - BLK, ls]
                    out_ref[last, ls] = pend_ref[last, ls] + parts[0][kh][0:BLK]
                    meta = parts[0][kh][2 * BLK:3 * BLK]
                    for b in range(bps):
                        own = parts[b][kh][BLK:2 * BLK]
                        if b + 1 < bps:
                            own = own + parts[b + 1][kh][0:BLK]
                            meta = meta + parts[b + 1][kh][2 * BLK:3 * BLK]
                        pend_ref[b * BLK:(b + 1) * BLK, ls] = own
                    meta_ref[:, ls] += meta
            for h in range(SWA_QH):
                dsink_ref[h:h + 1, :] += jnp.broadcast_to(dsinks[h], (1, 128))

    cur, prev, first = _swa_specs(rs, ns)
    late = lambda w: pl.BlockSpec((rs, w), lambda i: (jnp.maximum(i - 1, 0), 0))
    bias = _swa_bias()
    return _pallas(
        body, name="swa_bwd", grid=(ns + 1,),
        in_specs=[pl.BlockSpec(memory_space=pltpu.SMEM), _full(bias.shape), cur(512), prev(256), cur(256), first(256),
                  prev(256), cur(256), first(256), cur(512), cur(512)],
        out_specs=[cur(512), late(256), late(256), first(256), first(256), _full((SWA_QH, 128))],
        out_shape=[jax.ShapeDtypeStruct((t, SWA_W), F32), jax.ShapeDtypeStruct((t, 256), F32),
                   jax.ShapeDtypeStruct((t, 256), F32), jax.ShapeDtypeStruct((BLK, 256), F32),
                   jax.ShapeDtypeStruct((BLK, 256), F32), jax.ShapeDtypeStruct((SWA_QH, 128), F32)],
        scratch_shapes=[pltpu.VMEM((rs, 256), F32), pltpu.VMEM((rs, 256), F32)],
        args=(sinks, bias, sq, sk, sk, sk, sv, sv, sv, o, do), hook=hook)


def _mix_out(h1, ogla, gg, oswa, wgn, wsn, wout, wpost):
    t = h1.shape[0]
    tm = _row_tile(t)

    def body(h_ref, og_ref, gg_ref, os_ref, wgn_ref, wsn_ref, wout_ref, wpost_ref, h2_ref, cat_ref, m_ref):
        parts = []
        for h in range(GLA_HEADS):
            ls = slice(GLA_DV * h, GLA_DV * (h + 1))
            y, _, _ = _rms(og_ref[:, ls], wgn_ref[...])
            g = gg_ref[:, ls]
            parts.append(y * (g * _sigmoid(g)))
        ys, _, _ = _rms(os_ref[...], wsn_ref[...])
        cat = jnp.concatenate(parts + [ys], axis=1).astype(BF16)
        cat_ref[...] = cat
        m = _dot(cat, wout_ref[...])
        m_ref[...] = m
        y, _, _ = _rms(m, wpost_ref[...])
        h2_ref[...] = h_ref[...] + y

    def row(w):
        return pl.BlockSpec((tm, w), lambda i: (i, 0))

    return pl.pallas_call(
        body, name="mix_out", grid=(t // tm,),
        in_specs=[row(D_MODEL), row(512), row(512), row(512), _full((1, GLA_DV)), _full((1, SWA_W)),
                  _full((D_MODEL, D_MODEL)), _full((1, D_MODEL))],
        out_specs=[row(D_MODEL), row(D_MODEL), row(D_MODEL)],
        out_shape=[jax.ShapeDtypeStruct((t, D_MODEL), F32), jax.ShapeDtypeStruct((t, D_MODEL), BF16),
                   jax.ShapeDtypeStruct((t, D_MODEL), F32)],
        compiler_params=_cparams(1),
    )(h1, ogla, gg, oswa, wgn, wsn, wout, wpost)


def _mix_out_bwd(dh2, m, ogla, gg, oswa, wgn, wsn, wout, wpost, hook=None):
    t = dh2.shape[0]
    tm = _row_tile(t)

    def body(dh_ref, m_ref, og_ref, gg_ref, os_ref, wgn_ref, wsn_ref, wout_ref, wpost_ref,
             dog_ref, dgg_ref, dos_ref, dm_ref, dwpost_ref, dwgn_ref, dwsn_ref):
        @pl.when(pl.program_id(0) == 0)
        def _():
            dwpost_ref[...] = jnp.zeros_like(dwpost_ref)
            dwgn_ref[...] = jnp.zeros_like(dwgn_ref)
            dwsn_ref[...] = jnp.zeros_like(dwsn_ref)

        wpost = wpost_ref[...]
        _, mh, r = _rms(m_ref[...], wpost)
        dm, dw = _rms_bwd(mh, r, wpost, dh_ref[...])
        dwpost_ref[...] += dw
        dmb = dm.astype(BF16)
        dm_ref[...] = dmb
        dcat = _dg(dmb, wout_ref[...], NT)
        wgn = wgn_ref[...]
        for h in range(GLA_HEADS):
            ls = slice(GLA_DV * h, GLA_DV * (h + 1))
            dog = dcat[:, ls]
            g = gg_ref[:, ls]
            sg = _sigmoid(g)
            y, xh, r = _rms(og_ref[:, ls], wgn)
            dgg_ref[:, ls] = dog * y * (sg * (1.0 + g * (1.0 - sg)))
            dx, dw = _rms_bwd(xh, r, wgn, dog * (g * sg))
            dog_ref[:, ls] = dx
            dwgn_ref[...] += dw
        wsn = wsn_ref[...]
        _, xh, r = _rms(os_ref[...], wsn)
        dx, dw = _rms_bwd(xh, r, wsn, dcat[:, GLA_W:])
        dos_ref[...] = dx
        dwsn_ref[...] += dw

    def row(w):
        return pl.BlockSpec((tm, w), lambda i: (i, 0))

    def rshape(w, dt=F32):
        return jax.ShapeDtypeStruct((t, w), dt)

    return _pallas(
        body, name="mix_out_bwd", grid=(t // tm,),
        in_specs=[row(D_MODEL), row(D_MODEL), row(512), row(512), row(512), _full((1, GLA_DV)), _full((1, SWA_W)),
                  _full((D_MODEL, D_MODEL)), _full((1, D_MODEL))],
        out_specs=[row(512), row(512), row(512), row(D_MODEL), _full((1, D_MODEL)), _full((1, GLA_DV)),
                   _full((1, SWA_W))],
        out_shape=[rshape(512), rshape(512), rshape(512), rshape(D_MODEL, BF16),
                   jax.ShapeDtypeStruct((1, D_MODEL), F32), jax.ShapeDtypeStruct((1, GLA_DV), F32),
                   jax.ShapeDtypeStruct((1, SWA_W), F32)],
        args=(dh2, m, ogla, gg, oswa, wgn, wsn, wout, wpost), hook=hook)


def _mix_in_bwd(dh2, h1, wmixpre, winp, wa2p, bap, cos, sin, ga, dgq, dgk, dgv, dgg, dla, dsq, dsk, dsv, dkm, dvm):
    t = h1.shape[0]
    tm = _row_tile(t)

    def body(dh2_ref, h_ref, w_ref, win_ref, wa2_ref, ba_ref, cos_ref, sin_ref, ga_ref, dgq_ref, dgk_ref, dgv_ref,
             dgg_ref, dla_ref, dsq_ref, dsk_ref, dsv_ref, dkm_ref, dvm_ref,
             dh1_ref, dproj_ref, dw_ref, dwa2_ref, dba_ref):
        i = pl.program_id(0)

        @pl.when(i == 0)
        def _():
            dw_ref[...] = jnp.zeros_like(dw_ref)
            dwa2_ref[...] = jnp.zeros_like(dwa2_ref)
            dba_ref[...] = jnp.zeros_like(dba_ref)

        first = (i == 0).astype(F32)
        c = cos_ref[...]
        s = -sin_ref[...]
        fh = _first_half_mask(tm)
        dproj_ref[:, P_GQ:P_GK] = dgq_ref[...].astype(BF16)
        dproj_ref[:, P_GK:P_GV] = dgk_ref[...].astype(BF16)
        dproj_ref[:, P_GV:P_GG] = dgv_ref[...].astype(BF16)
        dproj_ref[:, P_GG:P_GA] = dgg_ref[...].astype(BF16)
        gab = ga_ref[...].astype(BF16)
        z = _dot(gab, wa2_ref[...]) + ba_ref[...]
        row_id = i * tm + lax.broadcasted_iota(jnp.int32, (tm, 1), 0)
        dz = jnp.where(row_id >= PAD, dla_ref[...] * (1.0 / GLA_TAU) * (1.0 - _sigmoid(z)), 0.0)
        dzb = dz.astype(BF16)
        dba_ref[...] += jnp.sum(dz, axis=0, keepdims=True)
        dwa2_ref[...] += _dg(gab, dzb, TN)
        dproj_ref[:, P_GA:P_SQ] = _dg(dzb, wa2_ref[...], NT).astype(BF16)
        for k in range(4):
            dy = dsq_ref[:, 128 * k:128 * (k + 1)]
            dproj_ref[:, P_SQ + 128 * k:P_SQ + 128 * (k + 1)] = (dy * c + _rot_half(dy, fh) * s).astype(BF16)
        for k in range(2):
            ls = slice(128 * k, 128 * (k + 1))
            dy = dsk_ref[:, ls]
            dy = jnp.concatenate([dy[:BLK] + first * dkm_ref[:, ls], dy[BLK:]], axis=0) if tm > BLK else (
                dy + first * dkm_ref[:, ls])
            dproj_ref[:, P_SK + 128 * k:P_SK + 128 * (k + 1)] = (dy * c + _rot_half(dy, fh) * s).astype(BF16)
            dv = dsv_ref[:, ls]
            dv = jnp.concatenate([dv[:BLK] + first * dvm_ref[:, ls], dv[BLK:]], axis=0) if tm > BLK else (
                dv + first * dvm_ref[:, ls])
            dproj_ref[:, P_SV + 128 * k:P_SV + 128 * (k + 1)] = dv.astype(BF16)
        dn = _dg(dproj_ref[...], win_ref[...], NT)
        w = w_ref[...]
        _, hh, r = _rms(h_ref[...], w)
        dx, dw = _rms_bwd(hh, r, w, dn)
        dw_ref[...] += dw
        dh1_ref[...] = dh2_ref[...] + dx

    def row(w):
        return pl.BlockSpec((tm, w), lambda i: (i, 0))

    return pl.pallas_call(
        body, name="mix_in_bwd", grid=(t // tm,),
        in_specs=[row(D_MODEL), row(D_MODEL), _full((1, D_MODEL)), _full((D_MODEL, P_END)), _full((128, GLA_KW)),
                  _full((1, GLA_KW)), row(128), row(128), row(128), row(256), row(256), row(512), row(512), row(256),
                  row(512), row(256), row(256), _full((BLK, 256)), _full((BLK, 256))],
        out_specs=[row(D_MODEL), row(P_END), _full((1, D_MODEL)), _full((128, GLA_KW)), _full((1, GLA_KW))],
        out_shape=[jax.ShapeDtypeStruct((t, D_MODEL), F32), jax.ShapeDtypeStruct((t, P_END), BF16),
                   jax.ShapeDtypeStruct((1, D_MODEL), F32), jax.ShapeDtypeStruct((128, GLA_KW), F32),
                   jax.ShapeDtypeStruct((1, GLA_KW), F32)],
        compiler_params=_cparams(1),
    )(dh2, h1, wmixpre, winp, wa2p, bap, cos, sin, ga, dgq, dgk, dgv, dgg, dla, dsq, dsk, dsv, dkm, dvm)


def _adamw_update(w, g, m, v):
    m = ADAM_B1 * m + (1.0 - ADAM_B1) * g
    v = ADAM_B2 * v + (1.0 - ADAM_B2) * (g * g)
    m_hat = m / (1.0 - ADAM_B1 ** ADAM_STEP)
    v_hat = v / (1.0 - ADAM_B2 ** ADAM_STEP)
    return -ADAM_LR * (m_hat / (jnp.sqrt(v_hat) + ADAM_EPS) + ADAM_WD * w), m, v


def _adamw_halves(w, g_mine, g_other, m, v, c_idx, row0=0):
    r, c = w.shape
    h = g_mine.shape[0]
    tr = _div_tile(math.gcd(r, h))
    nth = h // tr
    t0 = row0 // tr
    assert t0 * tr == row0

    def body(c_ref, w_ref, gm_ref, go_ref, m_ref, v_ref, g_ref, d_ref, nm_ref, nv_ref):
        hh = (t0 + pl.program_id(0)) // nth
        g = jnp.where(hh == c_ref[0], gm_ref[...], go_ref[...])
        g_ref[...] = g
        d_ref[...], nm_ref[...], nv_ref[...] = _adamw_update(w_ref[...], g, m_ref[...], v_ref[...])

    spec = pl.BlockSpec((tr, c), lambda i, c_ref: (i, 0))

    def gspec(is_mine):
        def index(i, c_ref):
            used = ((t0 + i) // nth == c_ref[0]) == is_mine
            return (jnp.where(used, (t0 + i) % nth, 0), 0)
        return pl.BlockSpec((tr, c), index)

    shape = jax.ShapeDtypeStruct((r, c), F32)
    return pl.pallas_call(
        body, name="adamw_halves",
        grid_spec=pltpu.PrefetchScalarGridSpec(
            num_scalar_prefetch=1, grid=(r // tr,), in_specs=[spec, gspec(True), gspec(False), spec, spec],
            out_specs=[spec] * 4),
        out_shape=[shape] * 4, compiler_params=_cparams(1),
    )(c_idx, w, g_mine, g_other, m, v)


def _place():
    x, y, c = lax.axis_index("x"), lax.axis_index("y"), lax.axis_index("c")
    chips = [(1 - x, y), (x, 1 - y), (1 - x, 1 - y)]
    return x, y, c, chips


def _remote(send_sem, recv_sem, src, dst, to):
    return pltpu.make_async_remote_copy(src_ref=src, dst_ref=dst, send_sem=send_sem, recv_sem=recv_sem,
                                        device_id=to, device_id_type=MESH)


def _half(ref_rows, c):
    h = ref_rows // 2
    return pl.ds(pl.multiple_of(c * h, 8), h)


def _own_slot(shard, q):
    return lax.dynamic_update_slice(jnp.zeros((N_CHIPS,) + shard.shape, shard.dtype), shard[None], (q, 0, 0))


def _stack_own_slot(mats, q_idx):
    r, w = mats[0].shape
    tr = _div_tile(r)
    per = r // tr
    n = len(mats)

    def body(q_ref, *refs):
        m_refs, o_ref = refs[:n], refs[n]
        s = pl.program_id(0)
        for k in range(n):
            @pl.when(s // per == k)
            def _(k=k):
                o_ref[...] = m_refs[k][...].astype(BF16)

    def rows_of(k):
        return lambda s, q_ref: (jnp.where(s // per == k, s % per, 0), 0)

    return pl.pallas_call(
        body, name="stack_own_slot",
        grid_spec=pltpu.PrefetchScalarGridSpec(
            num_scalar_prefetch=1, grid=(n * per,),
            in_specs=[pl.BlockSpec((tr, w), rows_of(k)) for k in range(n)],
            out_specs=pl.BlockSpec((None, tr, w), lambda s, q_ref: (q_ref[0], s, 0))),
        out_shape=jax.ShapeDtypeStruct((N_CHIPS, n * r, w), BF16), compiler_params=_cparams(1),
    )(q_idx, *mats)


class _GatherChips:
    has_mid = True

    def __init__(self, bufs):
        n = len(bufs)
        self.inputs = list(bufs)
        self.out_shape = [jax.ShapeDtypeStruct(b.shape, b.dtype) for b in bufs]
        self.aliases = [(t, t) for t in range(n)]
        self.scratch = [pltpu.SemaphoreType.DMA((n, 6)), pltpu.SemaphoreType.DMA((n, 6))]

    def start(self, ins, outs, scr):
        send, recv = scr
        x, y, c, chips = _place()
        q = 2 * x + y
        for t, (i_ref, o_ref) in enumerate(zip(ins, outs)):
            rows = _half(i_ref.shape[1], c)
            for j, (cx, cy) in enumerate(chips):
                _remote(send.at[t, j], recv.at[t, j], i_ref.at[q, rows], o_ref.at[q, rows], (cx, cy, c)).start()

    def mid(self, ins, outs, scr):
        send, recv = scr
        x, y, c, chips = _place()
        for t, o_ref in enumerate(outs):
            rows = _half(o_ref.shape[1], c)
            for j, (cx, cy) in enumerate(chips):
                slot = o_ref.at[2 * cx + cy, rows]
                _remote(send.at[t, j], recv.at[t, j], slot, slot, (cx, cy, c)).wait_recv()
                _remote(send.at[t, 3 + j], recv.at[t, 3 + j], slot, slot, (x, y, 1 - c)).start()

    def finish(self, ins, outs, scr):
        send, recv = scr
        x, y, c, chips = _place()
        for t, o_ref in enumerate(outs):
            mine, other = _half(o_ref.shape[1], c), _half(o_ref.shape[1], 1 - c)
            for j, (cx, cy) in enumerate(chips):
                slot = o_ref.at[2 * cx + cy, other]
                _remote(send.at[t, 3 + j], recv.at[t, 3 + j], slot, slot, (x, y, 1 - c)).wait_recv()
            for j, (cx, cy) in enumerate(chips):
                sent = o_ref.at[2 * cx + cy, mine]
                _remote(send.at[t, j], recv.at[t, j], sent, sent, (cx, cy, c)).wait_send()
                _remote(send.at[t, 3 + j], recv.at[t, 3 + j], sent, sent, (x, y, 1 - c)).wait_send()


class _PairExchange:
    has_mid = False
    aliases = ()

    def __init__(self, arrs):
        n = len(arrs)
        self.inputs = list(arrs)
        self.out_shape = [jax.ShapeDtypeStruct((a.shape[0], a.shape[1] // 2, a.shape[2]), a.dtype) for a in arrs]
        self.scratch = [pltpu.SemaphoreType.DMA((n,)), pltpu.SemaphoreType.DMA((n,))]

    def _copies(self, ins, outs, scr):
        send, recv = scr
        x, y, c, _ = _place()
        return [_remote(send.at[t], recv.at[t], i_ref.at[:, _half(i_ref.shape[1], 1 - c)], o_ref, (x, y, 1 - c))
                for t, (i_ref, o_ref) in enumerate(zip(ins, outs))]

    def start(self, ins, outs, scr):
        for cp in self._copies(ins, outs, scr):
            cp.start()

    def finish(self, ins, outs, scr):
        for cp in self._copies(ins, outs, scr):
            cp.wait()


class _ChipScatter:
    has_mid = False
    aliases = ()

    def __init__(self, arrs):
        n = len(arrs)
        self.inputs = list(arrs)
        self.out_shape = [jax.ShapeDtypeStruct((3,) + a.shape[1:], a.dtype) for a in arrs]
        self.scratch = [pltpu.SemaphoreType.DMA((n, 3)), pltpu.SemaphoreType.DMA((n, 3))]

    def _copies(self, ins, outs, scr):
        send, recv = scr
        x, y, c, chips = _place()
        return [_remote(send.at[t, j], recv.at[t, j], i_ref.at[2 * cx + cy], o_ref.at[j], (cx, cy, c))
                for t, (i_ref, o_ref) in enumerate(zip(ins, outs)) for j, (cx, cy) in enumerate(chips)]

    def start(self, ins, outs, scr):
        for cp in self._copies(ins, outs, scr):
            cp.start()

    def finish(self, ins, outs, scr):
        for cp in self._copies(ins, outs, scr):
            cp.wait()


class _PairShare:
    has_mid = False
    aliases = ()

    def __init__(self, arrs):
        n = len(arrs)
        self.inputs = list(arrs)
        self.out_shape = [jax.ShapeDtypeStruct(a.shape, a.dtype) for a in arrs]
        self.scratch = [pltpu.SemaphoreType.DMA((n,)), pltpu.SemaphoreType.DMA((n,))]

    def _copies(self, ins, outs, scr):
        send, recv = scr
        x, y, c, _ = _place()
        return [_remote(send.at[t], recv.at[t], i_ref, o_ref, (x, y, 1 - c))
                for t, (i_ref, o_ref) in enumerate(zip(ins, outs))]

    def start(self, ins, outs, scr):
        for cp in self._copies(ins, outs, scr):
            cp.start()

    def finish(self, ins, outs, scr):
        for cp in self._copies(ins, outs, scr):
            cp.wait()


def _comm_call(hook, name):
    n_in, n_out = len(hook.inputs), len(hook.out_shape)

    def body(*refs):
        ins, outs, scr = refs[:n_in], refs[n_in:n_in + n_out], refs[n_in + n_out:]
        hook.start(ins, outs, scr)
        if hook.has_mid:
            hook.mid(ins, outs, scr)
        hook.finish(ins, outs, scr)

    return pl.pallas_call(body, name=name, in_specs=[ANY] * n_in, out_specs=[ANY] * n_out,
                          out_shape=list(hook.out_shape), scratch_shapes=list(hook.scratch),
                          input_output_aliases=dict(hook.aliases))(*hook.inputs)


class _GatherDevices:
    has_mid = True
    aliases = ()

    def __init__(self, vecs):
        n = len(vecs)
        self.inputs = list(vecs)
        self.out_shape = [jax.ShapeDtypeStruct((N_DEV,) + v.shape, v.dtype) for v in vecs]
        self.scratch = [pltpu.SemaphoreType.DMA((n, 7)), pltpu.SemaphoreType.DMA((n, 7)),
                        pltpu.SemaphoreType.DMA((n,))]

    @staticmethod
    def _copy(scr, t, k, out_ref, block, to, src=None):
        send, recv, _ = scr
        px, py, pc = block
        slot = out_ref.at[4 * px + 2 * py + pc]
        return _remote(send.at[t, k], recv.at[t, k], slot if src is None else src, slot, to)

    def start(self, ins, outs, scr):
        x, y, c, chips = _place()
        me = (x, y, c)
        for t, (x_ref, out_ref) in enumerate(zip(ins, outs)):
            pltpu.make_async_copy(x_ref, out_ref.at[4 * x + 2 * y + c], scr[2].at[t]).start()
            self._copy(scr, t, 0, out_ref, me, (x, y, 1 - c), src=x_ref).start()
            for j, chip in enumerate(chips):
                self._copy(scr, t, 1 + j, out_ref, me, (*chip, c), src=x_ref).start()

    def mid(self, ins, outs, scr):
        x, y, c, chips = _place()
        for t, out_ref in enumerate(outs):
            for j, chip in enumerate(chips):
                self._copy(scr, t, 1 + j, out_ref, (*chip, c), (x, y, c)).wait_recv()
                self._copy(scr, t, 4 + j, out_ref, (*chip, c), (x, y, 1 - c)).start()

    def finish(self, ins, outs, scr):
        x, y, c, chips = _place()
        me = (x, y, c)
        for t, (x_ref, out_ref) in enumerate(zip(ins, outs)):
            self._copy(scr, t, 0, out_ref, (x, y, 1 - c), me).wait_recv()
            for j, chip in enumerate(chips):
                self._copy(scr, t, 4 + j, out_ref, (*chip, 1 - c), me).wait_recv()
            self._copy(scr, t, 0, out_ref, me, (x, y, 1 - c), src=x_ref).wait_send()
            for j, chip in enumerate(chips):
                self._copy(scr, t, 1 + j, out_ref, me, (*chip, c), src=x_ref).wait_send()
                self._copy(scr, t, 4 + j, out_ref, (*chip, c), (x, y, 1 - c)).wait_send()
            pltpu.make_async_copy(x_ref, out_ref.at[4 * x + 2 * y + c], scr[2].at[t]).wait()


class _Hooks:
    def __init__(self, hooks):
        self.hooks = list(hooks)
        self.has_mid = any(h.has_mid for h in hooks)
        self.inputs = [a for h in hooks for a in h.inputs]
        self.out_shape = [s for h in hooks for s in h.out_shape]
        self.scratch = [s for h in hooks for s in h.scratch]
        self.aliases = []
        i0 = o0 = 0
        for h in hooks:
            self.aliases += [(i0 + a, o0 + b) for a, b in h.aliases]
            i0 += len(h.inputs)
            o0 += len(h.out_shape)

    def _each(self, ins, outs, scr):
        i0 = o0 = s0 = 0
        for h in self.hooks:
            ni, no, ns = len(h.inputs), len(h.out_shape), len(h.scratch)
            yield h, ins[i0:i0 + ni], outs[o0:o0 + no], scr[s0:s0 + ns]
            i0, o0, s0 = i0 + ni, o0 + no, s0 + ns

    def start(self, ins, outs, scr):
        for h, i, o, s in self._each(ins, outs, scr):
            h.start(i, o, s)

    def mid(self, ins, outs, scr):
        for h, i, o, s in self._each(ins, outs, scr):
            if h.has_mid:
                h.mid(i, o, s)

    def finish(self, ins, outs, scr):
        for h, i, o, s in self._each(ins, outs, scr):
            h.finish(i, o, s)

    def split(self, outs):
        res, o0 = [], 0
        for h in self.hooks:
            res.append(list(outs[o0:o0 + len(h.out_shape)]))
            o0 += len(h.out_shape)
        return res


def _pair_sum(g, other, c_idx):
    nq, r, w = g.shape
    h = r // 2
    tr = _div_tile(h)
    nt = h // tr

    def body(c_ref, g_ref, o_ref, s_ref):
        s_ref[...] = (g_ref[...].astype(F32) + o_ref[...].astype(F32)).astype(s_ref.dtype)

    return pl.pallas_call(
        body, name="pair_sum",
        grid_spec=pltpu.PrefetchScalarGridSpec(
            num_scalar_prefetch=1, grid=(nq, nt),
            in_specs=[pl.BlockSpec((None, tr, w), lambda k, i, c_ref: (k, c_ref[0] * nt + i, 0)),
                      pl.BlockSpec((None, tr, w), lambda k, i, c_ref: (k, i, 0))],
            out_specs=pl.BlockSpec((None, tr, w), lambda k, i, c_ref: (k, i, 0))),
        out_shape=jax.ShapeDtypeStruct((nq, h, w), g.dtype),
        compiler_params=_cparams(2),
    )(c_idx, g, other)


def _chip_sum(s, others, q_idx):
    _, h, w = s.shape
    tr = _div_tile(h)

    def body(q_ref, s_ref, o_ref, out_ref):
        out_ref[...] = ((s_ref[...].astype(F32) + o_ref[0].astype(F32)) + o_ref[1].astype(F32)) + o_ref[2].astype(F32)

    return pl.pallas_call(
        body, name="chip_sum",
        grid_spec=pltpu.PrefetchScalarGridSpec(
            num_scalar_prefetch=1, grid=(h // tr,),
            in_specs=[pl.BlockSpec((None, tr, w), lambda i, q_ref: (q_ref[0], i, 0)),
                      pl.BlockSpec((3, tr, w), lambda i, q_ref: (0, i, 0))],
            out_specs=pl.BlockSpec((tr, w), lambda i, q_ref: (i, 0))),
        out_shape=jax.ShapeDtypeStruct((h, w), F32),
        compiler_params=_cparams(1),
    )(q_idx, s, others)


def _small_update(q_idx, parts, ws, ms, vs, col_block):
    n = len(parts)
    has_w = [w is not None for w in ws]

    def body(q_ref, *refs):
        pos = 0
        ins = []
        for t in range(n):
            k = 4 if has_w[t] else 1
            ins.append(refs[pos:pos + k])
            pos += k
        outs = refs[pos:]
        opos = 0
        for t in range(n):
            p_ref = ins[t][0]
            g = p_ref[0]
            for s in range(1, p_ref.shape[0]):
                g = g + p_ref[s]
            if has_w[t]:
                _, w_ref, m_ref, v_ref = ins[t]
                g_ref, d_ref, nm_ref, nv_ref = outs[opos:opos + 4]
                opos += 4
                g_ref[...] = g
                d_ref[...], nm_ref[...], nv_ref[...] = _adamw_update(w_ref[...], g, m_ref[...], v_ref[...])
            else:
                outs[opos][...] = g
                opos += 1

    def whole(shape):
        nd = len(shape)
        return pl.BlockSpec(shape, lambda i, q_ref: (0,) * nd)

    in_specs, out_specs, out_shape, args = [], [], [], []
    for t in range(n):
        k, r, wf = parts[t].shape
        if col_block[t]:
            w = wf // N_CHIPS
            in_specs.append(pl.BlockSpec((k, r, w), lambda i, q_ref: (0, 0, q_ref[0])))
        else:
            w = wf
            in_specs.append(whole((k, r, wf)))
        args.append(parts[t])
        if has_w[t]:
            assert ws[t].shape == (r, w), (ws[t].shape, r, w)
            in_specs += [whole((r, w))] * 3
            args += [ws[t], ms[t], vs[t]]
            out_specs += [whole((r, w))] * 4
            out_shape += [jax.ShapeDtypeStruct((r, w), F32)] * 4
        else:
            out_specs.append(whole((r, w)))
            out_shape.append(jax.ShapeDtypeStruct((r, w), F32))
    return pl.pallas_call(
        body, name="small_update",
        grid_spec=pltpu.PrefetchScalarGridSpec(num_scalar_prefetch=1, grid=(1,), in_specs=in_specs,
                                               out_specs=out_specs),
        out_shape=out_shape, compiler_params=_cparams(1),
    )(q_idx, *args)


_PACK_SEGMENTS = ((0, 1552), None, (1552, 2064), (2064, 2128), (2064, 2128), (2128, 2192), (2128, 2192),
                  (2192, 2256), (2192, 2256), (2256, 2320), (2256, 2320))
_UNPACK_SEGMENTS = (((0, 1552), (0,)), ((1552, 2064), (P_SQ,)), ((2064, 2128), (P_SK, P_SK + 64)),
                    ((2128, 2192), (P_SK + 128, P_SK + 192)), ((2192, 2256), (P_SV, P_SV + 64)),
                    ((2256, 2320), (P_SV + 128, P_SV + 192)))


def _pack_win(w4):
    per = w4.shape[2]
    pieces = []
    for seg in _PACK_SEGMENTS:
        if seg is None:
            pieces.append(jnp.zeros((w4.shape[1], 128 - GLA_RANK), w4.dtype))
            continue
        for q in range(w4.shape[0]):
            lo, hi = max(seg[0], q * per), min(seg[1], (q + 1) * per)
            if lo < hi:
                pieces.append(w4[q][:, lo - q * per:hi - q * per])
    return jnp.concatenate(pieces, axis=1)


def _unpack_dwin(d):
    per = D_IN // N_CHIPS
    chips = []
    for q in range(N_CHIPS):
        pieces = []
        for (a, b), starts in _UNPACK_SEGMENTS:
            lo, hi = max(a, q * per), min(b, (q + 1) * per)
            if lo < hi:
                copies = [d[:, s + lo - a:s + hi - a] for s in starts]
                pieces.append(copies[0] if len(copies) == 1 else copies[0] + copies[1])
        chips.append(jnp.concatenate(pieces, axis=1))
    return jnp.stack(chips).astype(BF16)


def _local_step(x, target, meta, p):
    s = x.shape[0]
    t = s + BLK
    h0 = jnp.concatenate([jnp.zeros((PAD, D_MODEL), F32), meta, x], axis=0)
    cos, sin = _rope_tables(t)

    h1, n1, g1, u1, a1, f1 = _ffn_fwd(h0, p["ffn1_pre_norm"], p["ffn1_w"], p["ffn1_post_norm"])
    n2, gq, gk, gv, gg, ga, la, sq, sk, sv = _mix_proj(h1, p["mix_pre_norm"], p["w_in"], p["gla_w_a2"], p["gla_b_a"],
                                                       cos, sin)
    ogla, ss = _gla_fwd(gq, gk, gv, la)
    oswa = _swa_fwd(p["swa_sinks"], sq, sk, sv)
    h2, cat, m = _mix_out(h1, ogla, gg, oswa, p["gla_out_norm"], p["swa_out_norm"], p["w_out"], p["mix_post_norm"])
    grads = {}
    dy, n3, g3, u3, a3, df3, grads["ffn2_post_norm"], sse = _ffn_fwd(
        h2, p["ffn2_pre_norm"], p["ffn2_w"], p["ffn2_post_norm"], target=target)

    dh2, dg3, du3, grads["ffn2_pre_norm"] = _ffn_bwd(
        dy, h2, None, g3, u3, p["ffn2_pre_norm"], p["ffn2_w"], p["ffn2_post_norm"], df=df3)
    (gud,) = _ffn_wgrad(n3, df3, dg3, du3, a3)
    grads["ffn2_w_gate"], grads["ffn2_w_up"], grads["ffn2_w_down"] = gud[:, :FJ], gud[:, FJ:2 * FJ], gud[:, 2 * FJ:]

    dogla, dgg, doswa, dm, grads["mix_post_norm"], grads["gla_out_norm"], grads["swa_out_norm"] = _mix_out_bwd(
        dh2, m, ogla, gg, oswa, p["gla_out_norm"], p["swa_out_norm"], p["w_out"], p["mix_post_norm"])
    grads["w_out"] = _xty(cat, dm)
    dsq, dsk, dsv, dkm, dvm, dsinks = _swa_bwd(p["swa_sinks"], sq, sk, sv, oswa, doswa)
    grads["swa_sinks"] = dsinks[:, 0]
    dgq, dgk, dgv, dla = _gla_bwd(gq, gk, gv, la, ss, dogla)
    dh1, dproj, grads["mix_pre_norm"], dwa2p, grads["gla_b_a"] = _mix_in_bwd(
        dh2, h1, p["mix_pre_norm"], p["w_in"], p["gla_w_a2"], p["gla_b_a"], cos, sin, ga, dgq, dgk, dgv, dgg, dla,
        dsq, dsk, dsv, dkm, dvm)
    grads["gla_w_a2"] = dwa2p[:GLA_RANK]
    grads["w_in"] = _unpack_dwin(_xty(n2, dproj))

    dh0, df1, dg1, du1, grads["ffn1_pre_norm"], grads["ffn1_post_norm"] = _ffn_bwd(
        dh1, h0, f1, g1, u1, p["ffn1_pre_norm"], p["ffn1_w"], p["ffn1_post_norm"])
    (gud,) = _ffn_wgrad(n1, df1, dg1, du1, a1)
    grads["ffn1_w_gate"], grads["ffn1_w_up"], grads["ffn1_w_down"] = gud[:, :FJ], gud[:, FJ:2 * FJ], gud[:, 2 * FJ:]
    grads["meta_tokens"] = dh0[PAD:BLK]
    return sse[0, 0], dh0[BLK:], grads


WEIGHTS = ['meta_tokens', 'ffn1_pre_norm', 'ffn1_w_gate', 'ffn1_w_up', 'ffn1_w_down', 'ffn1_post_norm',
           'mix_pre_norm', 'w_in', 'gla_w_a2', 'gla_b_a', 'gla_out_norm', 'swa_sinks', 'swa_out_norm', 'w_out',
           'mix_post_norm', 'ffn2_pre_norm', 'ffn2_w_gate', 'ffn2_w_up', 'ffn2_w_down', 'ffn2_post_norm']
BIG = ['ffn1_w_gate', 'ffn1_w_up', 'ffn1_w_down', 'w_in', 'w_out', 'ffn2_w_gate', 'ffn2_w_up', 'ffn2_w_down']
SMALL = [n for n in WEIGHTS if n not in BIG]
FJ = D_FF // N_CHIPS
D_IN_J = D_IN // N_CHIPS
D_OUT_J = D_MODEL // N_CHIPS
TRANSPOSED = ('ffn1_w_gate', 'ffn1_w_up', 'ffn2_w_gate', 'ffn2_w_up')


def _shard2d(name, a):
    return a[0].T if name in TRANSPOSED else a[0]


def _unshard2d(name, a):
    return (a.T if name in TRANSPOSED else a)[None]


def kernel(x, meta_tokens, ffn1_pre_norm, ffn1_w_gate, ffn1_w_up, ffn1_w_down, ffn1_post_norm, mix_pre_norm, w_in, gla_w_a2, gla_b_a, gla_out_norm, swa_sinks, swa_out_norm, w_out, mix_post_norm, ffn2_pre_norm, ffn2_w_gate, ffn2_w_up, ffn2_w_down, ffn2_post_norm, loss_target, m_meta_tokens, m_ffn1_pre_norm, m_ffn1_w_gate, m_ffn1_w_up, m_ffn1_w_down, m_ffn1_post_norm, m_mix_pre_norm, m_w_in, m_gla_w_a2, m_gla_b_a, m_gla_out_norm, m_swa_sinks, m_swa_out_norm, m_w_out, m_mix_post_norm, m_ffn2_pre_norm, m_ffn2_w_gate, m_ffn2_w_up, m_ffn2_w_down, m_ffn2_post_norm, v_meta_tokens, v_ffn1_pre_norm, v_ffn1_w_gate, v_ffn1_w_up, v_ffn1_w_down, v_ffn1_post_norm, v_mix_pre_norm, v_w_in, v_gla_w_a2, v_gla_b_a, v_gla_out_norm, v_swa_sinks, v_swa_out_norm, v_w_out, v_mix_post_norm, v_ffn2_pre_norm, v_ffn2_w_gate, v_ffn2_w_up, v_ffn2_w_down, v_ffn2_post_norm):
    args = dict(locals())
    w = {n: args[n] for n in WEIGHTS}
    mom = {n: args["m_" + n] for n in WEIGHTS}
    var = {n: args["v_" + n] for n in WEIGHTS}
    cx, cy, cc = lax.axis_index("x"), lax.axis_index("y"), lax.axis_index("c")
    q_idx = (2 * cx + cy).astype(jnp.int32).reshape(1)
    c_idx = cc.astype(jnp.int32).reshape(1)

    q_chip = 2 * cx + cy
    bf = {n: _own_slot(_shard2d(n, w[n]).astype(BF16), q_chip) for n in ("w_in", "w_out")}
    for ffn in ("ffn1", "ffn2"):
        bf[ffn] = _stack_own_slot([_shard2d(ffn + s, w[ffn + s]) for s in ("_w_gate", "_w_up", "_w_down")], q_idx)
    qc_idx = jnp.stack([q_chip, cc]).astype(jnp.int32)
    sinks = w["swa_sinks"].reshape(SWA_QH)

    seq, target = x[0], loss_target[0]
    t = seq.shape[0] + BLK
    h0, n1 = _embed_norm(seq, _own_slot(w["meta_tokens"], q_chip), w["ffn1_pre_norm"])
    cos, sin = _rope_tables(t)
    late = _GatherChips([bf["w_in"], bf["w_out"], bf["ffn2"],
                         _own_slot(w["gla_w_a2"].reshape(GLA_RANK, GLA_KW // N_CHIPS), q_chip)])
    (h1, g1, u1, a1, f1), (w31,), (win4, wout4, w32, wa24) = _ffn_fwd_gather(
        h0, n1, bf["ffn1"], w["ffn1_post_norm"], qc_idx, late)
    wa2p = jnp.pad(wa24.transpose(1, 0, 2).reshape(GLA_RANK, GLA_KW), ((0, 128 - GLA_RANK), (0, 0))).astype(BF16)
    winp = _pack_win(win4)
    wout = wout4.reshape(D_MODEL, D_MODEL)
    n2, gq, gk, gv, gg, ga, la, sq, sk, sv = _mix_proj(h1, w["mix_pre_norm"], winp, wa2p, w["gla_b_a"], cos, sin)
    ogla, ss = _gla_fwd(gq, gk, gv, la)
    oswa = _swa_fwd(sinks, sq, sk, sv)
    h2, cat, m = _mix_out(h1, ogla, gg, oswa, w["gla_out_norm"], w["swa_out_norm"], wout, w["mix_post_norm"])
    g = {}
    dy, n3, g3, u3, a3, df3, g["ffn2_post_norm"], sse = _ffn_fwd(
        h2, w["ffn2_pre_norm"], w32, w["ffn2_post_norm"], target=target)

    dh2, dg3, du3, g["ffn2_pre_norm"] = _ffn_bwd(
        dy, h2, None, g3, u3, w["ffn2_pre_norm"], w32, w["ffn2_post_norm"], df=df3)
    (gf2,) = _ffn_wgrad(n3, df3, dg3, du3, a3)
    (dogla, dgg, doswa, dm, g["mix_post_norm"], g["gla_out_norm"], g["swa_out_norm"]), (rgf2,) = _mix_out_bwd(
        dh2, m, ogla, gg, oswa, w["gla_out_norm"], w["swa_out_norm"], wout, w["mix_post_norm"],
        hook=_PairExchange([gf2]))
    sgf2 = _pair_sum(gf2, rgf2, c_idx)
    gout = _xty(cat, dm).reshape(N_CHIPS, D_OUT_J, D_MODEL).astype(BF16)
    (dsq, dsk, dsv, dkm, dvm, dsinks), (ogf2,) = _swa_bwd(sinks, sq, sk, sv, oswa, doswa,
                                                          hook=_ChipScatter([sgf2]))
    g["swa_sinks"] = dsinks
    dgq, dgk, dgv, dla = _gla_bwd(gq, gk, gv, la, ss, dogla)
    dh1, dproj, g["mix_pre_norm"], dwa2p, g["gla_b_a"] = _mix_in_bwd(
        dh2, h1, w["mix_pre_norm"], winp, wa2p, w["gla_b_a"], cos, sin, ga, dgq, dgk, dgv, dgg, dla,
        dsq, dsk, dsv, dkm, dvm)
    g["gla_w_a2"] = dwa2p[:GLA_RANK]
    gin = _unpack_dwin(_xty(n2, dproj))
    (dh_first, grad_x, df1, dg1, du1, g["ffn1_pre_norm"], g["ffn1_post_norm"]), (rgin, rgout) = _ffn_bwd(
        dh1, h0, f1, g1, u1, w["ffn1_pre_norm"], w31, w["ffn1_post_norm"],
        hook=_PairExchange([gin, gout]), split_first_block=True)
    sgin, sgout = _pair_sum(gin, rgin, c_idx), _pair_sum(gout, rgout, c_idx)
    g["meta_tokens"] = dh_first[PAD:BLK]
    late_small = ["gla_w_a2", "swa_sinks"]
    direct = [n for n in SMALL if n not in late_small]
    names = direct + late_small
    half_f2 = _chip_sum(sgf2, ogf2, q_idx)
    hooks = _Hooks([_ChipScatter([sgin, sgout]), _GatherDevices([g[n] for n in names] + [sse]),
                    _PairShare([half_f2])])
    own1, others1, houts = _ffn_wgrad_reduce(n1, df1, dg1, du1, a1, qc_idx, hooks)
    (ogin, ogout), gathered, (other_f2,) = hooks.split(houts)
    halves = [_chip_sum(own1[None], others1, jnp.zeros((1,), jnp.int32))]
    halves += [_chip_sum(s, o, q_idx) for s, o in ((sgin, ogin), (sgout, ogout))]
    others = list(_comm_call(_PairShare(halves), "pair_share")) + [other_f2]
    halves.append(half_f2)
    reduced = {"ffn1_w_gate": (0, 0), "ffn1_w_up": (0, FJ), "ffn1_w_down": (0, 2 * FJ), "w_in": (1, 0),
               "w_out": (2, 0), "ffn2_w_gate": (3, 0), "ffn2_w_up": (3, FJ), "ffn2_w_down": (3, 2 * FJ)}
    grad, delta, new_m, new_v = {}, {}, {}, {}
    for n in BIG:
        k, row0 = reduced[n]
        outs = _adamw_halves(_shard2d(n, w[n]), halves[k], others[k], _shard2d(n, mom[n]), _shard2d(n, var[n]),
                             c_idx, row0)
        grad[n], delta[n], new_m[n], new_v[n] = [_unshard2d(n, a) for a in outs]

    late = late_small
    mat = lambda a: a.reshape(a.shape[-2:])
    none3 = [None] * (len(late) + 1)
    outs = _small_update(q_idx, gathered, [mat(w[n]) for n in direct] + none3, [mat(mom[n]) for n in direct] + none3,
                         [mat(var[n]) for n in direct] + none3, [n == "meta_tokens" for n in names] + [False])
    sum_a2, sum_sinks, sum_sse = outs[4 * len(direct):]
    loss = sum_sse[0, 0] * (0.5 / D_MODEL)
    g_late = [lax.dynamic_slice_in_dim(sum_a2, q_chip * (GLA_KW // N_CHIPS), GLA_KW // N_CHIPS, axis=1)[None],
              sum_sinks[:, 0].reshape(1, 1, SWA_QH)]
    outs = list(outs[:4 * len(direct)]) + list(_small_update(
        q_idx, g_late, [mat(w[n]) for n in late], [mat(mom[n]) for n in late], [mat(var[n]) for n in late],
        [False, False]))
    for k, n in enumerate(names):
        grad[n], delta[n], new_m[n], new_v[n] = [a.reshape(w[n].shape) for a in outs[4 * k:4 * k + 4]]

    return (loss, grad_x[None], *[grad[n] for n in WEIGHTS], *[delta[n] for n in WEIGHTS],
            *[new_m[n] for n in WEIGHTS], *[new_v[n] for n in WEIGHTS])
```

```python
import functools
import math

import numpy as np
import jax
import jax.numpy as jnp
from jax import lax
from jax.experimental import pallas as pl
from jax.experimental.pallas import tpu as pltpu

F32 = jnp.float32
BF16 = jnp.bfloat16
MESH = pl.DeviceIdType.MESH

D_MODEL = 1024
D_FF = 2816
N_CHIPS = 4
N_DEV = 8
N_META = 16
BLK = 128
PAD = BLK - N_META
GLA_CHUNK = 64
GLA_HEADS = 4
GLA_DV = 128
GLA_DK = 64
GLA_KW = GLA_HEADS * GLA_DK
GLA_W = GLA_HEADS * GLA_DV
GLA_RANK = 16
GLA_TAU = 16.0
SWA_HD = 64
SWA_QH = 8
SWA_KVH = 2
SWA_W = SWA_QH * SWA_HD
WINDOW = 128
ROPE_THETA = 10000.0
EPS = 1e-6
NEG_INF = -1e30
IN_SPLITS = (256, 256, 512, 512, 16, 512, 128, 128)
D_IN = sum(IN_SPLITS)
P_GQ, P_GK, P_GV, P_GG, P_GA, P_SQ, P_SK, P_SV, P_END = 0, 256, 512, 1024, 1536, 1664, 2176, 2432, 2688
ADAM_LR, ADAM_B1, ADAM_B2, ADAM_EPS, ADAM_WD, ADAM_STEP = 0.001, 0.9, 0.999, 1e-08, 0.01, 10
VMEM_LIMIT = 56 * 1024 * 1024

NT = (((1,), (1,)), ((), ()))
TN = (((0,), (0,)), ((), ()))


def _cparams(n_axes):
    return pltpu.CompilerParams(dimension_semantics=("arbitrary",) * n_axes, vmem_limit_bytes=VMEM_LIMIT)


def _row_tile(t):
    for tm in (640, 512, 384, 256, 128):
        if t % tm == 0:
            return tm
    raise ValueError(t)


SEQ_BLOCKS_PER_STEP = 5


def _seq_tile(t):
    return SEQ_BLOCKS_PER_STEP * BLK if t % (SEQ_BLOCKS_PER_STEP * BLK) == 0 else BLK


ROW_PARTS = 2


def _row_parts(tm):
    n = ROW_PARTS if tm % (16 * ROW_PARTS) == 0 else 1
    return [slice(k * (tm // n), (k + 1) * (tm // n)) for k in range(n)]


def _contract_tile(t):
    return 1664 if t % 1664 == 0 else _row_tile(t)


def _div_tile(r, cap=512):
    best = None
    for tr in range(8, min(r, cap) + 1, 8):
        if r % tr == 0:
            best = tr
    return best if best is not None else r


def _dot(a, b):
    return jnp.dot(a, b, preferred_element_type=F32)


def _dg(a, b, dims):
    return lax.dot_general(a, b, dims, preferred_element_type=F32)


def _rms(x, w):
    r = lax.rsqrt(jnp.mean(x * x, axis=-1, keepdims=True) + EPS)
    xh = x * r
    return xh * w, xh, r


def _rms_bwd(xh, r, w, dy):
    wdy = dy * w
    dx = r * (wdy - xh * jnp.mean(wdy * xh, axis=-1, keepdims=True))
    dw = jnp.sum(dy * xh, axis=0, keepdims=True)
    return dx, dw


def _sigmoid(x):
    return 1.0 / (1.0 + jnp.exp(-x))


def _full(shape):
    nd = len(shape)
    return pl.BlockSpec(shape, lambda *_: (0,) * nd)


ANY = pl.BlockSpec(memory_space=pl.ANY)


def _pallas(body, *, name, grid, in_specs, out_specs, out_shape, args, scratch_shapes=(), hook=None):
    n_axes = len(grid)
    if hook is None:
        return pl.pallas_call(body, name=name, grid=grid, in_specs=list(in_specs), out_specs=list(out_specs),
                              out_shape=list(out_shape), scratch_shapes=list(scratch_shapes),
                              compiler_params=_cparams(n_axes))(*args)
    n_in, n_out, n_scr = len(in_specs), len(out_specs), len(scratch_shapes)
    h_in, h_out = len(hook.inputs), len(hook.out_shape)
    total = math.prod(grid)

    def wrapped(*refs):
        ins, hins = refs[:n_in], refs[n_in:n_in + h_in]
        o0 = n_in + h_in
        outs, houts = refs[o0:o0 + n_out], refs[o0 + n_out:o0 + n_out + h_out]
        s0 = o0 + n_out + h_out
        scr, hscr = refs[s0:s0 + n_scr], refs[s0 + n_scr:]
        step = pl.program_id(0)
        for a in range(1, n_axes):
            step = step * grid[a] + pl.program_id(a)

        @pl.when(step == 0)
        def _():
            hook.start(hins, houts, hscr)

        body(*ins, *outs, *scr)

        if hook.has_mid:
            @pl.when(step == (3 * total) // 4)
            def _():
                hook.mid(hins, houts, hscr)

        @pl.when(step == total - 1)
        def _():
            hook.finish(hins, houts, hscr)

    res = pl.pallas_call(
        wrapped, name=name, grid=grid, in_specs=list(in_specs) + [ANY] * h_in,
        out_specs=list(out_specs) + [ANY] * h_out, out_shape=list(out_shape) + list(hook.out_shape),
        scratch_shapes=list(scratch_shapes) + list(hook.scratch), compiler_params=_cparams(n_axes),
        input_output_aliases={n_in + a: n_out + b for a, b in hook.aliases},
    )(*args, *hook.inputs)
    return res[:n_out], res[n_out:]


def _ffn_weight_specs(w3):
    fj = w3.shape[1] // 3
    return fj, [pl.BlockSpec((None, fj, D_MODEL), functools.partial(lambda i, j, k: (j, k, 0), k=k)) for k in range(3)]


def _ffn_fwd(h, wpre, w3, wpost, hook=None, target=None):
    t = h.shape[0]
    tm = _row_tile(t)
    nj, rows3, _ = w3.shape
    fj = rows3 // 3
    nblk = tm // BLK if target is not None else 0

    def body(*refs):
        h_ref, wpre_ref, w_hbm, wpost_ref = refs[:4]
        t_refs = refs[4:4 + nblk]
        hout_ref, n_ref, p1_ref, p2_ref, a_ref, f_ref = refs[4 + nblk:10 + nblk]
        acc_ref, wv, wsem = refs[-3:]
        i = pl.program_id(0)
        j = pl.program_id(1)

        @pl.when((i == 0) & (j == 0))
        def _():
            for k in range(nj):
                pltpu.make_async_copy(w_hbm.at[k], wv.at[k], wsem.at[k]).start()

        @pl.when(i == 0)
        def _():
            pltpu.make_async_copy(w_hbm.at[j], wv.at[j], wsem.at[j]).wait()

        @pl.when(j == 0)
        def _():
            y, _, _ = _rms(h_ref[...], wpre_ref[...])
            n_ref[...] = y.astype(BF16)
            acc_ref[...] = jnp.zeros_like(acc_ref)

        if target is not None:
            dwpost_ref, sse_ref = refs[10 + nblk:12 + nblk]

            @pl.when((i == 0) & (j == 0))
            def _():
                dwpost_ref[...] = jnp.zeros_like(dwpost_ref)
                sse_ref[...] = jnp.zeros_like(sse_ref)

        n = n_ref[...]
        g = _dg(n, wv[j, 0:fj], NT)
        u = _dg(n, wv[j, fj:2 * fj], NT)
        sg = _sigmoid(g)
        silu = g * sg
        p1_ref[...] = (u * (sg + silu * (1.0 - sg))).astype(BF16)
        p2_ref[...] = silu.astype(BF16)
        a = (silu * u).astype(BF16)
        a_ref[...] = a
        acc_ref[...] += _dot(a, wv[j, 2 * fj:3 * fj])

        @pl.when(j == nj - 1)
        def _():
            f = acc_ref[...]
            wpost = wpost_ref[...]
            y, fh, r = _rms(f, wpost)
            hout = h_ref[...] + 0.5 * y
            if target is None:
                f_ref[...] = f
                hout_ref[...] = hout
            else:
                sse = jnp.zeros((1, 1), F32)
                errs = []
                for k in range(nblk):
                    err = hout[k * BLK:(k + 1) * BLK] - t_refs[k][...]
                    if k == 0:
                        err = jnp.where(i > 0, err, 0.0)
                    errs.append(err)
                    sse = sse + jnp.sum(jnp.sum(err * err, axis=1, keepdims=True), axis=0, keepdims=True)
                dy = (jnp.concatenate(errs, axis=0) if nblk > 1 else errs[0]) * (1.0 / D_MODEL)
                hout_ref[...] = dy
                df, dw = _rms_bwd(fh, r, wpost, 0.5 * dy)
                f_ref[...] = df.astype(BF16)
                dwpost_ref[...] += dw
                sse_ref[...] += jnp.broadcast_to(sse, sse_ref.shape)

    row = pl.BlockSpec((tm, D_MODEL), lambda i, j: (i, 0))
    vec = pl.BlockSpec((1, D_MODEL), lambda i, j: (0, 0))
    act = pl.BlockSpec((None, tm, fj), lambda i, j: (j, i, 0))
    t_specs = [pl.BlockSpec((BLK, D_MODEL), functools.partial(lambda i, j, k: (jnp.maximum(nblk * i + k - 1, 0), 0), k=k))
               for k in range(nblk)]
    loss_spec = [vec, _full((1, 128))] if target is not None else []
    loss_shape = [jax.ShapeDtypeStruct((1, D_MODEL), F32), jax.ShapeDtypeStruct((1, 128), F32)] if (
        target is not None) else []
    return _pallas(
        body, name="ffn_fwd", grid=(t // tm, nj),
        in_specs=[row, vec, ANY, vec] + t_specs,
        out_specs=[row, row, act, act, act, row] + loss_spec,
        out_shape=[jax.ShapeDtypeStruct((t, D_MODEL), F32), jax.ShapeDtypeStruct((t, D_MODEL), BF16),
                   jax.ShapeDtypeStruct((nj, t, fj), BF16), jax.ShapeDtypeStruct((nj, t, fj), BF16),
                   jax.ShapeDtypeStruct((nj, t, fj), BF16),
                   jax.ShapeDtypeStruct((t, D_MODEL), F32 if target is None else BF16)] + loss_shape,
        scratch_shapes=[pltpu.VMEM((tm, D_MODEL), F32), pltpu.VMEM((nj, rows3, D_MODEL), BF16),
                        pltpu.SemaphoreType.DMA((nj,))],
        args=(h, wpre, w3, wpost) + (target,) * nblk, hook=hook)


def _ffn_bwd(dhout, h, f, p14, p24, wpre, w3, wpost, hook=None, df=None):
    t = h.shape[0]
    tm = _row_tile(t)
    nj = w3.shape[0]
    fj, wspecs = _ffn_weight_specs(w3)
    have_df = df is not None

    def body(dhout_ref, h_ref, f_ref, p1_ref, p2_ref, wpre_ref, wg_ref, wu_ref, wd_ref, wpost_ref, *rest):
        if have_df:
            dh_ref, dg_ref, du_ref, dwpre_ref, dn_ref = rest
            df_ref = f_ref
        else:
            dh_ref, df_ref, dg_ref, du_ref, dwpre_ref, dwpost_ref, dn_ref = rest
        i = pl.program_id(0)
        j = pl.program_id(1)

        @pl.when((i == 0) & (j == 0))
        def _():
            dwpre_ref[...] = jnp.zeros_like(dwpre_ref)
            if not have_df:
                dwpost_ref[...] = jnp.zeros_like(dwpost_ref)

        @pl.when(j == 0)
        def _():
            if not have_df:
                wpost = wpost_ref[...]
                _, fh, r = _rms(f_ref[...], wpost)
                dfv, dw = _rms_bwd(fh, r, wpost, 0.5 * dhout_ref[...])
                dwpost_ref[...] += dw
                df_ref[...] = dfv.astype(BF16)
            dn_ref[...] = jnp.zeros_like(dn_ref)

        parts = _row_parts(tm)
        das = [_dg(df_ref[rows, :], wd_ref[...], NT) for rows in parts]
        for rows, da in zip(parts, das):
            dg = (da * p1_ref[rows, :].astype(F32)).astype(BF16)
            du = (da * p2_ref[rows, :].astype(F32)).astype(BF16)
            dg_ref[rows, :] = dg
            du_ref[rows, :] = du
            dn_ref[rows, :] += _dot(dg, wg_ref[...]) + _dot(du, wu_ref[...])

        @pl.when(j == nj - 1)
        def _():
            wpre = wpre_ref[...]
            _, hh, r = _rms(h_ref[...], wpre)
            dx, dw = _rms_bwd(hh, r, wpre, dn_ref[...])
            dwpre_ref[...] += dw
            dh_ref[...] = dhout_ref[...] + dx

    row = pl.BlockSpec((tm, D_MODEL), lambda i, j: (i, 0))
    vec = pl.BlockSpec((1, D_MODEL), lambda i, j: (0, 0))
    act = pl.BlockSpec((None, tm, fj), lambda i, j: (j, i, 0))
    actshape = jax.ShapeDtypeStruct((nj, t, fj), BF16)
    rowf, rowb, vecf = (jax.ShapeDtypeStruct((t, D_MODEL), F32), jax.ShapeDtypeStruct((t, D_MODEL), BF16),
                        jax.ShapeDtypeStruct((1, D_MODEL), F32))
    return _pallas(
        body, name="ffn_bwd", grid=(t // tm, nj),
        in_specs=[row, row, row, act, act, vec] + wspecs + [vec],
        out_specs=[row, act, act, vec] if have_df else [row, row, act, act, vec, vec],
        out_shape=[rowf, actshape, actshape, vecf] if have_df else [rowf, rowb, actshape, actshape, vecf, vecf],
        scratch_shapes=[pltpu.VMEM((tm, D_MODEL), F32)],
        args=(dhout, h, df if have_df else f, p14, p24, wpre, w3, w3, w3, wpost), hook=hook)


def _ffn_wgrad(n, df, dg4, du4, a4, hook=None):
    t = n.shape[0]
    tm = _contract_tile(t)
    ni = t // tm
    nj, _, fj = dg4.shape

    def body(n_ref, df_ref, dg_ref, du_ref, a_ref, dw_ref, acc):
        i = pl.program_id(1)

        @pl.when(i == 0)
        def _():
            acc[...] = jnp.zeros_like(acc)

        nn = n_ref[...]
        acc[0:fj, :] += _dg(dg_ref[...], nn, TN)
        acc[fj:2 * fj, :] += _dg(du_ref[...], nn, TN)
        acc[2 * fj:3 * fj, :] += _dg(a_ref[...], df_ref[...], TN)

        @pl.when(i == ni - 1)
        def _():
            dw_ref[...] = acc[...].astype(BF16)

    row = pl.BlockSpec((tm, D_MODEL), lambda j, i: (i, 0))
    act = pl.BlockSpec((None, tm, fj), lambda j, i: (j, i, 0))
    return _pallas(
        body, name="ffn_wgrad", grid=(nj, ni),
        in_specs=[row, row, act, act, act],
        out_specs=[pl.BlockSpec((None, 3 * fj, D_MODEL), lambda j, i: (j, 0, 0))],
        out_shape=[jax.ShapeDtypeStruct((nj, 3 * fj, D_MODEL), BF16)],
        scratch_shapes=[pltpu.VMEM((3 * fj, D_MODEL), F32)],
        args=(n, df, dg4, du4, a4), hook=hook)


def _embed_norm(x, meta_buf, w):
    t = x.shape[0] + BLK
    tm = _row_tile(t)
    nblk = tm // BLK
    ni = t // tm
    gather = _GatherChips([meta_buf])

    def body(*refs):
        x_refs = refs[:nblk]
        w_ref, mb_in, h_ref, n_ref, mb_out, mv, msem, send, recv = refs[nblk:]
        step = pl.program_id(0)
        tile = (step + 1) % ni
        hook_refs = ([mb_in], [mb_out], [send, recv])
        steps = (0, 1, ni - 2) if ni >= 3 else (0, 0, 0)
        for at, phase in zip(steps, (gather.start, gather.mid, gather.finish)):
            @pl.when(step == at)
            def _(phase=phase):
                phase(*hook_refs)

        @pl.when(step == 0)
        def _():
            mv[...] = jnp.zeros_like(mv)

        @pl.when(step == ni - 1)
        def _():
            cp = pltpu.make_async_copy(mb_out, mv, msem)
            cp.start()
            cp.wait()

        meta = jnp.concatenate([mv[k] for k in range(N_CHIPS)], axis=1)
        first = jnp.concatenate([jnp.zeros((PAD, D_MODEL), F32), meta], axis=0)
        blocks = [jnp.where(tile == 0, first, x_refs[0][...])] + [r[...] for r in x_refs[1:]]
        h = jnp.concatenate(blocks, axis=0) if nblk > 1 else blocks[0]
        h_ref[...] = h
        y, _, _ = _rms(h, w_ref[...])
        n_ref[...] = y.astype(BF16)

    x_specs = [pl.BlockSpec((BLK, D_MODEL), functools.partial(
        lambda i, k: (jnp.maximum(nblk * ((i + 1) % ni) + k - 1, 0), 0), k=k)) for k in range(nblk)]
    row = pl.BlockSpec((tm, D_MODEL), lambda i: ((i + 1) % ni, 0))
    h0, n0, _ = pl.pallas_call(
        body, name="embed_norm", grid=(ni,),
        in_specs=x_specs + [_full((1, D_MODEL)), ANY], out_specs=[row, row, ANY],
        out_shape=[jax.ShapeDtypeStruct((t, D_MODEL), F32), jax.ShapeDtypeStruct((t, D_MODEL), BF16),
                   jax.ShapeDtypeStruct(meta_buf.shape, meta_buf.dtype)],
        scratch_shapes=[pltpu.VMEM(meta_buf.shape, meta_buf.dtype), pltpu.SemaphoreType.DMA] + list(gather.scratch),
        input_output_aliases={nblk + 1: 2},
        compiler_params=_cparams(1),
    )(*([x] * nblk), w, meta_buf)
    return h0, n0


FWD_RELATION = (None, 0, 1, 2)


def _ffn_fwd_gather(h, n, wbuf, wpost, qc_idx, late):
    t = h.shape[0]
    tm = _row_tile(t)
    ni = t // tm
    nj, rows3, _ = wbuf.shape
    fj = rows3 // 3
    assert nj == N_CHIPS and ni >= 4
    wbufs = [wbuf]
    nw = 1
    n_lin, n_lout = len(late.inputs), len(late.out_shape)
    wait_step = ni - 3

    def body(qc_ref, h_ref, n_ref, wpost_ref, *rest):
        wb_in = rest[:nw]
        lins = rest[nw:nw + n_lin]
        o0 = nw + n_lin
        hout_ref, p1_ref, p2_ref, a_ref, f_hbm = rest[o0:o0 + 5]
        wb = rest[o0 + 5:o0 + 5 + nw]
        louts = rest[o0 + 5 + nw:o0 + 5 + nw + n_lout]
        s0 = o0 + 5 + nw + n_lout
        wv, wsem, send, recv, fbuf, fr_sem, fw_sem = rest[s0:s0 + 7]
        lscr = rest[s0 + 7:]
        p = pl.program_id(0)
        i = pl.program_id(1)
        step = p * ni + i
        fslot = step % 3
        nslot = (step + 1) % 3

        def f_tile(tile):
            return f_hbm.at[pl.ds(pl.multiple_of(tile * tm, 8), tm)]

        @pl.when(step > 1)
        def _():
            pltpu.make_async_copy(fbuf.at[nslot], f_tile(i), fw_sem.at[nslot]).wait()

        nxt = step + 1

        @pl.when((nxt < N_CHIPS * ni) & (nxt >= ni))
        def _():
            pltpu.make_async_copy(f_tile(nxt % ni), fbuf.at[nslot], fr_sem.at[nslot]).start()

        @pl.when(p > 0)
        def _():
            pltpu.make_async_copy(f_tile(i), fbuf.at[fslot], fr_sem.at[fslot]).wait()
        x, y, c, chips = _place()
        q = 2 * x + y
        sibling = (x, y, 1 - c)
        mine, other = _half(rows3, c), _half(rows3, 1 - c)

        def load(chunk, slot, src):
            return [pltpu.make_async_copy(src[t].at[chunk], wv.at[slot, t], wsem.at[slot, t]) for t in range(nw)]

        @pl.when((p == 0) & (i == 0))
        def _():
            for j, (cx, cy) in enumerate(chips):
                for t in range(nw):
                    _remote(send.at[t, j], recv.at[t, j], wb_in[t].at[q, mine], wb[t].at[q, mine], (cx, cy, c)).start()
            for cp in load(q, 0, wb_in):
                cp.start()
            for cp in load(q, 0, wb_in):
                cp.wait()

        @pl.when((p == 1) & (i == 0))
        def _():
            late.start(lins, louts, lscr)

        for pp in range(1, N_CHIPS):
            j = FWD_RELATION[pp]
            cx, cy = chips[j]
            chunk = 2 * cx + cy

            @pl.when((p == pp - 1) & (i == wait_step))
            def _(j=j, cx=cx, cy=cy, chunk=chunk, pp=pp):
                for t in range(nw):
                    got = wb[t].at[chunk, mine]
                    _remote(send.at[t, j], recv.at[t, j], got, got, (cx, cy, c)).wait_recv()
                    _remote(send.at[t, 3 + j], recv.at[t, 3 + j], got, got, sibling).start()
                for t in range(nw):
                    rest_half = wb[t].at[chunk, other]
                    _remote(send.at[t, 3 + j], recv.at[t, 3 + j], rest_half, rest_half, sibling).wait_recv()
                for cp in load(chunk, pp % 2, wb):
                    cp.start()

            @pl.when((p == pp) & (i == 0))
            def _(chunk=chunk, pp=pp):
                for cp in load(chunk, pp % 2, wb):
                    cp.wait()

        @pl.when((p == N_CHIPS - 1) & (i == ni // 2))
        def _():
            late.mid(lins, louts, lscr)

        slot = p % 2
        nn = n_ref[...]
        g = _dg(nn, wv[slot, 0, 0:fj], NT)
        u = _dg(nn, wv[slot, 0, fj:2 * fj], NT)
        sg = _sigmoid(g)
        silu = g * sg
        p1_ref[...] = (u * (sg + silu * (1.0 - sg))).astype(BF16)
        p2_ref[...] = silu.astype(BF16)
        a = (silu * u).astype(BF16)
        a_ref[...] = a
        part = _dot(a, wv[slot, 0, 2 * fj:3 * fj])

        @pl.when(p == 0)
        def _():
            fbuf[fslot] = part

        @pl.when(p > 0)
        def _():
            fbuf[fslot] = fbuf[fslot] + part

        pltpu.make_async_copy(fbuf.at[fslot], f_tile(i), fw_sem.at[fslot]).start()

        @pl.when(p == N_CHIPS - 1)
        def _():
            yv, _, _ = _rms(fbuf[fslot], wpost_ref[...])
            hout_ref[...] = h_ref[...] + 0.5 * yv

        @pl.when((p == N_CHIPS - 1) & (i == ni - 1))
        def _():
            pslot = (step + 2) % 3
            pltpu.make_async_copy(fbuf.at[pslot], f_tile(i), fw_sem.at[pslot]).wait()
            pltpu.make_async_copy(fbuf.at[fslot], f_tile(i), fw_sem.at[fslot]).wait()
            for t in range(nw):
                for j, (cx, cy) in enumerate(chips):
                    sent = wb[t].at[2 * cx + cy, mine]
                    _remote(send.at[t, j], recv.at[t, j], sent, sent, (cx, cy, c)).wait_send()
                    _remote(send.at[t, 3 + j], recv.at[t, 3 + j], sent, sent, sibling).wait_send()
            late.finish(lins, louts, lscr)

    def last_pass_rows(p, i, qc_ref):
        return (jnp.where(p == N_CHIPS - 1, i, 0), 0)

    def chunk_rows(p, i, qc_ref):
        order = ((p & 1) << 1) | (p >> 1)
        return (jnp.bitwise_xor(qc_ref[0], order), i, 0)

    row = pl.BlockSpec((tm, D_MODEL), lambda p, i, qc_ref: (i, 0))
    last_row = pl.BlockSpec((tm, D_MODEL), last_pass_rows)
    act = pl.BlockSpec((None, tm, fj), chunk_rows)
    act_shape = jax.ShapeDtypeStruct((nj, t, fj), BF16)
    res = pl.pallas_call(
        body, name="ffn_fwd_gather",
        grid_spec=pltpu.PrefetchScalarGridSpec(
            num_scalar_prefetch=1, grid=(N_CHIPS, ni),
            in_specs=[last_row, row, pl.BlockSpec((1, D_MODEL), lambda p, i, qc_ref: (0, 0))]
            + [ANY] * (nw + n_lin),
            out_specs=[last_row, act, act, act, ANY] + [ANY] * (nw + n_lout),
            scratch_shapes=[pltpu.VMEM((2, nw, rows3, D_MODEL), BF16), pltpu.SemaphoreType.DMA((2, nw)),
                            pltpu.SemaphoreType.DMA((nw, 6)), pltpu.SemaphoreType.DMA((nw, 6)),
                            pltpu.VMEM((3, tm, D_MODEL), F32), pltpu.SemaphoreType.DMA((3,)),
                            pltpu.SemaphoreType.DMA((3,))] + list(late.scratch)),
        out_shape=[jax.ShapeDtypeStruct((t, D_MODEL), F32), act_shape, act_shape, act_shape,
                   jax.ShapeDtypeStruct((t, D_MODEL), F32)]
        + [jax.ShapeDtypeStruct(b.shape, b.dtype) for b in wbufs] + list(late.out_shape),
        input_output_aliases={**{4 + t: 5 + t for t in range(nw)},
                              **{4 + nw + a: 5 + nw + b for a, b in late.aliases}},
        compiler_params=_cparams(2),
    )(qc_idx, h, n, wpost, *wbufs, *late.inputs)
    return res[:5], res[5:5 + nw], res[5 + nw:]


PASS_RELATION = (2, 0, 1)


def _ffn_wgrad_reduce(n, df, dg4, du4, a4, qc_idx, hook):
    t = n.shape[0]
    tm = _contract_tile(t)
    ni = t // tm
    nj, _, fj = dg4.shape
    assert nj == N_CHIPS
    hrows = 3 * fj // 2
    n_hin, n_hout = len(hook.inputs), len(hook.out_shape)

    def body(qc_ref, n_ref, df_ref, dg_ref, du_ref, a_ref, *rest):
        hins = rest[:n_hin]
        own_ref, others_ref = rest[n_hin:n_hin + 2]
        houts = rest[n_hin + 2:n_hin + 2 + n_hout]
        s0 = n_hin + 2 + n_hout
        acc, stage, land, sumbuf, px_send, px_recv, cs_send, cs_recv, own_sem = rest[s0:s0 + 9]
        hscr = rest[s0 + 9:]
        k_pass = pl.program_id(0)
        i = pl.program_id(1)
        x, y, c, chips = _place()
        mine = pl.ds(pl.multiple_of(c * hrows, 8), hrows)
        other = pl.ds(pl.multiple_of((1 - c) * hrows, 8), hrows)

        def to_owner(k):
            j = PASS_RELATION[k]
            return _remote(cs_send.at[j], cs_recv.at[j], sumbuf.at[k % 2], others_ref.at[j], (*chips[j], c))

        @pl.when((k_pass == 0) & (i == 0))
        def _():
            hook.start(hins, houts, hscr)

        if hook.has_mid:
            @pl.when((k_pass == N_CHIPS // 2) & (i == 0))
            def _():
                hook.mid(hins, houts, hscr)

        @pl.when(i == 0)
        def _():
            acc[...] = jnp.zeros_like(acc)

        nn = n_ref[...]
        acc[0:fj, :] += _dg(dg_ref[...], nn, TN)
        acc[fj:2 * fj, :] += _dg(du_ref[...], nn, TN)
        acc[2 * fj:3 * fj, :] += _dg(a_ref[...], df_ref[...], TN)

        for k in range(N_CHIPS):
            @pl.when((k_pass == k) & (i == ni - 1))
            def _(k=k):
                slot = k % 2
                stage[...] = acc[other, :].astype(BF16)
                swap = _remote(px_send.at[k], px_recv.at[k], stage, land.at[slot], (x, y, 1 - c))
                swap.start()
                swap.wait_recv()
                pair = acc[mine, :] + land[slot].astype(F32)
                if k >= 2:
                    to_owner(k - 2).wait_send()
                sumbuf[slot] = pair.astype(BF16)
                swap.wait_send()
                if k < N_CHIPS - 1:
                    to_owner(k).start()
                else:
                    keep = pltpu.make_async_copy(sumbuf.at[slot], own_ref, own_sem)
                    keep.start()
                    for j in range(N_CHIPS - 1):
                        _remote(cs_send.at[j], cs_recv.at[j], sumbuf.at[0], others_ref.at[j], (*chips[j], c)).wait_recv()
                    to_owner(k - 1).wait_send()
                    keep.wait()
                    hook.finish(hins, houts, hscr)

    def chunk(k_pass, i, qc_ref):
        return (jnp.bitwise_xor(qc_ref[0], N_CHIPS - 1 - k_pass), i, 0)

    row = pl.BlockSpec((tm, D_MODEL), lambda k_pass, i, qc_ref: (i, 0))
    act = pl.BlockSpec((None, tm, fj), chunk)
    res = pl.pallas_call(
        body, name="ffn_wgrad_reduce",
        grid_spec=pltpu.PrefetchScalarGridSpec(
            num_scalar_prefetch=1, grid=(N_CHIPS, ni),
            in_specs=[row, row, act, act, act] + [ANY] * n_hin,
            out_specs=[ANY, ANY] + [ANY] * n_hout,
            scratch_shapes=[pltpu.VMEM((3 * fj, D_MODEL), F32), pltpu.VMEM((hrows, D_MODEL), BF16),
                            pltpu.VMEM((2, hrows, D_MODEL), BF16), pltpu.VMEM((2, hrows, D_MODEL), BF16),
                            pltpu.SemaphoreType.DMA((N_CHIPS,)), pltpu.SemaphoreType.DMA((N_CHIPS,)),
                            pltpu.SemaphoreType.DMA((N_CHIPS - 1,)), pltpu.SemaphoreType.DMA((N_CHIPS - 1,)),
                            pltpu.SemaphoreType.DMA] + list(hook.scratch)),
        out_shape=[jax.ShapeDtypeStruct((hrows, D_MODEL), BF16),
                   jax.ShapeDtypeStruct((N_CHIPS - 1, hrows, D_MODEL), BF16)] + list(hook.out_shape),
        compiler_params=_cparams(2),
    )(qc_idx, n, df, dg4, du4, a4, *hook.inputs)
    return res[0], res[1], res[2:]


def _xty(x, y):
    t, k = x.shape
    n = y.shape[1]
    tm = _contract_tile(t)
    tn = n if n <= 1024 else (896 if n % 896 == 0 else 128)
    steps = t // tm

    def body(x_ref, y_ref, o_ref, acc_ref):
        i = pl.program_id(1)
        part = _dg(x_ref[...], y_ref[...], TN)

        @pl.when(i == 0)
        def _():
            acc_ref[...] = part

        @pl.when(jnp.logical_and(i > 0, i < steps - 1))
        def _():
            acc_ref[...] += part

        @pl.when(i == steps - 1)
        def _():
            o_ref[...] = (acc_ref[...] + part).astype(BF16)

    assert steps > 1
    return pl.pallas_call(
        body, name="xty", grid=(n // tn, steps),
        in_specs=[pl.BlockSpec((tm, k), lambda j, i: (i, 0)), pl.BlockSpec((tm, tn), lambda j, i: (i, j))],
        out_specs=pl.BlockSpec((k, tn), lambda j, i: (0, j)),
        out_shape=jax.ShapeDtypeStruct((k, n), BF16),
        scratch_shapes=[pltpu.VMEM((k, tn), F32)],
        compiler_params=_cparams(2),
    )(x, y)


def _rope_tables(t):
    pos = (jnp.arange(t, dtype=jnp.int32) - PAD).astype(F32)
    inv_freq = 1.0 / (ROPE_THETA ** (jnp.arange(0, SWA_HD, 2, dtype=F32) / SWA_HD))
    half = SWA_HD // 2
    ang = pos[:, None] * jnp.tile(inv_freq, 4)[None, :]
    sign = jnp.tile(jnp.concatenate([-jnp.ones((half,), F32), jnp.ones((half,), F32)]), 2)
    return jnp.cos(ang), jnp.sin(ang) * sign[None, :]


def _rot_half(x, first_half):
    return jnp.where(first_half, pltpu.roll(x, 96, 1), pltpu.roll(x, 32, 1))


def _first_half_mask(rows):
    lane = lax.broadcasted_iota(jnp.int32, (rows, 128), 1)
    return (lane % 64) < 32


def _log_sigmoid(z):
    return jnp.minimum(z, 0.0) - jnp.log(1.0 + jnp.exp(-jnp.abs(z)))


def _mix_proj(h1, wmixpre, winp, wa2p, bap, cos, sin):
    t = h1.shape[0]
    tm = _row_tile(t)

    def body(h_ref, w_ref, win_ref, wa2_ref, ba_ref, cos_ref, sin_ref,
             n_ref, gq_ref, gk_ref, gv_ref, gg_ref, ga_ref, la_ref, sq_ref, sk_ref, sv_ref):
        y, _, _ = _rms(h_ref[...], w_ref[...])
        n = y.astype(BF16)
        n_ref[...] = n
        proj = _dot(n, win_ref[...])
        gq_ref[...] = proj[:, P_GQ:P_GK]
        gk_ref[...] = proj[:, P_GK:P_GV]
        gv_ref[...] = proj[:, P_GV:P_GG]
        gg_ref[...] = proj[:, P_GG:P_GA]
        ga = proj[:, P_GA:P_SQ]
        ga_ref[...] = ga
        z = _dot(ga.astype(BF16), wa2_ref[...]) + ba_ref[...]
        la_ref[...] = _log_sigmoid(z) * (1.0 / GLA_TAU)
        c = cos_ref[...]
        s = sin_ref[...]
        fh = _first_half_mask(tm)
        for k in range(4):
            x = proj[:, P_SQ + 128 * k:P_SQ + 128 * (k + 1)]
            sq_ref[:, 128 * k:128 * (k + 1)] = (x * c + _rot_half(x, fh) * s).astype(BF16)
        for k in range(2):
            x = proj[:, P_SK + 128 * k:P_SK + 128 * (k + 1)]
            sk_ref[:, 128 * k:128 * (k + 1)] = (x * c + _rot_half(x, fh) * s).astype(BF16)
        sv_ref[...] = proj[:, P_SV:P_END].astype(BF16)

    def row(w):
        return pl.BlockSpec((tm, w), lambda i: (i, 0))

    def rshape(w, dt):
        return jax.ShapeDtypeStruct((t, w), dt)

    return pl.pallas_call(
        body, name="mix_proj", grid=(t // tm,),
        in_specs=[row(D_MODEL), _full((1, D_MODEL)), _full((D_MODEL, P_END)), _full((128, GLA_KW)),
                  _full((1, GLA_KW)), row(128), row(128)],
        out_specs=[row(D_MODEL), row(256), row(256), row(512), row(512), row(128), row(256), row(512), row(256),
                   row(256)],
        out_shape=[rshape(D_MODEL, BF16), rshape(256, F32), rshape(256, F32), rshape(512, F32), rshape(512, F32),
                   rshape(128, F32), rshape(256, F32), rshape(512, BF16), rshape(256, BF16), rshape(256, BF16)],
        compiler_params=_cparams(1),
    )(h1, wmixpre, winp, wa2p, bap, cos, sin)


def _scan_rows(x, reverse=False):
    n = x.shape[0]
    row = lax.broadcasted_iota(jnp.int32, x.shape, 0)
    s = 1
    while s < n:
        if reverse:
            x = x + jnp.where(row < n - s, pltpu.roll(x, n - s, 0), 0.0)
        else:
            x = x + jnp.where(row >= s, pltpu.roll(x, s, 0), 0.0)
        s *= 2
    return x


def _gla_cumsum(la, tril_f):
    b = _scan_rows(la)
    row = lax.broadcasted_iota(jnp.int32, b.shape, 0)
    bm = jnp.sum(jnp.where(row == GLA_CHUNK // 2 - 1, b, 0.0), axis=0, keepdims=True)
    bl = jnp.sum(jnp.where(row == GLA_CHUNK - 1, b, 0.0), axis=0, keepdims=True)
    return b, bm, bl


def _gla_decays(la, tril_f):
    b, bm, bl = _gla_cumsum(la, tril_f)
    return jnp.exp(b - bm), jnp.exp(bm - b), jnp.exp(b), jnp.exp(bl - b), jnp.exp(bl)


def _gla_masks():
    c = GLA_CHUNK
    r = lax.broadcasted_iota(jnp.int32, (c, c), 0)
    col = lax.broadcasted_iota(jnp.int32, (c, c), 1)
    r4 = lax.broadcasted_iota(jnp.int32, (GLA_HEADS * c, c), 0) % c
    c4 = lax.broadcasted_iota(jnp.int32, (GLA_HEADS * c, c), 1)
    klane = lax.broadcasted_iota(jnp.int32, (c, GLA_KW), 1) // GLA_DK
    vlane = lax.broadcasted_iota(jnp.int32, (c, GLA_W), 1) // GLA_DV
    srow = lax.broadcasted_iota(jnp.int32, (GLA_W, GLA_KW), 0) // GLA_DV
    scol = lax.broadcasted_iota(jnp.int32, (GLA_W, GLA_KW), 1) // GLA_DK
    return dict(tril_f=(r >= col).astype(F32), triu_f=(r <= col).astype(F32), tril4=r4 >= c4,
                khead=[klane == h for h in range(GLA_HEADS)], vhead=[vlane == h for h in range(GLA_HEADS)],
                diag=srow == scol)


def _stack_heads(x, head_masks):
    return jnp.concatenate([jnp.where(m, x, 0.0) for m in head_masks], axis=0)


def _gla_fwd(gq, gk, gv, la):
    t = gq.shape[0]
    rg = _seq_tile(t)
    nb = t // rg
    ncb = rg // GLA_CHUNK
    c = GLA_CHUNK

    def body(q_ref, k_ref, v_ref, la_ref, o_ref, ss_ref, st_ref):
        @pl.when(pl.program_id(0) == 0)
        def _():
            st_ref[...] = jnp.zeros_like(st_ref)

        mk = _gla_masks()
        st = st_ref[...]
        for ch in range(ncb):
            rows = slice(ch * c, (ch + 1) * c)
            eq, ek, eb, ekl, ebl = _gla_decays(la_ref[rows, :], mk["tril_f"])
            qs = q_ref[rows, :] * (GLA_DK ** -0.5)
            k = k_ref[rows, :]
            v = v_ref[rows, :].astype(BF16)
            ss_ref[ch] = st
            q4 = _stack_heads(qs * eq, mk["khead"]).astype(BF16)
            a4 = jnp.where(mk["tril4"], _dg(q4, (k * ek).astype(BF16), NT), 0.0).astype(BF16)
            r4 = _dot(a4, v)
            intra = jnp.concatenate([r4[h * c:(h + 1) * c, GLA_DV * h:GLA_DV * (h + 1)] for h in range(GLA_HEADS)],
                                    axis=1)
            o_ref[rows, :] = intra + _dg((qs * eb).astype(BF16), st.astype(BF16), NT)
            st = st * ebl + jnp.where(mk["diag"], _dg(v, (k * ekl).astype(BF16), TN), 0.0)
        st_ref[...] = st

    def row(w):
        return pl.BlockSpec((rg, w), lambda i: (i, 0))

    return pl.pallas_call(
        body, name="gla_fwd", grid=(nb,),
        in_specs=[row(256), row(256), row(512), row(256)],
        out_specs=[row(512), pl.BlockSpec((ncb, GLA_W, GLA_KW), lambda i: (i, 0, 0))],
        out_shape=[jax.ShapeDtypeStruct((t, GLA_W), F32), jax.ShapeDtypeStruct((nb * ncb, GLA_W, GLA_KW), F32)],
        scratch_shapes=[pltpu.VMEM((GLA_W, GLA_KW), F32)],
        compiler_params=_cparams(1),
    )(gq, gk, gv, la)


def _gla_bwd(gq, gk, gv, la, ss, do):
    t = gq.shape[0]
    rg = _seq_tile(t)
    nb = t // rg
    ncb = rg // GLA_CHUNK
    c = GLA_CHUNK

    def body(q_ref, k_ref, v_ref, la_ref, ss_ref, do_ref, dq_ref, dk_ref, dv_ref, dla_ref, dst_ref):
        @pl.when(pl.program_id(0) == 0)
        def _():
            dst_ref[...] = jnp.zeros_like(dst_ref)

        mk = _gla_masks()
        last_row = lax.broadcasted_iota(jnp.int32, (c, GLA_KW), 0) == c - 1
        scale = GLA_DK ** -0.5
        dstn = dst_ref[...]
        for ch in reversed(range(ncb)):
            rows = slice(ch * c, (ch + 1) * c)
            eq, ek, eb, ekl, ebl = _gla_decays(la_ref[rows, :], mk["tril_f"])
            qs = q_ref[rows, :] * scale
            k = k_ref[rows, :]
            qt, kt, qh, kh = qs * eq, k * ek, qs * eb, k * ekl
            ktb, khb, qhb = kt.astype(BF16), kh.astype(BF16), qh.astype(BF16)
            v = v_ref[rows, :].astype(BF16)
            do_f = do_ref[rows, :]
            dob = do_f.astype(BF16)
            st = ss_ref[ch]
            stb = st.astype(BF16)
            dstb = dstn.astype(BF16)
            q4 = _stack_heads(qt, mk["khead"]).astype(BF16)
            do4 = _stack_heads(do_f, mk["vhead"]).astype(BF16)
            a4 = jnp.where(mk["tril4"], _dg(q4, ktb, NT), 0.0).astype(BF16)
            da4 = jnp.where(mk["tril4"], _dg(do4, v, NT), 0.0).astype(BF16)
            dv_ref[rows, :] = _dg(a4, do4, TN) + _dg(khb, dstb, NT)
            dq4 = _dot(da4, ktb)
            dqt = jnp.zeros((c, GLA_KW), F32)
            for h in range(GLA_HEADS):
                dqt = dqt + jnp.where(mk["khead"][h], dq4[h * c:(h + 1) * c], 0.0)
            dkt = _dg(da4, q4, TN)
            dqh = _dot(dob, stb)
            dkh = _dot(v, dstb)
            dbl = jnp.sum(dstn * st, axis=0, keepdims=True)
            dstn = dstn * ebl + jnp.where(mk["diag"], _dg(dob, qhb, TN), 0.0)
            dq_ref[rows, :] = scale * (dqt * eq + dqh * eb)
            dk_ref[rows, :] = dkt * ek + dkh * ekl
            dkk = dkh * kh
            db = dqt * qt - dkt * kt + dqh * qh - dkk
            db = db + jnp.where(last_row, jnp.sum(dkk, axis=0, keepdims=True) + ebl * dbl, 0.0)
            dla_ref[rows, :] = _scan_rows(db, reverse=True)
        dst_ref[...] = dstn

    def row(w):
        return pl.BlockSpec((rg, w), lambda i: (nb - 1 - i, 0))

    def rshape(w):
        return jax.ShapeDtypeStruct((t, w), F32)

    return pl.pallas_call(
        body, name="gla_bwd", grid=(nb,),
        in_specs=[row(256), row(256), row(512), row(256),
                  pl.BlockSpec((ncb, GLA_W, GLA_KW), lambda i: (nb - 1 - i, 0, 0)), row(512)],
        out_specs=[row(256), row(256), row(512), row(256)],
        out_shape=[rshape(256), rshape(256), rshape(512), rshape(256)],
        scratch_shapes=[pltpu.VMEM((GLA_W, GLA_KW), F32)],
        compiler_params=_cparams(1),
    )(gq, gk, gv, la, ss, do)


SWA_G = SWA_QH // SWA_KVH


def _swa_bias():
    n = jnp.arange(3, dtype=jnp.int32)[:, None, None]
    r = (jnp.arange(SWA_G * BLK, dtype=jnp.int32) % BLK)[None, :, None]
    c = jnp.arange(3 * BLK, dtype=jnp.int32)[None, None, :]
    seg = c // BLK
    cc = c % BLK
    qpos = n * BLK + r - PAD
    kpos = jnp.where(seg == 0, (n - 1) * BLK, jnp.where(seg == 1, n * BLK, 0)) + cc - PAD
    band = (seg < 2) & (kpos >= N_META) & (kpos <= qpos) & (qpos - kpos < WINDOW)
    meta = (seg == 2) & (kpos >= 0) & (kpos < N_META) & (kpos <= qpos)
    return jnp.where(band | meta, 0.0, NEG_INF).astype(F32)


def _swa_stack(ref, rows, kh, lo, dtype):
    parts = []
    for g in range(2):
        pair = ref[rows, 128 * (2 * kh + g):128 * (2 * kh + g + 1)]
        zero = jnp.zeros_like(pair)
        parts += [jnp.where(lo, pair, zero), jnp.where(lo, zero, pair)]
    return jnp.concatenate(parts, axis=0).astype(dtype)


def _swa_unstack(x4, lo):
    return [jnp.where(lo, x4[2 * g * BLK:(2 * g + 1) * BLK], x4[(2 * g + 1) * BLK:(2 * g + 2) * BLK])
            for g in range(2)]


def _swa_sink_col(sink_ref, kh):
    blk = lax.broadcasted_iota(jnp.int32, (SWA_G * BLK, 1), 0) // BLK
    col = jnp.full((SWA_G * BLK, 1), sink_ref[SWA_G * kh + SWA_G - 1], F32)
    for e in reversed(range(SWA_G - 1)):
        col = jnp.where(blk == e, sink_ref[SWA_G * kh + e], col)
    return col


def _swa_softmax(qk, bias, sink):
    s = qk * (SWA_HD ** -0.5) + bias
    m = jnp.maximum(jnp.max(s, axis=-1, keepdims=True), sink)
    p = jnp.exp(s - m)
    es = jnp.exp(sink - m)
    inv = 1.0 / (jnp.sum(p, axis=-1, keepdims=True) + es)
    return p * inv, es * inv


def _swa_keys(prev_ref, cur_ref, first_ref, b, ls):
    before = prev_ref[:, ls] if b == 0 else cur_ref[(b - 1) * BLK:b * BLK, ls]
    return jnp.concatenate([before, cur_ref[b * BLK:(b + 1) * BLK, ls], first_ref[:, ls]], axis=0)


def _swa_specs(rs, ns):
    bps = rs // BLK
    cur = lambda w: pl.BlockSpec((rs, w), lambda i: (jnp.minimum(i, ns - 1), 0))
    prev = lambda w: pl.BlockSpec((BLK, w), lambda i: (jnp.maximum(jnp.minimum(i, ns - 1) * bps - 1, 0), 0))
    first = lambda w: pl.BlockSpec((BLK, w), lambda i: (0, 0))
    return cur, prev, first


def _swa_fwd(sinks, sq, sk, sv):
    t = sq.shape[0]
    rs = _seq_tile(t)
    bps, ns = rs // BLK, t // rs

    def body(sink_ref, bias_ref, q_ref, kp_ref, kc_ref, km_ref, vp_ref, vc_ref, vm_ref, o_ref):
        i = pl.program_id(0)
        lo = lax.broadcasted_iota(jnp.int32, (BLK, 128), 1) < 64
        sink_cols = [_swa_sink_col(sink_ref, kh) for kh in range(SWA_KVH)]
        chains = [(b, kh) for b in range(bps) for kh in range(SWA_KVH)]
        scores = []
        for b, kh in chains:
            ls = slice(128 * kh, 128 * (kh + 1))
            q4 = _swa_stack(q_ref, slice(b * BLK, (b + 1) * BLK), kh, lo, BF16)
            scores.append(_dg(q4, _swa_keys(kp_ref, kc_ref, km_ref, b, ls), NT))
        probs = []
        for (b, kh), s in zip(chains, scores):
            p, _ = _swa_softmax(s, bias_ref[jnp.minimum(i * bps + b, 2)], sink_cols[kh])
            probs.append(p.astype(BF16))
        for (b, kh), p in zip(chains, probs):
            ls = slice(128 * kh, 128 * (kh + 1))
            rows = slice(b * BLK, (b + 1) * BLK)
            for g, pair in enumerate(_swa_unstack(_dot(p, _swa_keys(vp_ref, vc_ref, vm_ref, b, ls)), lo)):
                o_ref[rows, 128 * (2 * kh + g):128 * (2 * kh + g + 1)] = pair

    cur, prev, first = _swa_specs(rs, ns)
    bias = _swa_bias()
    return pl.pallas_call(
        body, name="swa_fwd", grid=(ns,),
        in_specs=[pl.BlockSpec(memory_space=pltpu.SMEM), _full(bias.shape), cur(512), prev(256), cur(256), first(256),
                  prev(256), cur(256), first(256)],
        out_specs=cur(512),
        out_shape=jax.ShapeDtypeStruct((t, SWA_W), F32),
        compiler_params=_cparams(1),
    )(sinks, bias, sq, sk, sk, sk, sv, sv, sv)


def _swa_bwd(sinks, sq, sk, sv, o, do, hook=None):
    t = sq.shape[0]
    rs = _seq_tile(t)
    bps, ns = rs // BLK, t // rs

    def body(sink_ref, bias_ref, q_ref, kp_ref, kc_ref, km_ref, vp_ref, vc_ref, vm_ref, o_ref, do_ref,
             dq_ref, dk_ref, dv_ref, dkm_ref, dvm_ref, dsink_ref, pk_ref, pv_ref):
        i = pl.program_id(0)

        @pl.when(i == 0)
        def _():
            pk_ref[...] = jnp.zeros_like(pk_ref)
            pv_ref[...] = jnp.zeros_like(pv_ref)
            dkm_ref[...] = jnp.zeros_like(dkm_ref)
            dvm_ref[...] = jnp.zeros_like(dvm_ref)
            dsink_ref[...] = jnp.zeros_like(dsink_ref)

        @pl.when(i == ns)
        def _():
            dk_ref[...] = pk_ref[...]
            dv_ref[...] = pv_ref[...]

        @pl.when(i < ns)
        def _():
            lo = lax.broadcasted_iota(jnp.int32, (BLK, 128), 1) < 64
            scale = SWA_HD ** -0.5
            sink_cols = [_swa_sink_col(sink_ref, kh) for kh in range(SWA_KVH)]
            parts_k = [[None] * SWA_KVH for _ in range(bps)]
            parts_v = [[None] * SWA_KVH for _ in range(bps)]
            dsinks = [jnp.zeros((1, 1), F32) for _ in range(SWA_QH)]
            chains = [(b, kh) for b in range(bps) for kh in range(SWA_KVH)]
            lanes = lambda kh: slice(128 * kh, 128 * (kh + 1))
            block = lambda b: slice(b * BLK, (b + 1) * BLK)
            q4s = [_swa_stack(q_ref, block(b), kh, lo, BF16) for b, kh in chains]
            scores = [_dg(q4, _swa_keys(kp_ref, kc_ref, km_ref, b, lanes(kh)), NT)
                      for (b, kh), q4 in zip(chains, q4s)]
            do4s = [_swa_stack(do_ref, block(b), kh, lo, F32) for b, kh in chains]
            do4bs = [d.astype(BF16) for d in do4s]
            dps = [_dg(d, _swa_keys(vp_ref, vc_ref, vm_ref, b, lanes(kh)), NT) for (b, kh), d in zip(chains, do4bs)]
            pbs, dss = [], []
            for n_chain, (b, kh) in enumerate(chains):
                p, psink = _swa_softmax(scores[n_chain], bias_ref[jnp.minimum(i * bps + b, 2)], sink_cols[kh])
                delta = jnp.sum(do4s[n_chain] * _swa_stack(o_ref, block(b), kh, lo, F32), axis=-1, keepdims=True)
                dss.append((p * (dps[n_chain] - delta) * scale).astype(BF16))
                pbs.append(p.astype(BF16))
                dsk = psink * delta
                for e in range(SWA_G):
                    h = SWA_G * kh + e
                    dsinks[h] = dsinks[h] - jnp.sum(dsk[e * BLK:(e + 1) * BLK], axis=0, keepdims=True)
            for n_chain, (b, kh) in enumerate(chains):
                kall = _swa_keys(kp_ref, kc_ref, km_ref, b, lanes(kh))
                for g, pair in enumerate(_swa_unstack(_dot(dss[n_chain], kall), lo)):
                    dq_ref[block(b), 128 * (2 * kh + g):128 * (2 * kh + g + 1)] = pair
                parts_k[b][kh] = _dg(dss[n_chain], q4s[n_chain], TN)
                parts_v[b][kh] = _dg(pbs[n_chain], do4bs[n_chain], TN)
            last = slice(rs - BLK, rs)
            for parts, out_ref, pend_ref, meta_ref in ((parts_k, dk_ref, pk_ref, dkm_ref),
                                                       (parts_v, dv_ref, pv_ref, dvm_ref)):
                for kh in range(SWA_KVH):
                    ls = slice(128 * kh, 128 * (kh + 1))
                    if bps > 1:
                        out_ref[0:rs - BLK, ls] = pend_ref[0:rs - BLK, ls]
                    out_ref[last, ls] = pend_ref[last, ls] + parts[0][kh][0:BLK]
                    meta = parts[0][kh][2 * BLK:3 * BLK]
                    for b in range(bps):
                        own = parts[b][kh][BLK:2 * BLK]
                        if b + 1 < bps:
                            own = own + parts[b + 1][kh][0:BLK]
                            meta = meta + parts[b + 1][kh][2 * BLK:3 * BLK]
                        pend_ref[b * BLK:(b + 1) * BLK, ls] = own
                    meta_ref[:, ls] += meta
            for h in range(SWA_QH):
                dsink_ref[h:h + 1, :] += jnp.broadcast_to(dsinks[h], (1, 128))

    cur, prev, first = _swa_specs(rs, ns)
    late = lambda w: pl.BlockSpec((rs, w), lambda i: (jnp.maximum(i - 1, 0), 0))
    bias = _swa_bias()
    return _pallas(
        body, name="swa_bwd", grid=(ns + 1,),
        in_specs=[pl.BlockSpec(memory_space=pltpu.SMEM), _full(bias.shape), cur(512), prev(256), cur(256), first(256),
                  prev(256), cur(256), first(256), cur(512), cur(512)],
        out_specs=[cur(512), late(256), late(256), first(256), first(256), _full((SWA_QH, 128))],
        out_shape=[jax.ShapeDtypeStruct((t, SWA_W), F32), jax.ShapeDtypeStruct((t, 256), F32),
                   jax.ShapeDtypeStruct((t, 256), F32), jax.ShapeDtypeStruct((BLK, 256), F32),
                   jax.ShapeDtypeStruct((BLK, 256), F32), jax.ShapeDtypeStruct((SWA_QH, 128), F32)],
        scratch_shapes=[pltpu.VMEM((rs, 256), F32), pltpu.VMEM((rs, 256), F32)],
        args=(sinks, bias, sq, sk, sk, sk, sv, sv, sv, o, do), hook=hook)


def _mix_out(h1, ogla, gg, oswa, wgn, wsn, wout, wpost):
    t = h1.shape[0]
    tm = _row_tile(t)

    def body(h_ref, og_ref, gg_ref, os_ref, wgn_ref, wsn_ref, wout_ref, wpost_ref, h2_ref, cat_ref, m_ref):
        parts = []
        for h in range(GLA_HEADS):
            ls = slice(GLA_DV * h, GLA_DV * (h + 1))
            y, _, _ = _rms(og_ref[:, ls], wgn_ref[...])
            g = gg_ref[:, ls]
            parts.append(y * (g * _sigmoid(g)))
        ys, _, _ = _rms(os_ref[...], wsn_ref[...])
        cat = jnp.concatenate(parts + [ys], axis=1).astype(BF16)
        cat_ref[...] = cat
        m = _dot(cat, wout_ref[...])
        m_ref[...] = m
        y, _, _ = _rms(m, wpost_ref[...])
        h2_ref[...] = h_ref[...] + y

    def row(w):
        return pl.BlockSpec((tm, w), lambda i: (i, 0))

    return pl.pallas_call(
        body, name="mix_out", grid=(t // tm,),
        in_specs=[row(D_MODEL), row(512), row(512), row(512), _full((1, GLA_DV)), _full((1, SWA_W)),
                  _full((D_MODEL, D_MODEL)), _full((1, D_MODEL))],
        out_specs=[row(D_MODEL), row(D_MODEL), row(D_MODEL)],
        out_shape=[jax.ShapeDtypeStruct((t, D_MODEL), F32), jax.ShapeDtypeStruct((t, D_MODEL), BF16),
                   jax.ShapeDtypeStruct((t, D_MODEL), F32)],
        compiler_params=_cparams(1),
    )(h1, ogla, gg, oswa, wgn, wsn, wout, wpost)


def _mix_out_bwd(dh2, m, ogla, gg, oswa, wgn, wsn, wout, wpost, hook=None):
    t = dh2.shape[0]
    tm = _row_tile(t)

    def body(dh_ref, m_ref, og_ref, gg_ref, os_ref, wgn_ref, wsn_ref, wout_ref, wpost_ref,
             dog_ref, dgg_ref, dos_ref, dm_ref, dwpost_ref, dwgn_ref, dwsn_ref):
        @pl.when(pl.program_id(0) == 0)
        def _():
            dwpost_ref[...] = jnp.zeros_like(dwpost_ref)
            dwgn_ref[...] = jnp.zeros_like(dwgn_ref)
            dwsn_ref[...] = jnp.zeros_like(dwsn_ref)

        wpost = wpost_ref[...]
        _, mh, r = _rms(m_ref[...], wpost)
        dm, dw = _rms_bwd(mh, r, wpost, dh_ref[...])
        dwpost_ref[...] += dw
        dmb = dm.astype(BF16)
        dm_ref[...] = dmb
        dcat = _dg(dmb, wout_ref[...], NT)
        wgn = wgn_ref[...]
        for h in range(GLA_HEADS):
            ls = slice(GLA_DV * h, GLA_DV * (h + 1))
            dog = dcat[:, ls]
            g = gg_ref[:, ls]
            sg = _sigmoid(g)
            y, xh, r = _rms(og_ref[:, ls], wgn)
            dgg_ref[:, ls] = dog * y * (sg * (1.0 + g * (1.0 - sg)))
            dx, dw = _rms_bwd(xh, r, wgn, dog * (g * sg))
            dog_ref[:, ls] = dx
            dwgn_ref[...] += dw
        wsn = wsn_ref[...]
        _, xh, r = _rms(os_ref[...], wsn)
        dx, dw = _rms_bwd(xh, r, wsn, dcat[:, GLA_W:])
        dos_ref[...] = dx
        dwsn_ref[...] += dw

    def row(w):
        return pl.BlockSpec((tm, w), lambda i: (i, 0))

    def rshape(w, dt=F32):
        return jax.ShapeDtypeStruct((t, w), dt)

    return _pallas(
        body, name="mix_out_bwd", grid=(t // tm,),
        in_specs=[row(D_MODEL), row(D_MODEL), row(512), row(512), row(512), _full((1, GLA_DV)), _full((1, SWA_W)),
                  _full((D_MODEL, D_MODEL)), _full((1, D_MODEL))],
        out_specs=[row(512), row(512), row(512), row(D_MODEL), _full((1, D_MODEL)), _full((1, GLA_DV)),
                   _full((1, SWA_W))],
        out_shape=[rshape(512), rshape(512), rshape(512), rshape(D_MODEL, BF16),
                   jax.ShapeDtypeStruct((1, D_MODEL), F32), jax.ShapeDtypeStruct((1, GLA_DV), F32),
                   jax.ShapeDtypeStruct((1, SWA_W), F32)],
        args=(dh2, m, ogla, gg, oswa, wgn, wsn, wout, wpost), hook=hook)


def _mix_in_bwd(dh2, h1, wmixpre, winp, wa2p, bap, cos, sin, ga, dgq, dgk, dgv, dgg, dla, dsq, dsk, dsv, dkm, dvm):
    t = h1.shape[0]
    tm = _row_tile(t)

    def body(dh2_ref, h_ref, w_ref, win_ref, wa2_ref, ba_ref, cos_ref, sin_ref, ga_ref, dgq_ref, dgk_ref, dgv_ref,
             dgg_ref, dla_ref, dsq_ref, dsk_ref, dsv_ref, dkm_ref, dvm_ref,
             dh1_ref, dproj_ref, dw_ref, dwa2_ref, dba_ref):
        i = pl.program_id(0)

        @pl.when(i == 0)
        def _():
            dw_ref[...] = jnp.zeros_like(dw_ref)
            dwa2_ref[...] = jnp.zeros_like(dwa2_ref)
            dba_ref[...] = jnp.zeros_like(dba_ref)

        first = (i == 0).astype(F32)
        c = cos_ref[...]
        s = -sin_ref[...]
        fh = _first_half_mask(tm)
        dproj_ref[:, P_GQ:P_GK] = dgq_ref[...].astype(BF16)
        dproj_ref[:, P_GK:P_GV] = dgk_ref[...].astype(BF16)
        dproj_ref[:, P_GV:P_GG] = dgv_ref[...].astype(BF16)
        dproj_ref[:, P_GG:P_GA] = dgg_ref[...].astype(BF16)
        gab = ga_ref[...].astype(BF16)
        z = _dot(gab, wa2_ref[...]) + ba_ref[...]
        row_id = i * tm + lax.broadcasted_iota(jnp.int32, (tm, 1), 0)
        dz = jnp.where(row_id >= PAD, dla_ref[...] * (1.0 / GLA_TAU) * (1.0 - _sigmoid(z)), 0.0)
        dzb = dz.astype(BF16)
        dba_ref[...] += jnp.sum(dz, axis=0, keepdims=True)
        dwa2_ref[...] += _dg(gab, dzb, TN)
        dproj_ref[:, P_GA:P_SQ] = _dg(dzb, wa2_ref[...], NT).astype(BF16)
        for k in range(4):
            dy = dsq_ref[:, 128 * k:128 * (k + 1)]
            dproj_ref[:, P_SQ + 128 * k:P_SQ + 128 * (k + 1)] = (dy * c + _rot_half(dy, fh) * s).astype(BF16)
        for k in range(2):
            ls = slice(128 * k, 128 * (k + 1))
            dy = dsk_ref[:, ls]
            dy = jnp.concatenate([dy[:BLK] + first * dkm_ref[:, ls], dy[BLK:]], axis=0) if tm > BLK else (
                dy + first * dkm_ref[:, ls])
            dproj_ref[:, P_SK + 128 * k:P_SK + 128 * (k + 1)] = (dy * c + _rot_half(dy, fh) * s).astype(BF16)
            dv = dsv_ref[:, ls]
            dv = jnp.concatenate([dv[:BLK] + first * dvm_ref[:, ls], dv[BLK:]], axis=0) if tm > BLK else (
                dv + first * dvm_ref[:, ls])
            dproj_ref[:, P_SV + 128 * k:P_SV + 128 * (k + 1)] = dv.astype(BF16)
        dn = _dg(dproj_ref[...], win_ref[...], NT)
        w = w_ref[...]
        _, hh, r = _rms(h_ref[...], w)
        dx, dw = _rms_bwd(hh, r, w, dn)
        dw_ref[...] += dw
        dh1_ref[...] = dh2_ref[...] + dx

    def row(w):
        return pl.BlockSpec((tm, w), lambda i: (i, 0))

    return pl.pallas_call(
        body, name="mix_in_bwd", grid=(t // tm,),
        in_specs=[row(D_MODEL), row(D_MODEL), _full((1, D_MODEL)), _full((D_MODEL, P_END)), _full((128, GLA_KW)),
                  _full((1, GLA_KW)), row(128), row(128), row(128), row(256), row(256), row(512), row(512), row(256),
                  row(512), row(256), row(256), _full((BLK, 256)), _full((BLK, 256))],
        out_specs=[row(D_MODEL), row(P_END), _full((1, D_MODEL)), _full((128, GLA_KW)), _full((1, GLA_KW))],
        out_shape=[jax.ShapeDtypeStruct((t, D_MODEL), F32), jax.ShapeDtypeStruct((t, P_END), BF16),
                   jax.ShapeDtypeStruct((1, D_MODEL), F32), jax.ShapeDtypeStruct((128, GLA_KW), F32),
                   jax.ShapeDtypeStruct((1, GLA_KW), F32)],
        compiler_params=_cparams(1),
    )(dh2, h1, wmixpre, winp, wa2p, bap, cos, sin, ga, dgq, dgk, dgv, dgg, dla, dsq, dsk, dsv, dkm, dvm)


def _adamw_update(w, g, m, v):
    m = ADAM_B1 * m + (1.0 - ADAM_B1) * g
    v = ADAM_B2 * v + (1.0 - ADAM_B2) * (g * g)
    m_hat = m / (1.0 - ADAM_B1 ** ADAM_STEP)
    v_hat = v / (1.0 - ADAM_B2 ** ADAM_STEP)
    return -ADAM_LR * (m_hat / (jnp.sqrt(v_hat) + ADAM_EPS) + ADAM_WD * w), m, v


def _adamw_halves(w, g_mine, g_other, m, v, c_idx, row0=0):
    r, c = w.shape
    h = g_mine.shape[0]
    tr = _div_tile(math.gcd(r, h))
    nth = h // tr
    t0 = row0 // tr
    assert t0 * tr == row0

    def body(c_ref, w_ref, gm_ref, go_ref, m_ref, v_ref, g_ref, d_ref, nm_ref, nv_ref):
        hh = (t0 + pl.program_id(0)) // nth
        g = jnp.where(hh == c_ref[0], gm_ref[...], go_ref[...])
        g_ref[...] = g
        d_ref[...], nm_ref[...], nv_ref[...] = _adamw_update(w_ref[...], g, m_ref[...], v_ref[...])

    spec = pl.BlockSpec((tr, c), lambda i, c_ref: (i, 0))

    def gspec(is_mine):
        def index(i, c_ref):
            used = ((t0 + i) // nth == c_ref[0]) == is_mine
            return (jnp.where(used, (t0 + i) % nth, 0), 0)
        return pl.BlockSpec((tr, c), index)

    shape = jax.ShapeDtypeStruct((r, c), F32)
    return pl.pallas_call(
        body, name="adamw_halves",
        grid_spec=pltpu.PrefetchScalarGridSpec(
            num_scalar_prefetch=1, grid=(r // tr,), in_specs=[spec, gspec(True), gspec(False), spec, spec],
            out_specs=[spec] * 4),
        out_shape=[shape] * 4, compiler_params=_cparams(1),
    )(c_idx, w, g_mine, g_other, m, v)


def _place():
    x, y, c = lax.axis_index("x"), lax.axis_index("y"), lax.axis_index("c")
    chips = [(1 - x, y), (x, 1 - y), (1 - x, 1 - y)]
    return x, y, c, chips


def _remote(send_sem, recv_sem, src, dst, to):
    return pltpu.make_async_remote_copy(src_ref=src, dst_ref=dst, send_sem=send_sem, recv_sem=recv_sem,
                                        device_id=to, device_id_type=MESH)


def _half(ref_rows, c):
    h = ref_rows // 2
    return pl.ds(pl.multiple_of(c * h, 8), h)


def _own_slot(shard, q):
    return lax.dynamic_update_slice(jnp.zeros((N_CHIPS,) + shard.shape, shard.dtype), shard[None], (q, 0, 0))


def _stack_own_slot(mats, q_idx):
    r, w = mats[0].shape
    tr = _div_tile(r)
    per = r // tr
    n = len(mats)

    def body(q_ref, *refs):
        m_refs, o_ref = refs[:n], refs[n]
        s = pl.program_id(0)
        for k in range(n):
            @pl.when(s // per == k)
            def _(k=k):
                o_ref[...] = m_refs[k][...].astype(BF16)

    def rows_of(k):
        return lambda s, q_ref: (jnp.where(s // per == k, s % per, 0), 0)

    return pl.pallas_call(
        body, name="stack_own_slot",
        grid_spec=pltpu.PrefetchScalarGridSpec(
            num_scalar_prefetch=1, grid=(n * per,),
            in_specs=[pl.BlockSpec((tr, w), rows_of(k)) for k in range(n)],
            out_specs=pl.BlockSpec((None, tr, w), lambda s, q_ref: (q_ref[0], s, 0))),
        out_shape=jax.ShapeDtypeStruct((N_CHIPS, n * r, w), BF16), compiler_params=_cparams(1),
    )(q_idx, *mats)


class _GatherChips:
    has_mid = True

    def __init__(self, bufs):
        n = len(bufs)
        self.inputs = list(bufs)
        self.out_shape = [jax.ShapeDtypeStruct(b.shape, b.dtype) for b in bufs]
        self.aliases = [(t, t) for t in range(n)]
        self.scratch = [pltpu.SemaphoreType.DMA((n, 6)), pltpu.SemaphoreType.DMA((n, 6))]

    def start(self, ins, outs, scr):
        send, recv = scr
        x, y, c, chips = _place()
        q = 2 * x + y
        for t, (i_ref, o_ref) in enumerate(zip(ins, outs)):
            rows = _half(i_ref.shape[1], c)
            for j, (cx, cy) in enumerate(chips):
                _remote(send.at[t, j], recv.at[t, j], i_ref.at[q, rows], o_ref.at[q, rows], (cx, cy, c)).start()

    def mid(self, ins, outs, scr):
        send, recv = scr
        x, y, c, chips = _place()
        for t, o_ref in enumerate(outs):
            rows = _half(o_ref.shape[1], c)
            for j, (cx, cy) in enumerate(chips):
                slot = o_ref.at[2 * cx + cy, rows]
                _remote(send.at[t, j], recv.at[t, j], slot, slot, (cx, cy, c)).wait_recv()
                _remote(send.at[t, 3 + j], recv.at[t, 3 + j], slot, slot, (x, y, 1 - c)).start()

    def finish(self, ins, outs, scr):
        send, recv = scr
        x, y, c, chips = _place()
        for t, o_ref in enumerate(outs):
            mine, other = _half(o_ref.shape[1], c), _half(o_ref.shape[1], 1 - c)
            for j, (cx, cy) in enumerate(chips):
                slot = o_ref.at[2 * cx + cy, other]
                _remote(send.at[t, 3 + j], recv.at[t, 3 + j], slot, slot, (x, y, 1 - c)).wait_recv()
            for j, (cx, cy) in enumerate(chips):
                sent = o_ref.at[2 * cx + cy, mine]
                _remote(send.at[t, j], recv.at[t, j], sent, sent, (cx, cy, c)).wait_send()
                _remote(send.at[t, 3 + j], recv.at[t, 3 + j], sent, sent, (x, y, 1 - c)).wait_send()


class _PairExchange:
    has_mid = False
    aliases = ()

    def __init__(self, arrs):
        n = len(arrs)
        self.inputs = list(arrs)
        self.out_shape = [jax.ShapeDtypeStruct((a.shape[0], a.shape[1] // 2, a.shape[2]), a.dtype) for a in arrs]
        self.scratch = [pltpu.SemaphoreType.DMA((n,)), pltpu.SemaphoreType.DMA((n,))]

    def _copies(self, ins, outs, scr):
        send, recv = scr
        x, y, c, _ = _place()
        return [_remote(send.at[t], recv.at[t], i_ref.at[:, _half(i_ref.shape[1], 1 - c)], o_ref, (x, y, 1 - c))
                for t, (i_ref, o_ref) in enumerate(zip(ins, outs))]

    def start(self, ins, outs, scr):
        for cp in self._copies(ins, outs, scr):
            cp.start()

    def finish(self, ins, outs, scr):
        for cp in self._copies(ins, outs, scr):
            cp.wait()


class _ChipScatter:
    has_mid = False
    aliases = ()

    def __init__(self, arrs):
        n = len(arrs)
        self.inputs = list(arrs)
        self.out_shape = [jax.ShapeDtypeStruct((3,) + a.shape[1:], a.dtype) for a in arrs]
        self.scratch = [pltpu.SemaphoreType.DMA((n, 3)), pltpu.SemaphoreType.DMA((n, 3))]

    def _copies(self, ins, outs, scr):
        send, recv = scr
        x, y, c, chips = _place()
        return [_remote(send.at[t, j], recv.at[t, j], i_ref.at[2 * cx + cy], o_ref.at[j], (cx, cy, c))
                for t, (i_ref, o_ref) in enumerate(zip(ins, outs)) for j, (cx, cy) in enumerate(chips)]

    def start(self, ins, outs, scr):
        for cp in self._copies(ins, outs, scr):
            cp.start()

    def finish(self, ins, outs, scr):
        for cp in self._copies(ins, outs, scr):
            cp.wait()


class _PairShare:
    has_mid = False
    aliases = ()

    def __init__(self, arrs):
        n = len(arrs)
        self.inputs = list(arrs)
        self.out_shape = [jax.ShapeDtypeStruct(a.shape, a.dtype) for a in arrs]
        self.scratch = [pltpu.SemaphoreType.DMA((n,)), pltpu.SemaphoreType.DMA((n,))]

    def _copies(self, ins, outs, scr):
        send, recv = scr
        x, y, c, _ = _place()
        return [_remote(send.at[t], recv.at[t], i_ref, o_ref, (x, y, 1 - c))
                for t, (i_ref, o_ref) in enumerate(zip(ins, outs))]

    def start(self, ins, outs, scr):
        for cp in self._copies(ins, outs, scr):
            cp.start()

    def finish(self, ins, outs, scr):
        for cp in self._copies(ins, outs, scr):
            cp.wait()


def _comm_call(hook, name):
    n_in, n_out = len(hook.inputs), len(hook.out_shape)

    def body(*refs):
        ins, outs, scr = refs[:n_in], refs[n_in:n_in + n_out], refs[n_in + n_out:]
        hook.start(ins, outs, scr)
        if hook.has_mid:
            hook.mid(ins, outs, scr)
        hook.finish(ins, outs, scr)

    return pl.pallas_call(body, name=name, in_specs=[ANY] * n_in, out_specs=[ANY] * n_out,
                          out_shape=list(hook.out_shape), scratch_shapes=list(hook.scratch),
                          input_output_aliases=dict(hook.aliases))(*hook.inputs)


class _GatherDevices:
    has_mid = True
    aliases = ()

    def __init__(self, vecs):
        n = len(vecs)
        self.inputs = list(vecs)
        self.out_shape = [jax.ShapeDtypeStruct((N_DEV,) + v.shape, v.dtype) for v in vecs]
        self.scratch = [pltpu.SemaphoreType.DMA((n, 7)), pltpu.SemaphoreType.DMA((n, 7)),
                        pltpu.SemaphoreType.DMA((n,))]

    @staticmethod
    def _copy(scr, t, k, out_ref, block, to, src=None):
        send, recv, _ = scr
        px, py, pc = block
        slot = out_ref.at[4 * px + 2 * py + pc]
        return _remote(send.at[t, k], recv.at[t, k], slot if src is None else src, slot, to)

    def start(self, ins, outs, scr):
        x, y, c, chips = _place()
        me = (x, y, c)
        for t, (x_ref, out_ref) in enumerate(zip(ins, outs)):
            pltpu.make_async_copy(x_ref, out_ref.at[4 * x + 2 * y + c], scr[2].at[t]).start()
            self._copy(scr, t, 0, out_ref, me, (x, y, 1 - c), src=x_ref).start()
            for j, chip in enumerate(chips):
                self._copy(scr, t, 1 + j, out_ref, me, (*chip, c), src=x_ref).start()

    def mid(self, ins, outs, scr):
        x, y, c, chips = _place()
        for t, out_ref in enumerate(outs):
            for j, chip in enumerate(chips):
                self._copy(scr, t, 1 + j, out_ref, (*chip, c), (x, y, c)).wait_recv()
                self._copy(scr, t, 4 + j, out_ref, (*chip, c), (x, y, 1 - c)).start()

    def finish(self, ins, outs, scr):
        x, y, c, chips = _place()
        me = (x, y, c)
        for t, (x_ref, out_ref) in enumerate(zip(ins, outs)):
            self._copy(scr, t, 0, out_ref, (x, y, 1 - c), me).wait_recv()
            for j, chip in enumerate(chips):
                self._copy(scr, t, 4 + j, out_ref, (*chip, 1 - c), me).wait_recv()
            self._copy(scr, t, 0, out_ref, me, (x, y, 1 - c), src=x_ref).wait_send()
            for j, chip in enumerate(chips):
                self._copy(scr, t, 1 + j, out_ref, me, (*chip, c), src=x_ref).wait_send()
                self._copy(scr, t, 4 + j, out_ref, (*chip, c), (x, y, 1 - c)).wait_send()
            pltpu.make_async_copy(x_ref, out_ref.at[4 * x + 2 * y + c], scr[2].at[t]).wait()


class _Hooks:
    def __init__(self, hooks):
        self.hooks = list(hooks)
        self.has_mid = any(h.has_mid for h in hooks)
        self.inputs = [a for h in hooks for a in h.inputs]
        self.out_shape = [s for h in hooks for s in h.out_shape]
        self.scratch = [s for h in hooks for s in h.scratch]
        self.aliases = []
        i0 = o0 = 0
        for h in hooks:
            self.aliases += [(i0 + a, o0 + b) for a, b in h.aliases]
            i0 += len(h.inputs)
            o0 += len(h.out_shape)

    def _each(self, ins, outs, scr):
        i0 = o0 = s0 = 0
        for h in self.hooks:
            ni, no, ns = len(h.inputs), len(h.out_shape), len(h.scratch)
            yield h, ins[i0:i0 + ni], outs[o0:o0 + no], scr[s0:s0 + ns]
            i0, o0, s0 = i0 + ni, o0 + no, s0 + ns

    def start(self, ins, outs, scr):
        for h, i, o, s in self._each(ins, outs, scr):
            h.start(i, o, s)

    def mid(self, ins, outs, scr):
        for h, i, o, s in self._each(ins, outs, scr):
            if h.has_mid:
                h.mid(i, o, s)

    def finish(self, ins, outs, scr):
        for h, i, o, s in self._each(ins, outs, scr):
            h.finish(i, o, s)

    def split(self, outs):
        res, o0 = [], 0
        for h in self.hooks:
            res.append(list(outs[o0:o0 + len(h.out_shape)]))
            o0 += len(h.out_shape)
        return res


def _pair_sum(g, other, c_idx):
    nq, r, w = g.shape
    h = r // 2
    tr = _div_tile(h)
    nt = h // tr

    def body(c_ref, g_ref, o_ref, s_ref):
        s_ref[...] = (g_ref[...].astype(F32) + o_ref[...].astype(F32)).astype(s_ref.dtype)

    return pl.pallas_call(
        body, name="pair_sum",
        grid_spec=pltpu.PrefetchScalarGridSpec(
            num_scalar_prefetch=1, grid=(nq, nt),
            in_specs=[pl.BlockSpec((None, tr, w), lambda k, i, c_ref: (k, c_ref[0] * nt + i, 0)),
                      pl.BlockSpec((None, tr, w), lambda k, i, c_ref: (k, i, 0))],
            out_specs=pl.BlockSpec((None, tr, w), lambda k, i, c_ref: (k, i, 0))),
        out_shape=jax.ShapeDtypeStruct((nq, h, w), g.dtype),
        compiler_params=_cparams(2),
    )(c_idx, g, other)


def _chip_sum(s, others, q_idx):
    _, h, w = s.shape
    tr = _div_tile(h)

    def body(q_ref, s_ref, o_ref, out_ref):
        out_ref[...] = ((s_ref[...].astype(F32) + o_ref[0].astype(F32)) + o_ref[1].astype(F32)) + o_ref[2].astype(F32)

    return pl.pallas_call(
        body, name="chip_sum",
        grid_spec=pltpu.PrefetchScalarGridSpec(
            num_scalar_prefetch=1, grid=(h // tr,),
            in_specs=[pl.BlockSpec((None, tr, w), lambda i, q_ref: (q_ref[0], i, 0)),
                      pl.BlockSpec((3, tr, w), lambda i, q_ref: (0, i, 0))],
            out_specs=pl.BlockSpec((tr, w), lambda i, q_ref: (i, 0))),
        out_shape=jax.ShapeDtypeStruct((h, w), F32),
        compiler_params=_cparams(1),
    )(q_idx, s, others)


def _small_update(q_idx, parts, ws, ms, vs, col_block):
    n = len(parts)
    has_w = [w is not None for w in ws]

    def body(q_ref, *refs):
        pos = 0
        ins = []
        for t in range(n):
            k = 4 if has_w[t] else 1
            ins.append(refs[pos:pos + k])
            pos += k
        outs = refs[pos:]
        opos = 0
        for t in range(n):
            p_ref = ins[t][0]
            g = p_ref[0]
            for s in range(1, p_ref.shape[0]):
                g = g + p_ref[s]
            if has_w[t]:
                _, w_ref, m_ref, v_ref = ins[t]
                g_ref, d_ref, nm_ref, nv_ref = outs[opos:opos + 4]
                opos += 4
                g_ref[...] = g
                d_ref[...], nm_ref[...], nv_ref[...] = _adamw_update(w_ref[...], g, m_ref[...], v_ref[...])
            else:
                outs[opos][...] = g
                opos += 1

    def whole(shape):
        nd = len(shape)
        return pl.BlockSpec(shape, lambda i, q_ref: (0,) * nd)

    in_specs, out_specs, out_shape, args = [], [], [], []
    for t in range(n):
        k, r, wf = parts[t].shape
        if col_block[t]:
            w = wf // N_CHIPS
            in_specs.append(pl.BlockSpec((k, r, w), lambda i, q_ref: (0, 0, q_ref[0])))
        else:
            w = wf
            in_specs.append(whole((k, r, wf)))
        args.append(parts[t])
        if has_w[t]:
            assert ws[t].shape == (r, w), (ws[t].shape, r, w)
            in_specs += [whole((r, w))] * 3
            args += [ws[t], ms[t], vs[t]]
            out_specs += [whole((r, w))] * 4
            out_shape += [jax.ShapeDtypeStruct((r, w), F32)] * 4
        else:
            out_specs.append(whole((r, w)))
            out_shape.append(jax.ShapeDtypeStruct((r, w), F32))
    return pl.pallas_call(
        body, name="small_update",
        grid_spec=pltpu.PrefetchScalarGridSpec(num_scalar_prefetch=1, grid=(1,), in_specs=in_specs,
                                               out_specs=out_specs),
        out_shape=out_shape, compiler_params=_cparams(1),
    )(q_idx, *args)


_PACK_SEGMENTS = ((0, 1552), None, (1552, 2064), (2064, 2128), (2064, 2128), (2128, 2192), (2128, 2192),
                  (2192, 2256), (2192, 2256), (2256, 2320), (2256, 2320))
_UNPACK_SEGMENTS = (((0, 1552), (0,)), ((1552, 2064), (P_SQ,)), ((2064, 2128), (P_SK, P_SK + 64)),
                    ((2128, 2192), (P_SK + 128, P_SK + 192)), ((2192, 2256), (P_SV, P_SV + 64)),
                    ((2256, 2320), (P_SV + 128, P_SV + 192)))


def _pack_win(w4):
    per = w4.shape[2]
    pieces = []
    for seg in _PACK_SEGMENTS:
        if seg is None:
            pieces.append(jnp.zeros((w4.shape[1], 128 - GLA_RANK), w4.dtype))
            continue
        for q in range(w4.shape[0]):
            lo, hi = max(seg[0], q * per), min(seg[1], (q + 1) * per)
            if lo < hi:
                pieces.append(w4[q][:, lo - q * per:hi - q * per])
    return jnp.concatenate(pieces, axis=1)


def _unpack_dwin(d):
    per = D_IN // N_CHIPS
    chips = []
    for q in range(N_CHIPS):
        pieces = []
        for (a, b), starts in _UNPACK_SEGMENTS:
            lo, hi = max(a, q * per), min(b, (q + 1) * per)
            if lo < hi:
                copies = [d[:, s + lo - a:s + hi - a] for s in starts]
                pieces.append(copies[0] if len(copies) == 1 else copies[0] + copies[1])
        chips.append(jnp.concatenate(pieces, axis=1))
    return jnp.stack(chips)


def _local_step(x, target, meta, p):
    s = x.shape[0]
    t = s + BLK
    h0 = jnp.concatenate([jnp.zeros((PAD, D_MODEL), F32), meta, x], axis=0)
    cos, sin = _rope_tables(t)

    h1, n1, g1, u1, a1, f1 = _ffn_fwd(h0, p["ffn1_pre_norm"], p["ffn1_w"], p["ffn1_post_norm"])
    n2, gq, gk, gv, gg, ga, la, sq, sk, sv = _mix_proj(h1, p["mix_pre_norm"], p["w_in"], p["gla_w_a2"], p["gla_b_a"],
                                                       cos, sin)
    ogla, ss = _gla_fwd(gq, gk, gv, la)
    oswa = _swa_fwd(p["swa_sinks"], sq, sk, sv)
    h2, cat, m = _mix_out(h1, ogla, gg, oswa, p["gla_out_norm"], p["swa_out_norm"], p["w_out"], p["mix_post_norm"])
    grads = {}
    dy, n3, g3, u3, a3, df3, grads["ffn2_post_norm"], sse = _ffn_fwd(
        h2, p["ffn2_pre_norm"], p["ffn2_w"], p["ffn2_post_norm"], target=target)

    dh2, dg3, du3, grads["ffn2_pre_norm"] = _ffn_bwd(
        dy, h2, None, g3, u3, p["ffn2_pre_norm"], p["ffn2_w"], p["ffn2_post_norm"], df=df3)
    (gud,) = _ffn_wgrad(n3, df3, dg3, du3, a3)
    grads["ffn2_w_gate"], grads["ffn2_w_up"], grads["ffn2_w_down"] = gud[:, :FJ], gud[:, FJ:2 * FJ], gud[:, 2 * FJ:]

    dogla, dgg, doswa, dm, grads["mix_post_norm"], grads["gla_out_norm"], grads["swa_out_norm"] = _mix_out_bwd(
        dh2, m, ogla, gg, oswa, p["gla_out_norm"], p["swa_out_norm"], p["w_out"], p["mix_post_norm"])
    grads["w_out"] = _xty(cat, dm)
    dsq, dsk, dsv, dkm, dvm, dsinks = _swa_bwd(p["swa_sinks"], sq, sk, sv, oswa, doswa)
    grads["swa_sinks"] = dsinks[:, 0]
    dgq, dgk, dgv, dla = _gla_bwd(gq, gk, gv, la, ss, dogla)
    dh1, dproj, grads["mix_pre_norm"], dwa2p, grads["gla_b_a"] = _mix_in_bwd(
        dh2, h1, p["mix_pre_norm"], p["w_in"], p["gla_w_a2"], p["gla_b_a"], cos, sin, ga, dgq, dgk, dgv, dgg, dla,
        dsq, dsk, dsv, dkm, dvm)
    grads["gla_w_a2"] = dwa2p[:GLA_RANK]
    grads["w_in"] = _unpack_dwin(_xty(n2, dproj))

    dh0, df1, dg1, du1, grads["ffn1_pre_norm"], grads["ffn1_post_norm"] = _ffn_bwd(
        dh1, h0, f1, g1, u1, p["ffn1_pre_norm"], p["ffn1_w"], p["ffn1_post_norm"])
    (gud,) = _ffn_wgrad(n1, df1, dg1, du1, a1)
    grads["ffn1_w_gate"], grads["ffn1_w_up"], grads["ffn1_w_down"] = gud[:, :FJ], gud[:, FJ:2 * FJ], gud[:, 2 * FJ:]
    grads["meta_tokens"] = dh0[PAD:BLK]
    return sse[0, 0], dh0[BLK:], grads


WEIGHTS = ['meta_tokens', 'ffn1_pre_norm', 'ffn1_w_gate', 'ffn1_w_up', 'ffn1_w_down', 'ffn1_post_norm',
           'mix_pre_norm', 'w_in', 'gla_w_a2', 'gla_b_a', 'gla_out_norm', 'swa_sinks', 'swa_out_norm', 'w_out',
           'mix_post_norm', 'ffn2_pre_norm', 'ffn2_w_gate', 'ffn2_w_up', 'ffn2_w_down', 'ffn2_post_norm']
BIG = ['ffn1_w_gate', 'ffn1_w_up', 'ffn1_w_down', 'w_in', 'w_out', 'ffn2_w_gate', 'ffn2_w_up', 'ffn2_w_down']
SMALL = [n for n in WEIGHTS if n not in BIG]
FJ = D_FF // N_CHIPS
D_IN_J = D_IN // N_CHIPS
D_OUT_J = D_MODEL // N_CHIPS
TRANSPOSED = ('ffn1_w_gate', 'ffn1_w_up', 'ffn2_w_gate', 'ffn2_w_up')


def _shard2d(name, a):
    return a[0].T if name in TRANSPOSED else a[0]


def _unshard2d(name, a):
    return (a.T if name in TRANSPOSED else a)[None]


def kernel(x, meta_tokens, ffn1_pre_norm, ffn1_w_gate, ffn1_w_up, ffn1_w_down, ffn1_post_norm, mix_pre_norm, w_in, gla_w_a2, gla_b_a, gla_out_norm, swa_sinks, swa_out_norm, w_out, mix_post_norm, ffn2_pre_norm, ffn2_w_gate, ffn2_w_up, ffn2_w_down, ffn2_post_norm, loss_target, m_meta_tokens, m_ffn1_pre_norm, m_ffn1_w_gate, m_ffn1_w_up, m_ffn1_w_down, m_ffn1_post_norm, m_mix_pre_norm, m_w_in, m_gla_w_a2, m_gla_b_a, m_gla_out_norm, m_swa_sinks, m_swa_out_norm, m_w_out, m_mix_post_norm, m_ffn2_pre_norm, m_ffn2_w_gate, m_ffn2_w_up, m_ffn2_w_down, m_ffn2_post_norm, v_meta_tokens, v_ffn1_pre_norm, v_ffn1_w_gate, v_ffn1_w_up, v_ffn1_w_down, v_ffn1_post_norm, v_mix_pre_norm, v_w_in, v_gla_w_a2, v_gla_b_a, v_gla_out_norm, v_swa_sinks, v_swa_out_norm, v_w_out, v_mix_post_norm, v_ffn2_pre_norm, v_ffn2_w_gate, v_ffn2_w_up, v_ffn2_w_down, v_ffn2_post_norm):
    args = dict(locals())
    w = {n: args[n] for n in WEIGHTS}
    mom = {n: args["m_" + n] for n in WEIGHTS}
    var = {n: args["v_" + n] for n in WEIGHTS}
    cx, cy, cc = lax.axis_index("x"), lax.axis_index("y"), lax.axis_index("c")
    q_idx = (2 * cx + cy).astype(jnp.int32).reshape(1)
    c_idx = cc.astype(jnp.int32).reshape(1)

    q_chip = 2 * cx + cy
    bf = {n: _own_slot(_shard2d(n, w[n]).astype(BF16), q_chip) for n in ("w_in", "w_out")}
    for ffn in ("ffn1", "ffn2"):
        bf[ffn] = _stack_own_slot([_shard2d(ffn + s, w[ffn + s]) for s in ("_w_gate", "_w_up", "_w_down")], q_idx)
    qc_idx = jnp.stack([q_chip, cc]).astype(jnp.int32)
    sinks = w["swa_sinks"].reshape(SWA_QH)

    seq, target = x[0], loss_target[0]
    t = seq.shape[0] + BLK
    h0, n1 = _embed_norm(seq, _own_slot(w["meta_tokens"], q_chip), w["ffn1_pre_norm"])
    cos, sin = _rope_tables(t)
    late = _GatherChips([bf["w_in"], bf["w_out"], bf["ffn2"],
                         _own_slot(w["gla_w_a2"].reshape(GLA_RANK, GLA_KW // N_CHIPS), q_chip)])
    (h1, g1, u1, a1, f1), (w31,), (win4, wout4, w32, wa24) = _ffn_fwd_gather(
        h0, n1, bf["ffn1"], w["ffn1_post_norm"], qc_idx, late)
    wa2p = jnp.pad(wa24.transpose(1, 0, 2).reshape(GLA_RANK, GLA_KW), ((0, 128 - GLA_RANK), (0, 0))).astype(BF16)
    winp = _pack_win(win4)
    wout = wout4.reshape(D_MODEL, D_MODEL)
    n2, gq, gk, gv, gg, ga, la, sq, sk, sv = _mix_proj(h1, w["mix_pre_norm"], winp, wa2p, w["gla_b_a"], cos, sin)
    ogla, ss = _gla_fwd(gq, gk, gv, la)
    oswa = _swa_fwd(sinks, sq, sk, sv)
    h2, cat, m = _mix_out(h1, ogla, gg, oswa, w["gla_out_norm"], w["swa_out_norm"], wout, w["mix_post_norm"])
    g = {}
    dy, n3, g3, u3, a3, df3, g["ffn2_post_norm"], sse = _ffn_fwd(
        h2, w["ffn2_pre_norm"], w32, w["ffn2_post_norm"], target=target)

    dh2, dg3, du3, g["ffn2_pre_norm"] = _ffn_bwd(
        dy, h2, None, g3, u3, w["ffn2_pre_norm"], w32, w["ffn2_post_norm"], df=df3)
    (gf2,) = _ffn_wgrad(n3, df3, dg3, du3, a3)
    (dogla, dgg, doswa, dm, g["mix_post_norm"], g["gla_out_norm"], g["swa_out_norm"]), (rgf2,) = _mix_out_bwd(
        dh2, m, ogla, gg, oswa, w["gla_out_norm"], w["swa_out_norm"], wout, w["mix_post_norm"],
        hook=_PairExchange([gf2]))
    sgf2 = _pair_sum(gf2, rgf2, c_idx)
    gout = _xty(cat, dm).reshape(N_CHIPS, D_OUT_J, D_MODEL)
    (dsq, dsk, dsv, dkm, dvm, dsinks), (ogf2,) = _swa_bwd(sinks, sq, sk, sv, oswa, doswa,
                                                          hook=_ChipScatter([sgf2]))
    g["swa_sinks"] = dsinks
    dgq, dgk, dgv, dla = _gla_bwd(gq, gk, gv, la, ss, dogla)
    dh1, dproj, g["mix_pre_norm"], dwa2p, g["gla_b_a"] = _mix_in_bwd(
        dh2, h1, w["mix_pre_norm"], winp, wa2p, w["gla_b_a"], cos, sin, ga, dgq, dgk, dgv, dgg, dla,
        dsq, dsk, dsv, dkm, dvm)
    g["gla_w_a2"] = dwa2p[:GLA_RANK]
    gin = _unpack_dwin(_xty(n2, dproj))
    (dh0, df1, dg1, du1, g["ffn1_pre_norm"], g["ffn1_post_norm"]), (rgin, rgout) = _ffn_bwd(
        dh1, h0, f1, g1, u1, w["ffn1_pre_norm"], w31, w["ffn1_post_norm"],
        hook=_PairExchange([gin, gout]))
    sgin, sgout = _pair_sum(gin, rgin, c_idx), _pair_sum(gout, rgout, c_idx)
    g["meta_tokens"] = dh0[PAD:BLK]
    grad_x = dh0[BLK:]
    late_small = ["gla_w_a2", "swa_sinks"]
    direct = [n for n in SMALL if n not in late_small]
    names = direct + late_small
    half_f2 = _chip_sum(sgf2, ogf2, q_idx)
    hooks = _Hooks([_ChipScatter([sgin, sgout]), _GatherDevices([g[n] for n in names] + [sse]),
                    _PairShare([half_f2])])
    own1, others1, houts = _ffn_wgrad_reduce(n1, df1, dg1, du1, a1, qc_idx, hooks)
    (ogin, ogout), gathered, (other_f2,) = hooks.split(houts)
    halves = [_chip_sum(own1[None], others1, jnp.zeros((1,), jnp.int32))]
    halves += [_chip_sum(s, o, q_idx) for s, o in ((sgin, ogin), (sgout, ogout))]
    others = list(_comm_call(_PairShare(halves), "pair_share")) + [other_f2]
    halves.append(half_f2)
    reduced = {"ffn1_w_gate": (0, 0), "ffn1_w_up": (0, FJ), "ffn1_w_down": (0, 2 * FJ), "w_in": (1, 0),
               "w_out": (2, 0), "ffn2_w_gate": (3, 0), "ffn2_w_up": (3, FJ), "ffn2_w_down": (3, 2 * FJ)}
    grad, delta, new_m, new_v = {}, {}, {}, {}
    for n in BIG:
        k, row0 = reduced[n]
        outs = _adamw_halves(_shard2d(n, w[n]), halves[k], others[k], _shard2d(n, mom[n]), _shard2d(n, var[n]),
                             c_idx, row0)
        grad[n], delta[n], new_m[n], new_v[n] = [_unshard2d(n, a) for a in outs]

    late = late_small
    mat = lambda a: a.reshape(a.shape[-2:])
    none3 = [None] * (len(late) + 1)
    outs = _small_update(q_idx, gathered, [mat(w[n]) for n in direct] + none3, [mat(mom[n]) for n in direct] + none3,
                         [mat(var[n]) for n in direct] + none3, [n == "meta_tokens" for n in names] + [False])
    sum_a2, sum_sinks, sum_sse = outs[4 * len(direct):]
    loss = sum_sse[0, 0] * (0.5 / D_MODEL)
    g_late = [lax.dynamic_slice_in_dim(sum_a2, q_chip * (GLA_KW // N_CHIPS), GLA_KW // N_CHIPS, axis=1)[None],
              sum_sinks[:, 0].reshape(1, 1, SWA_QH)]
    outs = list(outs[:4 * len(direct)]) + list(_small_update(
        q_idx, g_late, [mat(w[n]) for n in late], [mat(mom[n]) for n in late], [mat(var[n]) for n in late],
        [False, False]))
    for k, n in enumerate(names):
        grad[n], delta[n], new_m[n], new_v[n] = [a.reshape(w[n].shape) for a in outs[4 * k:4 * k + 4]]

    return (loss, grad_x[None], *[grad[n] for n in WEIGHTS], *[delta[n] for n in WEIGHTS],
            *[new_m[n] for n in WEIGHTS], *[new_v[n] for n in WEIGHTS])
```

```python
import functools
import math

import numpy as np
import jax
import jax.numpy as jnp
from jax import lax
from jax.experimental import pallas as pl
from jax.experimental.pallas import tpu as pltpu

F32 = jnp.float32
BF16 = jnp.bfloat16
MESH = pl.DeviceIdType.MESH

D_MODEL = 1024
D_FF = 2816
N_CHIPS = 4
N_DEV = 8
N_META = 16
BLK = 128
PAD = BLK - N_META
GLA_CHUNK = 64
GLA_HEADS = 4
GLA_DV = 128
GLA_DK = 64
GLA_KW = GLA_HEADS * GLA_DK
GLA_W = GLA_HEADS * GLA_DV
GLA_RANK = 16
GLA_TAU = 16.0
SWA_HD = 64
SWA_QH = 8
SWA_KVH = 2
SWA_W = SWA_QH * SWA_HD
WINDOW = 128
ROPE_THETA = 10000.0
EPS = 1e-6
NEG_INF = -1e30
IN_SPLITS = (256, 256, 512, 512, 16, 512, 128, 128)
D_IN = sum(IN_SPLITS)
P_GQ, P_GK, P_GV, P_GG, P_GA, P_SQ, P_SK, P_SV, P_END = 0, 256, 512, 1024, 1536, 1664, 2176, 2432, 2688
ADAM_LR, ADAM_B1, ADAM_B2, ADAM_EPS, ADAM_WD, ADAM_STEP = 0.001, 0.9, 0.999, 1e-08, 0.01, 10
VMEM_LIMIT = 56 * 1024 * 1024

NT = (((1,), (1,)), ((), ()))
TN = (((0,), (0,)), ((), ()))


def _cparams(n_axes):
    return pltpu.CompilerParams(dimension_semantics=("arbitrary",) * n_axes, vmem_limit_bytes=VMEM_LIMIT)


def _row_tile(t):
    for tm in (640, 512, 384, 256, 128):
        if t % tm == 0:
            return tm
    raise ValueError(t)


SEQ_BLOCKS_PER_STEP = 5


def _seq_tile(t):
    return SEQ_BLOCKS_PER_STEP * BLK if t % (SEQ_BLOCKS_PER_STEP * BLK) == 0 else BLK


ROW_PARTS = 2


def _row_parts(tm):
    n = ROW_PARTS if tm % (16 * ROW_PARTS) == 0 else 1
    return [slice(k * (tm // n), (k + 1) * (tm // n)) for k in range(n)]


def _contract_tile(t):
    return 1664 if t % 1664 == 0 else _row_tile(t)


def _div_tile(r, cap=512):
    best = None
    for tr in range(8, min(r, cap) + 1, 8):
        if r % tr == 0:
            best = tr
    return best if best is not None else r


def _dot(a, b):
    return jnp.dot(a, b, preferred_element_type=F32)


def _dg(a, b, dims):
    return lax.dot_general(a, b, dims, preferred_element_type=F32)


def _rms(x, w):
    r = lax.rsqrt(jnp.mean(x * x, axis=-1, keepdims=True) + EPS)
    xh = x * r
    return xh * w, xh, r


def _rms_bwd(xh, r, w, dy):
    wdy = dy * w
    dx = r * (wdy - xh * jnp.mean(wdy * xh, axis=-1, keepdims=True))
    dw = jnp.sum(dy * xh, axis=0, keepdims=True)
    return dx, dw


def _sigmoid(x):
    return 1.0 / (1.0 + jnp.exp(-x))


def _full(shape):
    nd = len(shape)
    return pl.BlockSpec(shape, lambda *_: (0,) * nd)


ANY = pl.BlockSpec(memory_space=pl.ANY)


def _pallas(body, *, name, grid, in_specs, out_specs, out_shape, args, scratch_shapes=(), hook=None):
    n_axes = len(grid)
    if hook is None:
        return pl.pallas_call(body, name=name, grid=grid, in_specs=list(in_specs), out_specs=list(out_specs),
                              out_shape=list(out_shape), scratch_shapes=list(scratch_shapes),
                              compiler_params=_cparams(n_axes))(*args)
    n_in, n_out, n_scr = len(in_specs), len(out_specs), len(scratch_shapes)
    h_in, h_out = len(hook.inputs), len(hook.out_shape)
    total = math.prod(grid)

    def wrapped(*refs):
        ins, hins = refs[:n_in], refs[n_in:n_in + h_in]
        o0 = n_in + h_in
        outs, houts = refs[o0:o0 + n_out], refs[o0 + n_out:o0 + n_out + h_out]
        s0 = o0 + n_out + h_out
        scr, hscr = refs[s0:s0 + n_scr], refs[s0 + n_scr:]
        step = pl.program_id(0)
        for a in range(1, n_axes):
            step = step * grid[a] + pl.program_id(a)

        @pl.when(step == 0)
        def _():
            hook.start(hins, houts, hscr)

        body(*ins, *outs, *scr)

        if hook.has_mid:
            @pl.when(step == (3 * total) // 4)
            def _():
                hook.mid(hins, houts, hscr)

        @pl.when(step == total - 1)
        def _():
            hook.finish(hins, houts, hscr)

    res = pl.pallas_call(
        wrapped, name=name, grid=grid, in_specs=list(in_specs) + [ANY] * h_in,
        out_specs=list(out_specs) + [ANY] * h_out, out_shape=list(out_shape) + list(hook.out_shape),
        scratch_shapes=list(scratch_shapes) + list(hook.scratch), compiler_params=_cparams(n_axes),
        input_output_aliases={n_in + a: n_out + b for a, b in hook.aliases},
    )(*args, *hook.inputs)
    return res[:n_out], res[n_out:]


def _ffn_weight_specs(w3):
    fj = w3.shape[1] // 3
    return fj, [pl.BlockSpec((None, fj, D_MODEL), functools.partial(lambda i, j, k: (j, k, 0), k=k)) for k in range(3)]


def _ffn_fwd(h, wpre, w3, wpost, hook=None, target=None):
    t = h.shape[0]
    tm = _row_tile(t)
    nj, rows3, _ = w3.shape
    fj = rows3 // 3
    nblk = tm // BLK if target is not None else 0

    def body(*refs):
        h_ref, wpre_ref, w_hbm, wpost_ref = refs[:4]
        t_refs = refs[4:4 + nblk]
        hout_ref, n_ref, p1_ref, p2_ref, a_ref, f_ref = refs[4 + nblk:10 + nblk]
        acc_ref, wv, wsem = refs[-3:]
        i = pl.program_id(0)
        j = pl.program_id(1)

        @pl.when((i == 0) & (j == 0))
        def _():
            for k in range(nj):
                pltpu.make_async_copy(w_hbm.at[k], wv.at[k], wsem.at[k]).start()

        @pl.when(i == 0)
        def _():
            pltpu.make_async_copy(w_hbm.at[j], wv.at[j], wsem.at[j]).wait()

        @pl.when(j == 0)
        def _():
            y, _, _ = _rms(h_ref[...], wpre_ref[...])
            n_ref[...] = y.astype(BF16)
            acc_ref[...] = jnp.zeros_like(acc_ref)

        if target is not None:
            dwpost_ref, sse_ref = refs[10 + nblk:12 + nblk]

            @pl.when((i == 0) & (j == 0))
            def _():
                dwpost_ref[...] = jnp.zeros_like(dwpost_ref)
                sse_ref[...] = jnp.zeros_like(sse_ref)

        n = n_ref[...]
        g = _dg(n, wv[j, 0:fj], NT)
        u = _dg(n, wv[j, fj:2 * fj], NT)
        sg = _sigmoid(g)
        silu = g * sg
        p1_ref[...] = (u * (sg + silu * (1.0 - sg))).astype(BF16)
        p2_ref[...] = silu.astype(BF16)
        a = (silu * u).astype(BF16)
        a_ref[...] = a
        acc_ref[...] += _dot(a, wv[j, 2 * fj:3 * fj])

        @pl.when(j == nj - 1)
        def _():
            f = acc_ref[...]
            wpost = wpost_ref[...]
            y, fh, r = _rms(f, wpost)
            hout = h_ref[...] + 0.5 * y
            if target is None:
                f_ref[...] = f
                hout_ref[...] = hout
            else:
                sse = jnp.zeros((1, 1), F32)
                errs = []
                for k in range(nblk):
                    err = hout[k * BLK:(k + 1) * BLK] - t_refs[k][...]
                    if k == 0:
                        err = jnp.where(i > 0, err, 0.0)
                    errs.append(err)
                    sse = sse + jnp.sum(jnp.sum(err * err, axis=1, keepdims=True), axis=0, keepdims=True)
                dy = (jnp.concatenate(errs, axis=0) if nblk > 1 else errs[0]) * (1.0 / D_MODEL)
                hout_ref[...] = dy
                df, dw = _rms_bwd(fh, r, wpost, 0.5 * dy)
                f_ref[...] = df.astype(BF16)
                dwpost_ref[...] += dw
                sse_ref[...] += jnp.broadcast_to(sse, sse_ref.shape)

    row = pl.BlockSpec((tm, D_MODEL), lambda i, j: (i, 0))
    vec = pl.BlockSpec((1, D_MODEL), lambda i, j: (0, 0))
    act = pl.BlockSpec((None, tm, fj), lambda i, j: (j, i, 0))
    t_specs = [pl.BlockSpec((BLK, D_MODEL), functools.partial(lambda i, j, k: (jnp.maximum(nblk * i + k - 1, 0), 0), k=k))
               for k in range(nblk)]
    loss_spec = [vec, _full((1, 128))] if target is not None else []
    loss_shape = [jax.ShapeDtypeStruct((1, D_MODEL), F32), jax.ShapeDtypeStruct((1, 128), F32)] if (
        target is not None) else []
    return _pallas(
        body, name="ffn_fwd", grid=(t // tm, nj),
        in_specs=[row, vec, ANY, vec] + t_specs,
        out_specs=[row, row, act, act, act, row] + loss_spec,
        out_shape=[jax.ShapeDtypeStruct((t, D_MODEL), F32), jax.ShapeDtypeStruct((t, D_MODEL), BF16),
                   jax.ShapeDtypeStruct((nj, t, fj), BF16), jax.ShapeDtypeStruct((nj, t, fj), BF16),
                   jax.ShapeDtypeStruct((nj, t, fj), BF16),
                   jax.ShapeDtypeStruct((t, D_MODEL), F32 if target is None else BF16)] + loss_shape,
        scratch_shapes=[pltpu.VMEM((tm, D_MODEL), F32), pltpu.VMEM((nj, rows3, D_MODEL), BF16),
                        pltpu.SemaphoreType.DMA((nj,))],
        args=(h, wpre, w3, wpost) + (target,) * nblk, hook=hook)


def _ffn_bwd(dhout, h, f, p14, p24, wpre, w3, wpost, df=None, split_first_block=False):
    t = h.shape[0]
    tm = _row_tile(t)
    ni = t // tm
    nj = w3.shape[0]
    fj, wspecs = _ffn_weight_specs(w3)
    have_df = df is not None
    assert not (have_df and split_first_block)

    def body(dhout_ref, h_ref, f_ref, p1_ref, p2_ref, wpre_ref, wg_ref, wu_ref, wd_ref, wpost_ref, *rest):
        if have_df:
            dh_ref, dg_ref, du_ref, dwpre_ref, dn_ref = rest
            df_ref = f_ref
        elif split_first_block:
            dh_ref, rest_hbm, df_ref, dg_ref, du_ref, dwpre_ref, dwpost_ref, dn_ref, dh_buf, dh_sem, first_sem = rest
        else:
            dh_ref, df_ref, dg_ref, du_ref, dwpre_ref, dwpost_ref, dn_ref = rest
        i = pl.program_id(0)
        j = pl.program_id(1)

        @pl.when((i == 0) & (j == 0))
        def _():
            dwpre_ref[...] = jnp.zeros_like(dwpre_ref)
            if not have_df:
                dwpost_ref[...] = jnp.zeros_like(dwpost_ref)

        @pl.when(j == 0)
        def _():
            if not have_df:
                wpost = wpost_ref[...]
                _, fh, r = _rms(f_ref[...], wpost)
                dfv, dw = _rms_bwd(fh, r, wpost, 0.5 * dhout_ref[...])
                dwpost_ref[...] += dw
                df_ref[...] = dfv.astype(BF16)
            dn_ref[...] = jnp.zeros_like(dn_ref)

        parts = _row_parts(tm)
        das = [_dg(df_ref[rows, :], wd_ref[...], NT) for rows in parts]
        for rows, da in zip(parts, das):
            dg = (da * p1_ref[rows, :].astype(F32)).astype(BF16)
            du = (da * p2_ref[rows, :].astype(F32)).astype(BF16)
            dg_ref[rows, :] = dg
            du_ref[rows, :] = du
            dn_ref[rows, :] += _dot(dg, wg_ref[...]) + _dot(du, wu_ref[...])

        @pl.when(j == nj - 1)
        def _():
            wpre = wpre_ref[...]
            _, hh, r = _rms(h_ref[...], wpre)
            dx, dw = _rms_bwd(hh, r, wpre, dn_ref[...])
            dwpre_ref[...] += dw
            dh = dhout_ref[...] + dx
            if not split_first_block:
                dh_ref[...] = dh
            else:
                slot = i % 2

                def to_rest(tile, sl):
                    rows = pl.ds(pl.multiple_of(tile * tm - BLK, 8), tm)
                    return pltpu.make_async_copy(dh_buf.at[sl], rest_hbm.at[rows], dh_sem.at[sl])

                first = pltpu.make_async_copy(dh_buf.at[0, BLK:tm], rest_hbm.at[0:tm - BLK], first_sem)

                @pl.when(i == 2)
                def _():
                    first.wait()

                @pl.when(i >= 3)
                def _():
                    to_rest(i, slot).wait()

                dh_buf[slot] = dh

                @pl.when(i == 0)
                def _():
                    dh_ref[...] = dh[0:BLK]
                    first.start()

                @pl.when(i > 0)
                def _():
                    to_rest(i, slot).start()

                @pl.when(i == ni - 1)
                def _():
                    if ni < 3:
                        first.wait()
                    if ni >= 3:
                        to_rest(i, 1 - slot).wait()
                    if ni >= 2:
                        to_rest(i, slot).wait()

    row = pl.BlockSpec((tm, D_MODEL), lambda i, j: (i, 0))
    vec = pl.BlockSpec((1, D_MODEL), lambda i, j: (0, 0))
    act = pl.BlockSpec((None, tm, fj), lambda i, j: (j, i, 0))
    actshape = jax.ShapeDtypeStruct((nj, t, fj), BF16)
    rowf, rowb, vecf = (jax.ShapeDtypeStruct((t, D_MODEL), F32), jax.ShapeDtypeStruct((t, D_MODEL), BF16),
                        jax.ShapeDtypeStruct((1, D_MODEL), F32))
    if have_df:
        out_specs, out_shape = [row, act, act, vec], [rowf, actshape, actshape, vecf]
    else:
        out_specs, out_shape = [row, row, act, act, vec, vec], [rowf, rowb, actshape, actshape, vecf, vecf]
    scratch = [pltpu.VMEM((tm, D_MODEL), F32)]
    if split_first_block:
        out_specs = [pl.BlockSpec((BLK, D_MODEL), lambda i, j: (0, 0)), ANY] + out_specs[1:]
        out_shape = [jax.ShapeDtypeStruct((BLK, D_MODEL), F32), jax.ShapeDtypeStruct((t - BLK, D_MODEL), F32)
                     ] + out_shape[1:]
        scratch += [pltpu.VMEM((2, tm, D_MODEL), F32), pltpu.SemaphoreType.DMA((2,)), pltpu.SemaphoreType.DMA]
    return _pallas(
        body, name="ffn_bwd", grid=(ni, nj),
        in_specs=[row, row, row, act, act, vec] + wspecs + [vec],
        out_specs=out_specs, out_shape=out_shape, scratch_shapes=scratch,
        args=(dhout, h, df if have_df else f, p14, p24, wpre, w3, w3, w3, wpost))


def _ffn_wgrad(n, df, dg4, du4, a4, hook=None):
    t = n.shape[0]
    tm = _contract_tile(t)
    ni = t // tm
    nj, _, fj = dg4.shape

    def body(n_ref, df_ref, dg_ref, du_ref, a_ref, dw_ref, acc):
        i = pl.program_id(1)

        @pl.when(i == 0)
        def _():
            acc[...] = jnp.zeros_like(acc)

        nn = n_ref[...]
        acc[0:fj, :] += _dg(dg_ref[...], nn, TN)
        acc[fj:2 * fj, :] += _dg(du_ref[...], nn, TN)
        acc[2 * fj:3 * fj, :] += _dg(a_ref[...], df_ref[...], TN)

        @pl.when(i == ni - 1)
        def _():
            dw_ref[...] = acc[...].astype(BF16)

    row = pl.BlockSpec((tm, D_MODEL), lambda j, i: (i, 0))
    act = pl.BlockSpec((None, tm, fj), lambda j, i: (j, i, 0))
    return _pallas(
        body, name="ffn_wgrad", grid=(nj, ni),
        in_specs=[row, row, act, act, act],
        out_specs=[pl.BlockSpec((None, 3 * fj, D_MODEL), lambda j, i: (j, 0, 0))],
        out_shape=[jax.ShapeDtypeStruct((nj, 3 * fj, D_MODEL), BF16)],
        scratch_shapes=[pltpu.VMEM((3 * fj, D_MODEL), F32)],
        args=(n, df, dg4, du4, a4), hook=hook)


def _embed_norm(x, meta_buf, w):
    t = x.shape[0] + BLK
    tm = _row_tile(t)
    nblk = tm // BLK
    ni = t // tm
    gather = _GatherChips([meta_buf])

    def body(*refs):
        x_refs = refs[:nblk]
        w_ref, mb_in, h_ref, n_ref, mb_out, mv, msem, send, recv = refs[nblk:]
        step = pl.program_id(0)
        tile = (step + 1) % ni
        hook_refs = ([mb_in], [mb_out], [send, recv])
        steps = (0, 1, ni - 2) if ni >= 3 else (0, 0, 0)
        for at, phase in zip(steps, (gather.start, gather.mid, gather.finish)):
            @pl.when(step == at)
            def _(phase=phase):
                phase(*hook_refs)

        @pl.when(step == 0)
        def _():
            mv[...] = jnp.zeros_like(mv)

        @pl.when(step == ni - 1)
        def _():
            cp = pltpu.make_async_copy(mb_out, mv, msem)
            cp.start()
            cp.wait()

        meta = jnp.concatenate([mv[k] for k in range(N_CHIPS)], axis=1)
        first = jnp.concatenate([jnp.zeros((PAD, D_MODEL), F32), meta], axis=0)
        blocks = [jnp.where(tile == 0, first, x_refs[0][...])] + [r[...] for r in x_refs[1:]]
        h = jnp.concatenate(blocks, axis=0) if nblk > 1 else blocks[0]
        h_ref[...] = h
        y, _, _ = _rms(h, w_ref[...])
        n_ref[...] = y.astype(BF16)

    x_specs = [pl.BlockSpec((BLK, D_MODEL), functools.partial(
        lambda i, k: (jnp.maximum(nblk * ((i + 1) % ni) + k - 1, 0), 0), k=k)) for k in range(nblk)]
    row = pl.BlockSpec((tm, D_MODEL), lambda i: ((i + 1) % ni, 0))
    h0, n0, _ = pl.pallas_call(
        body, name="embed_norm", grid=(ni,),
        in_specs=x_specs + [_full((1, D_MODEL)), ANY], out_specs=[row, row, ANY],
        out_shape=[jax.ShapeDtypeStruct((t, D_MODEL), F32), jax.ShapeDtypeStruct((t, D_MODEL), BF16),
                   jax.ShapeDtypeStruct(meta_buf.shape, meta_buf.dtype)],
        scratch_shapes=[pltpu.VMEM(meta_buf.shape, meta_buf.dtype), pltpu.SemaphoreType.DMA] + list(gather.scratch),
        input_output_aliases={nblk + 1: 2},
        compiler_params=_cparams(1),
    )(*([x] * nblk), w, meta_buf)
    return h0, n0


FWD_RELATION = (None, 0, 1, 2)


def _ffn_fwd_gather(h, n, wbuf, wpost, qc_idx, late):
    t = h.shape[0]
    tm = _row_tile(t)
    ni = t // tm
    nj, rows3, _ = wbuf.shape
    fj = rows3 // 3
    assert nj == N_CHIPS and ni >= 4
    wbufs = [wbuf]
    nw = 1
    n_lin, n_lout = len(late.inputs), len(late.out_shape)
    wait_step = ni - 3

    def body(qc_ref, h_ref, n_ref, wpost_ref, *rest):
        wb_in = rest[:nw]
        lins = rest[nw:nw + n_lin]
        o0 = nw + n_lin
        hout_ref, p1_ref, p2_ref, a_ref, f_hbm = rest[o0:o0 + 5]
        wb = rest[o0 + 5:o0 + 5 + nw]
        louts = rest[o0 + 5 + nw:o0 + 5 + nw + n_lout]
        s0 = o0 + 5 + nw + n_lout
        wv, wsem, send, recv, fbuf, fr_sem, fw_sem = rest[s0:s0 + 7]
        lscr = rest[s0 + 7:]
        p = pl.program_id(0)
        i = pl.program_id(1)
        step = p * ni + i
        fslot = step % 3
        nslot = (step + 1) % 3

        def f_tile(tile):
            return f_hbm.at[pl.ds(pl.multiple_of(tile * tm, 8), tm)]

        @pl.when(step > 1)
        def _():
            pltpu.make_async_copy(fbuf.at[nslot], f_tile(i), fw_sem.at[nslot]).wait()

        nxt = step + 1

        @pl.when((nxt < N_CHIPS * ni) & (nxt >= ni))
        def _():
            pltpu.make_async_copy(f_tile(nxt % ni), fbuf.at[nslot], fr_sem.at[nslot]).start()

        @pl.when(p > 0)
        def _():
            pltpu.make_async_copy(f_tile(i), fbuf.at[fslot], fr_sem.at[fslot]).wait()
        x, y, c, chips = _place()
        q = 2 * x + y
        sibling = (x, y, 1 - c)
        mine, other = _half(rows3, c), _half(rows3, 1 - c)

        def load(chunk, slot, src):
            return [pltpu.make_async_copy(src[t].at[chunk], wv.at[slot, t], wsem.at[slot, t]) for t in range(nw)]

        @pl.when((p == 0) & (i == 0))
        def _():
            for j, (cx, cy) in enumerate(chips):
                for t in range(nw):
                    _remote(send.at[t, j], recv.at[t, j], wb_in[t].at[q, mine], wb[t].at[q, mine], (cx, cy, c)).start()
            for cp in load(q, 0, wb_in):
                cp.start()
            for cp in load(q, 0, wb_in):
                cp.wait()

        @pl.when((p == 1) & (i == 0))
        def _():
            late.start(lins, louts, lscr)

        for pp in range(1, N_CHIPS):
            j = FWD_RELATION[pp]
            cx, cy = chips[j]
            chunk = 2 * cx + cy

            @pl.when((p == pp - 1) & (i == wait_step))
            def _(j=j, cx=cx, cy=cy, chunk=chunk, pp=pp):
                for t in range(nw):
                    got = wb[t].at[chunk, mine]
                    _remote(send.at[t, j], recv.at[t, j], got, got, (cx, cy, c)).wait_recv()
                    _remote(send.at[t, 3 + j], recv.at[t, 3 + j], got, got, sibling).start()
                for t in range(nw):
                    rest_half = wb[t].at[chunk, other]
                    _remote(send.at[t, 3 + j], recv.at[t, 3 + j], rest_half, rest_half, sibling).wait_recv()
                for cp in load(chunk, pp % 2, wb):
                    cp.start()

            @pl.when((p == pp) & (i == 0))
            def _(chunk=chunk, pp=pp):
                for cp in load(chunk, pp % 2, wb):
                    cp.wait()

        @pl.when((p == N_CHIPS - 1) & (i == ni // 2))
        def _():
            late.mid(lins, louts, lscr)

        slot = p % 2
        nn = n_ref[...]
        g = _dg(nn, wv[slot, 0, 0:fj], NT)
        u = _dg(nn, wv[slot, 0, fj:2 * fj], NT)
        sg = _sigmoid(g)
        silu = g * sg
        p1_ref[...] = (u * (sg + silu * (1.0 - sg))).astype(BF16)
        p2_ref[...] = silu.astype(BF16)
        a = (silu * u).astype(BF16)
        a_ref[...] = a
        part = _dot(a, wv[slot, 0, 2 * fj:3 * fj])

        @pl.when(p == 0)
        def _():
            fbuf[fslot] = part

        @pl.when(p > 0)
        def _():
            fbuf[fslot] = fbuf[fslot] + part

        pltpu.make_async_copy(fbuf.at[fslot], f_tile(i), fw_sem.at[fslot]).start()

        @pl.when(p == N_CHIPS - 1)
        def _():
            yv, _, _ = _rms(fbuf[fslot], wpost_ref[...])
            hout_ref[...] = h_ref[...] + 0.5 * yv

        @pl.when((p == N_CHIPS - 1) & (i == ni - 1))
        def _():
            pslot = (step + 2) % 3
            pltpu.make_async_copy(fbuf.at[pslot], f_tile(i), fw_sem.at[pslot]).wait()
            pltpu.make_async_copy(fbuf.at[fslot], f_tile(i), fw_sem.at[fslot]).wait()
            for t in range(nw):
                for j, (cx, cy) in enumerate(chips):
                    sent = wb[t].at[2 * cx + cy, mine]
                    _remote(send.at[t, j], recv.at[t, j], sent, sent, (cx, cy, c)).wait_send()
                    _remote(send.at[t, 3 + j], recv.at[t, 3 + j], sent, sent, sibling).wait_send()
            late.finish(lins, louts, lscr)

    def last_pass_rows(p, i, qc_ref):
        return (jnp.where(p == N_CHIPS - 1, i, 0), 0)

    def chunk_rows(p, i, qc_ref):
        order = ((p & 1) << 1) | (p >> 1)
        return (jnp.bitwise_xor(qc_ref[0], order), i, 0)

    row = pl.BlockSpec((tm, D_MODEL), lambda p, i, qc_ref: (i, 0))
    last_row = pl.BlockSpec((tm, D_MODEL), last_pass_rows)
    act = pl.BlockSpec((None, tm, fj), chunk_rows)
    act_shape = jax.ShapeDtypeStruct((nj, t, fj), BF16)
    res = pl.pallas_call(
        body, name="ffn_fwd_gather",
        grid_spec=pltpu.PrefetchScalarGridSpec(
            num_scalar_prefetch=1, grid=(N_CHIPS, ni),
            in_specs=[last_row, row, pl.BlockSpec((1, D_MODEL), lambda p, i, qc_ref: (0, 0))]
            + [ANY] * (nw + n_lin),
            out_specs=[last_row, act, act, act, ANY] + [ANY] * (nw + n_lout),
            scratch_shapes=[pltpu.VMEM((2, nw, rows3, D_MODEL), BF16), pltpu.SemaphoreType.DMA((2, nw)),
                            pltpu.SemaphoreType.DMA((nw, 6)), pltpu.SemaphoreType.DMA((nw, 6)),
                            pltpu.VMEM((3, tm, D_MODEL), F32), pltpu.SemaphoreType.DMA((3,)),
                            pltpu.SemaphoreType.DMA((3,))] + list(late.scratch)),
        out_shape=[jax.ShapeDtypeStruct((t, D_MODEL), F32), act_shape, act_shape, act_shape,
                   jax.ShapeDtypeStruct((t, D_MODEL), F32)]
        + [jax.ShapeDtypeStruct(b.shape, b.dtype) for b in wbufs] + list(late.out_shape),
        input_output_aliases={**{4 + t: 5 + t for t in range(nw)},
                              **{4 + nw + a: 5 + nw + b for a, b in late.aliases}},
        compiler_params=_cparams(2),
    )(qc_idx, h, n, wpost, *wbufs, *late.inputs)
    return res[:5], res[5:5 + nw], res[5 + nw:]


PASS_RELATION = (2, 0, 1)


def _ffn_wgrad_reduce(n, df, dg4, du4, a4, qc_idx, hook):
    t = n.shape[0]
    tm = _contract_tile(t)
    ni = t // tm
    nj, _, fj = dg4.shape
    assert nj == N_CHIPS
    hrows = 3 * fj // 2
    n_hin, n_hout = len(hook.inputs), len(hook.out_shape)

    def body(qc_ref, n_ref, df_ref, dg_ref, du_ref, a_ref, *rest):
        hins = rest[:n_hin]
        own_ref, others_ref = rest[n_hin:n_hin + 2]
        houts = rest[n_hin + 2:n_hin + 2 + n_hout]
        s0 = n_hin + 2 + n_hout
        acc, stage, land, sumbuf, px_send, px_recv, cs_send, cs_recv, own_sem = rest[s0:s0 + 9]
        hscr = rest[s0 + 9:]
        k_pass = pl.program_id(0)
        i = pl.program_id(1)
        x, y, c, chips = _place()
        mine = pl.ds(pl.multiple_of(c * hrows, 8), hrows)
        other = pl.ds(pl.multiple_of((1 - c) * hrows, 8), hrows)

        def to_owner(k):
            j = PASS_RELATION[k]
            return _remote(cs_send.at[j], cs_recv.at[j], sumbuf.at[k % 2], others_ref.at[j], (*chips[j], c))

        @pl.when((k_pass == 0) & (i == 0))
        def _():
            hook.start(hins, houts, hscr)

        if hook.has_mid:
            @pl.when((k_pass == N_CHIPS // 2) & (i == 0))
            def _():
                hook.mid(hins, houts, hscr)

        @pl.when(i == 0)
        def _():
            acc[...] = jnp.zeros_like(acc)

        nn = n_ref[...]
        acc[0:fj, :] += _dg(dg_ref[...], nn, TN)
        acc[fj:2 * fj, :] += _dg(du_ref[...], nn, TN)
        acc[2 * fj:3 * fj, :] += _dg(a_ref[...], df_ref[...], TN)

        for k in range(N_CHIPS):
            @pl.when((k_pass == k) & (i == ni - 1))
            def _(k=k):
                slot = k % 2
                stage[...] = acc[other, :].astype(BF16)
                swap = _remote(px_send.at[k], px_recv.at[k], stage, land.at[slot], (x, y, 1 - c))
                swap.start()
                swap.wait_recv()
                pair = acc[mine, :] + land[slot].astype(F32)
                if k >= 2:
                    to_owner(k - 2).wait_send()
                sumbuf[slot] = pair.astype(BF16)
                swap.wait_send()
                if k < N_CHIPS - 1:
                    to_owner(k).start()
                else:
                    keep = pltpu.make_async_copy(sumbuf.at[slot], own_ref, own_sem)
                    keep.start()
                    for j in range(N_CHIPS - 1):
                        _remote(cs_send.at[j], cs_recv.at[j], sumbuf.at[0], others_ref.at[j], (*chips[j], c)).wait_recv()
                    to_owner(k - 1).wait_send()
                    keep.wait()
                    hook.finish(hins, houts, hscr)

    def chunk(k_pass, i, qc_ref):
        return (jnp.bitwise_xor(qc_ref[0], N_CHIPS - 1 - k_pass), i, 0)

    row = pl.BlockSpec((tm, D_MODEL), lambda k_pass, i, qc_ref: (i, 0))
    act = pl.BlockSpec((None, tm, fj), chunk)
    res = pl.pallas_call(
        body, name="ffn_wgrad_reduce",
        grid_spec=pltpu.PrefetchScalarGridSpec(
            num_scalar_prefetch=1, grid=(N_CHIPS, ni),
            in_specs=[row, row, act, act, act] + [ANY] * n_hin,
            out_specs=[ANY, ANY] + [ANY] * n_hout,
            scratch_shapes=[pltpu.VMEM((3 * fj, D_MODEL), F32), pltpu.VMEM((hrows, D_MODEL), BF16),
                            pltpu.VMEM((2, hrows, D_MODEL), BF16), pltpu.VMEM((2, hrows, D_MODEL), BF16),
                            pltpu.SemaphoreType.DMA((N_CHIPS,)), pltpu.SemaphoreType.DMA((N_CHIPS,)),
                            pltpu.SemaphoreType.DMA((N_CHIPS - 1,)), pltpu.SemaphoreType.DMA((N_CHIPS - 1,)),
                            pltpu.SemaphoreType.DMA] + list(hook.scratch)),
        out_shape=[jax.ShapeDtypeStruct((hrows, D_MODEL), BF16),
                   jax.ShapeDtypeStruct((N_CHIPS - 1, hrows, D_MODEL), BF16)] + list(hook.out_shape),
        compiler_params=_cparams(2),
    )(qc_idx, n, df, dg4, du4, a4, *hook.inputs)
    return res[0], res[1], res[2:]


def _xty(x, y):
    t, k = x.shape
    n = y.shape[1]
    tm = _contract_tile(t)
    tn = n if n <= 1024 else (896 if n % 896 == 0 else 128)
    steps = t // tm

    def body(x_ref, y_ref, o_ref, acc_ref):
        i = pl.program_id(1)
        part = _dg(x_ref[...], y_ref[...], TN)

        @pl.when(i == 0)
        def _():
            acc_ref[...] = part

        @pl.when(jnp.logical_and(i > 0, i < steps - 1))
        def _():
            acc_ref[...] += part

        @pl.when(i == steps - 1)
        def _():
            o_ref[...] = (acc_ref[...] + part).astype(BF16)

    assert steps > 1
    return pl.pallas_call(
        body, name="xty", grid=(n // tn, steps),
        in_specs=[pl.BlockSpec((tm, k), lambda j, i: (i, 0)), pl.BlockSpec((tm, tn), lambda j, i: (i, j))],
        out_specs=pl.BlockSpec((k, tn), lambda j, i: (0, j)),
        out_shape=jax.ShapeDtypeStruct((k, n), BF16),
        scratch_shapes=[pltpu.VMEM((k, tn), F32)],
        compiler_params=_cparams(2),
    )(x, y)


def _rope_tables(t):
    pos = (jnp.arange(t, dtype=jnp.int32) - PAD).astype(F32)
    inv_freq = 1.0 / (ROPE_THETA ** (jnp.arange(0, SWA_HD, 2, dtype=F32) / SWA_HD))
    half = SWA_HD // 2
    ang = pos[:, None] * jnp.tile(inv_freq, 4)[None, :]
    sign = jnp.tile(jnp.concatenate([-jnp.ones((half,), F32), jnp.ones((half,), F32)]), 2)
    return jnp.cos(ang), jnp.sin(ang) * sign[None, :]


def _rot_half(x, first_half):
    return jnp.where(first_half, pltpu.roll(x, 96, 1), pltpu.roll(x, 32, 1))


def _first_half_mask(rows):
    lane = lax.broadcasted_iota(jnp.int32, (rows, 128), 1)
    return (lane % 64) < 32


def _log_sigmoid(z):
    return jnp.minimum(z, 0.0) - jnp.log(1.0 + jnp.exp(-jnp.abs(z)))


def _mix_proj(h1, wmixpre, winp, wa2p, bap, cos, sin):
    t = h1.shape[0]
    tm = _row_tile(t)

    def body(h_ref, w_ref, win_ref, wa2_ref, ba_ref, cos_ref, sin_ref,
             n_ref, gq_ref, gk_ref, gv_ref, gg_ref, ga_ref, la_ref, sq_ref, sk_ref, sv_ref):
        y, _, _ = _rms(h_ref[...], w_ref[...])
        n = y.astype(BF16)
        n_ref[...] = n
        proj = _dot(n, win_ref[...])
        gq_ref[...] = proj[:, P_GQ:P_GK]
        gk_ref[...] = proj[:, P_GK:P_GV]
        gv_ref[...] = proj[:, P_GV:P_GG]
        gg_ref[...] = proj[:, P_GG:P_GA]
        ga = proj[:, P_GA:P_SQ]
        ga_ref[...] = ga
        z = _dot(ga.astype(BF16), wa2_ref[...]) + ba_ref[...]
        la_ref[...] = _log_sigmoid(z) * (1.0 / GLA_TAU)
        c = cos_ref[...]
        s = sin_ref[...]
        fh = _first_half_mask(tm)
        for k in range(4):
            x = proj[:, P_SQ + 128 * k:P_SQ + 128 * (k + 1)]
            sq_ref[:, 128 * k:128 * (k + 1)] = (x * c + _rot_half(x, fh) * s).astype(BF16)
        for k in range(2):
            x = proj[:, P_SK + 128 * k:P_SK + 128 * (k + 1)]
            sk_ref[:, 128 * k:128 * (k + 1)] = (x * c + _rot_half(x, fh) * s).astype(BF16)
        sv_ref[...] = proj[:, P_SV:P_END].astype(BF16)

    def row(w):
        return pl.BlockSpec((tm, w), lambda i: (i, 0))

    def rshape(w, dt):
        return jax.ShapeDtypeStruct((t, w), dt)

    return pl.pallas_call(
        body, name="mix_proj", grid=(t // tm,),
        in_specs=[row(D_MODEL), _full((1, D_MODEL)), _full((D_MODEL, P_END)), _full((128, GLA_KW)),
                  _full((1, GLA_KW)), row(128), row(128)],
        out_specs=[row(D_MODEL), row(256), row(256), row(512), row(512), row(128), row(256), row(512), row(256),
                   row(256)],
        out_shape=[rshape(D_MODEL, BF16), rshape(256, F32), rshape(256, F32), rshape(512, F32), rshape(512, F32),
                   rshape(128, F32), rshape(256, F32), rshape(512, BF16), rshape(256, BF16), rshape(256, BF16)],
        compiler_params=_cparams(1),
    )(h1, wmixpre, winp, wa2p, bap, cos, sin)


def _scan_rows(x, reverse=False):
    n = x.shape[0]
    row = lax.broadcasted_iota(jnp.int32, x.shape, 0)
    s = 1
    while s < n:
        if reverse:
            x = x + jnp.where(row < n - s, pltpu.roll(x, n - s, 0), 0.0)
        else:
            x = x + jnp.where(row >= s, pltpu.roll(x, s, 0), 0.0)
        s *= 2
    return x


def _gla_cumsum(la, tril_f):
    b = _scan_rows(la)
    row = lax.broadcasted_iota(jnp.int32, b.shape, 0)
    bm = jnp.sum(jnp.where(row == GLA_CHUNK // 2 - 1, b, 0.0), axis=0, keepdims=True)
    bl = jnp.sum(jnp.where(row == GLA_CHUNK - 1, b, 0.0), axis=0, keepdims=True)
    return b, bm, bl


def _gla_decays(la, tril_f):
    b, bm, bl = _gla_cumsum(la, tril_f)
    return jnp.exp(b - bm), jnp.exp(bm - b), jnp.exp(b), jnp.exp(bl - b), jnp.exp(bl)


def _gla_masks():
    c = GLA_CHUNK
    r = lax.broadcasted_iota(jnp.int32, (c, c), 0)
    col = lax.broadcasted_iota(jnp.int32, (c, c), 1)
    r4 = lax.broadcasted_iota(jnp.int32, (GLA_HEADS * c, c), 0) % c
    c4 = lax.broadcasted_iota(jnp.int32, (GLA_HEADS * c, c), 1)
    klane = lax.broadcasted_iota(jnp.int32, (c, GLA_KW), 1) // GLA_DK
    vlane = lax.broadcasted_iota(jnp.int32, (c, GLA_W), 1) // GLA_DV
    srow = lax.broadcasted_iota(jnp.int32, (GLA_W, GLA_KW), 0) // GLA_DV
    scol = lax.broadcasted_iota(jnp.int32, (GLA_W, GLA_KW), 1) // GLA_DK
    return dict(tril_f=(r >= col).astype(F32), triu_f=(r <= col).astype(F32), tril4=r4 >= c4,
                khead=[klane == h for h in range(GLA_HEADS)], vhead=[vlane == h for h in range(GLA_HEADS)],
                diag=srow == scol)


def _stack_heads(x, head_masks):
    return jnp.concatenate([jnp.where(m, x, 0.0) for m in head_masks], axis=0)


def _gla_fwd(gq, gk, gv, la):
    t = gq.shape[0]
    rg = _seq_tile(t)
    nb = t // rg
    ncb = rg // GLA_CHUNK
    c = GLA_CHUNK

    def body(q_ref, k_ref, v_ref, la_ref, o_ref, ss_ref, st_ref):
        @pl.when(pl.program_id(0) == 0)
        def _():
            st_ref[...] = jnp.zeros_like(st_ref)

        mk = _gla_masks()
        st = st_ref[...]
        for ch in range(ncb):
            rows = slice(ch * c, (ch + 1) * c)
            eq, ek, eb, ekl, ebl = _gla_decays(la_ref[rows, :], mk["tril_f"])
            qs = q_ref[rows, :] * (GLA_DK ** -0.5)
            k = k_ref[rows, :]
            v = v_ref[rows, :].astype(BF16)
            ss_ref[ch] = st
            q4 = _stack_heads(qs * eq, mk["khead"]).astype(BF16)
            a4 = jnp.where(mk["tril4"], _dg(q4, (k * ek).astype(BF16), NT), 0.0).astype(BF16)
            r4 = _dot(a4, v)
            intra = jnp.concatenate([r4[h * c:(h + 1) * c, GLA_DV * h:GLA_DV * (h + 1)] for h in range(GLA_HEADS)],
                                    axis=1)
            o_ref[rows, :] = intra + _dg((qs * eb).astype(BF16), st.astype(BF16), NT)
            st = st * ebl + jnp.where(mk["diag"], _dg(v, (k * ekl).astype(BF16), TN), 0.0)
        st_ref[...] = st

    def row(w):
        return pl.BlockSpec((rg, w), lambda i: (i, 0))

    return pl.pallas_call(
        body, name="gla_fwd", grid=(nb,),
        in_specs=[row(256), row(256), row(512), row(256)],
        out_specs=[row(512), pl.BlockSpec((ncb, GLA_W, GLA_KW), lambda i: (i, 0, 0))],
        out_shape=[jax.ShapeDtypeStruct((t, GLA_W), F32), jax.ShapeDtypeStruct((nb * ncb, GLA_W, GLA_KW), F32)],
        scratch_shapes=[pltpu.VMEM((GLA_W, GLA_KW), F32)],
        compiler_params=_cparams(1),
    )(gq, gk, gv, la)


def _gla_bwd(gq, gk, gv, la, ss, do):
    t = gq.shape[0]
    rg = _seq_tile(t)
    nb = t // rg
    ncb = rg // GLA_CHUNK
    c = GLA_CHUNK

    def body(q_ref, k_ref, v_ref, la_ref, ss_ref, do_ref, dq_ref, dk_ref, dv_ref, dla_ref, dst_ref):
        @pl.when(pl.program_id(0) == 0)
        def _():
            dst_ref[...] = jnp.zeros_like(dst_ref)

        mk = _gla_masks()
        last_row = lax.broadcasted_iota(jnp.int32, (c, GLA_KW), 0) == c - 1
        scale = GLA_DK ** -0.5
        dstn = dst_ref[...]
        for ch in reversed(range(ncb)):
            rows = slice(ch * c, (ch + 1) * c)
            eq, ek, eb, ekl, ebl = _gla_decays(la_ref[rows, :], mk["tril_f"])
            qs = q_ref[rows, :] * scale
            k = k_ref[rows, :]
            qt, kt, qh, kh = qs * eq, k * ek, qs * eb, k * ekl
            ktb, khb, qhb = kt.astype(BF16), kh.astype(BF16), qh.astype(BF16)
            v = v_ref[rows, :].astype(BF16)
            do_f = do_ref[rows, :]
            dob = do_f.astype(BF16)
            st = ss_ref[ch]
            stb = st.astype(BF16)
            dstb = dstn.astype(BF16)
            q4 = _stack_heads(qt, mk["khead"]).astype(BF16)
            do4 = _stack_heads(do_f, mk["vhead"]).astype(BF16)
            a4 = jnp.where(mk["tril4"], _dg(q4, ktb, NT), 0.0).astype(BF16)
            da4 = jnp.where(mk["tril4"], _dg(do4, v, NT), 0.0).astype(BF16)
            dv_ref[rows, :] = _dg(a4, do4, TN) + _dg(khb, dstb, NT)
            dq4 = _dot(da4, ktb)
            dqt = jnp.zeros((c, GLA_KW), F32)
            for h in range(GLA_HEADS):
                dqt = dqt + jnp.where(mk["khead"][h], dq4[h * c:(h + 1) * c], 0.0)
            dkt = _dg(da4, q4, TN)
            dqh = _dot(dob, stb)
            dkh = _dot(v, dstb)
            dbl = jnp.sum(dstn * st, axis=0, keepdims=True)
            dstn = dstn * ebl + jnp.where(mk["diag"], _dg(dob, qhb, TN), 0.0)
            dq_ref[rows, :] = scale * (dqt * eq + dqh * eb)
            dk_ref[rows, :] = dkt * ek + dkh * ekl
            dkk = dkh * kh
            db = dqt * qt - dkt * kt + dqh * qh - dkk
            db = db + jnp.where(last_row, jnp.sum(dkk, axis=0, keepdims=True) + ebl * dbl, 0.0)
            dla_ref[rows, :] = _scan_rows(db, reverse=True)
        dst_ref[...] = dstn

    def row(w):
        return pl.BlockSpec((rg, w), lambda i: (nb - 1 - i, 0))

    def rshape(w):
        return jax.ShapeDtypeStruct((t, w), F32)

    return pl.pallas_call(
        body, name="gla_bwd", grid=(nb,),
        in_specs=[row(256), row(256), row(512), row(256),
                  pl.BlockSpec((ncb, GLA_W, GLA_KW), lambda i: (nb - 1 - i, 0, 0)), row(512)],
        out_specs=[row(256), row(256), row(512), row(256)],
        out_shape=[rshape(256), rshape(256), rshape(512), rshape(256)],
        scratch_shapes=[pltpu.VMEM((GLA_W, GLA_KW), F32)],
        compiler_params=_cparams(1),
    )(gq, gk, gv, la, ss, do)


SWA_G = SWA_QH // SWA_KVH


def _swa_bias():
    n = jnp.arange(3, dtype=jnp.int32)[:, None, None]
    r = (jnp.arange(SWA_G * BLK, dtype=jnp.int32) % BLK)[None, :, None]
    c = jnp.arange(3 * BLK, dtype=jnp.int32)[None, None, :]
    seg = c // BLK
    cc = c % BLK
    qpos = n * BLK + r - PAD
    kpos = jnp.where(seg == 0, (n - 1) * BLK, jnp.where(seg == 1, n * BLK, 0)) + cc - PAD
    band = (seg < 2) & (kpos >= N_META) & (kpos <= qpos) & (qpos - kpos < WINDOW)
    meta = (seg == 2) & (kpos >= 0) & (kpos < N_META) & (kpos <= qpos)
    return jnp.where(band | meta, 0.0, NEG_INF).astype(F32)


def _swa_stack(ref, rows, kh, lo, dtype):
    parts = []
    for g in range(2):
        pair = ref[rows, 128 * (2 * kh + g):128 * (2 * kh + g + 1)]
        zero = jnp.zeros_like(pair)
        parts += [jnp.where(lo, pair, zero), jnp.where(lo, zero, pair)]
    return jnp.concatenate(parts, axis=0).astype(dtype)


def _swa_unstack(x4, lo):
    return [jnp.where(lo, x4[2 * g * BLK:(2 * g + 1) * BLK], x4[(2 * g + 1) * BLK:(2 * g + 2) * BLK])
            for g in range(2)]


def _swa_sink_col(sink_ref, kh):
    blk = lax.broadcasted_iota(jnp.int32, (SWA_G * BLK, 1), 0) // BLK
    col = jnp.full((SWA_G * BLK, 1), sink_ref[SWA_G * kh + SWA_G - 1], F32)
    for e in reversed(range(SWA_G - 1)):
        col = jnp.where(blk == e, sink_ref[SWA_G * kh + e], col)
    return col


def _swa_softmax(qk, bias, sink):
    s = qk * (SWA_HD ** -0.5) + bias
    m = jnp.maximum(jnp.max(s, axis=-1, keepdims=True), sink)
    p = jnp.exp(s - m)
    es = jnp.exp(sink - m)
    inv = 1.0 / (jnp.sum(p, axis=-1, keepdims=True) + es)
    return p * inv, es * inv


def _swa_keys(prev_ref, cur_ref, first_ref, b, ls):
    before = prev_ref[:, ls] if b == 0 else cur_ref[(b - 1) * BLK:b * BLK, ls]
    return jnp.concatenate([before, cur_ref[b * BLK:(b + 1) * BLK, ls], first_ref[:, ls]], axis=0)


def _swa_specs(rs, ns):
    bps = rs // BLK
    cur = lambda w: pl.BlockSpec((rs, w), lambda i: (jnp.minimum(i, ns - 1), 0))
    prev = lambda w: pl.BlockSpec((BLK, w), lambda i: (jnp.maximum(jnp.minimum(i, ns - 1) * bps - 1, 0), 0))
    first = lambda w: pl.BlockSpec((BLK, w), lambda i: (0, 0))
    return cur, prev, first


def _swa_fwd(sinks, sq, sk, sv):
    t = sq.shape[0]
    rs = _seq_tile(t)
    bps, ns = rs // BLK, t // rs

    def body(sink_ref, bias_ref, q_ref, kp_ref, kc_ref, km_ref, vp_ref, vc_ref, vm_ref, o_ref):
        i = pl.program_id(0)
        lo = lax.broadcasted_iota(jnp.int32, (BLK, 128), 1) < 64
        sink_cols = [_swa_sink_col(sink_ref, kh) for kh in range(SWA_KVH)]
        chains = [(b, kh) for b in range(bps) for kh in range(SWA_KVH)]
        scores = []
        for b, kh in chains:
            ls = slice(128 * kh, 128 * (kh + 1))
            q4 = _swa_stack(q_ref, slice(b * BLK, (b + 1) * BLK), kh, lo, BF16)
            scores.append(_dg(q4, _swa_keys(kp_ref, kc_ref, km_ref, b, ls), NT))
        probs = []
        for (b, kh), s in zip(chains, scores):
            p, _ = _swa_softmax(s, bias_ref[jnp.minimum(i * bps + b, 2)], sink_cols[kh])
            probs.append(p.astype(BF16))
        for (b, kh), p in zip(chains, probs):
            ls = slice(128 * kh, 128 * (kh + 1))
            rows = slice(b * BLK, (b + 1) * BLK)
            for g, pair in enumerate(_swa_unstack(_dot(p, _swa_keys(vp_ref, vc_ref, vm_ref, b, ls)), lo)):
                o_ref[rows, 128 * (2 * kh + g):128 * (2 * kh + g + 1)] = pair

    cur, prev, first = _swa_specs(rs, ns)
    bias = _swa_bias()
    return pl.pallas_call(
        body, name="swa_fwd", grid=(ns,),
        in_specs=[pl.BlockSpec(memory_space=pltpu.SMEM), _full(bias.shape), cur(512), prev(256), cur(256), first(256),
                  prev(256), cur(256), first(256)],
        out_specs=cur(512),
        out_shape=jax.ShapeDtypeStruct((t, SWA_W), F32),
        compiler_params=_cparams(1),
    )(sinks, bias, sq, sk, sk, sk, sv, sv, sv)


def _swa_bwd(sinks, sq, sk, sv, o, do, hook=None):
    t = sq.shape[0]
    rs = _seq_tile(t)
    bps, ns = rs // BLK, t // rs

    def body(sink_ref, bias_ref, q_ref, kp_ref, kc_ref, km_ref, vp_ref, vc_ref, vm_ref, o_ref, do_ref,
             dq_ref, dk_ref, dv_ref, dkm_ref, dvm_ref, dsink_ref, pk_ref, pv_ref):
        i = pl.program_id(0)

        @pl.when(i == 0)
        def _():
            pk_ref[...] = jnp.zeros_like(pk_ref)
            pv_ref[...] = jnp.zeros_like(pv_ref)
            dkm_ref[...] = jnp.zeros_like(dkm_ref)
            dvm_ref[...] = jnp.zeros_like(dvm_ref)
            dsink_ref[...] = jnp.zeros_like(dsink_ref)

        @pl.when(i == ns)
        def _():
            dk_ref[...] = pk_ref[...]
            dv_ref[...] = pv_ref[...]

        @pl.when(i < ns)
        def _():
            lo = lax.broadcasted_iota(jnp.int32, (BLK, 128), 1) < 64
            scale = SWA_HD ** -0.5
            sink_cols = [_swa_sink_col(sink_ref, kh) for kh in range(SWA_KVH)]
            parts_k = [[None] * SWA_KVH for _ in range(bps)]
            parts_v = [[None] * SWA_KVH for _ in range(bps)]
            dsinks = [jnp.zeros((1, 1), F32) for _ in range(SWA_QH)]
            chains = [(b, kh) for b in range(bps) for kh in range(SWA_KVH)]
            lanes = lambda kh: slice(128 * kh, 128 * (kh + 1))
            block = lambda b: slice(b * BLK, (b + 1) * BLK)
            q4s = [_swa_stack(q_ref, block(b), kh, lo, BF16) for b, kh in chains]
            scores = [_dg(q4, _swa_keys(kp_ref, kc_ref, km_ref, b, lanes(kh)), NT)
                      for (b, kh), q4 in zip(chains, q4s)]
            do4s = [_swa_stack(do_ref, block(b), kh, lo, F32) for b, kh in chains]
            do4bs = [d.astype(BF16) for d in do4s]
            dps = [_dg(d, _swa_keys(vp_ref, vc_ref, vm_ref, b, lanes(kh)), NT) for (b, kh), d in zip(chains, do4bs)]
            pbs, dss = [], []
            for n_chain, (b, kh) in enumerate(chains):
                p, psink = _swa_softmax(scores[n_chain], bias_ref[jnp.minimum(i * bps + b, 2)], sink_cols[kh])
                delta = jnp.sum(do4s[n_chain] * _swa_stack(o_ref, block(b), kh, lo, F32), axis=-1, keepdims=True)
                dss.append((p * (dps[n_chain] - delta) * scale).astype(BF16))
                pbs.append(p.astype(BF16))
                dsk = psink * delta
                for e in range(SWA_G):
                    h = SWA_G * kh + e
                    dsinks[h] = dsinks[h] - jnp.sum(dsk[e * BLK:(e + 1) * BLK], axis=0, keepdims=True)
            for n_chain, (b, kh) in enumerate(chains):
                kall = _swa_keys(kp_ref, kc_ref, km_ref, b, lanes(kh))
                for g, pair in enumerate(_swa_unstack(_dot(dss[n_chain], kall), lo)):
                    dq_ref[block(b), 128 * (2 * kh + g):128 * (2 * kh + g + 1)] = pair
                parts_k[b][kh] = _dg(dss[n_chain], q4s[n_chain], TN)
                parts_v[b][kh] = _dg(pbs[n_chain], do4bs[n_chain], TN)
            last = slice(rs - BLK, rs)
            for parts, out_ref, pend_ref, meta_ref in ((parts_k, dk_ref, pk_ref, dkm_ref),
                                                       (parts_v, dv_ref, pv_ref, dvm_ref)):
                for kh in range(SWA_KVH):
                    ls = slice(128 * kh, 128 * (kh + 1))
                    if bps > 1:
                        out_ref[0:rs - BLK, ls] = pend_ref[0:rs - BLK, ls]
                    out_ref[last, ls] = pend_ref[last, ls] + parts[0][kh][0:BLK]
                    meta = parts[0][kh][2 * BLK:3 * BLK]
                    for b in range(bps):
                        own = parts[b][kh][BLK:2 * BLK]
                        if b + 1 < bps:
                            own = own + parts[b + 1][kh][0:BLK]
                            meta = meta + parts[b + 1][kh][2 * BLK:3 * BLK]
                        pend_ref[b * BLK:(b + 1) * BLK, ls] = own
                    meta_ref[:, ls] += meta
            for h in range(SWA_QH):
                dsink_ref[h:h + 1, :] += jnp.broadcast_to(dsinks[h], (1, 128))

    cur, prev, first = _swa_specs(rs, ns)
    late = lambda w: pl.BlockSpec((rs, w), lambda i: (jnp.maximum(i - 1, 0), 0))
    bias = _swa_bias()
    return _pallas(
        body, name="swa_bwd", grid=(ns + 1,),
        in_specs=[pl.BlockSpec(memory_space=pltpu.SMEM), _full(bias.shape), cur(512), prev(256), cur(256), first(256),
                  prev(256), cur(256), first(256), cur(512), cur(512)],
        out_specs=[cur(512), late(256), late(256), first(256), first(256), _full((SWA_QH, 128))],
        out_shape=[jax.ShapeDtypeStruct((t, SWA_W), F32), jax.ShapeDtypeStruct((t, 256), F32),
                   jax.ShapeDtypeStruct((t, 256), F32), jax.ShapeDtypeStruct((BLK, 256), F32),
                   jax.ShapeDtypeStruct((BLK, 256), F32), jax.ShapeDtypeStruct((SWA_QH, 128), F32)],
        scratch_shapes=[pltpu.VMEM((rs, 256), F32), pltpu.VMEM((rs, 256), F32)],
        args=(sinks, bias, sq, sk, sk, sk, sv, sv, sv, o, do), hook=hook)


def _mix_out(h1, ogla, gg, oswa, wgn, wsn, wout, wpost):
    t = h1.shape[0]
    tm = _row_tile(t)

    def body(h_ref, og_ref, gg_ref, os_ref, wgn_ref, wsn_ref, wout_ref, wpost_ref, h2_ref, cat_ref, m_ref):
        parts = []
        for h in range(GLA_HEADS):
            ls = slice(GLA_DV * h, GLA_DV * (h + 1))
            y, _, _ = _rms(og_ref[:, ls], wgn_ref[...])
            g = gg_ref[:, ls]
            parts.append(y * (g * _sigmoid(g)))
        ys, _, _ = _rms(os_ref[...], wsn_ref[...])
        cat = jnp.concatenate(parts + [ys], axis=1).astype(BF16)
        cat_ref[...] = cat
        m = _dot(cat, wout_ref[...])
        m_ref[...] = m
        y, _, _ = _rms(m, wpost_ref[...])
        h2_ref[...] = h_ref[...] + y

    def row(w):
        return pl.BlockSpec((tm, w), lambda i: (i, 0))

    return pl.pallas_call(
        body, name="mix_out", grid=(t // tm,),
        in_specs=[row(D_MODEL), row(512), row(512), row(512), _full((1, GLA_DV)), _full((1, SWA_W)),
                  _full((D_MODEL, D_MODEL)), _full((1, D_MODEL))],
        out_specs=[row(D_MODEL), row(D_MODEL), row(D_MODEL)],
        out_shape=[jax.ShapeDtypeStruct((t, D_MODEL), F32), jax.ShapeDtypeStruct((t, D_MODEL), BF16),
                   jax.ShapeDtypeStruct((t, D_MODEL), F32)],
        compiler_params=_cparams(1),
    )(h1, ogla, gg, oswa, wgn, wsn, wout, wpost)


def _mix_out_bwd(dh2, m, ogla, gg, oswa, wgn, wsn, wout, wpost, hook=None):
    t = dh2.shape[0]
    tm = _row_tile(t)

    def body(dh_ref, m_ref, og_ref, gg_ref, os_ref, wgn_ref, wsn_ref, wout_ref, wpost_ref,
             dog_ref, dgg_ref, dos_ref, dm_ref, dwpost_ref, dwgn_ref, dwsn_ref):
        @pl.when(pl.program_id(0) == 0)
        def _():
            dwpost_ref[...] = jnp.zeros_like(dwpost_ref)
            dwgn_ref[...] = jnp.zeros_like(dwgn_ref)
            dwsn_ref[...] = jnp.zeros_like(dwsn_ref)

        wpost = wpost_ref[...]
        _, mh, r = _rms(m_ref[...], wpost)
        dm, dw = _rms_bwd(mh, r, wpost, dh_ref[...])
        dwpost_ref[...] += dw
        dmb = dm.astype(BF16)
        dm_ref[...] = dmb
        dcat = _dg(dmb, wout_ref[...], NT)
        wgn = wgn_ref[...]
        for h in range(GLA_HEADS):
            ls = slice(GLA_DV * h, GLA_DV * (h + 1))
            dog = dcat[:, ls]
            g = gg_ref[:, ls]
            sg = _sigmoid(g)
            y, xh, r = _rms(og_ref[:, ls], wgn)
            dgg_ref[:, ls] = dog * y * (sg * (1.0 + g * (1.0 - sg)))
            dx, dw = _rms_bwd(xh, r, wgn, dog * (g * sg))
            dog_ref[:, ls] = dx
            dwgn_ref[...] += dw
        wsn = wsn_ref[...]
        _, xh, r = _rms(os_ref[...], wsn)
        dx, dw = _rms_bwd(xh, r, wsn, dcat[:, GLA_W:])
        dos_ref[...] = dx
        dwsn_ref[...] += dw

    def row(w):
        return pl.BlockSpec((tm, w), lambda i: (i, 0))

    def rshape(w, dt=F32):
        return jax.ShapeDtypeStruct((t, w), dt)

    return _pallas(
        body, name="mix_out_bwd", grid=(t // tm,),
        in_specs=[row(D_MODEL), row(D_MODEL), row(512), row(512), row(512), _full((1, GLA_DV)), _full((1, SWA_W)),
                  _full((D_MODEL, D_MODEL)), _full((1, D_MODEL))],
        out_specs=[row(512), row(512), row(512), row(D_MODEL), _full((1, D_MODEL)), _full((1, GLA_DV)),
                   _full((1, SWA_W))],
        out_shape=[rshape(512), rshape(512), rshape(512), rshape(D_MODEL, BF16),
                   jax.ShapeDtypeStruct((1, D_MODEL), F32), jax.ShapeDtypeStruct((1, GLA_DV), F32),
                   jax.ShapeDtypeStruct((1, SWA_W), F32)],
        args=(dh2, m, ogla, gg, oswa, wgn, wsn, wout, wpost), hook=hook)


def _mix_in_bwd(dh2, h1, wmixpre, winp, wa2p, bap, cos, sin, ga, dgq, dgk, dgv, dgg, dla, dsq, dsk, dsv, dkm, dvm):
    t = h1.shape[0]
    tm = _row_tile(t)

    def body(dh2_ref, h_ref, w_ref, win_ref, wa2_ref, ba_ref, cos_ref, sin_ref, ga_ref, dgq_ref, dgk_ref, dgv_ref,
             dgg_ref, dla_ref, dsq_ref, dsk_ref, dsv_ref, dkm_ref, dvm_ref,
             dh1_ref, dproj_ref, dw_ref, dwa2_ref, dba_ref):
        i = pl.program_id(0)

        @pl.when(i == 0)
        def _():
            dw_ref[...] = jnp.zeros_like(dw_ref)
            dwa2_ref[...] = jnp.zeros_like(dwa2_ref)
            dba_ref[...] = jnp.zeros_like(dba_ref)

        first = (i == 0).astype(F32)
        c = cos_ref[...]
        s = -sin_ref[...]
        fh = _first_half_mask(tm)
        dproj_ref[:, P_GQ:P_GK] = dgq_ref[...].astype(BF16)
        dproj_ref[:, P_GK:P_GV] = dgk_ref[...].astype(BF16)
        dproj_ref[:, P_GV:P_GG] = dgv_ref[...].astype(BF16)
        dproj_ref[:, P_GG:P_GA] = dgg_ref[...].astype(BF16)
        gab = ga_ref[...].astype(BF16)
        z = _dot(gab, wa2_ref[...]) + ba_ref[...]
        row_id = i * tm + lax.broadcasted_iota(jnp.int32, (tm, 1), 0)
        dz = jnp.where(row_id >= PAD, dla_ref[...] * (1.0 / GLA_TAU) * (1.0 - _sigmoid(z)), 0.0)
        dzb = dz.astype(BF16)
        dba_ref[...] += jnp.sum(dz, axis=0, keepdims=True)
        dwa2_ref[...] += _dg(gab, dzb, TN)
        dproj_ref[:, P_GA:P_SQ] = _dg(dzb, wa2_ref[...], NT).astype(BF16)
        for k in range(4):
            dy = dsq_ref[:, 128 * k:128 * (k + 1)]
            dproj_ref[:, P_SQ + 128 * k:P_SQ + 128 * (k + 1)] = (dy * c + _rot_half(dy, fh) * s).astype(BF16)
        for k in range(2):
            ls = slice(128 * k, 128 * (k + 1))
            dy = dsk_ref[:, ls]
            dy = jnp.concatenate([dy[:BLK] + first * dkm_ref[:, ls], dy[BLK:]], axis=0) if tm > BLK else (
                dy + first * dkm_ref[:, ls])
            dproj_ref[:, P_SK + 128 * k:P_SK + 128 * (k + 1)] = (dy * c + _rot_half(dy, fh) * s).astype(BF16)
            dv = dsv_ref[:, ls]
            dv = jnp.concatenate([dv[:BLK] + first * dvm_ref[:, ls], dv[BLK:]], axis=0) if tm > BLK else (
                dv + first * dvm_ref[:, ls])
            dproj_ref[:, P_SV + 128 * k:P_SV + 128 * (k + 1)] = dv.astype(BF16)
        dn = _dg(dproj_ref[...], win_ref[...], NT)
        w = w_ref[...]
        _, hh, r = _rms(h_ref[...], w)
        dx, dw = _rms_bwd(hh, r, w, dn)
        dw_ref[...] += dw
        dh1_ref[...] = dh2_ref[...] + dx

    def row(w):
        return pl.BlockSpec((tm, w), lambda i: (i, 0))

    return pl.pallas_call(
        body, name="mix_in_bwd", grid=(t // tm,),
        in_specs=[row(D_MODEL), row(D_MODEL), _full((1, D_MODEL)), _full((D_MODEL, P_END)), _full((128, GLA_KW)),
                  _full((1, GLA_KW)), row(128), row(128), row(128), row(256), row(256), row(512), row(512), row(256),
                  row(512), row(256), row(256), _full((BLK, 256)), _full((BLK, 256))],
        out_specs=[row(D_MODEL), row(P_END), _full((1, D_MODEL)), _full((128, GLA_KW)), _full((1, GLA_KW))],
        out_shape=[jax.ShapeDtypeStruct((t, D_MODEL), F32), jax.ShapeDtypeStruct((t, P_END), BF16),
                   jax.ShapeDtypeStruct((1, D_MODEL), F32), jax.ShapeDtypeStruct((128, GLA_KW), F32),
                   jax.ShapeDtypeStruct((1, GLA_KW), F32)],
        compiler_params=_cparams(1),
    )(dh2, h1, wmixpre, winp, wa2p, bap, cos, sin, ga, dgq, dgk, dgv, dgg, dla, dsq, dsk, dsv, dkm, dvm)


def _adamw_update(w, g, m, v):
    m = ADAM_B1 * m + (1.0 - ADAM_B1) * g
    v = ADAM_B2 * v + (1.0 - ADAM_B2) * (g * g)
    m_hat = m / (1.0 - ADAM_B1 ** ADAM_STEP)
    v_hat = v / (1.0 - ADAM_B2 ** ADAM_STEP)
    return -ADAM_LR * (m_hat / (jnp.sqrt(v_hat) + ADAM_EPS) + ADAM_WD * w), m, v


def _adamw_halves(w, g_mine, g_other, m, v, c_idx, row0=0):
    r, c = w.shape
    h = g_mine.shape[0]
    tr = _div_tile(math.gcd(r, h))
    nth = h // tr
    t0 = row0 // tr
    assert t0 * tr == row0

    def body(c_ref, w_ref, gm_ref, go_ref, m_ref, v_ref, g_ref, d_ref, nm_ref, nv_ref):
        hh = (t0 + pl.program_id(0)) // nth
        g = jnp.where(hh == c_ref[0], gm_ref[...], go_ref[...])
        g_ref[...] = g
        d_ref[...], nm_ref[...], nv_ref[...] = _adamw_update(w_ref[...], g, m_ref[...], v_ref[...])

    spec = pl.BlockSpec((tr, c), lambda i, c_ref: (i, 0))

    def gspec(is_mine):
        def index(i, c_ref):
            used = ((t0 + i) // nth == c_ref[0]) == is_mine
            return (jnp.where(used, (t0 + i) % nth, 0), 0)
        return pl.BlockSpec((tr, c), index)

    shape = jax.ShapeDtypeStruct((r, c), F32)
    return pl.pallas_call(
        body, name="adamw_halves",
        grid_spec=pltpu.PrefetchScalarGridSpec(
            num_scalar_prefetch=1, grid=(r // tr,), in_specs=[spec, gspec(True), gspec(False), spec, spec],
            out_specs=[spec] * 4),
        out_shape=[shape] * 4, compiler_params=_cparams(1),
    )(c_idx, w, g_mine, g_other, m, v)


def _place():
    x, y, c = lax.axis_index("x"), lax.axis_index("y"), lax.axis_index("c")
    chips = [(1 - x, y), (x, 1 - y), (1 - x, 1 - y)]
    return x, y, c, chips


def _remote(send_sem, recv_sem, src, dst, to):
    return pltpu.make_async_remote_copy(src_ref=src, dst_ref=dst, send_sem=send_sem, recv_sem=recv_sem,
                                        device_id=to, device_id_type=MESH)


def _half(ref_rows, c):
    h = ref_rows // 2
    return pl.ds(pl.multiple_of(c * h, 8), h)


def _own_slot(shard, q):
    return lax.dynamic_update_slice(jnp.zeros((N_CHIPS,) + shard.shape, shard.dtype), shard[None], (q, 0, 0))


def _stack_own_slot(mats, q_idx):
    r, w = mats[0].shape
    tr = _div_tile(r)
    per = r // tr
    n = len(mats)

    def body(q_ref, *refs):
        m_refs, o_ref = refs[:n], refs[n]
        s = pl.program_id(0)
        for k in range(n):
            @pl.when(s // per == k)
            def _(k=k):
                o_ref[...] = m_refs[k][...].astype(BF16)

    def rows_of(k):
        return lambda s, q_ref: (jnp.where(s // per == k, s % per, 0), 0)

    return pl.pallas_call(
        body, name="stack_own_slot",
        grid_spec=pltpu.PrefetchScalarGridSpec(
            num_scalar_prefetch=1, grid=(n * per,),
            in_specs=[pl.BlockSpec((tr, w), rows_of(k)) for k in range(n)],
            out_specs=pl.BlockSpec((None, tr, w), lambda s, q_ref: (q_ref[0], s, 0))),
        out_shape=jax.ShapeDtypeStruct((N_CHIPS, n * r, w), BF16), compiler_params=_cparams(1),
    )(q_idx, *mats)


class _GatherChips:
    has_mid = True

    def __init__(self, bufs):
        n = len(bufs)
        self.inputs = list(bufs)
        self.out_shape = [jax.ShapeDtypeStruct(b.shape, b.dtype) for b in bufs]
        self.aliases = [(t, t) for t in range(n)]
        self.scratch = [pltpu.SemaphoreType.DMA((n, 6)), pltpu.SemaphoreType.DMA((n, 6))]

    def start(self, ins, outs, scr):
        send, recv = scr
        x, y, c, chips = _place()
        q = 2 * x + y
        for t, (i_ref, o_ref) in enumerate(zip(ins, outs)):
            rows = _half(i_ref.shape[1], c)
            for j, (cx, cy) in enumerate(chips):
                _remote(send.at[t, j], recv.at[t, j], i_ref.at[q, rows], o_ref.at[q, rows], (cx, cy, c)).start()

    def mid(self, ins, outs, scr):
        send, recv = scr
        x, y, c, chips = _place()
        for t, o_ref in enumerate(outs):
            rows = _half(o_ref.shape[1], c)
            for j, (cx, cy) in enumerate(chips):
                slot = o_ref.at[2 * cx + cy, rows]
                _remote(send.at[t, j], recv.at[t, j], slot, slot, (cx, cy, c)).wait_recv()
                _remote(send.at[t, 3 + j], recv.at[t, 3 + j], slot, slot, (x, y, 1 - c)).start()

    def finish(self, ins, outs, scr):
        send, recv = scr
        x, y, c, chips = _place()
        for t, o_ref in enumerate(outs):
            mine, other = _half(o_ref.shape[1], c), _half(o_ref.shape[1], 1 - c)
            for j, (cx, cy) in enumerate(chips):
                slot = o_ref.at[2 * cx + cy, other]
                _remote(send.at[t, 3 + j], recv.at[t, 3 + j], slot, slot, (x, y, 1 - c)).wait_recv()
            for j, (cx, cy) in enumerate(chips):
                sent = o_ref.at[2 * cx + cy, mine]
                _remote(send.at[t, j], recv.at[t, j], sent, sent, (cx, cy, c)).wait_send()
                _remote(send.at[t, 3 + j], recv.at[t, 3 + j], sent, sent, (x, y, 1 - c)).wait_send()


class _PairExchange:
    has_mid = False
    aliases = ()

    def __init__(self, arrs):
        n = len(arrs)
        self.inputs = list(arrs)
        self.out_shape = [jax.ShapeDtypeStruct((a.shape[0], a.shape[1] // 2, a.shape[2]), a.dtype) for a in arrs]
        self.scratch = [pltpu.SemaphoreType.DMA((n,)), pltpu.SemaphoreType.DMA((n,))]

    def _copies(self, ins, outs, scr):
        send, recv = scr
        x, y, c, _ = _place()
        return [_remote(send.at[t], recv.at[t], i_ref.at[:, _half(i_ref.shape[1], 1 - c)], o_ref, (x, y, 1 - c))
                for t, (i_ref, o_ref) in enumerate(zip(ins, outs))]

    def start(self, ins, outs, scr):
        for cp in self._copies(ins, outs, scr):
            cp.start()

    def finish(self, ins, outs, scr):
        for cp in self._copies(ins, outs, scr):
            cp.wait()


class _ChipScatter:
    has_mid = False
    aliases = ()

    def __init__(self, arrs):
        n = len(arrs)
        self.inputs = list(arrs)
        self.out_shape = [jax.ShapeDtypeStruct((3,) + a.shape[1:], a.dtype) for a in arrs]
        self.scratch = [pltpu.SemaphoreType.DMA((n, 3)), pltpu.SemaphoreType.DMA((n, 3))]

    def _copies(self, ins, outs, scr):
        send, recv = scr
        x, y, c, chips = _place()
        return [_remote(send.at[t, j], recv.at[t, j], i_ref.at[2 * cx + cy], o_ref.at[j], (cx, cy, c))
                for t, (i_ref, o_ref) in enumerate(zip(ins, outs)) for j, (cx, cy) in enumerate(chips)]

    def start(self, ins, outs, scr):
        for cp in self._copies(ins, outs, scr):
            cp.start()

    def finish(self, ins, outs, scr):
        for cp in self._copies(ins, outs, scr):
            cp.wait()


class _PairShare:
    has_mid = False
    aliases = ()

    def __init__(self, arrs):
        n = len(arrs)
        self.inputs = list(arrs)
        self.out_shape = [jax.ShapeDtypeStruct(a.shape, a.dtype) for a in arrs]
        self.scratch = [pltpu.SemaphoreType.DMA((n,)), pltpu.SemaphoreType.DMA((n,))]

    def _copies(self, ins, outs, scr):
        send, recv = scr
        x, y, c, _ = _place()
        return [_remote(send.at[t], recv.at[t], i_ref, o_ref, (x, y, 1 - c))
                for t, (i_ref, o_ref) in enumerate(zip(ins, outs))]

    def start(self, ins, outs, scr):
        for cp in self._copies(ins, outs, scr):
            cp.start()

    def finish(self, ins, outs, scr):
        for cp in self._copies(ins, outs, scr):
            cp.wait()


def _comm_call(hook, name):
    n_in, n_out = len(hook.inputs), len(hook.out_shape)

    def body(*refs):
        ins, outs, scr = refs[:n_in], refs[n_in:n_in + n_out], refs[n_in + n_out:]
        hook.start(ins, outs, scr)
        if hook.has_mid:
            hook.mid(ins, outs, scr)
        hook.finish(ins, outs, scr)

    return pl.pallas_call(body, name=name, in_specs=[ANY] * n_in, out_specs=[ANY] * n_out,
                          out_shape=list(hook.out_shape), scratch_shapes=list(hook.scratch),
                          input_output_aliases=dict(hook.aliases))(*hook.inputs)


class _GatherDevices:
    has_mid = True
    aliases = ()

    def __init__(self, vecs):
        n = len(vecs)
        self.inputs = list(vecs)
        self.out_shape = [jax.ShapeDtypeStruct((N_DEV,) + v.shape, v.dtype) for v in vecs]
        self.scratch = [pltpu.SemaphoreType.DMA((n, 7)), pltpu.SemaphoreType.DMA((n, 7)),
                        pltpu.SemaphoreType.DMA((n,))]

    @staticmethod
    def _copy(scr, t, k, out_ref, block, to, src=None):
        send, recv, _ = scr
        px, py, pc = block
        slot = out_ref.at[4 * px + 2 * py + pc]
        return _remote(send.at[t, k], recv.at[t, k], slot if src is None else src, slot, to)

    def start(self, ins, outs, scr):
        x, y, c, chips = _place()
        me = (x, y, c)
        for t, (x_ref, out_ref) in enumerate(zip(ins, outs)):
            pltpu.make_async_copy(x_ref, out_ref.at[4 * x + 2 * y + c], scr[2].at[t]).start()
            self._copy(scr, t, 0, out_ref, me, (x, y, 1 - c), src=x_ref).start()
            for j, chip in enumerate(chips):
                self._copy(scr, t, 1 + j, out_ref, me, (*chip, c), src=x_ref).start()

    def mid(self, ins, outs, scr):
        x, y, c, chips = _place()
        for t, out_ref in enumerate(outs):
            for j, chip in enumerate(chips):
                self._copy(scr, t, 1 + j, out_ref, (*chip, c), (x, y, c)).wait_recv()
                self._copy(scr, t, 4 + j, out_ref, (*chip, c), (x, y, 1 - c)).start()

    def finish(self, ins, outs, scr):
        x, y, c, chips = _place()
        me = (x, y, c)
        for t, (x_ref, out_ref) in enumerate(zip(ins, outs)):
            self._copy(scr, t, 0, out_ref, (x, y, 1 - c), me).wait_recv()
            for j, chip in enumerate(chips):
                self._copy(scr, t, 4 + j, out_ref, (*chip, 1 - c), me).wait_recv()
            self._copy(scr, t, 0, out_ref, me, (x, y, 1 - c), src=x_ref).wait_send()
            for j, chip in enumerate(chips):
                self._copy(scr, t, 1 + j, out_ref, me, (*chip, c), src=x_ref).wait_send()
                self._copy(scr, t, 4 + j, out_ref, (*chip, c), (x, y, 1 - c)).wait_send()
            pltpu.make_async_copy(x_ref, out_ref.at[4 * x + 2 * y + c], scr[2].at[t]).wait()


class _Hooks:
    def __init__(self, hooks):
        self.hooks = list(hooks)
        self.has_mid = any(h.has_mid for h in hooks)
        self.inputs = [a for h in hooks for a in h.inputs]
        self.out_shape = [s for h in hooks for s in h.out_shape]
        self.scratch = [s for h in hooks for s in h.scratch]
        self.aliases = []
        i0 = o0 = 0
        for h in hooks:
            self.aliases += [(i0 + a, o0 + b) for a, b in h.aliases]
            i0 += len(h.inputs)
            o0 += len(h.out_shape)

    def _each(self, ins, outs, scr):
        i0 = o0 = s0 = 0
        for h in self.hooks:
            ni, no, ns = len(h.inputs), len(h.out_shape), len(h.scratch)
            yield h, ins[i0:i0 + ni], outs[o0:o0 + no], scr[s0:s0 + ns]
            i0, o0, s0 = i0 + ni, o0 + no, s0 + ns

    def start(self, ins, outs, scr):
        for h, i, o, s in self._each(ins, outs, scr):
            h.start(i, o, s)

    def mid(self, ins, outs, scr):
        for h, i, o, s in self._each(ins, outs, scr):
            if h.has_mid:
                h.mid(i, o, s)

    def finish(self, ins, outs, scr):
        for h, i, o, s in self._each(ins, outs, scr):
            h.finish(i, o, s)

    def split(self, outs):
        res, o0 = [], 0
        for h in self.hooks:
            res.append(list(outs[o0:o0 + len(h.out_shape)]))
            o0 += len(h.out_shape)
        return res


def _pair_sum(g, other, c_idx):
    nq, r, w = g.shape
    h = r // 2
    tr = _div_tile(h)
    nt = h // tr

    def body(c_ref, g_ref, o_ref, s_ref):
        s_ref[...] = (g_ref[...].astype(F32) + o_ref[...].astype(F32)).astype(s_ref.dtype)

    return pl.pallas_call(
        body, name="pair_sum",
        grid_spec=pltpu.PrefetchScalarGridSpec(
            num_scalar_prefetch=1, grid=(nq, nt),
            in_specs=[pl.BlockSpec((None, tr, w), lambda k, i, c_ref: (k, c_ref[0] * nt + i, 0)),
                      pl.BlockSpec((None, tr, w), lambda k, i, c_ref: (k, i, 0))],
            out_specs=pl.BlockSpec((None, tr, w), lambda k, i, c_ref: (k, i, 0))),
        out_shape=jax.ShapeDtypeStruct((nq, h, w), g.dtype),
        compiler_params=_cparams(2),
    )(c_idx, g, other)


def _chip_sum(s, others, q_idx):
    _, h, w = s.shape
    tr = _div_tile(h)

    def body(q_ref, s_ref, o_ref, out_ref):
        out_ref[...] = ((s_ref[...].astype(F32) + o_ref[0].astype(F32)) + o_ref[1].astype(F32)) + o_ref[2].astype(F32)

    return pl.pallas_call(
        body, name="chip_sum",
        grid_spec=pltpu.PrefetchScalarGridSpec(
            num_scalar_prefetch=1, grid=(h // tr,),
            in_specs=[pl.BlockSpec((None, tr, w), lambda i, q_ref: (q_ref[0], i, 0)),
                      pl.BlockSpec((3, tr, w), lambda i, q_ref: (0, i, 0))],
            out_specs=pl.BlockSpec((tr, w), lambda i, q_ref: (i, 0))),
        out_shape=jax.ShapeDtypeStruct((h, w), F32),
        compiler_params=_cparams(1),
    )(q_idx, s, others)


def _small_update(q_idx, parts, ws, ms, vs, col_block):
    n = len(parts)
    has_w = [w is not None for w in ws]

    def body(q_ref, *refs):
        pos = 0
        ins = []
        for t in range(n):
            k = 4 if has_w[t] else 1
            ins.append(refs[pos:pos + k])
            pos += k
        outs = refs[pos:]
        opos = 0
        for t in range(n):
            p_ref = ins[t][0]
            g = p_ref[0]
            for s in range(1, p_ref.shape[0]):
                g = g + p_ref[s]
            if has_w[t]:
                _, w_ref, m_ref, v_ref = ins[t]
                g_ref, d_ref, nm_ref, nv_ref = outs[opos:opos + 4]
                opos += 4
                g_ref[...] = g
                d_ref[...], nm_ref[...], nv_ref[...] = _adamw_update(w_ref[...], g, m_ref[...], v_ref[...])
            else:
                outs[opos][...] = g
                opos += 1

    def whole(shape):
        nd = len(shape)
        return pl.BlockSpec(shape, lambda i, q_ref: (0,) * nd)

    in_specs, out_specs, out_shape, args = [], [], [], []
    for t in range(n):
        k, r, wf = parts[t].shape
        if col_block[t]:
            w = wf // N_CHIPS
            in_specs.append(pl.BlockSpec((k, r, w), lambda i, q_ref: (0, 0, q_ref[0])))
        else:
            w = wf
            in_specs.append(whole((k, r, wf)))
        args.append(parts[t])
        if has_w[t]:
            assert ws[t].shape == (r, w), (ws[t].shape, r, w)
            in_specs += [whole((r, w))] * 3
            args += [ws[t], ms[t], vs[t]]
            out_specs += [whole((r, w))] * 4
            out_shape += [jax.ShapeDtypeStruct((r, w), F32)] * 4
        else:
            out_specs.append(whole((r, w)))
            out_shape.append(jax.ShapeDtypeStruct((r, w), F32))
    return pl.pallas_call(
        body, name="small_update",
        grid_spec=pltpu.PrefetchScalarGridSpec(num_scalar_prefetch=1, grid=(1,), in_specs=in_specs,
                                               out_specs=out_specs),
        out_shape=out_shape, compiler_params=_cparams(1),
    )(q_idx, *args)


_PACK_SEGMENTS = ((0, 1552), None, (1552, 2064), (2064, 2128), (2064, 2128), (2128, 2192), (2128, 2192),
                  (2192, 2256), (2192, 2256), (2256, 2320), (2256, 2320))
_UNPACK_SEGMENTS = (((0, 1552), (0,)), ((1552, 2064), (P_SQ,)), ((2064, 2128), (P_SK, P_SK + 64)),
                    ((2128, 2192), (P_SK + 128, P_SK + 192)), ((2192, 2256), (P_SV, P_SV + 64)),
                    ((2256, 2320), (P_SV + 128, P_SV + 192)))


def _pack_win(w4):
    per = w4.shape[2]
    pieces = []
    for seg in _PACK_SEGMENTS:
        if seg is None:
            pieces.append(jnp.zeros((w4.shape[1], 128 - GLA_RANK), w4.dtype))
            continue
        for q in range(w4.shape[0]):
            lo, hi = max(seg[0], q * per), min(seg[1], (q + 1) * per)
            if lo < hi:
                pieces.append(w4[q][:, lo - q * per:hi - q * per])
    return jnp.concatenate(pieces, axis=1)


def _unpack_dwin(d):
    per = D_IN // N_CHIPS
    chips = []
    for q in range(N_CHIPS):
        pieces = []
        for (a, b), starts in _UNPACK_SEGMENTS:
            lo, hi = max(a, q * per), min(b, (q + 1) * per)
            if lo < hi:
                copies = [d[:, s + lo - a:s + hi - a] for s in starts]
                pieces.append(copies[0] if len(copies) == 1 else copies[0] + copies[1])
        chips.append(jnp.concatenate(pieces, axis=1))
    return jnp.stack(chips)


def _local_step(x, target, meta, p):
    s = x.shape[0]
    t = s + BLK
    h0 = jnp.concatenate([jnp.zeros((PAD, D_MODEL), F32), meta, x], axis=0)
    cos, sin = _rope_tables(t)

    h1, n1, g1, u1, a1, f1 = _ffn_fwd(h0, p["ffn1_pre_norm"], p["ffn1_w"], p["ffn1_post_norm"])
    n2, gq, gk, gv, gg, ga, la, sq, sk, sv = _mix_proj(h1, p["mix_pre_norm"], p["w_in"], p["gla_w_a2"], p["gla_b_a"],
                                                       cos, sin)
    ogla, ss = _gla_fwd(gq, gk, gv, la)
    oswa = _swa_fwd(p["swa_sinks"], sq, sk, sv)
    h2, cat, m = _mix_out(h1, ogla, gg, oswa, p["gla_out_norm"], p["swa_out_norm"], p["w_out"], p["mix_post_norm"])
    grads = {}
    dy, n3, g3, u3, a3, df3, grads["ffn2_post_norm"], sse = _ffn_fwd(
        h2, p["ffn2_pre_norm"], p["ffn2_w"], p["ffn2_post_norm"], target=target)

    dh2, dg3, du3, grads["ffn2_pre_norm"] = _ffn_bwd(
        dy, h2, None, g3, u3, p["ffn2_pre_norm"], p["ffn2_w"], p["ffn2_post_norm"], df=df3)
    (gud,) = _ffn_wgrad(n3, df3, dg3, du3, a3)
    grads["ffn2_w_gate"], grads["ffn2_w_up"], grads["ffn2_w_down"] = gud[:, :FJ], gud[:, FJ:2 * FJ], gud[:, 2 * FJ:]

    dogla, dgg, doswa, dm, grads["mix_post_norm"], grads["gla_out_norm"], grads["swa_out_norm"] = _mix_out_bwd(
        dh2, m, ogla, gg, oswa, p["gla_out_norm"], p["swa_out_norm"], p["w_out"], p["mix_post_norm"])
    grads["w_out"] = _xty(cat, dm)
    dsq, dsk, dsv, dkm, dvm, dsinks = _swa_bwd(p["swa_sinks"], sq, sk, sv, oswa, doswa)
    grads["swa_sinks"] = dsinks[:, 0]
    dgq, dgk, dgv, dla = _gla_bwd(gq, gk, gv, la, ss, dogla)
    dh1, dproj, grads["mix_pre_norm"], dwa2p, grads["gla_b_a"] = _mix_in_bwd(
        dh2, h1, p["mix_pre_norm"], p["w_in"], p["gla_w_a2"], p["gla_b_a"], cos, sin, ga, dgq, dgk, dgv, dgg, dla,
        dsq, dsk, dsv, dkm, dvm)
    grads["gla_w_a2"] = dwa2p[:GLA_RANK]
    grads["w_in"] = _unpack_dwin(_xty(n2, dproj))

    dh0, df1, dg1, du1, grads["ffn1_pre_norm"], grads["ffn1_post_norm"] = _ffn_bwd(
        dh1, h0, f1, g1, u1, p["ffn1_pre_norm"], p["ffn1_w"], p["ffn1_post_norm"])
    (gud,) = _ffn_wgrad(n1, df1, dg1, du1, a1)
    grads["ffn1_w_gate"], grads["ffn1_w_up"], grads["ffn1_w_down"] = gud[:, :FJ], gud[:, FJ:2 * FJ], gud[:, 2 * FJ:]
    grads["meta_tokens"] = dh0[PAD:BLK]
    return sse[0, 0], dh0[BLK:], grads


WEIGHTS = ['meta_tokens', 'ffn1_pre_norm', 'ffn1_w_gate', 'ffn1_w_up', 'ffn1_w_down', 'ffn1_post_norm',
           'mix_pre_norm', 'w_in', 'gla_w_a2', 'gla_b_a', 'gla_out_norm', 'swa_sinks', 'swa_out_norm', 'w_out',
           'mix_post_norm', 'ffn2_pre_norm', 'ffn2_w_gate', 'ffn2_w_up', 'ffn2_w_down', 'ffn2_post_norm']
BIG = ['ffn1_w_gate', 'ffn1_w_up', 'ffn1_w_down', 'w_in', 'w_out', 'ffn2_w_gate', 'ffn2_w_up', 'ffn2_w_down']
SMALL = [n for n in WEIGHTS if n not in BIG]
FJ = D_FF // N_CHIPS
D_IN_J = D_IN // N_CHIPS
D_OUT_J = D_MODEL // N_CHIPS
TRANSPOSED = ('ffn1_w_gate', 'ffn1_w_up', 'ffn2_w_gate', 'ffn2_w_up')


def _shard2d(name, a):
    return a[0].T if name in TRANSPOSED else a[0]


def _unshard2d(name, a):
    return (a.T if name in TRANSPOSED else a)[None]


def kernel(x, meta_tokens, ffn1_pre_norm, ffn1_w_gate, ffn1_w_up, ffn1_w_down, ffn1_post_norm, mix_pre_norm, w_in, gla_w_a2, gla_b_a, gla_out_norm, swa_sinks, swa_out_norm, w_out, mix_post_norm, ffn2_pre_norm, ffn2_w_gate, ffn2_w_up, ffn2_w_down, ffn2_post_norm, loss_target, m_meta_tokens, m_ffn1_pre_norm, m_ffn1_w_gate, m_ffn1_w_up, m_ffn1_w_down, m_ffn1_post_norm, m_mix_pre_norm, m_w_in, m_gla_w_a2, m_gla_b_a, m_gla_out_norm, m_swa_sinks, m_swa_out_norm, m_w_out, m_mix_post_norm, m_ffn2_pre_norm, m_ffn2_w_gate, m_ffn2_w_up, m_ffn2_w_down, m_ffn2_post_norm, v_meta_tokens, v_ffn1_pre_norm, v_ffn1_w_gate, v_ffn1_w_up, v_ffn1_w_down, v_ffn1_post_norm, v_mix_pre_norm, v_w_in, v_gla_w_a2, v_gla_b_a, v_gla_out_norm, v_swa_sinks, v_swa_out_norm, v_w_out, v_mix_post_norm, v_ffn2_pre_norm, v_ffn2_w_gate, v_ffn2_w_up, v_ffn2_w_down, v_ffn2_post_norm):
    args = dict(locals())
    w = {n: args[n] for n in WEIGHTS}
    mom = {n: args["m_" + n] for n in WEIGHTS}
    var = {n: args["v_" + n] for n in WEIGHTS}
    cx, cy, cc = lax.axis_index("x"), lax.axis_index("y"), lax.axis_index("c")
    q_idx = (2 * cx + cy).astype(jnp.int32).reshape(1)
    c_idx = cc.astype(jnp.int32).reshape(1)

    q_chip = 2 * cx + cy
    bf = {n: _own_slot(_shard2d(n, w[n]).astype(BF16), q_chip) for n in ("w_in", "w_out")}
    for ffn in ("ffn1", "ffn2"):
        bf[ffn] = _stack_own_slot([_shard2d(ffn + s, w[ffn + s]) for s in ("_w_gate", "_w_up", "_w_down")], q_idx)
    qc_idx = jnp.stack([q_chip, cc]).astype(jnp.int32)
    sinks = w["swa_sinks"].reshape(SWA_QH)

    seq, target = x[0], loss_target[0]
    t = seq.shape[0] + BLK
    h0, n1 = _embed_norm(seq, _own_slot(w["meta_tokens"], q_chip), w["ffn1_pre_norm"])
    cos, sin = _rope_tables(t)
    late = _GatherChips([bf["w_in"], bf["w_out"], bf["ffn2"],
                         _own_slot(w["gla_w_a2"].reshape(GLA_RANK, GLA_KW // N_CHIPS), q_chip)])
    (h1, g1, u1, a1, f1), (w31,), (win4, wout4, w32, wa24) = _ffn_fwd_gather(
        h0, n1, bf["ffn1"], w["ffn1_post_norm"], qc_idx, late)
    wa2p = jnp.pad(wa24.transpose(1, 0, 2).reshape(GLA_RANK, GLA_KW), ((0, 128 - GLA_RANK), (0, 0))).astype(BF16)
    winp = _pack_win(win4)
    wout = wout4.reshape(D_MODEL, D_MODEL)
    n2, gq, gk, gv, gg, ga, la, sq, sk, sv = _mix_proj(h1, w["mix_pre_norm"], winp, wa2p, w["gla_b_a"], cos, sin)
    ogla, ss = _gla_fwd(gq, gk, gv, la)
    oswa = _swa_fwd(sinks, sq, sk, sv)
    h2, cat, m = _mix_out(h1, ogla, gg, oswa, w["gla_out_norm"], w["swa_out_norm"], wout, w["mix_post_norm"])
    g = {}
    dy, n3, g3, u3, a3, df3, g["ffn2_post_norm"], sse = _ffn_fwd(
        h2, w["ffn2_pre_norm"], w32, w["ffn2_post_norm"], target=target)

    dh2, dg3, du3, g["ffn2_pre_norm"] = _ffn_bwd(
        dy, h2, None, g3, u3, w["ffn2_pre_norm"], w32, w["ffn2_post_norm"], df=df3)
    (gf2,) = _ffn_wgrad(n3, df3, dg3, du3, a3)
    (dogla, dgg, doswa, dm, g["mix_post_norm"], g["gla_out_norm"], g["swa_out_norm"]), (rgf2,) = _mix_out_bwd(
        dh2, m, ogla, gg, oswa, w["gla_out_norm"], w["swa_out_norm"], wout, w["mix_post_norm"],
        hook=_PairExchange([gf2]))
    sgf2 = _pair_sum(gf2, rgf2, c_idx)
    gout = _xty(cat, dm).reshape(N_CHIPS, D_OUT_J, D_MODEL)
    (dsq, dsk, dsv, dkm, dvm, dsinks), (ogf2,) = _swa_bwd(sinks, sq, sk, sv, oswa, doswa,
                                                          hook=_ChipScatter([sgf2]))
    g["swa_sinks"] = dsinks
    dgq, dgk, dgv, dla = _gla_bwd(gq, gk, gv, la, ss, dogla)
    dh1, dproj, g["mix_pre_norm"], dwa2p, g["gla_b_a"] = _mix_in_bwd(
        dh2, h1, w["mix_pre_norm"], winp, wa2p, w["gla_b_a"], cos, sin, ga, dgq, dgk, dgv, dgg, dla,
        dsq, dsk, dsv, dkm, dvm)
    g["gla_w_a2"] = dwa2p[:GLA_RANK]
    gin = _unpack_dwin(_xty(n2, dproj))
    rgin, rgout = _comm_call(_PairExchange([gin, gout]), "pair_exchange")
    dh_first, grad_x, df1, dg1, du1, g["ffn1_pre_norm"], g["ffn1_post_norm"] = _ffn_bwd(
        dh1, h0, f1, g1, u1, w["ffn1_pre_norm"], w31, w["ffn1_post_norm"], split_first_block=True)
    sgin, sgout = _pair_sum(gin, rgin, c_idx), _pair_sum(gout, rgout, c_idx)
    g["meta_tokens"] = dh_first[PAD:BLK]
    late_small = ["gla_w_a2", "swa_sinks"]
    direct = [n for n in SMALL if n not in late_small]
    names = direct + late_small
    half_f2 = _chip_sum(sgf2, ogf2, q_idx)
    hooks = _Hooks([_ChipScatter([sgin, sgout]), _GatherDevices([g[n] for n in names] + [sse]),
                    _PairShare([half_f2])])
    own1, others1, houts = _ffn_wgrad_reduce(n1, df1, dg1, du1, a1, qc_idx, hooks)
    (ogin, ogout), gathered, (other_f2,) = hooks.split(houts)
    halves = [_chip_sum(own1[None], others1, jnp.zeros((1,), jnp.int32))]
    halves += [_chip_sum(s, o, q_idx) for s, o in ((sgin, ogin), (sgout, ogout))]
    others = list(_comm_call(_PairShare(halves), "pair_share")) + [other_f2]
    halves.append(half_f2)
    reduced = {"ffn1_w_gate": (0, 0), "ffn1_w_up": (0, FJ), "ffn1_w_down": (0, 2 * FJ), "w_in": (1, 0),
               "w_out": (2, 0), "ffn2_w_gate": (3, 0), "ffn2_w_up": (3, FJ), "ffn2_w_down": (3, 2 * FJ)}
    grad, delta, new_m, new_v = {}, {}, {}, {}
    for n in BIG:
        k, row0 = reduced[n]
        outs = _adamw_halves(_shard2d(n, w[n]), halves[k], others[k], _shard2d(n, mom[n]), _shard2d(n, var[n]),
                             c_idx, row0)
        grad[n], delta[n], new_m[n], new_v[n] = [_unshard2d(n, a) for a in outs]

    late = late_small
    mat = lambda a: a.reshape(a.shape[-2:])
    none3 = [None] * (len(late) + 1)
    outs = _small_update(q_idx, gathered, [mat(w[n]) for n in direct] + none3, [mat(mom[n]) for n in direct] + none3,
                         [mat(var[n]) for n in direct] + none3, [n == "meta_tokens" for n in names] + [False])
    sum_a2, sum_sinks, sum_sse = outs[4 * len(direct):]
    loss = sum_sse[0, 0] * (0.5 / D_MODEL)
    g_late = [lax.dynamic_slice_in_dim(sum_a2, q_chip * (GLA_KW // N_CHIPS), GLA_KW // N_CHIPS, axis=1)[None],
              sum_sinks[:, 0].reshape(1, 1, SWA_QH)]
    outs = list(outs[:4 * len(direct)]) + list(_small_update(
        q_idx, g_late, [mat(w[n]) for n in late], [mat(mom[n]) for n in late], [mat(var[n]) for n in late],
        [False, False]))
    for k, n in enumerate(names):
        grad[n], delta[n], new_m[n], new_v[n] = [a.reshape(w[n].shape) for a in outs[4 * k:4 * k + 4]]

    return (loss, grad_x[None], *[grad[n] for n in WEIGHTS], *[delta[n] for n in WEIGHTS],
            *[new_m[n] for n in WEIGHTS], *[new_v[n] for n in WEIGHTS])
```

```python
import functools
import math

import numpy as np
import jax
import jax.numpy as jnp
from jax import lax
from jax.experimental import pallas as pl
from jax.experimental.pallas import tpu as pltpu

F32 = jnp.float32
BF16 = jnp.bfloat16
MESH = pl.DeviceIdType.MESH

D_MODEL = 1024
D_FF = 2816
N_CHIPS = 4
N_DEV = 8
N_META = 16
BLK = 128
PAD = BLK - N_META
GLA_CHUNK = 64
GLA_HEADS = 4
GLA_DV = 128
GLA_DK = 64
GLA_KW = GLA_HEADS * GLA_DK
GLA_W = GLA_HEADS * GLA_DV
GLA_RANK = 16
GLA_TAU = 16.0
SWA_HD = 64
SWA_QH = 8
SWA_KVH = 2
SWA_W = SWA_QH * SWA_HD
WINDOW = 128
ROPE_THETA = 10000.0
EPS = 1e-6
NEG_INF = -1e30
IN_SPLITS = (256, 256, 512, 512, 16, 512, 128, 128)
D_IN = sum(IN_SPLITS)
P_GQ, P_GK, P_GV, P_GG, P_GA, P_SQ, P_SK, P_SV, P_END = 0, 256, 512, 1024, 1536, 1664, 2176, 2432, 2688
ADAM_LR, ADAM_B1, ADAM_B2, ADAM_EPS, ADAM_WD, ADAM_STEP = 0.001, 0.9, 0.999, 1e-08, 0.01, 10
VMEM_LIMIT = 56 * 1024 * 1024

NT = (((1,), (1,)), ((), ()))
TN = (((0,), (0,)), ((), ()))


def _cparams(n_axes):
    return pltpu.CompilerParams(dimension_semantics=("arbitrary",) * n_axes, vmem_limit_bytes=VMEM_LIMIT)


def _row_tile(t):
    for tm in (640, 512, 384, 256, 128):
        if t % tm == 0:
            return tm
    raise ValueError(t)


SEQ_BLOCKS_PER_STEP = 5


def _seq_tile(t):
    return SEQ_BLOCKS_PER_STEP * BLK if t % (SEQ_BLOCKS_PER_STEP * BLK) == 0 else BLK


ROW_PARTS = 2


def _row_parts(tm):
    n = ROW_PARTS if tm % (16 * ROW_PARTS) == 0 else 1
    return [slice(k * (tm // n), (k + 1) * (tm // n)) for k in range(n)]


def _contract_tile(t):
    return 1664 if t % 1664 == 0 else _row_tile(t)


ADAMW_TILE_ROWS = 176


def _div_tile(r, cap=512):
    best = None
    for tr in range(8, min(r, cap) + 1, 8):
        if r % tr == 0:
            best = tr
    return best if best is not None else r


def _dot(a, b):
    return jnp.dot(a, b, preferred_element_type=F32)


def _dg(a, b, dims):
    return lax.dot_general(a, b, dims, preferred_element_type=F32)


def _rms(x, w):
    r = lax.rsqrt(jnp.mean(x * x, axis=-1, keepdims=True) + EPS)
    xh = x * r
    return xh * w, xh, r


def _rms_bwd(xh, r, w, dy):
    wdy = dy * w
    dx = r * (wdy - xh * jnp.mean(wdy * xh, axis=-1, keepdims=True))
    dw = jnp.sum(dy * xh, axis=0, keepdims=True)
    return dx, dw


def _sigmoid(x):
    return 1.0 / (1.0 + jnp.exp(-x))


def _full(shape):
    nd = len(shape)
    return pl.BlockSpec(shape, lambda *_: (0,) * nd)


ANY = pl.BlockSpec(memory_space=pl.ANY)


def _pallas(body, *, name, grid, in_specs, out_specs, out_shape, args, scratch_shapes=(), hook=None):
    n_axes = len(grid)
    if hook is None:
        return pl.pallas_call(body, name=name, grid=grid, in_specs=list(in_specs), out_specs=list(out_specs),
                              out_shape=list(out_shape), scratch_shapes=list(scratch_shapes),
                              compiler_params=_cparams(n_axes))(*args)
    n_in, n_out, n_scr = len(in_specs), len(out_specs), len(scratch_shapes)
    h_in, h_out = len(hook.inputs), len(hook.out_shape)
    total = math.prod(grid)

    def wrapped(*refs):
        ins, hins = refs[:n_in], refs[n_in:n_in + h_in]
        o0 = n_in + h_in
        outs, houts = refs[o0:o0 + n_out], refs[o0 + n_out:o0 + n_out + h_out]
        s0 = o0 + n_out + h_out
        scr, hscr = refs[s0:s0 + n_scr], refs[s0 + n_scr:]
        step = pl.program_id(0)
        for a in range(1, n_axes):
            step = step * grid[a] + pl.program_id(a)

        @pl.when(step == 0)
        def _():
            hook.start(hins, houts, hscr)

        body(*ins, *outs, *scr)

        if hook.has_mid:
            @pl.when(step == (3 * total) // 4)
            def _():
                hook.mid(hins, houts, hscr)

        @pl.when(step == total - 1)
        def _():
            hook.finish(hins, houts, hscr)

    res = pl.pallas_call(
        wrapped, name=name, grid=grid, in_specs=list(in_specs) + [ANY] * h_in,
        out_specs=list(out_specs) + [ANY] * h_out, out_shape=list(out_shape) + list(hook.out_shape),
        scratch_shapes=list(scratch_shapes) + list(hook.scratch), compiler_params=_cparams(n_axes),
        input_output_aliases={n_in + a: n_out + b for a, b in hook.aliases},
    )(*args, *hook.inputs)
    return res[:n_out], res[n_out:]


def _ffn_weight_specs(w3):
    fj = w3.shape[1] // 3
    return fj, [pl.BlockSpec((None, fj, D_MODEL), functools.partial(lambda i, j, k: (j, k, 0), k=k)) for k in range(3)]


def _ffn_fwd(h, wpre, w3, wpost, hook=None, target=None):
    t = h.shape[0]
    tm = _row_tile(t)
    nj, rows3, _ = w3.shape
    fj = rows3 // 3
    nblk = tm // BLK if target is not None else 0

    def body(*refs):
        h_ref, wpre_ref, w_hbm, wpost_ref = refs[:4]
        t_refs = refs[4:4 + nblk]
        hout_ref, n_ref, p1_ref, p2_ref, a_ref, f_ref = refs[4 + nblk:10 + nblk]
        acc_ref, wv, wsem = refs[-3:]
        i = pl.program_id(0)
        j = pl.program_id(1)

        @pl.when((i == 0) & (j == 0))
        def _():
            for k in range(nj):
                pltpu.make_async_copy(w_hbm.at[k], wv.at[k], wsem.at[k]).start()

        @pl.when(i == 0)
        def _():
            pltpu.make_async_copy(w_hbm.at[j], wv.at[j], wsem.at[j]).wait()

        @pl.when(j == 0)
        def _():
            y, _, _ = _rms(h_ref[...], wpre_ref[...])
            n_ref[...] = y.astype(BF16)
            acc_ref[...] = jnp.zeros_like(acc_ref)

        if target is not None:
            dwpost_ref, sse_ref = refs[10 + nblk:12 + nblk]

            @pl.when((i == 0) & (j == 0))
            def _():
                dwpost_ref[...] = jnp.zeros_like(dwpost_ref)
                sse_ref[...] = jnp.zeros_like(sse_ref)

        n = n_ref[...]
        g = _dg(n, wv[j, 0:fj], NT)
        u = _dg(n, wv[j, fj:2 * fj], NT)
        sg = _sigmoid(g)
        silu = g * sg
        p1_ref[...] = (u * (sg + silu * (1.0 - sg))).astype(BF16)
        p2_ref[...] = silu.astype(BF16)
        a = (silu * u).astype(BF16)
        a_ref[...] = a
        acc_ref[...] += _dot(a, wv[j, 2 * fj:3 * fj])

        @pl.when(j == nj - 1)
        def _():
            f = acc_ref[...]
            wpost = wpost_ref[...]
            y, fh, r = _rms(f, wpost)
            hout = h_ref[...] + 0.5 * y
            if target is None:
                f_ref[...] = f
                hout_ref[...] = hout
            else:
                sse = jnp.zeros((1, 1), F32)
                errs = []
                for k in range(nblk):
                    err = hout[k * BLK:(k + 1) * BLK] - t_refs[k][...]
                    if k == 0:
                        err = jnp.where(i > 0, err, 0.0)
                    errs.append(err)
                    sse = sse + jnp.sum(jnp.sum(err * err, axis=1, keepdims=True), axis=0, keepdims=True)
                dy = (jnp.concatenate(errs, axis=0) if nblk > 1 else errs[0]) * (1.0 / D_MODEL)
                hout_ref[...] = dy
                df, dw = _rms_bwd(fh, r, wpost, 0.5 * dy)
                f_ref[...] = df.astype(BF16)
                dwpost_ref[...] += dw
                sse_ref[...] += jnp.broadcast_to(sse, sse_ref.shape)

    row = pl.BlockSpec((tm, D_MODEL), lambda i, j: (i, 0))
    vec = pl.BlockSpec((1, D_MODEL), lambda i, j: (0, 0))
    act = pl.BlockSpec((None, tm, fj), lambda i, j: (j, i, 0))
    t_specs = [pl.BlockSpec((BLK, D_MODEL), functools.partial(lambda i, j, k: (jnp.maximum(nblk * i + k - 1, 0), 0), k=k))
               for k in range(nblk)]
    loss_spec = [vec, _full((1, 128))] if target is not None else []
    loss_shape = [jax.ShapeDtypeStruct((1, D_MODEL), F32), jax.ShapeDtypeStruct((1, 128), F32)] if (
        target is not None) else []
    return _pallas(
        body, name="ffn_fwd", grid=(t // tm, nj),
        in_specs=[row, vec, ANY, vec] + t_specs,
        out_specs=[row, row, act, act, act, row] + loss_spec,
        out_shape=[jax.ShapeDtypeStruct((t, D_MODEL), F32), jax.ShapeDtypeStruct((t, D_MODEL), BF16),
                   jax.ShapeDtypeStruct((nj, t, fj), BF16), jax.ShapeDtypeStruct((nj, t, fj), BF16),
                   jax.ShapeDtypeStruct((nj, t, fj), BF16),
                   jax.ShapeDtypeStruct((t, D_MODEL), F32 if target is None else BF16)] + loss_shape,
        scratch_shapes=[pltpu.VMEM((tm, D_MODEL), F32), pltpu.VMEM((nj, rows3, D_MODEL), BF16),
                        pltpu.SemaphoreType.DMA((nj,))],
        args=(h, wpre, w3, wpost) + (target,) * nblk, hook=hook)


def _ffn_bwd(dhout, h, f, p14, p24, wpre, w3, wpost, df=None, split_first_block=False):
    t = h.shape[0]
    tm = _row_tile(t)
    ni = t // tm
    nj = w3.shape[0]
    fj, wspecs = _ffn_weight_specs(w3)
    have_df = df is not None
    assert not (have_df and split_first_block)

    def body(dhout_ref, h_ref, f_ref, p1_ref, p2_ref, wpre_ref, wg_ref, wu_ref, wd_ref, wpost_ref, *rest):
        if have_df:
            dh_ref, dg_ref, du_ref, dwpre_ref, dn_ref = rest
            df_ref = f_ref
        elif split_first_block:
            dh_ref, rest_hbm, df_ref, dg_ref, du_ref, dwpre_ref, dwpost_ref, dn_ref, dh_buf, dh_sem, first_sem = rest
        else:
            dh_ref, df_ref, dg_ref, du_ref, dwpre_ref, dwpost_ref, dn_ref = rest
        i = pl.program_id(0)
        j = pl.program_id(1)

        @pl.when((i == 0) & (j == 0))
        def _():
            dwpre_ref[...] = jnp.zeros_like(dwpre_ref)
            if not have_df:
                dwpost_ref[...] = jnp.zeros_like(dwpost_ref)

        @pl.when(j == 0)
        def _():
            if not have_df:
                wpost = wpost_ref[...]
                _, fh, r = _rms(f_ref[...], wpost)
                dfv, dw = _rms_bwd(fh, r, wpost, 0.5 * dhout_ref[...])
                dwpost_ref[...] += dw
                df_ref[...] = dfv.astype(BF16)
            dn_ref[...] = jnp.zeros_like(dn_ref)

        parts = _row_parts(tm)
        das = [_dg(df_ref[rows, :], wd_ref[...], NT) for rows in parts]
        for rows, da in zip(parts, das):
            dg = (da * p1_ref[rows, :].astype(F32)).astype(BF16)
            du = (da * p2_ref[rows, :].astype(F32)).astype(BF16)
            dg_ref[rows, :] = dg
            du_ref[rows, :] = du
            dn_ref[rows, :] += _dot(dg, wg_ref[...]) + _dot(du, wu_ref[...])

        @pl.when(j == nj - 1)
        def _():
            wpre = wpre_ref[...]
            _, hh, r = _rms(h_ref[...], wpre)
            dx, dw = _rms_bwd(hh, r, wpre, dn_ref[...])
            dwpre_ref[...] += dw
            dh = dhout_ref[...] + dx
            if not split_first_block:
                dh_ref[...] = dh
            else:
                slot = i % 2

                def to_rest(tile, sl):
                    rows = pl.ds(pl.multiple_of(tile * tm - BLK, 8), tm)
                    return pltpu.make_async_copy(dh_buf.at[sl], rest_hbm.at[rows], dh_sem.at[sl])

                first = pltpu.make_async_copy(dh_buf.at[0, BLK:tm], rest_hbm.at[0:tm - BLK], first_sem)

                @pl.when(i == 2)
                def _():
                    first.wait()

                @pl.when(i >= 3)
                def _():
                    to_rest(i, slot).wait()

                dh_buf[slot] = dh

                @pl.when(i == 0)
                def _():
                    dh_ref[...] = dh[0:BLK]
                    first.start()

                @pl.when(i > 0)
                def _():
                    to_rest(i, slot).start()

                @pl.when(i == ni - 1)
                def _():
                    if ni < 3:
                        first.wait()
                    if ni >= 3:
                        to_rest(i, 1 - slot).wait()
                    if ni >= 2:
                        to_rest(i, slot).wait()

    row = pl.BlockSpec((tm, D_MODEL), lambda i, j: (i, 0))
    vec = pl.BlockSpec((1, D_MODEL), lambda i, j: (0, 0))
    act = pl.BlockSpec((None, tm, fj), lambda i, j: (j, i, 0))
    actshape = jax.ShapeDtypeStruct((nj, t, fj), BF16)
    rowf, rowb, vecf = (jax.ShapeDtypeStruct((t, D_MODEL), F32), jax.ShapeDtypeStruct((t, D_MODEL), BF16),
                        jax.ShapeDtypeStruct((1, D_MODEL), F32))
    if have_df:
        out_specs, out_shape = [row, act, act, vec], [rowf, actshape, actshape, vecf]
    else:
        out_specs, out_shape = [row, row, act, act, vec, vec], [rowf, rowb, actshape, actshape, vecf, vecf]
    scratch = [pltpu.VMEM((tm, D_MODEL), F32)]
    if split_first_block:
        out_specs = [pl.BlockSpec((BLK, D_MODEL), lambda i, j: (0, 0)), ANY] + out_specs[1:]
        out_shape = [jax.ShapeDtypeStruct((BLK, D_MODEL), F32), jax.ShapeDtypeStruct((t - BLK, D_MODEL), F32)
                     ] + out_shape[1:]
        scratch += [pltpu.VMEM((2, tm, D_MODEL), F32), pltpu.SemaphoreType.DMA((2,)), pltpu.SemaphoreType.DMA]
    return _pallas(
        body, name="ffn_bwd", grid=(ni, nj),
        in_specs=[row, row, row, act, act, vec] + wspecs + [vec],
        out_specs=out_specs, out_shape=out_shape, scratch_shapes=scratch,
        args=(dhout, h, df if have_df else f, p14, p24, wpre, w3, w3, w3, wpost))


def _ffn_wgrad(n, df, dg4, du4, a4, hook=None):
    t = n.shape[0]
    tm = _contract_tile(t)
    ni = t // tm
    nj, _, fj = dg4.shape

    def body(n_ref, df_ref, dg_ref, du_ref, a_ref, dw_ref, acc):
        i = pl.program_id(1)

        @pl.when(i == 0)
        def _():
            acc[...] = jnp.zeros_like(acc)

        nn = n_ref[...]
        acc[0:fj, :] += _dg(dg_ref[...], nn, TN)
        acc[fj:2 * fj, :] += _dg(du_ref[...], nn, TN)
        acc[2 * fj:3 * fj, :] += _dg(a_ref[...], df_ref[...], TN)

        @pl.when(i == ni - 1)
        def _():
            dw_ref[...] = acc[...].astype(BF16)

    row = pl.BlockSpec((tm, D_MODEL), lambda j, i: (i, 0))
    act = pl.BlockSpec((None, tm, fj), lambda j, i: (j, i, 0))
    return _pallas(
        body, name="ffn_wgrad", grid=(nj, ni),
        in_specs=[row, row, act, act, act],
        out_specs=[pl.BlockSpec((None, 3 * fj, D_MODEL), lambda j, i: (j, 0, 0))],
        out_shape=[jax.ShapeDtypeStruct((nj, 3 * fj, D_MODEL), BF16)],
        scratch_shapes=[pltpu.VMEM((3 * fj, D_MODEL), F32)],
        args=(n, df, dg4, du4, a4), hook=hook)


def _embed_norm(x, meta_buf, w):
    t = x.shape[0] + BLK
    tm = _row_tile(t)
    nblk = tm // BLK
    ni = t // tm
    gather = _GatherChips([meta_buf])

    def body(*refs):
        x_refs = refs[:nblk]
        w_ref, mb_in, h_ref, n_ref, mb_out, mv, msem, send, recv = refs[nblk:]
        step = pl.program_id(0)
        tile = (step + 1) % ni
        hook_refs = ([mb_in], [mb_out], [send, recv])
        steps = (0, 1, ni - 2) if ni >= 3 else (0, 0, 0)
        for at, phase in zip(steps, (gather.start, gather.mid, gather.finish)):
            @pl.when(step == at)
            def _(phase=phase):
                phase(*hook_refs)

        @pl.when(step == 0)
        def _():
            mv[...] = jnp.zeros_like(mv)

        @pl.when(step == ni - 1)
        def _():
            cp = pltpu.make_async_copy(mb_out, mv, msem)
            cp.start()
            cp.wait()

        meta = jnp.concatenate([mv[k] for k in range(N_CHIPS)], axis=1)
        first = jnp.concatenate([jnp.zeros((PAD, D_MODEL), F32), meta], axis=0)
        blocks = [jnp.where(tile == 0, first, x_refs[0][...])] + [r[...] for r in x_refs[1:]]
        h = jnp.concatenate(blocks, axis=0) if nblk > 1 else blocks[0]
        h_ref[...] = h
        y, _, _ = _rms(h, w_ref[...])
        n_ref[...] = y.astype(BF16)

    x_specs = [pl.BlockSpec((BLK, D_MODEL), functools.partial(
        lambda i, k: (jnp.maximum(nblk * ((i + 1) % ni) + k - 1, 0), 0), k=k)) for k in range(nblk)]
    row = pl.BlockSpec((tm, D_MODEL), lambda i: ((i + 1) % ni, 0))
    h0, n0, _ = pl.pallas_call(
        body, name="embed_norm", grid=(ni,),
        in_specs=x_specs + [_full((1, D_MODEL)), ANY], out_specs=[row, row, ANY],
        out_shape=[jax.ShapeDtypeStruct((t, D_MODEL), F32), jax.ShapeDtypeStruct((t, D_MODEL), BF16),
                   jax.ShapeDtypeStruct(meta_buf.shape, meta_buf.dtype)],
        scratch_shapes=[pltpu.VMEM(meta_buf.shape, meta_buf.dtype), pltpu.SemaphoreType.DMA] + list(gather.scratch),
        input_output_aliases={nblk + 1: 2},
        compiler_params=_cparams(1),
    )(*([x] * nblk), w, meta_buf)
    return h0, n0


FWD_RELATION = (None, 0, 1, 2)


def _ffn_fwd_gather(h, n, wbuf, wpost, qc_idx, late):
    t = h.shape[0]
    tm = _row_tile(t)
    ni = t // tm
    nj, rows3, _ = wbuf.shape
    fj = rows3 // 3
    assert nj == N_CHIPS and ni >= 4
    wbufs = [wbuf]
    nw = 1
    n_lin, n_lout = len(late.inputs), len(late.out_shape)
    wait_step = ni - 3

    def body(qc_ref, h_ref, n_ref, wpost_ref, *rest):
        wb_in = rest[:nw]
        lins = rest[nw:nw + n_lin]
        o0 = nw + n_lin
        hout_ref, p1_ref, p2_ref, a_ref, f_hbm = rest[o0:o0 + 5]
        wb = rest[o0 + 5:o0 + 5 + nw]
        louts = rest[o0 + 5 + nw:o0 + 5 + nw + n_lout]
        s0 = o0 + 5 + nw + n_lout
        wv, wsem, send, recv, fbuf, fr_sem, fw_sem = rest[s0:s0 + 7]
        lscr = rest[s0 + 7:]
        p = pl.program_id(0)
        i = pl.program_id(1)
        step = p * ni + i
        fslot = step % 3
        nslot = (step + 1) % 3

        def f_tile(tile):
            return f_hbm.at[pl.ds(pl.multiple_of(tile * tm, 8), tm)]

        @pl.when(step > 1)
        def _():
            pltpu.make_async_copy(fbuf.at[nslot], f_tile(i), fw_sem.at[nslot]).wait()

        nxt = step + 1

        @pl.when((nxt < N_CHIPS * ni) & (nxt >= ni))
        def _():
            pltpu.make_async_copy(f_tile(nxt % ni), fbuf.at[nslot], fr_sem.at[nslot]).start()

        @pl.when(p > 0)
        def _():
            pltpu.make_async_copy(f_tile(i), fbuf.at[fslot], fr_sem.at[fslot]).wait()
        x, y, c, chips = _place()
        q = 2 * x + y
        sibling = (x, y, 1 - c)
        mine, other = _half(rows3, c), _half(rows3, 1 - c)

        def load(chunk, slot, src):
            return [pltpu.make_async_copy(src[t].at[chunk], wv.at[slot, t], wsem.at[slot, t]) for t in range(nw)]

        @pl.when((p == 0) & (i == 0))
        def _():
            for j, (cx, cy) in enumerate(chips):
                for t in range(nw):
                    _remote(send.at[t, j], recv.at[t, j], wb_in[t].at[q, mine], wb[t].at[q, mine], (cx, cy, c)).start()
            for cp in load(q, 0, wb_in):
                cp.start()
            for cp in load(q, 0, wb_in):
                cp.wait()

        @pl.when((p == 1) & (i == 0))
        def _():
            late.start(lins, louts, lscr)

        for pp in range(1, N_CHIPS):
            j = FWD_RELATION[pp]
            cx, cy = chips[j]
            chunk = 2 * cx + cy

            @pl.when((p == pp - 1) & (i == wait_step))
            def _(j=j, cx=cx, cy=cy, chunk=chunk, pp=pp):
                for t in range(nw):
                    got = wb[t].at[chunk, mine]
                    _remote(send.at[t, j], recv.at[t, j], got, got, (cx, cy, c)).wait_recv()
                    _remote(send.at[t, 3 + j], recv.at[t, 3 + j], got, got, sibling).start()
                for t in range(nw):
                    rest_half = wb[t].at[chunk, other]
                    _remote(send.at[t, 3 + j], recv.at[t, 3 + j], rest_half, rest_half, sibling).wait_recv()
                for cp in load(chunk, pp % 2, wb):
                    cp.start()

            @pl.when((p == pp) & (i == 0))
            def _(chunk=chunk, pp=pp):
                for cp in load(chunk, pp % 2, wb):
                    cp.wait()

        @pl.when((p == N_CHIPS - 1) & (i == ni // 2))
        def _():
            late.mid(lins, louts, lscr)

        slot = p % 2
        nn = n_ref[...]
        g = _dg(nn, wv[slot, 0, 0:fj], NT)
        u = _dg(nn, wv[slot, 0, fj:2 * fj], NT)
        sg = _sigmoid(g)
        silu = g * sg
        p1_ref[...] = (u * (sg + silu * (1.0 - sg))).astype(BF16)
        p2_ref[...] = silu.astype(BF16)
        a = (silu * u).astype(BF16)
        a_ref[...] = a
        part = _dot(a, wv[slot, 0, 2 * fj:3 * fj])

        @pl.when(p == 0)
        def _():
            fbuf[fslot] = part

        @pl.when(p > 0)
        def _():
            fbuf[fslot] = fbuf[fslot] + part

        pltpu.make_async_copy(fbuf.at[fslot], f_tile(i), fw_sem.at[fslot]).start()

        @pl.when(p == N_CHIPS - 1)
        def _():
            yv, _, _ = _rms(fbuf[fslot], wpost_ref[...])
            hout_ref[...] = h_ref[...] + 0.5 * yv

        @pl.when((p == N_CHIPS - 1) & (i == ni - 1))
        def _():
            pslot = (step + 2) % 3
            pltpu.make_async_copy(fbuf.at[pslot], f_tile(i), fw_sem.at[pslot]).wait()
            pltpu.make_async_copy(fbuf.at[fslot], f_tile(i), fw_sem.at[fslot]).wait()
            for t in range(nw):
                for j, (cx, cy) in enumerate(chips):
                    sent = wb[t].at[2 * cx + cy, mine]
                    _remote(send.at[t, j], recv.at[t, j], sent, sent, (cx, cy, c)).wait_send()
                    _remote(send.at[t, 3 + j], recv.at[t, 3 + j], sent, sent, sibling).wait_send()
            late.finish(lins, louts, lscr)

    def last_pass_rows(p, i, qc_ref):
        return (jnp.where(p == N_CHIPS - 1, i, 0), 0)

    def chunk_rows(p, i, qc_ref):
        order = ((p & 1) << 1) | (p >> 1)
        return (jnp.bitwise_xor(qc_ref[0], order), i, 0)

    row = pl.BlockSpec((tm, D_MODEL), lambda p, i, qc_ref: (i, 0))
    last_row = pl.BlockSpec((tm, D_MODEL), last_pass_rows)
    act = pl.BlockSpec((None, tm, fj), chunk_rows)
    act_shape = jax.ShapeDtypeStruct((nj, t, fj), BF16)
    res = pl.pallas_call(
        body, name="ffn_fwd_gather",
        grid_spec=pltpu.PrefetchScalarGridSpec(
            num_scalar_prefetch=1, grid=(N_CHIPS, ni),
            in_specs=[last_row, row, pl.BlockSpec((1, D_MODEL), lambda p, i, qc_ref: (0, 0))]
            + [ANY] * (nw + n_lin),
            out_specs=[last_row, act, act, act, ANY] + [ANY] * (nw + n_lout),
            scratch_shapes=[pltpu.VMEM((2, nw, rows3, D_MODEL), BF16), pltpu.SemaphoreType.DMA((2, nw)),
                            pltpu.SemaphoreType.DMA((nw, 6)), pltpu.SemaphoreType.DMA((nw, 6)),
                            pltpu.VMEM((3, tm, D_MODEL), F32), pltpu.SemaphoreType.DMA((3,)),
                            pltpu.SemaphoreType.DMA((3,))] + list(late.scratch)),
        out_shape=[jax.ShapeDtypeStruct((t, D_MODEL), F32), act_shape, act_shape, act_shape,
                   jax.ShapeDtypeStruct((t, D_MODEL), F32)]
        + [jax.ShapeDtypeStruct(b.shape, b.dtype) for b in wbufs] + list(late.out_shape),
        input_output_aliases={**{4 + t: 5 + t for t in range(nw)},
                              **{4 + nw + a: 5 + nw + b for a, b in late.aliases}},
        compiler_params=_cparams(2),
    )(qc_idx, h, n, wpost, *wbufs, *late.inputs)
    return res[:5], res[5:5 + nw], res[5 + nw:]


PASS_RELATION = (2, 0, 1)


def _ffn_wgrad_reduce(n, df, dg4, du4, a4, qc_idx, hook):
    t = n.shape[0]
    tm = _contract_tile(t)
    ni = t // tm
    nj, _, fj = dg4.shape
    assert nj == N_CHIPS
    hrows = 3 * fj // 2
    n_hin, n_hout = len(hook.inputs), len(hook.out_shape)

    def body(qc_ref, n_ref, df_ref, dg_ref, du_ref, a_ref, *rest):
        hins = rest[:n_hin]
        own_ref, others_ref = rest[n_hin:n_hin + 2]
        houts = rest[n_hin + 2:n_hin + 2 + n_hout]
        s0 = n_hin + 2 + n_hout
        acc, stage, land, sumbuf, px_send, px_recv, cs_send, cs_recv, own_sem = rest[s0:s0 + 9]
        hscr = rest[s0 + 9:]
        k_pass = pl.program_id(0)
        i = pl.program_id(1)
        x, y, c, chips = _place()
        mine = pl.ds(pl.multiple_of(c * hrows, 8), hrows)
        other = pl.ds(pl.multiple_of((1 - c) * hrows, 8), hrows)

        def to_owner(k):
            j = PASS_RELATION[k]
            return _remote(cs_send.at[j], cs_recv.at[j], sumbuf.at[k % 2], others_ref.at[j], (*chips[j], c))

        @pl.when((k_pass == 0) & (i == 0))
        def _():
            hook.start(hins, houts, hscr)

        if hook.has_mid:
            @pl.when((k_pass == N_CHIPS // 2) & (i == 0))
            def _():
                hook.mid(hins, houts, hscr)

        @pl.when(i == 0)
        def _():
            acc[...] = jnp.zeros_like(acc)

        nn = n_ref[...]
        acc[0:fj, :] += _dg(dg_ref[...], nn, TN)
        acc[fj:2 * fj, :] += _dg(du_ref[...], nn, TN)
        acc[2 * fj:3 * fj, :] += _dg(a_ref[...], df_ref[...], TN)

        for k in range(N_CHIPS):
            @pl.when((k_pass == k) & (i == ni - 1))
            def _(k=k):
                slot = k % 2
                stage[...] = acc[other, :].astype(BF16)
                swap = _remote(px_send.at[k], px_recv.at[k], stage, land.at[slot], (x, y, 1 - c))
                swap.start()
                swap.wait_recv()
                pair = acc[mine, :] + land[slot].astype(F32)
                if k >= 2:
                    to_owner(k - 2).wait_send()
                sumbuf[slot] = pair.astype(BF16)
                swap.wait_send()
                if k < N_CHIPS - 1:
                    to_owner(k).start()
                else:
                    keep = pltpu.make_async_copy(sumbuf.at[slot], own_ref, own_sem)
                    keep.start()
                    for j in range(N_CHIPS - 1):
                        _remote(cs_send.at[j], cs_recv.at[j], sumbuf.at[0], others_ref.at[j], (*chips[j], c)).wait_recv()
                    to_owner(k - 1).wait_send()
                    keep.wait()
                    hook.finish(hins, houts, hscr)

    def chunk(k_pass, i, qc_ref):
        return (jnp.bitwise_xor(qc_ref[0], N_CHIPS - 1 - k_pass), i, 0)

    row = pl.BlockSpec((tm, D_MODEL), lambda k_pass, i, qc_ref: (i, 0))
    act = pl.BlockSpec((None, tm, fj), chunk)
    res = pl.pallas_call(
        body, name="ffn_wgrad_reduce",
        grid_spec=pltpu.PrefetchScalarGridSpec(
            num_scalar_prefetch=1, grid=(N_CHIPS, ni),
            in_specs=[row, row, act, act, act] + [ANY] * n_hin,
            out_specs=[ANY, ANY] + [ANY] * n_hout,
            scratch_shapes=[pltpu.VMEM((3 * fj, D_MODEL), F32), pltpu.VMEM((hrows, D_MODEL), BF16),
                            pltpu.VMEM((2, hrows, D_MODEL), BF16), pltpu.VMEM((2, hrows, D_MODEL), BF16),
                            pltpu.SemaphoreType.DMA((N_CHIPS,)), pltpu.SemaphoreType.DMA((N_CHIPS,)),
                            pltpu.SemaphoreType.DMA((N_CHIPS - 1,)), pltpu.SemaphoreType.DMA((N_CHIPS - 1,)),
                            pltpu.SemaphoreType.DMA] + list(hook.scratch)),
        out_shape=[jax.ShapeDtypeStruct((hrows, D_MODEL), BF16),
                   jax.ShapeDtypeStruct((N_CHIPS - 1, hrows, D_MODEL), BF16)] + list(hook.out_shape),
        compiler_params=_cparams(2),
    )(qc_idx, n, df, dg4, du4, a4, *hook.inputs)
    return res[0], res[1], res[2:]


def _xty(x, y):
    t, k = x.shape
    n = y.shape[1]
    tm = _contract_tile(t)
    tn = n if n <= 1024 else (896 if n % 896 == 0 else 128)
    steps = t // tm

    def body(x_ref, y_ref, o_ref, acc_ref):
        i = pl.program_id(1)
        part = _dg(x_ref[...], y_ref[...], TN)

        @pl.when(i == 0)
        def _():
            acc_ref[...] = part

        @pl.when(jnp.logical_and(i > 0, i < steps - 1))
        def _():
            acc_ref[...] += part

        @pl.when(i == steps - 1)
        def _():
            o_ref[...] = (acc_ref[...] + part).astype(BF16)

    assert steps > 1
    return pl.pallas_call(
        body, name="xty", grid=(n // tn, steps),
        in_specs=[pl.BlockSpec((tm, k), lambda j, i: (i, 0)), pl.BlockSpec((tm, tn), lambda j, i: (i, j))],
        out_specs=pl.BlockSpec((k, tn), lambda j, i: (0, j)),
        out_shape=jax.ShapeDtypeStruct((k, n), BF16),
        scratch_shapes=[pltpu.VMEM((k, tn), F32)],
        compiler_params=_cparams(2),
    )(x, y)


def _rope_tables(t):
    pos = (jnp.arange(t, dtype=jnp.int32) - PAD).astype(F32)
    inv_freq = 1.0 / (ROPE_THETA ** (jnp.arange(0, SWA_HD, 2, dtype=F32) / SWA_HD))
    half = SWA_HD // 2
    ang = pos[:, None] * jnp.tile(inv_freq, 4)[None, :]
    sign = jnp.tile(jnp.concatenate([-jnp.ones((half,), F32), jnp.ones((half,), F32)]), 2)
    return jnp.cos(ang), jnp.sin(ang) * sign[None, :]


def _rot_half(x, first_half):
    return jnp.where(first_half, pltpu.roll(x, 96, 1), pltpu.roll(x, 32, 1))


def _first_half_mask(rows):
    lane = lax.broadcasted_iota(jnp.int32, (rows, 128), 1)
    return (lane % 64) < 32


def _log_sigmoid(z):
    return jnp.minimum(z, 0.0) - jnp.log(1.0 + jnp.exp(-jnp.abs(z)))


def _mix_proj(h1, wmixpre, winp, wa2p, bap, cos, sin):
    t = h1.shape[0]
    tm = _row_tile(t)

    def body(h_ref, w_ref, win_ref, wa2_ref, ba_ref, cos_ref, sin_ref,
             n_ref, gq_ref, gk_ref, gv_ref, gg_ref, ga_ref, la_ref, sq_ref, sk_ref, sv_ref):
        y, _, _ = _rms(h_ref[...], w_ref[...])
        n = y.astype(BF16)
        n_ref[...] = n
        proj = _dot(n, win_ref[...])
        gq_ref[...] = proj[:, P_GQ:P_GK]
        gk_ref[...] = proj[:, P_GK:P_GV]
        gv_ref[...] = proj[:, P_GV:P_GG]
        gg_ref[...] = proj[:, P_GG:P_GA]
        ga = proj[:, P_GA:P_SQ]
        ga_ref[...] = ga
        z = _dot(ga.astype(BF16), wa2_ref[...]) + ba_ref[...]
        la_ref[...] = _log_sigmoid(z) * (1.0 / GLA_TAU)
        c = cos_ref[...]
        s = sin_ref[...]
        fh = _first_half_mask(tm)
        for k in range(4):
            x = proj[:, P_SQ + 128 * k:P_SQ + 128 * (k + 1)]
            sq_ref[:, 128 * k:128 * (k + 1)] = (x * c + _rot_half(x, fh) * s).astype(BF16)
        for k in range(2):
            x = proj[:, P_SK + 128 * k:P_SK + 128 * (k + 1)]
            sk_ref[:, 128 * k:128 * (k + 1)] = (x * c + _rot_half(x, fh) * s).astype(BF16)
        sv_ref[...] = proj[:, P_SV:P_END].astype(BF16)

    def row(w):
        return pl.BlockSpec((tm, w), lambda i: (i, 0))

    def rshape(w, dt):
        return jax.ShapeDtypeStruct((t, w), dt)

    return pl.pallas_call(
        body, name="mix_proj", grid=(t // tm,),
        in_specs=[row(D_MODEL), _full((1, D_MODEL)), _full((D_MODEL, P_END)), _full((128, GLA_KW)),
                  _full((1, GLA_KW)), row(128), row(128)],
        out_specs=[row(D_MODEL), row(256), row(256), row(512), row(512), row(128), row(256), row(512), row(256),
                   row(256)],
        out_shape=[rshape(D_MODEL, BF16), rshape(256, F32), rshape(256, F32), rshape(512, F32), rshape(512, F32),
                   rshape(128, F32), rshape(256, F32), rshape(512, BF16), rshape(256, BF16), rshape(256, BF16)],
        compiler_params=_cparams(1),
    )(h1, wmixpre, winp, wa2p, bap, cos, sin)


def _scan_rows(x, reverse=False):
    n = x.shape[0]
    row = lax.broadcasted_iota(jnp.int32, x.shape, 0)
    s = 1
    while s < n:
        if reverse:
            x = x + jnp.where(row < n - s, pltpu.roll(x, n - s, 0), 0.0)
        else:
            x = x + jnp.where(row >= s, pltpu.roll(x, s, 0), 0.0)
        s *= 2
    return x


def _gla_cumsum(la, tril_f):
    b = _scan_rows(la)
    row = lax.broadcasted_iota(jnp.int32, b.shape, 0)
    bm = jnp.sum(jnp.where(row == GLA_CHUNK // 2 - 1, b, 0.0), axis=0, keepdims=True)
    bl = jnp.sum(jnp.where(row == GLA_CHUNK - 1, b, 0.0), axis=0, keepdims=True)
    return b, bm, bl


def _gla_decays(la, tril_f):
    b, bm, bl = _gla_cumsum(la, tril_f)
    return jnp.exp(b - bm), jnp.exp(bm - b), jnp.exp(b), jnp.exp(bl - b), jnp.exp(bl)


def _gla_masks():
    c = GLA_CHUNK
    r = lax.broadcasted_iota(jnp.int32, (c, c), 0)
    col = lax.broadcasted_iota(jnp.int32, (c, c), 1)
    r4 = lax.broadcasted_iota(jnp.int32, (GLA_HEADS * c, c), 0) % c
    c4 = lax.broadcasted_iota(jnp.int32, (GLA_HEADS * c, c), 1)
    klane = lax.broadcasted_iota(jnp.int32, (c, GLA_KW), 1) // GLA_DK
    vlane = lax.broadcasted_iota(jnp.int32, (c, GLA_W), 1) // GLA_DV
    srow = lax.broadcasted_iota(jnp.int32, (GLA_W, GLA_KW), 0) // GLA_DV
    scol = lax.broadcasted_iota(jnp.int32, (GLA_W, GLA_KW), 1) // GLA_DK
    return dict(tril_f=(r >= col).astype(F32), triu_f=(r <= col).astype(F32), tril4=r4 >= c4,
                khead=[klane == h for h in range(GLA_HEADS)], vhead=[vlane == h for h in range(GLA_HEADS)],
                diag=srow == scol)


def _stack_heads(x, head_masks):
    return jnp.concatenate([jnp.where(m, x, 0.0) for m in head_masks], axis=0)


def _gla_fwd(gq, gk, gv, la):
    t = gq.shape[0]
    rg = _seq_tile(t)
    nb = t // rg
    ncb = rg // GLA_CHUNK
    c = GLA_CHUNK

    def body(q_ref, k_ref, v_ref, la_ref, o_ref, ss_ref, st_ref):
        @pl.when(pl.program_id(0) == 0)
        def _():
            st_ref[...] = jnp.zeros_like(st_ref)

        mk = _gla_masks()
        st = st_ref[...]
        for ch in range(ncb):
            rows = slice(ch * c, (ch + 1) * c)
            eq, ek, eb, ekl, ebl = _gla_decays(la_ref[rows, :], mk["tril_f"])
            qs = q_ref[rows, :] * (GLA_DK ** -0.5)
            k = k_ref[rows, :]
            v = v_ref[rows, :].astype(BF16)
            ss_ref[ch] = st
            q4 = _stack_heads(qs * eq, mk["khead"]).astype(BF16)
            a4 = jnp.where(mk["tril4"], _dg(q4, (k * ek).astype(BF16), NT), 0.0).astype(BF16)
            r4 = _dot(a4, v)
            intra = jnp.concatenate([r4[h * c:(h + 1) * c, GLA_DV * h:GLA_DV * (h + 1)] for h in range(GLA_HEADS)],
                                    axis=1)
            o_ref[rows, :] = intra + _dg((qs * eb).astype(BF16), st.astype(BF16), NT)
            st = st * ebl + jnp.where(mk["diag"], _dg(v, (k * ekl).astype(BF16), TN), 0.0)
        st_ref[...] = st

    def row(w):
        return pl.BlockSpec((rg, w), lambda i: (i, 0))

    return pl.pallas_call(
        body, name="gla_fwd", grid=(nb,),
        in_specs=[row(256), row(256), row(512), row(256)],
        out_specs=[row(512), pl.BlockSpec((ncb, GLA_W, GLA_KW), lambda i: (i, 0, 0))],
        out_shape=[jax.ShapeDtypeStruct((t, GLA_W), F32), jax.ShapeDtypeStruct((nb * ncb, GLA_W, GLA_KW), F32)],
        scratch_shapes=[pltpu.VMEM((GLA_W, GLA_KW), F32)],
        compiler_params=_cparams(1),
    )(gq, gk, gv, la)


def _gla_bwd(gq, gk, gv, la, ss, do):
    t = gq.shape[0]
    rg = _seq_tile(t)
    nb = t // rg
    ncb = rg // GLA_CHUNK
    c = GLA_CHUNK

    def body(q_ref, k_ref, v_ref, la_ref, ss_ref, do_ref, dq_ref, dk_ref, dv_ref, dla_ref, dst_ref):
        @pl.when(pl.program_id(0) == 0)
        def _():
            dst_ref[...] = jnp.zeros_like(dst_ref)

        mk = _gla_masks()
        last_row = lax.broadcasted_iota(jnp.int32, (c, GLA_KW), 0) == c - 1
        scale = GLA_DK ** -0.5
        dstn = dst_ref[...]
        for ch in reversed(range(ncb)):
            rows = slice(ch * c, (ch + 1) * c)
            eq, ek, eb, ekl, ebl = _gla_decays(la_ref[rows, :], mk["tril_f"])
            qs = q_ref[rows, :] * scale
            k = k_ref[rows, :]
            qt, kt, qh, kh = qs * eq, k * ek, qs * eb, k * ekl
            ktb, khb, qhb = kt.astype(BF16), kh.astype(BF16), qh.astype(BF16)
            v = v_ref[rows, :].astype(BF16)
            do_f = do_ref[rows, :]
            dob = do_f.astype(BF16)
            st = ss_ref[ch]
            stb = st.astype(BF16)
            dstb = dstn.astype(BF16)
            q4 = _stack_heads(qt, mk["khead"]).astype(BF16)
            do4 = _stack_heads(do_f, mk["vhead"]).astype(BF16)
            a4 = jnp.where(mk["tril4"], _dg(q4, ktb, NT), 0.0).astype(BF16)
            da4 = jnp.where(mk["tril4"], _dg(do4, v, NT), 0.0).astype(BF16)
            dv_ref[rows, :] = _dg(a4, do4, TN) + _dg(khb, dstb, NT)
            dq4 = _dot(da4, ktb)
            dqt = jnp.zeros((c, GLA_KW), F32)
            for h in range(GLA_HEADS):
                dqt = dqt + jnp.where(mk["khead"][h], dq4[h * c:(h + 1) * c], 0.0)
            dkt = _dg(da4, q4, TN)
            dqh = _dot(dob, stb)
            dkh = _dot(v, dstb)
            dbl = jnp.sum(dstn * st, axis=0, keepdims=True)
            dstn = dstn * ebl + jnp.where(mk["diag"], _dg(dob, qhb, TN), 0.0)
            dq_ref[rows, :] = scale * (dqt * eq + dqh * eb)
            dk_ref[rows, :] = dkt * ek + dkh * ekl
            dkk = dkh * kh
            db = dqt * qt - dkt * kt + dqh * qh - dkk
            db = db + jnp.where(last_row, jnp.sum(dkk, axis=0, keepdims=True) + ebl * dbl, 0.0)
            dla_ref[rows, :] = _scan_rows(db, reverse=True)
        dst_ref[...] = dstn

    def row(w):
        return pl.BlockSpec((rg, w), lambda i: (nb - 1 - i, 0))

    def rshape(w):
        return jax.ShapeDtypeStruct((t, w), F32)

    return pl.pallas_call(
        body, name="gla_bwd", grid=(nb,),
        in_specs=[row(256), row(256), row(512), row(256),
                  pl.BlockSpec((ncb, GLA_W, GLA_KW), lambda i: (nb - 1 - i, 0, 0)), row(512)],
        out_specs=[row(256), row(256), row(512), row(256)],
        out_shape=[rshape(256), rshape(256), rshape(512), rshape(256)],
        scratch_shapes=[pltpu.VMEM((GLA_W, GLA_KW), F32)],
        compiler_params=_cparams(1),
    )(gq, gk, gv, la, ss, do)


SWA_G = SWA_QH // SWA_KVH


def _swa_bias():
    n = jnp.arange(3, dtype=jnp.int32)[:, None, None]
    r = (jnp.arange(SWA_G * BLK, dtype=jnp.int32) % BLK)[None, :, None]
    c = jnp.arange(3 * BLK, dtype=jnp.int32)[None, None, :]
    seg = c // BLK
    cc = c % BLK
    qpos = n * BLK + r - PAD
    kpos = jnp.where(seg == 0, (n - 1) * BLK, jnp.where(seg == 1, n * BLK, 0)) + cc - PAD
    band = (seg < 2) & (kpos >= N_META) & (kpos <= qpos) & (qpos - kpos < WINDOW)
    meta = (seg == 2) & (kpos >= 0) & (kpos < N_META) & (kpos <= qpos)
    return jnp.where(band | meta, 0.0, NEG_INF).astype(F32)


def _swa_stack(ref, rows, kh, lo, dtype):
    parts = []
    for g in range(2):
        pair = ref[rows, 128 * (2 * kh + g):128 * (2 * kh + g + 1)]
        zero = jnp.zeros_like(pair)
        parts += [jnp.where(lo, pair, zero), jnp.where(lo, zero, pair)]
    return jnp.concatenate(parts, axis=0).astype(dtype)


def _swa_unstack(x4, lo):
    return [jnp.where(lo, x4[2 * g * BLK:(2 * g + 1) * BLK], x4[(2 * g + 1) * BLK:(2 * g + 2) * BLK])
            for g in range(2)]


def _swa_sink_col(sink_ref, kh):
    blk = lax.broadcasted_iota(jnp.int32, (SWA_G * BLK, 1), 0) // BLK
    col = jnp.full((SWA_G * BLK, 1), sink_ref[SWA_G * kh + SWA_G - 1], F32)
    for e in reversed(range(SWA_G - 1)):
        col = jnp.where(blk == e, sink_ref[SWA_G * kh + e], col)
    return col


def _swa_softmax(qk, bias, sink):
    s = qk * (SWA_HD ** -0.5) + bias
    m = jnp.maximum(jnp.max(s, axis=-1, keepdims=True), sink)
    p = jnp.exp(s - m)
    es = jnp.exp(sink - m)
    inv = 1.0 / (jnp.sum(p, axis=-1, keepdims=True) + es)
    return p * inv, es * inv


def _swa_keys(prev_ref, cur_ref, first_ref, b, ls):
    before = prev_ref[:, ls] if b == 0 else cur_ref[(b - 1) * BLK:b * BLK, ls]
    return jnp.concatenate([before, cur_ref[b * BLK:(b + 1) * BLK, ls], first_ref[:, ls]], axis=0)


def _swa_specs(rs, ns):
    bps = rs // BLK
    cur = lambda w: pl.BlockSpec((rs, w), lambda i: (jnp.minimum(i, ns - 1), 0))
    prev = lambda w: pl.BlockSpec((BLK, w), lambda i: (jnp.maximum(jnp.minimum(i, ns - 1) * bps - 1, 0), 0))
    first = lambda w: pl.BlockSpec((BLK, w), lambda i: (0, 0))
    return cur, prev, first


def _swa_fwd(sinks, sq, sk, sv):
    t = sq.shape[0]
    rs = _seq_tile(t)
    bps, ns = rs // BLK, t // rs

    def body(sink_ref, bias_ref, q_ref, kp_ref, kc_ref, km_ref, vp_ref, vc_ref, vm_ref, o_ref):
        i = pl.program_id(0)
        lo = lax.broadcasted_iota(jnp.int32, (BLK, 128), 1) < 64
        sink_cols = [_swa_sink_col(sink_ref, kh) for kh in range(SWA_KVH)]
        chains = [(b, kh) for b in range(bps) for kh in range(SWA_KVH)]
        scores = []
        for b, kh in chains:
            ls = slice(128 * kh, 128 * (kh + 1))
            q4 = _swa_stack(q_ref, slice(b * BLK, (b + 1) * BLK), kh, lo, BF16)
            scores.append(_dg(q4, _swa_keys(kp_ref, kc_ref, km_ref, b, ls), NT))
        probs = []
        for (b, kh), s in zip(chains, scores):
            p, _ = _swa_softmax(s, bias_ref[jnp.minimum(i * bps + b, 2)], sink_cols[kh])
            probs.append(p.astype(BF16))
        for (b, kh), p in zip(chains, probs):
            ls = slice(128 * kh, 128 * (kh + 1))
            rows = slice(b * BLK, (b + 1) * BLK)
            for g, pair in enumerate(_swa_unstack(_dot(p, _swa_keys(vp_ref, vc_ref, vm_ref, b, ls)), lo)):
                o_ref[rows, 128 * (2 * kh + g):128 * (2 * kh + g + 1)] = pair

    cur, prev, first = _swa_specs(rs, ns)
    bias = _swa_bias()
    return pl.pallas_call(
        body, name="swa_fwd", grid=(ns,),
        in_specs=[pl.BlockSpec(memory_space=pltpu.SMEM), _full(bias.shape), cur(512), prev(256), cur(256), first(256),
                  prev(256), cur(256), first(256)],
        out_specs=cur(512),
        out_shape=jax.ShapeDtypeStruct((t, SWA_W), F32),
        compiler_params=_cparams(1),
    )(sinks, bias, sq, sk, sk, sk, sv, sv, sv)


def _swa_bwd(sinks, sq, sk, sv, o, do, hook=None):
    t = sq.shape[0]
    rs = _seq_tile(t)
    bps, ns = rs // BLK, t // rs

    def body(sink_ref, bias_ref, q_ref, kp_ref, kc_ref, km_ref, vp_ref, vc_ref, vm_ref, o_ref, do_ref,
             dq_ref, dk_ref, dv_ref, dkm_ref, dvm_ref, dsink_ref, pk_ref, pv_ref):
        i = pl.program_id(0)

        @pl.when(i == 0)
        def _():
            pk_ref[...] = jnp.zeros_like(pk_ref)
            pv_ref[...] = jnp.zeros_like(pv_ref)
            dkm_ref[...] = jnp.zeros_like(dkm_ref)
            dvm_ref[...] = jnp.zeros_like(dvm_ref)
            dsink_ref[...] = jnp.zeros_like(dsink_ref)

        @pl.when(i == ns)
        def _():
            dk_ref[...] = pk_ref[...]
            dv_ref[...] = pv_ref[...]

        @pl.when(i < ns)
        def _():
            lo = lax.broadcasted_iota(jnp.int32, (BLK, 128), 1) < 64
            scale = SWA_HD ** -0.5
            sink_cols = [_swa_sink_col(sink_ref, kh) for kh in range(SWA_KVH)]
            parts_k = [[None] * SWA_KVH for _ in range(bps)]
            parts_v = [[None] * SWA_KVH for _ in range(bps)]
            dsinks = [jnp.zeros((1, 1), F32) for _ in range(SWA_QH)]
            chains = [(b, kh) for b in range(bps) for kh in range(SWA_KVH)]
            lanes = lambda kh: slice(128 * kh, 128 * (kh + 1))
            block = lambda b: slice(b * BLK, (b + 1) * BLK)
            q4s = [_swa_stack(q_ref, block(b), kh, lo, BF16) for b, kh in chains]
            scores = [_dg(q4, _swa_keys(kp_ref, kc_ref, km_ref, b, lanes(kh)), NT)
                      for (b, kh), q4 in zip(chains, q4s)]
            do4s = [_swa_stack(do_ref, block(b), kh, lo, F32) for b, kh in chains]
            do4bs = [d.astype(BF16) for d in do4s]
            dps = [_dg(d, _swa_keys(vp_ref, vc_ref, vm_ref, b, lanes(kh)), NT) for (b, kh), d in zip(chains, do4bs)]
            pbs, dss = [], []
            for n_chain, (b, kh) in enumerate(chains):
                p, psink = _swa_softmax(scores[n_chain], bias_ref[jnp.minimum(i * bps + b, 2)], sink_cols[kh])
                delta = jnp.sum(do4s[n_chain] * _swa_stack(o_ref, block(b), kh, lo, F32), axis=-1, keepdims=True)
                dss.append((p * (dps[n_chain] - delta) * scale).astype(BF16))
                pbs.append(p.astype(BF16))
                dsk = psink * delta
                for e in range(SWA_G):
                    h = SWA_G * kh + e
                    dsinks[h] = dsinks[h] - jnp.sum(dsk[e * BLK:(e + 1) * BLK], axis=0, keepdims=True)
            for n_chain, (b, kh) in enumerate(chains):
                kall = _swa_keys(kp_ref, kc_ref, km_ref, b, lanes(kh))
                for g, pair in enumerate(_swa_unstack(_dot(dss[n_chain], kall), lo)):
                    dq_ref[block(b), 128 * (2 * kh + g):128 * (2 * kh + g + 1)] = pair
                parts_k[b][kh] = _dg(dss[n_chain], q4s[n_chain], TN)
                parts_v[b][kh] = _dg(pbs[n_chain], do4bs[n_chain], TN)
            last = slice(rs - BLK, rs)
            for parts, out_ref, pend_ref, meta_ref in ((parts_k, dk_ref, pk_ref, dkm_ref),
                                                       (parts_v, dv_ref, pv_ref, dvm_ref)):
                for kh in range(SWA_KVH):
                    ls = slice(128 * kh, 128 * (kh + 1))
                    if bps > 1:
                        out_ref[0:rs - BLK, ls] = pend_ref[0:rs - BLK, ls]
                    out_ref[last, ls] = pend_ref[last, ls] + parts[0][kh][0:BLK]
                    meta = parts[0][kh][2 * BLK:3 * BLK]
                    for b in range(bps):
                        own = parts[b][kh][BLK:2 * BLK]
                        if b + 1 < bps:
                            own = own + parts[b + 1][kh][0:BLK]
                            meta = meta + parts[b + 1][kh][2 * BLK:3 * BLK]
                        pend_ref[b * BLK:(b + 1) * BLK, ls] = own
                    meta_ref[:, ls] += meta
            for h in range(SWA_QH):
                dsink_ref[h:h + 1, :] += jnp.broadcast_to(dsinks[h], (1, 128))

    cur, prev, first = _swa_specs(rs, ns)
    late = lambda w: pl.BlockSpec((rs, w), lambda i: (jnp.maximum(i - 1, 0), 0))
    bias = _swa_bias()
    return _pallas(
        body, name="swa_bwd", grid=(ns + 1,),
        in_specs=[pl.BlockSpec(memory_space=pltpu.SMEM), _full(bias.shape), cur(512), prev(256), cur(256), first(256),
                  prev(256), cur(256), first(256), cur(512), cur(512)],
        out_specs=[cur(512), late(256), late(256), first(256), first(256), _full((SWA_QH, 128))],
        out_shape=[jax.ShapeDtypeStruct((t, SWA_W), F32), jax.ShapeDtypeStruct((t, 256), F32),
                   jax.ShapeDtypeStruct((t, 256), F32), jax.ShapeDtypeStruct((BLK, 256), F32),
                   jax.ShapeDtypeStruct((BLK, 256), F32), jax.ShapeDtypeStruct((SWA_QH, 128), F32)],
        scratch_shapes=[pltpu.VMEM((rs, 256), F32), pltpu.VMEM((rs, 256), F32)],
        args=(sinks, bias, sq, sk, sk, sk, sv, sv, sv, o, do), hook=hook)


def _mix_out(h1, ogla, gg, oswa, wgn, wsn, wout, wpost):
    t = h1.shape[0]
    tm = _row_tile(t)

    def body(h_ref, og_ref, gg_ref, os_ref, wgn_ref, wsn_ref, wout_ref, wpost_ref, h2_ref, cat_ref, m_ref):
        parts = []
        for h in range(GLA_HEADS):
            ls = slice(GLA_DV * h, GLA_DV * (h + 1))
            y, _, _ = _rms(og_ref[:, ls], wgn_ref[...])
            g = gg_ref[:, ls]
            parts.append(y * (g * _sigmoid(g)))
        ys, _, _ = _rms(os_ref[...], wsn_ref[...])
        cat = jnp.concatenate(parts + [ys], axis=1).astype(BF16)
        cat_ref[...] = cat
        m = _dot(cat, wout_ref[...])
        m_ref[...] = m
        y, _, _ = _rms(m, wpost_ref[...])
        h2_ref[...] = h_ref[...] + y

    def row(w):
        return pl.BlockSpec((tm, w), lambda i: (i, 0))

    return pl.pallas_call(
        body, name="mix_out", grid=(t // tm,),
        in_specs=[row(D_MODEL), row(512), row(512), row(512), _full((1, GLA_DV)), _full((1, SWA_W)),
                  _full((D_MODEL, D_MODEL)), _full((1, D_MODEL))],
        out_specs=[row(D_MODEL), row(D_MODEL), row(D_MODEL)],
        out_shape=[jax.ShapeDtypeStruct((t, D_MODEL), F32), jax.ShapeDtypeStruct((t, D_MODEL), BF16),
                   jax.ShapeDtypeStruct((t, D_MODEL), F32)],
        compiler_params=_cparams(1),
    )(h1, ogla, gg, oswa, wgn, wsn, wout, wpost)


def _mix_out_bwd(dh2, m, ogla, gg, oswa, wgn, wsn, wout, wpost, hook=None):
    t = dh2.shape[0]
    tm = _row_tile(t)

    def body(dh_ref, m_ref, og_ref, gg_ref, os_ref, wgn_ref, wsn_ref, wout_ref, wpost_ref,
             dog_ref, dgg_ref, dos_ref, dm_ref, dwpost_ref, dwgn_ref, dwsn_ref):
        @pl.when(pl.program_id(0) == 0)
        def _():
            dwpost_ref[...] = jnp.zeros_like(dwpost_ref)
            dwgn_ref[...] = jnp.zeros_like(dwgn_ref)
            dwsn_ref[...] = jnp.zeros_like(dwsn_ref)

        wpost = wpost_ref[...]
        _, mh, r = _rms(m_ref[...], wpost)
        dm, dw = _rms_bwd(mh, r, wpost, dh_ref[...])
        dwpost_ref[...] += dw
        dmb = dm.astype(BF16)
        dm_ref[...] = dmb
        dcat = _dg(dmb, wout_ref[...], NT)
        wgn = wgn_ref[...]
        for h in range(GLA_HEADS):
            ls = slice(GLA_DV * h, GLA_DV * (h + 1))
            dog = dcat[:, ls]
            g = gg_ref[:, ls]
            sg = _sigmoid(g)
            y, xh, r = _rms(og_ref[:, ls], wgn)
            dgg_ref[:, ls] = dog * y * (sg * (1.0 + g * (1.0 - sg)))
            dx, dw = _rms_bwd(xh, r, wgn, dog * (g * sg))
            dog_ref[:, ls] = dx
            dwgn_ref[...] += dw
        wsn = wsn_ref[...]
        _, xh, r = _rms(os_ref[...], wsn)
        dx, dw = _rms_bwd(xh, r, wsn, dcat[:, GLA_W:])
        dos_ref[...] = dx
        dwsn_ref[...] += dw

    def row(w):
        return pl.BlockSpec((tm, w), lambda i: (i, 0))

    def rshape(w, dt=F32):
        return jax.ShapeDtypeStruct((t, w), dt)

    return _pallas(
        body, name="mix_out_bwd", grid=(t // tm,),
        in_specs=[row(D_MODEL), row(D_MODEL), row(512), row(512), row(512), _full((1, GLA_DV)), _full((1, SWA_W)),
                  _full((D_MODEL, D_MODEL)), _full((1, D_MODEL))],
        out_specs=[row(512), row(512), row(512), row(D_MODEL), _full((1, D_MODEL)), _full((1, GLA_DV)),
                   _full((1, SWA_W))],
        out_shape=[rshape(512), rshape(512), rshape(512), rshape(D_MODEL, BF16),
                   jax.ShapeDtypeStruct((1, D_MODEL), F32), jax.ShapeDtypeStruct((1, GLA_DV), F32),
                   jax.ShapeDtypeStruct((1, SWA_W), F32)],
        args=(dh2, m, ogla, gg, oswa, wgn, wsn, wout, wpost), hook=hook)


def _mix_in_bwd(dh2, h1, wmixpre, winp, wa2p, bap, cos, sin, ga, dgq, dgk, dgv, dgg, dla, dsq, dsk, dsv, dkm, dvm):
    t = h1.shape[0]
    tm = _row_tile(t)

    def body(dh2_ref, h_ref, w_ref, win_ref, wa2_ref, ba_ref, cos_ref, sin_ref, ga_ref, dgq_ref, dgk_ref, dgv_ref,
             dgg_ref, dla_ref, dsq_ref, dsk_ref, dsv_ref, dkm_ref, dvm_ref,
             dh1_ref, dproj_ref, dw_ref, dwa2_ref, dba_ref):
        i = pl.program_id(0)

        @pl.when(i == 0)
        def _():
            dw_ref[...] = jnp.zeros_like(dw_ref)
            dwa2_ref[...] = jnp.zeros_like(dwa2_ref)
            dba_ref[...] = jnp.zeros_like(dba_ref)

        first = (i == 0).astype(F32)
        c = cos_ref[...]
        s = -sin_ref[...]
        fh = _first_half_mask(tm)
        dproj_ref[:, P_GQ:P_GK] = dgq_ref[...].astype(BF16)
        dproj_ref[:, P_GK:P_GV] = dgk_ref[...].astype(BF16)
        dproj_ref[:, P_GV:P_GG] = dgv_ref[...].astype(BF16)
        dproj_ref[:, P_GG:P_GA] = dgg_ref[...].astype(BF16)
        gab = ga_ref[...].astype(BF16)
        z = _dot(gab, wa2_ref[...]) + ba_ref[...]
        row_id = i * tm + lax.broadcasted_iota(jnp.int32, (tm, 1), 0)
        dz = jnp.where(row_id >= PAD, dla_ref[...] * (1.0 / GLA_TAU) * (1.0 - _sigmoid(z)), 0.0)
        dzb = dz.astype(BF16)
        dba_ref[...] += jnp.sum(dz, axis=0, keepdims=True)
        dwa2_ref[...] += _dg(gab, dzb, TN)
        dproj_ref[:, P_GA:P_SQ] = _dg(dzb, wa2_ref[...], NT).astype(BF16)
        for k in range(4):
            dy = dsq_ref[:, 128 * k:128 * (k + 1)]
            dproj_ref[:, P_SQ + 128 * k:P_SQ + 128 * (k + 1)] = (dy * c + _rot_half(dy, fh) * s).astype(BF16)
        for k in range(2):
            ls = slice(128 * k, 128 * (k + 1))
            dy = dsk_ref[:, ls]
            dy = jnp.concatenate([dy[:BLK] + first * dkm_ref[:, ls], dy[BLK:]], axis=0) if tm > BLK else (
                dy + first * dkm_ref[:, ls])
            dproj_ref[:, P_SK + 128 * k:P_SK + 128 * (k + 1)] = (dy * c + _rot_half(dy, fh) * s).astype(BF16)
            dv = dsv_ref[:, ls]
            dv = jnp.concatenate([dv[:BLK] + first * dvm_ref[:, ls], dv[BLK:]], axis=0) if tm > BLK else (
                dv + first * dvm_ref[:, ls])
            dproj_ref[:, P_SV + 128 * k:P_SV + 128 * (k + 1)] = dv.astype(BF16)
        dn = _dg(dproj_ref[...], win_ref[...], NT)
        w = w_ref[...]
        _, hh, r = _rms(h_ref[...], w)
        dx, dw = _rms_bwd(hh, r, w, dn)
        dw_ref[...] += dw
        dh1_ref[...] = dh2_ref[...] + dx

    def row(w):
        return pl.BlockSpec((tm, w), lambda i: (i, 0))

    return pl.pallas_call(
        body, name="mix_in_bwd", grid=(t // tm,),
        in_specs=[row(D_MODEL), row(D_MODEL), _full((1, D_MODEL)), _full((D_MODEL, P_END)), _full((128, GLA_KW)),
                  _full((1, GLA_KW)), row(128), row(128), row(128), row(256), row(256), row(512), row(512), row(256),
                  row(512), row(256), row(256), _full((BLK, 256)), _full((BLK, 256))],
        out_specs=[row(D_MODEL), row(P_END), _full((1, D_MODEL)), _full((128, GLA_KW)), _full((1, GLA_KW))],
        out_shape=[jax.ShapeDtypeStruct((t, D_MODEL), F32), jax.ShapeDtypeStruct((t, P_END), BF16),
                   jax.ShapeDtypeStruct((1, D_MODEL), F32), jax.ShapeDtypeStruct((128, GLA_KW), F32),
                   jax.ShapeDtypeStruct((1, GLA_KW), F32)],
        compiler_params=_cparams(1),
    )(dh2, h1, wmixpre, winp, wa2p, bap, cos, sin, ga, dgq, dgk, dgv, dgg, dla, dsq, dsk, dsv, dkm, dvm)


def _adamw_update(w, g, m, v):
    m = ADAM_B1 * m + (1.0 - ADAM_B1) * g
    v = ADAM_B2 * v + (1.0 - ADAM_B2) * (g * g)
    m_hat = m / (1.0 - ADAM_B1 ** ADAM_STEP)
    v_hat = v / (1.0 - ADAM_B2 ** ADAM_STEP)
    return -ADAM_LR * (m_hat / (jnp.sqrt(v_hat) + ADAM_EPS) + ADAM_WD * w), m, v


def _adamw_halves(w, g_mine, g_other, m, v, c_idx, row0=0):
    r, c = w.shape
    h = g_mine.shape[0]
    tr = _div_tile(math.gcd(r, h), cap=ADAMW_TILE_ROWS)
    nth = h // tr
    t0 = row0 // tr
    assert t0 * tr == row0

    def body(c_ref, w_ref, gm_ref, go_ref, m_ref, v_ref, g_ref, d_ref, nm_ref, nv_ref):
        hh = (t0 + pl.program_id(0)) // nth
        g = jnp.where(hh == c_ref[0], gm_ref[...], go_ref[...])
        g_ref[...] = g
        d_ref[...], nm_ref[...], nv_ref[...] = _adamw_update(w_ref[...], g, m_ref[...], v_ref[...])

    spec = pl.BlockSpec((tr, c), lambda i, c_ref: (i, 0))

    def gspec(is_mine):
        def index(i, c_ref):
            used = ((t0 + i) // nth == c_ref[0]) == is_mine
            return (jnp.where(used, (t0 + i) % nth, 0), 0)
        return pl.BlockSpec((tr, c), index)

    shape = jax.ShapeDtypeStruct((r, c), F32)
    return pl.pallas_call(
        body, name="adamw_halves",
        grid_spec=pltpu.PrefetchScalarGridSpec(
            num_scalar_prefetch=1, grid=(r // tr,), in_specs=[spec, gspec(True), gspec(False), spec, spec],
            out_specs=[spec] * 4),
        out_shape=[shape] * 4, compiler_params=_cparams(1),
    )(c_idx, w, g_mine, g_other, m, v)


def _place():
    x, y, c = lax.axis_index("x"), lax.axis_index("y"), lax.axis_index("c")
    chips = [(1 - x, y), (x, 1 - y), (1 - x, 1 - y)]
    return x, y, c, chips


def _remote(send_sem, recv_sem, src, dst, to):
    return pltpu.make_async_remote_copy(src_ref=src, dst_ref=dst, send_sem=send_sem, recv_sem=recv_sem,
                                        device_id=to, device_id_type=MESH)


def _half(ref_rows, c):
    h = ref_rows // 2
    return pl.ds(pl.multiple_of(c * h, 8), h)


def _own_slot(shard, q):
    return lax.dynamic_update_slice(jnp.zeros((N_CHIPS,) + shard.shape, shard.dtype), shard[None], (q, 0, 0))


def _stack_own_slot(mats, q_idx):
    r, w = mats[0].shape
    tr = _div_tile(r)
    per = r // tr
    n = len(mats)

    def body(q_ref, *refs):
        m_refs, o_ref = refs[:n], refs[n]
        s = pl.program_id(0)
        for k in range(n):
            @pl.when(s // per == k)
            def _(k=k):
                o_ref[...] = m_refs[k][...].astype(BF16)

    def rows_of(k):
        return lambda s, q_ref: (jnp.clip(s - k * per, 0, per - 1), 0)

    return pl.pallas_call(
        body, name="stack_own_slot",
        grid_spec=pltpu.PrefetchScalarGridSpec(
            num_scalar_prefetch=1, grid=(n * per,),
            in_specs=[pl.BlockSpec((tr, w), rows_of(k)) for k in range(n)],
            out_specs=pl.BlockSpec((None, tr, w), lambda s, q_ref: (q_ref[0], s, 0))),
        out_shape=jax.ShapeDtypeStruct((N_CHIPS, n * r, w), BF16), compiler_params=_cparams(1),
    )(q_idx, *mats)


class _GatherChips:
    has_mid = True

    def __init__(self, bufs):
        n = len(bufs)
        self.inputs = list(bufs)
        self.out_shape = [jax.ShapeDtypeStruct(b.shape, b.dtype) for b in bufs]
        self.aliases = [(t, t) for t in range(n)]
        self.scratch = [pltpu.SemaphoreType.DMA((n, 6)), pltpu.SemaphoreType.DMA((n, 6))]

    def start(self, ins, outs, scr):
        send, recv = scr
        x, y, c, chips = _place()
        q = 2 * x + y
        for t, (i_ref, o_ref) in enumerate(zip(ins, outs)):
            rows = _half(i_ref.shape[1], c)
            for j, (cx, cy) in enumerate(chips):
                _remote(send.at[t, j], recv.at[t, j], i_ref.at[q, rows], o_ref.at[q, rows], (cx, cy, c)).start()

    def mid(self, ins, outs, scr):
        send, recv = scr
        x, y, c, chips = _place()
        for t, o_ref in enumerate(outs):
            rows = _half(o_ref.shape[1], c)
            for j, (cx, cy) in enumerate(chips):
                slot = o_ref.at[2 * cx + cy, rows]
                _remote(send.at[t, j], recv.at[t, j], slot, slot, (cx, cy, c)).wait_recv()
                _remote(send.at[t, 3 + j], recv.at[t, 3 + j], slot, slot, (x, y, 1 - c)).start()

    def finish(self, ins, outs, scr):
        send, recv = scr
        x, y, c, chips = _place()
        for t, o_ref in enumerate(outs):
            mine, other = _half(o_ref.shape[1], c), _half(o_ref.shape[1], 1 - c)
            for j, (cx, cy) in enumerate(chips):
                slot = o_ref.at[2 * cx + cy, other]
                _remote(send.at[t, 3 + j], recv.at[t, 3 + j], slot, slot, (x, y, 1 - c)).wait_recv()
            for j, (cx, cy) in enumerate(chips):
                sent = o_ref.at[2 * cx + cy, mine]
                _remote(send.at[t, j], recv.at[t, j], sent, sent, (cx, cy, c)).wait_send()
                _remote(send.at[t, 3 + j], recv.at[t, 3 + j], sent, sent, (x, y, 1 - c)).wait_send()


class _PairExchange:
    has_mid = False
    aliases = ()

    def __init__(self, arrs):
        n = len(arrs)
        self.inputs = list(arrs)
        self.out_shape = [jax.ShapeDtypeStruct((a.shape[0], a.shape[1] // 2, a.shape[2]), a.dtype) for a in arrs]
        self.scratch = [pltpu.SemaphoreType.DMA((n,)), pltpu.SemaphoreType.DMA((n,))]

    def _copies(self, ins, outs, scr):
        send, recv = scr
        x, y, c, _ = _place()
        return [_remote(send.at[t], recv.at[t], i_ref.at[:, _half(i_ref.shape[1], 1 - c)], o_ref, (x, y, 1 - c))
                for t, (i_ref, o_ref) in enumerate(zip(ins, outs))]

    def start(self, ins, outs, scr):
        for cp in self._copies(ins, outs, scr):
            cp.start()

    def finish(self, ins, outs, scr):
        for cp in self._copies(ins, outs, scr):
            cp.wait()


class _ChipScatter:
    has_mid = False
    aliases = ()

    def __init__(self, arrs):
        n = len(arrs)
        self.inputs = list(arrs)
        self.out_shape = [jax.ShapeDtypeStruct((3,) + a.shape[1:], a.dtype) for a in arrs]
        self.scratch = [pltpu.SemaphoreType.DMA((n, 3)), pltpu.SemaphoreType.DMA((n, 3))]

    def _copies(self, ins, outs, scr):
        send, recv = scr
        x, y, c, chips = _place()
        return [_remote(send.at[t, j], recv.at[t, j], i_ref.at[2 * cx + cy], o_ref.at[j], (cx, cy, c))
                for t, (i_ref, o_ref) in enumerate(zip(ins, outs)) for j, (cx, cy) in enumerate(chips)]

    def start(self, ins, outs, scr):
        for cp in self._copies(ins, outs, scr):
            cp.start()

    def finish(self, ins, outs, scr):
        for cp in self._copies(ins, outs, scr):
            cp.wait()


class _PairShare:
    has_mid = False
    aliases = ()

    def __init__(self, arrs):
        n = len(arrs)
        self.inputs = list(arrs)
        self.out_shape = [jax.ShapeDtypeStruct(a.shape, a.dtype) for a in arrs]
        self.scratch = [pltpu.SemaphoreType.DMA((n,)), pltpu.SemaphoreType.DMA((n,))]

    def _copies(self, ins, outs, scr):
        send, recv = scr
        x, y, c, _ = _place()
        return [_remote(send.at[t], recv.at[t], i_ref, o_ref, (x, y, 1 - c))
                for t, (i_ref, o_ref) in enumerate(zip(ins, outs))]

    def start(self, ins, outs, scr):
        for cp in self._copies(ins, outs, scr):
            cp.start()

    def finish(self, ins, outs, scr):
        for cp in self._copies(ins, outs, scr):
            cp.wait()


def _comm_call(hook, name):
    n_in, n_out = len(hook.inputs), len(hook.out_shape)

    def body(*refs):
        ins, outs, scr = refs[:n_in], refs[n_in:n_in + n_out], refs[n_in + n_out:]
        hook.start(ins, outs, scr)
        if hook.has_mid:
            hook.mid(ins, outs, scr)
        hook.finish(ins, outs, scr)

    return pl.pallas_call(body, name=name, in_specs=[ANY] * n_in, out_specs=[ANY] * n_out,
                          out_shape=list(hook.out_shape), scratch_shapes=list(hook.scratch),
                          input_output_aliases=dict(hook.aliases))(*hook.inputs)


class _GatherDevices:
    has_mid = True
    aliases = ()

    def __init__(self, vecs):
        n = len(vecs)
        self.inputs = list(vecs)
        self.out_shape = [jax.ShapeDtypeStruct((N_DEV,) + v.shape, v.dtype) for v in vecs]
        self.scratch = [pltpu.SemaphoreType.DMA((n, 7)), pltpu.SemaphoreType.DMA((n, 7)),
                        pltpu.SemaphoreType.DMA((n,))]

    @staticmethod
    def _copy(scr, t, k, out_ref, block, to, src=None):
        send, recv, _ = scr
        px, py, pc = block
        slot = out_ref.at[4 * px + 2 * py + pc]
        return _remote(send.at[t, k], recv.at[t, k], slot if src is None else src, slot, to)

    def start(self, ins, outs, scr):
        x, y, c, chips = _place()
        me = (x, y, c)
        for t, (x_ref, out_ref) in enumerate(zip(ins, outs)):
            pltpu.make_async_copy(x_ref, out_ref.at[4 * x + 2 * y + c], scr[2].at[t]).start()
            self._copy(scr, t, 0, out_ref, me, (x, y, 1 - c), src=x_ref).start()
            for j, chip in enumerate(chips):
                self._copy(scr, t, 1 + j, out_ref, me, (*chip, c), src=x_ref).start()

    def mid(self, ins, outs, scr):
        x, y, c, chips = _place()
        for t, out_ref in enumerate(outs):
            for j, chip in enumerate(chips):
                self._copy(scr, t, 1 + j, out_ref, (*chip, c), (x, y, c)).wait_recv()
                self._copy(scr, t, 4 + j, out_ref, (*chip, c), (x, y, 1 - c)).start()

    def finish(self, ins, outs, scr):
        x, y, c, chips = _place()
        me = (x, y, c)
        for t, (x_ref, out_ref) in enumerate(zip(ins, outs)):
            self._copy(scr, t, 0, out_ref, (x, y, 1 - c), me).wait_recv()
            for j, chip in enumerate(chips):
                self._copy(scr, t, 4 + j, out_ref, (*chip, 1 - c), me).wait_recv()
            self._copy(scr, t, 0, out_ref, me, (x, y, 1 - c), src=x_ref).wait_send()
            for j, chip in enumerate(chips):
                self._copy(scr, t, 1 + j, out_ref, me, (*chip, c), src=x_ref).wait_send()
                self._copy(scr, t, 4 + j, out_ref, (*chip, c), (x, y, 1 - c)).wait_send()
            pltpu.make_async_copy(x_ref, out_ref.at[4 * x + 2 * y + c], scr[2].at[t]).wait()


class _Hooks:
    def __init__(self, hooks):
        self.hooks = list(hooks)
        self.has_mid = any(h.has_mid for h in hooks)
        self.inputs = [a for h in hooks for a in h.inputs]
        self.out_shape = [s for h in hooks for s in h.out_shape]
        self.scratch = [s for h in hooks for s in h.scratch]
        self.aliases = []
        i0 = o0 = 0
        for h in hooks:
            self.aliases += [(i0 + a, o0 + b) for a, b in h.aliases]
            i0 += len(h.inputs)
            o0 += len(h.out_shape)

    def _each(self, ins, outs, scr):
        i0 = o0 = s0 = 0
        for h in self.hooks:
            ni, no, ns = len(h.inputs), len(h.out_shape), len(h.scratch)
            yield h, ins[i0:i0 + ni], outs[o0:o0 + no], scr[s0:s0 + ns]
            i0, o0, s0 = i0 + ni, o0 + no, s0 + ns

    def start(self, ins, outs, scr):
        for h, i, o, s in self._each(ins, outs, scr):
            h.start(i, o, s)

    def mid(self, ins, outs, scr):
        for h, i, o, s in self._each(ins, outs, scr):
            if h.has_mid:
                h.mid(i, o, s)

    def finish(self, ins, outs, scr):
        for h, i, o, s in self._each(ins, outs, scr):
            h.finish(i, o, s)

    def split(self, outs):
        res, o0 = [], 0
        for h in self.hooks:
            res.append(list(outs[o0:o0 + len(h.out_shape)]))
            o0 += len(h.out_shape)
        return res


def _pair_sum(g, other, c_idx):
    nq, r, w = g.shape
    h = r // 2
    tr = _div_tile(h)
    nt = h // tr

    def body(c_ref, g_ref, o_ref, s_ref):
        s_ref[...] = (g_ref[...].astype(F32) + o_ref[...].astype(F32)).astype(s_ref.dtype)

    return pl.pallas_call(
        body, name="pair_sum",
        grid_spec=pltpu.PrefetchScalarGridSpec(
            num_scalar_prefetch=1, grid=(nq, nt),
            in_specs=[pl.BlockSpec((None, tr, w), lambda k, i, c_ref: (k, c_ref[0] * nt + i, 0)),
                      pl.BlockSpec((None, tr, w), lambda k, i, c_ref: (k, i, 0))],
            out_specs=pl.BlockSpec((None, tr, w), lambda k, i, c_ref: (k, i, 0))),
        out_shape=jax.ShapeDtypeStruct((nq, h, w), g.dtype),
        compiler_params=_cparams(2),
    )(c_idx, g, other)


def _chip_sum(s, others, q_idx):
    _, h, w = s.shape
    tr = _div_tile(h)

    def body(q_ref, s_ref, o_ref, out_ref):
        out_ref[...] = ((s_ref[...].astype(F32) + o_ref[0].astype(F32)) + o_ref[1].astype(F32)) + o_ref[2].astype(F32)

    return pl.pallas_call(
        body, name="chip_sum",
        grid_spec=pltpu.PrefetchScalarGridSpec(
            num_scalar_prefetch=1, grid=(h // tr,),
            in_specs=[pl.BlockSpec((None, tr, w), lambda i, q_ref: (q_ref[0], i, 0)),
                      pl.BlockSpec((3, tr, w), lambda i, q_ref: (0, i, 0))],
            out_specs=pl.BlockSpec((tr, w), lambda i, q_ref: (i, 0))),
        out_shape=jax.ShapeDtypeStruct((h, w), F32),
        compiler_params=_cparams(1),
    )(q_idx, s, others)


def _small_update(q_idx, parts, ws, ms, vs, col_block):
    n = len(parts)
    has_w = [w is not None for w in ws]

    def body(q_ref, *refs):
        pos = 0
        ins = []
        for t in range(n):
            k = 4 if has_w[t] else 1
            ins.append(refs[pos:pos + k])
            pos += k
        outs = refs[pos:]
        opos = 0
        for t in range(n):
            p_ref = ins[t][0]
            g = p_ref[0]
            for s in range(1, p_ref.shape[0]):
                g = g + p_ref[s]
            if has_w[t]:
                _, w_ref, m_ref, v_ref = ins[t]
                g_ref, d_ref, nm_ref, nv_ref = outs[opos:opos + 4]
                opos += 4
                g_ref[...] = g
                d_ref[...], nm_ref[...], nv_ref[...] = _adamw_update(w_ref[...], g, m_ref[...], v_ref[...])
            else:
                outs[opos][...] = g
                opos += 1

    def whole(shape):
        nd = len(shape)
        return pl.BlockSpec(shape, lambda i, q_ref: (0,) * nd)

    in_specs, out_specs, out_shape, args = [], [], [], []
    for t in range(n):
        k, r, wf = parts[t].shape
        if col_block[t]:
            w = wf // N_CHIPS
            in_specs.append(pl.BlockSpec((k, r, w), lambda i, q_ref: (0, 0, q_ref[0])))
        else:
            w = wf
            in_specs.append(whole((k, r, wf)))
        args.append(parts[t])
        if has_w[t]:
            assert ws[t].shape == (r, w), (ws[t].shape, r, w)
            in_specs += [whole((r, w))] * 3
            args += [ws[t], ms[t], vs[t]]
            out_specs += [whole((r, w))] * 4
            out_shape += [jax.ShapeDtypeStruct((r, w), F32)] * 4
        else:
            out_specs.append(whole((r, w)))
            out_shape.append(jax.ShapeDtypeStruct((r, w), F32))
    return pl.pallas_call(
        body, name="small_update",
        grid_spec=pltpu.PrefetchScalarGridSpec(num_scalar_prefetch=1, grid=(1,), in_specs=in_specs,
                                               out_specs=out_specs),
        out_shape=out_shape, compiler_params=_cparams(1),
    )(q_idx, *args)


_PACK_SEGMENTS = ((0, 1552), None, (1552, 2064), (2064, 2128), (2064, 2128), (2128, 2192), (2128, 2192),
                  (2192, 2256), (2192, 2256), (2256, 2320), (2256, 2320))
_UNPACK_SEGMENTS = (((0, 1552), (0,)), ((1552, 2064), (P_SQ,)), ((2064, 2128), (P_SK, P_SK + 64)),
                    ((2128, 2192), (P_SK + 128, P_SK + 192)), ((2192, 2256), (P_SV, P_SV + 64)),
                    ((2256, 2320), (P_SV + 128, P_SV + 192)))


def _pack_win(w4):
    per = w4.shape[2]
    pieces = []
    for seg in _PACK_SEGMENTS:
        if seg is None:
            pieces.append(jnp.zeros((w4.shape[1], 128 - GLA_RANK), w4.dtype))
            continue
        for q in range(w4.shape[0]):
            lo, hi = max(seg[0], q * per), min(seg[1], (q + 1) * per)
            if lo < hi:
                pieces.append(w4[q][:, lo - q * per:hi - q * per])
    return jnp.concatenate(pieces, axis=1)


def _unpack_dwin(d):
    per = D_IN // N_CHIPS
    chips = []
    for q in range(N_CHIPS):
        pieces = []
        for (a, b), starts in _UNPACK_SEGMENTS:
            lo, hi = max(a, q * per), min(b, (q + 1) * per)
            if lo < hi:
                copies = [d[:, s + lo - a:s + hi - a] for s in starts]
                pieces.append(copies[0] if len(copies) == 1 else copies[0] + copies[1])
        chips.append(jnp.concatenate(pieces, axis=1))
    return jnp.stack(chips)


def _local_step(x, target, meta, p):
    s = x.shape[0]
    t = s + BLK
    h0 = jnp.concatenate([jnp.zeros((PAD, D_MODEL), F32), meta, x], axis=0)
    cos, sin = _rope_tables(t)

    h1, n1, g1, u1, a1, f1 = _ffn_fwd(h0, p["ffn1_pre_norm"], p["ffn1_w"], p["ffn1_post_norm"])
    n2, gq, gk, gv, gg, ga, la, sq, sk, sv = _mix_proj(h1, p["mix_pre_norm"], p["w_in"], p["gla_w_a2"], p["gla_b_a"],
                                                       cos, sin)
    ogla, ss = _gla_fwd(gq, gk, gv, la)
    oswa = _swa_fwd(p["swa_sinks"], sq, sk, sv)
    h2, cat, m = _mix_out(h1, ogla, gg, oswa, p["gla_out_norm"], p["swa_out_norm"], p["w_out"], p["mix_post_norm"])
    grads = {}
    dy, n3, g3, u3, a3, df3, grads["ffn2_post_norm"], sse = _ffn_fwd(
        h2, p["ffn2_pre_norm"], p["ffn2_w"], p["ffn2_post_norm"], target=target)

    dh2, dg3, du3, grads["ffn2_pre_norm"] = _ffn_bwd(
        dy, h2, None, g3, u3, p["ffn2_pre_norm"], p["ffn2_w"], p["ffn2_post_norm"], df=df3)
    (gud,) = _ffn_wgrad(n3, df3, dg3, du3, a3)
    grads["ffn2_w_gate"], grads["ffn2_w_up"], grads["ffn2_w_down"] = gud[:, :FJ], gud[:, FJ:2 * FJ], gud[:, 2 * FJ:]

    dogla, dgg, doswa, dm, grads["mix_post_norm"], grads["gla_out_norm"], grads["swa_out_norm"] = _mix_out_bwd(
        dh2, m, ogla, gg, oswa, p["gla_out_norm"], p["swa_out_norm"], p["w_out"], p["mix_post_norm"])
    grads["w_out"] = _xty(cat, dm)
    dsq, dsk, dsv, dkm, dvm, dsinks = _swa_bwd(p["swa_sinks"], sq, sk, sv, oswa, doswa)
    grads["swa_sinks"] = dsinks[:, 0]
    dgq, dgk, dgv, dla = _gla_bwd(gq, gk, gv, la, ss, dogla)
    dh1, dproj, grads["mix_pre_norm"], dwa2p, grads["gla_b_a"] = _mix_in_bwd(
        dh2, h1, p["mix_pre_norm"], p["w_in"], p["gla_w_a2"], p["gla_b_a"], cos, sin, ga, dgq, dgk, dgv, dgg, dla,
        dsq, dsk, dsv, dkm, dvm)
    grads["gla_w_a2"] = dwa2p[:GLA_RANK]
    grads["w_in"] = _unpack_dwin(_xty(n2, dproj))

    dh0, df1, dg1, du1, grads["ffn1_pre_norm"], grads["ffn1_post_norm"] = _ffn_bwd(
        dh1, h0, f1, g1, u1, p["ffn1_pre_norm"], p["ffn1_w"], p["ffn1_post_norm"])
    (gud,) = _ffn_wgrad(n1, df1, dg1, du1, a1)
    grads["ffn1_w_gate"], grads["ffn1_w_up"], grads["ffn1_w_down"] = gud[:, :FJ], gud[:, FJ:2 * FJ], gud[:, 2 * FJ:]
    grads["meta_tokens"] = dh0[PAD:BLK]
    return sse[0, 0], dh0[BLK:], grads


WEIGHTS = ['meta_tokens', 'ffn1_pre_norm', 'ffn1_w_gate', 'ffn1_w_up', 'ffn1_w_down', 'ffn1_post_norm',
           'mix_pre_norm', 'w_in', 'gla_w_a2', 'gla_b_a', 'gla_out_norm', 'swa_sinks', 'swa_out_norm', 'w_out',
           'mix_post_norm', 'ffn2_pre_norm', 'ffn2_w_gate', 'ffn2_w_up', 'ffn2_w_down', 'ffn2_post_norm']
BIG = ['ffn1_w_gate', 'ffn1_w_up', 'ffn1_w_down', 'w_in', 'w_out', 'ffn2_w_gate', 'ffn2_w_up', 'ffn2_w_down']
SMALL = [n for n in WEIGHTS if n not in BIG]
FJ = D_FF // N_CHIPS
D_IN_J = D_IN // N_CHIPS
D_OUT_J = D_MODEL // N_CHIPS
TRANSPOSED = ('ffn1_w_gate', 'ffn1_w_up', 'ffn2_w_gate', 'ffn2_w_up')


def _shard2d(name, a):
    return a[0].T if name in TRANSPOSED else a[0]


def _unshard2d(name, a):
    return (a.T if name in TRANSPOSED else a)[None]


def kernel(x, meta_tokens, ffn1_pre_norm, ffn1_w_gate, ffn1_w_up, ffn1_w_down, ffn1_post_norm, mix_pre_norm, w_in, gla_w_a2, gla_b_a, gla_out_norm, swa_sinks, swa_out_norm, w_out, mix_post_norm, ffn2_pre_norm, ffn2_w_gate, ffn2_w_up, ffn2_w_down, ffn2_post_norm, loss_target, m_meta_tokens, m_ffn1_pre_norm, m_ffn1_w_gate, m_ffn1_w_up, m_ffn1_w_down, m_ffn1_post_norm, m_mix_pre_norm, m_w_in, m_gla_w_a2, m_gla_b_a, m_gla_out_norm, m_swa_sinks, m_swa_out_norm, m_w_out, m_mix_post_norm, m_ffn2_pre_norm, m_ffn2_w_gate, m_ffn2_w_up, m_ffn2_w_down, m_ffn2_post_norm, v_meta_tokens, v_ffn1_pre_norm, v_ffn1_w_gate, v_ffn1_w_up, v_ffn1_w_down, v_ffn1_post_norm, v_mix_pre_norm, v_w_in, v_gla_w_a2, v_gla_b_a, v_gla_out_norm, v_swa_sinks, v_swa_out_norm, v_w_out, v_mix_post_norm, v_ffn2_pre_norm, v_ffn2_w_gate, v_ffn2_w_up, v_ffn2_w_down, v_ffn2_post_norm):
    args = dict(locals())
    w = {n: args[n] for n in WEIGHTS}
    mom = {n: args["m_" + n] for n in WEIGHTS}
    var = {n: args["v_" + n] for n in WEIGHTS}
    cx, cy, cc = lax.axis_index("x"), lax.axis_index("y"), lax.axis_index("c")
    q_idx = (2 * cx + cy).astype(jnp.int32).reshape(1)
    c_idx = cc.astype(jnp.int32).reshape(1)

    q_chip = 2 * cx + cy
    bf = {n: _own_slot(_shard2d(n, w[n]).astype(BF16), q_chip) for n in ("w_in", "w_out")}
    for ffn in ("ffn1", "ffn2"):
        bf[ffn] = _stack_own_slot([_shard2d(ffn + s, w[ffn + s]) for s in ("_w_gate", "_w_up", "_w_down")], q_idx)
    qc_idx = jnp.stack([q_chip, cc]).astype(jnp.int32)
    sinks = w["swa_sinks"].reshape(SWA_QH)

    seq, target = x[0], loss_target[0]
    t = seq.shape[0] + BLK
    h0, n1 = _embed_norm(seq, _own_slot(w["meta_tokens"], q_chip), w["ffn1_pre_norm"])
    cos, sin = _rope_tables(t)
    late = _GatherChips([bf["w_in"], bf["w_out"], bf["ffn2"],
                         _own_slot(w["gla_w_a2"].reshape(GLA_RANK, GLA_KW // N_CHIPS), q_chip)])
    (h1, g1, u1, a1, f1), (w31,), (win4, wout4, w32, wa24) = _ffn_fwd_gather(
        h0, n1, bf["ffn1"], w["ffn1_post_norm"], qc_idx, late)
    wa2p = jnp.pad(wa24.transpose(1, 0, 2).reshape(GLA_RANK, GLA_KW), ((0, 128 - GLA_RANK), (0, 0))).astype(BF16)
    winp = _pack_win(win4)
    wout = wout4.reshape(D_MODEL, D_MODEL)
    n2, gq, gk, gv, gg, ga, la, sq, sk, sv = _mix_proj(h1, w["mix_pre_norm"], winp, wa2p, w["gla_b_a"], cos, sin)
    ogla, ss = _gla_fwd(gq, gk, gv, la)
    oswa = _swa_fwd(sinks, sq, sk, sv)
    h2, cat, m = _mix_out(h1, ogla, gg, oswa, w["gla_out_norm"], w["swa_out_norm"], wout, w["mix_post_norm"])
    g = {}
    dy, n3, g3, u3, a3, df3, g["ffn2_post_norm"], sse = _ffn_fwd(
        h2, w["ffn2_pre_norm"], w32, w["ffn2_post_norm"], target=target)

    dh2, dg3, du3, g["ffn2_pre_norm"] = _ffn_bwd(
        dy, h2, None, g3, u3, w["ffn2_pre_norm"], w32, w["ffn2_post_norm"], df=df3)
    (gf2,) = _ffn_wgrad(n3, df3, dg3, du3, a3)
    (dogla, dgg, doswa, dm, g["mix_post_norm"], g["gla_out_norm"], g["swa_out_norm"]), (rgf2,) = _mix_out_bwd(
        dh2, m, ogla, gg, oswa, w["gla_out_norm"], w["swa_out_norm"], wout, w["mix_post_norm"],
        hook=_PairExchange([gf2]))
    sgf2 = _pair_sum(gf2, rgf2, c_idx)
    gout = _xty(cat, dm).reshape(N_CHIPS, D_OUT_J, D_MODEL)
    (dsq, dsk, dsv, dkm, dvm, dsinks), (ogf2,) = _swa_bwd(sinks, sq, sk, sv, oswa, doswa,
                                                          hook=_ChipScatter([sgf2]))
    g["swa_sinks"] = dsinks
    dgq, dgk, dgv, dla = _gla_bwd(gq, gk, gv, la, ss, dogla)
    dh1, dproj, g["mix_pre_norm"], dwa2p, g["gla_b_a"] = _mix_in_bwd(
        dh2, h1, w["mix_pre_norm"], winp, wa2p, w["gla_b_a"], cos, sin, ga, dgq, dgk, dgv, dgg, dla,
        dsq, dsk, dsv, dkm, dvm)
    g["gla_w_a2"] = dwa2p[:GLA_RANK]
    gin = _unpack_dwin(_xty(n2, dproj))
    rgin, rgout = _comm_call(_PairExchange([gin, gout]), "pair_exchange")
    dh_first, grad_x, df1, dg1, du1, g["ffn1_pre_norm"], g["ffn1_post_norm"] = _ffn_bwd(
        dh1, h0, f1, g1, u1, w["ffn1_pre_norm"], w31, w["ffn1_post_norm"], split_first_block=True)
    sgin, sgout = _pair_sum(gin, rgin, c_idx), _pair_sum(gout, rgout, c_idx)
    g["meta_tokens"] = dh_first[PAD:BLK]
    late_small = ["gla_w_a2", "swa_sinks"]
    direct = [n for n in SMALL if n not in late_small]
    names = direct + late_small
    half_f2 = _chip_sum(sgf2, ogf2, q_idx)
    hooks = _Hooks([_ChipScatter([sgin, sgout]), _GatherDevices([g[n] for n in names] + [sse]),
                    _PairShare([half_f2])])
    own1, others1, houts = _ffn_wgrad_reduce(n1, df1, dg1, du1, a1, qc_idx, hooks)
    (ogin, ogout), gathered, (other_f2,) = hooks.split(houts)
    halves = [_chip_sum(own1[None], others1, jnp.zeros((1,), jnp.int32))]
    halves += [_chip_sum(s, o, q_idx) for s, o in ((sgin, ogin), (sgout, ogout))]
    others = list(_comm_call(_PairShare(halves), "pair_share")) + [other_f2]
    halves.append(half_f2)
    reduced = {"ffn1_w_gate": (0, 0), "ffn1_w_up": (0, FJ), "ffn1_w_down": (0, 2 * FJ), "w_in": (1, 0),
               "w_out": (2, 0), "ffn2_w_gate": (3, 0), "ffn2_w_up": (3, FJ), "ffn2_w_down": (3, 2 * FJ)}
    grad, delta, new_m, new_v = {}, {}, {}, {}
    for n in BIG:
        k, row0 = reduced[n]
        outs = _adamw_halves(_shard2d(n, w[n]), halves[k], others[k], _shard2d(n, mom[n]), _shard2d(n, var[n]),
                             c_idx, row0)
        grad[n], delta[n], new_m[n], new_v[n] = [_unshard2d(n, a) for a in outs]

    late = late_small
    mat = lambda a: a.reshape(a.shape[-2:])
    none3 = [None] * (len(late) + 1)
    outs = _small_update(q_idx, gathered, [mat(w[n]) for n in direct] + none3, [mat(mom[n]) for n in direct] + none3,
                         [mat(var[n]) for n in direct] + none3, [n == "meta_tokens" for n in names] + [False])
    sum_a2, sum_sinks, sum_sse = outs[4 * len(direct):]
    loss = sum_sse[0, 0] * (0.5 / D_MODEL)
    g_late = [lax.dynamic_slice_in_dim(sum_a2, q_chip * (GLA_KW // N_CHIPS), GLA_KW // N_CHIPS, axis=1)[None],
              sum_sinks[:, 0].reshape(1, 1, SWA_QH)]
    outs = list(outs[:4 * len(direct)]) + list(_small_update(
        q_idx, g_late, [mat(w[n]) for n in late], [mat(mom[n]) for n in late], [mat(var[n]) for n in late],
        [False, False]))
    for k, n in enumerate(names):
        grad[n], delta[n], new_m[n], new_v[n] = [a.reshape(w[n].shape) for a in outs[4 * k:4 * k + 4]]

    return (loss, grad_x[None], *[grad[n] for n in WEIGHTS], *[delta[n] for n in WEIGHTS],
            *[new_m[n] for n in WEIGHTS], *[new_v[n] for n in WEIGHTS])
```

```python
import functools
import math

import numpy as np
import jax
import jax.numpy as jnp
from jax import lax
from jax.experimental import pallas as pl
from jax.experimental.pallas import tpu as pltpu

F32 = jnp.float32
BF16 = jnp.bfloat16
MESH = pl.DeviceIdType.MESH

D_MODEL = 1024
D_FF = 2816
N_CHIPS = 4
N_DEV = 8
N_META = 16
BLK = 128
PAD = BLK - N_META
GLA_CHUNK = 64
GLA_HEADS = 4
GLA_DV = 128
GLA_DK = 64
GLA_KW = GLA_HEADS * GLA_DK
GLA_W = GLA_HEADS * GLA_DV
GLA_RANK = 16
GLA_TAU = 16.0
SWA_HD = 64
SWA_QH = 8
SWA_KVH = 2
SWA_W = SWA_QH * SWA_HD
WINDOW = 128
ROPE_THETA = 10000.0
EPS = 1e-6
NEG_INF = -1e30
IN_SPLITS = (256, 256, 512, 512, 16, 512, 128, 128)
D_IN = sum(IN_SPLITS)
P_GQ, P_GK, P_GV, P_GG, P_GA, P_SQ, P_SK, P_SV, P_END = 0, 256, 512, 1024, 1536, 1664, 2176, 2432, 2688
ADAM_LR, ADAM_B1, ADAM_B2, ADAM_EPS, ADAM_WD, ADAM_STEP = 0.001, 0.9, 0.999, 1e-08, 0.01, 10
VMEM_LIMIT = 56 * 1024 * 1024

NT = (((1,), (1,)), ((), ()))
TN = (((0,), (0,)), ((), ()))


def _cparams(n_axes):
    return pltpu.CompilerParams(dimension_semantics=("arbitrary",) * n_axes, vmem_limit_bytes=VMEM_LIMIT)


def _row_tile(t):
    for tm in (640, 512, 384, 256, 128):
        if t % tm == 0:
            return tm
    raise ValueError(t)


SEQ_BLOCKS_PER_STEP = 5


def _seq_tile(t):
    return SEQ_BLOCKS_PER_STEP * BLK if t % (SEQ_BLOCKS_PER_STEP * BLK) == 0 else BLK


ROW_PARTS = 2


def _row_parts(tm):
    n = ROW_PARTS if tm % (16 * ROW_PARTS) == 0 else 1
    return [slice(k * (tm // n), (k + 1) * (tm // n)) for k in range(n)]


def _contract_tile(t):
    return 1664 if t % 1664 == 0 else _row_tile(t)


def _div_tile(r, cap=512):
    best = None
    for tr in range(8, min(r, cap) + 1, 8):
        if r % tr == 0:
            best = tr
    return best if best is not None else r


def _dot(a, b):
    return jnp.dot(a, b, preferred_element_type=F32)


def _dg(a, b, dims):
    return lax.dot_general(a, b, dims, preferred_element_type=F32)


def _rms(x, w):
    r = lax.rsqrt(jnp.mean(x * x, axis=-1, keepdims=True) + EPS)
    xh = x * r
    return xh * w, xh, r


def _rms_bwd(xh, r, w, dy):
    wdy = dy * w
    dx = r * (wdy - xh * jnp.mean(wdy * xh, axis=-1, keepdims=True))
    dw = jnp.sum(dy * xh, axis=0, keepdims=True)
    return dx, dw


def _sigmoid(x):
    return 1.0 / (1.0 + jnp.exp(-x))


def _full(shape):
    nd = len(shape)
    return pl.BlockSpec(shape, lambda *_: (0,) * nd)


ANY = pl.BlockSpec(memory_space=pl.ANY)


def _pallas(body, *, name, grid, in_specs, out_specs, out_shape, args, scratch_shapes=(), hook=None):
    n_axes = len(grid)
    if hook is None:
        return pl.pallas_call(body, name=name, grid=grid, in_specs=list(in_specs), out_specs=list(out_specs),
                              out_shape=list(out_shape), scratch_shapes=list(scratch_shapes),
                              compiler_params=_cparams(n_axes))(*args)
    n_in, n_out, n_scr = len(in_specs), len(out_specs), len(scratch_shapes)
    h_in, h_out = len(hook.inputs), len(hook.out_shape)
    total = math.prod(grid)

    def wrapped(*refs):
        ins, hins = refs[:n_in], refs[n_in:n_in + h_in]
        o0 = n_in + h_in
        outs, houts = refs[o0:o0 + n_out], refs[o0 + n_out:o0 + n_out + h_out]
        s0 = o0 + n_out + h_out
        scr, hscr = refs[s0:s0 + n_scr], refs[s0 + n_scr:]
        step = pl.program_id(0)
        for a in range(1, n_axes):
            step = step * grid[a] + pl.program_id(a)

        @pl.when(step == 0)
        def _():
            hook.start(hins, houts, hscr)

        body(*ins, *outs, *scr)

        if hook.has_mid:
            @pl.when(step == (3 * total) // 4)
            def _():
                hook.mid(hins, houts, hscr)

        @pl.when(step == total - 1)
        def _():
            hook.finish(hins, houts, hscr)

    res = pl.pallas_call(
        wrapped, name=name, grid=grid, in_specs=list(in_specs) + [ANY] * h_in,
        out_specs=list(out_specs) + [ANY] * h_out, out_shape=list(out_shape) + list(hook.out_shape),
        scratch_shapes=list(scratch_shapes) + list(hook.scratch), compiler_params=_cparams(n_axes),
        input_output_aliases={n_in + a: n_out + b for a, b in hook.aliases},
    )(*args, *hook.inputs)
    return res[:n_out], res[n_out:]


def _ffn_weight_specs(w3):
    fj = w3.shape[1] // 3
    return fj, [pl.BlockSpec((None, fj, D_MODEL), functools.partial(lambda i, j, k: (j, k, 0), k=k)) for k in range(3)]


def _ffn_fwd(h, wpre, w3, wpost, hook=None, target=None):
    t = h.shape[0]
    tm = _row_tile(t)
    nj, rows3, _ = w3.shape
    fj = rows3 // 3
    nblk = tm // BLK if target is not None else 0

    def body(*refs):
        h_ref, wpre_ref, w_hbm, wpost_ref = refs[:4]
        t_refs = refs[4:4 + nblk]
        hout_ref, n_ref, p1_ref, p2_ref, a_ref, f_ref = refs[4 + nblk:10 + nblk]
        acc_ref, wv, wsem = refs[-3:]
        i = pl.program_id(0)
        j = pl.program_id(1)

        @pl.when((i == 0) & (j == 0))
        def _():
            for k in range(nj):
                pltpu.make_async_copy(w_hbm.at[k], wv.at[k], wsem.at[k]).start()

        @pl.when(i == 0)
        def _():
            pltpu.make_async_copy(w_hbm.at[j], wv.at[j], wsem.at[j]).wait()

        @pl.when(j == 0)
        def _():
            y, _, _ = _rms(h_ref[...], wpre_ref[...])
            n_ref[...] = y.astype(BF16)
            acc_ref[...] = jnp.zeros_like(acc_ref)

        if target is not None:
            dwpost_ref, sse_ref = refs[10 + nblk:12 + nblk]

            @pl.when((i == 0) & (j == 0))
            def _():
                dwpost_ref[...] = jnp.zeros_like(dwpost_ref)
                sse_ref[...] = jnp.zeros_like(sse_ref)

        n = n_ref[...]
        g = _dg(n, wv[j, 0:fj], NT)
        u = _dg(n, wv[j, fj:2 * fj], NT)
        sg = _sigmoid(g)
        silu = g * sg
        p1_ref[...] = (u * (sg + silu * (1.0 - sg))).astype(BF16)
        p2_ref[...] = silu.astype(BF16)
        a = (silu * u).astype(BF16)
        a_ref[...] = a
        acc_ref[...] += _dot(a, wv[j, 2 * fj:3 * fj])

        @pl.when(j == nj - 1)
        def _():
            f = acc_ref[...]
            wpost = wpost_ref[...]
            y, fh, r = _rms(f, wpost)
            hout = h_ref[...] + 0.5 * y
            if target is None:
                f_ref[...] = f
                hout_ref[...] = hout
            else:
                sse = jnp.zeros((1, 1), F32)
                errs = []
                for k in range(nblk):
                    err = hout[k * BLK:(k + 1) * BLK] - t_refs[k][...]
                    if k == 0:
                        err = jnp.where(i > 0, err, 0.0)
                    errs.append(err)
                    sse = sse + jnp.sum(jnp.sum(err * err, axis=1, keepdims=True), axis=0, keepdims=True)
                dy = (jnp.concatenate(errs, axis=0) if nblk > 1 else errs[0]) * (1.0 / D_MODEL)
                hout_ref[...] = dy
                df, dw = _rms_bwd(fh, r, wpost, 0.5 * dy)
                f_ref[...] = df.astype(BF16)
                dwpost_ref[...] += dw
                sse_ref[...] += jnp.broadcast_to(sse, sse_ref.shape)

    row = pl.BlockSpec((tm, D_MODEL), lambda i, j: (i, 0))
    vec = pl.BlockSpec((1, D_MODEL), lambda i, j: (0, 0))
    act = pl.BlockSpec((None, tm, fj), lambda i, j: (j, i, 0))
    t_specs = [pl.BlockSpec((BLK, D_MODEL), functools.partial(lambda i, j, k: (jnp.maximum(nblk * i + k - 1, 0), 0), k=k))
               for k in range(nblk)]
    loss_spec = [vec, _full((1, 128))] if target is not None else []
    loss_shape = [jax.ShapeDtypeStruct((1, D_MODEL), F32), jax.ShapeDtypeStruct((1, 128), F32)] if (
        target is not None) else []
    return _pallas(
        body, name="ffn_fwd", grid=(t // tm, nj),
        in_specs=[row, vec, ANY, vec] + t_specs,
        out_specs=[row, row, act, act, act, row] + loss_spec,
        out_shape=[jax.ShapeDtypeStruct((t, D_MODEL), F32), jax.ShapeDtypeStruct((t, D_MODEL), BF16),
                   jax.ShapeDtypeStruct((nj, t, fj), BF16), jax.ShapeDtypeStruct((nj, t, fj), BF16),
                   jax.ShapeDtypeStruct((nj, t, fj), BF16),
                   jax.ShapeDtypeStruct((t, D_MODEL), F32 if target is None else BF16)] + loss_shape,
        scratch_shapes=[pltpu.VMEM((tm, D_MODEL), F32), pltpu.VMEM((nj, rows3, D_MODEL), BF16),
                        pltpu.SemaphoreType.DMA((nj,))],
        args=(h, wpre, w3, wpost) + (target,) * nblk, hook=hook)


def _ffn_bwd(dhout, h, f, p14, p24, wpre, w3, wpost, df=None, split_first_block=False):
    t = h.shape[0]
    tm = _row_tile(t)
    ni = t // tm
    nj = w3.shape[0]
    fj, wspecs = _ffn_weight_specs(w3)
    have_df = df is not None
    assert not (have_df and split_first_block)

    def body(dhout_ref, h_ref, f_ref, p1_ref, p2_ref, wpre_ref, wg_ref, wu_ref, wd_ref, wpost_ref, *rest):
        if have_df:
            dh_ref, dg_ref, du_ref, dwpre_ref, dn_ref = rest
            df_ref = f_ref
        elif split_first_block:
            dh_ref, rest_hbm, df_ref, dg_ref, du_ref, dwpre_ref, dwpost_ref, dn_ref, dh_buf, dh_sem, first_sem = rest
        else:
            dh_ref, df_ref, dg_ref, du_ref, dwpre_ref, dwpost_ref, dn_ref = rest
        i = pl.program_id(0)
        j = pl.program_id(1)

        @pl.when((i == 0) & (j == 0))
        def _():
            dwpre_ref[...] = jnp.zeros_like(dwpre_ref)
            if not have_df:
                dwpost_ref[...] = jnp.zeros_like(dwpost_ref)

        @pl.when(j == 0)
        def _():
            if not have_df:
                wpost = wpost_ref[...]
                _, fh, r = _rms(f_ref[...], wpost)
                dfv, dw = _rms_bwd(fh, r, wpost, 0.5 * dhout_ref[...])
                dwpost_ref[...] += dw
                df_ref[...] = dfv.astype(BF16)
            dn_ref[...] = jnp.zeros_like(dn_ref)

        parts = _row_parts(tm)
        das = [_dg(df_ref[rows, :], wd_ref[...], NT) for rows in parts]
        for rows, da in zip(parts, das):
            dg = (da * p1_ref[rows, :].astype(F32)).astype(BF16)
            du = (da * p2_ref[rows, :].astype(F32)).astype(BF16)
            dg_ref[rows, :] = dg
            du_ref[rows, :] = du
            dn_ref[rows, :] += _dot(dg, wg_ref[...]) + _dot(du, wu_ref[...])

        @pl.when(j == nj - 1)
        def _():
            wpre = wpre_ref[...]
            _, hh, r = _rms(h_ref[...], wpre)
            dx, dw = _rms_bwd(hh, r, wpre, dn_ref[...])
            dwpre_ref[...] += dw
            dh = dhout_ref[...] + dx
            if not split_first_block:
                dh_ref[...] = dh
            else:
                slot = i % 2

                def to_rest(tile, sl):
                    rows = pl.ds(pl.multiple_of(tile * tm - BLK, 8), tm)
                    return pltpu.make_async_copy(dh_buf.at[sl], rest_hbm.at[rows], dh_sem.at[sl])

                first = pltpu.make_async_copy(dh_buf.at[0, BLK:tm], rest_hbm.at[0:tm - BLK], first_sem)

                @pl.when(i == 2)
                def _():
                    first.wait()

                @pl.when(i >= 3)
                def _():
                    to_rest(i, slot).wait()

                dh_buf[slot] = dh

                @pl.when(i == 0)
                def _():
                    dh_ref[...] = dh[0:BLK]
                    first.start()

                @pl.when(i > 0)
                def _():
                    to_rest(i, slot).start()

                @pl.when(i == ni - 1)
                def _():
                    if ni < 3:
                        first.wait()
                    if ni >= 3:
                        to_rest(i, 1 - slot).wait()
                    if ni >= 2:
                        to_rest(i, slot).wait()

    row = pl.BlockSpec((tm, D_MODEL), lambda i, j: (i, 0))
    vec = pl.BlockSpec((1, D_MODEL), lambda i, j: (0, 0))
    act = pl.BlockSpec((None, tm, fj), lambda i, j: (j, i, 0))
    actshape = jax.ShapeDtypeStruct((nj, t, fj), BF16)
    rowf, rowb, vecf = (jax.ShapeDtypeStruct((t, D_MODEL), F32), jax.ShapeDtypeStruct((t, D_MODEL), BF16),
                        jax.ShapeDtypeStruct((1, D_MODEL), F32))
    if have_df:
        out_specs, out_shape = [row, act, act, vec], [rowf, actshape, actshape, vecf]
    else:
        out_specs, out_shape = [row, row, act, act, vec, vec], [rowf, rowb, actshape, actshape, vecf, vecf]
    scratch = [pltpu.VMEM((tm, D_MODEL), F32)]
    if split_first_block:
        out_specs = [pl.BlockSpec((BLK, D_MODEL), lambda i, j: (0, 0)), ANY] + out_specs[1:]
        out_shape = [jax.ShapeDtypeStruct((BLK, D_MODEL), F32), jax.ShapeDtypeStruct((t - BLK, D_MODEL), F32)
                     ] + out_shape[1:]
        scratch += [pltpu.VMEM((2, tm, D_MODEL), F32), pltpu.SemaphoreType.DMA((2,)), pltpu.SemaphoreType.DMA]
    return _pallas(
        body, name="ffn_bwd", grid=(ni, nj),
        in_specs=[row, row, row, act, act, vec] + wspecs + [vec],
        out_specs=out_specs, out_shape=out_shape, scratch_shapes=scratch,
        args=(dhout, h, df if have_df else f, p14, p24, wpre, w3, w3, w3, wpost))


def _ffn_wgrad(n, df, dg4, du4, a4, hook=None):
    t = n.shape[0]
    tm = _contract_tile(t)
    ni = t // tm
    nj, _, fj = dg4.shape

    def body(n_ref, df_ref, dg_ref, du_ref, a_ref, dw_ref, acc):
        i = pl.program_id(1)

        @pl.when(i == 0)
        def _():
            acc[...] = jnp.zeros_like(acc)

        nn = n_ref[...]
        acc[0:fj, :] += _dg(dg_ref[...], nn, TN)
        acc[fj:2 * fj, :] += _dg(du_ref[...], nn, TN)
        acc[2 * fj:3 * fj, :] += _dg(a_ref[...], df_ref[...], TN)

        @pl.when(i == ni - 1)
        def _():
            dw_ref[...] = acc[...].astype(BF16)

    row = pl.BlockSpec((tm, D_MODEL), lambda j, i: (i, 0))
    act = pl.BlockSpec((None, tm, fj), lambda j, i: (j, i, 0))
    return _pallas(
        body, name="ffn_wgrad", grid=(nj, ni),
        in_specs=[row, row, act, act, act],
        out_specs=[pl.BlockSpec((None, 3 * fj, D_MODEL), lambda j, i: (j, 0, 0))],
        out_shape=[jax.ShapeDtypeStruct((nj, 3 * fj, D_MODEL), BF16)],
        scratch_shapes=[pltpu.VMEM((3 * fj, D_MODEL), F32)],
        args=(n, df, dg4, du4, a4), hook=hook)


def _embed_norm(x, meta_buf, w):
    t = x.shape[0] + BLK
    tm = _row_tile(t)
    nblk = tm // BLK
    ni = t // tm
    gather = _GatherChips([meta_buf])

    def body(*refs):
        x_refs = refs[:nblk]
        w_ref, mb_in, h_ref, n_ref, mb_out, mv, msem, send, recv = refs[nblk:]
        step = pl.program_id(0)
        tile = (step + 1) % ni
        hook_refs = ([mb_in], [mb_out], [send, recv])
        steps = (0, 1, ni - 2) if ni >= 3 else (0, 0, 0)
        for at, phase in zip(steps, (gather.start, gather.mid, gather.finish)):
            @pl.when(step == at)
            def _(phase=phase):
                phase(*hook_refs)

        @pl.when(step == 0)
        def _():
            mv[...] = jnp.zeros_like(mv)

        @pl.when(step == ni - 1)
        def _():
            cp = pltpu.make_async_copy(mb_out, mv, msem)
            cp.start()
            cp.wait()

        meta = jnp.concatenate([mv[k] for k in range(N_CHIPS)], axis=1)
        first = jnp.concatenate([jnp.zeros((PAD, D_MODEL), F32), meta], axis=0)
        blocks = [jnp.where(tile == 0, first, x_refs[0][...])] + [r[...] for r in x_refs[1:]]
        h = jnp.concatenate(blocks, axis=0) if nblk > 1 else blocks[0]
        h_ref[...] = h
        y, _, _ = _rms(h, w_ref[...])
        n_ref[...] = y.astype(BF16)

    x_specs = [pl.BlockSpec((BLK, D_MODEL), functools.partial(
        lambda i, k: (jnp.maximum(nblk * ((i + 1) % ni) + k - 1, 0), 0), k=k)) for k in range(nblk)]
    row = pl.BlockSpec((tm, D_MODEL), lambda i: ((i + 1) % ni, 0))
    h0, n0, _ = pl.pallas_call(
        body, name="embed_norm", grid=(ni,),
        in_specs=x_specs + [_full((1, D_MODEL)), ANY], out_specs=[row, row, ANY],
        out_shape=[jax.ShapeDtypeStruct((t, D_MODEL), F32), jax.ShapeDtypeStruct((t, D_MODEL), BF16),
                   jax.ShapeDtypeStruct(meta_buf.shape, meta_buf.dtype)],
        scratch_shapes=[pltpu.VMEM(meta_buf.shape, meta_buf.dtype), pltpu.SemaphoreType.DMA] + list(gather.scratch),
        input_output_aliases={nblk + 1: 2},
        compiler_params=_cparams(1),
    )(*([x] * nblk), w, meta_buf)
    return h0, n0


FWD_RELATION = (None, 0, 1, 2)


def _ffn_fwd_gather(h, n, wbuf, wpost, qc_idx, late):
    t = h.shape[0]
    tm = _row_tile(t)
    ni = t // tm
    nj, rows3, _ = wbuf.shape
    fj = rows3 // 3
    assert nj == N_CHIPS and ni >= 4
    wbufs = [wbuf]
    nw = 1
    n_lin, n_lout = len(late.inputs), len(late.out_shape)
    wait_step = ni - 3

    def body(qc_ref, h_ref, n_ref, wpost_ref, *rest):
        wb_in = rest[:nw]
        lins = rest[nw:nw + n_lin]
        o0 = nw + n_lin
        hout_ref, p1_ref, p2_ref, a_ref, f_hbm = rest[o0:o0 + 5]
        wb = rest[o0 + 5:o0 + 5 + nw]
        louts = rest[o0 + 5 + nw:o0 + 5 + nw + n_lout]
        s0 = o0 + 5 + nw + n_lout
        wv, wsem, send, recv, fbuf, fr_sem, fw_sem = rest[s0:s0 + 7]
        lscr = rest[s0 + 7:]
        p = pl.program_id(0)
        i = pl.program_id(1)
        step = p * ni + i
        fslot = step % 3
        nslot = (step + 1) % 3

        def f_tile(tile):
            return f_hbm.at[pl.ds(pl.multiple_of(tile * tm, 8), tm)]

        @pl.when(step > 1)
        def _():
            pltpu.make_async_copy(fbuf.at[nslot], f_tile(i), fw_sem.at[nslot]).wait()

        nxt = step + 1

        @pl.when((nxt < N_CHIPS * ni) & (nxt >= ni))
        def _():
            pltpu.make_async_copy(f_tile(nxt % ni), fbuf.at[nslot], fr_sem.at[nslot]).start()

        @pl.when(p > 0)
        def _():
            pltpu.make_async_copy(f_tile(i), fbuf.at[fslot], fr_sem.at[fslot]).wait()
        x, y, c, chips = _place()
        q = 2 * x + y
        sibling = (x, y, 1 - c)
        mine, other = _half(rows3, c), _half(rows3, 1 - c)

        def load(chunk, slot, src):
            return [pltpu.make_async_copy(src[t].at[chunk], wv.at[slot, t], wsem.at[slot, t]) for t in range(nw)]

        @pl.when((p == 0) & (i == 0))
        def _():
            for j, (cx, cy) in enumerate(chips):
                for t in range(nw):
                    _remote(send.at[t, j], recv.at[t, j], wb_in[t].at[q, mine], wb[t].at[q, mine], (cx, cy, c)).start()
            for cp in load(q, 0, wb_in):
                cp.start()
            for cp in load(q, 0, wb_in):
                cp.wait()

        @pl.when((p == 1) & (i == 0))
        def _():
            late.start(lins, louts, lscr)

        for pp in range(1, N_CHIPS):
            j = FWD_RELATION[pp]
            cx, cy = chips[j]
            chunk = 2 * cx + cy

            @pl.when((p == pp - 1) & (i == wait_step))
            def _(j=j, cx=cx, cy=cy, chunk=chunk, pp=pp):
                for t in range(nw):
                    got = wb[t].at[chunk, mine]
                    _remote(send.at[t, j], recv.at[t, j], got, got, (cx, cy, c)).wait_recv()
                    _remote(send.at[t, 3 + j], recv.at[t, 3 + j], got, got, sibling).start()
                for t in range(nw):
                    rest_half = wb[t].at[chunk, other]
                    _remote(send.at[t, 3 + j], recv.at[t, 3 + j], rest_half, rest_half, sibling).wait_recv()
                for cp in load(chunk, pp % 2, wb):
                    cp.start()

            @pl.when((p == pp) & (i == 0))
            def _(chunk=chunk, pp=pp):
                for cp in load(chunk, pp % 2, wb):
                    cp.wait()

        @pl.when((p == N_CHIPS - 1) & (i == ni // 2))
        def _():
            late.mid(lins, louts, lscr)

        slot = p % 2
        nn = n_ref[...]
        g = _dg(nn, wv[slot, 0, 0:fj], NT)
        u = _dg(nn, wv[slot, 0, fj:2 * fj], NT)
        sg = _sigmoid(g)
        silu = g * sg
        p1_ref[...] = (u * (sg + silu * (1.0 - sg))).astype(BF16)
        p2_ref[...] = silu.astype(BF16)
        a = (silu * u).astype(BF16)
        a_ref[...] = a
        part = _dot(a, wv[slot, 0, 2 * fj:3 * fj])

        @pl.when(p == 0)
        def _():
            fbuf[fslot] = part

        @pl.when(p > 0)
        def _():
            fbuf[fslot] = fbuf[fslot] + part

        pltpu.make_async_copy(fbuf.at[fslot], f_tile(i), fw_sem.at[fslot]).start()

        @pl.when(p == N_CHIPS - 1)
        def _():
            yv, _, _ = _rms(fbuf[fslot], wpost_ref[...])
            hout_ref[...] = h_ref[...] + 0.5 * yv

        @pl.when((p == N_CHIPS - 1) & (i == ni - 1))
        def _():
            pslot = (step + 2) % 3
            pltpu.make_async_copy(fbuf.at[pslot], f_tile(i), fw_sem.at[pslot]).wait()
            pltpu.make_async_copy(fbuf.at[fslot], f_tile(i), fw_sem.at[fslot]).wait()
            for t in range(nw):
                for j, (cx, cy) in enumerate(chips):
                    sent = wb[t].at[2 * cx + cy, mine]
                    _remote(send.at[t, j], recv.at[t, j], sent, sent, (cx, cy, c)).wait_send()
                    _remote(send.at[t, 3 + j], recv.at[t, 3 + j], sent, sent, sibling).wait_send()
            late.finish(lins, louts, lscr)

    def last_pass_rows(p, i, qc_ref):
        return (jnp.where(p == N_CHIPS - 1, i, 0), 0)

    def chunk_rows(p, i, qc_ref):
        order = ((p & 1) << 1) | (p >> 1)
        return (jnp.bitwise_xor(qc_ref[0], order), i, 0)

    row = pl.BlockSpec((tm, D_MODEL), lambda p, i, qc_ref: (i, 0))
    last_row = pl.BlockSpec((tm, D_MODEL), last_pass_rows)
    act = pl.BlockSpec((None, tm, fj), chunk_rows)
    act_shape = jax.ShapeDtypeStruct((nj, t, fj), BF16)
    res = pl.pallas_call(
        body, name="ffn_fwd_gather",
        grid_spec=pltpu.PrefetchScalarGridSpec(
            num_scalar_prefetch=1, grid=(N_CHIPS, ni),
            in_specs=[last_row, row, pl.BlockSpec((1, D_MODEL), lambda p, i, qc_ref: (0, 0))]
            + [ANY] * (nw + n_lin),
            out_specs=[last_row, act, act, act, ANY] + [ANY] * (nw + n_lout),
            scratch_shapes=[pltpu.VMEM((2, nw, rows3, D_MODEL), BF16), pltpu.SemaphoreType.DMA((2, nw)),
                            pltpu.SemaphoreType.DMA((nw, 6)), pltpu.SemaphoreType.DMA((nw, 6)),
                            pltpu.VMEM((3, tm, D_MODEL), F32), pltpu.SemaphoreType.DMA((3,)),
                            pltpu.SemaphoreType.DMA((3,))] + list(late.scratch)),
        out_shape=[jax.ShapeDtypeStruct((t, D_MODEL), F32), act_shape, act_shape, act_shape,
                   jax.ShapeDtypeStruct((t, D_MODEL), F32)]
        + [jax.ShapeDtypeStruct(b.shape, b.dtype) for b in wbufs] + list(late.out_shape),
        input_output_aliases={**{4 + t: 5 + t for t in range(nw)},
                              **{4 + nw + a: 5 + nw + b for a, b in late.aliases}},
        compiler_params=_cparams(2),
    )(qc_idx, h, n, wpost, *wbufs, *late.inputs)
    return res[:5], res[5:5 + nw], res[5 + nw:]


PASS_RELATION = (2, 0, 1)


def _ffn_wgrad_reduce(n, df, dg4, du4, a4, qc_idx, hook):
    t = n.shape[0]
    tm = _contract_tile(t)
    ni = t // tm
    nj, _, fj = dg4.shape
    assert nj == N_CHIPS
    hrows = 3 * fj // 2
    n_hin, n_hout = len(hook.inputs), len(hook.out_shape)

    def body(qc_ref, n_ref, df_ref, dg_ref, du_ref, a_ref, *rest):
        hins = rest[:n_hin]
        own_ref, others_ref = rest[n_hin:n_hin + 2]
        houts = rest[n_hin + 2:n_hin + 2 + n_hout]
        s0 = n_hin + 2 + n_hout
        acc, stage, land, sumbuf, px_send, px_recv, cs_send, cs_recv, own_sem = rest[s0:s0 + 9]
        hscr = rest[s0 + 9:]
        k_pass = pl.program_id(0)
        i = pl.program_id(1)
        x, y, c, chips = _place()
        mine = pl.ds(pl.multiple_of(c * hrows, 8), hrows)
        other = pl.ds(pl.multiple_of((1 - c) * hrows, 8), hrows)

        def to_owner(k):
            j = PASS_RELATION[k]
            return _remote(cs_send.at[j], cs_recv.at[j], sumbuf.at[k % 2], others_ref.at[j], (*chips[j], c))

        @pl.when((k_pass == 0) & (i == 0))
        def _():
            hook.start(hins, houts, hscr)

        if hook.has_mid:
            @pl.when((k_pass == N_CHIPS // 2) & (i == 0))
            def _():
                hook.mid(hins, houts, hscr)

        @pl.when(i == 0)
        def _():
            acc[...] = jnp.zeros_like(acc)

        nn = n_ref[...]
        acc[0:fj, :] += _dg(dg_ref[...], nn, TN)
        acc[fj:2 * fj, :] += _dg(du_ref[...], nn, TN)
        acc[2 * fj:3 * fj, :] += _dg(a_ref[...], df_ref[...], TN)

        for k in range(N_CHIPS):
            @pl.when((k_pass == k) & (i == ni - 1))
            def _(k=k):
                slot = k % 2
                stage[...] = acc[other, :].astype(BF16)
                swap = _remote(px_send.at[k], px_recv.at[k], stage, land.at[slot], (x, y, 1 - c))
                swap.start()
                swap.wait_recv()
                pair = acc[mine, :] + land[slot].astype(F32)
                if k >= 2:
                    to_owner(k - 2).wait_send()
                sumbuf[slot] = pair.astype(BF16)
                swap.wait_send()
                if k < N_CHIPS - 1:
                    to_owner(k).start()
                else:
                    keep = pltpu.make_async_copy(sumbuf.at[slot], own_ref, own_sem)
                    keep.start()
                    for j in range(N_CHIPS - 1):
                        _remote(cs_send.at[j], cs_recv.at[j], sumbuf.at[0], others_ref.at[j], (*chips[j], c)).wait_recv()
                    to_owner(k - 1).wait_send()
                    keep.wait()
                    hook.finish(hins, houts, hscr)

    def chunk(k_pass, i, qc_ref):
        return (jnp.bitwise_xor(qc_ref[0], N_CHIPS - 1 - k_pass), i, 0)

    row = pl.BlockSpec((tm, D_MODEL), lambda k_pass, i, qc_ref: (i, 0))
    act = pl.BlockSpec((None, tm, fj), chunk)
    res = pl.pallas_call(
        body, name="ffn_wgrad_reduce",
        grid_spec=pltpu.PrefetchScalarGridSpec(
            num_scalar_prefetch=1, grid=(N_CHIPS, ni),
            in_specs=[row, row, act, act, act] + [ANY] * n_hin,
            out_specs=[ANY, ANY] + [ANY] * n_hout,
            scratch_shapes=[pltpu.VMEM((3 * fj, D_MODEL), F32), pltpu.VMEM((hrows, D_MODEL), BF16),
                            pltpu.VMEM((2, hrows, D_MODEL), BF16), pltpu.VMEM((2, hrows, D_MODEL), BF16),
                            pltpu.SemaphoreType.DMA((N_CHIPS,)), pltpu.SemaphoreType.DMA((N_CHIPS,)),
                            pltpu.SemaphoreType.DMA((N_CHIPS - 1,)), pltpu.SemaphoreType.DMA((N_CHIPS - 1,)),
                            pltpu.SemaphoreType.DMA] + list(hook.scratch)),
        out_shape=[jax.ShapeDtypeStruct((hrows, D_MODEL), BF16),
                   jax.ShapeDtypeStruct((N_CHIPS - 1, hrows, D_MODEL), BF16)] + list(hook.out_shape),
        compiler_params=_cparams(2),
    )(qc_idx, n, df, dg4, du4, a4, *hook.inputs)
    return res[0], res[1], res[2:]


def _xty(x, y):
    t, k = x.shape
    n = y.shape[1]
    tm = _contract_tile(t)
    tn = n if n <= 1024 else (896 if n % 896 == 0 else 128)
    steps = t // tm

    def body(x_ref, y_ref, o_ref, acc_ref):
        i = pl.program_id(1)
        part = _dg(x_ref[...], y_ref[...], TN)

        @pl.when(i == 0)
        def _():
            acc_ref[...] = part

        @pl.when(jnp.logical_and(i > 0, i < steps - 1))
        def _():
            acc_ref[...] += part

        @pl.when(i == steps - 1)
        def _():
            o_ref[...] = (acc_ref[...] + part).astype(BF16)

    assert steps > 1
    return pl.pallas_call(
        body, name="xty", grid=(n // tn, steps),
        in_specs=[pl.BlockSpec((tm, k), lambda j, i: (i, 0)), pl.BlockSpec((tm, tn), lambda j, i: (i, j))],
        out_specs=pl.BlockSpec((k, tn), lambda j, i: (0, j)),
        out_shape=jax.ShapeDtypeStruct((k, n), BF16),
        scratch_shapes=[pltpu.VMEM((k, tn), F32)],
        compiler_params=_cparams(2),
    )(x, y)


def _rope_tables(t):
    pos = (jnp.arange(t, dtype=jnp.int32) - PAD).astype(F32)
    inv_freq = 1.0 / (ROPE_THETA ** (jnp.arange(0, SWA_HD, 2, dtype=F32) / SWA_HD))
    half = SWA_HD // 2
    ang = pos[:, None] * jnp.tile(inv_freq, 4)[None, :]
    sign = jnp.tile(jnp.concatenate([-jnp.ones((half,), F32), jnp.ones((half,), F32)]), 2)
    return jnp.cos(ang), jnp.sin(ang) * sign[None, :]


def _rot_half(x, first_half):
    return jnp.where(first_half, pltpu.roll(x, 96, 1), pltpu.roll(x, 32, 1))


def _first_half_mask(rows):
    lane = lax.broadcasted_iota(jnp.int32, (rows, 128), 1)
    return (lane % 64) < 32


def _log_sigmoid(z):
    return jnp.minimum(z, 0.0) - jnp.log(1.0 + jnp.exp(-jnp.abs(z)))


def _mix_proj(h1, wmixpre, winp, wa2p, bap, cos, sin):
    t = h1.shape[0]
    tm = _row_tile(t)

    def body(h_ref, w_ref, win_ref, wa2_ref, ba_ref, cos_ref, sin_ref,
             n_ref, gq_ref, gk_ref, gv_ref, gg_ref, ga_ref, la_ref, sq_ref, sk_ref, sv_ref):
        y, _, _ = _rms(h_ref[...], w_ref[...])
        n = y.astype(BF16)
        n_ref[...] = n
        proj = _dot(n, win_ref[...])
        gq_ref[...] = proj[:, P_GQ:P_GK]
        gk_ref[...] = proj[:, P_GK:P_GV]
        gv_ref[...] = proj[:, P_GV:P_GG]
        gg_ref[...] = proj[:, P_GG:P_GA]
        ga = proj[:, P_GA:P_SQ]
        ga_ref[...] = ga
        z = _dot(ga.astype(BF16), wa2_ref[...]) + ba_ref[...]
        la_ref[...] = _log_sigmoid(z) * (1.0 / GLA_TAU)
        c = cos_ref[...]
        s = sin_ref[...]
        fh = _first_half_mask(tm)
        for k in range(4):
            x = proj[:, P_SQ + 128 * k:P_SQ + 128 * (k + 1)]
            sq_ref[:, 128 * k:128 * (k + 1)] = (x * c + _rot_half(x, fh) * s).astype(BF16)
        for k in range(2):
            x = proj[:, P_SK + 128 * k:P_SK + 128 * (k + 1)]
            sk_ref[:, 128 * k:128 * (k + 1)] = (x * c + _rot_half(x, fh) * s).astype(BF16)
        sv_ref[...] = proj[:, P_SV:P_END].astype(BF16)

    def row(w):
        return pl.BlockSpec((tm, w), lambda i: (i, 0))

    def rshape(w, dt):
        return jax.ShapeDtypeStruct((t, w), dt)

    return pl.pallas_call(
        body, name="mix_proj", grid=(t // tm,),
        in_specs=[row(D_MODEL), _full((1, D_MODEL)), _full((D_MODEL, P_END)), _full((128, GLA_KW)),
                  _full((1, GLA_KW)), row(128), row(128)],
        out_specs=[row(D_MODEL), row(256), row(256), row(512), row(512), row(128), row(256), row(512), row(256),
                   row(256)],
        out_shape=[rshape(D_MODEL, BF16), rshape(256, F32), rshape(256, F32), rshape(512, F32), rshape(512, F32),
                   rshape(128, F32), rshape(256, F32), rshape(512, BF16), rshape(256, BF16), rshape(256, BF16)],
        compiler_params=_cparams(1),
    )(h1, wmixpre, winp, wa2p, bap, cos, sin)


def _scan_rows(x, reverse=False):
    n = x.shape[0]
    row = lax.broadcasted_iota(jnp.int32, x.shape, 0)
    s = 1
    while s < n:
        if reverse:
            x = x + jnp.where(row < n - s, pltpu.roll(x, n - s, 0), 0.0)
        else:
            x = x + jnp.where(row >= s, pltpu.roll(x, s, 0), 0.0)
        s *= 2
    return x


def _gla_cumsum(la, tril_f):
    b = _scan_rows(la)
    row = lax.broadcasted_iota(jnp.int32, b.shape, 0)
    bm = jnp.sum(jnp.where(row == GLA_CHUNK // 2 - 1, b, 0.0), axis=0, keepdims=True)
    bl = jnp.sum(jnp.where(row == GLA_CHUNK - 1, b, 0.0), axis=0, keepdims=True)
    return b, bm, bl


def _gla_decays(la, tril_f):
    b, bm, bl = _gla_cumsum(la, tril_f)
    return jnp.exp(b - bm), jnp.exp(bm - b), jnp.exp(b), jnp.exp(bl - b), jnp.exp(bl)


def _gla_masks():
    c = GLA_CHUNK
    r = lax.broadcasted_iota(jnp.int32, (c, c), 0)
    col = lax.broadcasted_iota(jnp.int32, (c, c), 1)
    r4 = lax.broadcasted_iota(jnp.int32, (GLA_HEADS * c, c), 0) % c
    c4 = lax.broadcasted_iota(jnp.int32, (GLA_HEADS * c, c), 1)
    klane = lax.broadcasted_iota(jnp.int32, (c, GLA_KW), 1) // GLA_DK
    vlane = lax.broadcasted_iota(jnp.int32, (c, GLA_W), 1) // GLA_DV
    srow = lax.broadcasted_iota(jnp.int32, (GLA_W, GLA_KW), 0) // GLA_DV
    scol = lax.broadcasted_iota(jnp.int32, (GLA_W, GLA_KW), 1) // GLA_DK
    return dict(tril_f=(r >= col).astype(F32), triu_f=(r <= col).astype(F32), tril4=r4 >= c4,
                khead=[klane == h for h in range(GLA_HEADS)], vhead=[vlane == h for h in range(GLA_HEADS)],
                diag=srow == scol)


def _stack_heads(x, head_masks):
    return jnp.concatenate([jnp.where(m, x, 0.0) for m in head_masks], axis=0)


def _gla_fwd(gq, gk, gv, la):
    t = gq.shape[0]
    rg = _seq_tile(t)
    nb = t // rg
    ncb = rg // GLA_CHUNK
    c = GLA_CHUNK

    def body(q_ref, k_ref, v_ref, la_ref, o_ref, ss_ref, st_ref):
        @pl.when(pl.program_id(0) == 0)
        def _():
            st_ref[...] = jnp.zeros_like(st_ref)

        mk = _gla_masks()
        st = st_ref[...]
        for ch in range(ncb):
            rows = slice(ch * c, (ch + 1) * c)
            eq, ek, eb, ekl, ebl = _gla_decays(la_ref[rows, :], mk["tril_f"])
            qs = q_ref[rows, :] * (GLA_DK ** -0.5)
            k = k_ref[rows, :]
            v = v_ref[rows, :].astype(BF16)
            ss_ref[ch] = st
            q4 = _stack_heads(qs * eq, mk["khead"]).astype(BF16)
            a4 = jnp.where(mk["tril4"], _dg(q4, (k * ek).astype(BF16), NT), 0.0).astype(BF16)
            r4 = _dot(a4, v)
            intra = jnp.concatenate([r4[h * c:(h + 1) * c, GLA_DV * h:GLA_DV * (h + 1)] for h in range(GLA_HEADS)],
                                    axis=1)
            o_ref[rows, :] = intra + _dg((qs * eb).astype(BF16), st.astype(BF16), NT)
            st = st * ebl + jnp.where(mk["diag"], _dg(v, (k * ekl).astype(BF16), TN), 0.0)
        st_ref[...] = st

    def row(w):
        return pl.BlockSpec((rg, w), lambda i: (i, 0))

    return pl.pallas_call(
        body, name="gla_fwd", grid=(nb,),
        in_specs=[row(256), row(256), row(512), row(256)],
        out_specs=[row(512), pl.BlockSpec((ncb, GLA_W, GLA_KW), lambda i: (i, 0, 0))],
        out_shape=[jax.ShapeDtypeStruct((t, GLA_W), F32), jax.ShapeDtypeStruct((nb * ncb, GLA_W, GLA_KW), F32)],
        scratch_shapes=[pltpu.VMEM((GLA_W, GLA_KW), F32)],
        compiler_params=_cparams(1),
    )(gq, gk, gv, la)


def _gla_bwd(gq, gk, gv, la, ss, do):
    t = gq.shape[0]
    rg = _seq_tile(t)
    nb = t // rg
    ncb = rg // GLA_CHUNK
    c = GLA_CHUNK

    def body(q_ref, k_ref, v_ref, la_ref, ss_ref, do_ref, dq_ref, dk_ref, dv_ref, dla_ref, dst_ref):
        @pl.when(pl.program_id(0) == 0)
        def _():
            dst_ref[...] = jnp.zeros_like(dst_ref)

        mk = _gla_masks()
        last_row = lax.broadcasted_iota(jnp.int32, (c, GLA_KW), 0) == c - 1
        scale = GLA_DK ** -0.5
        dstn = dst_ref[...]
        for ch in reversed(range(ncb)):
            rows = slice(ch * c, (ch + 1) * c)
            eq, ek, eb, ekl, ebl = _gla_decays(la_ref[rows, :], mk["tril_f"])
            qs = q_ref[rows, :] * scale
            k = k_ref[rows, :]
            qt, kt, qh, kh = qs * eq, k * ek, qs * eb, k * ekl
            ktb, khb, qhb = kt.astype(BF16), kh.astype(BF16), qh.astype(BF16)
            v = v_ref[rows, :].astype(BF16)
            do_f = do_ref[rows, :]
            dob = do_f.astype(BF16)
            st = ss_ref[ch]
            stb = st.astype(BF16)
            dstb = dstn.astype(BF16)
            q4 = _stack_heads(qt, mk["khead"]).astype(BF16)
            do4 = _stack_heads(do_f, mk["vhead"]).astype(BF16)
            a4 = jnp.where(mk["tril4"], _dg(q4, ktb, NT), 0.0).astype(BF16)
            da4 = jnp.where(mk["tril4"], _dg(do4, v, NT), 0.0).astype(BF16)
            dv_ref[rows, :] = _dg(a4, do4, TN) + _dg(khb, dstb, NT)
            dq4 = _dot(da4, ktb)
            dqt = jnp.zeros((c, GLA_KW), F32)
            for h in range(GLA_HEADS):
                dqt = dqt + jnp.where(mk["khead"][h], dq4[h * c:(h + 1) * c], 0.0)
            dkt = _dg(da4, q4, TN)
            dqh = _dot(dob, stb)
            dkh = _dot(v, dstb)
            dbl = jnp.sum(dstn * st, axis=0, keepdims=True)
            dstn = dstn * ebl + jnp.where(mk["diag"], _dg(dob, qhb, TN), 0.0)
            dq_ref[rows, :] = scale * (dqt * eq + dqh * eb)
            dk_ref[rows, :] = dkt * ek + dkh * ekl
            dkk = dkh * kh
            db = dqt * qt - dkt * kt + dqh * qh - dkk
            db = db + jnp.where(last_row, jnp.sum(dkk, axis=0, keepdims=True) + ebl * dbl, 0.0)
            dla_ref[rows, :] = _scan_rows(db, reverse=True)
        dst_ref[...] = dstn

    def row(w):
        return pl.BlockSpec((rg, w), lambda i: (nb - 1 - i, 0))

    def rshape(w):
        return jax.ShapeDtypeStruct((t, w), F32)

    return pl.pallas_call(
        body, name="gla_bwd", grid=(nb,),
        in_specs=[row(256), row(256), row(512), row(256),
                  pl.BlockSpec((ncb, GLA_W, GLA_KW), lambda i: (nb - 1 - i, 0, 0)), row(512)],
        out_specs=[row(256), row(256), row(512), row(256)],
        out_shape=[rshape(256), rshape(256), rshape(512), rshape(256)],
        scratch_shapes=[pltpu.VMEM((GLA_W, GLA_KW), F32)],
        compiler_params=_cparams(1),
    )(gq, gk, gv, la, ss, do)


SWA_G = SWA_QH // SWA_KVH


def _swa_bias():
    n = jnp.arange(3, dtype=jnp.int32)[:, None, None]
    r = (jnp.arange(SWA_G * BLK, dtype=jnp.int32) % BLK)[None, :, None]
    c = jnp.arange(3 * BLK, dtype=jnp.int32)[None, None, :]
    seg = c // BLK
    cc = c % BLK
    qpos = n * BLK + r - PAD
    kpos = jnp.where(seg == 0, (n - 1) * BLK, jnp.where(seg == 1, n * BLK, 0)) + cc - PAD
    band = (seg < 2) & (kpos >= N_META) & (kpos <= qpos) & (qpos - kpos < WINDOW)
    meta = (seg == 2) & (kpos >= 0) & (kpos < N_META) & (kpos <= qpos)
    return jnp.where(band | meta, 0.0, NEG_INF).astype(F32)


def _swa_stack(ref, rows, kh, lo, dtype):
    parts = []
    for g in range(2):
        pair = ref[rows, 128 * (2 * kh + g):128 * (2 * kh + g + 1)]
        zero = jnp.zeros_like(pair)
        parts += [jnp.where(lo, pair, zero), jnp.where(lo, zero, pair)]
    return jnp.concatenate(parts, axis=0).astype(dtype)


def _swa_unstack(x4, lo):
    return [jnp.where(lo, x4[2 * g * BLK:(2 * g + 1) * BLK], x4[(2 * g + 1) * BLK:(2 * g + 2) * BLK])
            for g in range(2)]


def _swa_sink_col(sink_ref, kh):
    blk = lax.broadcasted_iota(jnp.int32, (SWA_G * BLK, 1), 0) // BLK
    col = jnp.full((SWA_G * BLK, 1), sink_ref[SWA_G * kh + SWA_G - 1], F32)
    for e in reversed(range(SWA_G - 1)):
        col = jnp.where(blk == e, sink_ref[SWA_G * kh + e], col)
    return col


def _swa_softmax(qk, bias, sink):
    s = qk * (SWA_HD ** -0.5) + bias
    m = jnp.maximum(jnp.max(s, axis=-1, keepdims=True), sink)
    p = jnp.exp(s - m)
    es = jnp.exp(sink - m)
    inv = 1.0 / (jnp.sum(p, axis=-1, keepdims=True) + es)
    return p * inv, es * inv


def _swa_keys(prev_ref, cur_ref, first_ref, b, ls):
    before = prev_ref[:, ls] if b == 0 else cur_ref[(b - 1) * BLK:b * BLK, ls]
    return jnp.concatenate([before, cur_ref[b * BLK:(b + 1) * BLK, ls], first_ref[:, ls]], axis=0)


def _swa_specs(rs, ns):
    bps = rs // BLK
    cur = lambda w: pl.BlockSpec((rs, w), lambda i: (jnp.minimum(i, ns - 1), 0))
    prev = lambda w: pl.BlockSpec((BLK, w), lambda i: (jnp.maximum(jnp.minimum(i, ns - 1) * bps - 1, 0), 0))
    first = lambda w: pl.BlockSpec((BLK, w), lambda i: (0, 0))
    return cur, prev, first


def _swa_fwd(sinks, sq, sk, sv):
    t = sq.shape[0]
    rs = _seq_tile(t)
    bps, ns = rs // BLK, t // rs

    def body(sink_ref, bias_ref, q_ref, kp_ref, kc_ref, km_ref, vp_ref, vc_ref, vm_ref, o_ref):
        i = pl.program_id(0)
        lo = lax.broadcasted_iota(jnp.int32, (BLK, 128), 1) < 64
        sink_cols = [_swa_sink_col(sink_ref, kh) for kh in range(SWA_KVH)]
        chains = [(b, kh) for b in range(bps) for kh in range(SWA_KVH)]
        scores = []
        for b, kh in chains:
            ls = slice(128 * kh, 128 * (kh + 1))
            q4 = _swa_stack(q_ref, slice(b * BLK, (b + 1) * BLK), kh, lo, BF16)
            scores.append(_dg(q4, _swa_keys(kp_ref, kc_ref, km_ref, b, ls), NT))
        probs = []
        for (b, kh), s in zip(chains, scores):
            p, _ = _swa_softmax(s, bias_ref[jnp.minimum(i * bps + b, 2)], sink_cols[kh])
            probs.append(p.astype(BF16))
        for (b, kh), p in zip(chains, probs):
            ls = slice(128 * kh, 128 * (kh + 1))
            rows = slice(b * BLK, (b + 1) * BLK)
            for g, pair in enumerate(_swa_unstack(_dot(p, _swa_keys(vp_ref, vc_ref, vm_ref, b, ls)), lo)):
                o_ref[rows, 128 * (2 * kh + g):128 * (2 * kh + g + 1)] = pair

    cur, prev, first = _swa_specs(rs, ns)
    bias = _swa_bias()
    return pl.pallas_call(
        body, name="swa_fwd", grid=(ns,),
        in_specs=[pl.BlockSpec(memory_space=pltpu.SMEM), _full(bias.shape), cur(512), prev(256), cur(256), first(256),
                  prev(256), cur(256), first(256)],
        out_specs=cur(512),
        out_shape=jax.ShapeDtypeStruct((t, SWA_W), F32),
        compiler_params=_cparams(1),
    )(sinks, bias, sq, sk, sk, sk, sv, sv, sv)


def _swa_bwd(sinks, sq, sk, sv, o, do, hook=None):
    t = sq.shape[0]
    rs = _seq_tile(t)
    bps, ns = rs // BLK, t // rs

    def body(sink_ref, bias_ref, q_ref, kp_ref, kc_ref, km_ref, vp_ref, vc_ref, vm_ref, o_ref, do_ref,
             dq_ref, dk_ref, dv_ref, dkm_ref, dvm_ref, dsink_ref, pk_ref, pv_ref):
        i = pl.program_id(0)

        @pl.when(i == 0)
        def _():
            pk_ref[...] = jnp.zeros_like(pk_ref)
            pv_ref[...] = jnp.zeros_like(pv_ref)
            dkm_ref[...] = jnp.zeros_like(dkm_ref)
            dvm_ref[...] = jnp.zeros_like(dvm_ref)
            dsink_ref[...] = jnp.zeros_like(dsink_ref)

        @pl.when(i == ns)
        def _():
            dk_ref[...] = pk_ref[...]
            dv_ref[...] = pv_ref[...]

        @pl.when(i < ns)
        def _():
            lo = lax.broadcasted_iota(jnp.int32, (BLK, 128), 1) < 64
            scale = SWA_HD ** -0.5
            sink_cols = [_swa_sink_col(sink_ref, kh) for kh in range(SWA_KVH)]
            parts_k = [[None] * SWA_KVH for _ in range(bps)]
            parts_v = [[None] * SWA_KVH for _ in range(bps)]
            dsinks = [jnp.zeros((1, 1), F32) for _ in range(SWA_QH)]
            chains = [(b, kh) for b in range(bps) for kh in range(SWA_KVH)]
            lanes = lambda kh: slice(128 * kh, 128 * (kh + 1))
            block = lambda b: slice(b * BLK, (b + 1) * BLK)
            q4s = [_swa_stack(q_ref, block(b), kh, lo, BF16) for b, kh in chains]
            scores = [_dg(q4, _swa_keys(kp_ref, kc_ref, km_ref, b, lanes(kh)), NT)
                      for (b, kh), q4 in zip(chains, q4s)]
            do4s = [_swa_stack(do_ref, block(b), kh, lo, F32) for b, kh in chains]
            do4bs = [d.astype(BF16) for d in do4s]
            dps = [_dg(d, _swa_keys(vp_ref, vc_ref, vm_ref, b, lanes(kh)), NT) for (b, kh), d in zip(chains, do4bs)]
            pbs, dss = [], []
            for n_chain, (b, kh) in enumerate(chains):
                p, psink = _swa_softmax(scores[n_chain], bias_ref[jnp.minimum(i * bps + b, 2)], sink_cols[kh])
                delta = jnp.sum(do4s[n_chain] * _swa_stack(o_ref, block(b), kh, lo, F32), axis=-1, keepdims=True)
                dss.append((p * (dps[n_chain] - delta) * scale).astype(BF16))
                pbs.append(p.astype(BF16))
                dsk = psink * delta
                for e in range(SWA_G):
                    h = SWA_G * kh + e
                    dsinks[h] = dsinks[h] - jnp.sum(dsk[e * BLK:(e + 1) * BLK], axis=0, keepdims=True)
            for n_chain, (b, kh) in enumerate(chains):
                kall = _swa_keys(kp_ref, kc_ref, km_ref, b, lanes(kh))
                for g, pair in enumerate(_swa_unstack(_dot(dss[n_chain], kall), lo)):
                    dq_ref[block(b), 128 * (2 * kh + g):128 * (2 * kh + g + 1)] = pair
                parts_k[b][kh] = _dg(dss[n_chain], q4s[n_chain], TN)
                parts_v[b][kh] = _dg(pbs[n_chain], do4bs[n_chain], TN)
            last = slice(rs - BLK, rs)
            for parts, out_ref, pend_ref, meta_ref in ((parts_k, dk_ref, pk_ref, dkm_ref),
                                                       (parts_v, dv_ref, pv_ref, dvm_ref)):
                for kh in range(SWA_KVH):
                    ls = slice(128 * kh, 128 * (kh + 1))
                    if bps > 1:
                        out_ref[0:rs - BLK, ls] = pend_ref[0:rs - BLK, ls]
                    out_ref[last, ls] = pend_ref[last, ls] + parts[0][kh][0:BLK]
                    meta = parts[0][kh][2 * BLK:3 * BLK]
                    for b in range(bps):
                        own = parts[b][kh][BLK:2 * BLK]
                        if b + 1 < bps:
                            own = own + parts[b + 1][kh][0:BLK]
                            meta = meta + parts[b + 1][kh][2 * BLK:3 * BLK]
                        pend_ref[b * BLK:(b + 1) * BLK, ls] = own
                    meta_ref[:, ls] += meta
            for h in range(SWA_QH):
                dsink_ref[h:h + 1, :] += jnp.broadcast_to(dsinks[h], (1, 128))

    cur, prev, first = _swa_specs(rs, ns)
    late = lambda w: pl.BlockSpec((rs, w), lambda i: (jnp.maximum(i - 1, 0), 0))
    bias = _swa_bias()
    return _pallas(
        body, name="swa_bwd", grid=(ns + 1,),
        in_specs=[pl.BlockSpec(memory_space=pltpu.SMEM), _full(bias.shape), cur(512), prev(256), cur(256), first(256),
                  prev(256), cur(256), first(256), cur(512), cur(512)],
        out_specs=[cur(512), late(256), late(256), first(256), first(256), _full((SWA_QH, 128))],
        out_shape=[jax.ShapeDtypeStruct((t, SWA_W), F32), jax.ShapeDtypeStruct((t, 256), F32),
                   jax.ShapeDtypeStruct((t, 256), F32), jax.ShapeDtypeStruct((BLK, 256), F32),
                   jax.ShapeDtypeStruct((BLK, 256), F32), jax.ShapeDtypeStruct((SWA_QH, 128), F32)],
        scratch_shapes=[pltpu.VMEM((rs, 256), F32), pltpu.VMEM((rs, 256), F32)],
        args=(sinks, bias, sq, sk, sk, sk, sv, sv, sv, o, do), hook=hook)


def _mix_out(h1, ogla, gg, oswa, wgn, wsn, wout, wpost):
    t = h1.shape[0]
    tm = _row_tile(t)

    def body(h_ref, og_ref, gg_ref, os_ref, wgn_ref, wsn_ref, wout_ref, wpost_ref, h2_ref, cat_ref, m_ref):
        parts = []
        for h in range(GLA_HEADS):
            ls = slice(GLA_DV * h, GLA_DV * (h + 1))
            y, _, _ = _rms(og_ref[:, ls], wgn_ref[...])
            g = gg_ref[:, ls]
            parts.append(y * (g * _sigmoid(g)))
        ys, _, _ = _rms(os_ref[...], wsn_ref[...])
        cat = jnp.concatenate(parts + [ys], axis=1).astype(BF16)
        cat_ref[...] = cat
        m = _dot(cat, wout_ref[...])
        m_ref[...] = m
        y, _, _ = _rms(m, wpost_ref[...])
        h2_ref[...] = h_ref[...] + y

    def row(w):
        return pl.BlockSpec((tm, w), lambda i: (i, 0))

    return pl.pallas_call(
        body, name="mix_out", grid=(t // tm,),
        in_specs=[row(D_MODEL), row(512), row(512), row(512), _full((1, GLA_DV)), _full((1, SWA_W)),
                  _full((D_MODEL, D_MODEL)), _full((1, D_MODEL))],
        out_specs=[row(D_MODEL), row(D_MODEL), row(D_MODEL)],
        out_shape=[jax.ShapeDtypeStruct((t, D_MODEL), F32), jax.ShapeDtypeStruct((t, D_MODEL), BF16),
                   jax.ShapeDtypeStruct((t, D_MODEL), F32)],
        compiler_params=_cparams(1),
    )(h1, ogla, gg, oswa, wgn, wsn, wout, wpost)


def _mix_out_bwd(dh2, m, ogla, gg, oswa, wgn, wsn, wout, wpost, hook=None):
    t = dh2.shape[0]
    tm = _row_tile(t)

    def body(dh_ref, m_ref, og_ref, gg_ref, os_ref, wgn_ref, wsn_ref, wout_ref, wpost_ref,
             dog_ref, dgg_ref, dos_ref, dm_ref, dwpost_ref, dwgn_ref, dwsn_ref):
        @pl.when(pl.program_id(0) == 0)
        def _():
            dwpost_ref[...] = jnp.zeros_like(dwpost_ref)
            dwgn_ref[...] = jnp.zeros_like(dwgn_ref)
            dwsn_ref[...] = jnp.zeros_like(dwsn_ref)

        wpost = wpost_ref[...]
        _, mh, r = _rms(m_ref[...], wpost)
        dm, dw = _rms_bwd(mh, r, wpost, dh_ref[...])
        dwpost_ref[...] += dw
        dmb = dm.astype(BF16)
        dm_ref[...] = dmb
        dcat = _dg(dmb, wout_ref[...], NT)
        wgn = wgn_ref[...]
        for h in range(GLA_HEADS):
            ls = slice(GLA_DV * h, GLA_DV * (h + 1))
            dog = dcat[:, ls]
            g = gg_ref[:, ls]
            sg = _sigmoid(g)
            y, xh, r = _rms(og_ref[:, ls], wgn)
            dgg_ref[:, ls] = dog * y * (sg * (1.0 + g * (1.0 - sg)))
            dx, dw = _rms_bwd(xh, r, wgn, dog * (g * sg))
            dog_ref[:, ls] = dx
            dwgn_ref[...] += dw
        wsn = wsn_ref[...]
        _, xh, r = _rms(os_ref[...], wsn)
        dx, dw = _rms_bwd(xh, r, wsn, dcat[:, GLA_W:])
        dos_ref[...] = dx
        dwsn_ref[...] += dw

    def row(w):
        return pl.BlockSpec((tm, w), lambda i: (i, 0))

    def rshape(w, dt=F32):
        return jax.ShapeDtypeStruct((t, w), dt)

    return _pallas(
        body, name="mix_out_bwd", grid=(t // tm,),
        in_specs=[row(D_MODEL), row(D_MODEL), row(512), row(512), row(512), _full((1, GLA_DV)), _full((1, SWA_W)),
                  _full((D_MODEL, D_MODEL)), _full((1, D_MODEL))],
        out_specs=[row(512), row(512), row(512), row(D_MODEL), _full((1, D_MODEL)), _full((1, GLA_DV)),
                   _full((1, SWA_W))],
        out_shape=[rshape(512), rshape(512), rshape(512), rshape(D_MODEL, BF16),
                   jax.ShapeDtypeStruct((1, D_MODEL), F32), jax.ShapeDtypeStruct((1, GLA_DV), F32),
                   jax.ShapeDtypeStruct((1, SWA_W), F32)],
        args=(dh2, m, ogla, gg, oswa, wgn, wsn, wout, wpost), hook=hook)


def _mix_in_bwd(dh2, h1, wmixpre, winp, wa2p, bap, cos, sin, ga, dgq, dgk, dgv, dgg, dla, dsq, dsk, dsv, dkm, dvm):
    t = h1.shape[0]
    tm = _row_tile(t)

    def body(dh2_ref, h_ref, w_ref, win_ref, wa2_ref, ba_ref, cos_ref, sin_ref, ga_ref, dgq_ref, dgk_ref, dgv_ref,
             dgg_ref, dla_ref, dsq_ref, dsk_ref, dsv_ref, dkm_ref, dvm_ref,
             dh1_ref, dproj_ref, dw_ref, dwa2_ref, dba_ref):
        i = pl.program_id(0)

        @pl.when(i == 0)
        def _():
            dw_ref[...] = jnp.zeros_like(dw_ref)
            dwa2_ref[...] = jnp.zeros_like(dwa2_ref)
            dba_ref[...] = jnp.zeros_like(dba_ref)

        first = (i == 0).astype(F32)
        c = cos_ref[...]
        s = -sin_ref[...]
        fh = _first_half_mask(tm)
        dproj_ref[:, P_GQ:P_GK] = dgq_ref[...].astype(BF16)
        dproj_ref[:, P_GK:P_GV] = dgk_ref[...].astype(BF16)
        dproj_ref[:, P_GV:P_GG] = dgv_ref[...].astype(BF16)
        dproj_ref[:, P_GG:P_GA] = dgg_ref[...].astype(BF16)
        gab = ga_ref[...].astype(BF16)
        z = _dot(gab, wa2_ref[...]) + ba_ref[...]
        row_id = i * tm + lax.broadcasted_iota(jnp.int32, (tm, 1), 0)
        dz = jnp.where(row_id >= PAD, dla_ref[...] * (1.0 / GLA_TAU) * (1.0 - _sigmoid(z)), 0.0)
        dzb = dz.astype(BF16)
        dba_ref[...] += jnp.sum(dz, axis=0, keepdims=True)
        dwa2_ref[...] += _dg(gab, dzb, TN)
        dproj_ref[:, P_GA:P_SQ] = _dg(dzb, wa2_ref[...], NT).astype(BF16)
        for k in range(4):
            dy = dsq_ref[:, 128 * k:128 * (k + 1)]
            dproj_ref[:, P_SQ + 128 * k:P_SQ + 128 * (k + 1)] = (dy * c + _rot_half(dy, fh) * s).astype(BF16)
        for k in range(2):
            ls = slice(128 * k, 128 * (k + 1))
            dy = dsk_ref[:, ls]
            dy = jnp.concatenate([dy[:BLK] + first * dkm_ref[:, ls], dy[BLK:]], axis=0) if tm > BLK else (
                dy + first * dkm_ref[:, ls])
            dproj_ref[:, P_SK + 128 * k:P_SK + 128 * (k + 1)] = (dy * c + _rot_half(dy, fh) * s).astype(BF16)
            dv = dsv_ref[:, ls]
            dv = jnp.concatenate([dv[:BLK] + first * dvm_ref[:, ls], dv[BLK:]], axis=0) if tm > BLK else (
                dv + first * dvm_ref[:, ls])
            dproj_ref[:, P_SV + 128 * k:P_SV + 128 * (k + 1)] = dv.astype(BF16)
        dn = _dg(dproj_ref[...], win_ref[...], NT)
        w = w_ref[...]
        _, hh, r = _rms(h_ref[...], w)
        dx, dw = _rms_bwd(hh, r, w, dn)
        dw_ref[...] += dw
        dh1_ref[...] = dh2_ref[...] + dx

    def row(w):
        return pl.BlockSpec((tm, w), lambda i: (i, 0))

    return pl.pallas_call(
        body, name="mix_in_bwd", grid=(t // tm,),
        in_specs=[row(D_MODEL), row(D_MODEL), _full((1, D_MODEL)), _full((D_MODEL, P_END)), _full((128, GLA_KW)),
                  _full((1, GLA_KW)), row(128), row(128), row(128), row(256), row(256), row(512), row(512), row(256),
                  row(512), row(256), row(256), _full((BLK, 256)), _full((BLK, 256))],
        out_specs=[row(D_MODEL), row(P_END), _full((1, D_MODEL)), _full((128, GLA_KW)), _full((1, GLA_KW))],
        out_shape=[jax.ShapeDtypeStruct((t, D_MODEL), F32), jax.ShapeDtypeStruct((t, P_END), BF16),
                   jax.ShapeDtypeStruct((1, D_MODEL), F32), jax.ShapeDtypeStruct((128, GLA_KW), F32),
                   jax.ShapeDtypeStruct((1, GLA_KW), F32)],
        compiler_params=_cparams(1),
    )(dh2, h1, wmixpre, winp, wa2p, bap, cos, sin, ga, dgq, dgk, dgv, dgg, dla, dsq, dsk, dsv, dkm, dvm)


def _adamw_update(w, g, m, v):
    m = ADAM_B1 * m + (1.0 - ADAM_B1) * g
    v = ADAM_B2 * v + (1.0 - ADAM_B2) * (g * g)
    m_hat = m / (1.0 - ADAM_B1 ** ADAM_STEP)
    v_hat = v / (1.0 - ADAM_B2 ** ADAM_STEP)
    return -ADAM_LR * (m_hat / (jnp.sqrt(v_hat) + ADAM_EPS) + ADAM_WD * w), m, v


def _adamw_halves(w, g_mine, g_other, m, v, c_idx, row0=0):
    r, c = w.shape
    h = g_mine.shape[0]
    tr = _div_tile(math.gcd(r, h))
    nth = h // tr
    t0 = row0 // tr
    assert t0 * tr == row0

    def body(c_ref, w_ref, gm_ref, go_ref, m_ref, v_ref, g_ref, d_ref, nm_ref, nv_ref):
        hh = (t0 + pl.program_id(0)) // nth
        g = jnp.where(hh == c_ref[0], gm_ref[...], go_ref[...])
        g_ref[...] = g
        d_ref[...], nm_ref[...], nv_ref[...] = _adamw_update(w_ref[...], g, m_ref[...], v_ref[...])

    spec = pl.BlockSpec((tr, c), lambda i, c_ref: (i, 0))

    def gspec(is_mine):
        def index(i, c_ref):
            used = ((t0 + i) // nth == c_ref[0]) == is_mine
            return (jnp.where(used, (t0 + i) % nth, 0), 0)
        return pl.BlockSpec((tr, c), index)

    shape = jax.ShapeDtypeStruct((r, c), F32)
    return pl.pallas_call(
        body, name="adamw_halves",
        grid_spec=pltpu.PrefetchScalarGridSpec(
            num_scalar_prefetch=1, grid=(r // tr,), in_specs=[spec, gspec(True), gspec(False), spec, spec],
            out_specs=[spec] * 4),
        out_shape=[shape] * 4, compiler_params=_cparams(1),
    )(c_idx, w, g_mine, g_other, m, v)


def _place():
    x, y, c = lax.axis_index("x"), lax.axis_index("y"), lax.axis_index("c")
    chips = [(1 - x, y), (x, 1 - y), (1 - x, 1 - y)]
    return x, y, c, chips


def _remote(send_sem, recv_sem, src, dst, to):
    return pltpu.make_async_remote_copy(src_ref=src, dst_ref=dst, send_sem=send_sem, recv_sem=recv_sem,
                                        device_id=to, device_id_type=MESH)


def _half(ref_rows, c):
    h = ref_rows // 2
    return pl.ds(pl.multiple_of(c * h, 8), h)


def _own_slot(shard, q):
    return lax.dynamic_update_slice(jnp.zeros((N_CHIPS,) + shard.shape, shard.dtype), shard[None], (q, 0, 0))


def _stack_own_slot(mats, q_idx):
    r, w = mats[0].shape
    tr = _div_tile(r)
    per = r // tr
    n = len(mats)

    def body(q_ref, *refs):
        m_refs, o_ref = refs[:n], refs[n]
        s = pl.program_id(0)
        for k in range(n):
            @pl.when(s // per == k)
            def _(k=k):
                o_ref[...] = m_refs[k][...].astype(BF16)

    def rows_of(k):
        return lambda s, q_ref: (jnp.where(s // per == k, s % per, 0), 0)

    return pl.pallas_call(
        body, name="stack_own_slot",
        grid_spec=pltpu.PrefetchScalarGridSpec(
            num_scalar_prefetch=1, grid=(n * per,),
            in_specs=[pl.BlockSpec((tr, w), rows_of(k)) for k in range(n)],
            out_specs=pl.BlockSpec((None, tr, w), lambda s, q_ref: (q_ref[0], s, 0))),
        out_shape=jax.ShapeDtypeStruct((N_CHIPS, n * r, w), BF16), compiler_params=_cparams(1),
    )(q_idx, *mats)


class _GatherChips:
    has_mid = True

    def __init__(self, bufs):
        n = len(bufs)
        self.inputs = list(bufs)
        self.out_shape = [jax.ShapeDtypeStruct(b.shape, b.dtype) for b in bufs]
        self.aliases = [(t, t) for t in range(n)]
        self.scratch = [pltpu.SemaphoreType.DMA((n, 6)), pltpu.SemaphoreType.DMA((n, 6))]

    def start(self, ins, outs, scr):
        send, recv = scr
        x, y, c, chips = _place()
        q = 2 * x + y
        for t, (i_ref, o_ref) in enumerate(zip(ins, outs)):
            rows = _half(i_ref.shape[1], c)
            for j, (cx, cy) in enumerate(chips):
                _remote(send.at[t, j], recv.at[t, j], i_ref.at[q, rows], o_ref.at[q, rows], (cx, cy, c)).start()

    def mid(self, ins, outs, scr):
        send, recv = scr
        x, y, c, chips = _place()
        for t, o_ref in enumerate(outs):
            rows = _half(o_ref.shape[1], c)
            for j, (cx, cy) in enumerate(chips):
                slot = o_ref.at[2 * cx + cy, rows]
                _remote(send.at[t, j], recv.at[t, j], slot, slot, (cx, cy, c)).wait_recv()
                _remote(send.at[t, 3 + j], recv.at[t, 3 + j], slot, slot, (x, y, 1 - c)).start()

    def finish(self, ins, outs, scr):
        send, recv = scr
        x, y, c, chips = _place()
        for t, o_ref in enumerate(outs):
            mine, other = _half(o_ref.shape[1], c), _half(o_ref.shape[1], 1 - c)
            for j, (cx, cy) in enumerate(chips):
                slot = o_ref.at[2 * cx + cy, other]
                _remote(send.at[t, 3 + j], recv.at[t, 3 + j], slot, slot, (x, y, 1 - c)).wait_recv()
            for j, (cx, cy) in enumerate(chips):
                sent = o_ref.at[2 * cx + cy, mine]
                _remote(send.at[t, j], recv.at[t, j], sent, sent, (cx, cy, c)).wait_send()
                _remote(send.at[t, 3 + j], recv.at[t, 3 + j], sent, sent, (x, y, 1 - c)).wait_send()


class _PairExchange:
    has_mid = False
    aliases = ()

    def __init__(self, arrs):
        n = len(arrs)
        self.inputs = list(arrs)
        self.out_shape = [jax.ShapeDtypeStruct((a.shape[0], a.shape[1] // 2, a.shape[2]), a.dtype) for a in arrs]
        self.scratch = [pltpu.SemaphoreType.DMA((n,)), pltpu.SemaphoreType.DMA((n,))]

    def _copies(self, ins, outs, scr):
        send, recv = scr
        x, y, c, _ = _place()
        return [_remote(send.at[t], recv.at[t], i_ref.at[:, _half(i_ref.shape[1], 1 - c)], o_ref, (x, y, 1 - c))
                for t, (i_ref, o_ref) in enumerate(zip(ins, outs))]

    def start(self, ins, outs, scr):
        for cp in self._copies(ins, outs, scr):
            cp.start()

    def finish(self, ins, outs, scr):
        for cp in self._copies(ins, outs, scr):
            cp.wait()


class _ChipScatter:
    has_mid = False
    aliases = ()

    def __init__(self, arrs):
        n = len(arrs)
        self.inputs = list(arrs)
        self.out_shape = [jax.ShapeDtypeStruct((3,) + a.shape[1:], a.dtype) for a in arrs]
        self.scratch = [pltpu.SemaphoreType.DMA((n, 3)), pltpu.SemaphoreType.DMA((n, 3))]

    def _copies(self, ins, outs, scr):
        send, recv = scr
        x, y, c, chips = _place()
        return [_remote(send.at[t, j], recv.at[t, j], i_ref.at[2 * cx + cy], o_ref.at[j], (cx, cy, c))
                for t, (i_ref, o_ref) in enumerate(zip(ins, outs)) for j, (cx, cy) in enumerate(chips)]

    def start(self, ins, outs, scr):
        for cp in self._copies(ins, outs, scr):
            cp.start()

    def finish(self, ins, outs, scr):
        for cp in self._copies(ins, outs, scr):
            cp.wait()


class _PairShare:
    has_mid = False
    aliases = ()

    def __init__(self, arrs):
        n = len(arrs)
        self.inputs = list(arrs)
        self.out_shape = [jax.ShapeDtypeStruct(a.shape, a.dtype) for a in arrs]
        self.scratch = [pltpu.SemaphoreType.DMA((n,)), pltpu.SemaphoreType.DMA((n,))]

    def _copies(self, ins, outs, scr):
        send, recv = scr
        x, y, c, _ = _place()
        return [_remote(send.at[t], recv.at[t], i_ref, o_ref, (x, y, 1 - c))
                for t, (i_ref, o_ref) in enumerate(zip(ins, outs))]

    def start(self, ins, outs, scr):
        for cp in self._copies(ins, outs, scr):
            cp.start()

    def finish(self, ins, outs, scr):
        for cp in self._copies(ins, outs, scr):
            cp.wait()


def _comm_call(hook, name):
    n_in, n_out = len(hook.inputs), len(hook.out_shape)

    def body(*refs):
        ins, outs, scr = refs[:n_in], refs[n_in:n_in + n_out], refs[n_in + n_out:]
        hook.start(ins, outs, scr)
        if hook.has_mid:
            hook.mid(ins, outs, scr)
        hook.finish(ins, outs, scr)

    return pl.pallas_call(body, name=name, in_specs=[ANY] * n_in, out_specs=[ANY] * n_out,
                          out_shape=list(hook.out_shape), scratch_shapes=list(hook.scratch),
                          input_output_aliases=dict(hook.aliases))(*hook.inputs)


class _GatherDevices:
    has_mid = True
    aliases = ()

    def __init__(self, vecs):
        n = len(vecs)
        self.inputs = list(vecs)
        self.out_shape = [jax.ShapeDtypeStruct((N_DEV,) + v.shape, v.dtype) for v in vecs]
        self.scratch = [pltpu.SemaphoreType.DMA((n, 7)), pltpu.SemaphoreType.DMA((n, 7)),
                        pltpu.SemaphoreType.DMA((n,))]

    @staticmethod
    def _copy(scr, t, k, out_ref, block, to, src=None):
        send, recv, _ = scr
        px, py, pc = block
        slot = out_ref.at[4 * px + 2 * py + pc]
        return _remote(send.at[t, k], recv.at[t, k], slot if src is None else src, slot, to)

    def start(self, ins, outs, scr):
        x, y, c, chips = _place()
        me = (x, y, c)
        for t, (x_ref, out_ref) in enumerate(zip(ins, outs)):
            pltpu.make_async_copy(x_ref, out_ref.at[4 * x + 2 * y + c], scr[2].at[t]).start()
            self._copy(scr, t, 0, out_ref, me, (x, y, 1 - c), src=x_ref).start()
            for j, chip in enumerate(chips):
                self._copy(scr, t, 1 + j, out_ref, me, (*chip, c), src=x_ref).start()

    def mid(self, ins, outs, scr):
        x, y, c, chips = _place()
        for t, out_ref in enumerate(outs):
            for j, chip in enumerate(chips):
                self._copy(scr, t, 1 + j, out_ref, (*chip, c), (x, y, c)).wait_recv()
                self._copy(scr, t, 4 + j, out_ref, (*chip, c), (x, y, 1 - c)).start()

    def finish(self, ins, outs, scr):
        x, y, c, chips = _place()
        me = (x, y, c)
        for t, (x_ref, out_ref) in enumerate(zip(ins, outs)):
            self._copy(scr, t, 0, out_ref, (x, y, 1 - c), me).wait_recv()
            for j, chip in enumerate(chips):
                self._copy(scr, t, 4 + j, out_ref, (*chip, 1 - c), me).wait_recv()
            self._copy(scr, t, 0, out_ref, me, (x, y, 1 - c), src=x_ref).wait_send()
            for j, chip in enumerate(chips):
                self._copy(scr, t, 1 + j, out_ref, me, (*chip, c), src=x_ref).wait_send()
                self._copy(scr, t, 4 + j, out_ref, (*chip, c), (x, y, 1 - c)).wait_send()
            pltpu.make_async_copy(x_ref, out_ref.at[4 * x + 2 * y + c], scr[2].at[t]).wait()


class _Hooks:
    def __init__(self, hooks):
        self.hooks = list(hooks)
        self.has_mid = any(h.has_mid for h in hooks)
        self.inputs = [a for h in hooks for a in h.inputs]
        self.out_shape = [s for h in hooks for s in h.out_shape]
        self.scratch = [s for h in hooks for s in h.scratch]
        self.aliases = []
        i0 = o0 = 0
        for h in hooks:
            self.aliases += [(i0 + a, o0 + b) for a, b in h.aliases]
            i0 += len(h.inputs)
            o0 += len(h.out_shape)

    def _each(self, ins, outs, scr):
        i0 = o0 = s0 = 0
        for h in self.hooks:
            ni, no, ns = len(h.inputs), len(h.out_shape), len(h.scratch)
            yield h, ins[i0:i0 + ni], outs[o0:o0 + no], scr[s0:s0 + ns]
            i0, o0, s0 = i0 + ni, o0 + no, s0 + ns

    def start(self, ins, outs, scr):
        for h, i, o, s in self._each(ins, outs, scr):
            h.start(i, o, s)

    def mid(self, ins, outs, scr):
        for h, i, o, s in self._each(ins, outs, scr):
            if h.has_mid:
                h.mid(i, o, s)

    def finish(self, ins, outs, scr):
        for h, i, o, s in self._each(ins, outs, scr):
            h.finish(i, o, s)

    def split(self, outs):
        res, o0 = [], 0
        for h in self.hooks:
            res.append(list(outs[o0:o0 + len(h.out_shape)]))
            o0 += len(h.out_shape)
        return res


def _pair_sum(g, other, c_idx):
    nq, r, w = g.shape
    h = r // 2
    tr = _div_tile(h)
    nt = h // tr

    def body(c_ref, g_ref, o_ref, s_ref):
        s_ref[...] = (g_ref[...].astype(F32) + o_ref[...].astype(F32)).astype(s_ref.dtype)

    return pl.pallas_call(
        body, name="pair_sum",
        grid_spec=pltpu.PrefetchScalarGridSpec(
            num_scalar_prefetch=1, grid=(nq, nt),
            in_specs=[pl.BlockSpec((None, tr, w), lambda k, i, c_ref: (k, c_ref[0] * nt + i, 0)),
                      pl.BlockSpec((None, tr, w), lambda k, i, c_ref: (k, i, 0))],
            out_specs=pl.BlockSpec((None, tr, w), lambda k, i, c_ref: (k, i, 0))),
        out_shape=jax.ShapeDtypeStruct((nq, h, w), g.dtype),
        compiler_params=_cparams(2),
    )(c_idx, g, other)


def _pair_exchange_sum(arrs):
    n = len(arrs)
    halves = [(a.shape[0], a.shape[1] // 2, a.shape[2]) for a in arrs]

    def body(*refs):
        ins, outs, mine, theirs = (refs[k * n:(k + 1) * n] for k in range(4))
        send, recv, load = refs[4 * n:]
        x, y, c, _ = _place()
        remote = [_remote(send.at[t], recv.at[t], ins[t].at[:, _half(ins[t].shape[1], 1 - c)], theirs[t],
                          (x, y, 1 - c)) for t in range(n)]
        local = [pltpu.make_async_copy(ins[t].at[:, _half(ins[t].shape[1], c)], mine[t], load.at[t])
                 for t in range(n)]
        for cp in remote + local:
            cp.start()
        for t in range(n):
            local[t].wait()
            remote[t].wait()
            for q in range(halves[t][0]):
                outs[t][q] = (mine[t][q].astype(F32) + theirs[t][q].astype(F32)).astype(outs[t].dtype)

    return pl.pallas_call(
        body, name="pair_exchange_sum", in_specs=[ANY] * n,
        out_shape=[jax.ShapeDtypeStruct(s, a.dtype) for s, a in zip(halves, arrs)],
        scratch_shapes=[pltpu.VMEM(s, a.dtype) for s, a in zip(halves, arrs)] * 2
        + [pltpu.SemaphoreType.DMA((n,))] * 3,
        compiler_params=_cparams(0),
    )(*arrs)


def _chip_sum(s, others, q_idx):
    _, h, w = s.shape
    tr = _div_tile(h)

    def body(q_ref, s_ref, o_ref, out_ref):
        out_ref[...] = ((s_ref[...].astype(F32) + o_ref[0].astype(F32)) + o_ref[1].astype(F32)) + o_ref[2].astype(F32)

    return pl.pallas_call(
        body, name="chip_sum",
        grid_spec=pltpu.PrefetchScalarGridSpec(
            num_scalar_prefetch=1, grid=(h // tr,),
            in_specs=[pl.BlockSpec((None, tr, w), lambda i, q_ref: (q_ref[0], i, 0)),
                      pl.BlockSpec((3, tr, w), lambda i, q_ref: (0, i, 0))],
            out_specs=pl.BlockSpec((tr, w), lambda i, q_ref: (i, 0))),
        out_shape=jax.ShapeDtypeStruct((h, w), F32),
        compiler_params=_cparams(1),
    )(q_idx, s, others)


def _small_update(q_idx, parts, ws, ms, vs, col_block):
    n = len(parts)
    has_w = [w is not None for w in ws]

    def body(q_ref, *refs):
        pos = 0
        ins = []
        for t in range(n):
            k = 4 if has_w[t] else 1
            ins.append(refs[pos:pos + k])
            pos += k
        outs = refs[pos:]
        opos = 0
        for t in range(n):
            p_ref = ins[t][0]
            g = p_ref[0]
            for s in range(1, p_ref.shape[0]):
                g = g + p_ref[s]
            if has_w[t]:
                _, w_ref, m_ref, v_ref = ins[t]
                g_ref, d_ref, nm_ref, nv_ref = outs[opos:opos + 4]
                opos += 4
                g_ref[...] = g
                d_ref[...], nm_ref[...], nv_ref[...] = _adamw_update(w_ref[...], g, m_ref[...], v_ref[...])
            else:
                outs[opos][...] = g
                opos += 1

    def whole(shape):
        nd = len(shape)
        return pl.BlockSpec(shape, lambda i, q_ref: (0,) * nd)

    in_specs, out_specs, out_shape, args = [], [], [], []
    for t in range(n):
        k, r, wf = parts[t].shape
        if col_block[t]:
            w = wf // N_CHIPS
            in_specs.append(pl.BlockSpec((k, r, w), lambda i, q_ref: (0, 0, q_ref[0])))
        else:
            w = wf
            in_specs.append(whole((k, r, wf)))
        args.append(parts[t])
        if has_w[t]:
            assert ws[t].shape == (r, w), (ws[t].shape, r, w)
            in_specs += [whole((r, w))] * 3
            args += [ws[t], ms[t], vs[t]]
            out_specs += [whole((r, w))] * 4
            out_shape += [jax.ShapeDtypeStruct((r, w), F32)] * 4
        else:
            out_specs.append(whole((r, w)))
            out_shape.append(jax.ShapeDtypeStruct((r, w), F32))
    return pl.pallas_call(
        body, name="small_update",
        grid_spec=pltpu.PrefetchScalarGridSpec(num_scalar_prefetch=1, grid=(1,), in_specs=in_specs,
                                               out_specs=out_specs),
        out_shape=out_shape, compiler_params=_cparams(1),
    )(q_idx, *args)


_PACK_SEGMENTS = ((0, 1552), None, (1552, 2064), (2064, 2128), (2064, 2128), (2128, 2192), (2128, 2192),
                  (2192, 2256), (2192, 2256), (2256, 2320), (2256, 2320))
_UNPACK_SEGMENTS = (((0, 1552), (0,)), ((1552, 2064), (P_SQ,)), ((2064, 2128), (P_SK, P_SK + 64)),
                    ((2128, 2192), (P_SK + 128, P_SK + 192)), ((2192, 2256), (P_SV, P_SV + 64)),
                    ((2256, 2320), (P_SV + 128, P_SV + 192)))


def _pack_win(w4):
    per = w4.shape[2]
    pieces = []
    for seg in _PACK_SEGMENTS:
        if seg is None:
            pieces.append(jnp.zeros((w4.shape[1], 128 - GLA_RANK), w4.dtype))
            continue
        for q in range(w4.shape[0]):
            lo, hi = max(seg[0], q * per), min(seg[1], (q + 1) * per)
            if lo < hi:
                pieces.append(w4[q][:, lo - q * per:hi - q * per])
    return jnp.concatenate(pieces, axis=1)


def _unpack_dwin(d):
    per = D_IN // N_CHIPS
    chips = []
    for q in range(N_CHIPS):
        pieces = []
        for (a, b), starts in _UNPACK_SEGMENTS:
            lo, hi = max(a, q * per), min(b, (q + 1) * per)
            if lo < hi:
                copies = [d[:, s + lo - a:s + hi - a] for s in starts]
                pieces.append(copies[0] if len(copies) == 1 else copies[0] + copies[1])
        chips.append(jnp.concatenate(pieces, axis=1))
    return jnp.stack(chips)


def _local_step(x, target, meta, p):
    s = x.shape[0]
    t = s + BLK
    h0 = jnp.concatenate([jnp.zeros((PAD, D_MODEL), F32), meta, x], axis=0)
    cos, sin = _rope_tables(t)

    h1, n1, g1, u1, a1, f1 = _ffn_fwd(h0, p["ffn1_pre_norm"], p["ffn1_w"], p["ffn1_post_norm"])
    n2, gq, gk, gv, gg, ga, la, sq, sk, sv = _mix_proj(h1, p["mix_pre_norm"], p["w_in"], p["gla_w_a2"], p["gla_b_a"],
                                                       cos, sin)
    ogla, ss = _gla_fwd(gq, gk, gv, la)
    oswa = _swa_fwd(p["swa_sinks"], sq, sk, sv)
    h2, cat, m = _mix_out(h1, ogla, gg, oswa, p["gla_out_norm"], p["swa_out_norm"], p["w_out"], p["mix_post_norm"])
    grads = {}
    dy, n3, g3, u3, a3, df3, grads["ffn2_post_norm"], sse = _ffn_fwd(
        h2, p["ffn2_pre_norm"], p["ffn2_w"], p["ffn2_post_norm"], target=target)

    dh2, dg3, du3, grads["ffn2_pre_norm"] = _ffn_bwd(
        dy, h2, None, g3, u3, p["ffn2_pre_norm"], p["ffn2_w"], p["ffn2_post_norm"], df=df3)
    (gud,) = _ffn_wgrad(n3, df3, dg3, du3, a3)
    grads["ffn2_w_gate"], grads["ffn2_w_up"], grads["ffn2_w_down"] = gud[:, :FJ], gud[:, FJ:2 * FJ], gud[:, 2 * FJ:]

    dogla, dgg, doswa, dm, grads["mix_post_norm"], grads["gla_out_norm"], grads["swa_out_norm"] = _mix_out_bwd(
        dh2, m, ogla, gg, oswa, p["gla_out_norm"], p["swa_out_norm"], p["w_out"], p["mix_post_norm"])
    grads["w_out"] = _xty(cat, dm)
    dsq, dsk, dsv, dkm, dvm, dsinks = _swa_bwd(p["swa_sinks"], sq, sk, sv, oswa, doswa)
    grads["swa_sinks"] = dsinks[:, 0]
    dgq, dgk, dgv, dla = _gla_bwd(gq, gk, gv, la, ss, dogla)
    dh1, dproj, grads["mix_pre_norm"], dwa2p, grads["gla_b_a"] = _mix_in_bwd(
        dh2, h1, p["mix_pre_norm"], p["w_in"], p["gla_w_a2"], p["gla_b_a"], cos, sin, ga, dgq, dgk, dgv, dgg, dla,
        dsq, dsk, dsv, dkm, dvm)
    grads["gla_w_a2"] = dwa2p[:GLA_RANK]
    grads["w_in"] = _unpack_dwin(_xty(n2, dproj))

    dh0, df1, dg1, du1, grads["ffn1_pre_norm"], grads["ffn1_post_norm"] = _ffn_bwd(
        dh1, h0, f1, g1, u1, p["ffn1_pre_norm"], p["ffn1_w"], p["ffn1_post_norm"])
    (gud,) = _ffn_wgrad(n1, df1, dg1, du1, a1)
    grads["ffn1_w_gate"], grads["ffn1_w_up"], grads["ffn1_w_down"] = gud[:, :FJ], gud[:, FJ:2 * FJ], gud[:, 2 * FJ:]
    grads["meta_tokens"] = dh0[PAD:BLK]
    return sse[0, 0], dh0[BLK:], grads


WEIGHTS = ['meta_tokens', 'ffn1_pre_norm', 'ffn1_w_gate', 'ffn1_w_up', 'ffn1_w_down', 'ffn1_post_norm',
           'mix_pre_norm', 'w_in', 'gla_w_a2', 'gla_b_a', 'gla_out_norm', 'swa_sinks', 'swa_out_norm', 'w_out',
           'mix_post_norm', 'ffn2_pre_norm', 'ffn2_w_gate', 'ffn2_w_up', 'ffn2_w_down', 'ffn2_post_norm']
BIG = ['ffn1_w_gate', 'ffn1_w_up', 'ffn1_w_down', 'w_in', 'w_out', 'ffn2_w_gate', 'ffn2_w_up', 'ffn2_w_down']
SMALL = [n for n in WEIGHTS if n not in BIG]
FJ = D_FF // N_CHIPS
D_IN_J = D_IN // N_CHIPS
D_OUT_J = D_MODEL // N_CHIPS
TRANSPOSED = ('ffn1_w_gate', 'ffn1_w_up', 'ffn2_w_gate', 'ffn2_w_up')


def _shard2d(name, a):
    return a[0].T if name in TRANSPOSED else a[0]


def _unshard2d(name, a):
    return (a.T if name in TRANSPOSED else a)[None]


def kernel(x, meta_tokens, ffn1_pre_norm, ffn1_w_gate, ffn1_w_up, ffn1_w_down, ffn1_post_norm, mix_pre_norm, w_in, gla_w_a2, gla_b_a, gla_out_norm, swa_sinks, swa_out_norm, w_out, mix_post_norm, ffn2_pre_norm, ffn2_w_gate, ffn2_w_up, ffn2_w_down, ffn2_post_norm, loss_target, m_meta_tokens, m_ffn1_pre_norm, m_ffn1_w_gate, m_ffn1_w_up, m_ffn1_w_down, m_ffn1_post_norm, m_mix_pre_norm, m_w_in, m_gla_w_a2, m_gla_b_a, m_gla_out_norm, m_swa_sinks, m_swa_out_norm, m_w_out, m_mix_post_norm, m_ffn2_pre_norm, m_ffn2_w_gate, m_ffn2_w_up, m_ffn2_w_down, m_ffn2_post_norm, v_meta_tokens, v_ffn1_pre_norm, v_ffn1_w_gate, v_ffn1_w_up, v_ffn1_w_down, v_ffn1_post_norm, v_mix_pre_norm, v_w_in, v_gla_w_a2, v_gla_b_a, v_gla_out_norm, v_swa_sinks, v_swa_out_norm, v_w_out, v_mix_post_norm, v_ffn2_pre_norm, v_ffn2_w_gate, v_ffn2_w_up, v_ffn2_w_down, v_ffn2_post_norm):
    args = dict(locals())
    w = {n: args[n] for n in WEIGHTS}
    mom = {n: args["m_" + n] for n in WEIGHTS}
    var = {n: args["v_" + n] for n in WEIGHTS}
    cx, cy, cc = lax.axis_index("x"), lax.axis_index("y"), lax.axis_index("c")
    q_idx = (2 * cx + cy).astype(jnp.int32).reshape(1)
    c_idx = cc.astype(jnp.int32).reshape(1)

    q_chip = 2 * cx + cy
    bf = {n: _own_slot(_shard2d(n, w[n]).astype(BF16), q_chip) for n in ("w_in", "w_out")}
    for ffn in ("ffn1", "ffn2"):
        bf[ffn] = _stack_own_slot([_shard2d(ffn + s, w[ffn + s]) for s in ("_w_gate", "_w_up", "_w_down")], q_idx)
    qc_idx = jnp.stack([q_chip, cc]).astype(jnp.int32)
    sinks = w["swa_sinks"].reshape(SWA_QH)

    seq, target = x[0], loss_target[0]
    t = seq.shape[0] + BLK
    h0, n1 = _embed_norm(seq, _own_slot(w["meta_tokens"], q_chip), w["ffn1_pre_norm"])
    cos, sin = _rope_tables(t)
    late = _GatherChips([bf["w_in"], bf["w_out"], bf["ffn2"],
                         _own_slot(w["gla_w_a2"].reshape(GLA_RANK, GLA_KW // N_CHIPS), q_chip)])
    (h1, g1, u1, a1, f1), (w31,), (win4, wout4, w32, wa24) = _ffn_fwd_gather(
        h0, n1, bf["ffn1"], w["ffn1_post_norm"], qc_idx, late)
    wa2p = jnp.pad(wa24.transpose(1, 0, 2).reshape(GLA_RANK, GLA_KW), ((0, 128 - GLA_RANK), (0, 0))).astype(BF16)
    winp = _pack_win(win4)
    wout = wout4.reshape(D_MODEL, D_MODEL)
    n2, gq, gk, gv, gg, ga, la, sq, sk, sv = _mix_proj(h1, w["mix_pre_norm"], winp, wa2p, w["gla_b_a"], cos, sin)
    ogla, ss = _gla_fwd(gq, gk, gv, la)
    oswa = _swa_fwd(sinks, sq, sk, sv)
    h2, cat, m = _mix_out(h1, ogla, gg, oswa, w["gla_out_norm"], w["swa_out_norm"], wout, w["mix_post_norm"])
    g = {}
    dy, n3, g3, u3, a3, df3, g["ffn2_post_norm"], sse = _ffn_fwd(
        h2, w["ffn2_pre_norm"], w32, w["ffn2_post_norm"], target=target)

    dh2, dg3, du3, g["ffn2_pre_norm"] = _ffn_bwd(
        dy, h2, None, g3, u3, w["ffn2_pre_norm"], w32, w["ffn2_post_norm"], df=df3)
    (gf2,) = _ffn_wgrad(n3, df3, dg3, du3, a3)
    (dogla, dgg, doswa, dm, g["mix_post_norm"], g["gla_out_norm"], g["swa_out_norm"]), (rgf2,) = _mix_out_bwd(
        dh2, m, ogla, gg, oswa, w["gla_out_norm"], w["swa_out_norm"], wout, w["mix_post_norm"],
        hook=_PairExchange([gf2]))
    sgf2 = _pair_sum(gf2, rgf2, c_idx)
    gout = _xty(cat, dm).reshape(N_CHIPS, D_OUT_J, D_MODEL)
    (dsq, dsk, dsv, dkm, dvm, dsinks), (ogf2,) = _swa_bwd(sinks, sq, sk, sv, oswa, doswa,
                                                          hook=_ChipScatter([sgf2]))
    g["swa_sinks"] = dsinks
    dgq, dgk, dgv, dla = _gla_bwd(gq, gk, gv, la, ss, dogla)
    dh1, dproj, g["mix_pre_norm"], dwa2p, g["gla_b_a"] = _mix_in_bwd(
        dh2, h1, w["mix_pre_norm"], winp, wa2p, w["gla_b_a"], cos, sin, ga, dgq, dgk, dgv, dgg, dla,
        dsq, dsk, dsv, dkm, dvm)
    g["gla_w_a2"] = dwa2p[:GLA_RANK]
    gin = _unpack_dwin(_xty(n2, dproj))
    sgin, sgout = _pair_exchange_sum([gin, gout])
    dh_first, grad_x, df1, dg1, du1, g["ffn1_pre_norm"], g["ffn1_post_norm"] = _ffn_bwd(
        dh1, h0, f1, g1, u1, w["ffn1_pre_norm"], w31, w["ffn1_post_norm"], split_first_block=True)
    g["meta_tokens"] = dh_first[PAD:BLK]
    late_small = ["gla_w_a2", "swa_sinks"]
    direct = [n for n in SMALL if n not in late_small]
    names = direct + late_small
    half_f2 = _chip_sum(sgf2, ogf2, q_idx)
    hooks = _Hooks([_ChipScatter([sgin, sgout]), _GatherDevices([g[n] for n in names] + [sse]),
                    _PairShare([half_f2])])
    own1, others1, houts = _ffn_wgrad_reduce(n1, df1, dg1, du1, a1, qc_idx, hooks)
    (ogin, ogout), gathered, (other_f2,) = hooks.split(houts)
    halves = [_chip_sum(own1[None], others1, jnp.zeros((1,), jnp.int32))]
    halves += [_chip_sum(s, o, q_idx) for s, o in ((sgin, ogin), (sgout, ogout))]
    others = list(_comm_call(_PairShare(halves), "pair_share")) + [other_f2]
    halves.append(half_f2)
    reduced = {"ffn1_w_gate": (0, 0), "ffn1_w_up": (0, FJ), "ffn1_w_down": (0, 2 * FJ), "w_in": (1, 0),
               "w_out": (2, 0), "ffn2_w_gate": (3, 0), "ffn2_w_up": (3, FJ), "ffn2_w_down": (3, 2 * FJ)}
    grad, delta, new_m, new_v = {}, {}, {}, {}
    for n in BIG:
        k, row0 = reduced[n]
        outs = _adamw_halves(_shard2d(n, w[n]), halves[k], others[k], _shard2d(n, mom[n]), _shard2d(n, var[n]),
                             c_idx, row0)
        grad[n], delta[n], new_m[n], new_v[n] = [_unshard2d(n, a) for a in outs]

    late = late_small
    mat = lambda a: a.reshape(a.shape[-2:])
    none3 = [None] * (len(late) + 1)
    outs = _small_update(q_idx, gathered, [mat(w[n]) for n in direct] + none3, [mat(mom[n]) for n in direct] + none3,
                         [mat(var[n]) for n in direct] + none3, [n == "meta_tokens" for n in names] + [False])
    sum_a2, sum_sinks, sum_sse = outs[4 * len(direct):]
    loss = sum_sse[0, 0] * (0.5 / D_MODEL)
    g_late = [lax.dynamic_slice_in_dim(sum_a2, q_chip * (GLA_KW // N_CHIPS), GLA_KW // N_CHIPS, axis=1)[None],
              sum_sinks[:, 0].reshape(1, 1, SWA_QH)]
    outs = list(outs[:4 * len(direct)]) + list(_small_update(
        q_idx, g_late, [mat(w[n]) for n in late], [mat(mom[n]) for n in late], [mat(var[n]) for n in late],
        [False, False]))
    for k, n in enumerate(names):
        grad[n], delta[n], new_m[n], new_v[n] = [a.reshape(w[n].shape) for a in outs[4 * k:4 * k + 4]]

    return (loss, grad_x[None], *[grad[n] for n in WEIGHTS], *[delta[n] for n in WEIGHTS],
            *[new_m[n] for n in WEIGHTS], *[new_v[n] for n in WEIGHTS])
```

```python
import functools
import math

import numpy as np
import jax
import jax.numpy as jnp
from jax import lax
from jax.experimental import pallas as pl
from jax.experimental.pallas import tpu as pltpu

F32 = jnp.float32
BF16 = jnp.bfloat16
MESH = pl.DeviceIdType.MESH

D_MODEL = 1024
D_FF = 2816
N_CHIPS = 4
N_DEV = 8
N_META = 16
BLK = 128
PAD = BLK - N_META
GLA_CHUNK = 64
GLA_HEADS = 4
GLA_DV = 128
GLA_DK = 64
GLA_KW = GLA_HEADS * GLA_DK
GLA_W = GLA_HEADS * GLA_DV
GLA_RANK = 16
GLA_TAU = 16.0
SWA_HD = 64
SWA_QH = 8
SWA_KVH = 2
SWA_W = SWA_QH * SWA_HD
WINDOW = 128
ROPE_THETA = 10000.0
EPS = 1e-6
NEG_INF = -1e30
IN_SPLITS = (256, 256, 512, 512, 16, 512, 128, 128)
D_IN = sum(IN_SPLITS)
P_GQ, P_GK, P_GV, P_GG, P_GA, P_SQ, P_SK, P_SV, P_END = 0, 256, 512, 1024, 1536, 1664, 2176, 2432, 2688
ADAM_LR, ADAM_B1, ADAM_B2, ADAM_EPS, ADAM_WD, ADAM_STEP = 0.001, 0.9, 0.999, 1e-08, 0.01, 10
VMEM_LIMIT = 56 * 1024 * 1024

NT = (((1,), (1,)), ((), ()))
TN = (((0,), (0,)), ((), ()))


def _cparams(n_axes):
    return pltpu.CompilerParams(dimension_semantics=("arbitrary",) * n_axes, vmem_limit_bytes=VMEM_LIMIT)


def _row_tile(t):
    for tm in (640, 512, 384, 256, 128):
        if t % tm == 0:
            return tm
    raise ValueError(t)


SEQ_BLOCKS_PER_STEP = 5


def _seq_tile(t):
    return SEQ_BLOCKS_PER_STEP * BLK if t % (SEQ_BLOCKS_PER_STEP * BLK) == 0 else BLK


ROW_PARTS = 2


def _row_parts(tm):
    n = ROW_PARTS if tm % (16 * ROW_PARTS) == 0 else 1
    return [slice(k * (tm // n), (k + 1) * (tm // n)) for k in range(n)]


def _contract_tile(t):
    return 1664 if t % 1664 == 0 else _row_tile(t)


def _div_tile(r, cap=512):
    best = None
    for tr in range(8, min(r, cap) + 1, 8):
        if r % tr == 0:
            best = tr
    return best if best is not None else r


def _dot(a, b):
    return jnp.dot(a, b, preferred_element_type=F32)


def _dg(a, b, dims):
    return lax.dot_general(a, b, dims, preferred_element_type=F32)


def _rms(x, w):
    r = lax.rsqrt(jnp.mean(x * x, axis=-1, keepdims=True) + EPS)
    xh = x * r
    return xh * w, xh, r


def _rms_bwd(xh, r, w, dy):
    wdy = dy * w
    dx = r * (wdy - xh * jnp.mean(wdy * xh, axis=-1, keepdims=True))
    dw = jnp.sum(dy * xh, axis=0, keepdims=True)
    return dx, dw


def _sigmoid(x):
    return 1.0 / (1.0 + jnp.exp(-x))


def _full(shape):
    nd = len(shape)
    return pl.BlockSpec(shape, lambda *_: (0,) * nd)


ANY = pl.BlockSpec(memory_space=pl.ANY)


def _pallas(body, *, name, grid, in_specs, out_specs, out_shape, args, scratch_shapes=(), hook=None):
    n_axes = len(grid)
    if hook is None:
        return pl.pallas_call(body, name=name, grid=grid, in_specs=list(in_specs), out_specs=list(out_specs),
                              out_shape=list(out_shape), scratch_shapes=list(scratch_shapes),
                              compiler_params=_cparams(n_axes))(*args)
    n_in, n_out, n_scr = len(in_specs), len(out_specs), len(scratch_shapes)
    h_in, h_out = len(hook.inputs), len(hook.out_shape)
    total = math.prod(grid)

    def wrapped(*refs):
        ins, hins = refs[:n_in], refs[n_in:n_in + h_in]
        o0 = n_in + h_in
        outs, houts = refs[o0:o0 + n_out], refs[o0 + n_out:o0 + n_out + h_out]
        s0 = o0 + n_out + h_out
        scr, hscr = refs[s0:s0 + n_scr], refs[s0 + n_scr:]
        step = pl.program_id(0)
        for a in range(1, n_axes):
            step = step * grid[a] + pl.program_id(a)

        @pl.when(step == 0)
        def _():
            hook.start(hins, houts, hscr)

        body(*ins, *outs, *scr)

        if hook.has_mid:
            @pl.when(step == (3 * total) // 4)
            def _():
                hook.mid(hins, houts, hscr)

        @pl.when(step == total - 1)
        def _():
            hook.finish(hins, houts, hscr)

    res = pl.pallas_call(
        wrapped, name=name, grid=grid, in_specs=list(in_specs) + [ANY] * h_in,
        out_specs=list(out_specs) + [ANY] * h_out, out_shape=list(out_shape) + list(hook.out_shape),
        scratch_shapes=list(scratch_shapes) + list(hook.scratch), compiler_params=_cparams(n_axes),
        input_output_aliases={n_in + a: n_out + b for a, b in hook.aliases},
    )(*args, *hook.inputs)
    return res[:n_out], res[n_out:]


def _ffn_weight_specs(w3):
    fj = w3.shape[1] // 3
    return fj, [pl.BlockSpec((None, fj, D_MODEL), functools.partial(lambda i, j, k: (j, k, 0), k=k)) for k in range(3)]


def _ffn_fwd(h, wpre, w3, wpost, hook=None, target=None):
    t = h.shape[0]
    tm = _row_tile(t)
    nj, rows3, _ = w3.shape
    fj = rows3 // 3
    nblk = tm // BLK if target is not None else 0

    def body(*refs):
        h_ref, wpre_ref, w_hbm, wpost_ref = refs[:4]
        t_refs = refs[4:4 + nblk]
        hout_ref, n_ref, p1_ref, p2_ref, a_ref, f_ref = refs[4 + nblk:10 + nblk]
        acc_ref, wv, wsem = refs[-3:]
        i = pl.program_id(0)
        j = pl.program_id(1)

        @pl.when((i == 0) & (j == 0))
        def _():
            for k in range(nj):
                pltpu.make_async_copy(w_hbm.at[k], wv.at[k], wsem.at[k]).start()

        @pl.when(i == 0)
        def _():
            pltpu.make_async_copy(w_hbm.at[j], wv.at[j], wsem.at[j]).wait()

        @pl.when(j == 0)
        def _():
            y, _, _ = _rms(h_ref[...], wpre_ref[...])
            n_ref[...] = y.astype(BF16)
            acc_ref[...] = jnp.zeros_like(acc_ref)

        if target is not None:
            dwpost_ref, sse_ref = refs[10 + nblk:12 + nblk]

            @pl.when((i == 0) & (j == 0))
            def _():
                dwpost_ref[...] = jnp.zeros_like(dwpost_ref)
                sse_ref[...] = jnp.zeros_like(sse_ref)

        n = n_ref[...]
        g = _dg(n, wv[j, 0:fj], NT)
        u = _dg(n, wv[j, fj:2 * fj], NT)
        sg = _sigmoid(g)
        silu = g * sg
        p1_ref[...] = (u * (sg + silu * (1.0 - sg))).astype(BF16)
        p2_ref[...] = silu.astype(BF16)
        a = (silu * u).astype(BF16)
        a_ref[...] = a
        acc_ref[...] += _dot(a, wv[j, 2 * fj:3 * fj])

        @pl.when(j == nj - 1)
        def _():
            f = acc_ref[...]
            wpost = wpost_ref[...]
            y, fh, r = _rms(f, wpost)
            hout = h_ref[...] + 0.5 * y
            if target is None:
                f_ref[...] = f
                hout_ref[...] = hout
            else:
                sse = jnp.zeros((1, 1), F32)
                errs = []
                for k in range(nblk):
                    err = hout[k * BLK:(k + 1) * BLK] - t_refs[k][...]
                    if k == 0:
                        err = jnp.where(i > 0, err, 0.0)
                    errs.append(err)
                    sse = sse + jnp.sum(jnp.sum(err * err, axis=1, keepdims=True), axis=0, keepdims=True)
                dy = (jnp.concatenate(errs, axis=0) if nblk > 1 else errs[0]) * (1.0 / D_MODEL)
                hout_ref[...] = dy
                df, dw = _rms_bwd(fh, r, wpost, 0.5 * dy)
                f_ref[...] = df.astype(BF16)
                dwpost_ref[...] += dw
                sse_ref[...] += jnp.broadcast_to(sse, sse_ref.shape)

    row = pl.BlockSpec((tm, D_MODEL), lambda i, j: (i, 0))
    vec = pl.BlockSpec((1, D_MODEL), lambda i, j: (0, 0))
    act = pl.BlockSpec((None, tm, fj), lambda i, j: (j, i, 0))
    t_specs = [pl.BlockSpec((BLK, D_MODEL), functools.partial(lambda i, j, k: (jnp.maximum(nblk * i + k - 1, 0), 0), k=k))
               for k in range(nblk)]
    loss_spec = [vec, _full((1, 128))] if target is not None else []
    loss_shape = [jax.ShapeDtypeStruct((1, D_MODEL), F32), jax.ShapeDtypeStruct((1, 128), F32)] if (
        target is not None) else []
    return _pallas(
        body, name="ffn_fwd", grid=(t // tm, nj),
        in_specs=[row, vec, ANY, vec] + t_specs,
        out_specs=[row, row, act, act, act, row] + loss_spec,
        out_shape=[jax.ShapeDtypeStruct((t, D_MODEL), F32), jax.ShapeDtypeStruct((t, D_MODEL), BF16),
                   jax.ShapeDtypeStruct((nj, t, fj), BF16), jax.ShapeDtypeStruct((nj, t, fj), BF16),
                   jax.ShapeDtypeStruct((nj, t, fj), BF16),
                   jax.ShapeDtypeStruct((t, D_MODEL), F32 if target is None else BF16)] + loss_shape,
        scratch_shapes=[pltpu.VMEM((tm, D_MODEL), F32), pltpu.VMEM((nj, rows3, D_MODEL), BF16),
                        pltpu.SemaphoreType.DMA((nj,))],
        args=(h, wpre, w3, wpost) + (target,) * nblk, hook=hook)


def _ffn_bwd(dhout, h, f, p14, p24, wpre, w3, wpost, df=None, split_first_block=False):
    t = h.shape[0]
    tm = _row_tile(t)
    ni = t // tm
    nj = w3.shape[0]
    fj, wspecs = _ffn_weight_specs(w3)
    have_df = df is not None
    assert not (have_df and split_first_block)

    def body(dhout_ref, h_ref, f_ref, p1_ref, p2_ref, wpre_ref, wg_ref, wu_ref, wd_ref, wpost_ref, *rest):
        if have_df:
            dh_ref, dg_ref, du_ref, dwpre_ref, dn_ref = rest
            df_ref = f_ref
        elif split_first_block:
            dh_ref, rest_hbm, df_ref, dg_ref, du_ref, dwpre_ref, dwpost_ref, dn_ref, dh_buf, dh_sem, first_sem = rest
        else:
            dh_ref, df_ref, dg_ref, du_ref, dwpre_ref, dwpost_ref, dn_ref = rest
        i = pl.program_id(0)
        j = pl.program_id(1)

        @pl.when((i == 0) & (j == 0))
        def _():
            dwpre_ref[...] = jnp.zeros_like(dwpre_ref)
            if not have_df:
                dwpost_ref[...] = jnp.zeros_like(dwpost_ref)

        @pl.when(j == 0)
        def _():
            if not have_df:
                wpost = wpost_ref[...]
                _, fh, r = _rms(f_ref[...], wpost)
                dfv, dw = _rms_bwd(fh, r, wpost, 0.5 * dhout_ref[...])
                dwpost_ref[...] += dw
                df_ref[...] = dfv.astype(BF16)
            dn_ref[...] = jnp.zeros_like(dn_ref)

        parts = _row_parts(tm)
        das = [_dg(df_ref[rows, :], wd_ref[...], NT) for rows in parts]
        for rows, da in zip(parts, das):
            dg = (da * p1_ref[rows, :].astype(F32)).astype(BF16)
            du = (da * p2_ref[rows, :].astype(F32)).astype(BF16)
            dg_ref[rows, :] = dg
            du_ref[rows, :] = du
            dn_ref[rows, :] += _dot(dg, wg_ref[...]) + _dot(du, wu_ref[...])

        @pl.when(j == nj - 1)
        def _():
            wpre = wpre_ref[...]
            _, hh, r = _rms(h_ref[...], wpre)
            dx, dw = _rms_bwd(hh, r, wpre, dn_ref[...])
            dwpre_ref[...] += dw
            dh = dhout_ref[...] + dx
            if not split_first_block:
                dh_ref[...] = dh
            else:
                slot = i % 2

                def to_rest(tile, sl):
                    rows = pl.ds(pl.multiple_of(tile * tm - BLK, 8), tm)
                    return pltpu.make_async_copy(dh_buf.at[sl], rest_hbm.at[rows], dh_sem.at[sl])

                first = pltpu.make_async_copy(dh_buf.at[0, BLK:tm], rest_hbm.at[0:tm - BLK], first_sem)

                @pl.when(i == 2)
                def _():
                    first.wait()

                @pl.when(i >= 3)
                def _():
                    to_rest(i, slot).wait()

                dh_buf[slot] = dh

                @pl.when(i == 0)
                def _():
                    dh_ref[...] = dh[0:BLK]
                    first.start()

                @pl.when(i > 0)
                def _():
                    to_rest(i, slot).start()

                @pl.when(i == ni - 1)
                def _():
                    if ni < 3:
                        first.wait()
                    if ni >= 3:
                        to_rest(i, 1 - slot).wait()
                    if ni >= 2:
                        to_rest(i, slot).wait()

    row = pl.BlockSpec((tm, D_MODEL), lambda i, j: (i, 0))
    vec = pl.BlockSpec((1, D_MODEL), lambda i, j: (0, 0))
    act = pl.BlockSpec((None, tm, fj), lambda i, j: (j, i, 0))
    actshape = jax.ShapeDtypeStruct((nj, t, fj), BF16)
    rowf, rowb, vecf = (jax.ShapeDtypeStruct((t, D_MODEL), F32), jax.ShapeDtypeStruct((t, D_MODEL), BF16),
                        jax.ShapeDtypeStruct((1, D_MODEL), F32))
    if have_df:
        out_specs, out_shape = [row, act, act, vec], [rowf, actshape, actshape, vecf]
    else:
        out_specs, out_shape = [row, row, act, act, vec, vec], [rowf, rowb, actshape, actshape, vecf, vecf]
    scratch = [pltpu.VMEM((tm, D_MODEL), F32)]
    if split_first_block:
        out_specs = [pl.BlockSpec((BLK, D_MODEL), lambda i, j: (0, 0)), ANY] + out_specs[1:]
        out_shape = [jax.ShapeDtypeStruct((BLK, D_MODEL), F32), jax.ShapeDtypeStruct((t - BLK, D_MODEL), F32)
                     ] + out_shape[1:]
        scratch += [pltpu.VMEM((2, tm, D_MODEL), F32), pltpu.SemaphoreType.DMA((2,)), pltpu.SemaphoreType.DMA]
    return _pallas(
        body, name="ffn_bwd", grid=(ni, nj),
        in_specs=[row, row, row, act, act, vec] + wspecs + [vec],
        out_specs=out_specs, out_shape=out_shape, scratch_shapes=scratch,
        args=(dhout, h, df if have_df else f, p14, p24, wpre, w3, w3, w3, wpost))


def _ffn_wgrad(n, df, dg4, du4, a4, hook=None):
    t = n.shape[0]
    tm = _contract_tile(t)
    ni = t // tm
    nj, _, fj = dg4.shape

    def body(n_ref, df_ref, dg_ref, du_ref, a_ref, dw_ref, acc):
        i = pl.program_id(1)

        @pl.when(i == 0)
        def _():
            acc[...] = jnp.zeros_like(acc)

        nn = n_ref[...]
        acc[0:fj, :] += _dg(dg_ref[...], nn, TN)
        acc[fj:2 * fj, :] += _dg(du_ref[...], nn, TN)
        acc[2 * fj:3 * fj, :] += _dg(a_ref[...], df_ref[...], TN)

        @pl.when(i == ni - 1)
        def _():
            dw_ref[...] = acc[...].astype(BF16)

    row = pl.BlockSpec((tm, D_MODEL), lambda j, i: (i, 0))
    act = pl.BlockSpec((None, tm, fj), lambda j, i: (j, i, 0))
    return _pallas(
        body, name="ffn_wgrad", grid=(nj, ni),
        in_specs=[row, row, act, act, act],
        out_specs=[pl.BlockSpec((None, 3 * fj, D_MODEL), lambda j, i: (j, 0, 0))],
        out_shape=[jax.ShapeDtypeStruct((nj, 3 * fj, D_MODEL), BF16)],
        scratch_shapes=[pltpu.VMEM((3 * fj, D_MODEL), F32)],
        args=(n, df, dg4, du4, a4), hook=hook)


def _embed_norm(x, meta_buf, w):
    t = x.shape[0] + BLK
    tm = _row_tile(t)
    nblk = tm // BLK
    ni = t // tm
    gather = _GatherChips([meta_buf])

    def body(*refs):
        x_refs = refs[:nblk]
        w_ref, mb_in, h_ref, n_ref, mb_out, mv, msem, send, recv = refs[nblk:]
        step = pl.program_id(0)
        tile = (step + 1) % ni
        hook_refs = ([mb_in], [mb_out], [send, recv])
        steps = (0, 1, ni - 2) if ni >= 3 else (0, 0, 0)
        for at, phase in zip(steps, (gather.start, gather.mid, gather.finish)):
            @pl.when(step == at)
            def _(phase=phase):
                phase(*hook_refs)

        @pl.when(step == 0)
        def _():
            mv[...] = jnp.zeros_like(mv)

        @pl.when(step == ni - 1)
        def _():
            cp = pltpu.make_async_copy(mb_out, mv, msem)
            cp.start()
            cp.wait()

        meta = jnp.concatenate([mv[k] for k in range(N_CHIPS)], axis=1)
        first = jnp.concatenate([jnp.zeros((PAD, D_MODEL), F32), meta], axis=0)
        blocks = [jnp.where(tile == 0, first, x_refs[0][...])] + [r[...] for r in x_refs[1:]]
        h = jnp.concatenate(blocks, axis=0) if nblk > 1 else blocks[0]
        h_ref[...] = h
        y, _, _ = _rms(h, w_ref[...])
        n_ref[...] = y.astype(BF16)

    x_specs = [pl.BlockSpec((BLK, D_MODEL), functools.partial(
        lambda i, k: (jnp.maximum(nblk * ((i + 1) % ni) + k - 1, 0), 0), k=k)) for k in range(nblk)]
    row = pl.BlockSpec((tm, D_MODEL), lambda i: ((i + 1) % ni, 0))
    h0, n0, _ = pl.pallas_call(
        body, name="embed_norm", grid=(ni,),
        in_specs=x_specs + [_full((1, D_MODEL)), ANY], out_specs=[row, row, ANY],
        out_shape=[jax.ShapeDtypeStruct((t, D_MODEL), F32), jax.ShapeDtypeStruct((t, D_MODEL), BF16),
                   jax.ShapeDtypeStruct(meta_buf.shape, meta_buf.dtype)],
        scratch_shapes=[pltpu.VMEM(meta_buf.shape, meta_buf.dtype), pltpu.SemaphoreType.DMA] + list(gather.scratch),
        input_output_aliases={nblk + 1: 2},
        compiler_params=_cparams(1),
    )(*([x] * nblk), w, meta_buf)
    return h0, n0


FWD_RELATION = (None, 0, 1, 2)


def _ffn_fwd_gather(h, n, wbuf, wpost, qc_idx, late):
    t = h.shape[0]
    tm = _row_tile(t)
    ni = t // tm
    nj, rows3, _ = wbuf.shape
    fj = rows3 // 3
    assert nj == N_CHIPS and ni >= 4
    wbufs = [wbuf]
    nw = 1
    n_lin, n_lout = len(late.inputs), len(late.out_shape)
    wait_step = ni - 3

    def body(qc_ref, h_ref, n_ref, wpost_ref, *rest):
        wb_in = rest[:nw]
        lins = rest[nw:nw + n_lin]
        o0 = nw + n_lin
        hout_ref, p1_ref, p2_ref, a_ref, f_hbm = rest[o0:o0 + 5]
        wb = rest[o0 + 5:o0 + 5 + nw]
        louts = rest[o0 + 5 + nw:o0 + 5 + nw + n_lout]
        s0 = o0 + 5 + nw + n_lout
        wv, wsem, send, recv, fbuf, fr_sem, fw_sem = rest[s0:s0 + 7]
        lscr = rest[s0 + 7:]
        p = pl.program_id(0)
        i = pl.program_id(1)
        step = p * ni + i
        fslot = step % 3
        nslot = (step + 1) % 3

        def f_tile(tile):
            return f_hbm.at[pl.ds(pl.multiple_of(tile * tm, 8), tm)]

        @pl.when(step > 1)
        def _():
            pltpu.make_async_copy(fbuf.at[nslot], f_tile(i), fw_sem.at[nslot]).wait()

        nxt = step + 1

        @pl.when((nxt < N_CHIPS * ni) & (nxt >= ni))
        def _():
            pltpu.make_async_copy(f_tile(nxt % ni), fbuf.at[nslot], fr_sem.at[nslot]).start(priority=1)

        @pl.when(p > 0)
        def _():
            pltpu.make_async_copy(f_tile(i), fbuf.at[fslot], fr_sem.at[fslot]).wait()
        x, y, c, chips = _place()
        q = 2 * x + y
        sibling = (x, y, 1 - c)
        mine, other = _half(rows3, c), _half(rows3, 1 - c)

        def load(chunk, slot, src):
            return [pltpu.make_async_copy(src[t].at[chunk], wv.at[slot, t], wsem.at[slot, t]) for t in range(nw)]

        @pl.when((p == 0) & (i == 0))
        def _():
            for j, (cx, cy) in enumerate(chips):
                for t in range(nw):
                    _remote(send.at[t, j], recv.at[t, j], wb_in[t].at[q, mine], wb[t].at[q, mine], (cx, cy, c)).start()
            for cp in load(q, 0, wb_in):
                cp.start()
            for cp in load(q, 0, wb_in):
                cp.wait()

        @pl.when((p == 1) & (i == 0))
        def _():
            late.start(lins, louts, lscr)

        for pp in range(1, N_CHIPS):
            j = FWD_RELATION[pp]
            cx, cy = chips[j]
            chunk = 2 * cx + cy

            @pl.when((p == pp - 1) & (i == wait_step))
            def _(j=j, cx=cx, cy=cy, chunk=chunk, pp=pp):
                for t in range(nw):
                    got = wb[t].at[chunk, mine]
                    _remote(send.at[t, j], recv.at[t, j], got, got, (cx, cy, c)).wait_recv()
                    _remote(send.at[t, 3 + j], recv.at[t, 3 + j], got, got, sibling).start()
                for t in range(nw):
                    rest_half = wb[t].at[chunk, other]
                    _remote(send.at[t, 3 + j], recv.at[t, 3 + j], rest_half, rest_half, sibling).wait_recv()
                for cp in load(chunk, pp % 2, wb):
                    cp.start()

            @pl.when((p == pp) & (i == 0))
            def _(chunk=chunk, pp=pp):
                for cp in load(chunk, pp % 2, wb):
                    cp.wait()

        @pl.when((p == N_CHIPS - 1) & (i == ni // 2))
        def _():
            late.mid(lins, louts, lscr)

        slot = p % 2
        nn = n_ref[...]
        g = _dg(nn, wv[slot, 0, 0:fj], NT)
        u = _dg(nn, wv[slot, 0, fj:2 * fj], NT)
        sg = _sigmoid(g)
        silu = g * sg
        p1_ref[...] = (u * (sg + silu * (1.0 - sg))).astype(BF16)
        p2_ref[...] = silu.astype(BF16)
        a = (silu * u).astype(BF16)
        a_ref[...] = a
        part = _dot(a, wv[slot, 0, 2 * fj:3 * fj])

        @pl.when(p == 0)
        def _():
            fbuf[fslot] = part

        @pl.when(p > 0)
        def _():
            fbuf[fslot] = fbuf[fslot] + part

        pltpu.make_async_copy(fbuf.at[fslot], f_tile(i), fw_sem.at[fslot]).start(priority=1)

        @pl.when(p == N_CHIPS - 1)
        def _():
            yv, _, _ = _rms(fbuf[fslot], wpost_ref[...])
            hout_ref[...] = h_ref[...] + 0.5 * yv

        @pl.when((p == N_CHIPS - 1) & (i == ni - 1))
        def _():
            pslot = (step + 2) % 3
            pltpu.make_async_copy(fbuf.at[pslot], f_tile(i), fw_sem.at[pslot]).wait()
            pltpu.make_async_copy(fbuf.at[fslot], f_tile(i), fw_sem.at[fslot]).wait()
            for t in range(nw):
                for j, (cx, cy) in enumerate(chips):
                    sent = wb[t].at[2 * cx + cy, mine]
                    _remote(send.at[t, j], recv.at[t, j], sent, sent, (cx, cy, c)).wait_send()
                    _remote(send.at[t, 3 + j], recv.at[t, 3 + j], sent, sent, sibling).wait_send()
            late.finish(lins, louts, lscr)

    def last_pass_rows(p, i, qc_ref):
        return (jnp.where(p == N_CHIPS - 1, i, 0), 0)

    def chunk_rows(p, i, qc_ref):
        order = ((p & 1) << 1) | (p >> 1)
        return (jnp.bitwise_xor(qc_ref[0], order), i, 0)

    row = pl.BlockSpec((tm, D_MODEL), lambda p, i, qc_ref: (i, 0))
    last_row = pl.BlockSpec((tm, D_MODEL), last_pass_rows)
    act = pl.BlockSpec((None, tm, fj), chunk_rows)
    act_shape = jax.ShapeDtypeStruct((nj, t, fj), BF16)
    res = pl.pallas_call(
        body, name="ffn_fwd_gather",
        grid_spec=pltpu.PrefetchScalarGridSpec(
            num_scalar_prefetch=1, grid=(N_CHIPS, ni),
            in_specs=[last_row, row, pl.BlockSpec((1, D_MODEL), lambda p, i, qc_ref: (0, 0))]
            + [ANY] * (nw + n_lin),
            out_specs=[last_row, act, act, act, ANY] + [ANY] * (nw + n_lout),
            scratch_shapes=[pltpu.VMEM((2, nw, rows3, D_MODEL), BF16), pltpu.SemaphoreType.DMA((2, nw)),
                            pltpu.SemaphoreType.DMA((nw, 6)), pltpu.SemaphoreType.DMA((nw, 6)),
                            pltpu.VMEM((3, tm, D_MODEL), F32), pltpu.SemaphoreType.DMA((3,)),
                            pltpu.SemaphoreType.DMA((3,))] + list(late.scratch)),
        out_shape=[jax.ShapeDtypeStruct((t, D_MODEL), F32), act_shape, act_shape, act_shape,
                   jax.ShapeDtypeStruct((t, D_MODEL), F32)]
        + [jax.ShapeDtypeStruct(b.shape, b.dtype) for b in wbufs] + list(late.out_shape),
        input_output_aliases={**{4 + t: 5 + t for t in range(nw)},
                              **{4 + nw + a: 5 + nw + b for a, b in late.aliases}},
        compiler_params=_cparams(2),
    )(qc_idx, h, n, wpost, *wbufs, *late.inputs)
    return res[:5], res[5:5 + nw], res[5 + nw:]


PASS_RELATION = (2, 0, 1)


def _ffn_wgrad_reduce(n, df, dg4, du4, a4, qc_idx, hook):
    t = n.shape[0]
    tm = _contract_tile(t)
    ni = t // tm
    nj, _, fj = dg4.shape
    assert nj == N_CHIPS
    hrows = 3 * fj // 2
    n_hin, n_hout = len(hook.inputs), len(hook.out_shape)

    def body(qc_ref, n_ref, df_ref, dg_ref, du_ref, a_ref, *rest):
        hins = rest[:n_hin]
        own_ref, others_ref = rest[n_hin:n_hin + 2]
        houts = rest[n_hin + 2:n_hin + 2 + n_hout]
        s0 = n_hin + 2 + n_hout
        acc, stage, land, sumbuf, px_send, px_recv, cs_send, cs_recv, own_sem = rest[s0:s0 + 9]
        hscr = rest[s0 + 9:]
        k_pass = pl.program_id(0)
        i = pl.program_id(1)
        x, y, c, chips = _place()
        mine = pl.ds(pl.multiple_of(c * hrows, 8), hrows)
        other = pl.ds(pl.multiple_of((1 - c) * hrows, 8), hrows)

        def to_owner(k):
            j = PASS_RELATION[k]
            return _remote(cs_send.at[j], cs_recv.at[j], sumbuf.at[k % 2], others_ref.at[j], (*chips[j], c))

        @pl.when((k_pass == 0) & (i == 0))
        def _():
            hook.start(hins, houts, hscr)

        if hook.has_mid:
            @pl.when((k_pass == N_CHIPS // 2) & (i == 0))
            def _():
                hook.mid(hins, houts, hscr)

        @pl.when(i == 0)
        def _():
            acc[...] = jnp.zeros_like(acc)

        nn = n_ref[...]
        acc[0:fj, :] += _dg(dg_ref[...], nn, TN)
        acc[fj:2 * fj, :] += _dg(du_ref[...], nn, TN)
        acc[2 * fj:3 * fj, :] += _dg(a_ref[...], df_ref[...], TN)

        for k in range(N_CHIPS):
            @pl.when((k_pass == k) & (i == ni - 1))
            def _(k=k):
                slot = k % 2
                stage[...] = acc[other, :].astype(BF16)
                swap = _remote(px_send.at[k], px_recv.at[k], stage, land.at[slot], (x, y, 1 - c))
                swap.start()
                swap.wait_recv()
                pair = acc[mine, :] + land[slot].astype(F32)
                if k >= 2:
                    to_owner(k - 2).wait_send()
                sumbuf[slot] = pair.astype(BF16)
                swap.wait_send()
                if k < N_CHIPS - 1:
                    to_owner(k).start()
                else:
                    keep = pltpu.make_async_copy(sumbuf.at[slot], own_ref, own_sem)
                    keep.start()
                    for j in range(N_CHIPS - 1):
                        _remote(cs_send.at[j], cs_recv.at[j], sumbuf.at[0], others_ref.at[j], (*chips[j], c)).wait_recv()
                    to_owner(k - 1).wait_send()
                    keep.wait()
                    hook.finish(hins, houts, hscr)

    def chunk(k_pass, i, qc_ref):
        return (jnp.bitwise_xor(qc_ref[0], N_CHIPS - 1 - k_pass), i, 0)

    row = pl.BlockSpec((tm, D_MODEL), lambda k_pass, i, qc_ref: (i, 0))
    act = pl.BlockSpec((None, tm, fj), chunk)
    res = pl.pallas_call(
        body, name="ffn_wgrad_reduce",
        grid_spec=pltpu.PrefetchScalarGridSpec(
            num_scalar_prefetch=1, grid=(N_CHIPS, ni),
            in_specs=[row, row, act, act, act] + [ANY] * n_hin,
            out_specs=[ANY, ANY] + [ANY] * n_hout,
            scratch_shapes=[pltpu.VMEM((3 * fj, D_MODEL), F32), pltpu.VMEM((hrows, D_MODEL), BF16),
                            pltpu.VMEM((2, hrows, D_MODEL), BF16), pltpu.VMEM((2, hrows, D_MODEL), BF16),
                            pltpu.SemaphoreType.DMA((N_CHIPS,)), pltpu.SemaphoreType.DMA((N_CHIPS,)),
                            pltpu.SemaphoreType.DMA((N_CHIPS - 1,)), pltpu.SemaphoreType.DMA((N_CHIPS - 1,)),
                            pltpu.SemaphoreType.DMA] + list(hook.scratch)),
        out_shape=[jax.ShapeDtypeStruct((hrows, D_MODEL), BF16),
                   jax.ShapeDtypeStruct((N_CHIPS - 1, hrows, D_MODEL), BF16)] + list(hook.out_shape),
        compiler_params=_cparams(2),
    )(qc_idx, n, df, dg4, du4, a4, *hook.inputs)
    return res[0], res[1], res[2:]


def _xty(x, y):
    t, k = x.shape
    n = y.shape[1]
    tm = _contract_tile(t)
    tn = n if n <= 1024 else (896 if n % 896 == 0 else 128)
    steps = t // tm

    def body(x_ref, y_ref, o_ref, acc_ref):
        i = pl.program_id(1)
        part = _dg(x_ref[...], y_ref[...], TN)

        @pl.when(i == 0)
        def _():
            acc_ref[...] = part

        @pl.when(jnp.logical_and(i > 0, i < steps - 1))
        def _():
            acc_ref[...] += part

        @pl.when(i == steps - 1)
        def _():
            o_ref[...] = (acc_ref[...] + part).astype(BF16)

    assert steps > 1
    return pl.pallas_call(
        body, name="xty", grid=(n // tn, steps),
        in_specs=[pl.BlockSpec((tm, k), lambda j, i: (i, 0)), pl.BlockSpec((tm, tn), lambda j, i: (i, j))],
        out_specs=pl.BlockSpec((k, tn), lambda j, i: (0, j)),
        out_shape=jax.ShapeDtypeStruct((k, n), BF16),
        scratch_shapes=[pltpu.VMEM((k, tn), F32)],
        compiler_params=_cparams(2),
    )(x, y)


def _rope_tables(t):
    pos = (jnp.arange(t, dtype=jnp.int32) - PAD).astype(F32)
    inv_freq = 1.0 / (ROPE_THETA ** (jnp.arange(0, SWA_HD, 2, dtype=F32) / SWA_HD))
    half = SWA_HD // 2
    ang = pos[:, None] * jnp.tile(inv_freq, 4)[None, :]
    sign = jnp.tile(jnp.concatenate([-jnp.ones((half,), F32), jnp.ones((half,), F32)]), 2)
    return jnp.cos(ang), jnp.sin(ang) * sign[None, :]


def _rot_half(x, first_half):
    return jnp.where(first_half, pltpu.roll(x, 96, 1), pltpu.roll(x, 32, 1))


def _first_half_mask(rows):
    lane = lax.broadcasted_iota(jnp.int32, (rows, 128), 1)
    return (lane % 64) < 32


def _log_sigmoid(z):
    return jnp.minimum(z, 0.0) - jnp.log(1.0 + jnp.exp(-jnp.abs(z)))


def _mix_proj(h1, wmixpre, winp, wa2p, bap, cos, sin):
    t = h1.shape[0]
    tm = _row_tile(t)

    def body(h_ref, w_ref, win_ref, wa2_ref, ba_ref, cos_ref, sin_ref,
             n_ref, gq_ref, gk_ref, gv_ref, gg_ref, ga_ref, la_ref, sq_ref, sk_ref, sv_ref):
        y, _, _ = _rms(h_ref[...], w_ref[...])
        n = y.astype(BF16)
        n_ref[...] = n
        proj = _dot(n, win_ref[...])
        gq_ref[...] = proj[:, P_GQ:P_GK]
        gk_ref[...] = proj[:, P_GK:P_GV]
        gv_ref[...] = proj[:, P_GV:P_GG]
        gg_ref[...] = proj[:, P_GG:P_GA]
        ga = proj[:, P_GA:P_SQ]
        ga_ref[...] = ga
        z = _dot(ga.astype(BF16), wa2_ref[...]) + ba_ref[...]
        la_ref[...] = _log_sigmoid(z) * (1.0 / GLA_TAU)
        c = cos_ref[...]
        s = sin_ref[...]
        fh = _first_half_mask(tm)
        for k in range(4):
            x = proj[:, P_SQ + 128 * k:P_SQ + 128 * (k + 1)]
            sq_ref[:, 128 * k:128 * (k + 1)] = (x * c + _rot_half(x, fh) * s).astype(BF16)
        for k in range(2):
            x = proj[:, P_SK + 128 * k:P_SK + 128 * (k + 1)]
            sk_ref[:, 128 * k:128 * (k + 1)] = (x * c + _rot_half(x, fh) * s).astype(BF16)
        sv_ref[...] = proj[:, P_SV:P_END].astype(BF16)

    def row(w):
        return pl.BlockSpec((tm, w), lambda i: (i, 0))

    def rshape(w, dt):
        return jax.ShapeDtypeStruct((t, w), dt)

    return pl.pallas_call(
        body, name="mix_proj", grid=(t // tm,),
        in_specs=[row(D_MODEL), _full((1, D_MODEL)), _full((D_MODEL, P_END)), _full((128, GLA_KW)),
                  _full((1, GLA_KW)), row(128), row(128)],
        out_specs=[row(D_MODEL), row(256), row(256), row(512), row(512), row(128), row(256), row(512), row(256),
                   row(256)],
        out_shape=[rshape(D_MODEL, BF16), rshape(256, F32), rshape(256, F32), rshape(512, F32), rshape(512, F32),
                   rshape(128, F32), rshape(256, F32), rshape(512, BF16), rshape(256, BF16), rshape(256, BF16)],
        compiler_params=_cparams(1),
    )(h1, wmixpre, winp, wa2p, bap, cos, sin)


def _scan_rows(x, reverse=False):
    n = x.shape[0]
    row = lax.broadcasted_iota(jnp.int32, x.shape, 0)
    s = 1
    while s < n:
        if reverse:
            x = x + jnp.where(row < n - s, pltpu.roll(x, n - s, 0), 0.0)
        else:
            x = x + jnp.where(row >= s, pltpu.roll(x, s, 0), 0.0)
        s *= 2
    return x


def _gla_cumsum(la, tril_f):
    b = _scan_rows(la)
    row = lax.broadcasted_iota(jnp.int32, b.shape, 0)
    bm = jnp.sum(jnp.where(row == GLA_CHUNK // 2 - 1, b, 0.0), axis=0, keepdims=True)
    bl = jnp.sum(jnp.where(row == GLA_CHUNK - 1, b, 0.0), axis=0, keepdims=True)
    return b, bm, bl


def _gla_decays(la, tril_f):
    b, bm, bl = _gla_cumsum(la, tril_f)
    return jnp.exp(b - bm), jnp.exp(bm - b), jnp.exp(b), jnp.exp(bl - b), jnp.exp(bl)


def _gla_masks():
    c = GLA_CHUNK
    r = lax.broadcasted_iota(jnp.int32, (c, c), 0)
    col = lax.broadcasted_iota(jnp.int32, (c, c), 1)
    r4 = lax.broadcasted_iota(jnp.int32, (GLA_HEADS * c, c), 0) % c
    c4 = lax.broadcasted_iota(jnp.int32, (GLA_HEADS * c, c), 1)
    klane = lax.broadcasted_iota(jnp.int32, (c, GLA_KW), 1) // GLA_DK
    vlane = lax.broadcasted_iota(jnp.int32, (c, GLA_W), 1) // GLA_DV
    srow = lax.broadcasted_iota(jnp.int32, (GLA_W, GLA_KW), 0) // GLA_DV
    scol = lax.broadcasted_iota(jnp.int32, (GLA_W, GLA_KW), 1) // GLA_DK
    return dict(tril_f=(r >= col).astype(F32), triu_f=(r <= col).astype(F32), tril4=r4 >= c4,
                khead=[klane == h for h in range(GLA_HEADS)], vhead=[vlane == h for h in range(GLA_HEADS)],
                diag=srow == scol)


def _stack_heads(x, head_masks):
    return jnp.concatenate([jnp.where(m, x, 0.0) for m in head_masks], axis=0)


def _gla_fwd(gq, gk, gv, la):
    t = gq.shape[0]
    rg = _seq_tile(t)
    nb = t // rg
    ncb = rg // GLA_CHUNK
    c = GLA_CHUNK

    def body(q_ref, k_ref, v_ref, la_ref, o_ref, ss_ref, st_ref):
        @pl.when(pl.program_id(0) == 0)
        def _():
            st_ref[...] = jnp.zeros_like(st_ref)

        mk = _gla_masks()
        st = st_ref[...]
        for ch in range(ncb):
            rows = slice(ch * c, (ch + 1) * c)
            eq, ek, eb, ekl, ebl = _gla_decays(la_ref[rows, :], mk["tril_f"])
            qs = q_ref[rows, :] * (GLA_DK ** -0.5)
            k = k_ref[rows, :]
            v = v_ref[rows, :].astype(BF16)
            ss_ref[ch] = st
            q4 = _stack_heads(qs * eq, mk["khead"]).astype(BF16)
            a4 = jnp.where(mk["tril4"], _dg(q4, (k * ek).astype(BF16), NT), 0.0).astype(BF16)
            r4 = _dot(a4, v)
            intra = jnp.concatenate([r4[h * c:(h + 1) * c, GLA_DV * h:GLA_DV * (h + 1)] for h in range(GLA_HEADS)],
                                    axis=1)
            o_ref[rows, :] = intra + _dg((qs * eb).astype(BF16), st.astype(BF16), NT)
            st = st * ebl + jnp.where(mk["diag"], _dg(v, (k * ekl).astype(BF16), TN), 0.0)
        st_ref[...] = st

    def row(w):
        return pl.BlockSpec((rg, w), lambda i: (i, 0))

    return pl.pallas_call(
        body, name="gla_fwd", grid=(nb,),
        in_specs=[row(256), row(256), row(512), row(256)],
        out_specs=[row(512), pl.BlockSpec((ncb, GLA_W, GLA_KW), lambda i: (i, 0, 0))],
        out_shape=[jax.ShapeDtypeStruct((t, GLA_W), F32), jax.ShapeDtypeStruct((nb * ncb, GLA_W, GLA_KW), F32)],
        scratch_shapes=[pltpu.VMEM((GLA_W, GLA_KW), F32)],
        compiler_params=_cparams(1),
    )(gq, gk, gv, la)


def _gla_bwd(gq, gk, gv, la, ss, do):
    t = gq.shape[0]
    rg = _seq_tile(t)
    nb = t // rg
    ncb = rg // GLA_CHUNK
    c = GLA_CHUNK

    def body(q_ref, k_ref, v_ref, la_ref, ss_ref, do_ref, dq_ref, dk_ref, dv_ref, dla_ref, dst_ref):
        @pl.when(pl.program_id(0) == 0)
        def _():
            dst_ref[...] = jnp.zeros_like(dst_ref)

        mk = _gla_masks()
        last_row = lax.broadcasted_iota(jnp.int32, (c, GLA_KW), 0) == c - 1
        scale = GLA_DK ** -0.5
        dstn = dst_ref[...]
        for ch in reversed(range(ncb)):
            rows = slice(ch * c, (ch + 1) * c)
            eq, ek, eb, ekl, ebl = _gla_decays(la_ref[rows, :], mk["tril_f"])
            qs = q_ref[rows, :] * scale
            k = k_ref[rows, :]
            qt, kt, qh, kh = qs * eq, k * ek, qs * eb, k * ekl
            ktb, khb, qhb = kt.astype(BF16), kh.astype(BF16), qh.astype(BF16)
            v = v_ref[rows, :].astype(BF16)
            do_f = do_ref[rows, :]
            dob = do_f.astype(BF16)
            st = ss_ref[ch]
            stb = st.astype(BF16)
            dstb = dstn.astype(BF16)
            q4 = _stack_heads(qt, mk["khead"]).astype(BF16)
            do4 = _stack_heads(do_f, mk["vhead"]).astype(BF16)
            a4 = jnp.where(mk["tril4"], _dg(q4, ktb, NT), 0.0).astype(BF16)
            da4 = jnp.where(mk["tril4"], _dg(do4, v, NT), 0.0).astype(BF16)
            dv_ref[rows, :] = _dg(a4, do4, TN) + _dg(khb, dstb, NT)
            dq4 = _dot(da4, ktb)
            dqt = jnp.zeros((c, GLA_KW), F32)
            for h in range(GLA_HEADS):
                dqt = dqt + jnp.where(mk["khead"][h], dq4[h * c:(h + 1) * c], 0.0)
            dkt = _dg(da4, q4, TN)
            dqh = _dot(dob, stb)
            dkh = _dot(v, dstb)
            dbl = jnp.sum(dstn * st, axis=0, keepdims=True)
            dstn = dstn * ebl + jnp.where(mk["diag"], _dg(dob, qhb, TN), 0.0)
            dq_ref[rows, :] = scale * (dqt * eq + dqh * eb)
            dk_ref[rows, :] = dkt * ek + dkh * ekl
            dkk = dkh * kh
            db = dqt * qt - dkt * kt + dqh * qh - dkk
            db = db + jnp.where(last_row, jnp.sum(dkk, axis=0, keepdims=True) + ebl * dbl, 0.0)
            dla_ref[rows, :] = _scan_rows(db, reverse=True)
        dst_ref[...] = dstn

    def row(w):
        return pl.BlockSpec((rg, w), lambda i: (nb - 1 - i, 0))

    def rshape(w):
        return jax.ShapeDtypeStruct((t, w), F32)

    return pl.pallas_call(
        body, name="gla_bwd", grid=(nb,),
        in_specs=[row(256), row(256), row(512), row(256),
                  pl.BlockSpec((ncb, GLA_W, GLA_KW), lambda i: (nb - 1 - i, 0, 0)), row(512)],
        out_specs=[row(256), row(256), row(512), row(256)],
        out_shape=[rshape(256), rshape(256), rshape(512), rshape(256)],
        scratch_shapes=[pltpu.VMEM((GLA_W, GLA_KW), F32)],
        compiler_params=_cparams(1),
    )(gq, gk, gv, la, ss, do)


SWA_G = SWA_QH // SWA_KVH


def _swa_bias():
    n = jnp.arange(3, dtype=jnp.int32)[:, None, None]
    r = (jnp.arange(SWA_G * BLK, dtype=jnp.int32) % BLK)[None, :, None]
    c = jnp.arange(3 * BLK, dtype=jnp.int32)[None, None, :]
    seg = c // BLK
    cc = c % BLK
    qpos = n * BLK + r - PAD
    kpos = jnp.where(seg == 0, (n - 1) * BLK, jnp.where(seg == 1, n * BLK, 0)) + cc - PAD
    band = (seg < 2) & (kpos >= N_META) & (kpos <= qpos) & (qpos - kpos < WINDOW)
    meta = (seg == 2) & (kpos >= 0) & (kpos < N_META) & (kpos <= qpos)
    return jnp.where(band | meta, 0.0, NEG_INF).astype(F32)


def _swa_stack(ref, rows, kh, lo, dtype):
    parts = []
    for g in range(2):
        pair = ref[rows, 128 * (2 * kh + g):128 * (2 * kh + g + 1)]
        zero = jnp.zeros_like(pair)
        parts += [jnp.where(lo, pair, zero), jnp.where(lo, zero, pair)]
    return jnp.concatenate(parts, axis=0).astype(dtype)


def _swa_unstack(x4, lo):
    return [jnp.where(lo, x4[2 * g * BLK:(2 * g + 1) * BLK], x4[(2 * g + 1) * BLK:(2 * g + 2) * BLK])
            for g in range(2)]


def _swa_sink_col(sink_ref, kh):
    blk = lax.broadcasted_iota(jnp.int32, (SWA_G * BLK, 1), 0) // BLK
    col = jnp.full((SWA_G * BLK, 1), sink_ref[SWA_G * kh + SWA_G - 1], F32)
    for e in reversed(range(SWA_G - 1)):
        col = jnp.where(blk == e, sink_ref[SWA_G * kh + e], col)
    return col


def _swa_softmax(qk, bias, sink):
    s = qk * (SWA_HD ** -0.5) + bias
    m = jnp.maximum(jnp.max(s, axis=-1, keepdims=True), sink)
    p = jnp.exp(s - m)
    es = jnp.exp(sink - m)
    inv = 1.0 / (jnp.sum(p, axis=-1, keepdims=True) + es)
    return p * inv, es * inv


def _swa_keys(prev_ref, cur_ref, first_ref, b, ls):
    before = prev_ref[:, ls] if b == 0 else cur_ref[(b - 1) * BLK:b * BLK, ls]
    return jnp.concatenate([before, cur_ref[b * BLK:(b + 1) * BLK, ls], first_ref[:, ls]], axis=0)


def _swa_specs(rs, ns):
    bps = rs // BLK
    cur = lambda w: pl.BlockSpec((rs, w), lambda i: (jnp.minimum(i, ns - 1), 0))
    prev = lambda w: pl.BlockSpec((BLK, w), lambda i: (jnp.maximum(jnp.minimum(i, ns - 1) * bps - 1, 0), 0))
    first = lambda w: pl.BlockSpec((BLK, w), lambda i: (0, 0))
    return cur, prev, first


def _swa_fwd(sinks, sq, sk, sv):
    t = sq.shape[0]
    rs = _seq_tile(t)
    bps, ns = rs // BLK, t // rs

    def body(sink_ref, bias_ref, q_ref, kp_ref, kc_ref, km_ref, vp_ref, vc_ref, vm_ref, o_ref):
        i = pl.program_id(0)
        lo = lax.broadcasted_iota(jnp.int32, (BLK, 128), 1) < 64
        sink_cols = [_swa_sink_col(sink_ref, kh) for kh in range(SWA_KVH)]
        chains = [(b, kh) for b in range(bps) for kh in range(SWA_KVH)]
        scores = []
        for b, kh in chains:
            ls = slice(128 * kh, 128 * (kh + 1))
            q4 = _swa_stack(q_ref, slice(b * BLK, (b + 1) * BLK), kh, lo, BF16)
            scores.append(_dg(q4, _swa_keys(kp_ref, kc_ref, km_ref, b, ls), NT))
        probs = []
        for (b, kh), s in zip(chains, scores):
            p, _ = _swa_softmax(s, bias_ref[jnp.minimum(i * bps + b, 2)], sink_cols[kh])
            probs.append(p.astype(BF16))
        for (b, kh), p in zip(chains, probs):
            ls = slice(128 * kh, 128 * (kh + 1))
            rows = slice(b * BLK, (b + 1) * BLK)
            for g, pair in enumerate(_swa_unstack(_dot(p, _swa_keys(vp_ref, vc_ref, vm_ref, b, ls)), lo)):
                o_ref[rows, 128 * (2 * kh + g):128 * (2 * kh + g + 1)] = pair

    cur, prev, first = _swa_specs(rs, ns)
    bias = _swa_bias()
    return pl.pallas_call(
        body, name="swa_fwd", grid=(ns,),
        in_specs=[pl.BlockSpec(memory_space=pltpu.SMEM), _full(bias.shape), cur(512), prev(256), cur(256), first(256),
                  prev(256), cur(256), first(256)],
        out_specs=cur(512),
        out_shape=jax.ShapeDtypeStruct((t, SWA_W), F32),
        compiler_params=_cparams(1),
    )(sinks, bias, sq, sk, sk, sk, sv, sv, sv)


def _swa_bwd(sinks, sq, sk, sv, o, do, hook=None):
    t = sq.shape[0]
    rs = _seq_tile(t)
    bps, ns = rs // BLK, t // rs

    def body(sink_ref, bias_ref, q_ref, kp_ref, kc_ref, km_ref, vp_ref, vc_ref, vm_ref, o_ref, do_ref,
             dq_ref, dk_ref, dv_ref, dkm_ref, dvm_ref, dsink_ref, pk_ref, pv_ref):
        i = pl.program_id(0)

        @pl.when(i == 0)
        def _():
            pk_ref[...] = jnp.zeros_like(pk_ref)
            pv_ref[...] = jnp.zeros_like(pv_ref)
            dkm_ref[...] = jnp.zeros_like(dkm_ref)
            dvm_ref[...] = jnp.zeros_like(dvm_ref)
            dsink_ref[...] = jnp.zeros_like(dsink_ref)

        @pl.when(i == ns)
        def _():
            dk_ref[...] = pk_ref[...]
            dv_ref[...] = pv_ref[...]

        @pl.when(i < ns)
        def _():
            lo = lax.broadcasted_iota(jnp.int32, (BLK, 128), 1) < 64
            scale = SWA_HD ** -0.5
            sink_cols = [_swa_sink_col(sink_ref, kh) for kh in range(SWA_KVH)]
            parts_k = [[None] * SWA_KVH for _ in range(bps)]
            parts_v = [[None] * SWA_KVH for _ in range(bps)]
            dsinks = [jnp.zeros((1, 1), F32) for _ in range(SWA_QH)]
            chains = [(b, kh) for b in range(bps) for kh in range(SWA_KVH)]
            lanes = lambda kh: slice(128 * kh, 128 * (kh + 1))
            block = lambda b: slice(b * BLK, (b + 1) * BLK)
            q4s = [_swa_stack(q_ref, block(b), kh, lo, BF16) for b, kh in chains]
            scores = [_dg(q4, _swa_keys(kp_ref, kc_ref, km_ref, b, lanes(kh)), NT)
                      for (b, kh), q4 in zip(chains, q4s)]
            do4s = [_swa_stack(do_ref, block(b), kh, lo, F32) for b, kh in chains]
            do4bs = [d.astype(BF16) for d in do4s]
            dps = [_dg(d, _swa_keys(vp_ref, vc_ref, vm_ref, b, lanes(kh)), NT) for (b, kh), d in zip(chains, do4bs)]
            pbs, dss = [], []
            for n_chain, (b, kh) in enumerate(chains):
                p, psink = _swa_softmax(scores[n_chain], bias_ref[jnp.minimum(i * bps + b, 2)], sink_cols[kh])
                delta = jnp.sum(do4s[n_chain] * _swa_stack(o_ref, block(b), kh, lo, F32), axis=-1, keepdims=True)
                dss.append((p * (dps[n_chain] - delta) * scale).astype(BF16))
                pbs.append(p.astype(BF16))
                dsk = psink * delta
                for e in range(SWA_G):
                    h = SWA_G * kh + e
                    dsinks[h] = dsinks[h] - jnp.sum(dsk[e * BLK:(e + 1) * BLK], axis=0, keepdims=True)
            for n_chain, (b, kh) in enumerate(chains):
                kall = _swa_keys(kp_ref, kc_ref, km_ref, b, lanes(kh))
                for g, pair in enumerate(_swa_unstack(_dot(dss[n_chain], kall), lo)):
                    dq_ref[block(b), 128 * (2 * kh + g):128 * (2 * kh + g + 1)] = pair
                parts_k[b][kh] = _dg(dss[n_chain], q4s[n_chain], TN)
                parts_v[b][kh] = _dg(pbs[n_chain], do4bs[n_chain], TN)
            last = slice(rs - BLK, rs)
            for parts, out_ref, pend_ref, meta_ref in ((parts_k, dk_ref, pk_ref, dkm_ref),
                                                       (parts_v, dv_ref, pv_ref, dvm_ref)):
                for kh in range(SWA_KVH):
                    ls = slice(128 * kh, 128 * (kh + 1))
                    if bps > 1:
                        out_ref[0:rs - BLK, ls] = pend_ref[0:rs - BLK, ls]
                    out_ref[last, ls] = pend_ref[last, ls] + parts[0][kh][0:BLK]
                    meta = parts[0][kh][2 * BLK:3 * BLK]
                    for b in range(bps):
                        own = parts[b][kh][BLK:2 * BLK]
                        if b + 1 < bps:
                            own = own + parts[b + 1][kh][0:BLK]
                            meta = meta + parts[b + 1][kh][2 * BLK:3 * BLK]
                        pend_ref[b * BLK:(b + 1) * BLK, ls] = own
                    meta_ref[:, ls] += meta
            for h in range(SWA_QH):
                dsink_ref[h:h + 1, :] += jnp.broadcast_to(dsinks[h], (1, 128))

    cur, prev, first = _swa_specs(rs, ns)
    late = lambda w: pl.BlockSpec((rs, w), lambda i: (jnp.maximum(i - 1, 0), 0))
    bias = _swa_bias()
    return _pallas(
        body, name="swa_bwd", grid=(ns + 1,),
        in_specs=[pl.BlockSpec(memory_space=pltpu.SMEM), _full(bias.shape), cur(512), prev(256), cur(256), first(256),
                  prev(256), cur(256), first(256), cur(512), cur(512)],
        out_specs=[cur(512), late(256), late(256), first(256), first(256), _full((SWA_QH, 128))],
        out_shape=[jax.ShapeDtypeStruct((t, SWA_W), F32), jax.ShapeDtypeStruct((t, 256), F32),
                   jax.ShapeDtypeStruct((t, 256), F32), jax.ShapeDtypeStruct((BLK, 256), F32),
                   jax.ShapeDtypeStruct((BLK, 256), F32), jax.ShapeDtypeStruct((SWA_QH, 128), F32)],
        scratch_shapes=[pltpu.VMEM((rs, 256), F32), pltpu.VMEM((rs, 256), F32)],
        args=(sinks, bias, sq, sk, sk, sk, sv, sv, sv, o, do), hook=hook)


def _mix_out(h1, ogla, gg, oswa, wgn, wsn, wout, wpost):
    t = h1.shape[0]
    tm = _row_tile(t)

    def body(h_ref, og_ref, gg_ref, os_ref, wgn_ref, wsn_ref, wout_ref, wpost_ref, h2_ref, cat_ref, m_ref):
        parts = []
        for h in range(GLA_HEADS):
            ls = slice(GLA_DV * h, GLA_DV * (h + 1))
            y, _, _ = _rms(og_ref[:, ls], wgn_ref[...])
            g = gg_ref[:, ls]
            parts.append(y * (g * _sigmoid(g)))
        ys, _, _ = _rms(os_ref[...], wsn_ref[...])
        cat = jnp.concatenate(parts + [ys], axis=1).astype(BF16)
        cat_ref[...] = cat
        m = _dot(cat, wout_ref[...])
        m_ref[...] = m
        y, _, _ = _rms(m, wpost_ref[...])
        h2_ref[...] = h_ref[...] + y

    def row(w):
        return pl.BlockSpec((tm, w), lambda i: (i, 0))

    return pl.pallas_call(
        body, name="mix_out", grid=(t // tm,),
        in_specs=[row(D_MODEL), row(512), row(512), row(512), _full((1, GLA_DV)), _full((1, SWA_W)),
                  _full((D_MODEL, D_MODEL)), _full((1, D_MODEL))],
        out_specs=[row(D_MODEL), row(D_MODEL), row(D_MODEL)],
        out_shape=[jax.ShapeDtypeStruct((t, D_MODEL), F32), jax.ShapeDtypeStruct((t, D_MODEL), BF16),
                   jax.ShapeDtypeStruct((t, D_MODEL), F32)],
        compiler_params=_cparams(1),
    )(h1, ogla, gg, oswa, wgn, wsn, wout, wpost)


def _mix_out_bwd(dh2, m, ogla, gg, oswa, wgn, wsn, wout, wpost, hook=None):
    t = dh2.shape[0]
    tm = _row_tile(t)

    def body(dh_ref, m_ref, og_ref, gg_ref, os_ref, wgn_ref, wsn_ref, wout_ref, wpost_ref,
             dog_ref, dgg_ref, dos_ref, dm_ref, dwpost_ref, dwgn_ref, dwsn_ref):
        @pl.when(pl.program_id(0) == 0)
        def _():
            dwpost_ref[...] = jnp.zeros_like(dwpost_ref)
            dwgn_ref[...] = jnp.zeros_like(dwgn_ref)
            dwsn_ref[...] = jnp.zeros_like(dwsn_ref)

        wpost = wpost_ref[...]
        _, mh, r = _rms(m_ref[...], wpost)
        dm, dw = _rms_bwd(mh, r, wpost, dh_ref[...])
        dwpost_ref[...] += dw
        dmb = dm.astype(BF16)
        dm_ref[...] = dmb
        dcat = _dg(dmb, wout_ref[...], NT)
        wgn = wgn_ref[...]
        for h in range(GLA_HEADS):
            ls = slice(GLA_DV * h, GLA_DV * (h + 1))
            dog = dcat[:, ls]
            g = gg_ref[:, ls]
            sg = _sigmoid(g)
            y, xh, r = _rms(og_ref[:, ls], wgn)
            dgg_ref[:, ls] = dog * y * (sg * (1.0 + g * (1.0 - sg)))
            dx, dw = _rms_bwd(xh, r, wgn, dog * (g * sg))
            dog_ref[:, ls] = dx
            dwgn_ref[...] += dw
        wsn = wsn_ref[...]
        _, xh, r = _rms(os_ref[...], wsn)
        dx, dw = _rms_bwd(xh, r, wsn, dcat[:, GLA_W:])
        dos_ref[...] = dx
        dwsn_ref[...] += dw

    def row(w):
        return pl.BlockSpec((tm, w), lambda i: (i, 0))

    def rshape(w, dt=F32):
        return jax.ShapeDtypeStruct((t, w), dt)

    return _pallas(
        body, name="mix_out_bwd", grid=(t // tm,),
        in_specs=[row(D_MODEL), row(D_MODEL), row(512), row(512), row(512), _full((1, GLA_DV)), _full((1, SWA_W)),
                  _full((D_MODEL, D_MODEL)), _full((1, D_MODEL))],
        out_specs=[row(512), row(512), row(512), row(D_MODEL), _full((1, D_MODEL)), _full((1, GLA_DV)),
                   _full((1, SWA_W))],
        out_shape=[rshape(512), rshape(512), rshape(512), rshape(D_MODEL, BF16),
                   jax.ShapeDtypeStruct((1, D_MODEL), F32), jax.ShapeDtypeStruct((1, GLA_DV), F32),
                   jax.ShapeDtypeStruct((1, SWA_W), F32)],
        args=(dh2, m, ogla, gg, oswa, wgn, wsn, wout, wpost), hook=hook)


def _mix_in_bwd(dh2, h1, wmixpre, winp, wa2p, bap, cos, sin, ga, dgq, dgk, dgv, dgg, dla, dsq, dsk, dsv, dkm, dvm):
    t = h1.shape[0]
    tm = _row_tile(t)

    def body(dh2_ref, h_ref, w_ref, win_ref, wa2_ref, ba_ref, cos_ref, sin_ref, ga_ref, dgq_ref, dgk_ref, dgv_ref,
             dgg_ref, dla_ref, dsq_ref, dsk_ref, dsv_ref, dkm_ref, dvm_ref,
             dh1_ref, dproj_ref, dw_ref, dwa2_ref, dba_ref):
        i = pl.program_id(0)

        @pl.when(i == 0)
        def _():
            dw_ref[...] = jnp.zeros_like(dw_ref)
            dwa2_ref[...] = jnp.zeros_like(dwa2_ref)
            dba_ref[...] = jnp.zeros_like(dba_ref)

        first = (i == 0).astype(F32)
        c = cos_ref[...]
        s = -sin_ref[...]
        fh = _first_half_mask(tm)
        dproj_ref[:, P_GQ:P_GK] = dgq_ref[...].astype(BF16)
        dproj_ref[:, P_GK:P_GV] = dgk_ref[...].astype(BF16)
        dproj_ref[:, P_GV:P_GG] = dgv_ref[...].astype(BF16)
        dproj_ref[:, P_GG:P_GA] = dgg_ref[...].astype(BF16)
        gab = ga_ref[...].astype(BF16)
        z = _dot(gab, wa2_ref[...]) + ba_ref[...]
        row_id = i * tm + lax.broadcasted_iota(jnp.int32, (tm, 1), 0)
        dz = jnp.where(row_id >= PAD, dla_ref[...] * (1.0 / GLA_TAU) * (1.0 - _sigmoid(z)), 0.0)
        dzb = dz.astype(BF16)
        dba_ref[...] += jnp.sum(dz, axis=0, keepdims=True)
        dwa2_ref[...] += _dg(gab, dzb, TN)
        dproj_ref[:, P_GA:P_SQ] = _dg(dzb, wa2_ref[...], NT).astype(BF16)
        for k in range(4):
            dy = dsq_ref[:, 128 * k:128 * (k + 1)]
            dproj_ref[:, P_SQ + 128 * k:P_SQ + 128 * (k + 1)] = (dy * c + _rot_half(dy, fh) * s).astype(BF16)
        for k in range(2):
            ls = slice(128 * k, 128 * (k + 1))
            dy = dsk_ref[:, ls]
            dy = jnp.concatenate([dy[:BLK] + first * dkm_ref[:, ls], dy[BLK:]], axis=0) if tm > BLK else (
                dy + first * dkm_ref[:, ls])
            dproj_ref[:, P_SK + 128 * k:P_SK + 128 * (k + 1)] = (dy * c + _rot_half(dy, fh) * s).astype(BF16)
            dv = dsv_ref[:, ls]
            dv = jnp.concatenate([dv[:BLK] + first * dvm_ref[:, ls], dv[BLK:]], axis=0) if tm > BLK else (
                dv + first * dvm_ref[:, ls])
            dproj_ref[:, P_SV + 128 * k:P_SV + 128 * (k + 1)] = dv.astype(BF16)
        dn = _dg(dproj_ref[...], win_ref[...], NT)
        w = w_ref[...]
        _, hh, r = _rms(h_ref[...], w)
        dx, dw = _rms_bwd(hh, r, w, dn)
        dw_ref[...] += dw
        dh1_ref[...] = dh2_ref[...] + dx

    def row(w):
        return pl.BlockSpec((tm, w), lambda i: (i, 0))

    return pl.pallas_call(
        body, name="mix_in_bwd", grid=(t // tm,),
        in_specs=[row(D_MODEL), row(D_MODEL), _full((1, D_MODEL)), _full((D_MODEL, P_END)), _full((128, GLA_KW)),
                  _full((1, GLA_KW)), row(128), row(128), row(128), row(256), row(256), row(512), row(512), row(256),
                  row(512), row(256), row(256), _full((BLK, 256)), _full((BLK, 256))],
        out_specs=[row(D_MODEL), row(P_END), _full((1, D_MODEL)), _full((128, GLA_KW)), _full((1, GLA_KW))],
        out_shape=[jax.ShapeDtypeStruct((t, D_MODEL), F32), jax.ShapeDtypeStruct((t, P_END), BF16),
                   jax.ShapeDtypeStruct((1, D_MODEL), F32), jax.ShapeDtypeStruct((128, GLA_KW), F32),
                   jax.ShapeDtypeStruct((1, GLA_KW), F32)],
        compiler_params=_cparams(1),
    )(dh2, h1, wmixpre, winp, wa2p, bap, cos, sin, ga, dgq, dgk, dgv, dgg, dla, dsq, dsk, dsv, dkm, dvm)


def _adamw_update(w, g, m, v):
    m = ADAM_B1 * m + (1.0 - ADAM_B1) * g
    v = ADAM_B2 * v + (1.0 - ADAM_B2) * (g * g)
    m_hat = m / (1.0 - ADAM_B1 ** ADAM_STEP)
    v_hat = v / (1.0 - ADAM_B2 ** ADAM_STEP)
    return -ADAM_LR * (m_hat / (jnp.sqrt(v_hat) + ADAM_EPS) + ADAM_WD * w), m, v


def _adamw_halves(w, g_mine, g_other, m, v, c_idx, row0=0):
    r, c = w.shape
    h = g_mine.shape[0]
    tr = _div_tile(math.gcd(r, h))
    nth = h // tr
    t0 = row0 // tr
    assert t0 * tr == row0

    def body(c_ref, w_ref, gm_ref, go_ref, m_ref, v_ref, g_ref, d_ref, nm_ref, nv_ref):
        hh = (t0 + pl.program_id(0)) // nth
        g = jnp.where(hh == c_ref[0], gm_ref[...], go_ref[...])
        g_ref[...] = g
        d_ref[...], nm_ref[...], nv_ref[...] = _adamw_update(w_ref[...], g, m_ref[...], v_ref[...])

    spec = pl.BlockSpec((tr, c), lambda i, c_ref: (i, 0))

    def gspec(is_mine):
        def index(i, c_ref):
            used = ((t0 + i) // nth == c_ref[0]) == is_mine
            return (jnp.where(used, (t0 + i) % nth, 0), 0)
        return pl.BlockSpec((tr, c), index)

    shape = jax.ShapeDtypeStruct((r, c), F32)
    return pl.pallas_call(
        body, name="adamw_halves",
        grid_spec=pltpu.PrefetchScalarGridSpec(
            num_scalar_prefetch=1, grid=(r // tr,), in_specs=[spec, gspec(True), gspec(False), spec, spec],
            out_specs=[spec] * 4),
        out_shape=[shape] * 4, compiler_params=_cparams(1),
    )(c_idx, w, g_mine, g_other, m, v)


def _place():
    x, y, c = lax.axis_index("x"), lax.axis_index("y"), lax.axis_index("c")
    chips = [(1 - x, y), (x, 1 - y), (1 - x, 1 - y)]
    return x, y, c, chips


def _remote(send_sem, recv_sem, src, dst, to):
    return pltpu.make_async_remote_copy(src_ref=src, dst_ref=dst, send_sem=send_sem, recv_sem=recv_sem,
                                        device_id=to, device_id_type=MESH)


def _half(ref_rows, c):
    h = ref_rows // 2
    return pl.ds(pl.multiple_of(c * h, 8), h)


def _own_slot(shard, q):
    return lax.dynamic_update_slice(jnp.zeros((N_CHIPS,) + shard.shape, shard.dtype), shard[None], (q, 0, 0))


def _stack_own_slot(mats, q_idx):
    r, w = mats[0].shape
    tr = _div_tile(r)
    per = r // tr
    n = len(mats)

    def body(q_ref, *refs):
        m_refs, o_ref = refs[:n], refs[n]
        s = pl.program_id(0)
        for k in range(n):
            @pl.when(s // per == k)
            def _(k=k):
                o_ref[...] = m_refs[k][...].astype(BF16)

    def rows_of(k):
        return lambda s, q_ref: (jnp.where(s // per == k, s % per, 0), 0)

    return pl.pallas_call(
        body, name="stack_own_slot",
        grid_spec=pltpu.PrefetchScalarGridSpec(
            num_scalar_prefetch=1, grid=(n * per,),
            in_specs=[pl.BlockSpec((tr, w), rows_of(k)) for k in range(n)],
            out_specs=pl.BlockSpec((None, tr, w), lambda s, q_ref: (q_ref[0], s, 0))),
        out_shape=jax.ShapeDtypeStruct((N_CHIPS, n * r, w), BF16), compiler_params=_cparams(1),
    )(q_idx, *mats)


class _GatherChips:
    has_mid = True

    def __init__(self, bufs):
        n = len(bufs)
        self.inputs = list(bufs)
        self.out_shape = [jax.ShapeDtypeStruct(b.shape, b.dtype) for b in bufs]
        self.aliases = [(t, t) for t in range(n)]
        self.scratch = [pltpu.SemaphoreType.DMA((n, 6)), pltpu.SemaphoreType.DMA((n, 6))]

    def start(self, ins, outs, scr):
        send, recv = scr
        x, y, c, chips = _place()
        q = 2 * x + y
        for t, (i_ref, o_ref) in enumerate(zip(ins, outs)):
            rows = _half(i_ref.shape[1], c)
            for j, (cx, cy) in enumerate(chips):
                _remote(send.at[t, j], recv.at[t, j], i_ref.at[q, rows], o_ref.at[q, rows], (cx, cy, c)).start()

    def mid(self, ins, outs, scr):
        send, recv = scr
        x, y, c, chips = _place()
        for t, o_ref in enumerate(outs):
            rows = _half(o_ref.shape[1], c)
            for j, (cx, cy) in enumerate(chips):
                slot = o_ref.at[2 * cx + cy, rows]
                _remote(send.at[t, j], recv.at[t, j], slot, slot, (cx, cy, c)).wait_recv()
                _remote(send.at[t, 3 + j], recv.at[t, 3 + j], slot, slot, (x, y, 1 - c)).start()

    def finish(self, ins, outs, scr):
        send, recv = scr
        x, y, c, chips = _place()
        for t, o_ref in enumerate(outs):
            mine, other = _half(o_ref.shape[1], c), _half(o_ref.shape[1], 1 - c)
            for j, (cx, cy) in enumerate(chips):
                slot = o_ref.at[2 * cx + cy, other]
                _remote(send.at[t, 3 + j], recv.at[t, 3 + j], slot, slot, (x, y, 1 - c)).wait_recv()
            for j, (cx, cy) in enumerate(chips):
                sent = o_ref.at[2 * cx + cy, mine]
                _remote(send.at[t, j], recv.at[t, j], sent, sent, (cx, cy, c)).wait_send()
                _remote(send.at[t, 3 + j], recv.at[t, 3 + j], sent, sent, (x, y, 1 - c)).wait_send()


class _PairExchange:
    has_mid = False
    aliases = ()

    def __init__(self, arrs):
        n = len(arrs)
        self.inputs = list(arrs)
        self.out_shape = [jax.ShapeDtypeStruct((a.shape[0], a.shape[1] // 2, a.shape[2]), a.dtype) for a in arrs]
        self.scratch = [pltpu.SemaphoreType.DMA((n,)), pltpu.SemaphoreType.DMA((n,))]

    def _copies(self, ins, outs, scr):
        send, recv = scr
        x, y, c, _ = _place()
        return [_remote(send.at[t], recv.at[t], i_ref.at[:, _half(i_ref.shape[1], 1 - c)], o_ref, (x, y, 1 - c))
                for t, (i_ref, o_ref) in enumerate(zip(ins, outs))]

    def start(self, ins, outs, scr):
        for cp in self._copies(ins, outs, scr):
            cp.start()

    def finish(self, ins, outs, scr):
        for cp in self._copies(ins, outs, scr):
            cp.wait()


class _ChipScatter:
    has_mid = False
    aliases = ()

    def __init__(self, arrs):
        n = len(arrs)
        self.inputs = list(arrs)
        self.out_shape = [jax.ShapeDtypeStruct((3,) + a.shape[1:], a.dtype) for a in arrs]
        self.scratch = [pltpu.SemaphoreType.DMA((n, 3)), pltpu.SemaphoreType.DMA((n, 3))]

    def _copies(self, ins, outs, scr):
        send, recv = scr
        x, y, c, chips = _place()
        return [_remote(send.at[t, j], recv.at[t, j], i_ref.at[2 * cx + cy], o_ref.at[j], (cx, cy, c))
                for t, (i_ref, o_ref) in enumerate(zip(ins, outs)) for j, (cx, cy) in enumerate(chips)]

    def start(self, ins, outs, scr):
        for cp in self._copies(ins, outs, scr):
            cp.start()

    def finish(self, ins, outs, scr):
        for cp in self._copies(ins, outs, scr):
            cp.wait()


class _PairShare:
    has_mid = False
    aliases = ()

    def __init__(self, arrs):
        n = len(arrs)
        self.inputs = list(arrs)
        self.out_shape = [jax.ShapeDtypeStruct(a.shape, a.dtype) for a in arrs]
        self.scratch = [pltpu.SemaphoreType.DMA((n,)), pltpu.SemaphoreType.DMA((n,))]

    def _copies(self, ins, outs, scr):
        send, recv = scr
        x, y, c, _ = _place()
        return [_remote(send.at[t], recv.at[t], i_ref, o_ref, (x, y, 1 - c))
                for t, (i_ref, o_ref) in enumerate(zip(ins, outs))]

    def start(self, ins, outs, scr):
        for cp in self._copies(ins, outs, scr):
            cp.start()

    def finish(self, ins, outs, scr):
        for cp in self._copies(ins, outs, scr):
            cp.wait()


def _comm_call(hook, name):
    n_in, n_out = len(hook.inputs), len(hook.out_shape)

    def body(*refs):
        ins, outs, scr = refs[:n_in], refs[n_in:n_in + n_out], refs[n_in + n_out:]
        hook.start(ins, outs, scr)
        if hook.has_mid:
            hook.mid(ins, outs, scr)
        hook.finish(ins, outs, scr)

    return pl.pallas_call(body, name=name, in_specs=[ANY] * n_in, out_specs=[ANY] * n_out,
                          out_shape=list(hook.out_shape), scratch_shapes=list(hook.scratch),
                          input_output_aliases=dict(hook.aliases))(*hook.inputs)


class _GatherDevices:
    has_mid = True
    aliases = ()

    def __init__(self, vecs):
        n = len(vecs)
        self.inputs = list(vecs)
        self.out_shape = [jax.ShapeDtypeStruct((N_DEV,) + v.shape, v.dtype) for v in vecs]
        self.scratch = [pltpu.SemaphoreType.DMA((n, 7)), pltpu.SemaphoreType.DMA((n, 7)),
                        pltpu.SemaphoreType.DMA((n,))]

    @staticmethod
    def _copy(scr, t, k, out_ref, block, to, src=None):
        send, recv, _ = scr
        px, py, pc = block
        slot = out_ref.at[4 * px + 2 * py + pc]
        return _remote(send.at[t, k], recv.at[t, k], slot if src is None else src, slot, to)

    def start(self, ins, outs, scr):
        x, y, c, chips = _place()
        me = (x, y, c)
        for t, (x_ref, out_ref) in enumerate(zip(ins, outs)):
            pltpu.make_async_copy(x_ref, out_ref.at[4 * x + 2 * y + c], scr[2].at[t]).start()
            self._copy(scr, t, 0, out_ref, me, (x, y, 1 - c), src=x_ref).start()
            for j, chip in enumerate(chips):
                self._copy(scr, t, 1 + j, out_ref, me, (*chip, c), src=x_ref).start()

    def mid(self, ins, outs, scr):
        x, y, c, chips = _place()
        for t, out_ref in enumerate(outs):
            for j, chip in enumerate(chips):
                self._copy(scr, t, 1 + j, out_ref, (*chip, c), (x, y, c)).wait_recv()
                self._copy(scr, t, 4 + j, out_ref, (*chip, c), (x, y, 1 - c)).start()

    def finish(self, ins, outs, scr):
        x, y, c, chips = _place()
        me = (x, y, c)
        for t, (x_ref, out_ref) in enumerate(zip(ins, outs)):
            self._copy(scr, t, 0, out_ref, (x, y, 1 - c), me).wait_recv()
            for j, chip in enumerate(chips):
                self._copy(scr, t, 4 + j, out_ref, (*chip, 1 - c), me).wait_recv()
            self._copy(scr, t, 0, out_ref, me, (x, y, 1 - c), src=x_ref).wait_send()
            for j, chip in enumerate(chips):
                self._copy(scr, t, 1 + j, out_ref, me, (*chip, c), src=x_ref).wait_send()
                self._copy(scr, t, 4 + j, out_ref, (*chip, c), (x, y, 1 - c)).wait_send()
            pltpu.make_async_copy(x_ref, out_ref.at[4 * x + 2 * y + c], scr[2].at[t]).wait()


class _Hooks:
    def __init__(self, hooks):
        self.hooks = list(hooks)
        self.has_mid = any(h.has_mid for h in hooks)
        self.inputs = [a for h in hooks for a in h.inputs]
        self.out_shape = [s for h in hooks for s in h.out_shape]
        self.scratch = [s for h in hooks for s in h.scratch]
        self.aliases = []
        i0 = o0 = 0
        for h in hooks:
            self.aliases += [(i0 + a, o0 + b) for a, b in h.aliases]
            i0 += len(h.inputs)
            o0 += len(h.out_shape)

    def _each(self, ins, outs, scr):
        i0 = o0 = s0 = 0
        for h in self.hooks:
            ni, no, ns = len(h.inputs), len(h.out_shape), len(h.scratch)
            yield h, ins[i0:i0 + ni], outs[o0:o0 + no], scr[s0:s0 + ns]
            i0, o0, s0 = i0 + ni, o0 + no, s0 + ns

    def start(self, ins, outs, scr):
        for h, i, o, s in self._each(ins, outs, scr):
            h.start(i, o, s)

    def mid(self, ins, outs, scr):
        for h, i, o, s in self._each(ins, outs, scr):
            if h.has_mid:
                h.mid(i, o, s)

    def finish(self, ins, outs, scr):
        for h, i, o, s in self._each(ins, outs, scr):
            h.finish(i, o, s)

    def split(self, outs):
        res, o0 = [], 0
        for h in self.hooks:
            res.append(list(outs[o0:o0 + len(h.out_shape)]))
            o0 += len(h.out_shape)
        return res


def _pair_sum(g, other, c_idx):
    nq, r, w = g.shape
    h = r // 2
    tr = _div_tile(h)
    nt = h // tr

    def body(c_ref, g_ref, o_ref, s_ref):
        s_ref[...] = (g_ref[...].astype(F32) + o_ref[...].astype(F32)).astype(s_ref.dtype)

    return pl.pallas_call(
        body, name="pair_sum",
        grid_spec=pltpu.PrefetchScalarGridSpec(
            num_scalar_prefetch=1, grid=(nq, nt),
            in_specs=[pl.BlockSpec((None, tr, w), lambda k, i, c_ref: (k, c_ref[0] * nt + i, 0)),
                      pl.BlockSpec((None, tr, w), lambda k, i, c_ref: (k, i, 0))],
            out_specs=pl.BlockSpec((None, tr, w), lambda k, i, c_ref: (k, i, 0))),
        out_shape=jax.ShapeDtypeStruct((nq, h, w), g.dtype),
        compiler_params=_cparams(2),
    )(c_idx, g, other)


def _pair_exchange_sum(arrs):
    n = len(arrs)
    halves = [(a.shape[0], a.shape[1] // 2, a.shape[2]) for a in arrs]

    def body(*refs):
        ins, outs, mine, theirs = (refs[k * n:(k + 1) * n] for k in range(4))
        send, recv, load = refs[4 * n:]
        x, y, c, _ = _place()
        remote = [_remote(send.at[t], recv.at[t], ins[t].at[:, _half(ins[t].shape[1], 1 - c)], theirs[t],
                          (x, y, 1 - c)) for t in range(n)]
        local = [pltpu.make_async_copy(ins[t].at[:, _half(ins[t].shape[1], c)], mine[t], load.at[t])
                 for t in range(n)]
        for cp in remote + local:
            cp.start()
        for t in range(n):
            local[t].wait()
            remote[t].wait()
            for q in range(halves[t][0]):
                outs[t][q] = (mine[t][q].astype(F32) + theirs[t][q].astype(F32)).astype(outs[t].dtype)

    return pl.pallas_call(
        body, name="pair_exchange_sum", in_specs=[ANY] * n,
        out_shape=[jax.ShapeDtypeStruct(s, a.dtype) for s, a in zip(halves, arrs)],
        scratch_shapes=[pltpu.VMEM(s, a.dtype) for s, a in zip(halves, arrs)] * 2
        + [pltpu.SemaphoreType.DMA((n,))] * 3,
        compiler_params=_cparams(0),
    )(*arrs)


def _chip_sum(s, others, q_idx):
    _, h, w = s.shape
    tr = _div_tile(h)

    def body(q_ref, s_ref, o_ref, out_ref):
        out_ref[...] = ((s_ref[...].astype(F32) + o_ref[0].astype(F32)) + o_ref[1].astype(F32)) + o_ref[2].astype(F32)

    return pl.pallas_call(
        body, name="chip_sum",
        grid_spec=pltpu.PrefetchScalarGridSpec(
            num_scalar_prefetch=1, grid=(h // tr,),
            in_specs=[pl.BlockSpec((None, tr, w), lambda i, q_ref: (q_ref[0], i, 0)),
                      pl.BlockSpec((3, tr, w), lambda i, q_ref: (0, i, 0))],
            out_specs=pl.BlockSpec((tr, w), lambda i, q_ref: (i, 0))),
        out_shape=jax.ShapeDtypeStruct((h, w), F32),
        compiler_params=_cparams(1),
    )(q_idx, s, others)


def _small_update(q_idx, parts, ws, ms, vs, col_block):
    n = len(parts)
    has_w = [w is not None for w in ws]

    def body(q_ref, *refs):
        pos = 0
        ins = []
        for t in range(n):
            k = 4 if has_w[t] else 1
            ins.append(refs[pos:pos + k])
            pos += k
        outs = refs[pos:]
        opos = 0
        for t in range(n):
            p_ref = ins[t][0]
            g = p_ref[0]
            for s in range(1, p_ref.shape[0]):
                g = g + p_ref[s]
            if has_w[t]:
                _, w_ref, m_ref, v_ref = ins[t]
                g_ref, d_ref, nm_ref, nv_ref = outs[opos:opos + 4]
                opos += 4
                g_ref[...] = g
                d_ref[...], nm_ref[...], nv_ref[...] = _adamw_update(w_ref[...], g, m_ref[...], v_ref[...])
            else:
                outs[opos][...] = g
                opos += 1

    def whole(shape):
        nd = len(shape)
        return pl.BlockSpec(shape, lambda i, q_ref: (0,) * nd)

    in_specs, out_specs, out_shape, args = [], [], [], []
    for t in range(n):
        k, r, wf = parts[t].shape
        if col_block[t]:
            w = wf // N_CHIPS
            in_specs.append(pl.BlockSpec((k, r, w), lambda i, q_ref: (0, 0, q_ref[0])))
        else:
            w = wf
            in_specs.append(whole((k, r, wf)))
        args.append(parts[t])
        if has_w[t]:
            assert ws[t].shape == (r, w), (ws[t].shape, r, w)
            in_specs += [whole((r, w))] * 3
            args += [ws[t], ms[t], vs[t]]
            out_specs += [whole((r, w))] * 4
            out_shape += [jax.ShapeDtypeStruct((r, w), F32)] * 4
        else:
            out_specs.append(whole((r, w)))
            out_shape.append(jax.ShapeDtypeStruct((r, w), F32))
    return pl.pallas_call(
        body, name="small_update",
        grid_spec=pltpu.PrefetchScalarGridSpec(num_scalar_prefetch=1, grid=(1,), in_specs=in_specs,
                                               out_specs=out_specs),
        out_shape=out_shape, compiler_params=_cparams(1),
    )(q_idx, *args)


_PACK_SEGMENTS = ((0, 1552), None, (1552, 2064), (2064, 2128), (2064, 2128), (2128, 2192), (2128, 2192),
                  (2192, 2256), (2192, 2256), (2256, 2320), (2256, 2320))
_UNPACK_SEGMENTS = (((0, 1552), (0,)), ((1552, 2064), (P_SQ,)), ((2064, 2128), (P_SK, P_SK + 64)),
                    ((2128, 2192), (P_SK + 128, P_SK + 192)), ((2192, 2256), (P_SV, P_SV + 64)),
                    ((2256, 2320), (P_SV + 128, P_SV + 192)))


def _pack_win(w4):
    per = w4.shape[2]
    pieces = []
    for seg in _PACK_SEGMENTS:
        if seg is None:
            pieces.append(jnp.zeros((w4.shape[1], 128 - GLA_RANK), w4.dtype))
            continue
        for q in range(w4.shape[0]):
            lo, hi = max(seg[0], q * per), min(seg[1], (q + 1) * per)
            if lo < hi:
                pieces.append(w4[q][:, lo - q * per:hi - q * per])
    return jnp.concatenate(pieces, axis=1)


def _unpack_dwin(d):
    per = D_IN // N_CHIPS
    chips = []
    for q in range(N_CHIPS):
        pieces = []
        for (a, b), starts in _UNPACK_SEGMENTS:
            lo, hi = max(a, q * per), min(b, (q + 1) * per)
            if lo < hi:
                copies = [d[:, s + lo - a:s + hi - a] for s in starts]
                pieces.append(copies[0] if len(copies) == 1 else copies[0] + copies[1])
        chips.append(jnp.concatenate(pieces, axis=1))
    return jnp.stack(chips)


def _local_step(x, target, meta, p):
    s = x.shape[0]
    t = s + BLK
    h0 = jnp.concatenate([jnp.zeros((PAD, D_MODEL), F32), meta, x], axis=0)
    cos, sin = _rope_tables(t)

    h1, n1, g1, u1, a1, f1 = _ffn_fwd(h0, p["ffn1_pre_norm"], p["ffn1_w"], p["ffn1_post_norm"])
    n2, gq, gk, gv, gg, ga, la, sq, sk, sv = _mix_proj(h1, p["mix_pre_norm"], p["w_in"], p["gla_w_a2"], p["gla_b_a"],
                                                       cos, sin)
    ogla, ss = _gla_fwd(gq, gk, gv, la)
    oswa = _swa_fwd(p["swa_sinks"], sq, sk, sv)
    h2, cat, m = _mix_out(h1, ogla, gg, oswa, p["gla_out_norm"], p["swa_out_norm"], p["w_out"], p["mix_post_norm"])
    grads = {}
    dy, n3, g3, u3, a3, df3, grads["ffn2_post_norm"], sse = _ffn_fwd(
        h2, p["ffn2_pre_norm"], p["ffn2_w"], p["ffn2_post_norm"], target=target)

    dh2, dg3, du3, grads["ffn2_pre_norm"] = _ffn_bwd(
        dy, h2, None, g3, u3, p["ffn2_pre_norm"], p["ffn2_w"], p["ffn2_post_norm"], df=df3)
    (gud,) = _ffn_wgrad(n3, df3, dg3, du3, a3)
    grads["ffn2_w_gate"], grads["ffn2_w_up"], grads["ffn2_w_down"] = gud[:, :FJ], gud[:, FJ:2 * FJ], gud[:, 2 * FJ:]

    dogla, dgg, doswa, dm, grads["mix_post_norm"], grads["gla_out_norm"], grads["swa_out_norm"] = _mix_out_bwd(
        dh2, m, ogla, gg, oswa, p["gla_out_norm"], p["swa_out_norm"], p["w_out"], p["mix_post_norm"])
    grads["w_out"] = _xty(cat, dm)
    dsq, dsk, dsv, dkm, dvm, dsinks = _swa_bwd(p["swa_sinks"], sq, sk, sv, oswa, doswa)
    grads["swa_sinks"] = dsinks[:, 0]
    dgq, dgk, dgv, dla = _gla_bwd(gq, gk, gv, la, ss, dogla)
    dh1, dproj, grads["mix_pre_norm"], dwa2p, grads["gla_b_a"] = _mix_in_bwd(
        dh2, h1, p["mix_pre_norm"], p["w_in"], p["gla_w_a2"], p["gla_b_a"], cos, sin, ga, dgq, dgk, dgv, dgg, dla,
        dsq, dsk, dsv, dkm, dvm)
    grads["gla_w_a2"] = dwa2p[:GLA_RANK]
    grads["w_in"] = _unpack_dwin(_xty(n2, dproj))

    dh0, df1, dg1, du1, grads["ffn1_pre_norm"], grads["ffn1_post_norm"] = _ffn_bwd(
        dh1, h0, f1, g1, u1, p["ffn1_pre_norm"], p["ffn1_w"], p["ffn1_post_norm"])
    (gud,) = _ffn_wgrad(n1, df1, dg1, du1, a1)
    grads["ffn1_w_gate"], grads["ffn1_w_up"], grads["ffn1_w_down"] = gud[:, :FJ], gud[:, FJ:2 * FJ], gud[:, 2 * FJ:]
    grads["meta_tokens"] = dh0[PAD:BLK]
    return sse[0, 0], dh0[BLK:], grads


WEIGHTS = ['meta_tokens', 'ffn1_pre_norm', 'ffn1_w_gate', 'ffn1_w_up', 'ffn1_w_down', 'ffn1_post_norm',
           'mix_pre_norm', 'w_in', 'gla_w_a2', 'gla_b_a', 'gla_out_norm', 'swa_sinks', 'swa_out_norm', 'w_out',
           'mix_post_norm', 'ffn2_pre_norm', 'ffn2_w_gate', 'ffn2_w_up', 'ffn2_w_down', 'ffn2_post_norm']
BIG = ['ffn1_w_gate', 'ffn1_w_up', 'ffn1_w_down', 'w_in', 'w_out', 'ffn2_w_gate', 'ffn2_w_up', 'ffn2_w_down']
SMALL = [n for n in WEIGHTS if n not in BIG]
FJ = D_FF // N_CHIPS
D_IN_J = D_IN // N_CHIPS
D_OUT_J = D_MODEL // N_CHIPS
TRANSPOSED = ('ffn1_w_gate', 'ffn1_w_up', 'ffn2_w_gate', 'ffn2_w_up')


def _shard2d(name, a):
    return a[0].T if name in TRANSPOSED else a[0]


def _unshard2d(name, a):
    return (a.T if name in TRANSPOSED else a)[None]


def kernel(x, meta_tokens, ffn1_pre_norm, ffn1_w_gate, ffn1_w_up, ffn1_w_down, ffn1_post_norm, mix_pre_norm, w_in, gla_w_a2, gla_b_a, gla_out_norm, swa_sinks, swa_out_norm, w_out, mix_post_norm, ffn2_pre_norm, ffn2_w_gate, ffn2_w_up, ffn2_w_down, ffn2_post_norm, loss_target, m_meta_tokens, m_ffn1_pre_norm, m_ffn1_w_gate, m_ffn1_w_up, m_ffn1_w_down, m_ffn1_post_norm, m_mix_pre_norm, m_w_in, m_gla_w_a2, m_gla_b_a, m_gla_out_norm, m_swa_sinks, m_swa_out_norm, m_w_out, m_mix_post_norm, m_ffn2_pre_norm, m_ffn2_w_gate, m_ffn2_w_up, m_ffn2_w_down, m_ffn2_post_norm, v_meta_tokens, v_ffn1_pre_norm, v_ffn1_w_gate, v_ffn1_w_up, v_ffn1_w_down, v_ffn1_post_norm, v_mix_pre_norm, v_w_in, v_gla_w_a2, v_gla_b_a, v_gla_out_norm, v_swa_sinks, v_swa_out_norm, v_w_out, v_mix_post_norm, v_ffn2_pre_norm, v_ffn2_w_gate, v_ffn2_w_up, v_ffn2_w_down, v_ffn2_post_norm):
    args = dict(locals())
    w = {n: args[n] for n in WEIGHTS}
    mom = {n: args["m_" + n] for n in WEIGHTS}
    var = {n: args["v_" + n] for n in WEIGHTS}
    cx, cy, cc = lax.axis_index("x"), lax.axis_index("y"), lax.axis_index("c")
    q_idx = (2 * cx + cy).astype(jnp.int32).reshape(1)
    c_idx = cc.astype(jnp.int32).reshape(1)

    q_chip = 2 * cx + cy
    bf = {n: _own_slot(_shard2d(n, w[n]).astype(BF16), q_chip) for n in ("w_in", "w_out")}
    for ffn in ("ffn1", "ffn2"):
        bf[ffn] = _stack_own_slot([_shard2d(ffn + s, w[ffn + s]) for s in ("_w_gate", "_w_up", "_w_down")], q_idx)
    qc_idx = jnp.stack([q_chip, cc]).astype(jnp.int32)
    sinks = w["swa_sinks"].reshape(SWA_QH)

    seq, target = x[0], loss_target[0]
    t = seq.shape[0] + BLK
    h0, n1 = _embed_norm(seq, _own_slot(w["meta_tokens"], q_chip), w["ffn1_pre_norm"])
    cos, sin = _rope_tables(t)
    late = _GatherChips([bf["w_in"], bf["w_out"], bf["ffn2"],
                         _own_slot(w["gla_w_a2"].reshape(GLA_RANK, GLA_KW // N_CHIPS), q_chip)])
    (h1, g1, u1, a1, f1), (w31,), (win4, wout4, w32, wa24) = _ffn_fwd_gather(
        h0, n1, bf["ffn1"], w["ffn1_post_norm"], qc_idx, late)
    wa2p = jnp.pad(wa24.transpose(1, 0, 2).reshape(GLA_RANK, GLA_KW), ((0, 128 - GLA_RANK), (0, 0))).astype(BF16)
    winp = _pack_win(win4)
    wout = wout4.reshape(D_MODEL, D_MODEL)
    n2, gq, gk, gv, gg, ga, la, sq, sk, sv = _mix_proj(h1, w["mix_pre_norm"], winp, wa2p, w["gla_b_a"], cos, sin)
    ogla, ss = _gla_fwd(gq, gk, gv, la)
    oswa = _swa_fwd(sinks, sq, sk, sv)
    h2, cat, m = _mix_out(h1, ogla, gg, oswa, w["gla_out_norm"], w["swa_out_norm"], wout, w["mix_post_norm"])
    g = {}
    dy, n3, g3, u3, a3, df3, g["ffn2_post_norm"], sse = _ffn_fwd(
        h2, w["ffn2_pre_norm"], w32, w["ffn2_post_norm"], target=target)

    dh2, dg3, du3, g["ffn2_pre_norm"] = _ffn_bwd(
        dy, h2, None, g3, u3, w["ffn2_pre_norm"], w32, w["ffn2_post_norm"], df=df3)
    (gf2,) = _ffn_wgrad(n3, df3, dg3, du3, a3)
    (dogla, dgg, doswa, dm, g["mix_post_norm"], g["gla_out_norm"], g["swa_out_norm"]), (rgf2,) = _mix_out_bwd(
        dh2, m, ogla, gg, oswa, w["gla_out_norm"], w["swa_out_norm"], wout, w["mix_post_norm"],
        hook=_PairExchange([gf2]))
    sgf2 = _pair_sum(gf2, rgf2, c_idx)
    gout = _xty(cat, dm).reshape(N_CHIPS, D_OUT_J, D_MODEL)
    (dsq, dsk, dsv, dkm, dvm, dsinks), (ogf2,) = _swa_bwd(sinks, sq, sk, sv, oswa, doswa,
                                                          hook=_ChipScatter([sgf2]))
    g["swa_sinks"] = dsinks
    dgq, dgk, dgv, dla = _gla_bwd(gq, gk, gv, la, ss, dogla)
    dh1, dproj, g["mix_pre_norm"], dwa2p, g["gla_b_a"] = _mix_in_bwd(
        dh2, h1, w["mix_pre_norm"], winp, wa2p, w["gla_b_a"], cos, sin, ga, dgq, dgk, dgv, dgg, dla,
        dsq, dsk, dsv, dkm, dvm)
    g["gla_w_a2"] = dwa2p[:GLA_RANK]
    gin = _unpack_dwin(_xty(n2, dproj))
    sgin, sgout = _pair_exchange_sum([gin, gout])
    dh_first, grad_x, df1, dg1, du1, g["ffn1_pre_norm"], g["ffn1_post_norm"] = _ffn_bwd(
        dh1, h0, f1, g1, u1, w["ffn1_pre_norm"], w31, w["ffn1_post_norm"], split_first_block=True)
    g["meta_tokens"] = dh_first[PAD:BLK]
    late_small = ["gla_w_a2", "swa_sinks"]
    direct = [n for n in SMALL if n not in late_small]
    names = direct + late_small
    half_f2 = _chip_sum(sgf2, ogf2, q_idx)
    hooks = _Hooks([_ChipScatter([sgin, sgout]), _GatherDevices([g[n] for n in names] + [sse]),
                    _PairShare([half_f2])])
    own1, others1, houts = _ffn_wgrad_reduce(n1, df1, dg1, du1, a1, qc_idx, hooks)
    (ogin, ogout), gathered, (other_f2,) = hooks.split(houts)
    halves = [_chip_sum(own1[None], others1, jnp.zeros((1,), jnp.int32))]
    halves += [_chip_sum(s, o, q_idx) for s, o in ((sgin, ogin), (sgout, ogout))]
    others = list(_comm_call(_PairShare(halves), "pair_share")) + [other_f2]
    halves.append(half_f2)
    reduced = {"ffn1_w_gate": (0, 0), "ffn1_w_up": (0, FJ), "ffn1_w_down": (0, 2 * FJ), "w_in": (1, 0),
               "w_out": (2, 0), "ffn2_w_gate": (3, 0), "ffn2_w_up": (3, FJ), "ffn2_w_down": (3, 2 * FJ)}
    grad, delta, new_m, new_v = {}, {}, {}, {}
    for n in BIG:
        k, row0 = reduced[n]
        outs = _adamw_halves(_shard2d(n, w[n]), halves[k], others[k], _shard2d(n, mom[n]), _shard2d(n, var[n]),
                             c_idx, row0)
        grad[n], delta[n], new_m[n], new_v[n] = [_unshard2d(n, a) for a in outs]

    late = late_small
    mat = lambda a: a.reshape(a.shape[-2:])
    none3 = [None] * (len(late) + 1)
    outs = _small_update(q_idx, gathered, [mat(w[n]) for n in direct] + none3, [mat(mom[n]) for n in direct] + none3,
                         [mat(var[n]) for n in direct] + none3, [n == "meta_tokens" for n in names] + [False])
    sum_a2, sum_sinks, sum_sse = outs[4 * len(direct):]
    loss = sum_sse[0, 0] * (0.5 / D_MODEL)
    g_late = [lax.dynamic_slice_in_dim(sum_a2, q_chip * (GLA_KW // N_CHIPS), GLA_KW // N_CHIPS, axis=1)[None],
              sum_sinks[:, 0].reshape(1, 1, SWA_QH)]
    outs = list(outs[:4 * len(direct)]) + list(_small_update(
        q_idx, g_late, [mat(w[n]) for n in late], [mat(mom[n]) for n in late], [mat(var[n]) for n in late],
        [False, False]))
    for k, n in enumerate(names):
        grad[n], delta[n], new_m[n], new_v[n] = [a.reshape(w[n].shape) for a in outs[4 * k:4 * k + 4]]

    return (loss, grad_x[None], *[grad[n] for n in WEIGHTS], *[delta[n] for n in WEIGHTS],
            *[new_m[n] for n in WEIGHTS], *[new_v[n] for n in WEIGHTS])
```

```python
import functools
import math

import numpy as np
import jax
import jax.numpy as jnp
from jax import lax
from jax.experimental import pallas as pl
from jax.experimental.pallas import tpu as pltpu

F32 = jnp.float32
BF16 = jnp.bfloat16
MESH = pl.DeviceIdType.MESH

D_MODEL = 1024
D_FF = 2816
N_CHIPS = 4
N_DEV = 8
N_META = 16
BLK = 128
PAD = BLK - N_META
GLA_CHUNK = 64
GLA_HEADS = 4
GLA_DV = 128
GLA_DK = 64
GLA_KW = GLA_HEADS * GLA_DK
GLA_W = GLA_HEADS * GLA_DV
GLA_RANK = 16
GLA_TAU = 16.0
SWA_HD = 64
SWA_QH = 8
SWA_KVH = 2
SWA_W = SWA_QH * SWA_HD
WINDOW = 128
ROPE_THETA = 10000.0
EPS = 1e-6
NEG_INF = -1e30
IN_SPLITS = (256, 256, 512, 512, 16, 512, 128, 128)
D_IN = sum(IN_SPLITS)
P_GQ, P_GK, P_GV, P_GG, P_GA, P_SQ, P_SK, P_SV, P_END = 0, 256, 512, 1024, 1536, 1664, 2176, 2432, 2688
ADAM_LR, ADAM_B1, ADAM_B2, ADAM_EPS, ADAM_WD, ADAM_STEP = 0.001, 0.9, 0.999, 1e-08, 0.01, 10
VMEM_LIMIT = 56 * 1024 * 1024

NT = (((1,), (1,)), ((), ()))
TN = (((0,), (0,)), ((), ()))


def _cparams(n_axes):
    return pltpu.CompilerParams(dimension_semantics=("arbitrary",) * n_axes, vmem_limit_bytes=VMEM_LIMIT)


def _row_tile(t):
    for tm in (640, 512, 384, 256, 128):
        if t % tm == 0:
            return tm
    raise ValueError(t)


SEQ_BLOCKS_PER_STEP = 5


def _seq_tile(t):
    return SEQ_BLOCKS_PER_STEP * BLK if t % (SEQ_BLOCKS_PER_STEP * BLK) == 0 else BLK


ROW_PARTS = 2


def _row_parts(tm):
    n = ROW_PARTS if tm % (16 * ROW_PARTS) == 0 else 1
    return [slice(k * (tm // n), (k + 1) * (tm // n)) for k in range(n)]


def _contract_tile(t):
    return 1664 if t % 1664 == 0 else _row_tile(t)


def _div_tile(r, cap=512):
    best = None
    for tr in range(8, min(r, cap) + 1, 8):
        if r % tr == 0:
            best = tr
    return best if best is not None else r


def _dot(a, b):
    return jnp.dot(a, b, preferred_element_type=F32)


def _dg(a, b, dims):
    return lax.dot_general(a, b, dims, preferred_element_type=F32)


def _rms(x, w):
    r = lax.rsqrt(jnp.mean(x * x, axis=-1, keepdims=True) + EPS)
    xh = x * r
    return xh * w, xh, r


def _rms_bwd(xh, r, w, dy):
    wdy = dy * w
    dx = r * (wdy - xh * jnp.mean(wdy * xh, axis=-1, keepdims=True))
    dw = jnp.sum(dy * xh, axis=0, keepdims=True)
    return dx, dw


def _sigmoid(x):
    return 1.0 / (1.0 + jnp.exp(-x))


def _full(shape):
    nd = len(shape)
    return pl.BlockSpec(shape, lambda *_: (0,) * nd)


ANY = pl.BlockSpec(memory_space=pl.ANY)


def _pallas(body, *, name, grid, in_specs, out_specs, out_shape, args, scratch_shapes=(), hook=None):
    n_axes = len(grid)
    if hook is None:
        return pl.pallas_call(body, name=name, grid=grid, in_specs=list(in_specs), out_specs=list(out_specs),
                              out_shape=list(out_shape), scratch_shapes=list(scratch_shapes),
                              compiler_params=_cparams(n_axes))(*args)
    n_in, n_out, n_scr = len(in_specs), len(out_specs), len(scratch_shapes)
    h_in, h_out = len(hook.inputs), len(hook.out_shape)
    total = math.prod(grid)

    def wrapped(*refs):
        ins, hins = refs[:n_in], refs[n_in:n_in + h_in]
        o0 = n_in + h_in
        outs, houts = refs[o0:o0 + n_out], refs[o0 + n_out:o0 + n_out + h_out]
        s0 = o0 + n_out + h_out
        scr, hscr = refs[s0:s0 + n_scr], refs[s0 + n_scr:]
        step = pl.program_id(0)
        for a in range(1, n_axes):
            step = step * grid[a] + pl.program_id(a)

        @pl.when(step == 0)
        def _():
            hook.start(hins, houts, hscr)

        body(*ins, *outs, *scr)

        if hook.has_mid:
            @pl.when(step == (3 * total) // 4)
            def _():
                hook.mid(hins, houts, hscr)

        @pl.when(step == total - 1)
        def _():
            hook.finish(hins, houts, hscr)

    res = pl.pallas_call(
        wrapped, name=name, grid=grid, in_specs=list(in_specs) + [ANY] * h_in,
        out_specs=list(out_specs) + [ANY] * h_out, out_shape=list(out_shape) + list(hook.out_shape),
        scratch_shapes=list(scratch_shapes) + list(hook.scratch), compiler_params=_cparams(n_axes),
        input_output_aliases={n_in + a: n_out + b for a, b in hook.aliases},
    )(*args, *hook.inputs)
    return res[:n_out], res[n_out:]


def _ffn_weight_specs(w3):
    fj = w3.shape[1] // 3
    return fj, [pl.BlockSpec((None, fj, D_MODEL), functools.partial(lambda i, j, k: (j, k, 0), k=k)) for k in range(3)]


def _ffn_fwd(h, wpre, w3, wpost, hook=None, target=None):
    t = h.shape[0]
    tm = _row_tile(t)
    nj, rows3, _ = w3.shape
    fj = rows3 // 3
    nblk = tm // BLK if target is not None else 0

    def body(*refs):
        h_ref, wpre_ref, w_hbm, wpost_ref = refs[:4]
        t_refs = refs[4:4 + nblk]
        hout_ref, n_ref, p1_ref, p2_ref, a_ref, f_ref = refs[4 + nblk:10 + nblk]
        acc_ref, wv, wsem = refs[-3:]
        i = pl.program_id(0)
        j = pl.program_id(1)

        @pl.when((i == 0) & (j == 0))
        def _():
            for k in range(nj):
                pltpu.make_async_copy(w_hbm.at[k], wv.at[k], wsem.at[k]).start()

        @pl.when(i == 0)
        def _():
            pltpu.make_async_copy(w_hbm.at[j], wv.at[j], wsem.at[j]).wait()

        @pl.when(j == 0)
        def _():
            y, _, _ = _rms(h_ref[...], wpre_ref[...])
            n_ref[...] = y.astype(BF16)
            acc_ref[...] = jnp.zeros_like(acc_ref)

        if target is not None:
            dwpost_ref, sse_ref = refs[10 + nblk:12 + nblk]

            @pl.when((i == 0) & (j == 0))
            def _():
                dwpost_ref[...] = jnp.zeros_like(dwpost_ref)
                sse_ref[...] = jnp.zeros_like(sse_ref)

        n = n_ref[...]
        g = _dg(n, wv[j, 0:fj], NT)
        u = _dg(n, wv[j, fj:2 * fj], NT)
        sg = _sigmoid(g)
        silu = g * sg
        p1_ref[...] = (u * (sg + silu * (1.0 - sg))).astype(BF16)
        p2_ref[...] = silu.astype(BF16)
        a = (silu * u).astype(BF16)
        a_ref[...] = a
        acc_ref[...] += _dot(a, wv[j, 2 * fj:3 * fj])

        @pl.when(j == nj - 1)
        def _():
            f = acc_ref[...]
            wpost = wpost_ref[...]
            y, fh, r = _rms(f, wpost)
            hout = h_ref[...] + 0.5 * y
            if target is None:
                f_ref[...] = f
                hout_ref[...] = hout
            else:
                sse = jnp.zeros((1, 1), F32)
                errs = []
                for k in range(nblk):
                    err = hout[k * BLK:(k + 1) * BLK] - t_refs[k][...]
                    if k == 0:
                        err = jnp.where(i > 0, err, 0.0)
                    errs.append(err)
                    sse = sse + jnp.sum(jnp.sum(err * err, axis=1, keepdims=True), axis=0, keepdims=True)
                dy = (jnp.concatenate(errs, axis=0) if nblk > 1 else errs[0]) * (1.0 / D_MODEL)
                hout_ref[...] = dy
                df, dw = _rms_bwd(fh, r, wpost, 0.5 * dy)
                f_ref[...] = df.astype(BF16)
                dwpost_ref[...] += dw
                sse_ref[...] += jnp.broadcast_to(sse, sse_ref.shape)

    row = pl.BlockSpec((tm, D_MODEL), lambda i, j: (i, 0))
    vec = pl.BlockSpec((1, D_MODEL), lambda i, j: (0, 0))
    act = pl.BlockSpec((None, tm, fj), lambda i, j: (j, i, 0))
    t_specs = [pl.BlockSpec((BLK, D_MODEL), functools.partial(lambda i, j, k: (jnp.maximum(nblk * i + k - 1, 0), 0), k=k))
               for k in range(nblk)]
    loss_spec = [vec, _full((1, 128))] if target is not None else []
    loss_shape = [jax.ShapeDtypeStruct((1, D_MODEL), F32), jax.ShapeDtypeStruct((1, 128), F32)] if (
        target is not None) else []
    return _pallas(
        body, name="ffn_fwd", grid=(t // tm, nj),
        in_specs=[row, vec, ANY, vec] + t_specs,
        out_specs=[row, row, act, act, act, row] + loss_spec,
        out_shape=[jax.ShapeDtypeStruct((t, D_MODEL), F32), jax.ShapeDtypeStruct((t, D_MODEL), BF16),
                   jax.ShapeDtypeStruct((nj, t, fj), BF16), jax.ShapeDtypeStruct((nj, t, fj), BF16),
                   jax.ShapeDtypeStruct((nj, t, fj), BF16),
                   jax.ShapeDtypeStruct((t, D_MODEL), F32 if target is None else BF16)] + loss_shape,
        scratch_shapes=[pltpu.VMEM((tm, D_MODEL), F32), pltpu.VMEM((nj, rows3, D_MODEL), BF16),
                        pltpu.SemaphoreType.DMA((nj,))],
        args=(h, wpre, w3, wpost) + (target,) * nblk, hook=hook)


def _ffn_bwd(dhout, h, f, p14, p24, wpre, w3, wpost, df=None, split_first_block=False):
    t = h.shape[0]
    tm = _row_tile(t)
    ni = t // tm
    nj = w3.shape[0]
    fj, wspecs = _ffn_weight_specs(w3)
    have_df = df is not None
    assert not (have_df and split_first_block)

    def body(dhout_ref, h_ref, f_ref, p1_ref, p2_ref, wpre_ref, wg_ref, wu_ref, wd_ref, wpost_ref, *rest):
        if have_df:
            dh_ref, dg_ref, du_ref, dwpre_ref, dn_ref = rest
            df_ref = f_ref
        elif split_first_block:
            dh_ref, rest_hbm, df_ref, dg_ref, du_ref, dwpre_ref, dwpost_ref, dn_ref, dh_buf, dh_sem, first_sem = rest
        else:
            dh_ref, df_ref, dg_ref, du_ref, dwpre_ref, dwpost_ref, dn_ref = rest
        i = pl.program_id(0)
        j = pl.program_id(1)

        @pl.when((i == 0) & (j == 0))
        def _():
            dwpre_ref[...] = jnp.zeros_like(dwpre_ref)
            if not have_df:
                dwpost_ref[...] = jnp.zeros_like(dwpost_ref)

        @pl.when(j == 0)
        def _():
            if not have_df:
                wpost = wpost_ref[...]
                _, fh, r = _rms(f_ref[...], wpost)
                dfv, dw = _rms_bwd(fh, r, wpost, 0.5 * dhout_ref[...])
                dwpost_ref[...] += dw
                df_ref[...] = dfv.astype(BF16)
            dn_ref[...] = jnp.zeros_like(dn_ref)

        parts = _row_parts(tm)
        das = [_dg(df_ref[rows, :], wd_ref[...], NT) for rows in parts]
        for rows, da in zip(parts, das):
            dg = (da * p1_ref[rows, :].astype(F32)).astype(BF16)
            du = (da * p2_ref[rows, :].astype(F32)).astype(BF16)
            dg_ref[rows, :] = dg
            du_ref[rows, :] = du
            dn_ref[rows, :] += _dot(dg, wg_ref[...]) + _dot(du, wu_ref[...])

        @pl.when(j == nj - 1)
        def _():
            wpre = wpre_ref[...]
            _, hh, r = _rms(h_ref[...], wpre)
            dx, dw = _rms_bwd(hh, r, wpre, dn_ref[...])
            dwpre_ref[...] += dw
            dh = dhout_ref[...] + dx
            if not split_first_block:
                dh_ref[...] = dh
            else:
                slot = i % 2

                def to_rest(tile, sl):
                    rows = pl.ds(pl.multiple_of(tile * tm - BLK, 8), tm)
                    return pltpu.make_async_copy(dh_buf.at[sl], rest_hbm.at[rows], dh_sem.at[sl])

                first = pltpu.make_async_copy(dh_buf.at[0, BLK:tm], rest_hbm.at[0:tm - BLK], first_sem)

                @pl.when(i == 2)
                def _():
                    first.wait()

                @pl.when(i >= 3)
                def _():
                    to_rest(i, slot).wait()

                dh_buf[slot] = dh

                @pl.when(i == 0)
                def _():
                    dh_ref[...] = dh[0:BLK]
                    first.start()

                @pl.when(i > 0)
                def _():
                    to_rest(i, slot).start()

                @pl.when(i == ni - 1)
                def _():
                    if ni < 3:
                        first.wait()
                    if ni >= 3:
                        to_rest(i, 1 - slot).wait()
                    if ni >= 2:
                        to_rest(i, slot).wait()

    row = pl.BlockSpec((tm, D_MODEL), lambda i, j: (i, 0))
    vec = pl.BlockSpec((1, D_MODEL), lambda i, j: (0, 0))
    act = pl.BlockSpec((None, tm, fj), lambda i, j: (j, i, 0))
    actshape = jax.ShapeDtypeStruct((nj, t, fj), BF16)
    rowf, rowb, vecf = (jax.ShapeDtypeStruct((t, D_MODEL), F32), jax.ShapeDtypeStruct((t, D_MODEL), BF16),
                        jax.ShapeDtypeStruct((1, D_MODEL), F32))
    if have_df:
        out_specs, out_shape = [row, act, act, vec], [rowf, actshape, actshape, vecf]
    else:
        out_specs, out_shape = [row, row, act, act, vec, vec], [rowf, rowb, actshape, actshape, vecf, vecf]
    scratch = [pltpu.VMEM((tm, D_MODEL), F32)]
    if split_first_block:
        out_specs = [pl.BlockSpec((BLK, D_MODEL), lambda i, j: (0, 0)), ANY] + out_specs[1:]
        out_shape = [jax.ShapeDtypeStruct((BLK, D_MODEL), F32), jax.ShapeDtypeStruct((t - BLK, D_MODEL), F32)
                     ] + out_shape[1:]
        scratch += [pltpu.VMEM((2, tm, D_MODEL), F32), pltpu.SemaphoreType.DMA((2,)), pltpu.SemaphoreType.DMA]
    return _pallas(
        body, name="ffn_bwd", grid=(ni, nj),
        in_specs=[row, row, row, act, act, vec] + wspecs + [vec],
        out_specs=out_specs, out_shape=out_shape, scratch_shapes=scratch,
        args=(dhout, h, df if have_df else f, p14, p24, wpre, w3, w3, w3, wpost))


def _ffn_wgrad(n, df, dg4, du4, a4, hook=None):
    t = n.shape[0]
    tm = _contract_tile(t)
    ni = t // tm
    nj, _, fj = dg4.shape

    def body(n_ref, df_ref, dg_ref, du_ref, a_ref, dw_ref, acc):
        i = pl.program_id(1)

        @pl.when(i == 0)
        def _():
            acc[...] = jnp.zeros_like(acc)

        nn = n_ref[...]
        acc[0:fj, :] += _dg(dg_ref[...], nn, TN)
        acc[fj:2 * fj, :] += _dg(du_ref[...], nn, TN)
        acc[2 * fj:3 * fj, :] += _dg(a_ref[...], df_ref[...], TN)

        @pl.when(i == ni - 1)
        def _():
            dw_ref[...] = acc[...].astype(BF16)

    row = pl.BlockSpec((tm, D_MODEL), lambda j, i: (i, 0))
    act = pl.BlockSpec((None, tm, fj), lambda j, i: (j, i, 0))
    return _pallas(
        body, name="ffn_wgrad", grid=(nj, ni),
        in_specs=[row, row, act, act, act],
        out_specs=[pl.BlockSpec((None, 3 * fj, D_MODEL), lambda j, i: (j, 0, 0))],
        out_shape=[jax.ShapeDtypeStruct((nj, 3 * fj, D_MODEL), BF16)],
        scratch_shapes=[pltpu.VMEM((3 * fj, D_MODEL), F32)],
        args=(n, df, dg4, du4, a4), hook=hook)


def _embed_norm(x, meta_buf, w):
    t = x.shape[0] + BLK
    tm = _row_tile(t)
    nblk = tm // BLK
    ni = t // tm
    gather = _GatherChips([meta_buf])

    def body(*refs):
        x_refs = refs[:nblk]
        w_ref, mb_in, h_ref, n_ref, mb_out, mv, msem, send, recv = refs[nblk:]
        step = pl.program_id(0)
        tile = (step + 1) % ni
        hook_refs = ([mb_in], [mb_out], [send, recv])
        steps = (0, 1, ni - 2) if ni >= 3 else (0, 0, 0)
        for at, phase in zip(steps, (gather.start, gather.mid, gather.finish)):
            @pl.when(step == at)
            def _(phase=phase):
                phase(*hook_refs)

        @pl.when(step == 0)
        def _():
            mv[...] = jnp.zeros_like(mv)

        @pl.when(step == ni - 1)
        def _():
            cp = pltpu.make_async_copy(mb_out, mv, msem)
            cp.start()
            cp.wait()

        meta = jnp.concatenate([mv[k] for k in range(N_CHIPS)], axis=1)
        first = jnp.concatenate([jnp.zeros((PAD, D_MODEL), F32), meta], axis=0)
        blocks = [jnp.where(tile == 0, first, x_refs[0][...])] + [r[...] for r in x_refs[1:]]
        h = jnp.concatenate(blocks, axis=0) if nblk > 1 else blocks[0]
        h_ref[...] = h
        y, _, _ = _rms(h, w_ref[...])
        n_ref[...] = y.astype(BF16)

    x_specs = [pl.BlockSpec((BLK, D_MODEL), functools.partial(
        lambda i, k: (jnp.maximum(nblk * ((i + 1) % ni) + k - 1, 0), 0), k=k)) for k in range(nblk)]
    row = pl.BlockSpec((tm, D_MODEL), lambda i: ((i + 1) % ni, 0))
    h0, n0, _ = pl.pallas_call(
        body, name="embed_norm", grid=(ni,),
        in_specs=x_specs + [_full((1, D_MODEL)), ANY], out_specs=[row, row, ANY],
        out_shape=[jax.ShapeDtypeStruct((t, D_MODEL), F32), jax.ShapeDtypeStruct((t, D_MODEL), BF16),
                   jax.ShapeDtypeStruct(meta_buf.shape, meta_buf.dtype)],
        scratch_shapes=[pltpu.VMEM(meta_buf.shape, meta_buf.dtype), pltpu.SemaphoreType.DMA] + list(gather.scratch),
        input_output_aliases={nblk + 1: 2},
        compiler_params=_cparams(1),
    )(*([x] * nblk), w, meta_buf)
    return h0, n0


FWD_RELATION = (None, 0, 1, 2)


def _ffn_fwd_gather(h, n, wbuf, wpost, qc_idx, late):
    t = h.shape[0]
    tm = _row_tile(t)
    ni = t // tm
    nj, rows3, _ = wbuf.shape
    fj = rows3 // 3
    assert nj == N_CHIPS and ni >= 4
    wbufs = [wbuf]
    nw = 1
    n_lin, n_lout = len(late.inputs), len(late.out_shape)
    wait_step = ni - 3

    def body(qc_ref, h_ref, n_ref, wpost_ref, *rest):
        wb_in = rest[:nw]
        lins = rest[nw:nw + n_lin]
        o0 = nw + n_lin
        hout_ref, p1_ref, p2_ref, a_ref, f_hbm = rest[o0:o0 + 5]
        wb = rest[o0 + 5:o0 + 5 + nw]
        louts = rest[o0 + 5 + nw:o0 + 5 + nw + n_lout]
        s0 = o0 + 5 + nw + n_lout
        wv, wsem, send, recv, fbuf, fr_sem, fw_sem = rest[s0:s0 + 7]
        lscr = rest[s0 + 7:]
        p = pl.program_id(0)
        i = pl.program_id(1)
        step = p * ni + i
        fslot = step % 3
        nslot = (step + 1) % 3

        def f_tile(tile):
            return f_hbm.at[pl.ds(pl.multiple_of(tile * tm, 8), tm)]

        @pl.when(step > 1)
        def _():
            pltpu.make_async_copy(fbuf.at[nslot], f_tile(i), fw_sem.at[nslot]).wait()

        nxt = step + 1

        @pl.when((nxt < N_CHIPS * ni) & (nxt >= ni))
        def _():
            pltpu.make_async_copy(f_tile(nxt % ni), fbuf.at[nslot], fr_sem.at[nslot]).start()

        @pl.when(p > 0)
        def _():
            pltpu.make_async_copy(f_tile(i), fbuf.at[fslot], fr_sem.at[fslot]).wait()
        x, y, c, chips = _place()
        q = 2 * x + y
        sibling = (x, y, 1 - c)
        mine, other = _half(rows3, c), _half(rows3, 1 - c)

        def load(chunk, slot, src):
            return [pltpu.make_async_copy(src[t].at[chunk], wv.at[slot, t], wsem.at[slot, t]) for t in range(nw)]

        @pl.when((p == 0) & (i == 0))
        def _():
            for j, (cx, cy) in enumerate(chips):
                for t in range(nw):
                    _remote(send.at[t, j], recv.at[t, j], wb_in[t].at[q, mine], wb[t].at[q, mine], (cx, cy, c)).start()
            for cp in load(q, 0, wb_in):
                cp.start()
            for cp in load(q, 0, wb_in):
                cp.wait()

        @pl.when((p == 1) & (i == 0))
        def _():
            late.start(lins, louts, lscr)

        for pp in range(1, N_CHIPS):
            j = FWD_RELATION[pp]
            cx, cy = chips[j]
            chunk = 2 * cx + cy

            @pl.when((p == pp - 1) & (i == wait_step))
            def _(j=j, cx=cx, cy=cy, chunk=chunk, pp=pp):
                for t in range(nw):
                    got = wb[t].at[chunk, mine]
                    _remote(send.at[t, j], recv.at[t, j], got, got, (cx, cy, c)).wait_recv()
                    _remote(send.at[t, 3 + j], recv.at[t, 3 + j], got, got, sibling).start()
                for t in range(nw):
                    rest_half = wb[t].at[chunk, other]
                    _remote(send.at[t, 3 + j], recv.at[t, 3 + j], rest_half, rest_half, sibling).wait_recv()
                for cp in load(chunk, pp % 2, wb):
                    cp.start()

            @pl.when((p == pp) & (i == 0))
            def _(chunk=chunk, pp=pp):
                for cp in load(chunk, pp % 2, wb):
                    cp.wait()

        @pl.when((p == N_CHIPS - 1) & (i == ni // 2))
        def _():
            late.mid(lins, louts, lscr)

        slot = p % 2
        nn = n_ref[...]
        g = _dg(nn, wv[slot, 0, 0:fj], NT)
        u = _dg(nn, wv[slot, 0, fj:2 * fj], NT)
        sg = _sigmoid(g)
        silu = g * sg
        p1_ref[...] = (u * (sg + silu * (1.0 - sg))).astype(BF16)
        p2_ref[...] = silu.astype(BF16)
        a = (silu * u).astype(BF16)
        a_ref[...] = a
        part = _dot(a, wv[slot, 0, 2 * fj:3 * fj])

        @pl.when(p == 0)
        def _():
            fbuf[fslot] = part

        @pl.when(p > 0)
        def _():
            fbuf[fslot] = fbuf[fslot] + part

        pltpu.make_async_copy(fbuf.at[fslot], f_tile(i), fw_sem.at[fslot]).start()

        @pl.when(p == N_CHIPS - 1)
        def _():
            yv, _, _ = _rms(fbuf[fslot], wpost_ref[...])
            hout_ref[...] = h_ref[...] + 0.5 * yv

        @pl.when((p == N_CHIPS - 1) & (i == ni - 1))
        def _():
            pslot = (step + 2) % 3
            pltpu.make_async_copy(fbuf.at[pslot], f_tile(i), fw_sem.at[pslot]).wait()
            pltpu.make_async_copy(fbuf.at[fslot], f_tile(i), fw_sem.at[fslot]).wait()
            for t in range(nw):
                for j, (cx, cy) in enumerate(chips):
                    sent = wb[t].at[2 * cx + cy, mine]
                    _remote(send.at[t, j], recv.at[t, j], sent, sent, (cx, cy, c)).wait_send()
                    _remote(send.at[t, 3 + j], recv.at[t, 3 + j], sent, sent, sibling).wait_send()
            late.finish(lins, louts, lscr)

    def last_pass_rows(p, i, qc_ref):
        return (jnp.where(p == N_CHIPS - 1, i, 0), 0)

    def chunk_rows(p, i, qc_ref):
        order = ((p & 1) << 1) | (p >> 1)
        return (jnp.bitwise_xor(qc_ref[0], order), i, 0)

    row = pl.BlockSpec((tm, D_MODEL), lambda p, i, qc_ref: (i, 0))
    last_row = pl.BlockSpec((tm, D_MODEL), last_pass_rows)
    act = pl.BlockSpec((None, tm, fj), chunk_rows)
    act_shape = jax.ShapeDtypeStruct((nj, t, fj), BF16)
    res = pl.pallas_call(
        body, name="ffn_fwd_gather",
        grid_spec=pltpu.PrefetchScalarGridSpec(
            num_scalar_prefetch=1, grid=(N_CHIPS, ni),
            in_specs=[last_row, row, pl.BlockSpec((1, D_MODEL), lambda p, i, qc_ref: (0, 0))]
            + [ANY] * (nw + n_lin),
            out_specs=[last_row, act, act, act, ANY] + [ANY] * (nw + n_lout),
            scratch_shapes=[pltpu.VMEM((2, nw, rows3, D_MODEL), BF16), pltpu.SemaphoreType.DMA((2, nw)),
                            pltpu.SemaphoreType.DMA((nw, 6)), pltpu.SemaphoreType.DMA((nw, 6)),
                            pltpu.VMEM((3, tm, D_MODEL), F32), pltpu.SemaphoreType.DMA((3,)),
                            pltpu.SemaphoreType.DMA((3,))] + list(late.scratch)),
        out_shape=[jax.ShapeDtypeStruct((t, D_MODEL), F32), act_shape, act_shape, act_shape,
                   jax.ShapeDtypeStruct((t, D_MODEL), F32)]
        + [jax.ShapeDtypeStruct(b.shape, b.dtype) for b in wbufs] + list(late.out_shape),
        input_output_aliases={**{4 + t: 5 + t for t in range(nw)},
                              **{4 + nw + a: 5 + nw + b for a, b in late.aliases}},
        compiler_params=_cparams(2),
    )(qc_idx, h, n, wpost, *wbufs, *late.inputs)
    return res[:5], res[5:5 + nw], res[5 + nw:]


PASS_RELATION = (2, 0, 1)


def _ffn_wgrad_reduce(n, df, dg4, du4, a4, qc_idx, hook):
    t = n.shape[0]
    tm = _contract_tile(t)
    ni = t // tm
    nj, _, fj = dg4.shape
    assert nj == N_CHIPS
    hrows = 3 * fj // 2
    n_hin, n_hout = len(hook.inputs), len(hook.out_shape)

    def body(qc_ref, n_ref, df_ref, dg_ref, du_ref, a_ref, *rest):
        hins = rest[:n_hin]
        own_ref, others_ref = rest[n_hin:n_hin + 2]
        houts = rest[n_hin + 2:n_hin + 2 + n_hout]
        s0 = n_hin + 2 + n_hout
        acc, stage, land, sumbuf, px_send, px_recv, cs_send, cs_recv, own_sem = rest[s0:s0 + 9]
        hscr = rest[s0 + 9:]
        k_pass = pl.program_id(0)
        i = pl.program_id(1)
        x, y, c, chips = _place()
        mine = pl.ds(pl.multiple_of(c * hrows, 8), hrows)
        other = pl.ds(pl.multiple_of((1 - c) * hrows, 8), hrows)

        def to_owner(k):
            j = PASS_RELATION[k]
            return _remote(cs_send.at[j], cs_recv.at[j], sumbuf.at[k % 2], others_ref.at[j], (*chips[j], c))

        @pl.when((k_pass == 0) & (i == 0))
        def _():
            hook.start(hins, houts, hscr)

        if hook.has_mid:
            @pl.when((k_pass == N_CHIPS // 2) & (i == 0))
            def _():
                hook.mid(hins, houts, hscr)

        @pl.when(i == 0)
        def _():
            acc[...] = jnp.zeros_like(acc)

        def add_gate():
            acc[0:fj, :] += _dg(dg_ref[...], n_ref[...], TN)

        def add_up():
            acc[fj:2 * fj, :] += _dg(du_ref[...], n_ref[...], TN)

        def add_down():
            acc[2 * fj:3 * fj, :] += _dg(a_ref[...], df_ref[...], TN)

        @pl.when(i < ni - 1)
        def _():
            add_gate()
            add_up()
            add_down()

        for cc, order in ((0, (add_up, add_down, add_gate)), (1, (add_gate, add_up, add_down))):
            @pl.when((i == ni - 1) & (c == cc))
            def _(order=order):
                order[0]()
                order[1]()
                stage[...] = acc[other, :].astype(BF16)
                _remote(px_send.at[k_pass], px_recv.at[k_pass], stage, land.at[k_pass % 2], (x, y, 1 - c)).start()
                order[2]()

        for k in range(N_CHIPS):
            @pl.when((k_pass == k) & (i == ni - 1))
            def _(k=k):
                slot = k % 2
                swap = _remote(px_send.at[k], px_recv.at[k], stage, land.at[slot], (x, y, 1 - c))
                swap.wait_recv()
                pair = acc[mine, :] + land[slot].astype(F32)
                if k >= 2:
                    to_owner(k - 2).wait_send()
                sumbuf[slot] = pair.astype(BF16)
                swap.wait_send()
                if k < N_CHIPS - 1:
                    to_owner(k).start()
                else:
                    keep = pltpu.make_async_copy(sumbuf.at[slot], own_ref, own_sem)
                    keep.start()
                    for j in range(N_CHIPS - 1):
                        _remote(cs_send.at[j], cs_recv.at[j], sumbuf.at[0], others_ref.at[j], (*chips[j], c)).wait_recv()
                    to_owner(k - 1).wait_send()
                    keep.wait()
                    hook.finish(hins, houts, hscr)

    def chunk(k_pass, i, qc_ref):
        return (jnp.bitwise_xor(qc_ref[0], N_CHIPS - 1 - k_pass), i, 0)

    row = pl.BlockSpec((tm, D_MODEL), lambda k_pass, i, qc_ref: (i, 0))
    act = pl.BlockSpec((None, tm, fj), chunk)
    res = pl.pallas_call(
        body, name="ffn_wgrad_reduce",
        grid_spec=pltpu.PrefetchScalarGridSpec(
            num_scalar_prefetch=1, grid=(N_CHIPS, ni),
            in_specs=[row, row, act, act, act] + [ANY] * n_hin,
            out_specs=[ANY, ANY] + [ANY] * n_hout,
            scratch_shapes=[pltpu.VMEM((3 * fj, D_MODEL), F32), pltpu.VMEM((hrows, D_MODEL), BF16),
                            pltpu.VMEM((2, hrows, D_MODEL), BF16), pltpu.VMEM((2, hrows, D_MODEL), BF16),
                            pltpu.SemaphoreType.DMA((N_CHIPS,)), pltpu.SemaphoreType.DMA((N_CHIPS,)),
                            pltpu.SemaphoreType.DMA((N_CHIPS - 1,)), pltpu.SemaphoreType.DMA((N_CHIPS - 1,)),
                            pltpu.SemaphoreType.DMA] + list(hook.scratch)),
        out_shape=[jax.ShapeDtypeStruct((hrows, D_MODEL), BF16),
                   jax.ShapeDtypeStruct((N_CHIPS - 1, hrows, D_MODEL), BF16)] + list(hook.out_shape),
        compiler_params=_cparams(2),
    )(qc_idx, n, df, dg4, du4, a4, *hook.inputs)
    return res[0], res[1], res[2:]


def _xty(x, y):
    t, k = x.shape
    n = y.shape[1]
    tm = _contract_tile(t)
    tn = n if n <= 1024 else (896 if n % 896 == 0 else 128)
    steps = t // tm

    def body(x_ref, y_ref, o_ref, acc_ref):
        i = pl.program_id(1)
        part = _dg(x_ref[...], y_ref[...], TN)

        @pl.when(i == 0)
        def _():
            acc_ref[...] = part

        @pl.when(jnp.logical_and(i > 0, i < steps - 1))
        def _():
            acc_ref[...] += part

        @pl.when(i == steps - 1)
        def _():
            o_ref[...] = (acc_ref[...] + part).astype(BF16)

    assert steps > 1
    return pl.pallas_call(
        body, name="xty", grid=(n // tn, steps),
        in_specs=[pl.BlockSpec((tm, k), lambda j, i: (i, 0)), pl.BlockSpec((tm, tn), lambda j, i: (i, j))],
        out_specs=pl.BlockSpec((k, tn), lambda j, i: (0, j)),
        out_shape=jax.ShapeDtypeStruct((k, n), BF16),
        scratch_shapes=[pltpu.VMEM((k, tn), F32)],
        compiler_params=_cparams(2),
    )(x, y)


def _rope_tables(t):
    pos = (jnp.arange(t, dtype=jnp.int32) - PAD).astype(F32)
    inv_freq = 1.0 / (ROPE_THETA ** (jnp.arange(0, SWA_HD, 2, dtype=F32) / SWA_HD))
    half = SWA_HD // 2
    ang = pos[:, None] * jnp.tile(inv_freq, 4)[None, :]
    sign = jnp.tile(jnp.concatenate([-jnp.ones((half,), F32), jnp.ones((half,), F32)]), 2)
    return jnp.cos(ang), jnp.sin(ang) * sign[None, :]


def _rot_half(x, first_half):
    return jnp.where(first_half, pltpu.roll(x, 96, 1), pltpu.roll(x, 32, 1))


def _first_half_mask(rows):
    lane = lax.broadcasted_iota(jnp.int32, (rows, 128), 1)
    return (lane % 64) < 32


def _log_sigmoid(z):
    return jnp.minimum(z, 0.0) - jnp.log(1.0 + jnp.exp(-jnp.abs(z)))


def _mix_proj(h1, wmixpre, winp, wa2p, bap, cos, sin):
    t = h1.shape[0]
    tm = _row_tile(t)

    def body(h_ref, w_ref, win_ref, wa2_ref, ba_ref, cos_ref, sin_ref,
             n_ref, gq_ref, gk_ref, gv_ref, gg_ref, ga_ref, la_ref, sq_ref, sk_ref, sv_ref):
        y, _, _ = _rms(h_ref[...], w_ref[...])
        n = y.astype(BF16)
        n_ref[...] = n
        proj = _dot(n, win_ref[...])
        gq_ref[...] = proj[:, P_GQ:P_GK]
        gk_ref[...] = proj[:, P_GK:P_GV]
        gv_ref[...] = proj[:, P_GV:P_GG]
        gg_ref[...] = proj[:, P_GG:P_GA]
        ga = proj[:, P_GA:P_SQ]
        ga_ref[...] = ga
        z = _dot(ga.astype(BF16), wa2_ref[...]) + ba_ref[...]
        la_ref[...] = _log_sigmoid(z) * (1.0 / GLA_TAU)
        c = cos_ref[...]
        s = sin_ref[...]
        fh = _first_half_mask(tm)
        for k in range(4):
            x = proj[:, P_SQ + 128 * k:P_SQ + 128 * (k + 1)]
            sq_ref[:, 128 * k:128 * (k + 1)] = (x * c + _rot_half(x, fh) * s).astype(BF16)
        for k in range(2):
            x = proj[:, P_SK + 128 * k:P_SK + 128 * (k + 1)]
            sk_ref[:, 128 * k:128 * (k + 1)] = (x * c + _rot_half(x, fh) * s).astype(BF16)
        sv_ref[...] = proj[:, P_SV:P_END].astype(BF16)

    def row(w):
        return pl.BlockSpec((tm, w), lambda i: (i, 0))

    def rshape(w, dt):
        return jax.ShapeDtypeStruct((t, w), dt)

    return pl.pallas_call(
        body, name="mix_proj", grid=(t // tm,),
        in_specs=[row(D_MODEL), _full((1, D_MODEL)), _full((D_MODEL, P_END)), _full((128, GLA_KW)),
                  _full((1, GLA_KW)), row(128), row(128)],
        out_specs=[row(D_MODEL), row(256), row(256), row(512), row(512), row(128), row(256), row(512), row(256),
                   row(256)],
        out_shape=[rshape(D_MODEL, BF16), rshape(256, F32), rshape(256, F32), rshape(512, F32), rshape(512, F32),
                   rshape(128, F32), rshape(256, F32), rshape(512, BF16), rshape(256, BF16), rshape(256, BF16)],
        compiler_params=_cparams(1),
    )(h1, wmixpre, winp, wa2p, bap, cos, sin)


def _scan_rows(x, reverse=False):
    n = x.shape[0]
    row = lax.broadcasted_iota(jnp.int32, x.shape, 0)
    s = 1
    while s < n:
        if reverse:
            x = x + jnp.where(row < n - s, pltpu.roll(x, n - s, 0), 0.0)
        else:
            x = x + jnp.where(row >= s, pltpu.roll(x, s, 0), 0.0)
        s *= 2
    return x


def _gla_cumsum(la, tril_f):
    b = _scan_rows(la)
    row = lax.broadcasted_iota(jnp.int32, b.shape, 0)
    bm = jnp.sum(jnp.where(row == GLA_CHUNK // 2 - 1, b, 0.0), axis=0, keepdims=True)
    bl = jnp.sum(jnp.where(row == GLA_CHUNK - 1, b, 0.0), axis=0, keepdims=True)
    return b, bm, bl


def _gla_decays(la, tril_f):
    b, bm, bl = _gla_cumsum(la, tril_f)
    return jnp.exp(b - bm), jnp.exp(bm - b), jnp.exp(b), jnp.exp(bl - b), jnp.exp(bl)


def _gla_masks():
    c = GLA_CHUNK
    r = lax.broadcasted_iota(jnp.int32, (c, c), 0)
    col = lax.broadcasted_iota(jnp.int32, (c, c), 1)
    r4 = lax.broadcasted_iota(jnp.int32, (GLA_HEADS * c, c), 0) % c
    c4 = lax.broadcasted_iota(jnp.int32, (GLA_HEADS * c, c), 1)
    klane = lax.broadcasted_iota(jnp.int32, (c, GLA_KW), 1) // GLA_DK
    vlane = lax.broadcasted_iota(jnp.int32, (c, GLA_W), 1) // GLA_DV
    srow = lax.broadcasted_iota(jnp.int32, (GLA_W, GLA_KW), 0) // GLA_DV
    scol = lax.broadcasted_iota(jnp.int32, (GLA_W, GLA_KW), 1) // GLA_DK
    return dict(tril_f=(r >= col).astype(F32), triu_f=(r <= col).astype(F32), tril4=r4 >= c4,
                khead=[klane == h for h in range(GLA_HEADS)], vhead=[vlane == h for h in range(GLA_HEADS)],
                diag=srow == scol)


def _stack_heads(x, head_masks):
    return jnp.concatenate([jnp.where(m, x, 0.0) for m in head_masks], axis=0)


def _gla_fwd(gq, gk, gv, la):
    t = gq.shape[0]
    rg = _seq_tile(t)
    nb = t // rg
    ncb = rg // GLA_CHUNK
    c = GLA_CHUNK

    def body(q_ref, k_ref, v_ref, la_ref, o_ref, ss_ref, st_ref):
        @pl.when(pl.program_id(0) == 0)
        def _():
            st_ref[...] = jnp.zeros_like(st_ref)

        mk = _gla_masks()
        st = st_ref[...]
        for ch in range(ncb):
            rows = slice(ch * c, (ch + 1) * c)
            eq, ek, eb, ekl, ebl = _gla_decays(la_ref[rows, :], mk["tril_f"])
            qs = q_ref[rows, :] * (GLA_DK ** -0.5)
            k = k_ref[rows, :]
            v = v_ref[rows, :].astype(BF16)
            ss_ref[ch] = st
            q4 = _stack_heads(qs * eq, mk["khead"]).astype(BF16)
            a4 = jnp.where(mk["tril4"], _dg(q4, (k * ek).astype(BF16), NT), 0.0).astype(BF16)
            r4 = _dot(a4, v)
            intra = jnp.concatenate([r4[h * c:(h + 1) * c, GLA_DV * h:GLA_DV * (h + 1)] for h in range(GLA_HEADS)],
                                    axis=1)
            o_ref[rows, :] = intra + _dg((qs * eb).astype(BF16), st.astype(BF16), NT)
            st = st * ebl + jnp.where(mk["diag"], _dg(v, (k * ekl).astype(BF16), TN), 0.0)
        st_ref[...] = st

    def row(w):
        return pl.BlockSpec((rg, w), lambda i: (i, 0))

    return pl.pallas_call(
        body, name="gla_fwd", grid=(nb,),
        in_specs=[row(256), row(256), row(512), row(256)],
        out_specs=[row(512), pl.BlockSpec((ncb, GLA_W, GLA_KW), lambda i: (i, 0, 0))],
        out_shape=[jax.ShapeDtypeStruct((t, GLA_W), F32), jax.ShapeDtypeStruct((nb * ncb, GLA_W, GLA_KW), F32)],
        scratch_shapes=[pltpu.VMEM((GLA_W, GLA_KW), F32)],
        compiler_params=_cparams(1),
    )(gq, gk, gv, la)


def _gla_bwd(gq, gk, gv, la, ss, do):
    t = gq.shape[0]
    rg = _seq_tile(t)
    nb = t // rg
    ncb = rg // GLA_CHUNK
    c = GLA_CHUNK

    def body(q_ref, k_ref, v_ref, la_ref, ss_ref, do_ref, dq_ref, dk_ref, dv_ref, dla_ref, dst_ref):
        @pl.when(pl.program_id(0) == 0)
        def _():
            dst_ref[...] = jnp.zeros_like(dst_ref)

        mk = _gla_masks()
        last_row = lax.broadcasted_iota(jnp.int32, (c, GLA_KW), 0) == c - 1
        scale = GLA_DK ** -0.5
        dstn = dst_ref[...]
        for ch in reversed(range(ncb)):
            rows = slice(ch * c, (ch + 1) * c)
            eq, ek, eb, ekl, ebl = _gla_decays(la_ref[rows, :], mk["tril_f"])
            qs = q_ref[rows, :] * scale
            k = k_ref[rows, :]
            qt, kt, qh, kh = qs * eq, k * ek, qs * eb, k * ekl
            ktb, khb, qhb = kt.astype(BF16), kh.astype(BF16), qh.astype(BF16)
            v = v_ref[rows, :].astype(BF16)
            do_f = do_ref[rows, :]
            dob = do_f.astype(BF16)
            st = ss_ref[ch]
            stb = st.astype(BF16)
            dstb = dstn.astype(BF16)
            q4 = _stack_heads(qt, mk["khead"]).astype(BF16)
            do4 = _stack_heads(do_f, mk["vhead"]).astype(BF16)
            a4 = jnp.where(mk["tril4"], _dg(q4, ktb, NT), 0.0).astype(BF16)
            da4 = jnp.where(mk["tril4"], _dg(do4, v, NT), 0.0).astype(BF16)
            dv_ref[rows, :] = _dg(a4, do4, TN) + _dg(khb, dstb, NT)
            dq4 = _dot(da4, ktb)
            dqt = jnp.zeros((c, GLA_KW), F32)
            for h in range(GLA_HEADS):
                dqt = dqt + jnp.where(mk["khead"][h], dq4[h * c:(h + 1) * c], 0.0)
            dkt = _dg(da4, q4, TN)
            dqh = _dot(dob, stb)
            dkh = _dot(v, dstb)
            dbl = jnp.sum(dstn * st, axis=0, keepdims=True)
            dstn = dstn * ebl + jnp.where(mk["diag"], _dg(dob, qhb, TN), 0.0)
            dq_ref[rows, :] = scale * (dqt * eq + dqh * eb)
            dk_ref[rows, :] = dkt * ek + dkh * ekl
            dkk = dkh * kh
            db = dqt * qt - dkt * kt + dqh * qh - dkk
            db = db + jnp.where(last_row, jnp.sum(dkk, axis=0, keepdims=True) + ebl * dbl, 0.0)
            dla_ref[rows, :] = _scan_rows(db, reverse=True)
        dst_ref[...] = dstn

    def row(w):
        return pl.BlockSpec((rg, w), lambda i: (nb - 1 - i, 0))

    def rshape(w):
        return jax.ShapeDtypeStruct((t, w), F32)

    return pl.pallas_call(
        body, name="gla_bwd", grid=(nb,),
        in_specs=[row(256), row(256), row(512), row(256),
                  pl.BlockSpec((ncb, GLA_W, GLA_KW), lambda i: (nb - 1 - i, 0, 0)), row(512)],
        out_specs=[row(256), row(256), row(512), row(256)],
        out_shape=[rshape(256), rshape(256), rshape(512), rshape(256)],
        scratch_shapes=[pltpu.VMEM((GLA_W, GLA_KW), F32)],
        compiler_params=_cparams(1),
    )(gq, gk, gv, la, ss, do)


SWA_G = SWA_QH // SWA_KVH


def _swa_bias():
    n = jnp.arange(3, dtype=jnp.int32)[:, None, None]
    r = (jnp.arange(SWA_G * BLK, dtype=jnp.int32) % BLK)[None, :, None]
    c = jnp.arange(3 * BLK, dtype=jnp.int32)[None, None, :]
    seg = c // BLK
    cc = c % BLK
    qpos = n * BLK + r - PAD
    kpos = jnp.where(seg == 0, (n - 1) * BLK, jnp.where(seg == 1, n * BLK, 0)) + cc - PAD
    band = (seg < 2) & (kpos >= N_META) & (kpos <= qpos) & (qpos - kpos < WINDOW)
    meta = (seg == 2) & (kpos >= 0) & (kpos < N_META) & (kpos <= qpos)
    return jnp.where(band | meta, 0.0, NEG_INF).astype(F32)


def _swa_stack(ref, rows, kh, lo, dtype):
    parts = []
    for g in range(2):
        pair = ref[rows, 128 * (2 * kh + g):128 * (2 * kh + g + 1)]
        zero = jnp.zeros_like(pair)
        parts += [jnp.where(lo, pair, zero), jnp.where(lo, zero, pair)]
    return jnp.concatenate(parts, axis=0).astype(dtype)


def _swa_unstack(x4, lo):
    return [jnp.where(lo, x4[2 * g * BLK:(2 * g + 1) * BLK], x4[(2 * g + 1) * BLK:(2 * g + 2) * BLK])
            for g in range(2)]


def _swa_sink_col(sink_ref, kh):
    blk = lax.broadcasted_iota(jnp.int32, (SWA_G * BLK, 1), 0) // BLK
    col = jnp.full((SWA_G * BLK, 1), sink_ref[SWA_G * kh + SWA_G - 1], F32)
    for e in reversed(range(SWA_G - 1)):
        col = jnp.where(blk == e, sink_ref[SWA_G * kh + e], col)
    return col


def _swa_softmax(qk, bias, sink):
    s = qk * (SWA_HD ** -0.5) + bias
    m = jnp.maximum(jnp.max(s, axis=-1, keepdims=True), sink)
    p = jnp.exp(s - m)
    es = jnp.exp(sink - m)
    inv = 1.0 / (jnp.sum(p, axis=-1, keepdims=True) + es)
    return p * inv, es * inv


def _swa_keys(prev_ref, cur_ref, first_ref, b, ls):
    before = prev_ref[:, ls] if b == 0 else cur_ref[(b - 1) * BLK:b * BLK, ls]
    return jnp.concatenate([before, cur_ref[b * BLK:(b + 1) * BLK, ls], first_ref[:, ls]], axis=0)


def _swa_specs(rs, ns):
    bps = rs // BLK
    cur = lambda w: pl.BlockSpec((rs, w), lambda i: (jnp.minimum(i, ns - 1), 0))
    prev = lambda w: pl.BlockSpec((BLK, w), lambda i: (jnp.maximum(jnp.minimum(i, ns - 1) * bps - 1, 0), 0))
    first = lambda w: pl.BlockSpec((BLK, w), lambda i: (0, 0))
    return cur, prev, first


def _swa_fwd(sinks, sq, sk, sv):
    t = sq.shape[0]
    rs = _seq_tile(t)
    bps, ns = rs // BLK, t // rs

    def body(sink_ref, bias_ref, q_ref, kp_ref, kc_ref, km_ref, vp_ref, vc_ref, vm_ref, o_ref):
        i = pl.program_id(0)
        lo = lax.broadcasted_iota(jnp.int32, (BLK, 128), 1) < 64
        sink_cols = [_swa_sink_col(sink_ref, kh) for kh in range(SWA_KVH)]
        chains = [(b, kh) for b in range(bps) for kh in range(SWA_KVH)]
        scores = []
        for b, kh in chains:
            ls = slice(128 * kh, 128 * (kh + 1))
            q4 = _swa_stack(q_ref, slice(b * BLK, (b + 1) * BLK), kh, lo, BF16)
            scores.append(_dg(q4, _swa_keys(kp_ref, kc_ref, km_ref, b, ls), NT))
        probs = []
        for (b, kh), s in zip(chains, scores):
            p, _ = _swa_softmax(s, bias_ref[jnp.minimum(i * bps + b, 2)], sink_cols[kh])
            probs.append(p.astype(BF16))
        for (b, kh), p in zip(chains, probs):
            ls = slice(128 * kh, 128 * (kh + 1))
            rows = slice(b * BLK, (b + 1) * BLK)
            for g, pair in enumerate(_swa_unstack(_dot(p, _swa_keys(vp_ref, vc_ref, vm_ref, b, ls)), lo)):
                o_ref[rows, 128 * (2 * kh + g):128 * (2 * kh + g + 1)] = pair

    cur, prev, first = _swa_specs(rs, ns)
    bias = _swa_bias()
    return pl.pallas_call(
        body, name="swa_fwd", grid=(ns,),
        in_specs=[pl.BlockSpec(memory_space=pltpu.SMEM), _full(bias.shape), cur(512), prev(256), cur(256), first(256),
                  prev(256), cur(256), first(256)],
        out_specs=cur(512),
        out_shape=jax.ShapeDtypeStruct((t, SWA_W), F32),
        compiler_params=_cparams(1),
    )(sinks, bias, sq, sk, sk, sk, sv, sv, sv)


def _swa_bwd(sinks, sq, sk, sv, o, do, hook=None):
    t = sq.shape[0]
    rs = _seq_tile(t)
    bps, ns = rs // BLK, t // rs

    def body(sink_ref, bias_ref, q_ref, kp_ref, kc_ref, km_ref, vp_ref, vc_ref, vm_ref, o_ref, do_ref,
             dq_ref, dk_ref, dv_ref, dkm_ref, dvm_ref, dsink_ref, pk_ref, pv_ref):
        i = pl.program_id(0)

        @pl.when(i == 0)
        def _():
            pk_ref[...] = jnp.zeros_like(pk_ref)
            pv_ref[...] = jnp.zeros_like(pv_ref)
            dkm_ref[...] = jnp.zeros_like(dkm_ref)
            dvm_ref[...] = jnp.zeros_like(dvm_ref)
            dsink_ref[...] = jnp.zeros_like(dsink_ref)

        @pl.when(i == ns)
        def _():
            dk_ref[...] = pk_ref[...]
            dv_ref[...] = pv_ref[...]

        @pl.when(i < ns)
        def _():
            lo = lax.broadcasted_iota(jnp.int32, (BLK, 128), 1) < 64
            scale = SWA_HD ** -0.5
            sink_cols = [_swa_sink_col(sink_ref, kh) for kh in range(SWA_KVH)]
            parts_k = [[None] * SWA_KVH for _ in range(bps)]
            parts_v = [[None] * SWA_KVH for _ in range(bps)]
            dsinks = [jnp.zeros((1, 1), F32) for _ in range(SWA_QH)]
            chains = [(b, kh) for b in range(bps) for kh in range(SWA_KVH)]
            lanes = lambda kh: slice(128 * kh, 128 * (kh + 1))
            block = lambda b: slice(b * BLK, (b + 1) * BLK)
            q4s = [_swa_stack(q_ref, block(b), kh, lo, BF16) for b, kh in chains]
            scores = [_dg(q4, _swa_keys(kp_ref, kc_ref, km_ref, b, lanes(kh)), NT)
                      for (b, kh), q4 in zip(chains, q4s)]
            do4s = [_swa_stack(do_ref, block(b), kh, lo, F32) for b, kh in chains]
            do4bs = [d.astype(BF16) for d in do4s]
            dps = [_dg(d, _swa_keys(vp_ref, vc_ref, vm_ref, b, lanes(kh)), NT) for (b, kh), d in zip(chains, do4bs)]
            pbs, dss = [], []
            for n_chain, (b, kh) in enumerate(chains):
                p, psink = _swa_softmax(scores[n_chain], bias_ref[jnp.minimum(i * bps + b, 2)], sink_cols[kh])
                delta = jnp.sum(do4s[n_chain] * _swa_stack(o_ref, block(b), kh, lo, F32), axis=-1, keepdims=True)
                dss.append((p * (dps[n_chain] - delta) * scale).astype(BF16))
                pbs.append(p.astype(BF16))
                dsk = psink * delta
                for e in range(SWA_G):
                    h = SWA_G * kh + e
                    dsinks[h] = dsinks[h] - jnp.sum(dsk[e * BLK:(e + 1) * BLK], axis=0, keepdims=True)
            for n_chain, (b, kh) in enumerate(chains):
                kall = _swa_keys(kp_ref, kc_ref, km_ref, b, lanes(kh))
                for g, pair in enumerate(_swa_unstack(_dot(dss[n_chain], kall), lo)):
                    dq_ref[block(b), 128 * (2 * kh + g):128 * (2 * kh + g + 1)] = pair
                parts_k[b][kh] = _dg(dss[n_chain], q4s[n_chain], TN)
                parts_v[b][kh] = _dg(pbs[n_chain], do4bs[n_chain], TN)
            last = slice(rs - BLK, rs)
            for parts, out_ref, pend_ref, meta_ref in ((parts_k, dk_ref, pk_ref, dkm_ref),
                                                       (parts_v, dv_ref, pv_ref, dvm_ref)):
                for kh in range(SWA_KVH):
                    ls = slice(128 * kh, 128 * (kh + 1))
                    if bps > 1:
                        out_ref[0:rs - BLK, ls] = pend_ref[0:rs - BLK, ls]
                    out_ref[last, ls] = pend_ref[last, ls] + parts[0][kh][0:BLK]
                    meta = parts[0][kh][2 * BLK:3 * BLK]
                    for b in range(bps):
                        own = parts[b][kh][BLK:2 * BLK]
                        if b + 1 < bps:
                            own = own + parts[b + 1][kh][0:BLK]
                            meta = meta + parts[b + 1][kh][2 * BLK:3 * BLK]
                        pend_ref[b * BLK:(b + 1) * BLK, ls] = own
                    meta_ref[:, ls] += meta
            for h in range(SWA_QH):
                dsink_ref[h:h + 1, :] += jnp.broadcast_to(dsinks[h], (1, 128))

    cur, prev, first = _swa_specs(rs, ns)
    late = lambda w: pl.BlockSpec((rs, w), lambda i: (jnp.maximum(i - 1, 0), 0))
    bias = _swa_bias()
    return _pallas(
        body, name="swa_bwd", grid=(ns + 1,),
        in_specs=[pl.BlockSpec(memory_space=pltpu.SMEM), _full(bias.shape), cur(512), prev(256), cur(256), first(256),
                  prev(256), cur(256), first(256), cur(512), cur(512)],
        out_specs=[cur(512), late(256), late(256), first(256), first(256), _full((SWA_QH, 128))],
        out_shape=[jax.ShapeDtypeStruct((t, SWA_W), F32), jax.ShapeDtypeStruct((t, 256), F32),
                   jax.ShapeDtypeStruct((t, 256), F32), jax.ShapeDtypeStruct((BLK, 256), F32),
                   jax.ShapeDtypeStruct((BLK, 256), F32), jax.ShapeDtypeStruct((SWA_QH, 128), F32)],
        scratch_shapes=[pltpu.VMEM((rs, 256), F32), pltpu.VMEM((rs, 256), F32)],
        args=(sinks, bias, sq, sk, sk, sk, sv, sv, sv, o, do), hook=hook)


def _mix_out(h1, ogla, gg, oswa, wgn, wsn, wout, wpost):
    t = h1.shape[0]
    tm = _row_tile(t)

    def body(h_ref, og_ref, gg_ref, os_ref, wgn_ref, wsn_ref, wout_ref, wpost_ref, h2_ref, cat_ref, m_ref):
        parts = []
        for h in range(GLA_HEADS):
            ls = slice(GLA_DV * h, GLA_DV * (h + 1))
            y, _, _ = _rms(og_ref[:, ls], wgn_ref[...])
            g = gg_ref[:, ls]
            parts.append(y * (g * _sigmoid(g)))
        ys, _, _ = _rms(os_ref[...], wsn_ref[...])
        cat = jnp.concatenate(parts + [ys], axis=1).astype(BF16)
        cat_ref[...] = cat
        m = _dot(cat, wout_ref[...])
        m_ref[...] = m
        y, _, _ = _rms(m, wpost_ref[...])
        h2_ref[...] = h_ref[...] + y

    def row(w):
        return pl.BlockSpec((tm, w), lambda i: (i, 0))

    return pl.pallas_call(
        body, name="mix_out", grid=(t // tm,),
        in_specs=[row(D_MODEL), row(512), row(512), row(512), _full((1, GLA_DV)), _full((1, SWA_W)),
                  _full((D_MODEL, D_MODEL)), _full((1, D_MODEL))],
        out_specs=[row(D_MODEL), row(D_MODEL), row(D_MODEL)],
        out_shape=[jax.ShapeDtypeStruct((t, D_MODEL), F32), jax.ShapeDtypeStruct((t, D_MODEL), BF16),
                   jax.ShapeDtypeStruct((t, D_MODEL), F32)],
        compiler_params=_cparams(1),
    )(h1, ogla, gg, oswa, wgn, wsn, wout, wpost)


def _mix_out_bwd(dh2, m, ogla, gg, oswa, wgn, wsn, wout, wpost, hook=None):
    t = dh2.shape[0]
    tm = _row_tile(t)

    def body(dh_ref, m_ref, og_ref, gg_ref, os_ref, wgn_ref, wsn_ref, wout_ref, wpost_ref,
             dog_ref, dgg_ref, dos_ref, dm_ref, dwpost_ref, dwgn_ref, dwsn_ref):
        @pl.when(pl.program_id(0) == 0)
        def _():
            dwpost_ref[...] = jnp.zeros_like(dwpost_ref)
            dwgn_ref[...] = jnp.zeros_like(dwgn_ref)
            dwsn_ref[...] = jnp.zeros_like(dwsn_ref)

        wpost = wpost_ref[...]
        _, mh, r = _rms(m_ref[...], wpost)
        dm, dw = _rms_bwd(mh, r, wpost, dh_ref[...])
        dwpost_ref[...] += dw
        dmb = dm.astype(BF16)
        dm_ref[...] = dmb
        dcat = _dg(dmb, wout_ref[...], NT)
        wgn = wgn_ref[...]
        for h in range(GLA_HEADS):
            ls = slice(GLA_DV * h, GLA_DV * (h + 1))
            dog = dcat[:, ls]
            g = gg_ref[:, ls]
            sg = _sigmoid(g)
            y, xh, r = _rms(og_ref[:, ls], wgn)
            dgg_ref[:, ls] = dog * y * (sg * (1.0 + g * (1.0 - sg)))
            dx, dw = _rms_bwd(xh, r, wgn, dog * (g * sg))
            dog_ref[:, ls] = dx
            dwgn_ref[...] += dw
        wsn = wsn_ref[...]
        _, xh, r = _rms(os_ref[...], wsn)
        dx, dw = _rms_bwd(xh, r, wsn, dcat[:, GLA_W:])
        dos_ref[...] = dx
        dwsn_ref[...] += dw

    def row(w):
        return pl.BlockSpec((tm, w), lambda i: (i, 0))

    def rshape(w, dt=F32):
        return jax.ShapeDtypeStruct((t, w), dt)

    return _pallas(
        body, name="mix_out_bwd", grid=(t // tm,),
        in_specs=[row(D_MODEL), row(D_MODEL), row(512), row(512), row(512), _full((1, GLA_DV)), _full((1, SWA_W)),
                  _full((D_MODEL, D_MODEL)), _full((1, D_MODEL))],
        out_specs=[row(512), row(512), row(512), row(D_MODEL), _full((1, D_MODEL)), _full((1, GLA_DV)),
                   _full((1, SWA_W))],
        out_shape=[rshape(512), rshape(512), rshape(512), rshape(D_MODEL, BF16),
                   jax.ShapeDtypeStruct((1, D_MODEL), F32), jax.ShapeDtypeStruct((1, GLA_DV), F32),
                   jax.ShapeDtypeStruct((1, SWA_W), F32)],
        args=(dh2, m, ogla, gg, oswa, wgn, wsn, wout, wpost), hook=hook)


def _mix_in_bwd(dh2, h1, wmixpre, winp, wa2p, bap, cos, sin, ga, dgq, dgk, dgv, dgg, dla, dsq, dsk, dsv, dkm, dvm):
    t = h1.shape[0]
    tm = _row_tile(t)

    def body(dh2_ref, h_ref, w_ref, win_ref, wa2_ref, ba_ref, cos_ref, sin_ref, ga_ref, dgq_ref, dgk_ref, dgv_ref,
             dgg_ref, dla_ref, dsq_ref, dsk_ref, dsv_ref, dkm_ref, dvm_ref,
             dh1_ref, dproj_ref, dw_ref, dwa2_ref, dba_ref):
        i = pl.program_id(0)

        @pl.when(i == 0)
        def _():
            dw_ref[...] = jnp.zeros_like(dw_ref)
            dwa2_ref[...] = jnp.zeros_like(dwa2_ref)
            dba_ref[...] = jnp.zeros_like(dba_ref)

        first = (i == 0).astype(F32)
        c = cos_ref[...]
        s = -sin_ref[...]
        fh = _first_half_mask(tm)
        dproj_ref[:, P_GQ:P_GK] = dgq_ref[...].astype(BF16)
        dproj_ref[:, P_GK:P_GV] = dgk_ref[...].astype(BF16)
        dproj_ref[:, P_GV:P_GG] = dgv_ref[...].astype(BF16)
        dproj_ref[:, P_GG:P_GA] = dgg_ref[...].astype(BF16)
        gab = ga_ref[...].astype(BF16)
        z = _dot(gab, wa2_ref[...]) + ba_ref[...]
        row_id = i * tm + lax.broadcasted_iota(jnp.int32, (tm, 1), 0)
        dz = jnp.where(row_id >= PAD, dla_ref[...] * (1.0 / GLA_TAU) * (1.0 - _sigmoid(z)), 0.0)
        dzb = dz.astype(BF16)
        dba_ref[...] += jnp.sum(dz, axis=0, keepdims=True)
        dwa2_ref[...] += _dg(gab, dzb, TN)
        dproj_ref[:, P_GA:P_SQ] = _dg(dzb, wa2_ref[...], NT).astype(BF16)
        for k in range(4):
            dy = dsq_ref[:, 128 * k:128 * (k + 1)]
            dproj_ref[:, P_SQ + 128 * k:P_SQ + 128 * (k + 1)] = (dy * c + _rot_half(dy, fh) * s).astype(BF16)
        for k in range(2):
            ls = slice(128 * k, 128 * (k + 1))
            dy = dsk_ref[:, ls]
            dy = jnp.concatenate([dy[:BLK] + first * dkm_ref[:, ls], dy[BLK:]], axis=0) if tm > BLK else (
                dy + first * dkm_ref[:, ls])
            dproj_ref[:, P_SK + 128 * k:P_SK + 128 * (k + 1)] = (dy * c + _rot_half(dy, fh) * s).astype(BF16)
            dv = dsv_ref[:, ls]
            dv = jnp.concatenate([dv[:BLK] + first * dvm_ref[:, ls], dv[BLK:]], axis=0) if tm > BLK else (
                dv + first * dvm_ref[:, ls])
            dproj_ref[:, P_SV + 128 * k:P_SV + 128 * (k + 1)] = dv.astype(BF16)
        dn = _dg(dproj_ref[...], win_ref[...], NT)
        w = w_ref[...]
        _, hh, r = _rms(h_ref[...], w)
        dx, dw = _rms_bwd(hh, r, w, dn)
        dw_ref[...] += dw
        dh1_ref[...] = dh2_ref[...] + dx

    def row(w):
        return pl.BlockSpec((tm, w), lambda i: (i, 0))

    return pl.pallas_call(
        body, name="mix_in_bwd", grid=(t // tm,),
        in_specs=[row(D_MODEL), row(D_MODEL), _full((1, D_MODEL)), _full((D_MODEL, P_END)), _full((128, GLA_KW)),
                  _full((1, GLA_KW)), row(128), row(128), row(128), row(256), row(256), row(512), row(512), row(256),
                  row(512), row(256), row(256), _full((BLK, 256)), _full((BLK, 256))],
        out_specs=[row(D_MODEL), row(P_END), _full((1, D_MODEL)), _full((128, GLA_KW)), _full((1, GLA_KW))],
        out_shape=[jax.ShapeDtypeStruct((t, D_MODEL), F32), jax.ShapeDtypeStruct((t, P_END), BF16),
                   jax.ShapeDtypeStruct((1, D_MODEL), F32), jax.ShapeDtypeStruct((128, GLA_KW), F32),
                   jax.ShapeDtypeStruct((1, GLA_KW), F32)],
        compiler_params=_cparams(1),
    )(dh2, h1, wmixpre, winp, wa2p, bap, cos, sin, ga, dgq, dgk, dgv, dgg, dla, dsq, dsk, dsv, dkm, dvm)


def _adamw_update(w, g, m, v):
    m = ADAM_B1 * m + (1.0 - ADAM_B1) * g
    v = ADAM_B2 * v + (1.0 - ADAM_B2) * (g * g)
    m_hat = m / (1.0 - ADAM_B1 ** ADAM_STEP)
    v_hat = v / (1.0 - ADAM_B2 ** ADAM_STEP)
    return -ADAM_LR * (m_hat / (jnp.sqrt(v_hat) + ADAM_EPS) + ADAM_WD * w), m, v


def _adamw_halves(w, g_mine, g_other, m, v, c_idx, row0=0):
    r, c = w.shape
    h = g_mine.shape[0]
    tr = _div_tile(math.gcd(r, h))
    nth = h // tr
    t0 = row0 // tr
    assert t0 * tr == row0

    def body(c_ref, w_ref, gm_ref, go_ref, m_ref, v_ref, g_ref, d_ref, nm_ref, nv_ref):
        hh = (t0 + pl.program_id(0)) // nth
        g = jnp.where(hh == c_ref[0], gm_ref[...], go_ref[...])
        g_ref[...] = g
        d_ref[...], nm_ref[...], nv_ref[...] = _adamw_update(w_ref[...], g, m_ref[...], v_ref[...])

    spec = pl.BlockSpec((tr, c), lambda i, c_ref: (i, 0))

    def gspec(is_mine):
        def index(i, c_ref):
            used = ((t0 + i) // nth == c_ref[0]) == is_mine
            return (jnp.where(used, (t0 + i) % nth, 0), 0)
        return pl.BlockSpec((tr, c), index)

    shape = jax.ShapeDtypeStruct((r, c), F32)
    return pl.pallas_call(
        body, name="adamw_halves",
        grid_spec=pltpu.PrefetchScalarGridSpec(
            num_scalar_prefetch=1, grid=(r // tr,), in_specs=[spec, gspec(True), gspec(False), spec, spec],
            out_specs=[spec] * 4),
        out_shape=[shape] * 4, compiler_params=_cparams(1),
    )(c_idx, w, g_mine, g_other, m, v)


def _place():
    x, y, c = lax.axis_index("x"), lax.axis_index("y"), lax.axis_index("c")
    chips = [(1 - x, y), (x, 1 - y), (1 - x, 1 - y)]
    return x, y, c, chips


def _remote(send_sem, recv_sem, src, dst, to):
    return pltpu.make_async_remote_copy(src_ref=src, dst_ref=dst, send_sem=send_sem, recv_sem=recv_sem,
                                        device_id=to, device_id_type=MESH)


def _half(ref_rows, c):
    h = ref_rows // 2
    return pl.ds(pl.multiple_of(c * h, 8), h)


def _own_slot(shard, q):
    return lax.dynamic_update_slice(jnp.zeros((N_CHIPS,) + shard.shape, shard.dtype), shard[None], (q, 0, 0))


def _stack_own_slot(mats, q_idx):
    r, w = mats[0].shape
    tr = _div_tile(r)
    per = r // tr
    n = len(mats)

    def body(q_ref, *refs):
        m_refs, o_ref = refs[:n], refs[n]
        s = pl.program_id(0)
        for k in range(n):
            @pl.when(s // per == k)
            def _(k=k):
                o_ref[...] = m_refs[k][...].astype(BF16)

    def rows_of(k):
        return lambda s, q_ref: (jnp.where(s // per == k, s % per, 0), 0)

    return pl.pallas_call(
        body, name="stack_own_slot",
        grid_spec=pltpu.PrefetchScalarGridSpec(
            num_scalar_prefetch=1, grid=(n * per,),
            in_specs=[pl.BlockSpec((tr, w), rows_of(k)) for k in range(n)],
            out_specs=pl.BlockSpec((None, tr, w), lambda s, q_ref: (q_ref[0], s, 0))),
        out_shape=jax.ShapeDtypeStruct((N_CHIPS, n * r, w), BF16), compiler_params=_cparams(1),
    )(q_idx, *mats)


class _GatherChips:
    has_mid = True

    def __init__(self, bufs):
        n = len(bufs)
        self.inputs = list(bufs)
        self.out_shape = [jax.ShapeDtypeStruct(b.shape, b.dtype) for b in bufs]
        self.aliases = [(t, t) for t in range(n)]
        self.scratch = [pltpu.SemaphoreType.DMA((n, 6)), pltpu.SemaphoreType.DMA((n, 6))]

    def start(self, ins, outs, scr):
        send, recv = scr
        x, y, c, chips = _place()
        q = 2 * x + y
        for t, (i_ref, o_ref) in enumerate(zip(ins, outs)):
            rows = _half(i_ref.shape[1], c)
            for j, (cx, cy) in enumerate(chips):
                _remote(send.at[t, j], recv.at[t, j], i_ref.at[q, rows], o_ref.at[q, rows], (cx, cy, c)).start()

    def mid(self, ins, outs, scr):
        send, recv = scr
        x, y, c, chips = _place()
        for t, o_ref in enumerate(outs):
            rows = _half(o_ref.shape[1], c)
            for j, (cx, cy) in enumerate(chips):
                slot = o_ref.at[2 * cx + cy, rows]
                _remote(send.at[t, j], recv.at[t, j], slot, slot, (cx, cy, c)).wait_recv()
                _remote(send.at[t, 3 + j], recv.at[t, 3 + j], slot, slot, (x, y, 1 - c)).start()

    def finish(self, ins, outs, scr):
        send, recv = scr
        x, y, c, chips = _place()
        for t, o_ref in enumerate(outs):
            mine, other = _half(o_ref.shape[1], c), _half(o_ref.shape[1], 1 - c)
            for j, (cx, cy) in enumerate(chips):
                slot = o_ref.at[2 * cx + cy, other]
                _remote(send.at[t, 3 + j], recv.at[t, 3 + j], slot, slot, (x, y, 1 - c)).wait_recv()
            for j, (cx, cy) in enumerate(chips):
                sent = o_ref.at[2 * cx + cy, mine]
                _remote(send.at[t, j], recv.at[t, j], sent, sent, (cx, cy, c)).wait_send()
                _remote(send.at[t, 3 + j], recv.at[t, 3 + j], sent, sent, (x, y, 1 - c)).wait_send()


class _PairExchange:
    has_mid = False
    aliases = ()

    def __init__(self, arrs):
        n = len(arrs)
        self.inputs = list(arrs)
        self.out_shape = [jax.ShapeDtypeStruct((a.shape[0], a.shape[1] // 2, a.shape[2]), a.dtype) for a in arrs]
        self.scratch = [pltpu.SemaphoreType.DMA((n,)), pltpu.SemaphoreType.DMA((n,))]

    def _copies(self, ins, outs, scr):
        send, recv = scr
        x, y, c, _ = _place()
        return [_remote(send.at[t], recv.at[t], i_ref.at[:, _half(i_ref.shape[1], 1 - c)], o_ref, (x, y, 1 - c))
                for t, (i_ref, o_ref) in enumerate(zip(ins, outs))]

    def start(self, ins, outs, scr):
        for cp in self._copies(ins, outs, scr):
            cp.start()

    def finish(self, ins, outs, scr):
        for cp in self._copies(ins, outs, scr):
            cp.wait()


class _ChipScatter:
    has_mid = False
    aliases = ()

    def __init__(self, arrs):
        n = len(arrs)
        self.inputs = list(arrs)
        self.out_shape = [jax.ShapeDtypeStruct((3,) + a.shape[1:], a.dtype) for a in arrs]
        self.scratch = [pltpu.SemaphoreType.DMA((n, 3)), pltpu.SemaphoreType.DMA((n, 3))]

    def _copies(self, ins, outs, scr):
        send, recv = scr
        x, y, c, chips = _place()
        return [_remote(send.at[t, j], recv.at[t, j], i_ref.at[2 * cx + cy], o_ref.at[j], (cx, cy, c))
                for t, (i_ref, o_ref) in enumerate(zip(ins, outs)) for j, (cx, cy) in enumerate(chips)]

    def start(self, ins, outs, scr):
        for cp in self._copies(ins, outs, scr):
            cp.start()

    def finish(self, ins, outs, scr):
        for cp in self._copies(ins, outs, scr):
            cp.wait()


class _PairShare:
    has_mid = False
    aliases = ()

    def __init__(self, arrs):
        n = len(arrs)
        self.inputs = list(arrs)
        self.out_shape = [jax.ShapeDtypeStruct(a.shape, a.dtype) for a in arrs]
        self.scratch = [pltpu.SemaphoreType.DMA((n,)), pltpu.SemaphoreType.DMA((n,))]

    def _copies(self, ins, outs, scr):
        send, recv = scr
        x, y, c, _ = _place()
        return [_remote(send.at[t], recv.at[t], i_ref, o_ref, (x, y, 1 - c))
                for t, (i_ref, o_ref) in enumerate(zip(ins, outs))]

    def start(self, ins, outs, scr):
        for cp in self._copies(ins, outs, scr):
            cp.start()

    def finish(self, ins, outs, scr):
        for cp in self._copies(ins, outs, scr):
            cp.wait()


def _comm_call(hook, name):
    n_in, n_out = len(hook.inputs), len(hook.out_shape)

    def body(*refs):
        ins, outs, scr = refs[:n_in], refs[n_in:n_in + n_out], refs[n_in + n_out:]
        hook.start(ins, outs, scr)
        if hook.has_mid:
            hook.mid(ins, outs, scr)
        hook.finish(ins, outs, scr)

    return pl.pallas_call(body, name=name, in_specs=[ANY] * n_in, out_specs=[ANY] * n_out,
                          out_shape=list(hook.out_shape), scratch_shapes=list(hook.scratch),
                          input_output_aliases=dict(hook.aliases))(*hook.inputs)


class _GatherDevices:
    has_mid = True
    aliases = ()

    def __init__(self, vecs):
        n = len(vecs)
        self.inputs = list(vecs)
        self.out_shape = [jax.ShapeDtypeStruct((N_DEV,) + v.shape, v.dtype) for v in vecs]
        self.scratch = [pltpu.SemaphoreType.DMA((n, 7)), pltpu.SemaphoreType.DMA((n, 7)),
                        pltpu.SemaphoreType.DMA((n,))]

    @staticmethod
    def _copy(scr, t, k, out_ref, block, to, src=None):
        send, recv, _ = scr
        px, py, pc = block
        slot = out_ref.at[4 * px + 2 * py + pc]
        return _remote(send.at[t, k], recv.at[t, k], slot if src is None else src, slot, to)

    def start(self, ins, outs, scr):
        x, y, c, chips = _place()
        me = (x, y, c)
        for t, (x_ref, out_ref) in enumerate(zip(ins, outs)):
            pltpu.make_async_copy(x_ref, out_ref.at[4 * x + 2 * y + c], scr[2].at[t]).start()
            self._copy(scr, t, 0, out_ref, me, (x, y, 1 - c), src=x_ref).start()
            for j, chip in enumerate(chips):
                self._copy(scr, t, 1 + j, out_ref, me, (*chip, c), src=x_ref).start()

    def mid(self, ins, outs, scr):
        x, y, c, chips = _place()
        for t, out_ref in enumerate(outs):
            for j, chip in enumerate(chips):
                self._copy(scr, t, 1 + j, out_ref, (*chip, c), (x, y, c)).wait_recv()
                self._copy(scr, t, 4 + j, out_ref, (*chip, c), (x, y, 1 - c)).start()

    def finish(self, ins, outs, scr):
        x, y, c, chips = _place()
        me = (x, y, c)
        for t, (x_ref, out_ref) in enumerate(zip(ins, outs)):
            self._copy(scr, t, 0, out_ref, (x, y, 1 - c), me).wait_recv()
            for j, chip in enumerate(chips):
                self._copy(scr, t, 4 + j, out_ref, (*chip, 1 - c), me).wait_recv()
            self._copy(scr, t, 0, out_ref, me, (x, y, 1 - c), src=x_ref).wait_send()
            for j, chip in enumerate(chips):
                self._copy(scr, t, 1 + j, out_ref, me, (*chip, c), src=x_ref).wait_send()
                self._copy(scr, t, 4 + j, out_ref, (*chip, c), (x, y, 1 - c)).wait_send()
            pltpu.make_async_copy(x_ref, out_ref.at[4 * x + 2 * y + c], scr[2].at[t]).wait()


class _Hooks:
    def __init__(self, hooks):
        self.hooks = list(hooks)
        self.has_mid = any(h.has_mid for h in hooks)
        self.inputs = [a for h in hooks for a in h.inputs]
        self.out_shape = [s for h in hooks for s in h.out_shape]
        self.scratch = [s for h in hooks for s in h.scratch]
        self.aliases = []
        i0 = o0 = 0
        for h in hooks:
            self.aliases += [(i0 + a, o0 + b) for a, b in h.aliases]
            i0 += len(h.inputs)
            o0 += len(h.out_shape)

    def _each(self, ins, outs, scr):
        i0 = o0 = s0 = 0
        for h in self.hooks:
            ni, no, ns = len(h.inputs), len(h.out_shape), len(h.scratch)
            yield h, ins[i0:i0 + ni], outs[o0:o0 + no], scr[s0:s0 + ns]
            i0, o0, s0 = i0 + ni, o0 + no, s0 + ns

    def start(self, ins, outs, scr):
        for h, i, o, s in self._each(ins, outs, scr):
            h.start(i, o, s)

    def mid(self, ins, outs, scr):
        for h, i, o, s in self._each(ins, outs, scr):
            if h.has_mid:
                h.mid(i, o, s)

    def finish(self, ins, outs, scr):
        for h, i, o, s in self._each(ins, outs, scr):
            h.finish(i, o, s)

    def split(self, outs):
        res, o0 = [], 0
        for h in self.hooks:
            res.append(list(outs[o0:o0 + len(h.out_shape)]))
            o0 += len(h.out_shape)
        return res


def _pair_sum(g, other, c_idx):
    nq, r, w = g.shape
    h = r // 2
    tr = _div_tile(h)
    nt = h // tr

    def body(c_ref, g_ref, o_ref, s_ref):
        s_ref[...] = (g_ref[...].astype(F32) + o_ref[...].astype(F32)).astype(s_ref.dtype)

    return pl.pallas_call(
        body, name="pair_sum",
        grid_spec=pltpu.PrefetchScalarGridSpec(
            num_scalar_prefetch=1, grid=(nq, nt),
            in_specs=[pl.BlockSpec((None, tr, w), lambda k, i, c_ref: (k, c_ref[0] * nt + i, 0)),
                      pl.BlockSpec((None, tr, w), lambda k, i, c_ref: (k, i, 0))],
            out_specs=pl.BlockSpec((None, tr, w), lambda k, i, c_ref: (k, i, 0))),
        out_shape=jax.ShapeDtypeStruct((nq, h, w), g.dtype),
        compiler_params=_cparams(2),
    )(c_idx, g, other)


def _pair_exchange_sum(arrs):
    n = len(arrs)
    halves = [(a.shape[0], a.shape[1] // 2, a.shape[2]) for a in arrs]

    def body(*refs):
        ins, outs, mine, theirs = (refs[k * n:(k + 1) * n] for k in range(4))
        send, recv, load = refs[4 * n:]
        x, y, c, _ = _place()
        remote = [_remote(send.at[t], recv.at[t], ins[t].at[:, _half(ins[t].shape[1], 1 - c)], theirs[t],
                          (x, y, 1 - c)) for t in range(n)]
        local = [pltpu.make_async_copy(ins[t].at[:, _half(ins[t].shape[1], c)], mine[t], load.at[t])
                 for t in range(n)]
        for cp in remote + local:
            cp.start()
        for t in range(n):
            local[t].wait()
            remote[t].wait()
            for q in range(halves[t][0]):
                outs[t][q] = (mine[t][q].astype(F32) + theirs[t][q].astype(F32)).astype(outs[t].dtype)

    return pl.pallas_call(
        body, name="pair_exchange_sum", in_specs=[ANY] * n,
        out_shape=[jax.ShapeDtypeStruct(s, a.dtype) for s, a in zip(halves, arrs)],
        scratch_shapes=[pltpu.VMEM(s, a.dtype) for s, a in zip(halves, arrs)] * 2
        + [pltpu.SemaphoreType.DMA((n,))] * 3,
        compiler_params=_cparams(0),
    )(*arrs)


def _chip_sum(s, others, q_idx):
    _, h, w = s.shape
    tr = _div_tile(h)

    def body(q_ref, s_ref, o_ref, out_ref):
        out_ref[...] = ((s_ref[...].astype(F32) + o_ref[0].astype(F32)) + o_ref[1].astype(F32)) + o_ref[2].astype(F32)

    return pl.pallas_call(
        body, name="chip_sum",
        grid_spec=pltpu.PrefetchScalarGridSpec(
            num_scalar_prefetch=1, grid=(h // tr,),
            in_specs=[pl.BlockSpec((None, tr, w), lambda i, q_ref: (q_ref[0], i, 0)),
                      pl.BlockSpec((3, tr, w), lambda i, q_ref: (0, i, 0))],
            out_specs=pl.BlockSpec((tr, w), lambda i, q_ref: (i, 0))),
        out_shape=jax.ShapeDtypeStruct((h, w), F32),
        compiler_params=_cparams(1),
    )(q_idx, s, others)


def _small_update(q_idx, parts, ws, ms, vs, col_block):
    n = len(parts)
    has_w = [w is not None for w in ws]

    def body(q_ref, *refs):
        pos = 0
        ins = []
        for t in range(n):
            k = 4 if has_w[t] else 1
            ins.append(refs[pos:pos + k])
            pos += k
        outs = refs[pos:]
        opos = 0
        for t in range(n):
            p_ref = ins[t][0]
            g = p_ref[0]
            for s in range(1, p_ref.shape[0]):
                g = g + p_ref[s]
            if has_w[t]:
                _, w_ref, m_ref, v_ref = ins[t]
                g_ref, d_ref, nm_ref, nv_ref = outs[opos:opos + 4]
                opos += 4
                g_ref[...] = g
                d_ref[...], nm_ref[...], nv_ref[...] = _adamw_update(w_ref[...], g, m_ref[...], v_ref[...])
            else:
                outs[opos][...] = g
                opos += 1

    def whole(shape):
        nd = len(shape)
        return pl.BlockSpec(shape, lambda i, q_ref: (0,) * nd)

    in_specs, out_specs, out_shape, args = [], [], [], []
    for t in range(n):
        k, r, wf = parts[t].shape
        if col_block[t]:
            w = wf // N_CHIPS
            in_specs.append(pl.BlockSpec((k, r, w), lambda i, q_ref: (0, 0, q_ref[0])))
        else:
            w = wf
            in_specs.append(whole((k, r, wf)))
        args.append(parts[t])
        if has_w[t]:
            assert ws[t].shape == (r, w), (ws[t].shape, r, w)
            in_specs += [whole((r, w))] * 3
            args += [ws[t], ms[t], vs[t]]
            out_specs += [whole((r, w))] * 4
            out_shape += [jax.ShapeDtypeStruct((r, w), F32)] * 4
        else:
            out_specs.append(whole((r, w)))
            out_shape.append(jax.ShapeDtypeStruct((r, w), F32))
    return pl.pallas_call(
        body, name="small_update",
        grid_spec=pltpu.PrefetchScalarGridSpec(num_scalar_prefetch=1, grid=(1,), in_specs=in_specs,
                                               out_specs=out_specs),
        out_shape=out_shape, compiler_params=_cparams(1),
    )(q_idx, *args)


_PACK_SEGMENTS = ((0, 1552), None, (1552, 2064), (2064, 2128), (2064, 2128), (2128, 2192), (2128, 2192),
                  (2192, 2256), (2192, 2256), (2256, 2320), (2256, 2320))
_UNPACK_SEGMENTS = (((0, 1552), (0,)), ((1552, 2064), (P_SQ,)), ((2064, 2128), (P_SK, P_SK + 64)),
                    ((2128, 2192), (P_SK + 128, P_SK + 192)), ((2192, 2256), (P_SV, P_SV + 64)),
                    ((2256, 2320), (P_SV + 128, P_SV + 192)))


def _pack_win(w4):
    per = w4.shape[2]
    pieces = []
    for seg in _PACK_SEGMENTS:
        if seg is None:
            pieces.append(jnp.zeros((w4.shape[1], 128 - GLA_RANK), w4.dtype))
            continue
        for q in range(w4.shape[0]):
            lo, hi = max(seg[0], q * per), min(seg[1], (q + 1) * per)
            if lo < hi:
                pieces.append(w4[q][:, lo - q * per:hi - q * per])
    return jnp.concatenate(pieces, axis=1)


def _unpack_dwin(d):
    per = D_IN // N_CHIPS
    chips = []
    for q in range(N_CHIPS):
        pieces = []
        for (a, b), starts in _UNPACK_SEGMENTS:
            lo, hi = max(a, q * per), min(b, (q + 1) * per)
            if lo < hi:
                copies = [d[:, s + lo - a:s + hi - a] for s in starts]
                pieces.append(copies[0] if len(copies) == 1 else copies[0] + copies[1])
        chips.append(jnp.concatenate(pieces, axis=1))
    return jnp.stack(chips)


def _local_step(x, target, meta, p):
    s = x.shape[0]
    t = s + BLK
    h0 = jnp.concatenate([jnp.zeros((PAD, D_MODEL), F32), meta, x], axis=0)
    cos, sin = _rope_tables(t)

    h1, n1, g1, u1, a1, f1 = _ffn_fwd(h0, p["ffn1_pre_norm"], p["ffn1_w"], p["ffn1_post_norm"])
    n2, gq, gk, gv, gg, ga, la, sq, sk, sv = _mix_proj(h1, p["mix_pre_norm"], p["w_in"], p["gla_w_a2"], p["gla_b_a"],
                                                       cos, sin)
    ogla, ss = _gla_fwd(gq, gk, gv, la)
    oswa = _swa_fwd(p["swa_sinks"], sq, sk, sv)
    h2, cat, m = _mix_out(h1, ogla, gg, oswa, p["gla_out_norm"], p["swa_out_norm"], p["w_out"], p["mix_post_norm"])
    grads = {}
    dy, n3, g3, u3, a3, df3, grads["ffn2_post_norm"], sse = _ffn_fwd(
        h2, p["ffn2_pre_norm"], p["ffn2_w"], p["ffn2_post_norm"], target=target)

    dh2, dg3, du3, grads["ffn2_pre_norm"] = _ffn_bwd(
        dy, h2, None, g3, u3, p["ffn2_pre_norm"], p["ffn2_w"], p["ffn2_post_norm"], df=df3)
    (gud,) = _ffn_wgrad(n3, df3, dg3, du3, a3)
    grads["ffn2_w_gate"], grads["ffn2_w_up"], grads["ffn2_w_down"] = gud[:, :FJ], gud[:, FJ:2 * FJ], gud[:, 2 * FJ:]

    dogla, dgg, doswa, dm, grads["mix_post_norm"], grads["gla_out_norm"], grads["swa_out_norm"] = _mix_out_bwd(
        dh2, m, ogla, gg, oswa, p["gla_out_norm"], p["swa_out_norm"], p["w_out"], p["mix_post_norm"])
    grads["w_out"] = _xty(cat, dm)
    dsq, dsk, dsv, dkm, dvm, dsinks = _swa_bwd(p["swa_sinks"], sq, sk, sv, oswa, doswa)
    grads["swa_sinks"] = dsinks[:, 0]
    dgq, dgk, dgv, dla = _gla_bwd(gq, gk, gv, la, ss, dogla)
    dh1, dproj, grads["mix_pre_norm"], dwa2p, grads["gla_b_a"] = _mix_in_bwd(
        dh2, h1, p["mix_pre_norm"], p["w_in"], p["gla_w_a2"], p["gla_b_a"], cos, sin, ga, dgq, dgk, dgv, dgg, dla,
        dsq, dsk, dsv, dkm, dvm)
    grads["gla_w_a2"] = dwa2p[:GLA_RANK]
    grads["w_in"] = _unpack_dwin(_xty(n2, dproj))

    dh0, df1, dg1, du1, grads["ffn1_pre_norm"], grads["ffn1_post_norm"] = _ffn_bwd(
        dh1, h0, f1, g1, u1, p["ffn1_pre_norm"], p["ffn1_w"], p["ffn1_post_norm"])
    (gud,) = _ffn_wgrad(n1, df1, dg1, du1, a1)
    grads["ffn1_w_gate"], grads["ffn1_w_up"], grads["ffn1_w_down"] = gud[:, :FJ], gud[:, FJ:2 * FJ], gud[:, 2 * FJ:]
    grads["meta_tokens"] = dh0[PAD:BLK]
    return sse[0, 0], dh0[BLK:], grads


WEIGHTS = ['meta_tokens', 'ffn1_pre_norm', 'ffn1_w_gate', 'ffn1_w_up', 'ffn1_w_down', 'ffn1_post_norm',
           'mix_pre_norm', 'w_in', 'gla_w_a2', 'gla_b_a', 'gla_out_norm', 'swa_sinks', 'swa_out_norm', 'w_out',
           'mix_post_norm', 'ffn2_pre_norm', 'ffn2_w_gate', 'ffn2_w_up', 'ffn2_w_down', 'ffn2_post_norm']
BIG = ['ffn1_w_gate', 'ffn1_w_up', 'ffn1_w_down', 'w_in', 'w_out', 'ffn2_w_gate', 'ffn2_w_up', 'ffn2_w_down']
SMALL = [n for n in WEIGHTS if n not in BIG]
FJ = D_FF // N_CHIPS
D_IN_J = D_IN // N_CHIPS
D_OUT_J = D_MODEL // N_CHIPS
TRANSPOSED = ('ffn1_w_gate', 'ffn1_w_up', 'ffn2_w_gate', 'ffn2_w_up')


def _shard2d(name, a):
    return a[0].T if name in TRANSPOSED else a[0]


def _unshard2d(name, a):
    return (a.T if name in TRANSPOSED else a)[None]


def kernel(x, meta_tokens, ffn1_pre_norm, ffn1_w_gate, ffn1_w_up, ffn1_w_down, ffn1_post_norm, mix_pre_norm, w_in, gla_w_a2, gla_b_a, gla_out_norm, swa_sinks, swa_out_norm, w_out, mix_post_norm, ffn2_pre_norm, ffn2_w_gate, ffn2_w_up, ffn2_w_down, ffn2_post_norm, loss_target, m_meta_tokens, m_ffn1_pre_norm, m_ffn1_w_gate, m_ffn1_w_up, m_ffn1_w_down, m_ffn1_post_norm, m_mix_pre_norm, m_w_in, m_gla_w_a2, m_gla_b_a, m_gla_out_norm, m_swa_sinks, m_swa_out_norm, m_w_out, m_mix_post_norm, m_ffn2_pre_norm, m_ffn2_w_gate, m_ffn2_w_up, m_ffn2_w_down, m_ffn2_post_norm, v_meta_tokens, v_ffn1_pre_norm, v_ffn1_w_gate, v_ffn1_w_up, v_ffn1_w_down, v_ffn1_post_norm, v_mix_pre_norm, v_w_in, v_gla_w_a2, v_gla_b_a, v_gla_out_norm, v_swa_sinks, v_swa_out_norm, v_w_out, v_mix_post_norm, v_ffn2_pre_norm, v_ffn2_w_gate, v_ffn2_w_up, v_ffn2_w_down, v_ffn2_post_norm):
    args = dict(locals())
    w = {n: args[n] for n in WEIGHTS}
    mom = {n: args["m_" + n] for n in WEIGHTS}
    var = {n: args["v_" + n] for n in WEIGHTS}
    cx, cy, cc = lax.axis_index("x"), lax.axis_index("y"), lax.axis_index("c")
    q_idx = (2 * cx + cy).astype(jnp.int32).reshape(1)
    c_idx = cc.astype(jnp.int32).reshape(1)

    q_chip = 2 * cx + cy
    bf = {n: _own_slot(_shard2d(n, w[n]).astype(BF16), q_chip) for n in ("w_in", "w_out")}
    for ffn in ("ffn1", "ffn2"):
        bf[ffn] = _stack_own_slot([_shard2d(ffn + s, w[ffn + s]) for s in ("_w_gate", "_w_up", "_w_down")], q_idx)
    qc_idx = jnp.stack([q_chip, cc]).astype(jnp.int32)
    sinks = w["swa_sinks"].reshape(SWA_QH)

    seq, target = x[0], loss_target[0]
    t = seq.shape[0] + BLK
    h0, n1 = _embed_norm(seq, _own_slot(w["meta_tokens"], q_chip), w["ffn1_pre_norm"])
    cos, sin = _rope_tables(t)
    late = _GatherChips([bf["w_in"], bf["w_out"], bf["ffn2"],
                         _own_slot(w["gla_w_a2"].reshape(GLA_RANK, GLA_KW // N_CHIPS), q_chip)])
    (h1, g1, u1, a1, f1), (w31,), (win4, wout4, w32, wa24) = _ffn_fwd_gather(
        h0, n1, bf["ffn1"], w["ffn1_post_norm"], qc_idx, late)
    wa2p = jnp.pad(wa24.transpose(1, 0, 2).reshape(GLA_RANK, GLA_KW), ((0, 128 - GLA_RANK), (0, 0))).astype(BF16)
    winp = _pack_win(win4)
    wout = wout4.reshape(D_MODEL, D_MODEL)
    n2, gq, gk, gv, gg, ga, la, sq, sk, sv = _mix_proj(h1, w["mix_pre_norm"], winp, wa2p, w["gla_b_a"], cos, sin)
    ogla, ss = _gla_fwd(gq, gk, gv, la)
    oswa = _swa_fwd(sinks, sq, sk, sv)
    h2, cat, m = _mix_out(h1, ogla, gg, oswa, w["gla_out_norm"], w["swa_out_norm"], wout, w["mix_post_norm"])
    g = {}
    dy, n3, g3, u3, a3, df3, g["ffn2_post_norm"], sse = _ffn_fwd(
        h2, w["ffn2_pre_norm"], w32, w["ffn2_post_norm"], target=target)

    dh2, dg3, du3, g["ffn2_pre_norm"] = _ffn_bwd(
        dy, h2, None, g3, u3, w["ffn2_pre_norm"], w32, w["ffn2_post_norm"], df=df3)
    (gf2,) = _ffn_wgrad(n3, df3, dg3, du3, a3)
    (dogla, dgg, doswa, dm, g["mix_post_norm"], g["gla_out_norm"], g["swa_out_norm"]), (rgf2,) = _mix_out_bwd(
        dh2, m, ogla, gg, oswa, w["gla_out_norm"], w["swa_out_norm"], wout, w["mix_post_norm"],
        hook=_PairExchange([gf2]))
    sgf2 = _pair_sum(gf2, rgf2, c_idx)
    gout = _xty(cat, dm).reshape(N_CHIPS, D_OUT_J, D_MODEL)
    (dsq, dsk, dsv, dkm, dvm, dsinks), (ogf2,) = _swa_bwd(sinks, sq, sk, sv, oswa, doswa,
                                                          hook=_ChipScatter([sgf2]))
    g["swa_sinks"] = dsinks
    dgq, dgk, dgv, dla = _gla_bwd(gq, gk, gv, la, ss, dogla)
    dh1, dproj, g["mix_pre_norm"], dwa2p, g["gla_b_a"] = _mix_in_bwd(
        dh2, h1, w["mix_pre_norm"], winp, wa2p, w["gla_b_a"], cos, sin, ga, dgq, dgk, dgv, dgg, dla,
        dsq, dsk, dsv, dkm, dvm)
    g["gla_w_a2"] = dwa2p[:GLA_RANK]
    gin = _unpack_dwin(_xty(n2, dproj))
    sgin, sgout = _pair_exchange_sum([gin, gout])
    dh_first, grad_x, df1, dg1, du1, g["ffn1_pre_norm"], g["ffn1_post_norm"] = _ffn_bwd(
        dh1, h0, f1, g1, u1, w["ffn1_pre_norm"], w31, w["ffn1_post_norm"], split_first_block=True)
    g["meta_tokens"] = dh_first[PAD:BLK]
    late_small = ["gla_w_a2", "swa_sinks"]
    direct = [n for n in SMALL if n not in late_small]
    names = direct + late_small
    half_f2 = _chip_sum(sgf2, ogf2, q_idx)
    hooks = _Hooks([_ChipScatter([sgin, sgout]), _GatherDevices([g[n] for n in names] + [sse]),
                    _PairShare([half_f2])])
    own1, others1, houts = _ffn_wgrad_reduce(n1, df1, dg1, du1, a1, qc_idx, hooks)
    (ogin, ogout), gathered, (other_f2,) = hooks.split(houts)
    halves = [_chip_sum(own1[None], others1, jnp.zeros((1,), jnp.int32))]
    halves += [_chip_sum(s, o, q_idx) for s, o in ((sgin, ogin), (sgout, ogout))]
    others = list(_comm_call(_PairShare(halves), "pair_share")) + [other_f2]
    halves.append(half_f2)
    reduced = {"ffn1_w_gate": (0, 0), "ffn1_w_up": (0, FJ), "ffn1_w_down": (0, 2 * FJ), "w_in": (1, 0),
               "w_out": (2, 0), "ffn2_w_gate": (3, 0), "ffn2_w_up": (3, FJ), "ffn2_w_down": (3, 2 * FJ)}
    grad, delta, new_m, new_v = {}, {}, {}, {}
    for n in BIG:
        k, row0 = reduced[n]
        outs = _adamw_halves(_shard2d(n, w[n]), halves[k], others[k], _shard2d(n, mom[n]), _shard2d(n, var[n]),
                             c_idx, row0)
        grad[n], delta[n], new_m[n], new_v[n] = [_unshard2d(n, a) for a in outs]

    late = late_small
    mat = lambda a: a.reshape(a.shape[-2:])
    none3 = [None] * (len(late) + 1)
    outs = _small_update(q_idx, gathered, [mat(w[n]) for n in direct] + none3, [mat(mom[n]) for n in direct] + none3,
                         [mat(var[n]) for n in direct] + none3, [n == "meta_tokens" for n in names] + [False])
    sum_a2, sum_sinks, sum_sse = outs[4 * len(direct):]
    loss = sum_sse[0, 0] * (0.5 / D_MODEL)
    g_late = [lax.dynamic_slice_in_dim(sum_a2, q_chip * (GLA_KW // N_CHIPS), GLA_KW // N_CHIPS, axis=1)[None],
              sum_sinks[:, 0].reshape(1, 1, SWA_QH)]
    outs = list(outs[:4 * len(direct)]) + list(_small_update(
        q_idx, g_late, [mat(w[n]) for n in late], [mat(mom[n]) for n in late], [mat(var[n]) for n in late],
        [False, False]))
    for k, n in enumerate(names):
        grad[n], delta[n], new_m[n], new_v[n] = [a.reshape(w[n].shape) for a in outs[4 * k:4 * k + 4]]

    return (loss, grad_x[None], *[grad[n] for n in WEIGHTS], *[delta[n] for n in WEIGHTS],
            *[new_m[n] for n in WEIGHTS], *[new_v[n] for n in WEIGHTS])
```
